```python
import jax, jax.numpy as jnp
from jax import lax
import numpy as np

D_MODEL = 1024
BATCH = 8
SEQ = 4096
DEPTH = 2

CHUNK = 64
POOL_WINDOWS = (2, 4, 8, 16)
POOL_GROUP = 64
D_POOL = POOL_GROUP * len(POOL_WINDOWS)
N_HEADS = 8
HEAD_DIM = 64
D_ATTN = N_HEADS * HEAD_DIM
N_PREV_CHUNKS = 8
BAND = (N_PREV_CHUNKS + 1) * CHUNK
REL_CLIP = 128
N_REL = 2 * REL_CLIP + 1
D_CONV = 256
CONV_WIDTH = 31
D_FF = 4 * D_MODEL
N_BRANCH = 3
IN_SIZES = (D_POOL, D_ATTN, D_ATTN, D_ATTN, 2 * D_CONV, N_BRANCH * D_MODEL)
IN_SPLITS = tuple(int(v) for v in np.cumsum(IN_SIZES)[:-1])
D_IN = int(sum(IN_SIZES))
ALPHA = (2.0 * DEPTH) ** 0.25
BETA = (8.0 * DEPTH) ** -0.25
LN_EPS = 1e-5
NEG_INF = -1e30

kernel_name = "hybrid_chunk_causal_pool_attn_conv_block"


def layer_norm(x, g=None, b=None):
    xf = x.astype(jnp.float32)
    mu = jnp.mean(xf, axis=-1, keepdims=True)
    var = jnp.mean(jnp.square(xf - mu), axis=-1, keepdims=True)
    y = (xf - mu) * lax.rsqrt(var + LN_EPS)
    if g is not None:
        y = y * g.astype(jnp.float32) + b.astype(jnp.float32)
    return y.astype(x.dtype)


def pool_mixer(a, w_pool, pool_scale):
    B, S, _ = a.shape
    t = jnp.arange(S)
    outs = []
    for gi, w in enumerate(POOL_WINDOWS):
        xg = a[..., gi * POOL_GROUP:(gi + 1) * POOL_GROUP].astype(jnp.float32)
        cs = jnp.cumsum(xg, axis=1)
        cs_lag = jnp.pad(cs, ((0, 0), (w, 0), (0, 0)))[:, :S]
        count = jnp.minimum(t + 1, w).astype(jnp.float32)[None, :, None]
        outs.append((cs - cs_lag) / count - xg)
    p = jnp.stack(outs, axis=2).astype(a.dtype)
    p = jnp.einsum('bsgc,gcd->bsgd', p, w_pool).reshape(B, S, D_POOL)
    return p * pool_scale


def chunk_attention(q, k, v, rel_bias):
    B, S, _ = q.shape
    nc = S // CHUNK
    q = q.reshape(B, nc, CHUNK, N_HEADS, HEAD_DIM) * (HEAD_DIM ** -0.5)
    pad = ((0, 0), (N_PREV_CHUNKS * CHUNK, 0), (0, 0))
    kc = jnp.pad(k, pad).reshape(B, nc + N_PREV_CHUNKS, CHUNK, N_HEADS, HEAD_DIM)
    vc = jnp.pad(v, pad).reshape(B, nc + N_PREV_CHUNKS, CHUNK, N_HEADS, HEAD_DIM)
    band_idx = jnp.arange(nc)[:, None] + jnp.arange(N_PREV_CHUNKS + 1)[None, :]
    kb = kc[:, band_idx].reshape(B, nc, BAND, N_HEADS, HEAD_DIM)
    vb = vc[:, band_idx].reshape(B, nc, BAND, N_HEADS, HEAD_DIM)
    s = jnp.einsum('bnqhd,bnkhd->bnhqk', q, kb).astype(jnp.float32)
    qi = jnp.arange(CHUNK)[:, None]
    kj = jnp.arange(BAND)[None, :]
    rel = jnp.clip(N_PREV_CHUNKS * CHUNK + qi - kj, -REL_CLIP, REL_CLIP) + REL_CLIP
    bias = rel_bias[:, rel].astype(jnp.float32)
    key_pos = jnp.arange(nc)[:, None] * CHUNK + kj - N_PREV_CHUNKS * CHUNK
    valid = (key_pos >= 0)[None, :, None, None, :]
    s = jnp.where(valid, s + bias[None, None], NEG_INF)
    p = jax.nn.softmax(s, axis=-1).astype(vb.dtype)
    o = jnp.einsum('bnhqk,bnkhd->bnqhd', p, vb)
    return o.reshape(B, S, D_ATTN)


def conv_module(cin, conv_w, conv_b, ln_g, ln_b):
    h = cin[..., :D_CONV] * jax.nn.sigmoid(cin[..., D_CONV:])
    h = jnp.pad(h, ((0, 0), (CONV_WIDTH - 1, 0), (0, 0)))
    h = lax.conv_general_dilated(h, conv_w[:, None, :].astype(h.dtype), window_strides=(1,),
                                 padding='VALID', dimension_numbers=('NWC', 'WIO', 'NWC'),
                                 feature_group_count=D_CONV) + conv_b
    return jax.nn.silu(layer_norm(h, ln_g, ln_b))


def _fwd_setup_inputs(seed: int = 0) -> dict:
    key = jax.random.key(seed)
    ks = jax.random.split(key, 26)
    L, D = DEPTH, D_MODEL

    def nrm(k, shape, scale):
        return jax.random.normal(k, shape, jnp.float32) * scale

    return {
        'x': nrm(ks[0], (BATCH, SEQ, D), 1.0),
        'c': nrm(ks[1], (BATCH, D), 1.0),
        'w_ada': nrm(ks[2], (L, D, 6 * D), 0.5 * D ** -0.5),
        'b_ada': nrm(ks[3], (L, 6 * D), 0.02),
        'w_in': nrm(ks[4], (L, D, D_IN), D ** -0.5),
        'b_gate': nrm(ks[5], (L, N_BRANCH * D), 0.1),
        'w_pool': nrm(ks[6], (L, len(POOL_WINDOWS), POOL_GROUP, POOL_GROUP), POOL_GROUP ** -0.5),
        'pool_scale': 1.0 + nrm(ks[7], (L, D_POOL), 0.1),
        'rel_bias': nrm(ks[8], (L, N_HEADS, N_REL), 0.1),
        'conv_w': nrm(ks[9], (L, CONV_WIDTH, D_CONV), CONV_WIDTH ** -0.5),
        'conv_b': nrm(ks[10], (L, D_CONV), 0.02),
        'conv_ln_g': 1.0 + nrm(ks[11], (L, D_CONV), 0.05),
        'conv_ln_b': nrm(ks[12], (L, D_CONV), 0.02),
        'w_br_pool': nrm(ks[13], (L, D_POOL, D), BETA * D_POOL ** -0.5),
        'w_br_attn': nrm(ks[14], (L, D_ATTN, D), BETA * D_ATTN ** -0.5),
        'w_br_conv': nrm(ks[15], (L, D_CONV, D), BETA * D_CONV ** -0.5),
        'w_o': nrm(ks[16], (L, D, D), BETA * D ** -0.5),
        'ln_mix_g': 1.0 + nrm(ks[17], (L, D), 0.05),
        'ln_mix_b': nrm(ks[18], (L, D), 0.02),
        'w_ff1': nrm(ks[19], (L, D, D_FF), D ** -0.5),
        'b_ff1': nrm(ks[20], (L, D_FF), 0.02),
        'w_ff2': nrm(ks[21], (L, D_FF, D), BETA * D_FF ** -0.5),
        'b_ff2': nrm(ks[22], (L, D), 0.02),
        'ln_ff_g': 1.0 + nrm(ks[23], (L, D), 0.05),
        'ln_ff_b': nrm(ks[24], (L, D), 0.02),
    }


def _fwd_reference(x, c, w_ada, b_ada, w_in, b_gate, w_pool, pool_scale, rel_bias, conv_w, conv_b,
              conv_ln_g, conv_ln_b, w_br_pool, w_br_attn, w_br_conv, w_o, ln_mix_g, ln_mix_b,
              w_ff1, b_ff1, w_ff2, b_ff2, ln_ff_g, ln_ff_b):
    B, S, _ = x.shape
    c_act = jax.nn.silu(c)
    for l in range(DEPTH):
        mod = (c_act @ w_ada[l] + b_ada[l])[:, None, :]
        sh_m, sc_m, g_m, sh_f, sc_f, g_f = jnp.split(mod, 6, axis=-1)

        u = layer_norm(x) * (1 + sc_m) + sh_m
        z = u @ w_in[l]
        z_pool, z_q, z_k, z_v, z_conv, z_gate = jnp.split(z, IN_SPLITS, axis=-1)
        y_pool = pool_mixer(z_pool, w_pool[l], pool_scale[l]) @ w_br_pool[l]
        y_attn = chunk_attention(z_q, z_k, z_v, rel_bias[l]) @ w_br_attn[l]
        y_conv = conv_module(z_conv, conv_w[l], conv_b[l], conv_ln_g[l], conv_ln_b[l]) @ w_br_conv[l]
        gates = jax.nn.sigmoid(z_gate + b_gate[l]).reshape(B, S, N_BRANCH, D_MODEL)
        merged = gates[:, :, 0] * y_pool + gates[:, :, 1] * y_attn + gates[:, :, 2] * y_conv
        mix_out = merged @ w_o[l]
        x = layer_norm(ALPHA * x + g_m * mix_out, ln_mix_g[l], ln_mix_b[l])

        u = layer_norm(x) * (1 + sc_f) + sh_f
        h = jnp.square(jax.nn.relu(u @ w_ff1[l] + b_ff1[l]))
        ff_out = h @ w_ff2[l] + b_ff2[l]
        x = layer_norm(ALPHA * x + g_f * ff_out, ln_ff_g[l], ln_ff_b[l])
    return x


import jax as _jax
import jax.numpy as _jnp

TWIN_FORMAT = 'train_step'
FWD_PARAMS = ['x', 'c', 'w_ada', 'b_ada', 'w_in', 'b_gate', 'w_pool', 'pool_scale', 'rel_bias', 'conv_w', 'conv_b', 'conv_ln_g', 'conv_ln_b', 'w_br_pool', 'w_br_attn', 'w_br_conv', 'w_o', 'ln_mix_g', 'ln_mix_b', 'w_ff1', 'b_ff1', 'w_ff2', 'b_ff2', 'ln_ff_g', 'ln_ff_b']
TWIN_WEIGHTS = ['w_ada', 'b_ada', 'w_in', 'b_gate', 'w_pool', 'pool_scale', 'rel_bias', 'conv_w', 'conv_b', 'conv_ln_g', 'conv_ln_b', 'w_br_pool', 'w_br_attn', 'w_br_conv', 'w_o', 'ln_mix_g', 'ln_mix_b', 'w_ff1', 'b_ff1', 'w_ff2', 'b_ff2', 'ln_ff_g', 'ln_ff_b']
TWIN_DIFF_INPUT = 'x'
TWIN_INPUTS = ['x', 'c', 'w_ada', 'b_ada', 'w_in', 'b_gate', 'w_pool', 'pool_scale', 'rel_bias', 'conv_w', 'conv_b', 'conv_ln_g', 'conv_ln_b', 'w_br_pool', 'w_br_attn', 'w_br_conv', 'w_o', 'ln_mix_g', 'ln_mix_b', 'w_ff1', 'b_ff1', 'w_ff2', 'b_ff2', 'ln_ff_g', 'ln_ff_b', 'loss_target', 'm_w_ada', 'm_b_ada', 'm_w_in', 'm_b_gate', 'm_w_pool', 'm_pool_scale', 'm_rel_bias', 'm_conv_w', 'm_conv_b', 'm_conv_ln_g', 'm_conv_ln_b', 'm_w_br_pool', 'm_w_br_attn', 'm_w_br_conv', 'm_w_o', 'm_ln_mix_g', 'm_ln_mix_b', 'm_w_ff1', 'm_b_ff1', 'm_w_ff2', 'm_b_ff2', 'm_ln_ff_g', 'm_ln_ff_b', 'v_w_ada', 'v_b_ada', 'v_w_in', 'v_b_gate', 'v_w_pool', 'v_pool_scale', 'v_rel_bias', 'v_conv_w', 'v_conv_b', 'v_conv_ln_g', 'v_conv_ln_b', 'v_w_br_pool', 'v_w_br_attn', 'v_w_br_conv', 'v_w_o', 'v_ln_mix_g', 'v_ln_mix_b', 'v_w_ff1', 'v_b_ff1', 'v_w_ff2', 'v_b_ff2', 'v_ln_ff_g', 'v_ln_ff_b']
TWIN_OUTPUTS = ['loss', 'grad_x', 'grad_w_ada', 'grad_b_ada', 'grad_w_in', 'grad_b_gate', 'grad_w_pool', 'grad_pool_scale', 'grad_rel_bias', 'grad_conv_w', 'grad_conv_b', 'grad_conv_ln_g', 'grad_conv_ln_b', 'grad_w_br_pool', 'grad_w_br_attn', 'grad_w_br_conv', 'grad_w_o', 'grad_ln_mix_g', 'grad_ln_mix_b', 'grad_w_ff1', 'grad_b_ff1', 'grad_w_ff2', 'grad_b_ff2', 'grad_ln_ff_g', 'grad_ln_ff_b', 'delta_w_ada', 'delta_b_ada', 'delta_w_in', 'delta_b_gate', 'delta_w_pool', 'delta_pool_scale', 'delta_rel_bias', 'delta_conv_w', 'delta_conv_b', 'delta_conv_ln_g', 'delta_conv_ln_b', 'delta_w_br_pool', 'delta_w_br_attn', 'delta_w_br_conv', 'delta_w_o', 'delta_ln_mix_g', 'delta_ln_mix_b', 'delta_w_ff1', 'delta_b_ff1', 'delta_w_ff2', 'delta_b_ff2', 'delta_ln_ff_g', 'delta_ln_ff_b', 'new_m_w_ada', 'new_m_b_ada', 'new_m_w_in', 'new_m_b_gate', 'new_m_w_pool', 'new_m_pool_scale', 'new_m_rel_bias', 'new_m_conv_w', 'new_m_conv_b', 'new_m_conv_ln_g', 'new_m_conv_ln_b', 'new_m_w_br_pool', 'new_m_w_br_attn', 'new_m_w_br_conv', 'new_m_w_o', 'new_m_ln_mix_g', 'new_m_ln_mix_b', 'new_m_w_ff1', 'new_m_b_ff1', 'new_m_w_ff2', 'new_m_b_ff2', 'new_m_ln_ff_g', 'new_m_ln_ff_b', 'new_v_w_ada', 'new_v_b_ada', 'new_v_w_in', 'new_v_b_gate', 'new_v_w_pool', 'new_v_pool_scale', 'new_v_rel_bias', 'new_v_conv_w', 'new_v_conv_b', 'new_v_conv_ln_g', 'new_v_conv_ln_b', 'new_v_w_br_pool', 'new_v_w_br_attn', 'new_v_w_br_conv', 'new_v_w_o', 'new_v_ln_mix_g', 'new_v_ln_mix_b', 'new_v_w_ff1', 'new_v_b_ff1', 'new_v_w_ff2', 'new_v_b_ff2', 'new_v_ln_ff_g', 'new_v_ln_ff_b']
TWIN_LEAF_KINDS = {'loss': 'loss', 'grad_x': 'grad_x', 'grad_w_ada': 'grad_w', 'grad_b_ada': 'grad_w', 'grad_w_in': 'grad_w', 'grad_b_gate': 'grad_w', 'grad_w_pool': 'grad_w', 'grad_pool_scale': 'grad_w', 'grad_rel_bias': 'grad_w', 'grad_conv_w': 'grad_w', 'grad_conv_b': 'grad_w', 'grad_conv_ln_g': 'grad_w', 'grad_conv_ln_b': 'grad_w', 'grad_w_br_pool': 'grad_w', 'grad_w_br_attn': 'grad_w', 'grad_w_br_conv': 'grad_w', 'grad_w_o': 'grad_w', 'grad_ln_mix_g': 'grad_w', 'grad_ln_mix_b': 'grad_w', 'grad_w_ff1': 'grad_w', 'grad_b_ff1': 'grad_w', 'grad_w_ff2': 'grad_w', 'grad_b_ff2': 'grad_w', 'grad_ln_ff_g': 'grad_w', 'grad_ln_ff_b': 'grad_w', 'delta_w_ada': 'delta_w', 'delta_b_ada': 'delta_w', 'delta_w_in': 'delta_w', 'delta_b_gate': 'delta_w', 'delta_w_pool': 'delta_w', 'delta_pool_scale': 'delta_w', 'delta_rel_bias': 'delta_w', 'delta_conv_w': 'delta_w', 'delta_conv_b': 'delta_w', 'delta_conv_ln_g': 'delta_w', 'delta_conv_ln_b': 'delta_w', 'delta_w_br_pool': 'delta_w', 'delta_w_br_attn': 'delta_w', 'delta_w_br_conv': 'delta_w', 'delta_w_o': 'delta_w', 'delta_ln_mix_g': 'delta_w', 'delta_ln_mix_b': 'delta_w', 'delta_w_ff1': 'delta_w', 'delta_b_ff1': 'delta_w', 'delta_w_ff2': 'delta_w', 'delta_b_ff2': 'delta_w', 'delta_ln_ff_g': 'delta_w', 'delta_ln_ff_b': 'delta_w', 'new_m_w_ada': 'new_m', 'new_m_b_ada': 'new_m', 'new_m_w_in': 'new_m', 'new_m_b_gate': 'new_m', 'new_m_w_pool': 'new_m', 'new_m_pool_scale': 'new_m', 'new_m_rel_bias': 'new_m', 'new_m_conv_w': 'new_m', 'new_m_conv_b': 'new_m', 'new_m_conv_ln_g': 'new_m', 'new_m_conv_ln_b': 'new_m', 'new_m_w_br_pool': 'new_m', 'new_m_w_br_attn': 'new_m', 'new_m_w_br_conv': 'new_m', 'new_m_w_o': 'new_m', 'new_m_ln_mix_g': 'new_m', 'new_m_ln_mix_b': 'new_m', 'new_m_w_ff1': 'new_m', 'new_m_b_ff1': 'new_m', 'new_m_w_ff2': 'new_m', 'new_m_b_ff2': 'new_m', 'new_m_ln_ff_g': 'new_m', 'new_m_ln_ff_b': 'new_m', 'new_v_w_ada': 'new_v', 'new_v_b_ada': 'new_v', 'new_v_w_in': 'new_v', 'new_v_b_gate': 'new_v', 'new_v_w_pool': 'new_v', 'new_v_pool_scale': 'new_v', 'new_v_rel_bias': 'new_v', 'new_v_conv_w': 'new_v', 'new_v_conv_b': 'new_v', 'new_v_conv_ln_g': 'new_v', 'new_v_conv_ln_b': 'new_v', 'new_v_w_br_pool': 'new_v', 'new_v_w_br_attn': 'new_v', 'new_v_w_br_conv': 'new_v', 'new_v_w_o': 'new_v', 'new_v_ln_mix_g': 'new_v', 'new_v_ln_mix_b': 'new_v', 'new_v_w_ff1': 'new_v', 'new_v_b_ff1': 'new_v', 'new_v_w_ff2': 'new_v', 'new_v_b_ff2': 'new_v', 'new_v_ln_ff_g': 'new_v', 'new_v_ln_ff_b': 'new_v'}


def _forward(args):
    return _fwd_reference(*[args[k] for k in FWD_PARAMS])


def _output_shape():
    out = _jax.eval_shape(lambda: _forward(_fwd_setup_inputs(0)))
    return out.shape, out.dtype

N_MICROBATCH = 1
ADAM_LR = 0.001
ADAM_B1 = 0.9
ADAM_B2 = 0.999
ADAM_EPS = 1e-08
ADAM_WD = 0.01
ADAM_STEP = 10
PER_EXAMPLE_BATCH_AXIS = {'x': 0, 'c': 0, 'loss_target': 0}
SHARED_INPUTS = []
_WEIGHT_DTYPES = {'w_ada': _jnp.float32, 'b_ada': _jnp.float32, 'w_in': _jnp.float32, 'b_gate': _jnp.float32, 'w_pool': _jnp.float32, 'pool_scale': _jnp.float32, 'rel_bias': _jnp.float32, 'conv_w': _jnp.float32, 'conv_b': _jnp.float32, 'conv_ln_g': _jnp.float32, 'conv_ln_b': _jnp.float32, 'w_br_pool': _jnp.float32, 'w_br_attn': _jnp.float32, 'w_br_conv': _jnp.float32, 'w_o': _jnp.float32, 'ln_mix_g': _jnp.float32, 'ln_mix_b': _jnp.float32, 'w_ff1': _jnp.float32, 'b_ff1': _jnp.float32, 'w_ff2': _jnp.float32, 'b_ff2': _jnp.float32, 'ln_ff_g': _jnp.float32, 'ln_ff_b': _jnp.float32}
MOMENT_SCALE = {'w_ada': 3.368313e-02, 'b_ada': 6.285660e-02, 'w_in': 3.079953e-03, 'b_gate': 1.470936e-03, 'w_pool': 1.011248e-02, 'pool_scale': 1.070591e-02, 'rel_bias': 6.070888e-04, 'conv_w': 7.192320e-03, 'conv_b': 1.672420e-02, 'conv_ln_g': 9.051906e-03, 'conv_ln_b': 1.135131e-02, 'w_br_pool': 9.865480e-03, 'w_br_attn': 4.608103e-03, 'w_br_conv': 7.287847e-03, 'w_o': 1.297632e-02, 'ln_mix_g': 2.932166e+00, 'ln_mix_b': 4.408243e-01, 'w_ff1': 1.633324e-02, 'b_ff1': 1.891615e-02, 'w_ff2': 6.449463e-02, 'b_ff2': 6.982010e-02, 'ln_ff_g': 2.304417e+01, 'ln_ff_b': 1.382743e+00}


def _to_microbatches(a, axis):
    t = _jnp.moveaxis(a, axis, 0)
    t = t.reshape((N_MICROBATCH, t.shape[0] // N_MICROBATCH) + t.shape[1:])
    return _jnp.moveaxis(t, 1, axis + 1)


def setup_inputs(seed: int = 0) -> dict:
    inp = _fwd_setup_inputs(seed)
    key = _jax.random.fold_in(_jax.random.key(seed), 7919)
    shape, _ = _output_shape()
    out = dict(inp)
    out["loss_target"] = _jax.random.normal(_jax.random.fold_in(key, 0), shape, _jnp.float32)
    for i, name in enumerate(TWIN_WEIGHTS):
        w = inp[name].astype(_jnp.float32)
        if MOMENT_SCALE is None:
            s = _jnp.sqrt(_jnp.mean(_jnp.square(w)) + 1e-30)
        else:
            s = MOMENT_SCALE[name]
        km, kv = _jax.random.split(_jax.random.fold_in(key, i + 1))
        out[name] = w
        out["m_" + name] = s * _jax.random.normal(km, w.shape, _jnp.float32)
        out["v_" + name] = (s * s) * _jax.random.uniform(kv, w.shape, _jnp.float32, 0.5, 1.5)
    if N_MICROBATCH > 1:
        for name, axis in PER_EXAMPLE_BATCH_AXIS.items():
            out[name] = _to_microbatches(out[name], axis)
    return {'x': out['x'], 'c': out['c'], 'w_ada': out['w_ada'], 'b_ada': out['b_ada'], 'w_in': out['w_in'], 'b_gate': out['b_gate'], 'w_pool': out['w_pool'], 'pool_scale': out['pool_scale'], 'rel_bias': out['rel_bias'], 'conv_w': out['conv_w'], 'conv_b': out['conv_b'], 'conv_ln_g': out['conv_ln_g'], 'conv_ln_b': out['conv_ln_b'], 'w_br_pool': out['w_br_pool'], 'w_br_attn': out['w_br_attn'], 'w_br_conv': out['w_br_conv'], 'w_o': out['w_o'], 'ln_mix_g': out['ln_mix_g'], 'ln_mix_b': out['ln_mix_b'], 'w_ff1': out['w_ff1'], 'b_ff1': out['b_ff1'], 'w_ff2': out['w_ff2'], 'b_ff2': out['b_ff2'], 'ln_ff_g': out['ln_ff_g'], 'ln_ff_b': out['ln_ff_b'], 'loss_target': out['loss_target'], 'm_w_ada': out['m_w_ada'], 'm_b_ada': out['m_b_ada'], 'm_w_in': out['m_w_in'], 'm_b_gate': out['m_b_gate'], 'm_w_pool': out['m_w_pool'], 'm_pool_scale': out['m_pool_scale'], 'm_rel_bias': out['m_rel_bias'], 'm_conv_w': out['m_conv_w'], 'm_conv_b': out['m_conv_b'], 'm_conv_ln_g': out['m_conv_ln_g'], 'm_conv_ln_b': out['m_conv_ln_b'], 'm_w_br_pool': out['m_w_br_pool'], 'm_w_br_attn': out['m_w_br_attn'], 'm_w_br_conv': out['m_w_br_conv'], 'm_w_o': out['m_w_o'], 'm_ln_mix_g': out['m_ln_mix_g'], 'm_ln_mix_b': out['m_ln_mix_b'], 'm_w_ff1': out['m_w_ff1'], 'm_b_ff1': out['m_b_ff1'], 'm_w_ff2': out['m_w_ff2'], 'm_b_ff2': out['m_b_ff2'], 'm_ln_ff_g': out['m_ln_ff_g'], 'm_ln_ff_b': out['m_ln_ff_b'], 'v_w_ada': out['v_w_ada'], 'v_b_ada': out['v_b_ada'], 'v_w_in': out['v_w_in'], 'v_b_gate': out['v_b_gate'], 'v_w_pool': out['v_w_pool'], 'v_pool_scale': out['v_pool_scale'], 'v_rel_bias': out['v_rel_bias'], 'v_conv_w': out['v_conv_w'], 'v_conv_b': out['v_conv_b'], 'v_conv_ln_g': out['v_conv_ln_g'], 'v_conv_ln_b': out['v_conv_ln_b'], 'v_w_br_pool': out['v_w_br_pool'], 'v_w_br_attn': out['v_w_br_attn'], 'v_w_br_conv': out['v_w_br_conv'], 'v_w_o': out['v_w_o'], 'v_ln_mix_g': out['v_ln_mix_g'], 'v_ln_mix_b': out['v_ln_mix_b'], 'v_w_ff1': out['v_w_ff1'], 'v_b_ff1': out['v_b_ff1'], 'v_w_ff2': out['v_w_ff2'], 'v_b_ff2': out['v_b_ff2'], 'v_ln_ff_g': out['v_ln_ff_g'], 'v_ln_ff_b': out['v_ln_ff_b']}


def _loss(weights, diff, rest, loss_target):
    with _jax.named_scope("forward"):
        args = {**rest, TWIN_DIFF_INPUT: diff, **{k: w.astype(_WEIGHT_DTYPES[k]) for k, w in weights.items()}}
        y = _forward(args)
    with _jax.named_scope("loss_head"):
        err = _jnp.square(y.astype(_jnp.float32) - loss_target)
        return 0.5 * _jnp.sum(_jnp.mean(err, axis=-1)) if err.ndim else 0.5 * err


def _adamw(w, g, m, v):
    m = ADAM_B1 * m + (1.0 - ADAM_B1) * g
    v = ADAM_B2 * v + (1.0 - ADAM_B2) * _jnp.square(g)
    m_hat = m / (1.0 - ADAM_B1 ** ADAM_STEP)
    v_hat = v / (1.0 - ADAM_B2 ** ADAM_STEP)
    delta = -ADAM_LR * (m_hat / (_jnp.sqrt(v_hat) + ADAM_EPS) + ADAM_WD * w)
    return delta, m, v


def reference(x, c, w_ada, b_ada, w_in, b_gate, w_pool, pool_scale, rel_bias, conv_w, conv_b, conv_ln_g, conv_ln_b, w_br_pool, w_br_attn, w_br_conv, w_o, ln_mix_g, ln_mix_b, w_ff1, b_ff1, w_ff2, b_ff2, ln_ff_g, ln_ff_b, loss_target, m_w_ada, m_b_ada, m_w_in, m_b_gate, m_w_pool, m_pool_scale, m_rel_bias, m_conv_w, m_conv_b, m_conv_ln_g, m_conv_ln_b, m_w_br_pool, m_w_br_attn, m_w_br_conv, m_w_o, m_ln_mix_g, m_ln_mix_b, m_w_ff1, m_b_ff1, m_w_ff2, m_b_ff2, m_ln_ff_g, m_ln_ff_b, v_w_ada, v_b_ada, v_w_in, v_b_gate, v_w_pool, v_pool_scale, v_rel_bias, v_conv_w, v_conv_b, v_conv_ln_g, v_conv_ln_b, v_w_br_pool, v_w_br_attn, v_w_br_conv, v_w_o, v_ln_mix_g, v_ln_mix_b, v_w_ff1, v_b_ff1, v_w_ff2, v_b_ff2, v_ln_ff_g, v_ln_ff_b):
    given = dict(x=x, c=c, w_ada=w_ada, b_ada=b_ada, w_in=w_in, b_gate=b_gate, w_pool=w_pool, pool_scale=pool_scale, rel_bias=rel_bias, conv_w=conv_w, conv_b=conv_b, conv_ln_g=conv_ln_g, conv_ln_b=conv_ln_b, w_br_pool=w_br_pool, w_br_attn=w_br_attn, w_br_conv=w_br_conv, w_o=w_o, ln_mix_g=ln_mix_g, ln_mix_b=ln_mix_b, w_ff1=w_ff1, b_ff1=b_ff1, w_ff2=w_ff2, b_ff2=b_ff2, ln_ff_g=ln_ff_g, ln_ff_b=ln_ff_b, loss_target=loss_target, m_w_ada=m_w_ada, m_b_ada=m_b_ada, m_w_in=m_w_in, m_b_gate=m_b_gate, m_w_pool=m_w_pool, m_pool_scale=m_pool_scale, m_rel_bias=m_rel_bias, m_conv_w=m_conv_w, m_conv_b=m_conv_b, m_conv_ln_g=m_conv_ln_g, m_conv_ln_b=m_conv_ln_b, m_w_br_pool=m_w_br_pool, m_w_br_attn=m_w_br_attn, m_w_br_conv=m_w_br_conv, m_w_o=m_w_o, m_ln_mix_g=m_ln_mix_g, m_ln_mix_b=m_ln_mix_b, m_w_ff1=m_w_ff1, m_b_ff1=m_b_ff1, m_w_ff2=m_w_ff2, m_b_ff2=m_b_ff2, m_ln_ff_g=m_ln_ff_g, m_ln_ff_b=m_ln_ff_b, v_w_ada=v_w_ada, v_b_ada=v_b_ada, v_w_in=v_w_in, v_b_gate=v_b_gate, v_w_pool=v_w_pool, v_pool_scale=v_pool_scale, v_rel_bias=v_rel_bias, v_conv_w=v_conv_w, v_conv_b=v_conv_b, v_conv_ln_g=v_conv_ln_g, v_conv_ln_b=v_conv_ln_b, v_w_br_pool=v_w_br_pool, v_w_br_attn=v_w_br_attn, v_w_br_conv=v_w_br_conv, v_w_o=v_w_o, v_ln_mix_g=v_ln_mix_g, v_ln_mix_b=v_ln_mix_b, v_w_ff1=v_w_ff1, v_b_ff1=v_b_ff1, v_w_ff2=v_w_ff2, v_b_ff2=v_b_ff2, v_ln_ff_g=v_ln_ff_g, v_ln_ff_b=v_ln_ff_b)
    weights = {n: given[n] for n in TWIN_WEIGHTS}
    shared = {n: given[n] for n in SHARED_INPUTS}
    per_example = {n: given[n] for n in ['x', 'c']}
    grad_fn = _jax.value_and_grad(_loss, argnums=(0, 1))

    def one_microbatch(ex, loss_target):
        ex = dict(ex)
        diff = ex.pop(TWIN_DIFF_INPUT)
        return grad_fn(weights, diff, {**shared, **ex}, loss_target)

    if N_MICROBATCH == 1:
        loss, (grad_w, grad_x) = one_microbatch(per_example, given["loss_target"])
    else:
        def body(carry, xs):
            loss_sum, grad_sum = carry
            l_k, (gw_k, gx_k) = one_microbatch(xs[0], xs[1])
            with _jax.named_scope("update"):
                return (loss_sum + l_k, _jax.tree.map(_jnp.add, grad_sum, gw_k)), gx_k

        init = (_jnp.zeros((), _jnp.float32), _jax.tree.map(_jnp.zeros_like, weights))
        (loss, grad_w), grad_x = _jax.lax.scan(body, init, (per_example, given["loss_target"]))
    with _jax.named_scope("update"):
        delta_w, new_m, new_v = {}, {}, {}
        for n in TWIN_WEIGHTS:
            delta_w[n], new_m[n], new_v[n] = _adamw(weights[n], grad_w[n], given["m_" + n], given["v_" + n])
    return (loss, grad_x, *[grad_w[n] for n in TWIN_WEIGHTS], *[delta_w[n] for n in TWIN_WEIGHTS],
            *[new_m[n] for n in TWIN_WEIGHTS], *[new_v[n] for n in TWIN_WEIGHTS])
```

```python
import functools

import jax
import jax.numpy as jnp
import numpy as np
from jax import lax
from jax.experimental import pallas as pl
from jax.experimental.pallas import tpu as pltpu

F32 = jnp.float32
BF16 = jnp.bfloat16

D_MODEL = 1024
DEPTH = 2
CHUNK = 64
POOL_WINDOWS = (2, 4, 8, 16)
POOL_GROUP = 64
D_POOL = 256
N_HEADS = 8
HEAD_DIM = 64
D_ATTN = 512
N_PREV_CHUNKS = 8
REL_CLIP = 128
N_REL = 2 * REL_CLIP + 1
D_CONV = 256
CONV_WIDTH = 31
D_FF = 4 * D_MODEL
D_IN = 5376
OFF_POOL, OFF_QKV, OFF_CONV, OFF_GATE = 0, 256, 1792, 2304
ALPHA = (2.0 * DEPTH) ** 0.25
LN_EPS = 1e-5
NEG_INF = -1e30
ADAM_LR, ADAM_B1, ADAM_B2, ADAM_EPS, ADAM_WD, ADAM_STEP = 0.001, 0.9, 0.999, 1e-08, 0.01, 10

N_DEV = 8
N_CHIP = 4
MESH = pl.DeviceIdType.MESH

QB = 2 * CHUNK
KPAD = N_PREV_CHUNKS * CHUNK
KW = QB + KPAD
SKEW_W = 768

VMEM_LIMIT = 56 * 1024 * 1024


def _cparams(**kw):
    return pltpu.CompilerParams(vmem_limit_bytes=VMEM_LIMIT, **kw)


def _full(shape):
    n = len(shape)
    return pl.BlockSpec(shape, lambda *_: (0,) * n)


_DIMS = {"nn": (((1,), (0,)), ((), ())), "nt": (((1,), (1,)), ((), ())), "tn": (((0,), (0,)), ((), ()))}


def _relu2(t):
    r = jnp.maximum(t, 0.0)
    return r * r


def _mm(a, b, mode, *, tm, tn, out_dtype, name, b_col0=0, n_out=None, a_fn=None, bias=None):
    if mode == "tn":
        k, m = a.shape
        n = b.shape[1] if n_out is None else n_out
        a_spec = pl.BlockSpec((k, tm), lambda i, j: (0, i))
        b_spec = pl.BlockSpec((k, tn), lambda i, j: (0, j + b_col0))
    elif mode == "nn":
        m, k = a.shape
        n = b.shape[1] if n_out is None else n_out
        a_spec = pl.BlockSpec((tm, k), lambda i, j: (i, 0))
        b_spec = pl.BlockSpec((k, tn), lambda i, j: (0, j + b_col0))
    else:
        m, k = a.shape
        n = b.shape[0] if n_out is None else n_out
        a_spec = pl.BlockSpec((tm, k), lambda i, j: (i, 0))
        b_spec = pl.BlockSpec((tn, k), lambda i, j: (j + b_col0, 0))
    assert m % tm == 0 and n % tn == 0, (name, m, n, tm, tn)
    dims = _DIMS[mode]

    def body(*refs):
        if bias is None:
            a_ref, b_ref, o_ref = refs
        else:
            a_ref, b_ref, bias_ref, o_ref = refs
        av = a_ref[...]
        if a_fn is not None:
            av = a_fn(av)
        acc = lax.dot_general(av.astype(BF16), b_ref[...].astype(BF16), dims, preferred_element_type=F32)
        if bias is not None:
            acc = acc + bias_ref[...]
        o_ref[...] = acc.astype(out_dtype)

    in_specs = [a_spec, b_spec]
    args = [a, b]
    if bias is not None:
        in_specs.append(pl.BlockSpec((1, tn), lambda i, j: (0, j)))
        args.append(bias)
    return pl.pallas_call(
        body, grid=(m // tm, n // tn), in_specs=in_specs,
        out_specs=pl.BlockSpec((tm, tn), lambda i, j: (i, j)),
        out_shape=jax.ShapeDtypeStruct((m, n), out_dtype), name=name, compiler_params=_cparams(),
    )(*args)


def _ln_hat(x):
    mu = jnp.mean(x, axis=-1, keepdims=True)
    xc = x - mu
    var = jnp.mean(xc * xc, axis=-1, keepdims=True)
    rstd = lax.rsqrt(var + LN_EPS)
    return xc * rstd, rstd


def _ln_hat_bwd(dhat, xhat, rstd):
    m1 = jnp.mean(dhat, axis=-1, keepdims=True)
    m2 = jnp.mean(dhat * xhat, axis=-1, keepdims=True)
    return rstd * (dhat - m1 - xhat * m2)


def _row_tile(s):
    return min(512, s)


def _acc_rows(ref, val, first):
    @pl.when(first)
    def _():
        ref[...] = jnp.zeros_like(ref)
    ref[...] += jnp.sum(val, axis=0, keepdims=True)


def _ln_mod(x, sc, sh, name):
    s, d = x.shape
    tm = _row_tile(s)

    def body(x_ref, sc_ref, sh_ref, u_ref):
        xhat, _ = _ln_hat(x_ref[...])
        u_ref[...] = (xhat * (1.0 + sc_ref[...]) + sh_ref[...]).astype(BF16)

    row = pl.BlockSpec((tm, d), lambda i: (i, 0))
    vec = pl.BlockSpec((1, d), lambda i: (0, 0))
    return pl.pallas_call(body, grid=(s // tm,), in_specs=[row, vec, vec], out_specs=row,
                          out_shape=jax.ShapeDtypeStruct((s, d), BF16), name=name, compiler_params=_cparams())(x, sc, sh)


def _ln_mod_bwd(du, x, sc, dres, name):
    s, d = x.shape
    tm = _row_tile(s)

    def body(du_ref, x_ref, sc_ref, dres_ref, dx_ref, dsc_ref, dsh_ref):
        first = pl.program_id(0) == 0
        duv = du_ref[...]
        xhat, rstd = _ln_hat(x_ref[...])
        dx_ref[...] = dres_ref[...] + _ln_hat_bwd(duv * (1.0 + sc_ref[...]), xhat, rstd)
        _acc_rows(dsc_ref, duv * xhat, first)
        _acc_rows(dsh_ref, duv, first)

    row = pl.BlockSpec((tm, d), lambda i: (i, 0))
    vec = pl.BlockSpec((1, d), lambda i: (0, 0))
    vs = jax.ShapeDtypeStruct((1, d), F32)
    return pl.pallas_call(body, grid=(s // tm,), in_specs=[row, row, vec, row], out_specs=[row, vec, vec],
                          out_shape=[jax.ShapeDtypeStruct((s, d), F32), vs, vs], name=name,
                          compiler_params=_cparams())(du, x, sc, dres)


def _resid_ln(x, f, g, gam, bet, name):
    s, d = x.shape
    tm = _row_tile(s)

    def body(x_ref, f_ref, g_ref, gam_ref, bet_ref, o_ref):
        rhat, _ = _ln_hat(ALPHA * x_ref[...] + g_ref[...] * f_ref[...])
        o_ref[...] = rhat * gam_ref[...] + bet_ref[...]

    row = pl.BlockSpec((tm, d), lambda i: (i, 0))
    vec = pl.BlockSpec((1, d), lambda i: (0, 0))
    return pl.pallas_call(body, grid=(s // tm,), in_specs=[row, row, vec, vec, vec], out_specs=row,
                          out_shape=jax.ShapeDtypeStruct((s, d), F32), name=name, compiler_params=_cparams())(x, f, g, gam, bet)


def _resid_ln_bwd(dxo, x, f, g, gam, name):
    s, d = x.shape
    tm = _row_tile(s)

    def body(dxo_ref, x_ref, f_ref, g_ref, gam_ref, dres_ref, df_ref, dgam_ref, dbet_ref, dg_ref, dbias_ref):
        first = pl.program_id(0) == 0
        dxov = dxo_ref[...]
        fv = f_ref[...]
        rhat, rstd = _ln_hat(ALPHA * x_ref[...] + g_ref[...] * fv)
        dr = _ln_hat_bwd(dxov * gam_ref[...], rhat, rstd)
        dfv = g_ref[...] * dr
        dres_ref[...] = ALPHA * dr
        df_ref[...] = dfv.astype(BF16)
        _acc_rows(dgam_ref, dxov * rhat, first)
        _acc_rows(dbet_ref, dxov, first)
        _acc_rows(dg_ref, dr * fv, first)
        _acc_rows(dbias_ref, dfv, first)

    row = pl.BlockSpec((tm, d), lambda i: (i, 0))
    vec = pl.BlockSpec((1, d), lambda i: (0, 0))
    vs = jax.ShapeDtypeStruct((1, d), F32)
    return pl.pallas_call(body, grid=(s // tm,), in_specs=[row, row, row, vec, vec],
                          out_specs=[row, row, vec, vec, vec, vec],
                          out_shape=[jax.ShapeDtypeStruct((s, d), F32), jax.ShapeDtypeStruct((s, d), BF16), vs, vs, vs, vs],
                          name=name, compiler_params=_cparams())(dxo, x, f, g, gam)


def _loss_grad(y, tgt, name):
    s, d = y.shape
    tm = _row_tile(s)
    n = s // tm

    def body(y_ref, t_ref, dy_ref, loss_ref, acc_ref):
        i = pl.program_id(0)
        e = y_ref[...] - t_ref[...]
        dy_ref[...] = e * (1.0 / d)
        _acc_rows(acc_ref, e * e, i == 0)

        @pl.when(i == n - 1)
        def _():
            tot = jnp.sum(acc_ref[...], axis=1, keepdims=True) * (0.5 / d)
            loss_ref[...] = jnp.broadcast_to(tot, (1, 128))

    row = pl.BlockSpec((tm, d), lambda i: (i, 0))
    return pl.pallas_call(body, grid=(n,), in_specs=[row, row],
                          out_specs=[row, pl.BlockSpec((1, 128), lambda i: (0, 0))],
                          out_shape=[jax.ShapeDtypeStruct((s, d), F32), jax.ShapeDtypeStruct((1, 128), F32)],
                          scratch_shapes=[pltpu.VMEM((1, d), F32)], name=name, compiler_params=_cparams())(y, tgt)


POOL_HALO = 16
POOL_ROWS = 256


def _pool_counts(r0, rows):
    t1 = (lax.broadcasted_iota(jnp.int32, (rows, 128), 0) + r0 + 1).astype(F32)
    low = lax.broadcasted_iota(jnp.int32, (rows, 128), 1) < POOL_GROUP
    wa = jnp.where(low, float(POOL_WINDOWS[0]), float(POOL_WINDOWS[1]))
    wb = jnp.where(low, float(POOL_WINDOWS[2]), float(POOL_WINDOWS[3]))
    return jnp.minimum(t1, wa), jnp.minimum(t1, wb), low


def _window_sums(win, off, rows, sign):
    def sl(j, half):
        return win[off + sign * j: off + sign * j + rows, 128 * half:128 * half + 128]
    a2 = sl(0, 0) + sl(1, 0)
    a4 = a2 + sl(2, 0) + sl(3, 0)
    a8 = sl(0, 1)
    for j in range(1, 8):
        a8 = a8 + sl(j, 1)
    a16 = a8
    for j in range(8, 16):
        a16 = a16 + sl(j, 1)
    return a2, a4, a8, a16


def _pool_fwd(zp, wp_bd, pscale, name):
    s = zp.shape[0]
    r = min(POOL_ROWS, s)

    def body(z_ref, wp_ref, sc_ref, p_ref, feat_ref, pad):
        pad[0:POOL_HALO, :] = jnp.zeros((POOL_HALO, D_POOL), F32)
        pad[POOL_HALO:, :] = z_ref[...]

        def step(i, carry):
            r0 = pl.multiple_of(i * r, r)
            win = pad[pl.ds(r0, r + POOL_HALO), :]
            a2, a4, a8, a16 = _window_sums(win, POOL_HALO, r, -1)
            ca, cb, low = _pool_counts(r0, r)
            x0 = win[POOL_HALO:, :]
            pa = jnp.where(low, a2, a4) / ca
            pb = jnp.where(low, a8, a16) / cb
            p = (jnp.concatenate([pa, pb], axis=1) - x0).astype(BF16)
            p_ref[pl.ds(r0, r), :] = p
            pw = jnp.dot(p, wp_ref[...], preferred_element_type=F32)
            feat_ref[pl.ds(r0, r), :] = (pw * sc_ref[...]).astype(BF16)
            return carry

        lax.fori_loop(0, s // r, step, 0)

    return pl.pallas_call(
        body, out_shape=[jax.ShapeDtypeStruct((s, D_POOL), BF16), jax.ShapeDtypeStruct((s, D_POOL), BF16)],
        scratch_shapes=[pltpu.VMEM((s + POOL_HALO, D_POOL), F32)], name=name, compiler_params=_cparams(),
    )(zp, wp_bd, pscale)


def _pool_bwd(dfeat, p, wp_bd, pscale, name):
    s = p.shape[0]
    r = min(POOL_ROWS, s)

    def body(df_ref, p_ref, wp_ref, sc_ref, dz_ref, dwp_ref, dsc_ref, gpad, dpbuf):
        dwp_ref[...] = jnp.zeros_like(dwp_ref)
        dsc_ref[...] = jnp.zeros_like(dsc_ref)
        gpad[s:, :] = jnp.zeros((POOL_HALO, D_POOL), F32)

        def step1(i, carry):
            r0 = pl.multiple_of(i * r, r)
            pv = p_ref[pl.ds(r0, r), :]
            dfv = df_ref[pl.ds(r0, r), :]
            pw = jnp.dot(pv, wp_ref[...], preferred_element_type=F32)
            dsc_ref[...] += jnp.sum(dfv * pw, axis=0, keepdims=True)
            dpw = (dfv * sc_ref[...]).astype(BF16)
            dwp_ref[...] += lax.dot_general(pv, dpw, _DIMS["tn"], preferred_element_type=F32)
            dp = lax.dot_general(dpw, wp_ref[...], _DIMS["nt"], preferred_element_type=F32)
            ca, cb, _ = _pool_counts(r0, r)
            gpad[pl.ds(r0, r), :] = dp / jnp.concatenate([ca, cb], axis=1)
            dpbuf[pl.ds(r0, r), :] = dp
            return carry

        lax.fori_loop(0, s // r, step1, 0)

        def step2(i, carry):
            r0 = pl.multiple_of(i * r, r)
            win = gpad[pl.ds(r0, r + POOL_HALO), :]
            a2, a4, a8, a16 = _window_sums(win, 0, r, 1)
            low = lax.broadcasted_iota(jnp.int32, (r, 128), 1) < POOL_GROUP
            acc = jnp.concatenate([jnp.where(low, a2, a4), jnp.where(low, a8, a16)], axis=1)
            dz_ref[pl.ds(r0, r), :] = (acc - dpbuf[pl.ds(r0, r), :]).astype(BF16)
            return carry

        lax.fori_loop(0, s // r, step2, 0)

    return pl.pallas_call(
        body,
        out_shape=[jax.ShapeDtypeStruct((s, D_POOL), BF16), jax.ShapeDtypeStruct((D_POOL, D_POOL), F32),
                   jax.ShapeDtypeStruct((1, D_POOL), F32)],
        scratch_shapes=[pltpu.VMEM((s + POOL_HALO, D_POOL), F32), pltpu.VMEM((s, D_POOL), F32)],
        name=name, compiler_params=_cparams(),
    )(dfeat, p, wp_bd, pscale)


def _skew_index():
    cp = lax.broadcasted_iota(jnp.int32, (SKEW_W, N_REL), 0)
    dist = jnp.where(cp < KW, KPAD - cp, KPAD + SKEW_W - cp)
    idx = jnp.clip(dist, -REL_CLIP, REL_CLIP) + REL_CLIP
    return (idx == lax.broadcasted_iota(jnp.int32, (SKEW_W, N_REL), 1)).astype(F32)


def _row_bits(b):
    return (lax.broadcasted_iota(jnp.int32, (QB, SKEW_W), 0) >> b) & 1 == 1


def _bias_block(rel_bias, name):
    def body(rb_ref, o_ref):
        onehot = _skew_index()
        row0 = lax.dot_general(rb_ref[...], onehot, _DIMS["nt"], precision=lax.Precision.HIGHEST,
                               preferred_element_type=F32)
        r = lax.broadcasted_iota(jnp.int32, (QB, KW), 0)
        kk = lax.broadcasted_iota(jnp.int32, (QB, KW), 1)
        cq, ck = r // CHUNK, kk // CHUNK
        band = (ck >= cq) & (ck <= cq + N_PREV_CHUNKS)
        for h in range(N_HEADS):
            t = jnp.broadcast_to(row0[h:h + 1, :], (QB, SKEW_W))
            for b in range(7):
                t = jnp.where(_row_bits(b), pltpu.roll(t, 1 << b, 1), t)
            o_ref[h] = jnp.where(band, t[:, :KW], NEG_INF)

    return pl.pallas_call(body, out_shape=jax.ShapeDtypeStruct((N_HEADS, QB, KW), F32), name=name,
                          compiler_params=_cparams())(rel_bias)


def _bias_block_bwd(ds_acc, name):
    def body(ds_ref, o_ref):
        sums = []
        for h in range(N_HEADS):
            t = jnp.concatenate([ds_ref[h], jnp.zeros((QB, SKEW_W - KW), F32)], axis=1)
            for b in range(7):
                t = jnp.where(_row_bits(b), pltpu.roll(t, SKEW_W - (1 << b), 1), t)
            sums.append(jnp.sum(t, axis=0, keepdims=True))
        allh = jnp.concatenate(sums, axis=0)
        o_ref[...] = jnp.dot(allh, _skew_index(), precision=lax.Precision.HIGHEST, preferred_element_type=F32)

    return pl.pallas_call(body, out_shape=jax.ShapeDtypeStruct((N_HEADS, N_REL), F32), name=name,
                          compiler_params=_cparams())(ds_acc)


def _scores(qh, kh, bias_h, valid):
    sc = lax.dot_general(qh, kh, _DIMS["nt"], preferred_element_type=F32) * (HEAD_DIM ** -0.5) + bias_h
    sc = jnp.where(valid, sc, NEG_INF)
    e = jnp.exp(sc - jnp.max(sc, axis=-1, keepdims=True))
    return e * (1.0 / jnp.sum(e, axis=-1, keepdims=True))


def _load_padded_kv(qkv_hbm, kpad, vpad, sems, s):
    kpad[0:KPAD, :] = jnp.zeros((KPAD, D_ATTN), BF16)
    vpad[0:KPAD, :] = jnp.zeros((KPAD, D_ATTN), BF16)
    ck = pltpu.make_async_copy(qkv_hbm.at[:, D_ATTN:2 * D_ATTN], kpad.at[pl.ds(KPAD, s), :], sems.at[0])
    cv = pltpu.make_async_copy(qkv_hbm.at[:, 2 * D_ATTN:3 * D_ATTN], vpad.at[pl.ds(KPAD, s), :], sems.at[1])
    ck.start()
    cv.start()
    ck.wait()
    cv.wait()


def _attn_fwd(qkv, bias, name):
    s = qkv.shape[0]

    def body(q_ref, qkv_hbm, bias_ref, o_ref, kpad, vpad, sems):
        i = pl.program_id(0)

        @pl.when(i == 0)
        def _():
            _load_padded_kv(qkv_hbm, kpad, vpad, sems, s)

        base = pl.multiple_of(i * QB, QB)
        kw = kpad[pl.ds(base, KW), :]
        vw = vpad[pl.ds(base, KW), :]
        q = q_ref[...]
        valid = lax.broadcasted_iota(jnp.int32, (QB, KW), 1) >= KPAD - base
        outs = []
        for h in range(N_HEADS):
            hs = slice(HEAD_DIM * h, HEAD_DIM * (h + 1))
            p = _scores(q[:, hs], kw[:, hs], bias_ref[h], valid)
            outs.append(jnp.dot(p.astype(BF16), vw[:, hs], preferred_element_type=F32))
        o_ref[...] = jnp.concatenate(outs, axis=1).astype(BF16)

    return pl.pallas_call(
        body, grid=(s // QB,),
        in_specs=[pl.BlockSpec((QB, D_ATTN), lambda i: (i, 0)), pl.BlockSpec(memory_space=pl.ANY),
                  _full((N_HEADS, QB, KW))],
        out_specs=pl.BlockSpec((QB, D_ATTN), lambda i: (i, 0)),
        out_shape=jax.ShapeDtypeStruct((s, D_ATTN), BF16),
        scratch_shapes=[pltpu.VMEM((s + KPAD, D_ATTN), BF16), pltpu.VMEM((s + KPAD, D_ATTN), BF16),
                        pltpu.SemaphoreType.DMA((2,))],
        name=name, compiler_params=_cparams(),
    )(qkv, qkv, bias)


def _attn_bwd(qkv, do, bias, name):
    s = qkv.shape[0]
    n = s // QB

    def body(q_ref, qkv_hbm, do_ref, bias_ref, dq_ref, dk_hbm, dv_hbm, ds_ref, kpad, vpad, dkacc, dvacc, sems):
        i = pl.program_id(0)

        @pl.when(i == 0)
        def _():
            _load_padded_kv(qkv_hbm, kpad, vpad, sems, s)
            dkacc[...] = jnp.zeros_like(dkacc)
            dvacc[...] = jnp.zeros_like(dvacc)
            ds_ref[...] = jnp.zeros_like(ds_ref)

        base = pl.multiple_of(i * QB, QB)
        kw = kpad[pl.ds(base, KW), :]
        vw = vpad[pl.ds(base, KW), :]
        q = q_ref[...]
        dov = do_ref[...]
        valid = lax.broadcasted_iota(jnp.int32, (QB, KW), 1) >= KPAD - base
        dqs, dks, dvs = [], [], []
        for h in range(N_HEADS):
            hs = slice(HEAD_DIM * h, HEAD_DIM * (h + 1))
            qh, kh, vh, doh = q[:, hs], kw[:, hs], vw[:, hs], dov[:, hs]
            p = _scores(qh, kh, bias_ref[h], valid)
            dvs.append(lax.dot_general(p.astype(BF16), doh, _DIMS["tn"], preferred_element_type=F32))
            dp = lax.dot_general(doh, vh, _DIMS["nt"], preferred_element_type=F32)
            ds = p * (dp - jnp.sum(dp * p, axis=-1, keepdims=True))
            ds_ref[h] += ds
            dsb = ds.astype(BF16)
            dqs.append(jnp.dot(dsb, kh, preferred_element_type=F32) * (HEAD_DIM ** -0.5))
            dks.append(lax.dot_general(dsb, qh, _DIMS["tn"], preferred_element_type=F32) * (HEAD_DIM ** -0.5))
        dq_ref[...] = jnp.concatenate(dqs, axis=1).astype(BF16)
        dkacc[pl.ds(base, KW), :] += jnp.concatenate(dks, axis=1)
        dvacc[pl.ds(base, KW), :] += jnp.concatenate(dvs, axis=1)

        @pl.when(i == n - 1)
        def _():
            ck = pltpu.make_async_copy(dkacc, dk_hbm, sems.at[0])
            cv = pltpu.make_async_copy(dvacc, dv_hbm, sems.at[1])
            ck.start()
            cv.start()
            ck.wait()
            cv.wait()

    blk = pl.BlockSpec((QB, D_ATTN), lambda i: (i, 0))
    acc_shape = jax.ShapeDtypeStruct((s + KPAD, D_ATTN), F32)
    return pl.pallas_call(
        body, grid=(n,),
        in_specs=[blk, pl.BlockSpec(memory_space=pl.ANY), blk, _full((N_HEADS, QB, KW))],
        out_specs=[blk, pl.BlockSpec(memory_space=pl.ANY), pl.BlockSpec(memory_space=pl.ANY), _full((N_HEADS, QB, KW))],
        out_shape=[jax.ShapeDtypeStruct((s, D_ATTN), BF16), acc_shape, acc_shape,
                   jax.ShapeDtypeStruct((N_HEADS, QB, KW), F32)],
        scratch_shapes=[pltpu.VMEM((s + KPAD, D_ATTN), BF16), pltpu.VMEM((s + KPAD, D_ATTN), BF16),
                        pltpu.VMEM((s + KPAD, D_ATTN), F32), pltpu.VMEM((s + KPAD, D_ATTN), F32),
                        pltpu.SemaphoreType.DMA((2,))],
        name=name, compiler_params=_cparams(),
    )(qkv, qkv, do, bias)


CONV_HALO = 32
CONV_ROWS = 64


def _sigmoid(t):
    return 1.0 / (1.0 + jnp.exp(-t))


def _glu_rows(z_ref, r0, rows):
    a = z_ref[pl.ds(r0, rows), 0:D_CONV]
    b = z_ref[pl.ds(r0, rows), D_CONV:2 * D_CONV]
    return a, _sigmoid(b)


def _conv_fwd(zc, conv_w, conv_b, ln_g, ln_b, name):
    s = zc.shape[0]
    rt = min(256, s)

    def body(z_ref, w_ref, cb_ref, g_ref, b_ref, cv_ref, feat_ref, hpad):
        hpad[0:CONV_HALO, :] = jnp.zeros((CONV_HALO, D_CONV), F32)

        def glu(i, carry):
            r0 = pl.multiple_of(i * rt, rt)
            a, sb = _glu_rows(z_ref, r0, rt)
            hpad[pl.ds(r0 + CONV_HALO, rt), :] = a * sb
            return carry

        lax.fori_loop(0, s // rt, glu, 0)
        w = w_ref[...]

        def conv(i, carry):
            r0 = pl.multiple_of(i * CONV_ROWS, CONV_ROWS)
            win = hpad[pl.ds(r0, CONV_ROWS + CONV_HALO), :]
            acc = jnp.broadcast_to(cb_ref[...], (CONV_ROWS, D_CONV))
            for k in range(CONV_WIDTH):
                acc = acc + win[2 + k:2 + k + CONV_ROWS, :] * w[k:k + 1, :]
            cv_ref[pl.ds(r0, CONV_ROWS), :] = acc
            yhat, _ = _ln_hat(acc)
            y = yhat * g_ref[...] + b_ref[...]
            feat_ref[pl.ds(r0, CONV_ROWS), :] = (y * _sigmoid(y)).astype(BF16)
            return carry

        lax.fori_loop(0, s // CONV_ROWS, conv, 0)

    return pl.pallas_call(
        body, out_shape=[jax.ShapeDtypeStruct((s, D_CONV), F32), jax.ShapeDtypeStruct((s, D_CONV), BF16)],
        scratch_shapes=[pltpu.VMEM((s + CONV_HALO, D_CONV), F32)], name=name, compiler_params=_cparams(),
    )(zc, conv_w, conv_b, ln_g, ln_b)


def _conv_bwd(dfeat, cv, zc, conv_w, ln_g, ln_b, name):
    s = zc.shape[0]
    rt = min(256, s)

    def body(df_ref, cv_ref, z_ref, w_ref, g_ref, b_ref, dz_ref, dw_ref, dcb_ref, dg_ref, db_ref, hpad, dcvpad, dwacc):
        hpad[0:CONV_HALO, :] = jnp.zeros((CONV_HALO, D_CONV), F32)
        dcvpad[s:, :] = jnp.zeros((CONV_HALO, D_CONV), F32)
        dwacc[...] = jnp.zeros_like(dwacc)
        dcb_ref[...] = jnp.zeros_like(dcb_ref)
        dg_ref[...] = jnp.zeros_like(dg_ref)
        db_ref[...] = jnp.zeros_like(db_ref)

        def pass1(i, carry):
            r0 = pl.multiple_of(i * rt, rt)
            a, sb = _glu_rows(z_ref, r0, rt)
            hpad[pl.ds(r0 + CONV_HALO, rt), :] = a * sb
            cvhat, rstd = _ln_hat(cv_ref[pl.ds(r0, rt), :])
            y = cvhat * g_ref[...] + b_ref[...]
            sg = _sigmoid(y)
            dy = df_ref[pl.ds(r0, rt), :] * (sg * (1.0 + y * (1.0 - sg)))
            dg_ref[...] += jnp.sum(dy * cvhat, axis=0, keepdims=True)
            db_ref[...] += jnp.sum(dy, axis=0, keepdims=True)
            dcv = _ln_hat_bwd(dy * g_ref[...], cvhat, rstd)
            dcb_ref[...] += jnp.sum(dcv, axis=0, keepdims=True)
            dcvpad[pl.ds(r0, rt), :] = dcv
            return carry

        lax.fori_loop(0, s // rt, pass1, 0)
        w = w_ref[...]

        def pass2(i, carry):
            r0 = pl.multiple_of(i * CONV_ROWS, CONV_ROWS)
            dwin = dcvpad[pl.ds(r0, CONV_ROWS + CONV_HALO), :]
            hwin = hpad[pl.ds(r0, CONV_ROWS + CONV_HALO), :]
            dcv = dwin[0:CONV_ROWS, :]
            dh = jnp.zeros((CONV_ROWS, D_CONV), F32)
            for k in range(CONV_WIDTH):
                dh = dh + dwin[30 - k:30 - k + CONV_ROWS, :] * w[k:k + 1, :]
                prod = dcv * hwin[2 + k:2 + k + CONV_ROWS, :]
                dwacc[8 * k:8 * k + 8, :] += jnp.sum(prod.reshape(CONV_ROWS // 8, 8, D_CONV), axis=0)
            a, sb = _glu_rows(z_ref, r0, CONV_ROWS)
            dz_ref[pl.ds(r0, CONV_ROWS), :] = jnp.concatenate([dh * sb, dh * a * sb * (1.0 - sb)], axis=1).astype(BF16)
            return carry

        lax.fori_loop(0, s // CONV_ROWS, pass2, 0)
        dw_ref[...] = jnp.sum(dwacc[...].reshape(32, 8, D_CONV), axis=1)

    vs = jax.ShapeDtypeStruct((1, D_CONV), F32)
    return pl.pallas_call(
        body,
        out_shape=[jax.ShapeDtypeStruct((s, 2 * D_CONV), BF16), jax.ShapeDtypeStruct((32, D_CONV), F32), vs, vs, vs],
        scratch_shapes=[pltpu.VMEM((s + CONV_HALO, D_CONV), F32), pltpu.VMEM((s + CONV_HALO, D_CONV), F32),
                        pltpu.VMEM((256, D_CONV), F32)],
        name=name, compiler_params=_cparams(),
    )(dfeat, cv, zc, conv_w, ln_g, ln_b)


def _merge(zg, b_gate, ys, name):
    s = zg.shape[0]
    tm = _row_tile(s)

    def body(zg_ref, bg_ref, y0_ref, y1_ref, y2_ref, o_ref):
        acc = None
        for j, y_ref in enumerate((y0_ref, y1_ref, y2_ref)):
            cs = slice(D_MODEL * j, D_MODEL * (j + 1))
            t = _sigmoid(zg_ref[:, cs] + bg_ref[:, cs]) * y_ref[...]
            acc = t if acc is None else acc + t
        o_ref[...] = acc.astype(BF16)

    row = pl.BlockSpec((tm, D_MODEL), lambda i: (i, 0))
    return pl.pallas_call(
        body, grid=(s // tm,),
        in_specs=[pl.BlockSpec((tm, 3 * D_MODEL), lambda i: (i, 0)), _full((1, 3 * D_MODEL)), row, row, row],
        out_specs=row, out_shape=jax.ShapeDtypeStruct((s, D_MODEL), BF16), name=name, compiler_params=_cparams(),
    )(zg, b_gate, *ys)


def _merge_bwd(dm, zg, b_gate, ys, name):
    s = zg.shape[0]
    tm = min(256, s)

    def body(dm_ref, zg_ref, bg_ref, y0_ref, y1_ref, y2_ref, d0_ref, d1_ref, d2_ref, dzg_ref, dbg_ref):
        first = pl.program_id(0) == 0

        @pl.when(first)
        def _():
            dbg_ref[...] = jnp.zeros_like(dbg_ref)

        dmv = dm_ref[...]
        for j, (y_ref, d_ref) in enumerate(((y0_ref, d0_ref), (y1_ref, d1_ref), (y2_ref, d2_ref))):
            cs = slice(D_MODEL * j, D_MODEL * (j + 1))
            g = _sigmoid(zg_ref[:, cs] + bg_ref[:, cs])
            d_ref[...] = (dmv * g).astype(BF16)
            dzg = dmv * y_ref[...] * g * (1.0 - g)
            dzg_ref[:, cs] = dzg.astype(BF16)
            dbg_ref[:, cs] += jnp.sum(dzg, axis=0, keepdims=True)

    row = pl.BlockSpec((tm, D_MODEL), lambda i: (i, 0))
    wide = pl.BlockSpec((tm, 3 * D_MODEL), lambda i: (i, 0))
    yb = jax.ShapeDtypeStruct((s, D_MODEL), BF16)
    return pl.pallas_call(
        body, grid=(s // tm,),
        in_specs=[row, wide, _full((1, 3 * D_MODEL)), row, row, row],
        out_specs=[row, row, row, wide, _full((1, 3 * D_MODEL))],
        out_shape=[yb, yb, yb, jax.ShapeDtypeStruct((s, 3 * D_MODEL), BF16), jax.ShapeDtypeStruct((1, 3 * D_MODEL), F32)],
        name=name, compiler_params=_cparams(),
    )(dm, zg, b_gate, *ys)


def _ff_hidden_bwd(dff, w_ff2, hpre, name):
    s = dff.shape[0]
    tm, tn = min(512, s), 1024

    def body(a_ref, b_ref, h_ref, o_ref, sum_ref):
        dh = lax.dot_general(a_ref[...], b_ref[...], _DIMS["nt"], preferred_element_type=F32)
        dpre = dh * (2.0 * jnp.maximum(h_ref[...], 0.0))
        o_ref[...] = dpre.astype(BF16)
        _acc_rows(sum_ref, dpre, pl.program_id(1) == 0)

    return pl.pallas_call(
        body, grid=(D_FF // tn, s // tm),
        in_specs=[pl.BlockSpec((tm, D_MODEL), lambda j, i: (i, 0)), pl.BlockSpec((tn, D_MODEL), lambda j, i: (j, 0)),
                  pl.BlockSpec((tm, tn), lambda j, i: (i, j))],
        out_specs=[pl.BlockSpec((tm, tn), lambda j, i: (i, j)), pl.BlockSpec((1, tn), lambda j, i: (0, j))],
        out_shape=[jax.ShapeDtypeStruct((s, D_FF), BF16), jax.ShapeDtypeStruct((1, D_FF), F32)],
        name=name, compiler_params=_cparams(),
    )(dff, w_ff2, hpre)


def _silu(t):
    return t * _sigmoid(t)


def _mod_fwd(c_all, w_ada_sh, b_ada_sh, name):
    cols = w_ada_sh.shape[2]

    def body(c_ref, w_ref, b_ref, o_ref):
        ca = _silu(c_ref[...]).astype(BF16)
        o_ref[0] = jnp.dot(ca, w_ref[0].astype(BF16), preferred_element_type=F32) + b_ref[0]

    return pl.pallas_call(
        body, grid=(DEPTH,),
        in_specs=[_full((N_DEV, D_MODEL)), pl.BlockSpec((1, D_MODEL, cols), lambda l: (l, 0, 0)),
                  pl.BlockSpec((1, 1, cols), lambda l: (l, 0, 0))],
        out_specs=pl.BlockSpec((1, N_DEV, cols), lambda l: (l, 0, 0)),
        out_shape=jax.ShapeDtypeStruct((DEPTH, N_DEV, cols), F32), name=name, compiler_params=_cparams(),
    )(c_all, w_ada_sh, b_ada_sh)


def _mod_bwd(c_all, dmod_sh, name):
    cols = dmod_sh.shape[2]

    def body(c_ref, d_ref, o_ref):
        ca = _silu(c_ref[...])
        o_ref[0] = lax.dot_general(ca, d_ref[0], _DIMS["tn"], precision=lax.Precision.HIGHEST,
                                   preferred_element_type=F32)

    return pl.pallas_call(
        body, grid=(DEPTH,),
        in_specs=[_full((N_DEV, D_MODEL)), pl.BlockSpec((1, N_DEV, cols), lambda l: (l, 0, 0))],
        out_specs=pl.BlockSpec((1, D_MODEL, cols), lambda l: (l, 0, 0)),
        out_shape=jax.ShapeDtypeStruct((DEPTH, D_MODEL, cols), F32), name=name, compiler_params=_cparams(),
    )(c_all, dmod_sh)


def _flat_tiles(rows, cols, itemsize_total):
    budget = 12 * 1024 * 1024
    tr = rows
    while tr % 16 == 0 and tr * cols * itemsize_total > budget:
        tr //= 2
    return tr


def _cast_bf16(w, layer, name):
    _, r, c = w.shape
    tr = _flat_tiles(r, c, 6)

    def body(w_ref, o_ref):
        o_ref[...] = w_ref[...].astype(BF16)

    return pl.pallas_call(body, grid=(r // tr,), in_specs=[pl.BlockSpec((None, tr, c), lambda i: (layer, i, 0))],
                          out_specs=pl.BlockSpec((tr, c), lambda i: (i, 0)),
                          out_shape=jax.ShapeDtypeStruct((r, c), BF16), name=name, compiler_params=_cparams())(w)


def _sum4(parts, name):
    _, r, c = parts.shape
    tr = _flat_tiles(r, c, 20)

    def body(p_ref, o_ref):
        o_ref[...] = ((p_ref[0] + p_ref[1]) + p_ref[2]) + p_ref[3]

    return pl.pallas_call(body, grid=(r // tr,), in_specs=[pl.BlockSpec((4, tr, c), lambda i: (0, i, 0))],
                          out_specs=pl.BlockSpec((tr, c), lambda i: (i, 0)),
                          out_shape=jax.ShapeDtypeStruct((r, c), F32), name=name, compiler_params=_cparams())(parts)


def _adam_math(w, g, m, v):
    m2 = ADAM_B1 * m + (1.0 - ADAM_B1) * g
    v2 = ADAM_B2 * v + (1.0 - ADAM_B2) * (g * g)
    m_hat = m2 / (1.0 - ADAM_B1 ** ADAM_STEP)
    v_hat = v2 / (1.0 - ADAM_B2 ** ADAM_STEP)
    delta = -ADAM_LR * (m_hat / (jnp.sqrt(v_hat) + ADAM_EPS) + ADAM_WD * w)
    return delta, m2, v2


def _adamw(w, m, v, grads, name):
    r, c = w.shape
    tr = _flat_tiles(r, c, 4 * (7 + len(grads)))

    def body(*refs):
        w_ref, m_ref, v_ref = refs[:3]
        g_refs = refs[3:3 + len(grads)]
        g_ref, d_ref, m2_ref, v2_ref = refs[3 + len(grads):]
        g = g_refs[0][...]
        for gr in g_refs[1:]:
            g = g + gr[...]
        delta, m2, v2 = _adam_math(w_ref[...], g, m_ref[...], v_ref[...])
        g_ref[...] = g
        d_ref[...] = delta
        m2_ref[...] = m2
        v2_ref[...] = v2

    blk = pl.BlockSpec((tr, c), lambda i: (i, 0))
    sh = jax.ShapeDtypeStruct((r, c), F32)
    return pl.pallas_call(body, grid=(r // tr,), in_specs=[blk] * (3 + len(grads)), out_specs=[blk] * 4,
                          out_shape=[sh] * 4, name=name, compiler_params=_cparams())(w, m, v, *grads)


def _adamw_small(w, m, v, g_all, name):
    r, c = w.shape

    def body(w_ref, m_ref, v_ref, g_ref, go_ref, d_ref, m2_ref, v2_ref):
        g = g_ref[0]
        for b in range(1, N_DEV):
            g = g + g_ref[b]
        delta, m2, v2 = _adam_math(w_ref[...], g, m_ref[...], v_ref[...])
        go_ref[...] = g
        d_ref[...] = delta
        m2_ref[...] = m2
        v2_ref[...] = v2

    sh = jax.ShapeDtypeStruct((r, c), F32)
    return pl.pallas_call(body, out_shape=[sh] * 4, name=name, compiler_params=_cparams())(w, m, v, g_all)


def _me():
    return lax.axis_index("x"), lax.axis_index("y"), lax.axis_index("c")


def _flip(v, bit):
    return 1 - v if bit else v


def _allgather_small(blk, name):
    r, c = blk.shape

    def body(x_ref, o_ref, send_sems, recv_sems):
        x, y, cc = _me()
        me = 4 * x + 2 * y + cc
        copies = []
        for k in range(1, N_DEV):
            peer = (_flip(x, k & 4), _flip(y, k & 2), _flip(cc, k & 1))
            cp = pltpu.make_async_remote_copy(src_ref=x_ref, dst_ref=o_ref.at[me], send_sem=send_sems.at[k - 1],
                                              recv_sem=recv_sems.at[k - 1], device_id=peer, device_id_type=MESH)
            cp.start()
            copies.append(cp)
        o_ref[me] = x_ref[...]
        for cp in copies:
            cp.wait()

    return pl.pallas_call(
        body, out_shape=jax.ShapeDtypeStruct((N_DEV, r, c), F32),
        in_specs=[pl.BlockSpec(memory_space=pltpu.VMEM)], out_specs=pl.BlockSpec(memory_space=pltpu.VMEM),
        scratch_shapes=[pltpu.SemaphoreType.DMA((N_DEV - 1,)), pltpu.SemaphoreType.DMA((N_DEV - 1,))],
        name=name, compiler_params=_cparams(),
    )(blk)


def _chip_exchange(arrs, name):
    n = len(arrs)

    def body(*refs):
        ins, outs = refs[:n], refs[n:2 * n]
        send_sems, recv_sems, loc_sems = refs[2 * n:]
        x, y, cc = _me()
        chip = 2 * x + y
        pending = []
        for j in range(n):
            lc = pltpu.make_async_copy(ins[j], outs[j].at[chip], loc_sems.at[j])
            lc.start()
            pending.append(lc)
            for k in range(1, N_CHIP):
                peer = (_flip(x, k & 2), _flip(y, k & 1), cc)
                cp = pltpu.make_async_remote_copy(src_ref=ins[j], dst_ref=outs[j].at[chip],
                                                  send_sem=send_sems.at[3 * j + k - 1], recv_sem=recv_sems.at[3 * j + k - 1],
                                                  device_id=peer, device_id_type=MESH)
                cp.start()
                pending.append(cp)
        for cp in pending:
            cp.wait()

    anyspec = pl.BlockSpec(memory_space=pl.ANY)
    return pl.pallas_call(
        body, out_shape=[jax.ShapeDtypeStruct((N_CHIP,) + a.shape, a.dtype) for a in arrs],
        in_specs=[anyspec] * n, out_specs=[anyspec] * n,
        scratch_shapes=[pltpu.SemaphoreType.DMA((3 * n,)), pltpu.SemaphoreType.DMA((3 * n,)), pltpu.SemaphoreType.DMA((n,))],
        name=name, compiler_params=_cparams(),
    )(*arrs)


def _chip_scatter(groups, name):
    flat = [a for g in groups for a in g]
    n, ng = len(flat), len(groups)

    def body(*refs):
        ins, outs = refs[:n], refs[n:n + ng]
        send_sems, recv_sems, loc_sems = refs[n + ng:]
        x, y, cc = _me()
        chip = 2 * x + y
        pending = []
        q = 0
        for j, g in enumerate(groups):
            for l, a in enumerate(g):
                rows = a.shape[1]
                dst = outs[j].at[chip, pl.ds(l * rows, rows), :]
                lc = pltpu.make_async_copy(ins[q].at[chip], dst, loc_sems.at[q])
                lc.start()
                pending.append(lc)
                for k in range(1, N_CHIP):
                    px, py = _flip(x, k & 2), _flip(y, k & 1)
                    cp = pltpu.make_async_remote_copy(src_ref=ins[q].at[2 * px + py], dst_ref=dst,
                                                      send_sem=send_sems.at[3 * q + k - 1],
                                                      recv_sem=recv_sems.at[3 * q + k - 1],
                                                      device_id=(px, py, cc), device_id_type=MESH)
                    cp.start()
                    pending.append(cp)
                q += 1
        for cp in pending:
            cp.wait()

    anyspec = pl.BlockSpec(memory_space=pl.ANY)
    out_shape = [jax.ShapeDtypeStruct((N_CHIP, len(g) * g[0].shape[1], g[0].shape[2]), g[0].dtype) for g in groups]
    return pl.pallas_call(
        body, out_shape=out_shape, in_specs=[anyspec] * n, out_specs=[anyspec] * ng,
        scratch_shapes=[pltpu.SemaphoreType.DMA((3 * n,)), pltpu.SemaphoreType.DMA((3 * n,)), pltpu.SemaphoreType.DMA((n,))],
        name=name, compiler_params=_cparams(),
    )(*flat)


def _sibling_swap(arrs, name):
    n = len(arrs)

    def body(*refs):
        ins, outs = refs[:n], refs[n:2 * n]
        send_sems, recv_sems = refs[2 * n:]
        x, y, cc = _me()
        pending = []
        for j in range(n):
            cp = pltpu.make_async_remote_copy(src_ref=ins[j], dst_ref=outs[j], send_sem=send_sems.at[j],
                                              recv_sem=recv_sems.at[j], device_id=(x, y, 1 - cc), device_id_type=MESH)
            cp.start()
            pending.append(cp)
        for cp in pending:
            cp.wait()

    anyspec = pl.BlockSpec(memory_space=pl.ANY)
    return pl.pallas_call(
        body, out_shape=[jax.ShapeDtypeStruct(a.shape, a.dtype) for a in arrs],
        in_specs=[anyspec] * n, out_specs=[anyspec] * n,
        scratch_shapes=[pltpu.SemaphoreType.DMA((n,)), pltpu.SemaphoreType.DMA((n,))],
        name=name, compiler_params=_cparams(),
    )(*arrs)


COL_SHARDED = ("w_in", "w_br_pool", "w_br_attn", "w_br_conv", "w_ff1")
ROW_SHARDED = ("w_o", "w_ff2")
BIG = COL_SHARDED + ROW_SHARDED
SMALL = ("b_ada", "b_gate", "w_pool", "pool_scale", "rel_bias", "conv_w", "conv_b", "conv_ln_g", "conv_ln_b",
         "ln_mix_g", "ln_mix_b", "b_ff1", "b_ff2", "ln_ff_g", "ln_ff_b")
PACK_W = 1024


def _pack(parts):
    rows = []
    for a in parts:
        flat = a.reshape(-1)
        n = -(-flat.shape[0] // PACK_W) * PACK_W
        rows.append(jnp.pad(flat, (0, n - flat.shape[0])).reshape(-1, PACK_W))
    out = jnp.concatenate(rows, axis=0)
    r = -(-out.shape[0] // 8) * 8
    return jnp.pad(out, ((0, r - out.shape[0]), (0, 0)))


def _unpack(packed, shapes):
    out, r0 = [], 0
    for shp in shapes:
        size = int(np.prod(shp))
        nr = -(-size // PACK_W)
        out.append(packed[r0:r0 + nr].reshape(-1)[:size].reshape(shp))
        r0 += nr
    return out


def _to_full_cols(g):
    _, r, cs = g.shape
    return jnp.transpose(g, (1, 0, 2)).reshape(r, N_CHIP * cs)


def _to_col_shards(w):
    r, c = w.shape
    return jnp.transpose(w.reshape(r, N_CHIP, c // N_CHIP), (1, 0, 2))


def _layer_fwd(l, x, mod, W, P):
    s = x.shape[0]
    sh_m, sc_m, g_m, sh_f, sc_f, g_f = [mod[l:l + 1, D_MODEL * j:D_MODEL * (j + 1)] for j in range(6)]
    n = lambda t: f"{t}{l}"
    w_in = W["w_in"][l]
    u = _ln_mod(x, sc_m, sh_m, n("ln_mod_mix"))
    tmz = min(1024, s)
    zp = _mm(u, w_in, "nn", tm=min(2048, s), tn=256, out_dtype=F32, name=n("z_pool"), b_col0=0, n_out=D_POOL)
    qkv = _mm(u, w_in, "nn", tm=tmz, tn=256, out_dtype=BF16, name=n("z_qkv"), b_col0=OFF_QKV // 256, n_out=3 * D_ATTN)
    zc = _mm(u, w_in, "nn", tm=tmz, tn=256, out_dtype=F32, name=n("z_conv"), b_col0=OFF_CONV // 256, n_out=2 * D_CONV)
    zg = _mm(u, w_in, "nn", tm=tmz, tn=768, out_dtype=F32, name=n("z_gate"), b_col0=OFF_GATE // 768, n_out=3 * D_MODEL)

    p, feat_pool = _pool_fwd(zp, P["wp_bd"][l], P["pool_scale"][l], n("pool_fwd"))
    bias = _bias_block(P["rel_bias"][l], n("bias_block"))
    o = _attn_fwd(qkv, bias, n("attn_fwd"))
    cv, feat_conv = _conv_fwd(zc, P["conv_w"][l], P["conv_b"][l], P["conv_ln_g"][l], P["conv_ln_b"][l], n("conv_fwd"))

    tmb = min(1024, s)
    y_pool = _mm(feat_pool, W["w_br_pool"][l], "nn", tm=tmb, tn=1024, out_dtype=F32, name=n("y_pool"))
    y_attn = _mm(o, W["w_br_attn"][l], "nn", tm=tmb, tn=1024, out_dtype=F32, name=n("y_attn"))
    y_conv = _mm(feat_conv, W["w_br_conv"][l], "nn", tm=tmb, tn=1024, out_dtype=F32, name=n("y_conv"))
    ys = (y_pool, y_attn, y_conv)
    merged = _merge(zg, P["b_gate"][l], ys, n("merge"))
    mix = _mm(merged, W["w_o"][l], "nn", tm=tmb, tn=1024, out_dtype=F32, name=n("mix_out"))
    x1 = _resid_ln(x, mix, g_m, P["ln_mix_g"][l], P["ln_mix_b"][l], n("resid_ln_mix"))

    u2 = _ln_mod(x1, sc_f, sh_f, n("ln_mod_ff"))
    hpre = _mm(u2, W["w_ff1"][l], "nn", tm=tmb, tn=1024, out_dtype=F32, name=n("ff1"), bias=P["b_ff1"][l])
    ff = _mm(hpre, W["w_ff2"][l], "nn", tm=min(256, s), tn=1024, out_dtype=F32, name=n("ff2"), a_fn=_relu2,
             bias=P["b_ff2"][l])
    x2 = _resid_ln(x1, ff, g_f, P["ln_ff_g"][l], P["ln_ff_b"][l], n("resid_ln_ff"))
    saved = dict(x=x, u=u, zp=zp, qkv=qkv, zc=zc, zg=zg, p=p, feat_pool=feat_pool, bias=bias, o=o, cv=cv,
                 feat_conv=feat_conv, ys=ys, merged=merged, mix=mix, x1=x1, u2=u2, hpre=hpre, ff=ff)
    return x2, saved


def _layer_bwd(l, dx2, mod, W, P, A):
    s = dx2.shape[0]
    sh_m, sc_m, g_m, sh_f, sc_f, g_f = [mod[l:l + 1, D_MODEL * j:D_MODEL * (j + 1)] for j in range(6)]
    n = lambda t: f"{t}{l}"
    tmb = min(1024, s)
    gw, gs = {}, {}

    dres, dff, gs["ln_ff_g"], gs["ln_ff_b"], dg_f, gs["b_ff2"] = _resid_ln_bwd(
        dx2, A["x1"], A["ff"], g_f, P["ln_ff_g"][l], n("resid_ln_ff_bwd"))
    gw["w_ff2"] = _mm(A["hpre"], dff, "tn", tm=256, tn=1024, out_dtype=F32, name=n("dw_ff2"), a_fn=_relu2)
    dhpre, gs["b_ff1"] = _ff_hidden_bwd(dff, W["w_ff2"][l], A["hpre"], n("ff_hidden_bwd"))
    gw["w_ff1"] = _mm(A["u2"], dhpre, "tn", tm=512, tn=1024, out_dtype=F32, name=n("dw_ff1"))
    du2 = _mm(dhpre, W["w_ff1"][l], "nt", tm=min(512, s), tn=512, out_dtype=F32, name=n("du_ff"))
    dx1, dsc_f, dsh_f = _ln_mod_bwd(du2, A["x1"], sc_f, dres, n("ln_mod_ff_bwd"))

    dres, dmix, gs["ln_mix_g"], gs["ln_mix_b"], dg_m, _ = _resid_ln_bwd(
        dx1, A["x"], A["mix"], g_m, P["ln_mix_g"][l], n("resid_ln_mix_bwd"))
    gw["w_o"] = _mm(A["merged"], dmix, "tn", tm=512, tn=1024, out_dtype=F32, name=n("dw_o"))
    dmerged = _mm(dmix, W["w_o"][l], "nt", tm=tmb, tn=1024, out_dtype=F32, name=n("d_merged"))
    dy_pool, dy_attn, dy_conv, dzg, gs["b_gate"] = _merge_bwd(dmerged, A["zg"], P["b_gate"][l], A["ys"], n("merge_bwd"))

    gw["w_br_pool"] = _mm(A["feat_pool"], dy_pool, "tn", tm=256, tn=1024, out_dtype=F32, name=n("dw_br_pool"))
    gw["w_br_attn"] = _mm(A["o"], dy_attn, "tn", tm=512, tn=1024, out_dtype=F32, name=n("dw_br_attn"))
    gw["w_br_conv"] = _mm(A["feat_conv"], dy_conv, "tn", tm=256, tn=1024, out_dtype=F32, name=n("dw_br_conv"))
    dfeat_pool = _mm(dy_pool, W["w_br_pool"][l], "nt", tm=tmb, tn=256, out_dtype=F32, name=n("d_feat_pool"))
    do = _mm(dy_attn, W["w_br_attn"][l], "nt", tm=tmb, tn=512, out_dtype=BF16, name=n("d_attn_out"))
    dfeat_conv = _mm(dy_conv, W["w_br_conv"][l], "nt", tm=tmb, tn=256, out_dtype=F32, name=n("d_feat_conv"))

    dzp, dwp_bd, gs["pool_scale"] = _pool_bwd(dfeat_pool, A["p"], P["wp_bd"][l], P["pool_scale"][l], n("pool_bwd"))
    gs["w_pool"] = jnp.stack([dwp_bd[POOL_GROUP * g:POOL_GROUP * (g + 1), POOL_GROUP * g:POOL_GROUP * (g + 1)]
                              for g in range(len(POOL_WINDOWS))])
    dq, dk, dv, ds_acc = _attn_bwd(A["qkv"], do, A["bias"], n("attn_bwd"))
    gs["rel_bias"] = _bias_block_bwd(ds_acc, n("bias_block_bwd"))
    dzc, dcw, gs["conv_b"], gs["conv_ln_g"], gs["conv_ln_b"] = _conv_bwd(
        dfeat_conv, A["cv"], A["zc"], P["conv_w"][l], P["conv_ln_g"][l], P["conv_ln_b"][l], n("conv_bwd"))
    gs["conv_w"] = dcw[:CONV_WIDTH]

    dz = jnp.concatenate([dzp, dq, dk[KPAD:].astype(BF16), dv[KPAD:].astype(BF16), dzc, dzg], axis=1)
    gw["w_in"] = _mm(A["u"], dz, "tn", tm=512, tn=768, out_dtype=F32, name=n("dw_in"))
    du = _mm(dz, W["w_in"][l], "nt", tm=min(512, s), tn=512, out_dtype=F32, name=n("du_mix"))
    dx, dsc_m, dsh_m = _ln_mod_bwd(du, A["x"], sc_m, dres, n("ln_mod_mix_bwd"))
    dmod = jnp.concatenate([dsh_m, dsc_m, dg_m, dsh_f, dsc_f, dg_f], axis=1)
    return dx, gw, gs, dmod


def _small_shapes():
    return {"b_ada": (6 * D_MODEL,), "b_gate": (3 * D_MODEL,), "w_pool": (4, POOL_GROUP, POOL_GROUP),
            "pool_scale": (D_POOL,), "rel_bias": (N_HEADS, N_REL), "conv_w": (CONV_WIDTH, D_CONV),
            "conv_b": (D_CONV,), "conv_ln_g": (D_CONV,), "conv_ln_b": (D_CONV,), "ln_mix_g": (D_MODEL,),
            "ln_mix_b": (D_MODEL,), "b_ff1": (D_FF,), "b_ff2": (D_MODEL,), "ln_ff_g": (D_MODEL,), "ln_ff_b": (D_MODEL,)}


def kernel(x, c, w_ada, b_ada, w_in, b_gate, w_pool, pool_scale, rel_bias, conv_w, conv_b, conv_ln_g, conv_ln_b, w_br_pool, w_br_attn, w_br_conv, w_o, ln_mix_g, ln_mix_b, w_ff1, b_ff1, w_ff2, b_ff2, ln_ff_g, ln_ff_b, loss_target, m_w_ada, m_b_ada, m_w_in, m_b_gate, m_w_pool, m_pool_scale, m_rel_bias, m_conv_w, m_conv_b, m_conv_ln_g, m_conv_ln_b, m_w_br_pool, m_w_br_attn, m_w_br_conv, m_w_o, m_ln_mix_g, m_ln_mix_b, m_w_ff1, m_b_ff1, m_w_ff2, m_b_ff2, m_ln_ff_g, m_ln_ff_b, v_w_ada, v_b_ada, v_w_in, v_b_gate, v_w_pool, v_pool_scale, v_rel_bias, v_conv_w, v_conv_b, v_conv_ln_g, v_conv_ln_b, v_w_br_pool, v_w_br_attn, v_w_br_conv, v_w_o, v_ln_mix_g, v_ln_mix_b, v_w_ff1, v_b_ff1, v_w_ff2, v_b_ff2, v_ln_ff_g, v_ln_ff_b):
    env = dict(locals())
    xi, yi, ci = _me()
    chip = 2 * xi + yi
    me = 4 * xi + 2 * yi + ci
    xs = x[0]
    tgt = loss_target[0]
    L = DEPTH

    c_all = _allgather_small(c.reshape(8, 128), "gather_c").reshape(N_DEV, D_MODEL)
    ada_cols = w_ada.shape[2]
    b_ada_sh = lax.dynamic_slice_in_dim(b_ada, chip * ada_cols, ada_cols, axis=1).reshape(L, 1, ada_cols)
    mod_part = _mod_fwd(c_all, w_ada, b_ada_sh, "mod_fwd")
    mod_g = _allgather_small(mod_part.reshape(-1, 128), "gather_mod").reshape(N_CHIP, 2, L, N_DEV, ada_cols)[:, 0]
    mod_all = jnp.transpose(mod_g, (1, 2, 0, 3)).reshape(L, N_DEV, 6 * D_MODEL)
    mod = lax.dynamic_index_in_dim(mod_all, me, axis=1, keepdims=False)

    shards = [_cast_bf16(env[k], l, f"cast_{k}{l}") for k in BIG for l in range(L)]
    gathered = _chip_exchange(shards, "gather_weights")
    W = {k: [None] * L for k in BIG}
    for (k, l), g in zip([(k, l) for k in BIG for l in range(L)], gathered):
        W[k][l] = _to_full_cols(g) if k in COL_SHARDED else g.reshape(-1, g.shape[-1])

    P = {k: env[k] for k in ("rel_bias", "conv_w")}
    for k in ("b_gate", "pool_scale", "conv_b", "conv_ln_g", "conv_ln_b", "ln_mix_g", "ln_mix_b", "b_ff1", "b_ff2",
              "ln_ff_g", "ln_ff_b"):
        P[k] = env[k].reshape(L, 1, -1)
    conv_w_full = _allgather_small(_pack([conv_w]), "gather_conv_w")
    n_cw = conv_w.size
    cw = conv_w_full.reshape(N_CHIP, 2, -1)[:, 0, :n_cw].reshape(N_CHIP, L, CONV_WIDTH, D_CONV // N_CHIP)
    P["conv_w"] = jnp.transpose(cw, (1, 2, 0, 3)).reshape(L, CONV_WIDTH, D_CONV)
    wp_bd = jnp.zeros((L, D_POOL, D_POOL), F32)
    for g in range(len(POOL_WINDOWS)):
        sl = slice(POOL_GROUP * g, POOL_GROUP * (g + 1))
        wp_bd = wp_bd.at[:, sl, sl].set(w_pool[:, g])
    P["wp_bd"] = wp_bd.astype(BF16)

    acts = []
    h = xs
    for l in range(L):
        h, saved = _layer_fwd(l, h, mod, W, P)
        acts.append(saved)
    dy, loss_part = _loss_grad(h, tgt, "loss_grad")
    loss = lax.psum(loss_part[0, 0], ("x", "y", "c"))

    gws, gss, dmods = [None] * L, [None] * L, [None] * L
    dh = dy
    for l in reversed(range(L)):
        dh, gws[l], gss[l], dmods[l] = _layer_bwd(l, dh, mod, W, P, acts[l])
    grad_x = dh[None]

    to_send = [[(_to_col_shards(gws[l][k]) if k in COL_SHARDED else
                 gws[l][k].reshape(N_CHIP, -1, gws[l][k].shape[-1])) for l in range(L)] for k in BIG]
    received = _chip_scatter(to_send, "scatter_grads")
    partial = [_sum4(r, f"sum_chips_{k}") for k, r in zip(BIG, received)]
    other = _sibling_swap(partial, "swap_partials")

    out = {}
    for k, pa, po in zip(BIG, partial, other):
        shp = env[k].shape
        flat = lambda a: a.reshape(-1, shp[-1])
        g_, d_, m_, v_ = _adamw(flat(env[k]), flat(env["m_" + k]), flat(env["v_" + k]), [pa, po], f"adamw_{k}")
        out[k] = tuple(a.reshape(shp) for a in (g_, d_, m_, v_))

    shapes = _small_shapes()
    small_names = [k for k in SMALL if k != "b_ada"]
    dmod_own = jnp.concatenate(dmods, axis=0)
    pack = _pack([dmod_own] + [jnp.stack([gss[l][k].reshape(shapes[k]) for l in range(L)]) for k in small_names])
    g_all = _allgather_small(pack.reshape(-1, 128), "gather_small").reshape(N_DEV, -1, PACK_W)

    dmod_all = g_all[:, :L * 6].reshape(N_DEV, L, 6 * D_MODEL)
    dmod_sh = jnp.transpose(lax.dynamic_slice_in_dim(dmod_all, chip * ada_cols, ada_cols, axis=2), (1, 0, 2))
    g_ada = _mod_bwd(c_all, dmod_sh, "mod_bwd")
    g_, d_, m_, v_ = _adamw(w_ada.reshape(-1, ada_cols), m_w_ada.reshape(-1, ada_cols), v_w_ada.reshape(-1, ada_cols),
                            [g_ada.reshape(-1, ada_cols)], "adamw_w_ada")
    out["w_ada"] = tuple(a.reshape(w_ada.shape) for a in (g_, d_, m_, v_))

    def small_pack(prefix):
        parts = [env[prefix + "b_ada"]]
        for k in small_names:
            a = env[prefix + k]
            if k == "conv_w":
                a = jnp.zeros((L,) + shapes[k], F32)
            parts.append(a)
        return _pack(parts)

    gp, dp, mp, vp = _adamw_small(small_pack(""), small_pack("m_"), small_pack("v_"), g_all, "adamw_small")
    full_shapes = [(L,) + shapes["b_ada"]] + [(L,) + shapes[k] for k in small_names]
    for tag, packed in (("g", gp), ("d", dp), ("m", mp), ("v", vp)):
        for k, a in zip(["b_ada"] + small_names, _unpack(packed, full_shapes)):
            out.setdefault(k, {})
            out[k][tag] = a
    g_cw_full = out["conv_w"]["g"]
    cw_cols = D_CONV // N_CHIP
    g_cw = lax.dynamic_slice_in_dim(g_cw_full, chip * cw_cols, cw_cols, axis=2)
    pad_rows = lambda a: jnp.pad(a.reshape(L * CONV_WIDTH, cw_cols), ((0, 2), (0, 0)))
    g_, d_, m_, v_ = _adamw(pad_rows(conv_w), pad_rows(m_conv_w), pad_rows(v_conv_w), [pad_rows(g_cw)], "adamw_conv_w")
    out["conv_w"] = tuple(a[:L * CONV_WIDTH].reshape(L, CONV_WIDTH, cw_cols) for a in (g_, d_, m_, v_))

    names = ["w_ada", "b_ada", "w_in", "b_gate", "w_pool", "pool_scale", "rel_bias", "conv_w", "conv_b", "conv_ln_g",
             "conv_ln_b", "w_br_pool", "w_br_attn", "w_br_conv", "w_o", "ln_mix_g", "ln_mix_b", "w_ff1", "b_ff1",
             "w_ff2", "b_ff2", "ln_ff_g", "ln_ff_b"]

    def pick(k, i):
        o = out[k]
        return o[i] if isinstance(o, tuple) else o["gdmv"[i]].reshape(env[k].shape)

    return (loss, grad_x, *[pick(k, 0) for k in names], *[pick(k, 1) for k in names],
            *[pick(k, 2) for k in names], *[pick(k, 3) for k in names])
```

```python
import functools

import jax
import jax.numpy as jnp
import numpy as np
from jax import lax
from jax.experimental import pallas as pl
from jax.experimental.pallas import tpu as pltpu

F32 = jnp.float32
BF16 = jnp.bfloat16

D_MODEL = 1024
DEPTH = 2
CHUNK = 64
POOL_WINDOWS = (2, 4, 8, 16)
POOL_GROUP = 64
D_POOL = 256
N_HEADS = 8
HEAD_DIM = 64
D_ATTN = 512
N_PREV_CHUNKS = 8
REL_CLIP = 128
N_REL = 2 * REL_CLIP + 1
D_CONV = 256
CONV_WIDTH = 31
D_FF = 4 * D_MODEL
D_IN = 5376
OFF_POOL, OFF_QKV, OFF_CONV, OFF_GATE = 0, 256, 1792, 2304
ALPHA = (2.0 * DEPTH) ** 0.25
LN_EPS = 1e-5
NEG_INF = -1e30
ADAM_LR, ADAM_B1, ADAM_B2, ADAM_EPS, ADAM_WD, ADAM_STEP = 0.001, 0.9, 0.999, 1e-08, 0.01, 10

N_DEV = 8
N_CHIP = 4
MESH = pl.DeviceIdType.MESH

QB = 2 * CHUNK
KPAD = N_PREV_CHUNKS * CHUNK
KW = QB + KPAD
SKEW_W = 768

VMEM_LIMIT = 56 * 1024 * 1024


def _cparams(**kw):
    return pltpu.CompilerParams(vmem_limit_bytes=VMEM_LIMIT, **kw)


def _full(shape):
    n = len(shape)
    return pl.BlockSpec(shape, lambda *_: (0,) * n)


_DIMS = {"nn": (((1,), (0,)), ((), ())), "nt": (((1,), (1,)), ((), ())), "tn": (((0,), (0,)), ((), ()))}


def _relu2(t):
    r = jnp.maximum(t, 0.0)
    return r * r


def _mm(a, b, mode, *, tm, tn, out_dtype, name, b_col0=0, n_out=None, a_fn=None, bias=None):
    if mode == "tn":
        k, m = a.shape
        n = b.shape[1] if n_out is None else n_out
        a_spec = pl.BlockSpec((k, tm), lambda i, j: (0, i))
        b_spec = pl.BlockSpec((k, tn), lambda i, j: (0, j + b_col0))
    elif mode == "nn":
        m, k = a.shape
        n = b.shape[1] if n_out is None else n_out
        a_spec = pl.BlockSpec((tm, k), lambda i, j: (i, 0))
        b_spec = pl.BlockSpec((k, tn), lambda i, j: (0, j + b_col0))
    else:
        m, k = a.shape
        n = b.shape[0] if n_out is None else n_out
        a_spec = pl.BlockSpec((tm, k), lambda i, j: (i, 0))
        b_spec = pl.BlockSpec((tn, k), lambda i, j: (j + b_col0, 0))
    assert m % tm == 0 and n % tn == 0, (name, m, n, tm, tn)
    dims = _DIMS[mode]

    def body(*refs):
        if bias is None:
            a_ref, b_ref, o_ref = refs
        else:
            a_ref, b_ref, bias_ref, o_ref = refs
        av = a_ref[...]
        if a_fn is not None:
            av = a_fn(av)
        acc = lax.dot_general(av.astype(BF16), b_ref[...].astype(BF16), dims, preferred_element_type=F32)
        if bias is not None:
            acc = acc + bias_ref[...]
        o_ref[...] = acc.astype(out_dtype)

    in_specs = [a_spec, b_spec]
    args = [a, b]
    if bias is not None:
        in_specs.append(pl.BlockSpec((1, tn), lambda i, j: (0, j)))
        args.append(bias)
    return pl.pallas_call(
        body, grid=(m // tm, n // tn), in_specs=in_specs,
        out_specs=pl.BlockSpec((tm, tn), lambda i, j: (i, j)),
        out_shape=jax.ShapeDtypeStruct((m, n), out_dtype), name=name, compiler_params=_cparams(),
    )(*args)


def _ln_hat(x):
    mu = jnp.mean(x, axis=-1, keepdims=True)
    xc = x - mu
    var = jnp.mean(xc * xc, axis=-1, keepdims=True)
    rstd = lax.rsqrt(var + LN_EPS)
    return xc * rstd, rstd


def _ln_hat_bwd(dhat, xhat, rstd):
    m1 = jnp.mean(dhat, axis=-1, keepdims=True)
    m2 = jnp.mean(dhat * xhat, axis=-1, keepdims=True)
    return rstd * (dhat - m1 - xhat * m2)


def _row_tile(s):
    return min(512, s)


def _acc_rows(ref, val, first):
    @pl.when(first)
    def _():
        ref[...] = jnp.zeros_like(ref)
    ref[...] += jnp.sum(val, axis=0, keepdims=True)


def _ln_mod(x, sc, sh, name):
    s, d = x.shape
    tm = _row_tile(s)

    def body(x_ref, sc_ref, sh_ref, u_ref):
        xhat, _ = _ln_hat(x_ref[...])
        u_ref[...] = (xhat * (1.0 + sc_ref[...]) + sh_ref[...]).astype(BF16)

    row = pl.BlockSpec((tm, d), lambda i: (i, 0))
    vec = pl.BlockSpec((1, d), lambda i: (0, 0))
    return pl.pallas_call(body, grid=(s // tm,), in_specs=[row, vec, vec], out_specs=row,
                          out_shape=jax.ShapeDtypeStruct((s, d), BF16), name=name, compiler_params=_cparams())(x, sc, sh)


def _ln_mod_bwd(du, x, sc, dres, name):
    s, d = x.shape
    tm = _row_tile(s)

    def body(du_ref, x_ref, sc_ref, dres_ref, dx_ref, dsc_ref, dsh_ref):
        first = pl.program_id(0) == 0
        duv = du_ref[...]
        xhat, rstd = _ln_hat(x_ref[...])
        dx_ref[...] = dres_ref[...] + _ln_hat_bwd(duv * (1.0 + sc_ref[...]), xhat, rstd)
        _acc_rows(dsc_ref, duv * xhat, first)
        _acc_rows(dsh_ref, duv, first)

    row = pl.BlockSpec((tm, d), lambda i: (i, 0))
    vec = pl.BlockSpec((1, d), lambda i: (0, 0))
    vs = jax.ShapeDtypeStruct((1, d), F32)
    return pl.pallas_call(body, grid=(s // tm,), in_specs=[row, row, vec, row], out_specs=[row, vec, vec],
                          out_shape=[jax.ShapeDtypeStruct((s, d), F32), vs, vs], name=name,
                          compiler_params=_cparams())(du, x, sc, dres)


def _resid_ln(x, f, g, gam, bet, name):
    s, d = x.shape
    tm = _row_tile(s)

    def body(x_ref, f_ref, g_ref, gam_ref, bet_ref, o_ref):
        rhat, _ = _ln_hat(ALPHA * x_ref[...] + g_ref[...] * f_ref[...])
        o_ref[...] = rhat * gam_ref[...] + bet_ref[...]

    row = pl.BlockSpec((tm, d), lambda i: (i, 0))
    vec = pl.BlockSpec((1, d), lambda i: (0, 0))
    return pl.pallas_call(body, grid=(s // tm,), in_specs=[row, row, vec, vec, vec], out_specs=row,
                          out_shape=jax.ShapeDtypeStruct((s, d), F32), name=name, compiler_params=_cparams())(x, f, g, gam, bet)


def _resid_ln_bwd(dxo, x, f, g, gam, name):
    s, d = x.shape
    tm = _row_tile(s)

    def body(dxo_ref, x_ref, f_ref, g_ref, gam_ref, dres_ref, df_ref, dgam_ref, dbet_ref, dg_ref, dbias_ref):
        first = pl.program_id(0) == 0
        dxov = dxo_ref[...]
        fv = f_ref[...]
        rhat, rstd = _ln_hat(ALPHA * x_ref[...] + g_ref[...] * fv)
        dr = _ln_hat_bwd(dxov * gam_ref[...], rhat, rstd)
        dfv = g_ref[...] * dr
        dres_ref[...] = ALPHA * dr
        df_ref[...] = dfv.astype(BF16)
        _acc_rows(dgam_ref, dxov * rhat, first)
        _acc_rows(dbet_ref, dxov, first)
        _acc_rows(dg_ref, dr * fv, first)
        _acc_rows(dbias_ref, dfv, first)

    row = pl.BlockSpec((tm, d), lambda i: (i, 0))
    vec = pl.BlockSpec((1, d), lambda i: (0, 0))
    vs = jax.ShapeDtypeStruct((1, d), F32)
    return pl.pallas_call(body, grid=(s // tm,), in_specs=[row, row, row, vec, vec],
                          out_specs=[row, row, vec, vec, vec, vec],
                          out_shape=[jax.ShapeDtypeStruct((s, d), F32), jax.ShapeDtypeStruct((s, d), BF16), vs, vs, vs, vs],
                          name=name, compiler_params=_cparams())(dxo, x, f, g, gam)


def _loss_grad(y, tgt, name):
    s, d = y.shape
    tm = _row_tile(s)
    n = s // tm

    def body(y_ref, t_ref, dy_ref, loss_ref, acc_ref):
        i = pl.program_id(0)
        e = y_ref[...] - t_ref[...]
        dy_ref[...] = e * (1.0 / d)
        _acc_rows(acc_ref, e * e, i == 0)

        @pl.when(i == n - 1)
        def _():
            tot = jnp.sum(acc_ref[...], axis=1, keepdims=True) * (0.5 / d)
            loss_ref[...] = jnp.broadcast_to(tot, (1, 128))

    row = pl.BlockSpec((tm, d), lambda i: (i, 0))
    return pl.pallas_call(body, grid=(n,), in_specs=[row, row],
                          out_specs=[row, pl.BlockSpec((1, 128), lambda i: (0, 0))],
                          out_shape=[jax.ShapeDtypeStruct((s, d), F32), jax.ShapeDtypeStruct((1, 128), F32)],
                          scratch_shapes=[pltpu.VMEM((1, d), F32)], name=name, compiler_params=_cparams())(y, tgt)


POOL_HALO = 16
POOL_ROWS = 256


def _pool_counts(r0, rows):
    t1 = (lax.broadcasted_iota(jnp.int32, (rows, 128), 0) + r0 + 1).astype(F32)
    low = lax.broadcasted_iota(jnp.int32, (rows, 128), 1) < POOL_GROUP
    wa = jnp.where(low, float(POOL_WINDOWS[0]), float(POOL_WINDOWS[1]))
    wb = jnp.where(low, float(POOL_WINDOWS[2]), float(POOL_WINDOWS[3]))
    return jnp.minimum(t1, wa), jnp.minimum(t1, wb), low


def _window_sums(win, off, rows, sign):
    def sl(j, half):
        return win[off + sign * j: off + sign * j + rows, 128 * half:128 * half + 128]
    a2 = sl(0, 0) + sl(1, 0)
    a4 = a2 + sl(2, 0) + sl(3, 0)
    a8 = sl(0, 1)
    for j in range(1, 8):
        a8 = a8 + sl(j, 1)
    a16 = a8
    for j in range(8, 16):
        a16 = a16 + sl(j, 1)
    return a2, a4, a8, a16


def _pool_fwd(zp, wp_bd, pscale, name):
    s = zp.shape[0]
    r = min(POOL_ROWS, s)

    def body(z_ref, wp_ref, sc_ref, p_ref, feat_ref, pad):
        pad[0:POOL_HALO, :] = jnp.zeros((POOL_HALO, D_POOL), F32)
        pad[POOL_HALO:, :] = z_ref[...]

        def step(i, carry):
            r0 = pl.multiple_of(i * r, r)
            win = pad[pl.ds(r0, r + POOL_HALO), :]
            a2, a4, a8, a16 = _window_sums(win, POOL_HALO, r, -1)
            ca, cb, low = _pool_counts(r0, r)
            x0 = win[POOL_HALO:, :]
            pa = jnp.where(low, a2, a4) / ca
            pb = jnp.where(low, a8, a16) / cb
            p = (jnp.concatenate([pa, pb], axis=1) - x0).astype(BF16)
            p_ref[pl.ds(r0, r), :] = p
            pw = jnp.dot(p, wp_ref[...], preferred_element_type=F32)
            feat_ref[pl.ds(r0, r), :] = (pw * sc_ref[...]).astype(BF16)
            return carry

        lax.fori_loop(0, s // r, step, 0)

    return pl.pallas_call(
        body, out_shape=[jax.ShapeDtypeStruct((s, D_POOL), BF16), jax.ShapeDtypeStruct((s, D_POOL), BF16)],
        scratch_shapes=[pltpu.VMEM((s + POOL_HALO, D_POOL), F32)], name=name, compiler_params=_cparams(),
    )(zp, wp_bd, pscale)


def _pool_bwd(dfeat, p, wp_bd, pscale, name):
    s = p.shape[0]
    r = min(POOL_ROWS, s)

    def body(df_ref, p_ref, wp_ref, sc_ref, dz_ref, dwp_ref, dsc_ref, gpad, dpbuf):
        dwp_ref[...] = jnp.zeros_like(dwp_ref)
        dsc_ref[...] = jnp.zeros_like(dsc_ref)
        gpad[s:, :] = jnp.zeros((POOL_HALO, D_POOL), F32)

        def step1(i, carry):
            r0 = pl.multiple_of(i * r, r)
            pv = p_ref[pl.ds(r0, r), :]
            dfv = df_ref[pl.ds(r0, r), :]
            pw = jnp.dot(pv, wp_ref[...], preferred_element_type=F32)
            dsc_ref[...] += jnp.sum(dfv * pw, axis=0, keepdims=True)
            dpw = (dfv * sc_ref[...]).astype(BF16)
            dwp_ref[...] += lax.dot_general(pv, dpw, _DIMS["tn"], preferred_element_type=F32)
            dp = lax.dot_general(dpw, wp_ref[...], _DIMS["nt"], preferred_element_type=F32)
            ca, cb, _ = _pool_counts(r0, r)
            gpad[pl.ds(r0, r), :] = dp / jnp.concatenate([ca, cb], axis=1)
            dpbuf[pl.ds(r0, r), :] = dp
            return carry

        lax.fori_loop(0, s // r, step1, 0)

        def step2(i, carry):
            r0 = pl.multiple_of(i * r, r)
            win = gpad[pl.ds(r0, r + POOL_HALO), :]
            a2, a4, a8, a16 = _window_sums(win, 0, r, 1)
            low = lax.broadcasted_iota(jnp.int32, (r, 128), 1) < POOL_GROUP
            acc = jnp.concatenate([jnp.where(low, a2, a4), jnp.where(low, a8, a16)], axis=1)
            dz_ref[pl.ds(r0, r), :] = (acc - dpbuf[pl.ds(r0, r), :]).astype(BF16)
            return carry

        lax.fori_loop(0, s // r, step2, 0)

    return pl.pallas_call(
        body,
        out_shape=[jax.ShapeDtypeStruct((s, D_POOL), BF16), jax.ShapeDtypeStruct((D_POOL, D_POOL), F32),
                   jax.ShapeDtypeStruct((1, D_POOL), F32)],
        scratch_shapes=[pltpu.VMEM((s + POOL_HALO, D_POOL), F32), pltpu.VMEM((s, D_POOL), F32)],
        name=name, compiler_params=_cparams(),
    )(dfeat, p, wp_bd, pscale)


def _skew_index():
    cp = lax.broadcasted_iota(jnp.int32, (SKEW_W, N_REL), 0)
    dist = jnp.where(cp < KW, KPAD - cp, KPAD + SKEW_W - cp)
    idx = jnp.clip(dist, -REL_CLIP, REL_CLIP) + REL_CLIP
    return (idx == lax.broadcasted_iota(jnp.int32, (SKEW_W, N_REL), 1)).astype(F32)


def _row_bits(b):
    return (lax.broadcasted_iota(jnp.int32, (QB, SKEW_W), 0) >> b) & 1 == 1


def _bias_block(rel_bias, name):
    def body(rb_ref, o_ref):
        onehot = _skew_index()
        row0 = lax.dot_general(rb_ref[...], onehot, _DIMS["nt"], precision=lax.Precision.HIGHEST,
                               preferred_element_type=F32)
        r = lax.broadcasted_iota(jnp.int32, (QB, KW), 0)
        kk = lax.broadcasted_iota(jnp.int32, (QB, KW), 1)
        cq, ck = r // CHUNK, kk // CHUNK
        band = (ck >= cq) & (ck <= cq + N_PREV_CHUNKS)
        for h in range(N_HEADS):
            t = jnp.broadcast_to(row0[h:h + 1, :], (QB, SKEW_W))
            for b in range(7):
                t = jnp.where(_row_bits(b), pltpu.roll(t, 1 << b, 1), t)
            o_ref[h] = jnp.where(band, t[:, :KW], NEG_INF)

    return pl.pallas_call(body, out_shape=jax.ShapeDtypeStruct((N_HEADS, QB, KW), F32), name=name,
                          compiler_params=_cparams())(rel_bias)


def _bias_block_bwd(ds_acc, name):
    def body(ds_ref, o_ref):
        sums = []
        for h in range(N_HEADS):
            t = jnp.concatenate([ds_ref[h], jnp.zeros((QB, SKEW_W - KW), F32)], axis=1)
            for b in range(7):
                t = jnp.where(_row_bits(b), pltpu.roll(t, SKEW_W - (1 << b), 1), t)
            sums.append(jnp.sum(t, axis=0, keepdims=True))
        allh = jnp.concatenate(sums, axis=0)
        o_ref[...] = jnp.dot(allh, _skew_index(), precision=lax.Precision.HIGHEST, preferred_element_type=F32)

    return pl.pallas_call(body, out_shape=jax.ShapeDtypeStruct((N_HEADS, N_REL), F32), name=name,
                          compiler_params=_cparams())(ds_acc)


def _scores(qh, kh, bias_h, valid):
    sc = lax.dot_general(qh, kh, _DIMS["nt"], preferred_element_type=F32) * (HEAD_DIM ** -0.5) + bias_h
    sc = jnp.where(valid, sc, NEG_INF)
    e = jnp.exp(sc - jnp.max(sc, axis=-1, keepdims=True))
    return e * (1.0 / jnp.sum(e, axis=-1, keepdims=True))


def _load_padded_kv(qkv_hbm, kpad, vpad, sems, s):
    kpad[0:KPAD, :] = jnp.zeros((KPAD, D_ATTN), BF16)
    vpad[0:KPAD, :] = jnp.zeros((KPAD, D_ATTN), BF16)
    ck = pltpu.make_async_copy(qkv_hbm.at[:, D_ATTN:2 * D_ATTN], kpad.at[pl.ds(KPAD, s), :], sems.at[0])
    cv = pltpu.make_async_copy(qkv_hbm.at[:, 2 * D_ATTN:3 * D_ATTN], vpad.at[pl.ds(KPAD, s), :], sems.at[1])
    ck.start()
    cv.start()
    ck.wait()
    cv.wait()


def _attn_fwd(qkv, bias, name):
    s = qkv.shape[0]

    def body(q_ref, qkv_hbm, bias_ref, o_ref, kpad, vpad, sems):
        i = pl.program_id(0)

        @pl.when(i == 0)
        def _():
            _load_padded_kv(qkv_hbm, kpad, vpad, sems, s)

        base = pl.multiple_of(i * QB, QB)
        kw = kpad[pl.ds(base, KW), :]
        vw = vpad[pl.ds(base, KW), :]
        q = q_ref[...]
        valid = lax.broadcasted_iota(jnp.int32, (QB, KW), 1) >= KPAD - base
        outs = []
        for h in range(N_HEADS):
            hs = slice(HEAD_DIM * h, HEAD_DIM * (h + 1))
            p = _scores(q[:, hs], kw[:, hs], bias_ref[h], valid)
            outs.append(jnp.dot(p.astype(BF16), vw[:, hs], preferred_element_type=F32))
        o_ref[...] = jnp.concatenate(outs, axis=1).astype(BF16)

    return pl.pallas_call(
        body, grid=(s // QB,),
        in_specs=[pl.BlockSpec((QB, D_ATTN), lambda i: (i, 0)), pl.BlockSpec(memory_space=pl.ANY),
                  _full((N_HEADS, QB, KW))],
        out_specs=pl.BlockSpec((QB, D_ATTN), lambda i: (i, 0)),
        out_shape=jax.ShapeDtypeStruct((s, D_ATTN), BF16),
        scratch_shapes=[pltpu.VMEM((s + KPAD, D_ATTN), BF16), pltpu.VMEM((s + KPAD, D_ATTN), BF16),
                        pltpu.SemaphoreType.DMA((2,))],
        name=name, compiler_params=_cparams(),
    )(qkv, qkv, bias)


def _attn_bwd(qkv, do, bias, name):
    s = qkv.shape[0]
    n = s // QB

    def body(q_ref, qkv_hbm, do_ref, bias_ref, dq_ref, dk_hbm, dv_hbm, ds_ref, kpad, vpad, dkacc, dvacc, sems):
        i = pl.program_id(0)

        @pl.when(i == 0)
        def _():
            _load_padded_kv(qkv_hbm, kpad, vpad, sems, s)
            dkacc[...] = jnp.zeros_like(dkacc)
            dvacc[...] = jnp.zeros_like(dvacc)
            ds_ref[...] = jnp.zeros_like(ds_ref)

        base = pl.multiple_of(i * QB, QB)
        kw = kpad[pl.ds(base, KW), :]
        vw = vpad[pl.ds(base, KW), :]
        q = q_ref[...]
        dov = do_ref[...]
        valid = lax.broadcasted_iota(jnp.int32, (QB, KW), 1) >= KPAD - base
        dqs, dks, dvs = [], [], []
        for h in range(N_HEADS):
            hs = slice(HEAD_DIM * h, HEAD_DIM * (h + 1))
            qh, kh, vh, doh = q[:, hs], kw[:, hs], vw[:, hs], dov[:, hs]
            p = _scores(qh, kh, bias_ref[h], valid)
            dvs.append(lax.dot_general(p.astype(BF16), doh, _DIMS["tn"], preferred_element_type=F32))
            dp = lax.dot_general(doh, vh, _DIMS["nt"], preferred_element_type=F32)
            ds = p * (dp - jnp.sum(dp * p, axis=-1, keepdims=True))
            ds_ref[h] += ds
            dsb = ds.astype(BF16)
            dqs.append(jnp.dot(dsb, kh, preferred_element_type=F32) * (HEAD_DIM ** -0.5))
            dks.append(lax.dot_general(dsb, qh, _DIMS["tn"], preferred_element_type=F32) * (HEAD_DIM ** -0.5))
        dq_ref[...] = jnp.concatenate(dqs, axis=1).astype(BF16)
        dkacc[pl.ds(base, KW), :] += jnp.concatenate(dks, axis=1)
        dvacc[pl.ds(base, KW), :] += jnp.concatenate(dvs, axis=1)

        @pl.when(i == n - 1)
        def _():
            ck = pltpu.make_async_copy(dkacc, dk_hbm, sems.at[0])
            cv = pltpu.make_async_copy(dvacc, dv_hbm, sems.at[1])
            ck.start()
            cv.start()
            ck.wait()
            cv.wait()

    blk = pl.BlockSpec((QB, D_ATTN), lambda i: (i, 0))
    acc_shape = jax.ShapeDtypeStruct((s + KPAD, D_ATTN), F32)
    return pl.pallas_call(
        body, grid=(n,),
        in_specs=[blk, pl.BlockSpec(memory_space=pl.ANY), blk, _full((N_HEADS, QB, KW))],
        out_specs=[blk, pl.BlockSpec(memory_space=pl.ANY), pl.BlockSpec(memory_space=pl.ANY), _full((N_HEADS, QB, KW))],
        out_shape=[jax.ShapeDtypeStruct((s, D_ATTN), BF16), acc_shape, acc_shape,
                   jax.ShapeDtypeStruct((N_HEADS, QB, KW), F32)],
        scratch_shapes=[pltpu.VMEM((s + KPAD, D_ATTN), BF16), pltpu.VMEM((s + KPAD, D_ATTN), BF16),
                        pltpu.VMEM((s + KPAD, D_ATTN), F32), pltpu.VMEM((s + KPAD, D_ATTN), F32),
                        pltpu.SemaphoreType.DMA((2,))],
        name=name, compiler_params=_cparams(),
    )(qkv, qkv, do, bias)


CONV_HALO = 32
CONV_ROWS = 64


def _sigmoid(t):
    return 1.0 / (1.0 + jnp.exp(-t))


def _glu_rows(z_ref, r0, rows):
    a = z_ref[pl.ds(r0, rows), 0:D_CONV]
    b = z_ref[pl.ds(r0, rows), D_CONV:2 * D_CONV]
    return a, _sigmoid(b)


def _conv_fwd(zc, conv_w, conv_b, ln_g, ln_b, name):
    s = zc.shape[0]
    rt = min(256, s)

    def body(z_ref, w_ref, cb_ref, g_ref, b_ref, cv_ref, feat_ref, hpad):
        hpad[0:CONV_HALO, :] = jnp.zeros((CONV_HALO, D_CONV), F32)

        def glu(i, carry):
            r0 = pl.multiple_of(i * rt, rt)
            a, sb = _glu_rows(z_ref, r0, rt)
            hpad[pl.ds(r0 + CONV_HALO, rt), :] = a * sb
            return carry

        lax.fori_loop(0, s // rt, glu, 0)
        w = w_ref[...]

        def conv(i, carry):
            r0 = pl.multiple_of(i * CONV_ROWS, CONV_ROWS)
            win = hpad[pl.ds(r0, CONV_ROWS + CONV_HALO), :]
            acc = jnp.broadcast_to(cb_ref[...], (CONV_ROWS, D_CONV))
            for k in range(CONV_WIDTH):
                acc = acc + win[2 + k:2 + k + CONV_ROWS, :] * w[k:k + 1, :]
            cv_ref[pl.ds(r0, CONV_ROWS), :] = acc
            yhat, _ = _ln_hat(acc)
            y = yhat * g_ref[...] + b_ref[...]
            feat_ref[pl.ds(r0, CONV_ROWS), :] = (y * _sigmoid(y)).astype(BF16)
            return carry

        lax.fori_loop(0, s // CONV_ROWS, conv, 0)

    return pl.pallas_call(
        body, out_shape=[jax.ShapeDtypeStruct((s, D_CONV), F32), jax.ShapeDtypeStruct((s, D_CONV), BF16)],
        scratch_shapes=[pltpu.VMEM((s + CONV_HALO, D_CONV), F32)], name=name, compiler_params=_cparams(),
    )(zc, conv_w, conv_b, ln_g, ln_b)


def _conv_bwd(dfeat, cv, zc, conv_w, ln_g, ln_b, name):
    s = zc.shape[0]
    rt = min(256, s)

    def body(df_ref, cv_ref, z_ref, w_ref, g_ref, b_ref, dz_ref, dw_ref, dcb_ref, dg_ref, db_ref, hpad, dcvpad, dwacc):
        hpad[0:CONV_HALO, :] = jnp.zeros((CONV_HALO, D_CONV), F32)
        dcvpad[s:, :] = jnp.zeros((CONV_HALO, D_CONV), F32)
        dwacc[...] = jnp.zeros_like(dwacc)
        dcb_ref[...] = jnp.zeros_like(dcb_ref)
        dg_ref[...] = jnp.zeros_like(dg_ref)
        db_ref[...] = jnp.zeros_like(db_ref)

        def pass1(i, carry):
            r0 = pl.multiple_of(i * rt, rt)
            a, sb = _glu_rows(z_ref, r0, rt)
            hpad[pl.ds(r0 + CONV_HALO, rt), :] = a * sb
            cvhat, rstd = _ln_hat(cv_ref[pl.ds(r0, rt), :])
            y = cvhat * g_ref[...] + b_ref[...]
            sg = _sigmoid(y)
            dy = df_ref[pl.ds(r0, rt), :] * (sg * (1.0 + y * (1.0 - sg)))
            dg_ref[...] += jnp.sum(dy * cvhat, axis=0, keepdims=True)
            db_ref[...] += jnp.sum(dy, axis=0, keepdims=True)
            dcv = _ln_hat_bwd(dy * g_ref[...], cvhat, rstd)
            dcb_ref[...] += jnp.sum(dcv, axis=0, keepdims=True)
            dcvpad[pl.ds(r0, rt), :] = dcv
            return carry

        lax.fori_loop(0, s // rt, pass1, 0)
        w = w_ref[...]

        def pass2(i, carry):
            r0 = pl.multiple_of(i * CONV_ROWS, CONV_ROWS)
            dwin = dcvpad[pl.ds(r0, CONV_ROWS + CONV_HALO), :]
            hwin = hpad[pl.ds(r0, CONV_ROWS + CONV_HALO), :]
            dcv = dwin[0:CONV_ROWS, :]
            dh = jnp.zeros((CONV_ROWS, D_CONV), F32)
            for k in range(CONV_WIDTH):
                dh = dh + dwin[30 - k:30 - k + CONV_ROWS, :] * w[k:k + 1, :]
                prod = dcv * hwin[2 + k:2 + k + CONV_ROWS, :]
                dwacc[8 * k:8 * k + 8, :] += jnp.sum(prod.reshape(CONV_ROWS // 8, 8, D_CONV), axis=0)
            a, sb = _glu_rows(z_ref, r0, CONV_ROWS)
            dz_ref[pl.ds(r0, CONV_ROWS), :] = jnp.concatenate([dh * sb, dh * a * sb * (1.0 - sb)], axis=1).astype(BF16)
            return carry

        lax.fori_loop(0, s // CONV_ROWS, pass2, 0)
        dw_ref[...] = jnp.sum(dwacc[...].reshape(32, 8, D_CONV), axis=1)

    vs = jax.ShapeDtypeStruct((1, D_CONV), F32)
    return pl.pallas_call(
        body,
        out_shape=[jax.ShapeDtypeStruct((s, 2 * D_CONV), BF16), jax.ShapeDtypeStruct((32, D_CONV), F32), vs, vs, vs],
        scratch_shapes=[pltpu.VMEM((s + CONV_HALO, D_CONV), F32), pltpu.VMEM((s + CONV_HALO, D_CONV), F32),
                        pltpu.VMEM((256, D_CONV), F32)],
        name=name, compiler_params=_cparams(),
    )(dfeat, cv, zc, conv_w, ln_g, ln_b)


def _merge(zg, b_gate, ys, name):
    s = zg.shape[0]
    tm = _row_tile(s)

    def body(zg_ref, bg_ref, y0_ref, y1_ref, y2_ref, o_ref):
        acc = None
        for j, y_ref in enumerate((y0_ref, y1_ref, y2_ref)):
            cs = slice(D_MODEL * j, D_MODEL * (j + 1))
            t = _sigmoid(zg_ref[:, cs] + bg_ref[:, cs]) * y_ref[...]
            acc = t if acc is None else acc + t
        o_ref[...] = acc.astype(BF16)

    row = pl.BlockSpec((tm, D_MODEL), lambda i: (i, 0))
    return pl.pallas_call(
        body, grid=(s // tm,),
        in_specs=[pl.BlockSpec((tm, 3 * D_MODEL), lambda i: (i, 0)), _full((1, 3 * D_MODEL)), row, row, row],
        out_specs=row, out_shape=jax.ShapeDtypeStruct((s, D_MODEL), BF16), name=name, compiler_params=_cparams(),
    )(zg, b_gate, *ys)


def _merge_bwd(dm, zg, b_gate, ys, name):
    s = zg.shape[0]
    tm = min(256, s)

    def body(dm_ref, zg_ref, bg_ref, y0_ref, y1_ref, y2_ref, d0_ref, d1_ref, d2_ref, dzg_ref, dbg_ref):
        first = pl.program_id(0) == 0

        @pl.when(first)
        def _():
            dbg_ref[...] = jnp.zeros_like(dbg_ref)

        dmv = dm_ref[...]
        for j, (y_ref, d_ref) in enumerate(((y0_ref, d0_ref), (y1_ref, d1_ref), (y2_ref, d2_ref))):
            cs = slice(D_MODEL * j, D_MODEL * (j + 1))
            g = _sigmoid(zg_ref[:, cs] + bg_ref[:, cs])
            d_ref[...] = (dmv * g).astype(BF16)
            dzg = dmv * y_ref[...] * g * (1.0 - g)
            dzg_ref[:, cs] = dzg.astype(BF16)
            dbg_ref[:, cs] += jnp.sum(dzg, axis=0, keepdims=True)

    row = pl.BlockSpec((tm, D_MODEL), lambda i: (i, 0))
    wide = pl.BlockSpec((tm, 3 * D_MODEL), lambda i: (i, 0))
    yb = jax.ShapeDtypeStruct((s, D_MODEL), BF16)
    return pl.pallas_call(
        body, grid=(s // tm,),
        in_specs=[row, wide, _full((1, 3 * D_MODEL)), row, row, row],
        out_specs=[row, row, row, wide, _full((1, 3 * D_MODEL))],
        out_shape=[yb, yb, yb, jax.ShapeDtypeStruct((s, 3 * D_MODEL), BF16), jax.ShapeDtypeStruct((1, 3 * D_MODEL), F32)],
        name=name, compiler_params=_cparams(),
    )(dm, zg, b_gate, *ys)


def _ff_hidden_bwd(dff, w_ff2, hpre, name):
    s = dff.shape[0]
    tm, tn = min(512, s), 1024

    def body(a_ref, b_ref, h_ref, o_ref, sum_ref):
        dh = lax.dot_general(a_ref[...], b_ref[...], _DIMS["nt"], preferred_element_type=F32)
        dpre = dh * (2.0 * jnp.maximum(h_ref[...], 0.0))
        o_ref[...] = dpre.astype(BF16)
        _acc_rows(sum_ref, dpre, pl.program_id(1) == 0)

    return pl.pallas_call(
        body, grid=(D_FF // tn, s // tm),
        in_specs=[pl.BlockSpec((tm, D_MODEL), lambda j, i: (i, 0)), pl.BlockSpec((tn, D_MODEL), lambda j, i: (j, 0)),
                  pl.BlockSpec((tm, tn), lambda j, i: (i, j))],
        out_specs=[pl.BlockSpec((tm, tn), lambda j, i: (i, j)), pl.BlockSpec((1, tn), lambda j, i: (0, j))],
        out_shape=[jax.ShapeDtypeStruct((s, D_FF), BF16), jax.ShapeDtypeStruct((1, D_FF), F32)],
        name=name, compiler_params=_cparams(),
    )(dff, w_ff2, hpre)


def _silu(t):
    return t * _sigmoid(t)


def _mod_fwd(c_all, w_ada_sh, b_ada_sh, name):
    cols = w_ada_sh.shape[2]

    def body(c_ref, w_ref, b_ref, o_ref):
        ca = _silu(c_ref[...]).astype(BF16)
        o_ref[0] = jnp.dot(ca, w_ref[0].astype(BF16), preferred_element_type=F32) + b_ref[0]

    return pl.pallas_call(
        body, grid=(DEPTH,),
        in_specs=[_full((N_DEV, D_MODEL)), pl.BlockSpec((1, D_MODEL, cols), lambda l: (l, 0, 0)),
                  pl.BlockSpec((1, 1, cols), lambda l: (l, 0, 0))],
        out_specs=pl.BlockSpec((1, N_DEV, cols), lambda l: (l, 0, 0)),
        out_shape=jax.ShapeDtypeStruct((DEPTH, N_DEV, cols), F32), name=name, compiler_params=_cparams(),
    )(c_all, w_ada_sh, b_ada_sh)


def _mod_bwd(c_all, dmod_sh, name):
    cols = dmod_sh.shape[2]

    def body(c_ref, d_ref, o_ref):
        ca = _silu(c_ref[...])
        o_ref[0] = lax.dot_general(ca, d_ref[0], _DIMS["tn"], precision=lax.Precision.HIGHEST,
                                   preferred_element_type=F32)

    return pl.pallas_call(
        body, grid=(DEPTH,),
        in_specs=[_full((N_DEV, D_MODEL)), pl.BlockSpec((1, N_DEV, cols), lambda l: (l, 0, 0))],
        out_specs=pl.BlockSpec((1, D_MODEL, cols), lambda l: (l, 0, 0)),
        out_shape=jax.ShapeDtypeStruct((DEPTH, D_MODEL, cols), F32), name=name, compiler_params=_cparams(),
    )(c_all, dmod_sh)


def _flat_tiles(rows, cols, itemsize_total):
    budget = 12 * 1024 * 1024
    tr = rows
    while tr % 32 == 0 and tr * cols * itemsize_total > budget:
        tr //= 2
    return tr


def _sum2_bf16(a, b, name):
    n, r, c = a.shape
    tr = _flat_tiles(r, c, 6 * n)

    def body(a_ref, b_ref, o_ref):
        o_ref[...] = (a_ref[...].astype(F32) + b_ref[...].astype(F32)).astype(BF16)

    blk = pl.BlockSpec((n, tr, c), lambda i: (0, i, 0))
    return pl.pallas_call(body, grid=(r // tr,), in_specs=[blk, blk], out_specs=blk,
                          out_shape=jax.ShapeDtypeStruct((n, r, c), BF16), name=name, compiler_params=_cparams())(a, b)


def _sum4(parts, name):
    _, r, c = parts.shape
    tr = _flat_tiles(r, c, 12)

    def body(p_ref, o_ref):
        o_ref[...] = ((p_ref[0].astype(F32) + p_ref[1].astype(F32)) + p_ref[2].astype(F32)) + p_ref[3].astype(F32)

    return pl.pallas_call(body, grid=(r // tr,), in_specs=[pl.BlockSpec((4, tr, c), lambda i: (0, i, 0))],
                          out_specs=pl.BlockSpec((tr, c), lambda i: (i, 0)),
                          out_shape=jax.ShapeDtypeStruct((r, c), F32), name=name, compiler_params=_cparams())(parts)


def _adam_math(w, g, m, v):
    m2 = ADAM_B1 * m + (1.0 - ADAM_B1) * g
    v2 = ADAM_B2 * v + (1.0 - ADAM_B2) * (g * g)
    m_hat = m2 / (1.0 - ADAM_B1 ** ADAM_STEP)
    v_hat = v2 / (1.0 - ADAM_B2 ** ADAM_STEP)
    delta = -ADAM_LR * (m_hat / (jnp.sqrt(v_hat) + ADAM_EPS) + ADAM_WD * w)
    return delta, m2, v2


def _adamw(w, m, v, grads, name):
    r, c = w.shape
    tr = _flat_tiles(r, c, 4 * (7 + len(grads)))

    def body(*refs):
        w_ref, m_ref, v_ref = refs[:3]
        g_refs = refs[3:3 + len(grads)]
        g_ref, d_ref, m2_ref, v2_ref = refs[3 + len(grads):]
        g = g_refs[0][...]
        for gr in g_refs[1:]:
            g = g + gr[...]
        delta, m2, v2 = _adam_math(w_ref[...], g, m_ref[...], v_ref[...])
        g_ref[...] = g
        d_ref[...] = delta
        m2_ref[...] = m2
        v2_ref[...] = v2

    blk = pl.BlockSpec((tr, c), lambda i: (i, 0))
    sh = jax.ShapeDtypeStruct((r, c), F32)
    return pl.pallas_call(body, grid=(r // tr,), in_specs=[blk] * (3 + len(grads)), out_specs=[blk] * 4,
                          out_shape=[sh] * 4, name=name, compiler_params=_cparams())(w, m, v, *grads)


def _adamw_small(w, m, v, g_all, name):
    r, c = w.shape

    def body(w_ref, m_ref, v_ref, g_ref, go_ref, d_ref, m2_ref, v2_ref):
        g = g_ref[0]
        for b in range(1, N_DEV):
            g = g + g_ref[b]
        delta, m2, v2 = _adam_math(w_ref[...], g, m_ref[...], v_ref[...])
        go_ref[...] = g
        d_ref[...] = delta
        m2_ref[...] = m2
        v2_ref[...] = v2

    sh = jax.ShapeDtypeStruct((r, c), F32)
    return pl.pallas_call(body, out_shape=[sh] * 4, name=name, compiler_params=_cparams())(w, m, v, g_all)


def _me():
    return lax.axis_index("x"), lax.axis_index("y"), lax.axis_index("c")


def _flip(v, bit):
    return 1 - v if bit else v


def _allgather_small(blk, name):
    r, c = blk.shape

    def body(x_ref, o_ref, send_sems, recv_sems):
        x, y, cc = _me()
        me = 4 * x + 2 * y + cc
        copies = []
        for k in range(1, N_DEV):
            peer = (_flip(x, k & 4), _flip(y, k & 2), _flip(cc, k & 1))
            cp = pltpu.make_async_remote_copy(src_ref=x_ref, dst_ref=o_ref.at[me], send_sem=send_sems.at[k - 1],
                                              recv_sem=recv_sems.at[k - 1], device_id=peer, device_id_type=MESH)
            cp.start()
            copies.append(cp)
        o_ref[me] = x_ref[...]
        for cp in copies:
            cp.wait()

    return pl.pallas_call(
        body, out_shape=jax.ShapeDtypeStruct((N_DEV, r, c), F32),
        in_specs=[pl.BlockSpec(memory_space=pltpu.VMEM)], out_specs=pl.BlockSpec(memory_space=pltpu.VMEM),
        scratch_shapes=[pltpu.SemaphoreType.DMA((N_DEV - 1,)), pltpu.SemaphoreType.DMA((N_DEV - 1,))],
        name=name, compiler_params=_cparams(),
    )(blk)


def _gather_rows(shards, name):
    n = len(shards)

    def body(*refs):
        ins, outs = refs[:n], refs[n:2 * n]
        ici_send, ici_recv, d2d_send, d2d_recv, loc_sems = refs[2 * n:]
        x, y, cc = _me()
        chip = 2 * x + y
        sibling = (x, y, 1 - cc)
        local, sends, relays = [], [], []
        for j, a in enumerate(shards):
            rs = a.shape[0]
            half = rs // 2

            def rows(ch, h, j=j, rs=rs, half=half):
                return outs[j].at[pl.ds(ch * rs + h * half, half), :]

            lc = pltpu.make_async_copy(ins[j], outs[j].at[pl.ds(chip * rs, rs), :], loc_sems.at[j])
            lc.start()
            local.append(lc)
            for k in range(1, N_CHIP):
                px, py = _flip(x, k & 2), _flip(y, k & 1)
                pchip = 2 * px + py
                q = 3 * j + k - 1
                out_cp = pltpu.make_async_remote_copy(src_ref=ins[j].at[pl.ds(cc * half, half), :], dst_ref=rows(chip, cc),
                                                      send_sem=ici_send.at[q], recv_sem=ici_recv.at[q],
                                                      device_id=(px, py, cc), device_id_type=MESH)
                out_cp.start()
                sends.append(out_cp)
                arrival = pltpu.make_async_remote_copy(src_ref=rows(pchip, cc), dst_ref=rows(pchip, cc),
                                                       send_sem=ici_send.at[q], recv_sem=ici_recv.at[q],
                                                       device_id=(px, py, cc), device_id_type=MESH)
                forward = pltpu.make_async_remote_copy(src_ref=rows(pchip, cc), dst_ref=rows(pchip, cc),
                                                       send_sem=d2d_send.at[q], recv_sem=d2d_recv.at[q],
                                                       device_id=sibling, device_id_type=MESH)
                from_sibling = pltpu.make_async_remote_copy(src_ref=rows(pchip, 1 - cc), dst_ref=rows(pchip, 1 - cc),
                                                            send_sem=d2d_send.at[q], recv_sem=d2d_recv.at[q],
                                                            device_id=sibling, device_id_type=MESH)
                relays.append((arrival, forward, from_sibling))
        for arrival, forward, _ in relays:
            arrival.wait_recv()
            forward.start()
        for cp in sends:
            cp.wait_send()
        for _, forward, from_sibling in relays:
            forward.wait_send()
            from_sibling.wait_recv()
        for lc in local:
            lc.wait()

    anyspec = pl.BlockSpec(memory_space=pl.ANY)
    return pl.pallas_call(
        body, out_shape=[jax.ShapeDtypeStruct((N_CHIP * a.shape[0], a.shape[1]), a.dtype) for a in shards],
        in_specs=[anyspec] * n, out_specs=[anyspec] * n,
        scratch_shapes=[pltpu.SemaphoreType.DMA((3 * n,)), pltpu.SemaphoreType.DMA((3 * n,)),
                        pltpu.SemaphoreType.DMA((3 * n,)), pltpu.SemaphoreType.DMA((3 * n,)),
                        pltpu.SemaphoreType.DMA((n,))],
        name=name, compiler_params=_cparams(),
    )(*shards)


def _swap_halves(dws, name):
    n = len(dws)

    def body(*refs):
        ins, owns, recvs = refs[:n], refs[n:2 * n], refs[2 * n:3 * n]
        send_sems, recv_sems, loc_sems = refs[3 * n:]
        x, y, cc = _me()
        pending = []
        for j in range(n):
            lc = pltpu.make_async_copy(ins[j].at[:, cc], owns[j], loc_sems.at[j])
            cp = pltpu.make_async_remote_copy(src_ref=ins[j].at[:, 1 - cc], dst_ref=recvs[j], send_sem=send_sems.at[j],
                                              recv_sem=recv_sems.at[j], device_id=(x, y, 1 - cc), device_id_type=MESH)
            lc.start()
            cp.start()
            pending += [lc, cp]
        for cp in pending:
            cp.wait()

    anyspec = pl.BlockSpec(memory_space=pl.ANY)
    half_shape = [jax.ShapeDtypeStruct((a.shape[0],) + a.shape[2:], a.dtype) for a in dws]
    res = pl.pallas_call(
        body, out_shape=half_shape + half_shape, in_specs=[anyspec] * n, out_specs=[anyspec] * (2 * n),
        scratch_shapes=[pltpu.SemaphoreType.DMA((n,)), pltpu.SemaphoreType.DMA((n,)), pltpu.SemaphoreType.DMA((n,))],
        name=name, compiler_params=_cparams(),
    )(*dws)
    return res[:n], res[n:]


def _chip_scatter(arrs, name):
    n = len(arrs)

    def body(*refs):
        ins, outs = refs[:n], refs[n:2 * n]
        send_sems, recv_sems, loc_sems = refs[2 * n:]
        x, y, cc = _me()
        chip = 2 * x + y
        pending = []
        for j in range(n):
            lc = pltpu.make_async_copy(ins[j].at[chip], outs[j].at[chip], loc_sems.at[j])
            lc.start()
            pending.append(lc)
            for k in range(1, N_CHIP):
                px, py = _flip(x, k & 2), _flip(y, k & 1)
                cp = pltpu.make_async_remote_copy(src_ref=ins[j].at[2 * px + py], dst_ref=outs[j].at[chip],
                                                  send_sem=send_sems.at[3 * j + k - 1], recv_sem=recv_sems.at[3 * j + k - 1],
                                                  device_id=(px, py, cc), device_id_type=MESH)
                cp.start()
                pending.append(cp)
        for cp in pending:
            cp.wait()

    anyspec = pl.BlockSpec(memory_space=pl.ANY)
    return pl.pallas_call(
        body, out_shape=[jax.ShapeDtypeStruct(a.shape, a.dtype) for a in arrs],
        in_specs=[anyspec] * n, out_specs=[anyspec] * n,
        scratch_shapes=[pltpu.SemaphoreType.DMA((3 * n,)), pltpu.SemaphoreType.DMA((3 * n,)), pltpu.SemaphoreType.DMA((n,))],
        name=name, compiler_params=_cparams(),
    )(*arrs)


def _join_halves(groups, name):
    flat = [a for g in groups for a in g]
    n, ng = len(flat), len(groups)

    def body(*refs):
        ins, outs = refs[:n], refs[n:n + ng]
        send_sems, recv_sems, loc_sems = refs[n + ng:]
        x, y, cc = _me()
        pending = []
        q = 0
        for j, g in enumerate(groups):
            for l, a in enumerate(g):
                half = a.shape[0]
                dst = outs[j].at[l, pl.ds(cc * half, half), :]
                lc = pltpu.make_async_copy(ins[q], dst, loc_sems.at[q])
                cp = pltpu.make_async_remote_copy(src_ref=ins[q], dst_ref=dst, send_sem=send_sems.at[q],
                                                  recv_sem=recv_sems.at[q], device_id=(x, y, 1 - cc), device_id_type=MESH)
                lc.start()
                cp.start()
                pending += [lc, cp]
                q += 1
        for cp in pending:
            cp.wait()

    anyspec = pl.BlockSpec(memory_space=pl.ANY)
    out_shape = [jax.ShapeDtypeStruct((len(g), 2 * g[0].shape[0], g[0].shape[1]), g[0].dtype) for g in groups]
    return pl.pallas_call(
        body, out_shape=out_shape, in_specs=[anyspec] * n, out_specs=[anyspec] * ng,
        scratch_shapes=[pltpu.SemaphoreType.DMA((n,)), pltpu.SemaphoreType.DMA((n,)), pltpu.SemaphoreType.DMA((n,))],
        name=name, compiler_params=_cparams(),
    )(*flat)


COL_SHARDED = ("w_in", "w_br_pool", "w_br_attn", "w_br_conv", "w_ff1")
ROW_SHARDED = ("w_o", "w_ff2")
BIG = COL_SHARDED + ROW_SHARDED
SMALL = ("b_ada", "b_gate", "w_pool", "pool_scale", "rel_bias", "conv_w", "conv_b", "conv_ln_g", "conv_ln_b",
         "ln_mix_g", "ln_mix_b", "b_ff1", "b_ff2", "ln_ff_g", "ln_ff_b")
PACK_W = 1024


def _pack(parts):
    rows = []
    for a in parts:
        flat = a.reshape(-1)
        n = -(-flat.shape[0] // PACK_W) * PACK_W
        rows.append(jnp.pad(flat, (0, n - flat.shape[0])).reshape(-1, PACK_W))
    out = jnp.concatenate(rows, axis=0)
    r = -(-out.shape[0] // 8) * 8
    return jnp.pad(out, ((0, r - out.shape[0]), (0, 0)))


def _unpack(packed, shapes):
    out, r0 = [], 0
    for shp in shapes:
        size = int(np.prod(shp))
        nr = -(-size // PACK_W)
        out.append(packed[r0:r0 + nr].reshape(-1)[:size].reshape(shp))
        r0 += nr
    return out


def _layer_fwd(l, x, mod, W, P):
    s = x.shape[0]
    sh_m, sc_m, g_m, sh_f, sc_f, g_f = [mod[l:l + 1, D_MODEL * j:D_MODEL * (j + 1)] for j in range(6)]
    n = lambda t: f"{t}{l}"
    w_in = W["w_in"][l]
    u = _ln_mod(x, sc_m, sh_m, n("ln_mod_mix"))
    tmz = min(1024, s)
    zp = _mm(u, w_in, "nt", tm=min(2048, s), tn=256, out_dtype=F32, name=n("z_pool"), b_col0=0, n_out=D_POOL)
    qkv = _mm(u, w_in, "nt", tm=tmz, tn=256, out_dtype=BF16, name=n("z_qkv"), b_col0=OFF_QKV // 256, n_out=3 * D_ATTN)
    zc = _mm(u, w_in, "nt", tm=tmz, tn=256, out_dtype=F32, name=n("z_conv"), b_col0=OFF_CONV // 256, n_out=2 * D_CONV)
    zg = _mm(u, w_in, "nt", tm=tmz, tn=768, out_dtype=F32, name=n("z_gate"), b_col0=OFF_GATE // 768, n_out=3 * D_MODEL)

    p, feat_pool = _pool_fwd(zp, P["wp_bd"][l], P["pool_scale"][l], n("pool_fwd"))
    bias = _bias_block(P["rel_bias"][l], n("bias_block"))
    o = _attn_fwd(qkv, bias, n("attn_fwd"))
    cv, feat_conv = _conv_fwd(zc, P["conv_w"][l], P["conv_b"][l], P["conv_ln_g"][l], P["conv_ln_b"][l], n("conv_fwd"))

    tmb = min(1024, s)
    y_pool = _mm(feat_pool, W["w_br_pool"][l], "nt", tm=tmb, tn=1024, out_dtype=F32, name=n("y_pool"))
    y_attn = _mm(o, W["w_br_attn"][l], "nt", tm=tmb, tn=1024, out_dtype=F32, name=n("y_attn"))
    y_conv = _mm(feat_conv, W["w_br_conv"][l], "nt", tm=tmb, tn=1024, out_dtype=F32, name=n("y_conv"))
    ys = (y_pool, y_attn, y_conv)
    merged = _merge(zg, P["b_gate"][l], ys, n("merge"))
    mix = _mm(merged, W["w_o"][l], "nn", tm=tmb, tn=1024, out_dtype=F32, name=n("mix_out"))
    x1 = _resid_ln(x, mix, g_m, P["ln_mix_g"][l], P["ln_mix_b"][l], n("resid_ln_mix"))

    u2 = _ln_mod(x1, sc_f, sh_f, n("ln_mod_ff"))
    hpre = _mm(u2, W["w_ff1"][l], "nt", tm=tmb, tn=1024, out_dtype=F32, name=n("ff1"), bias=P["b_ff1"][l])
    ff = _mm(hpre, W["w_ff2"][l], "nn", tm=min(256, s), tn=1024, out_dtype=F32, name=n("ff2"), a_fn=_relu2,
             bias=P["b_ff2"][l])
    x2 = _resid_ln(x1, ff, g_f, P["ln_ff_g"][l], P["ln_ff_b"][l], n("resid_ln_ff"))
    saved = dict(x=x, u=u, zp=zp, qkv=qkv, zc=zc, zg=zg, p=p, feat_pool=feat_pool, bias=bias, o=o, cv=cv,
                 feat_conv=feat_conv, ys=ys, merged=merged, mix=mix, x1=x1, u2=u2, hpre=hpre, ff=ff)
    return x2, saved


def _layer_bwd(l, dx2, mod, W, P, A):
    s = dx2.shape[0]
    sh_m, sc_m, g_m, sh_f, sc_f, g_f = [mod[l:l + 1, D_MODEL * j:D_MODEL * (j + 1)] for j in range(6)]
    n = lambda t: f"{t}{l}"
    tmb = min(1024, s)
    gw, gs = {}, {}

    dres, dff, gs["ln_ff_g"], gs["ln_ff_b"], dg_f, gs["b_ff2"] = _resid_ln_bwd(
        dx2, A["x1"], A["ff"], g_f, P["ln_ff_g"][l], n("resid_ln_ff_bwd"))
    gw["w_ff2"] = _mm(A["hpre"], dff, "tn", tm=256, tn=1024, out_dtype=BF16, name=n("dw_ff2"), a_fn=_relu2)
    dhpre, gs["b_ff1"] = _ff_hidden_bwd(dff, W["w_ff2"][l], A["hpre"], n("ff_hidden_bwd"))
    gw["w_ff1"] = _mm(dhpre, A["u2"], "tn", tm=512, tn=1024, out_dtype=BF16, name=n("dw_ff1"))
    du2 = _mm(dhpre, W["w_ff1"][l], "nn", tm=min(512, s), tn=512, out_dtype=F32, name=n("du_ff"))
    dx1, dsc_f, dsh_f = _ln_mod_bwd(du2, A["x1"], sc_f, dres, n("ln_mod_ff_bwd"))

    dres, dmix, gs["ln_mix_g"], gs["ln_mix_b"], dg_m, _ = _resid_ln_bwd(
        dx1, A["x"], A["mix"], g_m, P["ln_mix_g"][l], n("resid_ln_mix_bwd"))
    gw["w_o"] = _mm(A["merged"], dmix, "tn", tm=512, tn=1024, out_dtype=BF16, name=n("dw_o"))
    dmerged = _mm(dmix, W["w_o"][l], "nt", tm=tmb, tn=1024, out_dtype=F32, name=n("d_merged"))
    dy_pool, dy_attn, dy_conv, dzg, gs["b_gate"] = _merge_bwd(dmerged, A["zg"], P["b_gate"][l], A["ys"], n("merge_bwd"))

    gw["w_br_pool"] = _mm(dy_pool, A["feat_pool"], "tn", tm=512, tn=256, out_dtype=BF16, name=n("dw_br_pool"))
    gw["w_br_attn"] = _mm(dy_attn, A["o"], "tn", tm=512, tn=512, out_dtype=BF16, name=n("dw_br_attn"))
    gw["w_br_conv"] = _mm(dy_conv, A["feat_conv"], "tn", tm=512, tn=256, out_dtype=BF16, name=n("dw_br_conv"))
    dfeat_pool = _mm(dy_pool, W["w_br_pool"][l], "nn", tm=tmb, tn=256, out_dtype=F32, name=n("d_feat_pool"))
    do = _mm(dy_attn, W["w_br_attn"][l], "nn", tm=tmb, tn=512, out_dtype=BF16, name=n("d_attn_out"))
    dfeat_conv = _mm(dy_conv, W["w_br_conv"][l], "nn", tm=tmb, tn=256, out_dtype=F32, name=n("d_feat_conv"))

    dzp, dwp_bd, gs["pool_scale"] = _pool_bwd(dfeat_pool, A["p"], P["wp_bd"][l], P["pool_scale"][l], n("pool_bwd"))
    gs["w_pool"] = jnp.stack([dwp_bd[POOL_GROUP * g:POOL_GROUP * (g + 1), POOL_GROUP * g:POOL_GROUP * (g + 1)]
                              for g in range(len(POOL_WINDOWS))])
    dq, dk, dv, ds_acc = _attn_bwd(A["qkv"], do, A["bias"], n("attn_bwd"))
    gs["rel_bias"] = _bias_block_bwd(ds_acc, n("bias_block_bwd"))
    dzc, dcw, gs["conv_b"], gs["conv_ln_g"], gs["conv_ln_b"] = _conv_bwd(
        dfeat_conv, A["cv"], A["zc"], P["conv_w"][l], P["conv_ln_g"][l], P["conv_ln_b"][l], n("conv_bwd"))
    gs["conv_w"] = dcw[:CONV_WIDTH]

    dz = jnp.concatenate([dzp, dq, dk[KPAD:].astype(BF16), dv[KPAD:].astype(BF16), dzc, dzg], axis=1)
    gw["w_in"] = _mm(dz, A["u"], "tn", tm=768, tn=1024, out_dtype=BF16, name=n("dw_in"))
    du = _mm(dz, W["w_in"][l], "nn", tm=min(512, s), tn=512, out_dtype=F32, name=n("du_mix"))
    dx, dsc_m, dsh_m = _ln_mod_bwd(du, A["x"], sc_m, dres, n("ln_mod_mix_bwd"))
    dmod = jnp.concatenate([dsh_m, dsc_m, dg_m, dsh_f, dsc_f, dg_f], axis=1)
    return dx, gw, gs, dmod


def _small_shapes():
    return {"b_ada": (6 * D_MODEL,), "b_gate": (3 * D_MODEL,), "w_pool": (4, POOL_GROUP, POOL_GROUP),
            "pool_scale": (D_POOL,), "rel_bias": (N_HEADS, N_REL), "conv_w": (CONV_WIDTH, D_CONV),
            "conv_b": (D_CONV,), "conv_ln_g": (D_CONV,), "conv_ln_b": (D_CONV,), "ln_mix_g": (D_MODEL,),
            "ln_mix_b": (D_MODEL,), "b_ff1": (D_FF,), "b_ff2": (D_MODEL,), "ln_ff_g": (D_MODEL,), "ln_ff_b": (D_MODEL,)}


def kernel(x, c, w_ada, b_ada, w_in, b_gate, w_pool, pool_scale, rel_bias, conv_w, conv_b, conv_ln_g, conv_ln_b, w_br_pool, w_br_attn, w_br_conv, w_o, ln_mix_g, ln_mix_b, w_ff1, b_ff1, w_ff2, b_ff2, ln_ff_g, ln_ff_b, loss_target, m_w_ada, m_b_ada, m_w_in, m_b_gate, m_w_pool, m_pool_scale, m_rel_bias, m_conv_w, m_conv_b, m_conv_ln_g, m_conv_ln_b, m_w_br_pool, m_w_br_attn, m_w_br_conv, m_w_o, m_ln_mix_g, m_ln_mix_b, m_w_ff1, m_b_ff1, m_w_ff2, m_b_ff2, m_ln_ff_g, m_ln_ff_b, v_w_ada, v_b_ada, v_w_in, v_b_gate, v_w_pool, v_pool_scale, v_rel_bias, v_conv_w, v_conv_b, v_conv_ln_g, v_conv_ln_b, v_w_br_pool, v_w_br_attn, v_w_br_conv, v_w_o, v_ln_mix_g, v_ln_mix_b, v_w_ff1, v_b_ff1, v_w_ff2, v_b_ff2, v_ln_ff_g, v_ln_ff_b):
    env = dict(locals())
    xi, yi, ci = _me()
    chip = 2 * xi + yi
    me = 4 * xi + 2 * yi + ci
    xs = x[0]
    tgt = loss_target[0]
    L = DEPTH

    c_all = _allgather_small(c.reshape(8, 128), "gather_c").reshape(N_DEV, D_MODEL)
    ada_cols = w_ada.shape[2]
    b_ada_sh = lax.dynamic_slice_in_dim(b_ada, chip * ada_cols, ada_cols, axis=1).reshape(L, 1, ada_cols)
    mod_part = _mod_fwd(c_all, w_ada, b_ada_sh, "mod_fwd")
    mod_g = _allgather_small(mod_part.reshape(-1, 128), "gather_mod").reshape(N_CHIP, 2, L, N_DEV, ada_cols)[:, 0]
    mod_all = jnp.transpose(mod_g, (1, 2, 0, 3)).reshape(L, N_DEV, 6 * D_MODEL)
    mod = lax.dynamic_index_in_dim(mod_all, me, axis=1, keepdims=False)

    W = {k: [None] * L for k in BIG}
    for l in range(L):
        shards = [(jnp.swapaxes(env[k][l], 0, 1) if k in COL_SHARDED else env[k][l]).astype(BF16) for k in BIG]
        for k, g in zip(BIG, _gather_rows(shards, f"gather_weights{l}")):
            W[k][l] = g

    P = {k: env[k] for k in ("rel_bias", "conv_w")}
    for k in ("b_gate", "pool_scale", "conv_b", "conv_ln_g", "conv_ln_b", "ln_mix_g", "ln_mix_b", "b_ff1", "b_ff2",
              "ln_ff_g", "ln_ff_b"):
        P[k] = env[k].reshape(L, 1, -1)
    conv_w_full = _allgather_small(_pack([conv_w]), "gather_conv_w")
    n_cw = conv_w.size
    cw = conv_w_full.reshape(N_CHIP, 2, -1)[:, 0, :n_cw].reshape(N_CHIP, L, CONV_WIDTH, D_CONV // N_CHIP)
    P["conv_w"] = jnp.transpose(cw, (1, 2, 0, 3)).reshape(L, CONV_WIDTH, D_CONV)
    wp_bd = jnp.zeros((L, D_POOL, D_POOL), F32)
    for g in range(len(POOL_WINDOWS)):
        sl = slice(POOL_GROUP * g, POOL_GROUP * (g + 1))
        wp_bd = wp_bd.at[:, sl, sl].set(w_pool[:, g])
    P["wp_bd"] = wp_bd.astype(BF16)

    acts = []
    h = xs
    for l in range(L):
        h, saved = _layer_fwd(l, h, mod, W, P)
        acts.append(saved)
    dy, loss_part = _loss_grad(h, tgt, "loss_grad")
    loss = lax.psum(loss_part[0, 0], ("x", "y", "c"))

    gws, gss, dmods = [None] * L, [None] * L, [None] * L
    dh = dy
    for l in reversed(range(L)):
        dh, gws[l], gss[l], dmods[l] = _layer_bwd(l, dh, mod, W, P, acts[l])
    grad_x = dh[None]

    reduced = [[None] * L for _ in BIG]
    for l in reversed(range(L)):
        dws = [gws[l][k] for k in BIG]
        own, got = _swap_halves([a.reshape(N_CHIP, 2, a.shape[0] // (2 * N_CHIP), a.shape[1]) for a in dws],
                                f"swap_halves{l}")
        both = [_sum2_bf16(a, b, f"sum_cores_{k}{l}") for k, a, b in zip(BIG, own, got)]
        for j, (k, r) in enumerate(zip(BIG, _chip_scatter(both, f"scatter_grads{l}"))):
            reduced[j][l] = _sum4(r, f"sum_chips_{k}{l}")
    joined = _join_halves(reduced, "join_halves")

    out = {}
    for k, g in zip(BIG, joined):
        shp = env[k].shape
        g = jnp.swapaxes(g, 1, 2) if k in COL_SHARDED else g
        flat = lambda a: a.reshape(-1, shp[-1])
        g_, d_, m_, v_ = _adamw(flat(env[k]), flat(env["m_" + k]), flat(env["v_" + k]), [flat(g)], f"adamw_{k}")
        out[k] = tuple(a.reshape(shp) for a in (g_, d_, m_, v_))

    shapes = _small_shapes()
    small_names = [k for k in SMALL if k != "b_ada"]
    dmod_own = jnp.concatenate(dmods, axis=0)
    pack = _pack([dmod_own] + [jnp.stack([gss[l][k].reshape(shapes[k]) for l in range(L)]) for k in small_names])
    g_all = _allgather_small(pack.reshape(-1, 128), "gather_small").reshape(N_DEV, -1, PACK_W)

    dmod_all = g_all[:, :L * 6].reshape(N_DEV, L, 6 * D_MODEL)
    dmod_sh = jnp.transpose(lax.dynamic_slice_in_dim(dmod_all, chip * ada_cols, ada_cols, axis=2), (1, 0, 2))
    g_ada = _mod_bwd(c_all, dmod_sh, "mod_bwd")
    g_, d_, m_, v_ = _adamw(w_ada.reshape(-1, ada_cols), m_w_ada.reshape(-1, ada_cols), v_w_ada.reshape(-1, ada_cols),
                            [g_ada.reshape(-1, ada_cols)], "adamw_w_ada")
    out["w_ada"] = tuple(a.reshape(w_ada.shape) for a in (g_, d_, m_, v_))

    def small_pack(prefix):
        parts = [env[prefix + "b_ada"]]
        for k in small_names:
            a = env[prefix + k]
            if k == "conv_w":
                a = jnp.zeros((L,) + shapes[k], F32)
            parts.append(a)
        return _pack(parts)

    gp, dp, mp, vp = _adamw_small(small_pack(""), small_pack("m_"), small_pack("v_"), g_all, "adamw_small")
    full_shapes = [(L,) + shapes["b_ada"]] + [(L,) + shapes[k] for k in small_names]
    for tag, packed in (("g", gp), ("d", dp), ("m", mp), ("v", vp)):
        for k, a in zip(["b_ada"] + small_names, _unpack(packed, full_shapes)):
            out.setdefault(k, {})
            out[k][tag] = a
    g_cw_full = out["conv_w"]["g"]
    cw_cols = D_CONV // N_CHIP
    g_cw = lax.dynamic_slice_in_dim(g_cw_full, chip * cw_cols, cw_cols, axis=2)
    pad_rows = lambda a: jnp.pad(a.reshape(L * CONV_WIDTH, cw_cols), ((0, 2), (0, 0)))
    g_, d_, m_, v_ = _adamw(pad_rows(conv_w), pad_rows(m_conv_w), pad_rows(v_conv_w), [pad_rows(g_cw)], "adamw_conv_w")
    out["conv_w"] = tuple(a[:L * CONV_WIDTH].reshape(L, CONV_WIDTH, cw_cols) for a in (g_, d_, m_, v_))

    names = ["w_ada", "b_ada", "w_in", "b_gate", "w_pool", "pool_scale", "rel_bias", "conv_w", "conv_b", "conv_ln_g",
             "conv_ln_b", "w_br_pool", "w_br_attn", "w_br_conv", "w_o", "ln_mix_g", "ln_mix_b", "w_ff1", "b_ff1",
             "w_ff2", "b_ff2", "ln_ff_g", "ln_ff_b"]

    def pick(k, i):
        o = out[k]
        return o[i] if isinstance(o, tuple) else o["gdmv"[i]].reshape(env[k].shape)

    return (loss, grad_x, *[pick(k, 0) for k in names], *[pick(k, 1) for k in names],
            *[pick(k, 2) for k in names], *[pick(k, 3) for k in names])
```

```python
import functools

import jax
import jax.numpy as jnp
import numpy as np
from jax import lax
from jax.experimental import pallas as pl
from jax.experimental.pallas import tpu as pltpu

F32 = jnp.float32
BF16 = jnp.bfloat16

D_MODEL = 1024
DEPTH = 2
CHUNK = 64
POOL_WINDOWS = (2, 4, 8, 16)
POOL_GROUP = 64
D_POOL = 256
N_HEADS = 8
HEAD_DIM = 64
D_ATTN = 512
N_PREV_CHUNKS = 8
REL_CLIP = 128
N_REL = 2 * REL_CLIP + 1
D_CONV = 256
CONV_WIDTH = 31
D_FF = 4 * D_MODEL
D_IN = 5376
OFF_POOL, OFF_QKV, OFF_CONV, OFF_GATE = 0, 256, 1792, 2304
ALPHA = (2.0 * DEPTH) ** 0.25
LN_EPS = 1e-5
NEG_INF = -1e30
ADAM_LR, ADAM_B1, ADAM_B2, ADAM_EPS, ADAM_WD, ADAM_STEP = 0.001, 0.9, 0.999, 1e-08, 0.01, 10

N_DEV = 8
N_CHIP = 4
MESH = pl.DeviceIdType.MESH

QB = 2 * CHUNK
KPAD = N_PREV_CHUNKS * CHUNK
KW = QB + KPAD
SKEW_W = 768

VMEM_LIMIT = 56 * 1024 * 1024


def _cparams(**kw):
    return pltpu.CompilerParams(vmem_limit_bytes=VMEM_LIMIT, **kw)


def _full(shape):
    n = len(shape)
    return pl.BlockSpec(shape, lambda *_: (0,) * n)


_DIMS = {"nn": (((1,), (0,)), ((), ())), "nt": (((1,), (1,)), ((), ())), "tn": (((0,), (0,)), ((), ()))}


def _relu2(t):
    r = jnp.maximum(t, 0.0)
    return r * r


def _mm(a, b, mode, *, tm, tn, out_dtype, name, b_col0=0, n_out=None, a_fn=None, bias=None, split_n=False):
    if mode == "tn":
        k, m = a.shape
        n = b.shape[1] if n_out is None else n_out
        a_spec = pl.BlockSpec((k, tm), lambda i, j: (0, i))
        b_spec = pl.BlockSpec((k, tn), lambda i, j: (0, j + b_col0))
    elif mode == "nn":
        m, k = a.shape
        n = b.shape[1] if n_out is None else n_out
        a_spec = pl.BlockSpec((tm, k), lambda i, j: (i, 0))
        b_spec = pl.BlockSpec((k, tn), lambda i, j: (0, j + b_col0))
    else:
        m, k = a.shape
        n = b.shape[0] if n_out is None else n_out
        a_spec = pl.BlockSpec((tm, k), lambda i, j: (i, 0))
        b_spec = pl.BlockSpec((tn, k), lambda i, j: (j + b_col0, 0))
    assert m % tm == 0 and n % tn == 0, (name, m, n, tm, tn)
    dims = _DIMS[mode]

    def body(*refs):
        if bias is None:
            a_ref, b_ref, o_ref = refs
        else:
            a_ref, b_ref, bias_ref, o_ref = refs
        av = a_ref[...]
        if a_fn is not None:
            av = a_fn(av)
        acc = lax.dot_general(av.astype(BF16), b_ref[...].astype(BF16), dims, preferred_element_type=F32)
        if bias is not None:
            acc = acc + bias_ref[...]
        o_ref[...] = acc.astype(out_dtype)

    in_specs = [a_spec, b_spec]
    args = [a, b]
    if bias is not None:
        in_specs.append(pl.BlockSpec((1, tn), lambda i, j: (0, j)))
        args.append(bias)
    if split_n:
        out_spec = pl.BlockSpec((None, tm, tn), lambda i, j: (j, i, 0))
        out_shape = jax.ShapeDtypeStruct((n // tn, m, tn), out_dtype)
    else:
        out_spec = pl.BlockSpec((tm, tn), lambda i, j: (i, j))
        out_shape = jax.ShapeDtypeStruct((m, n), out_dtype)
    return pl.pallas_call(body, grid=(m // tm, n // tn), in_specs=in_specs, out_specs=out_spec,
                          out_shape=out_shape, name=name, compiler_params=_cparams())(*args)


def _ln_hat(x):
    mu = jnp.mean(x, axis=-1, keepdims=True)
    xc = x - mu
    var = jnp.mean(xc * xc, axis=-1, keepdims=True)
    rstd = lax.rsqrt(var + LN_EPS)
    return xc * rstd, rstd


def _ln_hat_bwd(dhat, xhat, rstd):
    m1 = jnp.mean(dhat, axis=-1, keepdims=True)
    m2 = jnp.mean(dhat * xhat, axis=-1, keepdims=True)
    return rstd * (dhat - m1 - xhat * m2)


def _row_tile(s):
    return min(512, s)


def _acc_rows(ref, val, first):
    @pl.when(first)
    def _():
        ref[...] = jnp.zeros_like(ref)
    ref[...] += jnp.sum(val, axis=0, keepdims=True)


def _ln_mod(x, sc, sh, name):
    s, d = x.shape
    tm = _row_tile(s)

    def body(x_ref, sc_ref, sh_ref, u_ref):
        xhat, _ = _ln_hat(x_ref[...])
        u_ref[...] = (xhat * (1.0 + sc_ref[...]) + sh_ref[...]).astype(BF16)

    row = pl.BlockSpec((tm, d), lambda i: (i, 0))
    vec = pl.BlockSpec((1, d), lambda i: (0, 0))
    return pl.pallas_call(body, grid=(s // tm,), in_specs=[row, vec, vec], out_specs=row,
                          out_shape=jax.ShapeDtypeStruct((s, d), BF16), name=name, compiler_params=_cparams())(x, sc, sh)


def _ln_mod_bwd(du, x, sc, dres, name):
    s, d = x.shape
    tm = _row_tile(s)

    def body(du_ref, x_ref, sc_ref, dres_ref, dx_ref, dsc_ref, dsh_ref):
        first = pl.program_id(0) == 0
        duv = du_ref[...]
        xhat, rstd = _ln_hat(x_ref[...])
        dx_ref[...] = dres_ref[...] + _ln_hat_bwd(duv * (1.0 + sc_ref[...]), xhat, rstd)
        _acc_rows(dsc_ref, duv * xhat, first)
        _acc_rows(dsh_ref, duv, first)

    row = pl.BlockSpec((tm, d), lambda i: (i, 0))
    vec = pl.BlockSpec((1, d), lambda i: (0, 0))
    vs = jax.ShapeDtypeStruct((1, d), F32)
    return pl.pallas_call(body, grid=(s // tm,), in_specs=[row, row, vec, row], out_specs=[row, vec, vec],
                          out_shape=[jax.ShapeDtypeStruct((s, d), F32), vs, vs], name=name,
                          compiler_params=_cparams())(du, x, sc, dres)


def _resid_ln(x, f, g, gam, bet, name):
    s, d = x.shape
    tm = _row_tile(s)

    def body(x_ref, f_ref, g_ref, gam_ref, bet_ref, o_ref):
        rhat, _ = _ln_hat(ALPHA * x_ref[...] + g_ref[...] * f_ref[...])
        o_ref[...] = rhat * gam_ref[...] + bet_ref[...]

    row = pl.BlockSpec((tm, d), lambda i: (i, 0))
    vec = pl.BlockSpec((1, d), lambda i: (0, 0))
    return pl.pallas_call(body, grid=(s // tm,), in_specs=[row, row, vec, vec, vec], out_specs=row,
                          out_shape=jax.ShapeDtypeStruct((s, d), F32), name=name, compiler_params=_cparams())(x, f, g, gam, bet)


def _resid_ln_bwd(dxo, x, f, g, gam, name):
    s, d = x.shape
    tm = _row_tile(s)

    def body(dxo_ref, x_ref, f_ref, g_ref, gam_ref, dres_ref, df_ref, dgam_ref, dbet_ref, dg_ref, dbias_ref):
        first = pl.program_id(0) == 0
        dxov = dxo_ref[...]
        fv = f_ref[...]
        rhat, rstd = _ln_hat(ALPHA * x_ref[...] + g_ref[...] * fv)
        dr = _ln_hat_bwd(dxov * gam_ref[...], rhat, rstd)
        dfv = g_ref[...] * dr
        dres_ref[...] = ALPHA * dr
        df_ref[...] = dfv.astype(BF16)
        _acc_rows(dgam_ref, dxov * rhat, first)
        _acc_rows(dbet_ref, dxov, first)
        _acc_rows(dg_ref, dr * fv, first)
        _acc_rows(dbias_ref, dfv, first)

    row = pl.BlockSpec((tm, d), lambda i: (i, 0))
    vec = pl.BlockSpec((1, d), lambda i: (0, 0))
    vs = jax.ShapeDtypeStruct((1, d), F32)
    return pl.pallas_call(body, grid=(s // tm,), in_specs=[row, row, row, vec, vec],
                          out_specs=[row, row, vec, vec, vec, vec],
                          out_shape=[jax.ShapeDtypeStruct((s, d), F32), jax.ShapeDtypeStruct((s, d), BF16), vs, vs, vs, vs],
                          name=name, compiler_params=_cparams())(dxo, x, f, g, gam)


def _loss_grad(y, tgt, name):
    s, d = y.shape
    tm = _row_tile(s)
    n = s // tm

    def body(y_ref, t_ref, dy_ref, loss_ref, acc_ref):
        i = pl.program_id(0)
        e = y_ref[...] - t_ref[...]
        dy_ref[...] = e * (1.0 / d)
        _acc_rows(acc_ref, e * e, i == 0)

        @pl.when(i == n - 1)
        def _():
            tot = jnp.sum(acc_ref[...], axis=1, keepdims=True) * (0.5 / d)
            loss_ref[...] = jnp.broadcast_to(tot, (1, 128))

    row = pl.BlockSpec((tm, d), lambda i: (i, 0))
    return pl.pallas_call(body, grid=(n,), in_specs=[row, row],
                          out_specs=[row, pl.BlockSpec((1, 128), lambda i: (0, 0))],
                          out_shape=[jax.ShapeDtypeStruct((s, d), F32), jax.ShapeDtypeStruct((1, 128), F32)],
                          scratch_shapes=[pltpu.VMEM((1, d), F32)], name=name, compiler_params=_cparams())(y, tgt)


POOL_HALO = 16
POOL_ROWS = 256


def _pool_counts(r0, rows):
    t1 = (lax.broadcasted_iota(jnp.int32, (rows, 128), 0) + r0 + 1).astype(F32)
    low = lax.broadcasted_iota(jnp.int32, (rows, 128), 1) < POOL_GROUP
    wa = jnp.where(low, float(POOL_WINDOWS[0]), float(POOL_WINDOWS[1]))
    wb = jnp.where(low, float(POOL_WINDOWS[2]), float(POOL_WINDOWS[3]))
    return jnp.minimum(t1, wa), jnp.minimum(t1, wb), low


def _window_sums(win, off, rows, sign):
    def sl(j, half):
        return win[off + sign * j: off + sign * j + rows, 128 * half:128 * half + 128]
    a2 = sl(0, 0) + sl(1, 0)
    a4 = a2 + sl(2, 0) + sl(3, 0)
    a8 = sl(0, 1)
    for j in range(1, 8):
        a8 = a8 + sl(j, 1)
    a16 = a8
    for j in range(8, 16):
        a16 = a16 + sl(j, 1)
    return a2, a4, a8, a16


def _pool_fwd(zp, wp_bd, pscale, name):
    s = zp.shape[0]
    r = min(POOL_ROWS, s)

    def body(z_ref, wp_ref, sc_ref, p_ref, feat_ref, pad):
        pad[0:POOL_HALO, :] = jnp.zeros((POOL_HALO, D_POOL), F32)
        pad[POOL_HALO:, :] = z_ref[...]

        def step(i, carry):
            r0 = pl.multiple_of(i * r, r)
            win = pad[pl.ds(r0, r + POOL_HALO), :]
            a2, a4, a8, a16 = _window_sums(win, POOL_HALO, r, -1)
            ca, cb, low = _pool_counts(r0, r)
            x0 = win[POOL_HALO:, :]
            pa = jnp.where(low, a2, a4) / ca
            pb = jnp.where(low, a8, a16) / cb
            p = (jnp.concatenate([pa, pb], axis=1) - x0).astype(BF16)
            p_ref[pl.ds(r0, r), :] = p
            pw = jnp.dot(p, wp_ref[...], preferred_element_type=F32)
            feat_ref[pl.ds(r0, r), :] = (pw * sc_ref[...]).astype(BF16)
            return carry

        lax.fori_loop(0, s // r, step, 0)

    return pl.pallas_call(
        body, out_shape=[jax.ShapeDtypeStruct((s, D_POOL), BF16), jax.ShapeDtypeStruct((s, D_POOL), BF16)],
        scratch_shapes=[pltpu.VMEM((s + POOL_HALO, D_POOL), F32)], name=name, compiler_params=_cparams(),
    )(zp, wp_bd, pscale)


def _pool_bwd(dfeat, p, wp_bd, pscale, name):
    s = p.shape[0]
    r = min(POOL_ROWS, s)

    def body(df_ref, p_ref, wp_ref, sc_ref, dz_ref, dwp_ref, dsc_ref, gpad, dpbuf):
        dwp_ref[...] = jnp.zeros_like(dwp_ref)
        dsc_ref[...] = jnp.zeros_like(dsc_ref)
        gpad[s:, :] = jnp.zeros((POOL_HALO, D_POOL), F32)

        def step1(i, carry):
            r0 = pl.multiple_of(i * r, r)
            pv = p_ref[pl.ds(r0, r), :]
            dfv = df_ref[pl.ds(r0, r), :]
            pw = jnp.dot(pv, wp_ref[...], preferred_element_type=F32)
            dsc_ref[...] += jnp.sum(dfv * pw, axis=0, keepdims=True)
            dpw = (dfv * sc_ref[...]).astype(BF16)
            dwp_ref[...] += lax.dot_general(pv, dpw, _DIMS["tn"], preferred_element_type=F32)
            dp = lax.dot_general(dpw, wp_ref[...], _DIMS["nt"], preferred_element_type=F32)
            ca, cb, _ = _pool_counts(r0, r)
            gpad[pl.ds(r0, r), :] = dp / jnp.concatenate([ca, cb], axis=1)
            dpbuf[pl.ds(r0, r), :] = dp
            return carry

        lax.fori_loop(0, s // r, step1, 0)

        def step2(i, carry):
            r0 = pl.multiple_of(i * r, r)
            win = gpad[pl.ds(r0, r + POOL_HALO), :]
            a2, a4, a8, a16 = _window_sums(win, 0, r, 1)
            low = lax.broadcasted_iota(jnp.int32, (r, 128), 1) < POOL_GROUP
            acc = jnp.concatenate([jnp.where(low, a2, a4), jnp.where(low, a8, a16)], axis=1)
            dz_ref[pl.ds(r0, r), :] = (acc - dpbuf[pl.ds(r0, r), :]).astype(BF16)
            return carry

        lax.fori_loop(0, s // r, step2, 0)

    return pl.pallas_call(
        body,
        out_shape=[jax.ShapeDtypeStruct((s, D_POOL), BF16), jax.ShapeDtypeStruct((D_POOL, D_POOL), F32),
                   jax.ShapeDtypeStruct((1, D_POOL), F32)],
        scratch_shapes=[pltpu.VMEM((s + POOL_HALO, D_POOL), F32), pltpu.VMEM((s, D_POOL), F32)],
        name=name, compiler_params=_cparams(),
    )(dfeat, p, wp_bd, pscale)


def _skew_index():
    cp = lax.broadcasted_iota(jnp.int32, (SKEW_W, N_REL), 0)
    dist = jnp.where(cp < KW, KPAD - cp, KPAD + SKEW_W - cp)
    idx = jnp.clip(dist, -REL_CLIP, REL_CLIP) + REL_CLIP
    return (idx == lax.broadcasted_iota(jnp.int32, (SKEW_W, N_REL), 1)).astype(F32)


def _row_bits(b):
    return (lax.broadcasted_iota(jnp.int32, (QB, SKEW_W), 0) >> b) & 1 == 1


def _bias_block(rel_bias, name):
    def body(rb_ref, o_ref):
        onehot = _skew_index()
        row0 = lax.dot_general(rb_ref[...], onehot, _DIMS["nt"], precision=lax.Precision.HIGHEST,
                               preferred_element_type=F32)
        r = lax.broadcasted_iota(jnp.int32, (QB, KW), 0)
        kk = lax.broadcasted_iota(jnp.int32, (QB, KW), 1)
        cq, ck = r // CHUNK, kk // CHUNK
        band = (ck >= cq) & (ck <= cq + N_PREV_CHUNKS)
        for h in range(N_HEADS):
            t = jnp.broadcast_to(row0[h:h + 1, :], (QB, SKEW_W))
            for b in range(7):
                t = jnp.where(_row_bits(b), pltpu.roll(t, 1 << b, 1), t)
            o_ref[h] = jnp.where(band, t[:, :KW], NEG_INF)

    return pl.pallas_call(body, out_shape=jax.ShapeDtypeStruct((N_HEADS, QB, KW), F32), name=name,
                          compiler_params=_cparams())(rel_bias)


def _bias_block_bwd(ds_acc, name):
    def body(ds_ref, o_ref):
        sums = []
        for h in range(N_HEADS):
            t = jnp.concatenate([ds_ref[h], jnp.zeros((QB, SKEW_W - KW), F32)], axis=1)
            for b in range(7):
                t = jnp.where(_row_bits(b), pltpu.roll(t, SKEW_W - (1 << b), 1), t)
            sums.append(jnp.sum(t, axis=0, keepdims=True))
        allh = jnp.concatenate(sums, axis=0)
        o_ref[...] = jnp.dot(allh, _skew_index(), precision=lax.Precision.HIGHEST, preferred_element_type=F32)

    return pl.pallas_call(body, out_shape=jax.ShapeDtypeStruct((N_HEADS, N_REL), F32), name=name,
                          compiler_params=_cparams())(ds_acc)


def _scores(qh, kh, bias_h, valid):
    sc = lax.dot_general(qh, kh, _DIMS["nt"], preferred_element_type=F32) * (HEAD_DIM ** -0.5) + bias_h
    sc = jnp.where(valid, sc, NEG_INF)
    e = jnp.exp(sc - jnp.max(sc, axis=-1, keepdims=True))
    return e * (1.0 / jnp.sum(e, axis=-1, keepdims=True))


def _load_padded_kv(qkv_hbm, kpad, vpad, sems, s):
    kpad[0:KPAD, :] = jnp.zeros((KPAD, D_ATTN), BF16)
    vpad[0:KPAD, :] = jnp.zeros((KPAD, D_ATTN), BF16)
    ck = pltpu.make_async_copy(qkv_hbm.at[:, D_ATTN:2 * D_ATTN], kpad.at[pl.ds(KPAD, s), :], sems.at[0])
    cv = pltpu.make_async_copy(qkv_hbm.at[:, 2 * D_ATTN:3 * D_ATTN], vpad.at[pl.ds(KPAD, s), :], sems.at[1])
    ck.start()
    cv.start()
    ck.wait()
    cv.wait()


def _attn_fwd(qkv, bias, name):
    s = qkv.shape[0]

    def body(q_ref, qkv_hbm, bias_ref, o_ref, kpad, vpad, sems):
        i = pl.program_id(0)

        @pl.when(i == 0)
        def _():
            _load_padded_kv(qkv_hbm, kpad, vpad, sems, s)

        base = pl.multiple_of(i * QB, QB)
        kw = kpad[pl.ds(base, KW), :]
        vw = vpad[pl.ds(base, KW), :]
        q = q_ref[...]
        valid = lax.broadcasted_iota(jnp.int32, (QB, KW), 1) >= KPAD - base
        outs = []
        for h in range(N_HEADS):
            hs = slice(HEAD_DIM * h, HEAD_DIM * (h + 1))
            p = _scores(q[:, hs], kw[:, hs], bias_ref[h], valid)
            outs.append(jnp.dot(p.astype(BF16), vw[:, hs], preferred_element_type=F32))
        o_ref[...] = jnp.concatenate(outs, axis=1).astype(BF16)

    return pl.pallas_call(
        body, grid=(s // QB,),
        in_specs=[pl.BlockSpec((QB, D_ATTN), lambda i: (i, 0)), pl.BlockSpec(memory_space=pl.ANY),
                  _full((N_HEADS, QB, KW))],
        out_specs=pl.BlockSpec((QB, D_ATTN), lambda i: (i, 0)),
        out_shape=jax.ShapeDtypeStruct((s, D_ATTN), BF16),
        scratch_shapes=[pltpu.VMEM((s + KPAD, D_ATTN), BF16), pltpu.VMEM((s + KPAD, D_ATTN), BF16),
                        pltpu.SemaphoreType.DMA((2,))],
        name=name, compiler_params=_cparams(),
    )(qkv, qkv, bias)


def _attn_bwd(qkv, do, bias, name):
    s = qkv.shape[0]
    n = s // QB

    def body(q_ref, qkv_hbm, do_ref, bias_ref, dq_ref, dk_hbm, dv_hbm, ds_ref, kpad, vpad, dkacc, dvacc, sems):
        i = pl.program_id(0)

        @pl.when(i == 0)
        def _():
            _load_padded_kv(qkv_hbm, kpad, vpad, sems, s)
            dkacc[...] = jnp.zeros_like(dkacc)
            dvacc[...] = jnp.zeros_like(dvacc)
            ds_ref[...] = jnp.zeros_like(ds_ref)

        base = pl.multiple_of(i * QB, QB)
        kw = kpad[pl.ds(base, KW), :]
        vw = vpad[pl.ds(base, KW), :]
        q = q_ref[...]
        dov = do_ref[...]
        valid = lax.broadcasted_iota(jnp.int32, (QB, KW), 1) >= KPAD - base
        dqs, dks, dvs = [], [], []
        for h in range(N_HEADS):
            hs = slice(HEAD_DIM * h, HEAD_DIM * (h + 1))
            qh, kh, vh, doh = q[:, hs], kw[:, hs], vw[:, hs], dov[:, hs]
            p = _scores(qh, kh, bias_ref[h], valid)
            dvs.append(lax.dot_general(p.astype(BF16), doh, _DIMS["tn"], preferred_element_type=F32))
            dp = lax.dot_general(doh, vh, _DIMS["nt"], preferred_element_type=F32)
            ds = p * (dp - jnp.sum(dp * p, axis=-1, keepdims=True))
            ds_ref[h] += ds
            dsb = ds.astype(BF16)
            dqs.append(jnp.dot(dsb, kh, preferred_element_type=F32) * (HEAD_DIM ** -0.5))
            dks.append(lax.dot_general(dsb, qh, _DIMS["tn"], preferred_element_type=F32) * (HEAD_DIM ** -0.5))
        dq_ref[...] = jnp.concatenate(dqs, axis=1).astype(BF16)
        dkacc[pl.ds(base, KW), :] += jnp.concatenate(dks, axis=1)
        dvacc[pl.ds(base, KW), :] += jnp.concatenate(dvs, axis=1)

        @pl.when(i == n - 1)
        def _():
            ck = pltpu.make_async_copy(dkacc, dk_hbm, sems.at[0])
            cv = pltpu.make_async_copy(dvacc, dv_hbm, sems.at[1])
            ck.start()
            cv.start()
            ck.wait()
            cv.wait()

    blk = pl.BlockSpec((QB, D_ATTN), lambda i: (i, 0))
    acc_shape = jax.ShapeDtypeStruct((s + KPAD, D_ATTN), F32)
    return pl.pallas_call(
        body, grid=(n,),
        in_specs=[blk, pl.BlockSpec(memory_space=pl.ANY), blk, _full((N_HEADS, QB, KW))],
        out_specs=[blk, pl.BlockSpec(memory_space=pl.ANY), pl.BlockSpec(memory_space=pl.ANY), _full((N_HEADS, QB, KW))],
        out_shape=[jax.ShapeDtypeStruct((s, D_ATTN), BF16), acc_shape, acc_shape,
                   jax.ShapeDtypeStruct((N_HEADS, QB, KW), F32)],
        scratch_shapes=[pltpu.VMEM((s + KPAD, D_ATTN), BF16), pltpu.VMEM((s + KPAD, D_ATTN), BF16),
                        pltpu.VMEM((s + KPAD, D_ATTN), F32), pltpu.VMEM((s + KPAD, D_ATTN), F32),
                        pltpu.SemaphoreType.DMA((2,))],
        name=name, compiler_params=_cparams(),
    )(qkv, qkv, do, bias)


CONV_HALO = 32
CONV_ROWS = 64


def _sigmoid(t):
    return 1.0 / (1.0 + jnp.exp(-t))


def _glu_rows(z_ref, r0, rows):
    a = z_ref[pl.ds(r0, rows), 0:D_CONV]
    b = z_ref[pl.ds(r0, rows), D_CONV:2 * D_CONV]
    return a, _sigmoid(b)


def _conv_fwd(zc, conv_w, conv_b, ln_g, ln_b, name):
    s = zc.shape[0]
    rt = min(256, s)

    def body(z_ref, w_ref, cb_ref, g_ref, b_ref, cv_ref, feat_ref, hpad):
        hpad[0:CONV_HALO, :] = jnp.zeros((CONV_HALO, D_CONV), F32)

        def glu(i, carry):
            r0 = pl.multiple_of(i * rt, rt)
            a, sb = _glu_rows(z_ref, r0, rt)
            hpad[pl.ds(r0 + CONV_HALO, rt), :] = a * sb
            return carry

        lax.fori_loop(0, s // rt, glu, 0)
        w = w_ref[...]

        def conv(i, carry):
            r0 = pl.multiple_of(i * CONV_ROWS, CONV_ROWS)
            win = hpad[pl.ds(r0, CONV_ROWS + CONV_HALO), :]
            acc = jnp.broadcast_to(cb_ref[...], (CONV_ROWS, D_CONV))
            for k in range(CONV_WIDTH):
                acc = acc + win[2 + k:2 + k + CONV_ROWS, :] * w[k:k + 1, :]
            cv_ref[pl.ds(r0, CONV_ROWS), :] = acc
            yhat, _ = _ln_hat(acc)
            y = yhat * g_ref[...] + b_ref[...]
            feat_ref[pl.ds(r0, CONV_ROWS), :] = (y * _sigmoid(y)).astype(BF16)
            return carry

        lax.fori_loop(0, s // CONV_ROWS, conv, 0)

    return pl.pallas_call(
        body, out_shape=[jax.ShapeDtypeStruct((s, D_CONV), F32), jax.ShapeDtypeStruct((s, D_CONV), BF16)],
        scratch_shapes=[pltpu.VMEM((s + CONV_HALO, D_CONV), F32)], name=name, compiler_params=_cparams(),
    )(zc, conv_w, conv_b, ln_g, ln_b)


def _conv_bwd(dfeat, cv, zc, conv_w, ln_g, ln_b, name):
    s = zc.shape[0]
    rt = min(256, s)

    def body(df_ref, cv_ref, z_ref, w_ref, g_ref, b_ref, dz_ref, dw_ref, dcb_ref, dg_ref, db_ref, hpad, dcvpad, dwacc):
        hpad[0:CONV_HALO, :] = jnp.zeros((CONV_HALO, D_CONV), F32)
        dcvpad[s:, :] = jnp.zeros((CONV_HALO, D_CONV), F32)
        dwacc[...] = jnp.zeros_like(dwacc)
        dcb_ref[...] = jnp.zeros_like(dcb_ref)
        dg_ref[...] = jnp.zeros_like(dg_ref)
        db_ref[...] = jnp.zeros_like(db_ref)

        def pass1(i, carry):
            r0 = pl.multiple_of(i * rt, rt)
            a, sb = _glu_rows(z_ref, r0, rt)
            hpad[pl.ds(r0 + CONV_HALO, rt), :] = a * sb
            cvhat, rstd = _ln_hat(cv_ref[pl.ds(r0, rt), :])
            y = cvhat * g_ref[...] + b_ref[...]
            sg = _sigmoid(y)
            dy = df_ref[pl.ds(r0, rt), :] * (sg * (1.0 + y * (1.0 - sg)))
            dg_ref[...] += jnp.sum(dy * cvhat, axis=0, keepdims=True)
            db_ref[...] += jnp.sum(dy, axis=0, keepdims=True)
            dcv = _ln_hat_bwd(dy * g_ref[...], cvhat, rstd)
            dcb_ref[...] += jnp.sum(dcv, axis=0, keepdims=True)
            dcvpad[pl.ds(r0, rt), :] = dcv
            return carry

        lax.fori_loop(0, s // rt, pass1, 0)
        w = w_ref[...]

        def pass2(i, carry):
            r0 = pl.multiple_of(i * CONV_ROWS, CONV_ROWS)
            dwin = dcvpad[pl.ds(r0, CONV_ROWS + CONV_HALO), :]
            hwin = hpad[pl.ds(r0, CONV_ROWS + CONV_HALO), :]
            dcv = dwin[0:CONV_ROWS, :]
            dh = jnp.zeros((CONV_ROWS, D_CONV), F32)
            for k in range(CONV_WIDTH):
                dh = dh + dwin[30 - k:30 - k + CONV_ROWS, :] * w[k:k + 1, :]
                prod = dcv * hwin[2 + k:2 + k + CONV_ROWS, :]
                dwacc[8 * k:8 * k + 8, :] += jnp.sum(prod.reshape(CONV_ROWS // 8, 8, D_CONV), axis=0)
            a, sb = _glu_rows(z_ref, r0, CONV_ROWS)
            dz_ref[pl.ds(r0, CONV_ROWS), :] = jnp.concatenate([dh * sb, dh * a * sb * (1.0 - sb)], axis=1).astype(BF16)
            return carry

        lax.fori_loop(0, s // CONV_ROWS, pass2, 0)
        dw_ref[...] = jnp.sum(dwacc[...].reshape(32, 8, D_CONV), axis=1)

    vs = jax.ShapeDtypeStruct((1, D_CONV), F32)
    return pl.pallas_call(
        body,
        out_shape=[jax.ShapeDtypeStruct((s, 2 * D_CONV), BF16), jax.ShapeDtypeStruct((32, D_CONV), F32), vs, vs, vs],
        scratch_shapes=[pltpu.VMEM((s + CONV_HALO, D_CONV), F32), pltpu.VMEM((s + CONV_HALO, D_CONV), F32),
                        pltpu.VMEM((256, D_CONV), F32)],
        name=name, compiler_params=_cparams(),
    )(dfeat, cv, zc, conv_w, ln_g, ln_b)


def _merge(zg, b_gate, ys, name):
    s = zg.shape[0]
    tm = _row_tile(s)

    def body(zg_ref, bg_ref, y0_ref, y1_ref, y2_ref, o_ref):
        acc = None
        for j, y_ref in enumerate((y0_ref, y1_ref, y2_ref)):
            cs = slice(D_MODEL * j, D_MODEL * (j + 1))
            t = _sigmoid(zg_ref[:, cs] + bg_ref[:, cs]) * y_ref[...]
            acc = t if acc is None else acc + t
        o_ref[...] = acc.astype(BF16)

    row = pl.BlockSpec((tm, D_MODEL), lambda i: (i, 0))
    return pl.pallas_call(
        body, grid=(s // tm,),
        in_specs=[pl.BlockSpec((tm, 3 * D_MODEL), lambda i: (i, 0)), _full((1, 3 * D_MODEL)), row, row, row],
        out_specs=row, out_shape=jax.ShapeDtypeStruct((s, D_MODEL), BF16), name=name, compiler_params=_cparams(),
    )(zg, b_gate, *ys)


def _merge_bwd(dm, zg, b_gate, ys, name):
    s = zg.shape[0]
    tm = min(256, s)

    def body(dm_ref, zg_ref, bg_ref, y0_ref, y1_ref, y2_ref, d0_ref, d1_ref, d2_ref, dzg_ref, dbg_ref):
        first = pl.program_id(0) == 0

        @pl.when(first)
        def _():
            dbg_ref[...] = jnp.zeros_like(dbg_ref)

        dmv = dm_ref[...]
        for j, (y_ref, d_ref) in enumerate(((y0_ref, d0_ref), (y1_ref, d1_ref), (y2_ref, d2_ref))):
            cs = slice(D_MODEL * j, D_MODEL * (j + 1))
            g = _sigmoid(zg_ref[:, cs] + bg_ref[:, cs])
            d_ref[...] = (dmv * g).astype(BF16)
            dzg = dmv * y_ref[...] * g * (1.0 - g)
            dzg_ref[:, cs] = dzg.astype(BF16)
            dbg_ref[:, cs] += jnp.sum(dzg, axis=0, keepdims=True)

    row = pl.BlockSpec((tm, D_MODEL), lambda i: (i, 0))
    wide = pl.BlockSpec((tm, 3 * D_MODEL), lambda i: (i, 0))
    yb = jax.ShapeDtypeStruct((s, D_MODEL), BF16)
    return pl.pallas_call(
        body, grid=(s // tm,),
        in_specs=[row, wide, _full((1, 3 * D_MODEL)), row, row, row],
        out_specs=[row, row, row, wide, _full((1, 3 * D_MODEL))],
        out_shape=[yb, yb, yb, jax.ShapeDtypeStruct((s, 3 * D_MODEL), BF16), jax.ShapeDtypeStruct((1, 3 * D_MODEL), F32)],
        name=name, compiler_params=_cparams(),
    )(dm, zg, b_gate, *ys)


def _ff_hidden_bwd(dff, w_ff2, hpre, name):
    s = dff.shape[0]
    tm, tn = min(512, s), 1024

    def body(a_ref, b_ref, h_ref, o_ref, sum_ref):
        dh = lax.dot_general(a_ref[...], b_ref[...], _DIMS["nt"], preferred_element_type=F32)
        dpre = dh * (2.0 * jnp.maximum(h_ref[...], 0.0))
        o_ref[...] = dpre.astype(BF16)
        _acc_rows(sum_ref, dpre, pl.program_id(1) == 0)

    return pl.pallas_call(
        body, grid=(D_FF // tn, s // tm),
        in_specs=[pl.BlockSpec((tm, D_MODEL), lambda j, i: (i, 0)), pl.BlockSpec((tn, D_MODEL), lambda j, i: (j, 0)),
                  pl.BlockSpec((tm, tn), lambda j, i: (i, j))],
        out_specs=[pl.BlockSpec((tm, tn), lambda j, i: (i, j)), pl.BlockSpec((1, tn), lambda j, i: (0, j))],
        out_shape=[jax.ShapeDtypeStruct((s, D_FF), BF16), jax.ShapeDtypeStruct((1, D_FF), F32)],
        name=name, compiler_params=_cparams(),
    )(dff, w_ff2, hpre)


def _silu(t):
    return t * _sigmoid(t)


def _mod_fwd(c_all, w_ada_sh, b_ada_sh, name):
    cols = w_ada_sh.shape[2]

    def body(c_ref, w_ref, b_ref, o_ref):
        ca = _silu(c_ref[...]).astype(BF16)
        o_ref[0] = jnp.dot(ca, w_ref[0].astype(BF16), preferred_element_type=F32) + b_ref[0]

    return pl.pallas_call(
        body, grid=(DEPTH,),
        in_specs=[_full((N_DEV, D_MODEL)), pl.BlockSpec((1, D_MODEL, cols), lambda l: (l, 0, 0)),
                  pl.BlockSpec((1, 1, cols), lambda l: (l, 0, 0))],
        out_specs=pl.BlockSpec((1, N_DEV, cols), lambda l: (l, 0, 0)),
        out_shape=jax.ShapeDtypeStruct((DEPTH, N_DEV, cols), F32), name=name, compiler_params=_cparams(),
    )(c_all, w_ada_sh, b_ada_sh)


def _mod_bwd(c_all, dmod_sh, name):
    cols = dmod_sh.shape[2]

    def body(c_ref, d_ref, o_ref):
        ca = _silu(c_ref[...])
        o_ref[0] = lax.dot_general(ca, d_ref[0], _DIMS["tn"], precision=lax.Precision.HIGHEST,
                                   preferred_element_type=F32)

    return pl.pallas_call(
        body, grid=(DEPTH,),
        in_specs=[_full((N_DEV, D_MODEL)), pl.BlockSpec((1, N_DEV, cols), lambda l: (l, 0, 0))],
        out_specs=pl.BlockSpec((1, D_MODEL, cols), lambda l: (l, 0, 0)),
        out_shape=jax.ShapeDtypeStruct((DEPTH, D_MODEL, cols), F32), name=name, compiler_params=_cparams(),
    )(c_all, dmod_sh)


def _flat_tiles(rows, cols, itemsize_total):
    budget = 12 * 1024 * 1024
    tr = rows
    while tr % 32 == 0 and tr * cols * itemsize_total > budget:
        tr //= 2
    return tr


def _sum_cores(dw, recv, place, name):
    _, m, n = dw.shape
    tr = _flat_tiles(m, n, 6)

    def body(place_ref, a_ref, b_ref, o_ref):
        o_ref[...] = (a_ref[...].astype(F32) + b_ref[...].astype(F32)).astype(BF16)

    grid_spec = pltpu.PrefetchScalarGridSpec(
        num_scalar_prefetch=1, grid=(m // tr,),
        in_specs=[pl.BlockSpec((None, tr, n), lambda i, pr: (pr[0], i, 0)), pl.BlockSpec((tr, n), lambda i, pr: (i, 0))],
        out_specs=pl.BlockSpec((tr, n), lambda i, pr: (i, 0)))
    return pl.pallas_call(body, grid_spec=grid_spec, out_shape=jax.ShapeDtypeStruct((m, n), BF16), name=name,
                          compiler_params=_cparams())(place, dw, recv)


def _sum_chips(h, r, place, name):
    _, rs, n = h.shape
    tr = _flat_tiles(rs, n, 12)

    def body(place_ref, h_ref, r_ref, o_ref):
        o_ref[...] = ((h_ref[...].astype(F32) + r_ref[0].astype(F32)) + r_ref[1].astype(F32)) + r_ref[2].astype(F32)

    grid_spec = pltpu.PrefetchScalarGridSpec(
        num_scalar_prefetch=1, grid=(rs // tr,),
        in_specs=[pl.BlockSpec((None, tr, n), lambda i, pr: (pr[1], i, 0)), pl.BlockSpec((3, tr, n), lambda i, pr: (0, i, 0))],
        out_specs=pl.BlockSpec((tr, n), lambda i, pr: (i, 0)))
    return pl.pallas_call(body, grid_spec=grid_spec, out_shape=jax.ShapeDtypeStruct((rs, n), F32), name=name,
                          compiler_params=_cparams())(place, h, r)


def _adam_math(w, g, m, v):
    m2 = ADAM_B1 * m + (1.0 - ADAM_B1) * g
    v2 = ADAM_B2 * v + (1.0 - ADAM_B2) * (g * g)
    m_hat = m2 / (1.0 - ADAM_B1 ** ADAM_STEP)
    v_hat = v2 / (1.0 - ADAM_B2 ** ADAM_STEP)
    delta = -ADAM_LR * (m_hat / (jnp.sqrt(v_hat) + ADAM_EPS) + ADAM_WD * w)
    return delta, m2, v2


def _adamw(w, m, v, grads, name):
    r, c = w.shape
    tr = _flat_tiles(r, c, 4 * (7 + len(grads)))

    def body(*refs):
        w_ref, m_ref, v_ref = refs[:3]
        g_refs = refs[3:3 + len(grads)]
        g_ref, d_ref, m2_ref, v2_ref = refs[3 + len(grads):]
        g = g_refs[0][...]
        for gr in g_refs[1:]:
            g = g + gr[...]
        delta, m2, v2 = _adam_math(w_ref[...], g, m_ref[...], v_ref[...])
        g_ref[...] = g
        d_ref[...] = delta
        m2_ref[...] = m2
        v2_ref[...] = v2

    blk = pl.BlockSpec((tr, c), lambda i: (i, 0))
    sh = jax.ShapeDtypeStruct((r, c), F32)
    return pl.pallas_call(body, grid=(r // tr,), in_specs=[blk] * (3 + len(grads)), out_specs=[blk] * 4,
                          out_shape=[sh] * 4, name=name, compiler_params=_cparams())(w, m, v, *grads)


def _adamw_halves(w, m, v, own, other, place, split, name):
    nl, r, c = w.shape
    hr, hc = own[0].shape
    tr = _flat_tiles(hr, hc, 4 * (7 + 2 * nl))
    nt = hr // tr
    if split == "rows":
        w_spec = pl.BlockSpec((None, tr, c), lambda l, h, t, pr: (l, h * nt + t, 0))
    else:
        w_spec = pl.BlockSpec((None, tr, hc), lambda l, h, t, pr: (l, t, h))

    def g_spec(layer):
        return pl.BlockSpec((tr, hc), lambda l, h, t, pr: (jnp.where(l == layer, t, nt - 1), 0))

    def body(place_ref, w_ref, m_ref, v_ref, *refs):
        own_refs, other_refs = refs[:nl], refs[nl:2 * nl]
        g_ref, d_ref, m2_ref, v2_ref = refs[2 * nl:]
        layer = pl.program_id(0)
        mine = pl.program_id(1) == place_ref[0]
        g = None
        for li in range(nl):
            cand = jnp.where(mine, own_refs[li][...], other_refs[li][...])
            g = cand if g is None else jnp.where(layer == li, cand, g)
        delta, m2, v2 = _adam_math(w_ref[...], g, m_ref[...], v_ref[...])
        g_ref[...] = g
        d_ref[...] = delta
        m2_ref[...] = m2
        v2_ref[...] = v2

    grid_spec = pltpu.PrefetchScalarGridSpec(
        num_scalar_prefetch=1, grid=(nl, 2, nt),
        in_specs=[w_spec] * 3 + [g_spec(li) for li in range(nl)] * 2, out_specs=[w_spec] * 4)
    sh = jax.ShapeDtypeStruct((nl, r, c), F32)
    return pl.pallas_call(body, grid_spec=grid_spec, out_shape=[sh] * 4, name=name,
                          compiler_params=_cparams())(place, w, m, v, *own, *other)


def _adamw_small(w, m, v, g_all, name):
    r, c = w.shape

    def body(w_ref, m_ref, v_ref, g_ref, go_ref, d_ref, m2_ref, v2_ref):
        g = g_ref[0]
        for b in range(1, N_DEV):
            g = g + g_ref[b]
        delta, m2, v2 = _adam_math(w_ref[...], g, m_ref[...], v_ref[...])
        go_ref[...] = g
        d_ref[...] = delta
        m2_ref[...] = m2
        v2_ref[...] = v2

    sh = jax.ShapeDtypeStruct((r, c), F32)
    return pl.pallas_call(body, out_shape=[sh] * 4, name=name, compiler_params=_cparams())(w, m, v, g_all)


def _me():
    return lax.axis_index("x"), lax.axis_index("y"), lax.axis_index("c")


def _flip(v, bit):
    return 1 - v if bit else v


def _allgather_small(blk, name):
    r, c = blk.shape

    def body(x_ref, o_ref, send_sems, recv_sems):
        x, y, cc = _me()
        me = 4 * x + 2 * y + cc
        copies = []
        for k in range(1, N_DEV):
            peer = (_flip(x, k & 4), _flip(y, k & 2), _flip(cc, k & 1))
            cp = pltpu.make_async_remote_copy(src_ref=x_ref, dst_ref=o_ref.at[me], send_sem=send_sems.at[k - 1],
                                              recv_sem=recv_sems.at[k - 1], device_id=peer, device_id_type=MESH)
            cp.start()
            copies.append(cp)
        o_ref[me] = x_ref[...]
        for cp in copies:
            cp.wait()

    return pl.pallas_call(
        body, out_shape=jax.ShapeDtypeStruct((N_DEV, r, c), F32),
        in_specs=[pl.BlockSpec(memory_space=pltpu.VMEM)], out_specs=pl.BlockSpec(memory_space=pltpu.VMEM),
        scratch_shapes=[pltpu.SemaphoreType.DMA((N_DEV - 1,)), pltpu.SemaphoreType.DMA((N_DEV - 1,))],
        name=name, compiler_params=_cparams(),
    )(blk)


def _gather_rows(shards, name):
    n = len(shards)

    def body(*refs):
        ins, outs = refs[:n], refs[n:2 * n]
        ici_send, ici_recv, d2d_send, d2d_recv, loc_sems = refs[2 * n:2 * n + 5]
        stage = refs[2 * n + 5:]
        x, y, cc = _me()
        chip = 2 * x + y
        sibling = (x, y, 1 - cc)
        local, sends, relays = [], [], []
        for j in range(n):
            def rows(ch, h, j=j):
                return outs[j].at[ch, h]

            lc = pltpu.make_async_copy(ins[j], stage[j], loc_sems.at[j])
            lc.start()
            local.append((lc, pltpu.make_async_copy(stage[j], outs[j].at[chip], loc_sems.at[n + j])))
            for k in range(1, N_CHIP):
                px, py = _flip(x, k & 2), _flip(y, k & 1)
                pchip = 2 * px + py
                q = 3 * j + k - 1
                out_cp = pltpu.make_async_remote_copy(src_ref=ins[j].at[cc], dst_ref=rows(chip, cc),
                                                      send_sem=ici_send.at[q], recv_sem=ici_recv.at[q],
                                                      device_id=(px, py, cc), device_id_type=MESH)
                out_cp.start()
                sends.append(out_cp)
                arrival = pltpu.make_async_remote_copy(src_ref=rows(pchip, cc), dst_ref=rows(pchip, cc),
                                                       send_sem=ici_send.at[q], recv_sem=ici_recv.at[q],
                                                       device_id=(px, py, cc), device_id_type=MESH)
                forward = pltpu.make_async_remote_copy(src_ref=rows(pchip, cc), dst_ref=rows(pchip, cc),
                                                       send_sem=d2d_send.at[q], recv_sem=d2d_recv.at[q],
                                                       device_id=sibling, device_id_type=MESH)
                from_sibling = pltpu.make_async_remote_copy(src_ref=rows(pchip, 1 - cc), dst_ref=rows(pchip, 1 - cc),
                                                            send_sem=d2d_send.at[q], recv_sem=d2d_recv.at[q],
                                                            device_id=sibling, device_id_type=MESH)
                relays.append((arrival, forward, from_sibling))
        for lin, lout in local:
            lin.wait()
            lout.start()
        for arrival, forward, _ in relays:
            arrival.wait_recv()
            forward.start()
        for cp in sends:
            cp.wait_send()
        for _, forward, from_sibling in relays:
            forward.wait_send()
            from_sibling.wait_recv()
        for _, lout in local:
            lout.wait()

    anyspec = pl.BlockSpec(memory_space=pl.ANY)
    return pl.pallas_call(
        body, out_shape=[jax.ShapeDtypeStruct((N_CHIP,) + a.shape, a.dtype) for a in shards],
        in_specs=[anyspec] * n, out_specs=[anyspec] * n,
        scratch_shapes=[pltpu.SemaphoreType.DMA((3 * n,)), pltpu.SemaphoreType.DMA((3 * n,)),
                        pltpu.SemaphoreType.DMA((3 * n,)), pltpu.SemaphoreType.DMA((3 * n,)),
                        pltpu.SemaphoreType.DMA((2 * n,))] + [pltpu.VMEM(a.shape, a.dtype) for a in shards],
        name=name, compiler_params=_cparams(),
    )(*shards)


def _sibling_send(arrs, name, other_half=False):
    n = len(arrs)

    def body(*refs):
        ins, outs = refs[:n], refs[n:2 * n]
        send_sems, recv_sems = refs[2 * n:]
        x, y, cc = _me()
        pending = []
        for j in range(n):
            src = ins[j].at[1 - cc] if other_half else ins[j]
            cp = pltpu.make_async_remote_copy(src_ref=src, dst_ref=outs[j], send_sem=send_sems.at[j],
                                              recv_sem=recv_sems.at[j], device_id=(x, y, 1 - cc), device_id_type=MESH)
            cp.start()
            pending.append(cp)
        for cp in pending:
            cp.wait()

    anyspec = pl.BlockSpec(memory_space=pl.ANY)
    return pl.pallas_call(
        body, out_shape=[jax.ShapeDtypeStruct(a.shape[1:] if other_half else a.shape, a.dtype) for a in arrs],
        in_specs=[anyspec] * n, out_specs=[anyspec] * n,
        scratch_shapes=[pltpu.SemaphoreType.DMA((n,)), pltpu.SemaphoreType.DMA((n,))],
        name=name, compiler_params=_cparams(),
    )(*arrs)


def _chip_scatter(arrs, name):
    n = len(arrs)

    def body(*refs):
        ins, outs = refs[:n], refs[n:2 * n]
        send_sems, recv_sems = refs[2 * n:]
        x, y, cc = _me()
        pending = []
        for j in range(n):
            for k in range(1, N_CHIP):
                px, py = _flip(x, k & 2), _flip(y, k & 1)
                cp = pltpu.make_async_remote_copy(src_ref=ins[j].at[2 * px + py], dst_ref=outs[j].at[k - 1],
                                                  send_sem=send_sems.at[3 * j + k - 1], recv_sem=recv_sems.at[3 * j + k - 1],
                                                  device_id=(px, py, cc), device_id_type=MESH)
                cp.start()
                pending.append(cp)
        for cp in pending:
            cp.wait()

    anyspec = pl.BlockSpec(memory_space=pl.ANY)
    return pl.pallas_call(
        body, out_shape=[jax.ShapeDtypeStruct((N_CHIP - 1,) + a.shape[1:], a.dtype) for a in arrs],
        in_specs=[anyspec] * n, out_specs=[anyspec] * n,
        scratch_shapes=[pltpu.SemaphoreType.DMA((3 * n,)), pltpu.SemaphoreType.DMA((3 * n,))],
        name=name, compiler_params=_cparams(),
    )(*arrs)


COL_SHARDED = ("w_in", "w_br_pool", "w_br_attn", "w_br_conv", "w_ff1")
ROW_SHARDED = ("w_o", "w_ff2")
BIG = COL_SHARDED + ROW_SHARDED
SMALL = ("b_ada", "b_gate", "w_pool", "pool_scale", "rel_bias", "conv_w", "conv_b", "conv_ln_g", "conv_ln_b",
         "ln_mix_g", "ln_mix_b", "b_ff1", "b_ff2", "ln_ff_g", "ln_ff_b")
PACK_W = 1024


def _pack(parts):
    rows = []
    for a in parts:
        flat = a.reshape(-1)
        n = -(-flat.shape[0] // PACK_W) * PACK_W
        rows.append(jnp.pad(flat, (0, n - flat.shape[0])).reshape(-1, PACK_W))
    out = jnp.concatenate(rows, axis=0)
    r = -(-out.shape[0] // 8) * 8
    return jnp.pad(out, ((0, r - out.shape[0]), (0, 0)))


def _unpack(packed, shapes):
    out, r0 = [], 0
    for shp in shapes:
        size = int(np.prod(shp))
        nr = -(-size // PACK_W)
        out.append(packed[r0:r0 + nr].reshape(-1)[:size].reshape(shp))
        r0 += nr
    return out


def _layer_fwd(l, x, mod, W, P):
    s = x.shape[0]
    sh_m, sc_m, g_m, sh_f, sc_f, g_f = [mod[l:l + 1, D_MODEL * j:D_MODEL * (j + 1)] for j in range(6)]
    n = lambda t: f"{t}{l}"
    w_in = W["w_in"][l]
    u = _ln_mod(x, sc_m, sh_m, n("ln_mod_mix"))
    tmz = min(1024, s)
    zp = _mm(u, w_in, "nt", tm=min(2048, s), tn=256, out_dtype=F32, name=n("z_pool"), b_col0=0, n_out=D_POOL)
    qkv = _mm(u, w_in, "nt", tm=tmz, tn=256, out_dtype=BF16, name=n("z_qkv"), b_col0=OFF_QKV // 256, n_out=3 * D_ATTN)
    zc = _mm(u, w_in, "nt", tm=tmz, tn=256, out_dtype=F32, name=n("z_conv"), b_col0=OFF_CONV // 256, n_out=2 * D_CONV)
    zg = _mm(u, w_in, "nt", tm=tmz, tn=768, out_dtype=F32, name=n("z_gate"), b_col0=OFF_GATE // 768, n_out=3 * D_MODEL)

    p, feat_pool = _pool_fwd(zp, P["wp_bd"][l], P["pool_scale"][l], n("pool_fwd"))
    bias = _bias_block(P["rel_bias"][l], n("bias_block"))
    o = _attn_fwd(qkv, bias, n("attn_fwd"))
    cv, feat_conv = _conv_fwd(zc, P["conv_w"][l], P["conv_b"][l], P["conv_ln_g"][l], P["conv_ln_b"][l], n("conv_fwd"))

    tmb = min(1024, s)
    y_pool = _mm(feat_pool, W["w_br_pool"][l], "nt", tm=tmb, tn=1024, out_dtype=F32, name=n("y_pool"))
    y_attn = _mm(o, W["w_br_attn"][l], "nt", tm=tmb, tn=1024, out_dtype=F32, name=n("y_attn"))
    y_conv = _mm(feat_conv, W["w_br_conv"][l], "nt", tm=tmb, tn=1024, out_dtype=F32, name=n("y_conv"))
    ys = (y_pool, y_attn, y_conv)
    merged = _merge(zg, P["b_gate"][l], ys, n("merge"))
    mix = _mm(merged, W["w_o"][l], "nn", tm=tmb, tn=1024, out_dtype=F32, name=n("mix_out"))
    x1 = _resid_ln(x, mix, g_m, P["ln_mix_g"][l], P["ln_mix_b"][l], n("resid_ln_mix"))

    u2 = _ln_mod(x1, sc_f, sh_f, n("ln_mod_ff"))
    hpre = _mm(u2, W["w_ff1"][l], "nt", tm=tmb, tn=1024, out_dtype=F32, name=n("ff1"), bias=P["b_ff1"][l])
    ff = _mm(hpre, W["w_ff2"][l], "nn", tm=min(256, s), tn=1024, out_dtype=F32, name=n("ff2"), a_fn=_relu2,
             bias=P["b_ff2"][l])
    x2 = _resid_ln(x1, ff, g_f, P["ln_ff_g"][l], P["ln_ff_b"][l], n("resid_ln_ff"))
    saved = dict(x=x, u=u, zp=zp, qkv=qkv, zc=zc, zg=zg, p=p, feat_pool=feat_pool, bias=bias, o=o, cv=cv,
                 feat_conv=feat_conv, ys=ys, merged=merged, mix=mix, x1=x1, u2=u2, hpre=hpre, ff=ff)
    return x2, saved


def _layer_bwd(l, dx2, mod, W, P, A):
    s = dx2.shape[0]
    sh_m, sc_m, g_m, sh_f, sc_f, g_f = [mod[l:l + 1, D_MODEL * j:D_MODEL * (j + 1)] for j in range(6)]
    n = lambda t: f"{t}{l}"
    tmb = min(1024, s)
    gw, gs = {}, {}

    dres, dff, gs["ln_ff_g"], gs["ln_ff_b"], dg_f, gs["b_ff2"] = _resid_ln_bwd(
        dx2, A["x1"], A["ff"], g_f, P["ln_ff_g"][l], n("resid_ln_ff_bwd"))
    gw["w_ff2"] = _mm(A["hpre"], dff, "tn", tm=256, tn=512, out_dtype=BF16, name=n("dw_ff2"), a_fn=_relu2,
                      split_n=True)
    dhpre, gs["b_ff1"] = _ff_hidden_bwd(dff, W["w_ff2"][l], A["hpre"], n("ff_hidden_bwd"))
    gw["w_ff1"] = _mm(dhpre, A["u2"], "tn", tm=1024, tn=512, out_dtype=BF16, name=n("dw_ff1"), split_n=True)
    du2 = _mm(dhpre, W["w_ff1"][l], "nn", tm=min(512, s), tn=512, out_dtype=F32, name=n("du_ff"))
    dx1, dsc_f, dsh_f = _ln_mod_bwd(du2, A["x1"], sc_f, dres, n("ln_mod_ff_bwd"))

    dres, dmix, gs["ln_mix_g"], gs["ln_mix_b"], dg_m, _ = _resid_ln_bwd(
        dx1, A["x"], A["mix"], g_m, P["ln_mix_g"][l], n("resid_ln_mix_bwd"))
    gw["w_o"] = _mm(A["merged"], dmix, "tn", tm=1024, tn=512, out_dtype=BF16, name=n("dw_o"), split_n=True)
    dmerged = _mm(dmix, W["w_o"][l], "nt", tm=tmb, tn=1024, out_dtype=F32, name=n("d_merged"))
    dy_pool, dy_attn, dy_conv, dzg, gs["b_gate"] = _merge_bwd(dmerged, A["zg"], P["b_gate"][l], A["ys"], n("merge_bwd"))

    gw["w_br_pool"] = _mm(dy_pool, A["feat_pool"], "tn", tm=1024, tn=128, out_dtype=BF16, name=n("dw_br_pool"),
                          split_n=True)
    gw["w_br_attn"] = _mm(dy_attn, A["o"], "tn", tm=1024, tn=256, out_dtype=BF16, name=n("dw_br_attn"), split_n=True)
    gw["w_br_conv"] = _mm(dy_conv, A["feat_conv"], "tn", tm=1024, tn=128, out_dtype=BF16, name=n("dw_br_conv"),
                          split_n=True)
    dfeat_pool = _mm(dy_pool, W["w_br_pool"][l], "nn", tm=tmb, tn=256, out_dtype=F32, name=n("d_feat_pool"))
    do = _mm(dy_attn, W["w_br_attn"][l], "nn", tm=tmb, tn=512, out_dtype=BF16, name=n("d_attn_out"))
    dfeat_conv = _mm(dy_conv, W["w_br_conv"][l], "nn", tm=tmb, tn=256, out_dtype=F32, name=n("d_feat_conv"))

    dzp, dwp_bd, gs["pool_scale"] = _pool_bwd(dfeat_pool, A["p"], P["wp_bd"][l], P["pool_scale"][l], n("pool_bwd"))
    gs["w_pool"] = jnp.stack([dwp_bd[POOL_GROUP * g:POOL_GROUP * (g + 1), POOL_GROUP * g:POOL_GROUP * (g + 1)]
                              for g in range(len(POOL_WINDOWS))])
    dq, dk, dv, ds_acc = _attn_bwd(A["qkv"], do, A["bias"], n("attn_bwd"))
    gs["rel_bias"] = _bias_block_bwd(ds_acc, n("bias_block_bwd"))
    dzc, dcw, gs["conv_b"], gs["conv_ln_g"], gs["conv_ln_b"] = _conv_bwd(
        dfeat_conv, A["cv"], A["zc"], P["conv_w"][l], P["conv_ln_g"][l], P["conv_ln_b"][l], n("conv_bwd"))
    gs["conv_w"] = dcw[:CONV_WIDTH]

    dz = jnp.concatenate([dzp, dq, dk[KPAD:].astype(BF16), dv[KPAD:].astype(BF16), dzc, dzg], axis=1)
    gw["w_in"] = _mm(dz, A["u"], "tn", tm=768, tn=512, out_dtype=BF16, name=n("dw_in"), split_n=True)
    du = _mm(dz, W["w_in"][l], "nn", tm=min(512, s), tn=512, out_dtype=F32, name=n("du_mix"))
    dx, dsc_m, dsh_m = _ln_mod_bwd(du, A["x"], sc_m, dres, n("ln_mod_mix_bwd"))
    dmod = jnp.concatenate([dsh_m, dsc_m, dg_m, dsh_f, dsc_f, dg_f], axis=1)
    return dx, gw, gs, dmod


def _small_shapes():
    return {"b_ada": (6 * D_MODEL,), "b_gate": (3 * D_MODEL,), "w_pool": (4, POOL_GROUP, POOL_GROUP),
            "pool_scale": (D_POOL,), "rel_bias": (N_HEADS, N_REL), "conv_w": (CONV_WIDTH, D_CONV),
            "conv_b": (D_CONV,), "conv_ln_g": (D_CONV,), "conv_ln_b": (D_CONV,), "ln_mix_g": (D_MODEL,),
            "ln_mix_b": (D_MODEL,), "b_ff1": (D_FF,), "b_ff2": (D_MODEL,), "ln_ff_g": (D_MODEL,), "ln_ff_b": (D_MODEL,)}


def kernel(x, c, w_ada, b_ada, w_in, b_gate, w_pool, pool_scale, rel_bias, conv_w, conv_b, conv_ln_g, conv_ln_b, w_br_pool, w_br_attn, w_br_conv, w_o, ln_mix_g, ln_mix_b, w_ff1, b_ff1, w_ff2, b_ff2, ln_ff_g, ln_ff_b, loss_target, m_w_ada, m_b_ada, m_w_in, m_b_gate, m_w_pool, m_pool_scale, m_rel_bias, m_conv_w, m_conv_b, m_conv_ln_g, m_conv_ln_b, m_w_br_pool, m_w_br_attn, m_w_br_conv, m_w_o, m_ln_mix_g, m_ln_mix_b, m_w_ff1, m_b_ff1, m_w_ff2, m_b_ff2, m_ln_ff_g, m_ln_ff_b, v_w_ada, v_b_ada, v_w_in, v_b_gate, v_w_pool, v_pool_scale, v_rel_bias, v_conv_w, v_conv_b, v_conv_ln_g, v_conv_ln_b, v_w_br_pool, v_w_br_attn, v_w_br_conv, v_w_o, v_ln_mix_g, v_ln_mix_b, v_w_ff1, v_b_ff1, v_w_ff2, v_b_ff2, v_ln_ff_g, v_ln_ff_b):
    env = dict(locals())
    xi, yi, ci = _me()
    chip = 2 * xi + yi
    me = 4 * xi + 2 * yi + ci
    xs = x[0]
    tgt = loss_target[0]
    L = DEPTH

    c_all = _allgather_small(c.reshape(8, 128), "gather_c").reshape(N_DEV, D_MODEL)
    ada_cols = w_ada.shape[2]
    b_ada_sh = lax.dynamic_slice_in_dim(b_ada, chip * ada_cols, ada_cols, axis=1).reshape(L, 1, ada_cols)
    mod_part = _mod_fwd(c_all, w_ada, b_ada_sh, "mod_fwd")
    mod_g = _allgather_small(mod_part.reshape(-1, 128), "gather_mod").reshape(N_CHIP, 2, L, N_DEV, ada_cols)[:, 0]
    mod_all = jnp.transpose(mod_g, (1, 2, 0, 3)).reshape(L, N_DEV, 6 * D_MODEL)
    mod = lax.dynamic_index_in_dim(mod_all, me, axis=1, keepdims=False)

    W = {k: [None] * L for k in BIG}
    for l in range(L):
        shards = [(jnp.swapaxes(env[k][l], 0, 1) if k in COL_SHARDED else env[k][l]).astype(BF16) for k in BIG]
        shards = [a.reshape(2, a.shape[0] // 2, a.shape[1]) for a in shards]
        for k, g in zip(BIG, _gather_rows(shards, f"gather_weights{l}")):
            W[k][l] = g.reshape(-1, g.shape[-1])

    P = {k: env[k] for k in ("rel_bias", "conv_w")}
    for k in ("b_gate", "pool_scale", "conv_b", "conv_ln_g", "conv_ln_b", "ln_mix_g", "ln_mix_b", "b_ff1", "b_ff2",
              "ln_ff_g", "ln_ff_b"):
        P[k] = env[k].reshape(L, 1, -1)
    conv_w_full = _allgather_small(_pack([conv_w]), "gather_conv_w")
    n_cw = conv_w.size
    cw = conv_w_full.reshape(N_CHIP, 2, -1)[:, 0, :n_cw].reshape(N_CHIP, L, CONV_WIDTH, D_CONV // N_CHIP)
    P["conv_w"] = jnp.transpose(cw, (1, 2, 0, 3)).reshape(L, CONV_WIDTH, D_CONV)
    wp_bd = jnp.zeros((L, D_POOL, D_POOL), F32)
    for g in range(len(POOL_WINDOWS)):
        sl = slice(POOL_GROUP * g, POOL_GROUP * (g + 1))
        wp_bd = wp_bd.at[:, sl, sl].set(w_pool[:, g])
    P["wp_bd"] = wp_bd.astype(BF16)

    acts = []
    h = xs
    for l in range(L):
        h, saved = _layer_fwd(l, h, mod, W, P)
        acts.append(saved)
    dy, loss_part = _loss_grad(h, tgt, "loss_grad")
    loss = lax.psum(loss_part[0, 0], ("x", "y", "c"))

    gws, gss, dmods = [None] * L, [None] * L, [None] * L
    dh = dy
    for l in reversed(range(L)):
        dh, gws[l], gss[l], dmods[l] = _layer_bwd(l, dh, mod, W, P, acts[l])
    grad_x = dh[None]

    place = jnp.stack([ci, chip, chip ^ 1, chip ^ 2, chip ^ 3]).astype(jnp.int32)
    reduced = [[None] * L for _ in BIG]
    for l in reversed(range(L)):
        dws = [gws[l][k] for k in BIG]
        got = _sibling_send(dws, f"swap_blocks{l}", other_half=True)
        both = [_sum_cores(a, b, place, f"sum_cores_{k}{l}") for k, a, b in zip(BIG, dws, got)]
        both = [h.reshape(N_CHIP, -1, h.shape[-1]) for h in both]
        for j, (k, h, r) in enumerate(zip(BIG, both, _chip_scatter(both, f"scatter_grads{l}"))):
            reduced[j][l] = _sum_chips(h, r, place, f"sum_chips_{k}{l}")
    flat_reduced = [t for per_weight in reduced for t in per_weight]
    flat_other = _sibling_send(flat_reduced, "swap_reduced")

    out = {}
    for j, k in enumerate(BIG):
        own, other = reduced[j], flat_other[L * j:L * (j + 1)]
        if k in COL_SHARDED:
            own, other = [a.T for a in own], [a.T for a in other]
        out[k] = tuple(_adamw_halves(env[k], env["m_" + k], env["v_" + k], own, other, place,
                                     "rows" if k in COL_SHARDED else "cols", f"adamw_{k}"))

    shapes = _small_shapes()
    small_names = [k for k in SMALL if k != "b_ada"]
    dmod_own = jnp.concatenate(dmods, axis=0)
    pack = _pack([dmod_own] + [jnp.stack([gss[l][k].reshape(shapes[k]) for l in range(L)]) for k in small_names])
    g_all = _allgather_small(pack.reshape(-1, 128), "gather_small").reshape(N_DEV, -1, PACK_W)

    dmod_all = g_all[:, :L * 6].reshape(N_DEV, L, 6 * D_MODEL)
    dmod_sh = jnp.transpose(lax.dynamic_slice_in_dim(dmod_all, chip * ada_cols, ada_cols, axis=2), (1, 0, 2))
    g_ada = _mod_bwd(c_all, dmod_sh, "mod_bwd")
    g_, d_, m_, v_ = _adamw(w_ada.reshape(-1, ada_cols), m_w_ada.reshape(-1, ada_cols), v_w_ada.reshape(-1, ada_cols),
                            [g_ada.reshape(-1, ada_cols)], "adamw_w_ada")
    out["w_ada"] = tuple(a.reshape(w_ada.shape) for a in (g_, d_, m_, v_))

    def small_pack(prefix):
        parts = [env[prefix + "b_ada"]]
        for k in small_names:
            a = env[prefix + k]
            if k == "conv_w":
                a = jnp.zeros((L,) + shapes[k], F32)
            parts.append(a)
        return _pack(parts)

    gp, dp, mp, vp = _adamw_small(small_pack(""), small_pack("m_"), small_pack("v_"), g_all, "adamw_small")
    full_shapes = [(L,) + shapes["b_ada"]] + [(L,) + shapes[k] for k in small_names]
    for tag, packed in (("g", gp), ("d", dp), ("m", mp), ("v", vp)):
        for k, a in zip(["b_ada"] + small_names, _unpack(packed, full_shapes)):
            out.setdefault(k, {})
            out[k][tag] = a
    g_cw_full = out["conv_w"]["g"]
    cw_cols = D_CONV // N_CHIP
    g_cw = lax.dynamic_slice_in_dim(g_cw_full, chip * cw_cols, cw_cols, axis=2)
    pad_rows = lambda a: jnp.pad(a.reshape(L * CONV_WIDTH, cw_cols), ((0, 2), (0, 0)))
    g_, d_, m_, v_ = _adamw(pad_rows(conv_w), pad_rows(m_conv_w), pad_rows(v_conv_w), [pad_rows(g_cw)], "adamw_conv_w")
    out["conv_w"] = tuple(a[:L * CONV_WIDTH].reshape(L, CONV_WIDTH, cw_cols) for a in (g_, d_, m_, v_))

    names = ["w_ada", "b_ada", "w_in", "b_gate", "w_pool", "pool_scale", "rel_bias", "conv_w", "conv_b", "conv_ln_g",
             "conv_ln_b", "w_br_pool", "w_br_attn", "w_br_conv", "w_o", "ln_mix_g", "ln_mix_b", "w_ff1", "b_ff1",
             "w_ff2", "b_ff2", "ln_ff_g", "ln_ff_b"]

    def pick(k, i):
        o = out[k]
        return o[i] if isinstance(o, tuple) else o["gdmv"[i]].reshape(env[k].shape)

    return (loss, grad_x, *[pick(k, 0) for k in names], *[pick(k, 1) for k in names],
            *[pick(k, 2) for k in names], *[pick(k, 3) for k in names])
```

```python
import functools

import jax
import jax.numpy as jnp
import numpy as np
from jax import lax
from jax.experimental import pallas as pl
from jax.experimental.pallas import tpu as pltpu

F32 = jnp.float32
BF16 = jnp.bfloat16

D_MODEL = 1024
DEPTH = 2
CHUNK = 64
POOL_WINDOWS = (2, 4, 8, 16)
POOL_GROUP = 64
D_POOL = 256
N_HEADS = 8
HEAD_DIM = 64
D_ATTN = 512
N_PREV_CHUNKS = 8
REL_CLIP = 128
N_REL = 2 * REL_CLIP + 1
D_CONV = 256
CONV_WIDTH = 31
D_FF = 4 * D_MODEL
D_IN = 5376
OFF_POOL, OFF_QKV, OFF_CONV, OFF_GATE = 0, 256, 1792, 2304
ALPHA = (2.0 * DEPTH) ** 0.25
LN_EPS = 1e-5
NEG_INF = -1e30
ADAM_LR, ADAM_B1, ADAM_B2, ADAM_EPS, ADAM_WD, ADAM_STEP = 0.001, 0.9, 0.999, 1e-08, 0.01, 10

N_DEV = 8
N_CHIP = 4
MESH = pl.DeviceIdType.MESH

QB = 2 * CHUNK
KPAD = N_PREV_CHUNKS * CHUNK
KW = QB + KPAD
SKEW_W = 768

VMEM_LIMIT = 56 * 1024 * 1024


def _cparams(**kw):
    return pltpu.CompilerParams(vmem_limit_bytes=VMEM_LIMIT, **kw)


def _full(shape):
    n = len(shape)
    return pl.BlockSpec(shape, lambda *_: (0,) * n)


_DIMS = {"nn": (((1,), (0,)), ((), ())), "nt": (((1,), (1,)), ((), ())), "tn": (((0,), (0,)), ((), ()))}


def _relu2(t):
    r = jnp.maximum(t, 0.0)
    return r * r


def _mm(a, b, mode, *, tm, tn, out_dtype, name, b_col0=0, n_out=None, a_fn=None, bias=None, split_n=False,
        rider=None):
    if mode == "tn":
        k, m = a.shape
        n = b.shape[1] if n_out is None else n_out
        a_spec = pl.BlockSpec((k, tm), lambda i, j: (0, i))
        b_spec = pl.BlockSpec((k, tn), lambda i, j: (0, j + b_col0))
    elif mode == "nn":
        m, k = a.shape
        n = b.shape[1] if n_out is None else n_out
        a_spec = pl.BlockSpec((tm, k), lambda i, j: (i, 0))
        b_spec = pl.BlockSpec((k, tn), lambda i, j: (0, j + b_col0))
    else:
        m, k = a.shape
        n = b.shape[0] if n_out is None else n_out
        a_spec = pl.BlockSpec((tm, k), lambda i, j: (i, 0))
        b_spec = pl.BlockSpec((tn, k), lambda i, j: (j + b_col0, 0))
    assert m % tm == 0 and n % tn == 0, (name, m, n, tm, tn)
    dims = _DIMS[mode]

    def body(*refs):
        if bias is None:
            a_ref, b_ref, o_ref = refs
        else:
            a_ref, b_ref, bias_ref, o_ref = refs
        av = a_ref[...]
        if a_fn is not None:
            av = a_fn(av)
        acc = lax.dot_general(av.astype(BF16), b_ref[...].astype(BF16), dims, preferred_element_type=F32)
        if bias is not None:
            acc = acc + bias_ref[...]
        o_ref[...] = acc.astype(out_dtype)

    in_specs = [a_spec, b_spec]
    args = [a, b]
    if bias is not None:
        in_specs.append(pl.BlockSpec((1, tn), lambda i, j: (0, j)))
        args.append(bias)
    if split_n:
        out_spec = pl.BlockSpec((None, tm, tn), lambda i, j: (j, i, 0))
        out_shape = jax.ShapeDtypeStruct((n // tn, m, tn), out_dtype)
    else:
        out_spec = pl.BlockSpec((tm, tn), lambda i, j: (i, j))
        out_shape = jax.ShapeDtypeStruct((m, n), out_dtype)
    res = _call(body, name=name, grid=(m // tm, n // tn), in_specs=in_specs, out_specs=[out_spec],
                out_shape=[out_shape], scratch_shapes=[], args=args, rider=rider)
    return res[0] if rider is None else (res[0][0], res[1])


def _ln_hat(x):
    mu = jnp.mean(x, axis=-1, keepdims=True)
    xc = x - mu
    var = jnp.mean(xc * xc, axis=-1, keepdims=True)
    rstd = lax.rsqrt(var + LN_EPS)
    return xc * rstd, rstd


def _ln_hat_bwd(dhat, xhat, rstd):
    m1 = jnp.mean(dhat, axis=-1, keepdims=True)
    m2 = jnp.mean(dhat * xhat, axis=-1, keepdims=True)
    return rstd * (dhat - m1 - xhat * m2)


def _row_tile(s):
    return min(512, s)


def _acc_rows(ref, val, first):
    @pl.when(first)
    def _():
        ref[...] = jnp.zeros_like(ref)
    ref[...] += jnp.sum(val, axis=0, keepdims=True)


def _ln_mod(x, sc, sh, name):
    s, d = x.shape
    tm = _row_tile(s)

    def body(x_ref, sc_ref, sh_ref, u_ref):
        xhat, _ = _ln_hat(x_ref[...])
        u_ref[...] = (xhat * (1.0 + sc_ref[...]) + sh_ref[...]).astype(BF16)

    row = pl.BlockSpec((tm, d), lambda i: (i, 0))
    vec = pl.BlockSpec((1, d), lambda i: (0, 0))
    return pl.pallas_call(body, grid=(s // tm,), in_specs=[row, vec, vec], out_specs=row,
                          out_shape=jax.ShapeDtypeStruct((s, d), BF16), name=name, compiler_params=_cparams())(x, sc, sh)


def _ln_mod_bwd(du, x, sc, dres, name):
    s, d = x.shape
    tm = _row_tile(s)

    def body(du_ref, x_ref, sc_ref, dres_ref, dx_ref, dsc_ref, dsh_ref):
        first = pl.program_id(0) == 0
        duv = du_ref[...]
        xhat, rstd = _ln_hat(x_ref[...])
        dx_ref[...] = dres_ref[...] + _ln_hat_bwd(duv * (1.0 + sc_ref[...]), xhat, rstd)
        _acc_rows(dsc_ref, duv * xhat, first)
        _acc_rows(dsh_ref, duv, first)

    row = pl.BlockSpec((tm, d), lambda i: (i, 0))
    vec = pl.BlockSpec((1, d), lambda i: (0, 0))
    vs = jax.ShapeDtypeStruct((1, d), F32)
    return pl.pallas_call(body, grid=(s // tm,), in_specs=[row, row, vec, row], out_specs=[row, vec, vec],
                          out_shape=[jax.ShapeDtypeStruct((s, d), F32), vs, vs], name=name,
                          compiler_params=_cparams())(du, x, sc, dres)


def _resid_ln(x, f, g, gam, bet, name):
    s, d = x.shape
    tm = _row_tile(s)

    def body(x_ref, f_ref, g_ref, gam_ref, bet_ref, o_ref):
        rhat, _ = _ln_hat(ALPHA * x_ref[...] + g_ref[...] * f_ref[...])
        o_ref[...] = rhat * gam_ref[...] + bet_ref[...]

    row = pl.BlockSpec((tm, d), lambda i: (i, 0))
    vec = pl.BlockSpec((1, d), lambda i: (0, 0))
    return pl.pallas_call(body, grid=(s // tm,), in_specs=[row, row, vec, vec, vec], out_specs=row,
                          out_shape=jax.ShapeDtypeStruct((s, d), F32), name=name, compiler_params=_cparams())(x, f, g, gam, bet)


def _resid_ln_bwd(dxo, x, f, g, gam, name):
    s, d = x.shape
    tm = _row_tile(s)

    def body(dxo_ref, x_ref, f_ref, g_ref, gam_ref, dres_ref, df_ref, dgam_ref, dbet_ref, dg_ref, dbias_ref):
        first = pl.program_id(0) == 0
        dxov = dxo_ref[...]
        fv = f_ref[...]
        rhat, rstd = _ln_hat(ALPHA * x_ref[...] + g_ref[...] * fv)
        dr = _ln_hat_bwd(dxov * gam_ref[...], rhat, rstd)
        dfv = g_ref[...] * dr
        dres_ref[...] = ALPHA * dr
        df_ref[...] = dfv.astype(BF16)
        _acc_rows(dgam_ref, dxov * rhat, first)
        _acc_rows(dbet_ref, dxov, first)
        _acc_rows(dg_ref, dr * fv, first)
        _acc_rows(dbias_ref, dfv, first)

    row = pl.BlockSpec((tm, d), lambda i: (i, 0))
    vec = pl.BlockSpec((1, d), lambda i: (0, 0))
    vs = jax.ShapeDtypeStruct((1, d), F32)
    return pl.pallas_call(body, grid=(s // tm,), in_specs=[row, row, row, vec, vec],
                          out_specs=[row, row, vec, vec, vec, vec],
                          out_shape=[jax.ShapeDtypeStruct((s, d), F32), jax.ShapeDtypeStruct((s, d), BF16), vs, vs, vs, vs],
                          name=name, compiler_params=_cparams())(dxo, x, f, g, gam)


def _loss_grad(y, tgt, name):
    s, d = y.shape
    tm = _row_tile(s)
    n = s // tm

    def body(y_ref, t_ref, dy_ref, loss_ref, acc_ref):
        i = pl.program_id(0)
        e = y_ref[...] - t_ref[...]
        dy_ref[...] = e * (1.0 / d)
        _acc_rows(acc_ref, e * e, i == 0)

        @pl.when(i == n - 1)
        def _():
            tot = jnp.sum(acc_ref[...], axis=1, keepdims=True) * (0.5 / d)
            loss_ref[...] = jnp.broadcast_to(tot, (1, 128))

    row = pl.BlockSpec((tm, d), lambda i: (i, 0))
    return pl.pallas_call(body, grid=(n,), in_specs=[row, row],
                          out_specs=[row, pl.BlockSpec((1, 128), lambda i: (0, 0))],
                          out_shape=[jax.ShapeDtypeStruct((s, d), F32), jax.ShapeDtypeStruct((1, 128), F32)],
                          scratch_shapes=[pltpu.VMEM((1, d), F32)], name=name, compiler_params=_cparams())(y, tgt)


POOL_HALO = 16
POOL_ROWS = 256


def _pool_counts(r0, rows):
    t1 = (lax.broadcasted_iota(jnp.int32, (rows, 128), 0) + r0 + 1).astype(F32)
    low = lax.broadcasted_iota(jnp.int32, (rows, 128), 1) < POOL_GROUP
    wa = jnp.where(low, float(POOL_WINDOWS[0]), float(POOL_WINDOWS[1]))
    wb = jnp.where(low, float(POOL_WINDOWS[2]), float(POOL_WINDOWS[3]))
    return jnp.minimum(t1, wa), jnp.minimum(t1, wb), low


def _window_sums(win, off, rows, sign):
    def sl(j, half):
        return win[off + sign * j: off + sign * j + rows, 128 * half:128 * half + 128]
    a2 = sl(0, 0) + sl(1, 0)
    a4 = a2 + sl(2, 0) + sl(3, 0)
    a8 = sl(0, 1)
    for j in range(1, 8):
        a8 = a8 + sl(j, 1)
    a16 = a8
    for j in range(8, 16):
        a16 = a16 + sl(j, 1)
    return a2, a4, a8, a16


def _pool_fwd(zp, wp_bd, pscale, name):
    s = zp.shape[0]
    r = min(POOL_ROWS, s)

    def body(z_ref, wp_ref, sc_ref, p_ref, feat_ref, pad):
        pad[0:POOL_HALO, :] = jnp.zeros((POOL_HALO, D_POOL), F32)
        pad[POOL_HALO:, :] = z_ref[...]

        def step(i, carry):
            r0 = pl.multiple_of(i * r, r)
            win = pad[pl.ds(r0, r + POOL_HALO), :]
            a2, a4, a8, a16 = _window_sums(win, POOL_HALO, r, -1)
            ca, cb, low = _pool_counts(r0, r)
            x0 = win[POOL_HALO:, :]
            pa = jnp.where(low, a2, a4) / ca
            pb = jnp.where(low, a8, a16) / cb
            p = (jnp.concatenate([pa, pb], axis=1) - x0).astype(BF16)
            p_ref[pl.ds(r0, r), :] = p
            pw = jnp.dot(p, wp_ref[...], preferred_element_type=F32)
            feat_ref[pl.ds(r0, r), :] = (pw * sc_ref[...]).astype(BF16)
            return carry

        lax.fori_loop(0, s // r, step, 0)

    return pl.pallas_call(
        body, out_shape=[jax.ShapeDtypeStruct((s, D_POOL), BF16), jax.ShapeDtypeStruct((s, D_POOL), BF16)],
        scratch_shapes=[pltpu.VMEM((s + POOL_HALO, D_POOL), F32)], name=name, compiler_params=_cparams(),
    )(zp, wp_bd, pscale)


def _pool_bwd(dfeat, p, wp_bd, pscale, name):
    s = p.shape[0]
    r = min(POOL_ROWS, s)

    def body(df_ref, p_ref, wp_ref, sc_ref, dz_ref, dwp_ref, dsc_ref, gpad, dpbuf):
        dwp_ref[...] = jnp.zeros_like(dwp_ref)
        dsc_ref[...] = jnp.zeros_like(dsc_ref)
        gpad[s:, :] = jnp.zeros((POOL_HALO, D_POOL), F32)

        def step1(i, carry):
            r0 = pl.multiple_of(i * r, r)
            pv = p_ref[pl.ds(r0, r), :]
            dfv = df_ref[pl.ds(r0, r), :]
            pw = jnp.dot(pv, wp_ref[...], preferred_element_type=F32)
            dsc_ref[...] += jnp.sum(dfv * pw, axis=0, keepdims=True)
            dpw = (dfv * sc_ref[...]).astype(BF16)
            dwp_ref[...] += lax.dot_general(pv, dpw, _DIMS["tn"], preferred_element_type=F32)
            dp = lax.dot_general(dpw, wp_ref[...], _DIMS["nt"], preferred_element_type=F32)
            ca, cb, _ = _pool_counts(r0, r)
            gpad[pl.ds(r0, r), :] = dp / jnp.concatenate([ca, cb], axis=1)
            dpbuf[pl.ds(r0, r), :] = dp
            return carry

        lax.fori_loop(0, s // r, step1, 0)

        def step2(i, carry):
            r0 = pl.multiple_of(i * r, r)
            win = gpad[pl.ds(r0, r + POOL_HALO), :]
            a2, a4, a8, a16 = _window_sums(win, 0, r, 1)
            low = lax.broadcasted_iota(jnp.int32, (r, 128), 1) < POOL_GROUP
            acc = jnp.concatenate([jnp.where(low, a2, a4), jnp.where(low, a8, a16)], axis=1)
            dz_ref[pl.ds(r0, r), :] = (acc - dpbuf[pl.ds(r0, r), :]).astype(BF16)
            return carry

        lax.fori_loop(0, s // r, step2, 0)

    return pl.pallas_call(
        body,
        out_shape=[jax.ShapeDtypeStruct((s, D_POOL), BF16), jax.ShapeDtypeStruct((D_POOL, D_POOL), F32),
                   jax.ShapeDtypeStruct((1, D_POOL), F32)],
        scratch_shapes=[pltpu.VMEM((s + POOL_HALO, D_POOL), F32), pltpu.VMEM((s, D_POOL), F32)],
        name=name, compiler_params=_cparams(),
    )(dfeat, p, wp_bd, pscale)


def _skew_index():
    cp = lax.broadcasted_iota(jnp.int32, (SKEW_W, N_REL), 0)
    dist = jnp.where(cp < KW, KPAD - cp, KPAD + SKEW_W - cp)
    idx = jnp.clip(dist, -REL_CLIP, REL_CLIP) + REL_CLIP
    return (idx == lax.broadcasted_iota(jnp.int32, (SKEW_W, N_REL), 1)).astype(F32)


def _row_bits(b):
    return (lax.broadcasted_iota(jnp.int32, (QB, SKEW_W), 0) >> b) & 1 == 1


def _bias_block(rel_bias, name):
    def body(rb_ref, o_ref):
        onehot = _skew_index()
        row0 = lax.dot_general(rb_ref[...], onehot, _DIMS["nt"], precision=lax.Precision.HIGHEST,
                               preferred_element_type=F32)
        r = lax.broadcasted_iota(jnp.int32, (QB, KW), 0)
        kk = lax.broadcasted_iota(jnp.int32, (QB, KW), 1)
        cq, ck = r // CHUNK, kk // CHUNK
        band = (ck >= cq) & (ck <= cq + N_PREV_CHUNKS)
        for h in range(N_HEADS):
            t = jnp.broadcast_to(row0[h:h + 1, :], (QB, SKEW_W))
            for b in range(7):
                t = jnp.where(_row_bits(b), pltpu.roll(t, 1 << b, 1), t)
            o_ref[h] = jnp.where(band, t[:, :KW], NEG_INF)

    return pl.pallas_call(body, out_shape=jax.ShapeDtypeStruct((N_HEADS, QB, KW), F32), name=name,
                          compiler_params=_cparams())(rel_bias)


def _bias_block_bwd(ds_acc, name):
    def body(ds_ref, o_ref):
        sums = []
        for h in range(N_HEADS):
            t = jnp.concatenate([ds_ref[h], jnp.zeros((QB, SKEW_W - KW), F32)], axis=1)
            for b in range(7):
                t = jnp.where(_row_bits(b), pltpu.roll(t, SKEW_W - (1 << b), 1), t)
            sums.append(jnp.sum(t, axis=0, keepdims=True))
        allh = jnp.concatenate(sums, axis=0)
        o_ref[...] = jnp.dot(allh, _skew_index(), precision=lax.Precision.HIGHEST, preferred_element_type=F32)

    return pl.pallas_call(body, out_shape=jax.ShapeDtypeStruct((N_HEADS, N_REL), F32), name=name,
                          compiler_params=_cparams())(ds_acc)


def _scores(qh, kh, bias_h, valid):
    sc = lax.dot_general(qh, kh, _DIMS["nt"], preferred_element_type=F32) * (HEAD_DIM ** -0.5) + bias_h
    sc = jnp.where(valid, sc, NEG_INF)
    e = jnp.exp(sc - jnp.max(sc, axis=-1, keepdims=True))
    return e * (1.0 / jnp.sum(e, axis=-1, keepdims=True))


def _load_padded_kv(qkv_hbm, kpad, vpad, sems, s):
    kpad[0:KPAD, :] = jnp.zeros((KPAD, D_ATTN), BF16)
    vpad[0:KPAD, :] = jnp.zeros((KPAD, D_ATTN), BF16)
    ck = pltpu.make_async_copy(qkv_hbm.at[:, D_ATTN:2 * D_ATTN], kpad.at[pl.ds(KPAD, s), :], sems.at[0])
    cv = pltpu.make_async_copy(qkv_hbm.at[:, 2 * D_ATTN:3 * D_ATTN], vpad.at[pl.ds(KPAD, s), :], sems.at[1])
    ck.start()
    cv.start()
    ck.wait()
    cv.wait()


def _attn_fwd(qkv, bias, name, rider=None):
    s = qkv.shape[0]

    def body(q_ref, qkv_hbm, bias_ref, o_ref, kpad, vpad, sems):
        i = pl.program_id(0)

        @pl.when(i == 0)
        def _():
            _load_padded_kv(qkv_hbm, kpad, vpad, sems, s)

        base = pl.multiple_of(i * QB, QB)
        kw = kpad[pl.ds(base, KW), :]
        vw = vpad[pl.ds(base, KW), :]
        q = q_ref[...]
        valid = lax.broadcasted_iota(jnp.int32, (QB, KW), 1) >= KPAD - base
        outs = []
        for h in range(N_HEADS):
            hs = slice(HEAD_DIM * h, HEAD_DIM * (h + 1))
            p = _scores(q[:, hs], kw[:, hs], bias_ref[h], valid)
            outs.append(jnp.dot(p.astype(BF16), vw[:, hs], preferred_element_type=F32))
        o_ref[...] = jnp.concatenate(outs, axis=1).astype(BF16)

    res = _call(
        body, name=name, grid=(s // QB,),
        in_specs=[pl.BlockSpec((QB, D_ATTN), lambda i: (i, 0)), pl.BlockSpec(memory_space=pl.ANY),
                  _full((N_HEADS, QB, KW))],
        out_specs=[pl.BlockSpec((QB, D_ATTN), lambda i: (i, 0))],
        out_shape=[jax.ShapeDtypeStruct((s, D_ATTN), BF16)],
        scratch_shapes=[pltpu.VMEM((s + KPAD, D_ATTN), BF16), pltpu.VMEM((s + KPAD, D_ATTN), BF16),
                        pltpu.SemaphoreType.DMA((2,))],
        args=(qkv, qkv, bias), rider=rider)
    return res[0] if rider is None else (res[0][0], res[1])


def _attn_bwd(qkv, do, bias, name, rider=None):
    s = qkv.shape[0]
    n = s // QB

    def body(q_ref, qkv_hbm, do_ref, bias_ref, dq_ref, dk_hbm, dv_hbm, ds_ref, kpad, vpad, dkacc, dvacc, sems):
        i = pl.program_id(0)

        @pl.when(i == 0)
        def _():
            _load_padded_kv(qkv_hbm, kpad, vpad, sems, s)
            dkacc[...] = jnp.zeros_like(dkacc)
            dvacc[...] = jnp.zeros_like(dvacc)
            ds_ref[...] = jnp.zeros_like(ds_ref)

        base = pl.multiple_of(i * QB, QB)
        kw = kpad[pl.ds(base, KW), :]
        vw = vpad[pl.ds(base, KW), :]
        q = q_ref[...]
        dov = do_ref[...]
        valid = lax.broadcasted_iota(jnp.int32, (QB, KW), 1) >= KPAD - base
        dqs, dks, dvs = [], [], []
        for h in range(N_HEADS):
            hs = slice(HEAD_DIM * h, HEAD_DIM * (h + 1))
            qh, kh, vh, doh = q[:, hs], kw[:, hs], vw[:, hs], dov[:, hs]
            p = _scores(qh, kh, bias_ref[h], valid)
            dvs.append(lax.dot_general(p.astype(BF16), doh, _DIMS["tn"], preferred_element_type=F32))
            dp = lax.dot_general(doh, vh, _DIMS["nt"], preferred_element_type=F32)
            ds = p * (dp - jnp.sum(dp * p, axis=-1, keepdims=True))
            ds_ref[h] += ds
            dsb = ds.astype(BF16)
            dqs.append(jnp.dot(dsb, kh, preferred_element_type=F32) * (HEAD_DIM ** -0.5))
            dks.append(lax.dot_general(dsb, qh, _DIMS["tn"], preferred_element_type=F32) * (HEAD_DIM ** -0.5))
        dq_ref[...] = jnp.concatenate(dqs, axis=1).astype(BF16)
        dkacc[pl.ds(base, KW), :] += jnp.concatenate(dks, axis=1)
        dvacc[pl.ds(base, KW), :] += jnp.concatenate(dvs, axis=1)

        @pl.when(i == n - 1)
        def _():
            ck = pltpu.make_async_copy(dkacc, dk_hbm, sems.at[0])
            cv = pltpu.make_async_copy(dvacc, dv_hbm, sems.at[1])
            ck.start()
            cv.start()
            ck.wait()
            cv.wait()

    blk = pl.BlockSpec((QB, D_ATTN), lambda i: (i, 0))
    acc_shape = jax.ShapeDtypeStruct((s + KPAD, D_ATTN), F32)
    return _call(
        body, name=name, grid=(n,),
        in_specs=[blk, pl.BlockSpec(memory_space=pl.ANY), blk, _full((N_HEADS, QB, KW))],
        out_specs=[blk, pl.BlockSpec(memory_space=pl.ANY), pl.BlockSpec(memory_space=pl.ANY), _full((N_HEADS, QB, KW))],
        out_shape=[jax.ShapeDtypeStruct((s, D_ATTN), BF16), acc_shape, acc_shape,
                   jax.ShapeDtypeStruct((N_HEADS, QB, KW), F32)],
        scratch_shapes=[pltpu.VMEM((s + KPAD, D_ATTN), BF16), pltpu.VMEM((s + KPAD, D_ATTN), BF16),
                        pltpu.VMEM((s + KPAD, D_ATTN), F32), pltpu.VMEM((s + KPAD, D_ATTN), F32),
                        pltpu.SemaphoreType.DMA((2,))],
        args=(qkv, qkv, do, bias), rider=rider)


CONV_HALO = 32
CONV_ROWS = 64


def _sigmoid(t):
    return 1.0 / (1.0 + jnp.exp(-t))


def _glu_rows(z_ref, r0, rows):
    a = z_ref[pl.ds(r0, rows), 0:D_CONV]
    b = z_ref[pl.ds(r0, rows), D_CONV:2 * D_CONV]
    return a, _sigmoid(b)


def _conv_fwd(zc, conv_w, conv_b, ln_g, ln_b, name):
    s = zc.shape[0]
    rt = min(256, s)

    def body(z_ref, w_ref, cb_ref, g_ref, b_ref, cv_ref, feat_ref, hpad):
        hpad[0:CONV_HALO, :] = jnp.zeros((CONV_HALO, D_CONV), F32)

        def glu(i, carry):
            r0 = pl.multiple_of(i * rt, rt)
            a, sb = _glu_rows(z_ref, r0, rt)
            hpad[pl.ds(r0 + CONV_HALO, rt), :] = a * sb
            return carry

        lax.fori_loop(0, s // rt, glu, 0)
        w = w_ref[...]

        def conv(i, carry):
            r0 = pl.multiple_of(i * CONV_ROWS, CONV_ROWS)
            win = hpad[pl.ds(r0, CONV_ROWS + CONV_HALO), :]
            acc = jnp.broadcast_to(cb_ref[...], (CONV_ROWS, D_CONV))
            for k in range(CONV_WIDTH):
                acc = acc + win[2 + k:2 + k + CONV_ROWS, :] * w[k:k + 1, :]
            cv_ref[pl.ds(r0, CONV_ROWS), :] = acc
            yhat, _ = _ln_hat(acc)
            y = yhat * g_ref[...] + b_ref[...]
            feat_ref[pl.ds(r0, CONV_ROWS), :] = (y * _sigmoid(y)).astype(BF16)
            return carry

        lax.fori_loop(0, s // CONV_ROWS, conv, 0)

    return pl.pallas_call(
        body, out_shape=[jax.ShapeDtypeStruct((s, D_CONV), F32), jax.ShapeDtypeStruct((s, D_CONV), BF16)],
        scratch_shapes=[pltpu.VMEM((s + CONV_HALO, D_CONV), F32)], name=name, compiler_params=_cparams(),
    )(zc, conv_w, conv_b, ln_g, ln_b)


def _conv_bwd(dfeat, cv, zc, conv_w, ln_g, ln_b, name):
    s = zc.shape[0]
    rt = min(256, s)

    def body(df_ref, cv_ref, z_ref, w_ref, g_ref, b_ref, dz_ref, dw_ref, dcb_ref, dg_ref, db_ref, hpad, dcvpad, dwacc):
        hpad[0:CONV_HALO, :] = jnp.zeros((CONV_HALO, D_CONV), F32)
        dcvpad[s:, :] = jnp.zeros((CONV_HALO, D_CONV), F32)
        dwacc[...] = jnp.zeros_like(dwacc)
        dcb_ref[...] = jnp.zeros_like(dcb_ref)
        dg_ref[...] = jnp.zeros_like(dg_ref)
        db_ref[...] = jnp.zeros_like(db_ref)

        def pass1(i, carry):
            r0 = pl.multiple_of(i * rt, rt)
            a, sb = _glu_rows(z_ref, r0, rt)
            hpad[pl.ds(r0 + CONV_HALO, rt), :] = a * sb
            cvhat, rstd = _ln_hat(cv_ref[pl.ds(r0, rt), :])
            y = cvhat * g_ref[...] + b_ref[...]
            sg = _sigmoid(y)
            dy = df_ref[pl.ds(r0, rt), :] * (sg * (1.0 + y * (1.0 - sg)))
            dg_ref[...] += jnp.sum(dy * cvhat, axis=0, keepdims=True)
            db_ref[...] += jnp.sum(dy, axis=0, keepdims=True)
            dcv = _ln_hat_bwd(dy * g_ref[...], cvhat, rstd)
            dcb_ref[...] += jnp.sum(dcv, axis=0, keepdims=True)
            dcvpad[pl.ds(r0, rt), :] = dcv
            return carry

        lax.fori_loop(0, s // rt, pass1, 0)
        w = w_ref[...]

        def pass2(i, carry):
            r0 = pl.multiple_of(i * CONV_ROWS, CONV_ROWS)
            dwin = dcvpad[pl.ds(r0, CONV_ROWS + CONV_HALO), :]
            hwin = hpad[pl.ds(r0, CONV_ROWS + CONV_HALO), :]
            dcv = dwin[0:CONV_ROWS, :]
            dh = jnp.zeros((CONV_ROWS, D_CONV), F32)
            for k in range(CONV_WIDTH):
                dh = dh + dwin[30 - k:30 - k + CONV_ROWS, :] * w[k:k + 1, :]
                prod = dcv * hwin[2 + k:2 + k + CONV_ROWS, :]
                dwacc[8 * k:8 * k + 8, :] += jnp.sum(prod.reshape(CONV_ROWS // 8, 8, D_CONV), axis=0)
            a, sb = _glu_rows(z_ref, r0, CONV_ROWS)
            dz_ref[pl.ds(r0, CONV_ROWS), :] = jnp.concatenate([dh * sb, dh * a * sb * (1.0 - sb)], axis=1).astype(BF16)
            return carry

        lax.fori_loop(0, s // CONV_ROWS, pass2, 0)
        dw_ref[...] = jnp.sum(dwacc[...].reshape(32, 8, D_CONV), axis=1)

    vs = jax.ShapeDtypeStruct((1, D_CONV), F32)
    return pl.pallas_call(
        body,
        out_shape=[jax.ShapeDtypeStruct((s, 2 * D_CONV), BF16), jax.ShapeDtypeStruct((32, D_CONV), F32), vs, vs, vs],
        scratch_shapes=[pltpu.VMEM((s + CONV_HALO, D_CONV), F32), pltpu.VMEM((s + CONV_HALO, D_CONV), F32),
                        pltpu.VMEM((256, D_CONV), F32)],
        name=name, compiler_params=_cparams(),
    )(dfeat, cv, zc, conv_w, ln_g, ln_b)


def _merge(zg, b_gate, ys, name):
    s = zg.shape[0]
    tm = _row_tile(s)

    def body(zg_ref, bg_ref, y0_ref, y1_ref, y2_ref, o_ref):
        acc = None
        for j, y_ref in enumerate((y0_ref, y1_ref, y2_ref)):
            cs = slice(D_MODEL * j, D_MODEL * (j + 1))
            t = _sigmoid(zg_ref[:, cs] + bg_ref[:, cs]) * y_ref[...]
            acc = t if acc is None else acc + t
        o_ref[...] = acc.astype(BF16)

    row = pl.BlockSpec((tm, D_MODEL), lambda i: (i, 0))
    return pl.pallas_call(
        body, grid=(s // tm,),
        in_specs=[pl.BlockSpec((tm, 3 * D_MODEL), lambda i: (i, 0)), _full((1, 3 * D_MODEL)), row, row, row],
        out_specs=row, out_shape=jax.ShapeDtypeStruct((s, D_MODEL), BF16), name=name, compiler_params=_cparams(),
    )(zg, b_gate, *ys)


def _merge_bwd(dm, zg, b_gate, ys, name):
    s = zg.shape[0]
    tm = min(256, s)

    def body(dm_ref, zg_ref, bg_ref, y0_ref, y1_ref, y2_ref, d0_ref, d1_ref, d2_ref, dzg_ref, dbg_ref):
        first = pl.program_id(0) == 0

        @pl.when(first)
        def _():
            dbg_ref[...] = jnp.zeros_like(dbg_ref)

        dmv = dm_ref[...]
        for j, (y_ref, d_ref) in enumerate(((y0_ref, d0_ref), (y1_ref, d1_ref), (y2_ref, d2_ref))):
            cs = slice(D_MODEL * j, D_MODEL * (j + 1))
            g = _sigmoid(zg_ref[:, cs] + bg_ref[:, cs])
            d_ref[...] = (dmv * g).astype(BF16)
            dzg = dmv * y_ref[...] * g * (1.0 - g)
            dzg_ref[:, cs] = dzg.astype(BF16)
            dbg_ref[:, cs] += jnp.sum(dzg, axis=0, keepdims=True)

    row = pl.BlockSpec((tm, D_MODEL), lambda i: (i, 0))
    wide = pl.BlockSpec((tm, 3 * D_MODEL), lambda i: (i, 0))
    yb = jax.ShapeDtypeStruct((s, D_MODEL), BF16)
    return pl.pallas_call(
        body, grid=(s // tm,),
        in_specs=[row, wide, _full((1, 3 * D_MODEL)), row, row, row],
        out_specs=[row, row, row, wide, _full((1, 3 * D_MODEL))],
        out_shape=[yb, yb, yb, jax.ShapeDtypeStruct((s, 3 * D_MODEL), BF16), jax.ShapeDtypeStruct((1, 3 * D_MODEL), F32)],
        name=name, compiler_params=_cparams(),
    )(dm, zg, b_gate, *ys)


def _ff_hidden_bwd(dff, w_ff2, hpre, name):
    s = dff.shape[0]
    tm, tn = min(512, s), 1024

    def body(a_ref, b_ref, h_ref, o_ref, sum_ref):
        dh = lax.dot_general(a_ref[...], b_ref[...], _DIMS["nt"], preferred_element_type=F32)
        dpre = dh * (2.0 * jnp.maximum(h_ref[...], 0.0))
        o_ref[...] = dpre.astype(BF16)
        _acc_rows(sum_ref, dpre, pl.program_id(1) == 0)

    return pl.pallas_call(
        body, grid=(D_FF // tn, s // tm),
        in_specs=[pl.BlockSpec((tm, D_MODEL), lambda j, i: (i, 0)), pl.BlockSpec((tn, D_MODEL), lambda j, i: (j, 0)),
                  pl.BlockSpec((tm, tn), lambda j, i: (i, j))],
        out_specs=[pl.BlockSpec((tm, tn), lambda j, i: (i, j)), pl.BlockSpec((1, tn), lambda j, i: (0, j))],
        out_shape=[jax.ShapeDtypeStruct((s, D_FF), BF16), jax.ShapeDtypeStruct((1, D_FF), F32)],
        name=name, compiler_params=_cparams(),
    )(dff, w_ff2, hpre)


def _silu(t):
    return t * _sigmoid(t)


def _mod_fwd(c_all, w_ada_sh, b_ada_sh, name):
    cols = w_ada_sh.shape[2]

    def body(c_ref, w_ref, b_ref, o_ref):
        ca = _silu(c_ref[...]).astype(BF16)
        o_ref[0] = jnp.dot(ca, w_ref[0].astype(BF16), preferred_element_type=F32) + b_ref[0]

    return pl.pallas_call(
        body, grid=(DEPTH,),
        in_specs=[_full((N_DEV, D_MODEL)), pl.BlockSpec((1, D_MODEL, cols), lambda l: (l, 0, 0)),
                  pl.BlockSpec((1, 1, cols), lambda l: (l, 0, 0))],
        out_specs=pl.BlockSpec((1, N_DEV, cols), lambda l: (l, 0, 0)),
        out_shape=jax.ShapeDtypeStruct((DEPTH, N_DEV, cols), F32), name=name, compiler_params=_cparams(),
    )(c_all, w_ada_sh, b_ada_sh)


def _mod_bwd(c_all, dmod_sh, name):
    cols = dmod_sh.shape[2]

    def body(c_ref, d_ref, o_ref):
        ca = _silu(c_ref[...])
        o_ref[0] = lax.dot_general(ca, d_ref[0], _DIMS["tn"], precision=lax.Precision.HIGHEST,
                                   preferred_element_type=F32)

    return pl.pallas_call(
        body, grid=(DEPTH,),
        in_specs=[_full((N_DEV, D_MODEL)), pl.BlockSpec((1, N_DEV, cols), lambda l: (l, 0, 0))],
        out_specs=pl.BlockSpec((1, D_MODEL, cols), lambda l: (l, 0, 0)),
        out_shape=jax.ShapeDtypeStruct((DEPTH, D_MODEL, cols), F32), name=name, compiler_params=_cparams(),
    )(c_all, dmod_sh)


def _flat_tiles(rows, cols, itemsize_total):
    budget = 12 * 1024 * 1024
    tr = rows
    while tr % 32 == 0 and tr * cols * itemsize_total > budget:
        tr //= 2
    return tr


def _sum_cores(dw, recv, place, name):
    _, m, n = dw.shape
    tr = _flat_tiles(m, n, 6)

    def body(place_ref, a_ref, b_ref, o_ref):
        o_ref[...] = (a_ref[...].astype(F32) + b_ref[...].astype(F32)).astype(BF16)

    grid_spec = pltpu.PrefetchScalarGridSpec(
        num_scalar_prefetch=1, grid=(m // tr,),
        in_specs=[pl.BlockSpec((None, tr, n), lambda i, pr: (pr[0], i, 0)), pl.BlockSpec((tr, n), lambda i, pr: (i, 0))],
        out_specs=pl.BlockSpec((tr, n), lambda i, pr: (i, 0)))
    return pl.pallas_call(body, grid_spec=grid_spec, out_shape=jax.ShapeDtypeStruct((m, n), BF16), name=name,
                          compiler_params=_cparams())(place, dw, recv)


def _sum_chips(h, r, place, name):
    _, rs, n = h.shape
    tr = _flat_tiles(rs, n, 12)

    def body(place_ref, h_ref, r_ref, o_ref):
        o_ref[...] = ((h_ref[...].astype(F32) + r_ref[0].astype(F32)) + r_ref[1].astype(F32)) + r_ref[2].astype(F32)

    grid_spec = pltpu.PrefetchScalarGridSpec(
        num_scalar_prefetch=1, grid=(rs // tr,),
        in_specs=[pl.BlockSpec((None, tr, n), lambda i, pr: (pr[1], i, 0)), pl.BlockSpec((3, tr, n), lambda i, pr: (0, i, 0))],
        out_specs=pl.BlockSpec((tr, n), lambda i, pr: (i, 0)))
    return pl.pallas_call(body, grid_spec=grid_spec, out_shape=jax.ShapeDtypeStruct((rs, n), F32), name=name,
                          compiler_params=_cparams())(place, h, r)


def _adam_math(w, g, m, v):
    m2 = ADAM_B1 * m + (1.0 - ADAM_B1) * g
    v2 = ADAM_B2 * v + (1.0 - ADAM_B2) * (g * g)
    m_hat = m2 / (1.0 - ADAM_B1 ** ADAM_STEP)
    v_hat = v2 / (1.0 - ADAM_B2 ** ADAM_STEP)
    delta = -ADAM_LR * (m_hat / (jnp.sqrt(v_hat) + ADAM_EPS) + ADAM_WD * w)
    return delta, m2, v2


def _adamw(w, m, v, grads, name):
    r, c = w.shape
    tr = _flat_tiles(r, c, 4 * (7 + len(grads)))

    def body(*refs):
        w_ref, m_ref, v_ref = refs[:3]
        g_refs = refs[3:3 + len(grads)]
        g_ref, d_ref, m2_ref, v2_ref = refs[3 + len(grads):]
        g = g_refs[0][...]
        for gr in g_refs[1:]:
            g = g + gr[...]
        delta, m2, v2 = _adam_math(w_ref[...], g, m_ref[...], v_ref[...])
        g_ref[...] = g
        d_ref[...] = delta
        m2_ref[...] = m2
        v2_ref[...] = v2

    blk = pl.BlockSpec((tr, c), lambda i: (i, 0))
    sh = jax.ShapeDtypeStruct((r, c), F32)
    return pl.pallas_call(body, grid=(r // tr,), in_specs=[blk] * (3 + len(grads)), out_specs=[blk] * 4,
                          out_shape=[sh] * 4, name=name, compiler_params=_cparams())(w, m, v, *grads)


def _adamw_halves(w, m, v, own, other, place, split, name):
    nl, r, c = w.shape
    hr, hc = own[0].shape
    tr = _flat_tiles(hr, hc, 4 * (7 + 2 * nl))
    nt = hr // tr
    if split == "rows":
        w_spec = pl.BlockSpec((None, tr, c), lambda l, h, t, pr: (l, h * nt + t, 0))
    else:
        w_spec = pl.BlockSpec((None, tr, hc), lambda l, h, t, pr: (l, t, h))

    def g_spec(layer):
        return pl.BlockSpec((tr, hc), lambda l, h, t, pr: (jnp.where(l == layer, t, nt - 1), 0))

    def body(place_ref, w_ref, m_ref, v_ref, *refs):
        own_refs, other_refs = refs[:nl], refs[nl:2 * nl]
        g_ref, d_ref, m2_ref, v2_ref = refs[2 * nl:]
        layer = pl.program_id(0)
        mine = pl.program_id(1) == place_ref[0]
        g = None
        for li in range(nl):
            cand = jnp.where(mine, own_refs[li][...], other_refs[li][...])
            g = cand if g is None else jnp.where(layer == li, cand, g)
        delta, m2, v2 = _adam_math(w_ref[...], g, m_ref[...], v_ref[...])
        g_ref[...] = g
        d_ref[...] = delta
        m2_ref[...] = m2
        v2_ref[...] = v2

    grid_spec = pltpu.PrefetchScalarGridSpec(
        num_scalar_prefetch=1, grid=(nl, 2, nt),
        in_specs=[w_spec] * 3 + [g_spec(li) for li in range(nl)] * 2, out_specs=[w_spec] * 4)
    sh = jax.ShapeDtypeStruct((nl, r, c), F32)
    return pl.pallas_call(body, grid_spec=grid_spec, out_shape=[sh] * 4, name=name,
                          compiler_params=_cparams())(place, w, m, v, *own, *other)


def _adamw_small(w, m, v, g_all, name):
    r, c = w.shape

    def body(w_ref, m_ref, v_ref, g_ref, go_ref, d_ref, m2_ref, v2_ref):
        g = g_ref[0]
        for b in range(1, N_DEV):
            g = g + g_ref[b]
        delta, m2, v2 = _adam_math(w_ref[...], g, m_ref[...], v_ref[...])
        go_ref[...] = g
        d_ref[...] = delta
        m2_ref[...] = m2
        v2_ref[...] = v2

    sh = jax.ShapeDtypeStruct((r, c), F32)
    return pl.pallas_call(body, out_shape=[sh] * 4, name=name, compiler_params=_cparams())(w, m, v, g_all)


def _me():
    return lax.axis_index("x"), lax.axis_index("y"), lax.axis_index("c")


def _flip(v, bit):
    return 1 - v if bit else v


def _allgather_small(blk, name):
    r, c = blk.shape

    def body(x_ref, o_ref, send_sems, recv_sems):
        x, y, cc = _me()
        me = 4 * x + 2 * y + cc
        copies = []
        for k in range(1, N_DEV):
            peer = (_flip(x, k & 4), _flip(y, k & 2), _flip(cc, k & 1))
            cp = pltpu.make_async_remote_copy(src_ref=x_ref, dst_ref=o_ref.at[me], send_sem=send_sems.at[k - 1],
                                              recv_sem=recv_sems.at[k - 1], device_id=peer, device_id_type=MESH)
            cp.start()
            copies.append(cp)
        o_ref[me] = x_ref[...]
        for cp in copies:
            cp.wait()

    return pl.pallas_call(
        body, out_shape=jax.ShapeDtypeStruct((N_DEV, r, c), F32),
        in_specs=[pl.BlockSpec(memory_space=pltpu.VMEM)], out_specs=pl.BlockSpec(memory_space=pltpu.VMEM),
        scratch_shapes=[pltpu.SemaphoreType.DMA((N_DEV - 1,)), pltpu.SemaphoreType.DMA((N_DEV - 1,))],
        name=name, compiler_params=_cparams(),
    )(blk)


class _Rider:
    def __init__(self, arrays, out_shapes, scratch_shapes, start, finish):
        self.arrays, self.out_shapes, self.scratch_shapes = list(arrays), list(out_shapes), list(scratch_shapes)
        self.start, self.finish = start, finish


def _call(body, *, name, grid, in_specs, out_specs, out_shape, scratch_shapes, args, rider=None):
    if rider is None:
        return pl.pallas_call(body, grid=grid, in_specs=in_specs, out_specs=out_specs, out_shape=out_shape,
                              scratch_shapes=scratch_shapes, name=name, compiler_params=_cparams())(*args)
    ni, no, ns = len(in_specs), len(out_specs), len(scratch_shapes)
    ri, ro = len(rider.arrays), len(rider.out_shapes)
    steps = int(np.prod(grid))

    def wrapped(*refs):
        h_in, r_in = refs[:ni], refs[ni:ni + ri]
        h_out, r_out = refs[ni + ri:ni + ri + no], refs[ni + ri + no:ni + ri + no + ro]
        h_scr, r_scr = refs[ni + ri + no + ro:ni + ri + no + ro + ns], refs[ni + ri + no + ro + ns:]
        step = pl.program_id(0)
        for d in range(1, len(grid)):
            step = step * grid[d] + pl.program_id(d)

        @pl.when(step == 0)
        def _():
            rider.start(r_in, r_out, r_scr)

        body(*h_in, *h_out, *h_scr)

        @pl.when(step == steps - 1)
        def _():
            rider.finish(r_in, r_out, r_scr)

    anyspec = pl.BlockSpec(memory_space=pl.ANY)
    res = pl.pallas_call(
        wrapped, grid=grid, in_specs=list(in_specs) + [anyspec] * ri, out_specs=list(out_specs) + [anyspec] * ro,
        out_shape=list(out_shape) + rider.out_shapes, scratch_shapes=list(scratch_shapes) + rider.scratch_shapes,
        name=name, compiler_params=_cparams())(*args, *rider.arrays)
    return res[:no], res[no:]


def _run_rider(rider, name):
    ri = len(rider.arrays)

    def body(*refs):
        r_in, r_out, r_scr = refs[:ri], refs[ri:ri + len(rider.out_shapes)], refs[ri + len(rider.out_shapes):]
        rider.start(r_in, r_out, r_scr)
        rider.finish(r_in, r_out, r_scr)

    anyspec = pl.BlockSpec(memory_space=pl.ANY)
    return pl.pallas_call(body, in_specs=[anyspec] * ri, out_specs=[anyspec] * len(rider.out_shapes),
                          out_shape=rider.out_shapes, scratch_shapes=rider.scratch_shapes, name=name,
                          compiler_params=_cparams())(*rider.arrays)


def _gather_rider(shards):
    n = len(shards)

    def copies(ins, outs, scr):
        ici_send, ici_recv, d2d_send, d2d_recv, loc_sems = scr[:5]
        stage = scr[5:]
        x, y, cc = _me()
        chip = 2 * x + y
        sibling = (x, y, 1 - cc)
        local, sends, relays = [], [], []
        for j in range(n):
            def rows(ch, h, j=j):
                return outs[j].at[ch, h]

            lc = pltpu.make_async_copy(ins[j], stage[j], loc_sems.at[j])
            local.append((lc, pltpu.make_async_copy(stage[j], outs[j].at[chip], loc_sems.at[n + j])))
            for k in range(1, N_CHIP):
                px, py = _flip(x, k & 2), _flip(y, k & 1)
                pchip = 2 * px + py
                q = 3 * j + k - 1
                out_cp = pltpu.make_async_remote_copy(src_ref=ins[j].at[cc], dst_ref=rows(chip, cc),
                                                      send_sem=ici_send.at[q], recv_sem=ici_recv.at[q],
                                                      device_id=(px, py, cc), device_id_type=MESH)
                sends.append(out_cp)
                arrival = pltpu.make_async_remote_copy(src_ref=rows(pchip, cc), dst_ref=rows(pchip, cc),
                                                       send_sem=ici_send.at[q], recv_sem=ici_recv.at[q],
                                                       device_id=(px, py, cc), device_id_type=MESH)
                forward = pltpu.make_async_remote_copy(src_ref=rows(pchip, cc), dst_ref=rows(pchip, cc),
                                                       send_sem=d2d_send.at[q], recv_sem=d2d_recv.at[q],
                                                       device_id=sibling, device_id_type=MESH)
                from_sibling = pltpu.make_async_remote_copy(src_ref=rows(pchip, 1 - cc), dst_ref=rows(pchip, 1 - cc),
                                                            send_sem=d2d_send.at[q], recv_sem=d2d_recv.at[q],
                                                            device_id=sibling, device_id_type=MESH)
                relays.append((arrival, forward, from_sibling))
        return local, sends, relays

    def start(ins, outs, scr):
        local, sends, _ = copies(ins, outs, scr)
        for lin, _ in local:
            lin.start()
        for cp in sends:
            cp.start()

    def finish(ins, outs, scr):
        local, sends, relays = copies(ins, outs, scr)
        for lin, lout in local:
            lin.wait()
            lout.start()
        for arrival, forward, _ in relays:
            arrival.wait_recv()
            forward.start()
        for cp in sends:
            cp.wait_send()
        for _, forward, from_sibling in relays:
            forward.wait_send()
            from_sibling.wait_recv()
        for _, lout in local:
            lout.wait()

    scratch = [pltpu.SemaphoreType.DMA((3 * n,)), pltpu.SemaphoreType.DMA((3 * n,)), pltpu.SemaphoreType.DMA((3 * n,)),
               pltpu.SemaphoreType.DMA((3 * n,)), pltpu.SemaphoreType.DMA((2 * n,))]
    scratch += [pltpu.VMEM(a.shape, a.dtype) for a in shards]
    return _Rider(shards, [jax.ShapeDtypeStruct((N_CHIP,) + a.shape, a.dtype) for a in shards], scratch, start, finish)


def _sibling_send(arrs, name, other_half=False):
    n = len(arrs)

    def body(*refs):
        ins, outs = refs[:n], refs[n:2 * n]
        send_sems, recv_sems = refs[2 * n:]
        x, y, cc = _me()
        pending = []
        for j in range(n):
            src = ins[j].at[1 - cc] if other_half else ins[j]
            cp = pltpu.make_async_remote_copy(src_ref=src, dst_ref=outs[j], send_sem=send_sems.at[j],
                                              recv_sem=recv_sems.at[j], device_id=(x, y, 1 - cc), device_id_type=MESH)
            cp.start()
            pending.append(cp)
        for cp in pending:
            cp.wait()

    anyspec = pl.BlockSpec(memory_space=pl.ANY)
    return pl.pallas_call(
        body, out_shape=[jax.ShapeDtypeStruct(a.shape[1:] if other_half else a.shape, a.dtype) for a in arrs],
        in_specs=[anyspec] * n, out_specs=[anyspec] * n,
        scratch_shapes=[pltpu.SemaphoreType.DMA((n,)), pltpu.SemaphoreType.DMA((n,))],
        name=name, compiler_params=_cparams(),
    )(*arrs)


def _scatter_rider(arrs):
    n = len(arrs)

    def copies(ins, outs, scr):
        send_sems, recv_sems = scr
        x, y, cc = _me()
        cps = []
        for j in range(n):
            for k in range(1, N_CHIP):
                px, py = _flip(x, k & 2), _flip(y, k & 1)
                cps.append(pltpu.make_async_remote_copy(
                    src_ref=ins[j].at[2 * px + py], dst_ref=outs[j].at[k - 1], send_sem=send_sems.at[3 * j + k - 1],
                    recv_sem=recv_sems.at[3 * j + k - 1], device_id=(px, py, cc), device_id_type=MESH))
        return cps

    def start(ins, outs, scr):
        for cp in copies(ins, outs, scr):
            cp.start()

    def finish(ins, outs, scr):
        for cp in copies(ins, outs, scr):
            cp.wait()

    return _Rider(arrs, [jax.ShapeDtypeStruct((N_CHIP - 1,) + a.shape[1:], a.dtype) for a in arrs],
                  [pltpu.SemaphoreType.DMA((3 * n,)), pltpu.SemaphoreType.DMA((3 * n,))], start, finish)


COL_SHARDED = ("w_in", "w_br_pool", "w_br_attn", "w_br_conv", "w_ff1")
ROW_SHARDED = ("w_o", "w_ff2")
BIG = COL_SHARDED + ROW_SHARDED
SMALL = ("b_ada", "b_gate", "w_pool", "pool_scale", "rel_bias", "conv_w", "conv_b", "conv_ln_g", "conv_ln_b",
         "ln_mix_g", "ln_mix_b", "b_ff1", "b_ff2", "ln_ff_g", "ln_ff_b")
PACK_W = 1024


def _pack(parts):
    rows = []
    for a in parts:
        flat = a.reshape(-1)
        n = -(-flat.shape[0] // PACK_W) * PACK_W
        rows.append(jnp.pad(flat, (0, n - flat.shape[0])).reshape(-1, PACK_W))
    out = jnp.concatenate(rows, axis=0)
    r = -(-out.shape[0] // 8) * 8
    return jnp.pad(out, ((0, r - out.shape[0]), (0, 0)))


def _unpack(packed, shapes):
    out, r0 = [], 0
    for shp in shapes:
        size = int(np.prod(shp))
        nr = -(-size // PACK_W)
        out.append(packed[r0:r0 + nr].reshape(-1)[:size].reshape(shp))
        r0 += nr
    return out


def _hosted(fn, hook, *args, **kw):
    if hook is None:
        return fn(*args, **kw)
    res, rider_out = fn(*args, rider=hook[0], **kw)
    hook[1](rider_out)
    return res


def _layer_fwd(l, x, mod, W, P, hooks=None):
    hooks = hooks or {}
    s = x.shape[0]
    sh_m, sc_m, g_m, sh_f, sc_f, g_f = [mod[l:l + 1, D_MODEL * j:D_MODEL * (j + 1)] for j in range(6)]
    n = lambda t: f"{t}{l}"
    w_in = W["w_in"][l]
    u = _ln_mod(x, sc_m, sh_m, n("ln_mod_mix"))
    tmz = min(1024, s)
    zp = _mm(u, w_in, "nt", tm=min(2048, s), tn=256, out_dtype=F32, name=n("z_pool"), b_col0=0, n_out=D_POOL)
    qkv = _mm(u, w_in, "nt", tm=tmz, tn=256, out_dtype=BF16, name=n("z_qkv"), b_col0=OFF_QKV // 256, n_out=3 * D_ATTN)
    zc = _mm(u, w_in, "nt", tm=tmz, tn=256, out_dtype=F32, name=n("z_conv"), b_col0=OFF_CONV // 256, n_out=2 * D_CONV)
    zg = _mm(u, w_in, "nt", tm=tmz, tn=768, out_dtype=F32, name=n("z_gate"), b_col0=OFF_GATE // 768, n_out=3 * D_MODEL)

    p, feat_pool = _pool_fwd(zp, P["wp_bd"][l], P["pool_scale"][l], n("pool_fwd"))
    bias = _bias_block(P["rel_bias"][l], n("bias_block"))
    o = _hosted(_attn_fwd, hooks.get("attn"), qkv, bias, n("attn_fwd"))
    cv, feat_conv = _conv_fwd(zc, P["conv_w"][l], P["conv_b"][l], P["conv_ln_g"][l], P["conv_ln_b"][l], n("conv_fwd"))

    tmb = min(1024, s)
    y_pool = _mm(feat_pool, W["w_br_pool"][l], "nt", tm=tmb, tn=1024, out_dtype=F32, name=n("y_pool"))
    y_attn = _mm(o, W["w_br_attn"][l], "nt", tm=tmb, tn=1024, out_dtype=F32, name=n("y_attn"))
    y_conv = _mm(feat_conv, W["w_br_conv"][l], "nt", tm=tmb, tn=1024, out_dtype=F32, name=n("y_conv"))
    ys = (y_pool, y_attn, y_conv)
    merged = _merge(zg, P["b_gate"][l], ys, n("merge"))
    mix = _mm(merged, W["w_o"][l], "nn", tm=tmb, tn=1024, out_dtype=F32, name=n("mix_out"))
    x1 = _resid_ln(x, mix, g_m, P["ln_mix_g"][l], P["ln_mix_b"][l], n("resid_ln_mix"))

    u2 = _ln_mod(x1, sc_f, sh_f, n("ln_mod_ff"))
    hpre = _hosted(_mm, hooks.get("ff1"), u2, W["w_ff1"][l], "nt", tm=tmb, tn=1024, out_dtype=F32, name=n("ff1"),
                   bias=P["b_ff1"][l])
    ff = _hosted(_mm, hooks.get("ff2"), hpre, W["w_ff2"][l], "nn", tm=min(256, s), tn=1024, out_dtype=F32,
                 name=n("ff2"), a_fn=_relu2, bias=P["b_ff2"][l])
    x2 = _resid_ln(x1, ff, g_f, P["ln_ff_g"][l], P["ln_ff_b"][l], n("resid_ln_ff"))
    saved = dict(x=x, u=u, zp=zp, qkv=qkv, zc=zc, zg=zg, p=p, feat_pool=feat_pool, bias=bias, o=o, cv=cv,
                 feat_conv=feat_conv, ys=ys, merged=merged, mix=mix, x1=x1, u2=u2, hpre=hpre, ff=ff)
    return x2, saved


def _layer_bwd(l, dx2, mod, W, P, A, hooks=None):
    hooks = hooks or {}
    s = dx2.shape[0]
    sh_m, sc_m, g_m, sh_f, sc_f, g_f = [mod[l:l + 1, D_MODEL * j:D_MODEL * (j + 1)] for j in range(6)]
    n = lambda t: f"{t}{l}"
    tmb = min(1024, s)
    gw, gs = {}, {}

    dres, dff, gs["ln_ff_g"], gs["ln_ff_b"], dg_f, gs["b_ff2"] = _resid_ln_bwd(
        dx2, A["x1"], A["ff"], g_f, P["ln_ff_g"][l], n("resid_ln_ff_bwd"))
    gw["w_ff2"] = _mm(A["hpre"], dff, "tn", tm=256, tn=512, out_dtype=BF16, name=n("dw_ff2"), a_fn=_relu2,
                      split_n=True)
    dhpre, gs["b_ff1"] = _ff_hidden_bwd(dff, W["w_ff2"][l], A["hpre"], n("ff_hidden_bwd"))
    gw["w_ff1"] = _mm(dhpre, A["u2"], "tn", tm=1024, tn=512, out_dtype=BF16, name=n("dw_ff1"), split_n=True)
    du2 = _mm(dhpre, W["w_ff1"][l], "nn", tm=min(512, s), tn=512, out_dtype=F32, name=n("du_ff"))
    dx1, dsc_f, dsh_f = _ln_mod_bwd(du2, A["x1"], sc_f, dres, n("ln_mod_ff_bwd"))

    dres, dmix, gs["ln_mix_g"], gs["ln_mix_b"], dg_m, _ = _resid_ln_bwd(
        dx1, A["x"], A["mix"], g_m, P["ln_mix_g"][l], n("resid_ln_mix_bwd"))
    gw["w_o"] = _mm(A["merged"], dmix, "tn", tm=1024, tn=512, out_dtype=BF16, name=n("dw_o"), split_n=True)
    dmerged = _mm(dmix, W["w_o"][l], "nt", tm=tmb, tn=1024, out_dtype=F32, name=n("d_merged"))
    dy_pool, dy_attn, dy_conv, dzg, gs["b_gate"] = _merge_bwd(dmerged, A["zg"], P["b_gate"][l], A["ys"], n("merge_bwd"))

    gw["w_br_pool"] = _mm(dy_pool, A["feat_pool"], "tn", tm=1024, tn=128, out_dtype=BF16, name=n("dw_br_pool"),
                          split_n=True)
    gw["w_br_attn"] = _mm(dy_attn, A["o"], "tn", tm=1024, tn=256, out_dtype=BF16, name=n("dw_br_attn"), split_n=True)
    gw["w_br_conv"] = _mm(dy_conv, A["feat_conv"], "tn", tm=1024, tn=128, out_dtype=BF16, name=n("dw_br_conv"),
                          split_n=True)
    dfeat_pool = _mm(dy_pool, W["w_br_pool"][l], "nn", tm=tmb, tn=256, out_dtype=F32, name=n("d_feat_pool"))
    do = _mm(dy_attn, W["w_br_attn"][l], "nn", tm=tmb, tn=512, out_dtype=BF16, name=n("d_attn_out"))
    dfeat_conv = _mm(dy_conv, W["w_br_conv"][l], "nn", tm=tmb, tn=256, out_dtype=F32, name=n("d_feat_conv"))

    dzp, dwp_bd, gs["pool_scale"] = _pool_bwd(dfeat_pool, A["p"], P["wp_bd"][l], P["pool_scale"][l], n("pool_bwd"))
    gs["w_pool"] = jnp.stack([dwp_bd[POOL_GROUP * g:POOL_GROUP * (g + 1), POOL_GROUP * g:POOL_GROUP * (g + 1)]
                              for g in range(len(POOL_WINDOWS))])
    hook = hooks["attn"](gw) if "attn" in hooks else None
    dq, dk, dv, ds_acc = _hosted(_attn_bwd, hook, A["qkv"], do, A["bias"], n("attn_bwd"))
    gs["rel_bias"] = _bias_block_bwd(ds_acc, n("bias_block_bwd"))
    dzc, dcw, gs["conv_b"], gs["conv_ln_g"], gs["conv_ln_b"] = _conv_bwd(
        dfeat_conv, A["cv"], A["zc"], P["conv_w"][l], P["conv_ln_g"][l], P["conv_ln_b"][l], n("conv_bwd"))
    gs["conv_w"] = dcw[:CONV_WIDTH]

    dz = jnp.concatenate([dzp, dq, dk[KPAD:].astype(BF16), dv[KPAD:].astype(BF16), dzc, dzg], axis=1)
    gw["w_in"] = _mm(dz, A["u"], "tn", tm=768, tn=512, out_dtype=BF16, name=n("dw_in"), split_n=True)
    hook = hooks["du_mix"](gw) if "du_mix" in hooks else None
    du = _hosted(_mm, hook, dz, W["w_in"][l], "nn", tm=min(512, s), tn=512, out_dtype=F32, name=n("du_mix"))
    dx, dsc_m, dsh_m = _ln_mod_bwd(du, A["x"], sc_m, dres, n("ln_mod_mix_bwd"))
    dmod = jnp.concatenate([dsh_m, dsc_m, dg_m, dsh_f, dsc_f, dg_f], axis=1)
    return dx, gw, gs, dmod


def _small_shapes():
    return {"b_ada": (6 * D_MODEL,), "b_gate": (3 * D_MODEL,), "w_pool": (4, POOL_GROUP, POOL_GROUP),
            "pool_scale": (D_POOL,), "rel_bias": (N_HEADS, N_REL), "conv_w": (CONV_WIDTH, D_CONV),
            "conv_b": (D_CONV,), "conv_ln_g": (D_CONV,), "conv_ln_b": (D_CONV,), "ln_mix_g": (D_MODEL,),
            "ln_mix_b": (D_MODEL,), "b_ff1": (D_FF,), "b_ff2": (D_MODEL,), "ln_ff_g": (D_MODEL,), "ln_ff_b": (D_MODEL,)}


def kernel(x, c, w_ada, b_ada, w_in, b_gate, w_pool, pool_scale, rel_bias, conv_w, conv_b, conv_ln_g, conv_ln_b, w_br_pool, w_br_attn, w_br_conv, w_o, ln_mix_g, ln_mix_b, w_ff1, b_ff1, w_ff2, b_ff2, ln_ff_g, ln_ff_b, loss_target, m_w_ada, m_b_ada, m_w_in, m_b_gate, m_w_pool, m_pool_scale, m_rel_bias, m_conv_w, m_conv_b, m_conv_ln_g, m_conv_ln_b, m_w_br_pool, m_w_br_attn, m_w_br_conv, m_w_o, m_ln_mix_g, m_ln_mix_b, m_w_ff1, m_b_ff1, m_w_ff2, m_b_ff2, m_ln_ff_g, m_ln_ff_b, v_w_ada, v_b_ada, v_w_in, v_b_gate, v_w_pool, v_pool_scale, v_rel_bias, v_conv_w, v_conv_b, v_conv_ln_g, v_conv_ln_b, v_w_br_pool, v_w_br_attn, v_w_br_conv, v_w_o, v_ln_mix_g, v_ln_mix_b, v_w_ff1, v_b_ff1, v_w_ff2, v_b_ff2, v_ln_ff_g, v_ln_ff_b):
    env = dict(locals())
    xi, yi, ci = _me()
    chip = 2 * xi + yi
    me = 4 * xi + 2 * yi + ci
    xs = x[0]
    tgt = loss_target[0]
    L = DEPTH

    c_all = _allgather_small(c.reshape(8, 128), "gather_c").reshape(N_DEV, D_MODEL)
    ada_cols = w_ada.shape[2]
    b_ada_sh = lax.dynamic_slice_in_dim(b_ada, chip * ada_cols, ada_cols, axis=1).reshape(L, 1, ada_cols)
    mod_part = _mod_fwd(c_all, w_ada, b_ada_sh, "mod_fwd")
    mod_g = _allgather_small(mod_part.reshape(-1, 128), "gather_mod").reshape(N_CHIP, 2, L, N_DEV, ada_cols)[:, 0]
    mod_all = jnp.transpose(mod_g, (1, 2, 0, 3)).reshape(L, N_DEV, 6 * D_MODEL)
    mod = lax.dynamic_index_in_dim(mod_all, me, axis=1, keepdims=False)

    W = {k: [None] * L for k in BIG}

    def weight_gather(names, l):
        shards = [(jnp.swapaxes(env[k][l], 0, 1) if k in COL_SHARDED else env[k][l]).astype(BF16) for k in names]
        shards = [a.reshape(2, a.shape[0] // 2, a.shape[1]) for a in shards]

        def done(outs):
            for k, g in zip(names, outs):
                W[k][l] = g.reshape(-1, g.shape[-1])

        return _gather_rider(shards), done

    first_names = ("w_in", "w_br_pool", "w_br_attn", "w_br_conv", "w_o")
    late_names = ("w_ff1", "w_ff2")
    rider, done = weight_gather(first_names, 0)
    done(_run_rider(rider, "gather_weights_first0"))
    fwd_hooks = [{"attn": weight_gather(late_names, 0), "ff1": weight_gather(("w_in",), 1),
                  "ff2": weight_gather(("w_br_pool", "w_br_attn", "w_br_conv", "w_o"), 1)},
                 {"attn": weight_gather(late_names, 1)}]

    P = {k: env[k] for k in ("rel_bias", "conv_w")}
    for k in ("b_gate", "pool_scale", "conv_b", "conv_ln_g", "conv_ln_b", "ln_mix_g", "ln_mix_b", "b_ff1", "b_ff2",
              "ln_ff_g", "ln_ff_b"):
        P[k] = env[k].reshape(L, 1, -1)
    conv_w_full = _allgather_small(_pack([conv_w]), "gather_conv_w")
    n_cw = conv_w.size
    cw = conv_w_full.reshape(N_CHIP, 2, -1)[:, 0, :n_cw].reshape(N_CHIP, L, CONV_WIDTH, D_CONV // N_CHIP)
    P["conv_w"] = jnp.transpose(cw, (1, 2, 0, 3)).reshape(L, CONV_WIDTH, D_CONV)
    wp_bd = jnp.zeros((L, D_POOL, D_POOL), F32)
    for g in range(len(POOL_WINDOWS)):
        sl = slice(POOL_GROUP * g, POOL_GROUP * (g + 1))
        wp_bd = wp_bd.at[:, sl, sl].set(w_pool[:, g])
    P["wp_bd"] = wp_bd.astype(BF16)

    acts = []
    h = xs
    for l in range(L):
        h, saved = _layer_fwd(l, h, mod, W, P, fwd_hooks[l])
        acts.append(saved)
    dy, loss_part = _loss_grad(h, tgt, "loss_grad")
    loss = lax.psum(loss_part[0, 0], ("x", "y", "c"))

    place = jnp.stack([ci, chip, chip ^ 1, chip ^ 2, chip ^ 3]).astype(jnp.int32)
    scattered = {}

    def grad_scatter(items, tag):
        dws = [dw for _, _, dw in items]
        got = _sibling_send(dws, f"swap_blocks_{tag}", other_half=True)
        both = [_sum_cores(a, b, place, f"sum_cores_{k}{l}") for (k, l, _), a, b in zip(items, dws, got)]
        both = [hh.reshape(N_CHIP, -1, hh.shape[-1]) for hh in both]

        def done(outs):
            for (k, l, _), hh, r in zip(items, both, outs):
                scattered[(k, l)] = (hh, r)

        return _scatter_rider(both), done

    early = ("w_ff2", "w_ff1", "w_o", "w_br_pool", "w_br_attn", "w_br_conv")
    left_over = []

    def attn_hook(l):
        def hook(gw):
            items = left_over + [(k, l, gw[k]) for k in early]
            left_over.clear()
            return grad_scatter(items, f"attn{l}")
        return hook

    def last_hook(gw):
        return grad_scatter([("w_in", 0, gw["w_in"])], "last")

    gws, gss, dmods = [None] * L, [None] * L, [None] * L
    dh = dy
    for l in reversed(range(L)):
        hooks = {"attn": attn_hook(l)}
        if l == 0:
            hooks["du_mix"] = last_hook
        dh, gws[l], gss[l], dmods[l] = _layer_bwd(l, dh, mod, W, P, acts[l], hooks)
        if l > 0:
            left_over.append(("w_in", l, gws[l]["w_in"]))
    grad_x = dh[None]

    reduced = [[_sum_chips(*scattered[(k, l)], place, f"sum_chips_{k}{l}") for l in range(L)] for k in BIG]
    flat_reduced = [t for per_weight in reduced for t in per_weight]
    flat_other = _sibling_send(flat_reduced, "swap_reduced")

    out = {}
    for j, k in enumerate(BIG):
        own, other = reduced[j], flat_other[L * j:L * (j + 1)]
        if k in COL_SHARDED:
            own, other = [a.T for a in own], [a.T for a in other]
        out[k] = tuple(_adamw_halves(env[k], env["m_" + k], env["v_" + k], own, other, place,
                                     "rows" if k in COL_SHARDED else "cols", f"adamw_{k}"))

    shapes = _small_shapes()
    small_names = [k for k in SMALL if k != "b_ada"]
    dmod_own = jnp.concatenate(dmods, axis=0)
    pack = _pack([dmod_own] + [jnp.stack([gss[l][k].reshape(shapes[k]) for l in range(L)]) for k in small_names])
    g_all = _allgather_small(pack.reshape(-1, 128), "gather_small").reshape(N_DEV, -1, PACK_W)

    dmod_all = g_all[:, :L * 6].reshape(N_DEV, L, 6 * D_MODEL)
    dmod_sh = jnp.transpose(lax.dynamic_slice_in_dim(dmod_all, chip * ada_cols, ada_cols, axis=2), (1, 0, 2))
    g_ada = _mod_bwd(c_all, dmod_sh, "mod_bwd")
    g_, d_, m_, v_ = _adamw(w_ada.reshape(-1, ada_cols), m_w_ada.reshape(-1, ada_cols), v_w_ada.reshape(-1, ada_cols),
                            [g_ada.reshape(-1, ada_cols)], "adamw_w_ada")
    out["w_ada"] = tuple(a.reshape(w_ada.shape) for a in (g_, d_, m_, v_))

    def small_pack(prefix):
        parts = [env[prefix + "b_ada"]]
        for k in small_names:
            a = env[prefix + k]
            if k == "conv_w":
                a = jnp.zeros((L,) + shapes[k], F32)
            parts.append(a)
        return _pack(parts)

    gp, dp, mp, vp = _adamw_small(small_pack(""), small_pack("m_"), small_pack("v_"), g_all, "adamw_small")
    full_shapes = [(L,) + shapes["b_ada"]] + [(L,) + shapes[k] for k in small_names]
    for tag, packed in (("g", gp), ("d", dp), ("m", mp), ("v", vp)):
        for k, a in zip(["b_ada"] + small_names, _unpack(packed, full_shapes)):
            out.setdefault(k, {})
            out[k][tag] = a
    g_cw_full = out["conv_w"]["g"]
    cw_cols = D_CONV // N_CHIP
    g_cw = lax.dynamic_slice_in_dim(g_cw_full, chip * cw_cols, cw_cols, axis=2)
    pad_rows = lambda a: jnp.pad(a.reshape(L * CONV_WIDTH, cw_cols), ((0, 2), (0, 0)))
    g_, d_, m_, v_ = _adamw(pad_rows(conv_w), pad_rows(m_conv_w), pad_rows(v_conv_w), [pad_rows(g_cw)], "adamw_conv_w")
    out["conv_w"] = tuple(a[:L * CONV_WIDTH].reshape(L, CONV_WIDTH, cw_cols) for a in (g_, d_, m_, v_))

    names = ["w_ada", "b_ada", "w_in", "b_gate", "w_pool", "pool_scale", "rel_bias", "conv_w", "conv_b", "conv_ln_g",
             "conv_ln_b", "w_br_pool", "w_br_attn", "w_br_conv", "w_o", "ln_mix_g", "ln_mix_b", "w_ff1", "b_ff1",
             "w_ff2", "b_ff2", "ln_ff_g", "ln_ff_b"]

    def pick(k, i):
        o = out[k]
        return o[i] if isinstance(o, tuple) else o["gdmv"[i]].reshape(env[k].shape)

    return (loss, grad_x, *[pick(k, 0) for k in names], *[pick(k, 1) for k in names],
            *[pick(k, 2) for k in names], *[pick(k, 3) for k in names])
```

```python
import functools

import jax
import jax.numpy as jnp
import numpy as np
from jax import lax
from jax.experimental import pallas as pl
from jax.experimental.pallas import tpu as pltpu

F32 = jnp.float32
BF16 = jnp.bfloat16

D_MODEL = 1024
DEPTH = 2
CHUNK = 64
POOL_WINDOWS = (2, 4, 8, 16)
POOL_GROUP = 64
D_POOL = 256
N_HEADS = 8
HEAD_DIM = 64
D_ATTN = 512
N_PREV_CHUNKS = 8
REL_CLIP = 128
N_REL = 2 * REL_CLIP + 1
D_CONV = 256
CONV_WIDTH = 31
D_FF = 4 * D_MODEL
D_IN = 5376
OFF_POOL, OFF_QKV, OFF_CONV, OFF_GATE = 0, 256, 1792, 2304
ALPHA = (2.0 * DEPTH) ** 0.25
LN_EPS = 1e-5
NEG_INF = -1e30
ADAM_LR, ADAM_B1, ADAM_B2, ADAM_EPS, ADAM_WD, ADAM_STEP = 0.001, 0.9, 0.999, 1e-08, 0.01, 10

N_DEV = 8
N_CHIP = 4
MESH = pl.DeviceIdType.MESH

QB = 2 * CHUNK
KPAD = N_PREV_CHUNKS * CHUNK
KW = QB + KPAD
SKEW_W = 768

VMEM_LIMIT = 56 * 1024 * 1024


def _cparams(**kw):
    return pltpu.CompilerParams(vmem_limit_bytes=VMEM_LIMIT, **kw)


def _full(shape):
    n = len(shape)
    return pl.BlockSpec(shape, lambda *_: (0,) * n)


_DIMS = {"nn": (((1,), (0,)), ((), ())), "nt": (((1,), (1,)), ((), ())), "tn": (((0,), (0,)), ((), ()))}


def _relu2(t):
    r = jnp.maximum(t, 0.0)
    return r * r


def _mm(a, b, mode, *, tm, tn, out_dtype, name, b_col0=0, n_out=None, a_fn=None, bias=None, split_n=False,
        rider=None):
    if mode == "tn":
        k, m = a.shape
        n = b.shape[1] if n_out is None else n_out
        a_spec = pl.BlockSpec((k, tm), lambda i, j: (0, i))
        b_spec = pl.BlockSpec((k, tn), lambda i, j: (0, j + b_col0))
    elif mode == "nn":
        m, k = a.shape
        n = b.shape[1] if n_out is None else n_out
        a_spec = pl.BlockSpec((tm, k), lambda i, j: (i, 0))
        b_spec = pl.BlockSpec((k, tn), lambda i, j: (0, j + b_col0))
    else:
        m, k = a.shape
        n = b.shape[0] if n_out is None else n_out
        a_spec = pl.BlockSpec((tm, k), lambda i, j: (i, 0))
        b_spec = pl.BlockSpec((tn, k), lambda i, j: (j + b_col0, 0))
    assert m % tm == 0 and n % tn == 0, (name, m, n, tm, tn)
    dims = _DIMS[mode]

    def body(*refs):
        if bias is None:
            a_ref, b_ref, o_ref = refs
        else:
            a_ref, b_ref, bias_ref, o_ref = refs
        av = a_ref[...]
        if a_fn is not None:
            av = a_fn(av)
        acc = lax.dot_general(av.astype(BF16), b_ref[...].astype(BF16), dims, preferred_element_type=F32)
        if bias is not None:
            acc = acc + bias_ref[...]
        o_ref[...] = acc.astype(out_dtype)

    in_specs = [a_spec, b_spec]
    args = [a, b]
    if bias is not None:
        in_specs.append(pl.BlockSpec((1, tn), lambda i, j: (0, j)))
        args.append(bias)
    if split_n:
        out_spec = pl.BlockSpec((None, tm, tn), lambda i, j: (j, i, 0))
        out_shape = jax.ShapeDtypeStruct((n // tn, m, tn), out_dtype)
    else:
        out_spec = pl.BlockSpec((tm, tn), lambda i, j: (i, j))
        out_shape = jax.ShapeDtypeStruct((m, n), out_dtype)
    res = _call(body, name=name, grid=(m // tm, n // tn), in_specs=in_specs, out_specs=[out_spec],
                out_shape=[out_shape], scratch_shapes=[], args=args, rider=rider)
    return res[0] if rider is None else (res[0][0], res[1])


def _ln_hat(x):
    mu = jnp.mean(x, axis=-1, keepdims=True)
    xc = x - mu
    var = jnp.mean(xc * xc, axis=-1, keepdims=True)
    rstd = lax.rsqrt(var + LN_EPS)
    return xc * rstd, rstd


def _ln_hat_bwd(dhat, xhat, rstd):
    m1 = jnp.mean(dhat, axis=-1, keepdims=True)
    m2 = jnp.mean(dhat * xhat, axis=-1, keepdims=True)
    return rstd * (dhat - m1 - xhat * m2)


def _row_tile(s):
    return min(512, s)


def _acc_rows(ref, val, first):
    @pl.when(first)
    def _():
        ref[...] = jnp.zeros_like(ref)
    ref[...] += jnp.sum(val, axis=0, keepdims=True)


def _ln_mod(x, sc, sh, name):
    s, d = x.shape
    tm = _row_tile(s)

    def body(x_ref, sc_ref, sh_ref, u_ref):
        xhat, _ = _ln_hat(x_ref[...])
        u_ref[...] = (xhat * (1.0 + sc_ref[...]) + sh_ref[...]).astype(BF16)

    row = pl.BlockSpec((tm, d), lambda i: (i, 0))
    vec = pl.BlockSpec((1, d), lambda i: (0, 0))
    return pl.pallas_call(body, grid=(s // tm,), in_specs=[row, vec, vec], out_specs=row,
                          out_shape=jax.ShapeDtypeStruct((s, d), BF16), name=name, compiler_params=_cparams())(x, sc, sh)


def _ln_mod_bwd(du, x, sc, dres, name):
    s, d = x.shape
    tm = _row_tile(s)

    def body(du_ref, x_ref, sc_ref, dres_ref, dx_ref, dsc_ref, dsh_ref):
        first = pl.program_id(0) == 0
        duv = du_ref[...]
        xhat, rstd = _ln_hat(x_ref[...])
        dx_ref[...] = dres_ref[...] + _ln_hat_bwd(duv * (1.0 + sc_ref[...]), xhat, rstd)
        _acc_rows(dsc_ref, duv * xhat, first)
        _acc_rows(dsh_ref, duv, first)

    row = pl.BlockSpec((tm, d), lambda i: (i, 0))
    vec = pl.BlockSpec((1, d), lambda i: (0, 0))
    vs = jax.ShapeDtypeStruct((1, d), F32)
    return pl.pallas_call(body, grid=(s // tm,), in_specs=[row, row, vec, row], out_specs=[row, vec, vec],
                          out_shape=[jax.ShapeDtypeStruct((s, d), F32), vs, vs], name=name,
                          compiler_params=_cparams())(du, x, sc, dres)


def _resid_ln(x, f, g, gam, bet, name):
    s, d = x.shape
    tm = _row_tile(s)

    def body(x_ref, f_ref, g_ref, gam_ref, bet_ref, o_ref):
        rhat, _ = _ln_hat(ALPHA * x_ref[...] + g_ref[...] * f_ref[...])
        o_ref[...] = rhat * gam_ref[...] + bet_ref[...]

    row = pl.BlockSpec((tm, d), lambda i: (i, 0))
    vec = pl.BlockSpec((1, d), lambda i: (0, 0))
    return pl.pallas_call(body, grid=(s // tm,), in_specs=[row, row, vec, vec, vec], out_specs=row,
                          out_shape=jax.ShapeDtypeStruct((s, d), F32), name=name, compiler_params=_cparams())(x, f, g, gam, bet)


def _resid_ln_bwd(dxo, x, f, g, gam, name):
    s, d = x.shape
    tm = _row_tile(s)

    def body(dxo_ref, x_ref, f_ref, g_ref, gam_ref, dres_ref, df_ref, dgam_ref, dbet_ref, dg_ref, dbias_ref):
        first = pl.program_id(0) == 0
        dxov = dxo_ref[...]
        fv = f_ref[...]
        rhat, rstd = _ln_hat(ALPHA * x_ref[...] + g_ref[...] * fv)
        dr = _ln_hat_bwd(dxov * gam_ref[...], rhat, rstd)
        dfv = g_ref[...] * dr
        dres_ref[...] = ALPHA * dr
        df_ref[...] = dfv.astype(BF16)
        _acc_rows(dgam_ref, dxov * rhat, first)
        _acc_rows(dbet_ref, dxov, first)
        _acc_rows(dg_ref, dr * fv, first)
        _acc_rows(dbias_ref, dfv, first)

    row = pl.BlockSpec((tm, d), lambda i: (i, 0))
    vec = pl.BlockSpec((1, d), lambda i: (0, 0))
    vs = jax.ShapeDtypeStruct((1, d), F32)
    return pl.pallas_call(body, grid=(s // tm,), in_specs=[row, row, row, vec, vec],
                          out_specs=[row, row, vec, vec, vec, vec],
                          out_shape=[jax.ShapeDtypeStruct((s, d), F32), jax.ShapeDtypeStruct((s, d), BF16), vs, vs, vs, vs],
                          name=name, compiler_params=_cparams())(dxo, x, f, g, gam)


def _loss_grad(y, tgt, name):
    s, d = y.shape
    tm = _row_tile(s)
    n = s // tm

    def body(y_ref, t_ref, dy_ref, loss_ref, acc_ref):
        i = pl.program_id(0)
        e = y_ref[...] - t_ref[...]
        dy_ref[...] = e * (1.0 / d)
        _acc_rows(acc_ref, e * e, i == 0)

        @pl.when(i == n - 1)
        def _():
            tot = jnp.sum(acc_ref[...], axis=1, keepdims=True) * (0.5 / d)
            loss_ref[...] = jnp.broadcast_to(tot, (1, 128))

    row = pl.BlockSpec((tm, d), lambda i: (i, 0))
    return pl.pallas_call(body, grid=(n,), in_specs=[row, row],
                          out_specs=[row, pl.BlockSpec((1, 128), lambda i: (0, 0))],
                          out_shape=[jax.ShapeDtypeStruct((s, d), F32), jax.ShapeDtypeStruct((1, 128), F32)],
                          scratch_shapes=[pltpu.VMEM((1, d), F32)], name=name, compiler_params=_cparams())(y, tgt)


POOL_HALO = 16
POOL_ROWS = 256


def _pool_counts(r0, rows):
    t1 = (lax.broadcasted_iota(jnp.int32, (rows, 128), 0) + r0 + 1).astype(F32)
    low = lax.broadcasted_iota(jnp.int32, (rows, 128), 1) < POOL_GROUP
    wa = jnp.where(low, float(POOL_WINDOWS[0]), float(POOL_WINDOWS[1]))
    wb = jnp.where(low, float(POOL_WINDOWS[2]), float(POOL_WINDOWS[3]))
    return jnp.minimum(t1, wa), jnp.minimum(t1, wb), low


def _window_sums(win, off, rows, sign):
    def sl(j, half):
        return win[off + sign * j: off + sign * j + rows, 128 * half:128 * half + 128]
    a2 = sl(0, 0) + sl(1, 0)
    a4 = a2 + sl(2, 0) + sl(3, 0)
    a8 = sl(0, 1)
    for j in range(1, 8):
        a8 = a8 + sl(j, 1)
    a16 = a8
    for j in range(8, 16):
        a16 = a16 + sl(j, 1)
    return a2, a4, a8, a16


def _pool_fwd(zp, wp_bd, pscale, name):
    s = zp.shape[0]
    r = min(POOL_ROWS, s)

    def body(z_ref, wp_ref, sc_ref, p_ref, feat_ref, pad):
        pad[0:POOL_HALO, :] = jnp.zeros((POOL_HALO, D_POOL), F32)
        pad[POOL_HALO:, :] = z_ref[...]

        def step(i, carry):
            r0 = pl.multiple_of(i * r, r)
            win = pad[pl.ds(r0, r + POOL_HALO), :]
            a2, a4, a8, a16 = _window_sums(win, POOL_HALO, r, -1)
            ca, cb, low = _pool_counts(r0, r)
            x0 = win[POOL_HALO:, :]
            pa = jnp.where(low, a2, a4) / ca
            pb = jnp.where(low, a8, a16) / cb
            p = (jnp.concatenate([pa, pb], axis=1) - x0).astype(BF16)
            p_ref[pl.ds(r0, r), :] = p
            pw = jnp.dot(p, wp_ref[...], preferred_element_type=F32)
            feat_ref[pl.ds(r0, r), :] = (pw * sc_ref[...]).astype(BF16)
            return carry

        lax.fori_loop(0, s // r, step, 0)

    return pl.pallas_call(
        body, out_shape=[jax.ShapeDtypeStruct((s, D_POOL), BF16), jax.ShapeDtypeStruct((s, D_POOL), BF16)],
        scratch_shapes=[pltpu.VMEM((s + POOL_HALO, D_POOL), F32)], name=name, compiler_params=_cparams(),
    )(zp, wp_bd, pscale)


def _pool_bwd(dfeat, p, wp_bd, pscale, name):
    s = p.shape[0]
    r = min(POOL_ROWS, s)

    def body(df_ref, p_ref, wp_ref, sc_ref, dz_ref, dwp_ref, dsc_ref, gpad, dpbuf):
        dwp_ref[...] = jnp.zeros_like(dwp_ref)
        dsc_ref[...] = jnp.zeros_like(dsc_ref)
        gpad[s:, :] = jnp.zeros((POOL_HALO, D_POOL), F32)

        def step1(i, carry):
            r0 = pl.multiple_of(i * r, r)
            pv = p_ref[pl.ds(r0, r), :]
            dfv = df_ref[pl.ds(r0, r), :]
            pw = jnp.dot(pv, wp_ref[...], preferred_element_type=F32)
            dsc_ref[...] += jnp.sum(dfv * pw, axis=0, keepdims=True)
            dpw = (dfv * sc_ref[...]).astype(BF16)
            dwp_ref[...] += lax.dot_general(pv, dpw, _DIMS["tn"], preferred_element_type=F32)
            dp = lax.dot_general(dpw, wp_ref[...], _DIMS["nt"], preferred_element_type=F32)
            ca, cb, _ = _pool_counts(r0, r)
            gpad[pl.ds(r0, r), :] = dp / jnp.concatenate([ca, cb], axis=1)
            dpbuf[pl.ds(r0, r), :] = dp
            return carry

        lax.fori_loop(0, s // r, step1, 0)

        def step2(i, carry):
            r0 = pl.multiple_of(i * r, r)
            win = gpad[pl.ds(r0, r + POOL_HALO), :]
            a2, a4, a8, a16 = _window_sums(win, 0, r, 1)
            low = lax.broadcasted_iota(jnp.int32, (r, 128), 1) < POOL_GROUP
            acc = jnp.concatenate([jnp.where(low, a2, a4), jnp.where(low, a8, a16)], axis=1)
            dz_ref[pl.ds(r0, r), :] = (acc - dpbuf[pl.ds(r0, r), :]).astype(BF16)
            return carry

        lax.fori_loop(0, s // r, step2, 0)

    return pl.pallas_call(
        body,
        out_shape=[jax.ShapeDtypeStruct((s, D_POOL), BF16), jax.ShapeDtypeStruct((D_POOL, D_POOL), F32),
                   jax.ShapeDtypeStruct((1, D_POOL), F32)],
        scratch_shapes=[pltpu.VMEM((s + POOL_HALO, D_POOL), F32), pltpu.VMEM((s, D_POOL), F32)],
        name=name, compiler_params=_cparams(),
    )(dfeat, p, wp_bd, pscale)


def _skew_index():
    cp = lax.broadcasted_iota(jnp.int32, (SKEW_W, N_REL), 0)
    dist = jnp.where(cp < KW, KPAD - cp, KPAD + SKEW_W - cp)
    idx = jnp.clip(dist, -REL_CLIP, REL_CLIP) + REL_CLIP
    return (idx == lax.broadcasted_iota(jnp.int32, (SKEW_W, N_REL), 1)).astype(F32)


def _row_bits(b):
    return (lax.broadcasted_iota(jnp.int32, (QB, SKEW_W), 0) >> b) & 1 == 1


N_EDGE = KPAD // QB


def _bias_block(rel_bias, name):
    def body(rb_ref, o_ref):
        onehot = _skew_index()
        row0 = lax.dot_general(rb_ref[...], onehot, _DIMS["nt"], precision=lax.Precision.HIGHEST,
                               preferred_element_type=F32)
        r = lax.broadcasted_iota(jnp.int32, (QB, KW), 0)
        kk = lax.broadcasted_iota(jnp.int32, (QB, KW), 1)
        cq, ck = r // CHUNK, kk // CHUNK
        band = (ck >= cq) & (ck <= cq + N_PREV_CHUNKS)
        for h in range(N_HEADS):
            t = jnp.broadcast_to(row0[h:h + 1, :], (QB, SKEW_W))
            for b in range(7):
                t = jnp.where(_row_bits(b), pltpu.roll(t, 1 << b, 1), t)
            for e in range(N_EDGE + 1):
                o_ref[e, h] = jnp.where(band & (kk >= KPAD - e * QB), t[:, :KW], NEG_INF)

    return pl.pallas_call(body, out_shape=jax.ShapeDtypeStruct((N_EDGE + 1, N_HEADS, QB, KW), F32), name=name,
                          compiler_params=_cparams())(rel_bias)


def _bias_spec():
    return pl.BlockSpec((None, N_HEADS, QB, KW), lambda i: (jnp.minimum(i, N_EDGE), 0, 0, 0))


def _bias_block_bwd(ds_acc, name):
    def body(ds_ref, o_ref):
        sums = []
        for h in range(N_HEADS):
            t = jnp.concatenate([ds_ref[h], jnp.zeros((QB, SKEW_W - KW), F32)], axis=1)
            for b in range(7):
                t = jnp.where(_row_bits(b), pltpu.roll(t, SKEW_W - (1 << b), 1), t)
            sums.append(jnp.sum(t, axis=0, keepdims=True))
        allh = jnp.concatenate(sums, axis=0)
        o_ref[...] = jnp.dot(allh, _skew_index(), precision=lax.Precision.HIGHEST, preferred_element_type=F32)

    return pl.pallas_call(body, out_shape=jax.ShapeDtypeStruct((N_HEADS, N_REL), F32), name=name,
                          compiler_params=_cparams())(ds_acc)


def _scaled(q):
    return (q.astype(F32) * (HEAD_DIM ** -0.5)).astype(BF16)


def _scores(qh, kh, bias_h):
    sc = lax.dot_general(qh, kh, _DIMS["nt"], preferred_element_type=F32) + bias_h
    e = jnp.exp(sc - jnp.max(sc, axis=-1, keepdims=True))
    return e * (1.0 / jnp.sum(e, axis=-1, keepdims=True))


def _load_padded_kv(qkv_hbm, kpad, vpad, sems, s):
    kpad[0:KPAD, :] = jnp.zeros((KPAD, D_ATTN), BF16)
    vpad[0:KPAD, :] = jnp.zeros((KPAD, D_ATTN), BF16)
    ck = pltpu.make_async_copy(qkv_hbm.at[:, D_ATTN:2 * D_ATTN], kpad.at[pl.ds(KPAD, s), :], sems.at[0])
    cv = pltpu.make_async_copy(qkv_hbm.at[:, 2 * D_ATTN:3 * D_ATTN], vpad.at[pl.ds(KPAD, s), :], sems.at[1])
    ck.start()
    cv.start()
    ck.wait()
    cv.wait()


def _attn_fwd(qkv, bias, name, rider=None):
    s = qkv.shape[0]

    def body(q_ref, qkv_hbm, bias_ref, o_ref, kpad, vpad, sems):
        i = pl.program_id(0)

        @pl.when(i == 0)
        def _():
            _load_padded_kv(qkv_hbm, kpad, vpad, sems, s)

        base = pl.multiple_of(i * QB, QB)
        kw = kpad[pl.ds(base, KW), :]
        vw = vpad[pl.ds(base, KW), :]
        q = _scaled(q_ref[...])
        outs = []
        for h in range(N_HEADS):
            hs = slice(HEAD_DIM * h, HEAD_DIM * (h + 1))
            p = _scores(q[:, hs], kw[:, hs], bias_ref[h])
            outs.append(jnp.dot(p.astype(BF16), vw[:, hs], preferred_element_type=F32))
        o_ref[...] = jnp.concatenate(outs, axis=1).astype(BF16)

    res = _call(
        body, name=name, grid=(s // QB,),
        in_specs=[pl.BlockSpec((QB, D_ATTN), lambda i: (i, 0)), pl.BlockSpec(memory_space=pl.ANY),
                  _bias_spec()],
        out_specs=[pl.BlockSpec((QB, D_ATTN), lambda i: (i, 0))],
        out_shape=[jax.ShapeDtypeStruct((s, D_ATTN), BF16)],
        scratch_shapes=[pltpu.VMEM((s + KPAD, D_ATTN), BF16), pltpu.VMEM((s + KPAD, D_ATTN), BF16),
                        pltpu.SemaphoreType.DMA((2,))],
        args=(qkv, qkv, bias), rider=rider)
    return res[0] if rider is None else (res[0][0], res[1])


def _attn_bwd(qkv, do, bias, name, rider=None):
    s = qkv.shape[0]
    n = s // QB

    def body(q_ref, qkv_hbm, do_ref, bias_ref, dq_ref, dk_hbm, dv_hbm, ds_ref, kpad, vpad, dkacc, dvacc, sems):
        i = pl.program_id(0)

        @pl.when(i == 0)
        def _():
            _load_padded_kv(qkv_hbm, kpad, vpad, sems, s)
            dkacc[...] = jnp.zeros_like(dkacc)
            dvacc[...] = jnp.zeros_like(dvacc)
            ds_ref[...] = jnp.zeros_like(ds_ref)

        base = pl.multiple_of(i * QB, QB)
        kw = kpad[pl.ds(base, KW), :]
        vw = vpad[pl.ds(base, KW), :]
        q = _scaled(q_ref[...])
        dov = do_ref[...]
        dqs, dks, dvs = [], [], []
        for h in range(N_HEADS):
            hs = slice(HEAD_DIM * h, HEAD_DIM * (h + 1))
            qh, kh, vh, doh = q[:, hs], kw[:, hs], vw[:, hs], dov[:, hs]
            p = _scores(qh, kh, bias_ref[h])
            dvs.append(lax.dot_general(p.astype(BF16), doh, _DIMS["tn"], preferred_element_type=F32))
            dp = lax.dot_general(doh, vh, _DIMS["nt"], preferred_element_type=F32)
            ds = p * (dp - jnp.sum(dp * p, axis=-1, keepdims=True))
            ds_ref[h] += ds
            dsb = ds.astype(BF16)
            dqs.append(jnp.dot(dsb, kh, preferred_element_type=F32))
            dks.append(lax.dot_general(dsb, qh, _DIMS["tn"], preferred_element_type=F32))
        dq_ref[...] = (jnp.concatenate(dqs, axis=1) * (HEAD_DIM ** -0.5)).astype(BF16)
        dkacc[pl.ds(base, KW), :] += jnp.concatenate(dks, axis=1)
        dvacc[pl.ds(base, KW), :] += jnp.concatenate(dvs, axis=1)

        @pl.when(i == n - 1)
        def _():
            ck = pltpu.make_async_copy(dkacc, dk_hbm, sems.at[0])
            cv = pltpu.make_async_copy(dvacc, dv_hbm, sems.at[1])
            ck.start()
            cv.start()
            ck.wait()
            cv.wait()

    blk = pl.BlockSpec((QB, D_ATTN), lambda i: (i, 0))
    acc_shape = jax.ShapeDtypeStruct((s + KPAD, D_ATTN), F32)
    return _call(
        body, name=name, grid=(n,),
        in_specs=[blk, pl.BlockSpec(memory_space=pl.ANY), blk, _bias_spec()],
        out_specs=[blk, pl.BlockSpec(memory_space=pl.ANY), pl.BlockSpec(memory_space=pl.ANY), _full((N_HEADS, QB, KW))],
        out_shape=[jax.ShapeDtypeStruct((s, D_ATTN), BF16), acc_shape, acc_shape,
                   jax.ShapeDtypeStruct((N_HEADS, QB, KW), F32)],
        scratch_shapes=[pltpu.VMEM((s + KPAD, D_ATTN), BF16), pltpu.VMEM((s + KPAD, D_ATTN), BF16),
                        pltpu.VMEM((s + KPAD, D_ATTN), F32), pltpu.VMEM((s + KPAD, D_ATTN), F32),
                        pltpu.SemaphoreType.DMA((2,))],
        args=(qkv, qkv, do, bias), rider=rider)


CONV_HALO = 32
CONV_ROWS = 64


def _sigmoid(t):
    return 1.0 / (1.0 + jnp.exp(-t))


def _glu_rows(z_ref, r0, rows):
    a = z_ref[pl.ds(r0, rows), 0:D_CONV]
    b = z_ref[pl.ds(r0, rows), D_CONV:2 * D_CONV]
    return a, _sigmoid(b)


def _conv_fwd(zc, conv_w, conv_b, ln_g, ln_b, name):
    s = zc.shape[0]
    rt = min(256, s)

    def body(z_ref, w_ref, cb_ref, g_ref, b_ref, cv_ref, feat_ref, hpad):
        hpad[0:CONV_HALO, :] = jnp.zeros((CONV_HALO, D_CONV), F32)

        def glu(i, carry):
            r0 = pl.multiple_of(i * rt, rt)
            a, sb = _glu_rows(z_ref, r0, rt)
            hpad[pl.ds(r0 + CONV_HALO, rt), :] = a * sb
            return carry

        lax.fori_loop(0, s // rt, glu, 0)
        w = w_ref[...]

        def conv(i, carry):
            r0 = pl.multiple_of(i * CONV_ROWS, CONV_ROWS)
            win = hpad[pl.ds(r0, CONV_ROWS + CONV_HALO), :]
            acc = jnp.broadcast_to(cb_ref[...], (CONV_ROWS, D_CONV))
            for k in range(CONV_WIDTH):
                acc = acc + win[2 + k:2 + k + CONV_ROWS, :] * w[k:k + 1, :]
            cv_ref[pl.ds(r0, CONV_ROWS), :] = acc
            yhat, _ = _ln_hat(acc)
            y = yhat * g_ref[...] + b_ref[...]
            feat_ref[pl.ds(r0, CONV_ROWS), :] = (y * _sigmoid(y)).astype(BF16)
            return carry

        lax.fori_loop(0, s // CONV_ROWS, conv, 0)

    return pl.pallas_call(
        body, out_shape=[jax.ShapeDtypeStruct((s, D_CONV), F32), jax.ShapeDtypeStruct((s, D_CONV), BF16)],
        scratch_shapes=[pltpu.VMEM((s + CONV_HALO, D_CONV), F32)], name=name, compiler_params=_cparams(),
    )(zc, conv_w, conv_b, ln_g, ln_b)


def _conv_bwd(dfeat, cv, zc, conv_w, ln_g, ln_b, name):
    s = zc.shape[0]
    rt = min(256, s)

    def body(df_ref, cv_ref, z_ref, w_ref, g_ref, b_ref, dz_ref, dw_ref, dcb_ref, dg_ref, db_ref, hpad, dcvpad, dwacc):
        hpad[0:CONV_HALO, :] = jnp.zeros((CONV_HALO, D_CONV), F32)
        dcvpad[s:, :] = jnp.zeros((CONV_HALO, D_CONV), F32)
        dwacc[...] = jnp.zeros_like(dwacc)
        dcb_ref[...] = jnp.zeros_like(dcb_ref)
        dg_ref[...] = jnp.zeros_like(dg_ref)
        db_ref[...] = jnp.zeros_like(db_ref)

        def pass1(i, carry):
            r0 = pl.multiple_of(i * rt, rt)
            a, sb = _glu_rows(z_ref, r0, rt)
            hpad[pl.ds(r0 + CONV_HALO, rt), :] = a * sb
            cvhat, rstd = _ln_hat(cv_ref[pl.ds(r0, rt), :])
            y = cvhat * g_ref[...] + b_ref[...]
            sg = _sigmoid(y)
            dy = df_ref[pl.ds(r0, rt), :] * (sg * (1.0 + y * (1.0 - sg)))
            dg_ref[...] += jnp.sum(dy * cvhat, axis=0, keepdims=True)
            db_ref[...] += jnp.sum(dy, axis=0, keepdims=True)
            dcv = _ln_hat_bwd(dy * g_ref[...], cvhat, rstd)
            dcb_ref[...] += jnp.sum(dcv, axis=0, keepdims=True)
            dcvpad[pl.ds(r0, rt), :] = dcv
            return carry

        lax.fori_loop(0, s // rt, pass1, 0)
        w = w_ref[...]

        def pass2(i, carry):
            r0 = pl.multiple_of(i * CONV_ROWS, CONV_ROWS)
            dwin = dcvpad[pl.ds(r0, CONV_ROWS + CONV_HALO), :]
            hwin = hpad[pl.ds(r0, CONV_ROWS + CONV_HALO), :]
            dcv = dwin[0:CONV_ROWS, :]
            dh = jnp.zeros((CONV_ROWS, D_CONV), F32)
            for k in range(CONV_WIDTH):
                dh = dh + dwin[30 - k:30 - k + CONV_ROWS, :] * w[k:k + 1, :]
                prod = dcv * hwin[2 + k:2 + k + CONV_ROWS, :]
                dwacc[8 * k:8 * k + 8, :] += jnp.sum(prod.reshape(CONV_ROWS // 8, 8, D_CONV), axis=0)
            a, sb = _glu_rows(z_ref, r0, CONV_ROWS)
            dz_ref[pl.ds(r0, CONV_ROWS), :] = jnp.concatenate([dh * sb, dh * a * sb * (1.0 - sb)], axis=1).astype(BF16)
            return carry

        lax.fori_loop(0, s // CONV_ROWS, pass2, 0)
        dw_ref[...] = jnp.sum(dwacc[...].reshape(32, 8, D_CONV), axis=1)

    vs = jax.ShapeDtypeStruct((1, D_CONV), F32)
    return pl.pallas_call(
        body,
        out_shape=[jax.ShapeDtypeStruct((s, 2 * D_CONV), BF16), jax.ShapeDtypeStruct((32, D_CONV), F32), vs, vs, vs],
        scratch_shapes=[pltpu.VMEM((s + CONV_HALO, D_CONV), F32), pltpu.VMEM((s + CONV_HALO, D_CONV), F32),
                        pltpu.VMEM((256, D_CONV), F32)],
        name=name, compiler_params=_cparams(),
    )(dfeat, cv, zc, conv_w, ln_g, ln_b)


def _merge(zg, b_gate, ys, name):
    s = zg.shape[0]
    tm = _row_tile(s)

    def body(zg_ref, bg_ref, y0_ref, y1_ref, y2_ref, o_ref):
        acc = None
        for j, y_ref in enumerate((y0_ref, y1_ref, y2_ref)):
            cs = slice(D_MODEL * j, D_MODEL * (j + 1))
            t = _sigmoid(zg_ref[:, cs] + bg_ref[:, cs]) * y_ref[...]
            acc = t if acc is None else acc + t
        o_ref[...] = acc.astype(BF16)

    row = pl.BlockSpec((tm, D_MODEL), lambda i: (i, 0))
    return pl.pallas_call(
        body, grid=(s // tm,),
        in_specs=[pl.BlockSpec((tm, 3 * D_MODEL), lambda i: (i, 0)), _full((1, 3 * D_MODEL)), row, row, row],
        out_specs=row, out_shape=jax.ShapeDtypeStruct((s, D_MODEL), BF16), name=name, compiler_params=_cparams(),
    )(zg, b_gate, *ys)


def _merge_bwd(dm, zg, b_gate, ys, name):
    s = zg.shape[0]
    tm = min(256, s)

    def body(dm_ref, zg_ref, bg_ref, y0_ref, y1_ref, y2_ref, d0_ref, d1_ref, d2_ref, dzg_ref, dbg_ref):
        first = pl.program_id(0) == 0

        @pl.when(first)
        def _():
            dbg_ref[...] = jnp.zeros_like(dbg_ref)

        dmv = dm_ref[...]
        for j, (y_ref, d_ref) in enumerate(((y0_ref, d0_ref), (y1_ref, d1_ref), (y2_ref, d2_ref))):
            cs = slice(D_MODEL * j, D_MODEL * (j + 1))
            g = _sigmoid(zg_ref[:, cs] + bg_ref[:, cs])
            d_ref[...] = (dmv * g).astype(BF16)
            dzg = dmv * y_ref[...] * g * (1.0 - g)
            dzg_ref[:, cs] = dzg.astype(BF16)
            dbg_ref[:, cs] += jnp.sum(dzg, axis=0, keepdims=True)

    row = pl.BlockSpec((tm, D_MODEL), lambda i: (i, 0))
    wide = pl.BlockSpec((tm, 3 * D_MODEL), lambda i: (i, 0))
    yb = jax.ShapeDtypeStruct((s, D_MODEL), BF16)
    return pl.pallas_call(
        body, grid=(s // tm,),
        in_specs=[row, wide, _full((1, 3 * D_MODEL)), row, row, row],
        out_specs=[row, row, row, wide, _full((1, 3 * D_MODEL))],
        out_shape=[yb, yb, yb, jax.ShapeDtypeStruct((s, 3 * D_MODEL), BF16), jax.ShapeDtypeStruct((1, 3 * D_MODEL), F32)],
        name=name, compiler_params=_cparams(),
    )(dm, zg, b_gate, *ys)


def _ff_hidden(u2, w_ff1t, b_ff1, name, rider=None):
    s = u2.shape[0]
    tm, tn = min(1024, s), 1024

    def body(a_ref, b_ref, bias_ref, pre_ref, h_ref):
        acc = lax.dot_general(a_ref[...], b_ref[...], _DIMS["nt"], preferred_element_type=F32) + bias_ref[...]
        pre_ref[...] = acc.astype(BF16)
        h_ref[...] = _relu2(acc).astype(BF16)

    blk = pl.BlockSpec((tm, tn), lambda i, j: (i, j))
    sh = jax.ShapeDtypeStruct((s, D_FF), BF16)
    res = _call(body, name=name, grid=(s // tm, D_FF // tn),
                in_specs=[pl.BlockSpec((tm, D_MODEL), lambda i, j: (i, 0)), pl.BlockSpec((tn, D_MODEL), lambda i, j: (j, 0)),
                          pl.BlockSpec((1, tn), lambda i, j: (0, j))],
                out_specs=[blk, blk], out_shape=[sh, sh], scratch_shapes=[], args=(u2, w_ff1t, b_ff1), rider=rider)
    return tuple(res) if rider is None else (tuple(res[0]), res[1])


def _ff_hidden_bwd(dff, w_ff2, hpre, name):
    s = dff.shape[0]
    tm, tn = min(512, s), 1024

    def body(a_ref, b_ref, h_ref, o_ref, sum_ref):
        dh = lax.dot_general(a_ref[...], b_ref[...], _DIMS["nt"], preferred_element_type=F32)
        dpre = dh * (2.0 * jnp.maximum(h_ref[...].astype(F32), 0.0))
        o_ref[...] = dpre.astype(BF16)
        _acc_rows(sum_ref, dpre, pl.program_id(1) == 0)

    return pl.pallas_call(
        body, grid=(D_FF // tn, s // tm),
        in_specs=[pl.BlockSpec((tm, D_MODEL), lambda j, i: (i, 0)), pl.BlockSpec((tn, D_MODEL), lambda j, i: (j, 0)),
                  pl.BlockSpec((tm, tn), lambda j, i: (i, j))],
        out_specs=[pl.BlockSpec((tm, tn), lambda j, i: (i, j)), pl.BlockSpec((1, tn), lambda j, i: (0, j))],
        out_shape=[jax.ShapeDtypeStruct((s, D_FF), BF16), jax.ShapeDtypeStruct((1, D_FF), F32)],
        name=name, compiler_params=_cparams(),
    )(dff, w_ff2, hpre)


def _silu(t):
    return t * _sigmoid(t)


def _mod_fwd(c_all, w_ada_sh, b_ada_sh, name):
    cols = w_ada_sh.shape[2]

    def body(c_ref, w_ref, b_ref, o_ref):
        ca = _silu(c_ref[...]).astype(BF16)
        o_ref[0] = jnp.dot(ca, w_ref[0].astype(BF16), preferred_element_type=F32) + b_ref[0]

    return pl.pallas_call(
        body, grid=(DEPTH,),
        in_specs=[_full((N_DEV, D_MODEL)), pl.BlockSpec((1, D_MODEL, cols), lambda l: (l, 0, 0)),
                  pl.BlockSpec((1, 1, cols), lambda l: (l, 0, 0))],
        out_specs=pl.BlockSpec((1, N_DEV, cols), lambda l: (l, 0, 0)),
        out_shape=jax.ShapeDtypeStruct((DEPTH, N_DEV, cols), F32), name=name, compiler_params=_cparams(),
    )(c_all, w_ada_sh, b_ada_sh)


def _mod_bwd(c_all, dmod_sh, name):
    cols = dmod_sh.shape[2]

    def body(c_ref, d_ref, o_ref):
        ca = _silu(c_ref[...])
        o_ref[0] = lax.dot_general(ca, d_ref[0], _DIMS["tn"], precision=lax.Precision.HIGHEST,
                                   preferred_element_type=F32)

    return pl.pallas_call(
        body, grid=(DEPTH,),
        in_specs=[_full((N_DEV, D_MODEL)), pl.BlockSpec((1, N_DEV, cols), lambda l: (l, 0, 0))],
        out_specs=pl.BlockSpec((1, D_MODEL, cols), lambda l: (l, 0, 0)),
        out_shape=jax.ShapeDtypeStruct((DEPTH, D_MODEL, cols), F32), name=name, compiler_params=_cparams(),
    )(c_all, dmod_sh)


def _flat_tiles(rows, cols, itemsize_total):
    budget = 12 * 1024 * 1024
    tr = rows
    while tr % 32 == 0 and tr * cols * itemsize_total > budget:
        tr //= 2
    return tr


def _sum_cores(dw, recv, place, name):
    _, m, n = dw.shape
    tr = _flat_tiles(m, n, 6)

    def body(place_ref, a_ref, b_ref, o_ref):
        o_ref[...] = (a_ref[...].astype(F32) + b_ref[...].astype(F32)).astype(BF16)

    grid_spec = pltpu.PrefetchScalarGridSpec(
        num_scalar_prefetch=1, grid=(m // tr,),
        in_specs=[pl.BlockSpec((None, tr, n), lambda i, pr: (pr[0], i, 0)), pl.BlockSpec((tr, n), lambda i, pr: (i, 0))],
        out_specs=pl.BlockSpec((tr, n), lambda i, pr: (i, 0)))
    return pl.pallas_call(body, grid_spec=grid_spec, out_shape=jax.ShapeDtypeStruct((m, n), BF16), name=name,
                          compiler_params=_cparams())(place, dw, recv)


def _sum_chips(h, r, place, name):
    _, rs, n = h.shape
    tr = _flat_tiles(rs, n, 12)

    def body(place_ref, h_ref, r_ref, o_ref):
        o_ref[...] = ((h_ref[...].astype(F32) + r_ref[0].astype(F32)) + r_ref[1].astype(F32)) + r_ref[2].astype(F32)

    grid_spec = pltpu.PrefetchScalarGridSpec(
        num_scalar_prefetch=1, grid=(rs // tr,),
        in_specs=[pl.BlockSpec((None, tr, n), lambda i, pr: (pr[1], i, 0)), pl.BlockSpec((3, tr, n), lambda i, pr: (0, i, 0))],
        out_specs=pl.BlockSpec((tr, n), lambda i, pr: (i, 0)))
    return pl.pallas_call(body, grid_spec=grid_spec, out_shape=jax.ShapeDtypeStruct((rs, n), F32), name=name,
                          compiler_params=_cparams())(place, h, r)


def _adam_math(w, g, m, v):
    m2 = ADAM_B1 * m + (1.0 - ADAM_B1) * g
    v2 = ADAM_B2 * v + (1.0 - ADAM_B2) * (g * g)
    m_hat = m2 / (1.0 - ADAM_B1 ** ADAM_STEP)
    v_hat = v2 / (1.0 - ADAM_B2 ** ADAM_STEP)
    delta = -ADAM_LR * (m_hat / (jnp.sqrt(v_hat) + ADAM_EPS) + ADAM_WD * w)
    return delta, m2, v2


def _adamw(w, m, v, grads, name):
    r, c = w.shape
    tr = _flat_tiles(r, c, 4 * (7 + len(grads)))

    def body(*refs):
        w_ref, m_ref, v_ref = refs[:3]
        g_refs = refs[3:3 + len(grads)]
        g_ref, d_ref, m2_ref, v2_ref = refs[3 + len(grads):]
        g = g_refs[0][...]
        for gr in g_refs[1:]:
            g = g + gr[...]
        delta, m2, v2 = _adam_math(w_ref[...], g, m_ref[...], v_ref[...])
        g_ref[...] = g
        d_ref[...] = delta
        m2_ref[...] = m2
        v2_ref[...] = v2

    blk = pl.BlockSpec((tr, c), lambda i: (i, 0))
    sh = jax.ShapeDtypeStruct((r, c), F32)
    return pl.pallas_call(body, grid=(r // tr,), in_specs=[blk] * (3 + len(grads)), out_specs=[blk] * 4,
                          out_shape=[sh] * 4, name=name, compiler_params=_cparams())(w, m, v, *grads)


def _adamw_halves(w, m, v, own, other, place, split, name, rider=None):
    nl, r, c = w.shape
    hr, hc = own[0].shape
    tr = _flat_tiles(hr, hc, 4 * (7 + 2 * nl))
    nt = hr // tr
    if split == "rows":
        w_spec = pl.BlockSpec((None, tr, c), lambda l, h, t, pr: (l, h * nt + t, 0))
    else:
        w_spec = pl.BlockSpec((None, tr, hc), lambda l, h, t, pr: (l, t, h))

    def g_spec(layer):
        return pl.BlockSpec((tr, hc), lambda l, h, t, pr: (jnp.where(l == layer, t, nt - 1), 0))

    def body(place_ref, w_ref, m_ref, v_ref, *refs):
        own_refs, other_refs = refs[:nl], refs[nl:2 * nl]
        g_ref, d_ref, m2_ref, v2_ref = refs[2 * nl:]
        layer = pl.program_id(0)
        mine = pl.program_id(1) == place_ref[0]
        g = None
        for li in range(nl):
            cand = jnp.where(mine, own_refs[li][...], other_refs[li][...])
            g = cand if g is None else jnp.where(layer == li, cand, g)
        delta, m2, v2 = _adam_math(w_ref[...], g, m_ref[...], v_ref[...])
        g_ref[...] = g
        d_ref[...] = delta
        m2_ref[...] = m2
        v2_ref[...] = v2

    sh = jax.ShapeDtypeStruct((nl, r, c), F32)
    return _call(body, name=name, grid=(nl, 2, nt), in_specs=[w_spec] * 3 + [g_spec(li) for li in range(nl)] * 2,
                 out_specs=[w_spec] * 4, out_shape=[sh] * 4, scratch_shapes=[], args=(w, m, v, *own, *other),
                 rider=rider, prefetch=(place,))


def _adamw_small(w, m, v, g_all, name):
    r, c = w.shape

    def body(w_ref, m_ref, v_ref, g_ref, go_ref, d_ref, m2_ref, v2_ref):
        g = g_ref[0]
        for b in range(1, N_DEV):
            g = g + g_ref[b]
        delta, m2, v2 = _adam_math(w_ref[...], g, m_ref[...], v_ref[...])
        go_ref[...] = g
        d_ref[...] = delta
        m2_ref[...] = m2
        v2_ref[...] = v2

    sh = jax.ShapeDtypeStruct((r, c), F32)
    return pl.pallas_call(body, out_shape=[sh] * 4, name=name, compiler_params=_cparams())(w, m, v, g_all)


def _me():
    return lax.axis_index("x"), lax.axis_index("y"), lax.axis_index("c")


def _flip(v, bit):
    return 1 - v if bit else v


def _allgather_small(blk, name):
    r, c = blk.shape

    def body(x_ref, o_ref, send_sems, recv_sems):
        x, y, cc = _me()
        me = 4 * x + 2 * y + cc
        copies = []
        for k in range(1, N_DEV):
            peer = (_flip(x, k & 4), _flip(y, k & 2), _flip(cc, k & 1))
            cp = pltpu.make_async_remote_copy(src_ref=x_ref, dst_ref=o_ref.at[me], send_sem=send_sems.at[k - 1],
                                              recv_sem=recv_sems.at[k - 1], device_id=peer, device_id_type=MESH)
            cp.start()
            copies.append(cp)
        o_ref[me] = x_ref[...]
        for cp in copies:
            cp.wait()

    return pl.pallas_call(
        body, out_shape=jax.ShapeDtypeStruct((N_DEV, r, c), F32),
        in_specs=[pl.BlockSpec(memory_space=pltpu.VMEM)], out_specs=pl.BlockSpec(memory_space=pltpu.VMEM),
        scratch_shapes=[pltpu.SemaphoreType.DMA((N_DEV - 1,)), pltpu.SemaphoreType.DMA((N_DEV - 1,))],
        name=name, compiler_params=_cparams(),
    )(blk)


class _Rider:
    def __init__(self, arrays, out_shapes, scratch_shapes, start, finish):
        self.arrays, self.out_shapes, self.scratch_shapes = list(arrays), list(out_shapes), list(scratch_shapes)
        self.start, self.finish = start, finish


def _call(body, *, name, grid, in_specs, out_specs, out_shape, scratch_shapes, args, rider=None, prefetch=()):
    npf = len(prefetch)

    def launch(fn, in_specs, out_specs, out_shape, scratch_shapes, args):
        grid_spec = pltpu.PrefetchScalarGridSpec(num_scalar_prefetch=npf, grid=grid, in_specs=in_specs,
                                                 out_specs=out_specs, scratch_shapes=scratch_shapes)
        return pl.pallas_call(fn, grid_spec=grid_spec, out_shape=out_shape, name=name,
                              compiler_params=_cparams())(*prefetch, *args)

    if rider is None:
        return launch(body, list(in_specs), list(out_specs), list(out_shape), list(scratch_shapes), args)
    ni, no, ns = len(in_specs), len(out_specs), len(scratch_shapes)
    ri, ro = len(rider.arrays), len(rider.out_shapes)
    steps = int(np.prod(grid))

    def wrapped(*refs):
        pf, refs = refs[:npf], refs[npf:]
        h_in, r_in = refs[:ni], refs[ni:ni + ri]
        h_out, r_out = refs[ni + ri:ni + ri + no], refs[ni + ri + no:ni + ri + no + ro]
        h_scr, r_scr = refs[ni + ri + no + ro:ni + ri + no + ro + ns], refs[ni + ri + no + ro + ns:]
        step = pl.program_id(0)
        for d in range(1, len(grid)):
            step = step * grid[d] + pl.program_id(d)

        @pl.when(step == 0)
        def _():
            rider.start(r_in, r_out, r_scr)

        body(*pf, *h_in, *h_out, *h_scr)

        @pl.when(step == steps - 1)
        def _():
            rider.finish(r_in, r_out, r_scr)

    anyspec = pl.BlockSpec(memory_space=pl.ANY)
    res = launch(wrapped, list(in_specs) + [anyspec] * ri, list(out_specs) + [anyspec] * ro,
                 list(out_shape) + rider.out_shapes, list(scratch_shapes) + rider.scratch_shapes,
                 list(args) + rider.arrays)
    return res[:no], res[no:]


def _run_rider(rider, name):
    ri = len(rider.arrays)

    def body(*refs):
        r_in, r_out, r_scr = refs[:ri], refs[ri:ri + len(rider.out_shapes)], refs[ri + len(rider.out_shapes):]
        rider.start(r_in, r_out, r_scr)
        rider.finish(r_in, r_out, r_scr)

    anyspec = pl.BlockSpec(memory_space=pl.ANY)
    return pl.pallas_call(body, in_specs=[anyspec] * ri, out_specs=[anyspec] * len(rider.out_shapes),
                          out_shape=rider.out_shapes, scratch_shapes=rider.scratch_shapes, name=name,
                          compiler_params=_cparams())(*rider.arrays)


def _allgather_rider(blk):
    def copies(ins, outs, scr):
        send_sems, recv_sems, loc_sems, stage = scr
        x, y, cc = _me()
        me = 4 * x + 2 * y + cc
        remote = [pltpu.make_async_remote_copy(
            src_ref=ins[0], dst_ref=outs[0].at[me], send_sem=send_sems.at[k - 1], recv_sem=recv_sems.at[k - 1],
            device_id=(_flip(x, k & 4), _flip(y, k & 2), _flip(cc, k & 1)), device_id_type=MESH) for k in range(1, N_DEV)]
        return remote, pltpu.make_async_copy(ins[0], stage, loc_sems.at[0]), (outs[0].at[me], stage, loc_sems.at[1])

    def start(ins, outs, scr):
        remote, lin, _ = copies(ins, outs, scr)
        lin.start()
        for cp in remote:
            cp.start()

    def finish(ins, outs, scr):
        remote, lin, (dst, stage, sem) = copies(ins, outs, scr)
        lin.wait()
        lout = pltpu.make_async_copy(stage, dst, sem)
        lout.start()
        for cp in remote:
            cp.wait()
        lout.wait()

    return _Rider([blk], [jax.ShapeDtypeStruct((N_DEV,) + blk.shape, blk.dtype)],
                  [pltpu.SemaphoreType.DMA((N_DEV - 1,)), pltpu.SemaphoreType.DMA((N_DEV - 1,)),
                   pltpu.SemaphoreType.DMA((2,)), pltpu.VMEM(blk.shape, blk.dtype)], start, finish)


def _gather_rider(shards):
    n = len(shards)

    def copies(ins, outs, scr, relay=True):
        ici_send, ici_recv, d2d_send, d2d_recv, loc_sems = scr[:5]
        stage = scr[5:]
        x, y, cc = _me()
        chip = 2 * x + y
        sibling = (x, y, 1 - cc)
        local, sends, relays = [], [], []
        for j in range(n):
            def rows(ch, h, j=j):
                return outs[j].at[ch, h]

            lc = pltpu.make_async_copy(ins[j], stage[j], loc_sems.at[j])
            local.append((lc, pltpu.make_async_copy(stage[j], outs[j].at[chip], loc_sems.at[n + j]) if relay else None))
            for k in range(1, N_CHIP):
                px, py = _flip(x, k & 2), _flip(y, k & 1)
                pchip = 2 * px + py
                q = 3 * j + k - 1
                out_cp = pltpu.make_async_remote_copy(src_ref=ins[j].at[cc], dst_ref=rows(chip, cc),
                                                      send_sem=ici_send.at[q], recv_sem=ici_recv.at[q],
                                                      device_id=(px, py, cc), device_id_type=MESH)
                sends.append(out_cp)
                if not relay:
                    continue
                arrival = pltpu.make_async_remote_copy(src_ref=rows(pchip, cc), dst_ref=rows(pchip, cc),
                                                       send_sem=ici_send.at[q], recv_sem=ici_recv.at[q],
                                                       device_id=(px, py, cc), device_id_type=MESH)
                forward = pltpu.make_async_remote_copy(src_ref=rows(pchip, cc), dst_ref=rows(pchip, cc),
                                                       send_sem=d2d_send.at[q], recv_sem=d2d_recv.at[q],
                                                       device_id=sibling, device_id_type=MESH)
                from_sibling = pltpu.make_async_remote_copy(src_ref=rows(pchip, 1 - cc), dst_ref=rows(pchip, 1 - cc),
                                                            send_sem=d2d_send.at[q], recv_sem=d2d_recv.at[q],
                                                            device_id=sibling, device_id_type=MESH)
                relays.append((arrival, forward, from_sibling))
        return local, sends, relays

    def start(ins, outs, scr):
        local, sends, _ = copies(ins, outs, scr, relay=False)
        for lin, _ in local:
            lin.start()
        for cp in sends:
            cp.start()

    def finish(ins, outs, scr):
        local, sends, relays = copies(ins, outs, scr)
        for lin, lout in local:
            lin.wait()
            lout.start()
        for arrival, forward, _ in relays:
            arrival.wait_recv()
            forward.start()
        for cp in sends:
            cp.wait_send()
        for _, forward, from_sibling in relays:
            forward.wait_send()
            from_sibling.wait_recv()
        for _, lout in local:
            lout.wait()

    scratch = [pltpu.SemaphoreType.DMA((3 * n,)), pltpu.SemaphoreType.DMA((3 * n,)), pltpu.SemaphoreType.DMA((3 * n,)),
               pltpu.SemaphoreType.DMA((3 * n,)), pltpu.SemaphoreType.DMA((2 * n,))]
    scratch += [pltpu.VMEM(a.shape, a.dtype) for a in shards]
    return _Rider(shards, [jax.ShapeDtypeStruct((N_CHIP,) + a.shape, a.dtype) for a in shards], scratch, start, finish)


def _sibling_send(arrs, name, other_half=False):
    n = len(arrs)

    def body(*refs):
        ins, outs = refs[:n], refs[n:2 * n]
        send_sems, recv_sems = refs[2 * n:]
        x, y, cc = _me()
        pending = []
        for j in range(n):
            src = ins[j].at[1 - cc] if other_half else ins[j]
            cp = pltpu.make_async_remote_copy(src_ref=src, dst_ref=outs[j], send_sem=send_sems.at[j],
                                              recv_sem=recv_sems.at[j], device_id=(x, y, 1 - cc), device_id_type=MESH)
            cp.start()
            pending.append(cp)
        for cp in pending:
            cp.wait()

    anyspec = pl.BlockSpec(memory_space=pl.ANY)
    return pl.pallas_call(
        body, out_shape=[jax.ShapeDtypeStruct(a.shape[1:] if other_half else a.shape, a.dtype) for a in arrs],
        in_specs=[anyspec] * n, out_specs=[anyspec] * n,
        scratch_shapes=[pltpu.SemaphoreType.DMA((n,)), pltpu.SemaphoreType.DMA((n,))],
        name=name, compiler_params=_cparams(),
    )(*arrs)


def _scatter_rider(arrs):
    n = len(arrs)

    def copies(ins, outs, scr):
        send_sems, recv_sems = scr
        x, y, cc = _me()
        cps = []
        for j in range(n):
            for k in range(1, N_CHIP):
                px, py = _flip(x, k & 2), _flip(y, k & 1)
                cps.append(pltpu.make_async_remote_copy(
                    src_ref=ins[j].at[2 * px + py], dst_ref=outs[j].at[k - 1], send_sem=send_sems.at[3 * j + k - 1],
                    recv_sem=recv_sems.at[3 * j + k - 1], device_id=(px, py, cc), device_id_type=MESH))
        return cps

    def start(ins, outs, scr):
        for cp in copies(ins, outs, scr):
            cp.start()

    def finish(ins, outs, scr):
        for cp in copies(ins, outs, scr):
            cp.wait()

    return _Rider(arrs, [jax.ShapeDtypeStruct((N_CHIP - 1,) + a.shape[1:], a.dtype) for a in arrs],
                  [pltpu.SemaphoreType.DMA((3 * n,)), pltpu.SemaphoreType.DMA((3 * n,))], start, finish)


COL_SHARDED = ("w_in", "w_br_pool", "w_br_attn", "w_br_conv", "w_ff1")
ROW_SHARDED = ("w_o", "w_ff2")
BIG = COL_SHARDED + ROW_SHARDED
SMALL = ("b_ada", "b_gate", "w_pool", "pool_scale", "rel_bias", "conv_w", "conv_b", "conv_ln_g", "conv_ln_b",
         "ln_mix_g", "ln_mix_b", "b_ff1", "b_ff2", "ln_ff_g", "ln_ff_b")
PACK_W = 1024


def _pack(parts):
    rows = []
    for a in parts:
        flat = a.reshape(-1)
        n = -(-flat.shape[0] // PACK_W) * PACK_W
        rows.append(jnp.pad(flat, (0, n - flat.shape[0])).reshape(-1, PACK_W))
    out = jnp.concatenate(rows, axis=0)
    r = -(-out.shape[0] // 8) * 8
    return jnp.pad(out, ((0, r - out.shape[0]), (0, 0)))


def _unpack(packed, shapes):
    out, r0 = [], 0
    for shp in shapes:
        size = int(np.prod(shp))
        nr = -(-size // PACK_W)
        out.append(packed[r0:r0 + nr].reshape(-1)[:size].reshape(shp))
        r0 += nr
    return out


def _hosted(fn, hook, *args, **kw):
    if hook is None:
        return fn(*args, **kw)
    res, rider_out = fn(*args, rider=hook[0], **kw)
    hook[1](rider_out)
    return res


def _layer_fwd(l, x, mod, W, P, hooks=None):
    hooks = hooks or {}
    s = x.shape[0]
    sh_m, sc_m, g_m, sh_f, sc_f, g_f = [mod[l:l + 1, D_MODEL * j:D_MODEL * (j + 1)] for j in range(6)]
    n = lambda t: f"{t}{l}"
    w_in = W["w_in"][l]
    u = _ln_mod(x, sc_m, sh_m, n("ln_mod_mix"))
    tmz = min(1024, s)
    zp = _mm(u, w_in, "nt", tm=min(2048, s), tn=256, out_dtype=F32, name=n("z_pool"), b_col0=0, n_out=D_POOL)
    qkv = _mm(u, w_in, "nt", tm=tmz, tn=256, out_dtype=BF16, name=n("z_qkv"), b_col0=OFF_QKV // 256, n_out=3 * D_ATTN)
    zc = _mm(u, w_in, "nt", tm=tmz, tn=256, out_dtype=F32, name=n("z_conv"), b_col0=OFF_CONV // 256, n_out=2 * D_CONV)
    zg = _mm(u, w_in, "nt", tm=tmz, tn=768, out_dtype=BF16, name=n("z_gate"), b_col0=OFF_GATE // 768, n_out=3 * D_MODEL)

    p, feat_pool = _pool_fwd(zp, P["wp_bd"][l], P["pool_scale"][l], n("pool_fwd"))
    bias = _bias_block(P["rel_bias"][l], n("bias_block"))
    o = _hosted(_attn_fwd, hooks.get("attn"), qkv, bias, n("attn_fwd"))
    cv, feat_conv = _conv_fwd(zc, P["conv_w"][l], P["conv_b"][l], P["conv_ln_g"][l], P["conv_ln_b"][l], n("conv_fwd"))

    tmb = min(1024, s)
    y_pool = _mm(feat_pool, W["w_br_pool"][l], "nt", tm=tmb, tn=1024, out_dtype=F32, name=n("y_pool"))
    y_attn = _mm(o, W["w_br_attn"][l], "nt", tm=tmb, tn=1024, out_dtype=F32, name=n("y_attn"))
    y_conv = _mm(feat_conv, W["w_br_conv"][l], "nt", tm=tmb, tn=1024, out_dtype=F32, name=n("y_conv"))
    ys = (y_pool, y_attn, y_conv)
    merged = _merge(zg, P["b_gate"][l], ys, n("merge"))
    mix = _mm(merged, W["w_o"][l], "nn", tm=tmb, tn=1024, out_dtype=F32, name=n("mix_out"))
    x1 = _resid_ln(x, mix, g_m, P["ln_mix_g"][l], P["ln_mix_b"][l], n("resid_ln_mix"))

    u2 = _ln_mod(x1, sc_f, sh_f, n("ln_mod_ff"))
    hpre, hid = _hosted(_ff_hidden, hooks.get("ff1"), u2, W["w_ff1"][l], P["b_ff1"][l], n("ff1"))
    ff = _hosted(_mm, hooks.get("ff2"), hid, W["w_ff2"][l], "nn", tm=min(512, s), tn=1024, out_dtype=F32,
                 name=n("ff2"), bias=P["b_ff2"][l])
    x2 = _resid_ln(x1, ff, g_f, P["ln_ff_g"][l], P["ln_ff_b"][l], n("resid_ln_ff"))
    saved = dict(x=x, u=u, zp=zp, qkv=qkv, zc=zc, zg=zg, p=p, feat_pool=feat_pool, bias=bias, o=o, cv=cv,
                 feat_conv=feat_conv, ys=ys, merged=merged, mix=mix, x1=x1, u2=u2, hpre=hpre, hid=hid, ff=ff)
    return x2, saved


def _layer_bwd(l, dx2, mod, W, P, A, hooks=None):
    hooks = hooks or {}
    s = dx2.shape[0]
    sh_m, sc_m, g_m, sh_f, sc_f, g_f = [mod[l:l + 1, D_MODEL * j:D_MODEL * (j + 1)] for j in range(6)]
    n = lambda t: f"{t}{l}"
    tmb = min(1024, s)
    gw, gs = {}, {}

    dres, dff, gs["ln_ff_g"], gs["ln_ff_b"], dg_f, gs["b_ff2"] = _resid_ln_bwd(
        dx2, A["x1"], A["ff"], g_f, P["ln_ff_g"][l], n("resid_ln_ff_bwd"))
    gw["w_ff2"] = _mm(A["hid"], dff, "tn", tm=1024, tn=512, out_dtype=BF16, name=n("dw_ff2"), split_n=True)
    dhpre, gs["b_ff1"] = _ff_hidden_bwd(dff, W["w_ff2"][l], A["hpre"], n("ff_hidden_bwd"))
    gw["w_ff1"] = _mm(dhpre, A["u2"], "tn", tm=1024, tn=512, out_dtype=BF16, name=n("dw_ff1"), split_n=True)
    du2 = _mm(dhpre, W["w_ff1"][l], "nn", tm=min(512, s), tn=512, out_dtype=F32, name=n("du_ff"))
    dx1, dsc_f, dsh_f = _ln_mod_bwd(du2, A["x1"], sc_f, dres, n("ln_mod_ff_bwd"))

    dres, dmix, gs["ln_mix_g"], gs["ln_mix_b"], dg_m, _ = _resid_ln_bwd(
        dx1, A["x"], A["mix"], g_m, P["ln_mix_g"][l], n("resid_ln_mix_bwd"))
    gw["w_o"] = _mm(A["merged"], dmix, "tn", tm=1024, tn=512, out_dtype=BF16, name=n("dw_o"), split_n=True)
    dmerged = _mm(dmix, W["w_o"][l], "nt", tm=tmb, tn=1024, out_dtype=F32, name=n("d_merged"))
    dy_pool, dy_attn, dy_conv, dzg, gs["b_gate"] = _merge_bwd(dmerged, A["zg"], P["b_gate"][l], A["ys"], n("merge_bwd"))

    gw["w_br_pool"] = _mm(dy_pool, A["feat_pool"], "tn", tm=1024, tn=128, out_dtype=BF16, name=n("dw_br_pool"),
                          split_n=True)
    gw["w_br_attn"] = _mm(dy_attn, A["o"], "tn", tm=1024, tn=256, out_dtype=BF16, name=n("dw_br_attn"), split_n=True)
    gw["w_br_conv"] = _mm(dy_conv, A["feat_conv"], "tn", tm=1024, tn=128, out_dtype=BF16, name=n("dw_br_conv"),
                          split_n=True)
    dfeat_pool = _mm(dy_pool, W["w_br_pool"][l], "nn", tm=tmb, tn=256, out_dtype=F32, name=n("d_feat_pool"))
    do = _mm(dy_attn, W["w_br_attn"][l], "nn", tm=tmb, tn=512, out_dtype=BF16, name=n("d_attn_out"))
    dfeat_conv = _mm(dy_conv, W["w_br_conv"][l], "nn", tm=tmb, tn=256, out_dtype=F32, name=n("d_feat_conv"))

    dzp, dwp_bd, gs["pool_scale"] = _pool_bwd(dfeat_pool, A["p"], P["wp_bd"][l], P["pool_scale"][l], n("pool_bwd"))
    gs["w_pool"] = jnp.stack([dwp_bd[POOL_GROUP * g:POOL_GROUP * (g + 1), POOL_GROUP * g:POOL_GROUP * (g + 1)]
                              for g in range(len(POOL_WINDOWS))])
    hook = hooks["attn"](gw) if "attn" in hooks else None
    dq, dk, dv, ds_acc = _hosted(_attn_bwd, hook, A["qkv"], do, A["bias"], n("attn_bwd"))
    gs["rel_bias"] = _bias_block_bwd(ds_acc, n("bias_block_bwd"))
    dzc, dcw, gs["conv_b"], gs["conv_ln_g"], gs["conv_ln_b"] = _conv_bwd(
        dfeat_conv, A["cv"], A["zc"], P["conv_w"][l], P["conv_ln_g"][l], P["conv_ln_b"][l], n("conv_bwd"))
    gs["conv_w"] = dcw[:CONV_WIDTH]

    dz = jnp.concatenate([dzp, dq, dk[KPAD:].astype(BF16), dv[KPAD:].astype(BF16), dzc, dzg], axis=1)
    gw["w_in"] = _mm(dz, A["u"], "tn", tm=768, tn=512, out_dtype=BF16, name=n("dw_in"), split_n=True)
    hook = hooks["du_mix"](gw) if "du_mix" in hooks else None
    du = _hosted(_mm, hook, dz, W["w_in"][l], "nn", tm=min(512, s), tn=512, out_dtype=F32, name=n("du_mix"))
    dx, dsc_m, dsh_m = _ln_mod_bwd(du, A["x"], sc_m, dres, n("ln_mod_mix_bwd"))
    dmod = jnp.concatenate([dsh_m, dsc_m, dg_m, dsh_f, dsc_f, dg_f], axis=1)
    return dx, gw, gs, dmod


def _small_shapes():
    return {"b_ada": (6 * D_MODEL,), "b_gate": (3 * D_MODEL,), "w_pool": (4, POOL_GROUP, POOL_GROUP),
            "pool_scale": (D_POOL,), "rel_bias": (N_HEADS, N_REL), "conv_w": (CONV_WIDTH, D_CONV),
            "conv_b": (D_CONV,), "conv_ln_g": (D_CONV,), "conv_ln_b": (D_CONV,), "ln_mix_g": (D_MODEL,),
            "ln_mix_b": (D_MODEL,), "b_ff1": (D_FF,), "b_ff2": (D_MODEL,), "ln_ff_g": (D_MODEL,), "ln_ff_b": (D_MODEL,)}


def kernel(x, c, w_ada, b_ada, w_in, b_gate, w_pool, pool_scale, rel_bias, conv_w, conv_b, conv_ln_g, conv_ln_b, w_br_pool, w_br_attn, w_br_conv, w_o, ln_mix_g, ln_mix_b, w_ff1, b_ff1, w_ff2, b_ff2, ln_ff_g, ln_ff_b, loss_target, m_w_ada, m_b_ada, m_w_in, m_b_gate, m_w_pool, m_pool_scale, m_rel_bias, m_conv_w, m_conv_b, m_conv_ln_g, m_conv_ln_b, m_w_br_pool, m_w_br_attn, m_w_br_conv, m_w_o, m_ln_mix_g, m_ln_mix_b, m_w_ff1, m_b_ff1, m_w_ff2, m_b_ff2, m_ln_ff_g, m_ln_ff_b, v_w_ada, v_b_ada, v_w_in, v_b_gate, v_w_pool, v_pool_scale, v_rel_bias, v_conv_w, v_conv_b, v_conv_ln_g, v_conv_ln_b, v_w_br_pool, v_w_br_attn, v_w_br_conv, v_w_o, v_ln_mix_g, v_ln_mix_b, v_w_ff1, v_b_ff1, v_w_ff2, v_b_ff2, v_ln_ff_g, v_ln_ff_b):
    env = dict(locals())
    xi, yi, ci = _me()
    chip = 2 * xi + yi
    me = 4 * xi + 2 * yi + ci
    xs = x[0]
    tgt = loss_target[0]
    L = DEPTH

    c_all = _allgather_small(c.reshape(8, 128), "gather_c").reshape(N_DEV, D_MODEL)
    ada_cols = w_ada.shape[2]
    b_ada_sh = lax.dynamic_slice_in_dim(b_ada, chip * ada_cols, ada_cols, axis=1).reshape(L, 1, ada_cols)
    mod_part = _mod_fwd(c_all, w_ada, b_ada_sh, "mod_fwd")
    mod_g = _allgather_small(mod_part.reshape(-1, 128), "gather_mod").reshape(N_CHIP, 2, L, N_DEV, ada_cols)[:, 0]
    mod_all = jnp.transpose(mod_g, (1, 2, 0, 3)).reshape(L, N_DEV, 6 * D_MODEL)
    mod = lax.dynamic_index_in_dim(mod_all, me, axis=1, keepdims=False)

    W = {k: [None] * L for k in BIG}

    def weight_gather(names, l):
        shards = [(jnp.swapaxes(env[k][l], 0, 1) if k in COL_SHARDED else env[k][l]).astype(BF16) for k in names]
        shards = [a.reshape(2, a.shape[0] // 2, a.shape[1]) for a in shards]

        def done(outs):
            for k, g in zip(names, outs):
                W[k][l] = g.reshape(-1, g.shape[-1])

        return _gather_rider(shards), done

    first_names = ("w_in", "w_br_pool", "w_br_attn", "w_br_conv", "w_o")
    late_names = ("w_ff1", "w_ff2")
    rider, done = weight_gather(first_names, 0)
    done(_run_rider(rider, "gather_weights_first0"))
    fwd_hooks = [{"attn": weight_gather(late_names, 0), "ff1": weight_gather(("w_in",), 1),
                  "ff2": weight_gather(("w_br_pool", "w_br_attn", "w_br_conv", "w_o"), 1)},
                 {"attn": weight_gather(late_names, 1)}]

    P = {k: env[k] for k in ("rel_bias", "conv_w")}
    for k in ("b_gate", "pool_scale", "conv_b", "conv_ln_g", "conv_ln_b", "ln_mix_g", "ln_mix_b", "b_ff1", "b_ff2",
              "ln_ff_g", "ln_ff_b"):
        P[k] = env[k].reshape(L, 1, -1)
    conv_w_full = _allgather_small(_pack([conv_w]), "gather_conv_w")
    n_cw = conv_w.size
    cw = conv_w_full.reshape(N_CHIP, 2, -1)[:, 0, :n_cw].reshape(N_CHIP, L, CONV_WIDTH, D_CONV // N_CHIP)
    P["conv_w"] = jnp.transpose(cw, (1, 2, 0, 3)).reshape(L, CONV_WIDTH, D_CONV)
    wp_bd = jnp.zeros((L, D_POOL, D_POOL), F32)
    for g in range(len(POOL_WINDOWS)):
        sl = slice(POOL_GROUP * g, POOL_GROUP * (g + 1))
        wp_bd = wp_bd.at[:, sl, sl].set(w_pool[:, g])
    P["wp_bd"] = wp_bd.astype(BF16)

    acts = []
    h = xs
    for l in range(L):
        h, saved = _layer_fwd(l, h, mod, W, P, fwd_hooks[l])
        acts.append(saved)
    dy, loss_part = _loss_grad(h, tgt, "loss_grad")
    loss = lax.psum(loss_part[0, 0], ("x", "y", "c"))

    place = jnp.stack([ci, chip, chip ^ 1, chip ^ 2, chip ^ 3]).astype(jnp.int32)
    scattered = {}

    def grad_scatter(items, tag):
        dws = [dw for _, _, dw in items]
        got = _sibling_send(dws, f"swap_blocks_{tag}", other_half=True)
        both = [_sum_cores(a, b, place, f"sum_cores_{k}{l}") for (k, l, _), a, b in zip(items, dws, got)]
        both = [hh.reshape(N_CHIP, -1, hh.shape[-1]) for hh in both]

        def done(outs):
            for (k, l, _), hh, r in zip(items, both, outs):
                scattered[(k, l)] = (hh, r)

        return _scatter_rider(both), done

    early = ("w_ff2", "w_ff1", "w_o", "w_br_pool", "w_br_attn", "w_br_conv")
    left_over = []

    def attn_hook(l):
        def hook(gw):
            items = left_over + [(k, l, gw[k]) for k in early]
            left_over.clear()
            return grad_scatter(items, f"attn{l}")
        return hook

    def last_hook(gw):
        return grad_scatter([("w_in", 0, gw["w_in"])], "last")

    gws, gss, dmods = [None] * L, [None] * L, [None] * L
    dh = dy
    for l in reversed(range(L)):
        hooks = {"attn": attn_hook(l)}
        if l == 0:
            hooks["du_mix"] = last_hook
        dh, gws[l], gss[l], dmods[l] = _layer_bwd(l, dh, mod, W, P, acts[l], hooks)
        if l > 0:
            left_over.append(("w_in", l, gws[l]["w_in"]))
    grad_x = dh[None]

    reduced = [[_sum_chips(*scattered[(k, l)], place, f"sum_chips_{k}{l}") for l in range(L)] for k in BIG]
    flat_reduced = [t for per_weight in reduced for t in per_weight]
    flat_other = _sibling_send(flat_reduced, "swap_reduced")

    shapes = _small_shapes()
    small_names = [k for k in SMALL if k != "b_ada"]
    dmod_own = jnp.concatenate(dmods, axis=0)
    pack = _pack([dmod_own] + [jnp.stack([gss[l][k].reshape(shapes[k]) for l in range(L)]) for k in small_names])
    small_rider = _allgather_rider(pack.reshape(-1, 128))

    out = {}
    for j, k in enumerate(BIG):
        own, other = reduced[j], flat_other[L * j:L * (j + 1)]
        rider = small_rider if j == 0 else None
        if k == "w_in":
            t = lambda a: jnp.swapaxes(a, 1, 2)
            res = _adamw_halves(t(env[k]), t(env["m_" + k]), t(env["v_" + k]), own, other, place, "cols",
                                f"adamw_{k}", rider=rider)
            res, rider_out = res if rider is not None else (res, None)
            res = [t(a) for a in res]
        else:
            if k in COL_SHARDED:
                own, other = [a.T for a in own], [a.T for a in other]
            res = _adamw_halves(env[k], env["m_" + k], env["v_" + k], own, other, place,
                                "rows" if k in COL_SHARDED else "cols", f"adamw_{k}", rider=rider)
            res, rider_out = res if rider is not None else (res, None)
        if rider is not None:
            g_all = rider_out[0].reshape(N_DEV, -1, PACK_W)
        out[k] = tuple(res)

    dmod_all = g_all[:, :L * 6].reshape(N_DEV, L, 6 * D_MODEL)
    dmod_sh = jnp.transpose(lax.dynamic_slice_in_dim(dmod_all, chip * ada_cols, ada_cols, axis=2), (1, 0, 2))
    g_ada = _mod_bwd(c_all, dmod_sh, "mod_bwd")
    g_, d_, m_, v_ = _adamw(w_ada.reshape(-1, ada_cols), m_w_ada.reshape(-1, ada_cols), v_w_ada.reshape(-1, ada_cols),
                            [g_ada.reshape(-1, ada_cols)], "adamw_w_ada")
    out["w_ada"] = tuple(a.reshape(w_ada.shape) for a in (g_, d_, m_, v_))

    def small_pack(prefix):
        parts = [env[prefix + "b_ada"]]
        for k in small_names:
            a = env[prefix + k]
            if k == "conv_w":
                a = jnp.zeros((L,) + shapes[k], F32)
            parts.append(a)
        return _pack(parts)

    gp, dp, mp, vp = _adamw_small(small_pack(""), small_pack("m_"), small_pack("v_"), g_all, "adamw_small")
    full_shapes = [(L,) + shapes["b_ada"]] + [(L,) + shapes[k] for k in small_names]
    for tag, packed in (("g", gp), ("d", dp), ("m", mp), ("v", vp)):
        for k, a in zip(["b_ada"] + small_names, _unpack(packed, full_shapes)):
            out.setdefault(k, {})
            out[k][tag] = a
    g_cw_full = out["conv_w"]["g"]
    cw_cols = D_CONV // N_CHIP
    g_cw = lax.dynamic_slice_in_dim(g_cw_full, chip * cw_cols, cw_cols, axis=2)
    pad_rows = lambda a: jnp.pad(a.reshape(L * CONV_WIDTH, cw_cols), ((0, 2), (0, 0)))
    g_, d_, m_, v_ = _adamw(pad_rows(conv_w), pad_rows(m_conv_w), pad_rows(v_conv_w), [pad_rows(g_cw)], "adamw_conv_w")
    out["conv_w"] = tuple(a[:L * CONV_WIDTH].reshape(L, CONV_WIDTH, cw_cols) for a in (g_, d_, m_, v_))

    names = ["w_ada", "b_ada", "w_in", "b_gate", "w_pool", "pool_scale", "rel_bias", "conv_w", "conv_b", "conv_ln_g",
             "conv_ln_b", "w_br_pool", "w_br_attn", "w_br_conv", "w_o", "ln_mix_g", "ln_mix_b", "w_ff1", "b_ff1",
             "w_ff2", "b_ff2", "ln_ff_g", "ln_ff_b"]

    def pick(k, i):
        o = out[k]
        return o[i] if isinstance(o, tuple) else o["gdmv"[i]].reshape(env[k].shape)

    return (loss, grad_x, *[pick(k, 0) for k in names], *[pick(k, 1) for k in names],
            *[pick(k, 2) for k in names], *[pick(k, 3) for k in names])
```

```python
import functools

import jax
import jax.numpy as jnp
import numpy as np
from jax import lax
from jax.experimental import pallas as pl
from jax.experimental.pallas import tpu as pltpu

F32 = jnp.float32
BF16 = jnp.bfloat16

D_MODEL = 1024
DEPTH = 2
CHUNK = 64
POOL_WINDOWS = (2, 4, 8, 16)
POOL_GROUP = 64
D_POOL = 256
N_HEADS = 8
HEAD_DIM = 64
D_ATTN = 512
N_PREV_CHUNKS = 8
REL_CLIP = 128
N_REL = 2 * REL_CLIP + 1
D_CONV = 256
CONV_WIDTH = 31
D_FF = 4 * D_MODEL
D_IN = 5376
OFF_POOL, OFF_QKV, OFF_CONV, OFF_GATE = 0, 256, 1792, 2304
ALPHA = (2.0 * DEPTH) ** 0.25
LN_EPS = 1e-5
NEG_INF = -1e30
ADAM_LR, ADAM_B1, ADAM_B2, ADAM_EPS, ADAM_WD, ADAM_STEP = 0.001, 0.9, 0.999, 1e-08, 0.01, 10

N_DEV = 8
N_CHIP = 4
MESH = pl.DeviceIdType.MESH

QB = 2 * CHUNK
KPAD = N_PREV_CHUNKS * CHUNK
KW = QB + KPAD
SKEW_W = 768

VMEM_LIMIT = 56 * 1024 * 1024


def _cparams(**kw):
    return pltpu.CompilerParams(vmem_limit_bytes=VMEM_LIMIT, **kw)


def _full(shape):
    n = len(shape)
    return pl.BlockSpec(shape, lambda *_: (0,) * n)


_DIMS = {"nn": (((1,), (0,)), ((), ())), "nt": (((1,), (1,)), ((), ())), "tn": (((0,), (0,)), ((), ()))}


def _relu2(t):
    r = jnp.maximum(t, 0.0)
    return r * r


def _mm(a, b, mode, *, tm, tn, out_dtype, name, b_col0=0, n_out=None, bias=None, split_n=0, rider=None):
    if mode == "tn":
        k, m = a.shape
        n = b.shape[1] if n_out is None else n_out
        a_spec = pl.BlockSpec((k, tm), lambda i, j: (0, i))
        b_spec = pl.BlockSpec((k, tn), lambda i, j: (0, j + b_col0))
    elif mode == "nn":
        m, k = a.shape
        n = b.shape[1] if n_out is None else n_out
        a_spec = pl.BlockSpec((tm, k), lambda i, j: (i, 0))
        b_spec = pl.BlockSpec((k, tn), lambda i, j: (0, j + b_col0))
    else:
        m, k = a.shape
        n = b.shape[0] if n_out is None else n_out
        a_spec = pl.BlockSpec((tm, k), lambda i, j: (i, 0))
        b_spec = pl.BlockSpec((tn, k), lambda i, j: (j + b_col0, 0))
    assert m % tm == 0 and n % tn == 0, (name, m, n, tm, tn)
    dims = _DIMS[mode]

    def body(*refs):
        if bias is None:
            a_ref, b_ref, o_ref = refs
        else:
            a_ref, b_ref, bias_ref, o_ref = refs
        acc = lax.dot_general(a_ref[...].astype(BF16), b_ref[...].astype(BF16), dims, preferred_element_type=F32)
        if bias is not None:
            acc = acc + bias_ref[...]
        if split_n:
            for c in range(tn // split_n):
                o_ref[c] = acc[:, c * split_n:(c + 1) * split_n].astype(out_dtype)
        else:
            o_ref[...] = acc.astype(out_dtype)

    in_specs = [a_spec, b_spec]
    args = [a, b]
    if bias is not None:
        in_specs.append(pl.BlockSpec((1, tn), lambda i, j: (0, j)))
        args.append(bias)
    if split_n:
        out_spec = pl.BlockSpec((tn // split_n, tm, split_n), lambda i, j: (j, i, 0))
        out_shape = jax.ShapeDtypeStruct((n // split_n, m, split_n), out_dtype)
    else:
        out_spec = pl.BlockSpec((tm, tn), lambda i, j: (i, j))
        out_shape = jax.ShapeDtypeStruct((m, n), out_dtype)
    res = _call(body, name=name, grid=(m // tm, n // tn), in_specs=in_specs, out_specs=[out_spec],
                out_shape=[out_shape], scratch_shapes=[], args=args, rider=rider)
    return res[0] if rider is None else (res[0][0], res[1])


def _ln_hat(x):
    mu = jnp.mean(x, axis=-1, keepdims=True)
    xc = x - mu
    var = jnp.mean(xc * xc, axis=-1, keepdims=True)
    rstd = lax.rsqrt(var + LN_EPS)
    return xc * rstd, rstd


def _ln_hat_bwd(dhat, xhat, rstd):
    m1 = jnp.mean(dhat, axis=-1, keepdims=True)
    m2 = jnp.mean(dhat * xhat, axis=-1, keepdims=True)
    return rstd * (dhat - m1 - xhat * m2)


def _row_tile(s):
    return min(512, s)


def _acc_rows(ref, val, first):
    @pl.when(first)
    def _():
        ref[...] = jnp.zeros_like(ref)
    ref[...] += jnp.sum(val, axis=0, keepdims=True)


def _ln_mod(x, sc, sh, name):
    s, d = x.shape
    tm = _row_tile(s)

    def body(x_ref, sc_ref, sh_ref, u_ref):
        xhat, _ = _ln_hat(x_ref[...])
        u_ref[...] = (xhat * (1.0 + sc_ref[...]) + sh_ref[...]).astype(BF16)

    row = pl.BlockSpec((tm, d), lambda i: (i, 0))
    vec = pl.BlockSpec((1, d), lambda i: (0, 0))
    return pl.pallas_call(body, grid=(s // tm,), in_specs=[row, vec, vec], out_specs=row,
                          out_shape=jax.ShapeDtypeStruct((s, d), BF16), name=name, compiler_params=_cparams())(x, sc, sh)


def _ln_mod_bwd(du, x, sc, dres, name):
    s, d = x.shape
    tm = _row_tile(s)

    def body(du_ref, x_ref, sc_ref, dres_ref, dx_ref, dsc_ref, dsh_ref):
        first = pl.program_id(0) == 0
        duv = du_ref[...]
        xhat, rstd = _ln_hat(x_ref[...])
        dx_ref[...] = dres_ref[...] + _ln_hat_bwd(duv * (1.0 + sc_ref[...]), xhat, rstd)
        _acc_rows(dsc_ref, duv * xhat, first)
        _acc_rows(dsh_ref, duv, first)

    row = pl.BlockSpec((tm, d), lambda i: (i, 0))
    vec = pl.BlockSpec((1, d), lambda i: (0, 0))
    vs = jax.ShapeDtypeStruct((1, d), F32)
    return pl.pallas_call(body, grid=(s // tm,), in_specs=[row, row, vec, row], out_specs=[row, vec, vec],
                          out_shape=[jax.ShapeDtypeStruct((s, d), F32), vs, vs], name=name,
                          compiler_params=_cparams())(du, x, sc, dres)


def _resid_ln(x, f, g, gam, bet, name):
    s, d = x.shape
    tm = _row_tile(s)

    def body(x_ref, f_ref, g_ref, gam_ref, bet_ref, o_ref):
        rhat, _ = _ln_hat(ALPHA * x_ref[...] + g_ref[...] * f_ref[...])
        o_ref[...] = rhat * gam_ref[...] + bet_ref[...]

    row = pl.BlockSpec((tm, d), lambda i: (i, 0))
    vec = pl.BlockSpec((1, d), lambda i: (0, 0))
    return pl.pallas_call(body, grid=(s // tm,), in_specs=[row, row, vec, vec, vec], out_specs=row,
                          out_shape=jax.ShapeDtypeStruct((s, d), F32), name=name, compiler_params=_cparams())(x, f, g, gam, bet)


def _resid_ln_bwd(dxo, x, f, g, gam, name):
    s, d = x.shape
    tm = _row_tile(s)

    def body(dxo_ref, x_ref, f_ref, g_ref, gam_ref, dres_ref, df_ref, dgam_ref, dbet_ref, dg_ref, dbias_ref):
        first = pl.program_id(0) == 0
        dxov = dxo_ref[...]
        fv = f_ref[...]
        rhat, rstd = _ln_hat(ALPHA * x_ref[...] + g_ref[...] * fv)
        dr = _ln_hat_bwd(dxov * gam_ref[...], rhat, rstd)
        dfv = g_ref[...] * dr
        dres_ref[...] = ALPHA * dr
        df_ref[...] = dfv.astype(BF16)
        _acc_rows(dgam_ref, dxov * rhat, first)
        _acc_rows(dbet_ref, dxov, first)
        _acc_rows(dg_ref, dr * fv, first)
        _acc_rows(dbias_ref, dfv, first)

    row = pl.BlockSpec((tm, d), lambda i: (i, 0))
    vec = pl.BlockSpec((1, d), lambda i: (0, 0))
    vs = jax.ShapeDtypeStruct((1, d), F32)
    return pl.pallas_call(body, grid=(s // tm,), in_specs=[row, row, row, vec, vec],
                          out_specs=[row, row, vec, vec, vec, vec],
                          out_shape=[jax.ShapeDtypeStruct((s, d), F32), jax.ShapeDtypeStruct((s, d), BF16), vs, vs, vs, vs],
                          name=name, compiler_params=_cparams())(dxo, x, f, g, gam)


def _loss_grad(y, tgt, name):
    s, d = y.shape
    tm = _row_tile(s)
    n = s // tm

    def body(y_ref, t_ref, dy_ref, loss_ref, acc_ref):
        i = pl.program_id(0)
        e = y_ref[...] - t_ref[...]
        dy_ref[...] = e * (1.0 / d)
        _acc_rows(acc_ref, e * e, i == 0)

        @pl.when(i == n - 1)
        def _():
            tot = jnp.sum(acc_ref[...], axis=1, keepdims=True) * (0.5 / d)
            loss_ref[...] = jnp.broadcast_to(tot, (1, 128))

    row = pl.BlockSpec((tm, d), lambda i: (i, 0))
    return pl.pallas_call(body, grid=(n,), in_specs=[row, row],
                          out_specs=[row, pl.BlockSpec((1, 128), lambda i: (0, 0))],
                          out_shape=[jax.ShapeDtypeStruct((s, d), F32), jax.ShapeDtypeStruct((1, 128), F32)],
                          scratch_shapes=[pltpu.VMEM((1, d), F32)], name=name, compiler_params=_cparams())(y, tgt)


POOL_HALO = 16
POOL_ROWS = 256


def _pool_counts(r0, rows):
    t1 = (lax.broadcasted_iota(jnp.int32, (rows, 128), 0) + r0 + 1).astype(F32)
    low = lax.broadcasted_iota(jnp.int32, (rows, 128), 1) < POOL_GROUP
    wa = jnp.where(low, float(POOL_WINDOWS[0]), float(POOL_WINDOWS[1]))
    wb = jnp.where(low, float(POOL_WINDOWS[2]), float(POOL_WINDOWS[3]))
    return jnp.minimum(t1, wa), jnp.minimum(t1, wb), low


def _window_sums(win, off, rows, sign):
    def sl(j, half):
        return win[off + sign * j: off + sign * j + rows, 128 * half:128 * half + 128]
    a2 = sl(0, 0) + sl(1, 0)
    a4 = a2 + sl(2, 0) + sl(3, 0)
    a8 = sl(0, 1)
    for j in range(1, 8):
        a8 = a8 + sl(j, 1)
    a16 = a8
    for j in range(8, 16):
        a16 = a16 + sl(j, 1)
    return a2, a4, a8, a16


def _pool_fwd(zp, wp_bd, pscale, name):
    s = zp.shape[0]
    r = min(POOL_ROWS, s)

    def body(z_ref, wp_ref, sc_ref, p_ref, feat_ref, pad):
        pad[0:POOL_HALO, :] = jnp.zeros((POOL_HALO, D_POOL), F32)
        pad[POOL_HALO:, :] = z_ref[...]

        def step(i, carry):
            r0 = pl.multiple_of(i * r, r)
            win = pad[pl.ds(r0, r + POOL_HALO), :]
            a2, a4, a8, a16 = _window_sums(win, POOL_HALO, r, -1)
            ca, cb, low = _pool_counts(r0, r)
            x0 = win[POOL_HALO:, :]
            pa = jnp.where(low, a2, a4) / ca
            pb = jnp.where(low, a8, a16) / cb
            p = (jnp.concatenate([pa, pb], axis=1) - x0).astype(BF16)
            p_ref[pl.ds(r0, r), :] = p
            pw = jnp.dot(p, wp_ref[...], preferred_element_type=F32)
            feat_ref[pl.ds(r0, r), :] = (pw * sc_ref[...]).astype(BF16)
            return carry

        lax.fori_loop(0, s // r, step, 0)

    return pl.pallas_call(
        body, out_shape=[jax.ShapeDtypeStruct((s, D_POOL), BF16), jax.ShapeDtypeStruct((s, D_POOL), BF16)],
        scratch_shapes=[pltpu.VMEM((s + POOL_HALO, D_POOL), F32)], name=name, compiler_params=_cparams(),
    )(zp, wp_bd, pscale)


def _pool_bwd(dfeat, p, wp_bd, pscale, name):
    s = p.shape[0]
    r = min(POOL_ROWS, s)

    def body(df_ref, p_ref, wp_ref, sc_ref, dz_ref, dwp_ref, dsc_ref, gpad, dpbuf):
        dwp_ref[...] = jnp.zeros_like(dwp_ref)
        dsc_ref[...] = jnp.zeros_like(dsc_ref)
        gpad[s:, :] = jnp.zeros((POOL_HALO, D_POOL), F32)

        def step1(i, carry):
            r0 = pl.multiple_of(i * r, r)
            pv = p_ref[pl.ds(r0, r), :]
            dfv = df_ref[pl.ds(r0, r), :]
            pw = jnp.dot(pv, wp_ref[...], preferred_element_type=F32)
            dsc_ref[...] += jnp.sum(dfv * pw, axis=0, keepdims=True)
            dpw = (dfv * sc_ref[...]).astype(BF16)
            dwp_ref[...] += lax.dot_general(pv, dpw, _DIMS["tn"], preferred_element_type=F32)
            dp = lax.dot_general(dpw, wp_ref[...], _DIMS["nt"], preferred_element_type=F32)
            ca, cb, _ = _pool_counts(r0, r)
            gpad[pl.ds(r0, r), :] = dp / jnp.concatenate([ca, cb], axis=1)
            dpbuf[pl.ds(r0, r), :] = dp
            return carry

        lax.fori_loop(0, s // r, step1, 0)

        def step2(i, carry):
            r0 = pl.multiple_of(i * r, r)
            win = gpad[pl.ds(r0, r + POOL_HALO), :]
            a2, a4, a8, a16 = _window_sums(win, 0, r, 1)
            low = lax.broadcasted_iota(jnp.int32, (r, 128), 1) < POOL_GROUP
            acc = jnp.concatenate([jnp.where(low, a2, a4), jnp.where(low, a8, a16)], axis=1)
            dz_ref[pl.ds(r0, r), :] = (acc - dpbuf[pl.ds(r0, r), :]).astype(BF16)
            return carry

        lax.fori_loop(0, s // r, step2, 0)

    return pl.pallas_call(
        body,
        out_shape=[jax.ShapeDtypeStruct((s, D_POOL), BF16), jax.ShapeDtypeStruct((D_POOL, D_POOL), F32),
                   jax.ShapeDtypeStruct((1, D_POOL), F32)],
        scratch_shapes=[pltpu.VMEM((s + POOL_HALO, D_POOL), F32), pltpu.VMEM((s, D_POOL), F32)],
        name=name, compiler_params=_cparams(),
    )(dfeat, p, wp_bd, pscale)


def _skew_index():
    cp = lax.broadcasted_iota(jnp.int32, (SKEW_W, N_REL), 0)
    dist = jnp.where(cp < KW, KPAD - cp, KPAD + SKEW_W - cp)
    idx = jnp.clip(dist, -REL_CLIP, REL_CLIP) + REL_CLIP
    return (idx == lax.broadcasted_iota(jnp.int32, (SKEW_W, N_REL), 1)).astype(F32)


def _row_bits(b):
    return (lax.broadcasted_iota(jnp.int32, (QB, SKEW_W), 0) >> b) & 1 == 1


N_EDGE = KPAD // QB


def _bias_block(rel_bias, name):
    def body(rb_ref, o_ref):
        onehot = _skew_index()
        row0 = lax.dot_general(rb_ref[...], onehot, _DIMS["nt"], precision=lax.Precision.HIGHEST,
                               preferred_element_type=F32)
        r = lax.broadcasted_iota(jnp.int32, (QB, KW), 0)
        kk = lax.broadcasted_iota(jnp.int32, (QB, KW), 1)
        cq, ck = r // CHUNK, kk // CHUNK
        band = (ck >= cq) & (ck <= cq + N_PREV_CHUNKS)
        for h in range(N_HEADS):
            t = jnp.broadcast_to(row0[h:h + 1, :], (QB, SKEW_W))
            for b in range(7):
                t = jnp.where(_row_bits(b), pltpu.roll(t, 1 << b, 1), t)
            for e in range(N_EDGE + 1):
                o_ref[e, h] = jnp.where(band & (kk >= KPAD - e * QB), t[:, :KW], NEG_INF)

    return pl.pallas_call(body, out_shape=jax.ShapeDtypeStruct((N_EDGE + 1, N_HEADS, QB, KW), F32), name=name,
                          compiler_params=_cparams())(rel_bias)


def _bias_spec():
    return pl.BlockSpec((None, N_HEADS, QB, KW), lambda i: (jnp.minimum(i, N_EDGE), 0, 0, 0))


def _bias_block_bwd(ds_acc, name):
    def body(ds_ref, o_ref):
        sums = []
        for h in range(N_HEADS):
            t = jnp.concatenate([ds_ref[h], jnp.zeros((QB, SKEW_W - KW), F32)], axis=1)
            for b in range(7):
                t = jnp.where(_row_bits(b), pltpu.roll(t, SKEW_W - (1 << b), 1), t)
            sums.append(jnp.sum(t, axis=0, keepdims=True))
        allh = jnp.concatenate(sums, axis=0)
        o_ref[...] = jnp.dot(allh, _skew_index(), precision=lax.Precision.HIGHEST, preferred_element_type=F32)

    return pl.pallas_call(body, out_shape=jax.ShapeDtypeStruct((N_HEADS, N_REL), F32), name=name,
                          compiler_params=_cparams())(ds_acc)


def _scaled(q):
    return (q.astype(F32) * (HEAD_DIM ** -0.5)).astype(BF16)


def _probs(q, kw, bias_ref):
    sc = jnp.stack([lax.dot_general(q[:, HEAD_DIM * h:HEAD_DIM * (h + 1)], kw[:, HEAD_DIM * h:HEAD_DIM * (h + 1)],
                                    _DIMS["nt"], preferred_element_type=F32) + bias_ref[h] for h in range(N_HEADS)])
    e = jnp.exp(sc - jnp.max(sc, axis=-1, keepdims=True))
    return e * (1.0 / jnp.sum(e, axis=-1, keepdims=True))


def _load_padded_kv(qkv_hbm, kpad, vpad, sems, s):
    kpad[0:KPAD, :] = jnp.zeros((KPAD, D_ATTN), BF16)
    vpad[0:KPAD, :] = jnp.zeros((KPAD, D_ATTN), BF16)
    ck = pltpu.make_async_copy(qkv_hbm.at[:, D_ATTN:2 * D_ATTN], kpad.at[pl.ds(KPAD, s), :], sems.at[0])
    cv = pltpu.make_async_copy(qkv_hbm.at[:, 2 * D_ATTN:3 * D_ATTN], vpad.at[pl.ds(KPAD, s), :], sems.at[1])
    ck.start()
    cv.start()
    ck.wait()
    cv.wait()


def _attn_fwd(qkv, bias, name, rider=None):
    s = qkv.shape[0]

    def body(q_ref, qkv_hbm, bias_ref, o_ref, kpad, vpad, sems):
        i = pl.program_id(0)

        @pl.when(i == 0)
        def _():
            _load_padded_kv(qkv_hbm, kpad, vpad, sems, s)

        base = pl.multiple_of(i * QB, QB)
        kw = kpad[pl.ds(base, KW), :]
        vw = vpad[pl.ds(base, KW), :]
        q = _scaled(q_ref[...])
        p = _probs(q, kw, bias_ref).astype(BF16)
        outs = [jnp.dot(p[h], vw[:, HEAD_DIM * h:HEAD_DIM * (h + 1)], preferred_element_type=F32)
                for h in range(N_HEADS)]
        o_ref[...] = jnp.concatenate(outs, axis=1).astype(BF16)

    res = _call(
        body, name=name, grid=(s // QB,),
        in_specs=[pl.BlockSpec((QB, D_ATTN), lambda i: (i, 0)), pl.BlockSpec(memory_space=pl.ANY),
                  _bias_spec()],
        out_specs=[pl.BlockSpec((QB, D_ATTN), lambda i: (i, 0))],
        out_shape=[jax.ShapeDtypeStruct((s, D_ATTN), BF16)],
        scratch_shapes=[pltpu.VMEM((s + KPAD, D_ATTN), BF16), pltpu.VMEM((s + KPAD, D_ATTN), BF16),
                        pltpu.SemaphoreType.DMA((2,))],
        args=(qkv, qkv, bias), rider=rider)
    return res[0] if rider is None else (res[0][0], res[1])


def _attn_bwd(qkv, do, bias, name, rider=None):
    s = qkv.shape[0]
    n = s // QB

    def body(q_ref, qkv_hbm, do_ref, bias_ref, dq_ref, dk_hbm, dv_hbm, ds_ref, kpad, vpad, dkacc, dvacc, sems):
        i = pl.program_id(0)

        @pl.when(i == 0)
        def _():
            _load_padded_kv(qkv_hbm, kpad, vpad, sems, s)
            dkacc[...] = jnp.zeros_like(dkacc)
            dvacc[...] = jnp.zeros_like(dvacc)
            ds_ref[...] = jnp.zeros_like(ds_ref)

        base = pl.multiple_of(i * QB, QB)
        kw = kpad[pl.ds(base, KW), :]
        vw = vpad[pl.ds(base, KW), :]
        q = _scaled(q_ref[...])
        dov = do_ref[...]
        heads = [slice(HEAD_DIM * h, HEAD_DIM * (h + 1)) for h in range(N_HEADS)]
        p = _probs(q, kw, bias_ref)
        dp = jnp.stack([lax.dot_general(dov[:, hs], vw[:, hs], _DIMS["nt"], preferred_element_type=F32) for hs in heads])
        ds = p * (dp - jnp.sum(dp * p, axis=-1, keepdims=True))
        ds_ref[...] += ds
        pb, dsb = p.astype(BF16), ds.astype(BF16)
        dvs = [lax.dot_general(pb[h], dov[:, hs], _DIMS["tn"], preferred_element_type=F32) for h, hs in enumerate(heads)]
        dqs = [jnp.dot(dsb[h], kw[:, hs], preferred_element_type=F32) for h, hs in enumerate(heads)]
        dks = [lax.dot_general(dsb[h], q[:, hs], _DIMS["tn"], preferred_element_type=F32) for h, hs in enumerate(heads)]
        dq_ref[...] = (jnp.concatenate(dqs, axis=1) * (HEAD_DIM ** -0.5)).astype(BF16)
        dkacc[pl.ds(base, KW), :] += jnp.concatenate(dks, axis=1)
        dvacc[pl.ds(base, KW), :] += jnp.concatenate(dvs, axis=1)

        @pl.when(i == n - 1)
        def _():
            ck = pltpu.make_async_copy(dkacc, dk_hbm, sems.at[0])
            cv = pltpu.make_async_copy(dvacc, dv_hbm, sems.at[1])
            ck.start()
            cv.start()
            ck.wait()
            cv.wait()

    blk = pl.BlockSpec((QB, D_ATTN), lambda i: (i, 0))
    acc_shape = jax.ShapeDtypeStruct((s + KPAD, D_ATTN), F32)
    return _call(
        body, name=name, grid=(n,),
        in_specs=[blk, pl.BlockSpec(memory_space=pl.ANY), blk, _bias_spec()],
        out_specs=[blk, pl.BlockSpec(memory_space=pl.ANY), pl.BlockSpec(memory_space=pl.ANY), _full((N_HEADS, QB, KW))],
        out_shape=[jax.ShapeDtypeStruct((s, D_ATTN), BF16), acc_shape, acc_shape,
                   jax.ShapeDtypeStruct((N_HEADS, QB, KW), F32)],
        scratch_shapes=[pltpu.VMEM((s + KPAD, D_ATTN), BF16), pltpu.VMEM((s + KPAD, D_ATTN), BF16),
                        pltpu.VMEM((s + KPAD, D_ATTN), F32), pltpu.VMEM((s + KPAD, D_ATTN), F32),
                        pltpu.SemaphoreType.DMA((2,))],
        args=(qkv, qkv, do, bias), rider=rider)


CONV_HALO = 32
CONV_ROWS = 64


def _sigmoid(t):
    return 1.0 / (1.0 + jnp.exp(-t))


def _glu_rows(z_ref, r0, rows):
    a = z_ref[pl.ds(r0, rows), 0:D_CONV]
    b = z_ref[pl.ds(r0, rows), D_CONV:2 * D_CONV]
    return a, _sigmoid(b)


def _conv_fwd(zc, conv_w, conv_b, ln_g, ln_b, name):
    s = zc.shape[0]
    rt = min(256, s)

    def body(z_ref, w_ref, cb_ref, g_ref, b_ref, cv_ref, feat_ref, hpad):
        hpad[0:CONV_HALO, :] = jnp.zeros((CONV_HALO, D_CONV), F32)

        def glu(i, carry):
            r0 = pl.multiple_of(i * rt, rt)
            a, sb = _glu_rows(z_ref, r0, rt)
            hpad[pl.ds(r0 + CONV_HALO, rt), :] = a * sb
            return carry

        lax.fori_loop(0, s // rt, glu, 0)
        w = w_ref[...]

        def conv(i, carry):
            r0 = pl.multiple_of(i * CONV_ROWS, CONV_ROWS)
            win = hpad[pl.ds(r0, CONV_ROWS + CONV_HALO), :]
            acc = jnp.broadcast_to(cb_ref[...], (CONV_ROWS, D_CONV))
            for k in range(CONV_WIDTH):
                acc = acc + win[2 + k:2 + k + CONV_ROWS, :] * w[k:k + 1, :]
            cv_ref[pl.ds(r0, CONV_ROWS), :] = acc
            yhat, _ = _ln_hat(acc)
            y = yhat * g_ref[...] + b_ref[...]
            feat_ref[pl.ds(r0, CONV_ROWS), :] = (y * _sigmoid(y)).astype(BF16)
            return carry

        lax.fori_loop(0, s // CONV_ROWS, conv, 0)

    return pl.pallas_call(
        body, out_shape=[jax.ShapeDtypeStruct((s, D_CONV), F32), jax.ShapeDtypeStruct((s, D_CONV), BF16)],
        scratch_shapes=[pltpu.VMEM((s + CONV_HALO, D_CONV), F32)], name=name, compiler_params=_cparams(),
    )(zc, conv_w, conv_b, ln_g, ln_b)


def _conv_bwd(dfeat, cv, zc, conv_w, ln_g, ln_b, name):
    s = zc.shape[0]
    rt = min(256, s)

    def body(df_ref, cv_ref, z_ref, w_ref, g_ref, b_ref, dz_ref, dw_ref, dcb_ref, dg_ref, db_ref, hpad, dcvpad, dwacc):
        hpad[0:CONV_HALO, :] = jnp.zeros((CONV_HALO, D_CONV), F32)
        dcvpad[s:, :] = jnp.zeros((CONV_HALO, D_CONV), F32)
        dwacc[...] = jnp.zeros_like(dwacc)
        dcb_ref[...] = jnp.zeros_like(dcb_ref)
        dg_ref[...] = jnp.zeros_like(dg_ref)
        db_ref[...] = jnp.zeros_like(db_ref)

        def pass1(i, carry):
            r0 = pl.multiple_of(i * rt, rt)
            a, sb = _glu_rows(z_ref, r0, rt)
            hpad[pl.ds(r0 + CONV_HALO, rt), :] = a * sb
            cvhat, rstd = _ln_hat(cv_ref[pl.ds(r0, rt), :])
            y = cvhat * g_ref[...] + b_ref[...]
            sg = _sigmoid(y)
            dy = df_ref[pl.ds(r0, rt), :] * (sg * (1.0 + y * (1.0 - sg)))
            dg_ref[...] += jnp.sum(dy * cvhat, axis=0, keepdims=True)
            db_ref[...] += jnp.sum(dy, axis=0, keepdims=True)
            dcv = _ln_hat_bwd(dy * g_ref[...], cvhat, rstd)
            dcb_ref[...] += jnp.sum(dcv, axis=0, keepdims=True)
            dcvpad[pl.ds(r0, rt), :] = dcv
            return carry

        lax.fori_loop(0, s // rt, pass1, 0)
        w = w_ref[...]

        def pass2(i, carry):
            r0 = pl.multiple_of(i * CONV_ROWS, CONV_ROWS)
            dwin = dcvpad[pl.ds(r0, CONV_ROWS + CONV_HALO), :]
            hwin = hpad[pl.ds(r0, CONV_ROWS + CONV_HALO), :]
            dcv = dwin[0:CONV_ROWS, :]
            dh = jnp.zeros((CONV_ROWS, D_CONV), F32)
            for k in range(CONV_WIDTH):
                dh = dh + dwin[30 - k:30 - k + CONV_ROWS, :] * w[k:k + 1, :]
                prod = dcv * hwin[2 + k:2 + k + CONV_ROWS, :]
                dwacc[8 * k:8 * k + 8, :] += jnp.sum(prod.reshape(CONV_ROWS // 8, 8, D_CONV), axis=0)
            a, sb = _glu_rows(z_ref, r0, CONV_ROWS)
            dz_ref[pl.ds(r0, CONV_ROWS), :] = jnp.concatenate([dh * sb, dh * a * sb * (1.0 - sb)], axis=1).astype(BF16)
            return carry

        lax.fori_loop(0, s // CONV_ROWS, pass2, 0)
        dw_ref[...] = jnp.sum(dwacc[...].reshape(32, 8, D_CONV), axis=1)

    vs = jax.ShapeDtypeStruct((1, D_CONV), F32)
    return pl.pallas_call(
        body,
        out_shape=[jax.ShapeDtypeStruct((s, 2 * D_CONV), BF16), jax.ShapeDtypeStruct((32, D_CONV), F32), vs, vs, vs],
        scratch_shapes=[pltpu.VMEM((s + CONV_HALO, D_CONV), F32), pltpu.VMEM((s + CONV_HALO, D_CONV), F32),
                        pltpu.VMEM((256, D_CONV), F32)],
        name=name, compiler_params=_cparams(),
    )(dfeat, cv, zc, conv_w, ln_g, ln_b)


def _merge(zg, b_gate, ys, name):
    s = zg.shape[0]
    tm = _row_tile(s)

    def body(zg_ref, bg_ref, y0_ref, y1_ref, y2_ref, o_ref):
        acc = None
        for j, y_ref in enumerate((y0_ref, y1_ref, y2_ref)):
            cs = slice(D_MODEL * j, D_MODEL * (j + 1))
            t = _sigmoid(zg_ref[:, cs] + bg_ref[:, cs]) * y_ref[...]
            acc = t if acc is None else acc + t
        o_ref[...] = acc.astype(BF16)

    row = pl.BlockSpec((tm, D_MODEL), lambda i: (i, 0))
    return pl.pallas_call(
        body, grid=(s // tm,),
        in_specs=[pl.BlockSpec((tm, 3 * D_MODEL), lambda i: (i, 0)), _full((1, 3 * D_MODEL)), row, row, row],
        out_specs=row, out_shape=jax.ShapeDtypeStruct((s, D_MODEL), BF16), name=name, compiler_params=_cparams(),
    )(zg, b_gate, *ys)


def _merge_bwd(dm, zg, b_gate, ys, name):
    s = zg.shape[0]
    tm = min(256, s)

    def body(dm_ref, zg_ref, bg_ref, y0_ref, y1_ref, y2_ref, d0_ref, d1_ref, d2_ref, dzg_ref, dbg_ref):
        first = pl.program_id(0) == 0

        @pl.when(first)
        def _():
            dbg_ref[...] = jnp.zeros_like(dbg_ref)

        dmv = dm_ref[...]
        for j, (y_ref, d_ref) in enumerate(((y0_ref, d0_ref), (y1_ref, d1_ref), (y2_ref, d2_ref))):
            cs = slice(D_MODEL * j, D_MODEL * (j + 1))
            g = _sigmoid(zg_ref[:, cs] + bg_ref[:, cs])
            d_ref[...] = (dmv * g).astype(BF16)
            dzg = dmv * y_ref[...] * g * (1.0 - g)
            dzg_ref[:, cs] = dzg.astype(BF16)
            dbg_ref[:, cs] += jnp.sum(dzg, axis=0, keepdims=True)

    row = pl.BlockSpec((tm, D_MODEL), lambda i: (i, 0))
    wide = pl.BlockSpec((tm, 3 * D_MODEL), lambda i: (i, 0))
    yb = jax.ShapeDtypeStruct((s, D_MODEL), BF16)
    return pl.pallas_call(
        body, grid=(s // tm,),
        in_specs=[row, wide, _full((1, 3 * D_MODEL)), row, row, row],
        out_specs=[row, row, row, wide, _full((1, 3 * D_MODEL))],
        out_shape=[yb, yb, yb, jax.ShapeDtypeStruct((s, 3 * D_MODEL), BF16), jax.ShapeDtypeStruct((1, 3 * D_MODEL), F32)],
        name=name, compiler_params=_cparams(),
    )(dm, zg, b_gate, *ys)


def _ff_hidden(u2, w_ff1t, b_ff1, name, rider=None):
    s = u2.shape[0]
    tm, tn = min(1024, s), 1024

    def body(a_ref, b_ref, bias_ref, pre_ref, h_ref):
        acc = lax.dot_general(a_ref[...], b_ref[...], _DIMS["nt"], preferred_element_type=F32) + bias_ref[...]
        pre_ref[...] = acc.astype(BF16)
        h_ref[...] = _relu2(acc).astype(BF16)

    blk = pl.BlockSpec((tm, tn), lambda i, j: (i, j))
    sh = jax.ShapeDtypeStruct((s, D_FF), BF16)
    res = _call(body, name=name, grid=(s // tm, D_FF // tn),
                in_specs=[pl.BlockSpec((tm, D_MODEL), lambda i, j: (i, 0)), pl.BlockSpec((tn, D_MODEL), lambda i, j: (j, 0)),
                          pl.BlockSpec((1, tn), lambda i, j: (0, j))],
                out_specs=[blk, blk], out_shape=[sh, sh], scratch_shapes=[], args=(u2, w_ff1t, b_ff1), rider=rider)
    return tuple(res) if rider is None else (tuple(res[0]), res[1])


def _ff_hidden_bwd(dff, w_ff2, hpre, name):
    s = dff.shape[0]
    tm, tn = min(512, s), 1024

    def body(a_ref, b_ref, h_ref, o_ref, sum_ref):
        dh = lax.dot_general(a_ref[...], b_ref[...], _DIMS["nt"], preferred_element_type=F32)
        dpre = dh * (2.0 * jnp.maximum(h_ref[...].astype(F32), 0.0))
        o_ref[...] = dpre.astype(BF16)
        _acc_rows(sum_ref, dpre, pl.program_id(1) == 0)

    return pl.pallas_call(
        body, grid=(D_FF // tn, s // tm),
        in_specs=[pl.BlockSpec((tm, D_MODEL), lambda j, i: (i, 0)), pl.BlockSpec((tn, D_MODEL), lambda j, i: (j, 0)),
                  pl.BlockSpec((tm, tn), lambda j, i: (i, j))],
        out_specs=[pl.BlockSpec((tm, tn), lambda j, i: (i, j)), pl.BlockSpec((1, tn), lambda j, i: (0, j))],
        out_shape=[jax.ShapeDtypeStruct((s, D_FF), BF16), jax.ShapeDtypeStruct((1, D_FF), F32)],
        name=name, compiler_params=_cparams(),
    )(dff, w_ff2, hpre)


def _silu(t):
    return t * _sigmoid(t)


def _mod_fwd(c_all, w_ada_sh, b_ada_sh, name):
    cols = w_ada_sh.shape[2]

    def body(c_ref, w_ref, b_ref, o_ref):
        ca = _silu(c_ref[...]).astype(BF16)
        o_ref[0] = jnp.dot(ca, w_ref[0].astype(BF16), preferred_element_type=F32) + b_ref[0]

    return pl.pallas_call(
        body, grid=(DEPTH,),
        in_specs=[_full((N_DEV, D_MODEL)), pl.BlockSpec((1, D_MODEL, cols), lambda l: (l, 0, 0)),
                  pl.BlockSpec((1, 1, cols), lambda l: (l, 0, 0))],
        out_specs=pl.BlockSpec((1, N_DEV, cols), lambda l: (l, 0, 0)),
        out_shape=jax.ShapeDtypeStruct((DEPTH, N_DEV, cols), F32), name=name, compiler_params=_cparams(),
    )(c_all, w_ada_sh, b_ada_sh)


def _mod_bwd(c_all, dmod_sh, name):
    cols = dmod_sh.shape[2]

    def body(c_ref, d_ref, o_ref):
        ca = _silu(c_ref[...])
        o_ref[0] = lax.dot_general(ca, d_ref[0], _DIMS["tn"], precision=lax.Precision.HIGHEST,
                                   preferred_element_type=F32)

    return pl.pallas_call(
        body, grid=(DEPTH,),
        in_specs=[_full((N_DEV, D_MODEL)), pl.BlockSpec((1, N_DEV, cols), lambda l: (l, 0, 0))],
        out_specs=pl.BlockSpec((1, D_MODEL, cols), lambda l: (l, 0, 0)),
        out_shape=jax.ShapeDtypeStruct((DEPTH, D_MODEL, cols), F32), name=name, compiler_params=_cparams(),
    )(c_all, dmod_sh)


def _flat_tiles(rows, cols, itemsize_total):
    budget = 12 * 1024 * 1024
    tr = rows
    while tr % 32 == 0 and tr * cols * itemsize_total > budget:
        tr //= 2
    return tr


def _sum_cores(dw, recv, place, name):
    _, m, n = dw.shape
    tr = _flat_tiles(m, n, 6)

    def body(place_ref, a_ref, b_ref, o_ref):
        o_ref[...] = (a_ref[...].astype(F32) + b_ref[...].astype(F32)).astype(BF16)

    grid_spec = pltpu.PrefetchScalarGridSpec(
        num_scalar_prefetch=1, grid=(m // tr,),
        in_specs=[pl.BlockSpec((None, tr, n), lambda i, pr: (pr[0], i, 0)), pl.BlockSpec((tr, n), lambda i, pr: (i, 0))],
        out_specs=pl.BlockSpec((tr, n), lambda i, pr: (i, 0)))
    return pl.pallas_call(body, grid_spec=grid_spec, out_shape=jax.ShapeDtypeStruct((m, n), BF16), name=name,
                          compiler_params=_cparams())(place, dw, recv)


def _sum_chips(h, r, place, name):
    _, rs, n = h.shape
    tr = _flat_tiles(rs, n, 12)

    def body(place_ref, h_ref, r_ref, o_ref):
        o_ref[...] = ((h_ref[...].astype(F32) + r_ref[0].astype(F32)) + r_ref[1].astype(F32)) + r_ref[2].astype(F32)

    grid_spec = pltpu.PrefetchScalarGridSpec(
        num_scalar_prefetch=1, grid=(rs // tr,),
        in_specs=[pl.BlockSpec((None, tr, n), lambda i, pr: (pr[1], i, 0)), pl.BlockSpec((3, tr, n), lambda i, pr: (0, i, 0))],
        out_specs=pl.BlockSpec((tr, n), lambda i, pr: (i, 0)))
    return pl.pallas_call(body, grid_spec=grid_spec, out_shape=jax.ShapeDtypeStruct((rs, n), F32), name=name,
                          compiler_params=_cparams())(place, h, r)


def _adam_math(w, g, m, v):
    m2 = ADAM_B1 * m + (1.0 - ADAM_B1) * g
    v2 = ADAM_B2 * v + (1.0 - ADAM_B2) * (g * g)
    m_hat = m2 / (1.0 - ADAM_B1 ** ADAM_STEP)
    v_hat = v2 / (1.0 - ADAM_B2 ** ADAM_STEP)
    delta = -ADAM_LR * (m_hat / (jnp.sqrt(v_hat) + ADAM_EPS) + ADAM_WD * w)
    return delta, m2, v2


def _adamw(w, m, v, grads, name):
    r, c = w.shape
    tr = _flat_tiles(r, c, 4 * (7 + len(grads)))

    def body(*refs):
        w_ref, m_ref, v_ref = refs[:3]
        g_refs = refs[3:3 + len(grads)]
        g_ref, d_ref, m2_ref, v2_ref = refs[3 + len(grads):]
        g = g_refs[0][...]
        for gr in g_refs[1:]:
            g = g + gr[...]
        delta, m2, v2 = _adam_math(w_ref[...], g, m_ref[...], v_ref[...])
        g_ref[...] = g
        d_ref[...] = delta
        m2_ref[...] = m2
        v2_ref[...] = v2

    blk = pl.BlockSpec((tr, c), lambda i: (i, 0))
    sh = jax.ShapeDtypeStruct((r, c), F32)
    return pl.pallas_call(body, grid=(r // tr,), in_specs=[blk] * (3 + len(grads)), out_specs=[blk] * 4,
                          out_shape=[sh] * 4, name=name, compiler_params=_cparams())(w, m, v, *grads)


def _adamw_halves(w, m, v, own, other, place, split, name, rider=None):
    nl, r, c = w.shape
    hr, hc = own[0].shape
    tr = _flat_tiles(hr, hc, 4 * (7 + 2 * nl))
    nt = hr // tr
    if split == "rows":
        w_spec = pl.BlockSpec((None, tr, c), lambda l, h, t, pr: (l, h * nt + t, 0))
    else:
        w_spec = pl.BlockSpec((None, tr, hc), lambda l, h, t, pr: (l, t, h))

    def g_spec(layer):
        return pl.BlockSpec((tr, hc), lambda l, h, t, pr: (jnp.where(l == layer, t, nt - 1), 0))

    def body(place_ref, w_ref, m_ref, v_ref, *refs):
        own_refs, other_refs = refs[:nl], refs[nl:2 * nl]
        g_ref, d_ref, m2_ref, v2_ref = refs[2 * nl:]
        layer = pl.program_id(0)
        mine = pl.program_id(1) == place_ref[0]
        g = None
        for li in range(nl):
            cand = jnp.where(mine, own_refs[li][...], other_refs[li][...])
            g = cand if g is None else jnp.where(layer == li, cand, g)
        delta, m2, v2 = _adam_math(w_ref[...], g, m_ref[...], v_ref[...])
        g_ref[...] = g
        d_ref[...] = delta
        m2_ref[...] = m2
        v2_ref[...] = v2

    sh = jax.ShapeDtypeStruct((nl, r, c), F32)
    return _call(body, name=name, grid=(nl, 2, nt), in_specs=[w_spec] * 3 + [g_spec(li) for li in range(nl)] * 2,
                 out_specs=[w_spec] * 4, out_shape=[sh] * 4, scratch_shapes=[], args=(w, m, v, *own, *other),
                 rider=rider, prefetch=(place,))


def _adamw_small(w, m, v, g_all, name):
    r, c = w.shape

    def body(w_ref, m_ref, v_ref, g_ref, go_ref, d_ref, m2_ref, v2_ref):
        g = g_ref[0]
        for b in range(1, N_DEV):
            g = g + g_ref[b]
        delta, m2, v2 = _adam_math(w_ref[...], g, m_ref[...], v_ref[...])
        go_ref[...] = g
        d_ref[...] = delta
        m2_ref[...] = m2
        v2_ref[...] = v2

    sh = jax.ShapeDtypeStruct((r, c), F32)
    return pl.pallas_call(body, out_shape=[sh] * 4, name=name, compiler_params=_cparams())(w, m, v, g_all)


def _me():
    return lax.axis_index("x"), lax.axis_index("y"), lax.axis_index("c")


def _flip(v, bit):
    return 1 - v if bit else v


def _allgather_small(blk, name):
    r, c = blk.shape

    def body(x_ref, o_ref, send_sems, recv_sems):
        x, y, cc = _me()
        me = 4 * x + 2 * y + cc
        copies = []
        for k in range(1, N_DEV):
            peer = (_flip(x, k & 4), _flip(y, k & 2), _flip(cc, k & 1))
            cp = pltpu.make_async_remote_copy(src_ref=x_ref, dst_ref=o_ref.at[me], send_sem=send_sems.at[k - 1],
                                              recv_sem=recv_sems.at[k - 1], device_id=peer, device_id_type=MESH)
            cp.start()
            copies.append(cp)
        o_ref[me] = x_ref[...]
        for cp in copies:
            cp.wait()

    return pl.pallas_call(
        body, out_shape=jax.ShapeDtypeStruct((N_DEV, r, c), F32),
        in_specs=[pl.BlockSpec(memory_space=pltpu.VMEM)], out_specs=pl.BlockSpec(memory_space=pltpu.VMEM),
        scratch_shapes=[pltpu.SemaphoreType.DMA((N_DEV - 1,)), pltpu.SemaphoreType.DMA((N_DEV - 1,))],
        name=name, compiler_params=_cparams(),
    )(blk)


class _Rider:
    def __init__(self, arrays, out_shapes, scratch_shapes, start, finish):
        self.arrays, self.out_shapes, self.scratch_shapes = list(arrays), list(out_shapes), list(scratch_shapes)
        self.start, self.finish = start, finish


def _call(body, *, name, grid, in_specs, out_specs, out_shape, scratch_shapes, args, rider=None, prefetch=()):
    npf = len(prefetch)

    def launch(fn, in_specs, out_specs, out_shape, scratch_shapes, args):
        grid_spec = pltpu.PrefetchScalarGridSpec(num_scalar_prefetch=npf, grid=grid, in_specs=in_specs,
                                                 out_specs=out_specs, scratch_shapes=scratch_shapes)
        return pl.pallas_call(fn, grid_spec=grid_spec, out_shape=out_shape, name=name,
                              compiler_params=_cparams())(*prefetch, *args)

    if rider is None:
        return launch(body, list(in_specs), list(out_specs), list(out_shape), list(scratch_shapes), args)
    ni, no, ns = len(in_specs), len(out_specs), len(scratch_shapes)
    ri, ro = len(rider.arrays), len(rider.out_shapes)
    steps = int(np.prod(grid))

    def wrapped(*refs):
        pf, refs = refs[:npf], refs[npf:]
        h_in, r_in = refs[:ni], refs[ni:ni + ri]
        h_out, r_out = refs[ni + ri:ni + ri + no], refs[ni + ri + no:ni + ri + no + ro]
        h_scr, r_scr = refs[ni + ri + no + ro:ni + ri + no + ro + ns], refs[ni + ri + no + ro + ns:]
        step = pl.program_id(0)
        for d in range(1, len(grid)):
            step = step * grid[d] + pl.program_id(d)

        @pl.when(step == 0)
        def _():
            rider.start(r_in, r_out, r_scr)

        body(*pf, *h_in, *h_out, *h_scr)

        @pl.when(step == steps - 1)
        def _():
            rider.finish(r_in, r_out, r_scr)

    anyspec = pl.BlockSpec(memory_space=pl.ANY)
    res = launch(wrapped, list(in_specs) + [anyspec] * ri, list(out_specs) + [anyspec] * ro,
                 list(out_shape) + rider.out_shapes, list(scratch_shapes) + rider.scratch_shapes,
                 list(args) + rider.arrays)
    return res[:no], res[no:]


def _run_rider(rider, name):
    ri = len(rider.arrays)

    def body(*refs):
        r_in, r_out, r_scr = refs[:ri], refs[ri:ri + len(rider.out_shapes)], refs[ri + len(rider.out_shapes):]
        rider.start(r_in, r_out, r_scr)
        rider.finish(r_in, r_out, r_scr)

    anyspec = pl.BlockSpec(memory_space=pl.ANY)
    return pl.pallas_call(body, in_specs=[anyspec] * ri, out_specs=[anyspec] * len(rider.out_shapes),
                          out_shape=rider.out_shapes, scratch_shapes=rider.scratch_shapes, name=name,
                          compiler_params=_cparams())(*rider.arrays)


def _allgather_rider(blk):
    def copies(ins, outs, scr):
        send_sems, recv_sems, loc_sems, stage = scr
        x, y, cc = _me()
        me = 4 * x + 2 * y + cc
        remote = [pltpu.make_async_remote_copy(
            src_ref=ins[0], dst_ref=outs[0].at[me], send_sem=send_sems.at[k - 1], recv_sem=recv_sems.at[k - 1],
            device_id=(_flip(x, k & 4), _flip(y, k & 2), _flip(cc, k & 1)), device_id_type=MESH) for k in range(1, N_DEV)]
        return remote, pltpu.make_async_copy(ins[0], stage, loc_sems.at[0]), (outs[0].at[me], stage, loc_sems.at[1])

    def start(ins, outs, scr):
        remote, lin, _ = copies(ins, outs, scr)
        lin.start()
        for cp in remote:
            cp.start()

    def finish(ins, outs, scr):
        remote, lin, (dst, stage, sem) = copies(ins, outs, scr)
        lin.wait()
        lout = pltpu.make_async_copy(stage, dst, sem)
        lout.start()
        for cp in remote:
            cp.wait()
        lout.wait()

    return _Rider([blk], [jax.ShapeDtypeStruct((N_DEV,) + blk.shape, blk.dtype)],
                  [pltpu.SemaphoreType.DMA((N_DEV - 1,)), pltpu.SemaphoreType.DMA((N_DEV - 1,)),
                   pltpu.SemaphoreType.DMA((2,)), pltpu.VMEM(blk.shape, blk.dtype)], start, finish)


def _gather_rider(shards):
    n = len(shards)

    def copies(ins, outs, scr, relay=True):
        ici_send, ici_recv, d2d_send, d2d_recv, loc_sems = scr[:5]
        stage = scr[5:]
        x, y, cc = _me()
        chip = 2 * x + y
        sibling = (x, y, 1 - cc)
        local, sends, relays = [], [], []
        for j in range(n):
            def rows(ch, h, j=j):
                return outs[j].at[ch, h]

            lc = pltpu.make_async_copy(ins[j], stage[j], loc_sems.at[j])
            local.append((lc, pltpu.make_async_copy(stage[j], outs[j].at[chip], loc_sems.at[n + j]) if relay else None))
            for k in range(1, N_CHIP):
                px, py = _flip(x, k & 2), _flip(y, k & 1)
                pchip = 2 * px + py
                q = 3 * j + k - 1
                out_cp = pltpu.make_async_remote_copy(src_ref=ins[j].at[cc], dst_ref=rows(chip, cc),
                                                      send_sem=ici_send.at[q], recv_sem=ici_recv.at[q],
                                                      device_id=(px, py, cc), device_id_type=MESH)
                sends.append(out_cp)
                if not relay:
                    continue
                arrival = pltpu.make_async_remote_copy(src_ref=rows(pchip, cc), dst_ref=rows(pchip, cc),
                                                       send_sem=ici_send.at[q], recv_sem=ici_recv.at[q],
                                                       device_id=(px, py, cc), device_id_type=MESH)
                forward = pltpu.make_async_remote_copy(src_ref=rows(pchip, cc), dst_ref=rows(pchip, cc),
                                                       send_sem=d2d_send.at[q], recv_sem=d2d_recv.at[q],
                                                       device_id=sibling, device_id_type=MESH)
                from_sibling = pltpu.make_async_remote_copy(src_ref=rows(pchip, 1 - cc), dst_ref=rows(pchip, 1 - cc),
                                                            send_sem=d2d_send.at[q], recv_sem=d2d_recv.at[q],
                                                            device_id=sibling, device_id_type=MESH)
                relays.append((arrival, forward, from_sibling))
        return local, sends, relays

    def start(ins, outs, scr):
        local, sends, _ = copies(ins, outs, scr, relay=False)
        for lin, _ in local:
            lin.start()
        for cp in sends:
            cp.start()

    def finish(ins, outs, scr):
        local, sends, relays = copies(ins, outs, scr)
        for lin, lout in local:
            lin.wait()
            lout.start()
        for arrival, forward, _ in relays:
            arrival.wait_recv()
            forward.start()
        for cp in sends:
            cp.wait_send()
        for _, forward, from_sibling in relays:
            forward.wait_send()
            from_sibling.wait_recv()
        for _, lout in local:
            lout.wait()

    scratch = [pltpu.SemaphoreType.DMA((3 * n,)), pltpu.SemaphoreType.DMA((3 * n,)), pltpu.SemaphoreType.DMA((3 * n,)),
               pltpu.SemaphoreType.DMA((3 * n,)), pltpu.SemaphoreType.DMA((2 * n,))]
    scratch += [pltpu.VMEM(a.shape, a.dtype) for a in shards]
    return _Rider(shards, [jax.ShapeDtypeStruct((N_CHIP,) + a.shape, a.dtype) for a in shards], scratch, start, finish)


def _sibling_send(arrs, name, other_half=False):
    n = len(arrs)

    def body(*refs):
        ins, outs = refs[:n], refs[n:2 * n]
        send_sems, recv_sems = refs[2 * n:]
        x, y, cc = _me()
        pending = []
        for j in range(n):
            src = ins[j].at[1 - cc] if other_half else ins[j]
            cp = pltpu.make_async_remote_copy(src_ref=src, dst_ref=outs[j], send_sem=send_sems.at[j],
                                              recv_sem=recv_sems.at[j], device_id=(x, y, 1 - cc), device_id_type=MESH)
            cp.start()
            pending.append(cp)
        for cp in pending:
            cp.wait()

    anyspec = pl.BlockSpec(memory_space=pl.ANY)
    return pl.pallas_call(
        body, out_shape=[jax.ShapeDtypeStruct(a.shape[1:] if other_half else a.shape, a.dtype) for a in arrs],
        in_specs=[anyspec] * n, out_specs=[anyspec] * n,
        scratch_shapes=[pltpu.SemaphoreType.DMA((n,)), pltpu.SemaphoreType.DMA((n,))],
        name=name, compiler_params=_cparams(),
    )(*arrs)


def _scatter_rider(arrs):
    n = len(arrs)

    def copies(ins, outs, scr):
        send_sems, recv_sems = scr
        x, y, cc = _me()
        cps = []
        for j in range(n):
            for k in range(1, N_CHIP):
                px, py = _flip(x, k & 2), _flip(y, k & 1)
                cps.append(pltpu.make_async_remote_copy(
                    src_ref=ins[j].at[2 * px + py], dst_ref=outs[j].at[k - 1], send_sem=send_sems.at[3 * j + k - 1],
                    recv_sem=recv_sems.at[3 * j + k - 1], device_id=(px, py, cc), device_id_type=MESH))
        return cps

    def start(ins, outs, scr):
        for cp in copies(ins, outs, scr):
            cp.start()

    def finish(ins, outs, scr):
        for cp in copies(ins, outs, scr):
            cp.wait()

    return _Rider(arrs, [jax.ShapeDtypeStruct((N_CHIP - 1,) + a.shape[1:], a.dtype) for a in arrs],
                  [pltpu.SemaphoreType.DMA((3 * n,)), pltpu.SemaphoreType.DMA((3 * n,))], start, finish)


COL_SHARDED = ("w_in", "w_br_pool", "w_br_attn", "w_br_conv", "w_ff1")
ROW_SHARDED = ("w_o", "w_ff2")
BIG = COL_SHARDED + ROW_SHARDED
SMALL = ("b_ada", "b_gate", "w_pool", "pool_scale", "rel_bias", "conv_w", "conv_b", "conv_ln_g", "conv_ln_b",
         "ln_mix_g", "ln_mix_b", "b_ff1", "b_ff2", "ln_ff_g", "ln_ff_b")
PACK_W = 1024


def _pack(parts):
    rows = []
    for a in parts:
        flat = a.reshape(-1)
        n = -(-flat.shape[0] // PACK_W) * PACK_W
        rows.append(jnp.pad(flat, (0, n - flat.shape[0])).reshape(-1, PACK_W))
    out = jnp.concatenate(rows, axis=0)
    r = -(-out.shape[0] // 8) * 8
    return jnp.pad(out, ((0, r - out.shape[0]), (0, 0)))


def _unpack(packed, shapes):
    out, r0 = [], 0
    for shp in shapes:
        size = int(np.prod(shp))
        nr = -(-size // PACK_W)
        out.append(packed[r0:r0 + nr].reshape(-1)[:size].reshape(shp))
        r0 += nr
    return out


def _hosted(fn, hook, *args, **kw):
    if hook is None:
        return fn(*args, **kw)
    res, rider_out = fn(*args, rider=hook[0], **kw)
    hook[1](rider_out)
    return res


def _layer_fwd(l, x, mod, W, P, hooks=None):
    hooks = hooks or {}
    s = x.shape[0]
    sh_m, sc_m, g_m, sh_f, sc_f, g_f = [mod[l:l + 1, D_MODEL * j:D_MODEL * (j + 1)] for j in range(6)]
    n = lambda t: f"{t}{l}"
    w_in = W["w_in"][l]
    u = _ln_mod(x, sc_m, sh_m, n("ln_mod_mix"))
    tmz = min(1024, s)
    zp = _mm(u, w_in, "nt", tm=min(2048, s), tn=256, out_dtype=F32, name=n("z_pool"), b_col0=0, n_out=D_POOL)
    qkv = _mm(u, w_in, "nt", tm=tmz, tn=256, out_dtype=BF16, name=n("z_qkv"), b_col0=OFF_QKV // 256, n_out=3 * D_ATTN)
    zc = _mm(u, w_in, "nt", tm=tmz, tn=256, out_dtype=F32, name=n("z_conv"), b_col0=OFF_CONV // 256, n_out=2 * D_CONV)
    zg = _mm(u, w_in, "nt", tm=tmz, tn=768, out_dtype=BF16, name=n("z_gate"), b_col0=OFF_GATE // 768, n_out=3 * D_MODEL)

    p, feat_pool = _pool_fwd(zp, P["wp_bd"][l], P["pool_scale"][l], n("pool_fwd"))
    bias = _bias_block(P["rel_bias"][l], n("bias_block"))
    o = _hosted(_attn_fwd, hooks.get("attn"), qkv, bias, n("attn_fwd"))
    cv, feat_conv = _conv_fwd(zc, P["conv_w"][l], P["conv_b"][l], P["conv_ln_g"][l], P["conv_ln_b"][l], n("conv_fwd"))

    tmb = min(1024, s)
    y_pool = _mm(feat_pool, W["w_br_pool"][l], "nt", tm=tmb, tn=1024, out_dtype=F32, name=n("y_pool"))
    y_attn = _mm(o, W["w_br_attn"][l], "nt", tm=tmb, tn=1024, out_dtype=F32, name=n("y_attn"))
    y_conv = _mm(feat_conv, W["w_br_conv"][l], "nt", tm=tmb, tn=1024, out_dtype=F32, name=n("y_conv"))
    ys = (y_pool, y_attn, y_conv)
    merged = _merge(zg, P["b_gate"][l], ys, n("merge"))
    mix = _mm(merged, W["w_o"][l], "nn", tm=tmb, tn=1024, out_dtype=F32, name=n("mix_out"))
    x1 = _resid_ln(x, mix, g_m, P["ln_mix_g"][l], P["ln_mix_b"][l], n("resid_ln_mix"))

    u2 = _ln_mod(x1, sc_f, sh_f, n("ln_mod_ff"))
    hpre, hid = _hosted(_ff_hidden, hooks.get("ff1"), u2, W["w_ff1"][l], P["b_ff1"][l], n("ff1"))
    ff = _hosted(_mm, hooks.get("ff2"), hid, W["w_ff2"][l], "nn", tm=min(512, s), tn=1024, out_dtype=F32,
                 name=n("ff2"), bias=P["b_ff2"][l])
    x2 = _resid_ln(x1, ff, g_f, P["ln_ff_g"][l], P["ln_ff_b"][l], n("resid_ln_ff"))
    saved = dict(x=x, u=u, zp=zp, qkv=qkv, zc=zc, zg=zg, p=p, feat_pool=feat_pool, bias=bias, o=o, cv=cv,
                 feat_conv=feat_conv, ys=ys, merged=merged, mix=mix, x1=x1, u2=u2, hpre=hpre, hid=hid, ff=ff)
    return x2, saved


def _layer_bwd(l, dx2, mod, W, P, A, hooks=None):
    hooks = hooks or {}
    s = dx2.shape[0]
    sh_m, sc_m, g_m, sh_f, sc_f, g_f = [mod[l:l + 1, D_MODEL * j:D_MODEL * (j + 1)] for j in range(6)]
    n = lambda t: f"{t}{l}"
    tmb = min(1024, s)
    gw, gs = {}, {}

    dres, dff, gs["ln_ff_g"], gs["ln_ff_b"], dg_f, gs["b_ff2"] = _resid_ln_bwd(
        dx2, A["x1"], A["ff"], g_f, P["ln_ff_g"][l], n("resid_ln_ff_bwd"))
    gw["w_ff2"] = _mm(A["hid"], dff, "tn", tm=512, tn=1024, out_dtype=BF16, name=n("dw_ff2"), split_n=512)
    dhpre, gs["b_ff1"] = _ff_hidden_bwd(dff, W["w_ff2"][l], A["hpre"], n("ff_hidden_bwd"))
    gw["w_ff1"] = _mm(dhpre, A["u2"], "tn", tm=512, tn=1024, out_dtype=BF16, name=n("dw_ff1"), split_n=512)
    du2 = _mm(dhpre, W["w_ff1"][l], "nn", tm=min(512, s), tn=1024, out_dtype=F32, name=n("du_ff"))
    dx1, dsc_f, dsh_f = _ln_mod_bwd(du2, A["x1"], sc_f, dres, n("ln_mod_ff_bwd"))

    dres, dmix, gs["ln_mix_g"], gs["ln_mix_b"], dg_m, _ = _resid_ln_bwd(
        dx1, A["x"], A["mix"], g_m, P["ln_mix_g"][l], n("resid_ln_mix_bwd"))
    gw["w_o"] = _mm(A["merged"], dmix, "tn", tm=512, tn=1024, out_dtype=BF16, name=n("dw_o"), split_n=512)
    dmerged = _mm(dmix, W["w_o"][l], "nt", tm=tmb, tn=1024, out_dtype=F32, name=n("d_merged"))
    dy_pool, dy_attn, dy_conv, dzg, gs["b_gate"] = _merge_bwd(dmerged, A["zg"], P["b_gate"][l], A["ys"], n("merge_bwd"))

    gw["w_br_pool"] = _mm(dy_pool, A["feat_pool"], "tn", tm=512, tn=256, out_dtype=BF16, name=n("dw_br_pool"),
                          split_n=128)
    gw["w_br_attn"] = _mm(dy_attn, A["o"], "tn", tm=512, tn=512, out_dtype=BF16, name=n("dw_br_attn"), split_n=256)
    gw["w_br_conv"] = _mm(dy_conv, A["feat_conv"], "tn", tm=512, tn=256, out_dtype=BF16, name=n("dw_br_conv"),
                          split_n=128)
    dfeat_pool = _mm(dy_pool, W["w_br_pool"][l], "nn", tm=tmb, tn=256, out_dtype=F32, name=n("d_feat_pool"))
    do = _mm(dy_attn, W["w_br_attn"][l], "nn", tm=tmb, tn=512, out_dtype=BF16, name=n("d_attn_out"))
    dfeat_conv = _mm(dy_conv, W["w_br_conv"][l], "nn", tm=tmb, tn=256, out_dtype=F32, name=n("d_feat_conv"))

    dzp, dwp_bd, gs["pool_scale"] = _pool_bwd(dfeat_pool, A["p"], P["wp_bd"][l], P["pool_scale"][l], n("pool_bwd"))
    gs["w_pool"] = jnp.stack([dwp_bd[POOL_GROUP * g:POOL_GROUP * (g + 1), POOL_GROUP * g:POOL_GROUP * (g + 1)]
                              for g in range(len(POOL_WINDOWS))])
    hook = hooks["attn"](gw) if "attn" in hooks else None
    dq, dk, dv, ds_acc = _hosted(_attn_bwd, hook, A["qkv"], do, A["bias"], n("attn_bwd"))
    gs["rel_bias"] = _bias_block_bwd(ds_acc, n("bias_block_bwd"))
    dzc, dcw, gs["conv_b"], gs["conv_ln_g"], gs["conv_ln_b"] = _conv_bwd(
        dfeat_conv, A["cv"], A["zc"], P["conv_w"][l], P["conv_ln_g"][l], P["conv_ln_b"][l], n("conv_bwd"))
    gs["conv_w"] = dcw[:CONV_WIDTH]

    dz = jnp.concatenate([dzp, dq, dk[KPAD:].astype(BF16), dv[KPAD:].astype(BF16), dzc, dzg], axis=1)
    gw["w_in"] = _mm(dz, A["u"], "tn", tm=768, tn=1024, out_dtype=BF16, name=n("dw_in"), split_n=512)
    hook = hooks["du_mix"](gw) if "du_mix" in hooks else None
    du = _hosted(_mm, hook, dz, W["w_in"][l], "nn", tm=min(512, s), tn=1024, out_dtype=F32, name=n("du_mix"))
    dx, dsc_m, dsh_m = _ln_mod_bwd(du, A["x"], sc_m, dres, n("ln_mod_mix_bwd"))
    dmod = jnp.concatenate([dsh_m, dsc_m, dg_m, dsh_f, dsc_f, dg_f], axis=1)
    return dx, gw, gs, dmod


def _small_shapes():
    return {"b_ada": (6 * D_MODEL,), "b_gate": (3 * D_MODEL,), "w_pool": (4, POOL_GROUP, POOL_GROUP),
            "pool_scale": (D_POOL,), "rel_bias": (N_HEADS, N_REL), "conv_w": (CONV_WIDTH, D_CONV),
            "conv_b": (D_CONV,), "conv_ln_g": (D_CONV,), "conv_ln_b": (D_CONV,), "ln_mix_g": (D_MODEL,),
            "ln_mix_b": (D_MODEL,), "b_ff1": (D_FF,), "b_ff2": (D_MODEL,), "ln_ff_g": (D_MODEL,), "ln_ff_b": (D_MODEL,)}


def kernel(x, c, w_ada, b_ada, w_in, b_gate, w_pool, pool_scale, rel_bias, conv_w, conv_b, conv_ln_g, conv_ln_b, w_br_pool, w_br_attn, w_br_conv, w_o, ln_mix_g, ln_mix_b, w_ff1, b_ff1, w_ff2, b_ff2, ln_ff_g, ln_ff_b, loss_target, m_w_ada, m_b_ada, m_w_in, m_b_gate, m_w_pool, m_pool_scale, m_rel_bias, m_conv_w, m_conv_b, m_conv_ln_g, m_conv_ln_b, m_w_br_pool, m_w_br_attn, m_w_br_conv, m_w_o, m_ln_mix_g, m_ln_mix_b, m_w_ff1, m_b_ff1, m_w_ff2, m_b_ff2, m_ln_ff_g, m_ln_ff_b, v_w_ada, v_b_ada, v_w_in, v_b_gate, v_w_pool, v_pool_scale, v_rel_bias, v_conv_w, v_conv_b, v_conv_ln_g, v_conv_ln_b, v_w_br_pool, v_w_br_attn, v_w_br_conv, v_w_o, v_ln_mix_g, v_ln_mix_b, v_w_ff1, v_b_ff1, v_w_ff2, v_b_ff2, v_ln_ff_g, v_ln_ff_b):
    env = dict(locals())
    xi, yi, ci = _me()
    chip = 2 * xi + yi
    me = 4 * xi + 2 * yi + ci
    xs = x[0]
    tgt = loss_target[0]
    L = DEPTH

    c_all = _allgather_small(c.reshape(8, 128), "gather_c").reshape(N_DEV, D_MODEL)
    ada_cols = w_ada.shape[2]
    b_ada_sh = lax.dynamic_slice_in_dim(b_ada, chip * ada_cols, ada_cols, axis=1).reshape(L, 1, ada_cols)
    mod_part = _mod_fwd(c_all, w_ada, b_ada_sh, "mod_fwd")
    mod_g = _allgather_small(mod_part.reshape(-1, 128), "gather_mod").reshape(N_CHIP, 2, L, N_DEV, ada_cols)[:, 0]
    mod_all = jnp.transpose(mod_g, (1, 2, 0, 3)).reshape(L, N_DEV, 6 * D_MODEL)
    mod = lax.dynamic_index_in_dim(mod_all, me, axis=1, keepdims=False)

    W = {k: [None] * L for k in BIG}

    def weight_gather(names, l):
        shards = [(jnp.swapaxes(env[k][l], 0, 1) if k in COL_SHARDED else env[k][l]).astype(BF16) for k in names]
        shards = [a.reshape(2, a.shape[0] // 2, a.shape[1]) for a in shards]

        def done(outs):
            for k, g in zip(names, outs):
                W[k][l] = g.reshape(-1, g.shape[-1])

        return _gather_rider(shards), done

    first_names = ("w_in", "w_br_pool", "w_br_attn", "w_br_conv", "w_o")
    late_names = ("w_ff1", "w_ff2")
    rider, done = weight_gather(first_names, 0)
    done(_run_rider(rider, "gather_weights_first0"))
    fwd_hooks = [{"attn": weight_gather(late_names, 0), "ff1": weight_gather(("w_in",), 1),
                  "ff2": weight_gather(("w_br_pool", "w_br_attn", "w_br_conv", "w_o"), 1)},
                 {"attn": weight_gather(late_names, 1)}]

    P = {k: env[k] for k in ("rel_bias", "conv_w")}
    for k in ("b_gate", "pool_scale", "conv_b", "conv_ln_g", "conv_ln_b", "ln_mix_g", "ln_mix_b", "b_ff1", "b_ff2",
              "ln_ff_g", "ln_ff_b"):
        P[k] = env[k].reshape(L, 1, -1)
    conv_w_full = _allgather_small(_pack([conv_w]), "gather_conv_w")
    n_cw = conv_w.size
    cw = conv_w_full.reshape(N_CHIP, 2, -1)[:, 0, :n_cw].reshape(N_CHIP, L, CONV_WIDTH, D_CONV // N_CHIP)
    P["conv_w"] = jnp.transpose(cw, (1, 2, 0, 3)).reshape(L, CONV_WIDTH, D_CONV)
    wp_bd = jnp.zeros((L, D_POOL, D_POOL), F32)
    for g in range(len(POOL_WINDOWS)):
        sl = slice(POOL_GROUP * g, POOL_GROUP * (g + 1))
        wp_bd = wp_bd.at[:, sl, sl].set(w_pool[:, g])
    P["wp_bd"] = wp_bd.astype(BF16)

    acts = []
    h = xs
    for l in range(L):
        h, saved = _layer_fwd(l, h, mod, W, P, fwd_hooks[l])
        acts.append(saved)
    dy, loss_part = _loss_grad(h, tgt, "loss_grad")
    loss = lax.psum(loss_part[0, 0], ("x", "y", "c"))

    place = jnp.stack([ci, chip, chip ^ 1, chip ^ 2, chip ^ 3]).astype(jnp.int32)
    scattered = {}

    def grad_scatter(items, tag):
        dws = [dw for _, _, dw in items]
        got = _sibling_send(dws, f"swap_blocks_{tag}", other_half=True)
        both = [_sum_cores(a, b, place, f"sum_cores_{k}{l}") for (k, l, _), a, b in zip(items, dws, got)]
        both = [hh.reshape(N_CHIP, -1, hh.shape[-1]) for hh in both]

        def done(outs):
            for (k, l, _), hh, r in zip(items, both, outs):
                scattered[(k, l)] = (hh, r)

        return _scatter_rider(both), done

    early = ("w_ff2", "w_ff1", "w_o", "w_br_pool", "w_br_attn", "w_br_conv")
    left_over = []

    def attn_hook(l):
        def hook(gw):
            items = left_over + [(k, l, gw[k]) for k in early]
            left_over.clear()
            return grad_scatter(items, f"attn{l}")
        return hook

    def last_hook(gw):
        return grad_scatter([("w_in", 0, gw["w_in"])], "last")

    gws, gss, dmods = [None] * L, [None] * L, [None] * L
    dh = dy
    for l in reversed(range(L)):
        hooks = {"attn": attn_hook(l)}
        if l == 0:
            hooks["du_mix"] = last_hook
        dh, gws[l], gss[l], dmods[l] = _layer_bwd(l, dh, mod, W, P, acts[l], hooks)
        if l > 0:
            left_over.append(("w_in", l, gws[l]["w_in"]))
    grad_x = dh[None]

    reduced = [[_sum_chips(*scattered[(k, l)], place, f"sum_chips_{k}{l}") for l in range(L)] for k in BIG]
    flat_reduced = [t for per_weight in reduced for t in per_weight]
    flat_other = _sibling_send(flat_reduced, "swap_reduced")

    shapes = _small_shapes()
    small_names = [k for k in SMALL if k != "b_ada"]
    dmod_own = jnp.concatenate(dmods, axis=0)
    pack = _pack([dmod_own] + [jnp.stack([gss[l][k].reshape(shapes[k]) for l in range(L)]) for k in small_names])
    small_rider = _allgather_rider(pack.reshape(-1, 128))

    out = {}
    for j, k in enumerate(BIG):
        own, other = reduced[j], flat_other[L * j:L * (j + 1)]
        rider = small_rider if j == 0 else None
        if k == "w_in":
            t = lambda a: jnp.swapaxes(a, 1, 2)
            res = _adamw_halves(t(env[k]), t(env["m_" + k]), t(env["v_" + k]), own, other, place, "cols",
                                f"adamw_{k}", rider=rider)
            res, rider_out = res if rider is not None else (res, None)
            res = [t(a) for a in res]
        else:
            if k in COL_SHARDED:
                own, other = [a.T for a in own], [a.T for a in other]
            res = _adamw_halves(env[k], env["m_" + k], env["v_" + k], own, other, place,
                                "rows" if k in COL_SHARDED else "cols", f"adamw_{k}", rider=rider)
            res, rider_out = res if rider is not None else (res, None)
        if rider is not None:
            g_all = rider_out[0].reshape(N_DEV, -1, PACK_W)
        out[k] = tuple(res)

    dmod_all = g_all[:, :L * 6].reshape(N_DEV, L, 6 * D_MODEL)
    dmod_sh = jnp.transpose(lax.dynamic_slice_in_dim(dmod_all, chip * ada_cols, ada_cols, axis=2), (1, 0, 2))
    g_ada = _mod_bwd(c_all, dmod_sh, "mod_bwd")
    g_, d_, m_, v_ = _adamw(w_ada.reshape(-1, ada_cols), m_w_ada.reshape(-1, ada_cols), v_w_ada.reshape(-1, ada_cols),
                            [g_ada.reshape(-1, ada_cols)], "adamw_w_ada")
    out["w_ada"] = tuple(a.reshape(w_ada.shape) for a in (g_, d_, m_, v_))

    def small_pack(prefix):
        parts = [env[prefix + "b_ada"]]
        for k in small_names:
            a = env[prefix + k]
            if k == "conv_w":
                a = jnp.zeros((L,) + shapes[k], F32)
            parts.append(a)
        return _pack(parts)

    gp, dp, mp, vp = _adamw_small(small_pack(""), small_pack("m_"), small_pack("v_"), g_all, "adamw_small")
    full_shapes = [(L,) + shapes["b_ada"]] + [(L,) + shapes[k] for k in small_names]
    for tag, packed in (("g", gp), ("d", dp), ("m", mp), ("v", vp)):
        for k, a in zip(["b_ada"] + small_names, _unpack(packed, full_shapes)):
            out.setdefault(k, {})
            out[k][tag] = a
    g_cw_full = out["conv_w"]["g"]
    cw_cols = D_CONV // N_CHIP
    g_cw = lax.dynamic_slice_in_dim(g_cw_full, chip * cw_cols, cw_cols, axis=2)
    pad_rows = lambda a: jnp.pad(a.reshape(L * CONV_WIDTH, cw_cols), ((0, 2), (0, 0)))
    g_, d_, m_, v_ = _adamw(pad_rows(conv_w), pad_rows(m_conv_w), pad_rows(v_conv_w), [pad_rows(g_cw)], "adamw_conv_w")
    out["conv_w"] = tuple(a[:L * CONV_WIDTH].reshape(L, CONV_WIDTH, cw_cols) for a in (g_, d_, m_, v_))

    names = ["w_ada", "b_ada", "w_in", "b_gate", "w_pool", "pool_scale", "rel_bias", "conv_w", "conv_b", "conv_ln_g",
             "conv_ln_b", "w_br_pool", "w_br_attn", "w_br_conv", "w_o", "ln_mix_g", "ln_mix_b", "w_ff1", "b_ff1",
             "w_ff2", "b_ff2", "ln_ff_g", "ln_ff_b"]

    def pick(k, i):
        o = out[k]
        return o[i] if isinstance(o, tuple) else o["gdmv"[i]].reshape(env[k].shape)

    return (loss, grad_x, *[pick(k, 0) for k in names], *[pick(k, 1) for k in names],
            *[pick(k, 2) for k in names], *[pick(k, 3) for k in names])
```

```python
import functools

import jax
import jax.numpy as jnp
import numpy as np
from jax import lax
from jax.experimental import pallas as pl
from jax.experimental.pallas import tpu as pltpu

F32 = jnp.float32
BF16 = jnp.bfloat16

D_MODEL = 1024
DEPTH = 2
CHUNK = 64
POOL_WINDOWS = (2, 4, 8, 16)
POOL_GROUP = 64
D_POOL = 256
N_HEADS = 8
HEAD_DIM = 64
D_ATTN = 512
N_PREV_CHUNKS = 8
REL_CLIP = 128
N_REL = 2 * REL_CLIP + 1
D_CONV = 256
CONV_WIDTH = 31
D_FF = 4 * D_MODEL
D_IN = 5376
OFF_POOL, OFF_QKV, OFF_CONV, OFF_GATE = 0, 256, 1792, 2304
ALPHA = (2.0 * DEPTH) ** 0.25
LN_EPS = 1e-5
NEG_INF = -1e30
ADAM_LR, ADAM_B1, ADAM_B2, ADAM_EPS, ADAM_WD, ADAM_STEP = 0.001, 0.9, 0.999, 1e-08, 0.01, 10

N_DEV = 8
N_CHIP = 4
MESH = pl.DeviceIdType.MESH

QB = 2 * CHUNK
KPAD = N_PREV_CHUNKS * CHUNK
KW = QB + KPAD
SKEW_W = 768

VMEM_LIMIT = 56 * 1024 * 1024


def _cparams(**kw):
    return pltpu.CompilerParams(vmem_limit_bytes=VMEM_LIMIT, **kw)


def _full(shape):
    n = len(shape)
    return pl.BlockSpec(shape, lambda *_: (0,) * n)


_DIMS = {"nn": (((1,), (0,)), ((), ())), "nt": (((1,), (1,)), ((), ())), "tn": (((0,), (0,)), ((), ()))}


def _relu2(t):
    r = jnp.maximum(t, 0.0)
    return r * r


def _mm(a, b, mode, *, tm, tn, out_dtype, name, b_col0=0, n_out=None, bias=None, split_n=0, rider=None):
    if mode == "tn":
        k, m = a.shape
        n = b.shape[1] if n_out is None else n_out
        a_spec = pl.BlockSpec((k, tm), lambda i, j: (0, i))
        b_spec = pl.BlockSpec((k, tn), lambda i, j: (0, j + b_col0))
    elif mode == "nn":
        m, k = a.shape
        n = b.shape[1] if n_out is None else n_out
        a_spec = pl.BlockSpec((tm, k), lambda i, j: (i, 0))
        b_spec = pl.BlockSpec((k, tn), lambda i, j: (0, j + b_col0))
    else:
        m, k = a.shape
        n = b.shape[0] if n_out is None else n_out
        a_spec = pl.BlockSpec((tm, k), lambda i, j: (i, 0))
        b_spec = pl.BlockSpec((tn, k), lambda i, j: (j + b_col0, 0))
    assert m % tm == 0 and n % tn == 0, (name, m, n, tm, tn)
    dims = _DIMS[mode]

    def body(*refs):
        if bias is None:
            a_ref, b_ref, o_ref = refs
        else:
            a_ref, b_ref, bias_ref, o_ref = refs
        acc = lax.dot_general(a_ref[...].astype(BF16), b_ref[...].astype(BF16), dims, preferred_element_type=F32)
        if bias is not None:
            acc = acc + bias_ref[...]
        if split_n:
            for c in range(tn // split_n):
                o_ref[c] = acc[:, c * split_n:(c + 1) * split_n].astype(out_dtype)
        else:
            o_ref[...] = acc.astype(out_dtype)

    in_specs = [a_spec, b_spec]
    args = [a, b]
    if bias is not None:
        in_specs.append(pl.BlockSpec((1, tn), lambda i, j: (0, j)))
        args.append(bias)
    if split_n:
        out_spec = pl.BlockSpec((tn // split_n, tm, split_n), lambda i, j: (j, i, 0))
        out_shape = jax.ShapeDtypeStruct((n // split_n, m, split_n), out_dtype)
    else:
        out_spec = pl.BlockSpec((tm, tn), lambda i, j: (i, j))
        out_shape = jax.ShapeDtypeStruct((m, n), out_dtype)
    res = _call(body, name=name, grid=(m // tm, n // tn), in_specs=in_specs, out_specs=[out_spec],
                out_shape=[out_shape], scratch_shapes=[], args=args, rider=rider)
    return res[0] if rider is None else (res[0][0], res[1])


def _ln_hat(x):
    mu = jnp.mean(x, axis=-1, keepdims=True)
    xc = x - mu
    var = jnp.mean(xc * xc, axis=-1, keepdims=True)
    rstd = lax.rsqrt(var + LN_EPS)
    return xc * rstd, rstd


def _ln_hat_bwd(dhat, xhat, rstd):
    m1 = jnp.mean(dhat, axis=-1, keepdims=True)
    m2 = jnp.mean(dhat * xhat, axis=-1, keepdims=True)
    return rstd * (dhat - m1 - xhat * m2)


def _row_tile(s):
    return min(512, s)


def _acc_rows(ref, val, first):
    @pl.when(first)
    def _():
        ref[...] = jnp.zeros_like(ref)
    ref[...] += jnp.sum(val, axis=0, keepdims=True)


def _ln_mod(x, sc, sh, name):
    s, d = x.shape
    tm = _row_tile(s)

    def body(x_ref, sc_ref, sh_ref, u_ref):
        xhat, _ = _ln_hat(x_ref[...])
        u_ref[...] = (xhat * (1.0 + sc_ref[...]) + sh_ref[...]).astype(BF16)

    row = pl.BlockSpec((tm, d), lambda i: (i, 0))
    vec = pl.BlockSpec((1, d), lambda i: (0, 0))
    return pl.pallas_call(body, grid=(s // tm,), in_specs=[row, vec, vec], out_specs=row,
                          out_shape=jax.ShapeDtypeStruct((s, d), BF16), name=name, compiler_params=_cparams())(x, sc, sh)


def _mm_ln_mod_bwd(a, b, x, sc, dres, name, rider=None):
    s, k = a.shape
    d = b.shape[1]
    tm = min(512 if k <= 4096 else 256, s)

    def body(a_ref, b_ref, x_ref, sc_ref, dres_ref, dx_ref, dsc_ref, dsh_ref):
        first = pl.program_id(0) == 0
        duv = jnp.dot(a_ref[...], b_ref[...], preferred_element_type=F32)
        xhat, rstd = _ln_hat(x_ref[...])
        dx_ref[...] = dres_ref[...] + _ln_hat_bwd(duv * (1.0 + sc_ref[...]), xhat, rstd)
        _acc_rows(dsc_ref, duv * xhat, first)
        _acc_rows(dsh_ref, duv, first)

    row = pl.BlockSpec((tm, d), lambda i: (i, 0))
    vec = pl.BlockSpec((1, d), lambda i: (0, 0))
    vs = jax.ShapeDtypeStruct((1, d), F32)
    res = _call(body, name=name, grid=(s // tm,),
                in_specs=[pl.BlockSpec((tm, k), lambda i: (i, 0)), _full((k, d)), row, vec, row],
                out_specs=[row, vec, vec], out_shape=[jax.ShapeDtypeStruct((s, d), F32), vs, vs],
                scratch_shapes=[], args=(a, b, x, sc, dres), rider=rider)
    return tuple(res) if rider is None else (tuple(res[0]), res[1])


def _mm_resid_ln(a, b, bias, x, g, gam, bet, name, rider=None):
    s, k = a.shape
    d = b.shape[1]
    tm = min(512, s)

    def body(*refs):
        if bias is None:
            a_ref, b_ref, x_ref, g_ref, gam_ref, bet_ref, f_ref, o_ref = refs
        else:
            a_ref, b_ref, bias_ref, x_ref, g_ref, gam_ref, bet_ref, f_ref, o_ref = refs
        f = jnp.dot(a_ref[...], b_ref[...], preferred_element_type=F32)
        if bias is not None:
            f = f + bias_ref[...]
        f_ref[...] = f
        rhat, _ = _ln_hat(ALPHA * x_ref[...] + g_ref[...] * f)
        o_ref[...] = rhat * gam_ref[...] + bet_ref[...]

    row = pl.BlockSpec((tm, d), lambda i: (i, 0))
    vec = pl.BlockSpec((1, d), lambda i: (0, 0))
    in_specs = [pl.BlockSpec((tm, k), lambda i: (i, 0)), _full((k, d))] + ([vec] if bias is not None else []) + [row, vec, vec, vec]
    args = [a, b] + ([bias] if bias is not None else []) + [x, g, gam, bet]
    sh = jax.ShapeDtypeStruct((s, d), F32)
    res = _call(body, name=name, grid=(s // tm,), in_specs=in_specs, out_specs=[row, row], out_shape=[sh, sh],
                scratch_shapes=[], args=args, rider=rider)
    return tuple(res) if rider is None else (tuple(res[0]), res[1])


def _resid_ln_bwd(dxo, x, f, g, gam, name):
    s, d = x.shape
    tm = _row_tile(s)

    def body(dxo_ref, x_ref, f_ref, g_ref, gam_ref, dres_ref, df_ref, dgam_ref, dbet_ref, dg_ref, dbias_ref):
        first = pl.program_id(0) == 0
        dxov = dxo_ref[...]
        fv = f_ref[...]
        rhat, rstd = _ln_hat(ALPHA * x_ref[...] + g_ref[...] * fv)
        dr = _ln_hat_bwd(dxov * gam_ref[...], rhat, rstd)
        dfv = g_ref[...] * dr
        dres_ref[...] = ALPHA * dr
        df_ref[...] = dfv.astype(BF16)
        _acc_rows(dgam_ref, dxov * rhat, first)
        _acc_rows(dbet_ref, dxov, first)
        _acc_rows(dg_ref, dr * fv, first)
        _acc_rows(dbias_ref, dfv, first)

    row = pl.BlockSpec((tm, d), lambda i: (i, 0))
    vec = pl.BlockSpec((1, d), lambda i: (0, 0))
    vs = jax.ShapeDtypeStruct((1, d), F32)
    return pl.pallas_call(body, grid=(s // tm,), in_specs=[row, row, row, vec, vec],
                          out_specs=[row, row, vec, vec, vec, vec],
                          out_shape=[jax.ShapeDtypeStruct((s, d), F32), jax.ShapeDtypeStruct((s, d), BF16), vs, vs, vs, vs],
                          name=name, compiler_params=_cparams())(dxo, x, f, g, gam)


def _loss_grad(y, tgt, name):
    s, d = y.shape
    tm = _row_tile(s)
    n = s // tm

    def body(y_ref, t_ref, dy_ref, loss_ref, acc_ref):
        i = pl.program_id(0)
        e = y_ref[...] - t_ref[...]
        dy_ref[...] = e * (1.0 / d)
        _acc_rows(acc_ref, e * e, i == 0)

        @pl.when(i == n - 1)
        def _():
            tot = jnp.sum(acc_ref[...], axis=1, keepdims=True) * (0.5 / d)
            loss_ref[...] = jnp.broadcast_to(tot, (1, 128))

    row = pl.BlockSpec((tm, d), lambda i: (i, 0))
    return pl.pallas_call(body, grid=(n,), in_specs=[row, row],
                          out_specs=[row, pl.BlockSpec((1, 128), lambda i: (0, 0))],
                          out_shape=[jax.ShapeDtypeStruct((s, d), F32), jax.ShapeDtypeStruct((1, 128), F32)],
                          scratch_shapes=[pltpu.VMEM((1, d), F32)], name=name, compiler_params=_cparams())(y, tgt)


POOL_HALO = 16
POOL_ROWS = 256


def _pool_counts(r0, rows):
    t1 = (lax.broadcasted_iota(jnp.int32, (rows, 128), 0) + r0 + 1).astype(F32)
    low = lax.broadcasted_iota(jnp.int32, (rows, 128), 1) < POOL_GROUP
    wa = jnp.where(low, float(POOL_WINDOWS[0]), float(POOL_WINDOWS[1]))
    wb = jnp.where(low, float(POOL_WINDOWS[2]), float(POOL_WINDOWS[3]))
    return jnp.minimum(t1, wa), jnp.minimum(t1, wb), low


def _window_sums(win, off, rows, sign):
    def sl(j, half):
        return win[off + sign * j: off + sign * j + rows, 128 * half:128 * half + 128]
    a2 = sl(0, 0) + sl(1, 0)
    a4 = a2 + sl(2, 0) + sl(3, 0)
    a8 = sl(0, 1)
    for j in range(1, 8):
        a8 = a8 + sl(j, 1)
    a16 = a8
    for j in range(8, 16):
        a16 = a16 + sl(j, 1)
    return a2, a4, a8, a16


def _pool_fwd(zp, wp_bd, pscale, name):
    s = zp.shape[0]
    r = min(POOL_ROWS, s)

    def body(z_ref, wp_ref, sc_ref, p_ref, feat_ref, pad):
        pad[0:POOL_HALO, :] = jnp.zeros((POOL_HALO, D_POOL), F32)
        pad[POOL_HALO:, :] = z_ref[...]

        def step(i, carry):
            r0 = pl.multiple_of(i * r, r)
            win = pad[pl.ds(r0, r + POOL_HALO), :]
            a2, a4, a8, a16 = _window_sums(win, POOL_HALO, r, -1)
            ca, cb, low = _pool_counts(r0, r)
            x0 = win[POOL_HALO:, :]
            pa = jnp.where(low, a2, a4) / ca
            pb = jnp.where(low, a8, a16) / cb
            p = (jnp.concatenate([pa, pb], axis=1) - x0).astype(BF16)
            p_ref[pl.ds(r0, r), :] = p
            pw = jnp.dot(p, wp_ref[...], preferred_element_type=F32)
            feat_ref[pl.ds(r0, r), :] = (pw * sc_ref[...]).astype(BF16)
            return carry

        lax.fori_loop(0, s // r, step, 0)

    return pl.pallas_call(
        body, out_shape=[jax.ShapeDtypeStruct((s, D_POOL), BF16), jax.ShapeDtypeStruct((s, D_POOL), BF16)],
        scratch_shapes=[pltpu.VMEM((s + POOL_HALO, D_POOL), F32)], name=name, compiler_params=_cparams(),
    )(zp, wp_bd, pscale)


def _pool_bwd(dfeat, p, wp_bd, pscale, name):
    s = p.shape[0]
    r = min(POOL_ROWS, s)

    def body(df_ref, p_ref, wp_ref, sc_ref, dz_ref, dwp_ref, dsc_ref, gpad, dpbuf):
        dwp_ref[...] = jnp.zeros_like(dwp_ref)
        dsc_ref[...] = jnp.zeros_like(dsc_ref)
        gpad[s:, :] = jnp.zeros((POOL_HALO, D_POOL), F32)

        def step1(i, carry):
            r0 = pl.multiple_of(i * r, r)
            pv = p_ref[pl.ds(r0, r), :]
            dfv = df_ref[pl.ds(r0, r), :]
            pw = jnp.dot(pv, wp_ref[...], preferred_element_type=F32)
            dsc_ref[...] += jnp.sum(dfv * pw, axis=0, keepdims=True)
            dpw = (dfv * sc_ref[...]).astype(BF16)
            dwp_ref[...] += lax.dot_general(pv, dpw, _DIMS["tn"], preferred_element_type=F32)
            dp = lax.dot_general(dpw, wp_ref[...], _DIMS["nt"], preferred_element_type=F32)
            ca, cb, _ = _pool_counts(r0, r)
            gpad[pl.ds(r0, r), :] = dp / jnp.concatenate([ca, cb], axis=1)
            dpbuf[pl.ds(r0, r), :] = dp
            return carry

        lax.fori_loop(0, s // r, step1, 0)

        def step2(i, carry):
            r0 = pl.multiple_of(i * r, r)
            win = gpad[pl.ds(r0, r + POOL_HALO), :]
            a2, a4, a8, a16 = _window_sums(win, 0, r, 1)
            low = lax.broadcasted_iota(jnp.int32, (r, 128), 1) < POOL_GROUP
            acc = jnp.concatenate([jnp.where(low, a2, a4), jnp.where(low, a8, a16)], axis=1)
            dz_ref[pl.ds(r0, r), :] = (acc - dpbuf[pl.ds(r0, r), :]).astype(BF16)
            return carry

        lax.fori_loop(0, s // r, step2, 0)

    return pl.pallas_call(
        body,
        out_shape=[jax.ShapeDtypeStruct((s, D_POOL), BF16), jax.ShapeDtypeStruct((D_POOL, D_POOL), F32),
                   jax.ShapeDtypeStruct((1, D_POOL), F32)],
        scratch_shapes=[pltpu.VMEM((s + POOL_HALO, D_POOL), F32), pltpu.VMEM((s, D_POOL), F32)],
        name=name, compiler_params=_cparams(),
    )(dfeat, p, wp_bd, pscale)


def _skew_index():
    cp = lax.broadcasted_iota(jnp.int32, (SKEW_W, N_REL), 0)
    dist = jnp.where(cp < KW, KPAD - cp, KPAD + SKEW_W - cp)
    idx = jnp.clip(dist, -REL_CLIP, REL_CLIP) + REL_CLIP
    return (idx == lax.broadcasted_iota(jnp.int32, (SKEW_W, N_REL), 1)).astype(F32)


def _row_bits(b):
    return (lax.broadcasted_iota(jnp.int32, (QB, SKEW_W), 0) >> b) & 1 == 1


N_EDGE = KPAD // QB


def _bias_block(rel_bias, name):
    def body(rb_ref, o_ref):
        onehot = _skew_index()
        row0 = lax.dot_general(rb_ref[...], onehot, _DIMS["nt"], precision=lax.Precision.HIGHEST,
                               preferred_element_type=F32)
        r = lax.broadcasted_iota(jnp.int32, (QB, KW), 0)
        kk = lax.broadcasted_iota(jnp.int32, (QB, KW), 1)
        cq, ck = r // CHUNK, kk // CHUNK
        band = (ck >= cq) & (ck <= cq + N_PREV_CHUNKS)
        for h in range(N_HEADS):
            t = jnp.broadcast_to(row0[h:h + 1, :], (QB, SKEW_W))
            for b in range(7):
                t = jnp.where(_row_bits(b), pltpu.roll(t, 1 << b, 1), t)
            for e in range(N_EDGE + 1):
                o_ref[e, h] = jnp.where(band & (kk >= KPAD - e * QB), t[:, :KW], NEG_INF)

    return pl.pallas_call(body, out_shape=jax.ShapeDtypeStruct((N_EDGE + 1, N_HEADS, QB, KW), F32), name=name,
                          compiler_params=_cparams())(rel_bias)


def _bias_spec():
    return pl.BlockSpec((None, N_HEADS, QB, KW), lambda i: (jnp.minimum(i, N_EDGE), 0, 0, 0))


def _bias_block_bwd(ds_acc, name):
    def body(ds_ref, o_ref):
        sums = []
        for h in range(N_HEADS):
            t = jnp.concatenate([ds_ref[h], jnp.zeros((QB, SKEW_W - KW), F32)], axis=1)
            for b in range(7):
                t = jnp.where(_row_bits(b), pltpu.roll(t, SKEW_W - (1 << b), 1), t)
            sums.append(jnp.sum(t, axis=0, keepdims=True))
        allh = jnp.concatenate(sums, axis=0)
        o_ref[...] = jnp.dot(allh, _skew_index(), precision=lax.Precision.HIGHEST, preferred_element_type=F32)

    return pl.pallas_call(body, out_shape=jax.ShapeDtypeStruct((N_HEADS, N_REL), F32), name=name,
                          compiler_params=_cparams())(ds_acc)


def _scaled(q):
    return (q.astype(F32) * (HEAD_DIM ** -0.5)).astype(BF16)


def _probs(q, kw, bias_ref):
    sc = jnp.stack([lax.dot_general(q[:, HEAD_DIM * h:HEAD_DIM * (h + 1)], kw[:, HEAD_DIM * h:HEAD_DIM * (h + 1)],
                                    _DIMS["nt"], preferred_element_type=F32) + bias_ref[h] for h in range(N_HEADS)])
    e = jnp.exp(sc - jnp.max(sc, axis=-1, keepdims=True))
    return e * (1.0 / jnp.sum(e, axis=-1, keepdims=True))


def _load_padded_kv(qkv_hbm, kpad, vpad, sems, s):
    kpad[0:KPAD, :] = jnp.zeros((KPAD, D_ATTN), BF16)
    vpad[0:KPAD, :] = jnp.zeros((KPAD, D_ATTN), BF16)
    ck = pltpu.make_async_copy(qkv_hbm.at[:, D_ATTN:2 * D_ATTN], kpad.at[pl.ds(KPAD, s), :], sems.at[0])
    cv = pltpu.make_async_copy(qkv_hbm.at[:, 2 * D_ATTN:3 * D_ATTN], vpad.at[pl.ds(KPAD, s), :], sems.at[1])
    ck.start()
    cv.start()
    ck.wait()
    cv.wait()


def _attn_fwd(qkv, bias, name, rider=None):
    s = qkv.shape[0]

    def body(q_ref, qkv_hbm, bias_ref, o_ref, kpad, vpad, sems):
        i = pl.program_id(0)

        @pl.when(i == 0)
        def _():
            _load_padded_kv(qkv_hbm, kpad, vpad, sems, s)

        base = pl.multiple_of(i * QB, QB)
        kw = kpad[pl.ds(base, KW), :]
        vw = vpad[pl.ds(base, KW), :]
        q = _scaled(q_ref[...])
        p = _probs(q, kw, bias_ref).astype(BF16)
        outs = [jnp.dot(p[h], vw[:, HEAD_DIM * h:HEAD_DIM * (h + 1)], preferred_element_type=F32)
                for h in range(N_HEADS)]
        o_ref[...] = jnp.concatenate(outs, axis=1).astype(BF16)

    res = _call(
        body, name=name, grid=(s // QB,),
        in_specs=[pl.BlockSpec((QB, D_ATTN), lambda i: (i, 0)), pl.BlockSpec(memory_space=pl.ANY),
                  _bias_spec()],
        out_specs=[pl.BlockSpec((QB, D_ATTN), lambda i: (i, 0))],
        out_shape=[jax.ShapeDtypeStruct((s, D_ATTN), BF16)],
        scratch_shapes=[pltpu.VMEM((s + KPAD, D_ATTN), BF16), pltpu.VMEM((s + KPAD, D_ATTN), BF16),
                        pltpu.SemaphoreType.DMA((2,))],
        args=(qkv, qkv, bias), rider=rider)
    return res[0] if rider is None else (res[0][0], res[1])


def _attn_bwd(qkv, do, bias, name, rider=None):
    s = qkv.shape[0]
    n = s // QB

    def body(q_ref, qkv_hbm, do_ref, bias_ref, dq_ref, dk_hbm, dv_hbm, ds_ref, kpad, vpad, dkacc, dvacc, sems):
        i = pl.program_id(0)

        @pl.when(i == 0)
        def _():
            _load_padded_kv(qkv_hbm, kpad, vpad, sems, s)
            dkacc[...] = jnp.zeros_like(dkacc)
            dvacc[...] = jnp.zeros_like(dvacc)
            ds_ref[...] = jnp.zeros_like(ds_ref)

        base = pl.multiple_of(i * QB, QB)
        kw = kpad[pl.ds(base, KW), :]
        vw = vpad[pl.ds(base, KW), :]
        q = _scaled(q_ref[...])
        dov = do_ref[...]
        heads = [slice(HEAD_DIM * h, HEAD_DIM * (h + 1)) for h in range(N_HEADS)]
        p = _probs(q, kw, bias_ref)
        dp = jnp.stack([lax.dot_general(dov[:, hs], vw[:, hs], _DIMS["nt"], preferred_element_type=F32) for hs in heads])
        ds = p * (dp - jnp.sum(dp * p, axis=-1, keepdims=True))
        ds_ref[...] += ds
        pb, dsb = p.astype(BF16), ds.astype(BF16)
        dvs = [lax.dot_general(pb[h], dov[:, hs], _DIMS["tn"], preferred_element_type=F32) for h, hs in enumerate(heads)]
        dqs = [jnp.dot(dsb[h], kw[:, hs], preferred_element_type=F32) for h, hs in enumerate(heads)]
        dks = [lax.dot_general(dsb[h], q[:, hs], _DIMS["tn"], preferred_element_type=F32) for h, hs in enumerate(heads)]
        dq_ref[...] = (jnp.concatenate(dqs, axis=1) * (HEAD_DIM ** -0.5)).astype(BF16)
        dkacc[pl.ds(base, KW), :] += jnp.concatenate(dks, axis=1)
        dvacc[pl.ds(base, KW), :] += jnp.concatenate(dvs, axis=1)

        @pl.when(i == n - 1)
        def _():
            ck = pltpu.make_async_copy(dkacc, dk_hbm, sems.at[0])
            cv = pltpu.make_async_copy(dvacc, dv_hbm, sems.at[1])
            ck.start()
            cv.start()
            ck.wait()
            cv.wait()

    blk = pl.BlockSpec((QB, D_ATTN), lambda i: (i, 0))
    acc_shape = jax.ShapeDtypeStruct((s + KPAD, D_ATTN), F32)
    return _call(
        body, name=name, grid=(n,),
        in_specs=[blk, pl.BlockSpec(memory_space=pl.ANY), blk, _bias_spec()],
        out_specs=[blk, pl.BlockSpec(memory_space=pl.ANY), pl.BlockSpec(memory_space=pl.ANY), _full((N_HEADS, QB, KW))],
        out_shape=[jax.ShapeDtypeStruct((s, D_ATTN), BF16), acc_shape, acc_shape,
                   jax.ShapeDtypeStruct((N_HEADS, QB, KW), F32)],
        scratch_shapes=[pltpu.VMEM((s + KPAD, D_ATTN), BF16), pltpu.VMEM((s + KPAD, D_ATTN), BF16),
                        pltpu.VMEM((s + KPAD, D_ATTN), F32), pltpu.VMEM((s + KPAD, D_ATTN), F32),
                        pltpu.SemaphoreType.DMA((2,))],
        args=(qkv, qkv, do, bias), rider=rider)


CONV_HALO = 32
CONV_ROWS = 64


def _sigmoid(t):
    return 1.0 / (1.0 + jnp.exp(-t))


def _glu_rows(z_ref, r0, rows):
    a = z_ref[pl.ds(r0, rows), 0:D_CONV]
    b = z_ref[pl.ds(r0, rows), D_CONV:2 * D_CONV]
    return a, _sigmoid(b)


def _conv_fwd(zc, conv_w, conv_b, ln_g, ln_b, name):
    s = zc.shape[0]
    rt = min(256, s)

    def body(z_ref, w_ref, cb_ref, g_ref, b_ref, cv_ref, feat_ref, hpad):
        hpad[0:CONV_HALO, :] = jnp.zeros((CONV_HALO, D_CONV), F32)

        def glu(i, carry):
            r0 = pl.multiple_of(i * rt, rt)
            a, sb = _glu_rows(z_ref, r0, rt)
            hpad[pl.ds(r0 + CONV_HALO, rt), :] = a * sb
            return carry

        lax.fori_loop(0, s // rt, glu, 0)
        w = w_ref[...]

        def conv(i, carry):
            r0 = pl.multiple_of(i * CONV_ROWS, CONV_ROWS)
            win = hpad[pl.ds(r0, CONV_ROWS + CONV_HALO), :]
            acc = jnp.broadcast_to(cb_ref[...], (CONV_ROWS, D_CONV))
            for k in range(CONV_WIDTH):
                acc = acc + win[2 + k:2 + k + CONV_ROWS, :] * w[k:k + 1, :]
            cv_ref[pl.ds(r0, CONV_ROWS), :] = acc
            yhat, _ = _ln_hat(acc)
            y = yhat * g_ref[...] + b_ref[...]
            feat_ref[pl.ds(r0, CONV_ROWS), :] = (y * _sigmoid(y)).astype(BF16)
            return carry

        lax.fori_loop(0, s // CONV_ROWS, conv, 0)

    return pl.pallas_call(
        body, out_shape=[jax.ShapeDtypeStruct((s, D_CONV), F32), jax.ShapeDtypeStruct((s, D_CONV), BF16)],
        scratch_shapes=[pltpu.VMEM((s + CONV_HALO, D_CONV), F32)], name=name, compiler_params=_cparams(),
    )(zc, conv_w, conv_b, ln_g, ln_b)


def _conv_bwd(dfeat, cv, zc, conv_w, ln_g, ln_b, name):
    s = zc.shape[0]
    rt = min(256, s)

    def body(df_ref, cv_ref, z_ref, w_ref, g_ref, b_ref, dz_ref, dw_ref, dcb_ref, dg_ref, db_ref, hpad, dcvpad, dwacc):
        hpad[0:CONV_HALO, :] = jnp.zeros((CONV_HALO, D_CONV), F32)
        dcvpad[s:, :] = jnp.zeros((CONV_HALO, D_CONV), F32)
        dwacc[...] = jnp.zeros_like(dwacc)
        dcb_ref[...] = jnp.zeros_like(dcb_ref)
        dg_ref[...] = jnp.zeros_like(dg_ref)
        db_ref[...] = jnp.zeros_like(db_ref)

        def pass1(i, carry):
            r0 = pl.multiple_of(i * rt, rt)
            a, sb = _glu_rows(z_ref, r0, rt)
            hpad[pl.ds(r0 + CONV_HALO, rt), :] = a * sb
            cvhat, rstd = _ln_hat(cv_ref[pl.ds(r0, rt), :])
            y = cvhat * g_ref[...] + b_ref[...]
            sg = _sigmoid(y)
            dy = df_ref[pl.ds(r0, rt), :] * (sg * (1.0 + y * (1.0 - sg)))
            dg_ref[...] += jnp.sum(dy * cvhat, axis=0, keepdims=True)
            db_ref[...] += jnp.sum(dy, axis=0, keepdims=True)
            dcv = _ln_hat_bwd(dy * g_ref[...], cvhat, rstd)
            dcb_ref[...] += jnp.sum(dcv, axis=0, keepdims=True)
            dcvpad[pl.ds(r0, rt), :] = dcv
            return carry

        lax.fori_loop(0, s // rt, pass1, 0)
        w = w_ref[...]

        def pass2(i, carry):
            r0 = pl.multiple_of(i * CONV_ROWS, CONV_ROWS)
            dwin = dcvpad[pl.ds(r0, CONV_ROWS + CONV_HALO), :]
            hwin = hpad[pl.ds(r0, CONV_ROWS + CONV_HALO), :]
            dcv = dwin[0:CONV_ROWS, :]
            dh = jnp.zeros((CONV_ROWS, D_CONV), F32)
            for k in range(CONV_WIDTH):
                dh = dh + dwin[30 - k:30 - k + CONV_ROWS, :] * w[k:k + 1, :]
                prod = dcv * hwin[2 + k:2 + k + CONV_ROWS, :]
                dwacc[8 * k:8 * k + 8, :] += jnp.sum(prod.reshape(CONV_ROWS // 8, 8, D_CONV), axis=0)
            a, sb = _glu_rows(z_ref, r0, CONV_ROWS)
            dz_ref[pl.ds(r0, CONV_ROWS), :] = jnp.concatenate([dh * sb, dh * a * sb * (1.0 - sb)], axis=1).astype(BF16)
            return carry

        lax.fori_loop(0, s // CONV_ROWS, pass2, 0)
        dw_ref[...] = jnp.sum(dwacc[...].reshape(32, 8, D_CONV), axis=1)

    vs = jax.ShapeDtypeStruct((1, D_CONV), F32)
    return pl.pallas_call(
        body,
        out_shape=[jax.ShapeDtypeStruct((s, 2 * D_CONV), BF16), jax.ShapeDtypeStruct((32, D_CONV), F32), vs, vs, vs],
        scratch_shapes=[pltpu.VMEM((s + CONV_HALO, D_CONV), F32), pltpu.VMEM((s + CONV_HALO, D_CONV), F32),
                        pltpu.VMEM((256, D_CONV), F32)],
        name=name, compiler_params=_cparams(),
    )(dfeat, cv, zc, conv_w, ln_g, ln_b)


def _merge(zg, b_gate, ys, name):
    s = zg.shape[0]
    tm = _row_tile(s)

    def body(zg_ref, bg_ref, y0_ref, y1_ref, y2_ref, o_ref):
        acc = None
        for j, y_ref in enumerate((y0_ref, y1_ref, y2_ref)):
            cs = slice(D_MODEL * j, D_MODEL * (j + 1))
            t = _sigmoid(zg_ref[:, cs] + bg_ref[:, cs]) * y_ref[...]
            acc = t if acc is None else acc + t
        o_ref[...] = acc.astype(BF16)

    row = pl.BlockSpec((tm, D_MODEL), lambda i: (i, 0))
    return pl.pallas_call(
        body, grid=(s // tm,),
        in_specs=[pl.BlockSpec((tm, 3 * D_MODEL), lambda i: (i, 0)), _full((1, 3 * D_MODEL)), row, row, row],
        out_specs=row, out_shape=jax.ShapeDtypeStruct((s, D_MODEL), BF16), name=name, compiler_params=_cparams(),
    )(zg, b_gate, *ys)


def _merge_bwd(dm, zg, b_gate, ys, name):
    s = zg.shape[0]
    tm = min(256, s)

    def body(dm_ref, zg_ref, bg_ref, y0_ref, y1_ref, y2_ref, d0_ref, d1_ref, d2_ref, dzg_ref, dbg_ref):
        first = pl.program_id(0) == 0

        @pl.when(first)
        def _():
            dbg_ref[...] = jnp.zeros_like(dbg_ref)

        dmv = dm_ref[...]
        for j, (y_ref, d_ref) in enumerate(((y0_ref, d0_ref), (y1_ref, d1_ref), (y2_ref, d2_ref))):
            cs = slice(D_MODEL * j, D_MODEL * (j + 1))
            g = _sigmoid(zg_ref[:, cs] + bg_ref[:, cs])
            d_ref[...] = (dmv * g).astype(BF16)
            dzg = dmv * y_ref[...] * g * (1.0 - g)
            dzg_ref[:, cs] = dzg.astype(BF16)
            dbg_ref[:, cs] += jnp.sum(dzg, axis=0, keepdims=True)

    row = pl.BlockSpec((tm, D_MODEL), lambda i: (i, 0))
    wide = pl.BlockSpec((tm, 3 * D_MODEL), lambda i: (i, 0))
    yb = jax.ShapeDtypeStruct((s, D_MODEL), BF16)
    return pl.pallas_call(
        body, grid=(s // tm,),
        in_specs=[row, wide, _full((1, 3 * D_MODEL)), row, row, row],
        out_specs=[row, row, row, wide, _full((1, 3 * D_MODEL))],
        out_shape=[yb, yb, yb, jax.ShapeDtypeStruct((s, 3 * D_MODEL), BF16), jax.ShapeDtypeStruct((1, 3 * D_MODEL), F32)],
        name=name, compiler_params=_cparams(),
    )(dm, zg, b_gate, *ys)


def _ff_hidden(u2, w_ff1t, b_ff1, name, rider=None):
    s = u2.shape[0]
    tm, tn = min(1024, s), 1024

    def body(a_ref, b_ref, bias_ref, pre_ref, h_ref):
        acc = lax.dot_general(a_ref[...], b_ref[...], _DIMS["nt"], preferred_element_type=F32) + bias_ref[...]
        pre_ref[...] = acc.astype(BF16)
        h_ref[...] = _relu2(acc).astype(BF16)

    blk = pl.BlockSpec((tm, tn), lambda i, j: (i, j))
    sh = jax.ShapeDtypeStruct((s, D_FF), BF16)
    res = _call(body, name=name, grid=(s // tm, D_FF // tn),
                in_specs=[pl.BlockSpec((tm, D_MODEL), lambda i, j: (i, 0)), pl.BlockSpec((tn, D_MODEL), lambda i, j: (j, 0)),
                          pl.BlockSpec((1, tn), lambda i, j: (0, j))],
                out_specs=[blk, blk], out_shape=[sh, sh], scratch_shapes=[], args=(u2, w_ff1t, b_ff1), rider=rider)
    return tuple(res) if rider is None else (tuple(res[0]), res[1])


def _ff_hidden_bwd(dff, w_ff2, hpre, name):
    s = dff.shape[0]
    tm, tn = min(512, s), 1024

    def body(a_ref, b_ref, h_ref, o_ref, sum_ref):
        dh = lax.dot_general(a_ref[...], b_ref[...], _DIMS["nt"], preferred_element_type=F32)
        dpre = dh * (2.0 * jnp.maximum(h_ref[...].astype(F32), 0.0))
        o_ref[...] = dpre.astype(BF16)
        _acc_rows(sum_ref, dpre, pl.program_id(1) == 0)

    return pl.pallas_call(
        body, grid=(D_FF // tn, s // tm),
        in_specs=[pl.BlockSpec((tm, D_MODEL), lambda j, i: (i, 0)), pl.BlockSpec((tn, D_MODEL), lambda j, i: (j, 0)),
                  pl.BlockSpec((tm, tn), lambda j, i: (i, j))],
        out_specs=[pl.BlockSpec((tm, tn), lambda j, i: (i, j)), pl.BlockSpec((1, tn), lambda j, i: (0, j))],
        out_shape=[jax.ShapeDtypeStruct((s, D_FF), BF16), jax.ShapeDtypeStruct((1, D_FF), F32)],
        name=name, compiler_params=_cparams(),
    )(dff, w_ff2, hpre)


def _silu(t):
    return t * _sigmoid(t)


def _mod_fwd(c_all, w_ada_sh, b_ada_sh, name):
    cols = w_ada_sh.shape[2]

    def body(c_ref, w_ref, b_ref, o_ref):
        ca = _silu(c_ref[...]).astype(BF16)
        o_ref[0] = jnp.dot(ca, w_ref[0].astype(BF16), preferred_element_type=F32) + b_ref[0]

    return pl.pallas_call(
        body, grid=(DEPTH,),
        in_specs=[_full((N_DEV, D_MODEL)), pl.BlockSpec((1, D_MODEL, cols), lambda l: (l, 0, 0)),
                  pl.BlockSpec((1, 1, cols), lambda l: (l, 0, 0))],
        out_specs=pl.BlockSpec((1, N_DEV, cols), lambda l: (l, 0, 0)),
        out_shape=jax.ShapeDtypeStruct((DEPTH, N_DEV, cols), F32), name=name, compiler_params=_cparams(),
    )(c_all, w_ada_sh, b_ada_sh)


def _mod_bwd(c_all, dmod_sh, name):
    cols = dmod_sh.shape[2]

    def body(c_ref, d_ref, o_ref):
        ca = _silu(c_ref[...])
        o_ref[0] = lax.dot_general(ca, d_ref[0], _DIMS["tn"], precision=lax.Precision.HIGHEST,
                                   preferred_element_type=F32)

    return pl.pallas_call(
        body, grid=(DEPTH,),
        in_specs=[_full((N_DEV, D_MODEL)), pl.BlockSpec((1, N_DEV, cols), lambda l: (l, 0, 0))],
        out_specs=pl.BlockSpec((1, D_MODEL, cols), lambda l: (l, 0, 0)),
        out_shape=jax.ShapeDtypeStruct((DEPTH, D_MODEL, cols), F32), name=name, compiler_params=_cparams(),
    )(c_all, dmod_sh)


def _flat_tiles(rows, cols, itemsize_total):
    budget = 12 * 1024 * 1024
    tr = rows
    while tr % 32 == 0 and tr * cols * itemsize_total > budget:
        tr //= 2
    return tr


def _sum_cores(dw, recv, place, name):
    _, m, n = dw.shape
    tr = _flat_tiles(m, n, 6)

    def body(place_ref, a_ref, b_ref, o_ref):
        o_ref[...] = (a_ref[...].astype(F32) + b_ref[...].astype(F32)).astype(BF16)

    grid_spec = pltpu.PrefetchScalarGridSpec(
        num_scalar_prefetch=1, grid=(m // tr,),
        in_specs=[pl.BlockSpec((None, tr, n), lambda i, pr: (pr[0], i, 0)), pl.BlockSpec((tr, n), lambda i, pr: (i, 0))],
        out_specs=pl.BlockSpec((tr, n), lambda i, pr: (i, 0)))
    return pl.pallas_call(body, grid_spec=grid_spec, out_shape=jax.ShapeDtypeStruct((m, n), BF16), name=name,
                          compiler_params=_cparams())(place, dw, recv)


def _sum_chips(h, r, place, name):
    _, rs, n = h.shape
    tr = _flat_tiles(rs, n, 12)

    def body(place_ref, h_ref, r_ref, o_ref):
        o_ref[...] = ((h_ref[...].astype(F32) + r_ref[0].astype(F32)) + r_ref[1].astype(F32)) + r_ref[2].astype(F32)

    grid_spec = pltpu.PrefetchScalarGridSpec(
        num_scalar_prefetch=1, grid=(rs // tr,),
        in_specs=[pl.BlockSpec((None, tr, n), lambda i, pr: (pr[1], i, 0)), pl.BlockSpec((3, tr, n), lambda i, pr: (0, i, 0))],
        out_specs=pl.BlockSpec((tr, n), lambda i, pr: (i, 0)))
    return pl.pallas_call(body, grid_spec=grid_spec, out_shape=jax.ShapeDtypeStruct((rs, n), F32), name=name,
                          compiler_params=_cparams())(place, h, r)


def _adam_math(w, g, m, v):
    m2 = ADAM_B1 * m + (1.0 - ADAM_B1) * g
    v2 = ADAM_B2 * v + (1.0 - ADAM_B2) * (g * g)
    m_hat = m2 / (1.0 - ADAM_B1 ** ADAM_STEP)
    v_hat = v2 / (1.0 - ADAM_B2 ** ADAM_STEP)
    delta = -ADAM_LR * (m_hat / (jnp.sqrt(v_hat) + ADAM_EPS) + ADAM_WD * w)
    return delta, m2, v2


def _adamw(w, m, v, grads, name):
    r, c = w.shape
    tr = _flat_tiles(r, c, 4 * (7 + len(grads)))

    def body(*refs):
        w_ref, m_ref, v_ref = refs[:3]
        g_refs = refs[3:3 + len(grads)]
        g_ref, d_ref, m2_ref, v2_ref = refs[3 + len(grads):]
        g = g_refs[0][...]
        for gr in g_refs[1:]:
            g = g + gr[...]
        delta, m2, v2 = _adam_math(w_ref[...], g, m_ref[...], v_ref[...])
        g_ref[...] = g
        d_ref[...] = delta
        m2_ref[...] = m2
        v2_ref[...] = v2

    blk = pl.BlockSpec((tr, c), lambda i: (i, 0))
    sh = jax.ShapeDtypeStruct((r, c), F32)
    return pl.pallas_call(body, grid=(r // tr,), in_specs=[blk] * (3 + len(grads)), out_specs=[blk] * 4,
                          out_shape=[sh] * 4, name=name, compiler_params=_cparams())(w, m, v, *grads)


def _adamw_halves(w, m, v, own, other, place, split, name):
    nl, r, c = w.shape
    hr, hc = own[0].shape
    tr = _flat_tiles(hr, hc, 4 * (7 + 2 * nl))
    nt = hr // tr
    if split == "rows":
        w_spec = pl.BlockSpec((None, tr, c), lambda l, h, t, pr: (l, h * nt + t, 0))
    else:
        w_spec = pl.BlockSpec((None, tr, hc), lambda l, h, t, pr: (l, t, h))

    def g_spec(layer, mine):
        return pl.BlockSpec((tr, hc), lambda l, h, t, pr: (jnp.where((l == layer) & ((h == pr[0]) == mine), t, nt - 1), 0))

    def body(place_ref, w_ref, m_ref, v_ref, *refs):
        own_refs, other_refs = refs[:nl], refs[nl:2 * nl]
        g_ref, d_ref, m2_ref, v2_ref = refs[2 * nl:]
        layer = pl.program_id(0)
        mine = pl.program_id(1) == place_ref[0]
        g = None
        for li in range(nl):
            cand = jnp.where(mine, own_refs[li][...], other_refs[li][...])
            g = cand if g is None else jnp.where(layer == li, cand, g)
        delta, m2, v2 = _adam_math(w_ref[...], g, m_ref[...], v_ref[...])
        g_ref[...] = g
        d_ref[...] = delta
        m2_ref[...] = m2
        v2_ref[...] = v2

    sh = jax.ShapeDtypeStruct((nl, r, c), F32)
    g_specs = [g_spec(li, True) for li in range(nl)] + [g_spec(li, False) for li in range(nl)]
    return _call(body, name=name, grid=(nl, 2, nt), in_specs=[w_spec] * 3 + g_specs, out_specs=[w_spec] * 4,
                 out_shape=[sh] * 4, scratch_shapes=[], args=(w, m, v, *own, *other), prefetch=(place,))


def _adamw_small(w, m, v, g_all, name):
    r, c = w.shape

    def body(w_ref, m_ref, v_ref, g_ref, go_ref, d_ref, m2_ref, v2_ref):
        g = g_ref[0]
        for b in range(1, N_DEV):
            g = g + g_ref[b]
        delta, m2, v2 = _adam_math(w_ref[...], g, m_ref[...], v_ref[...])
        go_ref[...] = g
        d_ref[...] = delta
        m2_ref[...] = m2
        v2_ref[...] = v2

    sh = jax.ShapeDtypeStruct((r, c), F32)
    return pl.pallas_call(body, out_shape=[sh] * 4, name=name, compiler_params=_cparams())(w, m, v, g_all)


def _me():
    return lax.axis_index("x"), lax.axis_index("y"), lax.axis_index("c")


def _flip(v, bit):
    return 1 - v if bit else v


def _allgather_small(blk, name):
    r, c = blk.shape

    def body(x_ref, o_ref, send_sems, recv_sems):
        x, y, cc = _me()
        me = 4 * x + 2 * y + cc
        copies = []
        for k in range(1, N_DEV):
            peer = (_flip(x, k & 4), _flip(y, k & 2), _flip(cc, k & 1))
            cp = pltpu.make_async_remote_copy(src_ref=x_ref, dst_ref=o_ref.at[me], send_sem=send_sems.at[k - 1],
                                              recv_sem=recv_sems.at[k - 1], device_id=peer, device_id_type=MESH)
            cp.start()
            copies.append(cp)
        o_ref[me] = x_ref[...]
        for cp in copies:
            cp.wait()

    return pl.pallas_call(
        body, out_shape=jax.ShapeDtypeStruct((N_DEV, r, c), F32),
        in_specs=[pl.BlockSpec(memory_space=pltpu.VMEM)], out_specs=pl.BlockSpec(memory_space=pltpu.VMEM),
        scratch_shapes=[pltpu.SemaphoreType.DMA((N_DEV - 1,)), pltpu.SemaphoreType.DMA((N_DEV - 1,))],
        name=name, compiler_params=_cparams(),
    )(blk)


class _Rider:
    def __init__(self, arrays, out_shapes, scratch_shapes, start, finish):
        self.arrays, self.out_shapes, self.scratch_shapes = list(arrays), list(out_shapes), list(scratch_shapes)
        self.start, self.finish = start, finish


def _call(body, *, name, grid, in_specs, out_specs, out_shape, scratch_shapes, args, rider=None, prefetch=()):
    npf = len(prefetch)

    def launch(fn, in_specs, out_specs, out_shape, scratch_shapes, args):
        grid_spec = pltpu.PrefetchScalarGridSpec(num_scalar_prefetch=npf, grid=grid, in_specs=in_specs,
                                                 out_specs=out_specs, scratch_shapes=scratch_shapes)
        return pl.pallas_call(fn, grid_spec=grid_spec, out_shape=out_shape, name=name,
                              compiler_params=_cparams())(*prefetch, *args)

    if rider is None:
        return launch(body, list(in_specs), list(out_specs), list(out_shape), list(scratch_shapes), args)
    ni, no, ns = len(in_specs), len(out_specs), len(scratch_shapes)
    ri, ro = len(rider.arrays), len(rider.out_shapes)
    steps = int(np.prod(grid))

    def wrapped(*refs):
        pf, refs = refs[:npf], refs[npf:]
        h_in, r_in = refs[:ni], refs[ni:ni + ri]
        h_out, r_out = refs[ni + ri:ni + ri + no], refs[ni + ri + no:ni + ri + no + ro]
        h_scr, r_scr = refs[ni + ri + no + ro:ni + ri + no + ro + ns], refs[ni + ri + no + ro + ns:]
        step = pl.program_id(0)
        for d in range(1, len(grid)):
            step = step * grid[d] + pl.program_id(d)

        @pl.when(step == 0)
        def _():
            rider.start(r_in, r_out, r_scr)

        body(*pf, *h_in, *h_out, *h_scr)

        @pl.when(step == steps - 1)
        def _():
            rider.finish(r_in, r_out, r_scr)

    anyspec = pl.BlockSpec(memory_space=pl.ANY)
    res = launch(wrapped, list(in_specs) + [anyspec] * ri, list(out_specs) + [anyspec] * ro,
                 list(out_shape) + rider.out_shapes, list(scratch_shapes) + rider.scratch_shapes,
                 list(args) + rider.arrays)
    return res[:no], res[no:]


def _run_rider(rider, name):
    ri = len(rider.arrays)

    def body(*refs):
        r_in, r_out, r_scr = refs[:ri], refs[ri:ri + len(rider.out_shapes)], refs[ri + len(rider.out_shapes):]
        rider.start(r_in, r_out, r_scr)
        rider.finish(r_in, r_out, r_scr)

    anyspec = pl.BlockSpec(memory_space=pl.ANY)
    return pl.pallas_call(body, in_specs=[anyspec] * ri, out_specs=[anyspec] * len(rider.out_shapes),
                          out_shape=rider.out_shapes, scratch_shapes=rider.scratch_shapes, name=name,
                          compiler_params=_cparams())(*rider.arrays)


def _allgather_rider(blk):
    def copies(ins, outs, scr):
        send_sems, recv_sems, loc_sems, stage = scr
        x, y, cc = _me()
        me = 4 * x + 2 * y + cc
        remote = [pltpu.make_async_remote_copy(
            src_ref=ins[0], dst_ref=outs[0].at[me], send_sem=send_sems.at[k - 1], recv_sem=recv_sems.at[k - 1],
            device_id=(_flip(x, k & 4), _flip(y, k & 2), _flip(cc, k & 1)), device_id_type=MESH) for k in range(1, N_DEV)]
        return remote, pltpu.make_async_copy(ins[0], stage, loc_sems.at[0]), (outs[0].at[me], stage, loc_sems.at[1])

    def start(ins, outs, scr):
        remote, lin, _ = copies(ins, outs, scr)
        lin.start()
        for cp in remote:
            cp.start()

    def finish(ins, outs, scr):
        remote, lin, (dst, stage, sem) = copies(ins, outs, scr)
        lin.wait()
        lout = pltpu.make_async_copy(stage, dst, sem)
        lout.start()
        for cp in remote:
            cp.wait()
        lout.wait()

    return _Rider([blk], [jax.ShapeDtypeStruct((N_DEV,) + blk.shape, blk.dtype)],
                  [pltpu.SemaphoreType.DMA((N_DEV - 1,)), pltpu.SemaphoreType.DMA((N_DEV - 1,)),
                   pltpu.SemaphoreType.DMA((2,)), pltpu.VMEM(blk.shape, blk.dtype)], start, finish)


def _gather_rider(shards):
    n = len(shards)

    def copies(ins, outs, scr, relay=True):
        ici_send, ici_recv, d2d_send, d2d_recv, loc_sems = scr[:5]
        stage = scr[5:]
        x, y, cc = _me()
        chip = 2 * x + y
        sibling = (x, y, 1 - cc)
        local, sends, relays = [], [], []
        for j in range(n):
            def rows(ch, h, j=j):
                return outs[j].at[ch, h]

            lc = pltpu.make_async_copy(ins[j], stage[j], loc_sems.at[j])
            local.append((lc, pltpu.make_async_copy(stage[j], outs[j].at[chip], loc_sems.at[n + j]) if relay else None))
            for k in range(1, N_CHIP):
                px, py = _flip(x, k & 2), _flip(y, k & 1)
                pchip = 2 * px + py
                q = 3 * j + k - 1
                out_cp = pltpu.make_async_remote_copy(src_ref=ins[j].at[cc], dst_ref=rows(chip, cc),
                                                      send_sem=ici_send.at[q], recv_sem=ici_recv.at[q],
                                                      device_id=(px, py, cc), device_id_type=MESH)
                sends.append(out_cp)
                if not relay:
                    continue
                arrival = pltpu.make_async_remote_copy(src_ref=rows(pchip, cc), dst_ref=rows(pchip, cc),
                                                       send_sem=ici_send.at[q], recv_sem=ici_recv.at[q],
                                                       device_id=(px, py, cc), device_id_type=MESH)
                forward = pltpu.make_async_remote_copy(src_ref=rows(pchip, cc), dst_ref=rows(pchip, cc),
                                                       send_sem=d2d_send.at[q], recv_sem=d2d_recv.at[q],
                                                       device_id=sibling, device_id_type=MESH)
                from_sibling = pltpu.make_async_remote_copy(src_ref=rows(pchip, 1 - cc), dst_ref=rows(pchip, 1 - cc),
                                                            send_sem=d2d_send.at[q], recv_sem=d2d_recv.at[q],
                                                            device_id=sibling, device_id_type=MESH)
                relays.append((arrival, forward, from_sibling))
        return local, sends, relays

    def start(ins, outs, scr):
        local, sends, _ = copies(ins, outs, scr, relay=False)
        for lin, _ in local:
            lin.start()
        for cp in sends:
            cp.start()

    def finish(ins, outs, scr):
        local, sends, relays = copies(ins, outs, scr)
        for lin, lout in local:
            lin.wait()
            lout.start()
        for arrival, forward, _ in relays:
            arrival.wait_recv()
            forward.start()
        for cp in sends:
            cp.wait_send()
        for _, forward, from_sibling in relays:
            forward.wait_send()
            from_sibling.wait_recv()
        for _, lout in local:
            lout.wait()

    scratch = [pltpu.SemaphoreType.DMA((3 * n,)), pltpu.SemaphoreType.DMA((3 * n,)), pltpu.SemaphoreType.DMA((3 * n,)),
               pltpu.SemaphoreType.DMA((3 * n,)), pltpu.SemaphoreType.DMA((2 * n,))]
    scratch += [pltpu.VMEM(a.shape, a.dtype) for a in shards]
    return _Rider(shards, [jax.ShapeDtypeStruct((N_CHIP,) + a.shape, a.dtype) for a in shards], scratch, start, finish)


def _sibling_rider(arrs, other_half=False):
    n = len(arrs)

    def copies(ins, outs, scr):
        send_sems, recv_sems = scr
        x, y, cc = _me()
        return [pltpu.make_async_remote_copy(
            src_ref=ins[j].at[1 - cc] if other_half else ins[j], dst_ref=outs[j], send_sem=send_sems.at[j],
            recv_sem=recv_sems.at[j], device_id=(x, y, 1 - cc), device_id_type=MESH) for j in range(n)]

    def start(ins, outs, scr):
        for cp in copies(ins, outs, scr):
            cp.start()

    def finish(ins, outs, scr):
        for cp in copies(ins, outs, scr):
            cp.wait()

    return _Rider(arrs, [jax.ShapeDtypeStruct(a.shape[1:] if other_half else a.shape, a.dtype) for a in arrs],
                  [pltpu.SemaphoreType.DMA((n,)), pltpu.SemaphoreType.DMA((n,))], start, finish)


def _sibling_send(arrs, name, other_half=False):
    return _run_rider(_sibling_rider(arrs, other_half), name)


def _join_riders(first, second):
    ni, no, ns = len(first.arrays), len(first.out_shapes), len(first.scratch_shapes)

    def split(ins, outs, scr):
        return (ins[:ni], outs[:no], scr[:ns]), (ins[ni:], outs[no:], scr[ns:])

    def start(ins, outs, scr):
        a, b = split(ins, outs, scr)
        first.start(*a)
        second.start(*b)

    def finish(ins, outs, scr):
        a, b = split(ins, outs, scr)
        first.finish(*a)
        second.finish(*b)

    return _Rider(first.arrays + second.arrays, first.out_shapes + second.out_shapes,
                  first.scratch_shapes + second.scratch_shapes, start, finish)


def _scatter_rider(arrs):
    n = len(arrs)

    def copies(ins, outs, scr):
        send_sems, recv_sems = scr
        x, y, cc = _me()
        cps = []
        for j in range(n):
            for k in range(1, N_CHIP):
                px, py = _flip(x, k & 2), _flip(y, k & 1)
                cps.append(pltpu.make_async_remote_copy(
                    src_ref=ins[j].at[2 * px + py], dst_ref=outs[j].at[k - 1], send_sem=send_sems.at[3 * j + k - 1],
                    recv_sem=recv_sems.at[3 * j + k - 1], device_id=(px, py, cc), device_id_type=MESH))
        return cps

    def start(ins, outs, scr):
        for cp in copies(ins, outs, scr):
            cp.start()

    def finish(ins, outs, scr):
        for cp in copies(ins, outs, scr):
            cp.wait()

    return _Rider(arrs, [jax.ShapeDtypeStruct((N_CHIP - 1,) + a.shape[1:], a.dtype) for a in arrs],
                  [pltpu.SemaphoreType.DMA((3 * n,)), pltpu.SemaphoreType.DMA((3 * n,))], start, finish)


COL_SHARDED = ("w_in", "w_br_pool", "w_br_attn", "w_br_conv", "w_ff1")
ROW_SHARDED = ("w_o", "w_ff2")
BIG = COL_SHARDED + ROW_SHARDED
SMALL = ("b_ada", "b_gate", "w_pool", "pool_scale", "rel_bias", "conv_w", "conv_b", "conv_ln_g", "conv_ln_b",
         "ln_mix_g", "ln_mix_b", "b_ff1", "b_ff2", "ln_ff_g", "ln_ff_b")
PACK_W = 1024


def _pack(parts):
    rows = []
    for a in parts:
        flat = a.reshape(-1)
        n = -(-flat.shape[0] // PACK_W) * PACK_W
        rows.append(jnp.pad(flat, (0, n - flat.shape[0])).reshape(-1, PACK_W))
    out = jnp.concatenate(rows, axis=0)
    r = -(-out.shape[0] // 8) * 8
    return jnp.pad(out, ((0, r - out.shape[0]), (0, 0)))


def _unpack(packed, shapes):
    out, r0 = [], 0
    for shp in shapes:
        size = int(np.prod(shp))
        nr = -(-size // PACK_W)
        out.append(packed[r0:r0 + nr].reshape(-1)[:size].reshape(shp))
        r0 += nr
    return out


def _hosted(fn, hook, *args, **kw):
    if hook is None:
        return fn(*args, **kw)
    res, rider_out = fn(*args, rider=hook[0], **kw)
    hook[1](rider_out)
    return res


def _layer_fwd(l, x, mod, W, P, hooks=None):
    hooks = hooks or {}
    s = x.shape[0]
    sh_m, sc_m, g_m, sh_f, sc_f, g_f = [mod[l:l + 1, D_MODEL * j:D_MODEL * (j + 1)] for j in range(6)]
    n = lambda t: f"{t}{l}"
    w_in = W["w_in"][l]
    u = _ln_mod(x, sc_m, sh_m, n("ln_mod_mix"))
    tmz = min(1024, s)
    zp = _mm(u, w_in, "nt", tm=min(2048, s), tn=256, out_dtype=F32, name=n("z_pool"), b_col0=0, n_out=D_POOL)
    qkv = _mm(u, w_in, "nt", tm=tmz, tn=256, out_dtype=BF16, name=n("z_qkv"), b_col0=OFF_QKV // 256, n_out=3 * D_ATTN)
    zc = _mm(u, w_in, "nt", tm=tmz, tn=256, out_dtype=F32, name=n("z_conv"), b_col0=OFF_CONV // 256, n_out=2 * D_CONV)
    zg = _mm(u, w_in, "nt", tm=tmz, tn=768, out_dtype=BF16, name=n("z_gate"), b_col0=OFF_GATE // 768, n_out=3 * D_MODEL)

    p, feat_pool = _pool_fwd(zp, P["wp_bd"][l], P["pool_scale"][l], n("pool_fwd"))
    bias = _bias_block(P["rel_bias"][l], n("bias_block"))
    o = _hosted(_attn_fwd, hooks.get("attn"), qkv, bias, n("attn_fwd"))
    cv, feat_conv = _conv_fwd(zc, P["conv_w"][l], P["conv_b"][l], P["conv_ln_g"][l], P["conv_ln_b"][l], n("conv_fwd"))

    tmb = min(1024, s)
    y_pool = _mm(feat_pool, W["w_br_pool"][l], "nt", tm=tmb, tn=1024, out_dtype=F32, name=n("y_pool"))
    y_attn = _mm(o, W["w_br_attn"][l], "nt", tm=tmb, tn=1024, out_dtype=F32, name=n("y_attn"))
    y_conv = _mm(feat_conv, W["w_br_conv"][l], "nt", tm=tmb, tn=1024, out_dtype=F32, name=n("y_conv"))
    ys = (y_pool, y_attn, y_conv)
    merged = _merge(zg, P["b_gate"][l], ys, n("merge"))
    mix, x1 = _mm_resid_ln(merged, W["w_o"][l], None, x, g_m, P["ln_mix_g"][l], P["ln_mix_b"][l], n("mix_out"))

    u2 = _ln_mod(x1, sc_f, sh_f, n("ln_mod_ff"))
    hpre, hid = _hosted(_ff_hidden, hooks.get("ff1"), u2, W["w_ff1"][l], P["b_ff1"][l], n("ff1"))
    ff, x2 = _hosted(_mm_resid_ln, hooks.get("ff2"), hid, W["w_ff2"][l], P["b_ff2"][l], x1, g_f, P["ln_ff_g"][l],
                     P["ln_ff_b"][l], n("ff2"))
    saved = dict(x=x, u=u, zp=zp, qkv=qkv, zc=zc, zg=zg, p=p, feat_pool=feat_pool, bias=bias, o=o, cv=cv,
                 feat_conv=feat_conv, ys=ys, merged=merged, mix=mix, x1=x1, u2=u2, hpre=hpre, hid=hid, ff=ff)
    return x2, saved


def _layer_bwd(l, dx2, mod, W, P, A, hooks=None):
    hooks = hooks or {}
    s = dx2.shape[0]
    sh_m, sc_m, g_m, sh_f, sc_f, g_f = [mod[l:l + 1, D_MODEL * j:D_MODEL * (j + 1)] for j in range(6)]
    n = lambda t: f"{t}{l}"
    tmb = min(1024, s)
    gw, gs = {}, {}

    dres, dff, gs["ln_ff_g"], gs["ln_ff_b"], dg_f, gs["b_ff2"] = _resid_ln_bwd(
        dx2, A["x1"], A["ff"], g_f, P["ln_ff_g"][l], n("resid_ln_ff_bwd"))
    gw["w_ff2"] = _mm(A["hid"], dff, "tn", tm=512, tn=1024, out_dtype=BF16, name=n("dw_ff2"), split_n=512)
    dhpre, gs["b_ff1"] = _ff_hidden_bwd(dff, W["w_ff2"][l], A["hpre"], n("ff_hidden_bwd"))
    gw["w_ff1"] = _mm(dhpre, A["u2"], "tn", tm=512, tn=1024, out_dtype=BF16, name=n("dw_ff1"), split_n=512)
    dx1, dsc_f, dsh_f = _mm_ln_mod_bwd(dhpre, W["w_ff1"][l], A["x1"], sc_f, dres, n("du_ff"))

    dres, dmix, gs["ln_mix_g"], gs["ln_mix_b"], dg_m, _ = _resid_ln_bwd(
        dx1, A["x"], A["mix"], g_m, P["ln_mix_g"][l], n("resid_ln_mix_bwd"))
    gw["w_o"] = _mm(A["merged"], dmix, "tn", tm=512, tn=1024, out_dtype=BF16, name=n("dw_o"), split_n=512)
    dmerged = _mm(dmix, W["w_o"][l], "nt", tm=tmb, tn=1024, out_dtype=F32, name=n("d_merged"))
    dy_pool, dy_attn, dy_conv, dzg, gs["b_gate"] = _merge_bwd(dmerged, A["zg"], P["b_gate"][l], A["ys"], n("merge_bwd"))

    gw["w_br_pool"] = _mm(dy_pool, A["feat_pool"], "tn", tm=512, tn=256, out_dtype=BF16, name=n("dw_br_pool"),
                          split_n=128)
    gw["w_br_attn"] = _mm(dy_attn, A["o"], "tn", tm=512, tn=512, out_dtype=BF16, name=n("dw_br_attn"), split_n=256)
    gw["w_br_conv"] = _mm(dy_conv, A["feat_conv"], "tn", tm=512, tn=256, out_dtype=BF16, name=n("dw_br_conv"),
                          split_n=128)
    dfeat_pool = _mm(dy_pool, W["w_br_pool"][l], "nn", tm=tmb, tn=256, out_dtype=F32, name=n("d_feat_pool"))
    do = _mm(dy_attn, W["w_br_attn"][l], "nn", tm=tmb, tn=512, out_dtype=BF16, name=n("d_attn_out"))
    dfeat_conv = _mm(dy_conv, W["w_br_conv"][l], "nn", tm=tmb, tn=256, out_dtype=F32, name=n("d_feat_conv"))

    dzp, dwp_bd, gs["pool_scale"] = _pool_bwd(dfeat_pool, A["p"], P["wp_bd"][l], P["pool_scale"][l], n("pool_bwd"))
    gs["w_pool"] = jnp.stack([dwp_bd[POOL_GROUP * g:POOL_GROUP * (g + 1), POOL_GROUP * g:POOL_GROUP * (g + 1)]
                              for g in range(len(POOL_WINDOWS))])
    hook = hooks["attn"](gw) if "attn" in hooks else None
    dq, dk, dv, ds_acc = _hosted(_attn_bwd, hook, A["qkv"], do, A["bias"], n("attn_bwd"))
    gs["rel_bias"] = _bias_block_bwd(ds_acc, n("bias_block_bwd"))
    dzc, dcw, gs["conv_b"], gs["conv_ln_g"], gs["conv_ln_b"] = _conv_bwd(
        dfeat_conv, A["cv"], A["zc"], P["conv_w"][l], P["conv_ln_g"][l], P["conv_ln_b"][l], n("conv_bwd"))
    gs["conv_w"] = dcw[:CONV_WIDTH]

    dz = jnp.concatenate([dzp, dq, dk[KPAD:].astype(BF16), dv[KPAD:].astype(BF16), dzc, dzg], axis=1)
    gw["w_in"] = _mm(dz, A["u"], "tn", tm=768, tn=1024, out_dtype=BF16, name=n("dw_in"), split_n=512)
    hook = hooks["du_mix"](gw) if "du_mix" in hooks else None
    dx, dsc_m, dsh_m = _hosted(_mm_ln_mod_bwd, hook, dz, W["w_in"][l], A["x"], sc_m, dres, n("du_mix"))
    dmod = jnp.concatenate([dsh_m, dsc_m, dg_m, dsh_f, dsc_f, dg_f], axis=1)
    return dx, gw, gs, dmod


def _small_shapes():
    return {"b_ada": (6 * D_MODEL,), "b_gate": (3 * D_MODEL,), "w_pool": (4, POOL_GROUP, POOL_GROUP),
            "pool_scale": (D_POOL,), "rel_bias": (N_HEADS, N_REL), "conv_w": (CONV_WIDTH, D_CONV),
            "conv_b": (D_CONV,), "conv_ln_g": (D_CONV,), "conv_ln_b": (D_CONV,), "ln_mix_g": (D_MODEL,),
            "ln_mix_b": (D_MODEL,), "b_ff1": (D_FF,), "b_ff2": (D_MODEL,), "ln_ff_g": (D_MODEL,), "ln_ff_b": (D_MODEL,)}


def kernel(x, c, w_ada, b_ada, w_in, b_gate, w_pool, pool_scale, rel_bias, conv_w, conv_b, conv_ln_g, conv_ln_b, w_br_pool, w_br_attn, w_br_conv, w_o, ln_mix_g, ln_mix_b, w_ff1, b_ff1, w_ff2, b_ff2, ln_ff_g, ln_ff_b, loss_target, m_w_ada, m_b_ada, m_w_in, m_b_gate, m_w_pool, m_pool_scale, m_rel_bias, m_conv_w, m_conv_b, m_conv_ln_g, m_conv_ln_b, m_w_br_pool, m_w_br_attn, m_w_br_conv, m_w_o, m_ln_mix_g, m_ln_mix_b, m_w_ff1, m_b_ff1, m_w_ff2, m_b_ff2, m_ln_ff_g, m_ln_ff_b, v_w_ada, v_b_ada, v_w_in, v_b_gate, v_w_pool, v_pool_scale, v_rel_bias, v_conv_w, v_conv_b, v_conv_ln_g, v_conv_ln_b, v_w_br_pool, v_w_br_attn, v_w_br_conv, v_w_o, v_ln_mix_g, v_ln_mix_b, v_w_ff1, v_b_ff1, v_w_ff2, v_b_ff2, v_ln_ff_g, v_ln_ff_b):
    env = dict(locals())
    xi, yi, ci = _me()
    chip = 2 * xi + yi
    me = 4 * xi + 2 * yi + ci
    xs = x[0]
    tgt = loss_target[0]
    L = DEPTH

    c_all = _allgather_small(c.reshape(8, 128), "gather_c").reshape(N_DEV, D_MODEL)
    ada_cols = w_ada.shape[2]
    b_ada_sh = lax.dynamic_slice_in_dim(b_ada, chip * ada_cols, ada_cols, axis=1).reshape(L, 1, ada_cols)
    mod_part = _mod_fwd(c_all, w_ada, b_ada_sh, "mod_fwd")
    mod_g = _allgather_small(mod_part.reshape(-1, 128), "gather_mod").reshape(N_CHIP, 2, L, N_DEV, ada_cols)[:, 0]
    mod_all = jnp.transpose(mod_g, (1, 2, 0, 3)).reshape(L, N_DEV, 6 * D_MODEL)
    mod = lax.dynamic_index_in_dim(mod_all, me, axis=1, keepdims=False)

    W = {k: [None] * L for k in BIG}

    def weight_gather(names, l):
        shards = [(jnp.swapaxes(env[k][l], 0, 1) if k in COL_SHARDED else env[k][l]).astype(BF16) for k in names]
        shards = [a.reshape(2, a.shape[0] // 2, a.shape[1]) for a in shards]

        def done(outs):
            for k, g in zip(names, outs):
                W[k][l] = g.reshape(-1, g.shape[-1])

        return _gather_rider(shards), done

    first_names = ("w_in", "w_br_pool", "w_br_attn", "w_br_conv", "w_o")
    late_names = ("w_ff1", "w_ff2")
    rider, done = weight_gather(first_names, 0)
    done(_run_rider(rider, "gather_weights_first0"))
    fwd_hooks = [{"attn": weight_gather(late_names, 0), "ff1": weight_gather(("w_in",), 1),
                  "ff2": weight_gather(("w_br_pool", "w_br_attn", "w_br_conv", "w_o"), 1)},
                 {"attn": weight_gather(late_names, 1)}]

    P = {k: env[k] for k in ("rel_bias", "conv_w")}
    for k in ("b_gate", "pool_scale", "conv_b", "conv_ln_g", "conv_ln_b", "ln_mix_g", "ln_mix_b", "b_ff1", "b_ff2",
              "ln_ff_g", "ln_ff_b"):
        P[k] = env[k].reshape(L, 1, -1)
    conv_w_full = _allgather_small(_pack([conv_w]), "gather_conv_w")
    n_cw = conv_w.size
    cw = conv_w_full.reshape(N_CHIP, 2, -1)[:, 0, :n_cw].reshape(N_CHIP, L, CONV_WIDTH, D_CONV // N_CHIP)
    P["conv_w"] = jnp.transpose(cw, (1, 2, 0, 3)).reshape(L, CONV_WIDTH, D_CONV)
    wp_bd = jnp.zeros((L, D_POOL, D_POOL), F32)
    for g in range(len(POOL_WINDOWS)):
        sl = slice(POOL_GROUP * g, POOL_GROUP * (g + 1))
        wp_bd = wp_bd.at[:, sl, sl].set(w_pool[:, g])
    P["wp_bd"] = wp_bd.astype(BF16)

    acts = []
    h = xs
    for l in range(L):
        h, saved = _layer_fwd(l, h, mod, W, P, fwd_hooks[l])
        acts.append(saved)
    dy, loss_part = _loss_grad(h, tgt, "loss_grad")
    loss = lax.psum(loss_part[0, 0], ("x", "y", "c"))

    place = jnp.stack([ci, chip, chip ^ 1, chip ^ 2, chip ^ 3]).astype(jnp.int32)
    scattered = {}

    def grad_scatter(items, tag):
        dws = [dw for _, _, dw in items]
        got = _sibling_send(dws, f"swap_blocks_{tag}", other_half=True)
        both = [_sum_cores(a, b, place, f"sum_cores_{k}{l}") for (k, l, _), a, b in zip(items, dws, got)]
        both = [hh.reshape(N_CHIP, -1, hh.shape[-1]) for hh in both]

        def done(outs):
            for (k, l, _), hh, r in zip(items, both, outs):
                scattered[(k, l)] = (hh, r)

        return _scatter_rider(both), done

    early = ("w_ff2", "w_ff1", "w_o", "w_br_pool", "w_br_attn", "w_br_conv")
    left_over = []

    def attn_hook(l):
        def hook(gw):
            items = left_over + [(k, l, gw[k]) for k in early]
            left_over.clear()
            return grad_scatter(items, f"attn{l}")
        return hook

    def last_hook(gw):
        return grad_scatter([("w_in", 0, gw["w_in"])], "last")

    gws, gss, dmods = [None] * L, [None] * L, [None] * L
    dh = dy
    for l in reversed(range(L)):
        hooks = {"attn": attn_hook(l)}
        if l == 0:
            hooks["du_mix"] = last_hook
        dh, gws[l], gss[l], dmods[l] = _layer_bwd(l, dh, mod, W, P, acts[l], hooks)
        if l > 0:
            left_over.append(("w_in", l, gws[l]["w_in"]))
    grad_x = dh[None]

    reduced = [[_sum_chips(*scattered[(k, l)], place, f"sum_chips_{k}{l}") for l in range(L)] for k in BIG]
    flat_reduced = [t for per_weight in reduced for t in per_weight]

    shapes = _small_shapes()
    small_names = [k for k in SMALL if k != "b_ada"]
    dmod_own = jnp.concatenate(dmods, axis=0)
    pack = _pack([dmod_own] + [jnp.stack([gss[l][k].reshape(shapes[k]) for l in range(L)]) for k in small_names])
    last = _run_rider(_join_riders(_sibling_rider(flat_reduced), _allgather_rider(pack.reshape(-1, 128))),
                      "swap_reduced_gather_small")
    flat_other, g_all = last[:-1], last[-1].reshape(N_DEV, -1, PACK_W)

    out = {}
    for j, k in enumerate(BIG):
        own, other = reduced[j], flat_other[L * j:L * (j + 1)]
        if k == "w_in":
            t = lambda a: jnp.swapaxes(a, 1, 2)
            res = _adamw_halves(t(env[k]), t(env["m_" + k]), t(env["v_" + k]), own, other, place, "cols", f"adamw_{k}")
            res = [t(a) for a in res]
        else:
            if k in COL_SHARDED:
                own, other = [a.T for a in own], [a.T for a in other]
            res = _adamw_halves(env[k], env["m_" + k], env["v_" + k], own, other, place,
                                "rows" if k in COL_SHARDED else "cols", f"adamw_{k}")
        out[k] = tuple(res)

    dmod_all = g_all[:, :L * 6].reshape(N_DEV, L, 6 * D_MODEL)
    dmod_sh = jnp.transpose(lax.dynamic_slice_in_dim(dmod_all, chip * ada_cols, ada_cols, axis=2), (1, 0, 2))
    g_ada = _mod_bwd(c_all, dmod_sh, "mod_bwd")
    g_, d_, m_, v_ = _adamw(w_ada.reshape(-1, ada_cols), m_w_ada.reshape(-1, ada_cols), v_w_ada.reshape(-1, ada_cols),
                            [g_ada.reshape(-1, ada_cols)], "adamw_w_ada")
    out["w_ada"] = tuple(a.reshape(w_ada.shape) for a in (g_, d_, m_, v_))

    def small_pack(prefix):
        parts = [env[prefix + "b_ada"]]
        for k in small_names:
            a = env[prefix + k]
            if k == "conv_w":
                a = jnp.zeros((L,) + shapes[k], F32)
            parts.append(a)
        return _pack(parts)

    gp, dp, mp, vp = _adamw_small(small_pack(""), small_pack("m_"), small_pack("v_"), g_all, "adamw_small")
    full_shapes = [(L,) + shapes["b_ada"]] + [(L,) + shapes[k] for k in small_names]
    for tag, packed in (("g", gp), ("d", dp), ("m", mp), ("v", vp)):
        for k, a in zip(["b_ada"] + small_names, _unpack(packed, full_shapes)):
            out.setdefault(k, {})
            out[k][tag] = a
    g_cw_full = out["conv_w"]["g"]
    cw_cols = D_CONV // N_CHIP
    g_cw = lax.dynamic_slice_in_dim(g_cw_full, chip * cw_cols, cw_cols, axis=2)
    pad_rows = lambda a: jnp.pad(a.reshape(L * CONV_WIDTH, cw_cols), ((0, 2), (0, 0)))
    g_, d_, m_, v_ = _adamw(pad_rows(conv_w), pad_rows(m_conv_w), pad_rows(v_conv_w), [pad_rows(g_cw)], "adamw_conv_w")
    out["conv_w"] = tuple(a[:L * CONV_WIDTH].reshape(L, CONV_WIDTH, cw_cols) for a in (g_, d_, m_, v_))

    names = ["w_ada", "b_ada", "w_in", "b_gate", "w_pool", "pool_scale", "rel_bias", "conv_w", "conv_b", "conv_ln_g",
             "conv_ln_b", "w_br_pool", "w_br_attn", "w_br_conv", "w_o", "ln_mix_g", "ln_mix_b", "w_ff1", "b_ff1",
             "w_ff2", "b_ff2", "ln_ff_g", "ln_ff_b"]

    def pick(k, i):
        o = out[k]
        return o[i] if isinstance(o, tuple) else o["gdmv"[i]].reshape(env[k].shape)

    return (loss, grad_x, *[pick(k, 0) for k in names], *[pick(k, 1) for k in names],
            *[pick(k, 2) for k in names], *[pick(k, 3) for k in names])
```

```python
import functools

import jax
import jax.numpy as jnp
import numpy as np
from jax import lax
from jax.experimental import pallas as pl
from jax.experimental.pallas import tpu as pltpu

F32 = jnp.float32
BF16 = jnp.bfloat16

D_MODEL = 1024
DEPTH = 2
CHUNK = 64
POOL_WINDOWS = (2, 4, 8, 16)
POOL_GROUP = 64
D_POOL = 256
N_HEADS = 8
HEAD_DIM = 64
D_ATTN = 512
N_PREV_CHUNKS = 8
REL_CLIP = 128
N_REL = 2 * REL_CLIP + 1
D_CONV = 256
CONV_WIDTH = 31
D_FF = 4 * D_MODEL
D_IN = 5376
OFF_POOL, OFF_QKV, OFF_CONV, OFF_GATE = 0, 256, 1792, 2304
ALPHA = (2.0 * DEPTH) ** 0.25
LN_EPS = 1e-5
NEG_INF = -1e30
ADAM_LR, ADAM_B1, ADAM_B2, ADAM_EPS, ADAM_WD, ADAM_STEP = 0.001, 0.9, 0.999, 1e-08, 0.01, 10

N_DEV = 8
N_CHIP = 4
MESH = pl.DeviceIdType.MESH

QB = 2 * CHUNK
KPAD = N_PREV_CHUNKS * CHUNK
KW = QB + KPAD
SKEW_W = 768

VMEM_LIMIT = 56 * 1024 * 1024


def _cparams(**kw):
    return pltpu.CompilerParams(vmem_limit_bytes=VMEM_LIMIT, **kw)


def _full(shape):
    n = len(shape)
    return pl.BlockSpec(shape, lambda *_: (0,) * n)


_DIMS = {"nn": (((1,), (0,)), ((), ())), "nt": (((1,), (1,)), ((), ())), "tn": (((0,), (0,)), ((), ()))}


def _relu2(t):
    r = jnp.maximum(t, 0.0)
    return r * r


def _mm(a, b, mode, *, tm, tn, out_dtype, name, b_col0=0, n_out=None, bias=None, split_n=0, rider=None):
    if mode == "tn":
        k, m = a.shape
        n = b.shape[1] if n_out is None else n_out
        a_spec = pl.BlockSpec((k, tm), lambda i, j: (0, i))
        b_spec = pl.BlockSpec((k, tn), lambda i, j: (0, j + b_col0))
    elif mode == "nn":
        m, k = a.shape
        n = b.shape[1] if n_out is None else n_out
        a_spec = pl.BlockSpec((tm, k), lambda i, j: (i, 0))
        b_spec = pl.BlockSpec((k, tn), lambda i, j: (0, j + b_col0))
    else:
        m, k = a.shape
        n = b.shape[0] if n_out is None else n_out
        a_spec = pl.BlockSpec((tm, k), lambda i, j: (i, 0))
        b_spec = pl.BlockSpec((tn, k), lambda i, j: (j + b_col0, 0))
    assert m % tm == 0 and n % tn == 0, (name, m, n, tm, tn)
    dims = _DIMS[mode]

    def body(*refs):
        if bias is None:
            a_ref, b_ref, o_ref = refs
        else:
            a_ref, b_ref, bias_ref, o_ref = refs
        acc = lax.dot_general(a_ref[...].astype(BF16), b_ref[...].astype(BF16), dims, preferred_element_type=F32)
        if bias is not None:
            acc = acc + bias_ref[...]
        if split_n:
            for c in range(tn // split_n):
                o_ref[c] = acc[:, c * split_n:(c + 1) * split_n].astype(out_dtype)
        else:
            o_ref[...] = acc.astype(out_dtype)

    in_specs = [a_spec, b_spec]
    args = [a, b]
    if bias is not None:
        in_specs.append(pl.BlockSpec((1, tn), lambda i, j: (0, j)))
        args.append(bias)
    if split_n:
        out_spec = pl.BlockSpec((tn // split_n, tm, split_n), lambda i, j: (j, i, 0))
        out_shape = jax.ShapeDtypeStruct((n // split_n, m, split_n), out_dtype)
    else:
        out_spec = pl.BlockSpec((tm, tn), lambda i, j: (i, j))
        out_shape = jax.ShapeDtypeStruct((m, n), out_dtype)
    res = _call(body, name=name, grid=(m // tm, n // tn), in_specs=in_specs, out_specs=[out_spec],
                out_shape=[out_shape], scratch_shapes=[], args=args, rider=rider)
    return res[0] if rider is None else (res[0][0], res[1])


def _ln_hat(x):
    mu = jnp.mean(x, axis=-1, keepdims=True)
    xc = x - mu
    var = jnp.mean(xc * xc, axis=-1, keepdims=True)
    rstd = lax.rsqrt(var + LN_EPS)
    return xc * rstd, rstd


def _ln_hat_bwd(dhat, xhat, rstd):
    m1 = jnp.mean(dhat, axis=-1, keepdims=True)
    m2 = jnp.mean(dhat * xhat, axis=-1, keepdims=True)
    return rstd * (dhat - m1 - xhat * m2)


def _row_tile(s):
    return min(512, s)


def _acc_rows(ref, val, first):
    @pl.when(first)
    def _():
        ref[...] = jnp.zeros_like(ref)
    ref[...] += jnp.sum(val, axis=0, keepdims=True)


def _ln_mod(x, sc, sh, name):
    s, d = x.shape
    tm = _row_tile(s)

    def body(x_ref, sc_ref, sh_ref, u_ref):
        xhat, _ = _ln_hat(x_ref[...])
        u_ref[...] = (xhat * (1.0 + sc_ref[...]) + sh_ref[...]).astype(BF16)

    row = pl.BlockSpec((tm, d), lambda i: (i, 0))
    vec = pl.BlockSpec((1, d), lambda i: (0, 0))
    return pl.pallas_call(body, grid=(s // tm,), in_specs=[row, vec, vec], out_specs=row,
                          out_shape=jax.ShapeDtypeStruct((s, d), BF16), name=name, compiler_params=_cparams())(x, sc, sh)


def _mm_ln_mod_bwd(a, b, x, sc, dres, name, rider=None):
    segs = list(a) if isinstance(a, (list, tuple)) else [a]
    s = segs[0].shape[0]
    k, d = b.shape
    assert sum(t.shape[1] for t in segs) == k
    tm = min(512 if k <= 4096 else 256, s)
    ns = len(segs)

    def body(*refs):
        seg_refs = refs[:ns]
        b_ref, x_ref, sc_ref, dres_ref, dx_ref, dsc_ref, dsh_ref = refs[ns:]
        first = pl.program_id(0) == 0
        duv, r0 = None, 0
        for seg_ref in seg_refs:
            w = seg_ref.shape[1]
            part = jnp.dot(seg_ref[...], b_ref[r0:r0 + w, :], preferred_element_type=F32)
            duv = part if duv is None else duv + part
            r0 += w
        xhat, rstd = _ln_hat(x_ref[...])
        dx_ref[...] = dres_ref[...] + _ln_hat_bwd(duv * (1.0 + sc_ref[...]), xhat, rstd)
        _acc_rows(dsc_ref, duv * xhat, first)
        _acc_rows(dsh_ref, duv, first)

    row = pl.BlockSpec((tm, d), lambda i: (i, 0))
    vec = pl.BlockSpec((1, d), lambda i: (0, 0))
    vs = jax.ShapeDtypeStruct((1, d), F32)
    res = _call(body, name=name, grid=(s // tm,),
                in_specs=[pl.BlockSpec((tm, t.shape[1]), lambda i: (i, 0)) for t in segs] + [_full((k, d)), row, vec, row],
                out_specs=[row, vec, vec], out_shape=[jax.ShapeDtypeStruct((s, d), F32), vs, vs],
                scratch_shapes=[], args=(*segs, b, x, sc, dres), rider=rider)
    return tuple(res) if rider is None else (tuple(res[0]), res[1])


def _dw_segments(segs, u, name):
    s, d = u.shape
    tw = 256
    tiles = [t.shape[1] // tw for t in segs]
    starts = [sum(tiles[:j]) for j in range(len(segs))]
    ns = len(segs)

    def body(*refs):
        seg_refs, u_ref, o_ref = refs[:ns], refs[ns], refs[ns + 1]
        i = pl.program_id(0)
        for seg_ref, t0, nt in zip(seg_refs, starts, tiles):
            @pl.when((i >= t0) & (i < t0 + nt))
            def _(seg_ref=seg_ref):
                acc = lax.dot_general(seg_ref[...], u_ref[...], _DIMS["tn"], preferred_element_type=F32)
                o_ref[0] = acc[:, :d // 2].astype(BF16)
                o_ref[1] = acc[:, d // 2:].astype(BF16)

    def seg_spec(t0, nt):
        return pl.BlockSpec((s, tw), lambda i: (0, jnp.clip(i - t0, 0, nt - 1)))

    return pl.pallas_call(
        body, grid=(sum(tiles),), in_specs=[seg_spec(t0, nt) for t0, nt in zip(starts, tiles)] + [_full((s, d))],
        out_specs=pl.BlockSpec((2, tw, d // 2), lambda i: (0, i, 0)),
        out_shape=jax.ShapeDtypeStruct((2, sum(tiles) * tw, d // 2), BF16), name=name, compiler_params=_cparams(),
    )(*segs, u)


def _mm_resid_ln(a, b, bias, x, g, gam, bet, name, rider=None):
    s, k = a.shape
    d = b.shape[1]
    tm = min(512, s)

    def body(*refs):
        if bias is None:
            a_ref, b_ref, x_ref, g_ref, gam_ref, bet_ref, f_ref, o_ref = refs
        else:
            a_ref, b_ref, bias_ref, x_ref, g_ref, gam_ref, bet_ref, f_ref, o_ref = refs
        f = jnp.dot(a_ref[...], b_ref[...], preferred_element_type=F32)
        if bias is not None:
            f = f + bias_ref[...]
        f_ref[...] = f
        rhat, _ = _ln_hat(ALPHA * x_ref[...] + g_ref[...] * f)
        o_ref[...] = rhat * gam_ref[...] + bet_ref[...]

    row = pl.BlockSpec((tm, d), lambda i: (i, 0))
    vec = pl.BlockSpec((1, d), lambda i: (0, 0))
    in_specs = [pl.BlockSpec((tm, k), lambda i: (i, 0)), _full((k, d))] + ([vec] if bias is not None else []) + [row, vec, vec, vec]
    args = [a, b] + ([bias] if bias is not None else []) + [x, g, gam, bet]
    sh = jax.ShapeDtypeStruct((s, d), F32)
    res = _call(body, name=name, grid=(s // tm,), in_specs=in_specs, out_specs=[row, row], out_shape=[sh, sh],
                scratch_shapes=[], args=args, rider=rider)
    return tuple(res) if rider is None else (tuple(res[0]), res[1])


def _resid_ln_bwd(dxo, x, f, g, gam, name):
    s, d = x.shape
    tm = _row_tile(s)

    def body(dxo_ref, x_ref, f_ref, g_ref, gam_ref, dres_ref, df_ref, dgam_ref, dbet_ref, dg_ref, dbias_ref):
        first = pl.program_id(0) == 0
        dxov = dxo_ref[...]
        fv = f_ref[...]
        rhat, rstd = _ln_hat(ALPHA * x_ref[...] + g_ref[...] * fv)
        dr = _ln_hat_bwd(dxov * gam_ref[...], rhat, rstd)
        dfv = g_ref[...] * dr
        dres_ref[...] = ALPHA * dr
        df_ref[...] = dfv.astype(BF16)
        _acc_rows(dgam_ref, dxov * rhat, first)
        _acc_rows(dbet_ref, dxov, first)
        _acc_rows(dg_ref, dr * fv, first)
        _acc_rows(dbias_ref, dfv, first)

    row = pl.BlockSpec((tm, d), lambda i: (i, 0))
    vec = pl.BlockSpec((1, d), lambda i: (0, 0))
    vs = jax.ShapeDtypeStruct((1, d), F32)
    return pl.pallas_call(body, grid=(s // tm,), in_specs=[row, row, row, vec, vec],
                          out_specs=[row, row, vec, vec, vec, vec],
                          out_shape=[jax.ShapeDtypeStruct((s, d), F32), jax.ShapeDtypeStruct((s, d), BF16), vs, vs, vs, vs],
                          name=name, compiler_params=_cparams())(dxo, x, f, g, gam)


def _loss_grad(y, tgt, name):
    s, d = y.shape
    tm = _row_tile(s)
    n = s // tm

    def body(y_ref, t_ref, dy_ref, loss_ref, acc_ref):
        i = pl.program_id(0)
        e = y_ref[...] - t_ref[...]
        dy_ref[...] = e * (1.0 / d)
        _acc_rows(acc_ref, e * e, i == 0)

        @pl.when(i == n - 1)
        def _():
            tot = jnp.sum(acc_ref[...], axis=1, keepdims=True) * (0.5 / d)
            loss_ref[...] = jnp.broadcast_to(tot, (1, 128))

    row = pl.BlockSpec((tm, d), lambda i: (i, 0))
    return pl.pallas_call(body, grid=(n,), in_specs=[row, row],
                          out_specs=[row, pl.BlockSpec((1, 128), lambda i: (0, 0))],
                          out_shape=[jax.ShapeDtypeStruct((s, d), F32), jax.ShapeDtypeStruct((1, 128), F32)],
                          scratch_shapes=[pltpu.VMEM((1, d), F32)], name=name, compiler_params=_cparams())(y, tgt)


POOL_HALO = 16
POOL_ROWS = 256


def _pool_counts(r0, rows):
    t1 = (lax.broadcasted_iota(jnp.int32, (rows, 128), 0) + r0 + 1).astype(F32)
    low = lax.broadcasted_iota(jnp.int32, (rows, 128), 1) < POOL_GROUP
    wa = jnp.where(low, float(POOL_WINDOWS[0]), float(POOL_WINDOWS[1]))
    wb = jnp.where(low, float(POOL_WINDOWS[2]), float(POOL_WINDOWS[3]))
    return jnp.minimum(t1, wa), jnp.minimum(t1, wb), low


def _window_sums(win, off, rows, sign):
    def sl(j, half):
        return win[off + sign * j: off + sign * j + rows, 128 * half:128 * half + 128]
    a2 = sl(0, 0) + sl(1, 0)
    a4 = a2 + sl(2, 0) + sl(3, 0)
    a8 = sl(0, 1)
    for j in range(1, 8):
        a8 = a8 + sl(j, 1)
    a16 = a8
    for j in range(8, 16):
        a16 = a16 + sl(j, 1)
    return a2, a4, a8, a16


def _pool_fwd(zp, wp_bd, pscale, name):
    s = zp.shape[0]
    r = min(POOL_ROWS, s)

    def body(z_ref, wp_ref, sc_ref, p_ref, feat_ref, pad):
        pad[0:POOL_HALO, :] = jnp.zeros((POOL_HALO, D_POOL), F32)
        pad[POOL_HALO:, :] = z_ref[...]

        def step(i, carry):
            r0 = pl.multiple_of(i * r, r)
            win = pad[pl.ds(r0, r + POOL_HALO), :]
            a2, a4, a8, a16 = _window_sums(win, POOL_HALO, r, -1)
            ca, cb, low = _pool_counts(r0, r)
            x0 = win[POOL_HALO:, :]
            pa = jnp.where(low, a2, a4) / ca
            pb = jnp.where(low, a8, a16) / cb
            p = (jnp.concatenate([pa, pb], axis=1) - x0).astype(BF16)
            p_ref[pl.ds(r0, r), :] = p
            pw = jnp.dot(p, wp_ref[...], preferred_element_type=F32)
            feat_ref[pl.ds(r0, r), :] = (pw * sc_ref[...]).astype(BF16)
            return carry

        lax.fori_loop(0, s // r, step, 0)

    return pl.pallas_call(
        body, out_shape=[jax.ShapeDtypeStruct((s, D_POOL), BF16), jax.ShapeDtypeStruct((s, D_POOL), BF16)],
        scratch_shapes=[pltpu.VMEM((s + POOL_HALO, D_POOL), F32)], name=name, compiler_params=_cparams(),
    )(zp, wp_bd, pscale)


def _pool_bwd(dfeat, p, wp_bd, pscale, name):
    s = p.shape[0]
    r = min(POOL_ROWS, s)

    def body(df_ref, p_ref, wp_ref, sc_ref, dz_ref, dwp_ref, dsc_ref, gpad, dpbuf):
        dwp_ref[...] = jnp.zeros_like(dwp_ref)
        dsc_ref[...] = jnp.zeros_like(dsc_ref)
        gpad[s:, :] = jnp.zeros((POOL_HALO, D_POOL), F32)

        def step1(i, carry):
            r0 = pl.multiple_of(i * r, r)
            pv = p_ref[pl.ds(r0, r), :]
            dfv = df_ref[pl.ds(r0, r), :]
            pw = jnp.dot(pv, wp_ref[...], preferred_element_type=F32)
            dsc_ref[...] += jnp.sum(dfv * pw, axis=0, keepdims=True)
            dpw = (dfv * sc_ref[...]).astype(BF16)
            dwp_ref[...] += lax.dot_general(pv, dpw, _DIMS["tn"], preferred_element_type=F32)
            dp = lax.dot_general(dpw, wp_ref[...], _DIMS["nt"], preferred_element_type=F32)
            ca, cb, _ = _pool_counts(r0, r)
            gpad[pl.ds(r0, r), :] = dp / jnp.concatenate([ca, cb], axis=1)
            dpbuf[pl.ds(r0, r), :] = dp
            return carry

        lax.fori_loop(0, s // r, step1, 0)

        def step2(i, carry):
            r0 = pl.multiple_of(i * r, r)
            win = gpad[pl.ds(r0, r + POOL_HALO), :]
            a2, a4, a8, a16 = _window_sums(win, 0, r, 1)
            low = lax.broadcasted_iota(jnp.int32, (r, 128), 1) < POOL_GROUP
            acc = jnp.concatenate([jnp.where(low, a2, a4), jnp.where(low, a8, a16)], axis=1)
            dz_ref[pl.ds(r0, r), :] = (acc - dpbuf[pl.ds(r0, r), :]).astype(BF16)
            return carry

        lax.fori_loop(0, s // r, step2, 0)

    return pl.pallas_call(
        body,
        out_shape=[jax.ShapeDtypeStruct((s, D_POOL), BF16), jax.ShapeDtypeStruct((D_POOL, D_POOL), F32),
                   jax.ShapeDtypeStruct((1, D_POOL), F32)],
        scratch_shapes=[pltpu.VMEM((s + POOL_HALO, D_POOL), F32), pltpu.VMEM((s, D_POOL), F32)],
        name=name, compiler_params=_cparams(),
    )(dfeat, p, wp_bd, pscale)


def _skew_index():
    cp = lax.broadcasted_iota(jnp.int32, (SKEW_W, N_REL), 0)
    dist = jnp.where(cp < KW, KPAD - cp, KPAD + SKEW_W - cp)
    idx = jnp.clip(dist, -REL_CLIP, REL_CLIP) + REL_CLIP
    return (idx == lax.broadcasted_iota(jnp.int32, (SKEW_W, N_REL), 1)).astype(F32)


def _row_bits(b):
    return (lax.broadcasted_iota(jnp.int32, (QB, SKEW_W), 0) >> b) & 1 == 1


N_EDGE = KPAD // QB


def _bias_block(rel_bias, name):
    def body(rb_ref, o_ref):
        onehot = _skew_index()
        row0 = lax.dot_general(rb_ref[...], onehot, _DIMS["nt"], precision=lax.Precision.HIGHEST,
                               preferred_element_type=F32)
        r = lax.broadcasted_iota(jnp.int32, (QB, KW), 0)
        kk = lax.broadcasted_iota(jnp.int32, (QB, KW), 1)
        cq, ck = r // CHUNK, kk // CHUNK
        band = (ck >= cq) & (ck <= cq + N_PREV_CHUNKS)
        for h in range(N_HEADS):
            t = jnp.broadcast_to(row0[h:h + 1, :], (QB, SKEW_W))
            for b in range(7):
                t = jnp.where(_row_bits(b), pltpu.roll(t, 1 << b, 1), t)
            for e in range(N_EDGE + 1):
                o_ref[e, h] = jnp.where(band & (kk >= KPAD - e * QB), t[:, :KW], NEG_INF)

    return pl.pallas_call(body, out_shape=jax.ShapeDtypeStruct((N_EDGE + 1, N_HEADS, QB, KW), F32), name=name,
                          compiler_params=_cparams())(rel_bias)


def _bias_spec():
    return pl.BlockSpec((None, N_HEADS, QB, KW), lambda i: (jnp.minimum(i, N_EDGE), 0, 0, 0))


def _bias_block_bwd(ds_acc, name):
    def body(ds_ref, o_ref):
        sums = []
        for h in range(N_HEADS):
            t = jnp.concatenate([ds_ref[h], jnp.zeros((QB, SKEW_W - KW), F32)], axis=1)
            for b in range(7):
                t = jnp.where(_row_bits(b), pltpu.roll(t, SKEW_W - (1 << b), 1), t)
            sums.append(jnp.sum(t, axis=0, keepdims=True))
        allh = jnp.concatenate(sums, axis=0)
        o_ref[...] = jnp.dot(allh, _skew_index(), precision=lax.Precision.HIGHEST, preferred_element_type=F32)

    return pl.pallas_call(body, out_shape=jax.ShapeDtypeStruct((N_HEADS, N_REL), F32), name=name,
                          compiler_params=_cparams())(ds_acc)


def _scaled(q):
    return (q.astype(F32) * (HEAD_DIM ** -0.5)).astype(BF16)


def _probs(q, kw, bias_ref):
    sc = jnp.stack([lax.dot_general(q[:, HEAD_DIM * h:HEAD_DIM * (h + 1)], kw[:, HEAD_DIM * h:HEAD_DIM * (h + 1)],
                                    _DIMS["nt"], preferred_element_type=F32) + bias_ref[h] for h in range(N_HEADS)])
    e = jnp.exp(sc - jnp.max(sc, axis=-1, keepdims=True))
    return e * (1.0 / jnp.sum(e, axis=-1, keepdims=True))


def _load_padded_kv(qkv_hbm, kpad, vpad, sems, s):
    kpad[0:KPAD, :] = jnp.zeros((KPAD, D_ATTN), BF16)
    vpad[0:KPAD, :] = jnp.zeros((KPAD, D_ATTN), BF16)
    ck = pltpu.make_async_copy(qkv_hbm.at[:, D_ATTN:2 * D_ATTN], kpad.at[pl.ds(KPAD, s), :], sems.at[0])
    cv = pltpu.make_async_copy(qkv_hbm.at[:, 2 * D_ATTN:3 * D_ATTN], vpad.at[pl.ds(KPAD, s), :], sems.at[1])
    ck.start()
    cv.start()
    ck.wait()
    cv.wait()


def _attn_fwd(qkv, bias, name, rider=None):
    s = qkv.shape[0]

    def body(q_ref, qkv_hbm, bias_ref, o_ref, kpad, vpad, sems):
        i = pl.program_id(0)

        @pl.when(i == 0)
        def _():
            _load_padded_kv(qkv_hbm, kpad, vpad, sems, s)

        base = pl.multiple_of(i * QB, QB)
        kw = kpad[pl.ds(base, KW), :]
        vw = vpad[pl.ds(base, KW), :]
        q = _scaled(q_ref[...])
        p = _probs(q, kw, bias_ref).astype(BF16)
        outs = [jnp.dot(p[h], vw[:, HEAD_DIM * h:HEAD_DIM * (h + 1)], preferred_element_type=F32)
                for h in range(N_HEADS)]
        o_ref[...] = jnp.concatenate(outs, axis=1).astype(BF16)

    res = _call(
        body, name=name, grid=(s // QB,),
        in_specs=[pl.BlockSpec((QB, D_ATTN), lambda i: (i, 0)), pl.BlockSpec(memory_space=pl.ANY),
                  _bias_spec()],
        out_specs=[pl.BlockSpec((QB, D_ATTN), lambda i: (i, 0))],
        out_shape=[jax.ShapeDtypeStruct((s, D_ATTN), BF16)],
        scratch_shapes=[pltpu.VMEM((s + KPAD, D_ATTN), BF16), pltpu.VMEM((s + KPAD, D_ATTN), BF16),
                        pltpu.SemaphoreType.DMA((2,))],
        args=(qkv, qkv, bias), rider=rider)
    return res[0] if rider is None else (res[0][0], res[1])


def _attn_bwd(qkv, do, bias, name, rider=None):
    s = qkv.shape[0]
    n = s // QB

    def body(q_ref, qkv_hbm, do_ref, bias_ref, dq_ref, dk_hbm, dv_hbm, ds_ref, kpad, vpad, dkacc, dvacc, sems):
        i = pl.program_id(0)

        @pl.when(i == 0)
        def _():
            _load_padded_kv(qkv_hbm, kpad, vpad, sems, s)
            dkacc[...] = jnp.zeros_like(dkacc)
            dvacc[...] = jnp.zeros_like(dvacc)
            ds_ref[...] = jnp.zeros_like(ds_ref)

        base = pl.multiple_of(i * QB, QB)
        kw = kpad[pl.ds(base, KW), :]
        vw = vpad[pl.ds(base, KW), :]
        q = _scaled(q_ref[...])
        dov = do_ref[...]
        heads = [slice(HEAD_DIM * h, HEAD_DIM * (h + 1)) for h in range(N_HEADS)]
        p = _probs(q, kw, bias_ref)
        dp = jnp.stack([lax.dot_general(dov[:, hs], vw[:, hs], _DIMS["nt"], preferred_element_type=F32) for hs in heads])
        ds = p * (dp - jnp.sum(dp * p, axis=-1, keepdims=True))
        ds_ref[...] += ds
        pb, dsb = p.astype(BF16), ds.astype(BF16)
        dvs = [lax.dot_general(pb[h], dov[:, hs], _DIMS["tn"], preferred_element_type=F32) for h, hs in enumerate(heads)]
        dqs = [jnp.dot(dsb[h], kw[:, hs], preferred_element_type=F32) for h, hs in enumerate(heads)]
        dks = [lax.dot_general(dsb[h], q[:, hs], _DIMS["tn"], preferred_element_type=F32) for h, hs in enumerate(heads)]
        dq_ref[...] = (jnp.concatenate(dqs, axis=1) * (HEAD_DIM ** -0.5)).astype(BF16)
        dkacc[pl.ds(base, KW), :] += jnp.concatenate(dks, axis=1)
        dvacc[pl.ds(base, KW), :] += jnp.concatenate(dvs, axis=1)

        @pl.when(i == n - 1)
        def _():
            def cast(j, carry):
                rows = pl.ds(pl.multiple_of(KPAD + j * 512, 512), 512)
                kpad[rows, :] = dkacc[rows, :].astype(BF16)
                vpad[rows, :] = dvacc[rows, :].astype(BF16)
                return carry

            lax.fori_loop(0, s // 512, cast, 0)
            ck = pltpu.make_async_copy(kpad.at[pl.ds(KPAD, s), :], dk_hbm, sems.at[0])
            cv = pltpu.make_async_copy(vpad.at[pl.ds(KPAD, s), :], dv_hbm, sems.at[1])
            ck.start()
            cv.start()
            ck.wait()
            cv.wait()

    blk = pl.BlockSpec((QB, D_ATTN), lambda i: (i, 0))
    acc_shape = jax.ShapeDtypeStruct((s, D_ATTN), BF16)
    return _call(
        body, name=name, grid=(n,),
        in_specs=[blk, pl.BlockSpec(memory_space=pl.ANY), blk, _bias_spec()],
        out_specs=[blk, pl.BlockSpec(memory_space=pl.ANY), pl.BlockSpec(memory_space=pl.ANY), _full((N_HEADS, QB, KW))],
        out_shape=[jax.ShapeDtypeStruct((s, D_ATTN), BF16), acc_shape, acc_shape,
                   jax.ShapeDtypeStruct((N_HEADS, QB, KW), F32)],
        scratch_shapes=[pltpu.VMEM((s + KPAD, D_ATTN), BF16), pltpu.VMEM((s + KPAD, D_ATTN), BF16),
                        pltpu.VMEM((s + KPAD, D_ATTN), F32), pltpu.VMEM((s + KPAD, D_ATTN), F32),
                        pltpu.SemaphoreType.DMA((2,))],
        args=(qkv, qkv, do, bias), rider=rider)


CONV_HALO = 32
CONV_ROWS = 64


def _sigmoid(t):
    return 1.0 / (1.0 + jnp.exp(-t))


def _row_windows(win, rows):
    total = win.shape[0]
    shifted = {0: win}

    def get(o):
        j = o % 8
        if j not in shifted:
            shifted[j] = win[j:j + total - 8, :]
        return shifted[j][o - j:o - j + rows, :]

    return get


def _glu_rows(z_ref, r0, rows):
    a = z_ref[pl.ds(r0, rows), 0:D_CONV]
    b = z_ref[pl.ds(r0, rows), D_CONV:2 * D_CONV]
    return a, _sigmoid(b)


def _conv_fwd(zc, conv_w, conv_b, ln_g, ln_b, name):
    s = zc.shape[0]
    rt = min(256, s)

    def body(z_ref, w_ref, cb_ref, g_ref, b_ref, cv_ref, feat_ref, hpad):
        hpad[0:CONV_HALO, :] = jnp.zeros((CONV_HALO, D_CONV), F32)

        def glu(i, carry):
            r0 = pl.multiple_of(i * rt, rt)
            a, sb = _glu_rows(z_ref, r0, rt)
            hpad[pl.ds(r0 + CONV_HALO, rt), :] = a * sb
            return carry

        lax.fori_loop(0, s // rt, glu, 0)
        w = w_ref[...]

        def conv(i, carry):
            r0 = pl.multiple_of(i * CONV_ROWS, CONV_ROWS)
            win = _row_windows(hpad[pl.ds(r0, CONV_ROWS + CONV_HALO), :], CONV_ROWS)
            acc = jnp.broadcast_to(cb_ref[...], (CONV_ROWS, D_CONV))
            for k in range(CONV_WIDTH):
                acc = acc + win(2 + k) * w[k:k + 1, :]
            cv_ref[pl.ds(r0, CONV_ROWS), :] = acc
            yhat, _ = _ln_hat(acc)
            y = yhat * g_ref[...] + b_ref[...]
            feat_ref[pl.ds(r0, CONV_ROWS), :] = (y * _sigmoid(y)).astype(BF16)
            return carry

        lax.fori_loop(0, s // CONV_ROWS, conv, 0)

    return pl.pallas_call(
        body, out_shape=[jax.ShapeDtypeStruct((s, D_CONV), F32), jax.ShapeDtypeStruct((s, D_CONV), BF16)],
        scratch_shapes=[pltpu.VMEM((s + CONV_HALO, D_CONV), F32)], name=name, compiler_params=_cparams(),
    )(zc, conv_w, conv_b, ln_g, ln_b)


def _conv_bwd(dfeat, cv, zc, conv_w, ln_g, ln_b, name):
    s = zc.shape[0]
    rt = min(256, s)

    def body(df_ref, cv_ref, z_ref, w_ref, g_ref, b_ref, dz_ref, dw_ref, dcb_ref, dg_ref, db_ref, hpad, dcvpad, dwacc):
        hpad[0:CONV_HALO, :] = jnp.zeros((CONV_HALO, D_CONV), F32)
        dcvpad[s:, :] = jnp.zeros((CONV_HALO, D_CONV), F32)
        dwacc[...] = jnp.zeros_like(dwacc)
        dcb_ref[...] = jnp.zeros_like(dcb_ref)
        dg_ref[...] = jnp.zeros_like(dg_ref)
        db_ref[...] = jnp.zeros_like(db_ref)

        def pass1(i, carry):
            r0 = pl.multiple_of(i * rt, rt)
            a, sb = _glu_rows(z_ref, r0, rt)
            hpad[pl.ds(r0 + CONV_HALO, rt), :] = a * sb
            cvhat, rstd = _ln_hat(cv_ref[pl.ds(r0, rt), :])
            y = cvhat * g_ref[...] + b_ref[...]
            sg = _sigmoid(y)
            dy = df_ref[pl.ds(r0, rt), :] * (sg * (1.0 + y * (1.0 - sg)))
            dg_ref[...] += jnp.sum(dy * cvhat, axis=0, keepdims=True)
            db_ref[...] += jnp.sum(dy, axis=0, keepdims=True)
            dcv = _ln_hat_bwd(dy * g_ref[...], cvhat, rstd)
            dcb_ref[...] += jnp.sum(dcv, axis=0, keepdims=True)
            dcvpad[pl.ds(r0, rt), :] = dcv
            return carry

        lax.fori_loop(0, s // rt, pass1, 0)
        w = w_ref[...]

        def pass2(i, carry):
            r0 = pl.multiple_of(i * CONV_ROWS, CONV_ROWS)
            dwin = _row_windows(dcvpad[pl.ds(r0, CONV_ROWS + CONV_HALO), :], CONV_ROWS)
            hwin = _row_windows(hpad[pl.ds(r0, CONV_ROWS + CONV_HALO), :], CONV_ROWS)
            dcv = dwin(0)
            dh = jnp.zeros((CONV_ROWS, D_CONV), F32)
            for k in range(CONV_WIDTH):
                dh = dh + dwin(30 - k) * w[k:k + 1, :]
                prod = dcv * hwin(2 + k)
                dwacc[8 * k:8 * k + 8, :] += jnp.sum(prod.reshape(CONV_ROWS // 8, 8, D_CONV), axis=0)
            a, sb = _glu_rows(z_ref, r0, CONV_ROWS)
            dz_ref[pl.ds(r0, CONV_ROWS), :] = jnp.concatenate([dh * sb, dh * a * sb * (1.0 - sb)], axis=1).astype(BF16)
            return carry

        lax.fori_loop(0, s // CONV_ROWS, pass2, 0)
        dw_ref[...] = jnp.sum(dwacc[...].reshape(32, 8, D_CONV), axis=1)

    vs = jax.ShapeDtypeStruct((1, D_CONV), F32)
    return pl.pallas_call(
        body,
        out_shape=[jax.ShapeDtypeStruct((s, 2 * D_CONV), BF16), jax.ShapeDtypeStruct((32, D_CONV), F32), vs, vs, vs],
        scratch_shapes=[pltpu.VMEM((s + CONV_HALO, D_CONV), F32), pltpu.VMEM((s + CONV_HALO, D_CONV), F32),
                        pltpu.VMEM((256, D_CONV), F32)],
        name=name, compiler_params=_cparams(),
    )(dfeat, cv, zc, conv_w, ln_g, ln_b)


def _merge(zg, b_gate, ys, name):
    s = zg.shape[0]
    tm = _row_tile(s)

    def body(zg_ref, bg_ref, y0_ref, y1_ref, y2_ref, o_ref):
        acc = None
        for j, y_ref in enumerate((y0_ref, y1_ref, y2_ref)):
            cs = slice(D_MODEL * j, D_MODEL * (j + 1))
            t = _sigmoid(zg_ref[:, cs] + bg_ref[:, cs]) * y_ref[...]
            acc = t if acc is None else acc + t
        o_ref[...] = acc.astype(BF16)

    row = pl.BlockSpec((tm, D_MODEL), lambda i: (i, 0))
    return pl.pallas_call(
        body, grid=(s // tm,),
        in_specs=[pl.BlockSpec((tm, 3 * D_MODEL), lambda i: (i, 0)), _full((1, 3 * D_MODEL)), row, row, row],
        out_specs=row, out_shape=jax.ShapeDtypeStruct((s, D_MODEL), BF16), name=name, compiler_params=_cparams(),
    )(zg, b_gate, *ys)


def _merge_bwd(dm, zg, b_gate, ys, name):
    s = zg.shape[0]
    tm = min(256, s)

    def body(dm_ref, zg_ref, bg_ref, y0_ref, y1_ref, y2_ref, d0_ref, d1_ref, d2_ref, dzg_ref, dbg_ref):
        first = pl.program_id(0) == 0

        @pl.when(first)
        def _():
            dbg_ref[...] = jnp.zeros_like(dbg_ref)

        dmv = dm_ref[...]
        for j, (y_ref, d_ref) in enumerate(((y0_ref, d0_ref), (y1_ref, d1_ref), (y2_ref, d2_ref))):
            cs = slice(D_MODEL * j, D_MODEL * (j + 1))
            g = _sigmoid(zg_ref[:, cs] + bg_ref[:, cs])
            d_ref[...] = (dmv * g).astype(BF16)
            dzg = dmv * y_ref[...] * g * (1.0 - g)
            dzg_ref[:, cs] = dzg.astype(BF16)
            dbg_ref[:, cs] += jnp.sum(dzg, axis=0, keepdims=True)

    row = pl.BlockSpec((tm, D_MODEL), lambda i: (i, 0))
    wide = pl.BlockSpec((tm, 3 * D_MODEL), lambda i: (i, 0))
    yb = jax.ShapeDtypeStruct((s, D_MODEL), BF16)
    return pl.pallas_call(
        body, grid=(s // tm,),
        in_specs=[row, wide, _full((1, 3 * D_MODEL)), row, row, row],
        out_specs=[row, row, row, wide, _full((1, 3 * D_MODEL))],
        out_shape=[yb, yb, yb, jax.ShapeDtypeStruct((s, 3 * D_MODEL), BF16), jax.ShapeDtypeStruct((1, 3 * D_MODEL), F32)],
        name=name, compiler_params=_cparams(),
    )(dm, zg, b_gate, *ys)


def _ff_hidden(u2, w_ff1t, b_ff1, name, rider=None):
    s = u2.shape[0]
    tm, tn = min(1024, s), 1024

    def body(a_ref, b_ref, bias_ref, pre_ref, h_ref):
        acc = lax.dot_general(a_ref[...], b_ref[...], _DIMS["nt"], preferred_element_type=F32) + bias_ref[...]
        pre_ref[...] = acc.astype(BF16)
        h_ref[...] = _relu2(acc).astype(BF16)

    blk = pl.BlockSpec((tm, tn), lambda i, j: (i, j))
    sh = jax.ShapeDtypeStruct((s, D_FF), BF16)
    res = _call(body, name=name, grid=(s // tm, D_FF // tn),
                in_specs=[pl.BlockSpec((tm, D_MODEL), lambda i, j: (i, 0)), pl.BlockSpec((tn, D_MODEL), lambda i, j: (j, 0)),
                          pl.BlockSpec((1, tn), lambda i, j: (0, j))],
                out_specs=[blk, blk], out_shape=[sh, sh], scratch_shapes=[], args=(u2, w_ff1t, b_ff1), rider=rider)
    return tuple(res) if rider is None else (tuple(res[0]), res[1])


def _ff_hidden_bwd(dff, w_ff2, hpre, name):
    s = dff.shape[0]
    tm, tn = min(512, s), 1024

    def body(a_ref, b_ref, h_ref, o_ref, sum_ref):
        dh = lax.dot_general(a_ref[...], b_ref[...], _DIMS["nt"], preferred_element_type=F32)
        dpre = dh * (2.0 * jnp.maximum(h_ref[...].astype(F32), 0.0))
        o_ref[...] = dpre.astype(BF16)
        _acc_rows(sum_ref, dpre, pl.program_id(1) == 0)

    return pl.pallas_call(
        body, grid=(D_FF // tn, s // tm),
        in_specs=[pl.BlockSpec((tm, D_MODEL), lambda j, i: (i, 0)), pl.BlockSpec((tn, D_MODEL), lambda j, i: (j, 0)),
                  pl.BlockSpec((tm, tn), lambda j, i: (i, j))],
        out_specs=[pl.BlockSpec((tm, tn), lambda j, i: (i, j)), pl.BlockSpec((1, tn), lambda j, i: (0, j))],
        out_shape=[jax.ShapeDtypeStruct((s, D_FF), BF16), jax.ShapeDtypeStruct((1, D_FF), F32)],
        name=name, compiler_params=_cparams(),
    )(dff, w_ff2, hpre)


def _silu(t):
    return t * _sigmoid(t)


def _mod_fwd(c_all, w_ada_sh, b_ada_sh, name):
    cols = w_ada_sh.shape[2]

    def body(c_ref, w_ref, b_ref, o_ref):
        ca = _silu(c_ref[...]).astype(BF16)
        o_ref[0] = jnp.dot(ca, w_ref[0].astype(BF16), preferred_element_type=F32) + b_ref[0]

    return pl.pallas_call(
        body, grid=(DEPTH,),
        in_specs=[_full((N_DEV, D_MODEL)), pl.BlockSpec((1, D_MODEL, cols), lambda l: (l, 0, 0)),
                  pl.BlockSpec((1, 1, cols), lambda l: (l, 0, 0))],
        out_specs=pl.BlockSpec((1, N_DEV, cols), lambda l: (l, 0, 0)),
        out_shape=jax.ShapeDtypeStruct((DEPTH, N_DEV, cols), F32), name=name, compiler_params=_cparams(),
    )(c_all, w_ada_sh, b_ada_sh)


def _mod_bwd(c_all, dmod_sh, name):
    cols = dmod_sh.shape[2]

    def body(c_ref, d_ref, o_ref):
        ca = _silu(c_ref[...])
        o_ref[0] = lax.dot_general(ca, d_ref[0], _DIMS["tn"], precision=lax.Precision.HIGHEST,
                                   preferred_element_type=F32)

    return pl.pallas_call(
        body, grid=(DEPTH,),
        in_specs=[_full((N_DEV, D_MODEL)), pl.BlockSpec((1, N_DEV, cols), lambda l: (l, 0, 0))],
        out_specs=pl.BlockSpec((1, D_MODEL, cols), lambda l: (l, 0, 0)),
        out_shape=jax.ShapeDtypeStruct((DEPTH, D_MODEL, cols), F32), name=name, compiler_params=_cparams(),
    )(c_all, dmod_sh)


def _flat_tiles(rows, cols, itemsize_total):
    budget = 12 * 1024 * 1024
    tr = rows
    while tr % 32 == 0 and tr * cols * itemsize_total > budget:
        tr //= 2
    return tr


def _sum_cores(dw, recv, place, name):
    _, m, n = dw.shape
    tr = _flat_tiles(m, n, 6)

    def body(place_ref, a_ref, b_ref, o_ref):
        o_ref[...] = (a_ref[...].astype(F32) + b_ref[...].astype(F32)).astype(BF16)

    grid_spec = pltpu.PrefetchScalarGridSpec(
        num_scalar_prefetch=1, grid=(m // tr,),
        in_specs=[pl.BlockSpec((None, tr, n), lambda i, pr: (pr[0], i, 0)), pl.BlockSpec((tr, n), lambda i, pr: (i, 0))],
        out_specs=pl.BlockSpec((tr, n), lambda i, pr: (i, 0)))
    return pl.pallas_call(body, grid_spec=grid_spec, out_shape=jax.ShapeDtypeStruct((m, n), BF16), name=name,
                          compiler_params=_cparams())(place, dw, recv)


def _sum_chips(h, r, place, name):
    _, rs, n = h.shape
    tr = _flat_tiles(rs, n, 12)

    def body(place_ref, h_ref, r_ref, o_ref):
        o_ref[...] = ((h_ref[...].astype(F32) + r_ref[0].astype(F32)) + r_ref[1].astype(F32)) + r_ref[2].astype(F32)

    grid_spec = pltpu.PrefetchScalarGridSpec(
        num_scalar_prefetch=1, grid=(rs // tr,),
        in_specs=[pl.BlockSpec((None, tr, n), lambda i, pr: (pr[1], i, 0)), pl.BlockSpec((3, tr, n), lambda i, pr: (0, i, 0))],
        out_specs=pl.BlockSpec((tr, n), lambda i, pr: (i, 0)))
    return pl.pallas_call(body, grid_spec=grid_spec, out_shape=jax.ShapeDtypeStruct((rs, n), F32), name=name,
                          compiler_params=_cparams())(place, h, r)


def _adam_math(w, g, m, v):
    m2 = ADAM_B1 * m + (1.0 - ADAM_B1) * g
    v2 = ADAM_B2 * v + (1.0 - ADAM_B2) * (g * g)
    m_hat = m2 / (1.0 - ADAM_B1 ** ADAM_STEP)
    v_hat = v2 / (1.0 - ADAM_B2 ** ADAM_STEP)
    delta = -ADAM_LR * (m_hat / (jnp.sqrt(v_hat) + ADAM_EPS) + ADAM_WD * w)
    return delta, m2, v2


def _adamw(w, m, v, grads, name):
    r, c = w.shape
    tr = _flat_tiles(r, c, 4 * (7 + len(grads)))

    def body(*refs):
        w_ref, m_ref, v_ref = refs[:3]
        g_refs = refs[3:3 + len(grads)]
        g_ref, d_ref, m2_ref, v2_ref = refs[3 + len(grads):]
        g = g_refs[0][...]
        for gr in g_refs[1:]:
            g = g + gr[...]
        delta, m2, v2 = _adam_math(w_ref[...], g, m_ref[...], v_ref[...])
        g_ref[...] = g
        d_ref[...] = delta
        m2_ref[...] = m2
        v2_ref[...] = v2

    blk = pl.BlockSpec((tr, c), lambda i: (i, 0))
    sh = jax.ShapeDtypeStruct((r, c), F32)
    return pl.pallas_call(body, grid=(r // tr,), in_specs=[blk] * (3 + len(grads)), out_specs=[blk] * 4,
                          out_shape=[sh] * 4, name=name, compiler_params=_cparams())(w, m, v, *grads)


def _adamw_halves(w, m, v, own, other, place, split, name):
    nl, r, c = w.shape
    hr, hc = own[0].shape
    tr = _flat_tiles(hr, hc, 4 * (7 + 2 * nl))
    nt = hr // tr
    if split == "rows":
        w_spec = pl.BlockSpec((None, tr, c), lambda l, h, t, pr: (l, h * nt + t, 0))
    else:
        w_spec = pl.BlockSpec((None, tr, hc), lambda l, h, t, pr: (l, t, h))

    def g_spec(layer, mine):
        return pl.BlockSpec((tr, hc), lambda l, h, t, pr: (jnp.where((l == layer) & ((h == pr[0]) == mine), t, nt - 1), 0))

    def body(place_ref, w_ref, m_ref, v_ref, *refs):
        own_refs, other_refs = refs[:nl], refs[nl:2 * nl]
        g_ref, d_ref, m2_ref, v2_ref = refs[2 * nl:]
        layer = pl.program_id(0)
        mine = pl.program_id(1) == place_ref[0]
        g = None
        for li in range(nl):
            cand = jnp.where(mine, own_refs[li][...], other_refs[li][...])
            g = cand if g is None else jnp.where(layer == li, cand, g)
        delta, m2, v2 = _adam_math(w_ref[...], g, m_ref[...], v_ref[...])
        g_ref[...] = g
        d_ref[...] = delta
        m2_ref[...] = m2
        v2_ref[...] = v2

    sh = jax.ShapeDtypeStruct((nl, r, c), F32)
    g_specs = [g_spec(li, True) for li in range(nl)] + [g_spec(li, False) for li in range(nl)]
    return _call(body, name=name, grid=(nl, 2, nt), in_specs=[w_spec] * 3 + g_specs, out_specs=[w_spec] * 4,
                 out_shape=[sh] * 4, scratch_shapes=[], args=(w, m, v, *own, *other), prefetch=(place,))


def _adamw_small(w, m, v, g_all, name):
    r, c = w.shape

    def body(w_ref, m_ref, v_ref, g_ref, go_ref, d_ref, m2_ref, v2_ref):
        g = g_ref[0]
        for b in range(1, N_DEV):
            g = g + g_ref[b]
        delta, m2, v2 = _adam_math(w_ref[...], g, m_ref[...], v_ref[...])
        go_ref[...] = g
        d_ref[...] = delta
        m2_ref[...] = m2
        v2_ref[...] = v2

    sh = jax.ShapeDtypeStruct((r, c), F32)
    return pl.pallas_call(body, out_shape=[sh] * 4, name=name, compiler_params=_cparams())(w, m, v, g_all)


def _me():
    return lax.axis_index("x"), lax.axis_index("y"), lax.axis_index("c")


def _flip(v, bit):
    return 1 - v if bit else v


def _allgather_small(blk, name):
    r, c = blk.shape

    def body(x_ref, o_ref, send_sems, recv_sems):
        x, y, cc = _me()
        me = 4 * x + 2 * y + cc
        copies = []
        for k in range(1, N_DEV):
            peer = (_flip(x, k & 4), _flip(y, k & 2), _flip(cc, k & 1))
            cp = pltpu.make_async_remote_copy(src_ref=x_ref, dst_ref=o_ref.at[me], send_sem=send_sems.at[k - 1],
                                              recv_sem=recv_sems.at[k - 1], device_id=peer, device_id_type=MESH)
            cp.start()
            copies.append(cp)
        o_ref[me] = x_ref[...]
        for cp in copies:
            cp.wait()

    return pl.pallas_call(
        body, out_shape=jax.ShapeDtypeStruct((N_DEV, r, c), F32),
        in_specs=[pl.BlockSpec(memory_space=pltpu.VMEM)], out_specs=pl.BlockSpec(memory_space=pltpu.VMEM),
        scratch_shapes=[pltpu.SemaphoreType.DMA((N_DEV - 1,)), pltpu.SemaphoreType.DMA((N_DEV - 1,))],
        name=name, compiler_params=_cparams(),
    )(blk)


class _Rider:
    def __init__(self, arrays, out_shapes, scratch_shapes, start, finish):
        self.arrays, self.out_shapes, self.scratch_shapes = list(arrays), list(out_shapes), list(scratch_shapes)
        self.start, self.finish = start, finish


def _call(body, *, name, grid, in_specs, out_specs, out_shape, scratch_shapes, args, rider=None, prefetch=()):
    npf = len(prefetch)

    def launch(fn, in_specs, out_specs, out_shape, scratch_shapes, args):
        grid_spec = pltpu.PrefetchScalarGridSpec(num_scalar_prefetch=npf, grid=grid, in_specs=in_specs,
                                                 out_specs=out_specs, scratch_shapes=scratch_shapes)
        return pl.pallas_call(fn, grid_spec=grid_spec, out_shape=out_shape, name=name,
                              compiler_params=_cparams())(*prefetch, *args)

    if rider is None:
        return launch(body, list(in_specs), list(out_specs), list(out_shape), list(scratch_shapes), args)
    ni, no, ns = len(in_specs), len(out_specs), len(scratch_shapes)
    ri, ro = len(rider.arrays), len(rider.out_shapes)
    steps = int(np.prod(grid))

    def wrapped(*refs):
        pf, refs = refs[:npf], refs[npf:]
        h_in, r_in = refs[:ni], refs[ni:ni + ri]
        h_out, r_out = refs[ni + ri:ni + ri + no], refs[ni + ri + no:ni + ri + no + ro]
        h_scr, r_scr = refs[ni + ri + no + ro:ni + ri + no + ro + ns], refs[ni + ri + no + ro + ns:]
        step = pl.program_id(0)
        for d in range(1, len(grid)):
            step = step * grid[d] + pl.program_id(d)

        @pl.when(step == 0)
        def _():
            rider.start(r_in, r_out, r_scr)

        body(*pf, *h_in, *h_out, *h_scr)

        @pl.when(step == steps - 1)
        def _():
            rider.finish(r_in, r_out, r_scr)

    anyspec = pl.BlockSpec(memory_space=pl.ANY)
    res = launch(wrapped, list(in_specs) + [anyspec] * ri, list(out_specs) + [anyspec] * ro,
                 list(out_shape) + rider.out_shapes, list(scratch_shapes) + rider.scratch_shapes,
                 list(args) + rider.arrays)
    return res[:no], res[no:]


def _run_rider(rider, name):
    ri = len(rider.arrays)

    def body(*refs):
        r_in, r_out, r_scr = refs[:ri], refs[ri:ri + len(rider.out_shapes)], refs[ri + len(rider.out_shapes):]
        rider.start(r_in, r_out, r_scr)
        rider.finish(r_in, r_out, r_scr)

    anyspec = pl.BlockSpec(memory_space=pl.ANY)
    return pl.pallas_call(body, in_specs=[anyspec] * ri, out_specs=[anyspec] * len(rider.out_shapes),
                          out_shape=rider.out_shapes, scratch_shapes=rider.scratch_shapes, name=name,
                          compiler_params=_cparams())(*rider.arrays)


def _allgather_rider(blk):
    def copies(ins, outs, scr):
        send_sems, recv_sems, loc_sems, stage = scr
        x, y, cc = _me()
        me = 4 * x + 2 * y + cc
        remote = [pltpu.make_async_remote_copy(
            src_ref=ins[0], dst_ref=outs[0].at[me], send_sem=send_sems.at[k - 1], recv_sem=recv_sems.at[k - 1],
            device_id=(_flip(x, k & 4), _flip(y, k & 2), _flip(cc, k & 1)), device_id_type=MESH) for k in range(1, N_DEV)]
        return remote, pltpu.make_async_copy(ins[0], stage, loc_sems.at[0]), (outs[0].at[me], stage, loc_sems.at[1])

    def start(ins, outs, scr):
        remote, lin, _ = copies(ins, outs, scr)
        lin.start()
        for cp in remote:
            cp.start()

    def finish(ins, outs, scr):
        remote, lin, (dst, stage, sem) = copies(ins, outs, scr)
        lin.wait()
        lout = pltpu.make_async_copy(stage, dst, sem)
        lout.start()
        for cp in remote:
            cp.wait()
        lout.wait()

    return _Rider([blk], [jax.ShapeDtypeStruct((N_DEV,) + blk.shape, blk.dtype)],
                  [pltpu.SemaphoreType.DMA((N_DEV - 1,)), pltpu.SemaphoreType.DMA((N_DEV - 1,)),
                   pltpu.SemaphoreType.DMA((2,)), pltpu.VMEM(blk.shape, blk.dtype)], start, finish)


def _gather_rider(shards):
    n = len(shards)

    def copies(ins, outs, scr, relay=True):
        ici_send, ici_recv, d2d_send, d2d_recv, loc_sems = scr[:5]
        stage = scr[5:]
        x, y, cc = _me()
        chip = 2 * x + y
        sibling = (x, y, 1 - cc)
        local, sends, relays = [], [], []
        for j in range(n):
            def rows(ch, h, j=j):
                return outs[j].at[ch, h]

            lc = pltpu.make_async_copy(ins[j], stage[j], loc_sems.at[j])
            local.append((lc, pltpu.make_async_copy(stage[j], outs[j].at[chip], loc_sems.at[n + j]) if relay else None))
            for k in range(1, N_CHIP):
                px, py = _flip(x, k & 2), _flip(y, k & 1)
                pchip = 2 * px + py
                q = 3 * j + k - 1
                out_cp = pltpu.make_async_remote_copy(src_ref=ins[j].at[cc], dst_ref=rows(chip, cc),
                                                      send_sem=ici_send.at[q], recv_sem=ici_recv.at[q],
                                                      device_id=(px, py, cc), device_id_type=MESH)
                sends.append(out_cp)
                if not relay:
                    continue
                arrival = pltpu.make_async_remote_copy(src_ref=rows(pchip, cc), dst_ref=rows(pchip, cc),
                                                       send_sem=ici_send.at[q], recv_sem=ici_recv.at[q],
                                                       device_id=(px, py, cc), device_id_type=MESH)
                forward = pltpu.make_async_remote_copy(src_ref=rows(pchip, cc), dst_ref=rows(pchip, cc),
                                                       send_sem=d2d_send.at[q], recv_sem=d2d_recv.at[q],
                                                       device_id=sibling, device_id_type=MESH)
                from_sibling = pltpu.make_async_remote_copy(src_ref=rows(pchip, 1 - cc), dst_ref=rows(pchip, 1 - cc),
                                                            send_sem=d2d_send.at[q], recv_sem=d2d_recv.at[q],
                                                            device_id=sibling, device_id_type=MESH)
                relays.append((arrival, forward, from_sibling))
        return local, sends, relays

    def start(ins, outs, scr):
        local, sends, _ = copies(ins, outs, scr, relay=False)
        for lin, _ in local:
            lin.start()
        for cp in sends:
            cp.start()

    def finish(ins, outs, scr):
        local, sends, relays = copies(ins, outs, scr)
        for lin, lout in local:
            lin.wait()
            lout.start()
        for arrival, forward, _ in relays:
            arrival.wait_recv()
            forward.start()
        for cp in sends:
            cp.wait_send()
        for _, forward, from_sibling in relays:
            forward.wait_send()
            from_sibling.wait_recv()
        for _, lout in local:
            lout.wait()

    scratch = [pltpu.SemaphoreType.DMA((3 * n,)), pltpu.SemaphoreType.DMA((3 * n,)), pltpu.SemaphoreType.DMA((3 * n,)),
               pltpu.SemaphoreType.DMA((3 * n,)), pltpu.SemaphoreType.DMA((2 * n,))]
    scratch += [pltpu.VMEM(a.shape, a.dtype) for a in shards]
    return _Rider(shards, [jax.ShapeDtypeStruct((N_CHIP,) + a.shape, a.dtype) for a in shards], scratch, start, finish)


def _sibling_rider(arrs, other_half=False):
    n = len(arrs)

    def copies(ins, outs, scr):
        send_sems, recv_sems = scr
        x, y, cc = _me()
        return [pltpu.make_async_remote_copy(
            src_ref=ins[j].at[1 - cc] if other_half else ins[j], dst_ref=outs[j], send_sem=send_sems.at[j],
            recv_sem=recv_sems.at[j], device_id=(x, y, 1 - cc), device_id_type=MESH) for j in range(n)]

    def start(ins, outs, scr):
        for cp in copies(ins, outs, scr):
            cp.start()

    def finish(ins, outs, scr):
        for cp in copies(ins, outs, scr):
            cp.wait()

    return _Rider(arrs, [jax.ShapeDtypeStruct(a.shape[1:] if other_half else a.shape, a.dtype) for a in arrs],
                  [pltpu.SemaphoreType.DMA((n,)), pltpu.SemaphoreType.DMA((n,))], start, finish)


def _sibling_send(arrs, name, other_half=False):
    return _run_rider(_sibling_rider(arrs, other_half), name)


def _join_riders(first, second):
    ni, no, ns = len(first.arrays), len(first.out_shapes), len(first.scratch_shapes)

    def split(ins, outs, scr):
        return (ins[:ni], outs[:no], scr[:ns]), (ins[ni:], outs[no:], scr[ns:])

    def start(ins, outs, scr):
        a, b = split(ins, outs, scr)
        first.start(*a)
        second.start(*b)

    def finish(ins, outs, scr):
        a, b = split(ins, outs, scr)
        first.finish(*a)
        second.finish(*b)

    return _Rider(first.arrays + second.arrays, first.out_shapes + second.out_shapes,
                  first.scratch_shapes + second.scratch_shapes, start, finish)


def _scatter_rider(arrs):
    n = len(arrs)

    def copies(ins, outs, scr):
        send_sems, recv_sems = scr
        x, y, cc = _me()
        cps = []
        for j in range(n):
            for k in range(1, N_CHIP):
                px, py = _flip(x, k & 2), _flip(y, k & 1)
                cps.append(pltpu.make_async_remote_copy(
                    src_ref=ins[j].at[2 * px + py], dst_ref=outs[j].at[k - 1], send_sem=send_sems.at[3 * j + k - 1],
                    recv_sem=recv_sems.at[3 * j + k - 1], device_id=(px, py, cc), device_id_type=MESH))
        return cps

    def start(ins, outs, scr):
        for cp in copies(ins, outs, scr):
            cp.start()

    def finish(ins, outs, scr):
        for cp in copies(ins, outs, scr):
            cp.wait()

    return _Rider(arrs, [jax.ShapeDtypeStruct((N_CHIP - 1,) + a.shape[1:], a.dtype) for a in arrs],
                  [pltpu.SemaphoreType.DMA((3 * n,)), pltpu.SemaphoreType.DMA((3 * n,))], start, finish)


COL_SHARDED = ("w_in", "w_br_pool", "w_br_attn", "w_br_conv", "w_ff1")
ROW_SHARDED = ("w_o", "w_ff2")
BIG = COL_SHARDED + ROW_SHARDED
SMALL = ("b_ada", "b_gate", "w_pool", "pool_scale", "rel_bias", "conv_w", "conv_b", "conv_ln_g", "conv_ln_b",
         "ln_mix_g", "ln_mix_b", "b_ff1", "b_ff2", "ln_ff_g", "ln_ff_b")
PACK_W = 1024


def _pack(parts):
    rows = []
    for a in parts:
        flat = a.reshape(-1)
        n = -(-flat.shape[0] // PACK_W) * PACK_W
        rows.append(jnp.pad(flat, (0, n - flat.shape[0])).reshape(-1, PACK_W))
    out = jnp.concatenate(rows, axis=0)
    r = -(-out.shape[0] // 8) * 8
    return jnp.pad(out, ((0, r - out.shape[0]), (0, 0)))


def _unpack(packed, shapes):
    out, r0 = [], 0
    for shp in shapes:
        size = int(np.prod(shp))
        nr = -(-size // PACK_W)
        out.append(packed[r0:r0 + nr].reshape(-1)[:size].reshape(shp))
        r0 += nr
    return out


def _hosted(fn, hook, *args, **kw):
    if hook is None:
        return fn(*args, **kw)
    res, rider_out = fn(*args, rider=hook[0], **kw)
    hook[1](rider_out)
    return res


def _layer_fwd(l, x, mod, W, P, hooks=None):
    hooks = hooks or {}
    s = x.shape[0]
    sh_m, sc_m, g_m, sh_f, sc_f, g_f = [mod[l:l + 1, D_MODEL * j:D_MODEL * (j + 1)] for j in range(6)]
    n = lambda t: f"{t}{l}"
    w_in = W["w_in"][l]
    u = _ln_mod(x, sc_m, sh_m, n("ln_mod_mix"))
    tmz = min(1024, s)
    zp = _mm(u, w_in, "nt", tm=min(2048, s), tn=256, out_dtype=F32, name=n("z_pool"), b_col0=0, n_out=D_POOL)
    qkv = _mm(u, w_in, "nt", tm=tmz, tn=256, out_dtype=BF16, name=n("z_qkv"), b_col0=OFF_QKV // 256, n_out=3 * D_ATTN)
    zc = _mm(u, w_in, "nt", tm=tmz, tn=256, out_dtype=F32, name=n("z_conv"), b_col0=OFF_CONV // 256, n_out=2 * D_CONV)
    zg = _hosted(_mm, hooks.get("z_gate"), u, w_in, "nt", tm=tmz, tn=768, out_dtype=BF16, name=n("z_gate"),
                 b_col0=OFF_GATE // 768, n_out=3 * D_MODEL)

    p, feat_pool = _pool_fwd(zp, P["wp_bd"][l], P["pool_scale"][l], n("pool_fwd"))
    bias = _bias_block(P["rel_bias"][l], n("bias_block"))
    o = _hosted(_attn_fwd, hooks.get("attn"), qkv, bias, n("attn_fwd"))
    cv, feat_conv = _conv_fwd(zc, P["conv_w"][l], P["conv_b"][l], P["conv_ln_g"][l], P["conv_ln_b"][l], n("conv_fwd"))

    tmb = min(1024, s)
    y_pool = _mm(feat_pool, W["w_br_pool"][l], "nt", tm=tmb, tn=1024, out_dtype=F32, name=n("y_pool"))
    y_attn = _mm(o, W["w_br_attn"][l], "nt", tm=tmb, tn=1024, out_dtype=F32, name=n("y_attn"))
    y_conv = _mm(feat_conv, W["w_br_conv"][l], "nt", tm=tmb, tn=1024, out_dtype=F32, name=n("y_conv"))
    ys = (y_pool, y_attn, y_conv)
    merged = _merge(zg, P["b_gate"][l], ys, n("merge"))
    mix, x1 = _mm_resid_ln(merged, W["w_o"][l], None, x, g_m, P["ln_mix_g"][l], P["ln_mix_b"][l], n("mix_out"))

    u2 = _ln_mod(x1, sc_f, sh_f, n("ln_mod_ff"))
    hpre, hid = _hosted(_ff_hidden, hooks.get("ff1"), u2, W["w_ff1"][l], P["b_ff1"][l], n("ff1"))
    ff, x2 = _hosted(_mm_resid_ln, hooks.get("ff2"), hid, W["w_ff2"][l], P["b_ff2"][l], x1, g_f, P["ln_ff_g"][l],
                     P["ln_ff_b"][l], n("ff2"))
    saved = dict(x=x, u=u, zp=zp, qkv=qkv, zc=zc, zg=zg, p=p, feat_pool=feat_pool, bias=bias, o=o, cv=cv,
                 feat_conv=feat_conv, ys=ys, merged=merged, mix=mix, x1=x1, u2=u2, hpre=hpre, hid=hid, ff=ff)
    return x2, saved


def _layer_bwd(l, dx2, mod, W, P, A, hooks=None):
    hooks = hooks or {}
    s = dx2.shape[0]
    sh_m, sc_m, g_m, sh_f, sc_f, g_f = [mod[l:l + 1, D_MODEL * j:D_MODEL * (j + 1)] for j in range(6)]
    n = lambda t: f"{t}{l}"
    tmb = min(1024, s)
    gw, gs = {}, {}

    dres, dff, gs["ln_ff_g"], gs["ln_ff_b"], dg_f, gs["b_ff2"] = _resid_ln_bwd(
        dx2, A["x1"], A["ff"], g_f, P["ln_ff_g"][l], n("resid_ln_ff_bwd"))
    gw["w_ff2"] = _mm(A["hid"], dff, "tn", tm=512, tn=1024, out_dtype=BF16, name=n("dw_ff2"), split_n=512)
    dhpre, gs["b_ff1"] = _ff_hidden_bwd(dff, W["w_ff2"][l], A["hpre"], n("ff_hidden_bwd"))
    gw["w_ff1"] = _mm(dhpre, A["u2"], "tn", tm=512, tn=1024, out_dtype=BF16, name=n("dw_ff1"), split_n=512)
    dx1, dsc_f, dsh_f = _mm_ln_mod_bwd(dhpre, W["w_ff1"][l], A["x1"], sc_f, dres, n("du_ff"))

    dres, dmix, gs["ln_mix_g"], gs["ln_mix_b"], dg_m, _ = _resid_ln_bwd(
        dx1, A["x"], A["mix"], g_m, P["ln_mix_g"][l], n("resid_ln_mix_bwd"))
    gw["w_o"] = _mm(A["merged"], dmix, "tn", tm=512, tn=1024, out_dtype=BF16, name=n("dw_o"), split_n=512)
    dmerged = _mm(dmix, W["w_o"][l], "nt", tm=tmb, tn=1024, out_dtype=F32, name=n("d_merged"))
    dy_pool, dy_attn, dy_conv, dzg, gs["b_gate"] = _merge_bwd(dmerged, A["zg"], P["b_gate"][l], A["ys"], n("merge_bwd"))

    gw["w_br_pool"] = _mm(dy_pool, A["feat_pool"], "tn", tm=512, tn=256, out_dtype=BF16, name=n("dw_br_pool"),
                          split_n=128)
    gw["w_br_attn"] = _mm(dy_attn, A["o"], "tn", tm=512, tn=512, out_dtype=BF16, name=n("dw_br_attn"), split_n=256)
    gw["w_br_conv"] = _mm(dy_conv, A["feat_conv"], "tn", tm=512, tn=256, out_dtype=BF16, name=n("dw_br_conv"),
                          split_n=128)
    dfeat_pool = _mm(dy_pool, W["w_br_pool"][l], "nn", tm=tmb, tn=256, out_dtype=F32, name=n("d_feat_pool"))
    do = _mm(dy_attn, W["w_br_attn"][l], "nn", tm=tmb, tn=512, out_dtype=BF16, name=n("d_attn_out"))
    dfeat_conv = _mm(dy_conv, W["w_br_conv"][l], "nn", tm=tmb, tn=256, out_dtype=F32, name=n("d_feat_conv"))

    dzp, dwp_bd, gs["pool_scale"] = _pool_bwd(dfeat_pool, A["p"], P["wp_bd"][l], P["pool_scale"][l], n("pool_bwd"))
    gs["w_pool"] = jnp.stack([dwp_bd[POOL_GROUP * g:POOL_GROUP * (g + 1), POOL_GROUP * g:POOL_GROUP * (g + 1)]
                              for g in range(len(POOL_WINDOWS))])
    hook = hooks["attn"](gw) if "attn" in hooks else None
    dq, dk, dv, ds_acc = _hosted(_attn_bwd, hook, A["qkv"], do, A["bias"], n("attn_bwd"))
    gs["rel_bias"] = _bias_block_bwd(ds_acc, n("bias_block_bwd"))
    dzc, dcw, gs["conv_b"], gs["conv_ln_g"], gs["conv_ln_b"] = _conv_bwd(
        dfeat_conv, A["cv"], A["zc"], P["conv_w"][l], P["conv_ln_g"][l], P["conv_ln_b"][l], n("conv_bwd"))
    gs["conv_w"] = dcw[:CONV_WIDTH]

    dz = [dzp, dq, dk, dv, dzc, dzg]
    gw["w_in"] = _dw_segments(dz, A["u"], n("dw_in"))
    hook = hooks["du_mix"](gw) if "du_mix" in hooks else None
    dx, dsc_m, dsh_m = _hosted(_mm_ln_mod_bwd, hook, dz, W["w_in"][l], A["x"], sc_m, dres, n("du_mix"))
    dmod = jnp.concatenate([dsh_m, dsc_m, dg_m, dsh_f, dsc_f, dg_f], axis=1)
    return dx, gw, gs, dmod


def _small_shapes():
    return {"b_ada": (6 * D_MODEL,), "b_gate": (3 * D_MODEL,), "w_pool": (4, POOL_GROUP, POOL_GROUP),
            "pool_scale": (D_POOL,), "rel_bias": (N_HEADS, N_REL), "conv_w": (CONV_WIDTH, D_CONV),
            "conv_b": (D_CONV,), "conv_ln_g": (D_CONV,), "conv_ln_b": (D_CONV,), "ln_mix_g": (D_MODEL,),
            "ln_mix_b": (D_MODEL,), "b_ff1": (D_FF,), "b_ff2": (D_MODEL,), "ln_ff_g": (D_MODEL,), "ln_ff_b": (D_MODEL,)}


def kernel(x, c, w_ada, b_ada, w_in, b_gate, w_pool, pool_scale, rel_bias, conv_w, conv_b, conv_ln_g, conv_ln_b, w_br_pool, w_br_attn, w_br_conv, w_o, ln_mix_g, ln_mix_b, w_ff1, b_ff1, w_ff2, b_ff2, ln_ff_g, ln_ff_b, loss_target, m_w_ada, m_b_ada, m_w_in, m_b_gate, m_w_pool, m_pool_scale, m_rel_bias, m_conv_w, m_conv_b, m_conv_ln_g, m_conv_ln_b, m_w_br_pool, m_w_br_attn, m_w_br_conv, m_w_o, m_ln_mix_g, m_ln_mix_b, m_w_ff1, m_b_ff1, m_w_ff2, m_b_ff2, m_ln_ff_g, m_ln_ff_b, v_w_ada, v_b_ada, v_w_in, v_b_gate, v_w_pool, v_pool_scale, v_rel_bias, v_conv_w, v_conv_b, v_conv_ln_g, v_conv_ln_b, v_w_br_pool, v_w_br_attn, v_w_br_conv, v_w_o, v_ln_mix_g, v_ln_mix_b, v_w_ff1, v_b_ff1, v_w_ff2, v_b_ff2, v_ln_ff_g, v_ln_ff_b):
    env = dict(locals())
    xi, yi, ci = _me()
    chip = 2 * xi + yi
    me = 4 * xi + 2 * yi + ci
    xs = x[0]
    tgt = loss_target[0]
    L = DEPTH

    first = _allgather_small(jnp.concatenate([c.reshape(8, 128), _pack([conv_w]).reshape(-1, 128)]), "gather_c_conv_w")
    c_all = first[:, :8].reshape(N_DEV, D_MODEL)
    ada_cols = w_ada.shape[2]
    b_ada_sh = lax.dynamic_slice_in_dim(b_ada, chip * ada_cols, ada_cols, axis=1).reshape(L, 1, ada_cols)
    mod_part = _mod_fwd(c_all, w_ada, b_ada_sh, "mod_fwd")
    mod_g = _allgather_small(mod_part.reshape(-1, 128), "gather_mod").reshape(N_CHIP, 2, L, N_DEV, ada_cols)[:, 0]
    mod_all = jnp.transpose(mod_g, (1, 2, 0, 3)).reshape(L, N_DEV, 6 * D_MODEL)
    mod = lax.dynamic_index_in_dim(mod_all, me, axis=1, keepdims=False)

    W = {k: [None] * L for k in BIG}

    def weight_gather(names, l):
        shards = [(jnp.swapaxes(env[k][l], 0, 1) if k in COL_SHARDED else env[k][l]).astype(BF16) for k in names]
        shards = [a.reshape(2, a.shape[0] // 2, a.shape[1]) for a in shards]

        def done(outs):
            for k, g in zip(names, outs):
                W[k][l] = g.reshape(-1, g.shape[-1])

        return _gather_rider(shards), done

    branch_names = ("w_br_pool", "w_br_attn", "w_br_conv", "w_o")
    late_names = ("w_ff1", "w_ff2")
    rider, done = weight_gather(("w_in",), 0)
    done(_run_rider(rider, "gather_w_in0"))
    fwd_hooks = [{"z_gate": weight_gather(branch_names, 0), "attn": weight_gather(late_names, 0),
                  "ff1": weight_gather(("w_in",), 1), "ff2": weight_gather(branch_names, 1)},
                 {"attn": weight_gather(late_names, 1)}]

    P = {k: env[k] for k in ("rel_bias", "conv_w")}
    for k in ("b_gate", "pool_scale", "conv_b", "conv_ln_g", "conv_ln_b", "ln_mix_g", "ln_mix_b", "b_ff1", "b_ff2",
              "ln_ff_g", "ln_ff_b"):
        P[k] = env[k].reshape(L, 1, -1)
    n_cw = conv_w.size
    cw = first[:, 8:].reshape(N_CHIP, 2, -1)[:, 0, :n_cw].reshape(N_CHIP, L, CONV_WIDTH, D_CONV // N_CHIP)
    P["conv_w"] = jnp.transpose(cw, (1, 2, 0, 3)).reshape(L, CONV_WIDTH, D_CONV)
    wp_bd = jnp.zeros((L, D_POOL, D_POOL), F32)
    for g in range(len(POOL_WINDOWS)):
        sl = slice(POOL_GROUP * g, POOL_GROUP * (g + 1))
        wp_bd = wp_bd.at[:, sl, sl].set(w_pool[:, g])
    P["wp_bd"] = wp_bd.astype(BF16)

    acts = []
    h = xs
    for l in range(L):
        h, saved = _layer_fwd(l, h, mod, W, P, fwd_hooks[l])
        acts.append(saved)
    dy, loss_part = _loss_grad(h, tgt, "loss_grad")
    loss = lax.psum(loss_part[0, 0], ("x", "y", "c"))

    place = jnp.stack([ci, chip, chip ^ 1, chip ^ 2, chip ^ 3]).astype(jnp.int32)
    scattered = {}

    def grad_scatter(items, tag):
        dws = [dw for _, _, dw in items]
        got = _sibling_send(dws, f"swap_blocks_{tag}", other_half=True)
        both = [_sum_cores(a, b, place, f"sum_cores_{k}{l}") for (k, l, _), a, b in zip(items, dws, got)]
        both = [hh.reshape(N_CHIP, -1, hh.shape[-1]) for hh in both]

        def done(outs):
            for (k, l, _), hh, r in zip(items, both, outs):
                scattered[(k, l)] = (hh, r)

        return _scatter_rider(both), done

    early = ("w_ff2", "w_ff1", "w_o", "w_br_pool", "w_br_attn", "w_br_conv")
    left_over = []

    def attn_hook(l):
        def hook(gw):
            items = left_over + [(k, l, gw[k]) for k in early]
            left_over.clear()
            return grad_scatter(items, f"attn{l}")
        return hook

    def last_hook(gw):
        return grad_scatter([("w_in", 0, gw["w_in"])], "last")

    gws, gss, dmods = [None] * L, [None] * L, [None] * L
    dh = dy
    for l in reversed(range(L)):
        hooks = {"attn": attn_hook(l)}
        if l == 0:
            hooks["du_mix"] = last_hook
        dh, gws[l], gss[l], dmods[l] = _layer_bwd(l, dh, mod, W, P, acts[l], hooks)
        if l > 0:
            left_over.append(("w_in", l, gws[l]["w_in"]))
    grad_x = dh[None]

    reduced = [[_sum_chips(*scattered[(k, l)], place, f"sum_chips_{k}{l}") for l in range(L)] for k in BIG]
    flat_reduced = [t for per_weight in reduced for t in per_weight]

    shapes = _small_shapes()
    small_names = [k for k in SMALL if k != "b_ada"]
    dmod_own = jnp.concatenate(dmods, axis=0)
    pack = _pack([dmod_own] + [jnp.stack([gss[l][k].reshape(shapes[k]) for l in range(L)]) for k in small_names])
    last = _run_rider(_join_riders(_sibling_rider(flat_reduced), _allgather_rider(pack.reshape(-1, 128))),
                      "swap_reduced_gather_small")
    flat_other, g_all = last[:-1], last[-1].reshape(N_DEV, -1, PACK_W)

    out = {}
    for j, k in enumerate(BIG):
        own, other = reduced[j], flat_other[L * j:L * (j + 1)]
        if k == "w_in":
            t = lambda a: jnp.swapaxes(a, 1, 2)
            res = _adamw_halves(t(env[k]), t(env["m_" + k]), t(env["v_" + k]), own, other, place, "cols", f"adamw_{k}")
            res = [t(a) for a in res]
        else:
            if k in COL_SHARDED:
                own, other = [a.T for a in own], [a.T for a in other]
            res = _adamw_halves(env[k], env["m_" + k], env["v_" + k], own, other, place,
                                "rows" if k in COL_SHARDED else "cols", f"adamw_{k}")
        out[k] = tuple(res)

    dmod_all = g_all[:, :L * 6].reshape(N_DEV, L, 6 * D_MODEL)
    dmod_sh = jnp.transpose(lax.dynamic_slice_in_dim(dmod_all, chip * ada_cols, ada_cols, axis=2), (1, 0, 2))
    g_ada = _mod_bwd(c_all, dmod_sh, "mod_bwd")
    g_, d_, m_, v_ = _adamw(w_ada.reshape(-1, ada_cols), m_w_ada.reshape(-1, ada_cols), v_w_ada.reshape(-1, ada_cols),
                            [g_ada.reshape(-1, ada_cols)], "adamw_w_ada")
    out["w_ada"] = tuple(a.reshape(w_ada.shape) for a in (g_, d_, m_, v_))

    def small_pack(prefix):
        parts = [env[prefix + "b_ada"]]
        for k in small_names:
            a = env[prefix + k]
            if k == "conv_w":
                a = jnp.zeros((L,) + shapes[k], F32)
            parts.append(a)
        return _pack(parts)

    gp, dp, mp, vp = _adamw_small(small_pack(""), small_pack("m_"), small_pack("v_"), g_all, "adamw_small")
    full_shapes = [(L,) + shapes["b_ada"]] + [(L,) + shapes[k] for k in small_names]
    for tag, packed in (("g", gp), ("d", dp), ("m", mp), ("v", vp)):
        for k, a in zip(["b_ada"] + small_names, _unpack(packed, full_shapes)):
            out.setdefault(k, {})
            out[k][tag] = a
    g_cw_full = out["conv_w"]["g"]
    cw_cols = D_CONV // N_CHIP
    g_cw = lax.dynamic_slice_in_dim(g_cw_full, chip * cw_cols, cw_cols, axis=2)
    pad_rows = lambda a: jnp.pad(a.reshape(L * CONV_WIDTH, cw_cols), ((0, 2), (0, 0)))
    g_, d_, m_, v_ = _adamw(pad_rows(conv_w), pad_rows(m_conv_w), pad_rows(v_conv_w), [pad_rows(g_cw)], "adamw_conv_w")
    out["conv_w"] = tuple(a[:L * CONV_WIDTH].reshape(L, CONV_WIDTH, cw_cols) for a in (g_, d_, m_, v_))

    names = ["w_ada", "b_ada", "w_in", "b_gate", "w_pool", "pool_scale", "rel_bias", "conv_w", "conv_b", "conv_ln_g",
             "conv_ln_b", "w_br_pool", "w_br_attn", "w_br_conv", "w_o", "ln_mix_g", "ln_mix_b", "w_ff1", "b_ff1",
             "w_ff2", "b_ff2", "ln_ff_g", "ln_ff_b"]

    def pick(k, i):
        o = out[k]
        return o[i] if isinstance(o, tuple) else o["gdmv"[i]].reshape(env[k].shape)

    return (loss, grad_x, *[pick(k, 0) for k in names], *[pick(k, 1) for k in names],
            *[pick(k, 2) for k in names], *[pick(k, 3) for k in names])
```

```python
import functools

import jax
import jax.numpy as jnp
import numpy as np
from jax import lax
from jax.experimental import pallas as pl
from jax.experimental.pallas import tpu as pltpu

F32 = jnp.float32
BF16 = jnp.bfloat16

D_MODEL = 1024
DEPTH = 2
CHUNK = 64
POOL_WINDOWS = (2, 4, 8, 16)
POOL_GROUP = 64
D_POOL = 256
N_HEADS = 8
HEAD_DIM = 64
D_ATTN = 512
N_PREV_CHUNKS = 8
REL_CLIP = 128
N_REL = 2 * REL_CLIP + 1
D_CONV = 256
CONV_WIDTH = 31
D_FF = 4 * D_MODEL
D_IN = 5376
OFF_POOL, OFF_QKV, OFF_CONV, OFF_GATE = 0, 256, 1792, 2304
ALPHA = (2.0 * DEPTH) ** 0.25
LN_EPS = 1e-5
NEG_INF = -1e30
ADAM_LR, ADAM_B1, ADAM_B2, ADAM_EPS, ADAM_WD, ADAM_STEP = 0.001, 0.9, 0.999, 1e-08, 0.01, 10

N_DEV = 8
N_CHIP = 4
MESH = pl.DeviceIdType.MESH

QB = 2 * CHUNK
KPAD = N_PREV_CHUNKS * CHUNK
KW = QB + KPAD
SKEW_W = 768

VMEM_LIMIT = 56 * 1024 * 1024


def _cparams(**kw):
    return pltpu.CompilerParams(vmem_limit_bytes=VMEM_LIMIT, **kw)


def _full(shape):
    n = len(shape)
    return pl.BlockSpec(shape, lambda *_: (0,) * n)


_DIMS = {"nn": (((1,), (0,)), ((), ())), "nt": (((1,), (1,)), ((), ())), "tn": (((0,), (0,)), ((), ()))}


def _relu2(t):
    r = jnp.maximum(t, 0.0)
    return r * r


def _mm(a, b, mode, *, tm, tn, out_dtype, name, b_col0=0, n_out=None, bias=None, split_n=0, rider=None):
    if mode == "tn":
        k, m = a.shape
        n = b.shape[1] if n_out is None else n_out
        a_spec = pl.BlockSpec((k, tm), lambda i, j: (0, i))
        b_spec = pl.BlockSpec((k, tn), lambda i, j: (0, j + b_col0))
    elif mode == "nn":
        m, k = a.shape
        n = b.shape[1] if n_out is None else n_out
        a_spec = pl.BlockSpec((tm, k), lambda i, j: (i, 0))
        b_spec = pl.BlockSpec((k, tn), lambda i, j: (0, j + b_col0))
    else:
        m, k = a.shape
        n = b.shape[0] if n_out is None else n_out
        a_spec = pl.BlockSpec((tm, k), lambda i, j: (i, 0))
        b_spec = pl.BlockSpec((tn, k), lambda i, j: (j + b_col0, 0))
    assert m % tm == 0 and n % tn == 0, (name, m, n, tm, tn)
    dims = _DIMS[mode]

    def body(*refs):
        if bias is None:
            a_ref, b_ref, o_ref = refs
        else:
            a_ref, b_ref, bias_ref, o_ref = refs
        acc = lax.dot_general(a_ref[...].astype(BF16), b_ref[...].astype(BF16), dims, preferred_element_type=F32)
        if bias is not None:
            acc = acc + bias_ref[...]
        if split_n:
            for c in range(tn // split_n):
                o_ref[c] = acc[:, c * split_n:(c + 1) * split_n].astype(out_dtype)
        else:
            o_ref[...] = acc.astype(out_dtype)

    in_specs = [a_spec, b_spec]
    args = [a, b]
    if bias is not None:
        in_specs.append(pl.BlockSpec((1, tn), lambda i, j: (0, j)))
        args.append(bias)
    if split_n:
        out_spec = pl.BlockSpec((tn // split_n, tm, split_n), lambda i, j: (j, i, 0))
        out_shape = jax.ShapeDtypeStruct((n // split_n, m, split_n), out_dtype)
    else:
        out_spec = pl.BlockSpec((tm, tn), lambda i, j: (i, j))
        out_shape = jax.ShapeDtypeStruct((m, n), out_dtype)
    res = _call(body, name=name, grid=(m // tm, n // tn), in_specs=in_specs, out_specs=[out_spec],
                out_shape=[out_shape], scratch_shapes=[], args=args, rider=rider)
    return res[0] if rider is None else (res[0][0], res[1])


def _ln_hat(x):
    mu = jnp.mean(x, axis=-1, keepdims=True)
    xc = x - mu
    var = jnp.mean(xc * xc, axis=-1, keepdims=True)
    rstd = lax.rsqrt(var + LN_EPS)
    return xc * rstd, rstd


def _ln_hat_bwd(dhat, xhat, rstd):
    m1 = jnp.mean(dhat, axis=-1, keepdims=True)
    m2 = jnp.mean(dhat * xhat, axis=-1, keepdims=True)
    return rstd * (dhat - m1 - xhat * m2)


def _row_tile(s):
    return min(512, s)


def _acc_rows(ref, val, first):
    @pl.when(first)
    def _():
        ref[...] = jnp.zeros_like(ref)
    ref[...] += jnp.sum(val, axis=0, keepdims=True)


def _ln_mod(x, sc, sh, name):
    s, d = x.shape
    tm = _row_tile(s)

    def body(x_ref, sc_ref, sh_ref, u_ref):
        xhat, _ = _ln_hat(x_ref[...])
        u_ref[...] = (xhat * (1.0 + sc_ref[...]) + sh_ref[...]).astype(BF16)

    row = pl.BlockSpec((tm, d), lambda i: (i, 0))
    vec = pl.BlockSpec((1, d), lambda i: (0, 0))
    return pl.pallas_call(body, grid=(s // tm,), in_specs=[row, vec, vec], out_specs=row,
                          out_shape=jax.ShapeDtypeStruct((s, d), BF16), name=name, compiler_params=_cparams())(x, sc, sh)


def _mm_ln_mod_bwd(a, b, x, sc, dres, name, rider=None):
    segs = list(a) if isinstance(a, (list, tuple)) else [a]
    s = segs[0].shape[0]
    k, d = b.shape
    assert sum(t.shape[1] for t in segs) == k
    tm = min(512 if k <= 4096 else 256, s)
    ns = len(segs)

    def body(*refs):
        seg_refs = refs[:ns]
        b_ref, x_ref, sc_ref, dres_ref, dx_ref, dsc_ref, dsh_ref = refs[ns:]
        first = pl.program_id(0) == 0
        duv, r0 = None, 0
        for seg_ref in seg_refs:
            w = seg_ref.shape[1]
            part = jnp.dot(seg_ref[...], b_ref[r0:r0 + w, :], preferred_element_type=F32)
            duv = part if duv is None else duv + part
            r0 += w
        xhat, rstd = _ln_hat(x_ref[...])
        dx_ref[...] = dres_ref[...] + _ln_hat_bwd(duv * (1.0 + sc_ref[...]), xhat, rstd)
        _acc_rows(dsc_ref, duv * xhat, first)
        _acc_rows(dsh_ref, duv, first)

    row = pl.BlockSpec((tm, d), lambda i: (i, 0))
    vec = pl.BlockSpec((1, d), lambda i: (0, 0))
    vs = jax.ShapeDtypeStruct((1, d), F32)
    res = _call(body, name=name, grid=(s // tm,),
                in_specs=[pl.BlockSpec((tm, t.shape[1]), lambda i: (i, 0)) for t in segs] + [_full((k, d)), row, vec, row],
                out_specs=[row, vec, vec], out_shape=[jax.ShapeDtypeStruct((s, d), F32), vs, vs],
                scratch_shapes=[], args=(*segs, b, x, sc, dres), rider=rider)
    return tuple(res) if rider is None else (tuple(res[0]), res[1])


def _dw_segments(segs, u, name):
    s, d = u.shape
    tw = 256
    tiles = [t.shape[1] // tw for t in segs]
    starts = [sum(tiles[:j]) for j in range(len(segs))]
    ns = len(segs)

    def body(*refs):
        seg_refs, u_ref, o_ref = refs[:ns], refs[ns], refs[ns + 1]
        i = pl.program_id(0)
        for seg_ref, t0, nt in zip(seg_refs, starts, tiles):
            @pl.when((i >= t0) & (i < t0 + nt))
            def _(seg_ref=seg_ref):
                acc = lax.dot_general(seg_ref[...], u_ref[...], _DIMS["tn"], preferred_element_type=F32)
                o_ref[0] = acc[:, :d // 2].astype(BF16)
                o_ref[1] = acc[:, d // 2:].astype(BF16)

    def seg_spec(t0, nt):
        return pl.BlockSpec((s, tw), lambda i: (0, jnp.clip(i - t0, 0, nt - 1)))

    return pl.pallas_call(
        body, grid=(sum(tiles),), in_specs=[seg_spec(t0, nt) for t0, nt in zip(starts, tiles)] + [_full((s, d))],
        out_specs=pl.BlockSpec((2, tw, d // 2), lambda i: (0, i, 0)),
        out_shape=jax.ShapeDtypeStruct((2, sum(tiles) * tw, d // 2), BF16), name=name, compiler_params=_cparams(),
    )(*segs, u)


def _mm_resid_ln(a, b, bias, x, g, gam, bet, name, rider=None):
    s, k = a.shape
    d = b.shape[1]
    tm = min(512, s)

    def body(*refs):
        if bias is None:
            a_ref, b_ref, x_ref, g_ref, gam_ref, bet_ref, f_ref, o_ref = refs
        else:
            a_ref, b_ref, bias_ref, x_ref, g_ref, gam_ref, bet_ref, f_ref, o_ref = refs
        f = jnp.dot(a_ref[...], b_ref[...], preferred_element_type=F32)
        if bias is not None:
            f = f + bias_ref[...]
        f_ref[...] = f
        rhat, _ = _ln_hat(ALPHA * x_ref[...] + g_ref[...] * f)
        o_ref[...] = rhat * gam_ref[...] + bet_ref[...]

    row = pl.BlockSpec((tm, d), lambda i: (i, 0))
    vec = pl.BlockSpec((1, d), lambda i: (0, 0))
    in_specs = [pl.BlockSpec((tm, k), lambda i: (i, 0)), _full((k, d))] + ([vec] if bias is not None else []) + [row, vec, vec, vec]
    args = [a, b] + ([bias] if bias is not None else []) + [x, g, gam, bet]
    sh = jax.ShapeDtypeStruct((s, d), F32)
    res = _call(body, name=name, grid=(s // tm,), in_specs=in_specs, out_specs=[row, row], out_shape=[sh, sh],
                scratch_shapes=[], args=args, rider=rider)
    return tuple(res) if rider is None else (tuple(res[0]), res[1])


def _resid_ln_bwd(dxo, x, f, g, gam, name, tgt=None):
    s, d = x.shape
    tm = _row_tile(s)
    n = s // tm

    def body(*refs):
        if tgt is None:
            dxo_ref, x_ref, f_ref, g_ref, gam_ref, dres_ref, df_ref, dgam_ref, dbet_ref, dg_ref, dbias_ref = refs
            dxov = dxo_ref[...]
        else:
            (dxo_ref, t_ref, x_ref, f_ref, g_ref, gam_ref, dres_ref, df_ref, dgam_ref, dbet_ref, dg_ref, dbias_ref,
             loss_ref, sq_ref) = refs
            err = dxo_ref[...] - t_ref[...]
            dxov = err * (1.0 / d)
            _acc_rows(sq_ref, err * err, pl.program_id(0) == 0)

            @pl.when(pl.program_id(0) == n - 1)
            def _():
                tot = jnp.sum(sq_ref[...], axis=1, keepdims=True) * (0.5 / d)
                loss_ref[...] = jnp.broadcast_to(tot, (1, 128))

        first = pl.program_id(0) == 0
        fv = f_ref[...]
        rhat, rstd = _ln_hat(ALPHA * x_ref[...] + g_ref[...] * fv)
        dr = _ln_hat_bwd(dxov * gam_ref[...], rhat, rstd)
        dfv = g_ref[...] * dr
        dres_ref[...] = ALPHA * dr
        df_ref[...] = dfv.astype(BF16)
        _acc_rows(dgam_ref, dxov * rhat, first)
        _acc_rows(dbet_ref, dxov, first)
        _acc_rows(dg_ref, dr * fv, first)
        _acc_rows(dbias_ref, dfv, first)

    row = pl.BlockSpec((tm, d), lambda i: (i, 0))
    vec = pl.BlockSpec((1, d), lambda i: (0, 0))
    vs = jax.ShapeDtypeStruct((1, d), F32)
    out_specs = [row, row, vec, vec, vec, vec]
    out_shape = [jax.ShapeDtypeStruct((s, d), F32), jax.ShapeDtypeStruct((s, d), BF16), vs, vs, vs, vs]
    if tgt is None:
        return pl.pallas_call(body, grid=(n,), in_specs=[row, row, row, vec, vec], out_specs=out_specs,
                              out_shape=out_shape, name=name, compiler_params=_cparams())(dxo, x, f, g, gam)
    return pl.pallas_call(body, grid=(n,), in_specs=[row, row, row, row, vec, vec],
                          out_specs=out_specs + [pl.BlockSpec((1, 128), lambda i: (0, 0))],
                          out_shape=out_shape + [jax.ShapeDtypeStruct((1, 128), F32)],
                          scratch_shapes=[pltpu.VMEM((1, d), F32)], name=name,
                          compiler_params=_cparams())(dxo, tgt, x, f, g, gam)


POOL_HALO = 16
POOL_ROWS = 256


def _pool_counts(r0, rows):
    t1 = (lax.broadcasted_iota(jnp.int32, (rows, 128), 0) + r0 + 1).astype(F32)
    low = lax.broadcasted_iota(jnp.int32, (rows, 128), 1) < POOL_GROUP
    wa = jnp.where(low, float(POOL_WINDOWS[0]), float(POOL_WINDOWS[1]))
    wb = jnp.where(low, float(POOL_WINDOWS[2]), float(POOL_WINDOWS[3]))
    return jnp.minimum(t1, wa), jnp.minimum(t1, wb), low


def _window_sums(win, off, rows, sign):
    def sl(j, half):
        return win[off + sign * j: off + sign * j + rows, 128 * half:128 * half + 128]
    a2 = sl(0, 0) + sl(1, 0)
    a4 = a2 + sl(2, 0) + sl(3, 0)
    a8 = sl(0, 1)
    for j in range(1, 8):
        a8 = a8 + sl(j, 1)
    a16 = a8
    for j in range(8, 16):
        a16 = a16 + sl(j, 1)
    return a2, a4, a8, a16


def _pool_fwd(zp, wp_bd, pscale, name):
    s = zp.shape[0]
    r = min(POOL_ROWS, s)

    def body(z_ref, wp_ref, sc_ref, p_ref, feat_ref, pad):
        pad[0:POOL_HALO, :] = jnp.zeros((POOL_HALO, D_POOL), F32)
        pad[POOL_HALO:, :] = z_ref[...]

        def step(i, carry):
            r0 = pl.multiple_of(i * r, r)
            win = pad[pl.ds(r0, r + POOL_HALO), :]
            a2, a4, a8, a16 = _window_sums(win, POOL_HALO, r, -1)
            ca, cb, low = _pool_counts(r0, r)
            x0 = win[POOL_HALO:, :]
            pa = jnp.where(low, a2, a4) / ca
            pb = jnp.where(low, a8, a16) / cb
            p = (jnp.concatenate([pa, pb], axis=1) - x0).astype(BF16)
            p_ref[pl.ds(r0, r), :] = p
            pw = jnp.dot(p, wp_ref[...], preferred_element_type=F32)
            feat_ref[pl.ds(r0, r), :] = (pw * sc_ref[...]).astype(BF16)
            return carry

        lax.fori_loop(0, s // r, step, 0)

    return pl.pallas_call(
        body, out_shape=[jax.ShapeDtypeStruct((s, D_POOL), BF16), jax.ShapeDtypeStruct((s, D_POOL), BF16)],
        scratch_shapes=[pltpu.VMEM((s + POOL_HALO, D_POOL), F32)], name=name, compiler_params=_cparams(),
    )(zp, wp_bd, pscale)


def _pool_bwd(dfeat, p, wp_bd, pscale, name):
    s = p.shape[0]
    r = min(POOL_ROWS, s)

    def body(df_ref, p_ref, wp_ref, sc_ref, dz_ref, dwp_ref, dsc_ref, gpad, dpbuf):
        dwp_ref[...] = jnp.zeros_like(dwp_ref)
        dsc_ref[...] = jnp.zeros_like(dsc_ref)
        gpad[s:, :] = jnp.zeros((POOL_HALO, D_POOL), F32)

        def step1(i, carry):
            r0 = pl.multiple_of(i * r, r)
            pv = p_ref[pl.ds(r0, r), :]
            dfv = df_ref[pl.ds(r0, r), :]
            pw = jnp.dot(pv, wp_ref[...], preferred_element_type=F32)
            dsc_ref[...] += jnp.sum(dfv * pw, axis=0, keepdims=True)
            dpw = (dfv * sc_ref[...]).astype(BF16)
            dwp_ref[...] += lax.dot_general(pv, dpw, _DIMS["tn"], preferred_element_type=F32)
            dp = lax.dot_general(dpw, wp_ref[...], _DIMS["nt"], preferred_element_type=F32)
            ca, cb, _ = _pool_counts(r0, r)
            gpad[pl.ds(r0, r), :] = dp / jnp.concatenate([ca, cb], axis=1)
            dpbuf[pl.ds(r0, r), :] = dp
            return carry

        lax.fori_loop(0, s // r, step1, 0)

        def step2(i, carry):
            r0 = pl.multiple_of(i * r, r)
            win = gpad[pl.ds(r0, r + POOL_HALO), :]
            a2, a4, a8, a16 = _window_sums(win, 0, r, 1)
            low = lax.broadcasted_iota(jnp.int32, (r, 128), 1) < POOL_GROUP
            acc = jnp.concatenate([jnp.where(low, a2, a4), jnp.where(low, a8, a16)], axis=1)
            dz_ref[pl.ds(r0, r), :] = (acc - dpbuf[pl.ds(r0, r), :]).astype(BF16)
            return carry

        lax.fori_loop(0, s // r, step2, 0)

    return pl.pallas_call(
        body,
        out_shape=[jax.ShapeDtypeStruct((s, D_POOL), BF16), jax.ShapeDtypeStruct((D_POOL, D_POOL), F32),
                   jax.ShapeDtypeStruct((1, D_POOL), F32)],
        scratch_shapes=[pltpu.VMEM((s + POOL_HALO, D_POOL), F32), pltpu.VMEM((s, D_POOL), F32)],
        name=name, compiler_params=_cparams(),
    )(dfeat, p, wp_bd, pscale)


def _skew_index():
    cp = lax.broadcasted_iota(jnp.int32, (SKEW_W, N_REL), 0)
    dist = jnp.where(cp < KW, KPAD - cp, KPAD + SKEW_W - cp)
    idx = jnp.clip(dist, -REL_CLIP, REL_CLIP) + REL_CLIP
    return (idx == lax.broadcasted_iota(jnp.int32, (SKEW_W, N_REL), 1)).astype(F32)


def _row_bits(b):
    return (lax.broadcasted_iota(jnp.int32, (QB, SKEW_W), 0) >> b) & 1 == 1


N_EDGE = KPAD // QB


def _bias_block(rel_bias, name):
    def body(rb_ref, o_ref):
        onehot = _skew_index()
        row0 = lax.dot_general(rb_ref[...], onehot, _DIMS["nt"], precision=lax.Precision.HIGHEST,
                               preferred_element_type=F32)
        r = lax.broadcasted_iota(jnp.int32, (QB, KW), 0)
        kk = lax.broadcasted_iota(jnp.int32, (QB, KW), 1)
        cq, ck = r // CHUNK, kk // CHUNK
        band = (ck >= cq) & (ck <= cq + N_PREV_CHUNKS)
        for h in range(N_HEADS):
            t = jnp.broadcast_to(row0[h:h + 1, :], (QB, SKEW_W))
            for b in range(7):
                t = jnp.where(_row_bits(b), pltpu.roll(t, 1 << b, 1), t)
            for e in range(N_EDGE + 1):
                o_ref[e, h] = jnp.where(band & (kk >= KPAD - e * QB), t[:, :KW], NEG_INF)

    return pl.pallas_call(body, out_shape=jax.ShapeDtypeStruct((N_EDGE + 1, N_HEADS, QB, KW), F32), name=name,
                          compiler_params=_cparams())(rel_bias)


def _bias_spec():
    return pl.BlockSpec((None, N_HEADS, QB, KW), lambda i: (jnp.minimum(i, N_EDGE), 0, 0, 0))


def _bias_block_bwd(ds_acc, name):
    def body(ds_ref, o_ref):
        sums = []
        for h in range(N_HEADS):
            t = jnp.concatenate([ds_ref[h], jnp.zeros((QB, SKEW_W - KW), F32)], axis=1)
            for b in range(7):
                t = jnp.where(_row_bits(b), pltpu.roll(t, SKEW_W - (1 << b), 1), t)
            sums.append(jnp.sum(t, axis=0, keepdims=True))
        allh = jnp.concatenate(sums, axis=0)
        o_ref[...] = jnp.dot(allh, _skew_index(), precision=lax.Precision.HIGHEST, preferred_element_type=F32)

    return pl.pallas_call(body, out_shape=jax.ShapeDtypeStruct((N_HEADS, N_REL), F32), name=name,
                          compiler_params=_cparams())(ds_acc)


def _scaled(q):
    return (q.astype(F32) * (HEAD_DIM ** -0.5)).astype(BF16)


def _probs(q, kw, bias_ref):
    sc = jnp.stack([lax.dot_general(q[:, HEAD_DIM * h:HEAD_DIM * (h + 1)], kw[:, HEAD_DIM * h:HEAD_DIM * (h + 1)],
                                    _DIMS["nt"], preferred_element_type=F32) + bias_ref[h] for h in range(N_HEADS)])
    e = jnp.exp(sc - jnp.max(sc, axis=-1, keepdims=True))
    return e * (1.0 / jnp.sum(e, axis=-1, keepdims=True))


def _load_padded_kv(qkv_hbm, kpad, vpad, sems, s):
    kpad[0:KPAD, :] = jnp.zeros((KPAD, D_ATTN), BF16)
    vpad[0:KPAD, :] = jnp.zeros((KPAD, D_ATTN), BF16)
    ck = pltpu.make_async_copy(qkv_hbm.at[:, D_ATTN:2 * D_ATTN], kpad.at[pl.ds(KPAD, s), :], sems.at[0])
    cv = pltpu.make_async_copy(qkv_hbm.at[:, 2 * D_ATTN:3 * D_ATTN], vpad.at[pl.ds(KPAD, s), :], sems.at[1])
    ck.start()
    cv.start()
    ck.wait()
    cv.wait()


def _attn_fwd(qkv, bias, name, rider=None):
    s = qkv.shape[0]

    def body(q_ref, qkv_hbm, bias_ref, o_ref, kpad, vpad, sems):
        i = pl.program_id(0)

        @pl.when(i == 0)
        def _():
            _load_padded_kv(qkv_hbm, kpad, vpad, sems, s)

        base = pl.multiple_of(i * QB, QB)
        kw = kpad[pl.ds(base, KW), :]
        vw = vpad[pl.ds(base, KW), :]
        q = _scaled(q_ref[...])
        p = _probs(q, kw, bias_ref).astype(BF16)
        outs = [jnp.dot(p[h], vw[:, HEAD_DIM * h:HEAD_DIM * (h + 1)], preferred_element_type=F32)
                for h in range(N_HEADS)]
        o_ref[...] = jnp.concatenate(outs, axis=1).astype(BF16)

    res = _call(
        body, name=name, grid=(s // QB,),
        in_specs=[pl.BlockSpec((QB, D_ATTN), lambda i: (i, 0)), pl.BlockSpec(memory_space=pl.ANY),
                  _bias_spec()],
        out_specs=[pl.BlockSpec((QB, D_ATTN), lambda i: (i, 0))],
        out_shape=[jax.ShapeDtypeStruct((s, D_ATTN), BF16)],
        scratch_shapes=[pltpu.VMEM((s + KPAD, D_ATTN), BF16), pltpu.VMEM((s + KPAD, D_ATTN), BF16),
                        pltpu.SemaphoreType.DMA((2,))],
        args=(qkv, qkv, bias), rider=rider)
    return res[0] if rider is None else (res[0][0], res[1])


def _attn_bwd(qkv, do, bias, name, rider=None):
    s = qkv.shape[0]
    n = s // QB

    def body(q_ref, qkv_hbm, do_ref, bias_ref, dq_ref, dk_hbm, dv_hbm, ds_ref, kpad, vpad, dkacc, dvacc, sems):
        i = pl.program_id(0)

        @pl.when(i == 0)
        def _():
            _load_padded_kv(qkv_hbm, kpad, vpad, sems, s)
            dkacc[...] = jnp.zeros_like(dkacc)
            dvacc[...] = jnp.zeros_like(dvacc)
            ds_ref[...] = jnp.zeros_like(ds_ref)

        base = pl.multiple_of(i * QB, QB)
        kw = kpad[pl.ds(base, KW), :]
        vw = vpad[pl.ds(base, KW), :]
        q = _scaled(q_ref[...])
        dov = do_ref[...]
        heads = [slice(HEAD_DIM * h, HEAD_DIM * (h + 1)) for h in range(N_HEADS)]
        p = _probs(q, kw, bias_ref)
        dp = jnp.stack([lax.dot_general(dov[:, hs], vw[:, hs], _DIMS["nt"], preferred_element_type=F32) for hs in heads])
        ds = p * (dp - jnp.sum(dp * p, axis=-1, keepdims=True))
        ds_ref[...] += ds
        pb, dsb = p.astype(BF16), ds.astype(BF16)
        dvs = [lax.dot_general(pb[h], dov[:, hs], _DIMS["tn"], preferred_element_type=F32) for h, hs in enumerate(heads)]
        dqs = [jnp.dot(dsb[h], kw[:, hs], preferred_element_type=F32) for h, hs in enumerate(heads)]
        dks = [lax.dot_general(dsb[h], q[:, hs], _DIMS["tn"], preferred_element_type=F32) for h, hs in enumerate(heads)]
        dq_ref[...] = (jnp.concatenate(dqs, axis=1) * (HEAD_DIM ** -0.5)).astype(BF16)
        dkacc[pl.ds(base, KW), :] += jnp.concatenate(dks, axis=1)
        dvacc[pl.ds(base, KW), :] += jnp.concatenate(dvs, axis=1)

        @pl.when(i == n - 1)
        def _():
            def cast(j, carry):
                rows = pl.ds(pl.multiple_of(KPAD + j * 512, 512), 512)
                kpad[rows, :] = dkacc[rows, :].astype(BF16)
                vpad[rows, :] = dvacc[rows, :].astype(BF16)
                return carry

            lax.fori_loop(0, s // 512, cast, 0)
            ck = pltpu.make_async_copy(kpad.at[pl.ds(KPAD, s), :], dk_hbm, sems.at[0])
            cv = pltpu.make_async_copy(vpad.at[pl.ds(KPAD, s), :], dv_hbm, sems.at[1])
            ck.start()
            cv.start()
            ck.wait()
            cv.wait()

    blk = pl.BlockSpec((QB, D_ATTN), lambda i: (i, 0))
    acc_shape = jax.ShapeDtypeStruct((s, D_ATTN), BF16)
    return _call(
        body, name=name, grid=(n,),
        in_specs=[blk, pl.BlockSpec(memory_space=pl.ANY), blk, _bias_spec()],
        out_specs=[blk, pl.BlockSpec(memory_space=pl.ANY), pl.BlockSpec(memory_space=pl.ANY), _full((N_HEADS, QB, KW))],
        out_shape=[jax.ShapeDtypeStruct((s, D_ATTN), BF16), acc_shape, acc_shape,
                   jax.ShapeDtypeStruct((N_HEADS, QB, KW), F32)],
        scratch_shapes=[pltpu.VMEM((s + KPAD, D_ATTN), BF16), pltpu.VMEM((s + KPAD, D_ATTN), BF16),
                        pltpu.VMEM((s + KPAD, D_ATTN), F32), pltpu.VMEM((s + KPAD, D_ATTN), F32),
                        pltpu.SemaphoreType.DMA((2,))],
        args=(qkv, qkv, do, bias), rider=rider)


CONV_HALO = 32
CONV_ROWS = 64


def _sigmoid(t):
    return 1.0 / (1.0 + jnp.exp(-t))


CONV_WIN = CONV_ROWS + CONV_HALO - 8


def _row_windows(ref, r0, buf):
    win = ref[pl.ds(r0, CONV_ROWS + CONV_HALO), :]
    for j in range(1, 8):
        buf[j - 1] = win[j:j + CONV_WIN, :]

    def get(o):
        j, a = o % 8, o - o % 8
        if j == 0:
            return ref[pl.ds(r0 + a, CONV_ROWS), :]
        return buf[j - 1, a:a + CONV_ROWS, :]

    return get


def _glu_rows(z_ref, r0, rows):
    a = z_ref[pl.ds(r0, rows), 0:D_CONV]
    b = z_ref[pl.ds(r0, rows), D_CONV:2 * D_CONV]
    return a, _sigmoid(b)


def _conv_fwd(zc, conv_w, conv_b, ln_g, ln_b, name):
    s = zc.shape[0]
    rt = min(256, s)

    def body(z_ref, w_ref, cb_ref, g_ref, b_ref, cv_ref, feat_ref, hpad, shifts):
        hpad[0:CONV_HALO, :] = jnp.zeros((CONV_HALO, D_CONV), F32)

        def glu(i, carry):
            r0 = pl.multiple_of(i * rt, rt)
            a, sb = _glu_rows(z_ref, r0, rt)
            hpad[pl.ds(r0 + CONV_HALO, rt), :] = a * sb
            return carry

        lax.fori_loop(0, s // rt, glu, 0)
        w = w_ref[...]

        def conv(i, carry):
            r0 = pl.multiple_of(i * CONV_ROWS, CONV_ROWS)
            win = _row_windows(hpad, r0, shifts)
            acc = jnp.broadcast_to(cb_ref[...], (CONV_ROWS, D_CONV))
            for k in range(CONV_WIDTH):
                acc = acc + win(2 + k) * w[k:k + 1, :]
            cv_ref[pl.ds(r0, CONV_ROWS), :] = acc
            yhat, _ = _ln_hat(acc)
            y = yhat * g_ref[...] + b_ref[...]
            feat_ref[pl.ds(r0, CONV_ROWS), :] = (y * _sigmoid(y)).astype(BF16)
            return carry

        lax.fori_loop(0, s // CONV_ROWS, conv, 0)

    return pl.pallas_call(
        body, out_shape=[jax.ShapeDtypeStruct((s, D_CONV), F32), jax.ShapeDtypeStruct((s, D_CONV), BF16)],
        scratch_shapes=[pltpu.VMEM((s + CONV_HALO, D_CONV), F32), pltpu.VMEM((7, CONV_WIN, D_CONV), F32)],
        name=name, compiler_params=_cparams(),
    )(zc, conv_w, conv_b, ln_g, ln_b)


def _conv_bwd(dfeat, cv, zc, conv_w, ln_g, ln_b, name):
    s = zc.shape[0]
    rt = min(256, s)

    def body(df_ref, cv_ref, z_ref, w_ref, g_ref, b_ref, dz_ref, dw_ref, dcb_ref, dg_ref, db_ref, hpad, dcvpad, dwacc,
             hshifts, dshifts):
        hpad[0:CONV_HALO, :] = jnp.zeros((CONV_HALO, D_CONV), F32)
        dcvpad[s:, :] = jnp.zeros((CONV_HALO, D_CONV), F32)
        dwacc[...] = jnp.zeros_like(dwacc)
        dcb_ref[...] = jnp.zeros_like(dcb_ref)
        dg_ref[...] = jnp.zeros_like(dg_ref)
        db_ref[...] = jnp.zeros_like(db_ref)

        def pass1(i, carry):
            r0 = pl.multiple_of(i * rt, rt)
            a, sb = _glu_rows(z_ref, r0, rt)
            hpad[pl.ds(r0 + CONV_HALO, rt), :] = a * sb
            cvhat, rstd = _ln_hat(cv_ref[pl.ds(r0, rt), :])
            y = cvhat * g_ref[...] + b_ref[...]
            sg = _sigmoid(y)
            dy = df_ref[pl.ds(r0, rt), :] * (sg * (1.0 + y * (1.0 - sg)))
            dg_ref[...] += jnp.sum(dy * cvhat, axis=0, keepdims=True)
            db_ref[...] += jnp.sum(dy, axis=0, keepdims=True)
            dcv = _ln_hat_bwd(dy * g_ref[...], cvhat, rstd)
            dcb_ref[...] += jnp.sum(dcv, axis=0, keepdims=True)
            dcvpad[pl.ds(r0, rt), :] = dcv
            return carry

        lax.fori_loop(0, s // rt, pass1, 0)
        w = w_ref[...]

        def pass2(i, carry):
            r0 = pl.multiple_of(i * CONV_ROWS, CONV_ROWS)
            dwin = _row_windows(dcvpad, r0, dshifts)
            hwin = _row_windows(hpad, r0, hshifts)
            dcv = dwin(0)
            dh = jnp.zeros((CONV_ROWS, D_CONV), F32)
            for k in range(CONV_WIDTH):
                dh = dh + dwin(30 - k) * w[k:k + 1, :]
                prod = dcv * hwin(2 + k)
                dwacc[8 * k:8 * k + 8, :] += jnp.sum(prod.reshape(CONV_ROWS // 8, 8, D_CONV), axis=0)
            a, sb = _glu_rows(z_ref, r0, CONV_ROWS)
            dz_ref[pl.ds(r0, CONV_ROWS), :] = jnp.concatenate([dh * sb, dh * a * sb * (1.0 - sb)], axis=1).astype(BF16)
            return carry

        lax.fori_loop(0, s // CONV_ROWS, pass2, 0)
        dw_ref[...] = jnp.sum(dwacc[...].reshape(32, 8, D_CONV), axis=1)

    vs = jax.ShapeDtypeStruct((1, D_CONV), F32)
    return pl.pallas_call(
        body,
        out_shape=[jax.ShapeDtypeStruct((s, 2 * D_CONV), BF16), jax.ShapeDtypeStruct((32, D_CONV), F32), vs, vs, vs],
        scratch_shapes=[pltpu.VMEM((s + CONV_HALO, D_CONV), F32), pltpu.VMEM((s + CONV_HALO, D_CONV), F32),
                        pltpu.VMEM((256, D_CONV), F32), pltpu.VMEM((7, CONV_WIN, D_CONV), F32),
                        pltpu.VMEM((7, CONV_WIN, D_CONV), F32)],
        name=name, compiler_params=_cparams(),
    )(dfeat, cv, zc, conv_w, ln_g, ln_b)


def _merge(zg, b_gate, ys, name):
    s = zg.shape[0]
    tm = _row_tile(s)

    def body(zg_ref, bg_ref, y0_ref, y1_ref, y2_ref, o_ref):
        acc = None
        for j, y_ref in enumerate((y0_ref, y1_ref, y2_ref)):
            cs = slice(D_MODEL * j, D_MODEL * (j + 1))
            t = _sigmoid(zg_ref[:, cs] + bg_ref[:, cs]) * y_ref[...]
            acc = t if acc is None else acc + t
        o_ref[...] = acc.astype(BF16)

    row = pl.BlockSpec((tm, D_MODEL), lambda i: (i, 0))
    return pl.pallas_call(
        body, grid=(s // tm,),
        in_specs=[pl.BlockSpec((tm, 3 * D_MODEL), lambda i: (i, 0)), _full((1, 3 * D_MODEL)), row, row, row],
        out_specs=row, out_shape=jax.ShapeDtypeStruct((s, D_MODEL), BF16), name=name, compiler_params=_cparams(),
    )(zg, b_gate, *ys)


def _merge_bwd(dm, zg, b_gate, ys, name):
    s = zg.shape[0]
    tm = min(256, s)

    def body(dm_ref, zg_ref, bg_ref, y0_ref, y1_ref, y2_ref, d0_ref, d1_ref, d2_ref, dzg_ref, dbg_ref):
        first = pl.program_id(0) == 0

        @pl.when(first)
        def _():
            dbg_ref[...] = jnp.zeros_like(dbg_ref)

        dmv = dm_ref[...]
        for j, (y_ref, d_ref) in enumerate(((y0_ref, d0_ref), (y1_ref, d1_ref), (y2_ref, d2_ref))):
            cs = slice(D_MODEL * j, D_MODEL * (j + 1))
            g = _sigmoid(zg_ref[:, cs] + bg_ref[:, cs])
            d_ref[...] = (dmv * g).astype(BF16)
            dzg = dmv * y_ref[...] * g * (1.0 - g)
            dzg_ref[:, cs] = dzg.astype(BF16)
            dbg_ref[:, cs] += jnp.sum(dzg, axis=0, keepdims=True)

    row = pl.BlockSpec((tm, D_MODEL), lambda i: (i, 0))
    wide = pl.BlockSpec((tm, 3 * D_MODEL), lambda i: (i, 0))
    yb = jax.ShapeDtypeStruct((s, D_MODEL), BF16)
    return pl.pallas_call(
        body, grid=(s // tm,),
        in_specs=[row, wide, _full((1, 3 * D_MODEL)), row, row, row],
        out_specs=[row, row, row, wide, _full((1, 3 * D_MODEL))],
        out_shape=[yb, yb, yb, jax.ShapeDtypeStruct((s, 3 * D_MODEL), BF16), jax.ShapeDtypeStruct((1, 3 * D_MODEL), F32)],
        name=name, compiler_params=_cparams(),
    )(dm, zg, b_gate, *ys)


def _ff_hidden(u2, w_ff1t, b_ff1, name, rider=None):
    s = u2.shape[0]
    tm, tn = min(1024, s), 1024

    def body(a_ref, b_ref, bias_ref, pre_ref, h_ref):
        acc = lax.dot_general(a_ref[...], b_ref[...], _DIMS["nt"], preferred_element_type=F32) + bias_ref[...]
        pre_ref[...] = acc.astype(BF16)
        h_ref[...] = _relu2(acc).astype(BF16)

    blk = pl.BlockSpec((tm, tn), lambda i, j: (i, j))
    sh = jax.ShapeDtypeStruct((s, D_FF), BF16)
    res = _call(body, name=name, grid=(s // tm, D_FF // tn),
                in_specs=[pl.BlockSpec((tm, D_MODEL), lambda i, j: (i, 0)), pl.BlockSpec((tn, D_MODEL), lambda i, j: (j, 0)),
                          pl.BlockSpec((1, tn), lambda i, j: (0, j))],
                out_specs=[blk, blk], out_shape=[sh, sh], scratch_shapes=[], args=(u2, w_ff1t, b_ff1), rider=rider)
    return tuple(res) if rider is None else (tuple(res[0]), res[1])


def _ff_hidden_bwd(dff, w_ff2, hpre, name):
    s = dff.shape[0]
    tm, tn = min(512, s), 1024

    def body(a_ref, b_ref, h_ref, o_ref, sum_ref):
        dh = lax.dot_general(a_ref[...], b_ref[...], _DIMS["nt"], preferred_element_type=F32)
        dpre = dh * (2.0 * jnp.maximum(h_ref[...].astype(F32), 0.0))
        o_ref[...] = dpre.astype(BF16)
        _acc_rows(sum_ref, dpre, pl.program_id(1) == 0)

    return pl.pallas_call(
        body, grid=(D_FF // tn, s // tm),
        in_specs=[pl.BlockSpec((tm, D_MODEL), lambda j, i: (i, 0)), pl.BlockSpec((tn, D_MODEL), lambda j, i: (j, 0)),
                  pl.BlockSpec((tm, tn), lambda j, i: (i, j))],
        out_specs=[pl.BlockSpec((tm, tn), lambda j, i: (i, j)), pl.BlockSpec((1, tn), lambda j, i: (0, j))],
        out_shape=[jax.ShapeDtypeStruct((s, D_FF), BF16), jax.ShapeDtypeStruct((1, D_FF), F32)],
        name=name, compiler_params=_cparams(),
    )(dff, w_ff2, hpre)


def _silu(t):
    return t * _sigmoid(t)


def _mod_fwd(c_all, w_ada_sh, b_ada_sh, name):
    cols = w_ada_sh.shape[2]

    def body(c_ref, w_ref, b_ref, o_ref):
        ca = _silu(c_ref[...]).astype(BF16)
        o_ref[0] = jnp.dot(ca, w_ref[0].astype(BF16), preferred_element_type=F32) + b_ref[0]

    return pl.pallas_call(
        body, grid=(DEPTH,),
        in_specs=[_full((N_DEV, D_MODEL)), pl.BlockSpec((1, D_MODEL, cols), lambda l: (l, 0, 0)),
                  pl.BlockSpec((1, 1, cols), lambda l: (l, 0, 0))],
        out_specs=pl.BlockSpec((1, N_DEV, cols), lambda l: (l, 0, 0)),
        out_shape=jax.ShapeDtypeStruct((DEPTH, N_DEV, cols), F32), name=name, compiler_params=_cparams(),
    )(c_all, w_ada_sh, b_ada_sh)


def _mod_bwd(c_all, dmod_sh, name):
    cols = dmod_sh.shape[2]

    def body(c_ref, d_ref, o_ref):
        ca = _silu(c_ref[...])
        o_ref[0] = lax.dot_general(ca, d_ref[0], _DIMS["tn"], precision=lax.Precision.HIGHEST,
                                   preferred_element_type=F32)

    return pl.pallas_call(
        body, grid=(DEPTH,),
        in_specs=[_full((N_DEV, D_MODEL)), pl.BlockSpec((1, N_DEV, cols), lambda l: (l, 0, 0))],
        out_specs=pl.BlockSpec((1, D_MODEL, cols), lambda l: (l, 0, 0)),
        out_shape=jax.ShapeDtypeStruct((DEPTH, D_MODEL, cols), F32), name=name, compiler_params=_cparams(),
    )(c_all, dmod_sh)


def _flat_tiles(rows, cols, itemsize_total):
    budget = 12 * 1024 * 1024
    tr = rows
    while tr % 32 == 0 and tr * cols * itemsize_total > budget:
        tr //= 2
    return tr


def _sum_cores(dw, recv, place, name):
    _, m, n = dw.shape
    tr = _flat_tiles(m, n, 6)

    def body(place_ref, a_ref, b_ref, o_ref):
        o_ref[...] = (a_ref[...].astype(F32) + b_ref[...].astype(F32)).astype(BF16)

    grid_spec = pltpu.PrefetchScalarGridSpec(
        num_scalar_prefetch=1, grid=(m // tr,),
        in_specs=[pl.BlockSpec((None, tr, n), lambda i, pr: (pr[0], i, 0)), pl.BlockSpec((tr, n), lambda i, pr: (i, 0))],
        out_specs=pl.BlockSpec((tr, n), lambda i, pr: (i, 0)))
    return pl.pallas_call(body, grid_spec=grid_spec, out_shape=jax.ShapeDtypeStruct((m, n), BF16), name=name,
                          compiler_params=_cparams())(place, dw, recv)


def _sum_chips(h, r, place, name):
    _, rs, n = h.shape
    tr = _flat_tiles(rs, n, 12)

    def body(place_ref, h_ref, r_ref, o_ref):
        o_ref[...] = ((h_ref[...].astype(F32) + r_ref[0].astype(F32)) + r_ref[1].astype(F32)) + r_ref[2].astype(F32)

    grid_spec = pltpu.PrefetchScalarGridSpec(
        num_scalar_prefetch=1, grid=(rs // tr,),
        in_specs=[pl.BlockSpec((None, tr, n), lambda i, pr: (pr[1], i, 0)), pl.BlockSpec((3, tr, n), lambda i, pr: (0, i, 0))],
        out_specs=pl.BlockSpec((tr, n), lambda i, pr: (i, 0)))
    return pl.pallas_call(body, grid_spec=grid_spec, out_shape=jax.ShapeDtypeStruct((rs, n), F32), name=name,
                          compiler_params=_cparams())(place, h, r)


def _adam_math(w, g, m, v):
    m2 = ADAM_B1 * m + (1.0 - ADAM_B1) * g
    v2 = ADAM_B2 * v + (1.0 - ADAM_B2) * (g * g)
    m_hat = m2 / (1.0 - ADAM_B1 ** ADAM_STEP)
    v_hat = v2 / (1.0 - ADAM_B2 ** ADAM_STEP)
    delta = -ADAM_LR * (m_hat / (jnp.sqrt(v_hat) + ADAM_EPS) + ADAM_WD * w)
    return delta, m2, v2


def _adamw(w, m, v, grads, name):
    r, c = w.shape
    tr = _flat_tiles(r, c, 4 * (7 + len(grads)))

    def body(*refs):
        w_ref, m_ref, v_ref = refs[:3]
        g_refs = refs[3:3 + len(grads)]
        g_ref, d_ref, m2_ref, v2_ref = refs[3 + len(grads):]
        g = g_refs[0][...]
        for gr in g_refs[1:]:
            g = g + gr[...]
        delta, m2, v2 = _adam_math(w_ref[...], g, m_ref[...], v_ref[...])
        g_ref[...] = g
        d_ref[...] = delta
        m2_ref[...] = m2
        v2_ref[...] = v2

    blk = pl.BlockSpec((tr, c), lambda i: (i, 0))
    sh = jax.ShapeDtypeStruct((r, c), F32)
    return pl.pallas_call(body, grid=(r // tr,), in_specs=[blk] * (3 + len(grads)), out_specs=[blk] * 4,
                          out_shape=[sh] * 4, name=name, compiler_params=_cparams())(w, m, v, *grads)


def _adamw_halves(w, m, v, own, other, place, split, name):
    nl, r, c = w.shape
    hr, hc = own[0].shape
    tr = _flat_tiles(hr, hc, 4 * (7 + 2 * nl))
    nt = hr // tr
    if split == "rows":
        w_spec = pl.BlockSpec((None, tr, c), lambda l, h, t, pr: (l, h * nt + t, 0))
    else:
        w_spec = pl.BlockSpec((None, tr, hc), lambda l, h, t, pr: (l, t, h))

    def g_spec(layer, mine):
        return pl.BlockSpec((tr, hc), lambda l, h, t, pr: (jnp.where((l == layer) & ((h == pr[0]) == mine), t, nt - 1), 0))

    def body(place_ref, w_ref, m_ref, v_ref, *refs):
        own_refs, other_refs = refs[:nl], refs[nl:2 * nl]
        g_ref, d_ref, m2_ref, v2_ref = refs[2 * nl:]
        layer = pl.program_id(0)
        mine = pl.program_id(1) == place_ref[0]
        g = None
        for li in range(nl):
            cand = jnp.where(mine, own_refs[li][...], other_refs[li][...])
            g = cand if g is None else jnp.where(layer == li, cand, g)
        delta, m2, v2 = _adam_math(w_ref[...], g, m_ref[...], v_ref[...])
        g_ref[...] = g
        d_ref[...] = delta
        m2_ref[...] = m2
        v2_ref[...] = v2

    sh = jax.ShapeDtypeStruct((nl, r, c), F32)
    g_specs = [g_spec(li, True) for li in range(nl)] + [g_spec(li, False) for li in range(nl)]
    return _call(body, name=name, grid=(nl, 2, nt), in_specs=[w_spec] * 3 + g_specs, out_specs=[w_spec] * 4,
                 out_shape=[sh] * 4, scratch_shapes=[], args=(w, m, v, *own, *other), prefetch=(place,))


def _adamw_small(w, m, v, g_all, name):
    r, c = w.shape

    def body(w_ref, m_ref, v_ref, g_ref, go_ref, d_ref, m2_ref, v2_ref):
        g = g_ref[0]
        for b in range(1, N_DEV):
            g = g + g_ref[b]
        delta, m2, v2 = _adam_math(w_ref[...], g, m_ref[...], v_ref[...])
        go_ref[...] = g
        d_ref[...] = delta
        m2_ref[...] = m2
        v2_ref[...] = v2

    sh = jax.ShapeDtypeStruct((r, c), F32)
    return pl.pallas_call(body, out_shape=[sh] * 4, name=name, compiler_params=_cparams())(w, m, v, g_all)


def _me():
    return lax.axis_index("x"), lax.axis_index("y"), lax.axis_index("c")


def _flip(v, bit):
    return 1 - v if bit else v


def _allgather_small(blk, name):
    r, c = blk.shape

    def body(x_ref, o_ref, send_sems, recv_sems):
        x, y, cc = _me()
        me = 4 * x + 2 * y + cc
        copies = []
        for k in range(1, N_DEV):
            peer = (_flip(x, k & 4), _flip(y, k & 2), _flip(cc, k & 1))
            cp = pltpu.make_async_remote_copy(src_ref=x_ref, dst_ref=o_ref.at[me], send_sem=send_sems.at[k - 1],
                                              recv_sem=recv_sems.at[k - 1], device_id=peer, device_id_type=MESH)
            cp.start()
            copies.append(cp)
        o_ref[me] = x_ref[...]
        for cp in copies:
            cp.wait()

    return pl.pallas_call(
        body, out_shape=jax.ShapeDtypeStruct((N_DEV, r, c), F32),
        in_specs=[pl.BlockSpec(memory_space=pltpu.VMEM)], out_specs=pl.BlockSpec(memory_space=pltpu.VMEM),
        scratch_shapes=[pltpu.SemaphoreType.DMA((N_DEV - 1,)), pltpu.SemaphoreType.DMA((N_DEV - 1,))],
        name=name, compiler_params=_cparams(),
    )(blk)


class _Rider:
    def __init__(self, arrays, out_shapes, scratch_shapes, start, finish):
        self.arrays, self.out_shapes, self.scratch_shapes = list(arrays), list(out_shapes), list(scratch_shapes)
        self.start, self.finish = start, finish


def _call(body, *, name, grid, in_specs, out_specs, out_shape, scratch_shapes, args, rider=None, prefetch=()):
    npf = len(prefetch)

    def launch(fn, in_specs, out_specs, out_shape, scratch_shapes, args):
        grid_spec = pltpu.PrefetchScalarGridSpec(num_scalar_prefetch=npf, grid=grid, in_specs=in_specs,
                                                 out_specs=out_specs, scratch_shapes=scratch_shapes)
        return pl.pallas_call(fn, grid_spec=grid_spec, out_shape=out_shape, name=name,
                              compiler_params=_cparams())(*prefetch, *args)

    if rider is None:
        return launch(body, list(in_specs), list(out_specs), list(out_shape), list(scratch_shapes), args)
    ni, no, ns = len(in_specs), len(out_specs), len(scratch_shapes)
    ri, ro = len(rider.arrays), len(rider.out_shapes)
    steps = int(np.prod(grid))

    def wrapped(*refs):
        pf, refs = refs[:npf], refs[npf:]
        h_in, r_in = refs[:ni], refs[ni:ni + ri]
        h_out, r_out = refs[ni + ri:ni + ri + no], refs[ni + ri + no:ni + ri + no + ro]
        h_scr, r_scr = refs[ni + ri + no + ro:ni + ri + no + ro + ns], refs[ni + ri + no + ro + ns:]
        step = pl.program_id(0)
        for d in range(1, len(grid)):
            step = step * grid[d] + pl.program_id(d)

        @pl.when(step == 0)
        def _():
            rider.start(r_in, r_out, r_scr)

        body(*pf, *h_in, *h_out, *h_scr)

        @pl.when(step == steps - 1)
        def _():
            rider.finish(r_in, r_out, r_scr)

    anyspec = pl.BlockSpec(memory_space=pl.ANY)
    res = launch(wrapped, list(in_specs) + [anyspec] * ri, list(out_specs) + [anyspec] * ro,
                 list(out_shape) + rider.out_shapes, list(scratch_shapes) + rider.scratch_shapes,
                 list(args) + rider.arrays)
    return res[:no], res[no:]


def _run_rider(rider, name):
    ri = len(rider.arrays)

    def body(*refs):
        r_in, r_out, r_scr = refs[:ri], refs[ri:ri + len(rider.out_shapes)], refs[ri + len(rider.out_shapes):]
        rider.start(r_in, r_out, r_scr)
        rider.finish(r_in, r_out, r_scr)

    anyspec = pl.BlockSpec(memory_space=pl.ANY)
    return pl.pallas_call(body, in_specs=[anyspec] * ri, out_specs=[anyspec] * len(rider.out_shapes),
                          out_shape=rider.out_shapes, scratch_shapes=rider.scratch_shapes, name=name,
                          compiler_params=_cparams())(*rider.arrays)


def _allgather_rider(blk):
    def copies(ins, outs, scr):
        send_sems, recv_sems, loc_sems, stage = scr
        x, y, cc = _me()
        me = 4 * x + 2 * y + cc
        remote = [pltpu.make_async_remote_copy(
            src_ref=ins[0], dst_ref=outs[0].at[me], send_sem=send_sems.at[k - 1], recv_sem=recv_sems.at[k - 1],
            device_id=(_flip(x, k & 4), _flip(y, k & 2), _flip(cc, k & 1)), device_id_type=MESH) for k in range(1, N_DEV)]
        return remote, pltpu.make_async_copy(ins[0], stage, loc_sems.at[0]), (outs[0].at[me], stage, loc_sems.at[1])

    def start(ins, outs, scr):
        remote, lin, _ = copies(ins, outs, scr)
        lin.start()
        for cp in remote:
            cp.start()

    def finish(ins, outs, scr):
        remote, lin, (dst, stage, sem) = copies(ins, outs, scr)
        lin.wait()
        lout = pltpu.make_async_copy(stage, dst, sem)
        lout.start()
        for cp in remote:
            cp.wait()
        lout.wait()

    return _Rider([blk], [jax.ShapeDtypeStruct((N_DEV,) + blk.shape, blk.dtype)],
                  [pltpu.SemaphoreType.DMA((N_DEV - 1,)), pltpu.SemaphoreType.DMA((N_DEV - 1,)),
                   pltpu.SemaphoreType.DMA((2,)), pltpu.VMEM(blk.shape, blk.dtype)], start, finish)


def _gather_rider(shards):
    n = len(shards)

    def copies(ins, outs, scr, relay=True):
        ici_send, ici_recv, d2d_send, d2d_recv, loc_sems = scr[:5]
        stage = scr[5:]
        x, y, cc = _me()
        chip = 2 * x + y
        sibling = (x, y, 1 - cc)
        local, sends, relays = [], [], []
        for j in range(n):
            def rows(ch, h, j=j):
                return outs[j].at[ch, h]

            lc = pltpu.make_async_copy(ins[j], stage[j], loc_sems.at[j])
            local.append((lc, pltpu.make_async_copy(stage[j], outs[j].at[chip], loc_sems.at[n + j]) if relay else None))
            for k in range(1, N_CHIP):
                px, py = _flip(x, k & 2), _flip(y, k & 1)
                pchip = 2 * px + py
                q = 3 * j + k - 1
                out_cp = pltpu.make_async_remote_copy(src_ref=ins[j].at[cc], dst_ref=rows(chip, cc),
                                                      send_sem=ici_send.at[q], recv_sem=ici_recv.at[q],
                                                      device_id=(px, py, cc), device_id_type=MESH)
                sends.append(out_cp)
                if not relay:
                    continue
                arrival = pltpu.make_async_remote_copy(src_ref=rows(pchip, cc), dst_ref=rows(pchip, cc),
                                                       send_sem=ici_send.at[q], recv_sem=ici_recv.at[q],
                                                       device_id=(px, py, cc), device_id_type=MESH)
                forward = pltpu.make_async_remote_copy(src_ref=rows(pchip, cc), dst_ref=rows(pchip, cc),
                                                       send_sem=d2d_send.at[q], recv_sem=d2d_recv.at[q],
                                                       device_id=sibling, device_id_type=MESH)
                from_sibling = pltpu.make_async_remote_copy(src_ref=rows(pchip, 1 - cc), dst_ref=rows(pchip, 1 - cc),
                                                            send_sem=d2d_send.at[q], recv_sem=d2d_recv.at[q],
                                                            device_id=sibling, device_id_type=MESH)
                relays.append((arrival, forward, from_sibling))
        return local, sends, relays

    def start(ins, outs, scr):
        local, sends, _ = copies(ins, outs, scr, relay=False)
        for lin, _ in local:
            lin.start()
        for cp in sends:
            cp.start()

    def finish(ins, outs, scr):
        local, sends, relays = copies(ins, outs, scr)
        for lin, lout in local:
            lin.wait()
            lout.start()
        for arrival, forward, _ in relays:
            arrival.wait_recv()
            forward.start()
        for cp in sends:
            cp.wait_send()
        for _, forward, from_sibling in relays:
            forward.wait_send()
            from_sibling.wait_recv()
        for _, lout in local:
            lout.wait()

    scratch = [pltpu.SemaphoreType.DMA((3 * n,)), pltpu.SemaphoreType.DMA((3 * n,)), pltpu.SemaphoreType.DMA((3 * n,)),
               pltpu.SemaphoreType.DMA((3 * n,)), pltpu.SemaphoreType.DMA((2 * n,))]
    scratch += [pltpu.VMEM(a.shape, a.dtype) for a in shards]
    return _Rider(shards, [jax.ShapeDtypeStruct((N_CHIP,) + a.shape, a.dtype) for a in shards], scratch, start, finish)


def _sibling_rider(arrs, other_half=False):
    n = len(arrs)

    def copies(ins, outs, scr):
        send_sems, recv_sems = scr
        x, y, cc = _me()
        return [pltpu.make_async_remote_copy(
            src_ref=ins[j].at[1 - cc] if other_half else ins[j], dst_ref=outs[j], send_sem=send_sems.at[j],
            recv_sem=recv_sems.at[j], device_id=(x, y, 1 - cc), device_id_type=MESH) for j in range(n)]

    def start(ins, outs, scr):
        for cp in copies(ins, outs, scr):
            cp.start()

    def finish(ins, outs, scr):
        for cp in copies(ins, outs, scr):
            cp.wait()

    return _Rider(arrs, [jax.ShapeDtypeStruct(a.shape[1:] if other_half else a.shape, a.dtype) for a in arrs],
                  [pltpu.SemaphoreType.DMA((n,)), pltpu.SemaphoreType.DMA((n,))], start, finish)


def _sibling_send(arrs, name, other_half=False):
    return _run_rider(_sibling_rider(arrs, other_half), name)


def _join_riders(first, second):
    ni, no, ns = len(first.arrays), len(first.out_shapes), len(first.scratch_shapes)

    def split(ins, outs, scr):
        return (ins[:ni], outs[:no], scr[:ns]), (ins[ni:], outs[no:], scr[ns:])

    def start(ins, outs, scr):
        a, b = split(ins, outs, scr)
        first.start(*a)
        second.start(*b)

    def finish(ins, outs, scr):
        a, b = split(ins, outs, scr)
        first.finish(*a)
        second.finish(*b)

    return _Rider(first.arrays + second.arrays, first.out_shapes + second.out_shapes,
                  first.scratch_shapes + second.scratch_shapes, start, finish)


def _scatter_rider(arrs):
    n = len(arrs)

    def copies(ins, outs, scr):
        send_sems, recv_sems = scr
        x, y, cc = _me()
        cps = []
        for j in range(n):
            for k in range(1, N_CHIP):
                px, py = _flip(x, k & 2), _flip(y, k & 1)
                cps.append(pltpu.make_async_remote_copy(
                    src_ref=ins[j].at[2 * px + py], dst_ref=outs[j].at[k - 1], send_sem=send_sems.at[3 * j + k - 1],
                    recv_sem=recv_sems.at[3 * j + k - 1], device_id=(px, py, cc), device_id_type=MESH))
        return cps

    def start(ins, outs, scr):
        for cp in copies(ins, outs, scr):
            cp.start()

    def finish(ins, outs, scr):
        for cp in copies(ins, outs, scr):
            cp.wait()

    return _Rider(arrs, [jax.ShapeDtypeStruct((N_CHIP - 1,) + a.shape[1:], a.dtype) for a in arrs],
                  [pltpu.SemaphoreType.DMA((3 * n,)), pltpu.SemaphoreType.DMA((3 * n,))], start, finish)


COL_SHARDED = ("w_in", "w_br_pool", "w_br_attn", "w_br_conv", "w_ff1")
ROW_SHARDED = ("w_o", "w_ff2")
BIG = COL_SHARDED + ROW_SHARDED
SMALL = ("b_ada", "b_gate", "w_pool", "pool_scale", "rel_bias", "conv_w", "conv_b", "conv_ln_g", "conv_ln_b",
         "ln_mix_g", "ln_mix_b", "b_ff1", "b_ff2", "ln_ff_g", "ln_ff_b")
PACK_W = 1024


def _pack(parts):
    rows = []
    for a in parts:
        flat = a.reshape(-1)
        n = -(-flat.shape[0] // PACK_W) * PACK_W
        rows.append(jnp.pad(flat, (0, n - flat.shape[0])).reshape(-1, PACK_W))
    out = jnp.concatenate(rows, axis=0)
    r = -(-out.shape[0] // 8) * 8
    return jnp.pad(out, ((0, r - out.shape[0]), (0, 0)))


def _unpack(packed, shapes):
    out, r0 = [], 0
    for shp in shapes:
        size = int(np.prod(shp))
        nr = -(-size // PACK_W)
        out.append(packed[r0:r0 + nr].reshape(-1)[:size].reshape(shp))
        r0 += nr
    return out


def _hosted(fn, hook, *args, **kw):
    if hook is None:
        return fn(*args, **kw)
    res, rider_out = fn(*args, rider=hook[0], **kw)
    hook[1](rider_out)
    return res


def _layer_fwd(l, x, mod, W, P, hooks=None):
    hooks = hooks or {}
    s = x.shape[0]
    sh_m, sc_m, g_m, sh_f, sc_f, g_f = [mod[l:l + 1, D_MODEL * j:D_MODEL * (j + 1)] for j in range(6)]
    n = lambda t: f"{t}{l}"
    w_in = W["w_in"][l]
    u = _ln_mod(x, sc_m, sh_m, n("ln_mod_mix"))
    tmz = min(1024, s)
    zp = _mm(u, w_in, "nt", tm=min(2048, s), tn=256, out_dtype=F32, name=n("z_pool"), b_col0=0, n_out=D_POOL)
    qkv = _mm(u, w_in, "nt", tm=tmz, tn=256, out_dtype=BF16, name=n("z_qkv"), b_col0=OFF_QKV // 256, n_out=3 * D_ATTN)
    zc = _mm(u, w_in, "nt", tm=tmz, tn=256, out_dtype=F32, name=n("z_conv"), b_col0=OFF_CONV // 256, n_out=2 * D_CONV)
    zg = _hosted(_mm, hooks.get("z_gate"), u, w_in, "nt", tm=tmz, tn=768, out_dtype=BF16, name=n("z_gate"),
                 b_col0=OFF_GATE // 768, n_out=3 * D_MODEL)

    p, feat_pool = _pool_fwd(zp, P["wp_bd"][l], P["pool_scale"][l], n("pool_fwd"))
    bias = _bias_block(P["rel_bias"][l], n("bias_block"))
    o = _hosted(_attn_fwd, hooks.get("attn"), qkv, bias, n("attn_fwd"))
    cv, feat_conv = _conv_fwd(zc, P["conv_w"][l], P["conv_b"][l], P["conv_ln_g"][l], P["conv_ln_b"][l], n("conv_fwd"))

    tmb = min(1024, s)
    y_pool = _mm(feat_pool, W["w_br_pool"][l], "nt", tm=tmb, tn=1024, out_dtype=BF16, name=n("y_pool"))
    y_attn = _mm(o, W["w_br_attn"][l], "nt", tm=tmb, tn=1024, out_dtype=BF16, name=n("y_attn"))
    y_conv = _mm(feat_conv, W["w_br_conv"][l], "nt", tm=tmb, tn=1024, out_dtype=BF16, name=n("y_conv"))
    ys = (y_pool, y_attn, y_conv)
    merged = _merge(zg, P["b_gate"][l], ys, n("merge"))
    mix, x1 = _mm_resid_ln(merged, W["w_o"][l], None, x, g_m, P["ln_mix_g"][l], P["ln_mix_b"][l], n("mix_out"))

    u2 = _ln_mod(x1, sc_f, sh_f, n("ln_mod_ff"))
    hpre, hid = _hosted(_ff_hidden, hooks.get("ff1"), u2, W["w_ff1"][l], P["b_ff1"][l], n("ff1"))
    ff, x2 = _hosted(_mm_resid_ln, hooks.get("ff2"), hid, W["w_ff2"][l], P["b_ff2"][l], x1, g_f, P["ln_ff_g"][l],
                     P["ln_ff_b"][l], n("ff2"))
    saved = dict(x=x, u=u, zp=zp, qkv=qkv, zc=zc, zg=zg, p=p, feat_pool=feat_pool, bias=bias, o=o, cv=cv,
                 feat_conv=feat_conv, ys=ys, merged=merged, mix=mix, x1=x1, u2=u2, hpre=hpre, hid=hid, ff=ff)
    return x2, saved


def _layer_bwd(l, dx2, mod, W, P, A, hooks=None, tgt=None):
    hooks = hooks or {}
    s = dx2.shape[0]
    sh_m, sc_m, g_m, sh_f, sc_f, g_f = [mod[l:l + 1, D_MODEL * j:D_MODEL * (j + 1)] for j in range(6)]
    n = lambda t: f"{t}{l}"
    tmb = min(1024, s)
    gw, gs = {}, {}

    dres, dff, gs["ln_ff_g"], gs["ln_ff_b"], dg_f, gs["b_ff2"], *loss_part = _resid_ln_bwd(
        dx2, A["x1"], A["ff"], g_f, P["ln_ff_g"][l], n("resid_ln_ff_bwd"), tgt=tgt)
    gw["w_ff2"] = _mm(A["hid"], dff, "tn", tm=512, tn=1024, out_dtype=BF16, name=n("dw_ff2"), split_n=512)
    dhpre, gs["b_ff1"] = _ff_hidden_bwd(dff, W["w_ff2"][l], A["hpre"], n("ff_hidden_bwd"))
    gw["w_ff1"] = _mm(dhpre, A["u2"], "tn", tm=512, tn=1024, out_dtype=BF16, name=n("dw_ff1"), split_n=512)
    dx1, dsc_f, dsh_f = _mm_ln_mod_bwd(dhpre, W["w_ff1"][l], A["x1"], sc_f, dres, n("du_ff"))

    dres, dmix, gs["ln_mix_g"], gs["ln_mix_b"], dg_m, _ = _resid_ln_bwd(
        dx1, A["x"], A["mix"], g_m, P["ln_mix_g"][l], n("resid_ln_mix_bwd"))
    gw["w_o"] = _mm(A["merged"], dmix, "tn", tm=512, tn=1024, out_dtype=BF16, name=n("dw_o"), split_n=512)
    dmerged = _mm(dmix, W["w_o"][l], "nt", tm=tmb, tn=1024, out_dtype=F32, name=n("d_merged"))
    dy_pool, dy_attn, dy_conv, dzg, gs["b_gate"] = _merge_bwd(dmerged, A["zg"], P["b_gate"][l], A["ys"], n("merge_bwd"))

    gw["w_br_pool"] = _mm(dy_pool, A["feat_pool"], "tn", tm=512, tn=256, out_dtype=BF16, name=n("dw_br_pool"),
                          split_n=128)
    gw["w_br_attn"] = _mm(dy_attn, A["o"], "tn", tm=512, tn=512, out_dtype=BF16, name=n("dw_br_attn"), split_n=256)
    gw["w_br_conv"] = _mm(dy_conv, A["feat_conv"], "tn", tm=512, tn=256, out_dtype=BF16, name=n("dw_br_conv"),
                          split_n=128)
    dfeat_pool = _mm(dy_pool, W["w_br_pool"][l], "nn", tm=tmb, tn=256, out_dtype=F32, name=n("d_feat_pool"))
    do = _mm(dy_attn, W["w_br_attn"][l], "nn", tm=tmb, tn=512, out_dtype=BF16, name=n("d_attn_out"))
    dfeat_conv = _mm(dy_conv, W["w_br_conv"][l], "nn", tm=tmb, tn=256, out_dtype=F32, name=n("d_feat_conv"))

    dzp, dwp_bd, gs["pool_scale"] = _pool_bwd(dfeat_pool, A["p"], P["wp_bd"][l], P["pool_scale"][l], n("pool_bwd"))
    gs["w_pool"] = jnp.stack([dwp_bd[POOL_GROUP * g:POOL_GROUP * (g + 1), POOL_GROUP * g:POOL_GROUP * (g + 1)]
                              for g in range(len(POOL_WINDOWS))])
    hook = hooks["attn"](gw) if "attn" in hooks else None
    dq, dk, dv, ds_acc = _hosted(_attn_bwd, hook, A["qkv"], do, A["bias"], n("attn_bwd"))
    gs["rel_bias"] = _bias_block_bwd(ds_acc, n("bias_block_bwd"))
    dzc, dcw, gs["conv_b"], gs["conv_ln_g"], gs["conv_ln_b"] = _conv_bwd(
        dfeat_conv, A["cv"], A["zc"], P["conv_w"][l], P["conv_ln_g"][l], P["conv_ln_b"][l], n("conv_bwd"))
    gs["conv_w"] = dcw[:CONV_WIDTH]

    dz = [dzp, dq, dk, dv, dzc, dzg]
    gw["w_in"] = _dw_segments(dz, A["u"], n("dw_in"))
    hook = hooks["du_mix"](gw) if "du_mix" in hooks else None
    dx, dsc_m, dsh_m = _hosted(_mm_ln_mod_bwd, hook, dz, W["w_in"][l], A["x"], sc_m, dres, n("du_mix"))
    dmod = jnp.concatenate([dsh_m, dsc_m, dg_m, dsh_f, dsc_f, dg_f], axis=1)
    return (dx, gw, gs, dmod) if tgt is None else (dx, gw, gs, dmod, loss_part[0])


def _small_shapes():
    return {"b_ada": (6 * D_MODEL,), "b_gate": (3 * D_MODEL,), "w_pool": (4, POOL_GROUP, POOL_GROUP),
            "pool_scale": (D_POOL,), "rel_bias": (N_HEADS, N_REL), "conv_w": (CONV_WIDTH, D_CONV),
            "conv_b": (D_CONV,), "conv_ln_g": (D_CONV,), "conv_ln_b": (D_CONV,), "ln_mix_g": (D_MODEL,),
            "ln_mix_b": (D_MODEL,), "b_ff1": (D_FF,), "b_ff2": (D_MODEL,), "ln_ff_g": (D_MODEL,), "ln_ff_b": (D_MODEL,)}


def kernel(x, c, w_ada, b_ada, w_in, b_gate, w_pool, pool_scale, rel_bias, conv_w, conv_b, conv_ln_g, conv_ln_b, w_br_pool, w_br_attn, w_br_conv, w_o, ln_mix_g, ln_mix_b, w_ff1, b_ff1, w_ff2, b_ff2, ln_ff_g, ln_ff_b, loss_target, m_w_ada, m_b_ada, m_w_in, m_b_gate, m_w_pool, m_pool_scale, m_rel_bias, m_conv_w, m_conv_b, m_conv_ln_g, m_conv_ln_b, m_w_br_pool, m_w_br_attn, m_w_br_conv, m_w_o, m_ln_mix_g, m_ln_mix_b, m_w_ff1, m_b_ff1, m_w_ff2, m_b_ff2, m_ln_ff_g, m_ln_ff_b, v_w_ada, v_b_ada, v_w_in, v_b_gate, v_w_pool, v_pool_scale, v_rel_bias, v_conv_w, v_conv_b, v_conv_ln_g, v_conv_ln_b, v_w_br_pool, v_w_br_attn, v_w_br_conv, v_w_o, v_ln_mix_g, v_ln_mix_b, v_w_ff1, v_b_ff1, v_w_ff2, v_b_ff2, v_ln_ff_g, v_ln_ff_b):
    env = dict(locals())
    xi, yi, ci = _me()
    chip = 2 * xi + yi
    me = 4 * xi + 2 * yi + ci
    xs = x[0]
    tgt = loss_target[0]
    L = DEPTH

    first = _allgather_small(jnp.concatenate([c.reshape(8, 128), _pack([conv_w]).reshape(-1, 128)]), "gather_c_conv_w")
    c_all = first[:, :8].reshape(N_DEV, D_MODEL)
    ada_cols = w_ada.shape[2]
    b_ada_sh = lax.dynamic_slice_in_dim(b_ada, chip * ada_cols, ada_cols, axis=1).reshape(L, 1, ada_cols)
    mod_part = _mod_fwd(c_all, w_ada, b_ada_sh, "mod_fwd")
    mod_g = _allgather_small(mod_part.reshape(-1, 128), "gather_mod").reshape(N_CHIP, 2, L, N_DEV, ada_cols)[:, 0]
    mod_all = jnp.transpose(mod_g, (1, 2, 0, 3)).reshape(L, N_DEV, 6 * D_MODEL)
    mod = lax.dynamic_index_in_dim(mod_all, me, axis=1, keepdims=False)

    W = {k: [None] * L for k in BIG}

    def weight_gather(names, l):
        shards = [(jnp.swapaxes(env[k][l], 0, 1) if k in COL_SHARDED else env[k][l]).astype(BF16) for k in names]
        shards = [a.reshape(2, a.shape[0] // 2, a.shape[1]) for a in shards]

        def done(outs):
            for k, g in zip(names, outs):
                W[k][l] = g.reshape(-1, g.shape[-1])

        return _gather_rider(shards), done

    branch_names = ("w_br_pool", "w_br_attn", "w_br_conv", "w_o")
    late_names = ("w_ff1", "w_ff2")
    rider, done = weight_gather(("w_in",), 0)
    done(_run_rider(rider, "gather_w_in0"))
    fwd_hooks = [{"z_gate": weight_gather(branch_names, 0), "attn": weight_gather(late_names, 0),
                  "ff1": weight_gather(("w_in",), 1), "ff2": weight_gather(branch_names, 1)},
                 {"attn": weight_gather(late_names, 1)}]

    P = {k: env[k] for k in ("rel_bias", "conv_w")}
    for k in ("b_gate", "pool_scale", "conv_b", "conv_ln_g", "conv_ln_b", "ln_mix_g", "ln_mix_b", "b_ff1", "b_ff2",
              "ln_ff_g", "ln_ff_b"):
        P[k] = env[k].reshape(L, 1, -1)
    n_cw = conv_w.size
    cw = first[:, 8:].reshape(N_CHIP, 2, -1)[:, 0, :n_cw].reshape(N_CHIP, L, CONV_WIDTH, D_CONV // N_CHIP)
    P["conv_w"] = jnp.transpose(cw, (1, 2, 0, 3)).reshape(L, CONV_WIDTH, D_CONV)
    wp_bd = jnp.zeros((L, D_POOL, D_POOL), F32)
    for g in range(len(POOL_WINDOWS)):
        sl = slice(POOL_GROUP * g, POOL_GROUP * (g + 1))
        wp_bd = wp_bd.at[:, sl, sl].set(w_pool[:, g])
    P["wp_bd"] = wp_bd.astype(BF16)

    acts = []
    h = xs
    for l in range(L):
        h, saved = _layer_fwd(l, h, mod, W, P, fwd_hooks[l])
        acts.append(saved)

    place = jnp.stack([ci, chip, chip ^ 1, chip ^ 2, chip ^ 3]).astype(jnp.int32)
    scattered = {}

    def grad_scatter(items, tag):
        dws = [dw for _, _, dw in items]
        got = _sibling_send(dws, f"swap_blocks_{tag}", other_half=True)
        both = [_sum_cores(a, b, place, f"sum_cores_{k}{l}") for (k, l, _), a, b in zip(items, dws, got)]
        both = [hh.reshape(N_CHIP, -1, hh.shape[-1]) for hh in both]

        def done(outs):
            for (k, l, _), hh, r in zip(items, both, outs):
                scattered[(k, l)] = (hh, r)

        return _scatter_rider(both), done

    early = ("w_ff2", "w_ff1", "w_o", "w_br_pool", "w_br_attn", "w_br_conv")
    left_over = []

    def attn_hook(l):
        def hook(gw):
            items = left_over + [(k, l, gw[k]) for k in early]
            left_over.clear()
            return grad_scatter(items, f"attn{l}")
        return hook

    def last_hook(gw):
        return grad_scatter([("w_in", 0, gw["w_in"])], "last")

    gws, gss, dmods = [None] * L, [None] * L, [None] * L
    dh = h
    for l in reversed(range(L)):
        hooks = {"attn": attn_hook(l)}
        if l == 0:
            hooks["du_mix"] = last_hook
        if l == L - 1:
            dh, gws[l], gss[l], dmods[l], loss_part = _layer_bwd(l, dh, mod, W, P, acts[l], hooks, tgt=tgt)
        else:
            dh, gws[l], gss[l], dmods[l] = _layer_bwd(l, dh, mod, W, P, acts[l], hooks)
        if l > 0:
            left_over.append(("w_in", l, gws[l]["w_in"]))
    grad_x = dh[None]
    loss = lax.psum(loss_part[0, 0], ("x", "y", "c"))

    reduced = [[_sum_chips(*scattered[(k, l)], place, f"sum_chips_{k}{l}") for l in range(L)] for k in BIG]
    flat_reduced = [t for per_weight in reduced for t in per_weight]

    shapes = _small_shapes()
    small_names = [k for k in SMALL if k != "b_ada"]
    dmod_own = jnp.concatenate(dmods, axis=0)
    pack = _pack([dmod_own] + [jnp.stack([gss[l][k].reshape(shapes[k]) for l in range(L)]) for k in small_names])
    last = _run_rider(_join_riders(_sibling_rider(flat_reduced), _allgather_rider(pack.reshape(-1, 128))),
                      "swap_reduced_gather_small")
    flat_other, g_all = last[:-1], last[-1].reshape(N_DEV, -1, PACK_W)

    out = {}
    for j, k in enumerate(BIG):
        own, other = reduced[j], flat_other[L * j:L * (j + 1)]
        if k == "w_in":
            t = lambda a: jnp.swapaxes(a, 1, 2)
            res = _adamw_halves(t(env[k]), t(env["m_" + k]), t(env["v_" + k]), own, other, place, "cols", f"adamw_{k}")
            res = [t(a) for a in res]
        else:
            if k in COL_SHARDED:
                own, other = [a.T for a in own], [a.T for a in other]
            res = _adamw_halves(env[k], env["m_" + k], env["v_" + k], own, other, place,
                                "rows" if k in COL_SHARDED else "cols", f"adamw_{k}")
        out[k] = tuple(res)

    dmod_all = g_all[:, :L * 6].reshape(N_DEV, L, 6 * D_MODEL)
    dmod_sh = jnp.transpose(lax.dynamic_slice_in_dim(dmod_all, chip * ada_cols, ada_cols, axis=2), (1, 0, 2))
    g_ada = _mod_bwd(c_all, dmod_sh, "mod_bwd")
    g_, d_, m_, v_ = _adamw(w_ada.reshape(-1, ada_cols), m_w_ada.reshape(-1, ada_cols), v_w_ada.reshape(-1, ada_cols),
                            [g_ada.reshape(-1, ada_cols)], "adamw_w_ada")
    out["w_ada"] = tuple(a.reshape(w_ada.shape) for a in (g_, d_, m_, v_))

    def small_pack(prefix):
        parts = [env[prefix + "b_ada"]]
        for k in small_names:
            a = env[prefix + k]
            if k == "conv_w":
                a = jnp.zeros((L,) + shapes[k], F32)
            parts.append(a)
        return _pack(parts)

    gp, dp, mp, vp = _adamw_small(small_pack(""), small_pack("m_"), small_pack("v_"), g_all, "adamw_small")
    full_shapes = [(L,) + shapes["b_ada"]] + [(L,) + shapes[k] for k in small_names]
    for tag, packed in (("g", gp), ("d", dp), ("m", mp), ("v", vp)):
        for k, a in zip(["b_ada"] + small_names, _unpack(packed, full_shapes)):
            out.setdefault(k, {})
            out[k][tag] = a
    g_cw_full = out["conv_w"]["g"]
    cw_cols = D_CONV // N_CHIP
    g_cw = lax.dynamic_slice_in_dim(g_cw_full, chip * cw_cols, cw_cols, axis=2)
    pad_rows = lambda a: jnp.pad(a.reshape(L * CONV_WIDTH, cw_cols), ((0, 2), (0, 0)))
    g_, d_, m_, v_ = _adamw(pad_rows(conv_w), pad_rows(m_conv_w), pad_rows(v_conv_w), [pad_rows(g_cw)], "adamw_conv_w")
    out["conv_w"] = tuple(a[:L * CONV_WIDTH].reshape(L, CONV_WIDTH, cw_cols) for a in (g_, d_, m_, v_))

    names = ["w_ada", "b_ada", "w_in", "b_gate", "w_pool", "pool_scale", "rel_bias", "conv_w", "conv_b", "conv_ln_g",
             "conv_ln_b", "w_br_pool", "w_br_attn", "w_br_conv", "w_o", "ln_mix_g", "ln_mix_b", "w_ff1", "b_ff1",
             "w_ff2", "b_ff2", "ln_ff_g", "ln_ff_b"]

    def pick(k, i):
        o = out[k]
        return o[i] if isinstance(o, tuple) else o["gdmv"[i]].reshape(env[k].shape)

    return (loss, grad_x, *[pick(k, 0) for k in names], *[pick(k, 1) for k in names],
            *[pick(k, 2) for k in names], *[pick(k, 3) for k in names])
```

```python
import functools

import jax
import jax.numpy as jnp
import numpy as np
from jax import lax
from jax.experimental import pallas as pl
from jax.experimental.pallas import tpu as pltpu

F32 = jnp.float32
BF16 = jnp.bfloat16

D_MODEL = 1024
DEPTH = 2
CHUNK = 64
POOL_WINDOWS = (2, 4, 8, 16)
POOL_GROUP = 64
D_POOL = 256
N_HEADS = 8
HEAD_DIM = 64
D_ATTN = 512
N_PREV_CHUNKS = 8
REL_CLIP = 128
N_REL = 2 * REL_CLIP + 1
D_CONV = 256
CONV_WIDTH = 31
D_FF = 4 * D_MODEL
D_IN = 5376
OFF_POOL, OFF_QKV, OFF_CONV, OFF_GATE = 0, 256, 1792, 2304
ALPHA = (2.0 * DEPTH) ** 0.25
LN_EPS = 1e-5
NEG_INF = -1e30
ADAM_LR, ADAM_B1, ADAM_B2, ADAM_EPS, ADAM_WD, ADAM_STEP = 0.001, 0.9, 0.999, 1e-08, 0.01, 10

N_DEV = 8
N_CHIP = 4
MESH = pl.DeviceIdType.MESH

QB = 2 * CHUNK
KPAD = N_PREV_CHUNKS * CHUNK
KW = QB + KPAD
SKEW_W = 768

VMEM_LIMIT = 56 * 1024 * 1024


def _cparams(**kw):
    return pltpu.CompilerParams(vmem_limit_bytes=VMEM_LIMIT, **kw)


def _full(shape):
    n = len(shape)
    return pl.BlockSpec(shape, lambda *_: (0,) * n)


_DIMS = {"nn": (((1,), (0,)), ((), ())), "nt": (((1,), (1,)), ((), ())), "tn": (((0,), (0,)), ((), ()))}


def _relu2(t):
    r = jnp.maximum(t, 0.0)
    return r * r


def _mm(a, b, mode, *, tm, tn, out_dtype, name, b_col0=0, n_out=None, bias=None, split_n=0, rider=None):
    if mode == "tn":
        k, m = a.shape
        n = b.shape[1] if n_out is None else n_out
        a_spec = pl.BlockSpec((k, tm), lambda i, j: (0, i))
        b_spec = pl.BlockSpec((k, tn), lambda i, j: (0, j + b_col0))
    elif mode == "nn":
        m, k = a.shape
        n = b.shape[1] if n_out is None else n_out
        a_spec = pl.BlockSpec((tm, k), lambda i, j: (i, 0))
        b_spec = pl.BlockSpec((k, tn), lambda i, j: (0, j + b_col0))
    else:
        m, k = a.shape
        n = b.shape[0] if n_out is None else n_out
        a_spec = pl.BlockSpec((tm, k), lambda i, j: (i, 0))
        b_spec = pl.BlockSpec((tn, k), lambda i, j: (j + b_col0, 0))
    assert m % tm == 0 and n % tn == 0, (name, m, n, tm, tn)
    dims = _DIMS[mode]

    def body(*refs):
        if bias is None:
            a_ref, b_ref, o_ref = refs
        else:
            a_ref, b_ref, bias_ref, o_ref = refs
        acc = lax.dot_general(a_ref[...].astype(BF16), b_ref[...].astype(BF16), dims, preferred_element_type=F32)
        if bias is not None:
            acc = acc + bias_ref[...]
        if split_n:
            for c in range(tn // split_n):
                o_ref[c] = acc[:, c * split_n:(c + 1) * split_n].astype(out_dtype)
        else:
            o_ref[...] = acc.astype(out_dtype)

    in_specs = [a_spec, b_spec]
    args = [a, b]
    if bias is not None:
        in_specs.append(pl.BlockSpec((1, tn), lambda i, j: (0, j)))
        args.append(bias)
    if split_n:
        out_spec = pl.BlockSpec((tn // split_n, tm, split_n), lambda i, j: (j, i, 0))
        out_shape = jax.ShapeDtypeStruct((n // split_n, m, split_n), out_dtype)
    else:
        out_spec = pl.BlockSpec((tm, tn), lambda i, j: (i, j))
        out_shape = jax.ShapeDtypeStruct((m, n), out_dtype)
    res = _call(body, name=name, grid=(m // tm, n // tn), in_specs=in_specs, out_specs=[out_spec],
                out_shape=[out_shape], scratch_shapes=[], args=args, rider=rider)
    return res[0] if rider is None else (res[0][0], res[1])


def _ln_hat(x):
    mu = jnp.mean(x, axis=-1, keepdims=True)
    xc = x - mu
    var = jnp.mean(xc * xc, axis=-1, keepdims=True)
    rstd = lax.rsqrt(var + LN_EPS)
    return xc * rstd, rstd


def _ln_hat_bwd(dhat, xhat, rstd):
    m1 = jnp.mean(dhat, axis=-1, keepdims=True)
    m2 = jnp.mean(dhat * xhat, axis=-1, keepdims=True)
    return rstd * (dhat - m1 - xhat * m2)


def _row_tile(s):
    return min(512, s)


def _acc_rows(ref, val, first):
    @pl.when(first)
    def _():
        ref[...] = jnp.zeros_like(ref)
    ref[...] += jnp.sum(val, axis=0, keepdims=True)


def _ln_mod(x, sc, sh, name):
    s, d = x.shape
    tm = _row_tile(s)

    def body(x_ref, sc_ref, sh_ref, u_ref):
        xhat, _ = _ln_hat(x_ref[...])
        u_ref[...] = (xhat * (1.0 + sc_ref[...]) + sh_ref[...]).astype(BF16)

    row = pl.BlockSpec((tm, d), lambda i: (i, 0))
    vec = pl.BlockSpec((1, d), lambda i: (0, 0))
    return pl.pallas_call(body, grid=(s // tm,), in_specs=[row, vec, vec], out_specs=row,
                          out_shape=jax.ShapeDtypeStruct((s, d), BF16), name=name, compiler_params=_cparams())(x, sc, sh)


def _resid_bwd_tile(dxo, x, f, g, gam):
    rhat, rstd = _ln_hat(ALPHA * x + g * f)
    dr = _ln_hat_bwd(dxo * gam, rhat, rstd)
    return ALPHA * dr, g * dr, dxo * rhat, dr * f


def _mm_ln_mod_bwd(a, b, x, sc, dres, name, rider=None, nxt=None):
    segs = list(a) if isinstance(a, (list, tuple)) else [a]
    s = segs[0].shape[0]
    k, d = b.shape
    assert sum(t.shape[1] for t in segs) == k
    tm = min(512 if k <= 4096 and nxt is None else 256, s)
    ns = len(segs)

    def body(*refs):
        seg_refs = refs[:ns]
        if nxt is None:
            b_ref, x_ref, sc_ref, dres_ref, dx_ref, dsc_ref, dsh_ref = refs[ns:]
        else:
            (b_ref, x_ref, sc_ref, dres_ref, xp_ref, fp_ref, gp_ref, gamp_ref,
             dresp_ref, dfp_ref, dsc_ref, dsh_ref, dgam_ref, dbet_ref, dg_ref, dbias_ref) = refs[ns:]
        first = pl.program_id(0) == 0
        duv, r0 = None, 0
        for seg_ref in seg_refs:
            w = seg_ref.shape[1]
            part = jnp.dot(seg_ref[...], b_ref[r0:r0 + w, :], preferred_element_type=F32)
            duv = part if duv is None else duv + part
            r0 += w
        xhat, rstd = _ln_hat(x_ref[...])
        dxv = dres_ref[...] + _ln_hat_bwd(duv * (1.0 + sc_ref[...]), xhat, rstd)
        _acc_rows(dsc_ref, duv * xhat, first)
        _acc_rows(dsh_ref, duv, first)
        if nxt is None:
            dx_ref[...] = dxv
        else:
            dresp, dfp, t_gam, t_g = _resid_bwd_tile(dxv, xp_ref[...], fp_ref[...], gp_ref[...], gamp_ref[...])
            dresp_ref[...] = dresp
            dfp_ref[...] = dfp.astype(BF16)
            _acc_rows(dgam_ref, t_gam, first)
            _acc_rows(dbet_ref, dxv, first)
            _acc_rows(dg_ref, t_g, first)
            _acc_rows(dbias_ref, dfp, first)

    row = pl.BlockSpec((tm, d), lambda i: (i, 0))
    vec = pl.BlockSpec((1, d), lambda i: (0, 0))
    vs = jax.ShapeDtypeStruct((1, d), F32)
    rows = jax.ShapeDtypeStruct((s, d), F32)
    in_specs = [pl.BlockSpec((tm, t.shape[1]), lambda i: (i, 0)) for t in segs] + [_full((k, d)), row, vec, row]
    args = (*segs, b, x, sc, dres)
    if nxt is None:
        out_specs, out_shape = [row, vec, vec], [rows, vs, vs]
    else:
        in_specs += [row, row, vec, vec]
        args += tuple(nxt)
        out_specs = [row, row] + [vec] * 6
        out_shape = [rows, jax.ShapeDtypeStruct((s, d), BF16)] + [vs] * 6
    res = _call(body, name=name, grid=(s // tm,), in_specs=in_specs, out_specs=out_specs, out_shape=out_shape,
                scratch_shapes=[], args=args, rider=rider)
    return tuple(res) if rider is None else (tuple(res[0]), res[1])


def _dw_segments(segs, u, name):
    s, d = u.shape
    tw = 256
    tiles = [t.shape[1] // tw for t in segs]
    starts = [sum(tiles[:j]) for j in range(len(segs))]
    ns = len(segs)

    def body(*refs):
        seg_refs, u_ref, o_ref = refs[:ns], refs[ns], refs[ns + 1]
        i = pl.program_id(0)
        for seg_ref, t0, nt in zip(seg_refs, starts, tiles):
            @pl.when((i >= t0) & (i < t0 + nt))
            def _(seg_ref=seg_ref):
                acc = lax.dot_general(seg_ref[...], u_ref[...], _DIMS["tn"], preferred_element_type=F32)
                o_ref[0] = acc[:, :d // 2].astype(BF16)
                o_ref[1] = acc[:, d // 2:].astype(BF16)

    def seg_spec(t0, nt):
        return pl.BlockSpec((s, tw), lambda i: (0, jnp.clip(i - t0, 0, nt - 1)))

    return pl.pallas_call(
        body, grid=(sum(tiles),), in_specs=[seg_spec(t0, nt) for t0, nt in zip(starts, tiles)] + [_full((s, d))],
        out_specs=pl.BlockSpec((2, tw, d // 2), lambda i: (0, i, 0)),
        out_shape=jax.ShapeDtypeStruct((2, sum(tiles) * tw, d // 2), BF16), name=name, compiler_params=_cparams(),
    )(*segs, u)


def _mm_resid_ln(a, b, bias, x, g, gam, bet, name, rider=None):
    s, k = a.shape
    d = b.shape[1]
    tm = min(512, s)

    def body(*refs):
        if bias is None:
            a_ref, b_ref, x_ref, g_ref, gam_ref, bet_ref, f_ref, o_ref = refs
        else:
            a_ref, b_ref, bias_ref, x_ref, g_ref, gam_ref, bet_ref, f_ref, o_ref = refs
        f = jnp.dot(a_ref[...], b_ref[...], preferred_element_type=F32)
        if bias is not None:
            f = f + bias_ref[...]
        f_ref[...] = f
        rhat, _ = _ln_hat(ALPHA * x_ref[...] + g_ref[...] * f)
        o_ref[...] = rhat * gam_ref[...] + bet_ref[...]

    row = pl.BlockSpec((tm, d), lambda i: (i, 0))
    vec = pl.BlockSpec((1, d), lambda i: (0, 0))
    in_specs = [pl.BlockSpec((tm, k), lambda i: (i, 0)), _full((k, d))] + ([vec] if bias is not None else []) + [row, vec, vec, vec]
    args = [a, b] + ([bias] if bias is not None else []) + [x, g, gam, bet]
    sh = jax.ShapeDtypeStruct((s, d), F32)
    res = _call(body, name=name, grid=(s // tm,), in_specs=in_specs, out_specs=[row, row], out_shape=[sh, sh],
                scratch_shapes=[], args=args, rider=rider)
    return tuple(res) if rider is None else (tuple(res[0]), res[1])


def _resid_ln_bwd(dxo, x, f, g, gam, name, tgt=None):
    s, d = x.shape
    tm = _row_tile(s)
    n = s // tm

    def body(*refs):
        if tgt is None:
            dxo_ref, x_ref, f_ref, g_ref, gam_ref, dres_ref, df_ref, dgam_ref, dbet_ref, dg_ref, dbias_ref = refs
            dxov = dxo_ref[...]
        else:
            (dxo_ref, t_ref, x_ref, f_ref, g_ref, gam_ref, dres_ref, df_ref, dgam_ref, dbet_ref, dg_ref, dbias_ref,
             loss_ref, sq_ref) = refs
            err = dxo_ref[...] - t_ref[...]
            dxov = err * (1.0 / d)
            _acc_rows(sq_ref, err * err, pl.program_id(0) == 0)

            @pl.when(pl.program_id(0) == n - 1)
            def _():
                tot = jnp.sum(sq_ref[...], axis=1, keepdims=True) * (0.5 / d)
                loss_ref[...] = jnp.broadcast_to(tot, (1, 128))

        first = pl.program_id(0) == 0
        dres, dfv, t_gam, t_g = _resid_bwd_tile(dxov, x_ref[...], f_ref[...], g_ref[...], gam_ref[...])
        dres_ref[...] = dres
        df_ref[...] = dfv.astype(BF16)
        _acc_rows(dgam_ref, t_gam, first)
        _acc_rows(dbet_ref, dxov, first)
        _acc_rows(dg_ref, t_g, first)
        _acc_rows(dbias_ref, dfv, first)

    row = pl.BlockSpec((tm, d), lambda i: (i, 0))
    vec = pl.BlockSpec((1, d), lambda i: (0, 0))
    vs = jax.ShapeDtypeStruct((1, d), F32)
    out_specs = [row, row, vec, vec, vec, vec]
    out_shape = [jax.ShapeDtypeStruct((s, d), F32), jax.ShapeDtypeStruct((s, d), BF16), vs, vs, vs, vs]
    if tgt is None:
        return pl.pallas_call(body, grid=(n,), in_specs=[row, row, row, vec, vec], out_specs=out_specs,
                              out_shape=out_shape, name=name, compiler_params=_cparams())(dxo, x, f, g, gam)
    return pl.pallas_call(body, grid=(n,), in_specs=[row, row, row, row, vec, vec],
                          out_specs=out_specs + [pl.BlockSpec((1, 128), lambda i: (0, 0))],
                          out_shape=out_shape + [jax.ShapeDtypeStruct((1, 128), F32)],
                          scratch_shapes=[pltpu.VMEM((1, d), F32)], name=name,
                          compiler_params=_cparams())(dxo, tgt, x, f, g, gam)


POOL_HALO = 16
POOL_ROWS = 256


def _pool_counts(r0, rows):
    t1 = (lax.broadcasted_iota(jnp.int32, (rows, 128), 0) + r0 + 1).astype(F32)
    low = lax.broadcasted_iota(jnp.int32, (rows, 128), 1) < POOL_GROUP
    wa = jnp.where(low, float(POOL_WINDOWS[0]), float(POOL_WINDOWS[1]))
    wb = jnp.where(low, float(POOL_WINDOWS[2]), float(POOL_WINDOWS[3]))
    return jnp.minimum(t1, wa), jnp.minimum(t1, wb), low


def _window_sums(win, off, rows, sign):
    def sl(j, half):
        return win[off + sign * j: off + sign * j + rows, 128 * half:128 * half + 128]
    a2 = sl(0, 0) + sl(1, 0)
    a4 = a2 + sl(2, 0) + sl(3, 0)
    a8 = sl(0, 1)
    for j in range(1, 8):
        a8 = a8 + sl(j, 1)
    a16 = a8
    for j in range(8, 16):
        a16 = a16 + sl(j, 1)
    return a2, a4, a8, a16


def _pool_fwd(zp, wp_bd, pscale, name):
    s = zp.shape[0]
    r = min(POOL_ROWS, s)

    def body(z_ref, wp_ref, sc_ref, p_ref, feat_ref, pad):
        pad[0:POOL_HALO, :] = jnp.zeros((POOL_HALO, D_POOL), F32)
        pad[POOL_HALO:, :] = z_ref[...]

        def step(i, carry):
            r0 = pl.multiple_of(i * r, r)
            win = pad[pl.ds(r0, r + POOL_HALO), :]
            a2, a4, a8, a16 = _window_sums(win, POOL_HALO, r, -1)
            ca, cb, low = _pool_counts(r0, r)
            x0 = win[POOL_HALO:, :]
            pa = jnp.where(low, a2, a4) / ca
            pb = jnp.where(low, a8, a16) / cb
            p = (jnp.concatenate([pa, pb], axis=1) - x0).astype(BF16)
            p_ref[pl.ds(r0, r), :] = p
            pw = jnp.dot(p, wp_ref[...], preferred_element_type=F32)
            feat_ref[pl.ds(r0, r), :] = (pw * sc_ref[...]).astype(BF16)
            return carry

        lax.fori_loop(0, s // r, step, 0)

    return pl.pallas_call(
        body, out_shape=[jax.ShapeDtypeStruct((s, D_POOL), BF16), jax.ShapeDtypeStruct((s, D_POOL), BF16)],
        scratch_shapes=[pltpu.VMEM((s + POOL_HALO, D_POOL), F32)], name=name, compiler_params=_cparams(),
    )(zp, wp_bd, pscale)


def _pool_bwd(dfeat, p, wp_bd, pscale, name):
    s = p.shape[0]
    r = min(POOL_ROWS, s)

    def body(df_ref, p_ref, wp_ref, sc_ref, dz_ref, dwp_ref, dsc_ref, gpad, dpbuf):
        dwp_ref[...] = jnp.zeros_like(dwp_ref)
        dsc_ref[...] = jnp.zeros_like(dsc_ref)
        gpad[s:, :] = jnp.zeros((POOL_HALO, D_POOL), F32)

        def step1(i, carry):
            r0 = pl.multiple_of(i * r, r)
            pv = p_ref[pl.ds(r0, r), :]
            dfv = df_ref[pl.ds(r0, r), :]
            pw = jnp.dot(pv, wp_ref[...], preferred_element_type=F32)
            dsc_ref[...] += jnp.sum(dfv * pw, axis=0, keepdims=True)
            dpw = (dfv * sc_ref[...]).astype(BF16)
            dwp_ref[...] += lax.dot_general(pv, dpw, _DIMS["tn"], preferred_element_type=F32)
            dp = lax.dot_general(dpw, wp_ref[...], _DIMS["nt"], preferred_element_type=F32)
            ca, cb, _ = _pool_counts(r0, r)
            gpad[pl.ds(r0, r), :] = dp / jnp.concatenate([ca, cb], axis=1)
            dpbuf[pl.ds(r0, r), :] = dp
            return carry

        lax.fori_loop(0, s // r, step1, 0)

        def step2(i, carry):
            r0 = pl.multiple_of(i * r, r)
            win = gpad[pl.ds(r0, r + POOL_HALO), :]
            a2, a4, a8, a16 = _window_sums(win, 0, r, 1)
            low = lax.broadcasted_iota(jnp.int32, (r, 128), 1) < POOL_GROUP
            acc = jnp.concatenate([jnp.where(low, a2, a4), jnp.where(low, a8, a16)], axis=1)
            dz_ref[pl.ds(r0, r), :] = (acc - dpbuf[pl.ds(r0, r), :]).astype(BF16)
            return carry

        lax.fori_loop(0, s // r, step2, 0)

    return pl.pallas_call(
        body,
        out_shape=[jax.ShapeDtypeStruct((s, D_POOL), BF16), jax.ShapeDtypeStruct((D_POOL, D_POOL), F32),
                   jax.ShapeDtypeStruct((1, D_POOL), F32)],
        scratch_shapes=[pltpu.VMEM((s + POOL_HALO, D_POOL), F32), pltpu.VMEM((s, D_POOL), F32)],
        name=name, compiler_params=_cparams(),
    )(dfeat, p, wp_bd, pscale)


def _skew_index():
    cp = lax.broadcasted_iota(jnp.int32, (SKEW_W, N_REL), 0)
    dist = jnp.where(cp < KW, KPAD - cp, KPAD + SKEW_W - cp)
    idx = jnp.clip(dist, -REL_CLIP, REL_CLIP) + REL_CLIP
    return (idx == lax.broadcasted_iota(jnp.int32, (SKEW_W, N_REL), 1)).astype(F32)


def _row_bits(b):
    return (lax.broadcasted_iota(jnp.int32, (QB, SKEW_W), 0) >> b) & 1 == 1


N_EDGE = KPAD // QB


def _bias_block(rel_bias, name):
    def body(rb_ref, o_ref):
        onehot = _skew_index()
        row0 = lax.dot_general(rb_ref[...], onehot, _DIMS["nt"], precision=lax.Precision.HIGHEST,
                               preferred_element_type=F32)
        r = lax.broadcasted_iota(jnp.int32, (QB, KW), 0)
        kk = lax.broadcasted_iota(jnp.int32, (QB, KW), 1)
        cq, ck = r // CHUNK, kk // CHUNK
        band = (ck >= cq) & (ck <= cq + N_PREV_CHUNKS)
        for h in range(N_HEADS):
            t = jnp.broadcast_to(row0[h:h + 1, :], (QB, SKEW_W))
            for b in range(7):
                t = jnp.where(_row_bits(b), pltpu.roll(t, 1 << b, 1), t)
            for e in range(N_EDGE + 1):
                o_ref[e, h] = jnp.where(band & (kk >= KPAD - e * QB), t[:, :KW], NEG_INF)

    return pl.pallas_call(body, out_shape=jax.ShapeDtypeStruct((N_EDGE + 1, N_HEADS, QB, KW), F32), name=name,
                          compiler_params=_cparams())(rel_bias)


def _bias_spec():
    return pl.BlockSpec((None, N_HEADS, QB, KW), lambda i: (jnp.minimum(i, N_EDGE), 0, 0, 0))


def _bias_block_bwd(ds_acc, name):
    def body(ds_ref, o_ref):
        sums = []
        for h in range(N_HEADS):
            t = jnp.concatenate([ds_ref[h], jnp.zeros((QB, SKEW_W - KW), F32)], axis=1)
            for b in range(7):
                t = jnp.where(_row_bits(b), pltpu.roll(t, SKEW_W - (1 << b), 1), t)
            sums.append(jnp.sum(t, axis=0, keepdims=True))
        allh = jnp.concatenate(sums, axis=0)
        o_ref[...] = jnp.dot(allh, _skew_index(), precision=lax.Precision.HIGHEST, preferred_element_type=F32)

    return pl.pallas_call(body, out_shape=jax.ShapeDtypeStruct((N_HEADS, N_REL), F32), name=name,
                          compiler_params=_cparams())(ds_acc)


def _scaled(q):
    return (q.astype(F32) * (HEAD_DIM ** -0.5)).astype(BF16)


def _probs(q, kw, bias_ref):
    sc = jnp.stack([lax.dot_general(q[:, HEAD_DIM * h:HEAD_DIM * (h + 1)], kw[:, HEAD_DIM * h:HEAD_DIM * (h + 1)],
                                    _DIMS["nt"], preferred_element_type=F32) + bias_ref[h] for h in range(N_HEADS)])
    e = jnp.exp(sc - jnp.max(sc, axis=-1, keepdims=True))
    return e * (1.0 / jnp.sum(e, axis=-1, keepdims=True))


def _load_padded_kv(qkv_hbm, kpad, vpad, sems, s):
    kpad[0:KPAD, :] = jnp.zeros((KPAD, D_ATTN), BF16)
    vpad[0:KPAD, :] = jnp.zeros((KPAD, D_ATTN), BF16)
    ck = pltpu.make_async_copy(qkv_hbm.at[:, D_ATTN:2 * D_ATTN], kpad.at[pl.ds(KPAD, s), :], sems.at[0])
    cv = pltpu.make_async_copy(qkv_hbm.at[:, 2 * D_ATTN:3 * D_ATTN], vpad.at[pl.ds(KPAD, s), :], sems.at[1])
    ck.start()
    cv.start()
    ck.wait()
    cv.wait()


def _attn_fwd(qkv, bias, name, rider=None):
    s = qkv.shape[0]

    def body(q_ref, qkv_hbm, bias_ref, o_ref, kpad, vpad, sems):
        i = pl.program_id(0)

        @pl.when(i == 0)
        def _():
            _load_padded_kv(qkv_hbm, kpad, vpad, sems, s)

        base = pl.multiple_of(i * QB, QB)
        kw = kpad[pl.ds(base, KW), :]
        vw = vpad[pl.ds(base, KW), :]
        q = _scaled(q_ref[...])
        p = _probs(q, kw, bias_ref).astype(BF16)
        outs = [jnp.dot(p[h], vw[:, HEAD_DIM * h:HEAD_DIM * (h + 1)], preferred_element_type=F32)
                for h in range(N_HEADS)]
        o_ref[...] = jnp.concatenate(outs, axis=1).astype(BF16)

    res = _call(
        body, name=name, grid=(s // QB,),
        in_specs=[pl.BlockSpec((QB, D_ATTN), lambda i: (i, 0)), pl.BlockSpec(memory_space=pl.ANY),
                  _bias_spec()],
        out_specs=[pl.BlockSpec((QB, D_ATTN), lambda i: (i, 0))],
        out_shape=[jax.ShapeDtypeStruct((s, D_ATTN), BF16)],
        scratch_shapes=[pltpu.VMEM((s + KPAD, D_ATTN), BF16), pltpu.VMEM((s + KPAD, D_ATTN), BF16),
                        pltpu.SemaphoreType.DMA((2,))],
        args=(qkv, qkv, bias), rider=rider)
    return res[0] if rider is None else (res[0][0], res[1])


def _attn_bwd(qkv, do, bias, name, rider=None):
    s = qkv.shape[0]
    n = s // QB

    def body(q_ref, qkv_hbm, do_ref, bias_ref, dq_ref, dk_hbm, dv_hbm, ds_ref, kpad, vpad, dkacc, dvacc, sems):
        i = pl.program_id(0)

        @pl.when(i == 0)
        def _():
            _load_padded_kv(qkv_hbm, kpad, vpad, sems, s)
            dkacc[...] = jnp.zeros_like(dkacc)
            dvacc[...] = jnp.zeros_like(dvacc)
            ds_ref[...] = jnp.zeros_like(ds_ref)

        base = pl.multiple_of(i * QB, QB)
        kw = kpad[pl.ds(base, KW), :]
        vw = vpad[pl.ds(base, KW), :]
        q = _scaled(q_ref[...])
        dov = do_ref[...]
        heads = [slice(HEAD_DIM * h, HEAD_DIM * (h + 1)) for h in range(N_HEADS)]
        p = _probs(q, kw, bias_ref)
        dp = jnp.stack([lax.dot_general(dov[:, hs], vw[:, hs], _DIMS["nt"], preferred_element_type=F32) for hs in heads])
        ds = p * (dp - jnp.sum(dp * p, axis=-1, keepdims=True))
        ds_ref[...] += ds
        pb, dsb = p.astype(BF16), ds.astype(BF16)
        dvs = [lax.dot_general(pb[h], dov[:, hs], _DIMS["tn"], preferred_element_type=F32) for h, hs in enumerate(heads)]
        dqs = [jnp.dot(dsb[h], kw[:, hs], preferred_element_type=F32) for h, hs in enumerate(heads)]
        dks = [lax.dot_general(dsb[h], q[:, hs], _DIMS["tn"], preferred_element_type=F32) for h, hs in enumerate(heads)]
        dq_ref[...] = (jnp.concatenate(dqs, axis=1) * (HEAD_DIM ** -0.5)).astype(BF16)
        dkacc[pl.ds(base, KW), :] += jnp.concatenate(dks, axis=1)
        dvacc[pl.ds(base, KW), :] += jnp.concatenate(dvs, axis=1)

        @pl.when(i == n - 1)
        def _():
            def cast(j, carry):
                rows = pl.ds(pl.multiple_of(KPAD + j * 512, 512), 512)
                kpad[rows, :] = dkacc[rows, :].astype(BF16)
                vpad[rows, :] = dvacc[rows, :].astype(BF16)
                return carry

            lax.fori_loop(0, s // 512, cast, 0)
            ck = pltpu.make_async_copy(kpad.at[pl.ds(KPAD, s), :], dk_hbm, sems.at[0])
            cv = pltpu.make_async_copy(vpad.at[pl.ds(KPAD, s), :], dv_hbm, sems.at[1])
            ck.start()
            cv.start()
            ck.wait()
            cv.wait()

    blk = pl.BlockSpec((QB, D_ATTN), lambda i: (i, 0))
    acc_shape = jax.ShapeDtypeStruct((s, D_ATTN), BF16)
    return _call(
        body, name=name, grid=(n,),
        in_specs=[blk, pl.BlockSpec(memory_space=pl.ANY), blk, _bias_spec()],
        out_specs=[blk, pl.BlockSpec(memory_space=pl.ANY), pl.BlockSpec(memory_space=pl.ANY), _full((N_HEADS, QB, KW))],
        out_shape=[jax.ShapeDtypeStruct((s, D_ATTN), BF16), acc_shape, acc_shape,
                   jax.ShapeDtypeStruct((N_HEADS, QB, KW), F32)],
        scratch_shapes=[pltpu.VMEM((s + KPAD, D_ATTN), BF16), pltpu.VMEM((s + KPAD, D_ATTN), BF16),
                        pltpu.VMEM((s + KPAD, D_ATTN), F32), pltpu.VMEM((s + KPAD, D_ATTN), F32),
                        pltpu.SemaphoreType.DMA((2,))],
        args=(qkv, qkv, do, bias), rider=rider)


CONV_HALO = 32
CONV_ROWS = 64


def _sigmoid(t):
    return 1.0 / (1.0 + jnp.exp(-t))


CONV_WIN = CONV_ROWS + CONV_HALO - 8


def _row_windows(ref, r0, buf):
    win = ref[pl.ds(r0, CONV_ROWS + CONV_HALO), :]
    for j in range(1, 8):
        buf[j - 1] = win[j:j + CONV_WIN, :]

    def get(o):
        j, a = o % 8, o - o % 8
        if j == 0:
            return ref[pl.ds(r0 + a, CONV_ROWS), :]
        return buf[j - 1, a:a + CONV_ROWS, :]

    return get


def _glu_rows(z_ref, r0, rows):
    a = z_ref[pl.ds(r0, rows), 0:D_CONV]
    b = z_ref[pl.ds(r0, rows), D_CONV:2 * D_CONV]
    return a, _sigmoid(b)


def _conv_fwd(zc, conv_w, conv_b, ln_g, ln_b, name):
    s = zc.shape[0]
    rt = min(256, s)

    def body(z_ref, w_ref, cb_ref, g_ref, b_ref, cv_ref, feat_ref, hpad, shifts):
        hpad[0:CONV_HALO, :] = jnp.zeros((CONV_HALO, D_CONV), F32)

        def glu(i, carry):
            r0 = pl.multiple_of(i * rt, rt)
            a, sb = _glu_rows(z_ref, r0, rt)
            hpad[pl.ds(r0 + CONV_HALO, rt), :] = a * sb
            return carry

        lax.fori_loop(0, s // rt, glu, 0)
        w = w_ref[...]

        def conv(i, carry):
            r0 = pl.multiple_of(i * CONV_ROWS, CONV_ROWS)
            win = _row_windows(hpad, r0, shifts)
            acc = jnp.broadcast_to(cb_ref[...], (CONV_ROWS, D_CONV))
            for k in range(CONV_WIDTH):
                acc = acc + win(2 + k) * w[k:k + 1, :]
            cv_ref[pl.ds(r0, CONV_ROWS), :] = acc
            yhat, _ = _ln_hat(acc)
            y = yhat * g_ref[...] + b_ref[...]
            feat_ref[pl.ds(r0, CONV_ROWS), :] = (y * _sigmoid(y)).astype(BF16)
            return carry

        lax.fori_loop(0, s // CONV_ROWS, conv, 0)

    return pl.pallas_call(
        body, out_shape=[jax.ShapeDtypeStruct((s, D_CONV), F32), jax.ShapeDtypeStruct((s, D_CONV), BF16)],
        scratch_shapes=[pltpu.VMEM((s + CONV_HALO, D_CONV), F32), pltpu.VMEM((7, CONV_WIN, D_CONV), F32)],
        name=name, compiler_params=_cparams(),
    )(zc, conv_w, conv_b, ln_g, ln_b)


def _conv_bwd(dfeat, cv, zc, conv_w, ln_g, ln_b, name):
    s = zc.shape[0]
    rt = min(256, s)

    def body(df_ref, cv_ref, z_ref, w_ref, g_ref, b_ref, dz_ref, dw_ref, dcb_ref, dg_ref, db_ref, hpad, dcvpad, dwacc,
             hshifts, dshifts):
        hpad[0:CONV_HALO, :] = jnp.zeros((CONV_HALO, D_CONV), F32)
        dcvpad[s:, :] = jnp.zeros((CONV_HALO, D_CONV), F32)
        dwacc[...] = jnp.zeros_like(dwacc)
        dcb_ref[...] = jnp.zeros_like(dcb_ref)
        dg_ref[...] = jnp.zeros_like(dg_ref)
        db_ref[...] = jnp.zeros_like(db_ref)

        def pass1(i, carry):
            r0 = pl.multiple_of(i * rt, rt)
            a, sb = _glu_rows(z_ref, r0, rt)
            hpad[pl.ds(r0 + CONV_HALO, rt), :] = a * sb
            cvhat, rstd = _ln_hat(cv_ref[pl.ds(r0, rt), :])
            y = cvhat * g_ref[...] + b_ref[...]
            sg = _sigmoid(y)
            dy = df_ref[pl.ds(r0, rt), :] * (sg * (1.0 + y * (1.0 - sg)))
            dg_ref[...] += jnp.sum(dy * cvhat, axis=0, keepdims=True)
            db_ref[...] += jnp.sum(dy, axis=0, keepdims=True)
            dcv = _ln_hat_bwd(dy * g_ref[...], cvhat, rstd)
            dcb_ref[...] += jnp.sum(dcv, axis=0, keepdims=True)
            dcvpad[pl.ds(r0, rt), :] = dcv
            return carry

        lax.fori_loop(0, s // rt, pass1, 0)
        w = w_ref[...]

        def pass2(i, carry):
            r0 = pl.multiple_of(i * CONV_ROWS, CONV_ROWS)
            dwin = _row_windows(dcvpad, r0, dshifts)
            hwin = _row_windows(hpad, r0, hshifts)
            dcv = dwin(0)
            dh = jnp.zeros((CONV_ROWS, D_CONV), F32)
            for k in range(CONV_WIDTH):
                dh = dh + dwin(30 - k) * w[k:k + 1, :]
                prod = dcv * hwin(2 + k)
                dwacc[8 * k:8 * k + 8, :] += jnp.sum(prod.reshape(CONV_ROWS // 8, 8, D_CONV), axis=0)
            a, sb = _glu_rows(z_ref, r0, CONV_ROWS)
            dz_ref[pl.ds(r0, CONV_ROWS), :] = jnp.concatenate([dh * sb, dh * a * sb * (1.0 - sb)], axis=1).astype(BF16)
            return carry

        lax.fori_loop(0, s // CONV_ROWS, pass2, 0)
        dw_ref[...] = jnp.sum(dwacc[...].reshape(32, 8, D_CONV), axis=1)

    vs = jax.ShapeDtypeStruct((1, D_CONV), F32)
    return pl.pallas_call(
        body,
        out_shape=[jax.ShapeDtypeStruct((s, 2 * D_CONV), BF16), jax.ShapeDtypeStruct((32, D_CONV), F32), vs, vs, vs],
        scratch_shapes=[pltpu.VMEM((s + CONV_HALO, D_CONV), F32), pltpu.VMEM((s + CONV_HALO, D_CONV), F32),
                        pltpu.VMEM((256, D_CONV), F32), pltpu.VMEM((7, CONV_WIN, D_CONV), F32),
                        pltpu.VMEM((7, CONV_WIN, D_CONV), F32)],
        name=name, compiler_params=_cparams(),
    )(dfeat, cv, zc, conv_w, ln_g, ln_b)


def _merge(zg, b_gate, ys, name):
    s = zg.shape[0]
    tm = _row_tile(s)

    def body(zg_ref, bg_ref, y0_ref, y1_ref, y2_ref, o_ref):
        acc = None
        for j, y_ref in enumerate((y0_ref, y1_ref, y2_ref)):
            cs = slice(D_MODEL * j, D_MODEL * (j + 1))
            t = _sigmoid(zg_ref[:, cs] + bg_ref[:, cs]) * y_ref[...]
            acc = t if acc is None else acc + t
        o_ref[...] = acc.astype(BF16)

    row = pl.BlockSpec((tm, D_MODEL), lambda i: (i, 0))
    return pl.pallas_call(
        body, grid=(s // tm,),
        in_specs=[pl.BlockSpec((tm, 3 * D_MODEL), lambda i: (i, 0)), _full((1, 3 * D_MODEL)), row, row, row],
        out_specs=row, out_shape=jax.ShapeDtypeStruct((s, D_MODEL), BF16), name=name, compiler_params=_cparams(),
    )(zg, b_gate, *ys)


def _merge_bwd(dm, zg, b_gate, ys, name):
    s = zg.shape[0]
    tm = min(256, s)

    def body(dm_ref, zg_ref, bg_ref, y0_ref, y1_ref, y2_ref, d0_ref, d1_ref, d2_ref, dzg_ref, dbg_ref):
        first = pl.program_id(0) == 0

        @pl.when(first)
        def _():
            dbg_ref[...] = jnp.zeros_like(dbg_ref)

        dmv = dm_ref[...]
        for j, (y_ref, d_ref) in enumerate(((y0_ref, d0_ref), (y1_ref, d1_ref), (y2_ref, d2_ref))):
            cs = slice(D_MODEL * j, D_MODEL * (j + 1))
            g = _sigmoid(zg_ref[:, cs] + bg_ref[:, cs])
            d_ref[...] = (dmv * g).astype(BF16)
            dzg = dmv * y_ref[...] * g * (1.0 - g)
            dzg_ref[:, cs] = dzg.astype(BF16)
            dbg_ref[:, cs] += jnp.sum(dzg, axis=0, keepdims=True)

    row = pl.BlockSpec((tm, D_MODEL), lambda i: (i, 0))
    wide = pl.BlockSpec((tm, 3 * D_MODEL), lambda i: (i, 0))
    yb = jax.ShapeDtypeStruct((s, D_MODEL), BF16)
    return pl.pallas_call(
        body, grid=(s // tm,),
        in_specs=[row, wide, _full((1, 3 * D_MODEL)), row, row, row],
        out_specs=[row, row, row, wide, _full((1, 3 * D_MODEL))],
        out_shape=[yb, yb, yb, jax.ShapeDtypeStruct((s, 3 * D_MODEL), BF16), jax.ShapeDtypeStruct((1, 3 * D_MODEL), F32)],
        name=name, compiler_params=_cparams(),
    )(dm, zg, b_gate, *ys)


def _ff_hidden(u2, w_ff1t, b_ff1, name, rider=None):
    s = u2.shape[0]
    tm, tn = min(1024, s), 1024

    def body(a_ref, b_ref, bias_ref, pre_ref, h_ref):
        acc = lax.dot_general(a_ref[...], b_ref[...], _DIMS["nt"], preferred_element_type=F32) + bias_ref[...]
        pre_ref[...] = acc.astype(BF16)
        h_ref[...] = _relu2(acc).astype(BF16)

    blk = pl.BlockSpec((tm, tn), lambda i, j: (i, j))
    sh = jax.ShapeDtypeStruct((s, D_FF), BF16)
    res = _call(body, name=name, grid=(s // tm, D_FF // tn),
                in_specs=[pl.BlockSpec((tm, D_MODEL), lambda i, j: (i, 0)), pl.BlockSpec((tn, D_MODEL), lambda i, j: (j, 0)),
                          pl.BlockSpec((1, tn), lambda i, j: (0, j))],
                out_specs=[blk, blk], out_shape=[sh, sh], scratch_shapes=[], args=(u2, w_ff1t, b_ff1), rider=rider)
    return tuple(res) if rider is None else (tuple(res[0]), res[1])


def _ff_hidden_bwd(dff, w_ff2, hpre, name):
    s = dff.shape[0]
    tm, tn = min(512, s), 1024

    def body(a_ref, b_ref, h_ref, o_ref, sum_ref):
        dh = lax.dot_general(a_ref[...], b_ref[...], _DIMS["nt"], preferred_element_type=F32)
        dpre = dh * (2.0 * jnp.maximum(h_ref[...].astype(F32), 0.0))
        o_ref[...] = dpre.astype(BF16)
        _acc_rows(sum_ref, dpre, pl.program_id(1) == 0)

    return pl.pallas_call(
        body, grid=(D_FF // tn, s // tm),
        in_specs=[pl.BlockSpec((tm, D_MODEL), lambda j, i: (i, 0)), pl.BlockSpec((tn, D_MODEL), lambda j, i: (j, 0)),
                  pl.BlockSpec((tm, tn), lambda j, i: (i, j))],
        out_specs=[pl.BlockSpec((tm, tn), lambda j, i: (i, j)), pl.BlockSpec((1, tn), lambda j, i: (0, j))],
        out_shape=[jax.ShapeDtypeStruct((s, D_FF), BF16), jax.ShapeDtypeStruct((1, D_FF), F32)],
        name=name, compiler_params=_cparams(),
    )(dff, w_ff2, hpre)


def _silu(t):
    return t * _sigmoid(t)


def _mod_fwd(c_all, w_ada_sh, b_ada_sh, name):
    cols = w_ada_sh.shape[2]

    def body(c_ref, w_ref, b_ref, o_ref):
        ca = _silu(c_ref[...]).astype(BF16)
        o_ref[0] = jnp.dot(ca, w_ref[0].astype(BF16), preferred_element_type=F32) + b_ref[0]

    return pl.pallas_call(
        body, grid=(DEPTH,),
        in_specs=[_full((N_DEV, D_MODEL)), pl.BlockSpec((1, D_MODEL, cols), lambda l: (l, 0, 0)),
                  pl.BlockSpec((1, 1, cols), lambda l: (l, 0, 0))],
        out_specs=pl.BlockSpec((1, N_DEV, cols), lambda l: (l, 0, 0)),
        out_shape=jax.ShapeDtypeStruct((DEPTH, N_DEV, cols), F32), name=name, compiler_params=_cparams(),
    )(c_all, w_ada_sh, b_ada_sh)


def _mod_bwd(c_all, dmod_sh, name):
    cols = dmod_sh.shape[2]

    def body(c_ref, d_ref, o_ref):
        ca = _silu(c_ref[...])
        o_ref[0] = lax.dot_general(ca, d_ref[0], _DIMS["tn"], precision=lax.Precision.HIGHEST,
                                   preferred_element_type=F32)

    return pl.pallas_call(
        body, grid=(DEPTH,),
        in_specs=[_full((N_DEV, D_MODEL)), pl.BlockSpec((1, N_DEV, cols), lambda l: (l, 0, 0))],
        out_specs=pl.BlockSpec((1, D_MODEL, cols), lambda l: (l, 0, 0)),
        out_shape=jax.ShapeDtypeStruct((DEPTH, D_MODEL, cols), F32), name=name, compiler_params=_cparams(),
    )(c_all, dmod_sh)


def _flat_tiles(rows, cols, itemsize_total):
    budget = 12 * 1024 * 1024
    tr = rows
    while tr % 32 == 0 and tr * cols * itemsize_total > budget:
        tr //= 2
    return tr


def _sum_cores(dw, recv, place, name):
    _, m, n = dw.shape
    tr = _flat_tiles(m, n, 6)

    def body(place_ref, a_ref, b_ref, o_ref):
        o_ref[...] = (a_ref[...].astype(F32) + b_ref[...].astype(F32)).astype(BF16)

    grid_spec = pltpu.PrefetchScalarGridSpec(
        num_scalar_prefetch=1, grid=(m // tr,),
        in_specs=[pl.BlockSpec((None, tr, n), lambda i, pr: (pr[0], i, 0)), pl.BlockSpec((tr, n), lambda i, pr: (i, 0))],
        out_specs=pl.BlockSpec((tr, n), lambda i, pr: (i, 0)))
    return pl.pallas_call(body, grid_spec=grid_spec, out_shape=jax.ShapeDtypeStruct((m, n), BF16), name=name,
                          compiler_params=_cparams())(place, dw, recv)


def _sum_chips(h, r, place, name):
    _, rs, n = h.shape
    tr = _flat_tiles(rs, n, 12)

    def body(place_ref, h_ref, r_ref, o_ref):
        o_ref[...] = ((h_ref[...].astype(F32) + r_ref[0].astype(F32)) + r_ref[1].astype(F32)) + r_ref[2].astype(F32)

    grid_spec = pltpu.PrefetchScalarGridSpec(
        num_scalar_prefetch=1, grid=(rs // tr,),
        in_specs=[pl.BlockSpec((None, tr, n), lambda i, pr: (pr[1], i, 0)), pl.BlockSpec((3, tr, n), lambda i, pr: (0, i, 0))],
        out_specs=pl.BlockSpec((tr, n), lambda i, pr: (i, 0)))
    return pl.pallas_call(body, grid_spec=grid_spec, out_shape=jax.ShapeDtypeStruct((rs, n), F32), name=name,
                          compiler_params=_cparams())(place, h, r)


def _adam_math(w, g, m, v):
    m2 = ADAM_B1 * m + (1.0 - ADAM_B1) * g
    v2 = ADAM_B2 * v + (1.0 - ADAM_B2) * (g * g)
    m_hat = m2 / (1.0 - ADAM_B1 ** ADAM_STEP)
    v_hat = v2 / (1.0 - ADAM_B2 ** ADAM_STEP)
    delta = -ADAM_LR * (m_hat / (jnp.sqrt(v_hat) + ADAM_EPS) + ADAM_WD * w)
    return delta, m2, v2


def _adamw(w, m, v, grads, name):
    r, c = w.shape
    tr = _flat_tiles(r, c, 4 * (7 + len(grads)))

    def body(*refs):
        w_ref, m_ref, v_ref = refs[:3]
        g_refs = refs[3:3 + len(grads)]
        g_ref, d_ref, m2_ref, v2_ref = refs[3 + len(grads):]
        g = g_refs[0][...]
        for gr in g_refs[1:]:
            g = g + gr[...]
        delta, m2, v2 = _adam_math(w_ref[...], g, m_ref[...], v_ref[...])
        g_ref[...] = g
        d_ref[...] = delta
        m2_ref[...] = m2
        v2_ref[...] = v2

    blk = pl.BlockSpec((tr, c), lambda i: (i, 0))
    sh = jax.ShapeDtypeStruct((r, c), F32)
    return pl.pallas_call(body, grid=(r // tr,), in_specs=[blk] * (3 + len(grads)), out_specs=[blk] * 4,
                          out_shape=[sh] * 4, name=name, compiler_params=_cparams())(w, m, v, *grads)


def _adamw_halves(w, m, v, own, other, place, split, name):
    nl, r, c = w.shape
    hr, hc = own[0].shape
    tr = _flat_tiles(hr, hc, 4 * (7 + 2 * nl))
    nt = hr // tr
    if split == "rows":
        w_spec = pl.BlockSpec((None, tr, c), lambda l, h, t, pr: (l, h * nt + t, 0))
    else:
        w_spec = pl.BlockSpec((None, tr, hc), lambda l, h, t, pr: (l, t, h))

    def g_spec(layer, mine):
        return pl.BlockSpec((tr, hc), lambda l, h, t, pr: (jnp.where((l == layer) & ((h == pr[0]) == mine), t, nt - 1), 0))

    def body(place_ref, w_ref, m_ref, v_ref, *refs):
        own_refs, other_refs = refs[:nl], refs[nl:2 * nl]
        g_ref, d_ref, m2_ref, v2_ref = refs[2 * nl:]
        layer = pl.program_id(0)
        mine = pl.program_id(1) == place_ref[0]
        g = None
        for li in range(nl):
            cand = jnp.where(mine, own_refs[li][...], other_refs[li][...])
            g = cand if g is None else jnp.where(layer == li, cand, g)
        delta, m2, v2 = _adam_math(w_ref[...], g, m_ref[...], v_ref[...])
        g_ref[...] = g
        d_ref[...] = delta
        m2_ref[...] = m2
        v2_ref[...] = v2

    sh = jax.ShapeDtypeStruct((nl, r, c), F32)
    g_specs = [g_spec(li, True) for li in range(nl)] + [g_spec(li, False) for li in range(nl)]
    return _call(body, name=name, grid=(nl, 2, nt), in_specs=[w_spec] * 3 + g_specs, out_specs=[w_spec] * 4,
                 out_shape=[sh] * 4, scratch_shapes=[], args=(w, m, v, *own, *other), prefetch=(place,))


def _adamw_small(w, m, v, g_all, name):
    r, c = w.shape

    def body(w_ref, m_ref, v_ref, g_ref, go_ref, d_ref, m2_ref, v2_ref):
        g = g_ref[0]
        for b in range(1, N_DEV):
            g = g + g_ref[b]
        delta, m2, v2 = _adam_math(w_ref[...], g, m_ref[...], v_ref[...])
        go_ref[...] = g
        d_ref[...] = delta
        m2_ref[...] = m2
        v2_ref[...] = v2

    sh = jax.ShapeDtypeStruct((r, c), F32)
    return pl.pallas_call(body, out_shape=[sh] * 4, name=name, compiler_params=_cparams())(w, m, v, g_all)


def _me():
    return lax.axis_index("x"), lax.axis_index("y"), lax.axis_index("c")


def _flip(v, bit):
    return 1 - v if bit else v


def _allgather_small(blk, name):
    r, c = blk.shape

    def body(x_ref, o_ref, send_sems, recv_sems):
        x, y, cc = _me()
        me = 4 * x + 2 * y + cc
        copies = []
        for k in range(1, N_DEV):
            peer = (_flip(x, k & 4), _flip(y, k & 2), _flip(cc, k & 1))
            cp = pltpu.make_async_remote_copy(src_ref=x_ref, dst_ref=o_ref.at[me], send_sem=send_sems.at[k - 1],
                                              recv_sem=recv_sems.at[k - 1], device_id=peer, device_id_type=MESH)
            cp.start()
            copies.append(cp)
        o_ref[me] = x_ref[...]
        for cp in copies:
            cp.wait()

    return pl.pallas_call(
        body, out_shape=jax.ShapeDtypeStruct((N_DEV, r, c), F32),
        in_specs=[pl.BlockSpec(memory_space=pltpu.VMEM)], out_specs=pl.BlockSpec(memory_space=pltpu.VMEM),
        scratch_shapes=[pltpu.SemaphoreType.DMA((N_DEV - 1,)), pltpu.SemaphoreType.DMA((N_DEV - 1,))],
        name=name, compiler_params=_cparams(),
    )(blk)


class _Rider:
    def __init__(self, arrays, out_shapes, scratch_shapes, start, finish):
        self.arrays, self.out_shapes, self.scratch_shapes = list(arrays), list(out_shapes), list(scratch_shapes)
        self.start, self.finish = start, finish


def _call(body, *, name, grid, in_specs, out_specs, out_shape, scratch_shapes, args, rider=None, prefetch=()):
    npf = len(prefetch)

    def launch(fn, in_specs, out_specs, out_shape, scratch_shapes, args):
        grid_spec = pltpu.PrefetchScalarGridSpec(num_scalar_prefetch=npf, grid=grid, in_specs=in_specs,
                                                 out_specs=out_specs, scratch_shapes=scratch_shapes)
        return pl.pallas_call(fn, grid_spec=grid_spec, out_shape=out_shape, name=name,
                              compiler_params=_cparams())(*prefetch, *args)

    if rider is None:
        return launch(body, list(in_specs), list(out_specs), list(out_shape), list(scratch_shapes), args)
    ni, no, ns = len(in_specs), len(out_specs), len(scratch_shapes)
    ri, ro = len(rider.arrays), len(rider.out_shapes)
    steps = int(np.prod(grid))

    def wrapped(*refs):
        pf, refs = refs[:npf], refs[npf:]
        h_in, r_in = refs[:ni], refs[ni:ni + ri]
        h_out, r_out = refs[ni + ri:ni + ri + no], refs[ni + ri + no:ni + ri + no + ro]
        h_scr, r_scr = refs[ni + ri + no + ro:ni + ri + no + ro + ns], refs[ni + ri + no + ro + ns:]
        step = pl.program_id(0)
        for d in range(1, len(grid)):
            step = step * grid[d] + pl.program_id(d)

        @pl.when(step == 0)
        def _():
            rider.start(r_in, r_out, r_scr)

        body(*pf, *h_in, *h_out, *h_scr)

        @pl.when(step == steps - 1)
        def _():
            rider.finish(r_in, r_out, r_scr)

    anyspec = pl.BlockSpec(memory_space=pl.ANY)
    res = launch(wrapped, list(in_specs) + [anyspec] * ri, list(out_specs) + [anyspec] * ro,
                 list(out_shape) + rider.out_shapes, list(scratch_shapes) + rider.scratch_shapes,
                 list(args) + rider.arrays)
    return res[:no], res[no:]


def _run_rider(rider, name):
    ri = len(rider.arrays)

    def body(*refs):
        r_in, r_out, r_scr = refs[:ri], refs[ri:ri + len(rider.out_shapes)], refs[ri + len(rider.out_shapes):]
        rider.start(r_in, r_out, r_scr)
        rider.finish(r_in, r_out, r_scr)

    anyspec = pl.BlockSpec(memory_space=pl.ANY)
    return pl.pallas_call(body, in_specs=[anyspec] * ri, out_specs=[anyspec] * len(rider.out_shapes),
                          out_shape=rider.out_shapes, scratch_shapes=rider.scratch_shapes, name=name,
                          compiler_params=_cparams())(*rider.arrays)


def _allgather_rider(blk):
    def copies(ins, outs, scr):
        send_sems, recv_sems, loc_sems, stage = scr
        x, y, cc = _me()
        me = 4 * x + 2 * y + cc
        remote = [pltpu.make_async_remote_copy(
            src_ref=ins[0], dst_ref=outs[0].at[me], send_sem=send_sems.at[k - 1], recv_sem=recv_sems.at[k - 1],
            device_id=(_flip(x, k & 4), _flip(y, k & 2), _flip(cc, k & 1)), device_id_type=MESH) for k in range(1, N_DEV)]
        return remote, pltpu.make_async_copy(ins[0], stage, loc_sems.at[0]), (outs[0].at[me], stage, loc_sems.at[1])

    def start(ins, outs, scr):
        remote, lin, _ = copies(ins, outs, scr)
        lin.start()
        for cp in remote:
            cp.start()

    def finish(ins, outs, scr):
        remote, lin, (dst, stage, sem) = copies(ins, outs, scr)
        lin.wait()
        lout = pltpu.make_async_copy(stage, dst, sem)
        lout.start()
        for cp in remote:
            cp.wait()
        lout.wait()

    return _Rider([blk], [jax.ShapeDtypeStruct((N_DEV,) + blk.shape, blk.dtype)],
                  [pltpu.SemaphoreType.DMA((N_DEV - 1,)), pltpu.SemaphoreType.DMA((N_DEV - 1,)),
                   pltpu.SemaphoreType.DMA((2,)), pltpu.VMEM(blk.shape, blk.dtype)], start, finish)


def _gather_rider(shards):
    n = len(shards)

    def copies(ins, outs, scr, relay=True):
        ici_send, ici_recv, d2d_send, d2d_recv, loc_sems = scr[:5]
        stage = scr[5:]
        x, y, cc = _me()
        chip = 2 * x + y
        sibling = (x, y, 1 - cc)
        local, sends, relays = [], [], []
        for j in range(n):
            def rows(ch, h, j=j):
                return outs[j].at[ch, h]

            lc = pltpu.make_async_copy(ins[j], stage[j], loc_sems.at[j])
            local.append((lc, pltpu.make_async_copy(stage[j], outs[j].at[chip], loc_sems.at[n + j]) if relay else None))
            for k in range(1, N_CHIP):
                px, py = _flip(x, k & 2), _flip(y, k & 1)
                pchip = 2 * px + py
                q = 3 * j + k - 1
                out_cp = pltpu.make_async_remote_copy(src_ref=ins[j].at[cc], dst_ref=rows(chip, cc),
                                                      send_sem=ici_send.at[q], recv_sem=ici_recv.at[q],
                                                      device_id=(px, py, cc), device_id_type=MESH)
                sends.append(out_cp)
                if not relay:
                    continue
                arrival = pltpu.make_async_remote_copy(src_ref=rows(pchip, cc), dst_ref=rows(pchip, cc),
                                                       send_sem=ici_send.at[q], recv_sem=ici_recv.at[q],
                                                       device_id=(px, py, cc), device_id_type=MESH)
                forward = pltpu.make_async_remote_copy(src_ref=rows(pchip, cc), dst_ref=rows(pchip, cc),
                                                       send_sem=d2d_send.at[q], recv_sem=d2d_recv.at[q],
                                                       device_id=sibling, device_id_type=MESH)
                from_sibling = pltpu.make_async_remote_copy(src_ref=rows(pchip, 1 - cc), dst_ref=rows(pchip, 1 - cc),
                                                            send_sem=d2d_send.at[q], recv_sem=d2d_recv.at[q],
                                                            device_id=sibling, device_id_type=MESH)
                relays.append((arrival, forward, from_sibling))
        return local, sends, relays

    def start(ins, outs, scr):
        local, sends, _ = copies(ins, outs, scr, relay=False)
        for lin, _ in local:
            lin.start()
        for cp in sends:
            cp.start()

    def finish(ins, outs, scr):
        local, sends, relays = copies(ins, outs, scr)
        for lin, lout in local:
            lin.wait()
            lout.start()
        for arrival, forward, _ in relays:
            arrival.wait_recv()
            forward.start()
        for cp in sends:
            cp.wait_send()
        for _, forward, from_sibling in relays:
            forward.wait_send()
            from_sibling.wait_recv()
        for _, lout in local:
            lout.wait()

    scratch = [pltpu.SemaphoreType.DMA((3 * n,)), pltpu.SemaphoreType.DMA((3 * n,)), pltpu.SemaphoreType.DMA((3 * n,)),
               pltpu.SemaphoreType.DMA((3 * n,)), pltpu.SemaphoreType.DMA((2 * n,))]
    scratch += [pltpu.VMEM(a.shape, a.dtype) for a in shards]
    return _Rider(shards, [jax.ShapeDtypeStruct((N_CHIP,) + a.shape, a.dtype) for a in shards], scratch, start, finish)


def _sibling_rider(arrs, other_half=False):
    n = len(arrs)

    def copies(ins, outs, scr):
        send_sems, recv_sems = scr
        x, y, cc = _me()
        return [pltpu.make_async_remote_copy(
            src_ref=ins[j].at[1 - cc] if other_half else ins[j], dst_ref=outs[j], send_sem=send_sems.at[j],
            recv_sem=recv_sems.at[j], device_id=(x, y, 1 - cc), device_id_type=MESH) for j in range(n)]

    def start(ins, outs, scr):
        for cp in copies(ins, outs, scr):
            cp.start()

    def finish(ins, outs, scr):
        for cp in copies(ins, outs, scr):
            cp.wait()

    return _Rider(arrs, [jax.ShapeDtypeStruct(a.shape[1:] if other_half else a.shape, a.dtype) for a in arrs],
                  [pltpu.SemaphoreType.DMA((n,)), pltpu.SemaphoreType.DMA((n,))], start, finish)


def _sibling_send(arrs, name, other_half=False):
    return _run_rider(_sibling_rider(arrs, other_half), name)


def _join_riders(first, second):
    ni, no, ns = len(first.arrays), len(first.out_shapes), len(first.scratch_shapes)

    def split(ins, outs, scr):
        return (ins[:ni], outs[:no], scr[:ns]), (ins[ni:], outs[no:], scr[ns:])

    def start(ins, outs, scr):
        a, b = split(ins, outs, scr)
        first.start(*a)
        second.start(*b)

    def finish(ins, outs, scr):
        a, b = split(ins, outs, scr)
        first.finish(*a)
        second.finish(*b)

    return _Rider(first.arrays + second.arrays, first.out_shapes + second.out_shapes,
                  first.scratch_shapes + second.scratch_shapes, start, finish)


def _scatter_rider(arrs):
    n = len(arrs)

    def copies(ins, outs, scr):
        send_sems, recv_sems = scr
        x, y, cc = _me()
        cps = []
        for j in range(n):
            for k in range(1, N_CHIP):
                px, py = _flip(x, k & 2), _flip(y, k & 1)
                cps.append(pltpu.make_async_remote_copy(
                    src_ref=ins[j].at[2 * px + py], dst_ref=outs[j].at[k - 1], send_sem=send_sems.at[3 * j + k - 1],
                    recv_sem=recv_sems.at[3 * j + k - 1], device_id=(px, py, cc), device_id_type=MESH))
        return cps

    def start(ins, outs, scr):
        for cp in copies(ins, outs, scr):
            cp.start()

    def finish(ins, outs, scr):
        for cp in copies(ins, outs, scr):
            cp.wait()

    return _Rider(arrs, [jax.ShapeDtypeStruct((N_CHIP - 1,) + a.shape[1:], a.dtype) for a in arrs],
                  [pltpu.SemaphoreType.DMA((3 * n,)), pltpu.SemaphoreType.DMA((3 * n,))], start, finish)


COL_SHARDED = ("w_in", "w_br_pool", "w_br_attn", "w_br_conv", "w_ff1")
ROW_SHARDED = ("w_o", "w_ff2")
BIG = COL_SHARDED + ROW_SHARDED
SMALL = ("b_ada", "b_gate", "w_pool", "pool_scale", "rel_bias", "conv_w", "conv_b", "conv_ln_g", "conv_ln_b",
         "ln_mix_g", "ln_mix_b", "b_ff1", "b_ff2", "ln_ff_g", "ln_ff_b")
PACK_W = 1024


def _pack(parts):
    rows = []
    for a in parts:
        flat = a.reshape(-1)
        n = -(-flat.shape[0] // PACK_W) * PACK_W
        rows.append(jnp.pad(flat, (0, n - flat.shape[0])).reshape(-1, PACK_W))
    out = jnp.concatenate(rows, axis=0)
    r = -(-out.shape[0] // 8) * 8
    return jnp.pad(out, ((0, r - out.shape[0]), (0, 0)))


def _unpack(packed, shapes):
    out, r0 = [], 0
    for shp in shapes:
        size = int(np.prod(shp))
        nr = -(-size // PACK_W)
        out.append(packed[r0:r0 + nr].reshape(-1)[:size].reshape(shp))
        r0 += nr
    return out


def _hosted(fn, hook, *args, **kw):
    if hook is None:
        return fn(*args, **kw)
    res, rider_out = fn(*args, rider=hook[0], **kw)
    hook[1](rider_out)
    return res


def _layer_fwd(l, x, mod, W, P, hooks=None):
    hooks = hooks or {}
    s = x.shape[0]
    sh_m, sc_m, g_m, sh_f, sc_f, g_f = [mod[l:l + 1, D_MODEL * j:D_MODEL * (j + 1)] for j in range(6)]
    n = lambda t: f"{t}{l}"
    w_in = W["w_in"][l]
    u = _ln_mod(x, sc_m, sh_m, n("ln_mod_mix"))
    zp = _mm(u, w_in, "nt", tm=s, tn=256, out_dtype=F32, name=n("z_pool"), b_col0=0, n_out=D_POOL)
    qkv = _mm(u, w_in, "nt", tm=s, tn=256, out_dtype=BF16, name=n("z_qkv"), b_col0=OFF_QKV // 256, n_out=3 * D_ATTN)
    zc = _mm(u, w_in, "nt", tm=s, tn=256, out_dtype=F32, name=n("z_conv"), b_col0=OFF_CONV // 256, n_out=2 * D_CONV)
    zg = _hosted(_mm, hooks.get("z_gate"), u, w_in, "nt", tm=min(2048, s), tn=768, out_dtype=BF16, name=n("z_gate"),
                 b_col0=OFF_GATE // 768, n_out=3 * D_MODEL)

    p, feat_pool = _pool_fwd(zp, P["wp_bd"][l], P["pool_scale"][l], n("pool_fwd"))
    bias = _bias_block(P["rel_bias"][l], n("bias_block"))
    o = _hosted(_attn_fwd, hooks.get("attn"), qkv, bias, n("attn_fwd"))
    cv, feat_conv = _conv_fwd(zc, P["conv_w"][l], P["conv_b"][l], P["conv_ln_g"][l], P["conv_ln_b"][l], n("conv_fwd"))

    tmb = min(1024, s)
    y_pool = _mm(feat_pool, W["w_br_pool"][l], "nt", tm=tmb, tn=1024, out_dtype=BF16, name=n("y_pool"))
    y_attn = _mm(o, W["w_br_attn"][l], "nt", tm=tmb, tn=1024, out_dtype=BF16, name=n("y_attn"))
    y_conv = _mm(feat_conv, W["w_br_conv"][l], "nt", tm=tmb, tn=1024, out_dtype=BF16, name=n("y_conv"))
    ys = (y_pool, y_attn, y_conv)
    merged = _merge(zg, P["b_gate"][l], ys, n("merge"))
    mix, x1 = _mm_resid_ln(merged, W["w_o"][l], None, x, g_m, P["ln_mix_g"][l], P["ln_mix_b"][l], n("mix_out"))

    u2 = _ln_mod(x1, sc_f, sh_f, n("ln_mod_ff"))
    hpre, hid = _hosted(_ff_hidden, hooks.get("ff1"), u2, W["w_ff1"][l], P["b_ff1"][l], n("ff1"))
    ff, x2 = _hosted(_mm_resid_ln, hooks.get("ff2"), hid, W["w_ff2"][l], P["b_ff2"][l], x1, g_f, P["ln_ff_g"][l],
                     P["ln_ff_b"][l], n("ff2"))
    saved = dict(x=x, u=u, zp=zp, qkv=qkv, zc=zc, zg=zg, p=p, feat_pool=feat_pool, bias=bias, o=o, cv=cv,
                 feat_conv=feat_conv, ys=ys, merged=merged, mix=mix, x1=x1, u2=u2, hpre=hpre, hid=hid, ff=ff)
    return x2, saved


def _layer_bwd(l, dx2, mod, W, P, A, hooks=None, tgt=None, nxt=None):
    hooks = hooks or {}
    sh_m, sc_m, g_m, sh_f, sc_f, g_f = [mod[l:l + 1, D_MODEL * j:D_MODEL * (j + 1)] for j in range(6)]
    n = lambda t: f"{t}{l}"
    gw, gs = {}, {}

    if isinstance(dx2, tuple):
        dres, dff, gs["ln_ff_g"], gs["ln_ff_b"], dg_f, gs["b_ff2"] = dx2
    else:
        dres, dff, gs["ln_ff_g"], gs["ln_ff_b"], dg_f, gs["b_ff2"], *loss_part = _resid_ln_bwd(
            dx2, A["x1"], A["ff"], g_f, P["ln_ff_g"][l], n("resid_ln_ff_bwd"), tgt=tgt)
    s = dres.shape[0]
    tmb = min(1024, s)
    gw["w_ff2"] = _mm(A["hid"], dff, "tn", tm=512, tn=1024, out_dtype=BF16, name=n("dw_ff2"), split_n=512)
    dhpre, gs["b_ff1"] = _ff_hidden_bwd(dff, W["w_ff2"][l], A["hpre"], n("ff_hidden_bwd"))
    gw["w_ff1"] = _mm(dhpre, A["u2"], "tn", tm=512, tn=1024, out_dtype=BF16, name=n("dw_ff1"), split_n=512)

    dres, dmix, dsc_f, dsh_f, gs["ln_mix_g"], gs["ln_mix_b"], dg_m, _ = _mm_ln_mod_bwd(
        dhpre, W["w_ff1"][l], A["x1"], sc_f, dres, n("du_ff"), nxt=(A["x"], A["mix"], g_m, P["ln_mix_g"][l]))
    gw["w_o"] = _mm(A["merged"], dmix, "tn", tm=512, tn=1024, out_dtype=BF16, name=n("dw_o"), split_n=512)
    dmerged = _mm(dmix, W["w_o"][l], "nt", tm=tmb, tn=1024, out_dtype=F32, name=n("d_merged"))
    dy_pool, dy_attn, dy_conv, dzg, gs["b_gate"] = _merge_bwd(dmerged, A["zg"], P["b_gate"][l], A["ys"], n("merge_bwd"))

    gw["w_br_pool"] = _mm(dy_pool, A["feat_pool"], "tn", tm=512, tn=256, out_dtype=BF16, name=n("dw_br_pool"),
                          split_n=128)
    gw["w_br_attn"] = _mm(dy_attn, A["o"], "tn", tm=512, tn=512, out_dtype=BF16, name=n("dw_br_attn"), split_n=256)
    gw["w_br_conv"] = _mm(dy_conv, A["feat_conv"], "tn", tm=512, tn=256, out_dtype=BF16, name=n("dw_br_conv"),
                          split_n=128)
    dfeat_pool = _mm(dy_pool, W["w_br_pool"][l], "nn", tm=tmb, tn=256, out_dtype=F32, name=n("d_feat_pool"))
    do = _mm(dy_attn, W["w_br_attn"][l], "nn", tm=tmb, tn=512, out_dtype=BF16, name=n("d_attn_out"))
    dfeat_conv = _mm(dy_conv, W["w_br_conv"][l], "nn", tm=tmb, tn=256, out_dtype=F32, name=n("d_feat_conv"))

    dzp, dwp_bd, gs["pool_scale"] = _pool_bwd(dfeat_pool, A["p"], P["wp_bd"][l], P["pool_scale"][l], n("pool_bwd"))
    gs["w_pool"] = jnp.stack([dwp_bd[POOL_GROUP * g:POOL_GROUP * (g + 1), POOL_GROUP * g:POOL_GROUP * (g + 1)]
                              for g in range(len(POOL_WINDOWS))])
    hook = hooks["attn"](gw) if "attn" in hooks else None
    dq, dk, dv, ds_acc = _hosted(_attn_bwd, hook, A["qkv"], do, A["bias"], n("attn_bwd"))
    gs["rel_bias"] = _bias_block_bwd(ds_acc, n("bias_block_bwd"))
    dzc, dcw, gs["conv_b"], gs["conv_ln_g"], gs["conv_ln_b"] = _conv_bwd(
        dfeat_conv, A["cv"], A["zc"], P["conv_w"][l], P["conv_ln_g"][l], P["conv_ln_b"][l], n("conv_bwd"))
    gs["conv_w"] = dcw[:CONV_WIDTH]

    dz = [dzp, dq, dk, dv, dzc, dzg]
    gw["w_in"] = _dw_segments(dz, A["u"], n("dw_in"))
    hook = hooks["du_mix"](gw) if "du_mix" in hooks else None
    res = _hosted(_mm_ln_mod_bwd, hook, dz, W["w_in"][l], A["x"], sc_m, dres, n("du_mix"), nxt=nxt)
    if nxt is None:
        dx, dsc_m, dsh_m = res
    else:
        dx, dsc_m, dsh_m = (res[0], res[1], *res[4:]), res[2], res[3]
    dmod = jnp.concatenate([dsh_m, dsc_m, dg_m, dsh_f, dsc_f, dg_f], axis=1)
    return (dx, gw, gs, dmod) if tgt is None else (dx, gw, gs, dmod, loss_part[0])


def _small_shapes():
    return {"b_ada": (6 * D_MODEL,), "b_gate": (3 * D_MODEL,), "w_pool": (4, POOL_GROUP, POOL_GROUP),
            "pool_scale": (D_POOL,), "rel_bias": (N_HEADS, N_REL), "conv_w": (CONV_WIDTH, D_CONV),
            "conv_b": (D_CONV,), "conv_ln_g": (D_CONV,), "conv_ln_b": (D_CONV,), "ln_mix_g": (D_MODEL,),
            "ln_mix_b": (D_MODEL,), "b_ff1": (D_FF,), "b_ff2": (D_MODEL,), "ln_ff_g": (D_MODEL,), "ln_ff_b": (D_MODEL,)}


def kernel(x, c, w_ada, b_ada, w_in, b_gate, w_pool, pool_scale, rel_bias, conv_w, conv_b, conv_ln_g, conv_ln_b, w_br_pool, w_br_attn, w_br_conv, w_o, ln_mix_g, ln_mix_b, w_ff1, b_ff1, w_ff2, b_ff2, ln_ff_g, ln_ff_b, loss_target, m_w_ada, m_b_ada, m_w_in, m_b_gate, m_w_pool, m_pool_scale, m_rel_bias, m_conv_w, m_conv_b, m_conv_ln_g, m_conv_ln_b, m_w_br_pool, m_w_br_attn, m_w_br_conv, m_w_o, m_ln_mix_g, m_ln_mix_b, m_w_ff1, m_b_ff1, m_w_ff2, m_b_ff2, m_ln_ff_g, m_ln_ff_b, v_w_ada, v_b_ada, v_w_in, v_b_gate, v_w_pool, v_pool_scale, v_rel_bias, v_conv_w, v_conv_b, v_conv_ln_g, v_conv_ln_b, v_w_br_pool, v_w_br_attn, v_w_br_conv, v_w_o, v_ln_mix_g, v_ln_mix_b, v_w_ff1, v_b_ff1, v_w_ff2, v_b_ff2, v_ln_ff_g, v_ln_ff_b):
    env = dict(locals())
    xi, yi, ci = _me()
    chip = 2 * xi + yi
    me = 4 * xi + 2 * yi + ci
    xs = x[0]
    tgt = loss_target[0]
    L = DEPTH

    first = _allgather_small(jnp.concatenate([c.reshape(8, 128), _pack([conv_w]).reshape(-1, 128)]), "gather_c_conv_w")
    c_all = first[:, :8].reshape(N_DEV, D_MODEL)
    ada_cols = w_ada.shape[2]
    b_ada_sh = lax.dynamic_slice_in_dim(b_ada, chip * ada_cols, ada_cols, axis=1).reshape(L, 1, ada_cols)
    mod_part = _mod_fwd(c_all, w_ada, b_ada_sh, "mod_fwd")
    mod_g = _allgather_small(mod_part.reshape(-1, 128), "gather_mod").reshape(N_CHIP, 2, L, N_DEV, ada_cols)[:, 0]
    mod_all = jnp.transpose(mod_g, (1, 2, 0, 3)).reshape(L, N_DEV, 6 * D_MODEL)
    mod = lax.dynamic_index_in_dim(mod_all, me, axis=1, keepdims=False)

    W = {k: [None] * L for k in BIG}

    def weight_gather(names, l):
        shards = [(jnp.swapaxes(env[k][l], 0, 1) if k in COL_SHARDED else env[k][l]).astype(BF16) for k in names]
        shards = [a.reshape(2, a.shape[0] // 2, a.shape[1]) for a in shards]

        def done(outs):
            for k, g in zip(names, outs):
                W[k][l] = g.reshape(-1, g.shape[-1])

        return _gather_rider(shards), done

    branch_names = ("w_br_pool", "w_br_attn", "w_br_conv", "w_o")
    late_names = ("w_ff1", "w_ff2")
    rider, done = weight_gather(("w_in",), 0)
    done(_run_rider(rider, "gather_w_in0"))
    fwd_hooks = [{"z_gate": weight_gather(branch_names, 0), "attn": weight_gather(late_names, 0),
                  "ff1": weight_gather(("w_in",), 1), "ff2": weight_gather(branch_names, 1)},
                 {"attn": weight_gather(late_names, 1)}]

    P = {k: env[k] for k in ("rel_bias", "conv_w")}
    for k in ("b_gate", "pool_scale", "conv_b", "conv_ln_g", "conv_ln_b", "ln_mix_g", "ln_mix_b", "b_ff1", "b_ff2",
              "ln_ff_g", "ln_ff_b"):
        P[k] = env[k].reshape(L, 1, -1)
    n_cw = conv_w.size
    cw = first[:, 8:].reshape(N_CHIP, 2, -1)[:, 0, :n_cw].reshape(N_CHIP, L, CONV_WIDTH, D_CONV // N_CHIP)
    P["conv_w"] = jnp.transpose(cw, (1, 2, 0, 3)).reshape(L, CONV_WIDTH, D_CONV)
    wp_bd = jnp.zeros((L, D_POOL, D_POOL), F32)
    for g in range(len(POOL_WINDOWS)):
        sl = slice(POOL_GROUP * g, POOL_GROUP * (g + 1))
        wp_bd = wp_bd.at[:, sl, sl].set(w_pool[:, g])
    P["wp_bd"] = wp_bd.astype(BF16)

    acts = []
    h = xs
    for l in range(L):
        h, saved = _layer_fwd(l, h, mod, W, P, fwd_hooks[l])
        acts.append(saved)

    place = jnp.stack([ci, chip, chip ^ 1, chip ^ 2, chip ^ 3]).astype(jnp.int32)
    scattered = {}

    def grad_scatter(items, tag):
        dws = [dw for _, _, dw in items]
        got = _sibling_send(dws, f"swap_blocks_{tag}", other_half=True)
        both = [_sum_cores(a, b, place, f"sum_cores_{k}{l}") for (k, l, _), a, b in zip(items, dws, got)]
        both = [hh.reshape(N_CHIP, -1, hh.shape[-1]) for hh in both]

        def done(outs):
            for (k, l, _), hh, r in zip(items, both, outs):
                scattered[(k, l)] = (hh, r)

        return _scatter_rider(both), done

    early = ("w_ff2", "w_ff1", "w_o", "w_br_pool", "w_br_attn", "w_br_conv")
    left_over = []

    def attn_hook(l):
        def hook(gw):
            items = left_over + [(k, l, gw[k]) for k in early]
            left_over.clear()
            return grad_scatter(items, f"attn{l}")
        return hook

    def last_hook(gw):
        return grad_scatter([("w_in", 0, gw["w_in"])], "last")

    gws, gss, dmods = [None] * L, [None] * L, [None] * L
    dh = h
    for l in reversed(range(L)):
        hooks = {"attn": attn_hook(l)}
        if l == 0:
            hooks["du_mix"] = last_hook
        below = None
        if l > 0:
            below = (acts[l - 1]["x1"], acts[l - 1]["ff"], mod[l - 1:l, 5 * D_MODEL:], P["ln_ff_g"][l - 1])
        if l == L - 1:
            dh, gws[l], gss[l], dmods[l], loss_part = _layer_bwd(l, dh, mod, W, P, acts[l], hooks, tgt=tgt, nxt=below)
        else:
            dh, gws[l], gss[l], dmods[l] = _layer_bwd(l, dh, mod, W, P, acts[l], hooks, nxt=below)
        if l > 0:
            left_over.append(("w_in", l, gws[l]["w_in"]))
    grad_x = dh[None]
    loss = lax.psum(loss_part[0, 0], ("x", "y", "c"))

    reduced = [[_sum_chips(*scattered[(k, l)], place, f"sum_chips_{k}{l}") for l in range(L)] for k in BIG]
    flat_reduced = [t for per_weight in reduced for t in per_weight]

    shapes = _small_shapes()
    small_names = [k for k in SMALL if k != "b_ada"]
    dmod_own = jnp.concatenate(dmods, axis=0)
    pack = _pack([dmod_own] + [jnp.stack([gss[l][k].reshape(shapes[k]) for l in range(L)]) for k in small_names])
    last = _run_rider(_join_riders(_sibling_rider(flat_reduced), _allgather_rider(pack.reshape(-1, 128))),
                      "swap_reduced_gather_small")
    flat_other, g_all = last[:-1], last[-1].reshape(N_DEV, -1, PACK_W)

    out = {}
    for j, k in enumerate(BIG):
        own, other = reduced[j], flat_other[L * j:L * (j + 1)]
        if k == "w_in":
            t = lambda a: jnp.swapaxes(a, 1, 2)
            res = _adamw_halves(t(env[k]), t(env["m_" + k]), t(env["v_" + k]), own, other, place, "cols", f"adamw_{k}")
            res = [t(a) for a in res]
        else:
            if k in COL_SHARDED:
                own, other = [a.T for a in own], [a.T for a in other]
            res = _adamw_halves(env[k], env["m_" + k], env["v_" + k], own, other, place,
                                "rows" if k in COL_SHARDED else "cols", f"adamw_{k}")
        out[k] = tuple(res)

    dmod_all = g_all[:, :L * 6].reshape(N_DEV, L, 6 * D_MODEL)
    dmod_sh = jnp.transpose(lax.dynamic_slice_in_dim(dmod_all, chip * ada_cols, ada_cols, axis=2), (1, 0, 2))
    g_ada = _mod_bwd(c_all, dmod_sh, "mod_bwd")
    g_, d_, m_, v_ = _adamw(w_ada.reshape(-1, ada_cols), m_w_ada.reshape(-1, ada_cols), v_w_ada.reshape(-1, ada_cols),
                            [g_ada.reshape(-1, ada_cols)], "adamw_w_ada")
    out["w_ada"] = tuple(a.reshape(w_ada.shape) for a in (g_, d_, m_, v_))

    def small_pack(prefix):
        parts = [env[prefix + "b_ada"]]
        for k in small_names:
            a = env[prefix + k]
            if k == "conv_w":
                a = jnp.zeros((L,) + shapes[k], F32)
            parts.append(a)
        return _pack(parts)

    gp, dp, mp, vp = _adamw_small(small_pack(""), small_pack("m_"), small_pack("v_"), g_all, "adamw_small")
    full_shapes = [(L,) + shapes["b_ada"]] + [(L,) + shapes[k] for k in small_names]
    for tag, packed in (("g", gp), ("d", dp), ("m", mp), ("v", vp)):
        for k, a in zip(["b_ada"] + small_names, _unpack(packed, full_shapes)):
            out.setdefault(k, {})
            out[k][tag] = a
    g_cw_full = out["conv_w"]["g"]
    cw_cols = D_CONV // N_CHIP
    g_cw = lax.dynamic_slice_in_dim(g_cw_full, chip * cw_cols, cw_cols, axis=2)
    pad_rows = lambda a: jnp.pad(a.reshape(L * CONV_WIDTH, cw_cols), ((0, 2), (0, 0)))
    g_, d_, m_, v_ = _adamw(pad_rows(conv_w), pad_rows(m_conv_w), pad_rows(v_conv_w), [pad_rows(g_cw)], "adamw_conv_w")
    out["conv_w"] = tuple(a[:L * CONV_WIDTH].reshape(L, CONV_WIDTH, cw_cols) for a in (g_, d_, m_, v_))

    names = ["w_ada", "b_ada", "w_in", "b_gate", "w_pool", "pool_scale", "rel_bias", "conv_w", "conv_b", "conv_ln_g",
             "conv_ln_b", "w_br_pool", "w_br_attn", "w_br_conv", "w_o", "ln_mix_g", "ln_mix_b", "w_ff1", "b_ff1",
             "w_ff2", "b_ff2", "ln_ff_g", "ln_ff_b"]

    def pick(k, i):
        o = out[k]
        return o[i] if isinstance(o, tuple) else o["gdmv"[i]].reshape(env[k].shape)

    return (loss, grad_x, *[pick(k, 0) for k in names], *[pick(k, 1) for k in names],
            *[pick(k, 2) for k in names], *[pick(k, 3) for k in names])
```

```python
import functools

import jax
import jax.numpy as jnp
import numpy as np
from jax import lax
from jax.experimental import pallas as pl
from jax.experimental.pallas import tpu as pltpu

F32 = jnp.float32
BF16 = jnp.bfloat16

D_MODEL = 1024
DEPTH = 2
CHUNK = 64
POOL_WINDOWS = (2, 4, 8, 16)
POOL_GROUP = 64
D_POOL = 256
N_HEADS = 8
HEAD_DIM = 64
D_ATTN = 512
N_PREV_CHUNKS = 8
REL_CLIP = 128
N_REL = 2 * REL_CLIP + 1
D_CONV = 256
CONV_WIDTH = 31
D_FF = 4 * D_MODEL
D_IN = 5376
OFF_POOL, OFF_QKV, OFF_CONV, OFF_GATE = 0, 256, 1792, 2304
ALPHA = (2.0 * DEPTH) ** 0.25
LN_EPS = 1e-5
NEG_INF = -1e30
ADAM_LR, ADAM_B1, ADAM_B2, ADAM_EPS, ADAM_WD, ADAM_STEP = 0.001, 0.9, 0.999, 1e-08, 0.01, 10

N_DEV = 8
N_CHIP = 4
MESH = pl.DeviceIdType.MESH

QB = 2 * CHUNK
KPAD = N_PREV_CHUNKS * CHUNK
KW = QB + KPAD
SKEW_W = 768

VMEM_LIMIT = 56 * 1024 * 1024


def _cparams(**kw):
    return pltpu.CompilerParams(vmem_limit_bytes=VMEM_LIMIT, **kw)


def _full(shape):
    n = len(shape)
    return pl.BlockSpec(shape, lambda *_: (0,) * n)


_DIMS = {"nn": (((1,), (0,)), ((), ())), "nt": (((1,), (1,)), ((), ())), "tn": (((0,), (0,)), ((), ()))}


def _relu2(t):
    r = jnp.maximum(t, 0.0)
    return r * r


def _mm(a, b, mode, *, tm, tn, out_dtype, name, b_col0=0, n_out=None, bias=None, split_n=0, rider=None):
    if mode == "tn":
        k, m = a.shape
        n = b.shape[1] if n_out is None else n_out
        a_spec = pl.BlockSpec((k, tm), lambda i, j: (0, i))
        b_spec = pl.BlockSpec((k, tn), lambda i, j: (0, j + b_col0))
    elif mode == "nn":
        m, k = a.shape
        n = b.shape[1] if n_out is None else n_out
        a_spec = pl.BlockSpec((tm, k), lambda i, j: (i, 0))
        b_spec = pl.BlockSpec((k, tn), lambda i, j: (0, j + b_col0))
    else:
        m, k = a.shape
        n = b.shape[0] if n_out is None else n_out
        a_spec = pl.BlockSpec((tm, k), lambda i, j: (i, 0))
        b_spec = pl.BlockSpec((tn, k), lambda i, j: (j + b_col0, 0))
    assert m % tm == 0 and n % tn == 0, (name, m, n, tm, tn)
    dims = _DIMS[mode]

    def body(*refs):
        if bias is None:
            a_ref, b_ref, o_ref = refs
        else:
            a_ref, b_ref, bias_ref, o_ref = refs
        acc = lax.dot_general(a_ref[...].astype(BF16), b_ref[...].astype(BF16), dims, preferred_element_type=F32)
        if bias is not None:
            acc = acc + bias_ref[...]
        if split_n:
            for c in range(tn // split_n):
                o_ref[c] = acc[:, c * split_n:(c + 1) * split_n].astype(out_dtype)
        else:
            o_ref[...] = acc.astype(out_dtype)

    in_specs = [a_spec, b_spec]
    args = [a, b]
    if bias is not None:
        in_specs.append(pl.BlockSpec((1, tn), lambda i, j: (0, j)))
        args.append(bias)
    if split_n:
        out_spec = pl.BlockSpec((tn // split_n, tm, split_n), lambda i, j: (j, i, 0))
        out_shape = jax.ShapeDtypeStruct((n // split_n, m, split_n), out_dtype)
    else:
        out_spec = pl.BlockSpec((tm, tn), lambda i, j: (i, j))
        out_shape = jax.ShapeDtypeStruct((m, n), out_dtype)
    res = _call(body, name=name, grid=(m // tm, n // tn), in_specs=in_specs, out_specs=[out_spec],
                out_shape=[out_shape], scratch_shapes=[], args=args, rider=rider)
    return res[0] if rider is None else (res[0][0], res[1])


def _ln_hat(x):
    mu = jnp.mean(x, axis=-1, keepdims=True)
    xc = x - mu
    var = jnp.mean(xc * xc, axis=-1, keepdims=True)
    rstd = lax.rsqrt(var + LN_EPS)
    return xc * rstd, rstd


def _ln_hat_bwd(dhat, xhat, rstd):
    m1 = jnp.mean(dhat, axis=-1, keepdims=True)
    m2 = jnp.mean(dhat * xhat, axis=-1, keepdims=True)
    return rstd * (dhat - m1 - xhat * m2)


def _row_tile(s):
    return min(512, s)


def _acc_rows(ref, val, first):
    @pl.when(first)
    def _():
        ref[...] = jnp.zeros_like(ref)
    ref[...] += jnp.sum(val, axis=0, keepdims=True)


def _ln_mod(x, sc, sh, name):
    s, d = x.shape
    tm = _row_tile(s)

    def body(x_ref, sc_ref, sh_ref, u_ref):
        xhat, _ = _ln_hat(x_ref[...])
        u_ref[...] = (xhat * (1.0 + sc_ref[...]) + sh_ref[...]).astype(BF16)

    row = pl.BlockSpec((tm, d), lambda i: (i, 0))
    vec = pl.BlockSpec((1, d), lambda i: (0, 0))
    return pl.pallas_call(body, grid=(s // tm,), in_specs=[row, vec, vec], out_specs=row,
                          out_shape=jax.ShapeDtypeStruct((s, d), BF16), name=name, compiler_params=_cparams())(x, sc, sh)


def _resid_bwd_tile(dxo, x, f, g, gam):
    rhat, rstd = _ln_hat(ALPHA * x + g * f)
    dr = _ln_hat_bwd(dxo * gam, rhat, rstd)
    return ALPHA * dr, g * dr, dxo * rhat, dr * f


def _mm_ln_mod_bwd(a, b, x, sc, dres, name, rider=None, nxt=None):
    segs = list(a) if isinstance(a, (list, tuple)) else [a]
    s = segs[0].shape[0]
    k, d = b.shape
    assert sum(t.shape[1] for t in segs) == k
    tm = min(512 if k <= 4096 and nxt is None else 256, s)
    ns = len(segs)

    def body(*refs):
        seg_refs = refs[:ns]
        if nxt is None:
            b_ref, x_ref, sc_ref, dres_ref, dx_ref, dsc_ref, dsh_ref = refs[ns:]
        else:
            (b_ref, x_ref, sc_ref, dres_ref, xp_ref, fp_ref, gp_ref, gamp_ref,
             dresp_ref, dfp_ref, dsc_ref, dsh_ref, dgam_ref, dbet_ref, dg_ref, dbias_ref) = refs[ns:]
        first = pl.program_id(0) == 0
        duv, r0 = None, 0
        for seg_ref in seg_refs:
            w = seg_ref.shape[1]
            part = jnp.dot(seg_ref[...], b_ref[r0:r0 + w, :], preferred_element_type=F32)
            duv = part if duv is None else duv + part
            r0 += w
        xhat, rstd = _ln_hat(x_ref[...])
        dxv = dres_ref[...] + _ln_hat_bwd(duv * (1.0 + sc_ref[...]), xhat, rstd)
        _acc_rows(dsc_ref, duv * xhat, first)
        _acc_rows(dsh_ref, duv, first)
        if nxt is None:
            dx_ref[...] = dxv
        else:
            dresp, dfp, t_gam, t_g = _resid_bwd_tile(dxv, xp_ref[...], fp_ref[...], gp_ref[...], gamp_ref[...])
            dresp_ref[...] = dresp
            dfp_ref[...] = dfp.astype(BF16)
            _acc_rows(dgam_ref, t_gam, first)
            _acc_rows(dbet_ref, dxv, first)
            _acc_rows(dg_ref, t_g, first)
            _acc_rows(dbias_ref, dfp, first)

    row = pl.BlockSpec((tm, d), lambda i: (i, 0))
    vec = pl.BlockSpec((1, d), lambda i: (0, 0))
    vs = jax.ShapeDtypeStruct((1, d), F32)
    rows = jax.ShapeDtypeStruct((s, d), F32)
    in_specs = [pl.BlockSpec((tm, t.shape[1]), lambda i: (i, 0)) for t in segs] + [_full((k, d)), row, vec, row]
    args = (*segs, b, x, sc, dres)
    if nxt is None:
        out_specs, out_shape = [row, vec, vec], [rows, vs, vs]
    else:
        in_specs += [row, row, vec, vec]
        args += tuple(nxt)
        out_specs = [row, row] + [vec] * 6
        out_shape = [rows, jax.ShapeDtypeStruct((s, d), BF16)] + [vs] * 6
    res = _call(body, name=name, grid=(s // tm,), in_specs=in_specs, out_specs=out_specs, out_shape=out_shape,
                scratch_shapes=[], args=args, rider=rider)
    return tuple(res) if rider is None else (tuple(res[0]), res[1])


def _dw_segments(segs, u, name):
    s, d = u.shape
    tw = 256
    tiles = [t.shape[1] // tw for t in segs]
    starts = [sum(tiles[:j]) for j in range(len(segs))]
    ns = len(segs)

    def body(*refs):
        seg_refs, u_ref, o_ref = refs[:ns], refs[ns], refs[ns + 1]
        i = pl.program_id(0)
        for seg_ref, t0, nt in zip(seg_refs, starts, tiles):
            @pl.when((i >= t0) & (i < t0 + nt))
            def _(seg_ref=seg_ref):
                acc = lax.dot_general(seg_ref[...], u_ref[...], _DIMS["tn"], preferred_element_type=F32)
                o_ref[0] = acc[:, :d // 2].astype(BF16)
                o_ref[1] = acc[:, d // 2:].astype(BF16)

    def seg_spec(t0, nt):
        return pl.BlockSpec((s, tw), lambda i: (0, jnp.clip(i - t0, 0, nt - 1)))

    return pl.pallas_call(
        body, grid=(sum(tiles),), in_specs=[seg_spec(t0, nt) for t0, nt in zip(starts, tiles)] + [_full((s, d))],
        out_specs=pl.BlockSpec((2, tw, d // 2), lambda i: (0, i, 0)),
        out_shape=jax.ShapeDtypeStruct((2, sum(tiles) * tw, d // 2), BF16), name=name, compiler_params=_cparams(),
    )(*segs, u)


def _mm_resid_ln(a, b, bias, x, g, gam, bet, name, rider=None):
    s, k = a.shape
    d = b.shape[1]
    tm = min(512, s)

    def body(*refs):
        if bias is None:
            a_ref, b_ref, x_ref, g_ref, gam_ref, bet_ref, f_ref, o_ref = refs
        else:
            a_ref, b_ref, bias_ref, x_ref, g_ref, gam_ref, bet_ref, f_ref, o_ref = refs
        f = jnp.dot(a_ref[...], b_ref[...], preferred_element_type=F32)
        if bias is not None:
            f = f + bias_ref[...]
        f_ref[...] = f
        rhat, _ = _ln_hat(ALPHA * x_ref[...] + g_ref[...] * f)
        o_ref[...] = rhat * gam_ref[...] + bet_ref[...]

    row = pl.BlockSpec((tm, d), lambda i: (i, 0))
    vec = pl.BlockSpec((1, d), lambda i: (0, 0))
    in_specs = [pl.BlockSpec((tm, k), lambda i: (i, 0)), _full((k, d))] + ([vec] if bias is not None else []) + [row, vec, vec, vec]
    args = [a, b] + ([bias] if bias is not None else []) + [x, g, gam, bet]
    sh = jax.ShapeDtypeStruct((s, d), F32)
    res = _call(body, name=name, grid=(s // tm,), in_specs=in_specs, out_specs=[row, row], out_shape=[sh, sh],
                scratch_shapes=[], args=args, rider=rider)
    return tuple(res) if rider is None else (tuple(res[0]), res[1])


def _resid_ln_bwd(dxo, x, f, g, gam, name, tgt=None):
    s, d = x.shape
    tm = _row_tile(s)
    n = s // tm

    def body(*refs):
        if tgt is None:
            dxo_ref, x_ref, f_ref, g_ref, gam_ref, dres_ref, df_ref, dgam_ref, dbet_ref, dg_ref, dbias_ref = refs
            dxov = dxo_ref[...]
        else:
            (dxo_ref, t_ref, x_ref, f_ref, g_ref, gam_ref, dres_ref, df_ref, dgam_ref, dbet_ref, dg_ref, dbias_ref,
             loss_ref, sq_ref) = refs
            err = dxo_ref[...] - t_ref[...]
            dxov = err * (1.0 / d)
            _acc_rows(sq_ref, err * err, pl.program_id(0) == 0)

            @pl.when(pl.program_id(0) == n - 1)
            def _():
                tot = jnp.sum(sq_ref[...], axis=1, keepdims=True) * (0.5 / d)
                loss_ref[...] = jnp.broadcast_to(tot, (1, 128))

        first = pl.program_id(0) == 0
        dres, dfv, t_gam, t_g = _resid_bwd_tile(dxov, x_ref[...], f_ref[...], g_ref[...], gam_ref[...])
        dres_ref[...] = dres
        df_ref[...] = dfv.astype(BF16)
        _acc_rows(dgam_ref, t_gam, first)
        _acc_rows(dbet_ref, dxov, first)
        _acc_rows(dg_ref, t_g, first)
        _acc_rows(dbias_ref, dfv, first)

    row = pl.BlockSpec((tm, d), lambda i: (i, 0))
    vec = pl.BlockSpec((1, d), lambda i: (0, 0))
    vs = jax.ShapeDtypeStruct((1, d), F32)
    out_specs = [row, row, vec, vec, vec, vec]
    out_shape = [jax.ShapeDtypeStruct((s, d), F32), jax.ShapeDtypeStruct((s, d), BF16), vs, vs, vs, vs]
    if tgt is None:
        return pl.pallas_call(body, grid=(n,), in_specs=[row, row, row, vec, vec], out_specs=out_specs,
                              out_shape=out_shape, name=name, compiler_params=_cparams())(dxo, x, f, g, gam)
    return pl.pallas_call(body, grid=(n,), in_specs=[row, row, row, row, vec, vec],
                          out_specs=out_specs + [pl.BlockSpec((1, 128), lambda i: (0, 0))],
                          out_shape=out_shape + [jax.ShapeDtypeStruct((1, 128), F32)],
                          scratch_shapes=[pltpu.VMEM((1, d), F32)], name=name,
                          compiler_params=_cparams())(dxo, tgt, x, f, g, gam)


POOL_HALO = 16
POOL_ROWS = 256


def _pool_counts(r0, rows):
    t1 = (lax.broadcasted_iota(jnp.int32, (rows, 128), 0) + r0 + 1).astype(F32)
    low = lax.broadcasted_iota(jnp.int32, (rows, 128), 1) < POOL_GROUP
    wa = jnp.where(low, float(POOL_WINDOWS[0]), float(POOL_WINDOWS[1]))
    wb = jnp.where(low, float(POOL_WINDOWS[2]), float(POOL_WINDOWS[3]))
    return jnp.minimum(t1, wa), jnp.minimum(t1, wb), low


def _window_sums(win, off, rows, sign):
    def sl(j, half):
        return win[off + sign * j: off + sign * j + rows, 128 * half:128 * half + 128]
    a2 = sl(0, 0) + sl(1, 0)
    a4 = a2 + sl(2, 0) + sl(3, 0)
    a8 = sl(0, 1)
    for j in range(1, 8):
        a8 = a8 + sl(j, 1)
    a16 = a8
    for j in range(8, 16):
        a16 = a16 + sl(j, 1)
    return a2, a4, a8, a16


def _pool_fwd(zp, wp_bd, pscale, name):
    s = zp.shape[0]
    r = min(POOL_ROWS, s)

    def body(z_ref, wp_ref, sc_ref, p_ref, feat_ref, pad):
        pad[0:POOL_HALO, :] = jnp.zeros((POOL_HALO, D_POOL), F32)
        pad[POOL_HALO:, :] = z_ref[...]

        def step(i, carry):
            r0 = pl.multiple_of(i * r, r)
            win = pad[pl.ds(r0, r + POOL_HALO), :]
            a2, a4, a8, a16 = _window_sums(win, POOL_HALO, r, -1)
            ca, cb, low = _pool_counts(r0, r)
            x0 = win[POOL_HALO:, :]
            pa = jnp.where(low, a2, a4) / ca
            pb = jnp.where(low, a8, a16) / cb
            p = (jnp.concatenate([pa, pb], axis=1) - x0).astype(BF16)
            p_ref[pl.ds(r0, r), :] = p
            pw = jnp.dot(p, wp_ref[...], preferred_element_type=F32)
            feat_ref[pl.ds(r0, r), :] = (pw * sc_ref[...]).astype(BF16)
            return carry

        lax.fori_loop(0, s // r, step, 0)

    return pl.pallas_call(
        body, out_shape=[jax.ShapeDtypeStruct((s, D_POOL), BF16), jax.ShapeDtypeStruct((s, D_POOL), BF16)],
        scratch_shapes=[pltpu.VMEM((s + POOL_HALO, D_POOL), F32)], name=name, compiler_params=_cparams(),
    )(zp, wp_bd, pscale)


def _pool_bwd(dfeat, p, wp_bd, pscale, name):
    s = p.shape[0]
    r = min(POOL_ROWS, s)

    def body(df_ref, p_ref, wp_ref, sc_ref, dz_ref, dwp_ref, dsc_ref, gpad, dpbuf):
        dwp_ref[...] = jnp.zeros_like(dwp_ref)
        dsc_ref[...] = jnp.zeros_like(dsc_ref)
        gpad[s:, :] = jnp.zeros((POOL_HALO, D_POOL), F32)

        def step1(i, carry):
            r0 = pl.multiple_of(i * r, r)
            pv = p_ref[pl.ds(r0, r), :]
            dfv = df_ref[pl.ds(r0, r), :]
            pw = jnp.dot(pv, wp_ref[...], preferred_element_type=F32)
            dsc_ref[...] += jnp.sum(dfv * pw, axis=0, keepdims=True)
            dpw = (dfv * sc_ref[...]).astype(BF16)
            dwp_ref[...] += lax.dot_general(pv, dpw, _DIMS["tn"], preferred_element_type=F32)
            dp = lax.dot_general(dpw, wp_ref[...], _DIMS["nt"], preferred_element_type=F32)
            ca, cb, _ = _pool_counts(r0, r)
            gpad[pl.ds(r0, r), :] = dp / jnp.concatenate([ca, cb], axis=1)
            dpbuf[pl.ds(r0, r), :] = dp
            return carry

        lax.fori_loop(0, s // r, step1, 0)

        def step2(i, carry):
            r0 = pl.multiple_of(i * r, r)
            win = gpad[pl.ds(r0, r + POOL_HALO), :]
            a2, a4, a8, a16 = _window_sums(win, 0, r, 1)
            low = lax.broadcasted_iota(jnp.int32, (r, 128), 1) < POOL_GROUP
            acc = jnp.concatenate([jnp.where(low, a2, a4), jnp.where(low, a8, a16)], axis=1)
            dz_ref[pl.ds(r0, r), :] = (acc - dpbuf[pl.ds(r0, r), :]).astype(BF16)
            return carry

        lax.fori_loop(0, s // r, step2, 0)

    return pl.pallas_call(
        body,
        out_shape=[jax.ShapeDtypeStruct((s, D_POOL), BF16), jax.ShapeDtypeStruct((D_POOL, D_POOL), F32),
                   jax.ShapeDtypeStruct((1, D_POOL), F32)],
        scratch_shapes=[pltpu.VMEM((s + POOL_HALO, D_POOL), F32), pltpu.VMEM((s, D_POOL), F32)],
        name=name, compiler_params=_cparams(),
    )(dfeat, p, wp_bd, pscale)


def _skew_index():
    cp = lax.broadcasted_iota(jnp.int32, (SKEW_W, N_REL), 0)
    dist = jnp.where(cp < KW, KPAD - cp, KPAD + SKEW_W - cp)
    idx = jnp.clip(dist, -REL_CLIP, REL_CLIP) + REL_CLIP
    return (idx == lax.broadcasted_iota(jnp.int32, (SKEW_W, N_REL), 1)).astype(F32)


def _row_bits(b):
    return (lax.broadcasted_iota(jnp.int32, (QB, SKEW_W), 0) >> b) & 1 == 1


N_EDGE = KPAD // QB


def _bias_block(rel_bias, name):
    def body(rb_ref, o_ref):
        onehot = _skew_index()
        row0 = lax.dot_general(rb_ref[...], onehot, _DIMS["nt"], precision=lax.Precision.HIGHEST,
                               preferred_element_type=F32)
        r = lax.broadcasted_iota(jnp.int32, (QB, KW), 0)
        kk = lax.broadcasted_iota(jnp.int32, (QB, KW), 1)
        cq, ck = r // CHUNK, kk // CHUNK
        band = (ck >= cq) & (ck <= cq + N_PREV_CHUNKS)
        for h in range(N_HEADS):
            t = jnp.broadcast_to(row0[h:h + 1, :], (QB, SKEW_W))
            for b in range(7):
                t = jnp.where(_row_bits(b), pltpu.roll(t, 1 << b, 1), t)
            for e in range(N_EDGE + 1):
                o_ref[e, h] = jnp.where(band & (kk >= KPAD - e * QB), t[:, :KW], NEG_INF)

    return pl.pallas_call(body, out_shape=jax.ShapeDtypeStruct((N_EDGE + 1, N_HEADS, QB, KW), F32), name=name,
                          compiler_params=_cparams())(rel_bias)


def _bias_spec():
    return pl.BlockSpec((None, N_HEADS, QB, KW), lambda i: (jnp.minimum(i, N_EDGE), 0, 0, 0))


def _bias_block_bwd(ds_acc, name):
    def body(ds_ref, o_ref):
        sums = []
        for h in range(N_HEADS):
            t = jnp.concatenate([ds_ref[h], jnp.zeros((QB, SKEW_W - KW), F32)], axis=1)
            for b in range(7):
                t = jnp.where(_row_bits(b), pltpu.roll(t, SKEW_W - (1 << b), 1), t)
            sums.append(jnp.sum(t, axis=0, keepdims=True))
        allh = jnp.concatenate(sums, axis=0)
        o_ref[...] = jnp.dot(allh, _skew_index(), precision=lax.Precision.HIGHEST, preferred_element_type=F32)

    return pl.pallas_call(body, out_shape=jax.ShapeDtypeStruct((N_HEADS, N_REL), F32), name=name,
                          compiler_params=_cparams())(ds_acc)


def _scaled(q):
    return (q.astype(F32) * (HEAD_DIM ** -0.5)).astype(BF16)


def _probs(q, kw, bias_ref):
    sc = jnp.stack([lax.dot_general(q[:, HEAD_DIM * h:HEAD_DIM * (h + 1)], kw[:, HEAD_DIM * h:HEAD_DIM * (h + 1)],
                                    _DIMS["nt"], preferred_element_type=F32) + bias_ref[h] for h in range(N_HEADS)])
    e = jnp.exp(sc - jnp.max(sc, axis=-1, keepdims=True))
    return e * (1.0 / jnp.sum(e, axis=-1, keepdims=True))


def _load_padded_kv(qkv_hbm, kpad, vpad, sems, s):
    kpad[0:KPAD, :] = jnp.zeros((KPAD, D_ATTN), BF16)
    vpad[0:KPAD, :] = jnp.zeros((KPAD, D_ATTN), BF16)
    ck = pltpu.make_async_copy(qkv_hbm.at[:, D_ATTN:2 * D_ATTN], kpad.at[pl.ds(KPAD, s), :], sems.at[0])
    cv = pltpu.make_async_copy(qkv_hbm.at[:, 2 * D_ATTN:3 * D_ATTN], vpad.at[pl.ds(KPAD, s), :], sems.at[1])
    ck.start()
    cv.start()
    ck.wait()
    cv.wait()


def _attn_fwd(qkv, bias, name, rider=None):
    s = qkv.shape[0]

    def body(q_ref, qkv_hbm, bias_ref, o_ref, p_ref, kpad, vpad, sems):
        i = pl.program_id(0)

        @pl.when(i == 0)
        def _():
            _load_padded_kv(qkv_hbm, kpad, vpad, sems, s)

        base = pl.multiple_of(i * QB, QB)
        kw = kpad[pl.ds(base, KW), :]
        vw = vpad[pl.ds(base, KW), :]
        q = _scaled(q_ref[...])
        p = _probs(q, kw, bias_ref).astype(BF16)
        p_ref[...] = p
        outs = [jnp.dot(p[h], vw[:, HEAD_DIM * h:HEAD_DIM * (h + 1)], preferred_element_type=F32)
                for h in range(N_HEADS)]
        o_ref[...] = jnp.concatenate(outs, axis=1).astype(BF16)

    res = _call(
        body, name=name, grid=(s // QB,),
        in_specs=[pl.BlockSpec((QB, D_ATTN), lambda i: (i, 0)), pl.BlockSpec(memory_space=pl.ANY),
                  _bias_spec()],
        out_specs=[pl.BlockSpec((QB, D_ATTN), lambda i: (i, 0)), _probs_spec()],
        out_shape=[jax.ShapeDtypeStruct((s, D_ATTN), BF16), jax.ShapeDtypeStruct((N_HEADS, s, KW), BF16)],
        scratch_shapes=[pltpu.VMEM((s + KPAD, D_ATTN), BF16), pltpu.VMEM((s + KPAD, D_ATTN), BF16),
                        pltpu.SemaphoreType.DMA((2,))],
        args=(qkv, qkv, bias), rider=rider)
    return tuple(res) if rider is None else (tuple(res[0]), res[1])


def _probs_spec():
    return pl.BlockSpec((N_HEADS, QB, KW), lambda i: (0, i, 0))


def _attn_bwd(qkv, do, probs, name, rider=None):
    s = qkv.shape[0]
    n = s // QB

    def body(q_ref, qkv_hbm, do_ref, p_ref, dq_ref, dk_hbm, dv_hbm, ds_ref, kpad, vpad, dkacc, dvacc, sems):
        i = pl.program_id(0)

        @pl.when(i == 0)
        def _():
            _load_padded_kv(qkv_hbm, kpad, vpad, sems, s)
            dkacc[...] = jnp.zeros_like(dkacc)
            dvacc[...] = jnp.zeros_like(dvacc)
            ds_ref[...] = jnp.zeros_like(ds_ref)

        base = pl.multiple_of(i * QB, QB)
        kw = kpad[pl.ds(base, KW), :]
        vw = vpad[pl.ds(base, KW), :]
        q = _scaled(q_ref[...])
        dov = do_ref[...]
        heads = [slice(HEAD_DIM * h, HEAD_DIM * (h + 1)) for h in range(N_HEADS)]
        pb = p_ref[...]
        p = pb.astype(F32)
        dp = jnp.stack([lax.dot_general(dov[:, hs], vw[:, hs], _DIMS["nt"], preferred_element_type=F32) for hs in heads])
        ds = p * (dp - jnp.sum(dp * p, axis=-1, keepdims=True))
        ds_ref[...] += ds
        dsb = ds.astype(BF16)
        dvs = [lax.dot_general(pb[h], dov[:, hs], _DIMS["tn"], preferred_element_type=F32) for h, hs in enumerate(heads)]
        dqs = [jnp.dot(dsb[h], kw[:, hs], preferred_element_type=F32) for h, hs in enumerate(heads)]
        dks = [lax.dot_general(dsb[h], q[:, hs], _DIMS["tn"], preferred_element_type=F32) for h, hs in enumerate(heads)]
        dq_ref[...] = (jnp.concatenate(dqs, axis=1) * (HEAD_DIM ** -0.5)).astype(BF16)
        dkacc[pl.ds(base, KW), :] += jnp.concatenate(dks, axis=1)
        dvacc[pl.ds(base, KW), :] += jnp.concatenate(dvs, axis=1)

        @pl.when(i == n - 1)
        def _():
            def cast(j, carry):
                rows = pl.ds(pl.multiple_of(KPAD + j * 512, 512), 512)
                kpad[rows, :] = dkacc[rows, :].astype(BF16)
                vpad[rows, :] = dvacc[rows, :].astype(BF16)
                return carry

            lax.fori_loop(0, s // 512, cast, 0)
            ck = pltpu.make_async_copy(kpad.at[pl.ds(KPAD, s), :], dk_hbm, sems.at[0])
            cv = pltpu.make_async_copy(vpad.at[pl.ds(KPAD, s), :], dv_hbm, sems.at[1])
            ck.start()
            cv.start()
            ck.wait()
            cv.wait()

    blk = pl.BlockSpec((QB, D_ATTN), lambda i: (i, 0))
    acc_shape = jax.ShapeDtypeStruct((s, D_ATTN), BF16)
    return _call(
        body, name=name, grid=(n,),
        in_specs=[blk, pl.BlockSpec(memory_space=pl.ANY), blk, _probs_spec()],
        out_specs=[blk, pl.BlockSpec(memory_space=pl.ANY), pl.BlockSpec(memory_space=pl.ANY), _full((N_HEADS, QB, KW))],
        out_shape=[jax.ShapeDtypeStruct((s, D_ATTN), BF16), acc_shape, acc_shape,
                   jax.ShapeDtypeStruct((N_HEADS, QB, KW), F32)],
        scratch_shapes=[pltpu.VMEM((s + KPAD, D_ATTN), BF16), pltpu.VMEM((s + KPAD, D_ATTN), BF16),
                        pltpu.VMEM((s + KPAD, D_ATTN), F32), pltpu.VMEM((s + KPAD, D_ATTN), F32),
                        pltpu.SemaphoreType.DMA((2,))],
        args=(qkv, qkv, do, probs), rider=rider)


CONV_HALO = 32
CONV_ROWS = 64


def _sigmoid(t):
    return 1.0 / (1.0 + jnp.exp(-t))


CONV_WIN = CONV_ROWS + CONV_HALO - 8


def _row_windows(ref, r0, buf):
    win = ref[pl.ds(r0, CONV_ROWS + CONV_HALO), :]
    for j in range(1, 8):
        buf[j - 1] = win[j:j + CONV_WIN, :]

    def get(o):
        j, a = o % 8, o - o % 8
        if j == 0:
            return ref[pl.ds(r0 + a, CONV_ROWS), :]
        return buf[j - 1, a:a + CONV_ROWS, :]

    return get


def _glu_rows(z_ref, r0, rows):
    a = z_ref[pl.ds(r0, rows), 0:D_CONV]
    b = z_ref[pl.ds(r0, rows), D_CONV:2 * D_CONV]
    return a, _sigmoid(b)


def _conv_fwd(zc, conv_w, conv_b, ln_g, ln_b, name):
    s = zc.shape[0]
    rt = min(256, s)

    def body(z_ref, w_ref, cb_ref, g_ref, b_ref, cv_ref, feat_ref, hpad, shifts):
        hpad[0:CONV_HALO, :] = jnp.zeros((CONV_HALO, D_CONV), F32)

        def glu(i, carry):
            r0 = pl.multiple_of(i * rt, rt)
            a, sb = _glu_rows(z_ref, r0, rt)
            hpad[pl.ds(r0 + CONV_HALO, rt), :] = a * sb
            return carry

        lax.fori_loop(0, s // rt, glu, 0)
        w = w_ref[...]

        def conv(i, carry):
            r0 = pl.multiple_of(i * CONV_ROWS, CONV_ROWS)
            win = _row_windows(hpad, r0, shifts)
            acc = jnp.broadcast_to(cb_ref[...], (CONV_ROWS, D_CONV))
            for k in range(CONV_WIDTH):
                acc = acc + win(2 + k) * w[k:k + 1, :]
            cv_ref[pl.ds(r0, CONV_ROWS), :] = acc
            yhat, _ = _ln_hat(acc)
            y = yhat * g_ref[...] + b_ref[...]
            feat_ref[pl.ds(r0, CONV_ROWS), :] = (y * _sigmoid(y)).astype(BF16)
            return carry

        lax.fori_loop(0, s // CONV_ROWS, conv, 0)

    return pl.pallas_call(
        body, out_shape=[jax.ShapeDtypeStruct((s, D_CONV), F32), jax.ShapeDtypeStruct((s, D_CONV), BF16)],
        scratch_shapes=[pltpu.VMEM((s + CONV_HALO, D_CONV), F32), pltpu.VMEM((7, CONV_WIN, D_CONV), F32)],
        name=name, compiler_params=_cparams(),
    )(zc, conv_w, conv_b, ln_g, ln_b)


def _conv_bwd(dfeat, cv, zc, conv_w, ln_g, ln_b, name):
    s = zc.shape[0]
    rt = min(256, s)

    def body(df_ref, cv_ref, z_ref, w_ref, g_ref, b_ref, dz_ref, dw_ref, dcb_ref, dg_ref, db_ref, hpad, dcvpad, dwacc,
             hshifts, dshifts):
        hpad[0:CONV_HALO, :] = jnp.zeros((CONV_HALO, D_CONV), F32)
        dcvpad[s:, :] = jnp.zeros((CONV_HALO, D_CONV), F32)
        dwacc[...] = jnp.zeros_like(dwacc)
        dcb_ref[...] = jnp.zeros_like(dcb_ref)
        dg_ref[...] = jnp.zeros_like(dg_ref)
        db_ref[...] = jnp.zeros_like(db_ref)

        def pass1(i, carry):
            r0 = pl.multiple_of(i * rt, rt)
            a, sb = _glu_rows(z_ref, r0, rt)
            hpad[pl.ds(r0 + CONV_HALO, rt), :] = a * sb
            cvhat, rstd = _ln_hat(cv_ref[pl.ds(r0, rt), :])
            y = cvhat * g_ref[...] + b_ref[...]
            sg = _sigmoid(y)
            dy = df_ref[pl.ds(r0, rt), :] * (sg * (1.0 + y * (1.0 - sg)))
            dg_ref[...] += jnp.sum(dy * cvhat, axis=0, keepdims=True)
            db_ref[...] += jnp.sum(dy, axis=0, keepdims=True)
            dcv = _ln_hat_bwd(dy * g_ref[...], cvhat, rstd)
            dcb_ref[...] += jnp.sum(dcv, axis=0, keepdims=True)
            dcvpad[pl.ds(r0, rt), :] = dcv
            return carry

        lax.fori_loop(0, s // rt, pass1, 0)
        w = w_ref[...]

        def pass2(i, carry):
            r0 = pl.multiple_of(i * CONV_ROWS, CONV_ROWS)
            dwin = _row_windows(dcvpad, r0, dshifts)
            hwin = _row_windows(hpad, r0, hshifts)
            dcv = dwin(0)
            dh = jnp.zeros((CONV_ROWS, D_CONV), F32)
            for k in range(CONV_WIDTH):
                dh = dh + dwin(30 - k) * w[k:k + 1, :]
                prod = dcv * hwin(2 + k)
                dwacc[8 * k:8 * k + 8, :] += jnp.sum(prod.reshape(CONV_ROWS // 8, 8, D_CONV), axis=0)
            a, sb = _glu_rows(z_ref, r0, CONV_ROWS)
            dz_ref[pl.ds(r0, CONV_ROWS), :] = jnp.concatenate([dh * sb, dh * a * sb * (1.0 - sb)], axis=1).astype(BF16)
            return carry

        lax.fori_loop(0, s // CONV_ROWS, pass2, 0)
        dw_ref[...] = jnp.sum(dwacc[...].reshape(32, 8, D_CONV), axis=1)

    vs = jax.ShapeDtypeStruct((1, D_CONV), F32)
    return pl.pallas_call(
        body,
        out_shape=[jax.ShapeDtypeStruct((s, 2 * D_CONV), BF16), jax.ShapeDtypeStruct((32, D_CONV), F32), vs, vs, vs],
        scratch_shapes=[pltpu.VMEM((s + CONV_HALO, D_CONV), F32), pltpu.VMEM((s + CONV_HALO, D_CONV), F32),
                        pltpu.VMEM((256, D_CONV), F32), pltpu.VMEM((7, CONV_WIN, D_CONV), F32),
                        pltpu.VMEM((7, CONV_WIN, D_CONV), F32)],
        name=name, compiler_params=_cparams(),
    )(dfeat, cv, zc, conv_w, ln_g, ln_b)


def _merge(zg, b_gate, ys, name):
    s = zg.shape[0]
    tm = _row_tile(s)

    def body(zg_ref, bg_ref, y0_ref, y1_ref, y2_ref, o_ref):
        acc = None
        for j, y_ref in enumerate((y0_ref, y1_ref, y2_ref)):
            cs = slice(D_MODEL * j, D_MODEL * (j + 1))
            t = _sigmoid(zg_ref[:, cs] + bg_ref[:, cs]) * y_ref[...]
            acc = t if acc is None else acc + t
        o_ref[...] = acc.astype(BF16)

    row = pl.BlockSpec((tm, D_MODEL), lambda i: (i, 0))
    return pl.pallas_call(
        body, grid=(s // tm,),
        in_specs=[pl.BlockSpec((tm, 3 * D_MODEL), lambda i: (i, 0)), _full((1, 3 * D_MODEL)), row, row, row],
        out_specs=row, out_shape=jax.ShapeDtypeStruct((s, D_MODEL), BF16), name=name, compiler_params=_cparams(),
    )(zg, b_gate, *ys)


def _merge_bwd(dm, zg, b_gate, ys, name):
    s = zg.shape[0]
    tm = min(256, s)

    def body(dm_ref, zg_ref, bg_ref, y0_ref, y1_ref, y2_ref, d0_ref, d1_ref, d2_ref, dzg_ref, dbg_ref):
        first = pl.program_id(0) == 0

        @pl.when(first)
        def _():
            dbg_ref[...] = jnp.zeros_like(dbg_ref)

        dmv = dm_ref[...]
        for j, (y_ref, d_ref) in enumerate(((y0_ref, d0_ref), (y1_ref, d1_ref), (y2_ref, d2_ref))):
            cs = slice(D_MODEL * j, D_MODEL * (j + 1))
            g = _sigmoid(zg_ref[:, cs] + bg_ref[:, cs])
            d_ref[...] = (dmv * g).astype(BF16)
            dzg = dmv * y_ref[...] * g * (1.0 - g)
            dzg_ref[:, cs] = dzg.astype(BF16)
            dbg_ref[:, cs] += jnp.sum(dzg, axis=0, keepdims=True)

    row = pl.BlockSpec((tm, D_MODEL), lambda i: (i, 0))
    wide = pl.BlockSpec((tm, 3 * D_MODEL), lambda i: (i, 0))
    yb = jax.ShapeDtypeStruct((s, D_MODEL), BF16)
    return pl.pallas_call(
        body, grid=(s // tm,),
        in_specs=[row, wide, _full((1, 3 * D_MODEL)), row, row, row],
        out_specs=[row, row, row, wide, _full((1, 3 * D_MODEL))],
        out_shape=[yb, yb, yb, jax.ShapeDtypeStruct((s, 3 * D_MODEL), BF16), jax.ShapeDtypeStruct((1, 3 * D_MODEL), F32)],
        name=name, compiler_params=_cparams(),
    )(dm, zg, b_gate, *ys)


def _ff_hidden(u2, w_ff1t, b_ff1, name, rider=None):
    s = u2.shape[0]
    tm, tn = min(2048, s), 1024

    def body(a_ref, b_ref, bias_ref, pre_ref, h_ref):
        acc = lax.dot_general(a_ref[...], b_ref[...], _DIMS["nt"], preferred_element_type=F32) + bias_ref[...]
        pre_ref[...] = acc.astype(BF16)
        h_ref[...] = _relu2(acc).astype(BF16)

    blk = pl.BlockSpec((tm, tn), lambda i, j: (i, j))
    sh = jax.ShapeDtypeStruct((s, D_FF), BF16)
    res = _call(body, name=name, grid=(s // tm, D_FF // tn),
                in_specs=[pl.BlockSpec((tm, D_MODEL), lambda i, j: (i, 0)), pl.BlockSpec((tn, D_MODEL), lambda i, j: (j, 0)),
                          pl.BlockSpec((1, tn), lambda i, j: (0, j))],
                out_specs=[blk, blk], out_shape=[sh, sh], scratch_shapes=[], args=(u2, w_ff1t, b_ff1), rider=rider)
    return tuple(res) if rider is None else (tuple(res[0]), res[1])


def _ff_hidden_bwd(dff, w_ff2, hpre, name):
    s = dff.shape[0]
    tm, tn = min(1024, s), 1024

    def body(a_ref, b_ref, h_ref, o_ref, sum_ref):
        dh = lax.dot_general(a_ref[...], b_ref[...], _DIMS["nt"], preferred_element_type=F32)
        dpre = dh * (2.0 * jnp.maximum(h_ref[...].astype(F32), 0.0))
        o_ref[...] = dpre.astype(BF16)
        _acc_rows(sum_ref, dpre, pl.program_id(1) == 0)

    return pl.pallas_call(
        body, grid=(D_FF // tn, s // tm),
        in_specs=[pl.BlockSpec((tm, D_MODEL), lambda j, i: (i, 0)), pl.BlockSpec((tn, D_MODEL), lambda j, i: (j, 0)),
                  pl.BlockSpec((tm, tn), lambda j, i: (i, j))],
        out_specs=[pl.BlockSpec((tm, tn), lambda j, i: (i, j)), pl.BlockSpec((1, tn), lambda j, i: (0, j))],
        out_shape=[jax.ShapeDtypeStruct((s, D_FF), BF16), jax.ShapeDtypeStruct((1, D_FF), F32)],
        name=name, compiler_params=_cparams(),
    )(dff, w_ff2, hpre)


def _silu(t):
    return t * _sigmoid(t)


def _mod_fwd(c_all, w_ada_sh, b_ada_sh, name):
    cols = w_ada_sh.shape[2]

    def body(c_ref, w_ref, b_ref, o_ref):
        ca = _silu(c_ref[...]).astype(BF16)
        o_ref[0] = jnp.dot(ca, w_ref[0].astype(BF16), preferred_element_type=F32) + b_ref[0]

    return pl.pallas_call(
        body, grid=(DEPTH,),
        in_specs=[_full((N_DEV, D_MODEL)), pl.BlockSpec((1, D_MODEL, cols), lambda l: (l, 0, 0)),
                  pl.BlockSpec((1, 1, cols), lambda l: (l, 0, 0))],
        out_specs=pl.BlockSpec((1, N_DEV, cols), lambda l: (l, 0, 0)),
        out_shape=jax.ShapeDtypeStruct((DEPTH, N_DEV, cols), F32), name=name, compiler_params=_cparams(),
    )(c_all, w_ada_sh, b_ada_sh)


def _mod_bwd(c_all, dmod_sh, name):
    cols = dmod_sh.shape[2]

    def body(c_ref, d_ref, o_ref):
        ca = _silu(c_ref[...])
        o_ref[0] = lax.dot_general(ca, d_ref[0], _DIMS["tn"], precision=lax.Precision.HIGHEST,
                                   preferred_element_type=F32)

    return pl.pallas_call(
        body, grid=(DEPTH,),
        in_specs=[_full((N_DEV, D_MODEL)), pl.BlockSpec((1, N_DEV, cols), lambda l: (l, 0, 0))],
        out_specs=pl.BlockSpec((1, D_MODEL, cols), lambda l: (l, 0, 0)),
        out_shape=jax.ShapeDtypeStruct((DEPTH, D_MODEL, cols), F32), name=name, compiler_params=_cparams(),
    )(c_all, dmod_sh)


def _flat_tiles(rows, cols, itemsize_total):
    budget = 12 * 1024 * 1024
    tr = rows
    while tr % 32 == 0 and tr * cols * itemsize_total > budget:
        tr //= 2
    return tr


def _sum_cores(dw, recv, place, name):
    _, m, n = dw.shape
    tr = _flat_tiles(m, n, 6)

    def body(place_ref, a_ref, b_ref, o_ref):
        o_ref[...] = (a_ref[...].astype(F32) + b_ref[...].astype(F32)).astype(BF16)

    grid_spec = pltpu.PrefetchScalarGridSpec(
        num_scalar_prefetch=1, grid=(m // tr,),
        in_specs=[pl.BlockSpec((None, tr, n), lambda i, pr: (pr[0], i, 0)), pl.BlockSpec((tr, n), lambda i, pr: (i, 0))],
        out_specs=pl.BlockSpec((tr, n), lambda i, pr: (i, 0)))
    return pl.pallas_call(body, grid_spec=grid_spec, out_shape=jax.ShapeDtypeStruct((m, n), BF16), name=name,
                          compiler_params=_cparams())(place, dw, recv)


def _sum_chips(h, r, place, name):
    _, rs, n = h.shape
    tr = _flat_tiles(rs, n, 12)

    def body(place_ref, h_ref, r_ref, o_ref):
        o_ref[...] = ((h_ref[...].astype(F32) + r_ref[0].astype(F32)) + r_ref[1].astype(F32)) + r_ref[2].astype(F32)

    grid_spec = pltpu.PrefetchScalarGridSpec(
        num_scalar_prefetch=1, grid=(rs // tr,),
        in_specs=[pl.BlockSpec((None, tr, n), lambda i, pr: (pr[1], i, 0)), pl.BlockSpec((3, tr, n), lambda i, pr: (0, i, 0))],
        out_specs=pl.BlockSpec((tr, n), lambda i, pr: (i, 0)))
    return pl.pallas_call(body, grid_spec=grid_spec, out_shape=jax.ShapeDtypeStruct((rs, n), F32), name=name,
                          compiler_params=_cparams())(place, h, r)


def _adam_math(w, g, m, v):
    m2 = ADAM_B1 * m + (1.0 - ADAM_B1) * g
    v2 = ADAM_B2 * v + (1.0 - ADAM_B2) * (g * g)
    m_hat = m2 / (1.0 - ADAM_B1 ** ADAM_STEP)
    v_hat = v2 / (1.0 - ADAM_B2 ** ADAM_STEP)
    delta = -ADAM_LR * (m_hat / (jnp.sqrt(v_hat) + ADAM_EPS) + ADAM_WD * w)
    return delta, m2, v2


def _adamw(w, m, v, grads, name):
    r, c = w.shape
    tr = _flat_tiles(r, c, 4 * (7 + len(grads)))

    def body(*refs):
        w_ref, m_ref, v_ref = refs[:3]
        g_refs = refs[3:3 + len(grads)]
        g_ref, d_ref, m2_ref, v2_ref = refs[3 + len(grads):]
        g = g_refs[0][...]
        for gr in g_refs[1:]:
            g = g + gr[...]
        delta, m2, v2 = _adam_math(w_ref[...], g, m_ref[...], v_ref[...])
        g_ref[...] = g
        d_ref[...] = delta
        m2_ref[...] = m2
        v2_ref[...] = v2

    blk = pl.BlockSpec((tr, c), lambda i: (i, 0))
    sh = jax.ShapeDtypeStruct((r, c), F32)
    return pl.pallas_call(body, grid=(r // tr,), in_specs=[blk] * (3 + len(grads)), out_specs=[blk] * 4,
                          out_shape=[sh] * 4, name=name, compiler_params=_cparams())(w, m, v, *grads)


def _adamw_halves(w, m, v, own, other, place, split, name):
    nl, r, c = w.shape
    hr, hc = own[0].shape
    tr = _flat_tiles(hr, hc, 4 * (7 + 2 * nl))
    nt = hr // tr
    if split == "rows":
        w_spec = pl.BlockSpec((None, tr, c), lambda l, h, t, pr: (l, h * nt + t, 0))
    else:
        w_spec = pl.BlockSpec((None, tr, hc), lambda l, h, t, pr: (l, t, h))

    def g_spec(layer, mine):
        return pl.BlockSpec((tr, hc), lambda l, h, t, pr: (jnp.where((l == layer) & ((h == pr[0]) == mine), t, nt - 1), 0))

    def body(place_ref, w_ref, m_ref, v_ref, *refs):
        own_refs, other_refs = refs[:nl], refs[nl:2 * nl]
        g_ref, d_ref, m2_ref, v2_ref = refs[2 * nl:]
        layer = pl.program_id(0)
        mine = pl.program_id(1) == place_ref[0]
        g = None
        for li in range(nl):
            cand = jnp.where(mine, own_refs[li][...], other_refs[li][...])
            g = cand if g is None else jnp.where(layer == li, cand, g)
        delta, m2, v2 = _adam_math(w_ref[...], g, m_ref[...], v_ref[...])
        g_ref[...] = g
        d_ref[...] = delta
        m2_ref[...] = m2
        v2_ref[...] = v2

    sh = jax.ShapeDtypeStruct((nl, r, c), F32)
    g_specs = [g_spec(li, True) for li in range(nl)] + [g_spec(li, False) for li in range(nl)]
    return _call(body, name=name, grid=(nl, 2, nt), in_specs=[w_spec] * 3 + g_specs, out_specs=[w_spec] * 4,
                 out_shape=[sh] * 4, scratch_shapes=[], args=(w, m, v, *own, *other), prefetch=(place,))


def _adamw_small(w, m, v, g_all, name):
    r, c = w.shape

    def body(w_ref, m_ref, v_ref, g_ref, go_ref, d_ref, m2_ref, v2_ref):
        g = g_ref[0]
        for b in range(1, N_DEV):
            g = g + g_ref[b]
        delta, m2, v2 = _adam_math(w_ref[...], g, m_ref[...], v_ref[...])
        go_ref[...] = g
        d_ref[...] = delta
        m2_ref[...] = m2
        v2_ref[...] = v2

    sh = jax.ShapeDtypeStruct((r, c), F32)
    return pl.pallas_call(body, out_shape=[sh] * 4, name=name, compiler_params=_cparams())(w, m, v, g_all)


def _me():
    return lax.axis_index("x"), lax.axis_index("y"), lax.axis_index("c")


def _flip(v, bit):
    return 1 - v if bit else v


def _allgather_small(blk, name):
    r, c = blk.shape

    def body(x_ref, o_ref, send_sems, recv_sems):
        x, y, cc = _me()
        me = 4 * x + 2 * y + cc
        copies = []
        for k in range(1, N_DEV):
            peer = (_flip(x, k & 4), _flip(y, k & 2), _flip(cc, k & 1))
            cp = pltpu.make_async_remote_copy(src_ref=x_ref, dst_ref=o_ref.at[me], send_sem=send_sems.at[k - 1],
                                              recv_sem=recv_sems.at[k - 1], device_id=peer, device_id_type=MESH)
            cp.start()
            copies.append(cp)
        o_ref[me] = x_ref[...]
        for cp in copies:
            cp.wait()

    return pl.pallas_call(
        body, out_shape=jax.ShapeDtypeStruct((N_DEV, r, c), F32),
        in_specs=[pl.BlockSpec(memory_space=pltpu.VMEM)], out_specs=pl.BlockSpec(memory_space=pltpu.VMEM),
        scratch_shapes=[pltpu.SemaphoreType.DMA((N_DEV - 1,)), pltpu.SemaphoreType.DMA((N_DEV - 1,))],
        name=name, compiler_params=_cparams(),
    )(blk)


class _Rider:
    def __init__(self, arrays, out_shapes, scratch_shapes, start, finish):
        self.arrays, self.out_shapes, self.scratch_shapes = list(arrays), list(out_shapes), list(scratch_shapes)
        self.start, self.finish = start, finish


def _call(body, *, name, grid, in_specs, out_specs, out_shape, scratch_shapes, args, rider=None, prefetch=()):
    npf = len(prefetch)

    def launch(fn, in_specs, out_specs, out_shape, scratch_shapes, args):
        grid_spec = pltpu.PrefetchScalarGridSpec(num_scalar_prefetch=npf, grid=grid, in_specs=in_specs,
                                                 out_specs=out_specs, scratch_shapes=scratch_shapes)
        return pl.pallas_call(fn, grid_spec=grid_spec, out_shape=out_shape, name=name,
                              compiler_params=_cparams())(*prefetch, *args)

    if rider is None:
        return launch(body, list(in_specs), list(out_specs), list(out_shape), list(scratch_shapes), args)
    ni, no, ns = len(in_specs), len(out_specs), len(scratch_shapes)
    ri, ro = len(rider.arrays), len(rider.out_shapes)
    steps = int(np.prod(grid))

    def wrapped(*refs):
        pf, refs = refs[:npf], refs[npf:]
        h_in, r_in = refs[:ni], refs[ni:ni + ri]
        h_out, r_out = refs[ni + ri:ni + ri + no], refs[ni + ri + no:ni + ri + no + ro]
        h_scr, r_scr = refs[ni + ri + no + ro:ni + ri + no + ro + ns], refs[ni + ri + no + ro + ns:]
        step = pl.program_id(0)
        for d in range(1, len(grid)):
            step = step * grid[d] + pl.program_id(d)

        @pl.when(step == 0)
        def _():
            rider.start(r_in, r_out, r_scr)

        body(*pf, *h_in, *h_out, *h_scr)

        @pl.when(step == steps - 1)
        def _():
            rider.finish(r_in, r_out, r_scr)

    anyspec = pl.BlockSpec(memory_space=pl.ANY)
    res = launch(wrapped, list(in_specs) + [anyspec] * ri, list(out_specs) + [anyspec] * ro,
                 list(out_shape) + rider.out_shapes, list(scratch_shapes) + rider.scratch_shapes,
                 list(args) + rider.arrays)
    return res[:no], res[no:]


def _run_rider(rider, name):
    ri = len(rider.arrays)

    def body(*refs):
        r_in, r_out, r_scr = refs[:ri], refs[ri:ri + len(rider.out_shapes)], refs[ri + len(rider.out_shapes):]
        rider.start(r_in, r_out, r_scr)
        rider.finish(r_in, r_out, r_scr)

    anyspec = pl.BlockSpec(memory_space=pl.ANY)
    return pl.pallas_call(body, in_specs=[anyspec] * ri, out_specs=[anyspec] * len(rider.out_shapes),
                          out_shape=rider.out_shapes, scratch_shapes=rider.scratch_shapes, name=name,
                          compiler_params=_cparams())(*rider.arrays)


def _allgather_rider(blk):
    def copies(ins, outs, scr):
        send_sems, recv_sems, loc_sems, stage = scr
        x, y, cc = _me()
        me = 4 * x + 2 * y + cc
        remote = [pltpu.make_async_remote_copy(
            src_ref=ins[0], dst_ref=outs[0].at[me], send_sem=send_sems.at[k - 1], recv_sem=recv_sems.at[k - 1],
            device_id=(_flip(x, k & 4), _flip(y, k & 2), _flip(cc, k & 1)), device_id_type=MESH) for k in range(1, N_DEV)]
        return remote, pltpu.make_async_copy(ins[0], stage, loc_sems.at[0]), (outs[0].at[me], stage, loc_sems.at[1])

    def start(ins, outs, scr):
        remote, lin, _ = copies(ins, outs, scr)
        lin.start()
        for cp in remote:
            cp.start()

    def finish(ins, outs, scr):
        remote, lin, (dst, stage, sem) = copies(ins, outs, scr)
        lin.wait()
        lout = pltpu.make_async_copy(stage, dst, sem)
        lout.start()
        for cp in remote:
            cp.wait()
        lout.wait()

    return _Rider([blk], [jax.ShapeDtypeStruct((N_DEV,) + blk.shape, blk.dtype)],
                  [pltpu.SemaphoreType.DMA((N_DEV - 1,)), pltpu.SemaphoreType.DMA((N_DEV - 1,)),
                   pltpu.SemaphoreType.DMA((2,)), pltpu.VMEM(blk.shape, blk.dtype)], start, finish)


def _gather_rider(shards):
    n = len(shards)

    def copies(ins, outs, scr, relay=True):
        ici_send, ici_recv, d2d_send, d2d_recv, loc_sems = scr[:5]
        stage = scr[5:]
        x, y, cc = _me()
        chip = 2 * x + y
        sibling = (x, y, 1 - cc)
        local, sends, relays = [], [], []
        for j in range(n):
            def rows(ch, h, j=j):
                return outs[j].at[ch, h]

            lc = pltpu.make_async_copy(ins[j], stage[j], loc_sems.at[j])
            local.append((lc, pltpu.make_async_copy(stage[j], outs[j].at[chip], loc_sems.at[n + j]) if relay else None))
            for k in range(1, N_CHIP):
                px, py = _flip(x, k & 2), _flip(y, k & 1)
                pchip = 2 * px + py
                q = 3 * j + k - 1
                out_cp = pltpu.make_async_remote_copy(src_ref=ins[j].at[cc], dst_ref=rows(chip, cc),
                                                      send_sem=ici_send.at[q], recv_sem=ici_recv.at[q],
                                                      device_id=(px, py, cc), device_id_type=MESH)
                sends.append(out_cp)
                if not relay:
                    continue
                arrival = pltpu.make_async_remote_copy(src_ref=rows(pchip, cc), dst_ref=rows(pchip, cc),
                                                       send_sem=ici_send.at[q], recv_sem=ici_recv.at[q],
                                                       device_id=(px, py, cc), device_id_type=MESH)
                forward = pltpu.make_async_remote_copy(src_ref=rows(pchip, cc), dst_ref=rows(pchip, cc),
                                                       send_sem=d2d_send.at[q], recv_sem=d2d_recv.at[q],
                                                       device_id=sibling, device_id_type=MESH)
                from_sibling = pltpu.make_async_remote_copy(src_ref=rows(pchip, 1 - cc), dst_ref=rows(pchip, 1 - cc),
                                                            send_sem=d2d_send.at[q], recv_sem=d2d_recv.at[q],
                                                            device_id=sibling, device_id_type=MESH)
                relays.append((arrival, forward, from_sibling))
        return local, sends, relays

    def start(ins, outs, scr):
        local, sends, _ = copies(ins, outs, scr, relay=False)
        for lin, _ in local:
            lin.start()
        for cp in sends:
            cp.start()

    def finish(ins, outs, scr):
        local, sends, relays = copies(ins, outs, scr)
        for lin, lout in local:
            lin.wait()
            lout.start()
        for arrival, forward, _ in relays:
            arrival.wait_recv()
            forward.start()
        for cp in sends:
            cp.wait_send()
        for _, forward, from_sibling in relays:
            forward.wait_send()
            from_sibling.wait_recv()
        for _, lout in local:
            lout.wait()

    scratch = [pltpu.SemaphoreType.DMA((3 * n,)), pltpu.SemaphoreType.DMA((3 * n,)), pltpu.SemaphoreType.DMA((3 * n,)),
               pltpu.SemaphoreType.DMA((3 * n,)), pltpu.SemaphoreType.DMA((2 * n,))]
    scratch += [pltpu.VMEM(a.shape, a.dtype) for a in shards]
    return _Rider(shards, [jax.ShapeDtypeStruct((N_CHIP,) + a.shape, a.dtype) for a in shards], scratch, start, finish)


def _sibling_rider(arrs, other_half=False):
    n = len(arrs)

    def copies(ins, outs, scr):
        send_sems, recv_sems = scr
        x, y, cc = _me()
        return [pltpu.make_async_remote_copy(
            src_ref=ins[j].at[1 - cc] if other_half else ins[j], dst_ref=outs[j], send_sem=send_sems.at[j],
            recv_sem=recv_sems.at[j], device_id=(x, y, 1 - cc), device_id_type=MESH) for j in range(n)]

    def start(ins, outs, scr):
        for cp in copies(ins, outs, scr):
            cp.start()

    def finish(ins, outs, scr):
        for cp in copies(ins, outs, scr):
            cp.wait()

    return _Rider(arrs, [jax.ShapeDtypeStruct(a.shape[1:] if other_half else a.shape, a.dtype) for a in arrs],
                  [pltpu.SemaphoreType.DMA((n,)), pltpu.SemaphoreType.DMA((n,))], start, finish)


def _sibling_send(arrs, name, other_half=False):
    return _run_rider(_sibling_rider(arrs, other_half), name)


def _join_riders(first, second):
    ni, no, ns = len(first.arrays), len(first.out_shapes), len(first.scratch_shapes)

    def split(ins, outs, scr):
        return (ins[:ni], outs[:no], scr[:ns]), (ins[ni:], outs[no:], scr[ns:])

    def start(ins, outs, scr):
        a, b = split(ins, outs, scr)
        first.start(*a)
        second.start(*b)

    def finish(ins, outs, scr):
        a, b = split(ins, outs, scr)
        first.finish(*a)
        second.finish(*b)

    return _Rider(first.arrays + second.arrays, first.out_shapes + second.out_shapes,
                  first.scratch_shapes + second.scratch_shapes, start, finish)


def _scatter_rider(arrs):
    n = len(arrs)

    def copies(ins, outs, scr):
        send_sems, recv_sems = scr
        x, y, cc = _me()
        cps = []
        for j in range(n):
            for k in range(1, N_CHIP):
                px, py = _flip(x, k & 2), _flip(y, k & 1)
                cps.append(pltpu.make_async_remote_copy(
                    src_ref=ins[j].at[2 * px + py], dst_ref=outs[j].at[k - 1], send_sem=send_sems.at[3 * j + k - 1],
                    recv_sem=recv_sems.at[3 * j + k - 1], device_id=(px, py, cc), device_id_type=MESH))
        return cps

    def start(ins, outs, scr):
        for cp in copies(ins, outs, scr):
            cp.start()

    def finish(ins, outs, scr):
        for cp in copies(ins, outs, scr):
            cp.wait()

    return _Rider(arrs, [jax.ShapeDtypeStruct((N_CHIP - 1,) + a.shape[1:], a.dtype) for a in arrs],
                  [pltpu.SemaphoreType.DMA((3 * n,)), pltpu.SemaphoreType.DMA((3 * n,))], start, finish)


COL_SHARDED = ("w_in", "w_br_pool", "w_br_attn", "w_br_conv", "w_ff1")
ROW_SHARDED = ("w_o", "w_ff2")
BIG = COL_SHARDED + ROW_SHARDED
SMALL = ("b_ada", "b_gate", "w_pool", "pool_scale", "rel_bias", "conv_w", "conv_b", "conv_ln_g", "conv_ln_b",
         "ln_mix_g", "ln_mix_b", "b_ff1", "b_ff2", "ln_ff_g", "ln_ff_b")
PACK_W = 1024


def _pack(parts):
    rows = []
    for a in parts:
        flat = a.reshape(-1)
        n = -(-flat.shape[0] // PACK_W) * PACK_W
        rows.append(jnp.pad(flat, (0, n - flat.shape[0])).reshape(-1, PACK_W))
    out = jnp.concatenate(rows, axis=0)
    r = -(-out.shape[0] // 8) * 8
    return jnp.pad(out, ((0, r - out.shape[0]), (0, 0)))


def _unpack(packed, shapes):
    out, r0 = [], 0
    for shp in shapes:
        size = int(np.prod(shp))
        nr = -(-size // PACK_W)
        out.append(packed[r0:r0 + nr].reshape(-1)[:size].reshape(shp))
        r0 += nr
    return out


def _hosted(fn, hook, *args, **kw):
    if hook is None:
        return fn(*args, **kw)
    res, rider_out = fn(*args, rider=hook[0], **kw)
    hook[1](rider_out)
    return res


def _layer_fwd(l, x, mod, W, P, hooks=None):
    hooks = hooks or {}
    s = x.shape[0]
    sh_m, sc_m, g_m, sh_f, sc_f, g_f = [mod[l:l + 1, D_MODEL * j:D_MODEL * (j + 1)] for j in range(6)]
    n = lambda t: f"{t}{l}"
    w_in = W["w_in"][l]
    u = _ln_mod(x, sc_m, sh_m, n("ln_mod_mix"))
    zp = _mm(u, w_in, "nt", tm=s, tn=256, out_dtype=F32, name=n("z_pool"), b_col0=0, n_out=D_POOL)
    qkv = _mm(u, w_in, "nt", tm=s, tn=256, out_dtype=BF16, name=n("z_qkv"), b_col0=OFF_QKV // 256, n_out=3 * D_ATTN)
    zc = _mm(u, w_in, "nt", tm=s, tn=256, out_dtype=F32, name=n("z_conv"), b_col0=OFF_CONV // 256, n_out=2 * D_CONV)
    zg = _hosted(_mm, hooks.get("z_gate"), u, w_in, "nt", tm=min(2048, s), tn=768, out_dtype=BF16, name=n("z_gate"),
                 b_col0=OFF_GATE // 768, n_out=3 * D_MODEL)

    p, feat_pool = _pool_fwd(zp, P["wp_bd"][l], P["pool_scale"][l], n("pool_fwd"))
    bias = _bias_block(P["rel_bias"][l], n("bias_block"))
    o, probs = _hosted(_attn_fwd, hooks.get("attn"), qkv, bias, n("attn_fwd"))
    cv, feat_conv = _conv_fwd(zc, P["conv_w"][l], P["conv_b"][l], P["conv_ln_g"][l], P["conv_ln_b"][l], n("conv_fwd"))

    tmb = min(1024, s)
    y_pool = _mm(feat_pool, W["w_br_pool"][l], "nt", tm=tmb, tn=1024, out_dtype=BF16, name=n("y_pool"))
    y_attn = _mm(o, W["w_br_attn"][l], "nt", tm=tmb, tn=1024, out_dtype=BF16, name=n("y_attn"))
    y_conv = _mm(feat_conv, W["w_br_conv"][l], "nt", tm=tmb, tn=1024, out_dtype=BF16, name=n("y_conv"))
    ys = (y_pool, y_attn, y_conv)
    merged = _merge(zg, P["b_gate"][l], ys, n("merge"))
    mix, x1 = _mm_resid_ln(merged, W["w_o"][l], None, x, g_m, P["ln_mix_g"][l], P["ln_mix_b"][l], n("mix_out"))

    u2 = _ln_mod(x1, sc_f, sh_f, n("ln_mod_ff"))
    hpre, hid = _hosted(_ff_hidden, hooks.get("ff1"), u2, W["w_ff1"][l], P["b_ff1"][l], n("ff1"))
    ff, x2 = _hosted(_mm_resid_ln, hooks.get("ff2"), hid, W["w_ff2"][l], P["b_ff2"][l], x1, g_f, P["ln_ff_g"][l],
                     P["ln_ff_b"][l], n("ff2"))
    saved = dict(x=x, u=u, zp=zp, qkv=qkv, zc=zc, zg=zg, p=p, feat_pool=feat_pool, probs=probs, o=o, cv=cv,
                 feat_conv=feat_conv, ys=ys, merged=merged, mix=mix, x1=x1, u2=u2, hpre=hpre, hid=hid, ff=ff)
    return x2, saved


def _layer_bwd(l, dx2, mod, W, P, A, hooks=None, tgt=None, nxt=None):
    hooks = hooks or {}
    sh_m, sc_m, g_m, sh_f, sc_f, g_f = [mod[l:l + 1, D_MODEL * j:D_MODEL * (j + 1)] for j in range(6)]
    n = lambda t: f"{t}{l}"
    gw, gs = {}, {}

    if isinstance(dx2, tuple):
        dres, dff, gs["ln_ff_g"], gs["ln_ff_b"], dg_f, gs["b_ff2"] = dx2
    else:
        dres, dff, gs["ln_ff_g"], gs["ln_ff_b"], dg_f, gs["b_ff2"], *loss_part = _resid_ln_bwd(
            dx2, A["x1"], A["ff"], g_f, P["ln_ff_g"][l], n("resid_ln_ff_bwd"), tgt=tgt)
    s = dres.shape[0]
    tmb = min(1024, s)
    gw["w_ff2"] = _mm(A["hid"], dff, "tn", tm=512, tn=1024, out_dtype=BF16, name=n("dw_ff2"), split_n=512)
    dhpre, gs["b_ff1"] = _ff_hidden_bwd(dff, W["w_ff2"][l], A["hpre"], n("ff_hidden_bwd"))
    gw["w_ff1"] = _mm(dhpre, A["u2"], "tn", tm=512, tn=1024, out_dtype=BF16, name=n("dw_ff1"), split_n=512)

    dres, dmix, dsc_f, dsh_f, gs["ln_mix_g"], gs["ln_mix_b"], dg_m, _ = _mm_ln_mod_bwd(
        dhpre, W["w_ff1"][l], A["x1"], sc_f, dres, n("du_ff"), nxt=(A["x"], A["mix"], g_m, P["ln_mix_g"][l]))
    gw["w_o"] = _mm(A["merged"], dmix, "tn", tm=512, tn=1024, out_dtype=BF16, name=n("dw_o"), split_n=512)
    dmerged = _mm(dmix, W["w_o"][l], "nt", tm=tmb, tn=1024, out_dtype=F32, name=n("d_merged"))
    dy_pool, dy_attn, dy_conv, dzg, gs["b_gate"] = _merge_bwd(dmerged, A["zg"], P["b_gate"][l], A["ys"], n("merge_bwd"))

    gw["w_br_pool"] = _mm(dy_pool, A["feat_pool"], "tn", tm=512, tn=256, out_dtype=BF16, name=n("dw_br_pool"),
                          split_n=128)
    gw["w_br_attn"] = _mm(dy_attn, A["o"], "tn", tm=512, tn=512, out_dtype=BF16, name=n("dw_br_attn"), split_n=256)
    gw["w_br_conv"] = _mm(dy_conv, A["feat_conv"], "tn", tm=512, tn=256, out_dtype=BF16, name=n("dw_br_conv"),
                          split_n=128)
    dfeat_pool = _mm(dy_pool, W["w_br_pool"][l], "nn", tm=tmb, tn=256, out_dtype=F32, name=n("d_feat_pool"))
    do = _mm(dy_attn, W["w_br_attn"][l], "nn", tm=tmb, tn=512, out_dtype=BF16, name=n("d_attn_out"))
    dfeat_conv = _mm(dy_conv, W["w_br_conv"][l], "nn", tm=tmb, tn=256, out_dtype=F32, name=n("d_feat_conv"))

    dzp, dwp_bd, gs["pool_scale"] = _pool_bwd(dfeat_pool, A["p"], P["wp_bd"][l], P["pool_scale"][l], n("pool_bwd"))
    gs["w_pool"] = jnp.stack([dwp_bd[POOL_GROUP * g:POOL_GROUP * (g + 1), POOL_GROUP * g:POOL_GROUP * (g + 1)]
                              for g in range(len(POOL_WINDOWS))])
    hook = hooks["attn"](gw) if "attn" in hooks else None
    dq, dk, dv, ds_acc = _hosted(_attn_bwd, hook, A["qkv"], do, A["probs"], n("attn_bwd"))
    gs["rel_bias"] = _bias_block_bwd(ds_acc, n("bias_block_bwd"))
    dzc, dcw, gs["conv_b"], gs["conv_ln_g"], gs["conv_ln_b"] = _conv_bwd(
        dfeat_conv, A["cv"], A["zc"], P["conv_w"][l], P["conv_ln_g"][l], P["conv_ln_b"][l], n("conv_bwd"))
    gs["conv_w"] = dcw[:CONV_WIDTH]

    dz = [dzp, dq, dk, dv, dzc, dzg]
    gw["w_in"] = _dw_segments(dz, A["u"], n("dw_in"))
    hook = hooks["du_mix"](gw) if "du_mix" in hooks else None
    res = _hosted(_mm_ln_mod_bwd, hook, dz, W["w_in"][l], A["x"], sc_m, dres, n("du_mix"), nxt=nxt)
    if nxt is None:
        dx, dsc_m, dsh_m = res
    else:
        dx, dsc_m, dsh_m = (res[0], res[1], *res[4:]), res[2], res[3]
    dmod = jnp.concatenate([dsh_m, dsc_m, dg_m, dsh_f, dsc_f, dg_f], axis=1)
    return (dx, gw, gs, dmod) if tgt is None else (dx, gw, gs, dmod, loss_part[0])


def _small_shapes():
    return {"b_ada": (6 * D_MODEL,), "b_gate": (3 * D_MODEL,), "w_pool": (4, POOL_GROUP, POOL_GROUP),
            "pool_scale": (D_POOL,), "rel_bias": (N_HEADS, N_REL), "conv_w": (CONV_WIDTH, D_CONV),
            "conv_b": (D_CONV,), "conv_ln_g": (D_CONV,), "conv_ln_b": (D_CONV,), "ln_mix_g": (D_MODEL,),
            "ln_mix_b": (D_MODEL,), "b_ff1": (D_FF,), "b_ff2": (D_MODEL,), "ln_ff_g": (D_MODEL,), "ln_ff_b": (D_MODEL,)}


def kernel(x, c, w_ada, b_ada, w_in, b_gate, w_pool, pool_scale, rel_bias, conv_w, conv_b, conv_ln_g, conv_ln_b, w_br_pool, w_br_attn, w_br_conv, w_o, ln_mix_g, ln_mix_b, w_ff1, b_ff1, w_ff2, b_ff2, ln_ff_g, ln_ff_b, loss_target, m_w_ada, m_b_ada, m_w_in, m_b_gate, m_w_pool, m_pool_scale, m_rel_bias, m_conv_w, m_conv_b, m_conv_ln_g, m_conv_ln_b, m_w_br_pool, m_w_br_attn, m_w_br_conv, m_w_o, m_ln_mix_g, m_ln_mix_b, m_w_ff1, m_b_ff1, m_w_ff2, m_b_ff2, m_ln_ff_g, m_ln_ff_b, v_w_ada, v_b_ada, v_w_in, v_b_gate, v_w_pool, v_pool_scale, v_rel_bias, v_conv_w, v_conv_b, v_conv_ln_g, v_conv_ln_b, v_w_br_pool, v_w_br_attn, v_w_br_conv, v_w_o, v_ln_mix_g, v_ln_mix_b, v_w_ff1, v_b_ff1, v_w_ff2, v_b_ff2, v_ln_ff_g, v_ln_ff_b):
    env = dict(locals())
    xi, yi, ci = _me()
    chip = 2 * xi + yi
    me = 4 * xi + 2 * yi + ci
    xs = x[0]
    tgt = loss_target[0]
    L = DEPTH

    first = _allgather_small(jnp.concatenate([c.reshape(8, 128), _pack([conv_w]).reshape(-1, 128)]), "gather_c_conv_w")
    c_all = first[:, :8].reshape(N_DEV, D_MODEL)
    ada_cols = w_ada.shape[2]
    b_ada_sh = lax.dynamic_slice_in_dim(b_ada, chip * ada_cols, ada_cols, axis=1).reshape(L, 1, ada_cols)
    mod_part = _mod_fwd(c_all, w_ada, b_ada_sh, "mod_fwd")
    mod_g = _allgather_small(mod_part.reshape(-1, 128), "gather_mod").reshape(N_CHIP, 2, L, N_DEV, ada_cols)[:, 0]
    mod_all = jnp.transpose(mod_g, (1, 2, 0, 3)).reshape(L, N_DEV, 6 * D_MODEL)
    mod = lax.dynamic_index_in_dim(mod_all, me, axis=1, keepdims=False)

    W = {k: [None] * L for k in BIG}

    def weight_gather(names, l):
        shards = [(jnp.swapaxes(env[k][l], 0, 1) if k in COL_SHARDED else env[k][l]).astype(BF16) for k in names]
        shards = [a.reshape(2, a.shape[0] // 2, a.shape[1]) for a in shards]

        def done(outs):
            for k, g in zip(names, outs):
                W[k][l] = g.reshape(-1, g.shape[-1])

        return _gather_rider(shards), done

    branch_names = ("w_br_pool", "w_br_attn", "w_br_conv", "w_o")
    late_names = ("w_ff1", "w_ff2")
    rider, done = weight_gather(("w_in",), 0)
    done(_run_rider(rider, "gather_w_in0"))
    fwd_hooks = [{"z_gate": weight_gather(branch_names, 0), "attn": weight_gather(late_names, 0),
                  "ff1": weight_gather(("w_in",), 1), "ff2": weight_gather(branch_names, 1)},
                 {"attn": weight_gather(late_names, 1)}]

    P = {k: env[k] for k in ("rel_bias", "conv_w")}
    for k in ("b_gate", "pool_scale", "conv_b", "conv_ln_g", "conv_ln_b", "ln_mix_g", "ln_mix_b", "b_ff1", "b_ff2",
              "ln_ff_g", "ln_ff_b"):
        P[k] = env[k].reshape(L, 1, -1)
    n_cw = conv_w.size
    cw = first[:, 8:].reshape(N_CHIP, 2, -1)[:, 0, :n_cw].reshape(N_CHIP, L, CONV_WIDTH, D_CONV // N_CHIP)
    P["conv_w"] = jnp.transpose(cw, (1, 2, 0, 3)).reshape(L, CONV_WIDTH, D_CONV)
    wp_bd = jnp.zeros((L, D_POOL, D_POOL), F32)
    for g in range(len(POOL_WINDOWS)):
        sl = slice(POOL_GROUP * g, POOL_GROUP * (g + 1))
        wp_bd = wp_bd.at[:, sl, sl].set(w_pool[:, g])
    P["wp_bd"] = wp_bd.astype(BF16)

    acts = []
    h = xs
    for l in range(L):
        h, saved = _layer_fwd(l, h, mod, W, P, fwd_hooks[l])
        acts.append(saved)

    place = jnp.stack([ci, chip, chip ^ 1, chip ^ 2, chip ^ 3]).astype(jnp.int32)
    scattered = {}

    def grad_scatter(items, tag):
        dws = [dw for _, _, dw in items]
        got = _sibling_send(dws, f"swap_blocks_{tag}", other_half=True)
        both = [_sum_cores(a, b, place, f"sum_cores_{k}{l}") for (k, l, _), a, b in zip(items, dws, got)]
        both = [hh.reshape(N_CHIP, -1, hh.shape[-1]) for hh in both]

        def done(outs):
            for (k, l, _), hh, r in zip(items, both, outs):
                scattered[(k, l)] = (hh, r)

        return _scatter_rider(both), done

    early = ("w_ff2", "w_ff1", "w_o", "w_br_pool", "w_br_attn", "w_br_conv")
    left_over = []

    def attn_hook(l):
        def hook(gw):
            items = left_over + [(k, l, gw[k]) for k in early]
            left_over.clear()
            return grad_scatter(items, f"attn{l}")
        return hook

    def last_hook(gw):
        return grad_scatter([("w_in", 0, gw["w_in"])], "last")

    gws, gss, dmods = [None] * L, [None] * L, [None] * L
    dh = h
    for l in reversed(range(L)):
        hooks = {"attn": attn_hook(l)}
        if l == 0:
            hooks["du_mix"] = last_hook
        below = None
        if l > 0:
            below = (acts[l - 1]["x1"], acts[l - 1]["ff"], mod[l - 1:l, 5 * D_MODEL:], P["ln_ff_g"][l - 1])
        if l == L - 1:
            dh, gws[l], gss[l], dmods[l], loss_part = _layer_bwd(l, dh, mod, W, P, acts[l], hooks, tgt=tgt, nxt=below)
        else:
            dh, gws[l], gss[l], dmods[l] = _layer_bwd(l, dh, mod, W, P, acts[l], hooks, nxt=below)
        if l > 0:
            left_over.append(("w_in", l, gws[l]["w_in"]))
    grad_x = dh[None]
    loss = lax.psum(loss_part[0, 0], ("x", "y", "c"))

    reduced = [[_sum_chips(*scattered[(k, l)], place, f"sum_chips_{k}{l}") for l in range(L)] for k in BIG]
    flat_reduced = [t for per_weight in reduced for t in per_weight]

    shapes = _small_shapes()
    small_names = [k for k in SMALL if k != "b_ada"]
    dmod_own = jnp.concatenate(dmods, axis=0)
    pack = _pack([dmod_own] + [jnp.stack([gss[l][k].reshape(shapes[k]) for l in range(L)]) for k in small_names])
    last = _run_rider(_join_riders(_sibling_rider(flat_reduced), _allgather_rider(pack.reshape(-1, 128))),
                      "swap_reduced_gather_small")
    flat_other, g_all = last[:-1], last[-1].reshape(N_DEV, -1, PACK_W)

    out = {}
    for j, k in enumerate(BIG):
        own, other = reduced[j], flat_other[L * j:L * (j + 1)]
        if k == "w_in":
            t = lambda a: jnp.swapaxes(a, 1, 2)
            res = _adamw_halves(t(env[k]), t(env["m_" + k]), t(env["v_" + k]), own, other, place, "cols", f"adamw_{k}")
            res = [t(a) for a in res]
        else:
            if k in COL_SHARDED:
                own, other = [a.T for a in own], [a.T for a in other]
            res = _adamw_halves(env[k], env["m_" + k], env["v_" + k], own, other, place,
                                "rows" if k in COL_SHARDED else "cols", f"adamw_{k}")
        out[k] = tuple(res)

    dmod_all = g_all[:, :L * 6].reshape(N_DEV, L, 6 * D_MODEL)
    dmod_sh = jnp.transpose(lax.dynamic_slice_in_dim(dmod_all, chip * ada_cols, ada_cols, axis=2), (1, 0, 2))
    g_ada = _mod_bwd(c_all, dmod_sh, "mod_bwd")
    g_, d_, m_, v_ = _adamw(w_ada.reshape(-1, ada_cols), m_w_ada.reshape(-1, ada_cols), v_w_ada.reshape(-1, ada_cols),
                            [g_ada.reshape(-1, ada_cols)], "adamw_w_ada")
    out["w_ada"] = tuple(a.reshape(w_ada.shape) for a in (g_, d_, m_, v_))

    def small_pack(prefix):
        parts = [env[prefix + "b_ada"]]
        for k in small_names:
            a = env[prefix + k]
            if k == "conv_w":
                a = jnp.zeros((L,) + shapes[k], F32)
            parts.append(a)
        return _pack(parts)

    gp, dp, mp, vp = _adamw_small(small_pack(""), small_pack("m_"), small_pack("v_"), g_all, "adamw_small")
    full_shapes = [(L,) + shapes["b_ada"]] + [(L,) + shapes[k] for k in small_names]
    for tag, packed in (("g", gp), ("d", dp), ("m", mp), ("v", vp)):
        for k, a in zip(["b_ada"] + small_names, _unpack(packed, full_shapes)):
            out.setdefault(k, {})
            out[k][tag] = a
    g_cw_full = out["conv_w"]["g"]
    cw_cols = D_CONV // N_CHIP
    g_cw = lax.dynamic_slice_in_dim(g_cw_full, chip * cw_cols, cw_cols, axis=2)
    pad_rows = lambda a: jnp.pad(a.reshape(L * CONV_WIDTH, cw_cols), ((0, 2), (0, 0)))
    g_, d_, m_, v_ = _adamw(pad_rows(conv_w), pad_rows(m_conv_w), pad_rows(v_conv_w), [pad_rows(g_cw)], "adamw_conv_w")
    out["conv_w"] = tuple(a[:L * CONV_WIDTH].reshape(L, CONV_WIDTH, cw_cols) for a in (g_, d_, m_, v_))

    names = ["w_ada", "b_ada", "w_in", "b_gate", "w_pool", "pool_scale", "rel_bias", "conv_w", "conv_b", "conv_ln_g",
             "conv_ln_b", "w_br_pool", "w_br_attn", "w_br_conv", "w_o", "ln_mix_g", "ln_mix_b", "w_ff1", "b_ff1",
             "w_ff2", "b_ff2", "ln_ff_g", "ln_ff_b"]

    def pick(k, i):
        o = out[k]
        return o[i] if isinstance(o, tuple) else o["gdmv"[i]].reshape(env[k].shape)

    return (loss, grad_x, *[pick(k, 0) for k in names], *[pick(k, 1) for k in names],
            *[pick(k, 2) for k in names], *[pick(k, 3) for k in names])
```

```python
import functools

import jax
import jax.numpy as jnp
import numpy as np
from jax import lax
from jax.experimental import pallas as pl
from jax.experimental.pallas import tpu as pltpu

F32 = jnp.float32
BF16 = jnp.bfloat16

D_MODEL = 1024
DEPTH = 2
CHUNK = 64
POOL_WINDOWS = (2, 4, 8, 16)
POOL_GROUP = 64
D_POOL = 256
N_HEADS = 8
HEAD_DIM = 64
D_ATTN = 512
N_PREV_CHUNKS = 8
REL_CLIP = 128
N_REL = 2 * REL_CLIP + 1
D_CONV = 256
CONV_WIDTH = 31
D_FF = 4 * D_MODEL
D_IN = 5376
OFF_POOL, OFF_QKV, OFF_CONV, OFF_GATE = 0, 256, 1792, 2304
ALPHA = (2.0 * DEPTH) ** 0.25
LN_EPS = 1e-5
NEG_INF = -1e30
ADAM_LR, ADAM_B1, ADAM_B2, ADAM_EPS, ADAM_WD, ADAM_STEP = 0.001, 0.9, 0.999, 1e-08, 0.01, 10

N_DEV = 8
N_CHIP = 4
MESH = pl.DeviceIdType.MESH

QB = 2 * CHUNK
KPAD = N_PREV_CHUNKS * CHUNK
KW = QB + KPAD
SKEW_W = 768

VMEM_LIMIT = 56 * 1024 * 1024


def _cparams(**kw):
    return pltpu.CompilerParams(vmem_limit_bytes=VMEM_LIMIT, **kw)


def _full(shape):
    n = len(shape)
    return pl.BlockSpec(shape, lambda *_: (0,) * n)


_DIMS = {"nn": (((1,), (0,)), ((), ())), "nt": (((1,), (1,)), ((), ())), "tn": (((0,), (0,)), ((), ()))}


def _relu2(t):
    r = jnp.maximum(t, 0.0)
    return r * r


def _mm(a, b, mode, *, tm, tn, out_dtype, name, b_col0=0, n_out=None, bias=None, split_n=0, rider=None):
    if mode == "tn":
        k, m = a.shape
        n = b.shape[1] if n_out is None else n_out
        a_spec = pl.BlockSpec((k, tm), lambda i, j: (0, i))
        b_spec = pl.BlockSpec((k, tn), lambda i, j: (0, j + b_col0))
    elif mode == "nn":
        m, k = a.shape
        n = b.shape[1] if n_out is None else n_out
        a_spec = pl.BlockSpec((tm, k), lambda i, j: (i, 0))
        b_spec = pl.BlockSpec((k, tn), lambda i, j: (0, j + b_col0))
    else:
        m, k = a.shape
        n = b.shape[0] if n_out is None else n_out
        a_spec = pl.BlockSpec((tm, k), lambda i, j: (i, 0))
        b_spec = pl.BlockSpec((tn, k), lambda i, j: (j + b_col0, 0))
    assert m % tm == 0 and n % tn == 0, (name, m, n, tm, tn)
    dims = _DIMS[mode]

    def body(*refs):
        if bias is None:
            a_ref, b_ref, o_ref = refs
        else:
            a_ref, b_ref, bias_ref, o_ref = refs
        acc = lax.dot_general(a_ref[...].astype(BF16), b_ref[...].astype(BF16), dims, preferred_element_type=F32)
        if bias is not None:
            acc = acc + bias_ref[...]
        if split_n:
            for c in range(tn // split_n):
                o_ref[c] = acc[:, c * split_n:(c + 1) * split_n].astype(out_dtype)
        else:
            o_ref[...] = acc.astype(out_dtype)

    in_specs = [a_spec, b_spec]
    args = [a, b]
    if bias is not None:
        in_specs.append(pl.BlockSpec((1, tn), lambda i, j: (0, j)))
        args.append(bias)
    if split_n:
        out_spec = pl.BlockSpec((tn // split_n, tm, split_n), lambda i, j: (j, i, 0))
        out_shape = jax.ShapeDtypeStruct((n // split_n, m, split_n), out_dtype)
    else:
        out_spec = pl.BlockSpec((tm, tn), lambda i, j: (i, j))
        out_shape = jax.ShapeDtypeStruct((m, n), out_dtype)
    res = _call(body, name=name, grid=(m // tm, n // tn), in_specs=in_specs, out_specs=[out_spec],
                out_shape=[out_shape], scratch_shapes=[], args=args, rider=rider)
    return res[0] if rider is None else (res[0][0], res[1])


def _ln_hat(x):
    mu = jnp.mean(x, axis=-1, keepdims=True)
    xc = x - mu
    var = jnp.mean(xc * xc, axis=-1, keepdims=True)
    rstd = lax.rsqrt(var + LN_EPS)
    return xc * rstd, rstd


def _ln_hat_bwd(dhat, xhat, rstd):
    m1 = jnp.mean(dhat, axis=-1, keepdims=True)
    m2 = jnp.mean(dhat * xhat, axis=-1, keepdims=True)
    return rstd * (dhat - m1 - xhat * m2)


def _row_tile(s):
    return min(512, s)


def _acc_rows(ref, val, first):
    @pl.when(first)
    def _():
        ref[...] = jnp.zeros_like(ref)
    ref[...] += jnp.sum(val, axis=0, keepdims=True)


def _ln_mod(x, sc, sh, name):
    s, d = x.shape
    tm = _row_tile(s)

    def body(x_ref, sc_ref, sh_ref, u_ref):
        xhat, _ = _ln_hat(x_ref[...])
        u_ref[...] = (xhat * (1.0 + sc_ref[...]) + sh_ref[...]).astype(BF16)

    row = pl.BlockSpec((tm, d), lambda i: (i, 0))
    vec = pl.BlockSpec((1, d), lambda i: (0, 0))
    return pl.pallas_call(body, grid=(s // tm,), in_specs=[row, vec, vec], out_specs=row,
                          out_shape=jax.ShapeDtypeStruct((s, d), BF16), name=name, compiler_params=_cparams())(x, sc, sh)


def _resid_bwd_tile(dxo, x, f, g, gam):
    rhat, rstd = _ln_hat(ALPHA * x + g * f)
    dr = _ln_hat_bwd(dxo * gam, rhat, rstd)
    return ALPHA * dr, g * dr, dxo * rhat, dr * f


def _mm_ln_mod_bwd(a, b, x, sc, dres, name, rider=None, nxt=None):
    segs = list(a) if isinstance(a, (list, tuple)) else [a]
    s = segs[0].shape[0]
    k, d = b.shape
    assert sum(t.shape[1] for t in segs) == k
    tm = min(512 if k <= 4096 and nxt is None else 256, s)
    ns = len(segs)

    def body(*refs):
        seg_refs = refs[:ns]
        if nxt is None:
            b_ref, x_ref, sc_ref, dres_ref, dx_ref, dsc_ref, dsh_ref = refs[ns:]
        else:
            (b_ref, x_ref, sc_ref, dres_ref, xp_ref, fp_ref, gp_ref, gamp_ref,
             dresp_ref, dfp_ref, dsc_ref, dsh_ref, dgam_ref, dbet_ref, dg_ref, dbias_ref) = refs[ns:]
        first = pl.program_id(0) == 0
        duv, r0 = None, 0
        for seg_ref in seg_refs:
            w = seg_ref.shape[1]
            part = jnp.dot(seg_ref[...], b_ref[r0:r0 + w, :], preferred_element_type=F32)
            duv = part if duv is None else duv + part
            r0 += w
        xhat, rstd = _ln_hat(x_ref[...])
        dxv = dres_ref[...] + _ln_hat_bwd(duv * (1.0 + sc_ref[...]), xhat, rstd)
        _acc_rows(dsc_ref, duv * xhat, first)
        _acc_rows(dsh_ref, duv, first)
        if nxt is None:
            dx_ref[...] = dxv
        else:
            dresp, dfp, t_gam, t_g = _resid_bwd_tile(dxv, xp_ref[...], fp_ref[...], gp_ref[...], gamp_ref[...])
            dresp_ref[...] = dresp
            dfp_ref[...] = dfp.astype(BF16)
            _acc_rows(dgam_ref, t_gam, first)
            _acc_rows(dbet_ref, dxv, first)
            _acc_rows(dg_ref, t_g, first)
            _acc_rows(dbias_ref, dfp, first)

    row = pl.BlockSpec((tm, d), lambda i: (i, 0))
    vec = pl.BlockSpec((1, d), lambda i: (0, 0))
    vs = jax.ShapeDtypeStruct((1, d), F32)
    rows = jax.ShapeDtypeStruct((s, d), F32)
    in_specs = [pl.BlockSpec((tm, t.shape[1]), lambda i: (i, 0)) for t in segs] + [_full((k, d)), row, vec, row]
    args = (*segs, b, x, sc, dres)
    if nxt is None:
        out_specs, out_shape = [row, vec, vec], [rows, vs, vs]
    else:
        in_specs += [row, row, vec, vec]
        args += tuple(nxt)
        out_specs = [row, row] + [vec] * 6
        out_shape = [rows, jax.ShapeDtypeStruct((s, d), BF16)] + [vs] * 6
    res = _call(body, name=name, grid=(s // tm,), in_specs=in_specs, out_specs=out_specs, out_shape=out_shape,
                scratch_shapes=[], args=args, rider=rider)
    return tuple(res) if rider is None else (tuple(res[0]), res[1])


def _dw_segments(segs, u, name):
    s, d = u.shape
    tw = 256
    tiles = [t.shape[1] // tw for t in segs]
    starts = [sum(tiles[:j]) for j in range(len(segs))]
    ns = len(segs)

    def body(*refs):
        seg_refs, u_ref, o_ref = refs[:ns], refs[ns], refs[ns + 1]
        i = pl.program_id(0)
        for seg_ref, t0, nt in zip(seg_refs, starts, tiles):
            @pl.when((i >= t0) & (i < t0 + nt))
            def _(seg_ref=seg_ref):
                acc = lax.dot_general(seg_ref[...], u_ref[...], _DIMS["tn"], preferred_element_type=F32)
                o_ref[0] = acc[:, :d // 2].astype(BF16)
                o_ref[1] = acc[:, d // 2:].astype(BF16)

    def seg_spec(t0, nt):
        return pl.BlockSpec((s, tw), lambda i: (0, jnp.clip(i - t0, 0, nt - 1)))

    return pl.pallas_call(
        body, grid=(sum(tiles),), in_specs=[seg_spec(t0, nt) for t0, nt in zip(starts, tiles)] + [_full((s, d))],
        out_specs=pl.BlockSpec((2, tw, d // 2), lambda i: (0, i, 0)),
        out_shape=jax.ShapeDtypeStruct((2, sum(tiles) * tw, d // 2), BF16), name=name, compiler_params=_cparams(),
    )(*segs, u)


def _mm_resid_ln(a, b, bias, x, g, gam, bet, name, rider=None):
    s, k = a.shape
    d = b.shape[1]
    tm = min(512, s)

    def body(*refs):
        if bias is None:
            a_ref, b_ref, x_ref, g_ref, gam_ref, bet_ref, f_ref, o_ref = refs
        else:
            a_ref, b_ref, bias_ref, x_ref, g_ref, gam_ref, bet_ref, f_ref, o_ref = refs
        f = jnp.dot(a_ref[...], b_ref[...], preferred_element_type=F32)
        if bias is not None:
            f = f + bias_ref[...]
        f_ref[...] = f
        rhat, _ = _ln_hat(ALPHA * x_ref[...] + g_ref[...] * f)
        o_ref[...] = rhat * gam_ref[...] + bet_ref[...]

    row = pl.BlockSpec((tm, d), lambda i: (i, 0))
    vec = pl.BlockSpec((1, d), lambda i: (0, 0))
    in_specs = [pl.BlockSpec((tm, k), lambda i: (i, 0)), _full((k, d))] + ([vec] if bias is not None else []) + [row, vec, vec, vec]
    args = [a, b] + ([bias] if bias is not None else []) + [x, g, gam, bet]
    sh = jax.ShapeDtypeStruct((s, d), F32)
    res = _call(body, name=name, grid=(s // tm,), in_specs=in_specs, out_specs=[row, row], out_shape=[sh, sh],
                scratch_shapes=[], args=args, rider=rider)
    return tuple(res) if rider is None else (tuple(res[0]), res[1])


def _resid_ln_bwd(dxo, x, f, g, gam, name, tgt=None):
    s, d = x.shape
    tm = _row_tile(s)
    n = s // tm

    def body(*refs):
        if tgt is None:
            dxo_ref, x_ref, f_ref, g_ref, gam_ref, dres_ref, df_ref, dgam_ref, dbet_ref, dg_ref, dbias_ref = refs
            dxov = dxo_ref[...]
        else:
            (dxo_ref, t_ref, x_ref, f_ref, g_ref, gam_ref, dres_ref, df_ref, dgam_ref, dbet_ref, dg_ref, dbias_ref,
             loss_ref, sq_ref) = refs
            err = dxo_ref[...] - t_ref[...]
            dxov = err * (1.0 / d)
            _acc_rows(sq_ref, err * err, pl.program_id(0) == 0)

            @pl.when(pl.program_id(0) == n - 1)
            def _():
                tot = jnp.sum(sq_ref[...], axis=1, keepdims=True) * (0.5 / d)
                loss_ref[...] = jnp.broadcast_to(tot, (1, 128))

        first = pl.program_id(0) == 0
        dres, dfv, t_gam, t_g = _resid_bwd_tile(dxov, x_ref[...], f_ref[...], g_ref[...], gam_ref[...])
        dres_ref[...] = dres
        df_ref[...] = dfv.astype(BF16)
        _acc_rows(dgam_ref, t_gam, first)
        _acc_rows(dbet_ref, dxov, first)
        _acc_rows(dg_ref, t_g, first)
        _acc_rows(dbias_ref, dfv, first)

    row = pl.BlockSpec((tm, d), lambda i: (i, 0))
    vec = pl.BlockSpec((1, d), lambda i: (0, 0))
    vs = jax.ShapeDtypeStruct((1, d), F32)
    out_specs = [row, row, vec, vec, vec, vec]
    out_shape = [jax.ShapeDtypeStruct((s, d), F32), jax.ShapeDtypeStruct((s, d), BF16), vs, vs, vs, vs]
    if tgt is None:
        return pl.pallas_call(body, grid=(n,), in_specs=[row, row, row, vec, vec], out_specs=out_specs,
                              out_shape=out_shape, name=name, compiler_params=_cparams())(dxo, x, f, g, gam)
    return pl.pallas_call(body, grid=(n,), in_specs=[row, row, row, row, vec, vec],
                          out_specs=out_specs + [pl.BlockSpec((1, 128), lambda i: (0, 0))],
                          out_shape=out_shape + [jax.ShapeDtypeStruct((1, 128), F32)],
                          scratch_shapes=[pltpu.VMEM((1, d), F32)], name=name,
                          compiler_params=_cparams())(dxo, tgt, x, f, g, gam)


POOL_HALO = 16
POOL_ROWS = 256


def _pool_counts(r0, rows):
    t1 = (lax.broadcasted_iota(jnp.int32, (rows, 128), 0) + r0 + 1).astype(F32)
    low = lax.broadcasted_iota(jnp.int32, (rows, 128), 1) < POOL_GROUP
    wa = jnp.where(low, float(POOL_WINDOWS[0]), float(POOL_WINDOWS[1]))
    wb = jnp.where(low, float(POOL_WINDOWS[2]), float(POOL_WINDOWS[3]))
    return jnp.minimum(t1, wa), jnp.minimum(t1, wb), low


def _window_sums(win, off, rows, sign):
    def sl(j, half):
        return win[off + sign * j: off + sign * j + rows, 128 * half:128 * half + 128]
    a2 = sl(0, 0) + sl(1, 0)
    a4 = a2 + sl(2, 0) + sl(3, 0)
    a8 = sl(0, 1)
    for j in range(1, 8):
        a8 = a8 + sl(j, 1)
    a16 = a8
    for j in range(8, 16):
        a16 = a16 + sl(j, 1)
    return a2, a4, a8, a16


def _pool_fwd(zp, wp_bd, pscale, name):
    s = zp.shape[0]
    r = min(POOL_ROWS, s)

    def body(z_ref, wp_ref, sc_ref, p_ref, feat_ref, pad):
        pad[0:POOL_HALO, :] = jnp.zeros((POOL_HALO, D_POOL), F32)
        pad[POOL_HALO:, :] = z_ref[...]

        def step(i, carry):
            r0 = pl.multiple_of(i * r, r)
            win = pad[pl.ds(r0, r + POOL_HALO), :]
            a2, a4, a8, a16 = _window_sums(win, POOL_HALO, r, -1)
            ca, cb, low = _pool_counts(r0, r)
            x0 = win[POOL_HALO:, :]
            pa = jnp.where(low, a2, a4) / ca
            pb = jnp.where(low, a8, a16) / cb
            p = (jnp.concatenate([pa, pb], axis=1) - x0).astype(BF16)
            p_ref[pl.ds(r0, r), :] = p
            pw = jnp.dot(p, wp_ref[...], preferred_element_type=F32)
            feat_ref[pl.ds(r0, r), :] = (pw * sc_ref[...]).astype(BF16)
            return carry

        lax.fori_loop(0, s // r, step, 0)

    return pl.pallas_call(
        body, out_shape=[jax.ShapeDtypeStruct((s, D_POOL), BF16), jax.ShapeDtypeStruct((s, D_POOL), BF16)],
        scratch_shapes=[pltpu.VMEM((s + POOL_HALO, D_POOL), F32)], name=name, compiler_params=_cparams(),
    )(zp, wp_bd, pscale)


def _pool_bwd(dfeat, p, wp_bd, pscale, name):
    s = p.shape[0]
    r = min(POOL_ROWS, s)

    def body(df_ref, p_ref, wp_ref, sc_ref, dz_ref, dwp_ref, dsc_ref, gpad, dpbuf):
        dwp_ref[...] = jnp.zeros_like(dwp_ref)
        dsc_ref[...] = jnp.zeros_like(dsc_ref)
        gpad[s:, :] = jnp.zeros((POOL_HALO, D_POOL), F32)

        def step1(i, carry):
            r0 = pl.multiple_of(i * r, r)
            pv = p_ref[pl.ds(r0, r), :]
            dfv = df_ref[pl.ds(r0, r), :]
            pw = jnp.dot(pv, wp_ref[...], preferred_element_type=F32)
            dsc_ref[...] += jnp.sum(dfv * pw, axis=0, keepdims=True)
            dpw = (dfv * sc_ref[...]).astype(BF16)
            dwp_ref[...] += lax.dot_general(pv, dpw, _DIMS["tn"], preferred_element_type=F32)
            dp = lax.dot_general(dpw, wp_ref[...], _DIMS["nt"], preferred_element_type=F32)
            ca, cb, _ = _pool_counts(r0, r)
            gpad[pl.ds(r0, r), :] = dp / jnp.concatenate([ca, cb], axis=1)
            dpbuf[pl.ds(r0, r), :] = dp
            return carry

        lax.fori_loop(0, s // r, step1, 0)

        def step2(i, carry):
            r0 = pl.multiple_of(i * r, r)
            win = gpad[pl.ds(r0, r + POOL_HALO), :]
            a2, a4, a8, a16 = _window_sums(win, 0, r, 1)
            low = lax.broadcasted_iota(jnp.int32, (r, 128), 1) < POOL_GROUP
            acc = jnp.concatenate([jnp.where(low, a2, a4), jnp.where(low, a8, a16)], axis=1)
            dz_ref[pl.ds(r0, r), :] = (acc - dpbuf[pl.ds(r0, r), :]).astype(BF16)
            return carry

        lax.fori_loop(0, s // r, step2, 0)

    return pl.pallas_call(
        body,
        out_shape=[jax.ShapeDtypeStruct((s, D_POOL), BF16), jax.ShapeDtypeStruct((D_POOL, D_POOL), F32),
                   jax.ShapeDtypeStruct((1, D_POOL), F32)],
        scratch_shapes=[pltpu.VMEM((s + POOL_HALO, D_POOL), F32), pltpu.VMEM((s, D_POOL), F32)],
        name=name, compiler_params=_cparams(),
    )(dfeat, p, wp_bd, pscale)


def _skew_index():
    cp = lax.broadcasted_iota(jnp.int32, (SKEW_W, N_REL), 0)
    dist = jnp.where(cp < KW, KPAD - cp, KPAD + SKEW_W - cp)
    idx = jnp.clip(dist, -REL_CLIP, REL_CLIP) + REL_CLIP
    return (idx == lax.broadcasted_iota(jnp.int32, (SKEW_W, N_REL), 1)).astype(F32)


def _row_bits(b):
    return (lax.broadcasted_iota(jnp.int32, (QB, SKEW_W), 0) >> b) & 1 == 1


N_EDGE = KPAD // QB


def _bias_block(rel_bias, name):
    def body(rb_ref, o_ref):
        onehot = _skew_index()
        row0 = lax.dot_general(rb_ref[...], onehot, _DIMS["nt"], precision=lax.Precision.HIGHEST,
                               preferred_element_type=F32)
        r = lax.broadcasted_iota(jnp.int32, (QB, KW), 0)
        kk = lax.broadcasted_iota(jnp.int32, (QB, KW), 1)
        cq, ck = r // CHUNK, kk // CHUNK
        band = (ck >= cq) & (ck <= cq + N_PREV_CHUNKS)
        for h in range(N_HEADS):
            t = jnp.broadcast_to(row0[h:h + 1, :], (QB, SKEW_W))
            for b in range(7):
                t = jnp.where(_row_bits(b), pltpu.roll(t, 1 << b, 1), t)
            for e in range(N_EDGE + 1):
                o_ref[e, h] = jnp.where(band & (kk >= KPAD - e * QB), t[:, :KW], NEG_INF)

    return pl.pallas_call(body, out_shape=jax.ShapeDtypeStruct((N_EDGE + 1, N_HEADS, QB, KW), F32), name=name,
                          compiler_params=_cparams())(rel_bias)


def _bias_spec():
    return pl.BlockSpec((None, N_HEADS, QB, KW), lambda i: (jnp.minimum(i, N_EDGE), 0, 0, 0))


def _bias_block_bwd(ds_acc, name):
    def body(ds_ref, o_ref):
        sums = []
        for h in range(N_HEADS):
            t = jnp.concatenate([ds_ref[h], jnp.zeros((QB, SKEW_W - KW), F32)], axis=1)
            for b in range(7):
                t = jnp.where(_row_bits(b), pltpu.roll(t, SKEW_W - (1 << b), 1), t)
            sums.append(jnp.sum(t, axis=0, keepdims=True))
        allh = jnp.concatenate(sums, axis=0)
        o_ref[...] = jnp.dot(allh, _skew_index(), precision=lax.Precision.HIGHEST, preferred_element_type=F32)

    return pl.pallas_call(body, out_shape=jax.ShapeDtypeStruct((N_HEADS, N_REL), F32), name=name,
                          compiler_params=_cparams())(ds_acc)


def _scaled(q):
    return (q.astype(F32) * (HEAD_DIM ** -0.5)).astype(BF16)


def _probs(q, kw, bias_ref):
    sc = jnp.stack([lax.dot_general(q[:, HEAD_DIM * h:HEAD_DIM * (h + 1)], kw[:, HEAD_DIM * h:HEAD_DIM * (h + 1)],
                                    _DIMS["nt"], preferred_element_type=F32) + bias_ref[h] for h in range(N_HEADS)])
    e = jnp.exp(sc - jnp.max(sc, axis=-1, keepdims=True))
    return e * (1.0 / jnp.sum(e, axis=-1, keepdims=True))


def _load_padded_kv(qkv_hbm, kpad, vpad, sems, s):
    kpad[0:KPAD, :] = jnp.zeros((KPAD, D_ATTN), BF16)
    vpad[0:KPAD, :] = jnp.zeros((KPAD, D_ATTN), BF16)
    ck = pltpu.make_async_copy(qkv_hbm.at[:, D_ATTN:2 * D_ATTN], kpad.at[pl.ds(KPAD, s), :], sems.at[0])
    cv = pltpu.make_async_copy(qkv_hbm.at[:, 2 * D_ATTN:3 * D_ATTN], vpad.at[pl.ds(KPAD, s), :], sems.at[1])
    ck.start()
    cv.start()
    ck.wait()
    cv.wait()


def _attn_fwd(qkv, bias, name, rider=None):
    s = qkv.shape[0]

    def body(q_ref, qkv_hbm, bias_ref, o_ref, p_ref, kpad, vpad, sems):
        i = pl.program_id(0)

        @pl.when(i == 0)
        def _():
            _load_padded_kv(qkv_hbm, kpad, vpad, sems, s)

        base = pl.multiple_of(i * QB, QB)
        kw = kpad[pl.ds(base, KW), :]
        vw = vpad[pl.ds(base, KW), :]
        q = _scaled(q_ref[...])
        p = _probs(q, kw, bias_ref).astype(BF16)
        p_ref[...] = p
        outs = [jnp.dot(p[h], vw[:, HEAD_DIM * h:HEAD_DIM * (h + 1)], preferred_element_type=F32)
                for h in range(N_HEADS)]
        o_ref[...] = jnp.concatenate(outs, axis=1).astype(BF16)

    res = _call(
        body, name=name, grid=(s // QB,),
        in_specs=[pl.BlockSpec((QB, D_ATTN), lambda i: (i, 0)), pl.BlockSpec(memory_space=pl.ANY),
                  _bias_spec()],
        out_specs=[pl.BlockSpec((QB, D_ATTN), lambda i: (i, 0)), _probs_spec()],
        out_shape=[jax.ShapeDtypeStruct((s, D_ATTN), BF16), jax.ShapeDtypeStruct((N_HEADS, s, KW), BF16)],
        scratch_shapes=[pltpu.VMEM((s + KPAD, D_ATTN), BF16), pltpu.VMEM((s + KPAD, D_ATTN), BF16),
                        pltpu.SemaphoreType.DMA((2,))],
        args=(qkv, qkv, bias), rider=rider)
    return tuple(res) if rider is None else (tuple(res[0]), res[1])


def _probs_spec():
    return pl.BlockSpec((N_HEADS, QB, KW), lambda i: (0, i, 0))


def _attn_bwd(qkv, do, probs, name, rider=None):
    s = qkv.shape[0]
    n = s // QB

    def body(q_ref, qkv_hbm, do_ref, p_ref, dq_ref, dk_hbm, dv_hbm, ds_ref, kpad, vpad, dkacc, dvacc, sems):
        i = pl.program_id(0)

        @pl.when(i == 0)
        def _():
            _load_padded_kv(qkv_hbm, kpad, vpad, sems, s)
            dkacc[...] = jnp.zeros_like(dkacc)
            dvacc[...] = jnp.zeros_like(dvacc)
            ds_ref[...] = jnp.zeros_like(ds_ref)

        base = pl.multiple_of(i * QB, QB)
        kw = kpad[pl.ds(base, KW), :]
        vw = vpad[pl.ds(base, KW), :]
        q = _scaled(q_ref[...])
        dov = do_ref[...]
        heads = [slice(HEAD_DIM * h, HEAD_DIM * (h + 1)) for h in range(N_HEADS)]
        pb = p_ref[...]
        p = pb.astype(F32)
        dp = jnp.stack([lax.dot_general(dov[:, hs], vw[:, hs], _DIMS["nt"], preferred_element_type=F32) for hs in heads])
        ds = p * (dp - jnp.sum(dp * p, axis=-1, keepdims=True))
        ds_ref[...] += ds
        dsb = ds.astype(BF16)
        dvs = [lax.dot_general(pb[h], dov[:, hs], _DIMS["tn"], preferred_element_type=F32) for h, hs in enumerate(heads)]
        dqs = [jnp.dot(dsb[h], kw[:, hs], preferred_element_type=F32) for h, hs in enumerate(heads)]
        dks = [lax.dot_general(dsb[h], q[:, hs], _DIMS["tn"], preferred_element_type=F32) for h, hs in enumerate(heads)]
        dq_ref[...] = (jnp.concatenate(dqs, axis=1) * (HEAD_DIM ** -0.5)).astype(BF16)
        dkacc[pl.ds(base, KW), :] += jnp.concatenate(dks, axis=1)
        dvacc[pl.ds(base, KW), :] += jnp.concatenate(dvs, axis=1)

        @pl.when(i == n - 1)
        def _():
            def cast(j, carry):
                rows = pl.ds(pl.multiple_of(KPAD + j * 512, 512), 512)
                kpad[rows, :] = dkacc[rows, :].astype(BF16)
                vpad[rows, :] = dvacc[rows, :].astype(BF16)
                return carry

            lax.fori_loop(0, s // 512, cast, 0)
            ck = pltpu.make_async_copy(kpad.at[pl.ds(KPAD, s), :], dk_hbm, sems.at[0])
            cv = pltpu.make_async_copy(vpad.at[pl.ds(KPAD, s), :], dv_hbm, sems.at[1])
            ck.start()
            cv.start()
            ck.wait()
            cv.wait()

    blk = pl.BlockSpec((QB, D_ATTN), lambda i: (i, 0))
    acc_shape = jax.ShapeDtypeStruct((s, D_ATTN), BF16)
    return _call(
        body, name=name, grid=(n,),
        in_specs=[blk, pl.BlockSpec(memory_space=pl.ANY), blk, _probs_spec()],
        out_specs=[blk, pl.BlockSpec(memory_space=pl.ANY), pl.BlockSpec(memory_space=pl.ANY), _full((N_HEADS, QB, KW))],
        out_shape=[jax.ShapeDtypeStruct((s, D_ATTN), BF16), acc_shape, acc_shape,
                   jax.ShapeDtypeStruct((N_HEADS, QB, KW), F32)],
        scratch_shapes=[pltpu.VMEM((s + KPAD, D_ATTN), BF16), pltpu.VMEM((s + KPAD, D_ATTN), BF16),
                        pltpu.VMEM((s + KPAD, D_ATTN), F32), pltpu.VMEM((s + KPAD, D_ATTN), F32),
                        pltpu.SemaphoreType.DMA((2,))],
        args=(qkv, qkv, do, probs), rider=rider)


CONV_HALO = 32
CONV_ROWS = 64


def _sigmoid(t):
    return 1.0 / (1.0 + jnp.exp(-t))


CONV_WIN = CONV_ROWS + CONV_HALO - 8


def _row_windows(ref, r0, buf):
    win = ref[pl.ds(r0, CONV_ROWS + CONV_HALO), :]
    for j in range(1, 8):
        buf[j - 1] = win[j:j + CONV_WIN, :]

    def get(o):
        j, a = o % 8, o - o % 8
        if j == 0:
            return ref[pl.ds(r0 + a, CONV_ROWS), :]
        return buf[j - 1, a:a + CONV_ROWS, :]

    return get


def _glu_rows(z_ref, r0, rows):
    a = z_ref[pl.ds(r0, rows), 0:D_CONV]
    b = z_ref[pl.ds(r0, rows), D_CONV:2 * D_CONV]
    return a, _sigmoid(b)


def _conv_fwd(zc, conv_w, conv_b, ln_g, ln_b, name):
    s = zc.shape[0]
    rt = min(256, s)

    def body(z_ref, w_ref, cb_ref, g_ref, b_ref, cv_ref, feat_ref, hpad, shifts):
        hpad[0:CONV_HALO, :] = jnp.zeros((CONV_HALO, D_CONV), F32)

        def glu(i, carry):
            r0 = pl.multiple_of(i * rt, rt)
            a, sb = _glu_rows(z_ref, r0, rt)
            hpad[pl.ds(r0 + CONV_HALO, rt), :] = a * sb
            return carry

        lax.fori_loop(0, s // rt, glu, 0)
        w = w_ref[...]

        def conv(i, carry):
            r0 = pl.multiple_of(i * CONV_ROWS, CONV_ROWS)
            win = _row_windows(hpad, r0, shifts)
            acc = jnp.broadcast_to(cb_ref[...], (CONV_ROWS, D_CONV))
            for k in range(CONV_WIDTH):
                acc = acc + win(2 + k) * w[k:k + 1, :]
            cv_ref[pl.ds(r0, CONV_ROWS), :] = acc
            yhat, _ = _ln_hat(acc)
            y = yhat * g_ref[...] + b_ref[...]
            feat_ref[pl.ds(r0, CONV_ROWS), :] = (y * _sigmoid(y)).astype(BF16)
            return carry

        lax.fori_loop(0, s // CONV_ROWS, conv, 0)

    return pl.pallas_call(
        body, out_shape=[jax.ShapeDtypeStruct((s, D_CONV), F32), jax.ShapeDtypeStruct((s, D_CONV), BF16)],
        scratch_shapes=[pltpu.VMEM((s + CONV_HALO, D_CONV), F32), pltpu.VMEM((7, CONV_WIN, D_CONV), F32)],
        name=name, compiler_params=_cparams(),
    )(zc, conv_w, conv_b, ln_g, ln_b)


def _conv_bwd(dfeat, cv, zc, conv_w, ln_g, ln_b, name):
    s = zc.shape[0]
    rt = min(256, s)

    def body(df_ref, cv_ref, z_ref, w_ref, g_ref, b_ref, dz_ref, dw_ref, dcb_ref, dg_ref, db_ref, hpad, dcvpad, dwacc,
             hshifts, dshifts):
        hpad[0:CONV_HALO, :] = jnp.zeros((CONV_HALO, D_CONV), F32)
        dcvpad[s:, :] = jnp.zeros((CONV_HALO, D_CONV), F32)
        dwacc[...] = jnp.zeros_like(dwacc)
        dcb_ref[...] = jnp.zeros_like(dcb_ref)
        dg_ref[...] = jnp.zeros_like(dg_ref)
        db_ref[...] = jnp.zeros_like(db_ref)

        def pass1(i, carry):
            r0 = pl.multiple_of(i * rt, rt)
            a, sb = _glu_rows(z_ref, r0, rt)
            hpad[pl.ds(r0 + CONV_HALO, rt), :] = a * sb
            cvhat, rstd = _ln_hat(cv_ref[pl.ds(r0, rt), :])
            y = cvhat * g_ref[...] + b_ref[...]
            sg = _sigmoid(y)
            dy = df_ref[pl.ds(r0, rt), :] * (sg * (1.0 + y * (1.0 - sg)))
            dg_ref[...] += jnp.sum(dy * cvhat, axis=0, keepdims=True)
            db_ref[...] += jnp.sum(dy, axis=0, keepdims=True)
            dcv = _ln_hat_bwd(dy * g_ref[...], cvhat, rstd)
            dcb_ref[...] += jnp.sum(dcv, axis=0, keepdims=True)
            dcvpad[pl.ds(r0, rt), :] = dcv
            return carry

        lax.fori_loop(0, s // rt, pass1, 0)
        w = w_ref[...]

        def pass2(i, carry):
            r0 = pl.multiple_of(i * CONV_ROWS, CONV_ROWS)
            dwin = _row_windows(dcvpad, r0, dshifts)
            hwin = _row_windows(hpad, r0, hshifts)
            dcv = dwin(0)
            dh = jnp.zeros((CONV_ROWS, D_CONV), F32)
            for k in range(CONV_WIDTH):
                dh = dh + dwin(30 - k) * w[k:k + 1, :]
                prod = dcv * hwin(2 + k)
                dwacc[8 * k:8 * k + 8, :] += jnp.sum(prod.reshape(CONV_ROWS // 8, 8, D_CONV), axis=0)
            a, sb = _glu_rows(z_ref, r0, CONV_ROWS)
            dz_ref[pl.ds(r0, CONV_ROWS), :] = jnp.concatenate([dh * sb, dh * a * sb * (1.0 - sb)], axis=1).astype(BF16)
            return carry

        lax.fori_loop(0, s // CONV_ROWS, pass2, 0)
        dw_ref[...] = jnp.sum(dwacc[...].reshape(32, 8, D_CONV), axis=1)

    vs = jax.ShapeDtypeStruct((1, D_CONV), F32)
    return pl.pallas_call(
        body,
        out_shape=[jax.ShapeDtypeStruct((s, 2 * D_CONV), BF16), jax.ShapeDtypeStruct((32, D_CONV), F32), vs, vs, vs],
        scratch_shapes=[pltpu.VMEM((s + CONV_HALO, D_CONV), F32), pltpu.VMEM((s + CONV_HALO, D_CONV), F32),
                        pltpu.VMEM((256, D_CONV), F32), pltpu.VMEM((7, CONV_WIN, D_CONV), F32),
                        pltpu.VMEM((7, CONV_WIN, D_CONV), F32)],
        name=name, compiler_params=_cparams(),
    )(dfeat, cv, zc, conv_w, ln_g, ln_b)


def _merge(zg, b_gate, ys, name):
    s = zg.shape[0]
    tm = _row_tile(s)

    def body(zg_ref, bg_ref, y0_ref, y1_ref, y2_ref, o_ref):
        acc = None
        for j, y_ref in enumerate((y0_ref, y1_ref, y2_ref)):
            cs = slice(D_MODEL * j, D_MODEL * (j + 1))
            t = _sigmoid(zg_ref[:, cs] + bg_ref[:, cs]) * y_ref[...]
            acc = t if acc is None else acc + t
        o_ref[...] = acc.astype(BF16)

    row = pl.BlockSpec((tm, D_MODEL), lambda i: (i, 0))
    return pl.pallas_call(
        body, grid=(s // tm,),
        in_specs=[pl.BlockSpec((tm, 3 * D_MODEL), lambda i: (i, 0)), _full((1, 3 * D_MODEL)), row, row, row],
        out_specs=row, out_shape=jax.ShapeDtypeStruct((s, D_MODEL), BF16), name=name, compiler_params=_cparams(),
    )(zg, b_gate, *ys)


def _merge_bwd(dm, zg, b_gate, ys, name):
    s = zg.shape[0]
    tm = min(256, s)

    def body(dm_ref, zg_ref, bg_ref, y0_ref, y1_ref, y2_ref, d0_ref, d1_ref, d2_ref, dzg_ref, dbg_ref):
        first = pl.program_id(0) == 0

        @pl.when(first)
        def _():
            dbg_ref[...] = jnp.zeros_like(dbg_ref)

        dmv = dm_ref[...]
        for j, (y_ref, d_ref) in enumerate(((y0_ref, d0_ref), (y1_ref, d1_ref), (y2_ref, d2_ref))):
            cs = slice(D_MODEL * j, D_MODEL * (j + 1))
            g = _sigmoid(zg_ref[:, cs] + bg_ref[:, cs])
            d_ref[...] = (dmv * g).astype(BF16)
            dzg = dmv * y_ref[...] * g * (1.0 - g)
            dzg_ref[:, cs] = dzg.astype(BF16)
            dbg_ref[:, cs] += jnp.sum(dzg, axis=0, keepdims=True)

    row = pl.BlockSpec((tm, D_MODEL), lambda i: (i, 0))
    wide = pl.BlockSpec((tm, 3 * D_MODEL), lambda i: (i, 0))
    yb = jax.ShapeDtypeStruct((s, D_MODEL), BF16)
    return pl.pallas_call(
        body, grid=(s // tm,),
        in_specs=[row, wide, _full((1, 3 * D_MODEL)), row, row, row],
        out_specs=[row, row, row, wide, _full((1, 3 * D_MODEL))],
        out_shape=[yb, yb, yb, jax.ShapeDtypeStruct((s, 3 * D_MODEL), BF16), jax.ShapeDtypeStruct((1, 3 * D_MODEL), F32)],
        name=name, compiler_params=_cparams(),
    )(dm, zg, b_gate, *ys)


def _ff_hidden(u2, w_ff1t, b_ff1, name, rider=None):
    s = u2.shape[0]
    tm, tn = min(2048, s), 1024

    def body(a_ref, b_ref, bias_ref, pre_ref, h_ref):
        acc = lax.dot_general(a_ref[...], b_ref[...], _DIMS["nt"], preferred_element_type=F32) + bias_ref[...]
        pre_ref[...] = acc.astype(BF16)
        h_ref[...] = _relu2(acc).astype(BF16)

    blk = pl.BlockSpec((tm, tn), lambda i, j: (i, j))
    sh = jax.ShapeDtypeStruct((s, D_FF), BF16)
    res = _call(body, name=name, grid=(s // tm, D_FF // tn),
                in_specs=[pl.BlockSpec((tm, D_MODEL), lambda i, j: (i, 0)), pl.BlockSpec((tn, D_MODEL), lambda i, j: (j, 0)),
                          pl.BlockSpec((1, tn), lambda i, j: (0, j))],
                out_specs=[blk, blk], out_shape=[sh, sh], scratch_shapes=[], args=(u2, w_ff1t, b_ff1), rider=rider)
    return tuple(res) if rider is None else (tuple(res[0]), res[1])


def _ff_hidden_bwd(dff, w_ff2, hpre, name, rider=None):
    s = dff.shape[0]
    tm, tn = min(1024, s), 1024

    def body(a_ref, b_ref, h_ref, o_ref, sum_ref):
        dh = lax.dot_general(a_ref[...], b_ref[...], _DIMS["nt"], preferred_element_type=F32)
        dpre = dh * (2.0 * jnp.maximum(h_ref[...].astype(F32), 0.0))
        o_ref[...] = dpre.astype(BF16)
        _acc_rows(sum_ref, dpre, pl.program_id(1) == 0)

    res = _call(
        body, name=name, grid=(D_FF // tn, s // tm),
        in_specs=[pl.BlockSpec((tm, D_MODEL), lambda j, i: (i, 0)), pl.BlockSpec((tn, D_MODEL), lambda j, i: (j, 0)),
                  pl.BlockSpec((tm, tn), lambda j, i: (i, j))],
        out_specs=[pl.BlockSpec((tm, tn), lambda j, i: (i, j)), pl.BlockSpec((1, tn), lambda j, i: (0, j))],
        out_shape=[jax.ShapeDtypeStruct((s, D_FF), BF16), jax.ShapeDtypeStruct((1, D_FF), F32)],
        scratch_shapes=[], args=(dff, w_ff2, hpre), rider=rider)
    return tuple(res) if rider is None else (tuple(res[0]), res[1])


def _silu(t):
    return t * _sigmoid(t)


def _mod_fwd(c_all, w_ada_sh, b_ada_sh, name):
    cols = w_ada_sh.shape[2]

    def body(c_ref, w_ref, b_ref, o_ref):
        ca = _silu(c_ref[...]).astype(BF16)
        o_ref[0] = jnp.dot(ca, w_ref[0].astype(BF16), preferred_element_type=F32) + b_ref[0]

    return pl.pallas_call(
        body, grid=(DEPTH,),
        in_specs=[_full((N_DEV, D_MODEL)), pl.BlockSpec((1, D_MODEL, cols), lambda l: (l, 0, 0)),
                  pl.BlockSpec((1, 1, cols), lambda l: (l, 0, 0))],
        out_specs=pl.BlockSpec((1, N_DEV, cols), lambda l: (l, 0, 0)),
        out_shape=jax.ShapeDtypeStruct((DEPTH, N_DEV, cols), F32), name=name, compiler_params=_cparams(),
    )(c_all, w_ada_sh, b_ada_sh)


def _mod_bwd(c_all, dmod_sh, name):
    cols = dmod_sh.shape[2]

    def body(c_ref, d_ref, o_ref):
        ca = _silu(c_ref[...])
        o_ref[0] = lax.dot_general(ca, d_ref[0], _DIMS["tn"], precision=lax.Precision.HIGHEST,
                                   preferred_element_type=F32)

    return pl.pallas_call(
        body, grid=(DEPTH,),
        in_specs=[_full((N_DEV, D_MODEL)), pl.BlockSpec((1, N_DEV, cols), lambda l: (l, 0, 0))],
        out_specs=pl.BlockSpec((1, D_MODEL, cols), lambda l: (l, 0, 0)),
        out_shape=jax.ShapeDtypeStruct((DEPTH, D_MODEL, cols), F32), name=name, compiler_params=_cparams(),
    )(c_all, dmod_sh)


def _flat_tiles(rows, cols, itemsize_total):
    budget = 12 * 1024 * 1024
    tr = rows
    while tr % 32 == 0 and tr * cols * itemsize_total > budget:
        tr //= 2
    return tr


def _sum_cores(dw, recv, place, name):
    _, m, n = dw.shape
    tr = _flat_tiles(m, n, 6)

    def body(place_ref, a_ref, b_ref, o_ref):
        o_ref[...] = (a_ref[...].astype(F32) + b_ref[...].astype(F32)).astype(BF16)

    grid_spec = pltpu.PrefetchScalarGridSpec(
        num_scalar_prefetch=1, grid=(m // tr,),
        in_specs=[pl.BlockSpec((None, tr, n), lambda i, pr: (pr[0], i, 0)), pl.BlockSpec((tr, n), lambda i, pr: (i, 0))],
        out_specs=pl.BlockSpec((tr, n), lambda i, pr: (i, 0)))
    return pl.pallas_call(body, grid_spec=grid_spec, out_shape=jax.ShapeDtypeStruct((m, n), BF16), name=name,
                          compiler_params=_cparams())(place, dw, recv)


def _sum_chips(h, r, place, name):
    _, rs, n = h.shape
    tr = _flat_tiles(rs, n, 12)

    def body(place_ref, h_ref, r_ref, o_ref):
        o_ref[...] = ((h_ref[...].astype(F32) + r_ref[0].astype(F32)) + r_ref[1].astype(F32)) + r_ref[2].astype(F32)

    grid_spec = pltpu.PrefetchScalarGridSpec(
        num_scalar_prefetch=1, grid=(rs // tr,),
        in_specs=[pl.BlockSpec((None, tr, n), lambda i, pr: (pr[1], i, 0)), pl.BlockSpec((3, tr, n), lambda i, pr: (0, i, 0))],
        out_specs=pl.BlockSpec((tr, n), lambda i, pr: (i, 0)))
    return pl.pallas_call(body, grid_spec=grid_spec, out_shape=jax.ShapeDtypeStruct((rs, n), F32), name=name,
                          compiler_params=_cparams())(place, h, r)


def _adam_math(w, g, m, v):
    m2 = ADAM_B1 * m + (1.0 - ADAM_B1) * g
    v2 = ADAM_B2 * v + (1.0 - ADAM_B2) * (g * g)
    m_hat = m2 / (1.0 - ADAM_B1 ** ADAM_STEP)
    v_hat = v2 / (1.0 - ADAM_B2 ** ADAM_STEP)
    delta = -ADAM_LR * (m_hat / (jnp.sqrt(v_hat) + ADAM_EPS) + ADAM_WD * w)
    return delta, m2, v2


def _adamw(w, m, v, grads, name):
    r, c = w.shape
    tr = _flat_tiles(r, c, 4 * (7 + len(grads)))

    def body(*refs):
        w_ref, m_ref, v_ref = refs[:3]
        g_refs = refs[3:3 + len(grads)]
        g_ref, d_ref, m2_ref, v2_ref = refs[3 + len(grads):]
        g = g_refs[0][...]
        for gr in g_refs[1:]:
            g = g + gr[...]
        delta, m2, v2 = _adam_math(w_ref[...], g, m_ref[...], v_ref[...])
        g_ref[...] = g
        d_ref[...] = delta
        m2_ref[...] = m2
        v2_ref[...] = v2

    blk = pl.BlockSpec((tr, c), lambda i: (i, 0))
    sh = jax.ShapeDtypeStruct((r, c), F32)
    return pl.pallas_call(body, grid=(r // tr,), in_specs=[blk] * (3 + len(grads)), out_specs=[blk] * 4,
                          out_shape=[sh] * 4, name=name, compiler_params=_cparams())(w, m, v, *grads)


def _adamw_halves(w, m, v, own, other, place, split, name):
    nl, r, c = w.shape
    hr, hc = own[0].shape
    tr = _flat_tiles(hr, hc, 4 * (7 + 2 * nl))
    nt = hr // tr
    if split == "rows":
        w_spec = pl.BlockSpec((None, tr, c), lambda l, h, t, pr: (l, h * nt + t, 0))
    else:
        w_spec = pl.BlockSpec((None, tr, hc), lambda l, h, t, pr: (l, t, h))

    def g_spec(layer, mine):
        return pl.BlockSpec((tr, hc), lambda l, h, t, pr: (jnp.where((l == layer) & ((h == pr[0]) == mine), t, nt - 1), 0))

    def body(place_ref, w_ref, m_ref, v_ref, *refs):
        own_refs, other_refs = refs[:nl], refs[nl:2 * nl]
        g_ref, d_ref, m2_ref, v2_ref = refs[2 * nl:]
        layer = pl.program_id(0)
        mine = pl.program_id(1) == place_ref[0]
        g = None
        for li in range(nl):
            cand = jnp.where(mine, own_refs[li][...], other_refs[li][...])
            g = cand if g is None else jnp.where(layer == li, cand, g)
        delta, m2, v2 = _adam_math(w_ref[...], g, m_ref[...], v_ref[...])
        g_ref[...] = g
        d_ref[...] = delta
        m2_ref[...] = m2
        v2_ref[...] = v2

    sh = jax.ShapeDtypeStruct((nl, r, c), F32)
    g_specs = [g_spec(li, True) for li in range(nl)] + [g_spec(li, False) for li in range(nl)]
    return _call(body, name=name, grid=(nl, 2, nt), in_specs=[w_spec] * 3 + g_specs, out_specs=[w_spec] * 4,
                 out_shape=[sh] * 4, scratch_shapes=[], args=(w, m, v, *own, *other), prefetch=(place,))


def _adamw_small(w, m, v, g_all, name):
    r, c = w.shape

    def body(w_ref, m_ref, v_ref, g_ref, go_ref, d_ref, m2_ref, v2_ref):
        g = g_ref[0]
        for b in range(1, N_DEV):
            g = g + g_ref[b]
        delta, m2, v2 = _adam_math(w_ref[...], g, m_ref[...], v_ref[...])
        go_ref[...] = g
        d_ref[...] = delta
        m2_ref[...] = m2
        v2_ref[...] = v2

    sh = jax.ShapeDtypeStruct((r, c), F32)
    return pl.pallas_call(body, out_shape=[sh] * 4, name=name, compiler_params=_cparams())(w, m, v, g_all)


def _me():
    return lax.axis_index("x"), lax.axis_index("y"), lax.axis_index("c")


def _flip(v, bit):
    return 1 - v if bit else v


def _allgather_small(blk, name):
    r, c = blk.shape

    def body(x_ref, o_ref, send_sems, recv_sems):
        x, y, cc = _me()
        me = 4 * x + 2 * y + cc
        copies = []
        for k in range(1, N_DEV):
            peer = (_flip(x, k & 4), _flip(y, k & 2), _flip(cc, k & 1))
            cp = pltpu.make_async_remote_copy(src_ref=x_ref, dst_ref=o_ref.at[me], send_sem=send_sems.at[k - 1],
                                              recv_sem=recv_sems.at[k - 1], device_id=peer, device_id_type=MESH)
            cp.start()
            copies.append(cp)
        o_ref[me] = x_ref[...]
        for cp in copies:
            cp.wait()

    return pl.pallas_call(
        body, out_shape=jax.ShapeDtypeStruct((N_DEV, r, c), F32),
        in_specs=[pl.BlockSpec(memory_space=pltpu.VMEM)], out_specs=pl.BlockSpec(memory_space=pltpu.VMEM),
        scratch_shapes=[pltpu.SemaphoreType.DMA((N_DEV - 1,)), pltpu.SemaphoreType.DMA((N_DEV - 1,))],
        name=name, compiler_params=_cparams(),
    )(blk)


class _Rider:
    def __init__(self, arrays, out_shapes, scratch_shapes, start, finish):
        self.arrays, self.out_shapes, self.scratch_shapes = list(arrays), list(out_shapes), list(scratch_shapes)
        self.start, self.finish = start, finish


def _call(body, *, name, grid, in_specs, out_specs, out_shape, scratch_shapes, args, rider=None, prefetch=()):
    npf = len(prefetch)

    def launch(fn, in_specs, out_specs, out_shape, scratch_shapes, args):
        grid_spec = pltpu.PrefetchScalarGridSpec(num_scalar_prefetch=npf, grid=grid, in_specs=in_specs,
                                                 out_specs=out_specs, scratch_shapes=scratch_shapes)
        return pl.pallas_call(fn, grid_spec=grid_spec, out_shape=out_shape, name=name,
                              compiler_params=_cparams())(*prefetch, *args)

    if rider is None:
        return launch(body, list(in_specs), list(out_specs), list(out_shape), list(scratch_shapes), args)
    ni, no, ns = len(in_specs), len(out_specs), len(scratch_shapes)
    ri, ro = len(rider.arrays), len(rider.out_shapes)
    steps = int(np.prod(grid))

    def wrapped(*refs):
        pf, refs = refs[:npf], refs[npf:]
        h_in, r_in = refs[:ni], refs[ni:ni + ri]
        h_out, r_out = refs[ni + ri:ni + ri + no], refs[ni + ri + no:ni + ri + no + ro]
        h_scr, r_scr = refs[ni + ri + no + ro:ni + ri + no + ro + ns], refs[ni + ri + no + ro + ns:]
        step = pl.program_id(0)
        for d in range(1, len(grid)):
            step = step * grid[d] + pl.program_id(d)

        @pl.when(step == 0)
        def _():
            rider.start(r_in, r_out, r_scr)

        body(*pf, *h_in, *h_out, *h_scr)

        @pl.when(step == steps - 1)
        def _():
            rider.finish(r_in, r_out, r_scr)

    anyspec = pl.BlockSpec(memory_space=pl.ANY)
    res = launch(wrapped, list(in_specs) + [anyspec] * ri, list(out_specs) + [anyspec] * ro,
                 list(out_shape) + rider.out_shapes, list(scratch_shapes) + rider.scratch_shapes,
                 list(args) + rider.arrays)
    return res[:no], res[no:]


def _run_rider(rider, name):
    ri = len(rider.arrays)

    def body(*refs):
        r_in, r_out, r_scr = refs[:ri], refs[ri:ri + len(rider.out_shapes)], refs[ri + len(rider.out_shapes):]
        rider.start(r_in, r_out, r_scr)
        rider.finish(r_in, r_out, r_scr)

    anyspec = pl.BlockSpec(memory_space=pl.ANY)
    return pl.pallas_call(body, in_specs=[anyspec] * ri, out_specs=[anyspec] * len(rider.out_shapes),
                          out_shape=rider.out_shapes, scratch_shapes=rider.scratch_shapes, name=name,
                          compiler_params=_cparams())(*rider.arrays)


def _allgather_rider(blk):
    def copies(ins, outs, scr):
        send_sems, recv_sems, loc_sems, stage = scr
        x, y, cc = _me()
        me = 4 * x + 2 * y + cc
        remote = [pltpu.make_async_remote_copy(
            src_ref=ins[0], dst_ref=outs[0].at[me], send_sem=send_sems.at[k - 1], recv_sem=recv_sems.at[k - 1],
            device_id=(_flip(x, k & 4), _flip(y, k & 2), _flip(cc, k & 1)), device_id_type=MESH) for k in range(1, N_DEV)]
        return remote, pltpu.make_async_copy(ins[0], stage, loc_sems.at[0]), (outs[0].at[me], stage, loc_sems.at[1])

    def start(ins, outs, scr):
        remote, lin, _ = copies(ins, outs, scr)
        lin.start()
        for cp in remote:
            cp.start()

    def finish(ins, outs, scr):
        remote, lin, (dst, stage, sem) = copies(ins, outs, scr)
        lin.wait()
        lout = pltpu.make_async_copy(stage, dst, sem)
        lout.start()
        for cp in remote:
            cp.wait()
        lout.wait()

    return _Rider([blk], [jax.ShapeDtypeStruct((N_DEV,) + blk.shape, blk.dtype)],
                  [pltpu.SemaphoreType.DMA((N_DEV - 1,)), pltpu.SemaphoreType.DMA((N_DEV - 1,)),
                   pltpu.SemaphoreType.DMA((2,)), pltpu.VMEM(blk.shape, blk.dtype)], start, finish)


def _gather_rider(shards):
    n = len(shards)

    def copies(ins, outs, scr, relay=True):
        ici_send, ici_recv, d2d_send, d2d_recv, loc_sems = scr[:5]
        stage = scr[5:]
        x, y, cc = _me()
        chip = 2 * x + y
        sibling = (x, y, 1 - cc)
        local, sends, relays = [], [], []
        for j in range(n):
            def rows(ch, h, j=j):
                return outs[j].at[ch, h]

            lc = pltpu.make_async_copy(ins[j], stage[j], loc_sems.at[j])
            local.append((lc, pltpu.make_async_copy(stage[j], outs[j].at[chip], loc_sems.at[n + j]) if relay else None))
            for k in range(1, N_CHIP):
                px, py = _flip(x, k & 2), _flip(y, k & 1)
                pchip = 2 * px + py
                q = 3 * j + k - 1
                out_cp = pltpu.make_async_remote_copy(src_ref=ins[j].at[cc], dst_ref=rows(chip, cc),
                                                      send_sem=ici_send.at[q], recv_sem=ici_recv.at[q],
                                                      device_id=(px, py, cc), device_id_type=MESH)
                sends.append(out_cp)
                if not relay:
                    continue
                arrival = pltpu.make_async_remote_copy(src_ref=rows(pchip, cc), dst_ref=rows(pchip, cc),
                                                       send_sem=ici_send.at[q], recv_sem=ici_recv.at[q],
                                                       device_id=(px, py, cc), device_id_type=MESH)
                forward = pltpu.make_async_remote_copy(src_ref=rows(pchip, cc), dst_ref=rows(pchip, cc),
                                                       send_sem=d2d_send.at[q], recv_sem=d2d_recv.at[q],
                                                       device_id=sibling, device_id_type=MESH)
                from_sibling = pltpu.make_async_remote_copy(src_ref=rows(pchip, 1 - cc), dst_ref=rows(pchip, 1 - cc),
                                                            send_sem=d2d_send.at[q], recv_sem=d2d_recv.at[q],
                                                            device_id=sibling, device_id_type=MESH)
                relays.append((arrival, forward, from_sibling))
        return local, sends, relays

    def start(ins, outs, scr):
        local, sends, _ = copies(ins, outs, scr, relay=False)
        for lin, _ in local:
            lin.start()
        for cp in sends:
            cp.start()

    def finish(ins, outs, scr):
        local, sends, relays = copies(ins, outs, scr)
        for lin, lout in local:
            lin.wait()
            lout.start()
        for arrival, forward, _ in relays:
            arrival.wait_recv()
            forward.start()
        for cp in sends:
            cp.wait_send()
        for _, forward, from_sibling in relays:
            forward.wait_send()
            from_sibling.wait_recv()
        for _, lout in local:
            lout.wait()

    scratch = [pltpu.SemaphoreType.DMA((3 * n,)), pltpu.SemaphoreType.DMA((3 * n,)), pltpu.SemaphoreType.DMA((3 * n,)),
               pltpu.SemaphoreType.DMA((3 * n,)), pltpu.SemaphoreType.DMA((2 * n,))]
    scratch += [pltpu.VMEM(a.shape, a.dtype) for a in shards]
    return _Rider(shards, [jax.ShapeDtypeStruct((N_CHIP,) + a.shape, a.dtype) for a in shards], scratch, start, finish)


def _sibling_rider(arrs, other_half=False):
    n = len(arrs)

    def copies(ins, outs, scr):
        send_sems, recv_sems = scr
        x, y, cc = _me()
        return [pltpu.make_async_remote_copy(
            src_ref=ins[j].at[1 - cc] if other_half else ins[j], dst_ref=outs[j], send_sem=send_sems.at[j],
            recv_sem=recv_sems.at[j], device_id=(x, y, 1 - cc), device_id_type=MESH) for j in range(n)]

    def start(ins, outs, scr):
        for cp in copies(ins, outs, scr):
            cp.start()

    def finish(ins, outs, scr):
        for cp in copies(ins, outs, scr):
            cp.wait()

    return _Rider(arrs, [jax.ShapeDtypeStruct(a.shape[1:] if other_half else a.shape, a.dtype) for a in arrs],
                  [pltpu.SemaphoreType.DMA((n,)), pltpu.SemaphoreType.DMA((n,))], start, finish)


def _sibling_send(arrs, name, other_half=False):
    return _run_rider(_sibling_rider(arrs, other_half), name)


def _join_riders(first, second):
    ni, no, ns = len(first.arrays), len(first.out_shapes), len(first.scratch_shapes)

    def split(ins, outs, scr):
        return (ins[:ni], outs[:no], scr[:ns]), (ins[ni:], outs[no:], scr[ns:])

    def start(ins, outs, scr):
        a, b = split(ins, outs, scr)
        first.start(*a)
        second.start(*b)

    def finish(ins, outs, scr):
        a, b = split(ins, outs, scr)
        first.finish(*a)
        second.finish(*b)

    return _Rider(first.arrays + second.arrays, first.out_shapes + second.out_shapes,
                  first.scratch_shapes + second.scratch_shapes, start, finish)


def _scatter_rider(arrs):
    n = len(arrs)

    def copies(ins, outs, scr):
        send_sems, recv_sems = scr
        x, y, cc = _me()
        cps = []
        for j in range(n):
            for k in range(1, N_CHIP):
                px, py = _flip(x, k & 2), _flip(y, k & 1)
                cps.append(pltpu.make_async_remote_copy(
                    src_ref=ins[j].at[2 * px + py], dst_ref=outs[j].at[k - 1], send_sem=send_sems.at[3 * j + k - 1],
                    recv_sem=recv_sems.at[3 * j + k - 1], device_id=(px, py, cc), device_id_type=MESH))
        return cps

    def start(ins, outs, scr):
        for cp in copies(ins, outs, scr):
            cp.start()

    def finish(ins, outs, scr):
        for cp in copies(ins, outs, scr):
            cp.wait()

    return _Rider(arrs, [jax.ShapeDtypeStruct((N_CHIP - 1,) + a.shape[1:], a.dtype) for a in arrs],
                  [pltpu.SemaphoreType.DMA((3 * n,)), pltpu.SemaphoreType.DMA((3 * n,))], start, finish)


COL_SHARDED = ("w_in", "w_br_pool", "w_br_attn", "w_br_conv", "w_ff1")
ROW_SHARDED = ("w_o", "w_ff2")
BIG = COL_SHARDED + ROW_SHARDED
SMALL = ("b_ada", "b_gate", "w_pool", "pool_scale", "rel_bias", "conv_w", "conv_b", "conv_ln_g", "conv_ln_b",
         "ln_mix_g", "ln_mix_b", "b_ff1", "b_ff2", "ln_ff_g", "ln_ff_b")
PACK_W = 1024


def _pack(parts):
    rows = []
    for a in parts:
        flat = a.reshape(-1)
        n = -(-flat.shape[0] // PACK_W) * PACK_W
        rows.append(jnp.pad(flat, (0, n - flat.shape[0])).reshape(-1, PACK_W))
    out = jnp.concatenate(rows, axis=0)
    r = -(-out.shape[0] // 8) * 8
    return jnp.pad(out, ((0, r - out.shape[0]), (0, 0)))


def _unpack(packed, shapes):
    out, r0 = [], 0
    for shp in shapes:
        size = int(np.prod(shp))
        nr = -(-size // PACK_W)
        out.append(packed[r0:r0 + nr].reshape(-1)[:size].reshape(shp))
        r0 += nr
    return out


def _hosted(fn, hook, *args, **kw):
    if hook is None:
        return fn(*args, **kw)
    res, rider_out = fn(*args, rider=hook[0], **kw)
    hook[1](rider_out)
    return res


def _layer_fwd(l, x, mod, W, P, hooks=None):
    hooks = hooks or {}
    s = x.shape[0]
    sh_m, sc_m, g_m, sh_f, sc_f, g_f = [mod[l:l + 1, D_MODEL * j:D_MODEL * (j + 1)] for j in range(6)]
    n = lambda t: f"{t}{l}"
    w_in = W["w_in"][l]
    u = _ln_mod(x, sc_m, sh_m, n("ln_mod_mix"))
    zp = _mm(u, w_in, "nt", tm=s, tn=256, out_dtype=F32, name=n("z_pool"), b_col0=0, n_out=D_POOL)
    qkv = _mm(u, w_in, "nt", tm=s, tn=256, out_dtype=BF16, name=n("z_qkv"), b_col0=OFF_QKV // 256, n_out=3 * D_ATTN)
    zc = _mm(u, w_in, "nt", tm=s, tn=256, out_dtype=F32, name=n("z_conv"), b_col0=OFF_CONV // 256, n_out=2 * D_CONV)
    zg = _hosted(_mm, hooks.get("z_gate"), u, w_in, "nt", tm=min(2048, s), tn=768, out_dtype=BF16, name=n("z_gate"),
                 b_col0=OFF_GATE // 768, n_out=3 * D_MODEL)

    p, feat_pool = _pool_fwd(zp, P["wp_bd"][l], P["pool_scale"][l], n("pool_fwd"))
    bias = _bias_block(P["rel_bias"][l], n("bias_block"))
    o, probs = _hosted(_attn_fwd, hooks.get("attn"), qkv, bias, n("attn_fwd"))
    cv, feat_conv = _conv_fwd(zc, P["conv_w"][l], P["conv_b"][l], P["conv_ln_g"][l], P["conv_ln_b"][l], n("conv_fwd"))

    tmb = min(1024, s)
    y_pool = _mm(feat_pool, W["w_br_pool"][l], "nt", tm=tmb, tn=1024, out_dtype=BF16, name=n("y_pool"))
    y_attn = _mm(o, W["w_br_attn"][l], "nt", tm=tmb, tn=1024, out_dtype=BF16, name=n("y_attn"))
    y_conv = _mm(feat_conv, W["w_br_conv"][l], "nt", tm=tmb, tn=1024, out_dtype=BF16, name=n("y_conv"))
    ys = (y_pool, y_attn, y_conv)
    merged = _merge(zg, P["b_gate"][l], ys, n("merge"))
    mix, x1 = _mm_resid_ln(merged, W["w_o"][l], None, x, g_m, P["ln_mix_g"][l], P["ln_mix_b"][l], n("mix_out"))

    u2 = _ln_mod(x1, sc_f, sh_f, n("ln_mod_ff"))
    hpre, hid = _hosted(_ff_hidden, hooks.get("ff1"), u2, W["w_ff1"][l], P["b_ff1"][l], n("ff1"))
    ff, x2 = _hosted(_mm_resid_ln, hooks.get("ff2"), hid, W["w_ff2"][l], P["b_ff2"][l], x1, g_f, P["ln_ff_g"][l],
                     P["ln_ff_b"][l], n("ff2"))
    saved = dict(x=x, u=u, zp=zp, qkv=qkv, zc=zc, zg=zg, p=p, feat_pool=feat_pool, probs=probs, o=o, cv=cv,
                 feat_conv=feat_conv, ys=ys, merged=merged, mix=mix, x1=x1, u2=u2, hpre=hpre, hid=hid, ff=ff)
    return x2, saved


def _layer_bwd(l, dx2, mod, W, P, A, hooks=None, tgt=None, nxt=None):
    hooks = hooks or {}
    sh_m, sc_m, g_m, sh_f, sc_f, g_f = [mod[l:l + 1, D_MODEL * j:D_MODEL * (j + 1)] for j in range(6)]
    n = lambda t: f"{t}{l}"
    gw, gs = {}, {}

    if isinstance(dx2, tuple):
        dres, dff, gs["ln_ff_g"], gs["ln_ff_b"], dg_f, gs["b_ff2"] = dx2
    else:
        dres, dff, gs["ln_ff_g"], gs["ln_ff_b"], dg_f, gs["b_ff2"], *loss_part = _resid_ln_bwd(
            dx2, A["x1"], A["ff"], g_f, P["ln_ff_g"][l], n("resid_ln_ff_bwd"), tgt=tgt)
    s = dres.shape[0]
    tmb = min(1024, s)
    gw["w_ff2"] = _mm(A["hid"], dff, "tn", tm=512, tn=1024, out_dtype=BF16, name=n("dw_ff2"), split_n=512)
    hook = hooks["ff_hidden_bwd"](gw) if "ff_hidden_bwd" in hooks else None
    dhpre, gs["b_ff1"] = _hosted(_ff_hidden_bwd, hook, dff, W["w_ff2"][l], A["hpre"], n("ff_hidden_bwd"))
    gw["w_ff1"] = _mm(dhpre, A["u2"], "tn", tm=512, tn=1024, out_dtype=BF16, name=n("dw_ff1"), split_n=512)

    hook = hooks["du_ff"](gw) if "du_ff" in hooks else None
    dres, dmix, dsc_f, dsh_f, gs["ln_mix_g"], gs["ln_mix_b"], dg_m, _ = _hosted(
        _mm_ln_mod_bwd, hook, dhpre, W["w_ff1"][l], A["x1"], sc_f, dres, n("du_ff"),
        nxt=(A["x"], A["mix"], g_m, P["ln_mix_g"][l]))
    gw["w_o"] = _mm(A["merged"], dmix, "tn", tm=512, tn=1024, out_dtype=BF16, name=n("dw_o"), split_n=512)
    dmerged = _mm(dmix, W["w_o"][l], "nt", tm=tmb, tn=1024, out_dtype=F32, name=n("d_merged"))
    dy_pool, dy_attn, dy_conv, dzg, gs["b_gate"] = _merge_bwd(dmerged, A["zg"], P["b_gate"][l], A["ys"], n("merge_bwd"))

    gw["w_br_pool"] = _mm(dy_pool, A["feat_pool"], "tn", tm=512, tn=256, out_dtype=BF16, name=n("dw_br_pool"),
                          split_n=128)
    gw["w_br_attn"] = _mm(dy_attn, A["o"], "tn", tm=512, tn=512, out_dtype=BF16, name=n("dw_br_attn"), split_n=256)
    gw["w_br_conv"] = _mm(dy_conv, A["feat_conv"], "tn", tm=512, tn=256, out_dtype=BF16, name=n("dw_br_conv"),
                          split_n=128)
    dfeat_pool = _mm(dy_pool, W["w_br_pool"][l], "nn", tm=tmb, tn=256, out_dtype=F32, name=n("d_feat_pool"))
    do = _mm(dy_attn, W["w_br_attn"][l], "nn", tm=tmb, tn=512, out_dtype=BF16, name=n("d_attn_out"))
    dfeat_conv = _mm(dy_conv, W["w_br_conv"][l], "nn", tm=tmb, tn=256, out_dtype=F32, name=n("d_feat_conv"))

    dzp, dwp_bd, gs["pool_scale"] = _pool_bwd(dfeat_pool, A["p"], P["wp_bd"][l], P["pool_scale"][l], n("pool_bwd"))
    gs["w_pool"] = jnp.stack([dwp_bd[POOL_GROUP * g:POOL_GROUP * (g + 1), POOL_GROUP * g:POOL_GROUP * (g + 1)]
                              for g in range(len(POOL_WINDOWS))])
    hook = hooks["attn"](gw) if "attn" in hooks else None
    dq, dk, dv, ds_acc = _hosted(_attn_bwd, hook, A["qkv"], do, A["probs"], n("attn_bwd"))
    gs["rel_bias"] = _bias_block_bwd(ds_acc, n("bias_block_bwd"))
    dzc, dcw, gs["conv_b"], gs["conv_ln_g"], gs["conv_ln_b"] = _conv_bwd(
        dfeat_conv, A["cv"], A["zc"], P["conv_w"][l], P["conv_ln_g"][l], P["conv_ln_b"][l], n("conv_bwd"))
    gs["conv_w"] = dcw[:CONV_WIDTH]

    dz = [dzp, dq, dk, dv, dzc, dzg]
    gw["w_in"] = _dw_segments(dz, A["u"], n("dw_in"))
    hook = hooks["du_mix"](gw) if "du_mix" in hooks else None
    res = _hosted(_mm_ln_mod_bwd, hook, dz, W["w_in"][l], A["x"], sc_m, dres, n("du_mix"), nxt=nxt)
    if nxt is None:
        dx, dsc_m, dsh_m = res
    else:
        dx, dsc_m, dsh_m = (res[0], res[1], *res[4:]), res[2], res[3]
    dmod = jnp.concatenate([dsh_m, dsc_m, dg_m, dsh_f, dsc_f, dg_f], axis=1)
    return (dx, gw, gs, dmod) if tgt is None else (dx, gw, gs, dmod, loss_part[0])


def _small_shapes():
    return {"b_ada": (6 * D_MODEL,), "b_gate": (3 * D_MODEL,), "w_pool": (4, POOL_GROUP, POOL_GROUP),
            "pool_scale": (D_POOL,), "rel_bias": (N_HEADS, N_REL), "conv_w": (CONV_WIDTH, D_CONV),
            "conv_b": (D_CONV,), "conv_ln_g": (D_CONV,), "conv_ln_b": (D_CONV,), "ln_mix_g": (D_MODEL,),
            "ln_mix_b": (D_MODEL,), "b_ff1": (D_FF,), "b_ff2": (D_MODEL,), "ln_ff_g": (D_MODEL,), "ln_ff_b": (D_MODEL,)}


def kernel(x, c, w_ada, b_ada, w_in, b_gate, w_pool, pool_scale, rel_bias, conv_w, conv_b, conv_ln_g, conv_ln_b, w_br_pool, w_br_attn, w_br_conv, w_o, ln_mix_g, ln_mix_b, w_ff1, b_ff1, w_ff2, b_ff2, ln_ff_g, ln_ff_b, loss_target, m_w_ada, m_b_ada, m_w_in, m_b_gate, m_w_pool, m_pool_scale, m_rel_bias, m_conv_w, m_conv_b, m_conv_ln_g, m_conv_ln_b, m_w_br_pool, m_w_br_attn, m_w_br_conv, m_w_o, m_ln_mix_g, m_ln_mix_b, m_w_ff1, m_b_ff1, m_w_ff2, m_b_ff2, m_ln_ff_g, m_ln_ff_b, v_w_ada, v_b_ada, v_w_in, v_b_gate, v_w_pool, v_pool_scale, v_rel_bias, v_conv_w, v_conv_b, v_conv_ln_g, v_conv_ln_b, v_w_br_pool, v_w_br_attn, v_w_br_conv, v_w_o, v_ln_mix_g, v_ln_mix_b, v_w_ff1, v_b_ff1, v_w_ff2, v_b_ff2, v_ln_ff_g, v_ln_ff_b):
    env = dict(locals())
    xi, yi, ci = _me()
    chip = 2 * xi + yi
    me = 4 * xi + 2 * yi + ci
    xs = x[0]
    tgt = loss_target[0]
    L = DEPTH

    first = _allgather_small(jnp.concatenate([c.reshape(8, 128), _pack([conv_w]).reshape(-1, 128)]), "gather_c_conv_w")
    c_all = first[:, :8].reshape(N_DEV, D_MODEL)
    ada_cols = w_ada.shape[2]
    b_ada_sh = lax.dynamic_slice_in_dim(b_ada, chip * ada_cols, ada_cols, axis=1).reshape(L, 1, ada_cols)
    mod_part = _mod_fwd(c_all, w_ada, b_ada_sh, "mod_fwd")
    mod_g = _allgather_small(mod_part.reshape(-1, 128), "gather_mod").reshape(N_CHIP, 2, L, N_DEV, ada_cols)[:, 0]
    mod_all = jnp.transpose(mod_g, (1, 2, 0, 3)).reshape(L, N_DEV, 6 * D_MODEL)
    mod = lax.dynamic_index_in_dim(mod_all, me, axis=1, keepdims=False)

    W = {k: [None] * L for k in BIG}

    def weight_gather(names, l):
        shards = [(jnp.swapaxes(env[k][l], 0, 1) if k in COL_SHARDED else env[k][l]).astype(BF16) for k in names]
        shards = [a.reshape(2, a.shape[0] // 2, a.shape[1]) for a in shards]

        def done(outs):
            for k, g in zip(names, outs):
                W[k][l] = g.reshape(-1, g.shape[-1])

        return _gather_rider(shards), done

    branch_names = ("w_br_pool", "w_br_attn", "w_br_conv", "w_o")
    late_names = ("w_ff1", "w_ff2")
    rider, done = weight_gather(("w_in",), 0)
    done(_run_rider(rider, "gather_w_in0"))
    fwd_hooks = [{"z_gate": weight_gather(branch_names, 0), "attn": weight_gather(late_names, 0),
                  "ff1": weight_gather(("w_in",), 1), "ff2": weight_gather(branch_names, 1)},
                 {"attn": weight_gather(late_names, 1)}]

    P = {k: env[k] for k in ("rel_bias", "conv_w")}
    for k in ("b_gate", "pool_scale", "conv_b", "conv_ln_g", "conv_ln_b", "ln_mix_g", "ln_mix_b", "b_ff1", "b_ff2",
              "ln_ff_g", "ln_ff_b"):
        P[k] = env[k].reshape(L, 1, -1)
    n_cw = conv_w.size
    cw = first[:, 8:].reshape(N_CHIP, 2, -1)[:, 0, :n_cw].reshape(N_CHIP, L, CONV_WIDTH, D_CONV // N_CHIP)
    P["conv_w"] = jnp.transpose(cw, (1, 2, 0, 3)).reshape(L, CONV_WIDTH, D_CONV)
    wp_bd = jnp.zeros((L, D_POOL, D_POOL), F32)
    for g in range(len(POOL_WINDOWS)):
        sl = slice(POOL_GROUP * g, POOL_GROUP * (g + 1))
        wp_bd = wp_bd.at[:, sl, sl].set(w_pool[:, g])
    P["wp_bd"] = wp_bd.astype(BF16)

    acts = []
    h = xs
    for l in range(L):
        h, saved = _layer_fwd(l, h, mod, W, P, fwd_hooks[l])
        acts.append(saved)

    place = jnp.stack([ci, chip, chip ^ 1, chip ^ 2, chip ^ 3]).astype(jnp.int32)
    scattered = {}

    def grad_scatter(items, tag):
        dws = [dw for _, _, dw in items]
        got = _sibling_send(dws, f"swap_blocks_{tag}", other_half=True)
        both = [_sum_cores(a, b, place, f"sum_cores_{k}{l}") for (k, l, _), a, b in zip(items, dws, got)]
        both = [hh.reshape(N_CHIP, -1, hh.shape[-1]) for hh in both]

        def done(outs):
            for (k, l, _), hh, r in zip(items, both, outs):
                scattered[(k, l)] = (hh, r)

        return _scatter_rider(both), done

    def scatter_hook(names, l, host):
        return lambda gw: grad_scatter([(k, l, gw[k]) for k in names], f"{host}{l}")

    gws, gss, dmods = [None] * L, [None] * L, [None] * L
    dh = h
    for l in reversed(range(L)):
        hooks = {"ff_hidden_bwd": scatter_hook(("w_ff2",), l, "ff_hidden_bwd"),
                 "du_ff": scatter_hook(("w_ff1",), l, "du_ff"),
                 "attn": scatter_hook(("w_o", "w_br_pool", "w_br_attn", "w_br_conv"), l, "attn_bwd"),
                 "du_mix": scatter_hook(("w_in",), l, "du_mix")}
        below = None
        if l > 0:
            below = (acts[l - 1]["x1"], acts[l - 1]["ff"], mod[l - 1:l, 5 * D_MODEL:], P["ln_ff_g"][l - 1])
        if l == L - 1:
            dh, gws[l], gss[l], dmods[l], loss_part = _layer_bwd(l, dh, mod, W, P, acts[l], hooks, tgt=tgt, nxt=below)
        else:
            dh, gws[l], gss[l], dmods[l] = _layer_bwd(l, dh, mod, W, P, acts[l], hooks, nxt=below)
    grad_x = dh[None]
    loss = lax.psum(loss_part[0, 0], ("x", "y", "c"))

    reduced = [[_sum_chips(*scattered[(k, l)], place, f"sum_chips_{k}{l}") for l in range(L)] for k in BIG]
    flat_reduced = [t for per_weight in reduced for t in per_weight]

    shapes = _small_shapes()
    small_names = [k for k in SMALL if k != "b_ada"]
    dmod_own = jnp.concatenate(dmods, axis=0)
    pack = _pack([dmod_own] + [jnp.stack([gss[l][k].reshape(shapes[k]) for l in range(L)]) for k in small_names])
    last = _run_rider(_join_riders(_sibling_rider(flat_reduced), _allgather_rider(pack.reshape(-1, 128))),
                      "swap_reduced_gather_small")
    flat_other, g_all = last[:-1], last[-1].reshape(N_DEV, -1, PACK_W)

    out = {}
    for j, k in enumerate(BIG):
        own, other = reduced[j], flat_other[L * j:L * (j + 1)]
        if k == "w_in":
            t = lambda a: jnp.swapaxes(a, 1, 2)
            res = _adamw_halves(t(env[k]), t(env["m_" + k]), t(env["v_" + k]), own, other, place, "cols", f"adamw_{k}")
            res = [t(a) for a in res]
        else:
            if k in COL_SHARDED:
                own, other = [a.T for a in own], [a.T for a in other]
            res = _adamw_halves(env[k], env["m_" + k], env["v_" + k], own, other, place,
                                "rows" if k in COL_SHARDED else "cols", f"adamw_{k}")
        out[k] = tuple(res)

    dmod_all = g_all[:, :L * 6].reshape(N_DEV, L, 6 * D_MODEL)
    dmod_sh = jnp.transpose(lax.dynamic_slice_in_dim(dmod_all, chip * ada_cols, ada_cols, axis=2), (1, 0, 2))
    g_ada = _mod_bwd(c_all, dmod_sh, "mod_bwd")
    g_, d_, m_, v_ = _adamw(w_ada.reshape(-1, ada_cols), m_w_ada.reshape(-1, ada_cols), v_w_ada.reshape(-1, ada_cols),
                            [g_ada.reshape(-1, ada_cols)], "adamw_w_ada")
    out["w_ada"] = tuple(a.reshape(w_ada.shape) for a in (g_, d_, m_, v_))

    def small_pack(prefix):
        parts = [env[prefix + "b_ada"]]
        for k in small_names:
            a = env[prefix + k]
            if k == "conv_w":
                a = jnp.zeros((L,) + shapes[k], F32)
            parts.append(a)
        return _pack(parts)

    gp, dp, mp, vp = _adamw_small(small_pack(""), small_pack("m_"), small_pack("v_"), g_all, "adamw_small")
    full_shapes = [(L,) + shapes["b_ada"]] + [(L,) + shapes[k] for k in small_names]
    for tag, packed in (("g", gp), ("d", dp), ("m", mp), ("v", vp)):
        for k, a in zip(["b_ada"] + small_names, _unpack(packed, full_shapes)):
            out.setdefault(k, {})
            out[k][tag] = a
    g_cw_full = out["conv_w"]["g"]
    cw_cols = D_CONV // N_CHIP
    g_cw = lax.dynamic_slice_in_dim(g_cw_full, chip * cw_cols, cw_cols, axis=2)
    pad_rows = lambda a: jnp.pad(a.reshape(L * CONV_WIDTH, cw_cols), ((0, 2), (0, 0)))
    g_, d_, m_, v_ = _adamw(pad_rows(conv_w), pad_rows(m_conv_w), pad_rows(v_conv_w), [pad_rows(g_cw)], "adamw_conv_w")
    out["conv_w"] = tuple(a[:L * CONV_WIDTH].reshape(L, CONV_WIDTH, cw_cols) for a in (g_, d_, m_, v_))

    names = ["w_ada", "b_ada", "w_in", "b_gate", "w_pool", "pool_scale", "rel_bias", "conv_w", "conv_b", "conv_ln_g",
             "conv_ln_b", "w_br_pool", "w_br_attn", "w_br_conv", "w_o", "ln_mix_g", "ln_mix_b", "w_ff1", "b_ff1",
             "w_ff2", "b_ff2", "ln_ff_g", "ln_ff_b"]

    def pick(k, i):
        o = out[k]
        return o[i] if isinstance(o, tuple) else o["gdmv"[i]].reshape(env[k].shape)

    return (loss, grad_x, *[pick(k, 0) for k in names], *[pick(k, 1) for k in names],
            *[pick(k, 2) for k in names], *[pick(k, 3) for k in names])
```

```python
import jax
import jax.numpy as jnp
import numpy as np
from jax import lax
from jax.experimental import pallas as pl
from jax.experimental.pallas import tpu as pltpu

F32 = jnp.float32
BF16 = jnp.bfloat16

D_MODEL = 1024
DEPTH = 2
CHUNK = 64
POOL_WINDOWS = (2, 4, 8, 16)
POOL_GROUP = 64
D_POOL = 256
N_HEADS = 8
HEAD_DIM = 64
D_ATTN = 512
N_PREV_CHUNKS = 8
REL_CLIP = 128
N_REL = 2 * REL_CLIP + 1
D_CONV = 256
CONV_WIDTH = 31
D_FF = 4 * D_MODEL
D_IN = 5376
OFF_POOL, OFF_QKV, OFF_CONV, OFF_GATE = 0, 256, 1792, 2304
ALPHA = (2.0 * DEPTH) ** 0.25
LN_EPS = 1e-5
NEG_INF = -1e30
ADAM_LR, ADAM_B1, ADAM_B2, ADAM_EPS, ADAM_WD, ADAM_STEP = 0.001, 0.9, 0.999, 1e-08, 0.01, 10

N_DEV = 8
N_CHIP = 4
MESH = pl.DeviceIdType.MESH

QB = 2 * CHUNK
KPAD = N_PREV_CHUNKS * CHUNK
KW = QB + KPAD
SKEW_W = 768

VMEM_LIMIT = 56 * 1024 * 1024


def _cparams(**kw):
    return pltpu.CompilerParams(vmem_limit_bytes=VMEM_LIMIT, **kw)


def _full(shape):
    n = len(shape)
    return pl.BlockSpec(shape, lambda *_: (0,) * n)


_DIMS = {"nn": (((1,), (0,)), ((), ())), "nt": (((1,), (1,)), ((), ())), "tn": (((0,), (0,)), ((), ()))}


def _relu2(t):
    r = jnp.maximum(t, 0.0)
    return r * r


def _mm(a, b, mode, *, tm, tn, out_dtype, name, b_col0=0, n_out=None, bias=None, split_n=0, rider=None):
    if mode == "tn":
        k, m = a.shape
        n = b.shape[1] if n_out is None else n_out
        a_spec = pl.BlockSpec((k, tm), lambda i, j: (0, i))
        b_spec = pl.BlockSpec((k, tn), lambda i, j: (0, j + b_col0))
    elif mode == "nn":
        m, k = a.shape
        n = b.shape[1] if n_out is None else n_out
        a_spec = pl.BlockSpec((tm, k), lambda i, j: (i, 0))
        b_spec = pl.BlockSpec((k, tn), lambda i, j: (0, j + b_col0))
    else:
        m, k = a.shape
        n = b.shape[0] if n_out is None else n_out
        a_spec = pl.BlockSpec((tm, k), lambda i, j: (i, 0))
        b_spec = pl.BlockSpec((tn, k), lambda i, j: (j + b_col0, 0))
    assert m % tm == 0 and n % tn == 0, (name, m, n, tm, tn)
    dims = _DIMS[mode]

    def body(*refs):
        if bias is None:
            a_ref, b_ref, o_ref = refs
        else:
            a_ref, b_ref, bias_ref, o_ref = refs
        acc = lax.dot_general(a_ref[...].astype(BF16), b_ref[...].astype(BF16), dims, preferred_element_type=F32)
        if bias is not None:
            acc = acc + bias_ref[...]
        if split_n:
            for c in range(tn // split_n):
                o_ref[c] = acc[:, c * split_n:(c + 1) * split_n].astype(out_dtype)
        else:
            o_ref[...] = acc.astype(out_dtype)

    in_specs = [a_spec, b_spec]
    args = [a, b]
    if bias is not None:
        in_specs.append(pl.BlockSpec((1, tn), lambda i, j: (0, j)))
        args.append(bias)
    if split_n:
        out_spec = pl.BlockSpec((tn // split_n, tm, split_n), lambda i, j: (j, i, 0))
        out_shape = jax.ShapeDtypeStruct((n // split_n, m, split_n), out_dtype)
    else:
        out_spec = pl.BlockSpec((tm, tn), lambda i, j: (i, j))
        out_shape = jax.ShapeDtypeStruct((m, n), out_dtype)
    res = _call(body, name=name, grid=(m // tm, n // tn), in_specs=in_specs, out_specs=[out_spec],
                out_shape=[out_shape], scratch_shapes=[], args=args, rider=rider)
    return res[0] if rider is None else (res[0][0], res[1])


def _ln_hat(x):
    mu = jnp.mean(x, axis=-1, keepdims=True)
    xc = x - mu
    var = jnp.mean(xc * xc, axis=-1, keepdims=True)
    rstd = lax.rsqrt(var + LN_EPS)
    return xc * rstd, rstd


def _ln_hat_bwd(dhat, xhat, rstd):
    m1 = jnp.mean(dhat, axis=-1, keepdims=True)
    m2 = jnp.mean(dhat * xhat, axis=-1, keepdims=True)
    return rstd * (dhat - m1 - xhat * m2)


def _row_tile(s):
    return min(512, s)


def _acc_rows(ref, val, first):
    @pl.when(first)
    def _():
        ref[...] = jnp.zeros_like(ref)
    ref[...] += jnp.sum(val, axis=0, keepdims=True)


def _ln_mod(x, sc, sh, name):
    s, d = x.shape
    tm = _row_tile(s)

    def body(x_ref, sc_ref, sh_ref, u_ref):
        xhat, _ = _ln_hat(x_ref[...])
        u_ref[...] = (xhat * (1.0 + sc_ref[...]) + sh_ref[...]).astype(BF16)

    row = pl.BlockSpec((tm, d), lambda i: (i, 0))
    vec = pl.BlockSpec((1, d), lambda i: (0, 0))
    return pl.pallas_call(body, grid=(s // tm,), in_specs=[row, vec, vec], out_specs=row,
                          out_shape=jax.ShapeDtypeStruct((s, d), BF16), name=name, compiler_params=_cparams())(x, sc, sh)


def _resid_bwd_tile(dxo, x, f, g, gam):
    rhat, rstd = _ln_hat(ALPHA * x + g * f)
    dr = _ln_hat_bwd(dxo * gam, rhat, rstd)
    return ALPHA * dr, g * dr, dxo * rhat, dr * f


def _mm_ln_mod_bwd(a, b, x, sc, dres, name, rider=None, nxt=None):
    segs = list(a) if isinstance(a, (list, tuple)) else [a]
    s = segs[0].shape[0]
    k, d = b.shape
    assert sum(t.shape[1] for t in segs) == k
    tm = min(512 if k <= 4096 and nxt is None else 256, s)
    ns = len(segs)

    def body(*refs):
        seg_refs = refs[:ns]
        if nxt is None:
            b_ref, x_ref, sc_ref, dres_ref, dx_ref, dsc_ref, dsh_ref = refs[ns:]
        else:
            (b_ref, x_ref, sc_ref, dres_ref, xp_ref, fp_ref, gp_ref, gamp_ref,
             dresp_ref, dfp_ref, dsc_ref, dsh_ref, dgam_ref, dbet_ref, dg_ref, dbias_ref) = refs[ns:]
        first = pl.program_id(0) == 0
        duv, r0 = None, 0
        for seg_ref in seg_refs:
            w = seg_ref.shape[1]
            part = jnp.dot(seg_ref[...], b_ref[r0:r0 + w, :], preferred_element_type=F32)
            duv = part if duv is None else duv + part
            r0 += w
        xhat, rstd = _ln_hat(x_ref[...])
        dxv = dres_ref[...] + _ln_hat_bwd(duv * (1.0 + sc_ref[...]), xhat, rstd)
        _acc_rows(dsc_ref, duv * xhat, first)
        _acc_rows(dsh_ref, duv, first)
        if nxt is None:
            dx_ref[...] = dxv
        else:
            dresp, dfp, t_gam, t_g = _resid_bwd_tile(dxv, xp_ref[...], fp_ref[...], gp_ref[...], gamp_ref[...])
            dresp_ref[...] = dresp
            dfp_ref[...] = dfp.astype(BF16)
            _acc_rows(dgam_ref, t_gam, first)
            _acc_rows(dbet_ref, dxv, first)
            _acc_rows(dg_ref, t_g, first)
            _acc_rows(dbias_ref, dfp, first)

    row = pl.BlockSpec((tm, d), lambda i: (i, 0))
    vec = pl.BlockSpec((1, d), lambda i: (0, 0))
    vs = jax.ShapeDtypeStruct((1, d), F32)
    rows = jax.ShapeDtypeStruct((s, d), F32)
    in_specs = [pl.BlockSpec((tm, t.shape[1]), lambda i: (i, 0)) for t in segs] + [_full((k, d)), row, vec, row]
    args = (*segs, b, x, sc, dres)
    if nxt is None:
        out_specs, out_shape = [row, vec, vec], [rows, vs, vs]
    else:
        in_specs += [row, row, vec, vec]
        args += tuple(nxt)
        out_specs = [row, row] + [vec] * 6
        out_shape = [rows, jax.ShapeDtypeStruct((s, d), BF16)] + [vs] * 6
    res = _call(body, name=name, grid=(s // tm,), in_specs=in_specs, out_specs=out_specs, out_shape=out_shape,
                scratch_shapes=[], args=args, rider=rider)
    return tuple(res) if rider is None else (tuple(res[0]), res[1])


def _dw_segments(segs, u, name):
    s, d = u.shape
    tw = 256
    tiles = [t.shape[1] // tw for t in segs]
    starts = [sum(tiles[:j]) for j in range(len(segs))]
    ns = len(segs)

    def body(*refs):
        seg_refs, u_ref, o_ref = refs[:ns], refs[ns], refs[ns + 1]
        i = pl.program_id(0)
        for seg_ref, t0, nt in zip(seg_refs, starts, tiles):
            @pl.when((i >= t0) & (i < t0 + nt))
            def _(seg_ref=seg_ref):
                acc = lax.dot_general(seg_ref[...], u_ref[...], _DIMS["tn"], preferred_element_type=F32)
                o_ref[0] = acc[:, :d // 2].astype(BF16)
                o_ref[1] = acc[:, d // 2:].astype(BF16)

    def seg_spec(t0, nt):
        return pl.BlockSpec((s, tw), lambda i: (0, jnp.clip(i - t0, 0, nt - 1)))

    return pl.pallas_call(
        body, grid=(sum(tiles),), in_specs=[seg_spec(t0, nt) for t0, nt in zip(starts, tiles)] + [_full((s, d))],
        out_specs=pl.BlockSpec((2, tw, d // 2), lambda i: (0, i, 0)),
        out_shape=jax.ShapeDtypeStruct((2, sum(tiles) * tw, d // 2), BF16), name=name, compiler_params=_cparams(),
    )(*segs, u)


def _mm_resid_ln(a, b, bias, x, g, gam, bet, name, rider=None):
    s, k = a.shape
    d = b.shape[1]
    tm = min(512, s)

    def body(*refs):
        if bias is None:
            a_ref, b_ref, x_ref, g_ref, gam_ref, bet_ref, f_ref, o_ref = refs
        else:
            a_ref, b_ref, bias_ref, x_ref, g_ref, gam_ref, bet_ref, f_ref, o_ref = refs
        f = jnp.dot(a_ref[...], b_ref[...], preferred_element_type=F32)
        if bias is not None:
            f = f + bias_ref[...]
        f_ref[...] = f
        rhat, _ = _ln_hat(ALPHA * x_ref[...] + g_ref[...] * f)
        o_ref[...] = rhat * gam_ref[...] + bet_ref[...]

    row = pl.BlockSpec((tm, d), lambda i: (i, 0))
    vec = pl.BlockSpec((1, d), lambda i: (0, 0))
    in_specs = [pl.BlockSpec((tm, k), lambda i: (i, 0)), _full((k, d))] + ([vec] if bias is not None else []) + [row, vec, vec, vec]
    args = [a, b] + ([bias] if bias is not None else []) + [x, g, gam, bet]
    sh = jax.ShapeDtypeStruct((s, d), F32)
    res = _call(body, name=name, grid=(s // tm,), in_specs=in_specs, out_specs=[row, row], out_shape=[sh, sh],
                scratch_shapes=[], args=args, rider=rider)
    return tuple(res) if rider is None else (tuple(res[0]), res[1])


def _resid_ln_bwd(dxo, x, f, g, gam, name, tgt=None):
    s, d = x.shape
    tm = _row_tile(s)
    n = s // tm

    def body(*refs):
        if tgt is None:
            dxo_ref, x_ref, f_ref, g_ref, gam_ref, dres_ref, df_ref, dgam_ref, dbet_ref, dg_ref, dbias_ref = refs
            dxov = dxo_ref[...]
        else:
            (dxo_ref, t_ref, x_ref, f_ref, g_ref, gam_ref, dres_ref, df_ref, dgam_ref, dbet_ref, dg_ref, dbias_ref,
             loss_ref, sq_ref) = refs
            err = dxo_ref[...] - t_ref[...]
            dxov = err * (1.0 / d)
            _acc_rows(sq_ref, err * err, pl.program_id(0) == 0)

            @pl.when(pl.program_id(0) == n - 1)
            def _():
                tot = jnp.sum(sq_ref[...], axis=1, keepdims=True) * (0.5 / d)
                loss_ref[...] = jnp.broadcast_to(tot, (1, 128))

        first = pl.program_id(0) == 0
        dres, dfv, t_gam, t_g = _resid_bwd_tile(dxov, x_ref[...], f_ref[...], g_ref[...], gam_ref[...])
        dres_ref[...] = dres
        df_ref[...] = dfv.astype(BF16)
        _acc_rows(dgam_ref, t_gam, first)
        _acc_rows(dbet_ref, dxov, first)
        _acc_rows(dg_ref, t_g, first)
        _acc_rows(dbias_ref, dfv, first)

    row = pl.BlockSpec((tm, d), lambda i: (i, 0))
    vec = pl.BlockSpec((1, d), lambda i: (0, 0))
    vs = jax.ShapeDtypeStruct((1, d), F32)
    out_specs = [row, row, vec, vec, vec, vec]
    out_shape = [jax.ShapeDtypeStruct((s, d), F32), jax.ShapeDtypeStruct((s, d), BF16), vs, vs, vs, vs]
    if tgt is None:
        return pl.pallas_call(body, grid=(n,), in_specs=[row, row, row, vec, vec], out_specs=out_specs,
                              out_shape=out_shape, name=name, compiler_params=_cparams())(dxo, x, f, g, gam)
    return pl.pallas_call(body, grid=(n,), in_specs=[row, row, row, row, vec, vec],
                          out_specs=out_specs + [pl.BlockSpec((1, 128), lambda i: (0, 0))],
                          out_shape=out_shape + [jax.ShapeDtypeStruct((1, 128), F32)],
                          scratch_shapes=[pltpu.VMEM((1, d), F32)], name=name,
                          compiler_params=_cparams())(dxo, tgt, x, f, g, gam)


POOL_HALO = 16
POOL_ROWS = 256


def _pool_counts(r0, rows):
    t1 = (lax.broadcasted_iota(jnp.int32, (rows, 128), 0) + r0 + 1).astype(F32)
    low = lax.broadcasted_iota(jnp.int32, (rows, 128), 1) < POOL_GROUP
    wa = jnp.where(low, float(POOL_WINDOWS[0]), float(POOL_WINDOWS[1]))
    wb = jnp.where(low, float(POOL_WINDOWS[2]), float(POOL_WINDOWS[3]))
    return jnp.minimum(t1, wa), jnp.minimum(t1, wb), low


def _window_sums(win, off, rows, sign):
    def sl(j, half):
        return win[off + sign * j: off + sign * j + rows, 128 * half:128 * half + 128]
    a2 = sl(0, 0) + sl(1, 0)
    a4 = a2 + sl(2, 0) + sl(3, 0)
    a8 = sl(0, 1)
    for j in range(1, 8):
        a8 = a8 + sl(j, 1)
    a16 = a8
    for j in range(8, 16):
        a16 = a16 + sl(j, 1)
    return a2, a4, a8, a16


def _pool_fwd(zp, wp_bd, pscale, name):
    s = zp.shape[0]
    r = min(POOL_ROWS, s)

    def body(z_ref, wp_ref, sc_ref, p_ref, feat_ref, pad):
        pad[0:POOL_HALO, :] = jnp.zeros((POOL_HALO, D_POOL), F32)
        pad[POOL_HALO:, :] = z_ref[...]

        def step(i, carry):
            r0 = pl.multiple_of(i * r, r)
            win = pad[pl.ds(r0, r + POOL_HALO), :]
            a2, a4, a8, a16 = _window_sums(win, POOL_HALO, r, -1)
            ca, cb, low = _pool_counts(r0, r)
            x0 = win[POOL_HALO:, :]
            pa = jnp.where(low, a2, a4) / ca
            pb = jnp.where(low, a8, a16) / cb
            p = (jnp.concatenate([pa, pb], axis=1) - x0).astype(BF16)
            p_ref[pl.ds(r0, r), :] = p
            pw = jnp.dot(p, wp_ref[...], preferred_element_type=F32)
            feat_ref[pl.ds(r0, r), :] = (pw * sc_ref[...]).astype(BF16)
            return carry

        lax.fori_loop(0, s // r, step, 0)

    return pl.pallas_call(
        body, out_shape=[jax.ShapeDtypeStruct((s, D_POOL), BF16), jax.ShapeDtypeStruct((s, D_POOL), BF16)],
        scratch_shapes=[pltpu.VMEM((s + POOL_HALO, D_POOL), F32)], name=name, compiler_params=_cparams(),
    )(zp, wp_bd, pscale)


def _pool_bwd(dfeat, p, wp_bd, pscale, name):
    s = p.shape[0]
    r = min(POOL_ROWS, s)

    def body(df_ref, p_ref, wp_ref, sc_ref, dz_ref, dwp_ref, dsc_ref, gpad, dpbuf):
        dwp_ref[...] = jnp.zeros_like(dwp_ref)
        dsc_ref[...] = jnp.zeros_like(dsc_ref)
        gpad[s:, :] = jnp.zeros((POOL_HALO, D_POOL), F32)

        def step1(i, carry):
            r0 = pl.multiple_of(i * r, r)
            pv = p_ref[pl.ds(r0, r), :]
            dfv = df_ref[pl.ds(r0, r), :]
            pw = jnp.dot(pv, wp_ref[...], preferred_element_type=F32)
            dsc_ref[...] += jnp.sum(dfv * pw, axis=0, keepdims=True)
            dpw = (dfv * sc_ref[...]).astype(BF16)
            dwp_ref[...] += lax.dot_general(pv, dpw, _DIMS["tn"], preferred_element_type=F32)
            dp = lax.dot_general(dpw, wp_ref[...], _DIMS["nt"], preferred_element_type=F32)
            ca, cb, _ = _pool_counts(r0, r)
            gpad[pl.ds(r0, r), :] = dp / jnp.concatenate([ca, cb], axis=1)
            dpbuf[pl.ds(r0, r), :] = dp
            return carry

        lax.fori_loop(0, s // r, step1, 0)

        def step2(i, carry):
            r0 = pl.multiple_of(i * r, r)
            win = gpad[pl.ds(r0, r + POOL_HALO), :]
            a2, a4, a8, a16 = _window_sums(win, 0, r, 1)
            low = lax.broadcasted_iota(jnp.int32, (r, 128), 1) < POOL_GROUP
            acc = jnp.concatenate([jnp.where(low, a2, a4), jnp.where(low, a8, a16)], axis=1)
            dz_ref[pl.ds(r0, r), :] = (acc - dpbuf[pl.ds(r0, r), :]).astype(BF16)
            return carry

        lax.fori_loop(0, s // r, step2, 0)

    return pl.pallas_call(
        body,
        out_shape=[jax.ShapeDtypeStruct((s, D_POOL), BF16), jax.ShapeDtypeStruct((D_POOL, D_POOL), F32),
                   jax.ShapeDtypeStruct((1, D_POOL), F32)],
        scratch_shapes=[pltpu.VMEM((s + POOL_HALO, D_POOL), F32), pltpu.VMEM((s, D_POOL), F32)],
        name=name, compiler_params=_cparams(),
    )(dfeat, p, wp_bd, pscale)


def _skew_index():
    cp = lax.broadcasted_iota(jnp.int32, (SKEW_W, N_REL), 0)
    dist = jnp.where(cp < KW, KPAD - cp, KPAD + SKEW_W - cp)
    idx = jnp.clip(dist, -REL_CLIP, REL_CLIP) + REL_CLIP
    return (idx == lax.broadcasted_iota(jnp.int32, (SKEW_W, N_REL), 1)).astype(F32)


def _row_bits(b):
    return (lax.broadcasted_iota(jnp.int32, (QB, SKEW_W), 0) >> b) & 1 == 1


N_EDGE = KPAD // QB


def _bias_block(rel_bias, name):
    def body(rb_ref, o_ref):
        onehot = _skew_index()
        row0 = lax.dot_general(rb_ref[...], onehot, _DIMS["nt"], precision=lax.Precision.HIGHEST,
                               preferred_element_type=F32)
        r = lax.broadcasted_iota(jnp.int32, (QB, KW), 0)
        kk = lax.broadcasted_iota(jnp.int32, (QB, KW), 1)
        cq, ck = r // CHUNK, kk // CHUNK
        band = (ck >= cq) & (ck <= cq + N_PREV_CHUNKS)
        for h in range(N_HEADS):
            t = jnp.broadcast_to(row0[h:h + 1, :], (QB, SKEW_W))
            for b in range(7):
                t = jnp.where(_row_bits(b), pltpu.roll(t, 1 << b, 1), t)
            for e in range(N_EDGE + 1):
                o_ref[e, h] = jnp.where(band & (kk >= KPAD - e * QB), t[:, :KW], NEG_INF)

    return pl.pallas_call(body, out_shape=jax.ShapeDtypeStruct((N_EDGE + 1, N_HEADS, QB, KW), F32), name=name,
                          compiler_params=_cparams())(rel_bias)


def _bias_spec():
    return pl.BlockSpec((None, N_HEADS, QB, KW), lambda i: (jnp.minimum(i, N_EDGE), 0, 0, 0))


def _bias_block_bwd(ds_acc, name):
    def body(ds_ref, o_ref):
        sums = []
        for h in range(N_HEADS):
            t = jnp.concatenate([ds_ref[h], jnp.zeros((QB, SKEW_W - KW), F32)], axis=1)
            for b in range(7):
                t = jnp.where(_row_bits(b), pltpu.roll(t, SKEW_W - (1 << b), 1), t)
            sums.append(jnp.sum(t, axis=0, keepdims=True))
        allh = jnp.concatenate(sums, axis=0)
        o_ref[...] = jnp.dot(allh, _skew_index(), precision=lax.Precision.HIGHEST, preferred_element_type=F32)

    return pl.pallas_call(body, out_shape=jax.ShapeDtypeStruct((N_HEADS, N_REL), F32), name=name,
                          compiler_params=_cparams())(ds_acc)


def _scaled(q):
    return (q.astype(F32) * (HEAD_DIM ** -0.5)).astype(BF16)


def _probs(q, kw, bias_ref):
    sc = jnp.stack([lax.dot_general(q[:, HEAD_DIM * h:HEAD_DIM * (h + 1)], kw[:, HEAD_DIM * h:HEAD_DIM * (h + 1)],
                                    _DIMS["nt"], preferred_element_type=F32) + bias_ref[h] for h in range(N_HEADS)])
    e = jnp.exp(sc - jnp.max(sc, axis=-1, keepdims=True))
    return e * (1.0 / jnp.sum(e, axis=-1, keepdims=True))


def _load_padded_kv(qkv_hbm, kpad, vpad, sems, s):
    kpad[0:KPAD, :] = jnp.zeros((KPAD, D_ATTN), BF16)
    vpad[0:KPAD, :] = jnp.zeros((KPAD, D_ATTN), BF16)
    ck = pltpu.make_async_copy(qkv_hbm.at[:, D_ATTN:2 * D_ATTN], kpad.at[pl.ds(KPAD, s), :], sems.at[0])
    cv = pltpu.make_async_copy(qkv_hbm.at[:, 2 * D_ATTN:3 * D_ATTN], vpad.at[pl.ds(KPAD, s), :], sems.at[1])
    ck.start()
    cv.start()
    ck.wait()
    cv.wait()


def _attn_fwd(qkv, bias, name, rider=None):
    s = qkv.shape[0]

    def body(q_ref, qkv_hbm, bias_ref, o_ref, p_ref, kpad, vpad, sems):
        i = pl.program_id(0)

        @pl.when(i == 0)
        def _():
            _load_padded_kv(qkv_hbm, kpad, vpad, sems, s)

        base = pl.multiple_of(i * QB, QB)
        kw = kpad[pl.ds(base, KW), :]
        vw = vpad[pl.ds(base, KW), :]
        q = _scaled(q_ref[...])
        p = _probs(q, kw, bias_ref).astype(BF16)
        p_ref[...] = p
        outs = [jnp.dot(p[h], vw[:, HEAD_DIM * h:HEAD_DIM * (h + 1)], preferred_element_type=F32)
                for h in range(N_HEADS)]
        o_ref[...] = jnp.concatenate(outs, axis=1).astype(BF16)

    res = _call(
        body, name=name, grid=(s // QB,),
        in_specs=[pl.BlockSpec((QB, D_ATTN), lambda i: (i, 0)), pl.BlockSpec(memory_space=pl.ANY),
                  _bias_spec()],
        out_specs=[pl.BlockSpec((QB, D_ATTN), lambda i: (i, 0)), _probs_spec()],
        out_shape=[jax.ShapeDtypeStruct((s, D_ATTN), BF16), jax.ShapeDtypeStruct((N_HEADS, s, KW), BF16)],
        scratch_shapes=[pltpu.VMEM((s + KPAD, D_ATTN), BF16), pltpu.VMEM((s + KPAD, D_ATTN), BF16),
                        pltpu.SemaphoreType.DMA((2,))],
        args=(qkv, qkv, bias), rider=rider)
    return tuple(res) if rider is None else (tuple(res[0]), res[1])


def _probs_spec():
    return pl.BlockSpec((N_HEADS, QB, KW), lambda i: (0, i, 0))


def _attn_bwd(qkv, do, probs, name, rider=None):
    s = qkv.shape[0]
    n = s // QB

    def body(q_ref, qkv_hbm, do_ref, p_ref, dq_ref, dk_hbm, dv_hbm, ds_ref, kpad, vpad, dkacc, dvacc, sems):
        i = pl.program_id(0)

        @pl.when(i == 0)
        def _():
            _load_padded_kv(qkv_hbm, kpad, vpad, sems, s)
            dkacc[...] = jnp.zeros_like(dkacc)
            dvacc[...] = jnp.zeros_like(dvacc)
            ds_ref[...] = jnp.zeros_like(ds_ref)

        base = pl.multiple_of(i * QB, QB)
        kw = kpad[pl.ds(base, KW), :]
        vw = vpad[pl.ds(base, KW), :]
        q = _scaled(q_ref[...])
        dov = do_ref[...]
        heads = [slice(HEAD_DIM * h, HEAD_DIM * (h + 1)) for h in range(N_HEADS)]
        pb = p_ref[...]
        p = pb.astype(F32)
        dp = jnp.stack([lax.dot_general(dov[:, hs], vw[:, hs], _DIMS["nt"], preferred_element_type=F32) for hs in heads])
        ds = p * (dp - jnp.sum(dp * p, axis=-1, keepdims=True))
        ds_ref[...] += ds
        dsb = ds.astype(BF16)
        dvs = [lax.dot_general(pb[h], dov[:, hs], _DIMS["tn"], preferred_element_type=F32) for h, hs in enumerate(heads)]
        dqs = [jnp.dot(dsb[h], kw[:, hs], preferred_element_type=F32) for h, hs in enumerate(heads)]
        dks = [lax.dot_general(dsb[h], q[:, hs], _DIMS["tn"], preferred_element_type=F32) for h, hs in enumerate(heads)]
        dq_ref[...] = (jnp.concatenate(dqs, axis=1) * (HEAD_DIM ** -0.5)).astype(BF16)
        dkacc[pl.ds(base, KW), :] += jnp.concatenate(dks, axis=1)
        dvacc[pl.ds(base, KW), :] += jnp.concatenate(dvs, axis=1)

        @pl.when(i == n - 1)
        def _():
            def cast(j, carry):
                rows = pl.ds(pl.multiple_of(KPAD + j * 512, 512), 512)
                kpad[rows, :] = dkacc[rows, :].astype(BF16)
                vpad[rows, :] = dvacc[rows, :].astype(BF16)
                return carry

            lax.fori_loop(0, s // 512, cast, 0)
            ck = pltpu.make_async_copy(kpad.at[pl.ds(KPAD, s), :], dk_hbm, sems.at[0])
            cv = pltpu.make_async_copy(vpad.at[pl.ds(KPAD, s), :], dv_hbm, sems.at[1])
            ck.start()
            cv.start()
            ck.wait()
            cv.wait()

    blk = pl.BlockSpec((QB, D_ATTN), lambda i: (i, 0))
    acc_shape = jax.ShapeDtypeStruct((s, D_ATTN), BF16)
    return _call(
        body, name=name, grid=(n,),
        in_specs=[blk, pl.BlockSpec(memory_space=pl.ANY), blk, _probs_spec()],
        out_specs=[blk, pl.BlockSpec(memory_space=pl.ANY), pl.BlockSpec(memory_space=pl.ANY), _full((N_HEADS, QB, KW))],
        out_shape=[jax.ShapeDtypeStruct((s, D_ATTN), BF16), acc_shape, acc_shape,
                   jax.ShapeDtypeStruct((N_HEADS, QB, KW), F32)],
        scratch_shapes=[pltpu.VMEM((s + KPAD, D_ATTN), BF16), pltpu.VMEM((s + KPAD, D_ATTN), BF16),
                        pltpu.VMEM((s + KPAD, D_ATTN), F32), pltpu.VMEM((s + KPAD, D_ATTN), F32),
                        pltpu.SemaphoreType.DMA((2,))],
        args=(qkv, qkv, do, probs), rider=rider)


CONV_HALO = 32
CONV_ROWS = 64


def _sigmoid(t):
    return 1.0 / (1.0 + jnp.exp(-t))


CONV_WIN = CONV_ROWS + CONV_HALO - 8


def _row_windows(ref, r0, buf):
    win = ref[pl.ds(r0, CONV_ROWS + CONV_HALO), :]
    for j in range(1, 8):
        buf[j - 1] = win[j:j + CONV_WIN, :]

    def get(o):
        j, a = o % 8, o - o % 8
        if j == 0:
            return ref[pl.ds(r0 + a, CONV_ROWS), :]
        return buf[j - 1, a:a + CONV_ROWS, :]

    return get


def _glu_rows(z_ref, r0, rows):
    a = z_ref[pl.ds(r0, rows), 0:D_CONV]
    b = z_ref[pl.ds(r0, rows), D_CONV:2 * D_CONV]
    return a, _sigmoid(b)


def _conv_fwd(zc, conv_w, conv_b, ln_g, ln_b, name):
    s = zc.shape[0]
    rt = min(256, s)

    def body(z_ref, w_ref, cb_ref, g_ref, b_ref, cv_ref, feat_ref, hpad, shifts):
        hpad[0:CONV_HALO, :] = jnp.zeros((CONV_HALO, D_CONV), F32)

        def glu(i, carry):
            r0 = pl.multiple_of(i * rt, rt)
            a, sb = _glu_rows(z_ref, r0, rt)
            hpad[pl.ds(r0 + CONV_HALO, rt), :] = a * sb
            return carry

        lax.fori_loop(0, s // rt, glu, 0)
        w = w_ref[...]

        def conv(i, carry):
            r0 = pl.multiple_of(i * CONV_ROWS, CONV_ROWS)
            win = _row_windows(hpad, r0, shifts)
            acc = jnp.broadcast_to(cb_ref[...], (CONV_ROWS, D_CONV))
            for k in range(CONV_WIDTH):
                acc = acc + win(2 + k) * w[k:k + 1, :]
            cv_ref[pl.ds(r0, CONV_ROWS), :] = acc
            yhat, _ = _ln_hat(acc)
            y = yhat * g_ref[...] + b_ref[...]
            feat_ref[pl.ds(r0, CONV_ROWS), :] = (y * _sigmoid(y)).astype(BF16)
            return carry

        lax.fori_loop(0, s // CONV_ROWS, conv, 0)

    return pl.pallas_call(
        body, out_shape=[jax.ShapeDtypeStruct((s, D_CONV), F32), jax.ShapeDtypeStruct((s, D_CONV), BF16)],
        scratch_shapes=[pltpu.VMEM((s + CONV_HALO, D_CONV), F32), pltpu.VMEM((7, CONV_WIN, D_CONV), F32)],
        name=name, compiler_params=_cparams(),
    )(zc, conv_w, conv_b, ln_g, ln_b)


def _conv_bwd(dfeat, cv, zc, conv_w, ln_g, ln_b, name):
    s = zc.shape[0]
    rt = min(256, s)

    def body(df_ref, cv_ref, z_ref, w_ref, g_ref, b_ref, dz_ref, dw_ref, dcb_ref, dg_ref, db_ref, hpad, dcvpad, dwacc,
             hshifts, dshifts):
        hpad[0:CONV_HALO, :] = jnp.zeros((CONV_HALO, D_CONV), F32)
        dcvpad[s:, :] = jnp.zeros((CONV_HALO, D_CONV), F32)
        dwacc[...] = jnp.zeros_like(dwacc)
        dcb_ref[...] = jnp.zeros_like(dcb_ref)
        dg_ref[...] = jnp.zeros_like(dg_ref)
        db_ref[...] = jnp.zeros_like(db_ref)

        def pass1(i, carry):
            r0 = pl.multiple_of(i * rt, rt)
            a, sb = _glu_rows(z_ref, r0, rt)
            hpad[pl.ds(r0 + CONV_HALO, rt), :] = a * sb
            cvhat, rstd = _ln_hat(cv_ref[pl.ds(r0, rt), :])
            y = cvhat * g_ref[...] + b_ref[...]
            sg = _sigmoid(y)
            dy = df_ref[pl.ds(r0, rt), :] * (sg * (1.0 + y * (1.0 - sg)))
            dg_ref[...] += jnp.sum(dy * cvhat, axis=0, keepdims=True)
            db_ref[...] += jnp.sum(dy, axis=0, keepdims=True)
            dcv = _ln_hat_bwd(dy * g_ref[...], cvhat, rstd)
            dcb_ref[...] += jnp.sum(dcv, axis=0, keepdims=True)
            dcvpad[pl.ds(r0, rt), :] = dcv
            return carry

        lax.fori_loop(0, s // rt, pass1, 0)
        w = w_ref[...]

        def pass2(i, carry):
            r0 = pl.multiple_of(i * CONV_ROWS, CONV_ROWS)
            dwin = _row_windows(dcvpad, r0, dshifts)
            hwin = _row_windows(hpad, r0, hshifts)
            dcv = dwin(0)
            dh = jnp.zeros((CONV_ROWS, D_CONV), F32)
            for k in range(CONV_WIDTH):
                dh = dh + dwin(30 - k) * w[k:k + 1, :]
                prod = dcv * hwin(2 + k)
                dwacc[8 * k:8 * k + 8, :] += jnp.sum(prod.reshape(CONV_ROWS // 8, 8, D_CONV), axis=0)
            a, sb = _glu_rows(z_ref, r0, CONV_ROWS)
            dz_ref[pl.ds(r0, CONV_ROWS), :] = jnp.concatenate([dh * sb, dh * a * sb * (1.0 - sb)], axis=1).astype(BF16)
            return carry

        lax.fori_loop(0, s // CONV_ROWS, pass2, 0)
        dw_ref[...] = jnp.sum(dwacc[...].reshape(32, 8, D_CONV), axis=1)

    vs = jax.ShapeDtypeStruct((1, D_CONV), F32)
    return pl.pallas_call(
        body,
        out_shape=[jax.ShapeDtypeStruct((s, 2 * D_CONV), BF16), jax.ShapeDtypeStruct((32, D_CONV), F32), vs, vs, vs],
        scratch_shapes=[pltpu.VMEM((s + CONV_HALO, D_CONV), F32), pltpu.VMEM((s + CONV_HALO, D_CONV), F32),
                        pltpu.VMEM((256, D_CONV), F32), pltpu.VMEM((7, CONV_WIN, D_CONV), F32),
                        pltpu.VMEM((7, CONV_WIN, D_CONV), F32)],
        name=name, compiler_params=_cparams(),
    )(dfeat, cv, zc, conv_w, ln_g, ln_b)


def _merge(zg, b_gate, ys, name):
    s = zg.shape[0]
    tm = _row_tile(s)

    def body(zg_ref, bg_ref, y0_ref, y1_ref, y2_ref, o_ref):
        acc = None
        for j, y_ref in enumerate((y0_ref, y1_ref, y2_ref)):
            cs = slice(D_MODEL * j, D_MODEL * (j + 1))
            t = _sigmoid(zg_ref[:, cs] + bg_ref[:, cs]) * y_ref[...]
            acc = t if acc is None else acc + t
        o_ref[...] = acc.astype(BF16)

    row = pl.BlockSpec((tm, D_MODEL), lambda i: (i, 0))
    return pl.pallas_call(
        body, grid=(s // tm,),
        in_specs=[pl.BlockSpec((tm, 3 * D_MODEL), lambda i: (i, 0)), _full((1, 3 * D_MODEL)), row, row, row],
        out_specs=row, out_shape=jax.ShapeDtypeStruct((s, D_MODEL), BF16), name=name, compiler_params=_cparams(),
    )(zg, b_gate, *ys)


def _merge_bwd(dm, zg, b_gate, ys, name):
    s = zg.shape[0]
    tm = min(256, s)

    def body(dm_ref, zg_ref, bg_ref, y0_ref, y1_ref, y2_ref, d0_ref, d1_ref, d2_ref, dzg_ref, dbg_ref):
        first = pl.program_id(0) == 0

        @pl.when(first)
        def _():
            dbg_ref[...] = jnp.zeros_like(dbg_ref)

        dmv = dm_ref[...]
        for j, (y_ref, d_ref) in enumerate(((y0_ref, d0_ref), (y1_ref, d1_ref), (y2_ref, d2_ref))):
            cs = slice(D_MODEL * j, D_MODEL * (j + 1))
            g = _sigmoid(zg_ref[:, cs] + bg_ref[:, cs])
            d_ref[...] = (dmv * g).astype(BF16)
            dzg = dmv * y_ref[...] * g * (1.0 - g)
            dzg_ref[:, cs] = dzg.astype(BF16)
            dbg_ref[:, cs] += jnp.sum(dzg, axis=0, keepdims=True)

    row = pl.BlockSpec((tm, D_MODEL), lambda i: (i, 0))
    wide = pl.BlockSpec((tm, 3 * D_MODEL), lambda i: (i, 0))
    yb = jax.ShapeDtypeStruct((s, D_MODEL), BF16)
    return pl.pallas_call(
        body, grid=(s // tm,),
        in_specs=[row, wide, _full((1, 3 * D_MODEL)), row, row, row],
        out_specs=[row, row, row, wide, _full((1, 3 * D_MODEL))],
        out_shape=[yb, yb, yb, jax.ShapeDtypeStruct((s, 3 * D_MODEL), BF16), jax.ShapeDtypeStruct((1, 3 * D_MODEL), F32)],
        name=name, compiler_params=_cparams(),
    )(dm, zg, b_gate, *ys)


def _ff_hidden(u2, w_ff1t, b_ff1, name, rider=None):
    s = u2.shape[0]
    tm, tn = min(2048, s), 1024

    def body(a_ref, b_ref, bias_ref, pre_ref, h_ref):
        acc = lax.dot_general(a_ref[...], b_ref[...], _DIMS["nt"], preferred_element_type=F32) + bias_ref[...]
        pre_ref[...] = acc.astype(BF16)
        h_ref[...] = _relu2(acc).astype(BF16)

    blk = pl.BlockSpec((tm, tn), lambda i, j: (i, j))
    sh = jax.ShapeDtypeStruct((s, D_FF), BF16)
    res = _call(body, name=name, grid=(s // tm, D_FF // tn),
                in_specs=[pl.BlockSpec((tm, D_MODEL), lambda i, j: (i, 0)), pl.BlockSpec((tn, D_MODEL), lambda i, j: (j, 0)),
                          pl.BlockSpec((1, tn), lambda i, j: (0, j))],
                out_specs=[blk, blk], out_shape=[sh, sh], scratch_shapes=[], args=(u2, w_ff1t, b_ff1), rider=rider)
    return tuple(res) if rider is None else (tuple(res[0]), res[1])


def _ff_hidden_bwd(dff, w_ff2, hpre, name, rider=None):
    s = dff.shape[0]
    tm, tn = min(1024, s), 1024

    def body(a_ref, b_ref, h_ref, o_ref, sum_ref):
        dh = lax.dot_general(a_ref[...], b_ref[...], _DIMS["nt"], preferred_element_type=F32)
        dpre = dh * (2.0 * jnp.maximum(h_ref[...].astype(F32), 0.0))
        o_ref[...] = dpre.astype(BF16)
        _acc_rows(sum_ref, dpre, pl.program_id(1) == 0)

    res = _call(
        body, name=name, grid=(D_FF // tn, s // tm),
        in_specs=[pl.BlockSpec((tm, D_MODEL), lambda j, i: (i, 0)), pl.BlockSpec((tn, D_MODEL), lambda j, i: (j, 0)),
                  pl.BlockSpec((tm, tn), lambda j, i: (i, j))],
        out_specs=[pl.BlockSpec((tm, tn), lambda j, i: (i, j)), pl.BlockSpec((1, tn), lambda j, i: (0, j))],
        out_shape=[jax.ShapeDtypeStruct((s, D_FF), BF16), jax.ShapeDtypeStruct((1, D_FF), F32)],
        scratch_shapes=[], args=(dff, w_ff2, hpre), rider=rider)
    return tuple(res) if rider is None else (tuple(res[0]), res[1])


def _silu(t):
    return t * _sigmoid(t)


def _mod_fwd(c_all, w_ada_sh, b_ada_sh, name):
    cols = w_ada_sh.shape[2]

    def body(c_ref, w_ref, b_ref, o_ref):
        ca = _silu(c_ref[...]).astype(BF16)
        o_ref[0] = jnp.dot(ca, w_ref[0].astype(BF16), preferred_element_type=F32) + b_ref[0]

    return pl.pallas_call(
        body, grid=(DEPTH,),
        in_specs=[_full((N_DEV, D_MODEL)), pl.BlockSpec((1, D_MODEL, cols), lambda l: (l, 0, 0)),
                  pl.BlockSpec((1, 1, cols), lambda l: (l, 0, 0))],
        out_specs=pl.BlockSpec((1, N_DEV, cols), lambda l: (l, 0, 0)),
        out_shape=jax.ShapeDtypeStruct((DEPTH, N_DEV, cols), F32), name=name, compiler_params=_cparams(),
    )(c_all, w_ada_sh, b_ada_sh)


def _mod_bwd(c_all, dmod_sh, name):
    cols = dmod_sh.shape[2]

    def body(c_ref, d_ref, o_ref):
        ca = _silu(c_ref[...])
        o_ref[0] = lax.dot_general(ca, d_ref[0], _DIMS["tn"], precision=lax.Precision.HIGHEST,
                                   preferred_element_type=F32)

    return pl.pallas_call(
        body, grid=(DEPTH,),
        in_specs=[_full((N_DEV, D_MODEL)), pl.BlockSpec((1, N_DEV, cols), lambda l: (l, 0, 0))],
        out_specs=pl.BlockSpec((1, D_MODEL, cols), lambda l: (l, 0, 0)),
        out_shape=jax.ShapeDtypeStruct((DEPTH, D_MODEL, cols), F32), name=name, compiler_params=_cparams(),
    )(c_all, dmod_sh)


def _flat_tiles(rows, cols, itemsize_total):
    budget = 12 * 1024 * 1024
    tr = rows
    while tr % 32 == 0 and tr * cols * itemsize_total > budget:
        tr //= 2
    return tr


def _sum_cores(dw, recv, place, name):
    _, m, n = dw.shape
    tr = _flat_tiles(m, n, 6)

    def body(place_ref, a_ref, b_ref, o_ref):
        o_ref[...] = (a_ref[...].astype(F32) + b_ref[...].astype(F32)).astype(BF16)

    grid_spec = pltpu.PrefetchScalarGridSpec(
        num_scalar_prefetch=1, grid=(m // tr,),
        in_specs=[pl.BlockSpec((None, tr, n), lambda i, pr: (pr[0], i, 0)), pl.BlockSpec((tr, n), lambda i, pr: (i, 0))],
        out_specs=pl.BlockSpec((tr, n), lambda i, pr: (i, 0)))
    return pl.pallas_call(body, grid_spec=grid_spec, out_shape=jax.ShapeDtypeStruct((m, n), BF16), name=name,
                          compiler_params=_cparams())(place, dw, recv)


def _sum_chips(h, r, place, name):
    _, rs, n = h.shape
    tr = _flat_tiles(rs, n, 12)

    def body(place_ref, h_ref, r_ref, o_ref):
        o_ref[...] = ((h_ref[...].astype(F32) + r_ref[0].astype(F32)) + r_ref[1].astype(F32)) + r_ref[2].astype(F32)

    grid_spec = pltpu.PrefetchScalarGridSpec(
        num_scalar_prefetch=1, grid=(rs // tr,),
        in_specs=[pl.BlockSpec((None, tr, n), lambda i, pr: (pr[1], i, 0)), pl.BlockSpec((3, tr, n), lambda i, pr: (0, i, 0))],
        out_specs=pl.BlockSpec((tr, n), lambda i, pr: (i, 0)))
    return pl.pallas_call(body, grid_spec=grid_spec, out_shape=jax.ShapeDtypeStruct((rs, n), F32), name=name,
                          compiler_params=_cparams())(place, h, r)


def _adam_math(w, g, m, v):
    m2 = ADAM_B1 * m + (1.0 - ADAM_B1) * g
    v2 = ADAM_B2 * v + (1.0 - ADAM_B2) * (g * g)
    m_hat = m2 / (1.0 - ADAM_B1 ** ADAM_STEP)
    v_hat = v2 / (1.0 - ADAM_B2 ** ADAM_STEP)
    delta = -ADAM_LR * (m_hat / (jnp.sqrt(v_hat) + ADAM_EPS) + ADAM_WD * w)
    return delta, m2, v2


def _adamw(w, m, v, grads, name):
    r, c = w.shape
    tr = _flat_tiles(r, c, 4 * (7 + len(grads)))

    def body(*refs):
        w_ref, m_ref, v_ref = refs[:3]
        g_refs = refs[3:3 + len(grads)]
        g_ref, d_ref, m2_ref, v2_ref = refs[3 + len(grads):]
        g = g_refs[0][...]
        for gr in g_refs[1:]:
            g = g + gr[...]
        delta, m2, v2 = _adam_math(w_ref[...], g, m_ref[...], v_ref[...])
        g_ref[...] = g
        d_ref[...] = delta
        m2_ref[...] = m2
        v2_ref[...] = v2

    blk = pl.BlockSpec((tr, c), lambda i: (i, 0))
    sh = jax.ShapeDtypeStruct((r, c), F32)
    return pl.pallas_call(body, grid=(r // tr,), in_specs=[blk] * (3 + len(grads)), out_specs=[blk] * 4,
                          out_shape=[sh] * 4, name=name, compiler_params=_cparams())(w, m, v, *grads)


def _adamw_halves(w, m, v, own, other, place, split, name):
    nl, r, c = w.shape
    hr, hc = own[0].shape
    tr = _flat_tiles(hr, hc, 4 * (7 + 2 * nl))
    nt = hr // tr
    if split == "rows":
        w_spec = pl.BlockSpec((None, tr, c), lambda l, h, t, pr: (l, h * nt + t, 0))
    else:
        w_spec = pl.BlockSpec((None, tr, hc), lambda l, h, t, pr: (l, t, h))

    def g_spec(layer, mine):
        return pl.BlockSpec((tr, hc), lambda l, h, t, pr: (jnp.where((l == layer) & ((h == pr[0]) == mine), t, nt - 1), 0))

    def body(place_ref, w_ref, m_ref, v_ref, *refs):
        own_refs, other_refs = refs[:nl], refs[nl:2 * nl]
        g_ref, d_ref, m2_ref, v2_ref = refs[2 * nl:]
        layer = pl.program_id(0)
        mine = pl.program_id(1) == place_ref[0]
        g = None
        for li in range(nl):
            cand = jnp.where(mine, own_refs[li][...], other_refs[li][...])
            g = cand if g is None else jnp.where(layer == li, cand, g)
        delta, m2, v2 = _adam_math(w_ref[...], g, m_ref[...], v_ref[...])
        g_ref[...] = g
        d_ref[...] = delta
        m2_ref[...] = m2
        v2_ref[...] = v2

    sh = jax.ShapeDtypeStruct((nl, r, c), F32)
    g_specs = [g_spec(li, True) for li in range(nl)] + [g_spec(li, False) for li in range(nl)]
    return _call(body, name=name, grid=(nl, 2, nt), in_specs=[w_spec] * 3 + g_specs, out_specs=[w_spec] * 4,
                 out_shape=[sh] * 4, scratch_shapes=[], args=(w, m, v, *own, *other), prefetch=(place,))


def _adamw_small(w, m, v, g_all, name):
    r, c = w.shape

    def body(w_ref, m_ref, v_ref, g_ref, go_ref, d_ref, m2_ref, v2_ref):
        g = g_ref[0]
        for b in range(1, N_DEV):
            g = g + g_ref[b]
        delta, m2, v2 = _adam_math(w_ref[...], g, m_ref[...], v_ref[...])
        go_ref[...] = g
        d_ref[...] = delta
        m2_ref[...] = m2
        v2_ref[...] = v2

    sh = jax.ShapeDtypeStruct((r, c), F32)
    return pl.pallas_call(body, out_shape=[sh] * 4, name=name, compiler_params=_cparams())(w, m, v, g_all)


def _me():
    return lax.axis_index("x"), lax.axis_index("y"), lax.axis_index("c")


def _flip(v, bit):
    return 1 - v if bit else v


def _allgather_small(blk, name):
    r, c = blk.shape

    def body(x_ref, o_ref, send_sems, recv_sems):
        x, y, cc = _me()
        me = 4 * x + 2 * y + cc
        copies = []
        for k in range(1, N_DEV):
            peer = (_flip(x, k & 4), _flip(y, k & 2), _flip(cc, k & 1))
            cp = pltpu.make_async_remote_copy(src_ref=x_ref, dst_ref=o_ref.at[me], send_sem=send_sems.at[k - 1],
                                              recv_sem=recv_sems.at[k - 1], device_id=peer, device_id_type=MESH)
            cp.start()
            copies.append(cp)
        o_ref[me] = x_ref[...]
        for cp in copies:
            cp.wait()

    return pl.pallas_call(
        body, out_shape=jax.ShapeDtypeStruct((N_DEV, r, c), F32),
        in_specs=[pl.BlockSpec(memory_space=pltpu.VMEM)], out_specs=pl.BlockSpec(memory_space=pltpu.VMEM),
        scratch_shapes=[pltpu.SemaphoreType.DMA((N_DEV - 1,)), pltpu.SemaphoreType.DMA((N_DEV - 1,))],
        name=name, compiler_params=_cparams(),
    )(blk)


class _Rider:
    def __init__(self, arrays, out_shapes, scratch_shapes, start, finish):
        self.arrays, self.out_shapes, self.scratch_shapes = list(arrays), list(out_shapes), list(scratch_shapes)
        self.start, self.finish = start, finish


def _call(body, *, name, grid, in_specs, out_specs, out_shape, scratch_shapes, args, rider=None, prefetch=()):
    npf = len(prefetch)

    def launch(fn, in_specs, out_specs, out_shape, scratch_shapes, args):
        grid_spec = pltpu.PrefetchScalarGridSpec(num_scalar_prefetch=npf, grid=grid, in_specs=in_specs,
                                                 out_specs=out_specs, scratch_shapes=scratch_shapes)
        return pl.pallas_call(fn, grid_spec=grid_spec, out_shape=out_shape, name=name,
                              compiler_params=_cparams())(*prefetch, *args)

    if rider is None:
        return launch(body, list(in_specs), list(out_specs), list(out_shape), list(scratch_shapes), args)
    ni, no, ns = len(in_specs), len(out_specs), len(scratch_shapes)
    ri, ro = len(rider.arrays), len(rider.out_shapes)
    steps = int(np.prod(grid))

    def wrapped(*refs):
        pf, refs = refs[:npf], refs[npf:]
        h_in, r_in = refs[:ni], refs[ni:ni + ri]
        h_out, r_out = refs[ni + ri:ni + ri + no], refs[ni + ri + no:ni + ri + no + ro]
        h_scr, r_scr = refs[ni + ri + no + ro:ni + ri + no + ro + ns], refs[ni + ri + no + ro + ns:]
        step = pl.program_id(0)
        for d in range(1, len(grid)):
            step = step * grid[d] + pl.program_id(d)

        @pl.when(step == 0)
        def _():
            rider.start(r_in, r_out, r_scr)

        body(*pf, *h_in, *h_out, *h_scr)

        @pl.when(step == steps - 1)
        def _():
            rider.finish(r_in, r_out, r_scr)

    anyspec = pl.BlockSpec(memory_space=pl.ANY)
    res = launch(wrapped, list(in_specs) + [anyspec] * ri, list(out_specs) + [anyspec] * ro,
                 list(out_shape) + rider.out_shapes, list(scratch_shapes) + rider.scratch_shapes,
                 list(args) + rider.arrays)
    return res[:no], res[no:]


def _run_rider(rider, name):
    ri = len(rider.arrays)

    def body(*refs):
        r_in, r_out, r_scr = refs[:ri], refs[ri:ri + len(rider.out_shapes)], refs[ri + len(rider.out_shapes):]
        rider.start(r_in, r_out, r_scr)
        rider.finish(r_in, r_out, r_scr)

    anyspec = pl.BlockSpec(memory_space=pl.ANY)
    return pl.pallas_call(body, in_specs=[anyspec] * ri, out_specs=[anyspec] * len(rider.out_shapes),
                          out_shape=rider.out_shapes, scratch_shapes=rider.scratch_shapes, name=name,
                          compiler_params=_cparams())(*rider.arrays)


def _allgather_rider(blk):
    def copies(ins, outs, scr):
        send_sems, recv_sems, loc_sems, stage = scr
        x, y, cc = _me()
        me = 4 * x + 2 * y + cc
        remote = [pltpu.make_async_remote_copy(
            src_ref=ins[0], dst_ref=outs[0].at[me], send_sem=send_sems.at[k - 1], recv_sem=recv_sems.at[k - 1],
            device_id=(_flip(x, k & 4), _flip(y, k & 2), _flip(cc, k & 1)), device_id_type=MESH) for k in range(1, N_DEV)]
        return remote, pltpu.make_async_copy(ins[0], stage, loc_sems.at[0]), (outs[0].at[me], stage, loc_sems.at[1])

    def start(ins, outs, scr):
        remote, lin, _ = copies(ins, outs, scr)
        lin.start()
        for cp in remote:
            cp.start()

    def finish(ins, outs, scr):
        remote, lin, (dst, stage, sem) = copies(ins, outs, scr)
        lin.wait()
        lout = pltpu.make_async_copy(stage, dst, sem)
        lout.start()
        for cp in remote:
            cp.wait()
        lout.wait()

    return _Rider([blk], [jax.ShapeDtypeStruct((N_DEV,) + blk.shape, blk.dtype)],
                  [pltpu.SemaphoreType.DMA((N_DEV - 1,)), pltpu.SemaphoreType.DMA((N_DEV - 1,)),
                   pltpu.SemaphoreType.DMA((2,)), pltpu.VMEM(blk.shape, blk.dtype)], start, finish)


def _gather_rider(shards):
    n = len(shards)

    def copies(ins, outs, scr, relay=True):
        ici_send, ici_recv, d2d_send, d2d_recv, loc_sems = scr[:5]
        stage = scr[5:]
        x, y, cc = _me()
        chip = 2 * x + y
        sibling = (x, y, 1 - cc)
        local, sends, relays = [], [], []
        for j in range(n):
            def rows(ch, h, j=j):
                return outs[j].at[ch, h]

            lc = pltpu.make_async_copy(ins[j], stage[j], loc_sems.at[j])
            local.append((lc, pltpu.make_async_copy(stage[j], outs[j].at[chip], loc_sems.at[n + j]) if relay else None))
            for k in range(1, N_CHIP):
                px, py = _flip(x, k & 2), _flip(y, k & 1)
                pchip = 2 * px + py
                q = 3 * j + k - 1
                out_cp = pltpu.make_async_remote_copy(src_ref=ins[j].at[cc], dst_ref=rows(chip, cc),
                                                      send_sem=ici_send.at[q], recv_sem=ici_recv.at[q],
                                                      device_id=(px, py, cc), device_id_type=MESH)
                sends.append(out_cp)
                if not relay:
                    continue
                arrival = pltpu.make_async_remote_copy(src_ref=rows(pchip, cc), dst_ref=rows(pchip, cc),
                                                       send_sem=ici_send.at[q], recv_sem=ici_recv.at[q],
                                                       device_id=(px, py, cc), device_id_type=MESH)
                forward = pltpu.make_async_remote_copy(src_ref=rows(pchip, cc), dst_ref=rows(pchip, cc),
                                                       send_sem=d2d_send.at[q], recv_sem=d2d_recv.at[q],
                                                       device_id=sibling, device_id_type=MESH)
                from_sibling = pltpu.make_async_remote_copy(src_ref=rows(pchip, 1 - cc), dst_ref=rows(pchip, 1 - cc),
                                                            send_sem=d2d_send.at[q], recv_sem=d2d_recv.at[q],
                                                            device_id=sibling, device_id_type=MESH)
                relays.append((arrival, forward, from_sibling))
        return local, sends, relays

    def start(ins, outs, scr):
        local, sends, _ = copies(ins, outs, scr, relay=False)
        for lin, _ in local:
            lin.start()
        for cp in sends:
            cp.start()

    def finish(ins, outs, scr):
        local, sends, relays = copies(ins, outs, scr)
        for lin, lout in local:
            lin.wait()
            lout.start()
        for arrival, forward, _ in relays:
            arrival.wait_recv()
            forward.start()
        for cp in sends:
            cp.wait_send()
        for _, forward, from_sibling in relays:
            forward.wait_send()
            from_sibling.wait_recv()
        for _, lout in local:
            lout.wait()

    scratch = [pltpu.SemaphoreType.DMA((3 * n,)), pltpu.SemaphoreType.DMA((3 * n,)), pltpu.SemaphoreType.DMA((3 * n,)),
               pltpu.SemaphoreType.DMA((3 * n,)), pltpu.SemaphoreType.DMA((2 * n,))]
    scratch += [pltpu.VMEM(a.shape, a.dtype) for a in shards]
    return _Rider(shards, [jax.ShapeDtypeStruct((N_CHIP,) + a.shape, a.dtype) for a in shards], scratch, start, finish)


def _sibling_rider(arrs, other_half=False):
    n = len(arrs)

    def copies(ins, outs, scr):
        send_sems, recv_sems = scr
        x, y, cc = _me()
        return [pltpu.make_async_remote_copy(
            src_ref=ins[j].at[1 - cc] if other_half else ins[j], dst_ref=outs[j], send_sem=send_sems.at[j],
            recv_sem=recv_sems.at[j], device_id=(x, y, 1 - cc), device_id_type=MESH) for j in range(n)]

    def start(ins, outs, scr):
        for cp in copies(ins, outs, scr):
            cp.start()

    def finish(ins, outs, scr):
        for cp in copies(ins, outs, scr):
            cp.wait()

    return _Rider(arrs, [jax.ShapeDtypeStruct(a.shape[1:] if other_half else a.shape, a.dtype) for a in arrs],
                  [pltpu.SemaphoreType.DMA((n,)), pltpu.SemaphoreType.DMA((n,))], start, finish)


def _sibling_send(arrs, name, other_half=False):
    return _run_rider(_sibling_rider(arrs, other_half), name)


def _join_riders(first, second):
    ni, no, ns = len(first.arrays), len(first.out_shapes), len(first.scratch_shapes)

    def split(ins, outs, scr):
        return (ins[:ni], outs[:no], scr[:ns]), (ins[ni:], outs[no:], scr[ns:])

    def start(ins, outs, scr):
        a, b = split(ins, outs, scr)
        first.start(*a)
        second.start(*b)

    def finish(ins, outs, scr):
        a, b = split(ins, outs, scr)
        first.finish(*a)
        second.finish(*b)

    return _Rider(first.arrays + second.arrays, first.out_shapes + second.out_shapes,
                  first.scratch_shapes + second.scratch_shapes, start, finish)


def _scatter_rider(arrs):
    n = len(arrs)

    def copies(ins, outs, scr):
        send_sems, recv_sems = scr
        x, y, cc = _me()
        cps = []
        for j in range(n):
            for k in range(1, N_CHIP):
                px, py = _flip(x, k & 2), _flip(y, k & 1)
                cps.append(pltpu.make_async_remote_copy(
                    src_ref=ins[j].at[2 * px + py], dst_ref=outs[j].at[k - 1], send_sem=send_sems.at[3 * j + k - 1],
                    recv_sem=recv_sems.at[3 * j + k - 1], device_id=(px, py, cc), device_id_type=MESH))
        return cps

    def start(ins, outs, scr):
        for cp in copies(ins, outs, scr):
            cp.start()

    def finish(ins, outs, scr):
        for cp in copies(ins, outs, scr):
            cp.wait()

    return _Rider(arrs, [jax.ShapeDtypeStruct((N_CHIP - 1,) + a.shape[1:], a.dtype) for a in arrs],
                  [pltpu.SemaphoreType.DMA((3 * n,)), pltpu.SemaphoreType.DMA((3 * n,))], start, finish)


COL_SHARDED = ("w_in", "w_br_pool", "w_br_attn", "w_br_conv", "w_ff1")
ROW_SHARDED = ("w_o", "w_ff2")
BIG = COL_SHARDED + ROW_SHARDED
SMALL = ("b_ada", "b_gate", "w_pool", "pool_scale", "rel_bias", "conv_w", "conv_b", "conv_ln_g", "conv_ln_b",
         "ln_mix_g", "ln_mix_b", "b_ff1", "b_ff2", "ln_ff_g", "ln_ff_b")
PACK_W = 1024


def _pack(parts):
    rows = []
    for a in parts:
        flat = a.reshape(-1)
        n = -(-flat.shape[0] // PACK_W) * PACK_W
        rows.append(jnp.pad(flat, (0, n - flat.shape[0])).reshape(-1, PACK_W))
    out = jnp.concatenate(rows, axis=0)
    r = -(-out.shape[0] // 8) * 8
    return jnp.pad(out, ((0, r - out.shape[0]), (0, 0)))


def _unpack(packed, shapes):
    out, r0 = [], 0
    for shp in shapes:
        size = int(np.prod(shp))
        nr = -(-size // PACK_W)
        out.append(packed[r0:r0 + nr].reshape(-1)[:size].reshape(shp))
        r0 += nr
    return out


def _hosted(fn, hook, *args, **kw):
    if hook is None:
        return fn(*args, **kw)
    res, rider_out = fn(*args, rider=hook[0], **kw)
    hook[1](rider_out)
    return res


def _layer_fwd(l, x, mod, W, P, hooks=None):
    hooks = hooks or {}
    s = x.shape[0]
    sh_m, sc_m, g_m, sh_f, sc_f, g_f = [mod[l:l + 1, D_MODEL * j:D_MODEL * (j + 1)] for j in range(6)]
    n = lambda t: f"{t}{l}"
    w_in = W["w_in"][l]
    u = _ln_mod(x, sc_m, sh_m, n("ln_mod_mix"))
    zp = _mm(u, w_in, "nt", tm=s, tn=256, out_dtype=F32, name=n("z_pool"), b_col0=0, n_out=D_POOL)
    qkv = _mm(u, w_in, "nt", tm=s, tn=256, out_dtype=BF16, name=n("z_qkv"), b_col0=OFF_QKV // 256, n_out=3 * D_ATTN)
    zc = _mm(u, w_in, "nt", tm=s, tn=256, out_dtype=F32, name=n("z_conv"), b_col0=OFF_CONV // 256, n_out=2 * D_CONV)
    zg = _hosted(_mm, hooks.get("z_gate"), u, w_in, "nt", tm=min(2048, s), tn=768, out_dtype=BF16, name=n("z_gate"),
                 b_col0=OFF_GATE // 768, n_out=3 * D_MODEL)

    p, feat_pool = _pool_fwd(zp, P["wp_bd"][l], P["pool_scale"][l], n("pool_fwd"))
    bias = _bias_block(P["rel_bias"][l], n("bias_block"))
    o, probs = _hosted(_attn_fwd, hooks.get("attn"), qkv, bias, n("attn_fwd"))
    cv, feat_conv = _conv_fwd(zc, P["conv_w"][l], P["conv_b"][l], P["conv_ln_g"][l], P["conv_ln_b"][l], n("conv_fwd"))

    tmb = min(1024, s)
    y_pool = _mm(feat_pool, W["w_br_pool"][l], "nt", tm=tmb, tn=1024, out_dtype=BF16, name=n("y_pool"))
    y_attn = _mm(o, W["w_br_attn"][l], "nt", tm=tmb, tn=1024, out_dtype=BF16, name=n("y_attn"))
    y_conv = _mm(feat_conv, W["w_br_conv"][l], "nt", tm=tmb, tn=1024, out_dtype=BF16, name=n("y_conv"))
    ys = (y_pool, y_attn, y_conv)
    merged = _merge(zg, P["b_gate"][l], ys, n("merge"))
    mix, x1 = _mm_resid_ln(merged, W["w_o"][l], None, x, g_m, P["ln_mix_g"][l], P["ln_mix_b"][l], n("mix_out"))

    u2 = _ln_mod(x1, sc_f, sh_f, n("ln_mod_ff"))
    hpre, hid = _hosted(_ff_hidden, hooks.get("ff1"), u2, W["w_ff1"][l], P["b_ff1"][l], n("ff1"))
    ff, x2 = _hosted(_mm_resid_ln, hooks.get("ff2"), hid, W["w_ff2"][l], P["b_ff2"][l], x1, g_f, P["ln_ff_g"][l],
                     P["ln_ff_b"][l], n("ff2"))
    saved = dict(x=x, u=u, zp=zp, qkv=qkv, zc=zc, zg=zg, p=p, feat_pool=feat_pool, probs=probs, o=o, cv=cv,
                 feat_conv=feat_conv, ys=ys, merged=merged, mix=mix, x1=x1, u2=u2, hpre=hpre, hid=hid, ff=ff)
    return x2, saved


def _layer_bwd(l, dx2, mod, W, P, A, hooks=None, tgt=None, nxt=None):
    hooks = hooks or {}
    sh_m, sc_m, g_m, sh_f, sc_f, g_f = [mod[l:l + 1, D_MODEL * j:D_MODEL * (j + 1)] for j in range(6)]
    n = lambda t: f"{t}{l}"
    gw, gs = {}, {}

    if isinstance(dx2, tuple):
        dres, dff, gs["ln_ff_g"], gs["ln_ff_b"], dg_f, gs["b_ff2"] = dx2
    else:
        dres, dff, gs["ln_ff_g"], gs["ln_ff_b"], dg_f, gs["b_ff2"], *loss_part = _resid_ln_bwd(
            dx2, A["x1"], A["ff"], g_f, P["ln_ff_g"][l], n("resid_ln_ff_bwd"), tgt=tgt)
    s = dres.shape[0]
    tmb = min(1024, s)
    gw["w_ff2"] = _mm(A["hid"], dff, "tn", tm=512, tn=1024, out_dtype=BF16, name=n("dw_ff2"), split_n=512)
    hook = hooks["ff_hidden_bwd"](gw) if "ff_hidden_bwd" in hooks else None
    dhpre, gs["b_ff1"] = _hosted(_ff_hidden_bwd, hook, dff, W["w_ff2"][l], A["hpre"], n("ff_hidden_bwd"))
    gw["w_ff1"] = _mm(dhpre, A["u2"], "tn", tm=512, tn=1024, out_dtype=BF16, name=n("dw_ff1"), split_n=512)

    hook = hooks["du_ff"](gw) if "du_ff" in hooks else None
    dres, dmix, dsc_f, dsh_f, gs["ln_mix_g"], gs["ln_mix_b"], dg_m, _ = _hosted(
        _mm_ln_mod_bwd, hook, dhpre, W["w_ff1"][l], A["x1"], sc_f, dres, n("du_ff"),
        nxt=(A["x"], A["mix"], g_m, P["ln_mix_g"][l]))
    gw["w_o"] = _mm(A["merged"], dmix, "tn", tm=512, tn=1024, out_dtype=BF16, name=n("dw_o"), split_n=512)
    dmerged = _mm(dmix, W["w_o"][l], "nt", tm=tmb, tn=1024, out_dtype=F32, name=n("d_merged"))
    dy_pool, dy_attn, dy_conv, dzg, gs["b_gate"] = _merge_bwd(dmerged, A["zg"], P["b_gate"][l], A["ys"], n("merge_bwd"))

    gw["w_br_pool"] = _mm(dy_pool, A["feat_pool"], "tn", tm=512, tn=256, out_dtype=BF16, name=n("dw_br_pool"),
                          split_n=128)
    gw["w_br_attn"] = _mm(dy_attn, A["o"], "tn", tm=512, tn=512, out_dtype=BF16, name=n("dw_br_attn"), split_n=256)
    gw["w_br_conv"] = _mm(dy_conv, A["feat_conv"], "tn", tm=512, tn=256, out_dtype=BF16, name=n("dw_br_conv"),
                          split_n=128)
    dfeat_pool = _mm(dy_pool, W["w_br_pool"][l], "nn", tm=tmb, tn=256, out_dtype=F32, name=n("d_feat_pool"))
    do = _mm(dy_attn, W["w_br_attn"][l], "nn", tm=tmb, tn=512, out_dtype=BF16, name=n("d_attn_out"))
    dfeat_conv = _mm(dy_conv, W["w_br_conv"][l], "nn", tm=tmb, tn=256, out_dtype=F32, name=n("d_feat_conv"))

    dzp, dwp_bd, gs["pool_scale"] = _pool_bwd(dfeat_pool, A["p"], P["wp_bd"][l], P["pool_scale"][l], n("pool_bwd"))
    gs["w_pool"] = jnp.stack([dwp_bd[POOL_GROUP * g:POOL_GROUP * (g + 1), POOL_GROUP * g:POOL_GROUP * (g + 1)]
                              for g in range(len(POOL_WINDOWS))])
    hook = hooks["attn"](gw) if "attn" in hooks else None
    dq, dk, dv, ds_acc = _hosted(_attn_bwd, hook, A["qkv"], do, A["probs"], n("attn_bwd"))
    gs["rel_bias"] = _bias_block_bwd(ds_acc, n("bias_block_bwd"))
    dzc, dcw, gs["conv_b"], gs["conv_ln_g"], gs["conv_ln_b"] = _conv_bwd(
        dfeat_conv, A["cv"], A["zc"], P["conv_w"][l], P["conv_ln_g"][l], P["conv_ln_b"][l], n("conv_bwd"))
    gs["conv_w"] = dcw[:CONV_WIDTH]

    dz = [dzp, dq, dk, dv, dzc, dzg]
    gw["w_in"] = _dw_segments(dz, A["u"], n("dw_in"))
    hook = hooks["du_mix"](gw) if "du_mix" in hooks else None
    res = _hosted(_mm_ln_mod_bwd, hook, dz, W["w_in"][l], A["x"], sc_m, dres, n("du_mix"), nxt=nxt)
    if nxt is None:
        dx, dsc_m, dsh_m = res
    else:
        dx, dsc_m, dsh_m = (res[0], res[1], *res[4:]), res[2], res[3]
    dmod = jnp.concatenate([dsh_m, dsc_m, dg_m, dsh_f, dsc_f, dg_f], axis=1)
    return (dx, gw, gs, dmod) if tgt is None else (dx, gw, gs, dmod, loss_part[0])


def _small_shapes():
    return {"b_ada": (6 * D_MODEL,), "b_gate": (3 * D_MODEL,), "w_pool": (4, POOL_GROUP, POOL_GROUP),
            "pool_scale": (D_POOL,), "rel_bias": (N_HEADS, N_REL), "conv_w": (CONV_WIDTH, D_CONV),
            "conv_b": (D_CONV,), "conv_ln_g": (D_CONV,), "conv_ln_b": (D_CONV,), "ln_mix_g": (D_MODEL,),
            "ln_mix_b": (D_MODEL,), "b_ff1": (D_FF,), "b_ff2": (D_MODEL,), "ln_ff_g": (D_MODEL,), "ln_ff_b": (D_MODEL,)}


def kernel(x, c, w_ada, b_ada, w_in, b_gate, w_pool, pool_scale, rel_bias, conv_w, conv_b, conv_ln_g, conv_ln_b, w_br_pool, w_br_attn, w_br_conv, w_o, ln_mix_g, ln_mix_b, w_ff1, b_ff1, w_ff2, b_ff2, ln_ff_g, ln_ff_b, loss_target, m_w_ada, m_b_ada, m_w_in, m_b_gate, m_w_pool, m_pool_scale, m_rel_bias, m_conv_w, m_conv_b, m_conv_ln_g, m_conv_ln_b, m_w_br_pool, m_w_br_attn, m_w_br_conv, m_w_o, m_ln_mix_g, m_ln_mix_b, m_w_ff1, m_b_ff1, m_w_ff2, m_b_ff2, m_ln_ff_g, m_ln_ff_b, v_w_ada, v_b_ada, v_w_in, v_b_gate, v_w_pool, v_pool_scale, v_rel_bias, v_conv_w, v_conv_b, v_conv_ln_g, v_conv_ln_b, v_w_br_pool, v_w_br_attn, v_w_br_conv, v_w_o, v_ln_mix_g, v_ln_mix_b, v_w_ff1, v_b_ff1, v_w_ff2, v_b_ff2, v_ln_ff_g, v_ln_ff_b):
    env = dict(locals())
    xi, yi, ci = _me()
    chip = 2 * xi + yi
    me = 4 * xi + 2 * yi + ci
    xs = x[0]
    tgt = loss_target[0]
    L = DEPTH

    first = _allgather_small(jnp.concatenate([c.reshape(8, 128), _pack([conv_w]).reshape(-1, 128)]), "gather_c_conv_w")
    c_all = first[:, :8].reshape(N_DEV, D_MODEL)
    ada_cols = w_ada.shape[2]
    b_ada_sh = lax.dynamic_slice_in_dim(b_ada, chip * ada_cols, ada_cols, axis=1).reshape(L, 1, ada_cols)
    mod_part = _mod_fwd(c_all, w_ada, b_ada_sh, "mod_fwd")
    mod_g = _allgather_small(mod_part.reshape(-1, 128), "gather_mod").reshape(N_CHIP, 2, L, N_DEV, ada_cols)[:, 0]
    mod_all = jnp.transpose(mod_g, (1, 2, 0, 3)).reshape(L, N_DEV, 6 * D_MODEL)
    mod = lax.dynamic_index_in_dim(mod_all, me, axis=1, keepdims=False)

    W = {k: [None] * L for k in BIG}

    def weight_gather(*items):
        shards = [(jnp.swapaxes(env[k][l], 0, 1) if k in COL_SHARDED else env[k][l]).astype(BF16) for k, l in items]
        shards = [a.reshape(2, a.shape[0] // 2, a.shape[1]) for a in shards]

        def done(outs):
            for (k, l), g in zip(items, outs):
                W[k][l] = g.reshape(-1, g.shape[-1])

        return _gather_rider(shards), done

    branch = lambda l: [(k, l) for k in ("w_br_pool", "w_br_attn", "w_br_conv", "w_o")]
    rider, done = weight_gather(("w_in", 0))
    done(_run_rider(rider, "gather_w_in0"))
    fwd_hooks = [{"z_gate": weight_gather(*branch(0)), "attn": weight_gather(("w_ff1", 0), ("w_in", 1)),
                  "ff1": weight_gather(("w_ff2", 0)), "ff2": weight_gather(*branch(1))},
                 {"attn": weight_gather(("w_ff1", 1)), "ff1": weight_gather(("w_ff2", 1))}]

    P = {k: env[k] for k in ("rel_bias", "conv_w")}
    for k in ("b_gate", "pool_scale", "conv_b", "conv_ln_g", "conv_ln_b", "ln_mix_g", "ln_mix_b", "b_ff1", "b_ff2",
              "ln_ff_g", "ln_ff_b"):
        P[k] = env[k].reshape(L, 1, -1)
    n_cw = conv_w.size
    cw = first[:, 8:].reshape(N_CHIP, 2, -1)[:, 0, :n_cw].reshape(N_CHIP, L, CONV_WIDTH, D_CONV // N_CHIP)
    P["conv_w"] = jnp.transpose(cw, (1, 2, 0, 3)).reshape(L, CONV_WIDTH, D_CONV)
    wp_bd = jnp.zeros((L, D_POOL, D_POOL), F32)
    for g in range(len(POOL_WINDOWS)):
        sl = slice(POOL_GROUP * g, POOL_GROUP * (g + 1))
        wp_bd = wp_bd.at[:, sl, sl].set(w_pool[:, g])
    P["wp_bd"] = wp_bd.astype(BF16)

    acts = []
    h = xs
    for l in range(L):
        h, saved = _layer_fwd(l, h, mod, W, P, fwd_hooks[l])
        acts.append(saved)

    place = jnp.stack([ci, chip, chip ^ 1, chip ^ 2, chip ^ 3]).astype(jnp.int32)
    scattered = {}

    def grad_scatter(items, tag):
        dws = [dw for _, _, dw in items]
        got = _sibling_send(dws, f"swap_blocks_{tag}", other_half=True)
        both = [_sum_cores(a, b, place, f"sum_cores_{k}{l}") for (k, l, _), a, b in zip(items, dws, got)]
        both = [hh.reshape(N_CHIP, -1, hh.shape[-1]) for hh in both]

        def done(outs):
            for (k, l, _), hh, r in zip(items, both, outs):
                scattered[(k, l)] = (hh, r)

        return _scatter_rider(both), done

    def scatter_hook(names, l, host):
        return lambda gw: grad_scatter([(k, l, gw[k]) for k in names], f"{host}{l}")

    gws, gss, dmods = [None] * L, [None] * L, [None] * L
    dh = h
    for l in reversed(range(L)):
        hooks = {"ff_hidden_bwd": scatter_hook(("w_ff2",), l, "ff_hidden_bwd"),
                 "du_ff": scatter_hook(("w_ff1",), l, "du_ff"),
                 "attn": scatter_hook(("w_o", "w_br_pool", "w_br_attn", "w_br_conv"), l, "attn_bwd"),
                 "du_mix": scatter_hook(("w_in",), l, "du_mix")}
        below = None
        if l > 0:
            below = (acts[l - 1]["x1"], acts[l - 1]["ff"], mod[l - 1:l, 5 * D_MODEL:], P["ln_ff_g"][l - 1])
        if l == L - 1:
            dh, gws[l], gss[l], dmods[l], loss_part = _layer_bwd(l, dh, mod, W, P, acts[l], hooks, tgt=tgt, nxt=below)
        else:
            dh, gws[l], gss[l], dmods[l] = _layer_bwd(l, dh, mod, W, P, acts[l], hooks, nxt=below)
    grad_x = dh[None]
    loss = lax.psum(loss_part[0, 0], ("x", "y", "c"))

    reduced = [[_sum_chips(*scattered[(k, l)], place, f"sum_chips_{k}{l}") for l in range(L)] for k in BIG]
    flat_reduced = [t for per_weight in reduced for t in per_weight]

    shapes = _small_shapes()
    small_names = [k for k in SMALL if k != "b_ada"]
    dmod_own = jnp.concatenate(dmods, axis=0)
    pack = _pack([dmod_own] + [jnp.stack([gss[l][k].reshape(shapes[k]) for l in range(L)]) for k in small_names])
    last = _run_rider(_join_riders(_sibling_rider(flat_reduced), _allgather_rider(pack.reshape(-1, 128))),
                      "swap_reduced_gather_small")
    flat_other, g_all = last[:-1], last[-1].reshape(N_DEV, -1, PACK_W)

    out = {}
    for j, k in enumerate(BIG):
        own, other = reduced[j], flat_other[L * j:L * (j + 1)]
        if k == "w_in":
            t = lambda a: jnp.swapaxes(a, 1, 2)
            res = _adamw_halves(t(env[k]), t(env["m_" + k]), t(env["v_" + k]), own, other, place, "cols", f"adamw_{k}")
            res = [t(a) for a in res]
        else:
            if k in COL_SHARDED:
                own, other = [a.T for a in own], [a.T for a in other]
            res = _adamw_halves(env[k], env["m_" + k], env["v_" + k], own, other, place,
                                "rows" if k in COL_SHARDED else "cols", f"adamw_{k}")
        out[k] = tuple(res)

    dmod_all = g_all[:, :L * 6].reshape(N_DEV, L, 6 * D_MODEL)
    dmod_sh = jnp.transpose(lax.dynamic_slice_in_dim(dmod_all, chip * ada_cols, ada_cols, axis=2), (1, 0, 2))
    g_ada = _mod_bwd(c_all, dmod_sh, "mod_bwd")
    g_, d_, m_, v_ = _adamw(w_ada.reshape(-1, ada_cols), m_w_ada.reshape(-1, ada_cols), v_w_ada.reshape(-1, ada_cols),
                            [g_ada.reshape(-1, ada_cols)], "adamw_w_ada")
    out["w_ada"] = tuple(a.reshape(w_ada.shape) for a in (g_, d_, m_, v_))

    def small_pack(prefix):
        parts = [env[prefix + "b_ada"]]
        for k in small_names:
            a = env[prefix + k]
            if k == "conv_w":
                a = jnp.zeros((L,) + shapes[k], F32)
            parts.append(a)
        return _pack(parts)

    gp, dp, mp, vp = _adamw_small(small_pack(""), small_pack("m_"), small_pack("v_"), g_all, "adamw_small")
    full_shapes = [(L,) + shapes["b_ada"]] + [(L,) + shapes[k] for k in small_names]
    for tag, packed in (("g", gp), ("d", dp), ("m", mp), ("v", vp)):
        for k, a in zip(["b_ada"] + small_names, _unpack(packed, full_shapes)):
            out.setdefault(k, {})
            out[k][tag] = a
    g_cw_full = out["conv_w"]["g"]
    cw_cols = D_CONV // N_CHIP
    g_cw = lax.dynamic_slice_in_dim(g_cw_full, chip * cw_cols, cw_cols, axis=2)
    pad_rows = lambda a: jnp.pad(a.reshape(L * CONV_WIDTH, cw_cols), ((0, 2), (0, 0)))
    g_, d_, m_, v_ = _adamw(pad_rows(conv_w), pad_rows(m_conv_w), pad_rows(v_conv_w), [pad_rows(g_cw)], "adamw_conv_w")
    out["conv_w"] = tuple(a[:L * CONV_WIDTH].reshape(L, CONV_WIDTH, cw_cols) for a in (g_, d_, m_, v_))

    names = ["w_ada", "b_ada", "w_in", "b_gate", "w_pool", "pool_scale", "rel_bias", "conv_w", "conv_b", "conv_ln_g",
             "conv_ln_b", "w_br_pool", "w_br_attn", "w_br_conv", "w_o", "ln_mix_g", "ln_mix_b", "w_ff1", "b_ff1",
             "w_ff2", "b_ff2", "ln_ff_g", "ln_ff_b"]

    def pick(k, i):
        o = out[k]
        return o[i] if isinstance(o, tuple) else o["gdmv"[i]].reshape(env[k].shape)

    return (loss, grad_x, *[pick(k, 0) for k in names], *[pick(k, 1) for k in names],
            *[pick(k, 2) for k in names], *[pick(k, 3) for k in names])
```

```python
import jax
import jax.numpy as jnp
import numpy as np
from jax import lax
from jax.experimental import pallas as pl
from jax.experimental.pallas import tpu as pltpu

F32 = jnp.float32
BF16 = jnp.bfloat16

D_MODEL = 1024
DEPTH = 2
CHUNK = 64
POOL_WINDOWS = (2, 4, 8, 16)
POOL_GROUP = 64
D_POOL = 256
N_HEADS = 8
HEAD_DIM = 64
D_ATTN = 512
N_PREV_CHUNKS = 8
REL_CLIP = 128
N_REL = 2 * REL_CLIP + 1
D_CONV = 256
CONV_WIDTH = 31
D_FF = 4 * D_MODEL
D_IN = 5376
OFF_POOL, OFF_QKV, OFF_CONV, OFF_GATE = 0, 256, 1792, 2304
ALPHA = (2.0 * DEPTH) ** 0.25
LN_EPS = 1e-5
NEG_INF = -1e30
ADAM_LR, ADAM_B1, ADAM_B2, ADAM_EPS, ADAM_WD, ADAM_STEP = 0.001, 0.9, 0.999, 1e-08, 0.01, 10

N_DEV = 8
N_CHIP = 4
MESH = pl.DeviceIdType.MESH

QB = 2 * CHUNK
KPAD = N_PREV_CHUNKS * CHUNK
KW = QB + KPAD
SKEW_W = 768

VMEM_LIMIT = 56 * 1024 * 1024


def _cparams(**kw):
    return pltpu.CompilerParams(vmem_limit_bytes=VMEM_LIMIT, **kw)


def _full(shape):
    n = len(shape)
    return pl.BlockSpec(shape, lambda *_: (0,) * n)


_DIMS = {"nn": (((1,), (0,)), ((), ())), "nt": (((1,), (1,)), ((), ())), "tn": (((0,), (0,)), ((), ()))}


def _relu2(t):
    r = jnp.maximum(t, 0.0)
    return r * r


def _mm(a, b, mode, *, tm, tn, out_dtype, name, b_col0=0, n_out=None, bias=None, split_n=0, rider=None):
    if mode == "tn":
        k, m = a.shape
        n = b.shape[1] if n_out is None else n_out
        a_spec = pl.BlockSpec((k, tm), lambda i, j: (0, i))
        b_spec = pl.BlockSpec((k, tn), lambda i, j: (0, j + b_col0))
    elif mode == "nn":
        m, k = a.shape
        n = b.shape[1] if n_out is None else n_out
        a_spec = pl.BlockSpec((tm, k), lambda i, j: (i, 0))
        b_spec = pl.BlockSpec((k, tn), lambda i, j: (0, j + b_col0))
    else:
        m, k = a.shape
        n = b.shape[0] if n_out is None else n_out
        a_spec = pl.BlockSpec((tm, k), lambda i, j: (i, 0))
        b_spec = pl.BlockSpec((tn, k), lambda i, j: (j + b_col0, 0))
    assert m % tm == 0 and n % tn == 0, (name, m, n, tm, tn)
    dims = _DIMS[mode]

    def body(*refs):
        if bias is None:
            a_ref, b_ref, o_ref = refs
        else:
            a_ref, b_ref, bias_ref, o_ref = refs
        acc = lax.dot_general(a_ref[...].astype(BF16), b_ref[...].astype(BF16), dims, preferred_element_type=F32)
        if bias is not None:
            acc = acc + bias_ref[...]
        if split_n:
            for c in range(tn // split_n):
                o_ref[c] = acc[:, c * split_n:(c + 1) * split_n].astype(out_dtype)
        else:
            o_ref[...] = acc.astype(out_dtype)

    in_specs = [a_spec, b_spec]
    args = [a, b]
    if bias is not None:
        in_specs.append(pl.BlockSpec((1, tn), lambda i, j: (0, j)))
        args.append(bias)
    if split_n:
        out_spec = pl.BlockSpec((tn // split_n, tm, split_n), lambda i, j: (j, i, 0))
        out_shape = jax.ShapeDtypeStruct((n // split_n, m, split_n), out_dtype)
    else:
        out_spec = pl.BlockSpec((tm, tn), lambda i, j: (i, j))
        out_shape = jax.ShapeDtypeStruct((m, n), out_dtype)
    res = _call(body, name=name, grid=(m // tm, n // tn), in_specs=in_specs, out_specs=[out_spec],
                out_shape=[out_shape], scratch_shapes=[], args=args, rider=rider)
    return res[0] if rider is None else (res[0][0], res[1])


def _ln_hat(x):
    mu = jnp.mean(x, axis=-1, keepdims=True)
    xc = x - mu
    var = jnp.mean(xc * xc, axis=-1, keepdims=True)
    rstd = lax.rsqrt(var + LN_EPS)
    return xc * rstd, rstd


def _ln_hat_bwd(dhat, xhat, rstd):
    m1 = jnp.mean(dhat, axis=-1, keepdims=True)
    m2 = jnp.mean(dhat * xhat, axis=-1, keepdims=True)
    return rstd * (dhat - m1 - xhat * m2)


def _row_tile(s):
    return min(512, s)


def _acc_rows(ref, val, first):
    @pl.when(first)
    def _():
        ref[...] = jnp.zeros_like(ref)
    ref[...] += jnp.sum(val, axis=0, keepdims=True)


def _ln_mod(x, sc, sh, name):
    s, d = x.shape
    tm = _row_tile(s)

    def body(x_ref, sc_ref, sh_ref, u_ref):
        xhat, _ = _ln_hat(x_ref[...])
        u_ref[...] = (xhat * (1.0 + sc_ref[...]) + sh_ref[...]).astype(BF16)

    row = pl.BlockSpec((tm, d), lambda i: (i, 0))
    vec = pl.BlockSpec((1, d), lambda i: (0, 0))
    return pl.pallas_call(body, grid=(s // tm,), in_specs=[row, vec, vec], out_specs=row,
                          out_shape=jax.ShapeDtypeStruct((s, d), BF16), name=name, compiler_params=_cparams())(x, sc, sh)


def _resid_bwd_tile(dxo, x, f, g, gam):
    rhat, rstd = _ln_hat(ALPHA * x + g * f)
    dr = _ln_hat_bwd(dxo * gam, rhat, rstd)
    return ALPHA * dr, g * dr, dxo * rhat, dr * f


def _mm_ln_mod_bwd(a, b, x, sc, dres, name, rider=None, nxt=None):
    segs = list(a) if isinstance(a, (list, tuple)) else [a]
    s = segs[0].shape[0]
    k, d = b.shape
    assert sum(t.shape[1] for t in segs) == k
    tm = min(512 if k <= 4096 and nxt is None else 256, s)
    ns = len(segs)

    def body(*refs):
        seg_refs = refs[:ns]
        if nxt is None:
            b_ref, x_ref, sc_ref, dres_ref, dx_ref, dsc_ref, dsh_ref = refs[ns:]
        else:
            (b_ref, x_ref, sc_ref, dres_ref, xp_ref, fp_ref, gp_ref, gamp_ref,
             dresp_ref, dfp_ref, dsc_ref, dsh_ref, dgam_ref, dbet_ref, dg_ref, dbias_ref) = refs[ns:]
        first = pl.program_id(0) == 0
        duv, r0 = None, 0
        for seg_ref in seg_refs:
            w = seg_ref.shape[1]
            part = jnp.dot(seg_ref[...], b_ref[r0:r0 + w, :], preferred_element_type=F32)
            duv = part if duv is None else duv + part
            r0 += w
        xhat, rstd = _ln_hat(x_ref[...])
        dxv = dres_ref[...] + _ln_hat_bwd(duv * (1.0 + sc_ref[...]), xhat, rstd)
        _acc_rows(dsc_ref, duv * xhat, first)
        _acc_rows(dsh_ref, duv, first)
        if nxt is None:
            dx_ref[...] = dxv
        else:
            dresp, dfp, t_gam, t_g = _resid_bwd_tile(dxv, xp_ref[...], fp_ref[...], gp_ref[...], gamp_ref[...])
            dresp_ref[...] = dresp
            dfp_ref[...] = dfp.astype(BF16)
            _acc_rows(dgam_ref, t_gam, first)
            _acc_rows(dbet_ref, dxv, first)
            _acc_rows(dg_ref, t_g, first)
            _acc_rows(dbias_ref, dfp, first)

    row = pl.BlockSpec((tm, d), lambda i: (i, 0))
    vec = pl.BlockSpec((1, d), lambda i: (0, 0))
    vs = jax.ShapeDtypeStruct((1, d), F32)
    rows = jax.ShapeDtypeStruct((s, d), F32)
    in_specs = [pl.BlockSpec((tm, t.shape[1]), lambda i: (i, 0)) for t in segs] + [_full((k, d)), row, vec, row]
    args = (*segs, b, x, sc, dres)
    if nxt is None:
        out_specs, out_shape = [row, vec, vec], [rows, vs, vs]
    else:
        in_specs += [row, row, vec, vec]
        args += tuple(nxt)
        out_specs = [row, row] + [vec] * 6
        out_shape = [rows, jax.ShapeDtypeStruct((s, d), BF16)] + [vs] * 6
    res = _call(body, name=name, grid=(s // tm,), in_specs=in_specs, out_specs=out_specs, out_shape=out_shape,
                scratch_shapes=[], args=args, rider=rider)
    return tuple(res) if rider is None else (tuple(res[0]), res[1])


def _dw_segments(segs, u, name):
    s, d = u.shape
    tw = 256
    tiles = [t.shape[1] // tw for t in segs]
    starts = [sum(tiles[:j]) for j in range(len(segs))]
    ns = len(segs)

    def body(*refs):
        seg_refs, u_ref, o_ref = refs[:ns], refs[ns], refs[ns + 1]
        i = pl.program_id(0)
        for seg_ref, t0, nt in zip(seg_refs, starts, tiles):
            @pl.when((i >= t0) & (i < t0 + nt))
            def _(seg_ref=seg_ref):
                acc = lax.dot_general(seg_ref[...], u_ref[...], _DIMS["tn"], preferred_element_type=F32)
                o_ref[0] = acc[:, :d // 2].astype(BF16)
                o_ref[1] = acc[:, d // 2:].astype(BF16)

    def seg_spec(t0, nt):
        return pl.BlockSpec((s, tw), lambda i: (0, jnp.clip(i - t0, 0, nt - 1)))

    return pl.pallas_call(
        body, grid=(sum(tiles),), in_specs=[seg_spec(t0, nt) for t0, nt in zip(starts, tiles)] + [_full((s, d))],
        out_specs=pl.BlockSpec((2, tw, d // 2), lambda i: (0, i, 0)),
        out_shape=jax.ShapeDtypeStruct((2, sum(tiles) * tw, d // 2), BF16), name=name, compiler_params=_cparams(),
    )(*segs, u)


def _mm_resid_ln(a, b, bias, x, g, gam, bet, name, rider=None, mod_next=None):
    s, k = a.shape
    d = b.shape[1]
    tm = min(512, s)
    nb, nm = int(bias is not None), 2 * int(mod_next is not None)

    def body(*refs):
        a_ref, b_ref = refs[:2]
        x_ref, g_ref, gam_ref, bet_ref = refs[2 + nb:6 + nb]
        f_ref, o_ref = refs[6 + nb + nm:8 + nb + nm]
        f = jnp.dot(a_ref[...], b_ref[...], preferred_element_type=F32)
        if bias is not None:
            f = f + refs[2][...]
        f_ref[...] = f
        rhat, _ = _ln_hat(ALPHA * x_ref[...] + g_ref[...] * f)
        y = rhat * gam_ref[...] + bet_ref[...]
        o_ref[...] = y
        if mod_next is not None:
            sc_ref, sh_ref = refs[6 + nb:8 + nb]
            yhat, _ = _ln_hat(y)
            refs[8 + nb + nm][...] = (yhat * (1.0 + sc_ref[...]) + sh_ref[...]).astype(BF16)

    row = pl.BlockSpec((tm, d), lambda i: (i, 0))
    vec = pl.BlockSpec((1, d), lambda i: (0, 0))
    in_specs = [pl.BlockSpec((tm, k), lambda i: (i, 0)), _full((k, d))] + [vec] * nb + [row, vec, vec, vec] + [vec] * nm
    args = [a, b] + ([bias] if nb else []) + [x, g, gam, bet] + (list(mod_next) if nm else [])
    sh = jax.ShapeDtypeStruct((s, d), F32)
    out_specs, out_shape = [row, row], [sh, sh]
    if nm:
        out_specs, out_shape = out_specs + [row], out_shape + [jax.ShapeDtypeStruct((s, d), BF16)]
    res = _call(body, name=name, grid=(s // tm,), in_specs=in_specs, out_specs=out_specs, out_shape=out_shape,
                scratch_shapes=[], args=args, rider=rider)
    return tuple(res) if rider is None else (tuple(res[0]), res[1])


def _resid_ln_bwd(dxo, x, f, g, gam, name, tgt=None):
    s, d = x.shape
    tm = _row_tile(s)
    n = s // tm

    def body(*refs):
        if tgt is None:
            dxo_ref, x_ref, f_ref, g_ref, gam_ref, dres_ref, df_ref, dgam_ref, dbet_ref, dg_ref, dbias_ref = refs
            dxov = dxo_ref[...]
        else:
            (dxo_ref, t_ref, x_ref, f_ref, g_ref, gam_ref, dres_ref, df_ref, dgam_ref, dbet_ref, dg_ref, dbias_ref,
             loss_ref, sq_ref) = refs
            err = dxo_ref[...] - t_ref[...]
            dxov = err * (1.0 / d)
            _acc_rows(sq_ref, err * err, pl.program_id(0) == 0)

            @pl.when(pl.program_id(0) == n - 1)
            def _():
                tot = jnp.sum(sq_ref[...], axis=1, keepdims=True) * (0.5 / d)
                loss_ref[...] = jnp.broadcast_to(tot, (1, 128))

        first = pl.program_id(0) == 0
        dres, dfv, t_gam, t_g = _resid_bwd_tile(dxov, x_ref[...], f_ref[...], g_ref[...], gam_ref[...])
        dres_ref[...] = dres
        df_ref[...] = dfv.astype(BF16)
        _acc_rows(dgam_ref, t_gam, first)
        _acc_rows(dbet_ref, dxov, first)
        _acc_rows(dg_ref, t_g, first)
        _acc_rows(dbias_ref, dfv, first)

    row = pl.BlockSpec((tm, d), lambda i: (i, 0))
    vec = pl.BlockSpec((1, d), lambda i: (0, 0))
    vs = jax.ShapeDtypeStruct((1, d), F32)
    out_specs = [row, row, vec, vec, vec, vec]
    out_shape = [jax.ShapeDtypeStruct((s, d), F32), jax.ShapeDtypeStruct((s, d), BF16), vs, vs, vs, vs]
    if tgt is None:
        return pl.pallas_call(body, grid=(n,), in_specs=[row, row, row, vec, vec], out_specs=out_specs,
                              out_shape=out_shape, name=name, compiler_params=_cparams())(dxo, x, f, g, gam)
    return pl.pallas_call(body, grid=(n,), in_specs=[row, row, row, row, vec, vec],
                          out_specs=out_specs + [pl.BlockSpec((1, 128), lambda i: (0, 0))],
                          out_shape=out_shape + [jax.ShapeDtypeStruct((1, 128), F32)],
                          scratch_shapes=[pltpu.VMEM((1, d), F32)], name=name,
                          compiler_params=_cparams())(dxo, tgt, x, f, g, gam)


POOL_HALO = 16
POOL_ROWS = 256


def _pool_counts(r0, rows):
    t1 = (lax.broadcasted_iota(jnp.int32, (rows, 128), 0) + r0 + 1).astype(F32)
    low = lax.broadcasted_iota(jnp.int32, (rows, 128), 1) < POOL_GROUP
    wa = jnp.where(low, float(POOL_WINDOWS[0]), float(POOL_WINDOWS[1]))
    wb = jnp.where(low, float(POOL_WINDOWS[2]), float(POOL_WINDOWS[3]))
    return jnp.minimum(t1, wa), jnp.minimum(t1, wb), low


def _window_sums(win, off, rows, sign):
    def sl(j, half):
        return win[off + sign * j: off + sign * j + rows, 128 * half:128 * half + 128]
    a2 = sl(0, 0) + sl(1, 0)
    a4 = a2 + sl(2, 0) + sl(3, 0)
    a8 = sl(0, 1)
    for j in range(1, 8):
        a8 = a8 + sl(j, 1)
    a16 = a8
    for j in range(8, 16):
        a16 = a16 + sl(j, 1)
    return a2, a4, a8, a16


def _pool_fwd(zp, wp_bd, pscale, name):
    s = zp.shape[0]
    r = min(POOL_ROWS, s)

    def body(z_ref, wp_ref, sc_ref, p_ref, feat_ref, pad):
        pad[0:POOL_HALO, :] = jnp.zeros((POOL_HALO, D_POOL), F32)
        pad[POOL_HALO:, :] = z_ref[...]

        def step(i, carry):
            r0 = pl.multiple_of(i * r, r)
            win = pad[pl.ds(r0, r + POOL_HALO), :]
            a2, a4, a8, a16 = _window_sums(win, POOL_HALO, r, -1)
            ca, cb, low = _pool_counts(r0, r)
            x0 = win[POOL_HALO:, :]
            pa = jnp.where(low, a2, a4) / ca
            pb = jnp.where(low, a8, a16) / cb
            p = (jnp.concatenate([pa, pb], axis=1) - x0).astype(BF16)
            p_ref[pl.ds(r0, r), :] = p
            pw = jnp.dot(p, wp_ref[...], preferred_element_type=F32)
            feat_ref[pl.ds(r0, r), :] = (pw * sc_ref[...]).astype(BF16)
            return carry

        lax.fori_loop(0, s // r, step, 0)

    return pl.pallas_call(
        body, out_shape=[jax.ShapeDtypeStruct((s, D_POOL), BF16), jax.ShapeDtypeStruct((s, D_POOL), BF16)],
        scratch_shapes=[pltpu.VMEM((s + POOL_HALO, D_POOL), F32)], name=name, compiler_params=_cparams(),
    )(zp, wp_bd, pscale)


def _pool_bwd(dfeat, p, wp_bd, pscale, name):
    s = p.shape[0]
    r = min(POOL_ROWS, s)

    def body(df_ref, p_ref, wp_ref, sc_ref, dz_ref, dwp_ref, dsc_ref, gpad, dpbuf):
        dwp_ref[...] = jnp.zeros_like(dwp_ref)
        dsc_ref[...] = jnp.zeros_like(dsc_ref)
        gpad[s:, :] = jnp.zeros((POOL_HALO, D_POOL), F32)

        def step1(i, carry):
            r0 = pl.multiple_of(i * r, r)
            pv = p_ref[pl.ds(r0, r), :]
            dfv = df_ref[pl.ds(r0, r), :]
            pw = jnp.dot(pv, wp_ref[...], preferred_element_type=F32)
            dsc_ref[...] += jnp.sum(dfv * pw, axis=0, keepdims=True)
            dpw = (dfv * sc_ref[...]).astype(BF16)
            dwp_ref[...] += lax.dot_general(pv, dpw, _DIMS["tn"], preferred_element_type=F32)
            dp = lax.dot_general(dpw, wp_ref[...], _DIMS["nt"], preferred_element_type=F32)
            ca, cb, _ = _pool_counts(r0, r)
            gpad[pl.ds(r0, r), :] = dp / jnp.concatenate([ca, cb], axis=1)
            dpbuf[pl.ds(r0, r), :] = dp
            return carry

        lax.fori_loop(0, s // r, step1, 0)

        def step2(i, carry):
            r0 = pl.multiple_of(i * r, r)
            win = gpad[pl.ds(r0, r + POOL_HALO), :]
            a2, a4, a8, a16 = _window_sums(win, 0, r, 1)
            low = lax.broadcasted_iota(jnp.int32, (r, 128), 1) < POOL_GROUP
            acc = jnp.concatenate([jnp.where(low, a2, a4), jnp.where(low, a8, a16)], axis=1)
            dz_ref[pl.ds(r0, r), :] = (acc - dpbuf[pl.ds(r0, r), :]).astype(BF16)
            return carry

        lax.fori_loop(0, s // r, step2, 0)

    return pl.pallas_call(
        body,
        out_shape=[jax.ShapeDtypeStruct((s, D_POOL), BF16), jax.ShapeDtypeStruct((D_POOL, D_POOL), F32),
                   jax.ShapeDtypeStruct((1, D_POOL), F32)],
        scratch_shapes=[pltpu.VMEM((s + POOL_HALO, D_POOL), F32), pltpu.VMEM((s, D_POOL), F32)],
        name=name, compiler_params=_cparams(),
    )(dfeat, p, wp_bd, pscale)


def _skew_index():
    cp = lax.broadcasted_iota(jnp.int32, (SKEW_W, N_REL), 0)
    dist = jnp.where(cp < KW, KPAD - cp, KPAD + SKEW_W - cp)
    idx = jnp.clip(dist, -REL_CLIP, REL_CLIP) + REL_CLIP
    return (idx == lax.broadcasted_iota(jnp.int32, (SKEW_W, N_REL), 1)).astype(F32)


def _row_bits(b):
    return (lax.broadcasted_iota(jnp.int32, (QB, SKEW_W), 0) >> b) & 1 == 1


N_EDGE = KPAD // QB


def _bias_block(rel_bias, name):
    def body(rb_ref, o_ref):
        onehot = _skew_index()
        row0 = lax.dot_general(rb_ref[...], onehot, _DIMS["nt"], precision=lax.Precision.HIGHEST,
                               preferred_element_type=F32)
        r = lax.broadcasted_iota(jnp.int32, (QB, KW), 0)
        kk = lax.broadcasted_iota(jnp.int32, (QB, KW), 1)
        cq, ck = r // CHUNK, kk // CHUNK
        band = (ck >= cq) & (ck <= cq + N_PREV_CHUNKS)
        for h in range(N_HEADS):
            t = jnp.broadcast_to(row0[h:h + 1, :], (QB, SKEW_W))
            for b in range(7):
                t = jnp.where(_row_bits(b), pltpu.roll(t, 1 << b, 1), t)
            for e in range(N_EDGE + 1):
                o_ref[e, h] = jnp.where(band & (kk >= KPAD - e * QB), t[:, :KW], NEG_INF)

    return pl.pallas_call(body, out_shape=jax.ShapeDtypeStruct((N_EDGE + 1, N_HEADS, QB, KW), F32), name=name,
                          compiler_params=_cparams())(rel_bias)


def _bias_spec():
    return pl.BlockSpec((None, N_HEADS, QB, KW), lambda i: (jnp.minimum(i, N_EDGE), 0, 0, 0))


def _bias_block_bwd(ds_acc, name):
    def body(ds_ref, o_ref):
        sums = []
        for h in range(N_HEADS):
            t = jnp.concatenate([ds_ref[h], jnp.zeros((QB, SKEW_W - KW), F32)], axis=1)
            for b in range(7):
                t = jnp.where(_row_bits(b), pltpu.roll(t, SKEW_W - (1 << b), 1), t)
            sums.append(jnp.sum(t, axis=0, keepdims=True))
        allh = jnp.concatenate(sums, axis=0)
        o_ref[...] = jnp.dot(allh, _skew_index(), precision=lax.Precision.HIGHEST, preferred_element_type=F32)

    return pl.pallas_call(body, out_shape=jax.ShapeDtypeStruct((N_HEADS, N_REL), F32), name=name,
                          compiler_params=_cparams())(ds_acc)


def _scaled(q):
    return (q.astype(F32) * (HEAD_DIM ** -0.5)).astype(BF16)


def _probs(q, kw, bias_ref):
    sc = jnp.stack([lax.dot_general(q[:, HEAD_DIM * h:HEAD_DIM * (h + 1)], kw[:, HEAD_DIM * h:HEAD_DIM * (h + 1)],
                                    _DIMS["nt"], preferred_element_type=F32) + bias_ref[h] for h in range(N_HEADS)])
    e = jnp.exp(sc - jnp.max(sc, axis=-1, keepdims=True))
    return e * (1.0 / jnp.sum(e, axis=-1, keepdims=True))


def _load_padded_kv(qkv_hbm, kpad, vpad, sems, s):
    kpad[0:KPAD, :] = jnp.zeros((KPAD, D_ATTN), BF16)
    vpad[0:KPAD, :] = jnp.zeros((KPAD, D_ATTN), BF16)
    ck = pltpu.make_async_copy(qkv_hbm.at[:, D_ATTN:2 * D_ATTN], kpad.at[pl.ds(KPAD, s), :], sems.at[0])
    cv = pltpu.make_async_copy(qkv_hbm.at[:, 2 * D_ATTN:3 * D_ATTN], vpad.at[pl.ds(KPAD, s), :], sems.at[1])
    ck.start()
    cv.start()
    ck.wait()
    cv.wait()


def _attn_fwd(qkv, bias, name, rider=None):
    s = qkv.shape[0]

    def body(q_ref, qkv_hbm, bias_ref, o_ref, p_ref, kpad, vpad, sems):
        i = pl.program_id(0)

        @pl.when(i == 0)
        def _():
            _load_padded_kv(qkv_hbm, kpad, vpad, sems, s)

        base = pl.multiple_of(i * QB, QB)
        kw = kpad[pl.ds(base, KW), :]
        vw = vpad[pl.ds(base, KW), :]
        q = _scaled(q_ref[...])
        p = _probs(q, kw, bias_ref).astype(BF16)
        p_ref[...] = p
        outs = [jnp.dot(p[h], vw[:, HEAD_DIM * h:HEAD_DIM * (h + 1)], preferred_element_type=F32)
                for h in range(N_HEADS)]
        o_ref[...] = jnp.concatenate(outs, axis=1).astype(BF16)

    res = _call(
        body, name=name, grid=(s // QB,),
        in_specs=[pl.BlockSpec((QB, D_ATTN), lambda i: (i, 0)), pl.BlockSpec(memory_space=pl.ANY),
                  _bias_spec()],
        out_specs=[pl.BlockSpec((QB, D_ATTN), lambda i: (i, 0)), _probs_spec()],
        out_shape=[jax.ShapeDtypeStruct((s, D_ATTN), BF16), jax.ShapeDtypeStruct((N_HEADS, s, KW), BF16)],
        scratch_shapes=[pltpu.VMEM((s + KPAD, D_ATTN), BF16), pltpu.VMEM((s + KPAD, D_ATTN), BF16),
                        pltpu.SemaphoreType.DMA((2,))],
        args=(qkv, qkv, bias), rider=rider)
    return tuple(res) if rider is None else (tuple(res[0]), res[1])


def _probs_spec():
    return pl.BlockSpec((N_HEADS, QB, KW), lambda i: (0, i, 0))


def _attn_bwd(qkv, do, probs, name, rider=None):
    s = qkv.shape[0]
    n = s // QB

    def body(q_ref, qkv_hbm, do_ref, p_ref, dq_ref, dk_hbm, dv_hbm, ds_ref, kpad, vpad, dkacc, dvacc, sems):
        i = pl.program_id(0)

        @pl.when(i == 0)
        def _():
            _load_padded_kv(qkv_hbm, kpad, vpad, sems, s)
            dkacc[...] = jnp.zeros_like(dkacc)
            dvacc[...] = jnp.zeros_like(dvacc)
            ds_ref[...] = jnp.zeros_like(ds_ref)

        base = pl.multiple_of(i * QB, QB)
        kw = kpad[pl.ds(base, KW), :]
        vw = vpad[pl.ds(base, KW), :]
        q = _scaled(q_ref[...])
        dov = do_ref[...]
        heads = [slice(HEAD_DIM * h, HEAD_DIM * (h + 1)) for h in range(N_HEADS)]
        pb = p_ref[...]
        p = pb.astype(F32)
        dp = jnp.stack([lax.dot_general(dov[:, hs], vw[:, hs], _DIMS["nt"], preferred_element_type=F32) for hs in heads])
        ds = p * (dp - jnp.sum(dp * p, axis=-1, keepdims=True))
        ds_ref[...] += ds
        dsb = ds.astype(BF16)
        dvs = [lax.dot_general(pb[h], dov[:, hs], _DIMS["tn"], preferred_element_type=F32) for h, hs in enumerate(heads)]
        dqs = [jnp.dot(dsb[h], kw[:, hs], preferred_element_type=F32) for h, hs in enumerate(heads)]
        dks = [lax.dot_general(dsb[h], q[:, hs], _DIMS["tn"], preferred_element_type=F32) for h, hs in enumerate(heads)]
        dq_ref[...] = (jnp.concatenate(dqs, axis=1) * (HEAD_DIM ** -0.5)).astype(BF16)
        dkacc[pl.ds(base, KW), :] += jnp.concatenate(dks, axis=1)
        dvacc[pl.ds(base, KW), :] += jnp.concatenate(dvs, axis=1)

        @pl.when(i == n - 1)
        def _():
            def cast(j, carry):
                rows = pl.ds(pl.multiple_of(KPAD + j * 512, 512), 512)
                kpad[rows, :] = dkacc[rows, :].astype(BF16)
                vpad[rows, :] = dvacc[rows, :].astype(BF16)
                return carry

            lax.fori_loop(0, s // 512, cast, 0)
            ck = pltpu.make_async_copy(kpad.at[pl.ds(KPAD, s), :], dk_hbm, sems.at[0])
            cv = pltpu.make_async_copy(vpad.at[pl.ds(KPAD, s), :], dv_hbm, sems.at[1])
            ck.start()
            cv.start()
            ck.wait()
            cv.wait()

    blk = pl.BlockSpec((QB, D_ATTN), lambda i: (i, 0))
    acc_shape = jax.ShapeDtypeStruct((s, D_ATTN), BF16)
    return _call(
        body, name=name, grid=(n,),
        in_specs=[blk, pl.BlockSpec(memory_space=pl.ANY), blk, _probs_spec()],
        out_specs=[blk, pl.BlockSpec(memory_space=pl.ANY), pl.BlockSpec(memory_space=pl.ANY), _full((N_HEADS, QB, KW))],
        out_shape=[jax.ShapeDtypeStruct((s, D_ATTN), BF16), acc_shape, acc_shape,
                   jax.ShapeDtypeStruct((N_HEADS, QB, KW), F32)],
        scratch_shapes=[pltpu.VMEM((s + KPAD, D_ATTN), BF16), pltpu.VMEM((s + KPAD, D_ATTN), BF16),
                        pltpu.VMEM((s + KPAD, D_ATTN), F32), pltpu.VMEM((s + KPAD, D_ATTN), F32),
                        pltpu.SemaphoreType.DMA((2,))],
        args=(qkv, qkv, do, probs), rider=rider)


CONV_HALO = 32
CONV_ROWS = 64


def _sigmoid(t):
    return 1.0 / (1.0 + jnp.exp(-t))


CONV_WIN = CONV_ROWS + CONV_HALO - 8


def _row_windows(ref, r0, buf):
    win = ref[pl.ds(r0, CONV_ROWS + CONV_HALO), :]
    for j in range(1, 8):
        buf[j - 1] = win[j:j + CONV_WIN, :]

    def get(o):
        j, a = o % 8, o - o % 8
        if j == 0:
            return ref[pl.ds(r0 + a, CONV_ROWS), :]
        return buf[j - 1, a:a + CONV_ROWS, :]

    return get


def _glu_rows(z_ref, r0, rows):
    a = z_ref[pl.ds(r0, rows), 0:D_CONV]
    b = z_ref[pl.ds(r0, rows), D_CONV:2 * D_CONV]
    return a, _sigmoid(b)


def _conv_fwd(zc, conv_w, conv_b, ln_g, ln_b, name):
    s = zc.shape[0]
    rt = min(256, s)

    def body(z_ref, w_ref, cb_ref, g_ref, b_ref, cv_ref, feat_ref, hpad, shifts):
        hpad[0:CONV_HALO, :] = jnp.zeros((CONV_HALO, D_CONV), F32)

        def glu(i, carry):
            r0 = pl.multiple_of(i * rt, rt)
            a, sb = _glu_rows(z_ref, r0, rt)
            hpad[pl.ds(r0 + CONV_HALO, rt), :] = a * sb
            return carry

        lax.fori_loop(0, s // rt, glu, 0)
        w = w_ref[...]

        def conv(i, carry):
            r0 = pl.multiple_of(i * CONV_ROWS, CONV_ROWS)
            win = _row_windows(hpad, r0, shifts)
            acc = jnp.broadcast_to(cb_ref[...], (CONV_ROWS, D_CONV))
            for k in range(CONV_WIDTH):
                acc = acc + win(2 + k) * w[k:k + 1, :]
            cv_ref[pl.ds(r0, CONV_ROWS), :] = acc
            yhat, _ = _ln_hat(acc)
            y = yhat * g_ref[...] + b_ref[...]
            feat_ref[pl.ds(r0, CONV_ROWS), :] = (y * _sigmoid(y)).astype(BF16)
            return carry

        lax.fori_loop(0, s // CONV_ROWS, conv, 0)

    return pl.pallas_call(
        body, out_shape=[jax.ShapeDtypeStruct((s, D_CONV), F32), jax.ShapeDtypeStruct((s, D_CONV), BF16)],
        scratch_shapes=[pltpu.VMEM((s + CONV_HALO, D_CONV), F32), pltpu.VMEM((7, CONV_WIN, D_CONV), F32)],
        name=name, compiler_params=_cparams(),
    )(zc, conv_w, conv_b, ln_g, ln_b)


def _conv_bwd(dfeat, cv, zc, conv_w, ln_g, ln_b, name):
    s = zc.shape[0]
    rt = min(256, s)

    def body(df_ref, cv_ref, z_ref, w_ref, g_ref, b_ref, dz_ref, dw_ref, dcb_ref, dg_ref, db_ref, hpad, dcvpad, dwacc,
             hshifts, dshifts):
        hpad[0:CONV_HALO, :] = jnp.zeros((CONV_HALO, D_CONV), F32)
        dcvpad[s:, :] = jnp.zeros((CONV_HALO, D_CONV), F32)
        dwacc[...] = jnp.zeros_like(dwacc)
        dcb_ref[...] = jnp.zeros_like(dcb_ref)
        dg_ref[...] = jnp.zeros_like(dg_ref)
        db_ref[...] = jnp.zeros_like(db_ref)

        def pass1(i, carry):
            r0 = pl.multiple_of(i * rt, rt)
            a, sb = _glu_rows(z_ref, r0, rt)
            hpad[pl.ds(r0 + CONV_HALO, rt), :] = a * sb
            cvhat, rstd = _ln_hat(cv_ref[pl.ds(r0, rt), :])
            y = cvhat * g_ref[...] + b_ref[...]
            sg = _sigmoid(y)
            dy = df_ref[pl.ds(r0, rt), :] * (sg * (1.0 + y * (1.0 - sg)))
            dg_ref[...] += jnp.sum(dy * cvhat, axis=0, keepdims=True)
            db_ref[...] += jnp.sum(dy, axis=0, keepdims=True)
            dcv = _ln_hat_bwd(dy * g_ref[...], cvhat, rstd)
            dcb_ref[...] += jnp.sum(dcv, axis=0, keepdims=True)
            dcvpad[pl.ds(r0, rt), :] = dcv
            return carry

        lax.fori_loop(0, s // rt, pass1, 0)
        w = w_ref[...]

        def pass2(i, carry):
            r0 = pl.multiple_of(i * CONV_ROWS, CONV_ROWS)
            dwin = _row_windows(dcvpad, r0, dshifts)
            hwin = _row_windows(hpad, r0, hshifts)
            dcv = dwin(0)
            dh = jnp.zeros((CONV_ROWS, D_CONV), F32)
            for k in range(CONV_WIDTH):
                dh = dh + dwin(30 - k) * w[k:k + 1, :]
                prod = dcv * hwin(2 + k)
                dwacc[8 * k:8 * k + 8, :] += jnp.sum(prod.reshape(CONV_ROWS // 8, 8, D_CONV), axis=0)
            a, sb = _glu_rows(z_ref, r0, CONV_ROWS)
            dz_ref[pl.ds(r0, CONV_ROWS), :] = jnp.concatenate([dh * sb, dh * a * sb * (1.0 - sb)], axis=1).astype(BF16)
            return carry

        lax.fori_loop(0, s // CONV_ROWS, pass2, 0)
        dw_ref[...] = jnp.sum(dwacc[...].reshape(32, 8, D_CONV), axis=1)

    vs = jax.ShapeDtypeStruct((1, D_CONV), F32)
    return pl.pallas_call(
        body,
        out_shape=[jax.ShapeDtypeStruct((s, 2 * D_CONV), BF16), jax.ShapeDtypeStruct((32, D_CONV), F32), vs, vs, vs],
        scratch_shapes=[pltpu.VMEM((s + CONV_HALO, D_CONV), F32), pltpu.VMEM((s + CONV_HALO, D_CONV), F32),
                        pltpu.VMEM((256, D_CONV), F32), pltpu.VMEM((7, CONV_WIN, D_CONV), F32),
                        pltpu.VMEM((7, CONV_WIN, D_CONV), F32)],
        name=name, compiler_params=_cparams(),
    )(dfeat, cv, zc, conv_w, ln_g, ln_b)


def _branch_out(feats, wts, name):
    s = feats[0].shape[0]
    tm = min(1024, s)

    def body(*refs):
        for f_ref, w_ref, o_ref in zip(refs[:3], refs[3:6], refs[6:]):
            o_ref[...] = lax.dot_general(f_ref[...], w_ref[...], _DIMS["nt"], preferred_element_type=F32).astype(BF16)

    row = pl.BlockSpec((tm, D_MODEL), lambda i: (i, 0))
    sh = jax.ShapeDtypeStruct((s, D_MODEL), BF16)
    return pl.pallas_call(
        body, grid=(s // tm,),
        in_specs=[pl.BlockSpec((tm, f.shape[1]), lambda i: (i, 0)) for f in feats] + [_full(w.shape) for w in wts],
        out_specs=[row] * 3, out_shape=[sh] * 3, name=name, compiler_params=_cparams(),
    )(*feats, *wts)


def _branch_in_bwd(dys, wts, out_dtypes, name):
    s = dys[0].shape[0]
    tm = min(1024, s)

    def body(*refs):
        for d_ref, w_ref, o_ref in zip(refs[:3], refs[3:6], refs[6:]):
            o_ref[...] = jnp.dot(d_ref[...], w_ref[...], preferred_element_type=F32).astype(o_ref.dtype)

    row = pl.BlockSpec((tm, D_MODEL), lambda i: (i, 0))
    return pl.pallas_call(
        body, grid=(s // tm,), in_specs=[row] * 3 + [_full(w.shape) for w in wts],
        out_specs=[pl.BlockSpec((tm, w.shape[1]), lambda i: (i, 0)) for w in wts],
        out_shape=[jax.ShapeDtypeStruct((s, w.shape[1]), dt) for w, dt in zip(wts, out_dtypes)],
        name=name, compiler_params=_cparams(),
    )(*dys, *wts)


def _branch_dw(dys, feats, name):
    s = dys[0].shape[0]
    tm = 512

    def body(*refs):
        for d_ref, f_ref, o_ref in zip(refs[:3], refs[3:6], refs[6:]):
            acc = lax.dot_general(d_ref[...], f_ref[...], _DIMS["tn"], preferred_element_type=F32)
            half = acc.shape[1] // 2
            o_ref[0] = acc[:, :half].astype(BF16)
            o_ref[1] = acc[:, half:].astype(BF16)

    return pl.pallas_call(
        body, grid=(D_MODEL // tm,),
        in_specs=[pl.BlockSpec((s, tm), lambda i: (0, i))] * 3 + [_full(f.shape) for f in feats],
        out_specs=[pl.BlockSpec((2, tm, f.shape[1] // 2), lambda i: (0, i, 0)) for f in feats],
        out_shape=[jax.ShapeDtypeStruct((2, D_MODEL, f.shape[1] // 2), BF16) for f in feats],
        name=name, compiler_params=_cparams(),
    )(*dys, *feats)


def _merge(zg, b_gate, ys, name):
    s = zg.shape[0]
    tm = _row_tile(s)

    def body(zg_ref, bg_ref, y0_ref, y1_ref, y2_ref, o_ref):
        acc = None
        for j, y_ref in enumerate((y0_ref, y1_ref, y2_ref)):
            cs = slice(D_MODEL * j, D_MODEL * (j + 1))
            t = _sigmoid(zg_ref[:, cs] + bg_ref[:, cs]) * y_ref[...]
            acc = t if acc is None else acc + t
        o_ref[...] = acc.astype(BF16)

    row = pl.BlockSpec((tm, D_MODEL), lambda i: (i, 0))
    return pl.pallas_call(
        body, grid=(s // tm,),
        in_specs=[pl.BlockSpec((tm, 3 * D_MODEL), lambda i: (i, 0)), _full((1, 3 * D_MODEL)), row, row, row],
        out_specs=row, out_shape=jax.ShapeDtypeStruct((s, D_MODEL), BF16), name=name, compiler_params=_cparams(),
    )(zg, b_gate, *ys)


def _merge_bwd(dm, zg, b_gate, ys, name):
    s = zg.shape[0]
    tm = min(256, s)

    def body(dm_ref, zg_ref, bg_ref, y0_ref, y1_ref, y2_ref, d0_ref, d1_ref, d2_ref, dzg_ref, dbg_ref):
        first = pl.program_id(0) == 0

        @pl.when(first)
        def _():
            dbg_ref[...] = jnp.zeros_like(dbg_ref)

        dmv = dm_ref[...]
        for j, (y_ref, d_ref) in enumerate(((y0_ref, d0_ref), (y1_ref, d1_ref), (y2_ref, d2_ref))):
            cs = slice(D_MODEL * j, D_MODEL * (j + 1))
            g = _sigmoid(zg_ref[:, cs] + bg_ref[:, cs])
            d_ref[...] = (dmv * g).astype(BF16)
            dzg = dmv * y_ref[...] * g * (1.0 - g)
            dzg_ref[:, cs] = dzg.astype(BF16)
            dbg_ref[:, cs] += jnp.sum(dzg, axis=0, keepdims=True)

    row = pl.BlockSpec((tm, D_MODEL), lambda i: (i, 0))
    wide = pl.BlockSpec((tm, 3 * D_MODEL), lambda i: (i, 0))
    yb = jax.ShapeDtypeStruct((s, D_MODEL), BF16)
    return pl.pallas_call(
        body, grid=(s // tm,),
        in_specs=[row, wide, _full((1, 3 * D_MODEL)), row, row, row],
        out_specs=[row, row, row, wide, _full((1, 3 * D_MODEL))],
        out_shape=[yb, yb, yb, jax.ShapeDtypeStruct((s, 3 * D_MODEL), BF16), jax.ShapeDtypeStruct((1, 3 * D_MODEL), F32)],
        name=name, compiler_params=_cparams(),
    )(dm, zg, b_gate, *ys)


def _ff_hidden(u2, w_ff1t, b_ff1, name, rider=None):
    s = u2.shape[0]
    tm, tn = min(2048, s), 1024

    def body(a_ref, b_ref, bias_ref, pre_ref, h_ref):
        acc = lax.dot_general(a_ref[...], b_ref[...], _DIMS["nt"], preferred_element_type=F32) + bias_ref[...]
        pre_ref[...] = acc.astype(BF16)
        h_ref[...] = _relu2(acc).astype(BF16)

    blk = pl.BlockSpec((tm, tn), lambda i, j: (i, j))
    sh = jax.ShapeDtypeStruct((s, D_FF), BF16)
    res = _call(body, name=name, grid=(s // tm, D_FF // tn),
                in_specs=[pl.BlockSpec((tm, D_MODEL), lambda i, j: (i, 0)), pl.BlockSpec((tn, D_MODEL), lambda i, j: (j, 0)),
                          pl.BlockSpec((1, tn), lambda i, j: (0, j))],
                out_specs=[blk, blk], out_shape=[sh, sh], scratch_shapes=[], args=(u2, w_ff1t, b_ff1), rider=rider)
    return tuple(res) if rider is None else (tuple(res[0]), res[1])


def _ff_hidden_bwd(dff, w_ff2, hpre, name, rider=None):
    s = dff.shape[0]
    tm, tn = min(1024, s), 1024

    def body(a_ref, b_ref, h_ref, o_ref, sum_ref):
        dh = lax.dot_general(a_ref[...], b_ref[...], _DIMS["nt"], preferred_element_type=F32)
        dpre = dh * (2.0 * jnp.maximum(h_ref[...].astype(F32), 0.0))
        o_ref[...] = dpre.astype(BF16)
        _acc_rows(sum_ref, dpre, pl.program_id(1) == 0)

    res = _call(
        body, name=name, grid=(D_FF // tn, s // tm),
        in_specs=[pl.BlockSpec((tm, D_MODEL), lambda j, i: (i, 0)), pl.BlockSpec((tn, D_MODEL), lambda j, i: (j, 0)),
                  pl.BlockSpec((tm, tn), lambda j, i: (i, j))],
        out_specs=[pl.BlockSpec((tm, tn), lambda j, i: (i, j)), pl.BlockSpec((1, tn), lambda j, i: (0, j))],
        out_shape=[jax.ShapeDtypeStruct((s, D_FF), BF16), jax.ShapeDtypeStruct((1, D_FF), F32)],
        scratch_shapes=[], args=(dff, w_ff2, hpre), rider=rider)
    return tuple(res) if rider is None else (tuple(res[0]), res[1])


def _silu(t):
    return t * _sigmoid(t)


def _mod_fwd(c_all, w_ada_sh, b_ada_sh, name):
    cols = w_ada_sh.shape[2]

    def body(c_ref, w_ref, b_ref, o_ref):
        ca = _silu(c_ref[...]).astype(BF16)
        o_ref[0] = jnp.dot(ca, w_ref[0].astype(BF16), preferred_element_type=F32) + b_ref[0]

    return pl.pallas_call(
        body, grid=(DEPTH,),
        in_specs=[_full((N_DEV, D_MODEL)), pl.BlockSpec((1, D_MODEL, cols), lambda l: (l, 0, 0)),
                  pl.BlockSpec((1, 1, cols), lambda l: (l, 0, 0))],
        out_specs=pl.BlockSpec((1, N_DEV, cols), lambda l: (l, 0, 0)),
        out_shape=jax.ShapeDtypeStruct((DEPTH, N_DEV, cols), F32), name=name, compiler_params=_cparams(),
    )(c_all, w_ada_sh, b_ada_sh)


def _mod_bwd(c_all, dmod_sh, name):
    cols = dmod_sh.shape[2]

    def body(c_ref, d_ref, o_ref):
        ca = _silu(c_ref[...])
        o_ref[0] = lax.dot_general(ca, d_ref[0], _DIMS["tn"], precision=lax.Precision.HIGHEST,
                                   preferred_element_type=F32)

    return pl.pallas_call(
        body, grid=(DEPTH,),
        in_specs=[_full((N_DEV, D_MODEL)), pl.BlockSpec((1, N_DEV, cols), lambda l: (l, 0, 0))],
        out_specs=pl.BlockSpec((1, D_MODEL, cols), lambda l: (l, 0, 0)),
        out_shape=jax.ShapeDtypeStruct((DEPTH, D_MODEL, cols), F32), name=name, compiler_params=_cparams(),
    )(c_all, dmod_sh)


def _flat_tiles(rows, cols, itemsize_total):
    budget = 12 * 1024 * 1024
    tr = rows
    while tr % 32 == 0 and tr * cols * itemsize_total > budget:
        tr //= 2
    return tr


def _sum_cores(dw, recv, place, name):
    _, m, n = dw.shape
    tr = _flat_tiles(m, n, 6)

    def body(place_ref, a_ref, b_ref, o_ref):
        o_ref[...] = (a_ref[...].astype(F32) + b_ref[...].astype(F32)).astype(BF16)

    grid_spec = pltpu.PrefetchScalarGridSpec(
        num_scalar_prefetch=1, grid=(m // tr,),
        in_specs=[pl.BlockSpec((None, tr, n), lambda i, pr: (pr[0], i, 0)), pl.BlockSpec((tr, n), lambda i, pr: (i, 0))],
        out_specs=pl.BlockSpec((tr, n), lambda i, pr: (i, 0)))
    return pl.pallas_call(body, grid_spec=grid_spec, out_shape=jax.ShapeDtypeStruct((m, n), BF16), name=name,
                          compiler_params=_cparams())(place, dw, recv)


def _sum_chips(h, r, place, name):
    _, rs, n = h.shape
    tr = _flat_tiles(rs, n, 12)

    def body(place_ref, h_ref, r_ref, o_ref):
        o_ref[...] = ((h_ref[...].astype(F32) + r_ref[0].astype(F32)) + r_ref[1].astype(F32)) + r_ref[2].astype(F32)

    grid_spec = pltpu.PrefetchScalarGridSpec(
        num_scalar_prefetch=1, grid=(rs // tr,),
        in_specs=[pl.BlockSpec((None, tr, n), lambda i, pr: (pr[1], i, 0)), pl.BlockSpec((3, tr, n), lambda i, pr: (0, i, 0))],
        out_specs=pl.BlockSpec((tr, n), lambda i, pr: (i, 0)))
    return pl.pallas_call(body, grid_spec=grid_spec, out_shape=jax.ShapeDtypeStruct((rs, n), F32), name=name,
                          compiler_params=_cparams())(place, h, r)


def _adam_math(w, g, m, v):
    m2 = ADAM_B1 * m + (1.0 - ADAM_B1) * g
    v2 = ADAM_B2 * v + (1.0 - ADAM_B2) * (g * g)
    m_hat = m2 / (1.0 - ADAM_B1 ** ADAM_STEP)
    v_hat = v2 / (1.0 - ADAM_B2 ** ADAM_STEP)
    delta = -ADAM_LR * (m_hat / (jnp.sqrt(v_hat) + ADAM_EPS) + ADAM_WD * w)
    return delta, m2, v2


def _adamw(w, m, v, grads, name):
    r, c = w.shape
    tr = _flat_tiles(r, c, 4 * (7 + len(grads)))

    def body(*refs):
        w_ref, m_ref, v_ref = refs[:3]
        g_refs = refs[3:3 + len(grads)]
        g_ref, d_ref, m2_ref, v2_ref = refs[3 + len(grads):]
        g = g_refs[0][...]
        for gr in g_refs[1:]:
            g = g + gr[...]
        delta, m2, v2 = _adam_math(w_ref[...], g, m_ref[...], v_ref[...])
        g_ref[...] = g
        d_ref[...] = delta
        m2_ref[...] = m2
        v2_ref[...] = v2

    blk = pl.BlockSpec((tr, c), lambda i: (i, 0))
    sh = jax.ShapeDtypeStruct((r, c), F32)
    return pl.pallas_call(body, grid=(r // tr,), in_specs=[blk] * (3 + len(grads)), out_specs=[blk] * 4,
                          out_shape=[sh] * 4, name=name, compiler_params=_cparams())(w, m, v, *grads)


def _adamw_halves(w, m, v, own, other, place, split, name):
    nl, r, c = w.shape
    hr, hc = own[0].shape
    tr = _flat_tiles(hr, hc, 4 * (7 + 2 * nl))
    nt = hr // tr
    if split == "rows":
        w_spec = pl.BlockSpec((None, tr, c), lambda l, h, t, pr: (l, h * nt + t, 0))
    else:
        w_spec = pl.BlockSpec((None, tr, hc), lambda l, h, t, pr: (l, t, h))

    def g_spec(layer, mine):
        return pl.BlockSpec((tr, hc), lambda l, h, t, pr: (jnp.where((l == layer) & ((h == pr[0]) == mine), t, nt - 1), 0))

    def body(place_ref, w_ref, m_ref, v_ref, *refs):
        own_refs, other_refs = refs[:nl], refs[nl:2 * nl]
        g_ref, d_ref, m2_ref, v2_ref = refs[2 * nl:]
        layer = pl.program_id(0)
        mine = pl.program_id(1) == place_ref[0]
        g = None
        for li in range(nl):
            cand = jnp.where(mine, own_refs[li][...], other_refs[li][...])
            g = cand if g is None else jnp.where(layer == li, cand, g)
        delta, m2, v2 = _adam_math(w_ref[...], g, m_ref[...], v_ref[...])
        g_ref[...] = g
        d_ref[...] = delta
        m2_ref[...] = m2
        v2_ref[...] = v2

    sh = jax.ShapeDtypeStruct((nl, r, c), F32)
    g_specs = [g_spec(li, True) for li in range(nl)] + [g_spec(li, False) for li in range(nl)]
    return _call(body, name=name, grid=(nl, 2, nt), in_specs=[w_spec] * 3 + g_specs, out_specs=[w_spec] * 4,
                 out_shape=[sh] * 4, scratch_shapes=[], args=(w, m, v, *own, *other), prefetch=(place,))


def _adamw_small(w, m, v, g_all, name):
    r, c = w.shape

    def body(w_ref, m_ref, v_ref, g_ref, go_ref, d_ref, m2_ref, v2_ref):
        g = g_ref[0]
        for b in range(1, N_DEV):
            g = g + g_ref[b]
        delta, m2, v2 = _adam_math(w_ref[...], g, m_ref[...], v_ref[...])
        go_ref[...] = g
        d_ref[...] = delta
        m2_ref[...] = m2
        v2_ref[...] = v2

    sh = jax.ShapeDtypeStruct((r, c), F32)
    return pl.pallas_call(body, out_shape=[sh] * 4, name=name, compiler_params=_cparams())(w, m, v, g_all)


def _me():
    return lax.axis_index("x"), lax.axis_index("y"), lax.axis_index("c")


def _flip(v, bit):
    return 1 - v if bit else v


def _allgather_small(blk, name):
    r, c = blk.shape

    def body(x_ref, o_ref, send_sems, recv_sems):
        x, y, cc = _me()
        me = 4 * x + 2 * y + cc
        copies = []
        for k in range(1, N_DEV):
            peer = (_flip(x, k & 4), _flip(y, k & 2), _flip(cc, k & 1))
            cp = pltpu.make_async_remote_copy(src_ref=x_ref, dst_ref=o_ref.at[me], send_sem=send_sems.at[k - 1],
                                              recv_sem=recv_sems.at[k - 1], device_id=peer, device_id_type=MESH)
            cp.start()
            copies.append(cp)
        o_ref[me] = x_ref[...]
        for cp in copies:
            cp.wait()

    return pl.pallas_call(
        body, out_shape=jax.ShapeDtypeStruct((N_DEV, r, c), F32),
        in_specs=[pl.BlockSpec(memory_space=pltpu.VMEM)], out_specs=pl.BlockSpec(memory_space=pltpu.VMEM),
        scratch_shapes=[pltpu.SemaphoreType.DMA((N_DEV - 1,)), pltpu.SemaphoreType.DMA((N_DEV - 1,))],
        name=name, compiler_params=_cparams(),
    )(blk)


class _Rider:
    def __init__(self, arrays, out_shapes, scratch_shapes, start, finish):
        self.arrays, self.out_shapes, self.scratch_shapes = list(arrays), list(out_shapes), list(scratch_shapes)
        self.start, self.finish = start, finish


def _call(body, *, name, grid, in_specs, out_specs, out_shape, scratch_shapes, args, rider=None, prefetch=()):
    npf = len(prefetch)

    def launch(fn, in_specs, out_specs, out_shape, scratch_shapes, args):
        grid_spec = pltpu.PrefetchScalarGridSpec(num_scalar_prefetch=npf, grid=grid, in_specs=in_specs,
                                                 out_specs=out_specs, scratch_shapes=scratch_shapes)
        return pl.pallas_call(fn, grid_spec=grid_spec, out_shape=out_shape, name=name,
                              compiler_params=_cparams())(*prefetch, *args)

    if rider is None:
        return launch(body, list(in_specs), list(out_specs), list(out_shape), list(scratch_shapes), args)
    ni, no, ns = len(in_specs), len(out_specs), len(scratch_shapes)
    ri, ro = len(rider.arrays), len(rider.out_shapes)
    steps = int(np.prod(grid))

    def wrapped(*refs):
        pf, refs = refs[:npf], refs[npf:]
        h_in, r_in = refs[:ni], refs[ni:ni + ri]
        h_out, r_out = refs[ni + ri:ni + ri + no], refs[ni + ri + no:ni + ri + no + ro]
        h_scr, r_scr = refs[ni + ri + no + ro:ni + ri + no + ro + ns], refs[ni + ri + no + ro + ns:]
        step = pl.program_id(0)
        for d in range(1, len(grid)):
            step = step * grid[d] + pl.program_id(d)

        @pl.when(step == 0)
        def _():
            rider.start(r_in, r_out, r_scr)

        body(*pf, *h_in, *h_out, *h_scr)

        @pl.when(step == steps - 1)
        def _():
            rider.finish(r_in, r_out, r_scr)

    anyspec = pl.BlockSpec(memory_space=pl.ANY)
    res = launch(wrapped, list(in_specs) + [anyspec] * ri, list(out_specs) + [anyspec] * ro,
                 list(out_shape) + rider.out_shapes, list(scratch_shapes) + rider.scratch_shapes,
                 list(args) + rider.arrays)
    return res[:no], res[no:]


def _run_rider(rider, name):
    ri = len(rider.arrays)

    def body(*refs):
        r_in, r_out, r_scr = refs[:ri], refs[ri:ri + len(rider.out_shapes)], refs[ri + len(rider.out_shapes):]
        rider.start(r_in, r_out, r_scr)
        rider.finish(r_in, r_out, r_scr)

    anyspec = pl.BlockSpec(memory_space=pl.ANY)
    return pl.pallas_call(body, in_specs=[anyspec] * ri, out_specs=[anyspec] * len(rider.out_shapes),
                          out_shape=rider.out_shapes, scratch_shapes=rider.scratch_shapes, name=name,
                          compiler_params=_cparams())(*rider.arrays)


def _allgather_rider(blk):
    def copies(ins, outs, scr):
        send_sems, recv_sems, loc_sems, stage = scr
        x, y, cc = _me()
        me = 4 * x + 2 * y + cc
        remote = [pltpu.make_async_remote_copy(
            src_ref=ins[0], dst_ref=outs[0].at[me], send_sem=send_sems.at[k - 1], recv_sem=recv_sems.at[k - 1],
            device_id=(_flip(x, k & 4), _flip(y, k & 2), _flip(cc, k & 1)), device_id_type=MESH) for k in range(1, N_DEV)]
        return remote, pltpu.make_async_copy(ins[0], stage, loc_sems.at[0]), (outs[0].at[me], stage, loc_sems.at[1])

    def start(ins, outs, scr):
        remote, lin, _ = copies(ins, outs, scr)
        lin.start()
        for cp in remote:
            cp.start()

    def finish(ins, outs, scr):
        remote, lin, (dst, stage, sem) = copies(ins, outs, scr)
        lin.wait()
        lout = pltpu.make_async_copy(stage, dst, sem)
        lout.start()
        for cp in remote:
            cp.wait()
        lout.wait()

    return _Rider([blk], [jax.ShapeDtypeStruct((N_DEV,) + blk.shape, blk.dtype)],
                  [pltpu.SemaphoreType.DMA((N_DEV - 1,)), pltpu.SemaphoreType.DMA((N_DEV - 1,)),
                   pltpu.SemaphoreType.DMA((2,)), pltpu.VMEM(blk.shape, blk.dtype)], start, finish)


def _gather_rider(shards):
    n = len(shards)

    def copies(ins, outs, scr, relay=True):
        ici_send, ici_recv, d2d_send, d2d_recv, loc_sems = scr[:5]
        stage = scr[5:]
        x, y, cc = _me()
        chip = 2 * x + y
        sibling = (x, y, 1 - cc)
        local, sends, relays = [], [], []
        for j in range(n):
            def rows(ch, h, j=j):
                return outs[j].at[ch, h]

            lc = pltpu.make_async_copy(ins[j], stage[j], loc_sems.at[j])
            local.append((lc, pltpu.make_async_copy(stage[j], outs[j].at[chip], loc_sems.at[n + j]) if relay else None))
            for k in range(1, N_CHIP):
                px, py = _flip(x, k & 2), _flip(y, k & 1)
                pchip = 2 * px + py
                q = 3 * j + k - 1
                out_cp = pltpu.make_async_remote_copy(src_ref=ins[j].at[cc], dst_ref=rows(chip, cc),
                                                      send_sem=ici_send.at[q], recv_sem=ici_recv.at[q],
                                                      device_id=(px, py, cc), device_id_type=MESH)
                sends.append(out_cp)
                if not relay:
                    continue
                arrival = pltpu.make_async_remote_copy(src_ref=rows(pchip, cc), dst_ref=rows(pchip, cc),
                                                       send_sem=ici_send.at[q], recv_sem=ici_recv.at[q],
                                                       device_id=(px, py, cc), device_id_type=MESH)
                forward = pltpu.make_async_remote_copy(src_ref=rows(pchip, cc), dst_ref=rows(pchip, cc),
                                                       send_sem=d2d_send.at[q], recv_sem=d2d_recv.at[q],
                                                       device_id=sibling, device_id_type=MESH)
                from_sibling = pltpu.make_async_remote_copy(src_ref=rows(pchip, 1 - cc), dst_ref=rows(pchip, 1 - cc),
                                                            send_sem=d2d_send.at[q], recv_sem=d2d_recv.at[q],
                                                            device_id=sibling, device_id_type=MESH)
                relays.append((arrival, forward, from_sibling))
        return local, sends, relays

    def start(ins, outs, scr):
        local, sends, _ = copies(ins, outs, scr, relay=False)
        for lin, _ in local:
            lin.start()
        for cp in sends:
            cp.start()

    def finish(ins, outs, scr):
        local, sends, relays = copies(ins, outs, scr)
        for lin, lout in local:
            lin.wait()
            lout.start()
        for arrival, forward, _ in relays:
            arrival.wait_recv()
            forward.start()
        for cp in sends:
            cp.wait_send()
        for _, forward, from_sibling in relays:
            forward.wait_send()
            from_sibling.wait_recv()
        for _, lout in local:
            lout.wait()

    scratch = [pltpu.SemaphoreType.DMA((3 * n,)), pltpu.SemaphoreType.DMA((3 * n,)), pltpu.SemaphoreType.DMA((3 * n,)),
               pltpu.SemaphoreType.DMA((3 * n,)), pltpu.SemaphoreType.DMA((2 * n,))]
    scratch += [pltpu.VMEM(a.shape, a.dtype) for a in shards]
    return _Rider(shards, [jax.ShapeDtypeStruct((N_CHIP,) + a.shape, a.dtype) for a in shards], scratch, start, finish)


def _sibling_rider(arrs, other_half=False):
    n = len(arrs)

    def copies(ins, outs, scr):
        send_sems, recv_sems = scr
        x, y, cc = _me()
        return [pltpu.make_async_remote_copy(
            src_ref=ins[j].at[1 - cc] if other_half else ins[j], dst_ref=outs[j], send_sem=send_sems.at[j],
            recv_sem=recv_sems.at[j], device_id=(x, y, 1 - cc), device_id_type=MESH) for j in range(n)]

    def start(ins, outs, scr):
        for cp in copies(ins, outs, scr):
            cp.start()

    def finish(ins, outs, scr):
        for cp in copies(ins, outs, scr):
            cp.wait()

    return _Rider(arrs, [jax.ShapeDtypeStruct(a.shape[1:] if other_half else a.shape, a.dtype) for a in arrs],
                  [pltpu.SemaphoreType.DMA((n,)), pltpu.SemaphoreType.DMA((n,))], start, finish)


def _sibling_send(arrs, name, other_half=False):
    return _run_rider(_sibling_rider(arrs, other_half), name)


def _join_riders(first, second):
    ni, no, ns = len(first.arrays), len(first.out_shapes), len(first.scratch_shapes)

    def split(ins, outs, scr):
        return (ins[:ni], outs[:no], scr[:ns]), (ins[ni:], outs[no:], scr[ns:])

    def start(ins, outs, scr):
        a, b = split(ins, outs, scr)
        first.start(*a)
        second.start(*b)

    def finish(ins, outs, scr):
        a, b = split(ins, outs, scr)
        first.finish(*a)
        second.finish(*b)

    return _Rider(first.arrays + second.arrays, first.out_shapes + second.out_shapes,
                  first.scratch_shapes + second.scratch_shapes, start, finish)


def _scatter_rider(arrs):
    n = len(arrs)

    def copies(ins, outs, scr):
        send_sems, recv_sems = scr
        x, y, cc = _me()
        cps = []
        for j in range(n):
            for k in range(1, N_CHIP):
                px, py = _flip(x, k & 2), _flip(y, k & 1)
                cps.append(pltpu.make_async_remote_copy(
                    src_ref=ins[j].at[2 * px + py], dst_ref=outs[j].at[k - 1], send_sem=send_sems.at[3 * j + k - 1],
                    recv_sem=recv_sems.at[3 * j + k - 1], device_id=(px, py, cc), device_id_type=MESH))
        return cps

    def start(ins, outs, scr):
        for cp in copies(ins, outs, scr):
            cp.start()

    def finish(ins, outs, scr):
        for cp in copies(ins, outs, scr):
            cp.wait()

    return _Rider(arrs, [jax.ShapeDtypeStruct((N_CHIP - 1,) + a.shape[1:], a.dtype) for a in arrs],
                  [pltpu.SemaphoreType.DMA((3 * n,)), pltpu.SemaphoreType.DMA((3 * n,))], start, finish)


COL_SHARDED = ("w_in", "w_br_pool", "w_br_attn", "w_br_conv", "w_ff1")
ROW_SHARDED = ("w_o", "w_ff2")
BIG = COL_SHARDED + ROW_SHARDED
SMALL = ("b_ada", "b_gate", "w_pool", "pool_scale", "rel_bias", "conv_w", "conv_b", "conv_ln_g", "conv_ln_b",
         "ln_mix_g", "ln_mix_b", "b_ff1", "b_ff2", "ln_ff_g", "ln_ff_b")
PACK_W = 1024


def _pack(parts):
    rows = []
    for a in parts:
        flat = a.reshape(-1)
        n = -(-flat.shape[0] // PACK_W) * PACK_W
        rows.append(jnp.pad(flat, (0, n - flat.shape[0])).reshape(-1, PACK_W))
    out = jnp.concatenate(rows, axis=0)
    r = -(-out.shape[0] // 8) * 8
    return jnp.pad(out, ((0, r - out.shape[0]), (0, 0)))


def _unpack(packed, shapes):
    out, r0 = [], 0
    for shp in shapes:
        size = int(np.prod(shp))
        nr = -(-size // PACK_W)
        out.append(packed[r0:r0 + nr].reshape(-1)[:size].reshape(shp))
        r0 += nr
    return out


def _hosted(fn, hook, *args, **kw):
    if hook is None:
        return fn(*args, **kw)
    res, rider_out = fn(*args, rider=hook[0], **kw)
    hook[1](rider_out)
    return res


def _layer_fwd(l, x, mod, W, P, hooks=None, u=None):
    hooks = hooks or {}
    s = x.shape[0]
    sh_m, sc_m, g_m, sh_f, sc_f, g_f = [mod[l:l + 1, D_MODEL * j:D_MODEL * (j + 1)] for j in range(6)]
    n = lambda t: f"{t}{l}"
    w_in = W["w_in"][l]
    if u is None:
        u = _ln_mod(x, sc_m, sh_m, n("ln_mod_mix"))
    zp = _mm(u, w_in, "nt", tm=s, tn=256, out_dtype=F32, name=n("z_pool"), b_col0=0, n_out=D_POOL)
    qkv = _mm(u, w_in, "nt", tm=s, tn=256, out_dtype=BF16, name=n("z_qkv"), b_col0=OFF_QKV // 256, n_out=3 * D_ATTN)
    zc = _mm(u, w_in, "nt", tm=s, tn=256, out_dtype=F32, name=n("z_conv"), b_col0=OFF_CONV // 256, n_out=2 * D_CONV)
    zg = _hosted(_mm, hooks.get("z_gate"), u, w_in, "nt", tm=min(2048, s), tn=768, out_dtype=BF16, name=n("z_gate"),
                 b_col0=OFF_GATE // 768, n_out=3 * D_MODEL)

    p, feat_pool = _pool_fwd(zp, P["wp_bd"][l], P["pool_scale"][l], n("pool_fwd"))
    bias = _bias_block(P["rel_bias"][l], n("bias_block"))
    o, probs = _hosted(_attn_fwd, hooks.get("attn"), qkv, bias, n("attn_fwd"))
    cv, feat_conv = _conv_fwd(zc, P["conv_w"][l], P["conv_b"][l], P["conv_ln_g"][l], P["conv_ln_b"][l], n("conv_fwd"))

    branch_w = (W["w_br_pool"][l], W["w_br_attn"][l], W["w_br_conv"][l])
    ys = tuple(_branch_out((feat_pool, o, feat_conv), branch_w, n("branch_out")))
    merged = _merge(zg, P["b_gate"][l], ys, n("merge"))
    mix, x1, u2 = _mm_resid_ln(merged, W["w_o"][l], None, x, g_m, P["ln_mix_g"][l], P["ln_mix_b"][l], n("mix_out"),
                               mod_next=(sc_f, sh_f))

    hpre, hid = _hosted(_ff_hidden, hooks.get("ff1"), u2, W["w_ff1"][l], P["b_ff1"][l], n("ff1"))
    above = None if l + 1 == mod.shape[0] else (mod[l + 1:l + 2, D_MODEL:2 * D_MODEL], mod[l + 1:l + 2, 0:D_MODEL])
    ff, x2, *u_next = _hosted(_mm_resid_ln, hooks.get("ff2"), hid, W["w_ff2"][l], P["b_ff2"][l], x1, g_f,
                              P["ln_ff_g"][l], P["ln_ff_b"][l], n("ff2"), mod_next=above)
    saved = dict(x=x, u=u, zp=zp, qkv=qkv, zc=zc, zg=zg, p=p, feat_pool=feat_pool, probs=probs, o=o, cv=cv,
                 feat_conv=feat_conv, ys=ys, merged=merged, mix=mix, x1=x1, u2=u2, hpre=hpre, hid=hid, ff=ff,
                 u_next=u_next[0] if u_next else None)
    return x2, saved


def _layer_bwd(l, dx2, mod, W, P, A, hooks=None, tgt=None, nxt=None):
    hooks = hooks or {}
    sh_m, sc_m, g_m, sh_f, sc_f, g_f = [mod[l:l + 1, D_MODEL * j:D_MODEL * (j + 1)] for j in range(6)]
    n = lambda t: f"{t}{l}"
    gw, gs = {}, {}

    if isinstance(dx2, tuple):
        dres, dff, gs["ln_ff_g"], gs["ln_ff_b"], dg_f, gs["b_ff2"] = dx2
    else:
        dres, dff, gs["ln_ff_g"], gs["ln_ff_b"], dg_f, gs["b_ff2"], *loss_part = _resid_ln_bwd(
            dx2, A["x1"], A["ff"], g_f, P["ln_ff_g"][l], n("resid_ln_ff_bwd"), tgt=tgt)
    s = dres.shape[0]
    tmb = min(1024, s)
    gw["w_ff2"] = _mm(A["hid"], dff, "tn", tm=512, tn=1024, out_dtype=BF16, name=n("dw_ff2"), split_n=512)
    hook = hooks["ff_hidden_bwd"](gw) if "ff_hidden_bwd" in hooks else None
    dhpre, gs["b_ff1"] = _hosted(_ff_hidden_bwd, hook, dff, W["w_ff2"][l], A["hpre"], n("ff_hidden_bwd"))
    gw["w_ff1"] = _mm(dhpre, A["u2"], "tn", tm=512, tn=1024, out_dtype=BF16, name=n("dw_ff1"), split_n=512)

    hook = hooks["du_ff"](gw) if "du_ff" in hooks else None
    dres, dmix, dsc_f, dsh_f, gs["ln_mix_g"], gs["ln_mix_b"], dg_m, _ = _hosted(
        _mm_ln_mod_bwd, hook, dhpre, W["w_ff1"][l], A["x1"], sc_f, dres, n("du_ff"),
        nxt=(A["x"], A["mix"], g_m, P["ln_mix_g"][l]))
    gw["w_o"] = _mm(A["merged"], dmix, "tn", tm=512, tn=1024, out_dtype=BF16, name=n("dw_o"), split_n=512)
    dmerged = _mm(dmix, W["w_o"][l], "nt", tm=tmb, tn=1024, out_dtype=F32, name=n("d_merged"))
    dy_pool, dy_attn, dy_conv, dzg, gs["b_gate"] = _merge_bwd(dmerged, A["zg"], P["b_gate"][l], A["ys"], n("merge_bwd"))

    dys = (dy_pool, dy_attn, dy_conv)
    gw["w_br_pool"], gw["w_br_attn"], gw["w_br_conv"] = _branch_dw(
        dys, (A["feat_pool"], A["o"], A["feat_conv"]), n("dw_branch"))
    dfeat_pool, do, dfeat_conv = _branch_in_bwd(
        dys, (W["w_br_pool"][l], W["w_br_attn"][l], W["w_br_conv"][l]), (F32, BF16, F32), n("d_branch_in"))

    dzp, dwp_bd, gs["pool_scale"] = _pool_bwd(dfeat_pool, A["p"], P["wp_bd"][l], P["pool_scale"][l], n("pool_bwd"))
    gs["w_pool"] = jnp.stack([dwp_bd[POOL_GROUP * g:POOL_GROUP * (g + 1), POOL_GROUP * g:POOL_GROUP * (g + 1)]
                              for g in range(len(POOL_WINDOWS))])
    hook = hooks["attn"](gw) if "attn" in hooks else None
    dq, dk, dv, ds_acc = _hosted(_attn_bwd, hook, A["qkv"], do, A["probs"], n("attn_bwd"))
    gs["rel_bias"] = _bias_block_bwd(ds_acc, n("bias_block_bwd"))
    dzc, dcw, gs["conv_b"], gs["conv_ln_g"], gs["conv_ln_b"] = _conv_bwd(
        dfeat_conv, A["cv"], A["zc"], P["conv_w"][l], P["conv_ln_g"][l], P["conv_ln_b"][l], n("conv_bwd"))
    gs["conv_w"] = dcw[:CONV_WIDTH]

    dz = [dzp, dq, dk, dv, dzc, dzg]
    gw["w_in"] = _dw_segments(dz, A["u"], n("dw_in"))
    hook = hooks["du_mix"](gw) if "du_mix" in hooks else None
    res = _hosted(_mm_ln_mod_bwd, hook, dz, W["w_in"][l], A["x"], sc_m, dres, n("du_mix"), nxt=nxt)
    if nxt is None:
        dx, dsc_m, dsh_m = res
    else:
        dx, dsc_m, dsh_m = (res[0], res[1], *res[4:]), res[2], res[3]
    dmod = jnp.concatenate([dsh_m, dsc_m, dg_m, dsh_f, dsc_f, dg_f], axis=1)
    return (dx, gw, gs, dmod) if tgt is None else (dx, gw, gs, dmod, loss_part[0])


def _small_shapes():
    return {"b_ada": (6 * D_MODEL,), "b_gate": (3 * D_MODEL,), "w_pool": (4, POOL_GROUP, POOL_GROUP),
            "pool_scale": (D_POOL,), "rel_bias": (N_HEADS, N_REL), "conv_w": (CONV_WIDTH, D_CONV),
            "conv_b": (D_CONV,), "conv_ln_g": (D_CONV,), "conv_ln_b": (D_CONV,), "ln_mix_g": (D_MODEL,),
            "ln_mix_b": (D_MODEL,), "b_ff1": (D_FF,), "b_ff2": (D_MODEL,), "ln_ff_g": (D_MODEL,), "ln_ff_b": (D_MODEL,)}


def kernel(x, c, w_ada, b_ada, w_in, b_gate, w_pool, pool_scale, rel_bias, conv_w, conv_b, conv_ln_g, conv_ln_b, w_br_pool, w_br_attn, w_br_conv, w_o, ln_mix_g, ln_mix_b, w_ff1, b_ff1, w_ff2, b_ff2, ln_ff_g, ln_ff_b, loss_target, m_w_ada, m_b_ada, m_w_in, m_b_gate, m_w_pool, m_pool_scale, m_rel_bias, m_conv_w, m_conv_b, m_conv_ln_g, m_conv_ln_b, m_w_br_pool, m_w_br_attn, m_w_br_conv, m_w_o, m_ln_mix_g, m_ln_mix_b, m_w_ff1, m_b_ff1, m_w_ff2, m_b_ff2, m_ln_ff_g, m_ln_ff_b, v_w_ada, v_b_ada, v_w_in, v_b_gate, v_w_pool, v_pool_scale, v_rel_bias, v_conv_w, v_conv_b, v_conv_ln_g, v_conv_ln_b, v_w_br_pool, v_w_br_attn, v_w_br_conv, v_w_o, v_ln_mix_g, v_ln_mix_b, v_w_ff1, v_b_ff1, v_w_ff2, v_b_ff2, v_ln_ff_g, v_ln_ff_b):
    env = dict(locals())
    xi, yi, ci = _me()
    chip = 2 * xi + yi
    me = 4 * xi + 2 * yi + ci
    xs = x[0]
    tgt = loss_target[0]
    L = DEPTH

    first = _allgather_small(jnp.concatenate([c.reshape(8, 128), _pack([conv_w]).reshape(-1, 128)]), "gather_c_conv_w")
    c_all = first[:, :8].reshape(N_DEV, D_MODEL)
    ada_cols = w_ada.shape[2]
    b_ada_sh = lax.dynamic_slice_in_dim(b_ada, chip * ada_cols, ada_cols, axis=1).reshape(L, 1, ada_cols)
    mod_part = _mod_fwd(c_all, w_ada, b_ada_sh, "mod_fwd")
    mod_g = _allgather_small(mod_part.reshape(-1, 128), "gather_mod").reshape(N_CHIP, 2, L, N_DEV, ada_cols)[:, 0]
    mod_all = jnp.transpose(mod_g, (1, 2, 0, 3)).reshape(L, N_DEV, 6 * D_MODEL)
    mod = lax.dynamic_index_in_dim(mod_all, me, axis=1, keepdims=False)

    W = {k: [None] * L for k in BIG}

    def weight_gather(*items):
        shards = [(jnp.swapaxes(env[k][l], 0, 1) if k in COL_SHARDED else env[k][l]).astype(BF16) for k, l in items]
        shards = [a.reshape(2, a.shape[0] // 2, a.shape[1]) for a in shards]

        def done(outs):
            for (k, l), g in zip(items, outs):
                W[k][l] = g.reshape(-1, g.shape[-1])

        return _gather_rider(shards), done

    branch = lambda l: [(k, l) for k in ("w_br_pool", "w_br_attn", "w_br_conv", "w_o")]
    rider, done = weight_gather(("w_in", 0))
    done(_run_rider(rider, "gather_w_in0"))
    fwd_hooks = [{"z_gate": weight_gather(*branch(0)), "attn": weight_gather(("w_ff1", 0), ("w_ff2", 0)),
                  "ff1": weight_gather(("w_in", 1)), "ff2": weight_gather(*branch(1))},
                 {"attn": weight_gather(("w_ff1", 1), ("w_ff2", 1))}]

    P = {k: env[k] for k in ("rel_bias", "conv_w")}
    for k in ("b_gate", "pool_scale", "conv_b", "conv_ln_g", "conv_ln_b", "ln_mix_g", "ln_mix_b", "b_ff1", "b_ff2",
              "ln_ff_g", "ln_ff_b"):
        P[k] = env[k].reshape(L, 1, -1)
    n_cw = conv_w.size
    cw = first[:, 8:].reshape(N_CHIP, 2, -1)[:, 0, :n_cw].reshape(N_CHIP, L, CONV_WIDTH, D_CONV // N_CHIP)
    P["conv_w"] = jnp.transpose(cw, (1, 2, 0, 3)).reshape(L, CONV_WIDTH, D_CONV)
    wp_bd = jnp.zeros((L, D_POOL, D_POOL), F32)
    for g in range(len(POOL_WINDOWS)):
        sl = slice(POOL_GROUP * g, POOL_GROUP * (g + 1))
        wp_bd = wp_bd.at[:, sl, sl].set(w_pool[:, g])
    P["wp_bd"] = wp_bd.astype(BF16)

    acts = []
    h = xs
    for l in range(L):
        h, saved = _layer_fwd(l, h, mod, W, P, fwd_hooks[l], u=acts[-1]["u_next"] if acts else None)
        acts.append(saved)

    place = jnp.stack([ci, chip, chip ^ 1, chip ^ 2, chip ^ 3]).astype(jnp.int32)
    scattered = {}

    def grad_scatter(items, tag):
        dws = [dw for _, _, dw in items]
        got = _sibling_send(dws, f"swap_blocks_{tag}", other_half=True)
        both = [_sum_cores(a, b, place, f"sum_cores_{k}{l}") for (k, l, _), a, b in zip(items, dws, got)]
        both = [hh.reshape(N_CHIP, -1, hh.shape[-1]) for hh in both]

        def done(outs):
            for (k, l, _), hh, r in zip(items, both, outs):
                scattered[(k, l)] = (hh, r)

        return _scatter_rider(both), done

    def scatter_hook(names, l, host):
        return lambda gw: grad_scatter([(k, l, gw[k]) for k in names], f"{host}{l}")

    gws, gss, dmods = [None] * L, [None] * L, [None] * L
    dh = h
    for l in reversed(range(L)):
        hooks = {"ff_hidden_bwd": scatter_hook(("w_ff2",), l, "ff_hidden_bwd"),
                 "du_ff": scatter_hook(("w_ff1",), l, "du_ff"),
                 "attn": scatter_hook(("w_o", "w_br_pool", "w_br_attn", "w_br_conv"), l, "attn_bwd"),
                 "du_mix": scatter_hook(("w_in",), l, "du_mix")}
        below = None
        if l > 0:
            below = (acts[l - 1]["x1"], acts[l - 1]["ff"], mod[l - 1:l, 5 * D_MODEL:], P["ln_ff_g"][l - 1])
        if l == L - 1:
            dh, gws[l], gss[l], dmods[l], loss_part = _layer_bwd(l, dh, mod, W, P, acts[l], hooks, tgt=tgt, nxt=below)
        else:
            dh, gws[l], gss[l], dmods[l] = _layer_bwd(l, dh, mod, W, P, acts[l], hooks, nxt=below)
    grad_x = dh[None]
    loss = lax.psum(loss_part[0, 0], ("x", "y", "c"))

    reduced = [[_sum_chips(*scattered[(k, l)], place, f"sum_chips_{k}{l}") for l in range(L)] for k in BIG]
    flat_reduced = [t for per_weight in reduced for t in per_weight]

    shapes = _small_shapes()
    small_names = [k for k in SMALL if k != "b_ada"]
    dmod_own = jnp.concatenate(dmods, axis=0)
    pack = _pack([dmod_own] + [jnp.stack([gss[l][k].reshape(shapes[k]) for l in range(L)]) for k in small_names])
    last = _run_rider(_join_riders(_sibling_rider(flat_reduced), _allgather_rider(pack.reshape(-1, 128))),
                      "swap_reduced_gather_small")
    flat_other, g_all = last[:-1], last[-1].reshape(N_DEV, -1, PACK_W)

    out = {}
    for j, k in enumerate(BIG):
        own, other = reduced[j], flat_other[L * j:L * (j + 1)]
        if k == "w_in":
            t = lambda a: jnp.swapaxes(a, 1, 2)
            res = _adamw_halves(t(env[k]), t(env["m_" + k]), t(env["v_" + k]), own, other, place, "cols", f"adamw_{k}")
            res = [t(a) for a in res]
        else:
            if k in COL_SHARDED:
                own, other = [a.T for a in own], [a.T for a in other]
            res = _adamw_halves(env[k], env["m_" + k], env["v_" + k], own, other, place,
                                "rows" if k in COL_SHARDED else "cols", f"adamw_{k}")
        out[k] = tuple(res)

    dmod_all = g_all[:, :L * 6].reshape(N_DEV, L, 6 * D_MODEL)
    dmod_sh = jnp.transpose(lax.dynamic_slice_in_dim(dmod_all, chip * ada_cols, ada_cols, axis=2), (1, 0, 2))
    g_ada = _mod_bwd(c_all, dmod_sh, "mod_bwd")
    g_, d_, m_, v_ = _adamw(w_ada.reshape(-1, ada_cols), m_w_ada.reshape(-1, ada_cols), v_w_ada.reshape(-1, ada_cols),
                            [g_ada.reshape(-1, ada_cols)], "adamw_w_ada")
    out["w_ada"] = tuple(a.reshape(w_ada.shape) for a in (g_, d_, m_, v_))

    def small_pack(prefix):
        parts = [env[prefix + "b_ada"]]
        for k in small_names:
            a = env[prefix + k]
            if k == "conv_w":
                a = jnp.zeros((L,) + shapes[k], F32)
            parts.append(a)
        return _pack(parts)

    gp, dp, mp, vp = _adamw_small(small_pack(""), small_pack("m_"), small_pack("v_"), g_all, "adamw_small")
    full_shapes = [(L,) + shapes["b_ada"]] + [(L,) + shapes[k] for k in small_names]
    for tag, packed in (("g", gp), ("d", dp), ("m", mp), ("v", vp)):
        for k, a in zip(["b_ada"] + small_names, _unpack(packed, full_shapes)):
            out.setdefault(k, {})
            out[k][tag] = a
    g_cw_full = out["conv_w"]["g"]
    cw_cols = D_CONV // N_CHIP
    g_cw = lax.dynamic_slice_in_dim(g_cw_full, chip * cw_cols, cw_cols, axis=2)
    pad_rows = lambda a: jnp.pad(a.reshape(L * CONV_WIDTH, cw_cols), ((0, 2), (0, 0)))
    g_, d_, m_, v_ = _adamw(pad_rows(conv_w), pad_rows(m_conv_w), pad_rows(v_conv_w), [pad_rows(g_cw)], "adamw_conv_w")
    out["conv_w"] = tuple(a[:L * CONV_WIDTH].reshape(L, CONV_WIDTH, cw_cols) for a in (g_, d_, m_, v_))

    names = ["w_ada", "b_ada", "w_in", "b_gate", "w_pool", "pool_scale", "rel_bias", "conv_w", "conv_b", "conv_ln_g",
             "conv_ln_b", "w_br_pool", "w_br_attn", "w_br_conv", "w_o", "ln_mix_g", "ln_mix_b", "w_ff1", "b_ff1",
             "w_ff2", "b_ff2", "ln_ff_g", "ln_ff_b"]

    def pick(k, i):
        o = out[k]
        return o[i] if isinstance(o, tuple) else o["gdmv"[i]].reshape(env[k].shape)

    return (loss, grad_x, *[pick(k, 0) for k in names], *[pick(k, 1) for k in names],
            *[pick(k, 2) for k in names], *[pick(k, 3) for k in names])
```

```python
import jax
import jax.numpy as jnp
import numpy as np
from jax import lax
from jax.experimental import pallas as pl
from jax.experimental.pallas import tpu as pltpu

F32 = jnp.float32
BF16 = jnp.bfloat16

D_MODEL = 1024
DEPTH = 2
CHUNK = 64
POOL_WINDOWS = (2, 4, 8, 16)
POOL_GROUP = 64
D_POOL = 256
N_HEADS = 8
HEAD_DIM = 64
D_ATTN = 512
N_PREV_CHUNKS = 8
REL_CLIP = 128
N_REL = 2 * REL_CLIP + 1
D_CONV = 256
CONV_WIDTH = 31
D_FF = 4 * D_MODEL
D_IN = 5376
OFF_POOL, OFF_QKV, OFF_CONV, OFF_GATE = 0, 256, 1792, 2304
ALPHA = (2.0 * DEPTH) ** 0.25
LN_EPS = 1e-5
NEG_INF = -1e30
ADAM_LR, ADAM_B1, ADAM_B2, ADAM_EPS, ADAM_WD, ADAM_STEP = 0.001, 0.9, 0.999, 1e-08, 0.01, 10

N_DEV = 8
N_CHIP = 4
MESH = pl.DeviceIdType.MESH

QB = 2 * CHUNK
KPAD = N_PREV_CHUNKS * CHUNK
KW = QB + KPAD
SKEW_W = 768

VMEM_LIMIT = 56 * 1024 * 1024


def _cparams(**kw):
    return pltpu.CompilerParams(vmem_limit_bytes=VMEM_LIMIT, **kw)


def _full(shape):
    n = len(shape)
    return pl.BlockSpec(shape, lambda *_: (0,) * n)


_DIMS = {"nn": (((1,), (0,)), ((), ())), "nt": (((1,), (1,)), ((), ())), "tn": (((0,), (0,)), ((), ()))}


def _relu2(t):
    r = jnp.maximum(t, 0.0)
    return r * r


def _mm(a, b, mode, *, tm, tn, out_dtype, name, b_col0=0, n_out=None, bias=None, split_n=0, rider=None):
    if mode == "tn":
        k, m = a.shape
        n = b.shape[1] if n_out is None else n_out
        a_spec = pl.BlockSpec((k, tm), lambda i, j: (0, i))
        b_spec = pl.BlockSpec((k, tn), lambda i, j: (0, j + b_col0))
    elif mode == "nn":
        m, k = a.shape
        n = b.shape[1] if n_out is None else n_out
        a_spec = pl.BlockSpec((tm, k), lambda i, j: (i, 0))
        b_spec = pl.BlockSpec((k, tn), lambda i, j: (0, j + b_col0))
    else:
        m, k = a.shape
        n = b.shape[0] if n_out is None else n_out
        a_spec = pl.BlockSpec((tm, k), lambda i, j: (i, 0))
        b_spec = pl.BlockSpec((tn, k), lambda i, j: (j + b_col0, 0))
    assert m % tm == 0 and n % tn == 0, (name, m, n, tm, tn)
    dims = _DIMS[mode]

    def body(*refs):
        if bias is None:
            a_ref, b_ref, o_ref = refs
        else:
            a_ref, b_ref, bias_ref, o_ref = refs
        acc = lax.dot_general(a_ref[...].astype(BF16), b_ref[...].astype(BF16), dims, preferred_element_type=F32)
        if bias is not None:
            acc = acc + bias_ref[...]
        if split_n:
            for c in range(tn // split_n):
                o_ref[c] = acc[:, c * split_n:(c + 1) * split_n].astype(out_dtype)
        else:
            o_ref[...] = acc.astype(out_dtype)

    in_specs = [a_spec, b_spec]
    args = [a, b]
    if bias is not None:
        in_specs.append(pl.BlockSpec((1, tn), lambda i, j: (0, j)))
        args.append(bias)
    if split_n:
        out_spec = pl.BlockSpec((tn // split_n, tm, split_n), lambda i, j: (j, i, 0))
        out_shape = jax.ShapeDtypeStruct((n // split_n, m, split_n), out_dtype)
    else:
        out_spec = pl.BlockSpec((tm, tn), lambda i, j: (i, j))
        out_shape = jax.ShapeDtypeStruct((m, n), out_dtype)
    res = _call(body, name=name, grid=(m // tm, n // tn), in_specs=in_specs, out_specs=[out_spec],
                out_shape=[out_shape], scratch_shapes=[], args=args, rider=rider)
    return res[0] if rider is None else (res[0][0], res[1])


def _ln_hat(x):
    mu = jnp.mean(x, axis=-1, keepdims=True)
    xc = x - mu
    var = jnp.mean(xc * xc, axis=-1, keepdims=True)
    rstd = lax.rsqrt(var + LN_EPS)
    return xc * rstd, rstd


def _ln_hat_bwd(dhat, xhat, rstd):
    m1 = jnp.mean(dhat, axis=-1, keepdims=True)
    m2 = jnp.mean(dhat * xhat, axis=-1, keepdims=True)
    return rstd * (dhat - m1 - xhat * m2)


def _row_tile(s):
    return min(512, s)


def _acc_rows(ref, val, first):
    @pl.when(first)
    def _():
        ref[...] = jnp.zeros_like(ref)
    ref[...] += jnp.sum(val, axis=0, keepdims=True)


def _ln_mod(x, sc, sh, name):
    s, d = x.shape
    tm = _row_tile(s)

    def body(x_ref, sc_ref, sh_ref, u_ref):
        xhat, _ = _ln_hat(x_ref[...])
        u_ref[...] = (xhat * (1.0 + sc_ref[...]) + sh_ref[...]).astype(BF16)

    row = pl.BlockSpec((tm, d), lambda i: (i, 0))
    vec = pl.BlockSpec((1, d), lambda i: (0, 0))
    return pl.pallas_call(body, grid=(s // tm,), in_specs=[row, vec, vec], out_specs=row,
                          out_shape=jax.ShapeDtypeStruct((s, d), BF16), name=name, compiler_params=_cparams())(x, sc, sh)


def _resid_bwd_tile(dxo, x, f, g, gam):
    rhat, rstd = _ln_hat(ALPHA * x + g * f)
    dr = _ln_hat_bwd(dxo * gam, rhat, rstd)
    return ALPHA * dr, g * dr, dxo * rhat, dr * f


def _mm_ln_mod_bwd(a, b, x, sc, dres, name, rider=None, nxt=None):
    segs = list(a) if isinstance(a, (list, tuple)) else [a]
    s = segs[0].shape[0]
    k, d = b.shape
    assert sum(t.shape[1] for t in segs) == k
    tm = min(512 if k <= 4096 and nxt is None else 256, s)
    ns = len(segs)

    def body(*refs):
        seg_refs = refs[:ns]
        if nxt is None:
            b_ref, x_ref, sc_ref, dres_ref, dx_ref, dsc_ref, dsh_ref = refs[ns:-1]
        else:
            (b_ref, x_ref, sc_ref, dres_ref, xp_ref, fp_ref, gp_ref, gamp_ref,
             dresp_ref, dfp_ref, dsc_ref, dsh_ref, dgam_ref, dbet_ref, dg_ref, dbias_ref) = refs[ns:-1]
        du_buf = refs[-1]
        step = pl.program_id(0)
        live = step > 0
        sums = [dsc_ref, dsh_ref] + ([] if nxt is None else [dgam_ref, dbet_ref, dg_ref, dbias_ref])

        @pl.when(step == 0)
        def _():
            du_buf[...] = jnp.zeros_like(du_buf)
            for ref in sums:
                ref[...] = jnp.zeros_like(ref)

        duv = du_buf[...]
        new, r0 = None, 0
        for seg_ref in seg_refs:
            w = seg_ref.shape[1]
            part = jnp.dot(seg_ref[...], b_ref[r0:r0 + w, :], preferred_element_type=F32)
            new = part if new is None else new + part
            r0 += w
        xhat, rstd = _ln_hat(x_ref[...])
        dxv = dres_ref[...] + _ln_hat_bwd(duv * (1.0 + sc_ref[...]), xhat, rstd)
        tiles = [duv * xhat, duv]
        if nxt is None:
            dx_ref[...] = dxv
        else:
            dresp, dfp, t_gam, t_g = _resid_bwd_tile(dxv, xp_ref[...], fp_ref[...], gp_ref[...], gamp_ref[...])
            dresp_ref[...] = dresp
            dfp_ref[...] = dfp.astype(BF16)
            tiles += [t_gam, dxv, t_g, dfp]
        for ref, t in zip(sums, tiles):
            ref[...] += jnp.where(live, jnp.sum(t, axis=0, keepdims=True), 0.0)
        du_buf[...] = new

    n = s // tm
    row = pl.BlockSpec((tm, d), lambda i: (jnp.maximum(i - 1, 0), 0))
    vec = pl.BlockSpec((1, d), lambda i: (0, 0))
    vs = jax.ShapeDtypeStruct((1, d), F32)
    rows = jax.ShapeDtypeStruct((s, d), F32)
    in_specs = [pl.BlockSpec((tm, t.shape[1]), lambda i: (jnp.minimum(i, n - 1), 0)) for t in segs]
    in_specs += [_full((k, d)), row, vec, row]
    args = (*segs, b, x, sc, dres)
    if nxt is None:
        out_specs, out_shape = [row, vec, vec], [rows, vs, vs]
    else:
        in_specs += [row, row, vec, vec]
        args += tuple(nxt)
        out_specs = [row, row] + [vec] * 6
        out_shape = [rows, jax.ShapeDtypeStruct((s, d), BF16)] + [vs] * 6
    res = _call(body, name=name, grid=(n + 1,), in_specs=in_specs, out_specs=out_specs, out_shape=out_shape,
                scratch_shapes=[pltpu.VMEM((tm, d), F32)], args=args, rider=rider)
    return tuple(res) if rider is None else (tuple(res[0]), res[1])


def _dw_segments(segs, u, name):
    s, d = u.shape
    tw = 256
    tiles = [t.shape[1] // tw for t in segs]
    starts = [sum(tiles[:j]) for j in range(len(segs))]
    ns = len(segs)

    def body(*refs):
        seg_refs, u_ref, o_ref = refs[:ns], refs[ns], refs[ns + 1]
        i = pl.program_id(0)
        for seg_ref, t0, nt in zip(seg_refs, starts, tiles):
            @pl.when((i >= t0) & (i < t0 + nt))
            def _(seg_ref=seg_ref):
                acc = lax.dot_general(seg_ref[...], u_ref[...], _DIMS["tn"], preferred_element_type=F32)
                o_ref[0] = acc[:, :d // 2].astype(BF16)
                o_ref[1] = acc[:, d // 2:].astype(BF16)

    def seg_spec(t0, nt):
        return pl.BlockSpec((s, tw), lambda i: (0, jnp.clip(i - t0, 0, nt - 1)))

    return pl.pallas_call(
        body, grid=(sum(tiles),), in_specs=[seg_spec(t0, nt) for t0, nt in zip(starts, tiles)] + [_full((s, d))],
        out_specs=pl.BlockSpec((2, tw, d // 2), lambda i: (0, i, 0)),
        out_shape=jax.ShapeDtypeStruct((2, sum(tiles) * tw, d // 2), BF16), name=name, compiler_params=_cparams(),
    )(*segs, u)


def _mm_resid_ln(a, b, bias, x, g, gam, bet, name, rider=None, mod_next=None):
    s, k = a.shape
    d = b.shape[1]
    tm = min(512, s)
    nb, nm = int(bias is not None), 2 * int(mod_next is not None)

    def body(*refs):
        a_ref, b_ref = refs[:2]
        x_ref, g_ref, gam_ref, bet_ref = refs[2 + nb:6 + nb]
        f_ref, o_ref = refs[6 + nb + nm:8 + nb + nm]
        f = jnp.dot(a_ref[...], b_ref[...], preferred_element_type=F32)
        if bias is not None:
            f = f + refs[2][...]
        f_ref[...] = f
        rhat, _ = _ln_hat(ALPHA * x_ref[...] + g_ref[...] * f)
        y = rhat * gam_ref[...] + bet_ref[...]
        o_ref[...] = y
        if mod_next is not None:
            sc_ref, sh_ref = refs[6 + nb:8 + nb]
            yhat, _ = _ln_hat(y)
            refs[8 + nb + nm][...] = (yhat * (1.0 + sc_ref[...]) + sh_ref[...]).astype(BF16)

    row = pl.BlockSpec((tm, d), lambda i: (i, 0))
    vec = pl.BlockSpec((1, d), lambda i: (0, 0))
    in_specs = [pl.BlockSpec((tm, k), lambda i: (i, 0)), _full((k, d))] + [vec] * nb + [row, vec, vec, vec] + [vec] * nm
    args = [a, b] + ([bias] if nb else []) + [x, g, gam, bet] + (list(mod_next) if nm else [])
    sh = jax.ShapeDtypeStruct((s, d), F32)
    out_specs, out_shape = [row, row], [sh, sh]
    if nm:
        out_specs, out_shape = out_specs + [row], out_shape + [jax.ShapeDtypeStruct((s, d), BF16)]
    res = _call(body, name=name, grid=(s // tm,), in_specs=in_specs, out_specs=out_specs, out_shape=out_shape,
                scratch_shapes=[], args=args, rider=rider)
    return tuple(res) if rider is None else (tuple(res[0]), res[1])


def _resid_ln_bwd(dxo, x, f, g, gam, name, tgt=None):
    s, d = x.shape
    tm = _row_tile(s)
    n = s // tm

    def body(*refs):
        if tgt is None:
            dxo_ref, x_ref, f_ref, g_ref, gam_ref, dres_ref, df_ref, dgam_ref, dbet_ref, dg_ref, dbias_ref = refs
            dxov = dxo_ref[...]
        else:
            (dxo_ref, t_ref, x_ref, f_ref, g_ref, gam_ref, dres_ref, df_ref, dgam_ref, dbet_ref, dg_ref, dbias_ref,
             loss_ref, sq_ref) = refs
            err = dxo_ref[...] - t_ref[...]
            dxov = err * (1.0 / d)
            _acc_rows(sq_ref, err * err, pl.program_id(0) == 0)

            @pl.when(pl.program_id(0) == n - 1)
            def _():
                tot = jnp.sum(sq_ref[...], axis=1, keepdims=True) * (0.5 / d)
                loss_ref[...] = jnp.broadcast_to(tot, (1, 128))

        first = pl.program_id(0) == 0
        dres, dfv, t_gam, t_g = _resid_bwd_tile(dxov, x_ref[...], f_ref[...], g_ref[...], gam_ref[...])
        dres_ref[...] = dres
        df_ref[...] = dfv.astype(BF16)
        _acc_rows(dgam_ref, t_gam, first)
        _acc_rows(dbet_ref, dxov, first)
        _acc_rows(dg_ref, t_g, first)
        _acc_rows(dbias_ref, dfv, first)

    row = pl.BlockSpec((tm, d), lambda i: (i, 0))
    vec = pl.BlockSpec((1, d), lambda i: (0, 0))
    vs = jax.ShapeDtypeStruct((1, d), F32)
    out_specs = [row, row, vec, vec, vec, vec]
    out_shape = [jax.ShapeDtypeStruct((s, d), F32), jax.ShapeDtypeStruct((s, d), BF16), vs, vs, vs, vs]
    if tgt is None:
        return pl.pallas_call(body, grid=(n,), in_specs=[row, row, row, vec, vec], out_specs=out_specs,
                              out_shape=out_shape, name=name, compiler_params=_cparams())(dxo, x, f, g, gam)
    return pl.pallas_call(body, grid=(n,), in_specs=[row, row, row, row, vec, vec],
                          out_specs=out_specs + [pl.BlockSpec((1, 128), lambda i: (0, 0))],
                          out_shape=out_shape + [jax.ShapeDtypeStruct((1, 128), F32)],
                          scratch_shapes=[pltpu.VMEM((1, d), F32)], name=name,
                          compiler_params=_cparams())(dxo, tgt, x, f, g, gam)


POOL_HALO = 16
POOL_ROWS = 256


def _pool_counts(r0, rows):
    t1 = (lax.broadcasted_iota(jnp.int32, (rows, 128), 0) + r0 + 1).astype(F32)
    low = lax.broadcasted_iota(jnp.int32, (rows, 128), 1) < POOL_GROUP
    wa = jnp.where(low, float(POOL_WINDOWS[0]), float(POOL_WINDOWS[1]))
    wb = jnp.where(low, float(POOL_WINDOWS[2]), float(POOL_WINDOWS[3]))
    return jnp.minimum(t1, wa), jnp.minimum(t1, wb), low


def _window_sums(win, off, rows, sign):
    def sl(j, half):
        return win[off + sign * j: off + sign * j + rows, 128 * half:128 * half + 128]
    a2 = sl(0, 0) + sl(1, 0)
    a4 = a2 + sl(2, 0) + sl(3, 0)
    a8 = sl(0, 1)
    for j in range(1, 8):
        a8 = a8 + sl(j, 1)
    a16 = a8
    for j in range(8, 16):
        a16 = a16 + sl(j, 1)
    return a2, a4, a8, a16


def _pool_fwd(zp, wp_bd, pscale, name):
    s = zp.shape[0]
    r = min(POOL_ROWS, s)

    def body(z_ref, wp_ref, sc_ref, p_ref, feat_ref, pad):
        pad[0:POOL_HALO, :] = jnp.zeros((POOL_HALO, D_POOL), F32)
        pad[POOL_HALO:, :] = z_ref[...]

        def step(i, carry):
            r0 = pl.multiple_of(i * r, r)
            win = pad[pl.ds(r0, r + POOL_HALO), :]
            a2, a4, a8, a16 = _window_sums(win, POOL_HALO, r, -1)
            ca, cb, low = _pool_counts(r0, r)
            x0 = win[POOL_HALO:, :]
            pa = jnp.where(low, a2, a4) / ca
            pb = jnp.where(low, a8, a16) / cb
            p = (jnp.concatenate([pa, pb], axis=1) - x0).astype(BF16)
            p_ref[pl.ds(r0, r), :] = p
            pw = jnp.dot(p, wp_ref[...], preferred_element_type=F32)
            feat_ref[pl.ds(r0, r), :] = (pw * sc_ref[...]).astype(BF16)
            return carry

        lax.fori_loop(0, s // r, step, 0)

    return pl.pallas_call(
        body, out_shape=[jax.ShapeDtypeStruct((s, D_POOL), BF16), jax.ShapeDtypeStruct((s, D_POOL), BF16)],
        scratch_shapes=[pltpu.VMEM((s + POOL_HALO, D_POOL), F32)], name=name, compiler_params=_cparams(),
    )(zp, wp_bd, pscale)


def _pool_bwd(dfeat, p, wp_bd, pscale, name):
    s = p.shape[0]
    r = min(POOL_ROWS, s)

    def body(df_ref, p_ref, wp_ref, sc_ref, dz_ref, dwp_ref, dsc_ref, gpad, dpbuf):
        dwp_ref[...] = jnp.zeros_like(dwp_ref)
        dsc_ref[...] = jnp.zeros_like(dsc_ref)
        gpad[s:, :] = jnp.zeros((POOL_HALO, D_POOL), F32)

        def step1(i, carry):
            r0 = pl.multiple_of(i * r, r)
            pv = p_ref[pl.ds(r0, r), :]
            dfv = df_ref[pl.ds(r0, r), :]
            pw = jnp.dot(pv, wp_ref[...], preferred_element_type=F32)
            dsc_ref[...] += jnp.sum(dfv * pw, axis=0, keepdims=True)
            dpw = (dfv * sc_ref[...]).astype(BF16)
            dwp_ref[...] += lax.dot_general(pv, dpw, _DIMS["tn"], preferred_element_type=F32)
            dp = lax.dot_general(dpw, wp_ref[...], _DIMS["nt"], preferred_element_type=F32)
            ca, cb, _ = _pool_counts(r0, r)
            gpad[pl.ds(r0, r), :] = dp / jnp.concatenate([ca, cb], axis=1)
            dpbuf[pl.ds(r0, r), :] = dp
            return carry

        lax.fori_loop(0, s // r, step1, 0)

        def step2(i, carry):
            r0 = pl.multiple_of(i * r, r)
            win = gpad[pl.ds(r0, r + POOL_HALO), :]
            a2, a4, a8, a16 = _window_sums(win, 0, r, 1)
            low = lax.broadcasted_iota(jnp.int32, (r, 128), 1) < POOL_GROUP
            acc = jnp.concatenate([jnp.where(low, a2, a4), jnp.where(low, a8, a16)], axis=1)
            dz_ref[pl.ds(r0, r), :] = (acc - dpbuf[pl.ds(r0, r), :]).astype(BF16)
            return carry

        lax.fori_loop(0, s // r, step2, 0)

    return pl.pallas_call(
        body,
        out_shape=[jax.ShapeDtypeStruct((s, D_POOL), BF16), jax.ShapeDtypeStruct((D_POOL, D_POOL), F32),
                   jax.ShapeDtypeStruct((1, D_POOL), F32)],
        scratch_shapes=[pltpu.VMEM((s + POOL_HALO, D_POOL), F32), pltpu.VMEM((s, D_POOL), F32)],
        name=name, compiler_params=_cparams(),
    )(dfeat, p, wp_bd, pscale)


def _skew_index():
    cp = lax.broadcasted_iota(jnp.int32, (SKEW_W, N_REL), 0)
    dist = jnp.where(cp < KW, KPAD - cp, KPAD + SKEW_W - cp)
    idx = jnp.clip(dist, -REL_CLIP, REL_CLIP) + REL_CLIP
    return (idx == lax.broadcasted_iota(jnp.int32, (SKEW_W, N_REL), 1)).astype(F32)


def _row_bits(b):
    return (lax.broadcasted_iota(jnp.int32, (QB, SKEW_W), 0) >> b) & 1 == 1


N_EDGE = KPAD // QB


def _bias_block(rel_bias, name):
    def body(rb_ref, o_ref):
        onehot = _skew_index()
        row0 = lax.dot_general(rb_ref[...], onehot, _DIMS["nt"], precision=lax.Precision.HIGHEST,
                               preferred_element_type=F32)
        r = lax.broadcasted_iota(jnp.int32, (QB, KW), 0)
        kk = lax.broadcasted_iota(jnp.int32, (QB, KW), 1)
        cq, ck = r // CHUNK, kk // CHUNK
        band = (ck >= cq) & (ck <= cq + N_PREV_CHUNKS)
        for h in range(N_HEADS):
            t = jnp.broadcast_to(row0[h:h + 1, :], (QB, SKEW_W))
            for b in range(7):
                t = jnp.where(_row_bits(b), pltpu.roll(t, 1 << b, 1), t)
            for e in range(N_EDGE + 1):
                o_ref[e, h] = jnp.where(band & (kk >= KPAD - e * QB), t[:, :KW], NEG_INF)

    return pl.pallas_call(body, out_shape=jax.ShapeDtypeStruct((N_EDGE + 1, N_HEADS, QB, KW), F32), name=name,
                          compiler_params=_cparams())(rel_bias)


def _bias_spec():
    return pl.BlockSpec((None, N_HEADS, QB, KW), lambda i: (jnp.minimum(i, N_EDGE), 0, 0, 0))


def _bias_block_bwd(ds_acc, name):
    def body(ds_ref, o_ref):
        sums = []
        for h in range(N_HEADS):
            t = jnp.concatenate([ds_ref[h], jnp.zeros((QB, SKEW_W - KW), F32)], axis=1)
            for b in range(7):
                t = jnp.where(_row_bits(b), pltpu.roll(t, SKEW_W - (1 << b), 1), t)
            sums.append(jnp.sum(t, axis=0, keepdims=True))
        allh = jnp.concatenate(sums, axis=0)
        o_ref[...] = jnp.dot(allh, _skew_index(), precision=lax.Precision.HIGHEST, preferred_element_type=F32)

    return pl.pallas_call(body, out_shape=jax.ShapeDtypeStruct((N_HEADS, N_REL), F32), name=name,
                          compiler_params=_cparams())(ds_acc)


def _scaled(q):
    return (q.astype(F32) * (HEAD_DIM ** -0.5)).astype(BF16)


def _probs(q, kw, bias_ref):
    sc = jnp.stack([lax.dot_general(q[:, HEAD_DIM * h:HEAD_DIM * (h + 1)], kw[:, HEAD_DIM * h:HEAD_DIM * (h + 1)],
                                    _DIMS["nt"], preferred_element_type=F32) + bias_ref[h] for h in range(N_HEADS)])
    e = jnp.exp(sc - jnp.max(sc, axis=-1, keepdims=True))
    return e * (1.0 / jnp.sum(e, axis=-1, keepdims=True))


def _load_padded_kv(qkv_hbm, kpad, vpad, sems, s):
    kpad[0:KPAD, :] = jnp.zeros((KPAD, D_ATTN), BF16)
    vpad[0:KPAD, :] = jnp.zeros((KPAD, D_ATTN), BF16)
    ck = pltpu.make_async_copy(qkv_hbm.at[:, D_ATTN:2 * D_ATTN], kpad.at[pl.ds(KPAD, s), :], sems.at[0])
    cv = pltpu.make_async_copy(qkv_hbm.at[:, 2 * D_ATTN:3 * D_ATTN], vpad.at[pl.ds(KPAD, s), :], sems.at[1])
    ck.start()
    cv.start()
    ck.wait()
    cv.wait()


def _attn_fwd(qkv, bias, name, rider=None):
    s = qkv.shape[0]

    def body(q_ref, qkv_hbm, bias_ref, o_ref, p_ref, kpad, vpad, sems):
        i = pl.program_id(0)

        @pl.when(i == 0)
        def _():
            _load_padded_kv(qkv_hbm, kpad, vpad, sems, s)

        base = pl.multiple_of(i * QB, QB)
        kw = kpad[pl.ds(base, KW), :]
        vw = vpad[pl.ds(base, KW), :]
        q = _scaled(q_ref[...])
        p = _probs(q, kw, bias_ref).astype(BF16)
        p_ref[...] = p
        outs = [jnp.dot(p[h], vw[:, HEAD_DIM * h:HEAD_DIM * (h + 1)], preferred_element_type=F32)
                for h in range(N_HEADS)]
        o_ref[...] = jnp.concatenate(outs, axis=1).astype(BF16)

    res = _call(
        body, name=name, grid=(s // QB,),
        in_specs=[pl.BlockSpec((QB, D_ATTN), lambda i: (i, 0)), pl.BlockSpec(memory_space=pl.ANY),
                  _bias_spec()],
        out_specs=[pl.BlockSpec((QB, D_ATTN), lambda i: (i, 0)), _probs_spec()],
        out_shape=[jax.ShapeDtypeStruct((s, D_ATTN), BF16), jax.ShapeDtypeStruct((N_HEADS, s, KW), BF16)],
        scratch_shapes=[pltpu.VMEM((s + KPAD, D_ATTN), BF16), pltpu.VMEM((s + KPAD, D_ATTN), BF16),
                        pltpu.SemaphoreType.DMA((2,))],
        args=(qkv, qkv, bias), rider=rider)
    return tuple(res) if rider is None else (tuple(res[0]), res[1])


def _probs_spec():
    return pl.BlockSpec((N_HEADS, QB, KW), lambda i: (0, i, 0))


def _attn_bwd(qkv, do, probs, name, rider=None):
    s = qkv.shape[0]
    n = s // QB

    def body(q_ref, qkv_hbm, do_ref, p_ref, dq_ref, dk_hbm, dv_hbm, ds_ref, kpad, vpad, dkacc, dvacc, sems):
        i = pl.program_id(0)

        @pl.when(i == 0)
        def _():
            _load_padded_kv(qkv_hbm, kpad, vpad, sems, s)
            dkacc[...] = jnp.zeros_like(dkacc)
            dvacc[...] = jnp.zeros_like(dvacc)
            ds_ref[...] = jnp.zeros_like(ds_ref)

        base = pl.multiple_of(i * QB, QB)
        kw = kpad[pl.ds(base, KW), :]
        vw = vpad[pl.ds(base, KW), :]
        q = _scaled(q_ref[...])
        dov = do_ref[...]
        heads = [slice(HEAD_DIM * h, HEAD_DIM * (h + 1)) for h in range(N_HEADS)]
        pb = p_ref[...]
        p = pb.astype(F32)
        dp = jnp.stack([lax.dot_general(dov[:, hs], vw[:, hs], _DIMS["nt"], preferred_element_type=F32) for hs in heads])
        ds = p * (dp - jnp.sum(dp * p, axis=-1, keepdims=True))
        ds_ref[...] += ds
        dsb = ds.astype(BF16)
        dvs = [lax.dot_general(pb[h], dov[:, hs], _DIMS["tn"], preferred_element_type=F32) for h, hs in enumerate(heads)]
        dqs = [jnp.dot(dsb[h], kw[:, hs], preferred_element_type=F32) for h, hs in enumerate(heads)]
        dks = [lax.dot_general(dsb[h], q[:, hs], _DIMS["tn"], preferred_element_type=F32) for h, hs in enumerate(heads)]
        dq_ref[...] = (jnp.concatenate(dqs, axis=1) * (HEAD_DIM ** -0.5)).astype(BF16)
        dkacc[pl.ds(base, KW), :] += jnp.concatenate(dks, axis=1)
        dvacc[pl.ds(base, KW), :] += jnp.concatenate(dvs, axis=1)

        @pl.when(i == n - 1)
        def _():
            def cast(j, carry):
                rows = pl.ds(pl.multiple_of(KPAD + j * 512, 512), 512)
                kpad[rows, :] = dkacc[rows, :].astype(BF16)
                vpad[rows, :] = dvacc[rows, :].astype(BF16)
                return carry

            lax.fori_loop(0, s // 512, cast, 0)
            ck = pltpu.make_async_copy(kpad.at[pl.ds(KPAD, s), :], dk_hbm, sems.at[0])
            cv = pltpu.make_async_copy(vpad.at[pl.ds(KPAD, s), :], dv_hbm, sems.at[1])
            ck.start()
            cv.start()
            ck.wait()
            cv.wait()

    blk = pl.BlockSpec((QB, D_ATTN), lambda i: (i, 0))
    acc_shape = jax.ShapeDtypeStruct((s, D_ATTN), BF16)
    return _call(
        body, name=name, grid=(n,),
        in_specs=[blk, pl.BlockSpec(memory_space=pl.ANY), blk, _probs_spec()],
        out_specs=[blk, pl.BlockSpec(memory_space=pl.ANY), pl.BlockSpec(memory_space=pl.ANY), _full((N_HEADS, QB, KW))],
        out_shape=[jax.ShapeDtypeStruct((s, D_ATTN), BF16), acc_shape, acc_shape,
                   jax.ShapeDtypeStruct((N_HEADS, QB, KW), F32)],
        scratch_shapes=[pltpu.VMEM((s + KPAD, D_ATTN), BF16), pltpu.VMEM((s + KPAD, D_ATTN), BF16),
                        pltpu.VMEM((s + KPAD, D_ATTN), F32), pltpu.VMEM((s + KPAD, D_ATTN), F32),
                        pltpu.SemaphoreType.DMA((2,))],
        args=(qkv, qkv, do, probs), rider=rider)


CONV_HALO = 32
CONV_ROWS = 64


def _sigmoid(t):
    return 1.0 / (1.0 + jnp.exp(-t))


CONV_WIN = CONV_ROWS + CONV_HALO - 8


def _row_windows(ref, r0, buf):
    win = ref[pl.ds(r0, CONV_ROWS + CONV_HALO), :]
    for j in range(1, 8):
        buf[j - 1] = win[j:j + CONV_WIN, :]

    def get(o):
        j, a = o % 8, o - o % 8
        if j == 0:
            return ref[pl.ds(r0 + a, CONV_ROWS), :]
        return buf[j - 1, a:a + CONV_ROWS, :]

    return get


def _glu_rows(z_ref, r0, rows):
    a = z_ref[pl.ds(r0, rows), 0:D_CONV]
    b = z_ref[pl.ds(r0, rows), D_CONV:2 * D_CONV]
    return a, _sigmoid(b)


def _conv_fwd(zc, conv_w, conv_b, ln_g, ln_b, name):
    s = zc.shape[0]
    rt = min(256, s)

    def body(z_ref, w_ref, cb_ref, g_ref, b_ref, cv_ref, feat_ref, hpad, shifts):
        hpad[0:CONV_HALO, :] = jnp.zeros((CONV_HALO, D_CONV), F32)

        def glu(i, carry):
            r0 = pl.multiple_of(i * rt, rt)
            a, sb = _glu_rows(z_ref, r0, rt)
            hpad[pl.ds(r0 + CONV_HALO, rt), :] = a * sb
            return carry

        lax.fori_loop(0, s // rt, glu, 0)
        w = w_ref[...]

        def conv(i, carry):
            r0 = pl.multiple_of(i * CONV_ROWS, CONV_ROWS)
            win = _row_windows(hpad, r0, shifts)
            acc = jnp.broadcast_to(cb_ref[...], (CONV_ROWS, D_CONV))
            for k in range(CONV_WIDTH):
                acc = acc + win(2 + k) * w[k:k + 1, :]
            cv_ref[pl.ds(r0, CONV_ROWS), :] = acc
            yhat, _ = _ln_hat(acc)
            y = yhat * g_ref[...] + b_ref[...]
            feat_ref[pl.ds(r0, CONV_ROWS), :] = (y * _sigmoid(y)).astype(BF16)
            return carry

        lax.fori_loop(0, s // CONV_ROWS, conv, 0)

    return pl.pallas_call(
        body, out_shape=[jax.ShapeDtypeStruct((s, D_CONV), F32), jax.ShapeDtypeStruct((s, D_CONV), BF16)],
        scratch_shapes=[pltpu.VMEM((s + CONV_HALO, D_CONV), F32), pltpu.VMEM((7, CONV_WIN, D_CONV), F32)],
        name=name, compiler_params=_cparams(),
    )(zc, conv_w, conv_b, ln_g, ln_b)


def _conv_bwd(dfeat, cv, zc, conv_w, ln_g, ln_b, name):
    s = zc.shape[0]
    rt = min(256, s)

    def body(df_ref, cv_ref, z_ref, w_ref, g_ref, b_ref, dz_ref, dw_ref, dcb_ref, dg_ref, db_ref, hpad, dcvpad, dwacc,
             hshifts, dshifts):
        hpad[0:CONV_HALO, :] = jnp.zeros((CONV_HALO, D_CONV), F32)
        dcvpad[s:, :] = jnp.zeros((CONV_HALO, D_CONV), F32)
        dwacc[...] = jnp.zeros_like(dwacc)
        dcb_ref[...] = jnp.zeros_like(dcb_ref)
        dg_ref[...] = jnp.zeros_like(dg_ref)
        db_ref[...] = jnp.zeros_like(db_ref)

        def pass1(i, carry):
            r0 = pl.multiple_of(i * rt, rt)
            a, sb = _glu_rows(z_ref, r0, rt)
            hpad[pl.ds(r0 + CONV_HALO, rt), :] = a * sb
            cvhat, rstd = _ln_hat(cv_ref[pl.ds(r0, rt), :])
            y = cvhat * g_ref[...] + b_ref[...]
            sg = _sigmoid(y)
            dy = df_ref[pl.ds(r0, rt), :] * (sg * (1.0 + y * (1.0 - sg)))
            dg_ref[...] += jnp.sum(dy * cvhat, axis=0, keepdims=True)
            db_ref[...] += jnp.sum(dy, axis=0, keepdims=True)
            dcv = _ln_hat_bwd(dy * g_ref[...], cvhat, rstd)
            dcb_ref[...] += jnp.sum(dcv, axis=0, keepdims=True)
            dcvpad[pl.ds(r0, rt), :] = dcv
            return carry

        lax.fori_loop(0, s // rt, pass1, 0)
        w = w_ref[...]

        def pass2(i, carry):
            r0 = pl.multiple_of(i * CONV_ROWS, CONV_ROWS)
            dwin = _row_windows(dcvpad, r0, dshifts)
            hwin = _row_windows(hpad, r0, hshifts)
            dcv = dwin(0)
            dh = jnp.zeros((CONV_ROWS, D_CONV), F32)
            for k in range(CONV_WIDTH):
                dh = dh + dwin(30 - k) * w[k:k + 1, :]
                prod = dcv * hwin(2 + k)
                dwacc[8 * k:8 * k + 8, :] += jnp.sum(prod.reshape(CONV_ROWS // 8, 8, D_CONV), axis=0)
            a, sb = _glu_rows(z_ref, r0, CONV_ROWS)
            dz_ref[pl.ds(r0, CONV_ROWS), :] = jnp.concatenate([dh * sb, dh * a * sb * (1.0 - sb)], axis=1).astype(BF16)
            return carry

        lax.fori_loop(0, s // CONV_ROWS, pass2, 0)
        dw_ref[...] = jnp.sum(dwacc[...].reshape(32, 8, D_CONV), axis=1)

    vs = jax.ShapeDtypeStruct((1, D_CONV), F32)
    return pl.pallas_call(
        body,
        out_shape=[jax.ShapeDtypeStruct((s, 2 * D_CONV), BF16), jax.ShapeDtypeStruct((32, D_CONV), F32), vs, vs, vs],
        scratch_shapes=[pltpu.VMEM((s + CONV_HALO, D_CONV), F32), pltpu.VMEM((s + CONV_HALO, D_CONV), F32),
                        pltpu.VMEM((256, D_CONV), F32), pltpu.VMEM((7, CONV_WIN, D_CONV), F32),
                        pltpu.VMEM((7, CONV_WIN, D_CONV), F32)],
        name=name, compiler_params=_cparams(),
    )(dfeat, cv, zc, conv_w, ln_g, ln_b)


def _branch_out(feats, wts, name):
    s = feats[0].shape[0]
    tm = min(1024, s)

    def body(*refs):
        for f_ref, w_ref, o_ref in zip(refs[:3], refs[3:6], refs[6:]):
            o_ref[...] = lax.dot_general(f_ref[...], w_ref[...], _DIMS["nt"], preferred_element_type=F32).astype(BF16)

    row = pl.BlockSpec((tm, D_MODEL), lambda i: (i, 0))
    sh = jax.ShapeDtypeStruct((s, D_MODEL), BF16)
    return pl.pallas_call(
        body, grid=(s // tm,),
        in_specs=[pl.BlockSpec((tm, f.shape[1]), lambda i: (i, 0)) for f in feats] + [_full(w.shape) for w in wts],
        out_specs=[row] * 3, out_shape=[sh] * 3, name=name, compiler_params=_cparams(),
    )(*feats, *wts)


def _branch_in_bwd(dys, wts, out_dtypes, name):
    s = dys[0].shape[0]
    tm = min(1024, s)

    def body(*refs):
        for d_ref, w_ref, o_ref in zip(refs[:3], refs[3:6], refs[6:]):
            o_ref[...] = jnp.dot(d_ref[...], w_ref[...], preferred_element_type=F32).astype(o_ref.dtype)

    row = pl.BlockSpec((tm, D_MODEL), lambda i: (i, 0))
    return pl.pallas_call(
        body, grid=(s // tm,), in_specs=[row] * 3 + [_full(w.shape) for w in wts],
        out_specs=[pl.BlockSpec((tm, w.shape[1]), lambda i: (i, 0)) for w in wts],
        out_shape=[jax.ShapeDtypeStruct((s, w.shape[1]), dt) for w, dt in zip(wts, out_dtypes)],
        name=name, compiler_params=_cparams(),
    )(*dys, *wts)


def _branch_dw(dys, feats, name):
    s = dys[0].shape[0]
    tm = 512

    def body(*refs):
        for d_ref, f_ref, o_ref in zip(refs[:3], refs[3:6], refs[6:]):
            acc = lax.dot_general(d_ref[...], f_ref[...], _DIMS["tn"], preferred_element_type=F32)
            half = acc.shape[1] // 2
            o_ref[0] = acc[:, :half].astype(BF16)
            o_ref[1] = acc[:, half:].astype(BF16)

    return pl.pallas_call(
        body, grid=(D_MODEL // tm,),
        in_specs=[pl.BlockSpec((s, tm), lambda i: (0, i))] * 3 + [_full(f.shape) for f in feats],
        out_specs=[pl.BlockSpec((2, tm, f.shape[1] // 2), lambda i: (0, i, 0)) for f in feats],
        out_shape=[jax.ShapeDtypeStruct((2, D_MODEL, f.shape[1] // 2), BF16) for f in feats],
        name=name, compiler_params=_cparams(),
    )(*dys, *feats)


def _merge(zg, b_gate, ys, name):
    s = zg.shape[0]
    tm = _row_tile(s)

    def body(zg_ref, bg_ref, y0_ref, y1_ref, y2_ref, o_ref):
        acc = None
        for j, y_ref in enumerate((y0_ref, y1_ref, y2_ref)):
            cs = slice(D_MODEL * j, D_MODEL * (j + 1))
            t = _sigmoid(zg_ref[:, cs] + bg_ref[:, cs]) * y_ref[...]
            acc = t if acc is None else acc + t
        o_ref[...] = acc.astype(BF16)

    row = pl.BlockSpec((tm, D_MODEL), lambda i: (i, 0))
    return pl.pallas_call(
        body, grid=(s // tm,),
        in_specs=[pl.BlockSpec((tm, 3 * D_MODEL), lambda i: (i, 0)), _full((1, 3 * D_MODEL)), row, row, row],
        out_specs=row, out_shape=jax.ShapeDtypeStruct((s, D_MODEL), BF16), name=name, compiler_params=_cparams(),
    )(zg, b_gate, *ys)


def _merge_bwd(dm, zg, b_gate, ys, name):
    s = zg.shape[0]
    tm = min(256, s)

    def body(dm_ref, zg_ref, bg_ref, y0_ref, y1_ref, y2_ref, d0_ref, d1_ref, d2_ref, dzg_ref, dbg_ref):
        first = pl.program_id(0) == 0

        @pl.when(first)
        def _():
            dbg_ref[...] = jnp.zeros_like(dbg_ref)

        dmv = dm_ref[...]
        for j, (y_ref, d_ref) in enumerate(((y0_ref, d0_ref), (y1_ref, d1_ref), (y2_ref, d2_ref))):
            cs = slice(D_MODEL * j, D_MODEL * (j + 1))
            g = _sigmoid(zg_ref[:, cs] + bg_ref[:, cs])
            d_ref[...] = (dmv * g).astype(BF16)
            dzg = dmv * y_ref[...] * g * (1.0 - g)
            dzg_ref[:, cs] = dzg.astype(BF16)
            dbg_ref[:, cs] += jnp.sum(dzg, axis=0, keepdims=True)

    row = pl.BlockSpec((tm, D_MODEL), lambda i: (i, 0))
    wide = pl.BlockSpec((tm, 3 * D_MODEL), lambda i: (i, 0))
    yb = jax.ShapeDtypeStruct((s, D_MODEL), BF16)
    return pl.pallas_call(
        body, grid=(s // tm,),
        in_specs=[row, wide, _full((1, 3 * D_MODEL)), row, row, row],
        out_specs=[row, row, row, wide, _full((1, 3 * D_MODEL))],
        out_shape=[yb, yb, yb, jax.ShapeDtypeStruct((s, 3 * D_MODEL), BF16), jax.ShapeDtypeStruct((1, 3 * D_MODEL), F32)],
        name=name, compiler_params=_cparams(),
    )(dm, zg, b_gate, *ys)


def _ff_hidden(u2, w_ff1t, b_ff1, name, rider=None):
    s = u2.shape[0]
    tm, tn = min(2048, s), 1024

    def body(a_ref, b_ref, bias_ref, pre_ref, h_ref):
        acc = lax.dot_general(a_ref[...], b_ref[...], _DIMS["nt"], preferred_element_type=F32) + bias_ref[...]
        pre_ref[...] = acc.astype(BF16)
        h_ref[...] = _relu2(acc).astype(BF16)

    blk = pl.BlockSpec((tm, tn), lambda i, j: (i, j))
    sh = jax.ShapeDtypeStruct((s, D_FF), BF16)
    res = _call(body, name=name, grid=(s // tm, D_FF // tn),
                in_specs=[pl.BlockSpec((tm, D_MODEL), lambda i, j: (i, 0)), pl.BlockSpec((tn, D_MODEL), lambda i, j: (j, 0)),
                          pl.BlockSpec((1, tn), lambda i, j: (0, j))],
                out_specs=[blk, blk], out_shape=[sh, sh], scratch_shapes=[], args=(u2, w_ff1t, b_ff1), rider=rider)
    return tuple(res) if rider is None else (tuple(res[0]), res[1])


def _ff_hidden_bwd(dff, w_ff2, hpre, name, rider=None):
    s = dff.shape[0]
    tm, tn = min(1024, s), 1024

    def body(a_ref, b_ref, h_ref, o_ref, sum_ref):
        dh = lax.dot_general(a_ref[...], b_ref[...], _DIMS["nt"], preferred_element_type=F32)
        dpre = dh * (2.0 * jnp.maximum(h_ref[...].astype(F32), 0.0))
        o_ref[...] = dpre.astype(BF16)
        _acc_rows(sum_ref, dpre, pl.program_id(1) == 0)

    res = _call(
        body, name=name, grid=(D_FF // tn, s // tm),
        in_specs=[pl.BlockSpec((tm, D_MODEL), lambda j, i: (i, 0)), pl.BlockSpec((tn, D_MODEL), lambda j, i: (j, 0)),
                  pl.BlockSpec((tm, tn), lambda j, i: (i, j))],
        out_specs=[pl.BlockSpec((tm, tn), lambda j, i: (i, j)), pl.BlockSpec((1, tn), lambda j, i: (0, j))],
        out_shape=[jax.ShapeDtypeStruct((s, D_FF), BF16), jax.ShapeDtypeStruct((1, D_FF), F32)],
        scratch_shapes=[], args=(dff, w_ff2, hpre), rider=rider)
    return tuple(res) if rider is None else (tuple(res[0]), res[1])


def _silu(t):
    return t * _sigmoid(t)


def _mod_fwd(c_all, w_ada_sh, b_ada_sh, name):
    cols = w_ada_sh.shape[2]

    def body(c_ref, w_ref, b_ref, o_ref):
        ca = _silu(c_ref[...]).astype(BF16)
        o_ref[0] = jnp.dot(ca, w_ref[0].astype(BF16), preferred_element_type=F32) + b_ref[0]

    return pl.pallas_call(
        body, grid=(DEPTH,),
        in_specs=[_full((N_DEV, D_MODEL)), pl.BlockSpec((1, D_MODEL, cols), lambda l: (l, 0, 0)),
                  pl.BlockSpec((1, 1, cols), lambda l: (l, 0, 0))],
        out_specs=pl.BlockSpec((1, N_DEV, cols), lambda l: (l, 0, 0)),
        out_shape=jax.ShapeDtypeStruct((DEPTH, N_DEV, cols), F32), name=name, compiler_params=_cparams(),
    )(c_all, w_ada_sh, b_ada_sh)


def _mod_bwd(c_all, dmod_sh, name):
    cols = dmod_sh.shape[2]

    def body(c_ref, d_ref, o_ref):
        ca = _silu(c_ref[...])
        o_ref[0] = lax.dot_general(ca, d_ref[0], _DIMS["tn"], precision=lax.Precision.HIGHEST,
                                   preferred_element_type=F32)

    return pl.pallas_call(
        body, grid=(DEPTH,),
        in_specs=[_full((N_DEV, D_MODEL)), pl.BlockSpec((1, N_DEV, cols), lambda l: (l, 0, 0))],
        out_specs=pl.BlockSpec((1, D_MODEL, cols), lambda l: (l, 0, 0)),
        out_shape=jax.ShapeDtypeStruct((DEPTH, D_MODEL, cols), F32), name=name, compiler_params=_cparams(),
    )(c_all, dmod_sh)


def _flat_tiles(rows, cols, itemsize_total):
    budget = 12 * 1024 * 1024
    tr = rows
    while tr % 32 == 0 and tr * cols * itemsize_total > budget:
        tr //= 2
    return tr


def _sum_cores(dw, recv, place, name):
    _, m, n = dw.shape
    tr = _flat_tiles(m, n, 6)

    def body(place_ref, a_ref, b_ref, o_ref):
        o_ref[...] = (a_ref[...].astype(F32) + b_ref[...].astype(F32)).astype(BF16)

    grid_spec = pltpu.PrefetchScalarGridSpec(
        num_scalar_prefetch=1, grid=(m // tr,),
        in_specs=[pl.BlockSpec((None, tr, n), lambda i, pr: (pr[0], i, 0)), pl.BlockSpec((tr, n), lambda i, pr: (i, 0))],
        out_specs=pl.BlockSpec((tr, n), lambda i, pr: (i, 0)))
    return pl.pallas_call(body, grid_spec=grid_spec, out_shape=jax.ShapeDtypeStruct((m, n), BF16), name=name,
                          compiler_params=_cparams())(place, dw, recv)


def _sum_chips(h, r, place, name):
    _, rs, n = h.shape
    tr = _flat_tiles(rs, n, 12)

    def body(place_ref, h_ref, r_ref, o_ref):
        o_ref[...] = ((h_ref[...].astype(F32) + r_ref[0].astype(F32)) + r_ref[1].astype(F32)) + r_ref[2].astype(F32)

    grid_spec = pltpu.PrefetchScalarGridSpec(
        num_scalar_prefetch=1, grid=(rs // tr,),
        in_specs=[pl.BlockSpec((None, tr, n), lambda i, pr: (pr[1], i, 0)), pl.BlockSpec((3, tr, n), lambda i, pr: (0, i, 0))],
        out_specs=pl.BlockSpec((tr, n), lambda i, pr: (i, 0)))
    return pl.pallas_call(body, grid_spec=grid_spec, out_shape=jax.ShapeDtypeStruct((rs, n), F32), name=name,
                          compiler_params=_cparams())(place, h, r)


def _adam_math(w, g, m, v):
    m2 = ADAM_B1 * m + (1.0 - ADAM_B1) * g
    v2 = ADAM_B2 * v + (1.0 - ADAM_B2) * (g * g)
    m_hat = m2 / (1.0 - ADAM_B1 ** ADAM_STEP)
    v_hat = v2 / (1.0 - ADAM_B2 ** ADAM_STEP)
    delta = -ADAM_LR * (m_hat / (jnp.sqrt(v_hat) + ADAM_EPS) + ADAM_WD * w)
    return delta, m2, v2


def _adamw(w, m, v, grads, name):
    r, c = w.shape
    tr = _flat_tiles(r, c, 4 * (7 + len(grads)))

    def body(*refs):
        w_ref, m_ref, v_ref = refs[:3]
        g_refs = refs[3:3 + len(grads)]
        g_ref, d_ref, m2_ref, v2_ref = refs[3 + len(grads):]
        g = g_refs[0][...]
        for gr in g_refs[1:]:
            g = g + gr[...]
        delta, m2, v2 = _adam_math(w_ref[...], g, m_ref[...], v_ref[...])
        g_ref[...] = g
        d_ref[...] = delta
        m2_ref[...] = m2
        v2_ref[...] = v2

    blk = pl.BlockSpec((tr, c), lambda i: (i, 0))
    sh = jax.ShapeDtypeStruct((r, c), F32)
    return pl.pallas_call(body, grid=(r // tr,), in_specs=[blk] * (3 + len(grads)), out_specs=[blk] * 4,
                          out_shape=[sh] * 4, name=name, compiler_params=_cparams())(w, m, v, *grads)


def _adamw_halves(w, m, v, own, other, place, split, name):
    nl, r, c = w.shape
    hr, hc = own[0].shape
    tr = _flat_tiles(hr, hc, 4 * (7 + 2 * nl))
    nt = hr // tr
    if split == "rows":
        w_spec = pl.BlockSpec((None, tr, c), lambda l, h, t, pr: (l, h * nt + t, 0))
    else:
        w_spec = pl.BlockSpec((None, tr, hc), lambda l, h, t, pr: (l, t, h))

    def g_spec(layer, mine):
        return pl.BlockSpec((tr, hc), lambda l, h, t, pr: (jnp.where((l == layer) & ((h == pr[0]) == mine), t, nt - 1), 0))

    def body(place_ref, w_ref, m_ref, v_ref, *refs):
        own_refs, other_refs = refs[:nl], refs[nl:2 * nl]
        g_ref, d_ref, m2_ref, v2_ref = refs[2 * nl:]
        layer = pl.program_id(0)
        mine = pl.program_id(1) == place_ref[0]
        g = None
        for li in range(nl):
            cand = jnp.where(mine, own_refs[li][...], other_refs[li][...])
            g = cand if g is None else jnp.where(layer == li, cand, g)
        delta, m2, v2 = _adam_math(w_ref[...], g, m_ref[...], v_ref[...])
        g_ref[...] = g
        d_ref[...] = delta
        m2_ref[...] = m2
        v2_ref[...] = v2

    sh = jax.ShapeDtypeStruct((nl, r, c), F32)
    g_specs = [g_spec(li, True) for li in range(nl)] + [g_spec(li, False) for li in range(nl)]
    return _call(body, name=name, grid=(nl, 2, nt), in_specs=[w_spec] * 3 + g_specs, out_specs=[w_spec] * 4,
                 out_shape=[sh] * 4, scratch_shapes=[], args=(w, m, v, *own, *other), prefetch=(place,))


def _adamw_small(w, m, v, g_all, name):
    r, c = w.shape

    def body(w_ref, m_ref, v_ref, g_ref, go_ref, d_ref, m2_ref, v2_ref):
        g = g_ref[0]
        for b in range(1, N_DEV):
            g = g + g_ref[b]
        delta, m2, v2 = _adam_math(w_ref[...], g, m_ref[...], v_ref[...])
        go_ref[...] = g
        d_ref[...] = delta
        m2_ref[...] = m2
        v2_ref[...] = v2

    sh = jax.ShapeDtypeStruct((r, c), F32)
    return pl.pallas_call(body, out_shape=[sh] * 4, name=name, compiler_params=_cparams())(w, m, v, g_all)


def _me():
    return lax.axis_index("x"), lax.axis_index("y"), lax.axis_index("c")


def _flip(v, bit):
    return 1 - v if bit else v


def _allgather_small(blk, name):
    r, c = blk.shape

    def body(x_ref, o_ref, send_sems, recv_sems):
        x, y, cc = _me()
        me = 4 * x + 2 * y + cc
        copies = []
        for k in range(1, N_DEV):
            peer = (_flip(x, k & 4), _flip(y, k & 2), _flip(cc, k & 1))
            cp = pltpu.make_async_remote_copy(src_ref=x_ref, dst_ref=o_ref.at[me], send_sem=send_sems.at[k - 1],
                                              recv_sem=recv_sems.at[k - 1], device_id=peer, device_id_type=MESH)
            cp.start()
            copies.append(cp)
        o_ref[me] = x_ref[...]
        for cp in copies:
            cp.wait()

    return pl.pallas_call(
        body, out_shape=jax.ShapeDtypeStruct((N_DEV, r, c), F32),
        in_specs=[pl.BlockSpec(memory_space=pltpu.VMEM)], out_specs=pl.BlockSpec(memory_space=pltpu.VMEM),
        scratch_shapes=[pltpu.SemaphoreType.DMA((N_DEV - 1,)), pltpu.SemaphoreType.DMA((N_DEV - 1,))],
        name=name, compiler_params=_cparams(),
    )(blk)


class _Rider:
    def __init__(self, arrays, out_shapes, scratch_shapes, start, finish):
        self.arrays, self.out_shapes, self.scratch_shapes = list(arrays), list(out_shapes), list(scratch_shapes)
        self.start, self.finish = start, finish


def _call(body, *, name, grid, in_specs, out_specs, out_shape, scratch_shapes, args, rider=None, prefetch=()):
    npf = len(prefetch)

    def launch(fn, in_specs, out_specs, out_shape, scratch_shapes, args):
        grid_spec = pltpu.PrefetchScalarGridSpec(num_scalar_prefetch=npf, grid=grid, in_specs=in_specs,
                                                 out_specs=out_specs, scratch_shapes=scratch_shapes)
        return pl.pallas_call(fn, grid_spec=grid_spec, out_shape=out_shape, name=name,
                              compiler_params=_cparams())(*prefetch, *args)

    if rider is None:
        return launch(body, list(in_specs), list(out_specs), list(out_shape), list(scratch_shapes), args)
    ni, no, ns = len(in_specs), len(out_specs), len(scratch_shapes)
    ri, ro = len(rider.arrays), len(rider.out_shapes)
    steps = int(np.prod(grid))

    def wrapped(*refs):
        pf, refs = refs[:npf], refs[npf:]
        h_in, r_in = refs[:ni], refs[ni:ni + ri]
        h_out, r_out = refs[ni + ri:ni + ri + no], refs[ni + ri + no:ni + ri + no + ro]
        h_scr, r_scr = refs[ni + ri + no + ro:ni + ri + no + ro + ns], refs[ni + ri + no + ro + ns:]
        step = pl.program_id(0)
        for d in range(1, len(grid)):
            step = step * grid[d] + pl.program_id(d)

        @pl.when(step == 0)
        def _():
            rider.start(r_in, r_out, r_scr)

        body(*pf, *h_in, *h_out, *h_scr)

        @pl.when(step == steps - 1)
        def _():
            rider.finish(r_in, r_out, r_scr)

    anyspec = pl.BlockSpec(memory_space=pl.ANY)
    res = launch(wrapped, list(in_specs) + [anyspec] * ri, list(out_specs) + [anyspec] * ro,
                 list(out_shape) + rider.out_shapes, list(scratch_shapes) + rider.scratch_shapes,
                 list(args) + rider.arrays)
    return res[:no], res[no:]


def _run_rider(rider, name):
    ri = len(rider.arrays)

    def body(*refs):
        r_in, r_out, r_scr = refs[:ri], refs[ri:ri + len(rider.out_shapes)], refs[ri + len(rider.out_shapes):]
        rider.start(r_in, r_out, r_scr)
        rider.finish(r_in, r_out, r_scr)

    anyspec = pl.BlockSpec(memory_space=pl.ANY)
    return pl.pallas_call(body, in_specs=[anyspec] * ri, out_specs=[anyspec] * len(rider.out_shapes),
                          out_shape=rider.out_shapes, scratch_shapes=rider.scratch_shapes, name=name,
                          compiler_params=_cparams())(*rider.arrays)


def _allgather_rider(blk):
    def copies(ins, outs, scr):
        send_sems, recv_sems, loc_sems, stage = scr
        x, y, cc = _me()
        me = 4 * x + 2 * y + cc
        remote = [pltpu.make_async_remote_copy(
            src_ref=ins[0], dst_ref=outs[0].at[me], send_sem=send_sems.at[k - 1], recv_sem=recv_sems.at[k - 1],
            device_id=(_flip(x, k & 4), _flip(y, k & 2), _flip(cc, k & 1)), device_id_type=MESH) for k in range(1, N_DEV)]
        return remote, pltpu.make_async_copy(ins[0], stage, loc_sems.at[0]), (outs[0].at[me], stage, loc_sems.at[1])

    def start(ins, outs, scr):
        remote, lin, _ = copies(ins, outs, scr)
        lin.start()
        for cp in remote:
            cp.start()

    def finish(ins, outs, scr):
        remote, lin, (dst, stage, sem) = copies(ins, outs, scr)
        lin.wait()
        lout = pltpu.make_async_copy(stage, dst, sem)
        lout.start()
        for cp in remote:
            cp.wait()
        lout.wait()

    return _Rider([blk], [jax.ShapeDtypeStruct((N_DEV,) + blk.shape, blk.dtype)],
                  [pltpu.SemaphoreType.DMA((N_DEV - 1,)), pltpu.SemaphoreType.DMA((N_DEV - 1,)),
                   pltpu.SemaphoreType.DMA((2,)), pltpu.VMEM(blk.shape, blk.dtype)], start, finish)


def _gather_rider(shards):
    n = len(shards)

    def copies(ins, outs, scr, relay=True):
        ici_send, ici_recv, d2d_send, d2d_recv, loc_sems = scr[:5]
        stage = scr[5:]
        x, y, cc = _me()
        chip = 2 * x + y
        sibling = (x, y, 1 - cc)
        local, sends, relays = [], [], []
        for j in range(n):
            def rows(ch, h, j=j):
                return outs[j].at[ch, h]

            lc = pltpu.make_async_copy(ins[j], stage[j], loc_sems.at[j])
            local.append((lc, pltpu.make_async_copy(stage[j], outs[j].at[chip], loc_sems.at[n + j]) if relay else None))
            for k in range(1, N_CHIP):
                px, py = _flip(x, k & 2), _flip(y, k & 1)
                pchip = 2 * px + py
                q = 3 * j + k - 1
                out_cp = pltpu.make_async_remote_copy(src_ref=ins[j].at[cc], dst_ref=rows(chip, cc),
                                                      send_sem=ici_send.at[q], recv_sem=ici_recv.at[q],
                                                      device_id=(px, py, cc), device_id_type=MESH)
                sends.append(out_cp)
                if not relay:
                    continue
                arrival = pltpu.make_async_remote_copy(src_ref=rows(pchip, cc), dst_ref=rows(pchip, cc),
                                                       send_sem=ici_send.at[q], recv_sem=ici_recv.at[q],
                                                       device_id=(px, py, cc), device_id_type=MESH)
                forward = pltpu.make_async_remote_copy(src_ref=rows(pchip, cc), dst_ref=rows(pchip, cc),
                                                       send_sem=d2d_send.at[q], recv_sem=d2d_recv.at[q],
                                                       device_id=sibling, device_id_type=MESH)
                from_sibling = pltpu.make_async_remote_copy(src_ref=rows(pchip, 1 - cc), dst_ref=rows(pchip, 1 - cc),
                                                            send_sem=d2d_send.at[q], recv_sem=d2d_recv.at[q],
                                                            device_id=sibling, device_id_type=MESH)
                relays.append((arrival, forward, from_sibling))
        return local, sends, relays

    def start(ins, outs, scr):
        local, sends, _ = copies(ins, outs, scr, relay=False)
        for lin, _ in local:
            lin.start()
        for cp in sends:
            cp.start()

    def finish(ins, outs, scr):
        local, sends, relays = copies(ins, outs, scr)
        for lin, lout in local:
            lin.wait()
            lout.start()
        for arrival, forward, _ in relays:
            arrival.wait_recv()
            forward.start()
        for cp in sends:
            cp.wait_send()
        for _, forward, from_sibling in relays:
            forward.wait_send()
            from_sibling.wait_recv()
        for _, lout in local:
            lout.wait()

    scratch = [pltpu.SemaphoreType.DMA((3 * n,)), pltpu.SemaphoreType.DMA((3 * n,)), pltpu.SemaphoreType.DMA((3 * n,)),
               pltpu.SemaphoreType.DMA((3 * n,)), pltpu.SemaphoreType.DMA((2 * n,))]
    scratch += [pltpu.VMEM(a.shape, a.dtype) for a in shards]
    return _Rider(shards, [jax.ShapeDtypeStruct((N_CHIP,) + a.shape, a.dtype) for a in shards], scratch, start, finish)


def _sibling_rider(arrs, other_half=False):
    n = len(arrs)

    def copies(ins, outs, scr):
        send_sems, recv_sems = scr
        x, y, cc = _me()
        return [pltpu.make_async_remote_copy(
            src_ref=ins[j].at[1 - cc] if other_half else ins[j], dst_ref=outs[j], send_sem=send_sems.at[j],
            recv_sem=recv_sems.at[j], device_id=(x, y, 1 - cc), device_id_type=MESH) for j in range(n)]

    def start(ins, outs, scr):
        for cp in copies(ins, outs, scr):
            cp.start()

    def finish(ins, outs, scr):
        for cp in copies(ins, outs, scr):
            cp.wait()

    return _Rider(arrs, [jax.ShapeDtypeStruct(a.shape[1:] if other_half else a.shape, a.dtype) for a in arrs],
                  [pltpu.SemaphoreType.DMA((n,)), pltpu.SemaphoreType.DMA((n,))], start, finish)


def _sibling_send(arrs, name, other_half=False):
    return _run_rider(_sibling_rider(arrs, other_half), name)


def _join_riders(first, second):
    ni, no, ns = len(first.arrays), len(first.out_shapes), len(first.scratch_shapes)

    def split(ins, outs, scr):
        return (ins[:ni], outs[:no], scr[:ns]), (ins[ni:], outs[no:], scr[ns:])

    def start(ins, outs, scr):
        a, b = split(ins, outs, scr)
        first.start(*a)
        second.start(*b)

    def finish(ins, outs, scr):
        a, b = split(ins, outs, scr)
        first.finish(*a)
        second.finish(*b)

    return _Rider(first.arrays + second.arrays, first.out_shapes + second.out_shapes,
                  first.scratch_shapes + second.scratch_shapes, start, finish)


def _scatter_rider(arrs):
    n = len(arrs)

    def copies(ins, outs, scr):
        send_sems, recv_sems = scr
        x, y, cc = _me()
        cps = []
        for j in range(n):
            for k in range(1, N_CHIP):
                px, py = _flip(x, k & 2), _flip(y, k & 1)
                cps.append(pltpu.make_async_remote_copy(
                    src_ref=ins[j].at[2 * px + py], dst_ref=outs[j].at[k - 1], send_sem=send_sems.at[3 * j + k - 1],
                    recv_sem=recv_sems.at[3 * j + k - 1], device_id=(px, py, cc), device_id_type=MESH))
        return cps

    def start(ins, outs, scr):
        for cp in copies(ins, outs, scr):
            cp.start()

    def finish(ins, outs, scr):
        for cp in copies(ins, outs, scr):
            cp.wait()

    return _Rider(arrs, [jax.ShapeDtypeStruct((N_CHIP - 1,) + a.shape[1:], a.dtype) for a in arrs],
                  [pltpu.SemaphoreType.DMA((3 * n,)), pltpu.SemaphoreType.DMA((3 * n,))], start, finish)


COL_SHARDED = ("w_in", "w_br_pool", "w_br_attn", "w_br_conv", "w_ff1")
ROW_SHARDED = ("w_o", "w_ff2")
BIG = COL_SHARDED + ROW_SHARDED
SMALL = ("b_ada", "b_gate", "w_pool", "pool_scale", "rel_bias", "conv_w", "conv_b", "conv_ln_g", "conv_ln_b",
         "ln_mix_g", "ln_mix_b", "b_ff1", "b_ff2", "ln_ff_g", "ln_ff_b")
PACK_W = 1024


def _pack(parts):
    rows = []
    for a in parts:
        flat = a.reshape(-1)
        n = -(-flat.shape[0] // PACK_W) * PACK_W
        rows.append(jnp.pad(flat, (0, n - flat.shape[0])).reshape(-1, PACK_W))
    out = jnp.concatenate(rows, axis=0)
    r = -(-out.shape[0] // 8) * 8
    return jnp.pad(out, ((0, r - out.shape[0]), (0, 0)))


def _unpack(packed, shapes):
    out, r0 = [], 0
    for shp in shapes:
        size = int(np.prod(shp))
        nr = -(-size // PACK_W)
        out.append(packed[r0:r0 + nr].reshape(-1)[:size].reshape(shp))
        r0 += nr
    return out


def _hosted(fn, hook, *args, **kw):
    if hook is None:
        return fn(*args, **kw)
    res, rider_out = fn(*args, rider=hook[0], **kw)
    hook[1](rider_out)
    return res


def _layer_fwd(l, x, mod, W, P, hooks=None, u=None):
    hooks = hooks or {}
    s = x.shape[0]
    sh_m, sc_m, g_m, sh_f, sc_f, g_f = [mod[l:l + 1, D_MODEL * j:D_MODEL * (j + 1)] for j in range(6)]
    n = lambda t: f"{t}{l}"
    w_in = W["w_in"][l]
    if u is None:
        u = _ln_mod(x, sc_m, sh_m, n("ln_mod_mix"))
    zp = _mm(u, w_in, "nt", tm=s, tn=256, out_dtype=F32, name=n("z_pool"), b_col0=0, n_out=D_POOL)
    qkv = _mm(u, w_in, "nt", tm=s, tn=256, out_dtype=BF16, name=n("z_qkv"), b_col0=OFF_QKV // 256, n_out=3 * D_ATTN)
    zc = _mm(u, w_in, "nt", tm=s, tn=256, out_dtype=F32, name=n("z_conv"), b_col0=OFF_CONV // 256, n_out=2 * D_CONV)
    zg = _hosted(_mm, hooks.get("z_gate"), u, w_in, "nt", tm=min(2048, s), tn=768, out_dtype=BF16, name=n("z_gate"),
                 b_col0=OFF_GATE // 768, n_out=3 * D_MODEL)

    p, feat_pool = _pool_fwd(zp, P["wp_bd"][l], P["pool_scale"][l], n("pool_fwd"))
    bias = _bias_block(P["rel_bias"][l], n("bias_block"))
    o, probs = _hosted(_attn_fwd, hooks.get("attn"), qkv, bias, n("attn_fwd"))
    cv, feat_conv = _conv_fwd(zc, P["conv_w"][l], P["conv_b"][l], P["conv_ln_g"][l], P["conv_ln_b"][l], n("conv_fwd"))

    branch_w = (W["w_br_pool"][l], W["w_br_attn"][l], W["w_br_conv"][l])
    ys = tuple(_branch_out((feat_pool, o, feat_conv), branch_w, n("branch_out")))
    merged = _merge(zg, P["b_gate"][l], ys, n("merge"))
    mix, x1, u2 = _mm_resid_ln(merged, W["w_o"][l], None, x, g_m, P["ln_mix_g"][l], P["ln_mix_b"][l], n("mix_out"),
                               mod_next=(sc_f, sh_f))

    hpre, hid = _hosted(_ff_hidden, hooks.get("ff1"), u2, W["w_ff1"][l], P["b_ff1"][l], n("ff1"))
    above = None if l + 1 == mod.shape[0] else (mod[l + 1:l + 2, D_MODEL:2 * D_MODEL], mod[l + 1:l + 2, 0:D_MODEL])
    ff, x2, *u_next = _hosted(_mm_resid_ln, hooks.get("ff2"), hid, W["w_ff2"][l], P["b_ff2"][l], x1, g_f,
                              P["ln_ff_g"][l], P["ln_ff_b"][l], n("ff2"), mod_next=above)
    saved = dict(x=x, u=u, zp=zp, qkv=qkv, zc=zc, zg=zg, p=p, feat_pool=feat_pool, probs=probs, o=o, cv=cv,
                 feat_conv=feat_conv, ys=ys, merged=merged, mix=mix, x1=x1, u2=u2, hpre=hpre, hid=hid, ff=ff,
                 u_next=u_next[0] if u_next else None)
    return x2, saved


def _layer_bwd(l, dx2, mod, W, P, A, hooks=None, tgt=None, nxt=None):
    hooks = hooks or {}
    sh_m, sc_m, g_m, sh_f, sc_f, g_f = [mod[l:l + 1, D_MODEL * j:D_MODEL * (j + 1)] for j in range(6)]
    n = lambda t: f"{t}{l}"
    gw, gs = {}, {}

    if isinstance(dx2, tuple):
        dres, dff, gs["ln_ff_g"], gs["ln_ff_b"], dg_f, gs["b_ff2"] = dx2
    else:
        dres, dff, gs["ln_ff_g"], gs["ln_ff_b"], dg_f, gs["b_ff2"], *loss_part = _resid_ln_bwd(
            dx2, A["x1"], A["ff"], g_f, P["ln_ff_g"][l], n("resid_ln_ff_bwd"), tgt=tgt)
    s = dres.shape[0]
    tmb = min(1024, s)
    gw["w_ff2"] = _mm(A["hid"], dff, "tn", tm=512, tn=1024, out_dtype=BF16, name=n("dw_ff2"), split_n=512)
    hook = hooks["ff_hidden_bwd"](gw) if "ff_hidden_bwd" in hooks else None
    dhpre, gs["b_ff1"] = _hosted(_ff_hidden_bwd, hook, dff, W["w_ff2"][l], A["hpre"], n("ff_hidden_bwd"))
    gw["w_ff1"] = _mm(dhpre, A["u2"], "tn", tm=512, tn=1024, out_dtype=BF16, name=n("dw_ff1"), split_n=512)

    hook = hooks["du_ff"](gw) if "du_ff" in hooks else None
    dres, dmix, dsc_f, dsh_f, gs["ln_mix_g"], gs["ln_mix_b"], dg_m, _ = _hosted(
        _mm_ln_mod_bwd, hook, dhpre, W["w_ff1"][l], A["x1"], sc_f, dres, n("du_ff"),
        nxt=(A["x"], A["mix"], g_m, P["ln_mix_g"][l]))
    gw["w_o"] = _mm(A["merged"], dmix, "tn", tm=512, tn=1024, out_dtype=BF16, name=n("dw_o"), split_n=512)
    dmerged = _mm(dmix, W["w_o"][l], "nt", tm=tmb, tn=1024, out_dtype=F32, name=n("d_merged"))
    dy_pool, dy_attn, dy_conv, dzg, gs["b_gate"] = _merge_bwd(dmerged, A["zg"], P["b_gate"][l], A["ys"], n("merge_bwd"))

    dys = (dy_pool, dy_attn, dy_conv)
    gw["w_br_pool"], gw["w_br_attn"], gw["w_br_conv"] = _branch_dw(
        dys, (A["feat_pool"], A["o"], A["feat_conv"]), n("dw_branch"))
    dfeat_pool, do, dfeat_conv = _branch_in_bwd(
        dys, (W["w_br_pool"][l], W["w_br_attn"][l], W["w_br_conv"][l]), (F32, BF16, F32), n("d_branch_in"))

    dzp, dwp_bd, gs["pool_scale"] = _pool_bwd(dfeat_pool, A["p"], P["wp_bd"][l], P["pool_scale"][l], n("pool_bwd"))
    gs["w_pool"] = jnp.stack([dwp_bd[POOL_GROUP * g:POOL_GROUP * (g + 1), POOL_GROUP * g:POOL_GROUP * (g + 1)]
                              for g in range(len(POOL_WINDOWS))])
    hook = hooks["attn"](gw) if "attn" in hooks else None
    dq, dk, dv, ds_acc = _hosted(_attn_bwd, hook, A["qkv"], do, A["probs"], n("attn_bwd"))
    gs["rel_bias"] = _bias_block_bwd(ds_acc, n("bias_block_bwd"))
    dzc, dcw, gs["conv_b"], gs["conv_ln_g"], gs["conv_ln_b"] = _conv_bwd(
        dfeat_conv, A["cv"], A["zc"], P["conv_w"][l], P["conv_ln_g"][l], P["conv_ln_b"][l], n("conv_bwd"))
    gs["conv_w"] = dcw[:CONV_WIDTH]

    dz = [dzp, dq, dk, dv, dzc, dzg]
    gw["w_in"] = _dw_segments(dz, A["u"], n("dw_in"))
    hook = hooks["du_mix"](gw) if "du_mix" in hooks else None
    res = _hosted(_mm_ln_mod_bwd, hook, dz, W["w_in"][l], A["x"], sc_m, dres, n("du_mix"), nxt=nxt)
    if nxt is None:
        dx, dsc_m, dsh_m = res
    else:
        dx, dsc_m, dsh_m = (res[0], res[1], *res[4:]), res[2], res[3]
    dmod = jnp.concatenate([dsh_m, dsc_m, dg_m, dsh_f, dsc_f, dg_f], axis=1)
    return (dx, gw, gs, dmod) if tgt is None else (dx, gw, gs, dmod, loss_part[0])


def _small_shapes():
    return {"b_ada": (6 * D_MODEL,), "b_gate": (3 * D_MODEL,), "w_pool": (4, POOL_GROUP, POOL_GROUP),
            "pool_scale": (D_POOL,), "rel_bias": (N_HEADS, N_REL), "conv_w": (CONV_WIDTH, D_CONV),
            "conv_b": (D_CONV,), "conv_ln_g": (D_CONV,), "conv_ln_b": (D_CONV,), "ln_mix_g": (D_MODEL,),
            "ln_mix_b": (D_MODEL,), "b_ff1": (D_FF,), "b_ff2": (D_MODEL,), "ln_ff_g": (D_MODEL,), "ln_ff_b": (D_MODEL,)}


def kernel(x, c, w_ada, b_ada, w_in, b_gate, w_pool, pool_scale, rel_bias, conv_w, conv_b, conv_ln_g, conv_ln_b, w_br_pool, w_br_attn, w_br_conv, w_o, ln_mix_g, ln_mix_b, w_ff1, b_ff1, w_ff2, b_ff2, ln_ff_g, ln_ff_b, loss_target, m_w_ada, m_b_ada, m_w_in, m_b_gate, m_w_pool, m_pool_scale, m_rel_bias, m_conv_w, m_conv_b, m_conv_ln_g, m_conv_ln_b, m_w_br_pool, m_w_br_attn, m_w_br_conv, m_w_o, m_ln_mix_g, m_ln_mix_b, m_w_ff1, m_b_ff1, m_w_ff2, m_b_ff2, m_ln_ff_g, m_ln_ff_b, v_w_ada, v_b_ada, v_w_in, v_b_gate, v_w_pool, v_pool_scale, v_rel_bias, v_conv_w, v_conv_b, v_conv_ln_g, v_conv_ln_b, v_w_br_pool, v_w_br_attn, v_w_br_conv, v_w_o, v_ln_mix_g, v_ln_mix_b, v_w_ff1, v_b_ff1, v_w_ff2, v_b_ff2, v_ln_ff_g, v_ln_ff_b):
    env = dict(locals())
    xi, yi, ci = _me()
    chip = 2 * xi + yi
    me = 4 * xi + 2 * yi + ci
    xs = x[0]
    tgt = loss_target[0]
    L = DEPTH

    first = _allgather_small(jnp.concatenate([c.reshape(8, 128), _pack([conv_w]).reshape(-1, 128)]), "gather_c_conv_w")
    c_all = first[:, :8].reshape(N_DEV, D_MODEL)
    ada_cols = w_ada.shape[2]
    b_ada_sh = lax.dynamic_slice_in_dim(b_ada, chip * ada_cols, ada_cols, axis=1).reshape(L, 1, ada_cols)
    mod_part = _mod_fwd(c_all, w_ada, b_ada_sh, "mod_fwd")
    mod_g = _allgather_small(mod_part.reshape(-1, 128), "gather_mod").reshape(N_CHIP, 2, L, N_DEV, ada_cols)[:, 0]
    mod_all = jnp.transpose(mod_g, (1, 2, 0, 3)).reshape(L, N_DEV, 6 * D_MODEL)
    mod = lax.dynamic_index_in_dim(mod_all, me, axis=1, keepdims=False)

    W = {k: [None] * L for k in BIG}

    def weight_gather(*items):
        shards = [(jnp.swapaxes(env[k][l], 0, 1) if k in COL_SHARDED else env[k][l]).astype(BF16) for k, l in items]
        shards = [a.reshape(2, a.shape[0] // 2, a.shape[1]) for a in shards]

        def done(outs):
            for (k, l), g in zip(items, outs):
                W[k][l] = g.reshape(-1, g.shape[-1])

        return _gather_rider(shards), done

    branch = lambda l: [(k, l) for k in ("w_br_pool", "w_br_attn", "w_br_conv", "w_o")]
    rider, done = weight_gather(("w_in", 0))
    done(_run_rider(rider, "gather_w_in0"))
    fwd_hooks = [{"z_gate": weight_gather(*branch(0)), "attn": weight_gather(("w_ff1", 0), ("w_ff2", 0)),
                  "ff1": weight_gather(("w_in", 1)), "ff2": weight_gather(*branch(1))},
                 {"attn": weight_gather(("w_ff1", 1), ("w_ff2", 1))}]

    P = {k: env[k] for k in ("rel_bias", "conv_w")}
    for k in ("b_gate", "pool_scale", "conv_b", "conv_ln_g", "conv_ln_b", "ln_mix_g", "ln_mix_b", "b_ff1", "b_ff2",
              "ln_ff_g", "ln_ff_b"):
        P[k] = env[k].reshape(L, 1, -1)
    n_cw = conv_w.size
    cw = first[:, 8:].reshape(N_CHIP, 2, -1)[:, 0, :n_cw].reshape(N_CHIP, L, CONV_WIDTH, D_CONV // N_CHIP)
    P["conv_w"] = jnp.transpose(cw, (1, 2, 0, 3)).reshape(L, CONV_WIDTH, D_CONV)
    wp_bd = jnp.zeros((L, D_POOL, D_POOL), F32)
    for g in range(len(POOL_WINDOWS)):
        sl = slice(POOL_GROUP * g, POOL_GROUP * (g + 1))
        wp_bd = wp_bd.at[:, sl, sl].set(w_pool[:, g])
    P["wp_bd"] = wp_bd.astype(BF16)

    acts = []
    h = xs
    for l in range(L):
        h, saved = _layer_fwd(l, h, mod, W, P, fwd_hooks[l], u=acts[-1]["u_next"] if acts else None)
        acts.append(saved)

    place = jnp.stack([ci, chip, chip ^ 1, chip ^ 2, chip ^ 3]).astype(jnp.int32)
    scattered = {}

    def grad_scatter(items, tag):
        dws = [dw for _, _, dw in items]
        got = _sibling_send(dws, f"swap_blocks_{tag}", other_half=True)
        both = [_sum_cores(a, b, place, f"sum_cores_{k}{l}") for (k, l, _), a, b in zip(items, dws, got)]
        both = [hh.reshape(N_CHIP, -1, hh.shape[-1]) for hh in both]

        def done(outs):
            for (k, l, _), hh, r in zip(items, both, outs):
                scattered[(k, l)] = (hh, r)

        return _scatter_rider(both), done

    def scatter_hook(names, l, host):
        return lambda gw: grad_scatter([(k, l, gw[k]) for k in names], f"{host}{l}")

    gws, gss, dmods = [None] * L, [None] * L, [None] * L
    dh = h
    for l in reversed(range(L)):
        hooks = {"du_ff": scatter_hook(("w_ff2",), l, "du_ff"),
                 "attn": scatter_hook(("w_ff1", "w_o", "w_br_pool", "w_br_attn", "w_br_conv"), l, "attn_bwd"),
                 "du_mix": scatter_hook(("w_in",), l, "du_mix")}
        below = None
        if l > 0:
            below = (acts[l - 1]["x1"], acts[l - 1]["ff"], mod[l - 1:l, 5 * D_MODEL:], P["ln_ff_g"][l - 1])
        if l == L - 1:
            dh, gws[l], gss[l], dmods[l], loss_part = _layer_bwd(l, dh, mod, W, P, acts[l], hooks, tgt=tgt, nxt=below)
        else:
            dh, gws[l], gss[l], dmods[l] = _layer_bwd(l, dh, mod, W, P, acts[l], hooks, nxt=below)
    grad_x = dh[None]

    reduced = [[_sum_chips(*scattered[(k, l)], place, f"sum_chips_{k}{l}") for l in range(L)] for k in BIG]
    flat_reduced = [t for per_weight in reduced for t in per_weight]

    shapes = _small_shapes()
    small_names = [k for k in SMALL if k != "b_ada"]
    dmod_own = jnp.concatenate(dmods, axis=0)
    pack = _pack([dmod_own] + [jnp.stack([gss[l][k].reshape(shapes[k]) for l in range(L)]) for k in small_names]
                 + [loss_part])
    last = _run_rider(_join_riders(_sibling_rider(flat_reduced), _allgather_rider(pack.reshape(-1, 128))),
                      "swap_reduced_gather_small")
    flat_other, g_all = last[:-1], last[-1].reshape(N_DEV, -1, PACK_W)

    out = {}
    for j, k in enumerate(BIG):
        own, other = reduced[j], flat_other[L * j:L * (j + 1)]
        if k == "w_in":
            t = lambda a: jnp.swapaxes(a, 1, 2)
            res = _adamw_halves(t(env[k]), t(env["m_" + k]), t(env["v_" + k]), own, other, place, "cols", f"adamw_{k}")
            res = [t(a) for a in res]
        else:
            if k in COL_SHARDED:
                own, other = [a.T for a in own], [a.T for a in other]
            res = _adamw_halves(env[k], env["m_" + k], env["v_" + k], own, other, place,
                                "rows" if k in COL_SHARDED else "cols", f"adamw_{k}")
        out[k] = tuple(res)

    dmod_all = g_all[:, :L * 6].reshape(N_DEV, L, 6 * D_MODEL)
    dmod_sh = jnp.transpose(lax.dynamic_slice_in_dim(dmod_all, chip * ada_cols, ada_cols, axis=2), (1, 0, 2))
    g_ada = _mod_bwd(c_all, dmod_sh, "mod_bwd")
    g_, d_, m_, v_ = _adamw(w_ada.reshape(-1, ada_cols), m_w_ada.reshape(-1, ada_cols), v_w_ada.reshape(-1, ada_cols),
                            [g_ada.reshape(-1, ada_cols)], "adamw_w_ada")
    out["w_ada"] = tuple(a.reshape(w_ada.shape) for a in (g_, d_, m_, v_))

    def small_pack(prefix):
        parts = [env[prefix + "b_ada"]]
        for k in small_names:
            a = env[prefix + k]
            if k == "conv_w":
                a = jnp.zeros((L,) + shapes[k], F32)
            parts.append(a)
        return _pack(parts + [jnp.zeros_like(loss_part)])

    gp, dp, mp, vp = _adamw_small(small_pack(""), small_pack("m_"), small_pack("v_"), g_all, "adamw_small")
    full_shapes = [(L,) + shapes["b_ada"]] + [(L,) + shapes[k] for k in small_names]
    loss = _unpack(gp, full_shapes + [(128,)])[-1][0]
    for tag, packed in (("g", gp), ("d", dp), ("m", mp), ("v", vp)):
        for k, a in zip(["b_ada"] + small_names, _unpack(packed, full_shapes)):
            out.setdefault(k, {})
            out[k][tag] = a
    g_cw_full = out["conv_w"]["g"]
    cw_cols = D_CONV // N_CHIP
    g_cw = lax.dynamic_slice_in_dim(g_cw_full, chip * cw_cols, cw_cols, axis=2)
    pad_rows = lambda a: jnp.pad(a.reshape(L * CONV_WIDTH, cw_cols), ((0, 2), (0, 0)))
    g_, d_, m_, v_ = _adamw(pad_rows(conv_w), pad_rows(m_conv_w), pad_rows(v_conv_w), [pad_rows(g_cw)], "adamw_conv_w")
    out["conv_w"] = tuple(a[:L * CONV_WIDTH].reshape(L, CONV_WIDTH, cw_cols) for a in (g_, d_, m_, v_))

    names = ["w_ada", "b_ada", "w_in", "b_gate", "w_pool", "pool_scale", "rel_bias", "conv_w", "conv_b", "conv_ln_g",
             "conv_ln_b", "w_br_pool", "w_br_attn", "w_br_conv", "w_o", "ln_mix_g", "ln_mix_b", "w_ff1", "b_ff1",
             "w_ff2", "b_ff2", "ln_ff_g", "ln_ff_b"]

    def pick(k, i):
        o = out[k]
        return o[i] if isinstance(o, tuple) else o["gdmv"[i]].reshape(env[k].shape)

    return (loss, grad_x, *[pick(k, 0) for k in names], *[pick(k, 1) for k in names],
            *[pick(k, 2) for k in names], *[pick(k, 3) for k in names])
```

```python
import jax
import jax.numpy as jnp
import numpy as np
from jax import lax
from jax.experimental import pallas as pl
from jax.experimental.pallas import tpu as pltpu

F32 = jnp.float32
BF16 = jnp.bfloat16

D_MODEL = 1024
DEPTH = 2
CHUNK = 64
POOL_WINDOWS = (2, 4, 8, 16)
POOL_GROUP = 64
D_POOL = 256
N_HEADS = 8
HEAD_DIM = 64
D_ATTN = 512
N_PREV_CHUNKS = 8
REL_CLIP = 128
N_REL = 2 * REL_CLIP + 1
D_CONV = 256
CONV_WIDTH = 31
D_FF = 4 * D_MODEL
D_IN = 5376
OFF_POOL, OFF_QKV, OFF_CONV, OFF_GATE = 0, 256, 1792, 2304
ALPHA = (2.0 * DEPTH) ** 0.25
LN_EPS = 1e-5
NEG_INF = -1e30
ADAM_LR, ADAM_B1, ADAM_B2, ADAM_EPS, ADAM_WD, ADAM_STEP = 0.001, 0.9, 0.999, 1e-08, 0.01, 10

N_DEV = 8
N_CHIP = 4
MESH = pl.DeviceIdType.MESH

QB = 2 * CHUNK
KPAD = N_PREV_CHUNKS * CHUNK
KW = QB + KPAD
SKEW_W = 768

VMEM_LIMIT = 56 * 1024 * 1024


def _cparams(**kw):
    return pltpu.CompilerParams(vmem_limit_bytes=VMEM_LIMIT, **kw)


def _full(shape):
    n = len(shape)
    return pl.BlockSpec(shape, lambda *_: (0,) * n)


_DIMS = {"nn": (((1,), (0,)), ((), ())), "nt": (((1,), (1,)), ((), ())), "tn": (((0,), (0,)), ((), ()))}


def _relu2(t):
    r = jnp.maximum(t, 0.0)
    return r * r


def _mm(a, b, mode, *, tm, tn, out_dtype, name, b_col0=0, n_out=None, bias=None, split_n=0, rider=None):
    if mode == "tn":
        k, m = a.shape
        n = b.shape[1] if n_out is None else n_out
        a_spec = pl.BlockSpec((k, tm), lambda i, j: (0, i))
        b_spec = pl.BlockSpec((k, tn), lambda i, j: (0, j + b_col0))
    elif mode == "nn":
        m, k = a.shape
        n = b.shape[1] if n_out is None else n_out
        a_spec = pl.BlockSpec((tm, k), lambda i, j: (i, 0))
        b_spec = pl.BlockSpec((k, tn), lambda i, j: (0, j + b_col0))
    else:
        m, k = a.shape
        n = b.shape[0] if n_out is None else n_out
        a_spec = pl.BlockSpec((tm, k), lambda i, j: (i, 0))
        b_spec = pl.BlockSpec((tn, k), lambda i, j: (j + b_col0, 0))
    assert m % tm == 0 and n % tn == 0, (name, m, n, tm, tn)
    dims = _DIMS[mode]

    def body(*refs):
        if bias is None:
            a_ref, b_ref, o_ref = refs
        else:
            a_ref, b_ref, bias_ref, o_ref = refs
        acc = lax.dot_general(a_ref[...].astype(BF16), b_ref[...].astype(BF16), dims, preferred_element_type=F32)
        if bias is not None:
            acc = acc + bias_ref[...]
        if split_n:
            for c in range(tn // split_n):
                o_ref[c] = acc[:, c * split_n:(c + 1) * split_n].astype(out_dtype)
        else:
            o_ref[...] = acc.astype(out_dtype)

    in_specs = [a_spec, b_spec]
    args = [a, b]
    if bias is not None:
        in_specs.append(pl.BlockSpec((1, tn), lambda i, j: (0, j)))
        args.append(bias)
    if split_n:
        out_spec = pl.BlockSpec((tn // split_n, tm, split_n), lambda i, j: (j, i, 0))
        out_shape = jax.ShapeDtypeStruct((n // split_n, m, split_n), out_dtype)
    else:
        out_spec = pl.BlockSpec((tm, tn), lambda i, j: (i, j))
        out_shape = jax.ShapeDtypeStruct((m, n), out_dtype)
    res = _call(body, name=name, grid=(m // tm, n // tn), in_specs=in_specs, out_specs=[out_spec],
                out_shape=[out_shape], scratch_shapes=[], args=args, rider=rider)
    return res[0] if rider is None else (res[0][0], res[1])


def _ln_hat(x):
    mu = jnp.mean(x, axis=-1, keepdims=True)
    xc = x - mu
    var = jnp.mean(xc * xc, axis=-1, keepdims=True)
    rstd = lax.rsqrt(var + LN_EPS)
    return xc * rstd, rstd


def _ln_hat_bwd(dhat, xhat, rstd):
    m1 = jnp.mean(dhat, axis=-1, keepdims=True)
    m2 = jnp.mean(dhat * xhat, axis=-1, keepdims=True)
    return rstd * (dhat - m1 - xhat * m2)


def _row_tile(s):
    return min(512, s)


def _acc_rows(ref, val, first):
    @pl.when(first)
    def _():
        ref[...] = jnp.zeros_like(ref)
    ref[...] += jnp.sum(val, axis=0, keepdims=True)


def _ln_mod(x, sc, sh, name):
    s, d = x.shape
    tm = _row_tile(s)

    def body(x_ref, sc_ref, sh_ref, u_ref):
        xhat, _ = _ln_hat(x_ref[...])
        u_ref[...] = (xhat * (1.0 + sc_ref[...]) + sh_ref[...]).astype(BF16)

    row = pl.BlockSpec((tm, d), lambda i: (i, 0))
    vec = pl.BlockSpec((1, d), lambda i: (0, 0))
    return pl.pallas_call(body, grid=(s // tm,), in_specs=[row, vec, vec], out_specs=row,
                          out_shape=jax.ShapeDtypeStruct((s, d), BF16), name=name, compiler_params=_cparams())(x, sc, sh)


def _resid_bwd_tile(dxo, x, f, g, gam):
    rhat, rstd = _ln_hat(ALPHA * x + g * f)
    dr = _ln_hat_bwd(dxo * gam, rhat, rstd)
    return ALPHA * dr, g * dr, dxo * rhat, dr * f


def _mm_ln_mod_bwd(a, b, x, sc, dres, name, rider=None, nxt=None):
    segs = list(a) if isinstance(a, (list, tuple)) else [a]
    s = segs[0].shape[0]
    k, d = b.shape
    assert sum(t.shape[1] for t in segs) == k
    tm = min(512 if k <= 4096 and nxt is None else 256, s)
    ns = len(segs)

    def body(*refs):
        seg_refs = refs[:ns]
        if nxt is None:
            b_ref, x_ref, sc_ref, dres_ref, dx_ref, dsc_ref, dsh_ref = refs[ns:]
        else:
            (b_ref, x_ref, sc_ref, dres_ref, xp_ref, fp_ref, gp_ref, gamp_ref,
             dresp_ref, dfp_ref, dsc_ref, dsh_ref, dgam_ref, dbet_ref, dg_ref, dbias_ref) = refs[ns:]
        first = pl.program_id(0) == 0
        duv, r0 = None, 0
        for seg_ref in seg_refs:
            w = seg_ref.shape[1]
            part = jnp.dot(seg_ref[...], b_ref[r0:r0 + w, :], preferred_element_type=F32)
            duv = part if duv is None else duv + part
            r0 += w
        xhat, rstd = _ln_hat(x_ref[...])
        dxv = dres_ref[...] + _ln_hat_bwd(duv * (1.0 + sc_ref[...]), xhat, rstd)
        _acc_rows(dsc_ref, duv * xhat, first)
        _acc_rows(dsh_ref, duv, first)
        if nxt is None:
            dx_ref[...] = dxv
        else:
            dresp, dfp, t_gam, t_g = _resid_bwd_tile(dxv, xp_ref[...], fp_ref[...], gp_ref[...], gamp_ref[...])
            dresp_ref[...] = dresp
            dfp_ref[...] = dfp.astype(BF16)
            _acc_rows(dgam_ref, t_gam, first)
            _acc_rows(dbet_ref, dxv, first)
            _acc_rows(dg_ref, t_g, first)
            _acc_rows(dbias_ref, dfp, first)

    row = pl.BlockSpec((tm, d), lambda i: (i, 0))
    vec = pl.BlockSpec((1, d), lambda i: (0, 0))
    vs = jax.ShapeDtypeStruct((1, d), F32)
    rows = jax.ShapeDtypeStruct((s, d), F32)
    in_specs = [pl.BlockSpec((tm, t.shape[1]), lambda i: (i, 0)) for t in segs] + [_full((k, d)), row, vec, row]
    args = (*segs, b, x, sc, dres)
    if nxt is None:
        out_specs, out_shape = [row, vec, vec], [rows, vs, vs]
    else:
        in_specs += [row, row, vec, vec]
        args += tuple(nxt)
        out_specs = [row, row] + [vec] * 6
        out_shape = [rows, jax.ShapeDtypeStruct((s, d), BF16)] + [vs] * 6
    res = _call(body, name=name, grid=(s // tm,), in_specs=in_specs, out_specs=out_specs, out_shape=out_shape,
                scratch_shapes=[], args=args, rider=rider)
    return tuple(res) if rider is None else (tuple(res[0]), res[1])


def _dw_segments(segs, u, name):
    s, d = u.shape
    tw = 256
    tiles = [t.shape[1] // tw for t in segs]
    starts = [sum(tiles[:j]) for j in range(len(segs))]
    ns = len(segs)

    def body(*refs):
        seg_refs, u_ref, o_ref = refs[:ns], refs[ns], refs[ns + 1]
        i = pl.program_id(0)
        for seg_ref, t0, nt in zip(seg_refs, starts, tiles):
            @pl.when((i >= t0) & (i < t0 + nt))
            def _(seg_ref=seg_ref):
                acc = lax.dot_general(seg_ref[...], u_ref[...], _DIMS["tn"], preferred_element_type=F32)
                o_ref[0] = acc[:, :d // 2].astype(BF16)
                o_ref[1] = acc[:, d // 2:].astype(BF16)

    def seg_spec(t0, nt):
        return pl.BlockSpec((s, tw), lambda i: (0, jnp.clip(i - t0, 0, nt - 1)))

    return pl.pallas_call(
        body, grid=(sum(tiles),), in_specs=[seg_spec(t0, nt) for t0, nt in zip(starts, tiles)] + [_full((s, d))],
        out_specs=pl.BlockSpec((2, tw, d // 2), lambda i: (0, i, 0)),
        out_shape=jax.ShapeDtypeStruct((2, sum(tiles) * tw, d // 2), BF16), name=name, compiler_params=_cparams(),
    )(*segs, u)


def _mm_resid_ln(a, b, bias, x, g, gam, bet, name, rider=None, mod_next=None):
    s, k = a.shape
    d = b.shape[1]
    tm = min(512, s)
    nb, nm = int(bias is not None), 2 * int(mod_next is not None)

    def body(*refs):
        a_ref, b_ref = refs[:2]
        x_ref, g_ref, gam_ref, bet_ref = refs[2 + nb:6 + nb]
        f_ref, o_ref = refs[6 + nb + nm:8 + nb + nm]
        f = jnp.dot(a_ref[...], b_ref[...], preferred_element_type=F32)
        if bias is not None:
            f = f + refs[2][...]
        f_ref[...] = f
        rhat, _ = _ln_hat(ALPHA * x_ref[...] + g_ref[...] * f)
        y = rhat * gam_ref[...] + bet_ref[...]
        o_ref[...] = y
        if mod_next is not None:
            sc_ref, sh_ref = refs[6 + nb:8 + nb]
            yhat, _ = _ln_hat(y)
            refs[8 + nb + nm][...] = (yhat * (1.0 + sc_ref[...]) + sh_ref[...]).astype(BF16)

    row = pl.BlockSpec((tm, d), lambda i: (i, 0))
    vec = pl.BlockSpec((1, d), lambda i: (0, 0))
    in_specs = [pl.BlockSpec((tm, k), lambda i: (i, 0)), _full((k, d))] + [vec] * nb + [row, vec, vec, vec] + [vec] * nm
    args = [a, b] + ([bias] if nb else []) + [x, g, gam, bet] + (list(mod_next) if nm else [])
    sh = jax.ShapeDtypeStruct((s, d), F32)
    out_specs, out_shape = [row, row], [sh, sh]
    if nm:
        out_specs, out_shape = out_specs + [row], out_shape + [jax.ShapeDtypeStruct((s, d), BF16)]
    res = _call(body, name=name, grid=(s // tm,), in_specs=in_specs, out_specs=out_specs, out_shape=out_shape,
                scratch_shapes=[], args=args, rider=rider)
    return tuple(res) if rider is None else (tuple(res[0]), res[1])


def _resid_ln_bwd(dxo, x, f, g, gam, name, tgt=None):
    s, d = x.shape
    tm = _row_tile(s)
    n = s // tm

    def body(*refs):
        if tgt is None:
            dxo_ref, x_ref, f_ref, g_ref, gam_ref, dres_ref, df_ref, dgam_ref, dbet_ref, dg_ref, dbias_ref = refs
            dxov = dxo_ref[...]
        else:
            (dxo_ref, t_ref, x_ref, f_ref, g_ref, gam_ref, dres_ref, df_ref, dgam_ref, dbet_ref, dg_ref, dbias_ref,
             loss_ref, sq_ref) = refs
            err = dxo_ref[...] - t_ref[...]
            dxov = err * (1.0 / d)
            _acc_rows(sq_ref, err * err, pl.program_id(0) == 0)

            @pl.when(pl.program_id(0) == n - 1)
            def _():
                tot = jnp.sum(sq_ref[...], axis=1, keepdims=True) * (0.5 / d)
                loss_ref[...] = jnp.broadcast_to(tot, (1, 128))

        first = pl.program_id(0) == 0
        dres, dfv, t_gam, t_g = _resid_bwd_tile(dxov, x_ref[...], f_ref[...], g_ref[...], gam_ref[...])
        dres_ref[...] = dres
        df_ref[...] = dfv.astype(BF16)
        _acc_rows(dgam_ref, t_gam, first)
        _acc_rows(dbet_ref, dxov, first)
        _acc_rows(dg_ref, t_g, first)
        _acc_rows(dbias_ref, dfv, first)

    row = pl.BlockSpec((tm, d), lambda i: (i, 0))
    vec = pl.BlockSpec((1, d), lambda i: (0, 0))
    vs = jax.ShapeDtypeStruct((1, d), F32)
    out_specs = [row, row, vec, vec, vec, vec]
    out_shape = [jax.ShapeDtypeStruct((s, d), F32), jax.ShapeDtypeStruct((s, d), BF16), vs, vs, vs, vs]
    if tgt is None:
        return pl.pallas_call(body, grid=(n,), in_specs=[row, row, row, vec, vec], out_specs=out_specs,
                              out_shape=out_shape, name=name, compiler_params=_cparams())(dxo, x, f, g, gam)
    return pl.pallas_call(body, grid=(n,), in_specs=[row, row, row, row, vec, vec],
                          out_specs=out_specs + [pl.BlockSpec((1, 128), lambda i: (0, 0))],
                          out_shape=out_shape + [jax.ShapeDtypeStruct((1, 128), F32)],
                          scratch_shapes=[pltpu.VMEM((1, d), F32)], name=name,
                          compiler_params=_cparams())(dxo, tgt, x, f, g, gam)


POOL_HALO = 16
POOL_ROWS = 256


def _pool_counts(r0, rows):
    t1 = (lax.broadcasted_iota(jnp.int32, (rows, 128), 0) + r0 + 1).astype(F32)
    low = lax.broadcasted_iota(jnp.int32, (rows, 128), 1) < POOL_GROUP
    wa = jnp.where(low, float(POOL_WINDOWS[0]), float(POOL_WINDOWS[1]))
    wb = jnp.where(low, float(POOL_WINDOWS[2]), float(POOL_WINDOWS[3]))
    return jnp.minimum(t1, wa), jnp.minimum(t1, wb), low


def _window_sums(win, off, rows, sign):
    def sl(j, half):
        return win[off + sign * j: off + sign * j + rows, 128 * half:128 * half + 128]
    a2 = sl(0, 0) + sl(1, 0)
    a4 = a2 + sl(2, 0) + sl(3, 0)
    a8 = sl(0, 1)
    for j in range(1, 8):
        a8 = a8 + sl(j, 1)
    a16 = a8
    for j in range(8, 16):
        a16 = a16 + sl(j, 1)
    return a2, a4, a8, a16


def _pool_fwd(zp, wp_bd, pscale, name):
    s = zp.shape[0]
    r = min(POOL_ROWS, s)

    def body(z_ref, wp_ref, sc_ref, p_ref, feat_ref, pad):
        pad[0:POOL_HALO, :] = jnp.zeros((POOL_HALO, D_POOL), F32)
        pad[POOL_HALO:, :] = z_ref[...]

        def step(i, carry):
            r0 = pl.multiple_of(i * r, r)
            win = pad[pl.ds(r0, r + POOL_HALO), :]
            a2, a4, a8, a16 = _window_sums(win, POOL_HALO, r, -1)
            ca, cb, low = _pool_counts(r0, r)
            x0 = win[POOL_HALO:, :]
            pa = jnp.where(low, a2, a4) / ca
            pb = jnp.where(low, a8, a16) / cb
            p = (jnp.concatenate([pa, pb], axis=1) - x0).astype(BF16)
            p_ref[pl.ds(r0, r), :] = p
            pw = jnp.dot(p, wp_ref[...], preferred_element_type=F32)
            feat_ref[pl.ds(r0, r), :] = (pw * sc_ref[...]).astype(BF16)
            return carry

        lax.fori_loop(0, s // r, step, 0)

    return pl.pallas_call(
        body, out_shape=[jax.ShapeDtypeStruct((s, D_POOL), BF16), jax.ShapeDtypeStruct((s, D_POOL), BF16)],
        scratch_shapes=[pltpu.VMEM((s + POOL_HALO, D_POOL), F32)], name=name, compiler_params=_cparams(),
    )(zp, wp_bd, pscale)


def _pool_bwd(dfeat, p, wp_bd, pscale, name):
    s = p.shape[0]
    r = min(POOL_ROWS, s)

    def body(df_ref, p_ref, wp_ref, sc_ref, dz_ref, dwp_ref, dsc_ref, gpad, dpbuf):
        dwp_ref[...] = jnp.zeros_like(dwp_ref)
        dsc_ref[...] = jnp.zeros_like(dsc_ref)
        gpad[s:, :] = jnp.zeros((POOL_HALO, D_POOL), F32)

        def step1(i, carry):
            r0 = pl.multiple_of(i * r, r)
            pv = p_ref[pl.ds(r0, r), :]
            dfv = df_ref[pl.ds(r0, r), :]
            pw = jnp.dot(pv, wp_ref[...], preferred_element_type=F32)
            dsc_ref[...] += jnp.sum(dfv * pw, axis=0, keepdims=True)
            dpw = (dfv * sc_ref[...]).astype(BF16)
            dwp_ref[...] += lax.dot_general(pv, dpw, _DIMS["tn"], preferred_element_type=F32)
            dp = lax.dot_general(dpw, wp_ref[...], _DIMS["nt"], preferred_element_type=F32)
            ca, cb, _ = _pool_counts(r0, r)
            gpad[pl.ds(r0, r), :] = dp / jnp.concatenate([ca, cb], axis=1)
            dpbuf[pl.ds(r0, r), :] = dp
            return carry

        lax.fori_loop(0, s // r, step1, 0)

        def step2(i, carry):
            r0 = pl.multiple_of(i * r, r)
            win = gpad[pl.ds(r0, r + POOL_HALO), :]
            a2, a4, a8, a16 = _window_sums(win, 0, r, 1)
            low = lax.broadcasted_iota(jnp.int32, (r, 128), 1) < POOL_GROUP
            acc = jnp.concatenate([jnp.where(low, a2, a4), jnp.where(low, a8, a16)], axis=1)
            dz_ref[pl.ds(r0, r), :] = (acc - dpbuf[pl.ds(r0, r), :]).astype(BF16)
            return carry

        lax.fori_loop(0, s // r, step2, 0)

    return pl.pallas_call(
        body,
        out_shape=[jax.ShapeDtypeStruct((s, D_POOL), BF16), jax.ShapeDtypeStruct((D_POOL, D_POOL), F32),
                   jax.ShapeDtypeStruct((1, D_POOL), F32)],
        scratch_shapes=[pltpu.VMEM((s + POOL_HALO, D_POOL), F32), pltpu.VMEM((s, D_POOL), F32)],
        name=name, compiler_params=_cparams(),
    )(dfeat, p, wp_bd, pscale)


def _skew_index():
    cp = lax.broadcasted_iota(jnp.int32, (SKEW_W, N_REL), 0)
    dist = jnp.where(cp < KW, KPAD - cp, KPAD + SKEW_W - cp)
    idx = jnp.clip(dist, -REL_CLIP, REL_CLIP) + REL_CLIP
    return (idx == lax.broadcasted_iota(jnp.int32, (SKEW_W, N_REL), 1)).astype(F32)


def _row_bits(b):
    return (lax.broadcasted_iota(jnp.int32, (QB, SKEW_W), 0) >> b) & 1 == 1


N_EDGE = KPAD // QB


def _bias_block(rel_bias, name):
    def body(rb_ref, o_ref):
        onehot = _skew_index()
        row0 = lax.dot_general(rb_ref[...], onehot, _DIMS["nt"], precision=lax.Precision.HIGHEST,
                               preferred_element_type=F32)
        r = lax.broadcasted_iota(jnp.int32, (QB, KW), 0)
        kk = lax.broadcasted_iota(jnp.int32, (QB, KW), 1)
        cq, ck = r // CHUNK, kk // CHUNK
        band = (ck >= cq) & (ck <= cq + N_PREV_CHUNKS)
        for h in range(N_HEADS):
            t = jnp.broadcast_to(row0[h:h + 1, :], (QB, SKEW_W))
            for b in range(7):
                t = jnp.where(_row_bits(b), pltpu.roll(t, 1 << b, 1), t)
            for e in range(N_EDGE + 1):
                o_ref[e, h] = jnp.where(band & (kk >= KPAD - e * QB), t[:, :KW], NEG_INF)

    return pl.pallas_call(body, out_shape=jax.ShapeDtypeStruct((N_EDGE + 1, N_HEADS, QB, KW), F32), name=name,
                          compiler_params=_cparams())(rel_bias)


def _bias_spec():
    return pl.BlockSpec((None, N_HEADS, QB, KW), lambda i: (jnp.minimum(i, N_EDGE), 0, 0, 0))


def _bias_block_bwd(ds_acc, name):
    def body(ds_ref, o_ref):
        sums = []
        for h in range(N_HEADS):
            t = jnp.concatenate([ds_ref[h], jnp.zeros((QB, SKEW_W - KW), F32)], axis=1)
            for b in range(7):
                t = jnp.where(_row_bits(b), pltpu.roll(t, SKEW_W - (1 << b), 1), t)
            sums.append(jnp.sum(t, axis=0, keepdims=True))
        allh = jnp.concatenate(sums, axis=0)
        o_ref[...] = jnp.dot(allh, _skew_index(), precision=lax.Precision.HIGHEST, preferred_element_type=F32)

    return pl.pallas_call(body, out_shape=jax.ShapeDtypeStruct((N_HEADS, N_REL), F32), name=name,
                          compiler_params=_cparams())(ds_acc)


def _scaled(q):
    return (q.astype(F32) * (HEAD_DIM ** -0.5)).astype(BF16)


def _probs(q, kw, bias_ref):
    sc = jnp.stack([lax.dot_general(q[:, HEAD_DIM * h:HEAD_DIM * (h + 1)], kw[:, HEAD_DIM * h:HEAD_DIM * (h + 1)],
                                    _DIMS["nt"], preferred_element_type=F32) + bias_ref[h] for h in range(N_HEADS)])
    e = jnp.exp(sc - jnp.max(sc, axis=-1, keepdims=True))
    return e * (1.0 / jnp.sum(e, axis=-1, keepdims=True))


def _load_padded_kv(qkv_hbm, kpad, vpad, sems, s):
    kpad[0:KPAD, :] = jnp.zeros((KPAD, D_ATTN), BF16)
    vpad[0:KPAD, :] = jnp.zeros((KPAD, D_ATTN), BF16)
    ck = pltpu.make_async_copy(qkv_hbm.at[:, D_ATTN:2 * D_ATTN], kpad.at[pl.ds(KPAD, s), :], sems.at[0])
    cv = pltpu.make_async_copy(qkv_hbm.at[:, 2 * D_ATTN:3 * D_ATTN], vpad.at[pl.ds(KPAD, s), :], sems.at[1])
    ck.start()
    cv.start()
    ck.wait()
    cv.wait()


def _attn_fwd(qkv, bias, name, rider=None):
    s = qkv.shape[0]

    def body(q_ref, qkv_hbm, bias_ref, o_ref, p_ref, kpad, vpad, sems):
        i = pl.program_id(0)

        @pl.when(i == 0)
        def _():
            _load_padded_kv(qkv_hbm, kpad, vpad, sems, s)

        base = pl.multiple_of(i * QB, QB)
        kw = kpad[pl.ds(base, KW), :]
        vw = vpad[pl.ds(base, KW), :]
        q = _scaled(q_ref[...])
        p = _probs(q, kw, bias_ref).astype(BF16)
        p_ref[...] = p
        outs = [jnp.dot(p[h], vw[:, HEAD_DIM * h:HEAD_DIM * (h + 1)], preferred_element_type=F32)
                for h in range(N_HEADS)]
        o_ref[...] = jnp.concatenate(outs, axis=1).astype(BF16)

    res = _call(
        body, name=name, grid=(s // QB,),
        in_specs=[pl.BlockSpec((QB, D_ATTN), lambda i: (i, 0)), pl.BlockSpec(memory_space=pl.ANY),
                  _bias_spec()],
        out_specs=[pl.BlockSpec((QB, D_ATTN), lambda i: (i, 0)), _probs_spec()],
        out_shape=[jax.ShapeDtypeStruct((s, D_ATTN), BF16), jax.ShapeDtypeStruct((N_HEADS, s, KW), BF16)],
        scratch_shapes=[pltpu.VMEM((s + KPAD, D_ATTN), BF16), pltpu.VMEM((s + KPAD, D_ATTN), BF16),
                        pltpu.SemaphoreType.DMA((2,))],
        args=(qkv, qkv, bias), rider=rider)
    return tuple(res) if rider is None else (tuple(res[0]), res[1])


def _probs_spec():
    return pl.BlockSpec((N_HEADS, QB, KW), lambda i: (0, i, 0))


def _attn_bwd(qkv, do, probs, name, rider=None):
    s = qkv.shape[0]
    n = s // QB

    def body(q_ref, qkv_hbm, do_ref, p_ref, dq_ref, dk_hbm, dv_hbm, ds_ref, kpad, vpad, dkacc, dvacc, sems):
        i = pl.program_id(0)

        @pl.when(i == 0)
        def _():
            _load_padded_kv(qkv_hbm, kpad, vpad, sems, s)
            dkacc[...] = jnp.zeros_like(dkacc)
            dvacc[...] = jnp.zeros_like(dvacc)
            ds_ref[...] = jnp.zeros_like(ds_ref)

        base = pl.multiple_of(i * QB, QB)
        kw = kpad[pl.ds(base, KW), :]
        vw = vpad[pl.ds(base, KW), :]
        q = _scaled(q_ref[...])
        dov = do_ref[...]
        heads = [slice(HEAD_DIM * h, HEAD_DIM * (h + 1)) for h in range(N_HEADS)]
        pb = p_ref[...]
        p = pb.astype(F32)
        dp = jnp.stack([lax.dot_general(dov[:, hs], vw[:, hs], _DIMS["nt"], preferred_element_type=F32) for hs in heads])
        ds = p * (dp - jnp.sum(dp * p, axis=-1, keepdims=True))
        ds_ref[...] += ds
        dsb = ds.astype(BF16)
        dvs = [lax.dot_general(pb[h], dov[:, hs], _DIMS["tn"], preferred_element_type=F32) for h, hs in enumerate(heads)]
        dqs = [jnp.dot(dsb[h], kw[:, hs], preferred_element_type=F32) for h, hs in enumerate(heads)]
        dks = [lax.dot_general(dsb[h], q[:, hs], _DIMS["tn"], preferred_element_type=F32) for h, hs in enumerate(heads)]
        dq_ref[...] = (jnp.concatenate(dqs, axis=1) * (HEAD_DIM ** -0.5)).astype(BF16)
        dkacc[pl.ds(base, KW), :] += jnp.concatenate(dks, axis=1)
        dvacc[pl.ds(base, KW), :] += jnp.concatenate(dvs, axis=1)

        @pl.when(i == n - 1)
        def _():
            def cast(j, carry):
                rows = pl.ds(pl.multiple_of(KPAD + j * 512, 512), 512)
                kpad[rows, :] = dkacc[rows, :].astype(BF16)
                vpad[rows, :] = dvacc[rows, :].astype(BF16)
                return carry

            lax.fori_loop(0, s // 512, cast, 0)
            ck = pltpu.make_async_copy(kpad.at[pl.ds(KPAD, s), :], dk_hbm, sems.at[0])
            cv = pltpu.make_async_copy(vpad.at[pl.ds(KPAD, s), :], dv_hbm, sems.at[1])
            ck.start()
            cv.start()
            ck.wait()
            cv.wait()

    blk = pl.BlockSpec((QB, D_ATTN), lambda i: (i, 0))
    acc_shape = jax.ShapeDtypeStruct((s, D_ATTN), BF16)
    return _call(
        body, name=name, grid=(n,),
        in_specs=[blk, pl.BlockSpec(memory_space=pl.ANY), blk, _probs_spec()],
        out_specs=[blk, pl.BlockSpec(memory_space=pl.ANY), pl.BlockSpec(memory_space=pl.ANY), _full((N_HEADS, QB, KW))],
        out_shape=[jax.ShapeDtypeStruct((s, D_ATTN), BF16), acc_shape, acc_shape,
                   jax.ShapeDtypeStruct((N_HEADS, QB, KW), F32)],
        scratch_shapes=[pltpu.VMEM((s + KPAD, D_ATTN), BF16), pltpu.VMEM((s + KPAD, D_ATTN), BF16),
                        pltpu.VMEM((s + KPAD, D_ATTN), F32), pltpu.VMEM((s + KPAD, D_ATTN), F32),
                        pltpu.SemaphoreType.DMA((2,))],
        args=(qkv, qkv, do, probs), rider=rider)


CONV_HALO = 32
CONV_ROWS = 64


def _sigmoid(t):
    return 1.0 / (1.0 + jnp.exp(-t))


CONV_WIN = CONV_ROWS + CONV_HALO - 8


def _row_windows(ref, r0, buf):
    win = ref[pl.ds(r0, CONV_ROWS + CONV_HALO), :]
    for j in range(1, 8):
        buf[j - 1] = win[j:j + CONV_WIN, :]

    def get(o):
        j, a = o % 8, o - o % 8
        if j == 0:
            return ref[pl.ds(r0 + a, CONV_ROWS), :]
        return buf[j - 1, a:a + CONV_ROWS, :]

    return get


def _glu_rows(z_ref, r0, rows):
    a = z_ref[pl.ds(r0, rows), 0:D_CONV]
    b = z_ref[pl.ds(r0, rows), D_CONV:2 * D_CONV]
    return a, _sigmoid(b)


def _conv_fwd(zc, conv_w, conv_b, ln_g, ln_b, name):
    s = zc.shape[0]
    rt = min(256, s)

    def body(z_ref, w_ref, cb_ref, g_ref, b_ref, cv_ref, feat_ref, hpad, shifts):
        hpad[0:CONV_HALO, :] = jnp.zeros((CONV_HALO, D_CONV), F32)

        def glu(i, carry):
            r0 = pl.multiple_of(i * rt, rt)
            a, sb = _glu_rows(z_ref, r0, rt)
            hpad[pl.ds(r0 + CONV_HALO, rt), :] = a * sb
            return carry

        lax.fori_loop(0, s // rt, glu, 0)
        w = w_ref[...]

        def conv(i, carry):
            r0 = pl.multiple_of(i * CONV_ROWS, CONV_ROWS)
            win = _row_windows(hpad, r0, shifts)
            acc = jnp.broadcast_to(cb_ref[...], (CONV_ROWS, D_CONV))
            for k in range(CONV_WIDTH):
                acc = acc + win(2 + k) * w[k:k + 1, :]
            cv_ref[pl.ds(r0, CONV_ROWS), :] = acc
            yhat, _ = _ln_hat(acc)
            y = yhat * g_ref[...] + b_ref[...]
            feat_ref[pl.ds(r0, CONV_ROWS), :] = (y * _sigmoid(y)).astype(BF16)
            return carry

        lax.fori_loop(0, s // CONV_ROWS, conv, 0)

    return pl.pallas_call(
        body, out_shape=[jax.ShapeDtypeStruct((s, D_CONV), F32), jax.ShapeDtypeStruct((s, D_CONV), BF16)],
        scratch_shapes=[pltpu.VMEM((s + CONV_HALO, D_CONV), F32), pltpu.VMEM((7, CONV_WIN, D_CONV), F32)],
        name=name, compiler_params=_cparams(),
    )(zc, conv_w, conv_b, ln_g, ln_b)


def _conv_bwd(dfeat, cv, zc, conv_w, ln_g, ln_b, name):
    s = zc.shape[0]
    rt = min(256, s)

    def body(df_ref, cv_ref, z_ref, w_ref, g_ref, b_ref, dz_ref, dw_ref, dcb_ref, dg_ref, db_ref, hpad, dcvpad, dwacc,
             hshifts, dshifts):
        hpad[0:CONV_HALO, :] = jnp.zeros((CONV_HALO, D_CONV), F32)
        dcvpad[s:, :] = jnp.zeros((CONV_HALO, D_CONV), F32)
        dwacc[...] = jnp.zeros_like(dwacc)
        dcb_ref[...] = jnp.zeros_like(dcb_ref)
        dg_ref[...] = jnp.zeros_like(dg_ref)
        db_ref[...] = jnp.zeros_like(db_ref)

        def pass1(i, carry):
            r0 = pl.multiple_of(i * rt, rt)
            a, sb = _glu_rows(z_ref, r0, rt)
            hpad[pl.ds(r0 + CONV_HALO, rt), :] = a * sb
            cvhat, rstd = _ln_hat(cv_ref[pl.ds(r0, rt), :])
            y = cvhat * g_ref[...] + b_ref[...]
            sg = _sigmoid(y)
            dy = df_ref[pl.ds(r0, rt), :] * (sg * (1.0 + y * (1.0 - sg)))
            dg_ref[...] += jnp.sum(dy * cvhat, axis=0, keepdims=True)
            db_ref[...] += jnp.sum(dy, axis=0, keepdims=True)
            dcv = _ln_hat_bwd(dy * g_ref[...], cvhat, rstd)
            dcb_ref[...] += jnp.sum(dcv, axis=0, keepdims=True)
            dcvpad[pl.ds(r0, rt), :] = dcv
            return carry

        lax.fori_loop(0, s // rt, pass1, 0)
        w = w_ref[...]

        def pass2(i, carry):
            r0 = pl.multiple_of(i * CONV_ROWS, CONV_ROWS)
            dwin = _row_windows(dcvpad, r0, dshifts)
            hwin = _row_windows(hpad, r0, hshifts)
            dcv = dwin(0)
            dh = jnp.zeros((CONV_ROWS, D_CONV), F32)
            for k in range(CONV_WIDTH):
                dh = dh + dwin(30 - k) * w[k:k + 1, :]
                prod = dcv * hwin(2 + k)
                dwacc[8 * k:8 * k + 8, :] += jnp.sum(prod.reshape(CONV_ROWS // 8, 8, D_CONV), axis=0)
            a, sb = _glu_rows(z_ref, r0, CONV_ROWS)
            dz_ref[pl.ds(r0, CONV_ROWS), :] = jnp.concatenate([dh * sb, dh * a * sb * (1.0 - sb)], axis=1).astype(BF16)
            return carry

        lax.fori_loop(0, s // CONV_ROWS, pass2, 0)
        dw_ref[...] = jnp.sum(dwacc[...].reshape(32, 8, D_CONV), axis=1)

    vs = jax.ShapeDtypeStruct((1, D_CONV), F32)
    return pl.pallas_call(
        body,
        out_shape=[jax.ShapeDtypeStruct((s, 2 * D_CONV), BF16), jax.ShapeDtypeStruct((32, D_CONV), F32), vs, vs, vs],
        scratch_shapes=[pltpu.VMEM((s + CONV_HALO, D_CONV), F32), pltpu.VMEM((s + CONV_HALO, D_CONV), F32),
                        pltpu.VMEM((256, D_CONV), F32), pltpu.VMEM((7, CONV_WIN, D_CONV), F32),
                        pltpu.VMEM((7, CONV_WIN, D_CONV), F32)],
        name=name, compiler_params=_cparams(),
    )(dfeat, cv, zc, conv_w, ln_g, ln_b)


def _branch_out(feats, wts, name):
    s = feats[0].shape[0]
    tm = min(1024, s)

    def body(*refs):
        for f_ref, w_ref, o_ref in zip(refs[:3], refs[3:6], refs[6:]):
            o_ref[...] = lax.dot_general(f_ref[...], w_ref[...], _DIMS["nt"], preferred_element_type=F32).astype(BF16)

    row = pl.BlockSpec((tm, D_MODEL), lambda i: (i, 0))
    sh = jax.ShapeDtypeStruct((s, D_MODEL), BF16)
    return pl.pallas_call(
        body, grid=(s // tm,),
        in_specs=[pl.BlockSpec((tm, f.shape[1]), lambda i: (i, 0)) for f in feats] + [_full(w.shape) for w in wts],
        out_specs=[row] * 3, out_shape=[sh] * 3, name=name, compiler_params=_cparams(),
    )(*feats, *wts)


def _branch_in_bwd(dys, wts, out_dtypes, name):
    s = dys[0].shape[0]
    tm = min(1024, s)

    def body(*refs):
        for d_ref, w_ref, o_ref in zip(refs[:3], refs[3:6], refs[6:]):
            o_ref[...] = jnp.dot(d_ref[...], w_ref[...], preferred_element_type=F32).astype(o_ref.dtype)

    row = pl.BlockSpec((tm, D_MODEL), lambda i: (i, 0))
    return pl.pallas_call(
        body, grid=(s // tm,), in_specs=[row] * 3 + [_full(w.shape) for w in wts],
        out_specs=[pl.BlockSpec((tm, w.shape[1]), lambda i: (i, 0)) for w in wts],
        out_shape=[jax.ShapeDtypeStruct((s, w.shape[1]), dt) for w, dt in zip(wts, out_dtypes)],
        name=name, compiler_params=_cparams(),
    )(*dys, *wts)


def _branch_dw(dys, feats, name):
    s = dys[0].shape[0]
    tm = 512

    def body(*refs):
        for d_ref, f_ref, o_ref in zip(refs[:3], refs[3:6], refs[6:]):
            acc = lax.dot_general(d_ref[...], f_ref[...], _DIMS["tn"], preferred_element_type=F32)
            half = acc.shape[1] // 2
            o_ref[0] = acc[:, :half].astype(BF16)
            o_ref[1] = acc[:, half:].astype(BF16)

    return pl.pallas_call(
        body, grid=(D_MODEL // tm,),
        in_specs=[pl.BlockSpec((s, tm), lambda i: (0, i))] * 3 + [_full(f.shape) for f in feats],
        out_specs=[pl.BlockSpec((2, tm, f.shape[1] // 2), lambda i: (0, i, 0)) for f in feats],
        out_shape=[jax.ShapeDtypeStruct((2, D_MODEL, f.shape[1] // 2), BF16) for f in feats],
        name=name, compiler_params=_cparams(),
    )(*dys, *feats)


def _merge(zg, b_gate, ys, name):
    s = zg.shape[0]
    tm = _row_tile(s)

    def body(zg_ref, bg_ref, y0_ref, y1_ref, y2_ref, o_ref):
        acc = None
        for j, y_ref in enumerate((y0_ref, y1_ref, y2_ref)):
            cs = slice(D_MODEL * j, D_MODEL * (j + 1))
            t = _sigmoid(zg_ref[:, cs] + bg_ref[:, cs]) * y_ref[...]
            acc = t if acc is None else acc + t
        o_ref[...] = acc.astype(BF16)

    row = pl.BlockSpec((tm, D_MODEL), lambda i: (i, 0))
    return pl.pallas_call(
        body, grid=(s // tm,),
        in_specs=[pl.BlockSpec((tm, 3 * D_MODEL), lambda i: (i, 0)), _full((1, 3 * D_MODEL)), row, row, row],
        out_specs=row, out_shape=jax.ShapeDtypeStruct((s, D_MODEL), BF16), name=name, compiler_params=_cparams(),
    )(zg, b_gate, *ys)


def _merge_bwd(dmix, w_o, zg, b_gate, ys, name):
    s = zg.shape[0]
    tm = min(256, s)

    def body(dmix_ref, wo_ref, zg_ref, bg_ref, y0_ref, y1_ref, y2_ref, d0_ref, d1_ref, d2_ref, dzg_ref, dbg_ref):
        first = pl.program_id(0) == 0

        @pl.when(first)
        def _():
            dbg_ref[...] = jnp.zeros_like(dbg_ref)

        dmv = lax.dot_general(dmix_ref[...], wo_ref[...], _DIMS["nt"], preferred_element_type=F32)
        for j, (y_ref, d_ref) in enumerate(((y0_ref, d0_ref), (y1_ref, d1_ref), (y2_ref, d2_ref))):
            cs = slice(D_MODEL * j, D_MODEL * (j + 1))
            g = _sigmoid(zg_ref[:, cs] + bg_ref[:, cs])
            d_ref[...] = (dmv * g).astype(BF16)
            dzg = dmv * y_ref[...] * g * (1.0 - g)
            dzg_ref[:, cs] = dzg.astype(BF16)
            dbg_ref[:, cs] += jnp.sum(dzg, axis=0, keepdims=True)

    row = pl.BlockSpec((tm, D_MODEL), lambda i: (i, 0))
    wide = pl.BlockSpec((tm, 3 * D_MODEL), lambda i: (i, 0))
    yb = jax.ShapeDtypeStruct((s, D_MODEL), BF16)
    return pl.pallas_call(
        body, grid=(s // tm,),
        in_specs=[row, _full(w_o.shape), wide, _full((1, 3 * D_MODEL)), row, row, row],
        out_specs=[row, row, row, wide, _full((1, 3 * D_MODEL))],
        out_shape=[yb, yb, yb, jax.ShapeDtypeStruct((s, 3 * D_MODEL), BF16), jax.ShapeDtypeStruct((1, 3 * D_MODEL), F32)],
        name=name, compiler_params=_cparams(),
    )(dmix, w_o, zg, b_gate, *ys)


def _ff_hidden(u2, w_ff1t, b_ff1, name, rider=None):
    s = u2.shape[0]
    tm, tn = min(2048, s), 1024

    def body(a_ref, b_ref, bias_ref, pre_ref, h_ref):
        acc = lax.dot_general(a_ref[...], b_ref[...], _DIMS["nt"], preferred_element_type=F32) + bias_ref[...]
        pre_ref[...] = acc.astype(BF16)
        h_ref[...] = _relu2(acc).astype(BF16)

    blk = pl.BlockSpec((tm, tn), lambda i, j: (i, j))
    sh = jax.ShapeDtypeStruct((s, D_FF), BF16)
    res = _call(body, name=name, grid=(s // tm, D_FF // tn),
                in_specs=[pl.BlockSpec((tm, D_MODEL), lambda i, j: (i, 0)), pl.BlockSpec((tn, D_MODEL), lambda i, j: (j, 0)),
                          pl.BlockSpec((1, tn), lambda i, j: (0, j))],
                out_specs=[blk, blk], out_shape=[sh, sh], scratch_shapes=[], args=(u2, w_ff1t, b_ff1), rider=rider)
    return tuple(res) if rider is None else (tuple(res[0]), res[1])


def _ff_hidden_bwd(dff, w_ff2, hpre, name):
    s = dff.shape[0]
    tm, tn = min(1024, s), 1024

    def body(a_ref, b_ref, h_ref, o_ref, sum_ref):
        dh = lax.dot_general(a_ref[...], b_ref[...], _DIMS["nt"], preferred_element_type=F32)
        dpre = dh * (2.0 * jnp.maximum(h_ref[...].astype(F32), 0.0))
        o_ref[...] = dpre.astype(BF16)
        _acc_rows(sum_ref, dpre, pl.program_id(1) == 0)

    res = _call(
        body, name=name, grid=(D_FF // tn, s // tm),
        in_specs=[pl.BlockSpec((tm, D_MODEL), lambda j, i: (i, 0)), pl.BlockSpec((tn, D_MODEL), lambda j, i: (j, 0)),
                  pl.BlockSpec((tm, tn), lambda j, i: (i, j))],
        out_specs=[pl.BlockSpec((tm, tn), lambda j, i: (i, j)), pl.BlockSpec((1, tn), lambda j, i: (0, j))],
        out_shape=[jax.ShapeDtypeStruct((s, D_FF), BF16), jax.ShapeDtypeStruct((1, D_FF), F32)],
        scratch_shapes=[], args=(dff, w_ff2, hpre))
    return tuple(res)


def _silu(t):
    return t * _sigmoid(t)


def _mod_fwd(c_all, w_ada_sh, b_ada_sh, name):
    cols = w_ada_sh.shape[2]

    def body(c_ref, w_ref, b_ref, o_ref):
        ca = _silu(c_ref[...]).astype(BF16)
        o_ref[0] = jnp.dot(ca, w_ref[0].astype(BF16), preferred_element_type=F32) + b_ref[0]

    return pl.pallas_call(
        body, grid=(DEPTH,),
        in_specs=[_full((N_DEV, D_MODEL)), pl.BlockSpec((1, D_MODEL, cols), lambda l: (l, 0, 0)),
                  pl.BlockSpec((1, 1, cols), lambda l: (l, 0, 0))],
        out_specs=pl.BlockSpec((1, N_DEV, cols), lambda l: (l, 0, 0)),
        out_shape=jax.ShapeDtypeStruct((DEPTH, N_DEV, cols), F32), name=name, compiler_params=_cparams(),
    )(c_all, w_ada_sh, b_ada_sh)


def _mod_bwd(c_all, dmod_sh, name):
    cols = dmod_sh.shape[2]

    def body(c_ref, d_ref, o_ref):
        ca = _silu(c_ref[...])
        o_ref[0] = lax.dot_general(ca, d_ref[0], _DIMS["tn"], precision=lax.Precision.HIGHEST,
                                   preferred_element_type=F32)

    return pl.pallas_call(
        body, grid=(DEPTH,),
        in_specs=[_full((N_DEV, D_MODEL)), pl.BlockSpec((1, N_DEV, cols), lambda l: (l, 0, 0))],
        out_specs=pl.BlockSpec((1, D_MODEL, cols), lambda l: (l, 0, 0)),
        out_shape=jax.ShapeDtypeStruct((DEPTH, D_MODEL, cols), F32), name=name, compiler_params=_cparams(),
    )(c_all, dmod_sh)


def _flat_tiles(rows, cols, itemsize_total):
    budget = 12 * 1024 * 1024
    tr = rows
    while tr % 32 == 0 and tr * cols * itemsize_total > budget:
        tr //= 2
    return tr


def _sum_cores(dw, recv, place, name):
    _, m, n = dw.shape
    tr = _flat_tiles(m, n, 6)

    def body(place_ref, a_ref, b_ref, o_ref):
        o_ref[...] = (a_ref[...].astype(F32) + b_ref[...].astype(F32)).astype(BF16)

    grid_spec = pltpu.PrefetchScalarGridSpec(
        num_scalar_prefetch=1, grid=(m // tr,),
        in_specs=[pl.BlockSpec((None, tr, n), lambda i, pr: (pr[0], i, 0)), pl.BlockSpec((tr, n), lambda i, pr: (i, 0))],
        out_specs=pl.BlockSpec((tr, n), lambda i, pr: (i, 0)))
    return pl.pallas_call(body, grid_spec=grid_spec, out_shape=jax.ShapeDtypeStruct((m, n), BF16), name=name,
                          compiler_params=_cparams())(place, dw, recv)


def _sum_chips(h, r, place, name):
    _, rs, n = h.shape
    tr = _flat_tiles(rs, n, 12)

    def body(place_ref, h_ref, r_ref, o_ref):
        o_ref[...] = ((h_ref[...].astype(F32) + r_ref[0].astype(F32)) + r_ref[1].astype(F32)) + r_ref[2].astype(F32)

    grid_spec = pltpu.PrefetchScalarGridSpec(
        num_scalar_prefetch=1, grid=(rs // tr,),
        in_specs=[pl.BlockSpec((None, tr, n), lambda i, pr: (pr[1], i, 0)), pl.BlockSpec((3, tr, n), lambda i, pr: (0, i, 0))],
        out_specs=pl.BlockSpec((tr, n), lambda i, pr: (i, 0)))
    return pl.pallas_call(body, grid_spec=grid_spec, out_shape=jax.ShapeDtypeStruct((rs, n), F32), name=name,
                          compiler_params=_cparams())(place, h, r)


def _adam_math(w, g, m, v):
    m2 = ADAM_B1 * m + (1.0 - ADAM_B1) * g
    v2 = ADAM_B2 * v + (1.0 - ADAM_B2) * (g * g)
    m_hat = m2 / (1.0 - ADAM_B1 ** ADAM_STEP)
    v_hat = v2 / (1.0 - ADAM_B2 ** ADAM_STEP)
    delta = -ADAM_LR * (m_hat / (jnp.sqrt(v_hat) + ADAM_EPS) + ADAM_WD * w)
    return delta, m2, v2


def _adamw(w, m, v, grads, name):
    r, c = w.shape
    tr = _flat_tiles(r, c, 4 * (7 + len(grads)))

    def body(*refs):
        w_ref, m_ref, v_ref = refs[:3]
        g_refs = refs[3:3 + len(grads)]
        g_ref, d_ref, m2_ref, v2_ref = refs[3 + len(grads):]
        g = g_refs[0][...]
        for gr in g_refs[1:]:
            g = g + gr[...]
        delta, m2, v2 = _adam_math(w_ref[...], g, m_ref[...], v_ref[...])
        g_ref[...] = g
        d_ref[...] = delta
        m2_ref[...] = m2
        v2_ref[...] = v2

    blk = pl.BlockSpec((tr, c), lambda i: (i, 0))
    sh = jax.ShapeDtypeStruct((r, c), F32)
    return pl.pallas_call(body, grid=(r // tr,), in_specs=[blk] * (3 + len(grads)), out_specs=[blk] * 4,
                          out_shape=[sh] * 4, name=name, compiler_params=_cparams())(w, m, v, *grads)


def _adamw_halves(w, m, v, own, other, place, split, name):
    nl, r, c = w.shape
    hr, hc = own[0].shape
    tr = _flat_tiles(hr, hc, 4 * (7 + 2 * nl))
    nt = hr // tr
    if split == "rows":
        w_spec = pl.BlockSpec((None, tr, c), lambda l, h, t, pr: (l, h * nt + t, 0))
    else:
        w_spec = pl.BlockSpec((None, tr, hc), lambda l, h, t, pr: (l, t, h))

    def g_spec(layer, mine):
        return pl.BlockSpec((tr, hc), lambda l, h, t, pr: (jnp.where((l == layer) & ((h == pr[0]) == mine), t, nt - 1), 0))

    def body(place_ref, w_ref, m_ref, v_ref, *refs):
        own_refs, other_refs = refs[:nl], refs[nl:2 * nl]
        g_ref, d_ref, m2_ref, v2_ref = refs[2 * nl:]
        layer = pl.program_id(0)
        mine = pl.program_id(1) == place_ref[0]
        g = None
        for li in range(nl):
            cand = jnp.where(mine, own_refs[li][...], other_refs[li][...])
            g = cand if g is None else jnp.where(layer == li, cand, g)
        delta, m2, v2 = _adam_math(w_ref[...], g, m_ref[...], v_ref[...])
        g_ref[...] = g
        d_ref[...] = delta
        m2_ref[...] = m2
        v2_ref[...] = v2

    sh = jax.ShapeDtypeStruct((nl, r, c), F32)
    g_specs = [g_spec(li, True) for li in range(nl)] + [g_spec(li, False) for li in range(nl)]
    return _call(body, name=name, grid=(nl, 2, nt), in_specs=[w_spec] * 3 + g_specs, out_specs=[w_spec] * 4,
                 out_shape=[sh] * 4, scratch_shapes=[], args=(w, m, v, *own, *other), prefetch=(place,))


def _adamw_small(w, m, v, g_all, name):
    r, c = w.shape

    def body(w_ref, m_ref, v_ref, g_ref, go_ref, d_ref, m2_ref, v2_ref):
        g = g_ref[0]
        for b in range(1, N_DEV):
            g = g + g_ref[b]
        delta, m2, v2 = _adam_math(w_ref[...], g, m_ref[...], v_ref[...])
        go_ref[...] = g
        d_ref[...] = delta
        m2_ref[...] = m2
        v2_ref[...] = v2

    sh = jax.ShapeDtypeStruct((r, c), F32)
    return pl.pallas_call(body, out_shape=[sh] * 4, name=name, compiler_params=_cparams())(w, m, v, g_all)


def _me():
    return lax.axis_index("x"), lax.axis_index("y"), lax.axis_index("c")


def _flip(v, bit):
    return 1 - v if bit else v


def _allgather_small(blk, name):
    r, c = blk.shape

    def body(x_ref, o_ref, send_sems, recv_sems):
        x, y, cc = _me()
        me = 4 * x + 2 * y + cc
        copies = []
        for k in range(1, N_DEV):
            peer = (_flip(x, k & 4), _flip(y, k & 2), _flip(cc, k & 1))
            cp = pltpu.make_async_remote_copy(src_ref=x_ref, dst_ref=o_ref.at[me], send_sem=send_sems.at[k - 1],
                                              recv_sem=recv_sems.at[k - 1], device_id=peer, device_id_type=MESH)
            cp.start()
            copies.append(cp)
        o_ref[me] = x_ref[...]
        for cp in copies:
            cp.wait()

    return pl.pallas_call(
        body, out_shape=jax.ShapeDtypeStruct((N_DEV, r, c), F32),
        in_specs=[pl.BlockSpec(memory_space=pltpu.VMEM)], out_specs=pl.BlockSpec(memory_space=pltpu.VMEM),
        scratch_shapes=[pltpu.SemaphoreType.DMA((N_DEV - 1,)), pltpu.SemaphoreType.DMA((N_DEV - 1,))],
        name=name, compiler_params=_cparams(),
    )(blk)


class _Rider:
    def __init__(self, arrays, out_shapes, scratch_shapes, start, finish):
        self.arrays, self.out_shapes, self.scratch_shapes = list(arrays), list(out_shapes), list(scratch_shapes)
        self.start, self.finish = start, finish


def _call(body, *, name, grid, in_specs, out_specs, out_shape, scratch_shapes, args, rider=None, prefetch=()):
    npf = len(prefetch)

    def launch(fn, in_specs, out_specs, out_shape, scratch_shapes, args):
        grid_spec = pltpu.PrefetchScalarGridSpec(num_scalar_prefetch=npf, grid=grid, in_specs=in_specs,
                                                 out_specs=out_specs, scratch_shapes=scratch_shapes)
        return pl.pallas_call(fn, grid_spec=grid_spec, out_shape=out_shape, name=name,
                              compiler_params=_cparams())(*prefetch, *args)

    if rider is None:
        return launch(body, list(in_specs), list(out_specs), list(out_shape), list(scratch_shapes), args)
    ni, no, ns = len(in_specs), len(out_specs), len(scratch_shapes)
    ri, ro = len(rider.arrays), len(rider.out_shapes)
    steps = int(np.prod(grid))

    def wrapped(*refs):
        pf, refs = refs[:npf], refs[npf:]
        h_in, r_in = refs[:ni], refs[ni:ni + ri]
        h_out, r_out = refs[ni + ri:ni + ri + no], refs[ni + ri + no:ni + ri + no + ro]
        h_scr, r_scr = refs[ni + ri + no + ro:ni + ri + no + ro + ns], refs[ni + ri + no + ro + ns:]
        step = pl.program_id(0)
        for d in range(1, len(grid)):
            step = step * grid[d] + pl.program_id(d)

        @pl.when(step == 0)
        def _():
            rider.start(r_in, r_out, r_scr)

        body(*pf, *h_in, *h_out, *h_scr)

        @pl.when(step == steps - 1)
        def _():
            rider.finish(r_in, r_out, r_scr)

    anyspec = pl.BlockSpec(memory_space=pl.ANY)
    res = launch(wrapped, list(in_specs) + [anyspec] * ri, list(out_specs) + [anyspec] * ro,
                 list(out_shape) + rider.out_shapes, list(scratch_shapes) + rider.scratch_shapes,
                 list(args) + rider.arrays)
    return res[:no], res[no:]


def _run_rider(rider, name):
    ri = len(rider.arrays)

    def body(*refs):
        r_in, r_out, r_scr = refs[:ri], refs[ri:ri + len(rider.out_shapes)], refs[ri + len(rider.out_shapes):]
        rider.start(r_in, r_out, r_scr)
        rider.finish(r_in, r_out, r_scr)

    anyspec = pl.BlockSpec(memory_space=pl.ANY)
    return pl.pallas_call(body, in_specs=[anyspec] * ri, out_specs=[anyspec] * len(rider.out_shapes),
                          out_shape=rider.out_shapes, scratch_shapes=rider.scratch_shapes, name=name,
                          compiler_params=_cparams())(*rider.arrays)


def _allgather_rider(blk):
    def copies(ins, outs, scr):
        send_sems, recv_sems, loc_sems, stage = scr
        x, y, cc = _me()
        me = 4 * x + 2 * y + cc
        remote = [pltpu.make_async_remote_copy(
            src_ref=ins[0], dst_ref=outs[0].at[me], send_sem=send_sems.at[k - 1], recv_sem=recv_sems.at[k - 1],
            device_id=(_flip(x, k & 4), _flip(y, k & 2), _flip(cc, k & 1)), device_id_type=MESH) for k in range(1, N_DEV)]
        return remote, pltpu.make_async_copy(ins[0], stage, loc_sems.at[0]), (outs[0].at[me], stage, loc_sems.at[1])

    def start(ins, outs, scr):
        remote, lin, _ = copies(ins, outs, scr)
        lin.start()
        for cp in remote:
            cp.start()

    def finish(ins, outs, scr):
        remote, lin, (dst, stage, sem) = copies(ins, outs, scr)
        lin.wait()
        lout = pltpu.make_async_copy(stage, dst, sem)
        lout.start()
        for cp in remote:
            cp.wait()
        lout.wait()

    return _Rider([blk], [jax.ShapeDtypeStruct((N_DEV,) + blk.shape, blk.dtype)],
                  [pltpu.SemaphoreType.DMA((N_DEV - 1,)), pltpu.SemaphoreType.DMA((N_DEV - 1,)),
                   pltpu.SemaphoreType.DMA((2,)), pltpu.VMEM(blk.shape, blk.dtype)], start, finish)


def _gather_rider(shards):
    n = len(shards)

    def copies(ins, outs, scr, relay=True):
        ici_send, ici_recv, d2d_send, d2d_recv, loc_sems = scr[:5]
        stage = scr[5:]
        x, y, cc = _me()
        chip = 2 * x + y
        sibling = (x, y, 1 - cc)
        local, sends, relays = [], [], []
        for j in range(n):
            def rows(ch, h, j=j):
                return outs[j].at[ch, h]

            lc = pltpu.make_async_copy(ins[j], stage[j], loc_sems.at[j])
            local.append((lc, pltpu.make_async_copy(stage[j], outs[j].at[chip], loc_sems.at[n + j]) if relay else None))
            for k in range(1, N_CHIP):
                px, py = _flip(x, k & 2), _flip(y, k & 1)
                pchip = 2 * px + py
                q = 3 * j + k - 1
                out_cp = pltpu.make_async_remote_copy(src_ref=ins[j].at[cc], dst_ref=rows(chip, cc),
                                                      send_sem=ici_send.at[q], recv_sem=ici_recv.at[q],
                                                      device_id=(px, py, cc), device_id_type=MESH)
                sends.append(out_cp)
                if not relay:
                    continue
                arrival = pltpu.make_async_remote_copy(src_ref=rows(pchip, cc), dst_ref=rows(pchip, cc),
                                                       send_sem=ici_send.at[q], recv_sem=ici_recv.at[q],
                                                       device_id=(px, py, cc), device_id_type=MESH)
                forward = pltpu.make_async_remote_copy(src_ref=rows(pchip, cc), dst_ref=rows(pchip, cc),
                                                       send_sem=d2d_send.at[q], recv_sem=d2d_recv.at[q],
                                                       device_id=sibling, device_id_type=MESH)
                from_sibling = pltpu.make_async_remote_copy(src_ref=rows(pchip, 1 - cc), dst_ref=rows(pchip, 1 - cc),
                                                            send_sem=d2d_send.at[q], recv_sem=d2d_recv.at[q],
                                                            device_id=sibling, device_id_type=MESH)
                relays.append((arrival, forward, from_sibling))
        return local, sends, relays

    def start(ins, outs, scr):
        local, sends, _ = copies(ins, outs, scr, relay=False)
        for lin, _ in local:
            lin.start()
        for cp in sends:
            cp.start()

    def finish(ins, outs, scr):
        local, sends, relays = copies(ins, outs, scr)
        for lin, lout in local:
            lin.wait()
            lout.start()
        for arrival, forward, _ in relays:
            arrival.wait_recv()
            forward.start()
        for cp in sends:
            cp.wait_send()
        for _, forward, from_sibling in relays:
            forward.wait_send()
            from_sibling.wait_recv()
        for _, lout in local:
            lout.wait()

    scratch = [pltpu.SemaphoreType.DMA((3 * n,)), pltpu.SemaphoreType.DMA((3 * n,)), pltpu.SemaphoreType.DMA((3 * n,)),
               pltpu.SemaphoreType.DMA((3 * n,)), pltpu.SemaphoreType.DMA((2 * n,))]
    scratch += [pltpu.VMEM(a.shape, a.dtype) for a in shards]
    return _Rider(shards, [jax.ShapeDtypeStruct((N_CHIP,) + a.shape, a.dtype) for a in shards], scratch, start, finish)


def _sibling_rider(arrs, other_half=False):
    n = len(arrs)

    def copies(ins, outs, scr):
        send_sems, recv_sems = scr
        x, y, cc = _me()
        return [pltpu.make_async_remote_copy(
            src_ref=ins[j].at[1 - cc] if other_half else ins[j], dst_ref=outs[j], send_sem=send_sems.at[j],
            recv_sem=recv_sems.at[j], device_id=(x, y, 1 - cc), device_id_type=MESH) for j in range(n)]

    def start(ins, outs, scr):
        for cp in copies(ins, outs, scr):
            cp.start()

    def finish(ins, outs, scr):
        for cp in copies(ins, outs, scr):
            cp.wait()

    return _Rider(arrs, [jax.ShapeDtypeStruct(a.shape[1:] if other_half else a.shape, a.dtype) for a in arrs],
                  [pltpu.SemaphoreType.DMA((n,)), pltpu.SemaphoreType.DMA((n,))], start, finish)


def _sibling_send(arrs, name, other_half=False):
    return _run_rider(_sibling_rider(arrs, other_half), name)


def _join_riders(first, second):
    ni, no, ns = len(first.arrays), len(first.out_shapes), len(first.scratch_shapes)

    def split(ins, outs, scr):
        return (ins[:ni], outs[:no], scr[:ns]), (ins[ni:], outs[no:], scr[ns:])

    def start(ins, outs, scr):
        a, b = split(ins, outs, scr)
        first.start(*a)
        second.start(*b)

    def finish(ins, outs, scr):
        a, b = split(ins, outs, scr)
        first.finish(*a)
        second.finish(*b)

    return _Rider(first.arrays + second.arrays, first.out_shapes + second.out_shapes,
                  first.scratch_shapes + second.scratch_shapes, start, finish)


def _scatter_rider(arrs):
    n = len(arrs)

    def copies(ins, outs, scr):
        send_sems, recv_sems = scr
        x, y, cc = _me()
        cps = []
        for j in range(n):
            for k in range(1, N_CHIP):
                px, py = _flip(x, k & 2), _flip(y, k & 1)
                cps.append(pltpu.make_async_remote_copy(
                    src_ref=ins[j].at[2 * px + py], dst_ref=outs[j].at[k - 1], send_sem=send_sems.at[3 * j + k - 1],
                    recv_sem=recv_sems.at[3 * j + k - 1], device_id=(px, py, cc), device_id_type=MESH))
        return cps

    def start(ins, outs, scr):
        for cp in copies(ins, outs, scr):
            cp.start()

    def finish(ins, outs, scr):
        for cp in copies(ins, outs, scr):
            cp.wait()

    return _Rider(arrs, [jax.ShapeDtypeStruct((N_CHIP - 1,) + a.shape[1:], a.dtype) for a in arrs],
                  [pltpu.SemaphoreType.DMA((3 * n,)), pltpu.SemaphoreType.DMA((3 * n,))], start, finish)


COL_SHARDED = ("w_in", "w_br_pool", "w_br_attn", "w_br_conv", "w_ff1")
ROW_SHARDED = ("w_o", "w_ff2")
BIG = COL_SHARDED + ROW_SHARDED
SMALL = ("b_ada", "b_gate", "w_pool", "pool_scale", "rel_bias", "conv_w", "conv_b", "conv_ln_g", "conv_ln_b",
         "ln_mix_g", "ln_mix_b", "b_ff1", "b_ff2", "ln_ff_g", "ln_ff_b")
PACK_W = 1024


def _pack(parts):
    rows = []
    for a in parts:
        flat = a.reshape(-1)
        n = -(-flat.shape[0] // PACK_W) * PACK_W
        rows.append(jnp.pad(flat, (0, n - flat.shape[0])).reshape(-1, PACK_W))
    out = jnp.concatenate(rows, axis=0)
    r = -(-out.shape[0] // 8) * 8
    return jnp.pad(out, ((0, r - out.shape[0]), (0, 0)))


def _unpack(packed, shapes):
    out, r0 = [], 0
    for shp in shapes:
        size = int(np.prod(shp))
        nr = -(-size // PACK_W)
        out.append(packed[r0:r0 + nr].reshape(-1)[:size].reshape(shp))
        r0 += nr
    return out


def _hosted(fn, hook, *args, **kw):
    if hook is None:
        return fn(*args, **kw)
    res, rider_out = fn(*args, rider=hook[0], **kw)
    hook[1](rider_out)
    return res


def _layer_fwd(l, x, mod, W, P, hooks=None, u=None):
    hooks = hooks or {}
    s = x.shape[0]
    sh_m, sc_m, g_m, sh_f, sc_f, g_f = [mod[l:l + 1, D_MODEL * j:D_MODEL * (j + 1)] for j in range(6)]
    n = lambda t: f"{t}{l}"
    w_in = W["w_in"][l]
    if u is None:
        u = _ln_mod(x, sc_m, sh_m, n("ln_mod_mix"))
    zp = _mm(u, w_in, "nt", tm=s, tn=256, out_dtype=F32, name=n("z_pool"), b_col0=0, n_out=D_POOL)
    qkv = _mm(u, w_in, "nt", tm=s, tn=256, out_dtype=BF16, name=n("z_qkv"), b_col0=OFF_QKV // 256, n_out=3 * D_ATTN)
    zc = _mm(u, w_in, "nt", tm=s, tn=256, out_dtype=F32, name=n("z_conv"), b_col0=OFF_CONV // 256, n_out=2 * D_CONV)
    zg = _hosted(_mm, hooks.get("z_gate"), u, w_in, "nt", tm=min(2048, s), tn=768, out_dtype=BF16, name=n("z_gate"),
                 b_col0=OFF_GATE // 768, n_out=3 * D_MODEL)

    p, feat_pool = _pool_fwd(zp, P["wp_bd"][l], P["pool_scale"][l], n("pool_fwd"))
    bias = _bias_block(P["rel_bias"][l], n("bias_block"))
    o, probs = _hosted(_attn_fwd, hooks.get("attn"), qkv, bias, n("attn_fwd"))
    cv, feat_conv = _conv_fwd(zc, P["conv_w"][l], P["conv_b"][l], P["conv_ln_g"][l], P["conv_ln_b"][l], n("conv_fwd"))

    branch_w = (W["w_br_pool"][l], W["w_br_attn"][l], W["w_br_conv"][l])
    ys = tuple(_branch_out((feat_pool, o, feat_conv), branch_w, n("branch_out")))
    merged = _merge(zg, P["b_gate"][l], ys, n("merge"))
    mix, x1, u2 = _mm_resid_ln(merged, W["w_o"][l], None, x, g_m, P["ln_mix_g"][l], P["ln_mix_b"][l], n("mix_out"),
                               mod_next=(sc_f, sh_f))

    hpre, hid = _hosted(_ff_hidden, hooks.get("ff1"), u2, W["w_ff1"][l], P["b_ff1"][l], n("ff1"))
    above = None if l + 1 == mod.shape[0] else (mod[l + 1:l + 2, D_MODEL:2 * D_MODEL], mod[l + 1:l + 2, 0:D_MODEL])
    ff, x2, *u_next = _hosted(_mm_resid_ln, hooks.get("ff2"), hid, W["w_ff2"][l], P["b_ff2"][l], x1, g_f,
                              P["ln_ff_g"][l], P["ln_ff_b"][l], n("ff2"), mod_next=above)
    saved = dict(x=x, u=u, zp=zp, qkv=qkv, zc=zc, zg=zg, p=p, feat_pool=feat_pool, probs=probs, o=o, cv=cv,
                 feat_conv=feat_conv, ys=ys, merged=merged, mix=mix, x1=x1, u2=u2, hpre=hpre, hid=hid, ff=ff,
                 u_next=u_next[0] if u_next else None)
    return x2, saved


def _layer_bwd(l, dx2, mod, W, P, A, hooks=None, tgt=None, nxt=None):
    hooks = hooks or {}
    sh_m, sc_m, g_m, sh_f, sc_f, g_f = [mod[l:l + 1, D_MODEL * j:D_MODEL * (j + 1)] for j in range(6)]
    n = lambda t: f"{t}{l}"
    gw, gs = {}, {}

    if isinstance(dx2, tuple):
        dres, dff, gs["ln_ff_g"], gs["ln_ff_b"], dg_f, gs["b_ff2"] = dx2
    else:
        dres, dff, gs["ln_ff_g"], gs["ln_ff_b"], dg_f, gs["b_ff2"], *loss_part = _resid_ln_bwd(
            dx2, A["x1"], A["ff"], g_f, P["ln_ff_g"][l], n("resid_ln_ff_bwd"), tgt=tgt)
    gw["w_ff2"] = _mm(A["hid"], dff, "tn", tm=512, tn=1024, out_dtype=BF16, name=n("dw_ff2"), split_n=512)
    dhpre, gs["b_ff1"] = _ff_hidden_bwd(dff, W["w_ff2"][l], A["hpre"], n("ff_hidden_bwd"))
    gw["w_ff1"] = _mm(dhpre, A["u2"], "tn", tm=512, tn=1024, out_dtype=BF16, name=n("dw_ff1"), split_n=512)

    hook = hooks["du_ff"](gw) if "du_ff" in hooks else None
    dres, dmix, dsc_f, dsh_f, gs["ln_mix_g"], gs["ln_mix_b"], dg_m, _ = _hosted(
        _mm_ln_mod_bwd, hook, dhpre, W["w_ff1"][l], A["x1"], sc_f, dres, n("du_ff"),
        nxt=(A["x"], A["mix"], g_m, P["ln_mix_g"][l]))
    gw["w_o"] = _mm(A["merged"], dmix, "tn", tm=512, tn=1024, out_dtype=BF16, name=n("dw_o"), split_n=512)
    dy_pool, dy_attn, dy_conv, dzg, gs["b_gate"] = _merge_bwd(dmix, W["w_o"][l], A["zg"], P["b_gate"][l], A["ys"],
                                                              n("merge_bwd"))

    dys = (dy_pool, dy_attn, dy_conv)
    gw["w_br_pool"], gw["w_br_attn"], gw["w_br_conv"] = _branch_dw(
        dys, (A["feat_pool"], A["o"], A["feat_conv"]), n("dw_branch"))
    dfeat_pool, do, dfeat_conv = _branch_in_bwd(
        dys, (W["w_br_pool"][l], W["w_br_attn"][l], W["w_br_conv"][l]), (F32, BF16, F32), n("d_branch_in"))

    dzp, dwp_bd, gs["pool_scale"] = _pool_bwd(dfeat_pool, A["p"], P["wp_bd"][l], P["pool_scale"][l], n("pool_bwd"))
    gs["w_pool"] = jnp.stack([dwp_bd[POOL_GROUP * g:POOL_GROUP * (g + 1), POOL_GROUP * g:POOL_GROUP * (g + 1)]
                              for g in range(len(POOL_WINDOWS))])
    hook = hooks["attn"](gw) if "attn" in hooks else None
    dq, dk, dv, ds_acc = _hosted(_attn_bwd, hook, A["qkv"], do, A["probs"], n("attn_bwd"))
    gs["rel_bias"] = _bias_block_bwd(ds_acc, n("bias_block_bwd"))
    dzc, dcw, gs["conv_b"], gs["conv_ln_g"], gs["conv_ln_b"] = _conv_bwd(
        dfeat_conv, A["cv"], A["zc"], P["conv_w"][l], P["conv_ln_g"][l], P["conv_ln_b"][l], n("conv_bwd"))
    gs["conv_w"] = dcw[:CONV_WIDTH]

    dz = [dzp, dq, dk, dv, dzc, dzg]
    gw["w_in"] = _dw_segments(dz, A["u"], n("dw_in"))
    hook = hooks["du_mix"](gw) if "du_mix" in hooks else None
    res = _hosted(_mm_ln_mod_bwd, hook, dz, W["w_in"][l], A["x"], sc_m, dres, n("du_mix"), nxt=nxt)
    if nxt is None:
        dx, dsc_m, dsh_m = res
    else:
        dx, dsc_m, dsh_m = (res[0], res[1], *res[4:]), res[2], res[3]
    dmod = jnp.concatenate([dsh_m, dsc_m, dg_m, dsh_f, dsc_f, dg_f], axis=1)
    return (dx, gw, gs, dmod) if tgt is None else (dx, gw, gs, dmod, loss_part[0])


def _small_shapes():
    return {"b_ada": (6 * D_MODEL,), "b_gate": (3 * D_MODEL,), "w_pool": (4, POOL_GROUP, POOL_GROUP),
            "pool_scale": (D_POOL,), "rel_bias": (N_HEADS, N_REL), "conv_w": (CONV_WIDTH, D_CONV),
            "conv_b": (D_CONV,), "conv_ln_g": (D_CONV,), "conv_ln_b": (D_CONV,), "ln_mix_g": (D_MODEL,),
            "ln_mix_b": (D_MODEL,), "b_ff1": (D_FF,), "b_ff2": (D_MODEL,), "ln_ff_g": (D_MODEL,), "ln_ff_b": (D_MODEL,)}


def kernel(x, c, w_ada, b_ada, w_in, b_gate, w_pool, pool_scale, rel_bias, conv_w, conv_b, conv_ln_g, conv_ln_b, w_br_pool, w_br_attn, w_br_conv, w_o, ln_mix_g, ln_mix_b, w_ff1, b_ff1, w_ff2, b_ff2, ln_ff_g, ln_ff_b, loss_target, m_w_ada, m_b_ada, m_w_in, m_b_gate, m_w_pool, m_pool_scale, m_rel_bias, m_conv_w, m_conv_b, m_conv_ln_g, m_conv_ln_b, m_w_br_pool, m_w_br_attn, m_w_br_conv, m_w_o, m_ln_mix_g, m_ln_mix_b, m_w_ff1, m_b_ff1, m_w_ff2, m_b_ff2, m_ln_ff_g, m_ln_ff_b, v_w_ada, v_b_ada, v_w_in, v_b_gate, v_w_pool, v_pool_scale, v_rel_bias, v_conv_w, v_conv_b, v_conv_ln_g, v_conv_ln_b, v_w_br_pool, v_w_br_attn, v_w_br_conv, v_w_o, v_ln_mix_g, v_ln_mix_b, v_w_ff1, v_b_ff1, v_w_ff2, v_b_ff2, v_ln_ff_g, v_ln_ff_b):
    env = dict(locals())
    xi, yi, ci = _me()
    chip = 2 * xi + yi
    me = 4 * xi + 2 * yi + ci
    xs = x[0]
    tgt = loss_target[0]
    L = DEPTH

    first = _allgather_small(jnp.concatenate([c.reshape(8, 128), _pack([conv_w]).reshape(-1, 128)]), "gather_c_conv_w")
    c_all = first[:, :8].reshape(N_DEV, D_MODEL)
    ada_cols = w_ada.shape[2]
    b_ada_sh = lax.dynamic_slice_in_dim(b_ada, chip * ada_cols, ada_cols, axis=1).reshape(L, 1, ada_cols)
    mod_part = _mod_fwd(c_all, w_ada, b_ada_sh, "mod_fwd")
    mod_g = _allgather_small(mod_part.reshape(-1, 128), "gather_mod").reshape(N_CHIP, 2, L, N_DEV, ada_cols)[:, 0]
    mod_all = jnp.transpose(mod_g, (1, 2, 0, 3)).reshape(L, N_DEV, 6 * D_MODEL)
    mod = lax.dynamic_index_in_dim(mod_all, me, axis=1, keepdims=False)

    W = {k: [None] * L for k in BIG}

    def weight_gather(*items):
        shards = [(jnp.swapaxes(env[k][l], 0, 1) if k in COL_SHARDED else env[k][l]).astype(BF16) for k, l in items]
        shards = [a.reshape(2, a.shape[0] // 2, a.shape[1]) for a in shards]

        def done(outs):
            for (k, l), g in zip(items, outs):
                W[k][l] = g.reshape(-1, g.shape[-1])

        return _gather_rider(shards), done

    branch = lambda l: [(k, l) for k in ("w_br_pool", "w_br_attn", "w_br_conv", "w_o")]
    rider, done = weight_gather(("w_in", 0))
    done(_run_rider(rider, "gather_w_in0"))
    fwd_hooks = [{"z_gate": weight_gather(*branch(0)), "attn": weight_gather(("w_ff1", 0), ("w_ff2", 0)),
                  "ff1": weight_gather(("w_in", 1)), "ff2": weight_gather(*branch(1))},
                 {"attn": weight_gather(("w_ff1", 1), ("w_ff2", 1))}]

    P = {k: env[k] for k in ("rel_bias", "conv_w")}
    for k in ("b_gate", "pool_scale", "conv_b", "conv_ln_g", "conv_ln_b", "ln_mix_g", "ln_mix_b", "b_ff1", "b_ff2",
              "ln_ff_g", "ln_ff_b"):
        P[k] = env[k].reshape(L, 1, -1)
    n_cw = conv_w.size
    cw = first[:, 8:].reshape(N_CHIP, 2, -1)[:, 0, :n_cw].reshape(N_CHIP, L, CONV_WIDTH, D_CONV // N_CHIP)
    P["conv_w"] = jnp.transpose(cw, (1, 2, 0, 3)).reshape(L, CONV_WIDTH, D_CONV)
    wp_bd = jnp.zeros((L, D_POOL, D_POOL), F32)
    for g in range(len(POOL_WINDOWS)):
        sl = slice(POOL_GROUP * g, POOL_GROUP * (g + 1))
        wp_bd = wp_bd.at[:, sl, sl].set(w_pool[:, g])
    P["wp_bd"] = wp_bd.astype(BF16)

    acts = []
    h = xs
    for l in range(L):
        h, saved = _layer_fwd(l, h, mod, W, P, fwd_hooks[l], u=acts[-1]["u_next"] if acts else None)
        acts.append(saved)

    place = jnp.stack([ci, chip, chip ^ 1, chip ^ 2, chip ^ 3]).astype(jnp.int32)
    scattered = {}

    def grad_scatter(items, tag):
        dws = [dw for _, _, dw in items]
        got = _sibling_send(dws, f"swap_blocks_{tag}", other_half=True)
        both = [_sum_cores(a, b, place, f"sum_cores_{k}{l}") for (k, l, _), a, b in zip(items, dws, got)]
        both = [hh.reshape(N_CHIP, -1, hh.shape[-1]) for hh in both]

        def done(outs):
            for (k, l, _), hh, r in zip(items, both, outs):
                scattered[(k, l)] = (hh, r)

        return _scatter_rider(both), done

    def scatter_hook(names, l, host):
        return lambda gw: grad_scatter([(k, l, gw[k]) for k in names], f"{host}{l}")

    gws, gss, dmods = [None] * L, [None] * L, [None] * L
    dh = h
    for l in reversed(range(L)):
        hooks = {"du_ff": scatter_hook(("w_ff2",), l, "du_ff"),
                 "attn": scatter_hook(("w_ff1", "w_o", "w_br_pool", "w_br_attn", "w_br_conv"), l, "attn_bwd"),
                 "du_mix": scatter_hook(("w_in",), l, "du_mix")}
        below = None
        if l > 0:
            below = (acts[l - 1]["x1"], acts[l - 1]["ff"], mod[l - 1:l, 5 * D_MODEL:], P["ln_ff_g"][l - 1])
        if l == L - 1:
            dh, gws[l], gss[l], dmods[l], loss_part = _layer_bwd(l, dh, mod, W, P, acts[l], hooks, tgt=tgt, nxt=below)
        else:
            dh, gws[l], gss[l], dmods[l] = _layer_bwd(l, dh, mod, W, P, acts[l], hooks, nxt=below)
    grad_x = dh[None]

    reduced = [[_sum_chips(*scattered[(k, l)], place, f"sum_chips_{k}{l}") for l in range(L)] for k in BIG]
    flat_reduced = [t for per_weight in reduced for t in per_weight]

    shapes = _small_shapes()
    small_names = [k for k in SMALL if k != "b_ada"]
    dmod_own = jnp.concatenate(dmods, axis=0)
    pack = _pack([dmod_own] + [jnp.stack([gss[l][k].reshape(shapes[k]) for l in range(L)]) for k in small_names]
                 + [loss_part])
    last = _run_rider(_join_riders(_sibling_rider(flat_reduced), _allgather_rider(pack.reshape(-1, 128))),
                      "swap_reduced_gather_small")
    flat_other, g_all = last[:-1], last[-1].reshape(N_DEV, -1, PACK_W)

    out = {}
    for j, k in enumerate(BIG):
        own, other = reduced[j], flat_other[L * j:L * (j + 1)]
        if k == "w_in":
            t = lambda a: jnp.swapaxes(a, 1, 2)
            res = _adamw_halves(t(env[k]), t(env["m_" + k]), t(env["v_" + k]), own, other, place, "cols", f"adamw_{k}")
            res = [t(a) for a in res]
        else:
            if k in COL_SHARDED:
                own, other = [a.T for a in own], [a.T for a in other]
            res = _adamw_halves(env[k], env["m_" + k], env["v_" + k], own, other, place,
                                "rows" if k in COL_SHARDED else "cols", f"adamw_{k}")
        out[k] = tuple(res)

    dmod_all = g_all[:, :L * 6].reshape(N_DEV, L, 6 * D_MODEL)
    dmod_sh = jnp.transpose(lax.dynamic_slice_in_dim(dmod_all, chip * ada_cols, ada_cols, axis=2), (1, 0, 2))
    g_ada = _mod_bwd(c_all, dmod_sh, "mod_bwd")
    g_, d_, m_, v_ = _adamw(w_ada.reshape(-1, ada_cols), m_w_ada.reshape(-1, ada_cols), v_w_ada.reshape(-1, ada_cols),
                            [g_ada.reshape(-1, ada_cols)], "adamw_w_ada")
    out["w_ada"] = tuple(a.reshape(w_ada.shape) for a in (g_, d_, m_, v_))

    def small_pack(prefix):
        parts = [env[prefix + "b_ada"]]
        for k in small_names:
            a = env[prefix + k]
            if k == "conv_w":
                a = jnp.zeros((L,) + shapes[k], F32)
            parts.append(a)
        return _pack(parts + [jnp.zeros_like(loss_part)])

    gp, dp, mp, vp = _adamw_small(small_pack(""), small_pack("m_"), small_pack("v_"), g_all, "adamw_small")
    full_shapes = [(L,) + shapes["b_ada"]] + [(L,) + shapes[k] for k in small_names]
    loss = _unpack(gp, full_shapes + [(128,)])[-1][0]
    for tag, packed in (("g", gp), ("d", dp), ("m", mp), ("v", vp)):
        for k, a in zip(["b_ada"] + small_names, _unpack(packed, full_shapes)):
            out.setdefault(k, {})
            out[k][tag] = a
    g_cw_full = out["conv_w"]["g"]
    cw_cols = D_CONV // N_CHIP
    g_cw = lax.dynamic_slice_in_dim(g_cw_full, chip * cw_cols, cw_cols, axis=2)
    pad_rows = lambda a: jnp.pad(a.reshape(L * CONV_WIDTH, cw_cols), ((0, 2), (0, 0)))
    g_, d_, m_, v_ = _adamw(pad_rows(conv_w), pad_rows(m_conv_w), pad_rows(v_conv_w), [pad_rows(g_cw)], "adamw_conv_w")
    out["conv_w"] = tuple(a[:L * CONV_WIDTH].reshape(L, CONV_WIDTH, cw_cols) for a in (g_, d_, m_, v_))

    names = ["w_ada", "b_ada", "w_in", "b_gate", "w_pool", "pool_scale", "rel_bias", "conv_w", "conv_b", "conv_ln_g",
             "conv_ln_b", "w_br_pool", "w_br_attn", "w_br_conv", "w_o", "ln_mix_g", "ln_mix_b", "w_ff1", "b_ff1",
             "w_ff2", "b_ff2", "ln_ff_g", "ln_ff_b"]

    def pick(k, i):
        o = out[k]
        return o[i] if isinstance(o, tuple) else o["gdmv"[i]].reshape(env[k].shape)

    return (loss, grad_x, *[pick(k, 0) for k in names], *[pick(k, 1) for k in names],
            *[pick(k, 2) for k in names], *[pick(k, 3) for k in names])
```

```python
import jax
import jax.numpy as jnp
import numpy as np
from jax import lax
from jax.experimental import pallas as pl
from jax.experimental.pallas import tpu as pltpu

F32 = jnp.float32
BF16 = jnp.bfloat16

D_MODEL = 1024
DEPTH = 2
CHUNK = 64
POOL_WINDOWS = (2, 4, 8, 16)
POOL_GROUP = 64
D_POOL = 256
N_HEADS = 8
HEAD_DIM = 64
D_ATTN = 512
N_PREV_CHUNKS = 8
REL_CLIP = 128
N_REL = 2 * REL_CLIP + 1
D_CONV = 256
CONV_WIDTH = 31
D_FF = 4 * D_MODEL
D_IN = 5376
OFF_POOL, OFF_QKV, OFF_CONV, OFF_GATE = 0, 256, 1792, 2304
ALPHA = (2.0 * DEPTH) ** 0.25
LN_EPS = 1e-5
NEG_INF = -1e30
ADAM_LR, ADAM_B1, ADAM_B2, ADAM_EPS, ADAM_WD, ADAM_STEP = 0.001, 0.9, 0.999, 1e-08, 0.01, 10

N_DEV = 8
N_CHIP = 4
MESH = pl.DeviceIdType.MESH

QB = 2 * CHUNK
KPAD = N_PREV_CHUNKS * CHUNK
KW = QB + KPAD
SKEW_W = 768

VMEM_LIMIT = 56 * 1024 * 1024


def _cparams(**kw):
    return pltpu.CompilerParams(vmem_limit_bytes=VMEM_LIMIT, **kw)


def _full(shape):
    n = len(shape)
    return pl.BlockSpec(shape, lambda *_: (0,) * n)


_DIMS = {"nn": (((1,), (0,)), ((), ())), "nt": (((1,), (1,)), ((), ())), "tn": (((0,), (0,)), ((), ()))}


def _relu2(t):
    r = jnp.maximum(t, 0.0)
    return r * r


def _mm(a, b, mode, *, tm, tn, out_dtype, name, b_col0=0, n_out=None, bias=None, split_n=0, rider=None):
    if mode == "tn":
        k, m = a.shape
        n = b.shape[1] if n_out is None else n_out
        a_spec = pl.BlockSpec((k, tm), lambda i, j: (0, i))
        b_spec = pl.BlockSpec((k, tn), lambda i, j: (0, j + b_col0))
    elif mode == "nn":
        m, k = a.shape
        n = b.shape[1] if n_out is None else n_out
        a_spec = pl.BlockSpec((tm, k), lambda i, j: (i, 0))
        b_spec = pl.BlockSpec((k, tn), lambda i, j: (0, j + b_col0))
    else:
        m, k = a.shape
        n = b.shape[0] if n_out is None else n_out
        a_spec = pl.BlockSpec((tm, k), lambda i, j: (i, 0))
        b_spec = pl.BlockSpec((tn, k), lambda i, j: (j + b_col0, 0))
    assert m % tm == 0 and n % tn == 0, (name, m, n, tm, tn)
    dims = _DIMS[mode]

    def body(*refs):
        if bias is None:
            a_ref, b_ref, o_ref = refs
        else:
            a_ref, b_ref, bias_ref, o_ref = refs
        acc = lax.dot_general(a_ref[...].astype(BF16), b_ref[...].astype(BF16), dims, preferred_element_type=F32)
        if bias is not None:
            acc = acc + bias_ref[...]
        if split_n:
            for c in range(tn // split_n):
                o_ref[c] = acc[:, c * split_n:(c + 1) * split_n].astype(out_dtype)
        else:
            o_ref[...] = acc.astype(out_dtype)

    in_specs = [a_spec, b_spec]
    args = [a, b]
    if bias is not None:
        in_specs.append(pl.BlockSpec((1, tn), lambda i, j: (0, j)))
        args.append(bias)
    if split_n:
        out_spec = pl.BlockSpec((tn // split_n, tm, split_n), lambda i, j: (j, i, 0))
        out_shape = jax.ShapeDtypeStruct((n // split_n, m, split_n), out_dtype)
    else:
        out_spec = pl.BlockSpec((tm, tn), lambda i, j: (i, j))
        out_shape = jax.ShapeDtypeStruct((m, n), out_dtype)
    res = _call(body, name=name, grid=(m // tm, n // tn), in_specs=in_specs, out_specs=[out_spec],
                out_shape=[out_shape], scratch_shapes=[], args=args, rider=rider)
    return res[0] if rider is None else (res[0][0], res[1])


def _ln_hat(x):
    mu = jnp.mean(x, axis=-1, keepdims=True)
    xc = x - mu
    var = jnp.mean(xc * xc, axis=-1, keepdims=True)
    rstd = lax.rsqrt(var + LN_EPS)
    return xc * rstd, rstd


def _ln_hat_bwd(dhat, xhat, rstd):
    m1 = jnp.mean(dhat, axis=-1, keepdims=True)
    m2 = jnp.mean(dhat * xhat, axis=-1, keepdims=True)
    return rstd * (dhat - m1 - xhat * m2)


def _row_tile(s):
    return min(512, s)


def _acc_rows(ref, val, first):
    @pl.when(first)
    def _():
        ref[...] = jnp.zeros_like(ref)
    ref[...] += jnp.sum(val, axis=0, keepdims=True)


def _ln_mod(x, sc, sh, name):
    s, d = x.shape
    tm = _row_tile(s)

    def body(x_ref, sc_ref, sh_ref, u_ref):
        xhat, _ = _ln_hat(x_ref[...])
        u_ref[...] = (xhat * (1.0 + sc_ref[...]) + sh_ref[...]).astype(BF16)

    row = pl.BlockSpec((tm, d), lambda i: (i, 0))
    vec = pl.BlockSpec((1, d), lambda i: (0, 0))
    return pl.pallas_call(body, grid=(s // tm,), in_specs=[row, vec, vec], out_specs=row,
                          out_shape=jax.ShapeDtypeStruct((s, d), BF16), name=name, compiler_params=_cparams())(x, sc, sh)


def _resid_bwd_tile(dxo, x, f, g, gam):
    rhat, rstd = _ln_hat(ALPHA * x + g * f)
    dr = _ln_hat_bwd(dxo * gam, rhat, rstd)
    return ALPHA * dr, g * dr, dxo * rhat, dr * f


def _mm_ln_mod_bwd(a, b, x, sc, dres, name, rider=None, nxt=None):
    segs = list(a) if isinstance(a, (list, tuple)) else [a]
    s = segs[0].shape[0]
    k, d = b.shape
    assert sum(t.shape[1] for t in segs) == k
    tm = min(512 if k <= 4096 and nxt is None else 256, s)
    ns = len(segs)

    def body(*refs):
        seg_refs = refs[:ns]
        if nxt is None:
            b_ref, x_ref, sc_ref, dres_ref, dx_ref, dsc_ref, dsh_ref = refs[ns:]
        else:
            (b_ref, x_ref, sc_ref, dres_ref, xp_ref, fp_ref, gp_ref, gamp_ref,
             dresp_ref, dfp_ref, dsc_ref, dsh_ref, dgam_ref, dbet_ref, dg_ref, dbias_ref) = refs[ns:]
        first = pl.program_id(0) == 0
        duv, r0 = None, 0
        for seg_ref in seg_refs:
            w = seg_ref.shape[1]
            part = jnp.dot(seg_ref[...], b_ref[r0:r0 + w, :], preferred_element_type=F32)
            duv = part if duv is None else duv + part
            r0 += w
        xhat, rstd = _ln_hat(x_ref[...])
        dxv = dres_ref[...] + _ln_hat_bwd(duv * (1.0 + sc_ref[...]), xhat, rstd)
        _acc_rows(dsc_ref, duv * xhat, first)
        _acc_rows(dsh_ref, duv, first)
        if nxt is None:
            dx_ref[...] = dxv
        else:
            dresp, dfp, t_gam, t_g = _resid_bwd_tile(dxv, xp_ref[...], fp_ref[...], gp_ref[...], gamp_ref[...])
            dresp_ref[...] = dresp
            dfp_ref[...] = dfp.astype(BF16)
            _acc_rows(dgam_ref, t_gam, first)
            _acc_rows(dbet_ref, dxv, first)
            _acc_rows(dg_ref, t_g, first)
            _acc_rows(dbias_ref, dfp, first)

    row = pl.BlockSpec((tm, d), lambda i: (i, 0))
    vec = pl.BlockSpec((1, d), lambda i: (0, 0))
    vs = jax.ShapeDtypeStruct((1, d), F32)
    rows = jax.ShapeDtypeStruct((s, d), F32)
    in_specs = [pl.BlockSpec((tm, t.shape[1]), lambda i: (i, 0)) for t in segs] + [_full((k, d)), row, vec, row]
    args = (*segs, b, x, sc, dres)
    if nxt is None:
        out_specs, out_shape = [row, vec, vec], [rows, vs, vs]
    else:
        in_specs += [row, row, vec, vec]
        args += tuple(nxt)
        out_specs = [row, row] + [vec] * 6
        out_shape = [rows, jax.ShapeDtypeStruct((s, d), BF16)] + [vs] * 6
    res = _call(body, name=name, grid=(s // tm,), in_specs=in_specs, out_specs=out_specs, out_shape=out_shape,
                scratch_shapes=[], args=args, rider=rider)
    return tuple(res) if rider is None else (tuple(res[0]), res[1])


def _dw_segments(segs, u, name):
    s, d = u.shape
    tw = 256
    tiles = [t.shape[1] // tw for t in segs]
    starts = [sum(tiles[:j]) for j in range(len(segs))]
    ns = len(segs)

    def body(*refs):
        seg_refs, u_ref, o_ref = refs[:ns], refs[ns], refs[ns + 1]
        i = pl.program_id(0)
        for seg_ref, t0, nt in zip(seg_refs, starts, tiles):
            @pl.when((i >= t0) & (i < t0 + nt))
            def _(seg_ref=seg_ref):
                acc = lax.dot_general(seg_ref[...], u_ref[...], _DIMS["tn"], preferred_element_type=F32)
                o_ref[0] = acc[:, :d // 2].astype(BF16)
                o_ref[1] = acc[:, d // 2:].astype(BF16)

    def seg_spec(t0, nt):
        return pl.BlockSpec((s, tw), lambda i: (0, jnp.clip(i - t0, 0, nt - 1)))

    return pl.pallas_call(
        body, grid=(sum(tiles),), in_specs=[seg_spec(t0, nt) for t0, nt in zip(starts, tiles)] + [_full((s, d))],
        out_specs=pl.BlockSpec((2, tw, d // 2), lambda i: (0, i, 0)),
        out_shape=jax.ShapeDtypeStruct((2, sum(tiles) * tw, d // 2), BF16), name=name, compiler_params=_cparams(),
    )(*segs, u)


def _mm_resid_ln(a, b, bias, x, g, gam, bet, name, rider=None, mod_next=None):
    s, k = a.shape
    d = b.shape[1]
    tm = min(512, s)
    nb, nm = int(bias is not None), 2 * int(mod_next is not None)

    def body(*refs):
        a_ref, b_ref = refs[:2]
        x_ref, g_ref, gam_ref, bet_ref = refs[2 + nb:6 + nb]
        f_ref, o_ref = refs[6 + nb + nm:8 + nb + nm]
        f = jnp.dot(a_ref[...], b_ref[...], preferred_element_type=F32)
        if bias is not None:
            f = f + refs[2][...]
        f_ref[...] = f
        rhat, _ = _ln_hat(ALPHA * x_ref[...] + g_ref[...] * f)
        y = rhat * gam_ref[...] + bet_ref[...]
        o_ref[...] = y
        if mod_next is not None:
            sc_ref, sh_ref = refs[6 + nb:8 + nb]
            yhat, _ = _ln_hat(y)
            refs[8 + nb + nm][...] = (yhat * (1.0 + sc_ref[...]) + sh_ref[...]).astype(BF16)

    row = pl.BlockSpec((tm, d), lambda i: (i, 0))
    vec = pl.BlockSpec((1, d), lambda i: (0, 0))
    in_specs = [pl.BlockSpec((tm, k), lambda i: (i, 0)), _full((k, d))] + [vec] * nb + [row, vec, vec, vec] + [vec] * nm
    args = [a, b] + ([bias] if nb else []) + [x, g, gam, bet] + (list(mod_next) if nm else [])
    sh = jax.ShapeDtypeStruct((s, d), F32)
    out_specs, out_shape = [row, row], [sh, sh]
    if nm:
        out_specs, out_shape = out_specs + [row], out_shape + [jax.ShapeDtypeStruct((s, d), BF16)]
    res = _call(body, name=name, grid=(s // tm,), in_specs=in_specs, out_specs=out_specs, out_shape=out_shape,
                scratch_shapes=[], args=args, rider=rider)
    return tuple(res) if rider is None else (tuple(res[0]), res[1])


def _resid_ln_bwd(dxo, x, f, g, gam, name, tgt=None):
    s, d = x.shape
    tm = _row_tile(s)
    n = s // tm

    def body(*refs):
        if tgt is None:
            dxo_ref, x_ref, f_ref, g_ref, gam_ref, dres_ref, df_ref, dgam_ref, dbet_ref, dg_ref, dbias_ref = refs
            dxov = dxo_ref[...]
        else:
            (dxo_ref, t_ref, x_ref, f_ref, g_ref, gam_ref, dres_ref, df_ref, dgam_ref, dbet_ref, dg_ref, dbias_ref,
             loss_ref, sq_ref) = refs
            err = dxo_ref[...] - t_ref[...]
            dxov = err * (1.0 / d)
            _acc_rows(sq_ref, err * err, pl.program_id(0) == 0)

            @pl.when(pl.program_id(0) == n - 1)
            def _():
                tot = jnp.sum(sq_ref[...], axis=1, keepdims=True) * (0.5 / d)
                loss_ref[...] = jnp.broadcast_to(tot, (1, 128))

        first = pl.program_id(0) == 0
        dres, dfv, t_gam, t_g = _resid_bwd_tile(dxov, x_ref[...], f_ref[...], g_ref[...], gam_ref[...])
        dres_ref[...] = dres
        df_ref[...] = dfv.astype(BF16)
        _acc_rows(dgam_ref, t_gam, first)
        _acc_rows(dbet_ref, dxov, first)
        _acc_rows(dg_ref, t_g, first)
        _acc_rows(dbias_ref, dfv, first)

    row = pl.BlockSpec((tm, d), lambda i: (i, 0))
    vec = pl.BlockSpec((1, d), lambda i: (0, 0))
    vs = jax.ShapeDtypeStruct((1, d), F32)
    out_specs = [row, row, vec, vec, vec, vec]
    out_shape = [jax.ShapeDtypeStruct((s, d), F32), jax.ShapeDtypeStruct((s, d), BF16), vs, vs, vs, vs]
    if tgt is None:
        return pl.pallas_call(body, grid=(n,), in_specs=[row, row, row, vec, vec], out_specs=out_specs,
                              out_shape=out_shape, name=name, compiler_params=_cparams())(dxo, x, f, g, gam)
    return pl.pallas_call(body, grid=(n,), in_specs=[row, row, row, row, vec, vec],
                          out_specs=out_specs + [pl.BlockSpec((1, 128), lambda i: (0, 0))],
                          out_shape=out_shape + [jax.ShapeDtypeStruct((1, 128), F32)],
                          scratch_shapes=[pltpu.VMEM((1, d), F32)], name=name,
                          compiler_params=_cparams())(dxo, tgt, x, f, g, gam)


POOL_HALO = 16
POOL_ROWS = 256


def _pool_counts(r0, rows):
    t1 = (lax.broadcasted_iota(jnp.int32, (rows, 128), 0) + r0 + 1).astype(F32)
    low = lax.broadcasted_iota(jnp.int32, (rows, 128), 1) < POOL_GROUP
    wa = jnp.where(low, float(POOL_WINDOWS[0]), float(POOL_WINDOWS[1]))
    wb = jnp.where(low, float(POOL_WINDOWS[2]), float(POOL_WINDOWS[3]))
    return jnp.minimum(t1, wa), jnp.minimum(t1, wb), low


def _window_sums(win, off, rows, sign):
    def sl(j, half):
        return win[off + sign * j: off + sign * j + rows, 128 * half:128 * half + 128]
    a2 = sl(0, 0) + sl(1, 0)
    a4 = a2 + sl(2, 0) + sl(3, 0)
    a8 = sl(0, 1)
    for j in range(1, 8):
        a8 = a8 + sl(j, 1)
    a16 = a8
    for j in range(8, 16):
        a16 = a16 + sl(j, 1)
    return a2, a4, a8, a16


def _pool_fwd(zp, wp_bd, pscale, name):
    s = zp.shape[0]
    r = min(POOL_ROWS, s)

    def body(z_ref, wp_ref, sc_ref, p_ref, feat_ref, pad):
        pad[0:POOL_HALO, :] = jnp.zeros((POOL_HALO, D_POOL), F32)
        pad[POOL_HALO:, :] = z_ref[...]

        def step(i, carry):
            r0 = pl.multiple_of(i * r, r)
            win = pad[pl.ds(r0, r + POOL_HALO), :]
            a2, a4, a8, a16 = _window_sums(win, POOL_HALO, r, -1)
            ca, cb, low = _pool_counts(r0, r)
            x0 = win[POOL_HALO:, :]
            pa = jnp.where(low, a2, a4) / ca
            pb = jnp.where(low, a8, a16) / cb
            p = (jnp.concatenate([pa, pb], axis=1) - x0).astype(BF16)
            p_ref[pl.ds(r0, r), :] = p
            pw = jnp.dot(p, wp_ref[...], preferred_element_type=F32)
            feat_ref[pl.ds(r0, r), :] = (pw * sc_ref[...]).astype(BF16)
            return carry

        lax.fori_loop(0, s // r, step, 0)

    return pl.pallas_call(
        body, out_shape=[jax.ShapeDtypeStruct((s, D_POOL), BF16), jax.ShapeDtypeStruct((s, D_POOL), BF16)],
        scratch_shapes=[pltpu.VMEM((s + POOL_HALO, D_POOL), F32)], name=name, compiler_params=_cparams(),
    )(zp, wp_bd, pscale)


def _pool_bwd(dfeat, p, wp_bd, pscale, name):
    s = p.shape[0]
    r = min(POOL_ROWS, s)

    def body(df_ref, p_ref, wp_ref, sc_ref, dz_ref, dwp_ref, dsc_ref, gpad, dpbuf):
        dwp_ref[...] = jnp.zeros_like(dwp_ref)
        dsc_ref[...] = jnp.zeros_like(dsc_ref)
        gpad[s:, :] = jnp.zeros((POOL_HALO, D_POOL), F32)

        def step1(i, carry):
            r0 = pl.multiple_of(i * r, r)
            pv = p_ref[pl.ds(r0, r), :]
            dfv = df_ref[pl.ds(r0, r), :]
            pw = jnp.dot(pv, wp_ref[...], preferred_element_type=F32)
            dsc_ref[...] += jnp.sum(dfv * pw, axis=0, keepdims=True)
            dpw = (dfv * sc_ref[...]).astype(BF16)
            dwp_ref[...] += lax.dot_general(pv, dpw, _DIMS["tn"], preferred_element_type=F32)
            dp = lax.dot_general(dpw, wp_ref[...], _DIMS["nt"], preferred_element_type=F32)
            ca, cb, _ = _pool_counts(r0, r)
            gpad[pl.ds(r0, r), :] = dp / jnp.concatenate([ca, cb], axis=1)
            dpbuf[pl.ds(r0, r), :] = dp
            return carry

        lax.fori_loop(0, s // r, step1, 0)

        def step2(i, carry):
            r0 = pl.multiple_of(i * r, r)
            win = gpad[pl.ds(r0, r + POOL_HALO), :]
            a2, a4, a8, a16 = _window_sums(win, 0, r, 1)
            low = lax.broadcasted_iota(jnp.int32, (r, 128), 1) < POOL_GROUP
            acc = jnp.concatenate([jnp.where(low, a2, a4), jnp.where(low, a8, a16)], axis=1)
            dz_ref[pl.ds(r0, r), :] = (acc - dpbuf[pl.ds(r0, r), :]).astype(BF16)
            return carry

        lax.fori_loop(0, s // r, step2, 0)

    return pl.pallas_call(
        body,
        out_shape=[jax.ShapeDtypeStruct((s, D_POOL), BF16), jax.ShapeDtypeStruct((D_POOL, D_POOL), F32),
                   jax.ShapeDtypeStruct((1, D_POOL), F32)],
        scratch_shapes=[pltpu.VMEM((s + POOL_HALO, D_POOL), F32), pltpu.VMEM((s, D_POOL), F32)],
        name=name, compiler_params=_cparams(),
    )(dfeat, p, wp_bd, pscale)


def _skew_index():
    cp = lax.broadcasted_iota(jnp.int32, (SKEW_W, N_REL), 0)
    dist = jnp.where(cp < KW, KPAD - cp, KPAD + SKEW_W - cp)
    idx = jnp.clip(dist, -REL_CLIP, REL_CLIP) + REL_CLIP
    return (idx == lax.broadcasted_iota(jnp.int32, (SKEW_W, N_REL), 1)).astype(F32)


def _row_bits(b):
    return (lax.broadcasted_iota(jnp.int32, (QB, SKEW_W), 0) >> b) & 1 == 1


N_EDGE = KPAD // QB


def _bias_block(rel_bias, name):
    def body(rb_ref, o_ref):
        onehot = _skew_index()
        row0 = lax.dot_general(rb_ref[...], onehot, _DIMS["nt"], precision=lax.Precision.HIGHEST,
                               preferred_element_type=F32)
        r = lax.broadcasted_iota(jnp.int32, (QB, KW), 0)
        kk = lax.broadcasted_iota(jnp.int32, (QB, KW), 1)
        cq, ck = r // CHUNK, kk // CHUNK
        band = (ck >= cq) & (ck <= cq + N_PREV_CHUNKS)
        for h in range(N_HEADS):
            t = jnp.broadcast_to(row0[h:h + 1, :], (QB, SKEW_W))
            for b in range(7):
                t = jnp.where(_row_bits(b), pltpu.roll(t, 1 << b, 1), t)
            for e in range(N_EDGE + 1):
                o_ref[e, h] = jnp.where(band & (kk >= KPAD - e * QB), t[:, :KW], NEG_INF)

    return pl.pallas_call(body, out_shape=jax.ShapeDtypeStruct((N_EDGE + 1, N_HEADS, QB, KW), F32), name=name,
                          compiler_params=_cparams())(rel_bias)


def _bias_spec():
    return pl.BlockSpec((None, N_HEADS, QB, KW), lambda i: (jnp.minimum(i, N_EDGE), 0, 0, 0))


def _bias_block_bwd(ds_acc, name):
    def body(ds_ref, o_ref):
        sums = []
        for h in range(N_HEADS):
            t = jnp.concatenate([ds_ref[h], jnp.zeros((QB, SKEW_W - KW), F32)], axis=1)
            for b in range(7):
                t = jnp.where(_row_bits(b), pltpu.roll(t, SKEW_W - (1 << b), 1), t)
            sums.append(jnp.sum(t, axis=0, keepdims=True))
        allh = jnp.concatenate(sums, axis=0)
        o_ref[...] = jnp.dot(allh, _skew_index(), precision=lax.Precision.HIGHEST, preferred_element_type=F32)

    return pl.pallas_call(body, out_shape=jax.ShapeDtypeStruct((N_HEADS, N_REL), F32), name=name,
                          compiler_params=_cparams())(ds_acc)


def _scaled(q):
    return (q.astype(F32) * (HEAD_DIM ** -0.5)).astype(BF16)


def _probs(q, kw, bias_ref):
    sc = jnp.stack([lax.dot_general(q[:, HEAD_DIM * h:HEAD_DIM * (h + 1)], kw[:, HEAD_DIM * h:HEAD_DIM * (h + 1)],
                                    _DIMS["nt"], preferred_element_type=F32) + bias_ref[h] for h in range(N_HEADS)])
    e = jnp.exp(sc - jnp.max(sc, axis=-1, keepdims=True))
    return e * (1.0 / jnp.sum(e, axis=-1, keepdims=True))


def _load_padded_kv(qkv_hbm, kpad, vpad, sems, s):
    kpad[0:KPAD, :] = jnp.zeros((KPAD, D_ATTN), BF16)
    vpad[0:KPAD, :] = jnp.zeros((KPAD, D_ATTN), BF16)
    ck = pltpu.make_async_copy(qkv_hbm.at[:, D_ATTN:2 * D_ATTN], kpad.at[pl.ds(KPAD, s), :], sems.at[0])
    cv = pltpu.make_async_copy(qkv_hbm.at[:, 2 * D_ATTN:3 * D_ATTN], vpad.at[pl.ds(KPAD, s), :], sems.at[1])
    ck.start()
    cv.start()
    ck.wait()
    cv.wait()


def _attn_fwd(qkv, bias, name, rider=None):
    s = qkv.shape[0]

    def body(q_ref, qkv_hbm, bias_ref, o_ref, p_ref, kpad, vpad, sems):
        i = pl.program_id(0)

        @pl.when(i == 0)
        def _():
            _load_padded_kv(qkv_hbm, kpad, vpad, sems, s)

        base = pl.multiple_of(i * QB, QB)
        kw = kpad[pl.ds(base, KW), :]
        vw = vpad[pl.ds(base, KW), :]
        q = _scaled(q_ref[...])
        p = _probs(q, kw, bias_ref).astype(BF16)
        p_ref[...] = p
        outs = [jnp.dot(p[h], vw[:, HEAD_DIM * h:HEAD_DIM * (h + 1)], preferred_element_type=F32)
                for h in range(N_HEADS)]
        o_ref[...] = jnp.concatenate(outs, axis=1).astype(BF16)

    res = _call(
        body, name=name, grid=(s // QB,),
        in_specs=[pl.BlockSpec((QB, D_ATTN), lambda i: (i, 0)), pl.BlockSpec(memory_space=pl.ANY),
                  _bias_spec()],
        out_specs=[pl.BlockSpec((QB, D_ATTN), lambda i: (i, 0)), _probs_spec()],
        out_shape=[jax.ShapeDtypeStruct((s, D_ATTN), BF16), jax.ShapeDtypeStruct((N_HEADS, s, KW), BF16)],
        scratch_shapes=[pltpu.VMEM((s + KPAD, D_ATTN), BF16), pltpu.VMEM((s + KPAD, D_ATTN), BF16),
                        pltpu.SemaphoreType.DMA((2,))],
        args=(qkv, qkv, bias), rider=rider)
    return tuple(res) if rider is None else (tuple(res[0]), res[1])


def _probs_spec():
    return pl.BlockSpec((N_HEADS, QB, KW), lambda i: (0, i, 0))


def _attn_bwd(qkv, do, probs, name, rider=None):
    s = qkv.shape[0]
    n = s // QB

    def body(q_ref, qkv_hbm, do_ref, p_ref, dq_ref, dk_hbm, dv_hbm, ds_ref, kpad, vpad, dkacc, dvacc, sems):
        i = pl.program_id(0)

        @pl.when(i == 0)
        def _():
            _load_padded_kv(qkv_hbm, kpad, vpad, sems, s)
            dkacc[...] = jnp.zeros_like(dkacc)
            dvacc[...] = jnp.zeros_like(dvacc)
            ds_ref[...] = jnp.zeros_like(ds_ref)

        base = pl.multiple_of(i * QB, QB)
        kw = kpad[pl.ds(base, KW), :]
        vw = vpad[pl.ds(base, KW), :]
        q = _scaled(q_ref[...])
        dov = do_ref[...]
        heads = [slice(HEAD_DIM * h, HEAD_DIM * (h + 1)) for h in range(N_HEADS)]
        pb = p_ref[...]
        p = pb.astype(F32)
        dp = jnp.stack([lax.dot_general(dov[:, hs], vw[:, hs], _DIMS["nt"], preferred_element_type=F32) for hs in heads])
        ds = p * (dp - jnp.sum(dp * p, axis=-1, keepdims=True))
        ds_ref[...] += ds
        dsb = ds.astype(BF16)
        dvs = [lax.dot_general(pb[h], dov[:, hs], _DIMS["tn"], preferred_element_type=F32) for h, hs in enumerate(heads)]
        dqs = [jnp.dot(dsb[h], kw[:, hs], preferred_element_type=F32) for h, hs in enumerate(heads)]
        dks = [lax.dot_general(dsb[h], q[:, hs], _DIMS["tn"], preferred_element_type=F32) for h, hs in enumerate(heads)]
        dq_ref[...] = (jnp.concatenate(dqs, axis=1) * (HEAD_DIM ** -0.5)).astype(BF16)
        dkacc[pl.ds(base, KW), :] += jnp.concatenate(dks, axis=1)
        dvacc[pl.ds(base, KW), :] += jnp.concatenate(dvs, axis=1)

        @pl.when(i == n - 1)
        def _():
            def cast(j, carry):
                rows = pl.ds(pl.multiple_of(KPAD + j * 512, 512), 512)
                kpad[rows, :] = dkacc[rows, :].astype(BF16)
                vpad[rows, :] = dvacc[rows, :].astype(BF16)
                return carry

            lax.fori_loop(0, s // 512, cast, 0)
            ck = pltpu.make_async_copy(kpad.at[pl.ds(KPAD, s), :], dk_hbm, sems.at[0])
            cv = pltpu.make_async_copy(vpad.at[pl.ds(KPAD, s), :], dv_hbm, sems.at[1])
            ck.start()
            cv.start()
            ck.wait()
            cv.wait()

    blk = pl.BlockSpec((QB, D_ATTN), lambda i: (i, 0))
    acc_shape = jax.ShapeDtypeStruct((s, D_ATTN), BF16)
    return _call(
        body, name=name, grid=(n,),
        in_specs=[blk, pl.BlockSpec(memory_space=pl.ANY), blk, _probs_spec()],
        out_specs=[blk, pl.BlockSpec(memory_space=pl.ANY), pl.BlockSpec(memory_space=pl.ANY), _full((N_HEADS, QB, KW))],
        out_shape=[jax.ShapeDtypeStruct((s, D_ATTN), BF16), acc_shape, acc_shape,
                   jax.ShapeDtypeStruct((N_HEADS, QB, KW), F32)],
        scratch_shapes=[pltpu.VMEM((s + KPAD, D_ATTN), BF16), pltpu.VMEM((s + KPAD, D_ATTN), BF16),
                        pltpu.VMEM((s + KPAD, D_ATTN), F32), pltpu.VMEM((s + KPAD, D_ATTN), F32),
                        pltpu.SemaphoreType.DMA((2,))],
        args=(qkv, qkv, do, probs), rider=rider)


CONV_HALO = 32
CONV_ROWS = 64


def _sigmoid(t):
    return 1.0 / (1.0 + jnp.exp(-t))


CONV_WIN = CONV_ROWS + CONV_HALO - 8


def _row_windows(ref, r0, buf):
    win = ref[pl.ds(r0, CONV_ROWS + CONV_HALO), :]
    for j in range(1, 8):
        buf[j - 1] = win[j:j + CONV_WIN, :]

    def get(o):
        j, a = o % 8, o - o % 8
        if j == 0:
            return ref[pl.ds(r0 + a, CONV_ROWS), :]
        return buf[j - 1, a:a + CONV_ROWS, :]

    return get


def _glu_rows(z_ref, r0, rows):
    a = z_ref[pl.ds(r0, rows), 0:D_CONV]
    b = z_ref[pl.ds(r0, rows), D_CONV:2 * D_CONV]
    return a, _sigmoid(b)


def _conv_fwd(zc, conv_w, conv_b, ln_g, ln_b, name):
    s = zc.shape[0]
    rt = min(256, s)

    def body(z_ref, w_ref, cb_ref, g_ref, b_ref, cv_ref, feat_ref, hpad, shifts):
        hpad[0:CONV_HALO, :] = jnp.zeros((CONV_HALO, D_CONV), F32)

        def glu(i, carry):
            r0 = pl.multiple_of(i * rt, rt)
            a, sb = _glu_rows(z_ref, r0, rt)
            hpad[pl.ds(r0 + CONV_HALO, rt), :] = a * sb
            return carry

        lax.fori_loop(0, s // rt, glu, 0)
        w = w_ref[...]

        def conv(i, carry):
            r0 = pl.multiple_of(i * CONV_ROWS, CONV_ROWS)
            win = _row_windows(hpad, r0, shifts)
            acc = jnp.broadcast_to(cb_ref[...], (CONV_ROWS, D_CONV))
            for k in range(CONV_WIDTH):
                acc = acc + win(2 + k) * w[k:k + 1, :]
            cv_ref[pl.ds(r0, CONV_ROWS), :] = acc
            yhat, _ = _ln_hat(acc)
            y = yhat * g_ref[...] + b_ref[...]
            feat_ref[pl.ds(r0, CONV_ROWS), :] = (y * _sigmoid(y)).astype(BF16)
            return carry

        lax.fori_loop(0, s // CONV_ROWS, conv, 0)

    return pl.pallas_call(
        body, out_shape=[jax.ShapeDtypeStruct((s, D_CONV), F32), jax.ShapeDtypeStruct((s, D_CONV), BF16)],
        scratch_shapes=[pltpu.VMEM((s + CONV_HALO, D_CONV), F32), pltpu.VMEM((7, CONV_WIN, D_CONV), F32)],
        name=name, compiler_params=_cparams(),
    )(zc, conv_w, conv_b, ln_g, ln_b)


def _conv_bwd(dfeat, cv, zc, conv_w, ln_g, ln_b, name):
    s = zc.shape[0]
    rt = min(256, s)

    def body(df_ref, cv_ref, z_ref, w_ref, g_ref, b_ref, dz_ref, dw_ref, dcb_ref, dg_ref, db_ref, hpad, dcvpad, dwacc,
             hshifts, dshifts):
        hpad[0:CONV_HALO, :] = jnp.zeros((CONV_HALO, D_CONV), F32)
        dcvpad[s:, :] = jnp.zeros((CONV_HALO, D_CONV), F32)
        dwacc[...] = jnp.zeros_like(dwacc)
        dcb_ref[...] = jnp.zeros_like(dcb_ref)
        dg_ref[...] = jnp.zeros_like(dg_ref)
        db_ref[...] = jnp.zeros_like(db_ref)

        def pass1(i, carry):
            r0 = pl.multiple_of(i * rt, rt)
            a, sb = _glu_rows(z_ref, r0, rt)
            hpad[pl.ds(r0 + CONV_HALO, rt), :] = a * sb
            cvhat, rstd = _ln_hat(cv_ref[pl.ds(r0, rt), :])
            y = cvhat * g_ref[...] + b_ref[...]
            sg = _sigmoid(y)
            dy = df_ref[pl.ds(r0, rt), :] * (sg * (1.0 + y * (1.0 - sg)))
            dg_ref[...] += jnp.sum(dy * cvhat, axis=0, keepdims=True)
            db_ref[...] += jnp.sum(dy, axis=0, keepdims=True)
            dcv = _ln_hat_bwd(dy * g_ref[...], cvhat, rstd)
            dcb_ref[...] += jnp.sum(dcv, axis=0, keepdims=True)
            dcvpad[pl.ds(r0, rt), :] = dcv
            return carry

        lax.fori_loop(0, s // rt, pass1, 0)
        w = w_ref[...]

        def pass2(i, carry):
            r0 = pl.multiple_of(i * CONV_ROWS, CONV_ROWS)
            dwin = _row_windows(dcvpad, r0, dshifts)
            hwin = _row_windows(hpad, r0, hshifts)
            dcv = dwin(0)
            dh = jnp.zeros((CONV_ROWS, D_CONV), F32)
            for k in range(CONV_WIDTH):
                dh = dh + dwin(30 - k) * w[k:k + 1, :]
                prod = dcv * hwin(2 + k)
                dwacc[8 * k:8 * k + 8, :] += jnp.sum(prod.reshape(CONV_ROWS // 8, 8, D_CONV), axis=0)
            a, sb = _glu_rows(z_ref, r0, CONV_ROWS)
            dz_ref[pl.ds(r0, CONV_ROWS), :] = jnp.concatenate([dh * sb, dh * a * sb * (1.0 - sb)], axis=1).astype(BF16)
            return carry

        lax.fori_loop(0, s // CONV_ROWS, pass2, 0)
        dw_ref[...] = jnp.sum(dwacc[...].reshape(32, 8, D_CONV), axis=1)

    vs = jax.ShapeDtypeStruct((1, D_CONV), F32)
    return pl.pallas_call(
        body,
        out_shape=[jax.ShapeDtypeStruct((s, 2 * D_CONV), BF16), jax.ShapeDtypeStruct((32, D_CONV), F32), vs, vs, vs],
        scratch_shapes=[pltpu.VMEM((s + CONV_HALO, D_CONV), F32), pltpu.VMEM((s + CONV_HALO, D_CONV), F32),
                        pltpu.VMEM((256, D_CONV), F32), pltpu.VMEM((7, CONV_WIN, D_CONV), F32),
                        pltpu.VMEM((7, CONV_WIN, D_CONV), F32)],
        name=name, compiler_params=_cparams(),
    )(dfeat, cv, zc, conv_w, ln_g, ln_b)


def _branch_out(feats, wts, name):
    s = feats[0].shape[0]
    tm = min(1024, s)

    def body(*refs):
        for f_ref, w_ref, o_ref in zip(refs[:3], refs[3:6], refs[6:]):
            o_ref[...] = lax.dot_general(f_ref[...], w_ref[...], _DIMS["nt"], preferred_element_type=F32).astype(BF16)

    row = pl.BlockSpec((tm, D_MODEL), lambda i: (i, 0))
    sh = jax.ShapeDtypeStruct((s, D_MODEL), BF16)
    return pl.pallas_call(
        body, grid=(s // tm,),
        in_specs=[pl.BlockSpec((tm, f.shape[1]), lambda i: (i, 0)) for f in feats] + [_full(w.shape) for w in wts],
        out_specs=[row] * 3, out_shape=[sh] * 3, name=name, compiler_params=_cparams(),
    )(*feats, *wts)


def _branch_in_bwd(dys, wts, out_dtypes, name):
    s = dys[0].shape[0]
    tm = min(1024, s)

    def body(*refs):
        for d_ref, w_ref, o_ref in zip(refs[:3], refs[3:6], refs[6:]):
            o_ref[...] = jnp.dot(d_ref[...], w_ref[...], preferred_element_type=F32).astype(o_ref.dtype)

    row = pl.BlockSpec((tm, D_MODEL), lambda i: (i, 0))
    return pl.pallas_call(
        body, grid=(s // tm,), in_specs=[row] * 3 + [_full(w.shape) for w in wts],
        out_specs=[pl.BlockSpec((tm, w.shape[1]), lambda i: (i, 0)) for w in wts],
        out_shape=[jax.ShapeDtypeStruct((s, w.shape[1]), dt) for w, dt in zip(wts, out_dtypes)],
        name=name, compiler_params=_cparams(),
    )(*dys, *wts)


def _branch_dw(dys, feats, name):
    s = dys[0].shape[0]
    tm = 512

    def body(*refs):
        for d_ref, f_ref, o_ref in zip(refs[:3], refs[3:6], refs[6:]):
            acc = lax.dot_general(d_ref[...], f_ref[...], _DIMS["tn"], preferred_element_type=F32)
            half = acc.shape[1] // 2
            o_ref[0] = acc[:, :half].astype(BF16)
            o_ref[1] = acc[:, half:].astype(BF16)

    return pl.pallas_call(
        body, grid=(D_MODEL // tm,),
        in_specs=[pl.BlockSpec((s, tm), lambda i: (0, i))] * 3 + [_full(f.shape) for f in feats],
        out_specs=[pl.BlockSpec((2, tm, f.shape[1] // 2), lambda i: (0, i, 0)) for f in feats],
        out_shape=[jax.ShapeDtypeStruct((2, D_MODEL, f.shape[1] // 2), BF16) for f in feats],
        name=name, compiler_params=_cparams(),
    )(*dys, *feats)


def _merge(zg, b_gate, ys, name):
    s = zg.shape[0]
    tm = _row_tile(s)

    def body(zg_ref, bg_ref, y0_ref, y1_ref, y2_ref, o_ref):
        acc = None
        for j, y_ref in enumerate((y0_ref, y1_ref, y2_ref)):
            cs = slice(D_MODEL * j, D_MODEL * (j + 1))
            t = _sigmoid(zg_ref[:, cs] + bg_ref[:, cs]) * y_ref[...]
            acc = t if acc is None else acc + t
        o_ref[...] = acc.astype(BF16)

    row = pl.BlockSpec((tm, D_MODEL), lambda i: (i, 0))
    return pl.pallas_call(
        body, grid=(s // tm,),
        in_specs=[pl.BlockSpec((tm, 3 * D_MODEL), lambda i: (i, 0)), _full((1, 3 * D_MODEL)), row, row, row],
        out_specs=row, out_shape=jax.ShapeDtypeStruct((s, D_MODEL), BF16), name=name, compiler_params=_cparams(),
    )(zg, b_gate, *ys)


def _merge_bwd(dmix, w_o, zg, b_gate, ys, name):
    s = zg.shape[0]
    tm = min(256, s)

    def body(dmix_ref, wo_ref, zg_ref, bg_ref, y0_ref, y1_ref, y2_ref, d0_ref, d1_ref, d2_ref, dzg_ref, dbg_ref):
        first = pl.program_id(0) == 0

        @pl.when(first)
        def _():
            dbg_ref[...] = jnp.zeros_like(dbg_ref)

        dmv = lax.dot_general(dmix_ref[...], wo_ref[...], _DIMS["nt"], preferred_element_type=F32)
        for j, (y_ref, d_ref) in enumerate(((y0_ref, d0_ref), (y1_ref, d1_ref), (y2_ref, d2_ref))):
            cs = slice(D_MODEL * j, D_MODEL * (j + 1))
            g = _sigmoid(zg_ref[:, cs] + bg_ref[:, cs])
            d_ref[...] = (dmv * g).astype(BF16)
            dzg = dmv * y_ref[...] * g * (1.0 - g)
            dzg_ref[:, cs] = dzg.astype(BF16)
            dbg_ref[:, cs] += jnp.sum(dzg, axis=0, keepdims=True)

    row = pl.BlockSpec((tm, D_MODEL), lambda i: (i, 0))
    wide = pl.BlockSpec((tm, 3 * D_MODEL), lambda i: (i, 0))
    yb = jax.ShapeDtypeStruct((s, D_MODEL), BF16)
    return pl.pallas_call(
        body, grid=(s // tm,),
        in_specs=[row, _full(w_o.shape), wide, _full((1, 3 * D_MODEL)), row, row, row],
        out_specs=[row, row, row, wide, _full((1, 3 * D_MODEL))],
        out_shape=[yb, yb, yb, jax.ShapeDtypeStruct((s, 3 * D_MODEL), BF16), jax.ShapeDtypeStruct((1, 3 * D_MODEL), F32)],
        name=name, compiler_params=_cparams(),
    )(dmix, w_o, zg, b_gate, *ys)


def _ff_hidden(u2, w_ff1t, b_ff1, name, rider=None):
    s = u2.shape[0]
    tm, tn = min(2048, s), 1024

    def body(a_ref, b_ref, bias_ref, pre_ref, h_ref):
        acc = lax.dot_general(a_ref[...], b_ref[...], _DIMS["nt"], preferred_element_type=F32) + bias_ref[...]
        pre_ref[...] = acc.astype(BF16)
        h_ref[...] = _relu2(acc).astype(BF16)

    blk = pl.BlockSpec((tm, tn), lambda i, j: (i, j))
    sh = jax.ShapeDtypeStruct((s, D_FF), BF16)
    res = _call(body, name=name, grid=(s // tm, D_FF // tn),
                in_specs=[pl.BlockSpec((tm, D_MODEL), lambda i, j: (i, 0)), pl.BlockSpec((tn, D_MODEL), lambda i, j: (j, 0)),
                          pl.BlockSpec((1, tn), lambda i, j: (0, j))],
                out_specs=[blk, blk], out_shape=[sh, sh], scratch_shapes=[], args=(u2, w_ff1t, b_ff1), rider=rider)
    return tuple(res) if rider is None else (tuple(res[0]), res[1])


def _ff_hidden_bwd(dff, w_ff2, hpre, name):
    s = dff.shape[0]
    tm, tn = min(1024, s), 1024

    def body(a_ref, b_ref, h_ref, o_ref, sum_ref):
        dh = lax.dot_general(a_ref[...], b_ref[...], _DIMS["nt"], preferred_element_type=F32)
        dpre = dh * (2.0 * jnp.maximum(h_ref[...].astype(F32), 0.0))
        o_ref[...] = dpre.astype(BF16)
        _acc_rows(sum_ref, dpre, pl.program_id(1) == 0)

    res = _call(
        body, name=name, grid=(D_FF // tn, s // tm),
        in_specs=[pl.BlockSpec((tm, D_MODEL), lambda j, i: (i, 0)), pl.BlockSpec((tn, D_MODEL), lambda j, i: (j, 0)),
                  pl.BlockSpec((tm, tn), lambda j, i: (i, j))],
        out_specs=[pl.BlockSpec((tm, tn), lambda j, i: (i, j)), pl.BlockSpec((1, tn), lambda j, i: (0, j))],
        out_shape=[jax.ShapeDtypeStruct((s, D_FF), BF16), jax.ShapeDtypeStruct((1, D_FF), F32)],
        scratch_shapes=[], args=(dff, w_ff2, hpre))
    return tuple(res)


def _silu(t):
    return t * _sigmoid(t)


def _mod_fwd(c_all, w_ada_sh, b_ada_sh, name):
    cols = w_ada_sh.shape[2]

    def body(c_ref, w_ref, b_ref, o_ref):
        ca = _silu(c_ref[...]).astype(BF16)
        o_ref[0] = jnp.dot(ca, w_ref[0].astype(BF16), preferred_element_type=F32) + b_ref[0]

    return pl.pallas_call(
        body, grid=(DEPTH,),
        in_specs=[_full((N_DEV, D_MODEL)), pl.BlockSpec((1, D_MODEL, cols), lambda l: (l, 0, 0)),
                  pl.BlockSpec((1, 1, cols), lambda l: (l, 0, 0))],
        out_specs=pl.BlockSpec((1, N_DEV, cols), lambda l: (l, 0, 0)),
        out_shape=jax.ShapeDtypeStruct((DEPTH, N_DEV, cols), F32), name=name, compiler_params=_cparams(),
    )(c_all, w_ada_sh, b_ada_sh)


def _mod_bwd(c_all, dmod_sh, name):
    cols = dmod_sh.shape[2]

    def body(c_ref, d_ref, o_ref):
        ca = _silu(c_ref[...])
        o_ref[0] = lax.dot_general(ca, d_ref[0], _DIMS["tn"], precision=lax.Precision.HIGHEST,
                                   preferred_element_type=F32)

    return pl.pallas_call(
        body, grid=(DEPTH,),
        in_specs=[_full((N_DEV, D_MODEL)), pl.BlockSpec((1, N_DEV, cols), lambda l: (l, 0, 0))],
        out_specs=pl.BlockSpec((1, D_MODEL, cols), lambda l: (l, 0, 0)),
        out_shape=jax.ShapeDtypeStruct((DEPTH, D_MODEL, cols), F32), name=name, compiler_params=_cparams(),
    )(c_all, dmod_sh)


def _flat_tiles(rows, cols, itemsize_total):
    budget = 12 * 1024 * 1024
    tr = rows
    while tr % 32 == 0 and tr * cols * itemsize_total > budget:
        tr //= 2
    return tr


def _sum_cores(dws, recvs, place, name):
    k = len(dws)

    def body(place_ref, *refs):
        for a_ref, b_ref, o_ref in zip(refs[:k], refs[k:2 * k], refs[2 * k:]):
            o_ref[...] = (a_ref[...].astype(F32) + b_ref[...].astype(F32)).astype(BF16)

    whole = [pl.BlockSpec(a.shape[1:], lambda i, pr: (0, 0)) for a in dws]
    mine = [pl.BlockSpec((None,) + a.shape[1:], lambda i, pr: (pr[0], 0, 0)) for a in dws]
    grid_spec = pltpu.PrefetchScalarGridSpec(num_scalar_prefetch=1, grid=(1,), in_specs=mine + whole, out_specs=whole)
    return pl.pallas_call(body, grid_spec=grid_spec, out_shape=[jax.ShapeDtypeStruct(a.shape[1:], BF16) for a in dws],
                          name=name, compiler_params=_cparams())(place, *dws, *recvs)


def _sum_chips(h, r, place, name):
    _, rs, n = h.shape
    tr = _flat_tiles(rs, n, 12)

    def body(place_ref, h_ref, r_ref, o_ref):
        o_ref[...] = ((h_ref[...].astype(F32) + r_ref[0].astype(F32)) + r_ref[1].astype(F32)) + r_ref[2].astype(F32)

    grid_spec = pltpu.PrefetchScalarGridSpec(
        num_scalar_prefetch=1, grid=(rs // tr,),
        in_specs=[pl.BlockSpec((None, tr, n), lambda i, pr: (pr[1], i, 0)), pl.BlockSpec((3, tr, n), lambda i, pr: (0, i, 0))],
        out_specs=pl.BlockSpec((tr, n), lambda i, pr: (i, 0)))
    return pl.pallas_call(body, grid_spec=grid_spec, out_shape=jax.ShapeDtypeStruct((rs, n), F32), name=name,
                          compiler_params=_cparams())(place, h, r)


def _adam_math(w, g, m, v):
    m2 = ADAM_B1 * m + (1.0 - ADAM_B1) * g
    v2 = ADAM_B2 * v + (1.0 - ADAM_B2) * (g * g)
    m_hat = m2 / (1.0 - ADAM_B1 ** ADAM_STEP)
    v_hat = v2 / (1.0 - ADAM_B2 ** ADAM_STEP)
    delta = -ADAM_LR * (m_hat / (jnp.sqrt(v_hat) + ADAM_EPS) + ADAM_WD * w)
    return delta, m2, v2


def _adamw(w, m, v, grads, name):
    r, c = w.shape
    tr = _flat_tiles(r, c, 4 * (7 + len(grads)))

    def body(*refs):
        w_ref, m_ref, v_ref = refs[:3]
        g_refs = refs[3:3 + len(grads)]
        g_ref, d_ref, m2_ref, v2_ref = refs[3 + len(grads):]
        g = g_refs[0][...]
        for gr in g_refs[1:]:
            g = g + gr[...]
        delta, m2, v2 = _adam_math(w_ref[...], g, m_ref[...], v_ref[...])
        g_ref[...] = g
        d_ref[...] = delta
        m2_ref[...] = m2
        v2_ref[...] = v2

    blk = pl.BlockSpec((tr, c), lambda i: (i, 0))
    sh = jax.ShapeDtypeStruct((r, c), F32)
    return pl.pallas_call(body, grid=(r // tr,), in_specs=[blk] * (3 + len(grads)), out_specs=[blk] * 4,
                          out_shape=[sh] * 4, name=name, compiler_params=_cparams())(w, m, v, *grads)


def _adamw_halves(w, m, v, own, other, place, split, name):
    nl, r, c = w.shape
    hr, hc = own[0].shape
    tr = _flat_tiles(hr, hc, 4 * (7 + 2 * nl))
    nt = hr // tr
    if split == "rows":
        w_spec = pl.BlockSpec((None, tr, c), lambda l, h, t, pr: (l, h * nt + t, 0))
    else:
        w_spec = pl.BlockSpec((None, tr, hc), lambda l, h, t, pr: (l, t, h))

    def g_spec(layer, mine):
        return pl.BlockSpec((tr, hc), lambda l, h, t, pr: (jnp.where((l == layer) & ((h == pr[0]) == mine), t, nt - 1), 0))

    def body(place_ref, w_ref, m_ref, v_ref, *refs):
        own_refs, other_refs = refs[:nl], refs[nl:2 * nl]
        g_ref, d_ref, m2_ref, v2_ref = refs[2 * nl:]
        layer = pl.program_id(0)
        mine = pl.program_id(1) == place_ref[0]
        g = None
        for li in range(nl):
            cand = jnp.where(mine, own_refs[li][...], other_refs[li][...])
            g = cand if g is None else jnp.where(layer == li, cand, g)
        delta, m2, v2 = _adam_math(w_ref[...], g, m_ref[...], v_ref[...])
        g_ref[...] = g
        d_ref[...] = delta
        m2_ref[...] = m2
        v2_ref[...] = v2

    sh = jax.ShapeDtypeStruct((nl, r, c), F32)
    g_specs = [g_spec(li, True) for li in range(nl)] + [g_spec(li, False) for li in range(nl)]
    return _call(body, name=name, grid=(nl, 2, nt), in_specs=[w_spec] * 3 + g_specs, out_specs=[w_spec] * 4,
                 out_shape=[sh] * 4, scratch_shapes=[], args=(w, m, v, *own, *other), prefetch=(place,))


def _adamw_small(w, m, v, g_all, name):
    r, c = w.shape

    def body(w_ref, m_ref, v_ref, g_ref, go_ref, d_ref, m2_ref, v2_ref):
        g = g_ref[0]
        for b in range(1, N_DEV):
            g = g + g_ref[b]
        delta, m2, v2 = _adam_math(w_ref[...], g, m_ref[...], v_ref[...])
        go_ref[...] = g
        d_ref[...] = delta
        m2_ref[...] = m2
        v2_ref[...] = v2

    sh = jax.ShapeDtypeStruct((r, c), F32)
    return pl.pallas_call(body, out_shape=[sh] * 4, name=name, compiler_params=_cparams())(w, m, v, g_all)


def _me():
    return lax.axis_index("x"), lax.axis_index("y"), lax.axis_index("c")


def _flip(v, bit):
    return 1 - v if bit else v


def _allgather_small(blk, name):
    r, c = blk.shape

    def body(x_ref, o_ref, send_sems, recv_sems):
        x, y, cc = _me()
        me = 4 * x + 2 * y + cc
        copies = []
        for k in range(1, N_DEV):
            peer = (_flip(x, k & 4), _flip(y, k & 2), _flip(cc, k & 1))
            cp = pltpu.make_async_remote_copy(src_ref=x_ref, dst_ref=o_ref.at[me], send_sem=send_sems.at[k - 1],
                                              recv_sem=recv_sems.at[k - 1], device_id=peer, device_id_type=MESH)
            cp.start()
            copies.append(cp)
        o_ref[me] = x_ref[...]
        for cp in copies:
            cp.wait()

    return pl.pallas_call(
        body, out_shape=jax.ShapeDtypeStruct((N_DEV, r, c), F32),
        in_specs=[pl.BlockSpec(memory_space=pltpu.VMEM)], out_specs=pl.BlockSpec(memory_space=pltpu.VMEM),
        scratch_shapes=[pltpu.SemaphoreType.DMA((N_DEV - 1,)), pltpu.SemaphoreType.DMA((N_DEV - 1,))],
        name=name, compiler_params=_cparams(),
    )(blk)


class _Rider:
    def __init__(self, arrays, out_shapes, scratch_shapes, start, finish):
        self.arrays, self.out_shapes, self.scratch_shapes = list(arrays), list(out_shapes), list(scratch_shapes)
        self.start, self.finish = start, finish


def _call(body, *, name, grid, in_specs, out_specs, out_shape, scratch_shapes, args, rider=None, prefetch=()):
    npf = len(prefetch)

    def launch(fn, in_specs, out_specs, out_shape, scratch_shapes, args):
        grid_spec = pltpu.PrefetchScalarGridSpec(num_scalar_prefetch=npf, grid=grid, in_specs=in_specs,
                                                 out_specs=out_specs, scratch_shapes=scratch_shapes)
        return pl.pallas_call(fn, grid_spec=grid_spec, out_shape=out_shape, name=name,
                              compiler_params=_cparams())(*prefetch, *args)

    if rider is None:
        return launch(body, list(in_specs), list(out_specs), list(out_shape), list(scratch_shapes), args)
    ni, no, ns = len(in_specs), len(out_specs), len(scratch_shapes)
    ri, ro = len(rider.arrays), len(rider.out_shapes)
    steps = int(np.prod(grid))

    def wrapped(*refs):
        pf, refs = refs[:npf], refs[npf:]
        h_in, r_in = refs[:ni], refs[ni:ni + ri]
        h_out, r_out = refs[ni + ri:ni + ri + no], refs[ni + ri + no:ni + ri + no + ro]
        h_scr, r_scr = refs[ni + ri + no + ro:ni + ri + no + ro + ns], refs[ni + ri + no + ro + ns:]
        step = pl.program_id(0)
        for d in range(1, len(grid)):
            step = step * grid[d] + pl.program_id(d)

        @pl.when(step == 0)
        def _():
            rider.start(r_in, r_out, r_scr)

        body(*pf, *h_in, *h_out, *h_scr)

        @pl.when(step == steps - 1)
        def _():
            rider.finish(r_in, r_out, r_scr)

    anyspec = pl.BlockSpec(memory_space=pl.ANY)
    res = launch(wrapped, list(in_specs) + [anyspec] * ri, list(out_specs) + [anyspec] * ro,
                 list(out_shape) + rider.out_shapes, list(scratch_shapes) + rider.scratch_shapes,
                 list(args) + rider.arrays)
    return res[:no], res[no:]


def _run_rider(rider, name):
    ri = len(rider.arrays)

    def body(*refs):
        r_in, r_out, r_scr = refs[:ri], refs[ri:ri + len(rider.out_shapes)], refs[ri + len(rider.out_shapes):]
        rider.start(r_in, r_out, r_scr)
        rider.finish(r_in, r_out, r_scr)

    anyspec = pl.BlockSpec(memory_space=pl.ANY)
    return pl.pallas_call(body, in_specs=[anyspec] * ri, out_specs=[anyspec] * len(rider.out_shapes),
                          out_shape=rider.out_shapes, scratch_shapes=rider.scratch_shapes, name=name,
                          compiler_params=_cparams())(*rider.arrays)


def _allgather_rider(blk):
    def copies(ins, outs, scr):
        send_sems, recv_sems, loc_sems, stage = scr
        x, y, cc = _me()
        me = 4 * x + 2 * y + cc
        remote = [pltpu.make_async_remote_copy(
            src_ref=ins[0], dst_ref=outs[0].at[me], send_sem=send_sems.at[k - 1], recv_sem=recv_sems.at[k - 1],
            device_id=(_flip(x, k & 4), _flip(y, k & 2), _flip(cc, k & 1)), device_id_type=MESH) for k in range(1, N_DEV)]
        return remote, pltpu.make_async_copy(ins[0], stage, loc_sems.at[0]), (outs[0].at[me], stage, loc_sems.at[1])

    def start(ins, outs, scr):
        remote, lin, _ = copies(ins, outs, scr)
        lin.start()
        for cp in remote:
            cp.start()

    def finish(ins, outs, scr):
        remote, lin, (dst, stage, sem) = copies(ins, outs, scr)
        lin.wait()
        lout = pltpu.make_async_copy(stage, dst, sem)
        lout.start()
        for cp in remote:
            cp.wait()
        lout.wait()

    return _Rider([blk], [jax.ShapeDtypeStruct((N_DEV,) + blk.shape, blk.dtype)],
                  [pltpu.SemaphoreType.DMA((N_DEV - 1,)), pltpu.SemaphoreType.DMA((N_DEV - 1,)),
                   pltpu.SemaphoreType.DMA((2,)), pltpu.VMEM(blk.shape, blk.dtype)], start, finish)


def _gather_rider(shards):
    n = len(shards)

    def copies(ins, outs, scr, relay=True):
        ici_send, ici_recv, d2d_send, d2d_recv, loc_sems = scr[:5]
        stage = scr[5:]
        x, y, cc = _me()
        chip = 2 * x + y
        sibling = (x, y, 1 - cc)
        local, sends, relays = [], [], []
        for j in range(n):
            def rows(ch, h, j=j):
                return outs[j].at[ch, h]

            lc = pltpu.make_async_copy(ins[j], stage[j], loc_sems.at[j])
            local.append((lc, pltpu.make_async_copy(stage[j], outs[j].at[chip], loc_sems.at[n + j]) if relay else None))
            for k in range(1, N_CHIP):
                px, py = _flip(x, k & 2), _flip(y, k & 1)
                pchip = 2 * px + py
                q = 3 * j + k - 1
                out_cp = pltpu.make_async_remote_copy(src_ref=ins[j].at[cc], dst_ref=rows(chip, cc),
                                                      send_sem=ici_send.at[q], recv_sem=ici_recv.at[q],
                                                      device_id=(px, py, cc), device_id_type=MESH)
                sends.append(out_cp)
                if not relay:
                    continue
                arrival = pltpu.make_async_remote_copy(src_ref=rows(pchip, cc), dst_ref=rows(pchip, cc),
                                                       send_sem=ici_send.at[q], recv_sem=ici_recv.at[q],
                                                       device_id=(px, py, cc), device_id_type=MESH)
                forward = pltpu.make_async_remote_copy(src_ref=rows(pchip, cc), dst_ref=rows(pchip, cc),
                                                       send_sem=d2d_send.at[q], recv_sem=d2d_recv.at[q],
                                                       device_id=sibling, device_id_type=MESH)
                from_sibling = pltpu.make_async_remote_copy(src_ref=rows(pchip, 1 - cc), dst_ref=rows(pchip, 1 - cc),
                                                            send_sem=d2d_send.at[q], recv_sem=d2d_recv.at[q],
                                                            device_id=sibling, device_id_type=MESH)
                relays.append((arrival, forward, from_sibling))
        return local, sends, relays

    def start(ins, outs, scr):
        local, sends, _ = copies(ins, outs, scr, relay=False)
        for lin, _ in local:
            lin.start()
        for cp in sends:
            cp.start()

    def finish(ins, outs, scr):
        local, sends, relays = copies(ins, outs, scr)
        for lin, lout in local:
            lin.wait()
            lout.start()
        for arrival, forward, _ in relays:
            arrival.wait_recv()
            forward.start()
        for cp in sends:
            cp.wait_send()
        for _, forward, from_sibling in relays:
            forward.wait_send()
            from_sibling.wait_recv()
        for _, lout in local:
            lout.wait()

    scratch = [pltpu.SemaphoreType.DMA((3 * n,)), pltpu.SemaphoreType.DMA((3 * n,)), pltpu.SemaphoreType.DMA((3 * n,)),
               pltpu.SemaphoreType.DMA((3 * n,)), pltpu.SemaphoreType.DMA((2 * n,))]
    scratch += [pltpu.VMEM(a.shape, a.dtype) for a in shards]
    return _Rider(shards, [jax.ShapeDtypeStruct((N_CHIP,) + a.shape, a.dtype) for a in shards], scratch, start, finish)


def _sibling_rider(arrs, other_half=False):
    n = len(arrs)

    def copies(ins, outs, scr):
        send_sems, recv_sems = scr
        x, y, cc = _me()
        return [pltpu.make_async_remote_copy(
            src_ref=ins[j].at[1 - cc] if other_half else ins[j], dst_ref=outs[j], send_sem=send_sems.at[j],
            recv_sem=recv_sems.at[j], device_id=(x, y, 1 - cc), device_id_type=MESH) for j in range(n)]

    def start(ins, outs, scr):
        for cp in copies(ins, outs, scr):
            cp.start()

    def finish(ins, outs, scr):
        for cp in copies(ins, outs, scr):
            cp.wait()

    return _Rider(arrs, [jax.ShapeDtypeStruct(a.shape[1:] if other_half else a.shape, a.dtype) for a in arrs],
                  [pltpu.SemaphoreType.DMA((n,)), pltpu.SemaphoreType.DMA((n,))], start, finish)


def _sibling_send(arrs, name, other_half=False):
    return _run_rider(_sibling_rider(arrs, other_half), name)


def _join_riders(first, second):
    ni, no, ns = len(first.arrays), len(first.out_shapes), len(first.scratch_shapes)

    def split(ins, outs, scr):
        return (ins[:ni], outs[:no], scr[:ns]), (ins[ni:], outs[no:], scr[ns:])

    def start(ins, outs, scr):
        a, b = split(ins, outs, scr)
        first.start(*a)
        second.start(*b)

    def finish(ins, outs, scr):
        a, b = split(ins, outs, scr)
        first.finish(*a)
        second.finish(*b)

    return _Rider(first.arrays + second.arrays, first.out_shapes + second.out_shapes,
                  first.scratch_shapes + second.scratch_shapes, start, finish)


def _scatter_rider(arrs):
    n = len(arrs)

    def copies(ins, outs, scr):
        send_sems, recv_sems = scr
        x, y, cc = _me()
        cps = []
        for j in range(n):
            for k in range(1, N_CHIP):
                px, py = _flip(x, k & 2), _flip(y, k & 1)
                cps.append(pltpu.make_async_remote_copy(
                    src_ref=ins[j].at[2 * px + py], dst_ref=outs[j].at[k - 1], send_sem=send_sems.at[3 * j + k - 1],
                    recv_sem=recv_sems.at[3 * j + k - 1], device_id=(px, py, cc), device_id_type=MESH))
        return cps

    def start(ins, outs, scr):
        for cp in copies(ins, outs, scr):
            cp.start()

    def finish(ins, outs, scr):
        for cp in copies(ins, outs, scr):
            cp.wait()

    return _Rider(arrs, [jax.ShapeDtypeStruct((N_CHIP - 1,) + a.shape[1:], a.dtype) for a in arrs],
                  [pltpu.SemaphoreType.DMA((3 * n,)), pltpu.SemaphoreType.DMA((3 * n,))], start, finish)


COL_SHARDED = ("w_in", "w_br_pool", "w_br_attn", "w_br_conv", "w_ff1")
ROW_SHARDED = ("w_o", "w_ff2")
BIG = COL_SHARDED + ROW_SHARDED
SMALL = ("b_ada", "b_gate", "w_pool", "pool_scale", "rel_bias", "conv_w", "conv_b", "conv_ln_g", "conv_ln_b",
         "ln_mix_g", "ln_mix_b", "b_ff1", "b_ff2", "ln_ff_g", "ln_ff_b")
PACK_W = 1024


def _pack(parts):
    rows = []
    for a in parts:
        flat = a.reshape(-1)
        n = -(-flat.shape[0] // PACK_W) * PACK_W
        rows.append(jnp.pad(flat, (0, n - flat.shape[0])).reshape(-1, PACK_W))
    out = jnp.concatenate(rows, axis=0)
    r = -(-out.shape[0] // 8) * 8
    return jnp.pad(out, ((0, r - out.shape[0]), (0, 0)))


def _unpack(packed, shapes):
    out, r0 = [], 0
    for shp in shapes:
        size = int(np.prod(shp))
        nr = -(-size // PACK_W)
        out.append(packed[r0:r0 + nr].reshape(-1)[:size].reshape(shp))
        r0 += nr
    return out


def _hosted(fn, hook, *args, **kw):
    if hook is None:
        return fn(*args, **kw)
    res, rider_out = fn(*args, rider=hook[0], **kw)
    hook[1](rider_out)
    return res


def _layer_fwd(l, x, mod, W, P, hooks=None, u=None):
    hooks = hooks or {}
    s = x.shape[0]
    sh_m, sc_m, g_m, sh_f, sc_f, g_f = [mod[l:l + 1, D_MODEL * j:D_MODEL * (j + 1)] for j in range(6)]
    n = lambda t: f"{t}{l}"
    w_in = W["w_in"][l]
    if u is None:
        u = _ln_mod(x, sc_m, sh_m, n("ln_mod_mix"))
    zp = _mm(u, w_in, "nt", tm=s, tn=256, out_dtype=F32, name=n("z_pool"), b_col0=0, n_out=D_POOL)
    qkv = _mm(u, w_in, "nt", tm=s, tn=256, out_dtype=BF16, name=n("z_qkv"), b_col0=OFF_QKV // 256, n_out=3 * D_ATTN)
    zc = _mm(u, w_in, "nt", tm=s, tn=256, out_dtype=F32, name=n("z_conv"), b_col0=OFF_CONV // 256, n_out=2 * D_CONV)
    zg = _hosted(_mm, hooks.get("z_gate"), u, w_in, "nt", tm=min(2048, s), tn=768, out_dtype=BF16, name=n("z_gate"),
                 b_col0=OFF_GATE // 768, n_out=3 * D_MODEL)

    p, feat_pool = _pool_fwd(zp, P["wp_bd"][l], P["pool_scale"][l], n("pool_fwd"))
    bias = _bias_block(P["rel_bias"][l], n("bias_block"))
    o, probs = _hosted(_attn_fwd, hooks.get("attn"), qkv, bias, n("attn_fwd"))
    cv, feat_conv = _conv_fwd(zc, P["conv_w"][l], P["conv_b"][l], P["conv_ln_g"][l], P["conv_ln_b"][l], n("conv_fwd"))

    branch_w = (W["w_br_pool"][l], W["w_br_attn"][l], W["w_br_conv"][l])
    ys = tuple(_branch_out((feat_pool, o, feat_conv), branch_w, n("branch_out")))
    merged = _merge(zg, P["b_gate"][l], ys, n("merge"))
    mix, x1, u2 = _mm_resid_ln(merged, W["w_o"][l], None, x, g_m, P["ln_mix_g"][l], P["ln_mix_b"][l], n("mix_out"),
                               mod_next=(sc_f, sh_f))

    hpre, hid = _hosted(_ff_hidden, hooks.get("ff1"), u2, W["w_ff1"][l], P["b_ff1"][l], n("ff1"))
    above = None if l + 1 == mod.shape[0] else (mod[l + 1:l + 2, D_MODEL:2 * D_MODEL], mod[l + 1:l + 2, 0:D_MODEL])
    ff, x2, *u_next = _hosted(_mm_resid_ln, hooks.get("ff2"), hid, W["w_ff2"][l], P["b_ff2"][l], x1, g_f,
                              P["ln_ff_g"][l], P["ln_ff_b"][l], n("ff2"), mod_next=above)
    saved = dict(x=x, u=u, zp=zp, qkv=qkv, zc=zc, zg=zg, p=p, feat_pool=feat_pool, probs=probs, o=o, cv=cv,
                 feat_conv=feat_conv, ys=ys, merged=merged, mix=mix, x1=x1, u2=u2, hpre=hpre, hid=hid, ff=ff,
                 u_next=u_next[0] if u_next else None)
    return x2, saved


def _layer_bwd(l, dx2, mod, W, P, A, hooks=None, tgt=None, nxt=None):
    hooks = hooks or {}
    sh_m, sc_m, g_m, sh_f, sc_f, g_f = [mod[l:l + 1, D_MODEL * j:D_MODEL * (j + 1)] for j in range(6)]
    n = lambda t: f"{t}{l}"
    gw, gs = {}, {}

    if isinstance(dx2, tuple):
        dres, dff, gs["ln_ff_g"], gs["ln_ff_b"], dg_f, gs["b_ff2"] = dx2
    else:
        dres, dff, gs["ln_ff_g"], gs["ln_ff_b"], dg_f, gs["b_ff2"], *loss_part = _resid_ln_bwd(
            dx2, A["x1"], A["ff"], g_f, P["ln_ff_g"][l], n("resid_ln_ff_bwd"), tgt=tgt)
    gw["w_ff2"] = _mm(A["hid"], dff, "tn", tm=512, tn=1024, out_dtype=BF16, name=n("dw_ff2"), split_n=512)
    dhpre, gs["b_ff1"] = _ff_hidden_bwd(dff, W["w_ff2"][l], A["hpre"], n("ff_hidden_bwd"))
    gw["w_ff1"] = _mm(dhpre, A["u2"], "tn", tm=512, tn=1024, out_dtype=BF16, name=n("dw_ff1"), split_n=512)

    hook = hooks["du_ff"](gw) if "du_ff" in hooks else None
    dres, dmix, dsc_f, dsh_f, gs["ln_mix_g"], gs["ln_mix_b"], dg_m, _ = _hosted(
        _mm_ln_mod_bwd, hook, dhpre, W["w_ff1"][l], A["x1"], sc_f, dres, n("du_ff"),
        nxt=(A["x"], A["mix"], g_m, P["ln_mix_g"][l]))
    gw["w_o"] = _mm(A["merged"], dmix, "tn", tm=512, tn=1024, out_dtype=BF16, name=n("dw_o"), split_n=512)
    dy_pool, dy_attn, dy_conv, dzg, gs["b_gate"] = _merge_bwd(dmix, W["w_o"][l], A["zg"], P["b_gate"][l], A["ys"],
                                                              n("merge_bwd"))

    dys = (dy_pool, dy_attn, dy_conv)
    gw["w_br_pool"], gw["w_br_attn"], gw["w_br_conv"] = _branch_dw(
        dys, (A["feat_pool"], A["o"], A["feat_conv"]), n("dw_branch"))
    dfeat_pool, do, dfeat_conv = _branch_in_bwd(
        dys, (W["w_br_pool"][l], W["w_br_attn"][l], W["w_br_conv"][l]), (F32, BF16, F32), n("d_branch_in"))

    dzp, dwp_bd, gs["pool_scale"] = _pool_bwd(dfeat_pool, A["p"], P["wp_bd"][l], P["pool_scale"][l], n("pool_bwd"))
    gs["w_pool"] = jnp.stack([dwp_bd[POOL_GROUP * g:POOL_GROUP * (g + 1), POOL_GROUP * g:POOL_GROUP * (g + 1)]
                              for g in range(len(POOL_WINDOWS))])
    hook = hooks["attn"](gw) if "attn" in hooks else None
    dq, dk, dv, ds_acc = _hosted(_attn_bwd, hook, A["qkv"], do, A["probs"], n("attn_bwd"))
    gs["rel_bias"] = _bias_block_bwd(ds_acc, n("bias_block_bwd"))
    dzc, dcw, gs["conv_b"], gs["conv_ln_g"], gs["conv_ln_b"] = _conv_bwd(
        dfeat_conv, A["cv"], A["zc"], P["conv_w"][l], P["conv_ln_g"][l], P["conv_ln_b"][l], n("conv_bwd"))
    gs["conv_w"] = dcw[:CONV_WIDTH]

    dz = [dzp, dq, dk, dv, dzc, dzg]
    gw["w_in"] = _dw_segments(dz, A["u"], n("dw_in"))
    hook = hooks["du_mix"](gw) if "du_mix" in hooks else None
    res = _hosted(_mm_ln_mod_bwd, hook, dz, W["w_in"][l], A["x"], sc_m, dres, n("du_mix"), nxt=nxt)
    if nxt is None:
        dx, dsc_m, dsh_m = res
    else:
        dx, dsc_m, dsh_m = (res[0], res[1], *res[4:]), res[2], res[3]
    dmod = jnp.concatenate([dsh_m, dsc_m, dg_m, dsh_f, dsc_f, dg_f], axis=1)
    return (dx, gw, gs, dmod) if tgt is None else (dx, gw, gs, dmod, loss_part[0])


def _small_shapes():
    return {"b_ada": (6 * D_MODEL,), "b_gate": (3 * D_MODEL,), "w_pool": (4, POOL_GROUP, POOL_GROUP),
            "pool_scale": (D_POOL,), "rel_bias": (N_HEADS, N_REL), "conv_w": (CONV_WIDTH, D_CONV),
            "conv_b": (D_CONV,), "conv_ln_g": (D_CONV,), "conv_ln_b": (D_CONV,), "ln_mix_g": (D_MODEL,),
            "ln_mix_b": (D_MODEL,), "b_ff1": (D_FF,), "b_ff2": (D_MODEL,), "ln_ff_g": (D_MODEL,), "ln_ff_b": (D_MODEL,)}


def kernel(x, c, w_ada, b_ada, w_in, b_gate, w_pool, pool_scale, rel_bias, conv_w, conv_b, conv_ln_g, conv_ln_b, w_br_pool, w_br_attn, w_br_conv, w_o, ln_mix_g, ln_mix_b, w_ff1, b_ff1, w_ff2, b_ff2, ln_ff_g, ln_ff_b, loss_target, m_w_ada, m_b_ada, m_w_in, m_b_gate, m_w_pool, m_pool_scale, m_rel_bias, m_conv_w, m_conv_b, m_conv_ln_g, m_conv_ln_b, m_w_br_pool, m_w_br_attn, m_w_br_conv, m_w_o, m_ln_mix_g, m_ln_mix_b, m_w_ff1, m_b_ff1, m_w_ff2, m_b_ff2, m_ln_ff_g, m_ln_ff_b, v_w_ada, v_b_ada, v_w_in, v_b_gate, v_w_pool, v_pool_scale, v_rel_bias, v_conv_w, v_conv_b, v_conv_ln_g, v_conv_ln_b, v_w_br_pool, v_w_br_attn, v_w_br_conv, v_w_o, v_ln_mix_g, v_ln_mix_b, v_w_ff1, v_b_ff1, v_w_ff2, v_b_ff2, v_ln_ff_g, v_ln_ff_b):
    env = dict(locals())
    xi, yi, ci = _me()
    chip = 2 * xi + yi
    me = 4 * xi + 2 * yi + ci
    xs = x[0]
    tgt = loss_target[0]
    L = DEPTH

    first = _allgather_small(jnp.concatenate([c.reshape(8, 128), _pack([conv_w]).reshape(-1, 128)]), "gather_c_conv_w")
    c_all = first[:, :8].reshape(N_DEV, D_MODEL)
    ada_cols = w_ada.shape[2]
    b_ada_sh = lax.dynamic_slice_in_dim(b_ada, chip * ada_cols, ada_cols, axis=1).reshape(L, 1, ada_cols)
    mod_part = _mod_fwd(c_all, w_ada, b_ada_sh, "mod_fwd")
    mod_g = _allgather_small(mod_part.reshape(-1, 128), "gather_mod").reshape(N_CHIP, 2, L, N_DEV, ada_cols)[:, 0]
    mod_all = jnp.transpose(mod_g, (1, 2, 0, 3)).reshape(L, N_DEV, 6 * D_MODEL)
    mod = lax.dynamic_index_in_dim(mod_all, me, axis=1, keepdims=False)

    W = {k: [None] * L for k in BIG}

    def weight_gather(*items):
        shards = [(jnp.swapaxes(env[k][l], 0, 1) if k in COL_SHARDED else env[k][l]).astype(BF16) for k, l in items]
        shards = [a.reshape(2, a.shape[0] // 2, a.shape[1]) for a in shards]

        def done(outs):
            for (k, l), g in zip(items, outs):
                W[k][l] = g.reshape(-1, g.shape[-1])

        return _gather_rider(shards), done

    branch = lambda l: [(k, l) for k in ("w_br_pool", "w_br_attn", "w_br_conv", "w_o")]
    rider, done = weight_gather(("w_in", 0))
    done(_run_rider(rider, "gather_w_in0"))
    fwd_hooks = [{"z_gate": weight_gather(*branch(0)), "attn": weight_gather(("w_ff1", 0), ("w_ff2", 0)),
                  "ff1": weight_gather(*branch(1)), "ff2": weight_gather(("w_in", 1))},
                 {"attn": weight_gather(("w_ff1", 1), ("w_ff2", 1))}]

    P = {k: env[k] for k in ("rel_bias", "conv_w")}
    for k in ("b_gate", "pool_scale", "conv_b", "conv_ln_g", "conv_ln_b", "ln_mix_g", "ln_mix_b", "b_ff1", "b_ff2",
              "ln_ff_g", "ln_ff_b"):
        P[k] = env[k].reshape(L, 1, -1)
    n_cw = conv_w.size
    cw = first[:, 8:].reshape(N_CHIP, 2, -1)[:, 0, :n_cw].reshape(N_CHIP, L, CONV_WIDTH, D_CONV // N_CHIP)
    P["conv_w"] = jnp.transpose(cw, (1, 2, 0, 3)).reshape(L, CONV_WIDTH, D_CONV)
    wp_bd = jnp.zeros((L, D_POOL, D_POOL), F32)
    for g in range(len(POOL_WINDOWS)):
        sl = slice(POOL_GROUP * g, POOL_GROUP * (g + 1))
        wp_bd = wp_bd.at[:, sl, sl].set(w_pool[:, g])
    P["wp_bd"] = wp_bd.astype(BF16)

    acts = []
    h = xs
    for l in range(L):
        h, saved = _layer_fwd(l, h, mod, W, P, fwd_hooks[l], u=acts[-1]["u_next"] if acts else None)
        acts.append(saved)

    place = jnp.stack([ci, chip, chip ^ 1, chip ^ 2, chip ^ 3]).astype(jnp.int32)
    scattered = {}

    def grad_scatter(items, tag):
        dws = [dw for _, _, dw in items]
        got = _sibling_send(dws, f"swap_blocks_{tag}", other_half=True)
        both = [hh.reshape(N_CHIP, -1, hh.shape[-1]) for hh in _sum_cores(dws, got, place, f"sum_cores_{tag}")]

        def done(outs):
            for (k, l, _), hh, r in zip(items, both, outs):
                scattered[(k, l)] = (hh, r)

        return _scatter_rider(both), done

    def scatter_hook(names, l, host):
        return lambda gw: grad_scatter([(k, l, gw[k]) for k in names], f"{host}{l}")

    gws, gss, dmods = [None] * L, [None] * L, [None] * L
    dh = h
    for l in reversed(range(L)):
        hooks = {"du_ff": scatter_hook(("w_ff2",), l, "du_ff"),
                 "attn": scatter_hook(("w_ff1", "w_o", "w_br_pool", "w_br_attn", "w_br_conv"), l, "attn_bwd"),
                 "du_mix": scatter_hook(("w_in",), l, "du_mix")}
        below = None
        if l > 0:
            below = (acts[l - 1]["x1"], acts[l - 1]["ff"], mod[l - 1:l, 5 * D_MODEL:], P["ln_ff_g"][l - 1])
        if l == L - 1:
            dh, gws[l], gss[l], dmods[l], loss_part = _layer_bwd(l, dh, mod, W, P, acts[l], hooks, tgt=tgt, nxt=below)
        else:
            dh, gws[l], gss[l], dmods[l] = _layer_bwd(l, dh, mod, W, P, acts[l], hooks, nxt=below)
    grad_x = dh[None]

    reduced = [[_sum_chips(*scattered[(k, l)], place, f"sum_chips_{k}{l}") for l in range(L)] for k in BIG]
    flat_reduced = [t for per_weight in reduced for t in per_weight]

    shapes = _small_shapes()
    small_names = [k for k in SMALL if k != "b_ada"]
    dmod_own = jnp.concatenate(dmods, axis=0)
    pack = _pack([dmod_own] + [jnp.stack([gss[l][k].reshape(shapes[k]) for l in range(L)]) for k in small_names]
                 + [loss_part])
    last = _run_rider(_join_riders(_sibling_rider(flat_reduced), _allgather_rider(pack.reshape(-1, 128))),
                      "swap_reduced_gather_small")
    flat_other, g_all = last[:-1], last[-1].reshape(N_DEV, -1, PACK_W)

    out = {}
    for j, k in enumerate(BIG):
        own, other = reduced[j], flat_other[L * j:L * (j + 1)]
        if k == "w_in":
            t = lambda a: jnp.swapaxes(a, 1, 2)
            res = _adamw_halves(t(env[k]), t(env["m_" + k]), t(env["v_" + k]), own, other, place, "cols", f"adamw_{k}")
            res = [t(a) for a in res]
        else:
            if k in COL_SHARDED:
                own, other = [a.T for a in own], [a.T for a in other]
            res = _adamw_halves(env[k], env["m_" + k], env["v_" + k], own, other, place,
                                "rows" if k in COL_SHARDED else "cols", f"adamw_{k}")
        out[k] = tuple(res)

    dmod_all = g_all[:, :L * 6].reshape(N_DEV, L, 6 * D_MODEL)
    dmod_sh = jnp.transpose(lax.dynamic_slice_in_dim(dmod_all, chip * ada_cols, ada_cols, axis=2), (1, 0, 2))
    g_ada = _mod_bwd(c_all, dmod_sh, "mod_bwd")
    g_, d_, m_, v_ = _adamw(w_ada.reshape(-1, ada_cols), m_w_ada.reshape(-1, ada_cols), v_w_ada.reshape(-1, ada_cols),
                            [g_ada.reshape(-1, ada_cols)], "adamw_w_ada")
    out["w_ada"] = tuple(a.reshape(w_ada.shape) for a in (g_, d_, m_, v_))

    def small_pack(prefix):
        parts = [env[prefix + "b_ada"]]
        for k in small_names:
            a = env[prefix + k]
            if k == "conv_w":
                a = jnp.zeros((L,) + shapes[k], F32)
            parts.append(a)
        return _pack(parts + [jnp.zeros_like(loss_part)])

    gp, dp, mp, vp = _adamw_small(small_pack(""), small_pack("m_"), small_pack("v_"), g_all, "adamw_small")
    full_shapes = [(L,) + shapes["b_ada"]] + [(L,) + shapes[k] for k in small_names]
    loss = _unpack(gp, full_shapes + [(128,)])[-1][0]
    for tag, packed in (("g", gp), ("d", dp), ("m", mp), ("v", vp)):
        for k, a in zip(["b_ada"] + small_names, _unpack(packed, full_shapes)):
            out.setdefault(k, {})
            out[k][tag] = a
    g_cw_full = out["conv_w"]["g"]
    cw_cols = D_CONV // N_CHIP
    g_cw = lax.dynamic_slice_in_dim(g_cw_full, chip * cw_cols, cw_cols, axis=2)
    pad_rows = lambda a: jnp.pad(a.reshape(L * CONV_WIDTH, cw_cols), ((0, 2), (0, 0)))
    g_, d_, m_, v_ = _adamw(pad_rows(conv_w), pad_rows(m_conv_w), pad_rows(v_conv_w), [pad_rows(g_cw)], "adamw_conv_w")
    out["conv_w"] = tuple(a[:L * CONV_WIDTH].reshape(L, CONV_WIDTH, cw_cols) for a in (g_, d_, m_, v_))

    names = ["w_ada", "b_ada", "w_in", "b_gate", "w_pool", "pool_scale", "rel_bias", "conv_w", "conv_b", "conv_ln_g",
             "conv_ln_b", "w_br_pool", "w_br_attn", "w_br_conv", "w_o", "ln_mix_g", "ln_mix_b", "w_ff1", "b_ff1",
             "w_ff2", "b_ff2", "ln_ff_g", "ln_ff_b"]

    def pick(k, i):
        o = out[k]
        return o[i] if isinstance(o, tuple) else o["gdmv"[i]].reshape(env[k].shape)

    return (loss, grad_x, *[pick(k, 0) for k in names], *[pick(k, 1) for k in names],
            *[pick(k, 2) for k in names], *[pick(k, 3) for k in names])
```

```python
import jax
import jax.numpy as jnp
import numpy as np
from jax import lax
from jax.experimental import pallas as pl
from jax.experimental.pallas import tpu as pltpu

F32 = jnp.float32
BF16 = jnp.bfloat16

D_MODEL = 1024
DEPTH = 2
CHUNK = 64
POOL_WINDOWS = (2, 4, 8, 16)
POOL_GROUP = 64
D_POOL = 256
N_HEADS = 8
HEAD_DIM = 64
D_ATTN = 512
N_PREV_CHUNKS = 8
REL_CLIP = 128
N_REL = 2 * REL_CLIP + 1
D_CONV = 256
CONV_WIDTH = 31
D_FF = 4 * D_MODEL
D_IN = 5376
OFF_POOL, OFF_QKV, OFF_CONV, OFF_GATE = 0, 256, 1792, 2304
ALPHA = (2.0 * DEPTH) ** 0.25
LN_EPS = 1e-5
NEG_INF = -1e30
ADAM_LR, ADAM_B1, ADAM_B2, ADAM_EPS, ADAM_WD, ADAM_STEP = 0.001, 0.9, 0.999, 1e-08, 0.01, 10

N_DEV = 8
N_CHIP = 4
MESH = pl.DeviceIdType.MESH

QB = 2 * CHUNK
KPAD = N_PREV_CHUNKS * CHUNK
KW = QB + KPAD
SKEW_W = 768

VMEM_LIMIT = 56 * 1024 * 1024


def _cparams(**kw):
    return pltpu.CompilerParams(vmem_limit_bytes=VMEM_LIMIT, **kw)


def _full(shape):
    n = len(shape)
    return pl.BlockSpec(shape, lambda *_: (0,) * n)


_DIMS = {"nn": (((1,), (0,)), ((), ())), "nt": (((1,), (1,)), ((), ())), "tn": (((0,), (0,)), ((), ()))}


def _relu2(t):
    r = jnp.maximum(t, 0.0)
    return r * r


def _mm(a, b, mode, *, tm, tn, out_dtype, name, b_col0=0, n_out=None, bias=None, split_n=0, rider=None):
    if mode == "tn":
        k, m = a.shape
        n = b.shape[1] if n_out is None else n_out
        a_spec = pl.BlockSpec((k, tm), lambda i, j: (0, i))
        b_spec = pl.BlockSpec((k, tn), lambda i, j: (0, j + b_col0))
    elif mode == "nn":
        m, k = a.shape
        n = b.shape[1] if n_out is None else n_out
        a_spec = pl.BlockSpec((tm, k), lambda i, j: (i, 0))
        b_spec = pl.BlockSpec((k, tn), lambda i, j: (0, j + b_col0))
    else:
        m, k = a.shape
        n = b.shape[0] if n_out is None else n_out
        a_spec = pl.BlockSpec((tm, k), lambda i, j: (i, 0))
        b_spec = pl.BlockSpec((tn, k), lambda i, j: (j + b_col0, 0))
    assert m % tm == 0 and n % tn == 0, (name, m, n, tm, tn)
    dims = _DIMS[mode]

    def body(*refs):
        if bias is None:
            a_ref, b_ref, o_ref = refs
        else:
            a_ref, b_ref, bias_ref, o_ref = refs
        acc = lax.dot_general(a_ref[...].astype(BF16), b_ref[...].astype(BF16), dims, preferred_element_type=F32)
        if bias is not None:
            acc = acc + bias_ref[...]
        if split_n:
            for c in range(tn // split_n):
                o_ref[c] = acc[:, c * split_n:(c + 1) * split_n].astype(out_dtype)
        else:
            o_ref[...] = acc.astype(out_dtype)

    in_specs = [a_spec, b_spec]
    args = [a, b]
    if bias is not None:
        in_specs.append(pl.BlockSpec((1, tn), lambda i, j: (0, j)))
        args.append(bias)
    if split_n:
        out_spec = pl.BlockSpec((tn // split_n, tm, split_n), lambda i, j: (j, i, 0))
        out_shape = jax.ShapeDtypeStruct((n // split_n, m, split_n), out_dtype)
    else:
        out_spec = pl.BlockSpec((tm, tn), lambda i, j: (i, j))
        out_shape = jax.ShapeDtypeStruct((m, n), out_dtype)
    res = _call(body, name=name, grid=(m // tm, n // tn), in_specs=in_specs, out_specs=[out_spec],
                out_shape=[out_shape], scratch_shapes=[], args=args, rider=rider)
    return res[0] if rider is None else (res[0][0], res[1])


def _ln_hat(x):
    mu = jnp.mean(x, axis=-1, keepdims=True)
    xc = x - mu
    var = jnp.mean(xc * xc, axis=-1, keepdims=True)
    rstd = lax.rsqrt(var + LN_EPS)
    return xc * rstd, rstd


def _ln_hat_bwd(dhat, xhat, rstd):
    m1 = jnp.mean(dhat, axis=-1, keepdims=True)
    m2 = jnp.mean(dhat * xhat, axis=-1, keepdims=True)
    return rstd * (dhat - m1 - xhat * m2)


def _row_tile(s):
    return min(512, s)


def _acc_rows(ref, val, first):
    @pl.when(first)
    def _():
        ref[...] = jnp.zeros_like(ref)
    ref[...] += jnp.sum(val, axis=0, keepdims=True)


def _ln_mod(x, sc, sh, name):
    s, d = x.shape
    tm = _row_tile(s)

    def body(x_ref, sc_ref, sh_ref, u_ref):
        xhat, _ = _ln_hat(x_ref[...])
        u_ref[...] = (xhat * (1.0 + sc_ref[...]) + sh_ref[...]).astype(BF16)

    row = pl.BlockSpec((tm, d), lambda i: (i, 0))
    vec = pl.BlockSpec((1, d), lambda i: (0, 0))
    return pl.pallas_call(body, grid=(s // tm,), in_specs=[row, vec, vec], out_specs=row,
                          out_shape=jax.ShapeDtypeStruct((s, d), BF16), name=name, compiler_params=_cparams())(x, sc, sh)


def _resid_bwd_tile(dxo, x, f, g, gam):
    rhat, rstd = _ln_hat(ALPHA * x + g * f)
    dr = _ln_hat_bwd(dxo * gam, rhat, rstd)
    return ALPHA * dr, g * dr, dxo * rhat, dr * f


def _mm_ln_mod_bwd(a, b, x, sc, dres, name, rider=None, nxt=None):
    segs = list(a) if isinstance(a, (list, tuple)) else [a]
    s = segs[0].shape[0]
    k, d = b.shape
    assert sum(t.shape[1] for t in segs) == k
    tm = min(512 if k <= 4096 and nxt is None else 256, s)
    ns = len(segs)

    def body(*refs):
        seg_refs = refs[:ns]
        if nxt is None:
            b_ref, x_ref, sc_ref, dres_ref, dx_ref, dsc_ref, dsh_ref = refs[ns:]
        else:
            (b_ref, x_ref, sc_ref, dres_ref, xp_ref, fp_ref, gp_ref, gamp_ref,
             dresp_ref, dfp_ref, dsc_ref, dsh_ref, dgam_ref, dbet_ref, dg_ref, dbias_ref) = refs[ns:]
        first = pl.program_id(0) == 0
        duv, r0 = None, 0
        for seg_ref in seg_refs:
            w = seg_ref.shape[1]
            part = jnp.dot(seg_ref[...], b_ref[r0:r0 + w, :], preferred_element_type=F32)
            duv = part if duv is None else duv + part
            r0 += w
        xhat, rstd = _ln_hat(x_ref[...])
        dxv = dres_ref[...] + _ln_hat_bwd(duv * (1.0 + sc_ref[...]), xhat, rstd)
        _acc_rows(dsc_ref, duv * xhat, first)
        _acc_rows(dsh_ref, duv, first)
        if nxt is None:
            dx_ref[...] = dxv
        else:
            dresp, dfp, t_gam, t_g = _resid_bwd_tile(dxv, xp_ref[...], fp_ref[...], gp_ref[...], gamp_ref[...])
            dresp_ref[...] = dresp
            dfp_ref[...] = dfp.astype(BF16)
            _acc_rows(dgam_ref, t_gam, first)
            _acc_rows(dbet_ref, dxv, first)
            _acc_rows(dg_ref, t_g, first)
            _acc_rows(dbias_ref, dfp, first)

    row = pl.BlockSpec((tm, d), lambda i: (i, 0))
    vec = pl.BlockSpec((1, d), lambda i: (0, 0))
    vs = jax.ShapeDtypeStruct((1, d), F32)
    rows = jax.ShapeDtypeStruct((s, d), F32)
    in_specs = [pl.BlockSpec((tm, t.shape[1]), lambda i: (i, 0)) for t in segs] + [_full((k, d)), row, vec, row]
    args = (*segs, b, x, sc, dres)
    if nxt is None:
        out_specs, out_shape = [row, vec, vec], [rows, vs, vs]
    else:
        in_specs += [row, row, vec, vec]
        args += tuple(nxt)
        out_specs = [row, row] + [vec] * 6
        out_shape = [rows, jax.ShapeDtypeStruct((s, d), BF16)] + [vs] * 6
    res = _call(body, name=name, grid=(s // tm,), in_specs=in_specs, out_specs=out_specs, out_shape=out_shape,
                scratch_shapes=[], args=args, rider=rider)
    return tuple(res) if rider is None else (tuple(res[0]), res[1])


def _dw_segments(segs, u, name):
    s, d = u.shape
    tw = 256
    tiles = [t.shape[1] // tw for t in segs]
    starts = [sum(tiles[:j]) for j in range(len(segs))]
    ns = len(segs)

    def body(*refs):
        seg_refs, u_ref, o_ref = refs[:ns], refs[ns], refs[ns + 1]
        i = pl.program_id(0)
        for seg_ref, t0, nt in zip(seg_refs, starts, tiles):
            @pl.when((i >= t0) & (i < t0 + nt))
            def _(seg_ref=seg_ref):
                acc = lax.dot_general(seg_ref[...], u_ref[...], _DIMS["tn"], preferred_element_type=F32)
                o_ref[0] = acc[:, :d // 2].astype(BF16)
                o_ref[1] = acc[:, d // 2:].astype(BF16)

    def seg_spec(t0, nt):
        return pl.BlockSpec((s, tw), lambda i: (0, jnp.clip(i - t0, 0, nt - 1)))

    return pl.pallas_call(
        body, grid=(sum(tiles),), in_specs=[seg_spec(t0, nt) for t0, nt in zip(starts, tiles)] + [_full((s, d))],
        out_specs=pl.BlockSpec((2, tw, d // 2), lambda i: (0, i, 0)),
        out_shape=jax.ShapeDtypeStruct((2, sum(tiles) * tw, d // 2), BF16), name=name, compiler_params=_cparams(),
    )(*segs, u)


def _mm_resid_ln(a, b, bias, x, g, gam, bet, name, rider=None, mod_next=None):
    s, k = a.shape
    d = b.shape[1]
    tm = min(512, s)
    nb, nm = int(bias is not None), 2 * int(mod_next is not None)

    def body(*refs):
        a_ref, b_ref = refs[:2]
        x_ref, g_ref, gam_ref, bet_ref = refs[2 + nb:6 + nb]
        f_ref, o_ref = refs[6 + nb + nm:8 + nb + nm]
        f = jnp.dot(a_ref[...], b_ref[...], preferred_element_type=F32)
        if bias is not None:
            f = f + refs[2][...]
        f_ref[...] = f
        rhat, _ = _ln_hat(ALPHA * x_ref[...] + g_ref[...] * f)
        y = rhat * gam_ref[...] + bet_ref[...]
        o_ref[...] = y
        if mod_next is not None:
            sc_ref, sh_ref = refs[6 + nb:8 + nb]
            yhat, _ = _ln_hat(y)
            refs[8 + nb + nm][...] = (yhat * (1.0 + sc_ref[...]) + sh_ref[...]).astype(BF16)

    row = pl.BlockSpec((tm, d), lambda i: (i, 0))
    vec = pl.BlockSpec((1, d), lambda i: (0, 0))
    in_specs = [pl.BlockSpec((tm, k), lambda i: (i, 0)), _full((k, d))] + [vec] * nb + [row, vec, vec, vec] + [vec] * nm
    args = [a, b] + ([bias] if nb else []) + [x, g, gam, bet] + (list(mod_next) if nm else [])
    sh = jax.ShapeDtypeStruct((s, d), F32)
    out_specs, out_shape = [row, row], [sh, sh]
    if nm:
        out_specs, out_shape = out_specs + [row], out_shape + [jax.ShapeDtypeStruct((s, d), BF16)]
    res = _call(body, name=name, grid=(s // tm,), in_specs=in_specs, out_specs=out_specs, out_shape=out_shape,
                scratch_shapes=[], args=args, rider=rider)
    return tuple(res) if rider is None else (tuple(res[0]), res[1])


def _resid_ln_bwd(dxo, x, f, g, gam, name, tgt=None):
    s, d = x.shape
    tm = _row_tile(s)
    n = s // tm

    def body(*refs):
        if tgt is None:
            dxo_ref, x_ref, f_ref, g_ref, gam_ref, dres_ref, df_ref, dgam_ref, dbet_ref, dg_ref, dbias_ref = refs
            dxov = dxo_ref[...]
        else:
            (dxo_ref, t_ref, x_ref, f_ref, g_ref, gam_ref, dres_ref, df_ref, dgam_ref, dbet_ref, dg_ref, dbias_ref,
             loss_ref, sq_ref) = refs
            err = dxo_ref[...] - t_ref[...]
            dxov = err * (1.0 / d)
            _acc_rows(sq_ref, err * err, pl.program_id(0) == 0)

            @pl.when(pl.program_id(0) == n - 1)
            def _():
                tot = jnp.sum(sq_ref[...], axis=1, keepdims=True) * (0.5 / d)
                loss_ref[...] = jnp.broadcast_to(tot, (1, 128))

        first = pl.program_id(0) == 0
        dres, dfv, t_gam, t_g = _resid_bwd_tile(dxov, x_ref[...], f_ref[...], g_ref[...], gam_ref[...])
        dres_ref[...] = dres
        df_ref[...] = dfv.astype(BF16)
        _acc_rows(dgam_ref, t_gam, first)
        _acc_rows(dbet_ref, dxov, first)
        _acc_rows(dg_ref, t_g, first)
        _acc_rows(dbias_ref, dfv, first)

    row = pl.BlockSpec((tm, d), lambda i: (i, 0))
    vec = pl.BlockSpec((1, d), lambda i: (0, 0))
    vs = jax.ShapeDtypeStruct((1, d), F32)
    out_specs = [row, row, vec, vec, vec, vec]
    out_shape = [jax.ShapeDtypeStruct((s, d), F32), jax.ShapeDtypeStruct((s, d), BF16), vs, vs, vs, vs]
    if tgt is None:
        return pl.pallas_call(body, grid=(n,), in_specs=[row, row, row, vec, vec], out_specs=out_specs,
                              out_shape=out_shape, name=name, compiler_params=_cparams())(dxo, x, f, g, gam)
    return pl.pallas_call(body, grid=(n,), in_specs=[row, row, row, row, vec, vec],
                          out_specs=out_specs + [pl.BlockSpec((1, 128), lambda i: (0, 0))],
                          out_shape=out_shape + [jax.ShapeDtypeStruct((1, 128), F32)],
                          scratch_shapes=[pltpu.VMEM((1, d), F32)], name=name,
                          compiler_params=_cparams())(dxo, tgt, x, f, g, gam)


POOL_HALO = 16
POOL_ROWS = 256


def _pool_counts(r0, rows):
    t1 = (lax.broadcasted_iota(jnp.int32, (rows, 128), 0) + r0 + 1).astype(F32)
    low = lax.broadcasted_iota(jnp.int32, (rows, 128), 1) < POOL_GROUP
    wa = jnp.where(low, float(POOL_WINDOWS[0]), float(POOL_WINDOWS[1]))
    wb = jnp.where(low, float(POOL_WINDOWS[2]), float(POOL_WINDOWS[3]))
    return jnp.minimum(t1, wa), jnp.minimum(t1, wb), low


def _window_sums(win, off, rows, sign):
    def sl(j, half):
        return win[off + sign * j: off + sign * j + rows, 128 * half:128 * half + 128]
    a2 = sl(0, 0) + sl(1, 0)
    a4 = a2 + sl(2, 0) + sl(3, 0)
    a8 = sl(0, 1)
    for j in range(1, 8):
        a8 = a8 + sl(j, 1)
    a16 = a8
    for j in range(8, 16):
        a16 = a16 + sl(j, 1)
    return a2, a4, a8, a16


def _pool_fwd(zp, wp_bd, pscale, name):
    s = zp.shape[0]
    r = min(POOL_ROWS, s)

    def body(z_ref, wp_ref, sc_ref, p_ref, feat_ref, pad):
        pad[0:POOL_HALO, :] = jnp.zeros((POOL_HALO, D_POOL), F32)
        pad[POOL_HALO:, :] = z_ref[...]

        def step(i, carry):
            r0 = pl.multiple_of(i * r, r)
            win = pad[pl.ds(r0, r + POOL_HALO), :]
            a2, a4, a8, a16 = _window_sums(win, POOL_HALO, r, -1)
            ca, cb, low = _pool_counts(r0, r)
            x0 = win[POOL_HALO:, :]
            pa = jnp.where(low, a2, a4) / ca
            pb = jnp.where(low, a8, a16) / cb
            p = (jnp.concatenate([pa, pb], axis=1) - x0).astype(BF16)
            p_ref[pl.ds(r0, r), :] = p
            pw = jnp.dot(p, wp_ref[...], preferred_element_type=F32)
            feat_ref[pl.ds(r0, r), :] = (pw * sc_ref[...]).astype(BF16)
            return carry

        lax.fori_loop(0, s // r, step, 0)

    return pl.pallas_call(
        body, out_shape=[jax.ShapeDtypeStruct((s, D_POOL), BF16), jax.ShapeDtypeStruct((s, D_POOL), BF16)],
        scratch_shapes=[pltpu.VMEM((s + POOL_HALO, D_POOL), F32)], name=name, compiler_params=_cparams(),
    )(zp, wp_bd, pscale)


def _pool_bwd(dfeat, p, wp_bd, pscale, name):
    s = p.shape[0]
    r = min(POOL_ROWS, s)

    def body(df_ref, p_ref, wp_ref, sc_ref, dz_ref, dwp_ref, dsc_ref, gpad, dpbuf):
        dwp_ref[...] = jnp.zeros_like(dwp_ref)
        dsc_ref[...] = jnp.zeros_like(dsc_ref)
        gpad[s:, :] = jnp.zeros((POOL_HALO, D_POOL), F32)

        def step1(i, carry):
            r0 = pl.multiple_of(i * r, r)
            pv = p_ref[pl.ds(r0, r), :]
            dfv = df_ref[pl.ds(r0, r), :]
            pw = jnp.dot(pv, wp_ref[...], preferred_element_type=F32)
            dsc_ref[...] += jnp.sum(dfv * pw, axis=0, keepdims=True)
            dpw = (dfv * sc_ref[...]).astype(BF16)
            dwp_ref[...] += lax.dot_general(pv, dpw, _DIMS["tn"], preferred_element_type=F32)
            dp = lax.dot_general(dpw, wp_ref[...], _DIMS["nt"], preferred_element_type=F32)
            ca, cb, _ = _pool_counts(r0, r)
            gpad[pl.ds(r0, r), :] = dp / jnp.concatenate([ca, cb], axis=1)
            dpbuf[pl.ds(r0, r), :] = dp
            return carry

        lax.fori_loop(0, s // r, step1, 0)

        def step2(i, carry):
            r0 = pl.multiple_of(i * r, r)
            win = gpad[pl.ds(r0, r + POOL_HALO), :]
            a2, a4, a8, a16 = _window_sums(win, 0, r, 1)
            low = lax.broadcasted_iota(jnp.int32, (r, 128), 1) < POOL_GROUP
            acc = jnp.concatenate([jnp.where(low, a2, a4), jnp.where(low, a8, a16)], axis=1)
            dz_ref[pl.ds(r0, r), :] = (acc - dpbuf[pl.ds(r0, r), :]).astype(BF16)
            return carry

        lax.fori_loop(0, s // r, step2, 0)

    return pl.pallas_call(
        body,
        out_shape=[jax.ShapeDtypeStruct((s, D_POOL), BF16), jax.ShapeDtypeStruct((D_POOL, D_POOL), F32),
                   jax.ShapeDtypeStruct((1, D_POOL), F32)],
        scratch_shapes=[pltpu.VMEM((s + POOL_HALO, D_POOL), F32), pltpu.VMEM((s, D_POOL), F32)],
        name=name, compiler_params=_cparams(),
    )(dfeat, p, wp_bd, pscale)


def _skew_index():
    cp = lax.broadcasted_iota(jnp.int32, (SKEW_W, N_REL), 0)
    dist = jnp.where(cp < KW, KPAD - cp, KPAD + SKEW_W - cp)
    idx = jnp.clip(dist, -REL_CLIP, REL_CLIP) + REL_CLIP
    return (idx == lax.broadcasted_iota(jnp.int32, (SKEW_W, N_REL), 1)).astype(F32)


def _row_bits(b):
    return (lax.broadcasted_iota(jnp.int32, (QB, SKEW_W), 0) >> b) & 1 == 1


N_EDGE = KPAD // QB


def _bias_block(rel_bias, name):
    def body(rb_ref, o_ref):
        onehot = _skew_index()
        row0 = lax.dot_general(rb_ref[...], onehot, _DIMS["nt"], precision=lax.Precision.HIGHEST,
                               preferred_element_type=F32)
        r = lax.broadcasted_iota(jnp.int32, (QB, KW), 0)
        kk = lax.broadcasted_iota(jnp.int32, (QB, KW), 1)
        cq, ck = r // CHUNK, kk // CHUNK
        band = (ck >= cq) & (ck <= cq + N_PREV_CHUNKS)
        for h in range(N_HEADS):
            t = jnp.broadcast_to(row0[h:h + 1, :], (QB, SKEW_W))
            for b in range(7):
                t = jnp.where(_row_bits(b), pltpu.roll(t, 1 << b, 1), t)
            for e in range(N_EDGE + 1):
                o_ref[e, h] = jnp.where(band & (kk >= KPAD - e * QB), t[:, :KW], NEG_INF)

    return pl.pallas_call(body, out_shape=jax.ShapeDtypeStruct((N_EDGE + 1, N_HEADS, QB, KW), F32), name=name,
                          compiler_params=_cparams())(rel_bias)


def _bias_spec():
    return pl.BlockSpec((None, N_HEADS, QB, KW), lambda i: (jnp.minimum(i, N_EDGE), 0, 0, 0))


def _bias_block_bwd(ds_acc, name):
    def body(ds_ref, o_ref):
        sums = []
        for h in range(N_HEADS):
            t = jnp.concatenate([ds_ref[h], jnp.zeros((QB, SKEW_W - KW), F32)], axis=1)
            for b in range(7):
                t = jnp.where(_row_bits(b), pltpu.roll(t, SKEW_W - (1 << b), 1), t)
            sums.append(jnp.sum(t, axis=0, keepdims=True))
        allh = jnp.concatenate(sums, axis=0)
        o_ref[...] = jnp.dot(allh, _skew_index(), precision=lax.Precision.HIGHEST, preferred_element_type=F32)

    return pl.pallas_call(body, out_shape=jax.ShapeDtypeStruct((N_HEADS, N_REL), F32), name=name,
                          compiler_params=_cparams())(ds_acc)


def _scaled(q):
    return (q.astype(F32) * (HEAD_DIM ** -0.5)).astype(BF16)


def _probs(q, kw, bias_ref):
    sc = jnp.stack([lax.dot_general(q[:, HEAD_DIM * h:HEAD_DIM * (h + 1)], kw[:, HEAD_DIM * h:HEAD_DIM * (h + 1)],
                                    _DIMS["nt"], preferred_element_type=F32) + bias_ref[h] for h in range(N_HEADS)])
    e = jnp.exp(sc - jnp.max(sc, axis=-1, keepdims=True))
    return e * (1.0 / jnp.sum(e, axis=-1, keepdims=True))


def _load_padded_kv(qkv_hbm, kpad, vpad, sems, s):
    kpad[0:KPAD, :] = jnp.zeros((KPAD, D_ATTN), BF16)
    vpad[0:KPAD, :] = jnp.zeros((KPAD, D_ATTN), BF16)
    ck = pltpu.make_async_copy(qkv_hbm.at[:, D_ATTN:2 * D_ATTN], kpad.at[pl.ds(KPAD, s), :], sems.at[0])
    cv = pltpu.make_async_copy(qkv_hbm.at[:, 2 * D_ATTN:3 * D_ATTN], vpad.at[pl.ds(KPAD, s), :], sems.at[1])
    ck.start()
    cv.start()
    ck.wait()
    cv.wait()


def _attn_fwd(qkv, bias, name, rider=None):
    s = qkv.shape[0]

    def body(q_ref, qkv_hbm, bias_ref, o_ref, p_ref, kpad, vpad, sems):
        i = pl.program_id(0)

        @pl.when(i == 0)
        def _():
            _load_padded_kv(qkv_hbm, kpad, vpad, sems, s)

        base = pl.multiple_of(i * QB, QB)
        kw = kpad[pl.ds(base, KW), :]
        vw = vpad[pl.ds(base, KW), :]
        q = _scaled(q_ref[...])
        p = _probs(q, kw, bias_ref).astype(BF16)
        p_ref[...] = p
        outs = [jnp.dot(p[h], vw[:, HEAD_DIM * h:HEAD_DIM * (h + 1)], preferred_element_type=F32)
                for h in range(N_HEADS)]
        o_ref[...] = jnp.concatenate(outs, axis=1).astype(BF16)

    res = _call(
        body, name=name, grid=(s // QB,),
        in_specs=[pl.BlockSpec((QB, D_ATTN), lambda i: (i, 0)), pl.BlockSpec(memory_space=pl.ANY),
                  _bias_spec()],
        out_specs=[pl.BlockSpec((QB, D_ATTN), lambda i: (i, 0)), _probs_spec()],
        out_shape=[jax.ShapeDtypeStruct((s, D_ATTN), BF16), jax.ShapeDtypeStruct((N_HEADS, s, KW), BF16)],
        scratch_shapes=[pltpu.VMEM((s + KPAD, D_ATTN), BF16), pltpu.VMEM((s + KPAD, D_ATTN), BF16),
                        pltpu.SemaphoreType.DMA((2,))],
        args=(qkv, qkv, bias), rider=rider)
    return tuple(res) if rider is None else (tuple(res[0]), res[1])


def _probs_spec():
    return pl.BlockSpec((N_HEADS, QB, KW), lambda i: (0, i, 0))


def _attn_bwd(qkv, do, probs, name, rider=None):
    s = qkv.shape[0]
    n = s // QB

    def body(q_ref, qkv_hbm, do_ref, p_ref, dq_ref, dk_hbm, dv_hbm, ds_ref, kpad, vpad, dkacc, dvacc, sems):
        i = pl.program_id(0)

        @pl.when(i == 0)
        def _():
            _load_padded_kv(qkv_hbm, kpad, vpad, sems, s)
            dkacc[...] = jnp.zeros_like(dkacc)
            dvacc[...] = jnp.zeros_like(dvacc)
            ds_ref[...] = jnp.zeros_like(ds_ref)

        base = pl.multiple_of(i * QB, QB)
        kw = kpad[pl.ds(base, KW), :]
        vw = vpad[pl.ds(base, KW), :]
        q = _scaled(q_ref[...])
        dov = do_ref[...]
        heads = [slice(HEAD_DIM * h, HEAD_DIM * (h + 1)) for h in range(N_HEADS)]
        pb = p_ref[...]
        p = pb.astype(F32)
        dp = jnp.stack([lax.dot_general(dov[:, hs], vw[:, hs], _DIMS["nt"], preferred_element_type=F32) for hs in heads])
        ds = p * (dp - jnp.sum(dp * p, axis=-1, keepdims=True))
        ds_ref[...] += ds
        dsb = ds.astype(BF16)
        dvs = [lax.dot_general(pb[h], dov[:, hs], _DIMS["tn"], preferred_element_type=F32) for h, hs in enumerate(heads)]
        dqs = [jnp.dot(dsb[h], kw[:, hs], preferred_element_type=F32) for h, hs in enumerate(heads)]
        dks = [lax.dot_general(dsb[h], q[:, hs], _DIMS["tn"], preferred_element_type=F32) for h, hs in enumerate(heads)]
        dq_ref[...] = (jnp.concatenate(dqs, axis=1) * (HEAD_DIM ** -0.5)).astype(BF16)
        dkacc[pl.ds(base, KW), :] += jnp.concatenate(dks, axis=1)
        dvacc[pl.ds(base, KW), :] += jnp.concatenate(dvs, axis=1)

        @pl.when(i == n - 1)
        def _():
            def cast(j, carry):
                rows = pl.ds(pl.multiple_of(KPAD + j * 512, 512), 512)
                kpad[rows, :] = dkacc[rows, :].astype(BF16)
                vpad[rows, :] = dvacc[rows, :].astype(BF16)
                return carry

            lax.fori_loop(0, s // 512, cast, 0)
            ck = pltpu.make_async_copy(kpad.at[pl.ds(KPAD, s), :], dk_hbm, sems.at[0])
            cv = pltpu.make_async_copy(vpad.at[pl.ds(KPAD, s), :], dv_hbm, sems.at[1])
            ck.start()
            cv.start()
            ck.wait()
            cv.wait()

    blk = pl.BlockSpec((QB, D_ATTN), lambda i: (i, 0))
    acc_shape = jax.ShapeDtypeStruct((s, D_ATTN), BF16)
    return _call(
        body, name=name, grid=(n,),
        in_specs=[blk, pl.BlockSpec(memory_space=pl.ANY), blk, _probs_spec()],
        out_specs=[blk, pl.BlockSpec(memory_space=pl.ANY), pl.BlockSpec(memory_space=pl.ANY), _full((N_HEADS, QB, KW))],
        out_shape=[jax.ShapeDtypeStruct((s, D_ATTN), BF16), acc_shape, acc_shape,
                   jax.ShapeDtypeStruct((N_HEADS, QB, KW), F32)],
        scratch_shapes=[pltpu.VMEM((s + KPAD, D_ATTN), BF16), pltpu.VMEM((s + KPAD, D_ATTN), BF16),
                        pltpu.VMEM((s + KPAD, D_ATTN), F32), pltpu.VMEM((s + KPAD, D_ATTN), F32),
                        pltpu.SemaphoreType.DMA((2,))],
        args=(qkv, qkv, do, probs), rider=rider)


CONV_HALO = 32
CONV_ROWS = 64


def _sigmoid(t):
    return 1.0 / (1.0 + jnp.exp(-t))


CONV_WIN = CONV_ROWS + CONV_HALO - 8


def _row_windows(ref, r0, buf):
    win = ref[pl.ds(r0, CONV_ROWS + CONV_HALO), :]
    for j in range(1, 8):
        buf[j - 1] = win[j:j + CONV_WIN, :]

    def get(o):
        j, a = o % 8, o - o % 8
        if j == 0:
            return ref[pl.ds(r0 + a, CONV_ROWS), :]
        return buf[j - 1, a:a + CONV_ROWS, :]

    return get


def _glu_rows(z_ref, r0, rows):
    a = z_ref[pl.ds(r0, rows), 0:D_CONV]
    b = z_ref[pl.ds(r0, rows), D_CONV:2 * D_CONV]
    return a, _sigmoid(b)


def _conv_fwd(zc, conv_w, conv_b, ln_g, ln_b, name):
    s = zc.shape[0]
    rt = min(256, s)

    def body(z_ref, w_ref, cb_ref, g_ref, b_ref, cv_ref, feat_ref, hpad, shifts):
        hpad[0:CONV_HALO, :] = jnp.zeros((CONV_HALO, D_CONV), F32)

        def glu(i, carry):
            r0 = pl.multiple_of(i * rt, rt)
            a, sb = _glu_rows(z_ref, r0, rt)
            hpad[pl.ds(r0 + CONV_HALO, rt), :] = a * sb
            return carry

        lax.fori_loop(0, s // rt, glu, 0)
        w = w_ref[...]

        def conv(i, carry):
            r0 = pl.multiple_of(i * CONV_ROWS, CONV_ROWS)
            win = _row_windows(hpad, r0, shifts)
            acc = jnp.broadcast_to(cb_ref[...], (CONV_ROWS, D_CONV))
            for k in range(CONV_WIDTH):
                acc = acc + win(2 + k) * w[k:k + 1, :]
            cv_ref[pl.ds(r0, CONV_ROWS), :] = acc
            yhat, _ = _ln_hat(acc)
            y = yhat * g_ref[...] + b_ref[...]
            feat_ref[pl.ds(r0, CONV_ROWS), :] = (y * _sigmoid(y)).astype(BF16)
            return carry

        lax.fori_loop(0, s // CONV_ROWS, conv, 0)

    return pl.pallas_call(
        body, out_shape=[jax.ShapeDtypeStruct((s, D_CONV), F32), jax.ShapeDtypeStruct((s, D_CONV), BF16)],
        scratch_shapes=[pltpu.VMEM((s + CONV_HALO, D_CONV), F32), pltpu.VMEM((7, CONV_WIN, D_CONV), F32)],
        name=name, compiler_params=_cparams(),
    )(zc, conv_w, conv_b, ln_g, ln_b)


def _conv_bwd(dfeat, cv, zc, conv_w, ln_g, ln_b, name):
    s = zc.shape[0]
    rt = min(256, s)

    def body(df_ref, cv_ref, z_ref, w_ref, g_ref, b_ref, dz_ref, dw_ref, dcb_ref, dg_ref, db_ref, hpad, dcvpad, dwacc,
             hshifts, dshifts):
        hpad[0:CONV_HALO, :] = jnp.zeros((CONV_HALO, D_CONV), F32)
        dcvpad[s:, :] = jnp.zeros((CONV_HALO, D_CONV), F32)
        dwacc[...] = jnp.zeros_like(dwacc)
        dcb_ref[...] = jnp.zeros_like(dcb_ref)
        dg_ref[...] = jnp.zeros_like(dg_ref)
        db_ref[...] = jnp.zeros_like(db_ref)

        def pass1(i, carry):
            r0 = pl.multiple_of(i * rt, rt)
            a, sb = _glu_rows(z_ref, r0, rt)
            hpad[pl.ds(r0 + CONV_HALO, rt), :] = a * sb
            cvhat, rstd = _ln_hat(cv_ref[pl.ds(r0, rt), :])
            y = cvhat * g_ref[...] + b_ref[...]
            sg = _sigmoid(y)
            dy = df_ref[pl.ds(r0, rt), :] * (sg * (1.0 + y * (1.0 - sg)))
            dg_ref[...] += jnp.sum(dy * cvhat, axis=0, keepdims=True)
            db_ref[...] += jnp.sum(dy, axis=0, keepdims=True)
            dcv = _ln_hat_bwd(dy * g_ref[...], cvhat, rstd)
            dcb_ref[...] += jnp.sum(dcv, axis=0, keepdims=True)
            dcvpad[pl.ds(r0, rt), :] = dcv
            return carry

        lax.fori_loop(0, s // rt, pass1, 0)
        w = w_ref[...]

        def pass2(i, carry):
            r0 = pl.multiple_of(i * CONV_ROWS, CONV_ROWS)
            dwin = _row_windows(dcvpad, r0, dshifts)
            hwin = _row_windows(hpad, r0, hshifts)
            dcv = dwin(0)
            dh = jnp.zeros((CONV_ROWS, D_CONV), F32)
            for k in range(CONV_WIDTH):
                dh = dh + dwin(30 - k) * w[k:k + 1, :]
                prod = dcv * hwin(2 + k)
                dwacc[8 * k:8 * k + 8, :] += jnp.sum(prod.reshape(CONV_ROWS // 8, 8, D_CONV), axis=0)
            a, sb = _glu_rows(z_ref, r0, CONV_ROWS)
            dz_ref[pl.ds(r0, CONV_ROWS), :] = jnp.concatenate([dh * sb, dh * a * sb * (1.0 - sb)], axis=1).astype(BF16)
            return carry

        lax.fori_loop(0, s // CONV_ROWS, pass2, 0)
        dw_ref[...] = jnp.sum(dwacc[...].reshape(32, 8, D_CONV), axis=1)

    vs = jax.ShapeDtypeStruct((1, D_CONV), F32)
    return pl.pallas_call(
        body,
        out_shape=[jax.ShapeDtypeStruct((s, 2 * D_CONV), BF16), jax.ShapeDtypeStruct((32, D_CONV), F32), vs, vs, vs],
        scratch_shapes=[pltpu.VMEM((s + CONV_HALO, D_CONV), F32), pltpu.VMEM((s + CONV_HALO, D_CONV), F32),
                        pltpu.VMEM((256, D_CONV), F32), pltpu.VMEM((7, CONV_WIN, D_CONV), F32),
                        pltpu.VMEM((7, CONV_WIN, D_CONV), F32)],
        name=name, compiler_params=_cparams(),
    )(dfeat, cv, zc, conv_w, ln_g, ln_b)


def _branch_out(feats, wts, name):
    s = feats[0].shape[0]
    tm = min(1024, s)

    def body(*refs):
        for f_ref, w_ref, o_ref in zip(refs[:3], refs[3:6], refs[6:]):
            o_ref[...] = lax.dot_general(f_ref[...], w_ref[...], _DIMS["nt"], preferred_element_type=F32).astype(BF16)

    row = pl.BlockSpec((tm, D_MODEL), lambda i: (i, 0))
    sh = jax.ShapeDtypeStruct((s, D_MODEL), BF16)
    return pl.pallas_call(
        body, grid=(s // tm,),
        in_specs=[pl.BlockSpec((tm, f.shape[1]), lambda i: (i, 0)) for f in feats] + [_full(w.shape) for w in wts],
        out_specs=[row] * 3, out_shape=[sh] * 3, name=name, compiler_params=_cparams(),
    )(*feats, *wts)


def _branch_in_bwd(dys, wts, out_dtypes, name):
    s = dys[0].shape[0]
    tm = min(1024, s)

    def body(*refs):
        for d_ref, w_ref, o_ref in zip(refs[:3], refs[3:6], refs[6:]):
            o_ref[...] = jnp.dot(d_ref[...], w_ref[...], preferred_element_type=F32).astype(o_ref.dtype)

    row = pl.BlockSpec((tm, D_MODEL), lambda i: (i, 0))
    return pl.pallas_call(
        body, grid=(s // tm,), in_specs=[row] * 3 + [_full(w.shape) for w in wts],
        out_specs=[pl.BlockSpec((tm, w.shape[1]), lambda i: (i, 0)) for w in wts],
        out_shape=[jax.ShapeDtypeStruct((s, w.shape[1]), dt) for w, dt in zip(wts, out_dtypes)],
        name=name, compiler_params=_cparams(),
    )(*dys, *wts)


def _branch_dw(dys, feats, name):
    s = dys[0].shape[0]
    tm = 512

    def body(*refs):
        for d_ref, f_ref, o_ref in zip(refs[:3], refs[3:6], refs[6:]):
            acc = lax.dot_general(d_ref[...], f_ref[...], _DIMS["tn"], preferred_element_type=F32)
            half = acc.shape[1] // 2
            o_ref[0] = acc[:, :half].astype(BF16)
            o_ref[1] = acc[:, half:].astype(BF16)

    return pl.pallas_call(
        body, grid=(D_MODEL // tm,),
        in_specs=[pl.BlockSpec((s, tm), lambda i: (0, i))] * 3 + [_full(f.shape) for f in feats],
        out_specs=[pl.BlockSpec((2, tm, f.shape[1] // 2), lambda i: (0, i, 0)) for f in feats],
        out_shape=[jax.ShapeDtypeStruct((2, D_MODEL, f.shape[1] // 2), BF16) for f in feats],
        name=name, compiler_params=_cparams(),
    )(*dys, *feats)


def _merge(zg, b_gate, ys, name):
    s = zg.shape[0]
    tm = _row_tile(s)

    def body(zg_ref, bg_ref, y0_ref, y1_ref, y2_ref, o_ref):
        acc = None
        for j, y_ref in enumerate((y0_ref, y1_ref, y2_ref)):
            cs = slice(D_MODEL * j, D_MODEL * (j + 1))
            t = _sigmoid(zg_ref[:, cs] + bg_ref[:, cs]) * y_ref[...]
            acc = t if acc is None else acc + t
        o_ref[...] = acc.astype(BF16)

    row = pl.BlockSpec((tm, D_MODEL), lambda i: (i, 0))
    return pl.pallas_call(
        body, grid=(s // tm,),
        in_specs=[pl.BlockSpec((tm, 3 * D_MODEL), lambda i: (i, 0)), _full((1, 3 * D_MODEL)), row, row, row],
        out_specs=row, out_shape=jax.ShapeDtypeStruct((s, D_MODEL), BF16), name=name, compiler_params=_cparams(),
    )(zg, b_gate, *ys)


def _merge_bwd(dmix, w_o, zg, b_gate, ys, name):
    s = zg.shape[0]
    tm = min(256, s)

    def body(dmix_ref, wo_ref, zg_ref, bg_ref, y0_ref, y1_ref, y2_ref, d0_ref, d1_ref, d2_ref, dzg_ref, dbg_ref):
        first = pl.program_id(0) == 0

        @pl.when(first)
        def _():
            dbg_ref[...] = jnp.zeros_like(dbg_ref)

        dmv = lax.dot_general(dmix_ref[...], wo_ref[...], _DIMS["nt"], preferred_element_type=F32)
        for j, (y_ref, d_ref) in enumerate(((y0_ref, d0_ref), (y1_ref, d1_ref), (y2_ref, d2_ref))):
            cs = slice(D_MODEL * j, D_MODEL * (j + 1))
            g = _sigmoid(zg_ref[:, cs] + bg_ref[:, cs])
            d_ref[...] = (dmv * g).astype(BF16)
            dzg = dmv * y_ref[...] * g * (1.0 - g)
            dzg_ref[:, cs] = dzg.astype(BF16)
            dbg_ref[:, cs] += jnp.sum(dzg, axis=0, keepdims=True)

    row = pl.BlockSpec((tm, D_MODEL), lambda i: (i, 0))
    wide = pl.BlockSpec((tm, 3 * D_MODEL), lambda i: (i, 0))
    yb = jax.ShapeDtypeStruct((s, D_MODEL), BF16)
    return pl.pallas_call(
        body, grid=(s // tm,),
        in_specs=[row, _full(w_o.shape), wide, _full((1, 3 * D_MODEL)), row, row, row],
        out_specs=[row, row, row, wide, _full((1, 3 * D_MODEL))],
        out_shape=[yb, yb, yb, jax.ShapeDtypeStruct((s, 3 * D_MODEL), BF16), jax.ShapeDtypeStruct((1, 3 * D_MODEL), F32)],
        name=name, compiler_params=_cparams(),
    )(dmix, w_o, zg, b_gate, *ys)


def _ff_hidden(u2, w_ff1t, b_ff1, name, rider=None):
    s = u2.shape[0]
    tm, tn = min(2048, s), 1024

    def body(a_ref, b_ref, bias_ref, pre_ref, h_ref):
        acc = lax.dot_general(a_ref[...], b_ref[...], _DIMS["nt"], preferred_element_type=F32) + bias_ref[...]
        pre_ref[...] = acc.astype(BF16)
        h_ref[...] = _relu2(acc).astype(BF16)

    blk = pl.BlockSpec((tm, tn), lambda i, j: (i, j))
    sh = jax.ShapeDtypeStruct((s, D_FF), BF16)
    res = _call(body, name=name, grid=(s // tm, D_FF // tn),
                in_specs=[pl.BlockSpec((tm, D_MODEL), lambda i, j: (i, 0)), pl.BlockSpec((tn, D_MODEL), lambda i, j: (j, 0)),
                          pl.BlockSpec((1, tn), lambda i, j: (0, j))],
                out_specs=[blk, blk], out_shape=[sh, sh], scratch_shapes=[], args=(u2, w_ff1t, b_ff1), rider=rider)
    return tuple(res) if rider is None else (tuple(res[0]), res[1])


def _ff_hidden_bwd(dff, w_ff2, hpre, name):
    s = dff.shape[0]
    tm, tn = min(1024, s), 1024

    def body(a_ref, b_ref, h_ref, o_ref, sum_ref):
        dh = lax.dot_general(a_ref[...], b_ref[...], _DIMS["nt"], preferred_element_type=F32)
        dpre = dh * (2.0 * jnp.maximum(h_ref[...].astype(F32), 0.0))
        o_ref[...] = dpre.astype(BF16)
        _acc_rows(sum_ref, dpre, pl.program_id(1) == 0)

    res = _call(
        body, name=name, grid=(D_FF // tn, s // tm),
        in_specs=[pl.BlockSpec((tm, D_MODEL), lambda j, i: (i, 0)), pl.BlockSpec((tn, D_MODEL), lambda j, i: (j, 0)),
                  pl.BlockSpec((tm, tn), lambda j, i: (i, j))],
        out_specs=[pl.BlockSpec((tm, tn), lambda j, i: (i, j)), pl.BlockSpec((1, tn), lambda j, i: (0, j))],
        out_shape=[jax.ShapeDtypeStruct((s, D_FF), BF16), jax.ShapeDtypeStruct((1, D_FF), F32)],
        scratch_shapes=[], args=(dff, w_ff2, hpre))
    return tuple(res)


def _silu(t):
    return t * _sigmoid(t)


def _mod_fwd(c_all, w_ada_sh, b_ada_sh, name):
    cols = w_ada_sh.shape[2]

    def body(c_ref, w_ref, b_ref, o_ref):
        ca = _silu(c_ref[...]).astype(BF16)
        o_ref[0] = jnp.dot(ca, w_ref[0].astype(BF16), preferred_element_type=F32) + b_ref[0]

    return pl.pallas_call(
        body, grid=(DEPTH,),
        in_specs=[_full((N_DEV, D_MODEL)), pl.BlockSpec((1, D_MODEL, cols), lambda l: (l, 0, 0)),
                  pl.BlockSpec((1, 1, cols), lambda l: (l, 0, 0))],
        out_specs=pl.BlockSpec((1, N_DEV, cols), lambda l: (l, 0, 0)),
        out_shape=jax.ShapeDtypeStruct((DEPTH, N_DEV, cols), F32), name=name, compiler_params=_cparams(),
    )(c_all, w_ada_sh, b_ada_sh)


def _mod_bwd(c_all, dmod_sh, name):
    cols = dmod_sh.shape[2]

    def body(c_ref, d_ref, o_ref):
        ca = _silu(c_ref[...])
        o_ref[0] = lax.dot_general(ca, d_ref[0], _DIMS["tn"], precision=lax.Precision.HIGHEST,
                                   preferred_element_type=F32)

    return pl.pallas_call(
        body, grid=(DEPTH,),
        in_specs=[_full((N_DEV, D_MODEL)), pl.BlockSpec((1, N_DEV, cols), lambda l: (l, 0, 0))],
        out_specs=pl.BlockSpec((1, D_MODEL, cols), lambda l: (l, 0, 0)),
        out_shape=jax.ShapeDtypeStruct((DEPTH, D_MODEL, cols), F32), name=name, compiler_params=_cparams(),
    )(c_all, dmod_sh)


def _flat_tiles(rows, cols, itemsize_total):
    budget = 12 * 1024 * 1024
    tr = rows
    while tr % 32 == 0 and tr * cols * itemsize_total > budget:
        tr //= 2
    return tr


def _sum_cores(dws, recvs, place, name):
    k = len(dws)

    def body(place_ref, *refs):
        for a_ref, b_ref, o_ref in zip(refs[:k], refs[k:2 * k], refs[2 * k:]):
            o_ref[...] = (a_ref[...].astype(F32) + b_ref[...].astype(F32)).astype(BF16)

    whole = [pl.BlockSpec(a.shape[1:], lambda i, pr: (0, 0)) for a in dws]
    mine = [pl.BlockSpec((None,) + a.shape[1:], lambda i, pr: (pr[0], 0, 0)) for a in dws]
    grid_spec = pltpu.PrefetchScalarGridSpec(num_scalar_prefetch=1, grid=(1,), in_specs=mine + whole, out_specs=whole)
    return pl.pallas_call(body, grid_spec=grid_spec, out_shape=[jax.ShapeDtypeStruct(a.shape[1:], BF16) for a in dws],
                          name=name, compiler_params=_cparams())(place, *dws, *recvs)


def _sum_chips(hs, rs, place, name):
    k = len(hs)

    def body(place_ref, *refs):
        for h_ref, r_ref, o_ref in zip(refs[:k], refs[k:2 * k], refs[2 * k:]):
            o_ref[...] = ((h_ref[...].astype(F32) + r_ref[0].astype(F32)) + r_ref[1].astype(F32)) + r_ref[2].astype(F32)

    own = [pl.BlockSpec((None,) + h.shape[1:], lambda i, pr: (pr[1], 0, 0)) for h in hs]
    got = [pl.BlockSpec(r.shape, lambda i, pr: (0, 0, 0)) for r in rs]
    out = [pl.BlockSpec(h.shape[1:], lambda i, pr: (0, 0)) for h in hs]
    grid_spec = pltpu.PrefetchScalarGridSpec(num_scalar_prefetch=1, grid=(1,), in_specs=own + got, out_specs=out)
    return pl.pallas_call(body, grid_spec=grid_spec, out_shape=[jax.ShapeDtypeStruct(h.shape[1:], F32) for h in hs],
                          name=name, compiler_params=_cparams())(place, *hs, *rs)


def _adam_math(w, g, m, v):
    m2 = ADAM_B1 * m + (1.0 - ADAM_B1) * g
    v2 = ADAM_B2 * v + (1.0 - ADAM_B2) * (g * g)
    m_hat = m2 / (1.0 - ADAM_B1 ** ADAM_STEP)
    v_hat = v2 / (1.0 - ADAM_B2 ** ADAM_STEP)
    delta = -ADAM_LR * (m_hat / (jnp.sqrt(v_hat) + ADAM_EPS) + ADAM_WD * w)
    return delta, m2, v2


def _adamw(w, m, v, grads, name):
    r, c = w.shape
    tr = _flat_tiles(r, c, 4 * (7 + len(grads)))

    def body(*refs):
        w_ref, m_ref, v_ref = refs[:3]
        g_refs = refs[3:3 + len(grads)]
        g_ref, d_ref, m2_ref, v2_ref = refs[3 + len(grads):]
        g = g_refs[0][...]
        for gr in g_refs[1:]:
            g = g + gr[...]
        delta, m2, v2 = _adam_math(w_ref[...], g, m_ref[...], v_ref[...])
        g_ref[...] = g
        d_ref[...] = delta
        m2_ref[...] = m2
        v2_ref[...] = v2

    blk = pl.BlockSpec((tr, c), lambda i: (i, 0))
    sh = jax.ShapeDtypeStruct((r, c), F32)
    return pl.pallas_call(body, grid=(r // tr,), in_specs=[blk] * (3 + len(grads)), out_specs=[blk] * 4,
                          out_shape=[sh] * 4, name=name, compiler_params=_cparams())(w, m, v, *grads)


def _adamw_halves(w, m, v, own, other, place, split, name):
    nl, r, c = w.shape
    hr, hc = own[0].shape
    tr = _flat_tiles(hr, hc, 4 * (7 + 2 * nl))
    nt = hr // tr
    if split == "rows":
        w_spec = pl.BlockSpec((None, tr, c), lambda l, h, t, pr: (l, h * nt + t, 0))
    else:
        w_spec = pl.BlockSpec((None, tr, hc), lambda l, h, t, pr: (l, t, h))

    def g_spec(layer, mine):
        return pl.BlockSpec((tr, hc), lambda l, h, t, pr: (jnp.where((l == layer) & ((h == pr[0]) == mine), t, nt - 1), 0))

    def body(place_ref, w_ref, m_ref, v_ref, *refs):
        own_refs, other_refs = refs[:nl], refs[nl:2 * nl]
        g_ref, d_ref, m2_ref, v2_ref = refs[2 * nl:]
        layer = pl.program_id(0)
        mine = pl.program_id(1) == place_ref[0]
        g = None
        for li in range(nl):
            cand = jnp.where(mine, own_refs[li][...], other_refs[li][...])
            g = cand if g is None else jnp.where(layer == li, cand, g)
        delta, m2, v2 = _adam_math(w_ref[...], g, m_ref[...], v_ref[...])
        g_ref[...] = g
        d_ref[...] = delta
        m2_ref[...] = m2
        v2_ref[...] = v2

    sh = jax.ShapeDtypeStruct((nl, r, c), F32)
    g_specs = [g_spec(li, True) for li in range(nl)] + [g_spec(li, False) for li in range(nl)]
    return _call(body, name=name, grid=(nl, 2, nt), in_specs=[w_spec] * 3 + g_specs, out_specs=[w_spec] * 4,
                 out_shape=[sh] * 4, scratch_shapes=[], args=(w, m, v, *own, *other), prefetch=(place,))


def _adamw_small(w, m, v, g_all, name):
    r, c = w.shape

    def body(w_ref, m_ref, v_ref, g_ref, go_ref, d_ref, m2_ref, v2_ref):
        g = g_ref[0]
        for b in range(1, N_DEV):
            g = g + g_ref[b]
        delta, m2, v2 = _adam_math(w_ref[...], g, m_ref[...], v_ref[...])
        go_ref[...] = g
        d_ref[...] = delta
        m2_ref[...] = m2
        v2_ref[...] = v2

    sh = jax.ShapeDtypeStruct((r, c), F32)
    return pl.pallas_call(body, out_shape=[sh] * 4, name=name, compiler_params=_cparams())(w, m, v, g_all)


def _me():
    return lax.axis_index("x"), lax.axis_index("y"), lax.axis_index("c")


def _flip(v, bit):
    return 1 - v if bit else v


def _allgather_small(blk, name):
    r, c = blk.shape

    def body(x_ref, o_ref, send_sems, recv_sems):
        x, y, cc = _me()
        me = 4 * x + 2 * y + cc
        copies = []
        for k in range(1, N_DEV):
            peer = (_flip(x, k & 4), _flip(y, k & 2), _flip(cc, k & 1))
            cp = pltpu.make_async_remote_copy(src_ref=x_ref, dst_ref=o_ref.at[me], send_sem=send_sems.at[k - 1],
                                              recv_sem=recv_sems.at[k - 1], device_id=peer, device_id_type=MESH)
            cp.start()
            copies.append(cp)
        o_ref[me] = x_ref[...]
        for cp in copies:
            cp.wait()

    return pl.pallas_call(
        body, out_shape=jax.ShapeDtypeStruct((N_DEV, r, c), F32),
        in_specs=[pl.BlockSpec(memory_space=pltpu.VMEM)], out_specs=pl.BlockSpec(memory_space=pltpu.VMEM),
        scratch_shapes=[pltpu.SemaphoreType.DMA((N_DEV - 1,)), pltpu.SemaphoreType.DMA((N_DEV - 1,))],
        name=name, compiler_params=_cparams(),
    )(blk)


class _Rider:
    def __init__(self, arrays, out_shapes, scratch_shapes, start, finish):
        self.arrays, self.out_shapes, self.scratch_shapes = list(arrays), list(out_shapes), list(scratch_shapes)
        self.start, self.finish = start, finish


def _call(body, *, name, grid, in_specs, out_specs, out_shape, scratch_shapes, args, rider=None, prefetch=()):
    npf = len(prefetch)

    def launch(fn, in_specs, out_specs, out_shape, scratch_shapes, args):
        grid_spec = pltpu.PrefetchScalarGridSpec(num_scalar_prefetch=npf, grid=grid, in_specs=in_specs,
                                                 out_specs=out_specs, scratch_shapes=scratch_shapes)
        return pl.pallas_call(fn, grid_spec=grid_spec, out_shape=out_shape, name=name,
                              compiler_params=_cparams())(*prefetch, *args)

    if rider is None:
        return launch(body, list(in_specs), list(out_specs), list(out_shape), list(scratch_shapes), args)
    ni, no, ns = len(in_specs), len(out_specs), len(scratch_shapes)
    ri, ro = len(rider.arrays), len(rider.out_shapes)
    steps = int(np.prod(grid))

    def wrapped(*refs):
        pf, refs = refs[:npf], refs[npf:]
        h_in, r_in = refs[:ni], refs[ni:ni + ri]
        h_out, r_out = refs[ni + ri:ni + ri + no], refs[ni + ri + no:ni + ri + no + ro]
        h_scr, r_scr = refs[ni + ri + no + ro:ni + ri + no + ro + ns], refs[ni + ri + no + ro + ns:]
        step = pl.program_id(0)
        for d in range(1, len(grid)):
            step = step * grid[d] + pl.program_id(d)

        @pl.when(step == 0)
        def _():
            rider.start(r_in, r_out, r_scr)

        body(*pf, *h_in, *h_out, *h_scr)

        @pl.when(step == steps - 1)
        def _():
            rider.finish(r_in, r_out, r_scr)

    anyspec = pl.BlockSpec(memory_space=pl.ANY)
    res = launch(wrapped, list(in_specs) + [anyspec] * ri, list(out_specs) + [anyspec] * ro,
                 list(out_shape) + rider.out_shapes, list(scratch_shapes) + rider.scratch_shapes,
                 list(args) + rider.arrays)
    return res[:no], res[no:]


def _run_rider(rider, name):
    ri = len(rider.arrays)

    def body(*refs):
        r_in, r_out, r_scr = refs[:ri], refs[ri:ri + len(rider.out_shapes)], refs[ri + len(rider.out_shapes):]
        rider.start(r_in, r_out, r_scr)
        rider.finish(r_in, r_out, r_scr)

    anyspec = pl.BlockSpec(memory_space=pl.ANY)
    return pl.pallas_call(body, in_specs=[anyspec] * ri, out_specs=[anyspec] * len(rider.out_shapes),
                          out_shape=rider.out_shapes, scratch_shapes=rider.scratch_shapes, name=name,
                          compiler_params=_cparams())(*rider.arrays)


def _allgather_rider(blk):
    def copies(ins, outs, scr):
        send_sems, recv_sems, loc_sems, stage = scr
        x, y, cc = _me()
        me = 4 * x + 2 * y + cc
        remote = [pltpu.make_async_remote_copy(
            src_ref=ins[0], dst_ref=outs[0].at[me], send_sem=send_sems.at[k - 1], recv_sem=recv_sems.at[k - 1],
            device_id=(_flip(x, k & 4), _flip(y, k & 2), _flip(cc, k & 1)), device_id_type=MESH) for k in range(1, N_DEV)]
        return remote, pltpu.make_async_copy(ins[0], stage, loc_sems.at[0]), (outs[0].at[me], stage, loc_sems.at[1])

    def start(ins, outs, scr):
        remote, lin, _ = copies(ins, outs, scr)
        lin.start()
        for cp in remote:
            cp.start()

    def finish(ins, outs, scr):
        remote, lin, (dst, stage, sem) = copies(ins, outs, scr)
        lin.wait()
        lout = pltpu.make_async_copy(stage, dst, sem)
        lout.start()
        for cp in remote:
            cp.wait()
        lout.wait()

    return _Rider([blk], [jax.ShapeDtypeStruct((N_DEV,) + blk.shape, blk.dtype)],
                  [pltpu.SemaphoreType.DMA((N_DEV - 1,)), pltpu.SemaphoreType.DMA((N_DEV - 1,)),
                   pltpu.SemaphoreType.DMA((2,)), pltpu.VMEM(blk.shape, blk.dtype)], start, finish)


def _gather_rider(shards):
    n = len(shards)

    def copies(ins, outs, scr, relay=True):
        ici_send, ici_recv, d2d_send, d2d_recv, loc_sems = scr[:5]
        stage = scr[5:]
        x, y, cc = _me()
        chip = 2 * x + y
        sibling = (x, y, 1 - cc)
        local, sends, relays = [], [], []
        for j in range(n):
            def rows(ch, h, j=j):
                return outs[j].at[ch, h]

            lc = pltpu.make_async_copy(ins[j], stage[j], loc_sems.at[j])
            local.append((lc, pltpu.make_async_copy(stage[j], outs[j].at[chip], loc_sems.at[n + j]) if relay else None))
            for k in range(1, N_CHIP):
                px, py = _flip(x, k & 2), _flip(y, k & 1)
                pchip = 2 * px + py
                q = 3 * j + k - 1
                out_cp = pltpu.make_async_remote_copy(src_ref=ins[j].at[cc], dst_ref=rows(chip, cc),
                                                      send_sem=ici_send.at[q], recv_sem=ici_recv.at[q],
                                                      device_id=(px, py, cc), device_id_type=MESH)
                sends.append(out_cp)
                if not relay:
                    continue
                arrival = pltpu.make_async_remote_copy(src_ref=rows(pchip, cc), dst_ref=rows(pchip, cc),
                                                       send_sem=ici_send.at[q], recv_sem=ici_recv.at[q],
                                                       device_id=(px, py, cc), device_id_type=MESH)
                forward = pltpu.make_async_remote_copy(src_ref=rows(pchip, cc), dst_ref=rows(pchip, cc),
                                                       send_sem=d2d_send.at[q], recv_sem=d2d_recv.at[q],
                                                       device_id=sibling, device_id_type=MESH)
                from_sibling = pltpu.make_async_remote_copy(src_ref=rows(pchip, 1 - cc), dst_ref=rows(pchip, 1 - cc),
                                                            send_sem=d2d_send.at[q], recv_sem=d2d_recv.at[q],
                                                            device_id=sibling, device_id_type=MESH)
                relays.append((arrival, forward, from_sibling))
        return local, sends, relays

    def start(ins, outs, scr):
        local, sends, _ = copies(ins, outs, scr, relay=False)
        for lin, _ in local:
            lin.start()
        for cp in sends:
            cp.start()

    def finish(ins, outs, scr):
        local, sends, relays = copies(ins, outs, scr)
        for lin, lout in local:
            lin.wait()
            lout.start()
        for arrival, forward, _ in relays:
            arrival.wait_recv()
            forward.start()
        for cp in sends:
            cp.wait_send()
        for _, forward, from_sibling in relays:
            forward.wait_send()
            from_sibling.wait_recv()
        for _, lout in local:
            lout.wait()

    scratch = [pltpu.SemaphoreType.DMA((3 * n,)), pltpu.SemaphoreType.DMA((3 * n,)), pltpu.SemaphoreType.DMA((3 * n,)),
               pltpu.SemaphoreType.DMA((3 * n,)), pltpu.SemaphoreType.DMA((2 * n,))]
    scratch += [pltpu.VMEM(a.shape, a.dtype) for a in shards]
    return _Rider(shards, [jax.ShapeDtypeStruct((N_CHIP,) + a.shape, a.dtype) for a in shards], scratch, start, finish)


def _sibling_rider(arrs, other_half=False):
    n = len(arrs)

    def copies(ins, outs, scr):
        send_sems, recv_sems = scr
        x, y, cc = _me()
        return [pltpu.make_async_remote_copy(
            src_ref=ins[j].at[1 - cc] if other_half else ins[j], dst_ref=outs[j], send_sem=send_sems.at[j],
            recv_sem=recv_sems.at[j], device_id=(x, y, 1 - cc), device_id_type=MESH) for j in range(n)]

    def start(ins, outs, scr):
        for cp in copies(ins, outs, scr):
            cp.start()

    def finish(ins, outs, scr):
        for cp in copies(ins, outs, scr):
            cp.wait()

    return _Rider(arrs, [jax.ShapeDtypeStruct(a.shape[1:] if other_half else a.shape, a.dtype) for a in arrs],
                  [pltpu.SemaphoreType.DMA((n,)), pltpu.SemaphoreType.DMA((n,))], start, finish)


def _sibling_send(arrs, name, other_half=False):
    return _run_rider(_sibling_rider(arrs, other_half), name)


def _join_riders(first, second):
    ni, no, ns = len(first.arrays), len(first.out_shapes), len(first.scratch_shapes)

    def split(ins, outs, scr):
        return (ins[:ni], outs[:no], scr[:ns]), (ins[ni:], outs[no:], scr[ns:])

    def start(ins, outs, scr):
        a, b = split(ins, outs, scr)
        first.start(*a)
        second.start(*b)

    def finish(ins, outs, scr):
        a, b = split(ins, outs, scr)
        first.finish(*a)
        second.finish(*b)

    return _Rider(first.arrays + second.arrays, first.out_shapes + second.out_shapes,
                  first.scratch_shapes + second.scratch_shapes, start, finish)


def _scatter_rider(arrs):
    n = len(arrs)

    def copies(ins, outs, scr):
        send_sems, recv_sems = scr
        x, y, cc = _me()
        cps = []
        for j in range(n):
            for k in range(1, N_CHIP):
                px, py = _flip(x, k & 2), _flip(y, k & 1)
                cps.append(pltpu.make_async_remote_copy(
                    src_ref=ins[j].at[2 * px + py], dst_ref=outs[j].at[k - 1], send_sem=send_sems.at[3 * j + k - 1],
                    recv_sem=recv_sems.at[3 * j + k - 1], device_id=(px, py, cc), device_id_type=MESH))
        return cps

    def start(ins, outs, scr):
        for cp in copies(ins, outs, scr):
            cp.start()

    def finish(ins, outs, scr):
        for cp in copies(ins, outs, scr):
            cp.wait()

    return _Rider(arrs, [jax.ShapeDtypeStruct((N_CHIP - 1,) + a.shape[1:], a.dtype) for a in arrs],
                  [pltpu.SemaphoreType.DMA((3 * n,)), pltpu.SemaphoreType.DMA((3 * n,))], start, finish)


COL_SHARDED = ("w_in", "w_br_pool", "w_br_attn", "w_br_conv", "w_ff1")
ROW_SHARDED = ("w_o", "w_ff2")
BIG = COL_SHARDED + ROW_SHARDED
SMALL = ("b_ada", "b_gate", "w_pool", "pool_scale", "rel_bias", "conv_w", "conv_b", "conv_ln_g", "conv_ln_b",
         "ln_mix_g", "ln_mix_b", "b_ff1", "b_ff2", "ln_ff_g", "ln_ff_b")
PACK_W = 1024


def _pack(parts):
    rows = []
    for a in parts:
        flat = a.reshape(-1)
        n = -(-flat.shape[0] // PACK_W) * PACK_W
        rows.append(jnp.pad(flat, (0, n - flat.shape[0])).reshape(-1, PACK_W))
    out = jnp.concatenate(rows, axis=0)
    r = -(-out.shape[0] // 8) * 8
    return jnp.pad(out, ((0, r - out.shape[0]), (0, 0)))


def _unpack(packed, shapes):
    out, r0 = [], 0
    for shp in shapes:
        size = int(np.prod(shp))
        nr = -(-size // PACK_W)
        out.append(packed[r0:r0 + nr].reshape(-1)[:size].reshape(shp))
        r0 += nr
    return out


def _hosted(fn, hook, *args, **kw):
    if hook is None:
        return fn(*args, **kw)
    res, rider_out = fn(*args, rider=hook[0], **kw)
    hook[1](rider_out)
    return res


def _layer_fwd(l, x, mod, W, P, hooks=None, u=None):
    hooks = hooks or {}
    s = x.shape[0]
    sh_m, sc_m, g_m, sh_f, sc_f, g_f = [mod[l:l + 1, D_MODEL * j:D_MODEL * (j + 1)] for j in range(6)]
    n = lambda t: f"{t}{l}"
    w_in = W["w_in"][l]
    if u is None:
        u = _ln_mod(x, sc_m, sh_m, n("ln_mod_mix"))
    zp = _mm(u, w_in, "nt", tm=s, tn=256, out_dtype=F32, name=n("z_pool"), b_col0=0, n_out=D_POOL)
    qkv = _mm(u, w_in, "nt", tm=s, tn=256, out_dtype=BF16, name=n("z_qkv"), b_col0=OFF_QKV // 256, n_out=3 * D_ATTN)
    zc = _mm(u, w_in, "nt", tm=s, tn=256, out_dtype=F32, name=n("z_conv"), b_col0=OFF_CONV // 256, n_out=2 * D_CONV)
    zg = _hosted(_mm, hooks.get("z_gate"), u, w_in, "nt", tm=min(2048, s), tn=768, out_dtype=BF16, name=n("z_gate"),
                 b_col0=OFF_GATE // 768, n_out=3 * D_MODEL)

    p, feat_pool = _pool_fwd(zp, P["wp_bd"][l], P["pool_scale"][l], n("pool_fwd"))
    bias = _bias_block(P["rel_bias"][l], n("bias_block"))
    o, probs = _hosted(_attn_fwd, hooks.get("attn"), qkv, bias, n("attn_fwd"))
    cv, feat_conv = _conv_fwd(zc, P["conv_w"][l], P["conv_b"][l], P["conv_ln_g"][l], P["conv_ln_b"][l], n("conv_fwd"))

    branch_w = (W["w_br_pool"][l], W["w_br_attn"][l], W["w_br_conv"][l])
    ys = tuple(_branch_out((feat_pool, o, feat_conv), branch_w, n("branch_out")))
    merged = _merge(zg, P["b_gate"][l], ys, n("merge"))
    mix, x1, u2 = _mm_resid_ln(merged, W["w_o"][l], None, x, g_m, P["ln_mix_g"][l], P["ln_mix_b"][l], n("mix_out"),
                               mod_next=(sc_f, sh_f))

    hpre, hid = _hosted(_ff_hidden, hooks.get("ff1"), u2, W["w_ff1"][l], P["b_ff1"][l], n("ff1"))
    above = None if l + 1 == mod.shape[0] else (mod[l + 1:l + 2, D_MODEL:2 * D_MODEL], mod[l + 1:l + 2, 0:D_MODEL])
    ff, x2, *u_next = _hosted(_mm_resid_ln, hooks.get("ff2"), hid, W["w_ff2"][l], P["b_ff2"][l], x1, g_f,
                              P["ln_ff_g"][l], P["ln_ff_b"][l], n("ff2"), mod_next=above)
    saved = dict(x=x, u=u, zp=zp, qkv=qkv, zc=zc, zg=zg, p=p, feat_pool=feat_pool, probs=probs, o=o, cv=cv,
                 feat_conv=feat_conv, ys=ys, merged=merged, mix=mix, x1=x1, u2=u2, hpre=hpre, hid=hid, ff=ff,
                 u_next=u_next[0] if u_next else None)
    return x2, saved


def _layer_bwd(l, dx2, mod, W, P, A, hooks=None, tgt=None, nxt=None):
    hooks = hooks or {}
    sh_m, sc_m, g_m, sh_f, sc_f, g_f = [mod[l:l + 1, D_MODEL * j:D_MODEL * (j + 1)] for j in range(6)]
    n = lambda t: f"{t}{l}"
    gw, gs = {}, {}

    if isinstance(dx2, tuple):
        dres, dff, gs["ln_ff_g"], gs["ln_ff_b"], dg_f, gs["b_ff2"] = dx2
    else:
        dres, dff, gs["ln_ff_g"], gs["ln_ff_b"], dg_f, gs["b_ff2"], *loss_part = _resid_ln_bwd(
            dx2, A["x1"], A["ff"], g_f, P["ln_ff_g"][l], n("resid_ln_ff_bwd"), tgt=tgt)
    gw["w_ff2"] = _mm(A["hid"], dff, "tn", tm=512, tn=1024, out_dtype=BF16, name=n("dw_ff2"), split_n=512)
    dhpre, gs["b_ff1"] = _ff_hidden_bwd(dff, W["w_ff2"][l], A["hpre"], n("ff_hidden_bwd"))
    gw["w_ff1"] = _mm(dhpre, A["u2"], "tn", tm=512, tn=1024, out_dtype=BF16, name=n("dw_ff1"), split_n=512)

    hook = hooks["du_ff"](gw) if "du_ff" in hooks else None
    dres, dmix, dsc_f, dsh_f, gs["ln_mix_g"], gs["ln_mix_b"], dg_m, _ = _hosted(
        _mm_ln_mod_bwd, hook, dhpre, W["w_ff1"][l], A["x1"], sc_f, dres, n("du_ff"),
        nxt=(A["x"], A["mix"], g_m, P["ln_mix_g"][l]))
    gw["w_o"] = _mm(A["merged"], dmix, "tn", tm=512, tn=1024, out_dtype=BF16, name=n("dw_o"), split_n=512)
    dy_pool, dy_attn, dy_conv, dzg, gs["b_gate"] = _merge_bwd(dmix, W["w_o"][l], A["zg"], P["b_gate"][l], A["ys"],
                                                              n("merge_bwd"))

    dys = (dy_pool, dy_attn, dy_conv)
    gw["w_br_pool"], gw["w_br_attn"], gw["w_br_conv"] = _branch_dw(
        dys, (A["feat_pool"], A["o"], A["feat_conv"]), n("dw_branch"))
    dfeat_pool, do, dfeat_conv = _branch_in_bwd(
        dys, (W["w_br_pool"][l], W["w_br_attn"][l], W["w_br_conv"][l]), (F32, BF16, F32), n("d_branch_in"))

    dzp, dwp_bd, gs["pool_scale"] = _pool_bwd(dfeat_pool, A["p"], P["wp_bd"][l], P["pool_scale"][l], n("pool_bwd"))
    gs["w_pool"] = jnp.stack([dwp_bd[POOL_GROUP * g:POOL_GROUP * (g + 1), POOL_GROUP * g:POOL_GROUP * (g + 1)]
                              for g in range(len(POOL_WINDOWS))])
    hook = hooks["attn"](gw) if "attn" in hooks else None
    dq, dk, dv, ds_acc = _hosted(_attn_bwd, hook, A["qkv"], do, A["probs"], n("attn_bwd"))
    gs["rel_bias"] = _bias_block_bwd(ds_acc, n("bias_block_bwd"))
    dzc, dcw, gs["conv_b"], gs["conv_ln_g"], gs["conv_ln_b"] = _conv_bwd(
        dfeat_conv, A["cv"], A["zc"], P["conv_w"][l], P["conv_ln_g"][l], P["conv_ln_b"][l], n("conv_bwd"))
    gs["conv_w"] = dcw[:CONV_WIDTH]

    dz = [dzp, dq, dk, dv, dzc, dzg]
    gw["w_in"] = _dw_segments(dz, A["u"], n("dw_in"))
    hook = hooks["du_mix"](gw) if "du_mix" in hooks else None
    res = _hosted(_mm_ln_mod_bwd, hook, dz, W["w_in"][l], A["x"], sc_m, dres, n("du_mix"), nxt=nxt)
    if nxt is None:
        dx, dsc_m, dsh_m = res
    else:
        dx, dsc_m, dsh_m = (res[0], res[1], *res[4:]), res[2], res[3]
    dmod = jnp.concatenate([dsh_m, dsc_m, dg_m, dsh_f, dsc_f, dg_f], axis=1)
    return (dx, gw, gs, dmod) if tgt is None else (dx, gw, gs, dmod, loss_part[0])


def _small_shapes():
    return {"b_ada": (6 * D_MODEL,), "b_gate": (3 * D_MODEL,), "w_pool": (4, POOL_GROUP, POOL_GROUP),
            "pool_scale": (D_POOL,), "rel_bias": (N_HEADS, N_REL), "conv_w": (CONV_WIDTH, D_CONV),
            "conv_b": (D_CONV,), "conv_ln_g": (D_CONV,), "conv_ln_b": (D_CONV,), "ln_mix_g": (D_MODEL,),
            "ln_mix_b": (D_MODEL,), "b_ff1": (D_FF,), "b_ff2": (D_MODEL,), "ln_ff_g": (D_MODEL,), "ln_ff_b": (D_MODEL,)}


def kernel(x, c, w_ada, b_ada, w_in, b_gate, w_pool, pool_scale, rel_bias, conv_w, conv_b, conv_ln_g, conv_ln_b, w_br_pool, w_br_attn, w_br_conv, w_o, ln_mix_g, ln_mix_b, w_ff1, b_ff1, w_ff2, b_ff2, ln_ff_g, ln_ff_b, loss_target, m_w_ada, m_b_ada, m_w_in, m_b_gate, m_w_pool, m_pool_scale, m_rel_bias, m_conv_w, m_conv_b, m_conv_ln_g, m_conv_ln_b, m_w_br_pool, m_w_br_attn, m_w_br_conv, m_w_o, m_ln_mix_g, m_ln_mix_b, m_w_ff1, m_b_ff1, m_w_ff2, m_b_ff2, m_ln_ff_g, m_ln_ff_b, v_w_ada, v_b_ada, v_w_in, v_b_gate, v_w_pool, v_pool_scale, v_rel_bias, v_conv_w, v_conv_b, v_conv_ln_g, v_conv_ln_b, v_w_br_pool, v_w_br_attn, v_w_br_conv, v_w_o, v_ln_mix_g, v_ln_mix_b, v_w_ff1, v_b_ff1, v_w_ff2, v_b_ff2, v_ln_ff_g, v_ln_ff_b):
    env = dict(locals())
    xi, yi, ci = _me()
    chip = 2 * xi + yi
    me = 4 * xi + 2 * yi + ci
    xs = x[0]
    tgt = loss_target[0]
    L = DEPTH

    first = _allgather_small(jnp.concatenate([c.reshape(8, 128), _pack([conv_w]).reshape(-1, 128)]), "gather_c_conv_w")
    c_all = first[:, :8].reshape(N_DEV, D_MODEL)
    ada_cols = w_ada.shape[2]
    b_ada_sh = lax.dynamic_slice_in_dim(b_ada, chip * ada_cols, ada_cols, axis=1).reshape(L, 1, ada_cols)
    mod_part = _mod_fwd(c_all, w_ada, b_ada_sh, "mod_fwd")

    W = {k: [None] * L for k in BIG}

    def weight_gather(*items):
        shards = [(jnp.swapaxes(env[k][l], 0, 1) if k in COL_SHARDED else env[k][l]).astype(BF16) for k, l in items]
        shards = [a.reshape(2, a.shape[0] // 2, a.shape[1]) for a in shards]

        def done(outs):
            for (k, l), g in zip(items, outs):
                W[k][l] = g.reshape(-1, g.shape[-1])

        return _gather_rider(shards), done

    branch = lambda l: [(k, l) for k in ("w_br_pool", "w_br_attn", "w_br_conv", "w_o")]
    rider, done = weight_gather(("w_in", 0))
    first_out = _run_rider(_join_riders(_allgather_rider(mod_part.reshape(-1, 128)), rider), "gather_mod_w_in0")
    done(first_out[1:])
    mod_g = first_out[0].reshape(N_CHIP, 2, L, N_DEV, ada_cols)[:, 0]
    mod_all = jnp.transpose(mod_g, (1, 2, 0, 3)).reshape(L, N_DEV, 6 * D_MODEL)
    mod = lax.dynamic_index_in_dim(mod_all, me, axis=1, keepdims=False)
    fwd_hooks = [{"z_gate": weight_gather(*branch(0)), "attn": weight_gather(("w_ff1", 0), ("w_ff2", 0)),
                  "ff1": weight_gather(*branch(1)), "ff2": weight_gather(("w_in", 1))},
                 {"attn": weight_gather(("w_ff1", 1), ("w_ff2", 1))}]

    P = {k: env[k] for k in ("rel_bias", "conv_w")}
    for k in ("b_gate", "pool_scale", "conv_b", "conv_ln_g", "conv_ln_b", "ln_mix_g", "ln_mix_b", "b_ff1", "b_ff2",
              "ln_ff_g", "ln_ff_b"):
        P[k] = env[k].reshape(L, 1, -1)
    n_cw = conv_w.size
    cw = first[:, 8:].reshape(N_CHIP, 2, -1)[:, 0, :n_cw].reshape(N_CHIP, L, CONV_WIDTH, D_CONV // N_CHIP)
    P["conv_w"] = jnp.transpose(cw, (1, 2, 0, 3)).reshape(L, CONV_WIDTH, D_CONV)
    wp_bd = jnp.zeros((L, D_POOL, D_POOL), F32)
    for g in range(len(POOL_WINDOWS)):
        sl = slice(POOL_GROUP * g, POOL_GROUP * (g + 1))
        wp_bd = wp_bd.at[:, sl, sl].set(w_pool[:, g])
    P["wp_bd"] = wp_bd.astype(BF16)

    acts = []
    h = xs
    for l in range(L):
        h, saved = _layer_fwd(l, h, mod, W, P, fwd_hooks[l], u=acts[-1]["u_next"] if acts else None)
        acts.append(saved)

    place = jnp.stack([ci, chip, chip ^ 1, chip ^ 2, chip ^ 3]).astype(jnp.int32)
    scattered = {}

    def grad_scatter(items, tag):
        dws = [dw for _, _, dw in items]
        got = _sibling_send(dws, f"swap_blocks_{tag}", other_half=True)
        both = [hh.reshape(N_CHIP, -1, hh.shape[-1]) for hh in _sum_cores(dws, got, place, f"sum_cores_{tag}")]

        def done(outs):
            for (k, l, _), hh, r in zip(items, both, outs):
                scattered[(k, l)] = (hh, r)

        return _scatter_rider(both), done

    def scatter_hook(names, l, host):
        return lambda gw: grad_scatter([(k, l, gw[k]) for k in names], f"{host}{l}")

    gws, gss, dmods = [None] * L, [None] * L, [None] * L
    dh = h
    for l in reversed(range(L)):
        hooks = {"du_ff": scatter_hook(("w_ff2",), l, "du_ff"),
                 "attn": scatter_hook(("w_ff1", "w_o", "w_br_pool", "w_br_attn", "w_br_conv"), l, "attn_bwd"),
                 "du_mix": scatter_hook(("w_in",), l, "du_mix")}
        below = None
        if l > 0:
            below = (acts[l - 1]["x1"], acts[l - 1]["ff"], mod[l - 1:l, 5 * D_MODEL:], P["ln_ff_g"][l - 1])
        if l == L - 1:
            dh, gws[l], gss[l], dmods[l], loss_part = _layer_bwd(l, dh, mod, W, P, acts[l], hooks, tgt=tgt, nxt=below)
        else:
            dh, gws[l], gss[l], dmods[l] = _layer_bwd(l, dh, mod, W, P, acts[l], hooks, nxt=below)
    grad_x = dh[None]

    reduced = [[None] * L for _ in BIG]
    groups = (("w_in", "w_br_pool", "w_br_attn", "w_br_conv"), ("w_o", "w_ff1", "w_ff2"))
    for l in range(L):
        for gi, names in enumerate(groups):
            pairs = [scattered[(k, l)] for k in names]
            sums = _sum_chips([p[0] for p in pairs], [p[1] for p in pairs], place, f"sum_chips_{gi}_{l}")
            for k, t in zip(names, sums):
                reduced[BIG.index(k)][l] = t
    flat_reduced = [t for per_weight in reduced for t in per_weight]

    shapes = _small_shapes()
    small_names = [k for k in SMALL if k != "b_ada"]
    dmod_own = jnp.concatenate(dmods, axis=0)
    pack = _pack([dmod_own] + [jnp.stack([gss[l][k].reshape(shapes[k]) for l in range(L)]) for k in small_names]
                 + [loss_part])
    last = _run_rider(_join_riders(_sibling_rider(flat_reduced), _allgather_rider(pack.reshape(-1, 128))),
                      "swap_reduced_gather_small")
    flat_other, g_all = last[:-1], last[-1].reshape(N_DEV, -1, PACK_W)

    out = {}
    for j, k in enumerate(BIG):
        own, other = reduced[j], flat_other[L * j:L * (j + 1)]
        if k == "w_in":
            t = lambda a: jnp.swapaxes(a, 1, 2)
            res = _adamw_halves(t(env[k]), t(env["m_" + k]), t(env["v_" + k]), own, other, place, "cols", f"adamw_{k}")
            res = [t(a) for a in res]
        else:
            if k in COL_SHARDED:
                own, other = [a.T for a in own], [a.T for a in other]
            res = _adamw_halves(env[k], env["m_" + k], env["v_" + k], own, other, place,
                                "rows" if k in COL_SHARDED else "cols", f"adamw_{k}")
        out[k] = tuple(res)

    dmod_all = g_all[:, :L * 6].reshape(N_DEV, L, 6 * D_MODEL)
    dmod_sh = jnp.transpose(lax.dynamic_slice_in_dim(dmod_all, chip * ada_cols, ada_cols, axis=2), (1, 0, 2))
    g_ada = _mod_bwd(c_all, dmod_sh, "mod_bwd")
    g_, d_, m_, v_ = _adamw(w_ada.reshape(-1, ada_cols), m_w_ada.reshape(-1, ada_cols), v_w_ada.reshape(-1, ada_cols),
                            [g_ada.reshape(-1, ada_cols)], "adamw_w_ada")
    out["w_ada"] = tuple(a.reshape(w_ada.shape) for a in (g_, d_, m_, v_))

    def small_pack(prefix):
        parts = [env[prefix + "b_ada"]]
        for k in small_names:
            a = env[prefix + k]
            if k == "conv_w":
                a = jnp.zeros((L,) + shapes[k], F32)
            parts.append(a)
        return _pack(parts + [jnp.zeros_like(loss_part)])

    gp, dp, mp, vp = _adamw_small(small_pack(""), small_pack("m_"), small_pack("v_"), g_all, "adamw_small")
    full_shapes = [(L,) + shapes["b_ada"]] + [(L,) + shapes[k] for k in small_names]
    loss = _unpack(gp, full_shapes + [(128,)])[-1][0]
    for tag, packed in (("g", gp), ("d", dp), ("m", mp), ("v", vp)):
        for k, a in zip(["b_ada"] + small_names, _unpack(packed, full_shapes)):
            out.setdefault(k, {})
            out[k][tag] = a
    g_cw_full = out["conv_w"]["g"]
    cw_cols = D_CONV // N_CHIP
    g_cw = lax.dynamic_slice_in_dim(g_cw_full, chip * cw_cols, cw_cols, axis=2)
    pad_rows = lambda a: jnp.pad(a.reshape(L * CONV_WIDTH, cw_cols), ((0, 2), (0, 0)))
    g_, d_, m_, v_ = _adamw(pad_rows(conv_w), pad_rows(m_conv_w), pad_rows(v_conv_w), [pad_rows(g_cw)], "adamw_conv_w")
    out["conv_w"] = tuple(a[:L * CONV_WIDTH].reshape(L, CONV_WIDTH, cw_cols) for a in (g_, d_, m_, v_))

    names = ["w_ada", "b_ada", "w_in", "b_gate", "w_pool", "pool_scale", "rel_bias", "conv_w", "conv_b", "conv_ln_g",
             "conv_ln_b", "w_br_pool", "w_br_attn", "w_br_conv", "w_o", "ln_mix_g", "ln_mix_b", "w_ff1", "b_ff1",
             "w_ff2", "b_ff2", "ln_ff_g", "ln_ff_b"]

    def pick(k, i):
        o = out[k]
        return o[i] if isinstance(o, tuple) else o["gdmv"[i]].reshape(env[k].shape)

    return (loss, grad_x, *[pick(k, 0) for k in names], *[pick(k, 1) for k in names],
            *[pick(k, 2) for k in names], *[pick(k, 3) for k in names])
```

```python
import jax
import jax.numpy as jnp
import numpy as np
from jax import lax
from jax.experimental import pallas as pl
from jax.experimental.pallas import tpu as pltpu

F32 = jnp.float32
BF16 = jnp.bfloat16

D_MODEL = 1024
DEPTH = 2
CHUNK = 64
POOL_WINDOWS = (2, 4, 8, 16)
POOL_GROUP = 64
D_POOL = 256
N_HEADS = 8
HEAD_DIM = 64
D_ATTN = 512
N_PREV_CHUNKS = 8
REL_CLIP = 128
N_REL = 2 * REL_CLIP + 1
D_CONV = 256
CONV_WIDTH = 31
D_FF = 4 * D_MODEL
D_IN = 5376
OFF_POOL, OFF_QKV, OFF_CONV, OFF_GATE = 0, 256, 1792, 2304
ALPHA = (2.0 * DEPTH) ** 0.25
LN_EPS = 1e-5
NEG_INF = -1e30
ADAM_LR, ADAM_B1, ADAM_B2, ADAM_EPS, ADAM_WD, ADAM_STEP = 0.001, 0.9, 0.999, 1e-08, 0.01, 10

N_DEV = 8
N_CHIP = 4
MESH = pl.DeviceIdType.MESH

QB = 2 * CHUNK
KPAD = N_PREV_CHUNKS * CHUNK
KW = QB + KPAD
SKEW_W = 768

VMEM_LIMIT = 56 * 1024 * 1024


def _cparams(**kw):
    return pltpu.CompilerParams(vmem_limit_bytes=VMEM_LIMIT, **kw)


def _full(shape):
    n = len(shape)
    return pl.BlockSpec(shape, lambda *_: (0,) * n)


_DIMS = {"nn": (((1,), (0,)), ((), ())), "nt": (((1,), (1,)), ((), ())), "tn": (((0,), (0,)), ((), ()))}


def _relu2(t):
    r = jnp.maximum(t, 0.0)
    return r * r


def _mm(a, b, mode, *, tm, tn, out_dtype, name, b_col0=0, n_out=None, bias=None, split_n=0, rider=None):
    if mode == "tn":
        k, m = a.shape
        n = b.shape[1] if n_out is None else n_out
        a_spec = pl.BlockSpec((k, tm), lambda i, j: (0, i))
        b_spec = pl.BlockSpec((k, tn), lambda i, j: (0, j + b_col0))
    elif mode == "nn":
        m, k = a.shape
        n = b.shape[1] if n_out is None else n_out
        a_spec = pl.BlockSpec((tm, k), lambda i, j: (i, 0))
        b_spec = pl.BlockSpec((k, tn), lambda i, j: (0, j + b_col0))
    else:
        m, k = a.shape
        n = b.shape[0] if n_out is None else n_out
        a_spec = pl.BlockSpec((tm, k), lambda i, j: (i, 0))
        b_spec = pl.BlockSpec((tn, k), lambda i, j: (j + b_col0, 0))
    assert m % tm == 0 and n % tn == 0, (name, m, n, tm, tn)
    dims = _DIMS[mode]

    def body(*refs):
        if bias is None:
            a_ref, b_ref, o_ref = refs
        else:
            a_ref, b_ref, bias_ref, o_ref = refs
        acc = lax.dot_general(a_ref[...].astype(BF16), b_ref[...].astype(BF16), dims, preferred_element_type=F32)
        if bias is not None:
            acc = acc + bias_ref[...]
        if split_n:
            for c in range(tn // split_n):
                o_ref[c] = acc[:, c * split_n:(c + 1) * split_n].astype(out_dtype)
        else:
            o_ref[...] = acc.astype(out_dtype)

    in_specs = [a_spec, b_spec]
    args = [a, b]
    if bias is not None:
        in_specs.append(pl.BlockSpec((1, tn), lambda i, j: (0, j)))
        args.append(bias)
    if split_n:
        out_spec = pl.BlockSpec((tn // split_n, tm, split_n), lambda i, j: (j, i, 0))
        out_shape = jax.ShapeDtypeStruct((n // split_n, m, split_n), out_dtype)
    else:
        out_spec = pl.BlockSpec((tm, tn), lambda i, j: (i, j))
        out_shape = jax.ShapeDtypeStruct((m, n), out_dtype)
    res = _call(body, name=name, grid=(m // tm, n // tn), in_specs=in_specs, out_specs=[out_spec],
                out_shape=[out_shape], scratch_shapes=[], args=args, rider=rider)
    return res[0] if rider is None else (res[0][0], res[1])


def _ln_hat(x):
    mu = jnp.mean(x, axis=-1, keepdims=True)
    xc = x - mu
    var = jnp.mean(xc * xc, axis=-1, keepdims=True)
    rstd = lax.rsqrt(var + LN_EPS)
    return xc * rstd, rstd


def _ln_hat_bwd(dhat, xhat, rstd):
    m1 = jnp.mean(dhat, axis=-1, keepdims=True)
    m2 = jnp.mean(dhat * xhat, axis=-1, keepdims=True)
    return rstd * (dhat - m1 - xhat * m2)


def _row_tile(s):
    return min(512, s)


def _acc_rows(ref, val, first):
    @pl.when(first)
    def _():
        ref[...] = jnp.zeros_like(ref)
    ref[...] += jnp.sum(val, axis=0, keepdims=True)


def _ln_mod(x, sc, sh, name):
    s, d = x.shape
    tm = _row_tile(s)

    def body(x_ref, sc_ref, sh_ref, u_ref):
        xhat, _ = _ln_hat(x_ref[...])
        u_ref[...] = (xhat * (1.0 + sc_ref[...]) + sh_ref[...]).astype(BF16)

    row = pl.BlockSpec((tm, d), lambda i: (i, 0))
    vec = pl.BlockSpec((1, d), lambda i: (0, 0))
    return pl.pallas_call(body, grid=(s // tm,), in_specs=[row, vec, vec], out_specs=row,
                          out_shape=jax.ShapeDtypeStruct((s, d), BF16), name=name, compiler_params=_cparams())(x, sc, sh)


def _resid_bwd_tile(dxo, x, f, g, gam):
    rhat, rstd = _ln_hat(ALPHA * x + g * f)
    dr = _ln_hat_bwd(dxo * gam, rhat, rstd)
    return ALPHA * dr, g * dr, dxo * rhat, dr * f


def _mm_ln_mod_bwd(a, b, x, sc, dres, name, rider=None, nxt=None):
    segs = list(a) if isinstance(a, (list, tuple)) else [a]
    s = segs[0].shape[0]
    k, d = b.shape
    assert sum(t.shape[1] for t in segs) == k
    tm = min(512 if k <= 4096 and nxt is None else 256, s)
    ns = len(segs)

    def body(*refs):
        seg_refs = refs[:ns]
        if nxt is None:
            b_ref, x_ref, sc_ref, dres_ref, dx_ref, dsc_ref, dsh_ref = refs[ns:]
        else:
            (b_ref, x_ref, sc_ref, dres_ref, xp_ref, fp_ref, gp_ref, gamp_ref,
             dresp_ref, dfp_ref, dsc_ref, dsh_ref, dgam_ref, dbet_ref, dg_ref, dbias_ref) = refs[ns:]
        first = pl.program_id(0) == 0
        duv, r0 = None, 0
        for seg_ref in seg_refs:
            w = seg_ref.shape[1]
            part = jnp.dot(seg_ref[...], b_ref[r0:r0 + w, :], preferred_element_type=F32)
            duv = part if duv is None else duv + part
            r0 += w
        xhat, rstd = _ln_hat(x_ref[...])
        dxv = dres_ref[...] + _ln_hat_bwd(duv * (1.0 + sc_ref[...]), xhat, rstd)
        _acc_rows(dsc_ref, duv * xhat, first)
        _acc_rows(dsh_ref, duv, first)
        if nxt is None:
            dx_ref[...] = dxv
        else:
            dresp, dfp, t_gam, t_g = _resid_bwd_tile(dxv, xp_ref[...], fp_ref[...], gp_ref[...], gamp_ref[...])
            dresp_ref[...] = dresp
            dfp_ref[...] = dfp.astype(BF16)
            _acc_rows(dgam_ref, t_gam, first)
            _acc_rows(dbet_ref, dxv, first)
            _acc_rows(dg_ref, t_g, first)
            _acc_rows(dbias_ref, dfp, first)

    row = pl.BlockSpec((tm, d), lambda i: (i, 0))
    vec = pl.BlockSpec((1, d), lambda i: (0, 0))
    vs = jax.ShapeDtypeStruct((1, d), F32)
    rows = jax.ShapeDtypeStruct((s, d), F32)
    in_specs = [pl.BlockSpec((tm, t.shape[1]), lambda i: (i, 0)) for t in segs] + [_full((k, d)), row, vec, row]
    args = (*segs, b, x, sc, dres)
    if nxt is None:
        out_specs, out_shape = [row, vec, vec], [rows, vs, vs]
    else:
        in_specs += [row, row, vec, vec]
        args += tuple(nxt)
        out_specs = [row, row] + [vec] * 6
        out_shape = [rows, jax.ShapeDtypeStruct((s, d), BF16)] + [vs] * 6
    res = _call(body, name=name, grid=(s // tm,), in_specs=in_specs, out_specs=out_specs, out_shape=out_shape,
                scratch_shapes=[], args=args, rider=rider)
    return tuple(res) if rider is None else (tuple(res[0]), res[1])


def _dw_segments(segs, u, name):
    s, d = u.shape
    tw = 256
    tiles = [t.shape[1] // tw for t in segs]
    starts = [sum(tiles[:j]) for j in range(len(segs))]
    ns = len(segs)

    def body(*refs):
        seg_refs, u_ref, o_ref = refs[:ns], refs[ns], refs[ns + 1]
        i = pl.program_id(0)
        for seg_ref, t0, nt in zip(seg_refs, starts, tiles):
            @pl.when((i >= t0) & (i < t0 + nt))
            def _(seg_ref=seg_ref):
                acc = lax.dot_general(seg_ref[...], u_ref[...], _DIMS["tn"], preferred_element_type=F32)
                o_ref[0] = acc[:, :d // 2].astype(BF16)
                o_ref[1] = acc[:, d // 2:].astype(BF16)

    def seg_spec(t0, nt):
        return pl.BlockSpec((s, tw), lambda i: (0, jnp.clip(i - t0, 0, nt - 1)))

    return pl.pallas_call(
        body, grid=(sum(tiles),), in_specs=[seg_spec(t0, nt) for t0, nt in zip(starts, tiles)] + [_full((s, d))],
        out_specs=pl.BlockSpec((2, tw, d // 2), lambda i: (0, i, 0)),
        out_shape=jax.ShapeDtypeStruct((2, sum(tiles) * tw, d // 2), BF16), name=name, compiler_params=_cparams(),
    )(*segs, u)


def _mm_resid_ln(a, b, bias, x, g, gam, bet, name, rider=None, mod_next=None):
    s, k = a.shape
    d = b.shape[1]
    tm = min(512, s)
    nb, nm = int(bias is not None), 2 * int(mod_next is not None)

    def body(*refs):
        a_ref, b_ref = refs[:2]
        x_ref, g_ref, gam_ref, bet_ref = refs[2 + nb:6 + nb]
        f_ref, o_ref = refs[6 + nb + nm:8 + nb + nm]
        f = jnp.dot(a_ref[...], b_ref[...], preferred_element_type=F32)
        if bias is not None:
            f = f + refs[2][...]
        f_ref[...] = f
        rhat, _ = _ln_hat(ALPHA * x_ref[...] + g_ref[...] * f)
        y = rhat * gam_ref[...] + bet_ref[...]
        o_ref[...] = y
        if mod_next is not None:
            sc_ref, sh_ref = refs[6 + nb:8 + nb]
            yhat, _ = _ln_hat(y)
            refs[8 + nb + nm][...] = (yhat * (1.0 + sc_ref[...]) + sh_ref[...]).astype(BF16)

    row = pl.BlockSpec((tm, d), lambda i: (i, 0))
    vec = pl.BlockSpec((1, d), lambda i: (0, 0))
    in_specs = [pl.BlockSpec((tm, k), lambda i: (i, 0)), _full((k, d))] + [vec] * nb + [row, vec, vec, vec] + [vec] * nm
    args = [a, b] + ([bias] if nb else []) + [x, g, gam, bet] + (list(mod_next) if nm else [])
    sh = jax.ShapeDtypeStruct((s, d), F32)
    out_specs, out_shape = [row, row], [sh, sh]
    if nm:
        out_specs, out_shape = out_specs + [row], out_shape + [jax.ShapeDtypeStruct((s, d), BF16)]
    res = _call(body, name=name, grid=(s // tm,), in_specs=in_specs, out_specs=out_specs, out_shape=out_shape,
                scratch_shapes=[], args=args, rider=rider)
    return tuple(res) if rider is None else (tuple(res[0]), res[1])


def _resid_ln_bwd(dxo, x, f, g, gam, name, tgt=None):
    s, d = x.shape
    tm = _row_tile(s)
    n = s // tm

    def body(*refs):
        if tgt is None:
            dxo_ref, x_ref, f_ref, g_ref, gam_ref, dres_ref, df_ref, dgam_ref, dbet_ref, dg_ref, dbias_ref = refs
            dxov = dxo_ref[...]
        else:
            (dxo_ref, t_ref, x_ref, f_ref, g_ref, gam_ref, dres_ref, df_ref, dgam_ref, dbet_ref, dg_ref, dbias_ref,
             loss_ref, sq_ref) = refs
            err = dxo_ref[...] - t_ref[...]
            dxov = err * (1.0 / d)
            _acc_rows(sq_ref, err * err, pl.program_id(0) == 0)

            @pl.when(pl.program_id(0) == n - 1)
            def _():
                tot = jnp.sum(sq_ref[...], axis=1, keepdims=True) * (0.5 / d)
                loss_ref[...] = jnp.broadcast_to(tot, (1, 128))

        first = pl.program_id(0) == 0
        dres, dfv, t_gam, t_g = _resid_bwd_tile(dxov, x_ref[...], f_ref[...], g_ref[...], gam_ref[...])
        dres_ref[...] = dres
        df_ref[...] = dfv.astype(BF16)
        _acc_rows(dgam_ref, t_gam, first)
        _acc_rows(dbet_ref, dxov, first)
        _acc_rows(dg_ref, t_g, first)
        _acc_rows(dbias_ref, dfv, first)

    row = pl.BlockSpec((tm, d), lambda i: (i, 0))
    vec = pl.BlockSpec((1, d), lambda i: (0, 0))
    vs = jax.ShapeDtypeStruct((1, d), F32)
    out_specs = [row, row, vec, vec, vec, vec]
    out_shape = [jax.ShapeDtypeStruct((s, d), F32), jax.ShapeDtypeStruct((s, d), BF16), vs, vs, vs, vs]
    if tgt is None:
        return pl.pallas_call(body, grid=(n,), in_specs=[row, row, row, vec, vec], out_specs=out_specs,
                              out_shape=out_shape, name=name, compiler_params=_cparams())(dxo, x, f, g, gam)
    return pl.pallas_call(body, grid=(n,), in_specs=[row, row, row, row, vec, vec],
                          out_specs=out_specs + [pl.BlockSpec((1, 128), lambda i: (0, 0))],
                          out_shape=out_shape + [jax.ShapeDtypeStruct((1, 128), F32)],
                          scratch_shapes=[pltpu.VMEM((1, d), F32)], name=name,
                          compiler_params=_cparams())(dxo, tgt, x, f, g, gam)


POOL_HALO = 16
POOL_ROWS = 256


def _pool_counts(r0, rows):
    t1 = (lax.broadcasted_iota(jnp.int32, (rows, 128), 0) + r0 + 1).astype(F32)
    low = lax.broadcasted_iota(jnp.int32, (rows, 128), 1) < POOL_GROUP
    wa = jnp.where(low, float(POOL_WINDOWS[0]), float(POOL_WINDOWS[1]))
    wb = jnp.where(low, float(POOL_WINDOWS[2]), float(POOL_WINDOWS[3]))
    return jnp.minimum(t1, wa), jnp.minimum(t1, wb), low


def _window_sums(win, off, rows, sign):
    def sl(j, half):
        return win[off + sign * j: off + sign * j + rows, 128 * half:128 * half + 128]
    a2 = sl(0, 0) + sl(1, 0)
    a4 = a2 + sl(2, 0) + sl(3, 0)
    a8 = sl(0, 1)
    for j in range(1, 8):
        a8 = a8 + sl(j, 1)
    a16 = a8
    for j in range(8, 16):
        a16 = a16 + sl(j, 1)
    return a2, a4, a8, a16


def _pool_fwd(zp, wp_bd, pscale, name):
    s = zp.shape[0]
    r = min(POOL_ROWS, s)

    def body(z_ref, wp_ref, sc_ref, p_ref, feat_ref, pad):
        pad[0:POOL_HALO, :] = jnp.zeros((POOL_HALO, D_POOL), F32)
        pad[POOL_HALO:, :] = z_ref[...]

        def step(i, carry):
            r0 = pl.multiple_of(i * r, r)
            win = pad[pl.ds(r0, r + POOL_HALO), :]
            a2, a4, a8, a16 = _window_sums(win, POOL_HALO, r, -1)
            ca, cb, low = _pool_counts(r0, r)
            x0 = win[POOL_HALO:, :]
            pa = jnp.where(low, a2, a4) / ca
            pb = jnp.where(low, a8, a16) / cb
            p = (jnp.concatenate([pa, pb], axis=1) - x0).astype(BF16)
            p_ref[pl.ds(r0, r), :] = p
            pw = jnp.dot(p, wp_ref[...], preferred_element_type=F32)
            feat_ref[pl.ds(r0, r), :] = (pw * sc_ref[...]).astype(BF16)
            return carry

        lax.fori_loop(0, s // r, step, 0)

    return pl.pallas_call(
        body, out_shape=[jax.ShapeDtypeStruct((s, D_POOL), BF16), jax.ShapeDtypeStruct((s, D_POOL), BF16)],
        scratch_shapes=[pltpu.VMEM((s + POOL_HALO, D_POOL), F32)], name=name, compiler_params=_cparams(),
    )(zp, wp_bd, pscale)


def _pool_bwd(dfeat, p, wp_bd, pscale, name):
    s = p.shape[0]
    r = min(POOL_ROWS, s)

    def body(df_ref, p_ref, wp_ref, sc_ref, dz_ref, dwp_ref, dsc_ref, gpad, dpbuf):
        dwp_ref[...] = jnp.zeros_like(dwp_ref)
        dsc_ref[...] = jnp.zeros_like(dsc_ref)
        gpad[s:, :] = jnp.zeros((POOL_HALO, D_POOL), F32)

        def step1(i, carry):
            r0 = pl.multiple_of(i * r, r)
            pv = p_ref[pl.ds(r0, r), :]
            dfv = df_ref[pl.ds(r0, r), :]
            pw = jnp.dot(pv, wp_ref[...], preferred_element_type=F32)
            dsc_ref[...] += jnp.sum(dfv * pw, axis=0, keepdims=True)
            dpw = (dfv * sc_ref[...]).astype(BF16)
            dwp_ref[...] += lax.dot_general(pv, dpw, _DIMS["tn"], preferred_element_type=F32)
            dp = lax.dot_general(dpw, wp_ref[...], _DIMS["nt"], preferred_element_type=F32)
            ca, cb, _ = _pool_counts(r0, r)
            gpad[pl.ds(r0, r), :] = dp / jnp.concatenate([ca, cb], axis=1)
            dpbuf[pl.ds(r0, r), :] = dp
            return carry

        lax.fori_loop(0, s // r, step1, 0)

        def step2(i, carry):
            r0 = pl.multiple_of(i * r, r)
            win = gpad[pl.ds(r0, r + POOL_HALO), :]
            a2, a4, a8, a16 = _window_sums(win, 0, r, 1)
            low = lax.broadcasted_iota(jnp.int32, (r, 128), 1) < POOL_GROUP
            acc = jnp.concatenate([jnp.where(low, a2, a4), jnp.where(low, a8, a16)], axis=1)
            dz_ref[pl.ds(r0, r), :] = (acc - dpbuf[pl.ds(r0, r), :]).astype(BF16)
            return carry

        lax.fori_loop(0, s // r, step2, 0)

    return pl.pallas_call(
        body,
        out_shape=[jax.ShapeDtypeStruct((s, D_POOL), BF16), jax.ShapeDtypeStruct((D_POOL, D_POOL), F32),
                   jax.ShapeDtypeStruct((1, D_POOL), F32)],
        scratch_shapes=[pltpu.VMEM((s + POOL_HALO, D_POOL), F32), pltpu.VMEM((s, D_POOL), F32)],
        name=name, compiler_params=_cparams(),
    )(dfeat, p, wp_bd, pscale)


def _skew_index():
    cp = lax.broadcasted_iota(jnp.int32, (SKEW_W, N_REL), 0)
    dist = jnp.where(cp < KW, KPAD - cp, KPAD + SKEW_W - cp)
    idx = jnp.clip(dist, -REL_CLIP, REL_CLIP) + REL_CLIP
    return (idx == lax.broadcasted_iota(jnp.int32, (SKEW_W, N_REL), 1)).astype(F32)


def _row_bits(b):
    return (lax.broadcasted_iota(jnp.int32, (QB, SKEW_W), 0) >> b) & 1 == 1


N_EDGE = KPAD // QB


def _bias_block(rel_bias, name):
    def body(rb_ref, o_ref):
        onehot = _skew_index()
        row0 = lax.dot_general(rb_ref[...], onehot, _DIMS["nt"], precision=lax.Precision.HIGHEST,
                               preferred_element_type=F32)
        r = lax.broadcasted_iota(jnp.int32, (QB, KW), 0)
        kk = lax.broadcasted_iota(jnp.int32, (QB, KW), 1)
        cq, ck = r // CHUNK, kk // CHUNK
        band = (ck >= cq) & (ck <= cq + N_PREV_CHUNKS)
        for h in range(N_HEADS):
            t = jnp.broadcast_to(row0[h:h + 1, :], (QB, SKEW_W))
            for b in range(7):
                t = jnp.where(_row_bits(b), pltpu.roll(t, 1 << b, 1), t)
            for e in range(N_EDGE + 1):
                o_ref[e, h] = jnp.where(band & (kk >= KPAD - e * QB), t[:, :KW], NEG_INF)

    return pl.pallas_call(body, out_shape=jax.ShapeDtypeStruct((N_EDGE + 1, N_HEADS, QB, KW), F32), name=name,
                          compiler_params=_cparams())(rel_bias)


def _bias_spec():
    return pl.BlockSpec((None, N_HEADS, QB, KW), lambda i: (jnp.minimum(i, N_EDGE), 0, 0, 0))


def _bias_block_bwd(ds_acc, name):
    def body(ds_ref, o_ref):
        sums = []
        for h in range(N_HEADS):
            t = jnp.concatenate([ds_ref[h], jnp.zeros((QB, SKEW_W - KW), F32)], axis=1)
            for b in range(7):
                t = jnp.where(_row_bits(b), pltpu.roll(t, SKEW_W - (1 << b), 1), t)
            sums.append(jnp.sum(t, axis=0, keepdims=True))
        allh = jnp.concatenate(sums, axis=0)
        o_ref[...] = jnp.dot(allh, _skew_index(), precision=lax.Precision.HIGHEST, preferred_element_type=F32)

    return pl.pallas_call(body, out_shape=jax.ShapeDtypeStruct((N_HEADS, N_REL), F32), name=name,
                          compiler_params=_cparams())(ds_acc)


def _scaled(q):
    return (q.astype(F32) * (HEAD_DIM ** -0.5)).astype(BF16)


def _probs(q, kw, bias_ref):
    sc = jnp.stack([lax.dot_general(q[:, HEAD_DIM * h:HEAD_DIM * (h + 1)], kw[:, HEAD_DIM * h:HEAD_DIM * (h + 1)],
                                    _DIMS["nt"], preferred_element_type=F32) + bias_ref[h] for h in range(N_HEADS)])
    e = jnp.exp(sc - jnp.max(sc, axis=-1, keepdims=True))
    return e * (1.0 / jnp.sum(e, axis=-1, keepdims=True))


def _load_padded_kv(qkv_hbm, kpad, vpad, sems, s):
    kpad[0:KPAD, :] = jnp.zeros((KPAD, D_ATTN), BF16)
    vpad[0:KPAD, :] = jnp.zeros((KPAD, D_ATTN), BF16)
    ck = pltpu.make_async_copy(qkv_hbm.at[:, D_ATTN:2 * D_ATTN], kpad.at[pl.ds(KPAD, s), :], sems.at[0])
    cv = pltpu.make_async_copy(qkv_hbm.at[:, 2 * D_ATTN:3 * D_ATTN], vpad.at[pl.ds(KPAD, s), :], sems.at[1])
    ck.start()
    cv.start()
    ck.wait()
    cv.wait()


def _attn_fwd(qkv, bias, name, rider=None):
    s = qkv.shape[0]

    def body(q_ref, qkv_hbm, bias_ref, o_ref, p_ref, kpad, vpad, sems):
        i = pl.program_id(0)

        @pl.when(i == 0)
        def _():
            _load_padded_kv(qkv_hbm, kpad, vpad, sems, s)

        base = pl.multiple_of(i * QB, QB)
        kw = kpad[pl.ds(base, KW), :]
        vw = vpad[pl.ds(base, KW), :]
        q = _scaled(q_ref[...])
        p = _probs(q, kw, bias_ref).astype(BF16)
        p_ref[...] = p
        outs = [jnp.dot(p[h], vw[:, HEAD_DIM * h:HEAD_DIM * (h + 1)], preferred_element_type=F32)
                for h in range(N_HEADS)]
        o_ref[...] = jnp.concatenate(outs, axis=1).astype(BF16)

    res = _call(
        body, name=name, grid=(s // QB,),
        in_specs=[pl.BlockSpec((QB, D_ATTN), lambda i: (i, 0)), pl.BlockSpec(memory_space=pl.ANY),
                  _bias_spec()],
        out_specs=[pl.BlockSpec((QB, D_ATTN), lambda i: (i, 0)), _probs_spec()],
        out_shape=[jax.ShapeDtypeStruct((s, D_ATTN), BF16), jax.ShapeDtypeStruct((N_HEADS, s, KW), BF16)],
        scratch_shapes=[pltpu.VMEM((s + KPAD, D_ATTN), BF16), pltpu.VMEM((s + KPAD, D_ATTN), BF16),
                        pltpu.SemaphoreType.DMA((2,))],
        args=(qkv, qkv, bias), rider=rider)
    return tuple(res) if rider is None else (tuple(res[0]), res[1])


def _probs_spec():
    return pl.BlockSpec((N_HEADS, QB, KW), lambda i: (0, i, 0))


def _attn_bwd(qkv, do, probs, name, rider=None):
    s = qkv.shape[0]
    n = s // QB

    def body(q_ref, qkv_hbm, do_ref, p_ref, dq_ref, dk_hbm, dv_hbm, ds_ref, kpad, vpad, dkacc, dvacc, sems):
        i = pl.program_id(0)

        @pl.when(i == 0)
        def _():
            _load_padded_kv(qkv_hbm, kpad, vpad, sems, s)
            dkacc[...] = jnp.zeros_like(dkacc)
            dvacc[...] = jnp.zeros_like(dvacc)
            ds_ref[...] = jnp.zeros_like(ds_ref)

        base = pl.multiple_of(i * QB, QB)
        kw = kpad[pl.ds(base, KW), :]
        vw = vpad[pl.ds(base, KW), :]
        q = _scaled(q_ref[...])
        dov = do_ref[...]
        heads = [slice(HEAD_DIM * h, HEAD_DIM * (h + 1)) for h in range(N_HEADS)]
        pb = p_ref[...]
        p = pb.astype(F32)
        dp = jnp.stack([lax.dot_general(dov[:, hs], vw[:, hs], _DIMS["nt"], preferred_element_type=F32) for hs in heads])
        ds = p * (dp - jnp.sum(dp * p, axis=-1, keepdims=True))
        ds_ref[...] += ds
        dsb = ds.astype(BF16)
        dvs = [lax.dot_general(pb[h], dov[:, hs], _DIMS["tn"], preferred_element_type=F32) for h, hs in enumerate(heads)]
        dqs = [jnp.dot(dsb[h], kw[:, hs], preferred_element_type=F32) for h, hs in enumerate(heads)]
        dks = [lax.dot_general(dsb[h], q[:, hs], _DIMS["tn"], preferred_element_type=F32) for h, hs in enumerate(heads)]
        dq_ref[...] = (jnp.concatenate(dqs, axis=1) * (HEAD_DIM ** -0.5)).astype(BF16)
        dkacc[pl.ds(base, KW), :] += jnp.concatenate(dks, axis=1)
        dvacc[pl.ds(base, KW), :] += jnp.concatenate(dvs, axis=1)

        @pl.when(i == n - 1)
        def _():
            def cast(j, carry):
                rows = pl.ds(pl.multiple_of(KPAD + j * 512, 512), 512)
                kpad[rows, :] = dkacc[rows, :].astype(BF16)
                vpad[rows, :] = dvacc[rows, :].astype(BF16)
                return carry

            lax.fori_loop(0, s // 512, cast, 0)
            ck = pltpu.make_async_copy(kpad.at[pl.ds(KPAD, s), :], dk_hbm, sems.at[0])
            cv = pltpu.make_async_copy(vpad.at[pl.ds(KPAD, s), :], dv_hbm, sems.at[1])
            ck.start()
            cv.start()
            ck.wait()
            cv.wait()

    blk = pl.BlockSpec((QB, D_ATTN), lambda i: (i, 0))
    acc_shape = jax.ShapeDtypeStruct((s, D_ATTN), BF16)
    return _call(
        body, name=name, grid=(n,),
        in_specs=[blk, pl.BlockSpec(memory_space=pl.ANY), blk, _probs_spec()],
        out_specs=[blk, pl.BlockSpec(memory_space=pl.ANY), pl.BlockSpec(memory_space=pl.ANY), _full((N_HEADS, QB, KW))],
        out_shape=[jax.ShapeDtypeStruct((s, D_ATTN), BF16), acc_shape, acc_shape,
                   jax.ShapeDtypeStruct((N_HEADS, QB, KW), F32)],
        scratch_shapes=[pltpu.VMEM((s + KPAD, D_ATTN), BF16), pltpu.VMEM((s + KPAD, D_ATTN), BF16),
                        pltpu.VMEM((s + KPAD, D_ATTN), F32), pltpu.VMEM((s + KPAD, D_ATTN), F32),
                        pltpu.SemaphoreType.DMA((2,))],
        args=(qkv, qkv, do, probs), rider=rider)


CONV_HALO = 32
CONV_ROWS = 64


def _sigmoid(t):
    return 1.0 / (1.0 + jnp.exp(-t))


CONV_WIN = CONV_ROWS + CONV_HALO - 8


def _row_windows(ref, r0, buf):
    win = ref[pl.ds(r0, CONV_ROWS + CONV_HALO), :]
    for j in range(1, 8):
        buf[j - 1] = win[j:j + CONV_WIN, :]

    def get(o):
        j, a = o % 8, o - o % 8
        if j == 0:
            return ref[pl.ds(r0 + a, CONV_ROWS), :]
        return buf[j - 1, a:a + CONV_ROWS, :]

    return get


def _glu_rows(z_ref, r0, rows):
    a = z_ref[pl.ds(r0, rows), 0:D_CONV]
    b = z_ref[pl.ds(r0, rows), D_CONV:2 * D_CONV]
    return a, _sigmoid(b)


def _conv_fwd(zc, conv_w, conv_b, ln_g, ln_b, name):
    s = zc.shape[0]
    rt = min(256, s)

    def body(z_ref, w_ref, cb_ref, g_ref, b_ref, cv_ref, feat_ref, hpad, shifts):
        hpad[0:CONV_HALO, :] = jnp.zeros((CONV_HALO, D_CONV), F32)

        def glu(i, carry):
            r0 = pl.multiple_of(i * rt, rt)
            a, sb = _glu_rows(z_ref, r0, rt)
            hpad[pl.ds(r0 + CONV_HALO, rt), :] = a * sb
            return carry

        lax.fori_loop(0, s // rt, glu, 0)
        w = w_ref[...]

        def conv(i, carry):
            r0 = pl.multiple_of(i * CONV_ROWS, CONV_ROWS)
            win = _row_windows(hpad, r0, shifts)
            acc = jnp.broadcast_to(cb_ref[...], (CONV_ROWS, D_CONV))
            for k in range(CONV_WIDTH):
                acc = acc + win(2 + k) * w[k:k + 1, :]
            cv_ref[pl.ds(r0, CONV_ROWS), :] = acc
            yhat, _ = _ln_hat(acc)
            y = yhat * g_ref[...] + b_ref[...]
            feat_ref[pl.ds(r0, CONV_ROWS), :] = (y * _sigmoid(y)).astype(BF16)
            return carry

        lax.fori_loop(0, s // CONV_ROWS, conv, 0)

    return pl.pallas_call(
        body, out_shape=[jax.ShapeDtypeStruct((s, D_CONV), F32), jax.ShapeDtypeStruct((s, D_CONV), BF16)],
        scratch_shapes=[pltpu.VMEM((s + CONV_HALO, D_CONV), F32), pltpu.VMEM((7, CONV_WIN, D_CONV), F32)],
        name=name, compiler_params=_cparams(),
    )(zc, conv_w, conv_b, ln_g, ln_b)


def _conv_bwd(dfeat, cv, zc, conv_w, ln_g, ln_b, name):
    s = zc.shape[0]
    rt = min(256, s)

    def body(df_ref, cv_ref, z_ref, w_ref, g_ref, b_ref, dz_ref, dw_ref, dcb_ref, dg_ref, db_ref, hpad, dcvpad, dwacc,
             hshifts, dshifts):
        hpad[0:CONV_HALO, :] = jnp.zeros((CONV_HALO, D_CONV), F32)
        dcvpad[s:, :] = jnp.zeros((CONV_HALO, D_CONV), F32)
        dwacc[...] = jnp.zeros_like(dwacc)
        dcb_ref[...] = jnp.zeros_like(dcb_ref)
        dg_ref[...] = jnp.zeros_like(dg_ref)
        db_ref[...] = jnp.zeros_like(db_ref)

        def pass1(i, carry):
            r0 = pl.multiple_of(i * rt, rt)
            a, sb = _glu_rows(z_ref, r0, rt)
            hpad[pl.ds(r0 + CONV_HALO, rt), :] = a * sb
            cvhat, rstd = _ln_hat(cv_ref[pl.ds(r0, rt), :])
            y = cvhat * g_ref[...] + b_ref[...]
            sg = _sigmoid(y)
            dy = df_ref[pl.ds(r0, rt), :] * (sg * (1.0 + y * (1.0 - sg)))
            dg_ref[...] += jnp.sum(dy * cvhat, axis=0, keepdims=True)
            db_ref[...] += jnp.sum(dy, axis=0, keepdims=True)
            dcv = _ln_hat_bwd(dy * g_ref[...], cvhat, rstd)
            dcb_ref[...] += jnp.sum(dcv, axis=0, keepdims=True)
            dcvpad[pl.ds(r0, rt), :] = dcv
            return carry

        lax.fori_loop(0, s // rt, pass1, 0)
        w = w_ref[...]

        def pass2(i, carry):
            r0 = pl.multiple_of(i * CONV_ROWS, CONV_ROWS)
            dwin = _row_windows(dcvpad, r0, dshifts)
            hwin = _row_windows(hpad, r0, hshifts)
            dcv = dwin(0)
            dh = jnp.zeros((CONV_ROWS, D_CONV), F32)
            for k in range(CONV_WIDTH):
                dh = dh + dwin(30 - k) * w[k:k + 1, :]
                prod = dcv * hwin(2 + k)
                dwacc[8 * k:8 * k + 8, :] += jnp.sum(prod.reshape(CONV_ROWS // 8, 8, D_CONV), axis=0)
            a, sb = _glu_rows(z_ref, r0, CONV_ROWS)
            dz_ref[pl.ds(r0, CONV_ROWS), :] = jnp.concatenate([dh * sb, dh * a * sb * (1.0 - sb)], axis=1).astype(BF16)
            return carry

        lax.fori_loop(0, s // CONV_ROWS, pass2, 0)
        dw_ref[...] = jnp.sum(dwacc[...].reshape(32, 8, D_CONV), axis=1)

    vs = jax.ShapeDtypeStruct((1, D_CONV), F32)
    return pl.pallas_call(
        body,
        out_shape=[jax.ShapeDtypeStruct((s, 2 * D_CONV), BF16), jax.ShapeDtypeStruct((32, D_CONV), F32), vs, vs, vs],
        scratch_shapes=[pltpu.VMEM((s + CONV_HALO, D_CONV), F32), pltpu.VMEM((s + CONV_HALO, D_CONV), F32),
                        pltpu.VMEM((256, D_CONV), F32), pltpu.VMEM((7, CONV_WIN, D_CONV), F32),
                        pltpu.VMEM((7, CONV_WIN, D_CONV), F32)],
        name=name, compiler_params=_cparams(),
    )(dfeat, cv, zc, conv_w, ln_g, ln_b)


def _branch_out(feats, wts, name):
    s = feats[0].shape[0]
    tm = min(1024, s)

    def body(*refs):
        for f_ref, w_ref, o_ref in zip(refs[:3], refs[3:6], refs[6:]):
            o_ref[...] = lax.dot_general(f_ref[...], w_ref[...], _DIMS["nt"], preferred_element_type=F32).astype(BF16)

    row = pl.BlockSpec((tm, D_MODEL), lambda i: (i, 0))
    sh = jax.ShapeDtypeStruct((s, D_MODEL), BF16)
    return pl.pallas_call(
        body, grid=(s // tm,),
        in_specs=[pl.BlockSpec((tm, f.shape[1]), lambda i: (i, 0)) for f in feats] + [_full(w.shape) for w in wts],
        out_specs=[row] * 3, out_shape=[sh] * 3, name=name, compiler_params=_cparams(),
    )(*feats, *wts)


def _branch_in_bwd(dys, wts, out_dtypes, name):
    s = dys[0].shape[0]
    tm = min(1024, s)

    def body(*refs):
        for d_ref, w_ref, o_ref in zip(refs[:3], refs[3:6], refs[6:]):
            o_ref[...] = jnp.dot(d_ref[...], w_ref[...], preferred_element_type=F32).astype(o_ref.dtype)

    row = pl.BlockSpec((tm, D_MODEL), lambda i: (i, 0))
    return pl.pallas_call(
        body, grid=(s // tm,), in_specs=[row] * 3 + [_full(w.shape) for w in wts],
        out_specs=[pl.BlockSpec((tm, w.shape[1]), lambda i: (i, 0)) for w in wts],
        out_shape=[jax.ShapeDtypeStruct((s, w.shape[1]), dt) for w, dt in zip(wts, out_dtypes)],
        name=name, compiler_params=_cparams(),
    )(*dys, *wts)


def _branch_dw(dys, feats, name):
    s = dys[0].shape[0]
    tm = 512

    def body(*refs):
        for d_ref, f_ref, o_ref in zip(refs[:3], refs[3:6], refs[6:]):
            acc = lax.dot_general(d_ref[...], f_ref[...], _DIMS["tn"], preferred_element_type=F32)
            half = acc.shape[1] // 2
            o_ref[0] = acc[:, :half].astype(BF16)
            o_ref[1] = acc[:, half:].astype(BF16)

    return pl.pallas_call(
        body, grid=(D_MODEL // tm,),
        in_specs=[pl.BlockSpec((s, tm), lambda i: (0, i))] * 3 + [_full(f.shape) for f in feats],
        out_specs=[pl.BlockSpec((2, tm, f.shape[1] // 2), lambda i: (0, i, 0)) for f in feats],
        out_shape=[jax.ShapeDtypeStruct((2, D_MODEL, f.shape[1] // 2), BF16) for f in feats],
        name=name, compiler_params=_cparams(),
    )(*dys, *feats)


def _merge(zg, b_gate, ys, name):
    s = zg.shape[0]
    tm = _row_tile(s)

    def body(zg_ref, bg_ref, y0_ref, y1_ref, y2_ref, o_ref):
        acc = None
        for j, y_ref in enumerate((y0_ref, y1_ref, y2_ref)):
            cs = slice(D_MODEL * j, D_MODEL * (j + 1))
            t = _sigmoid(zg_ref[:, cs] + bg_ref[:, cs]) * y_ref[...]
            acc = t if acc is None else acc + t
        o_ref[...] = acc.astype(BF16)

    row = pl.BlockSpec((tm, D_MODEL), lambda i: (i, 0))
    return pl.pallas_call(
        body, grid=(s // tm,),
        in_specs=[pl.BlockSpec((tm, 3 * D_MODEL), lambda i: (i, 0)), _full((1, 3 * D_MODEL)), row, row, row],
        out_specs=row, out_shape=jax.ShapeDtypeStruct((s, D_MODEL), BF16), name=name, compiler_params=_cparams(),
    )(zg, b_gate, *ys)


def _merge_bwd(dmix, w_o, zg, b_gate, ys, name):
    s = zg.shape[0]
    tm = min(256, s)

    def body(dmix_ref, wo_ref, zg_ref, bg_ref, y0_ref, y1_ref, y2_ref, d0_ref, d1_ref, d2_ref, dzg_ref, dbg_ref):
        first = pl.program_id(0) == 0

        @pl.when(first)
        def _():
            dbg_ref[...] = jnp.zeros_like(dbg_ref)

        dmv = lax.dot_general(dmix_ref[...], wo_ref[...], _DIMS["nt"], preferred_element_type=F32)
        for j, (y_ref, d_ref) in enumerate(((y0_ref, d0_ref), (y1_ref, d1_ref), (y2_ref, d2_ref))):
            cs = slice(D_MODEL * j, D_MODEL * (j + 1))
            g = _sigmoid(zg_ref[:, cs] + bg_ref[:, cs])
            d_ref[...] = (dmv * g).astype(BF16)
            dzg = dmv * y_ref[...] * g * (1.0 - g)
            dzg_ref[:, cs] = dzg.astype(BF16)
            dbg_ref[:, cs] += jnp.sum(dzg, axis=0, keepdims=True)

    row = pl.BlockSpec((tm, D_MODEL), lambda i: (i, 0))
    wide = pl.BlockSpec((tm, 3 * D_MODEL), lambda i: (i, 0))
    yb = jax.ShapeDtypeStruct((s, D_MODEL), BF16)
    return pl.pallas_call(
        body, grid=(s // tm,),
        in_specs=[row, _full(w_o.shape), wide, _full((1, 3 * D_MODEL)), row, row, row],
        out_specs=[row, row, row, wide, _full((1, 3 * D_MODEL))],
        out_shape=[yb, yb, yb, jax.ShapeDtypeStruct((s, 3 * D_MODEL), BF16), jax.ShapeDtypeStruct((1, 3 * D_MODEL), F32)],
        name=name, compiler_params=_cparams(),
    )(dmix, w_o, zg, b_gate, *ys)


def _ff_hidden(u2, w_ff1t, b_ff1, name, rider=None):
    s = u2.shape[0]
    tm, tn = min(2048, s), 1024

    def body(a_ref, b_ref, bias_ref, pre_ref, h_ref):
        acc = lax.dot_general(a_ref[...], b_ref[...], _DIMS["nt"], preferred_element_type=F32) + bias_ref[...]
        pre_ref[...] = acc.astype(BF16)
        h_ref[...] = _relu2(acc).astype(BF16)

    blk = pl.BlockSpec((tm, tn), lambda i, j: (i, j))
    sh = jax.ShapeDtypeStruct((s, D_FF), BF16)
    res = _call(body, name=name, grid=(s // tm, D_FF // tn),
                in_specs=[pl.BlockSpec((tm, D_MODEL), lambda i, j: (i, 0)), pl.BlockSpec((tn, D_MODEL), lambda i, j: (j, 0)),
                          pl.BlockSpec((1, tn), lambda i, j: (0, j))],
                out_specs=[blk, blk], out_shape=[sh, sh], scratch_shapes=[], args=(u2, w_ff1t, b_ff1), rider=rider)
    return tuple(res) if rider is None else (tuple(res[0]), res[1])


def _ff_hidden_bwd(dff, w_ff2, hpre, name, rider=None):
    s = dff.shape[0]
    tm, tn = min(1024, s), 1024

    def body(a_ref, b_ref, h_ref, o_ref, sum_ref):
        dh = lax.dot_general(a_ref[...], b_ref[...], _DIMS["nt"], preferred_element_type=F32)
        dpre = dh * (2.0 * jnp.maximum(h_ref[...].astype(F32), 0.0))
        o_ref[...] = dpre.astype(BF16)
        _acc_rows(sum_ref, dpre, pl.program_id(1) == 0)

    res = _call(
        body, name=name, grid=(D_FF // tn, s // tm),
        in_specs=[pl.BlockSpec((tm, D_MODEL), lambda j, i: (i, 0)), pl.BlockSpec((tn, D_MODEL), lambda j, i: (j, 0)),
                  pl.BlockSpec((tm, tn), lambda j, i: (i, j))],
        out_specs=[pl.BlockSpec((tm, tn), lambda j, i: (i, j)), pl.BlockSpec((1, tn), lambda j, i: (0, j))],
        out_shape=[jax.ShapeDtypeStruct((s, D_FF), BF16), jax.ShapeDtypeStruct((1, D_FF), F32)],
        scratch_shapes=[], args=(dff, w_ff2, hpre), rider=rider)
    return tuple(res) if rider is None else (tuple(res[0]), res[1])


def _silu(t):
    return t * _sigmoid(t)


def _mod_fwd(c_all, w_ada_sh, b_ada_sh, name):
    cols = w_ada_sh.shape[2]

    def body(c_ref, w_ref, b_ref, o_ref):
        ca = _silu(c_ref[...]).astype(BF16)
        o_ref[0] = jnp.dot(ca, w_ref[0].astype(BF16), preferred_element_type=F32) + b_ref[0]

    return pl.pallas_call(
        body, grid=(DEPTH,),
        in_specs=[_full((N_DEV, D_MODEL)), pl.BlockSpec((1, D_MODEL, cols), lambda l: (l, 0, 0)),
                  pl.BlockSpec((1, 1, cols), lambda l: (l, 0, 0))],
        out_specs=pl.BlockSpec((1, N_DEV, cols), lambda l: (l, 0, 0)),
        out_shape=jax.ShapeDtypeStruct((DEPTH, N_DEV, cols), F32), name=name, compiler_params=_cparams(),
    )(c_all, w_ada_sh, b_ada_sh)


def _mod_bwd(c_all, dmod_sh, name):
    cols = dmod_sh.shape[2]

    def body(c_ref, d_ref, o_ref):
        ca = _silu(c_ref[...])
        o_ref[0] = lax.dot_general(ca, d_ref[0], _DIMS["tn"], precision=lax.Precision.HIGHEST,
                                   preferred_element_type=F32)

    return pl.pallas_call(
        body, grid=(DEPTH,),
        in_specs=[_full((N_DEV, D_MODEL)), pl.BlockSpec((1, N_DEV, cols), lambda l: (l, 0, 0))],
        out_specs=pl.BlockSpec((1, D_MODEL, cols), lambda l: (l, 0, 0)),
        out_shape=jax.ShapeDtypeStruct((DEPTH, D_MODEL, cols), F32), name=name, compiler_params=_cparams(),
    )(c_all, dmod_sh)


def _flat_tiles(rows, cols, itemsize_total):
    budget = 12 * 1024 * 1024
    tr = rows
    while tr % 32 == 0 and tr * cols * itemsize_total > budget:
        tr //= 2
    return tr


def _sum_cores(dws, recvs, place, name):
    k = len(dws)

    def body(place_ref, *refs):
        for a_ref, b_ref, o_ref in zip(refs[:k], refs[k:2 * k], refs[2 * k:]):
            o_ref[...] = (a_ref[...].astype(F32) + b_ref[...].astype(F32)).astype(BF16)

    whole = [pl.BlockSpec(a.shape[1:], lambda i, pr: (0, 0)) for a in dws]
    mine = [pl.BlockSpec((None,) + a.shape[1:], lambda i, pr: (pr[0], 0, 0)) for a in dws]
    grid_spec = pltpu.PrefetchScalarGridSpec(num_scalar_prefetch=1, grid=(1,), in_specs=mine + whole, out_specs=whole)
    return pl.pallas_call(body, grid_spec=grid_spec, out_shape=[jax.ShapeDtypeStruct(a.shape[1:], BF16) for a in dws],
                          name=name, compiler_params=_cparams())(place, *dws, *recvs)


def _sum_chips(hs, rs, place, name):
    k = len(hs)

    def body(place_ref, *refs):
        for h_ref, r_ref, o_ref in zip(refs[:k], refs[k:2 * k], refs[2 * k:]):
            o_ref[...] = ((h_ref[...].astype(F32) + r_ref[0].astype(F32)) + r_ref[1].astype(F32)) + r_ref[2].astype(F32)

    own = [pl.BlockSpec((None,) + h.shape[1:], lambda i, pr: (pr[1], 0, 0)) for h in hs]
    got = [pl.BlockSpec(r.shape, lambda i, pr: (0, 0, 0)) for r in rs]
    out = [pl.BlockSpec(h.shape[1:], lambda i, pr: (0, 0)) for h in hs]
    grid_spec = pltpu.PrefetchScalarGridSpec(num_scalar_prefetch=1, grid=(1,), in_specs=own + got, out_specs=out)
    return pl.pallas_call(body, grid_spec=grid_spec, out_shape=[jax.ShapeDtypeStruct(h.shape[1:], F32) for h in hs],
                          name=name, compiler_params=_cparams())(place, *hs, *rs)


def _adam_math(w, g, m, v):
    m2 = ADAM_B1 * m + (1.0 - ADAM_B1) * g
    v2 = ADAM_B2 * v + (1.0 - ADAM_B2) * (g * g)
    m_hat = m2 / (1.0 - ADAM_B1 ** ADAM_STEP)
    v_hat = v2 / (1.0 - ADAM_B2 ** ADAM_STEP)
    delta = -ADAM_LR * (m_hat / (jnp.sqrt(v_hat) + ADAM_EPS) + ADAM_WD * w)
    return delta, m2, v2


def _adamw(w, m, v, grads, name):
    r, c = w.shape
    tr = _flat_tiles(r, c, 4 * (7 + len(grads)))

    def body(*refs):
        w_ref, m_ref, v_ref = refs[:3]
        g_refs = refs[3:3 + len(grads)]
        g_ref, d_ref, m2_ref, v2_ref = refs[3 + len(grads):]
        g = g_refs[0][...]
        for gr in g_refs[1:]:
            g = g + gr[...]
        delta, m2, v2 = _adam_math(w_ref[...], g, m_ref[...], v_ref[...])
        g_ref[...] = g
        d_ref[...] = delta
        m2_ref[...] = m2
        v2_ref[...] = v2

    blk = pl.BlockSpec((tr, c), lambda i: (i, 0))
    sh = jax.ShapeDtypeStruct((r, c), F32)
    return pl.pallas_call(body, grid=(r // tr,), in_specs=[blk] * (3 + len(grads)), out_specs=[blk] * 4,
                          out_shape=[sh] * 4, name=name, compiler_params=_cparams())(w, m, v, *grads)


def _adamw_halves(w, m, v, own, other, place, split, name):
    nl, r, c = w.shape
    hr, hc = own[0].shape
    tr = _flat_tiles(hr, hc, 4 * (7 + 2 * nl))
    nt = hr // tr
    if split == "rows":
        w_spec = pl.BlockSpec((None, tr, c), lambda l, h, t, pr: (l, h * nt + t, 0))
    else:
        w_spec = pl.BlockSpec((None, tr, hc), lambda l, h, t, pr: (l, t, h))

    def g_spec(layer, mine):
        return pl.BlockSpec((tr, hc), lambda l, h, t, pr: (jnp.where((l == layer) & ((h == pr[0]) == mine), t, nt - 1), 0))

    def body(place_ref, w_ref, m_ref, v_ref, *refs):
        own_refs, other_refs = refs[:nl], refs[nl:2 * nl]
        g_ref, d_ref, m2_ref, v2_ref = refs[2 * nl:]
        layer = pl.program_id(0)
        mine = pl.program_id(1) == place_ref[0]
        g = None
        for li in range(nl):
            cand = jnp.where(mine, own_refs[li][...], other_refs[li][...])
            g = cand if g is None else jnp.where(layer == li, cand, g)
        delta, m2, v2 = _adam_math(w_ref[...], g, m_ref[...], v_ref[...])
        g_ref[...] = g
        d_ref[...] = delta
        m2_ref[...] = m2
        v2_ref[...] = v2

    sh = jax.ShapeDtypeStruct((nl, r, c), F32)
    g_specs = [g_spec(li, True) for li in range(nl)] + [g_spec(li, False) for li in range(nl)]
    return _call(body, name=name, grid=(nl, 2, nt), in_specs=[w_spec] * 3 + g_specs, out_specs=[w_spec] * 4,
                 out_shape=[sh] * 4, scratch_shapes=[], args=(w, m, v, *own, *other), prefetch=(place,))


def _adamw_small(w, m, v, g_all, name):
    r, c = w.shape

    def body(w_ref, m_ref, v_ref, g_ref, go_ref, d_ref, m2_ref, v2_ref):
        g = g_ref[0]
        for b in range(1, N_DEV):
            g = g + g_ref[b]
        delta, m2, v2 = _adam_math(w_ref[...], g, m_ref[...], v_ref[...])
        go_ref[...] = g
        d_ref[...] = delta
        m2_ref[...] = m2
        v2_ref[...] = v2

    sh = jax.ShapeDtypeStruct((r, c), F32)
    return pl.pallas_call(body, out_shape=[sh] * 4, name=name, compiler_params=_cparams())(w, m, v, g_all)


def _me():
    return lax.axis_index("x"), lax.axis_index("y"), lax.axis_index("c")


def _flip(v, bit):
    return 1 - v if bit else v


def _allgather_small(blk, name):
    r, c = blk.shape

    def body(x_ref, o_ref, send_sems, recv_sems):
        x, y, cc = _me()
        me = 4 * x + 2 * y + cc
        copies = []
        for k in range(1, N_DEV):
            peer = (_flip(x, k & 4), _flip(y, k & 2), _flip(cc, k & 1))
            cp = pltpu.make_async_remote_copy(src_ref=x_ref, dst_ref=o_ref.at[me], send_sem=send_sems.at[k - 1],
                                              recv_sem=recv_sems.at[k - 1], device_id=peer, device_id_type=MESH)
            cp.start()
            copies.append(cp)
        o_ref[me] = x_ref[...]
        for cp in copies:
            cp.wait()

    return pl.pallas_call(
        body, out_shape=jax.ShapeDtypeStruct((N_DEV, r, c), F32),
        in_specs=[pl.BlockSpec(memory_space=pltpu.VMEM)], out_specs=pl.BlockSpec(memory_space=pltpu.VMEM),
        scratch_shapes=[pltpu.SemaphoreType.DMA((N_DEV - 1,)), pltpu.SemaphoreType.DMA((N_DEV - 1,))],
        name=name, compiler_params=_cparams(),
    )(blk)


class _Rider:
    def __init__(self, arrays, out_shapes, scratch_shapes, start, finish):
        self.arrays, self.out_shapes, self.scratch_shapes = list(arrays), list(out_shapes), list(scratch_shapes)
        self.start, self.finish = start, finish


def _call(body, *, name, grid, in_specs, out_specs, out_shape, scratch_shapes, args, rider=None, prefetch=()):
    npf = len(prefetch)

    def launch(fn, in_specs, out_specs, out_shape, scratch_shapes, args):
        grid_spec = pltpu.PrefetchScalarGridSpec(num_scalar_prefetch=npf, grid=grid, in_specs=in_specs,
                                                 out_specs=out_specs, scratch_shapes=scratch_shapes)
        return pl.pallas_call(fn, grid_spec=grid_spec, out_shape=out_shape, name=name,
                              compiler_params=_cparams())(*prefetch, *args)

    if rider is None:
        return launch(body, list(in_specs), list(out_specs), list(out_shape), list(scratch_shapes), args)
    ni, no, ns = len(in_specs), len(out_specs), len(scratch_shapes)
    ri, ro = len(rider.arrays), len(rider.out_shapes)
    steps = int(np.prod(grid))

    def wrapped(*refs):
        pf, refs = refs[:npf], refs[npf:]
        h_in, r_in = refs[:ni], refs[ni:ni + ri]
        h_out, r_out = refs[ni + ri:ni + ri + no], refs[ni + ri + no:ni + ri + no + ro]
        h_scr, r_scr = refs[ni + ri + no + ro:ni + ri + no + ro + ns], refs[ni + ri + no + ro + ns:]
        step = pl.program_id(0)
        for d in range(1, len(grid)):
            step = step * grid[d] + pl.program_id(d)

        @pl.when(step == 0)
        def _():
            rider.start(r_in, r_out, r_scr)

        body(*pf, *h_in, *h_out, *h_scr)

        @pl.when(step == steps - 1)
        def _():
            rider.finish(r_in, r_out, r_scr)

    anyspec = pl.BlockSpec(memory_space=pl.ANY)
    res = launch(wrapped, list(in_specs) + [anyspec] * ri, list(out_specs) + [anyspec] * ro,
                 list(out_shape) + rider.out_shapes, list(scratch_shapes) + rider.scratch_shapes,
                 list(args) + rider.arrays)
    return res[:no], res[no:]


def _run_rider(rider, name):
    ri = len(rider.arrays)

    def body(*refs):
        r_in, r_out, r_scr = refs[:ri], refs[ri:ri + len(rider.out_shapes)], refs[ri + len(rider.out_shapes):]
        rider.start(r_in, r_out, r_scr)
        rider.finish(r_in, r_out, r_scr)

    anyspec = pl.BlockSpec(memory_space=pl.ANY)
    return pl.pallas_call(body, in_specs=[anyspec] * ri, out_specs=[anyspec] * len(rider.out_shapes),
                          out_shape=rider.out_shapes, scratch_shapes=rider.scratch_shapes, name=name,
                          compiler_params=_cparams())(*rider.arrays)


def _allgather_rider(blk):
    def copies(ins, outs, scr):
        send_sems, recv_sems, loc_sems, stage = scr
        x, y, cc = _me()
        me = 4 * x + 2 * y + cc
        remote = [pltpu.make_async_remote_copy(
            src_ref=ins[0], dst_ref=outs[0].at[me], send_sem=send_sems.at[k - 1], recv_sem=recv_sems.at[k - 1],
            device_id=(_flip(x, k & 4), _flip(y, k & 2), _flip(cc, k & 1)), device_id_type=MESH) for k in range(1, N_DEV)]
        return remote, pltpu.make_async_copy(ins[0], stage, loc_sems.at[0]), (outs[0].at[me], stage, loc_sems.at[1])

    def start(ins, outs, scr):
        remote, lin, _ = copies(ins, outs, scr)
        lin.start()
        for cp in remote:
            cp.start()

    def finish(ins, outs, scr):
        remote, lin, (dst, stage, sem) = copies(ins, outs, scr)
        lin.wait()
        lout = pltpu.make_async_copy(stage, dst, sem)
        lout.start()
        for cp in remote:
            cp.wait()
        lout.wait()

    return _Rider([blk], [jax.ShapeDtypeStruct((N_DEV,) + blk.shape, blk.dtype)],
                  [pltpu.SemaphoreType.DMA((N_DEV - 1,)), pltpu.SemaphoreType.DMA((N_DEV - 1,)),
                   pltpu.SemaphoreType.DMA((2,)), pltpu.VMEM(blk.shape, blk.dtype)], start, finish)


def _gather_rider(shards):
    n = len(shards)

    def copies(ins, outs, scr, relay=True):
        ici_send, ici_recv, d2d_send, d2d_recv, loc_sems = scr[:5]
        stage = scr[5:]
        x, y, cc = _me()
        chip = 2 * x + y
        sibling = (x, y, 1 - cc)
        local, sends, relays = [], [], []
        for j in range(n):
            def rows(ch, h, j=j):
                return outs[j].at[ch, h]

            lc = pltpu.make_async_copy(ins[j], stage[j], loc_sems.at[j])
            local.append((lc, pltpu.make_async_copy(stage[j], outs[j].at[chip], loc_sems.at[n + j]) if relay else None))
            for k in range(1, N_CHIP):
                px, py = _flip(x, k & 2), _flip(y, k & 1)
                pchip = 2 * px + py
                q = 3 * j + k - 1
                out_cp = pltpu.make_async_remote_copy(src_ref=ins[j].at[cc], dst_ref=rows(chip, cc),
                                                      send_sem=ici_send.at[q], recv_sem=ici_recv.at[q],
                                                      device_id=(px, py, cc), device_id_type=MESH)
                sends.append(out_cp)
                if not relay:
                    continue
                arrival = pltpu.make_async_remote_copy(src_ref=rows(pchip, cc), dst_ref=rows(pchip, cc),
                                                       send_sem=ici_send.at[q], recv_sem=ici_recv.at[q],
                                                       device_id=(px, py, cc), device_id_type=MESH)
                forward = pltpu.make_async_remote_copy(src_ref=rows(pchip, cc), dst_ref=rows(pchip, cc),
                                                       send_sem=d2d_send.at[q], recv_sem=d2d_recv.at[q],
                                                       device_id=sibling, device_id_type=MESH)
                from_sibling = pltpu.make_async_remote_copy(src_ref=rows(pchip, 1 - cc), dst_ref=rows(pchip, 1 - cc),
                                                            send_sem=d2d_send.at[q], recv_sem=d2d_recv.at[q],
                                                            device_id=sibling, device_id_type=MESH)
                relays.append((arrival, forward, from_sibling))
        return local, sends, relays

    def start(ins, outs, scr):
        local, sends, _ = copies(ins, outs, scr, relay=False)
        for lin, _ in local:
            lin.start()
        for cp in sends:
            cp.start()

    def finish(ins, outs, scr):
        local, sends, relays = copies(ins, outs, scr)
        for lin, lout in local:
            lin.wait()
            lout.start()
        for arrival, forward, _ in relays:
            arrival.wait_recv()
            forward.start()
        for cp in sends:
            cp.wait_send()
        for _, forward, from_sibling in relays:
            forward.wait_send()
            from_sibling.wait_recv()
        for _, lout in local:
            lout.wait()

    scratch = [pltpu.SemaphoreType.DMA((3 * n,)), pltpu.SemaphoreType.DMA((3 * n,)), pltpu.SemaphoreType.DMA((3 * n,)),
               pltpu.SemaphoreType.DMA((3 * n,)), pltpu.SemaphoreType.DMA((2 * n,))]
    scratch += [pltpu.VMEM(a.shape, a.dtype) for a in shards]
    return _Rider(shards, [jax.ShapeDtypeStruct((N_CHIP,) + a.shape, a.dtype) for a in shards], scratch, start, finish)


def _sibling_rider(arrs, other_half=False):
    n = len(arrs)

    def copies(ins, outs, scr):
        send_sems, recv_sems = scr
        x, y, cc = _me()
        return [pltpu.make_async_remote_copy(
            src_ref=ins[j].at[1 - cc] if other_half else ins[j], dst_ref=outs[j], send_sem=send_sems.at[j],
            recv_sem=recv_sems.at[j], device_id=(x, y, 1 - cc), device_id_type=MESH) for j in range(n)]

    def start(ins, outs, scr):
        for cp in copies(ins, outs, scr):
            cp.start()

    def finish(ins, outs, scr):
        for cp in copies(ins, outs, scr):
            cp.wait()

    return _Rider(arrs, [jax.ShapeDtypeStruct(a.shape[1:] if other_half else a.shape, a.dtype) for a in arrs],
                  [pltpu.SemaphoreType.DMA((n,)), pltpu.SemaphoreType.DMA((n,))], start, finish)


def _sibling_send(arrs, name, other_half=False):
    return _run_rider(_sibling_rider(arrs, other_half), name)


def _join_riders(first, second):
    ni, no, ns = len(first.arrays), len(first.out_shapes), len(first.scratch_shapes)

    def split(ins, outs, scr):
        return (ins[:ni], outs[:no], scr[:ns]), (ins[ni:], outs[no:], scr[ns:])

    def start(ins, outs, scr):
        a, b = split(ins, outs, scr)
        first.start(*a)
        second.start(*b)

    def finish(ins, outs, scr):
        a, b = split(ins, outs, scr)
        first.finish(*a)
        second.finish(*b)

    return _Rider(first.arrays + second.arrays, first.out_shapes + second.out_shapes,
                  first.scratch_shapes + second.scratch_shapes, start, finish)


def _scatter_rider(arrs):
    n = len(arrs)

    def copies(ins, outs, scr):
        send_sems, recv_sems = scr
        x, y, cc = _me()
        cps = []
        for j in range(n):
            for k in range(1, N_CHIP):
                px, py = _flip(x, k & 2), _flip(y, k & 1)
                cps.append(pltpu.make_async_remote_copy(
                    src_ref=ins[j].at[2 * px + py], dst_ref=outs[j].at[k - 1], send_sem=send_sems.at[3 * j + k - 1],
                    recv_sem=recv_sems.at[3 * j + k - 1], device_id=(px, py, cc), device_id_type=MESH))
        return cps

    def start(ins, outs, scr):
        for cp in copies(ins, outs, scr):
            cp.start()

    def finish(ins, outs, scr):
        for cp in copies(ins, outs, scr):
            cp.wait()

    return _Rider(arrs, [jax.ShapeDtypeStruct((N_CHIP - 1,) + a.shape[1:], a.dtype) for a in arrs],
                  [pltpu.SemaphoreType.DMA((3 * n,)), pltpu.SemaphoreType.DMA((3 * n,))], start, finish)


COL_SHARDED = ("w_in", "w_br_pool", "w_br_attn", "w_br_conv", "w_ff1")
ROW_SHARDED = ("w_o", "w_ff2")
BIG = COL_SHARDED + ROW_SHARDED
SMALL = ("b_ada", "b_gate", "w_pool", "pool_scale", "rel_bias", "conv_w", "conv_b", "conv_ln_g", "conv_ln_b",
         "ln_mix_g", "ln_mix_b", "b_ff1", "b_ff2", "ln_ff_g", "ln_ff_b")
PACK_W = 1024


def _pack(parts):
    rows = []
    for a in parts:
        flat = a.reshape(-1)
        n = -(-flat.shape[0] // PACK_W) * PACK_W
        rows.append(jnp.pad(flat, (0, n - flat.shape[0])).reshape(-1, PACK_W))
    out = jnp.concatenate(rows, axis=0)
    r = -(-out.shape[0] // 8) * 8
    return jnp.pad(out, ((0, r - out.shape[0]), (0, 0)))


def _unpack(packed, shapes):
    out, r0 = [], 0
    for shp in shapes:
        size = int(np.prod(shp))
        nr = -(-size // PACK_W)
        out.append(packed[r0:r0 + nr].reshape(-1)[:size].reshape(shp))
        r0 += nr
    return out


def _hosted(fn, hook, *args, **kw):
    if hook is None:
        return fn(*args, **kw)
    res, rider_out = fn(*args, rider=hook[0], **kw)
    hook[1](rider_out)
    return res


def _layer_fwd(l, x, mod, W, P, hooks=None, u=None):
    hooks = hooks or {}
    s = x.shape[0]
    sh_m, sc_m, g_m, sh_f, sc_f, g_f = [mod[l:l + 1, D_MODEL * j:D_MODEL * (j + 1)] for j in range(6)]
    n = lambda t: f"{t}{l}"
    w_in = W["w_in"][l]
    if u is None:
        u = _ln_mod(x, sc_m, sh_m, n("ln_mod_mix"))
    zp = _mm(u, w_in, "nt", tm=s, tn=256, out_dtype=F32, name=n("z_pool"), b_col0=0, n_out=D_POOL)
    qkv = _mm(u, w_in, "nt", tm=s, tn=256, out_dtype=BF16, name=n("z_qkv"), b_col0=OFF_QKV // 256, n_out=3 * D_ATTN)
    zc = _mm(u, w_in, "nt", tm=s, tn=256, out_dtype=F32, name=n("z_conv"), b_col0=OFF_CONV // 256, n_out=2 * D_CONV)
    zg = _hosted(_mm, hooks.get("z_gate"), u, w_in, "nt", tm=min(2048, s), tn=768, out_dtype=BF16, name=n("z_gate"),
                 b_col0=OFF_GATE // 768, n_out=3 * D_MODEL)

    p, feat_pool = _pool_fwd(zp, P["wp_bd"][l], P["pool_scale"][l], n("pool_fwd"))
    bias = _bias_block(P["rel_bias"][l], n("bias_block"))
    o, probs = _hosted(_attn_fwd, hooks.get("attn"), qkv, bias, n("attn_fwd"))
    cv, feat_conv = _conv_fwd(zc, P["conv_w"][l], P["conv_b"][l], P["conv_ln_g"][l], P["conv_ln_b"][l], n("conv_fwd"))

    branch_w = (W["w_br_pool"][l], W["w_br_attn"][l], W["w_br_conv"][l])
    ys = tuple(_branch_out((feat_pool, o, feat_conv), branch_w, n("branch_out")))
    merged = _merge(zg, P["b_gate"][l], ys, n("merge"))
    mix, x1, u2 = _mm_resid_ln(merged, W["w_o"][l], None, x, g_m, P["ln_mix_g"][l], P["ln_mix_b"][l], n("mix_out"),
                               mod_next=(sc_f, sh_f))

    hpre, hid = _hosted(_ff_hidden, hooks.get("ff1"), u2, W["w_ff1"][l], P["b_ff1"][l], n("ff1"))
    above = None if l + 1 == mod.shape[0] else (mod[l + 1:l + 2, D_MODEL:2 * D_MODEL], mod[l + 1:l + 2, 0:D_MODEL])
    ff, x2, *u_next = _hosted(_mm_resid_ln, hooks.get("ff2"), hid, W["w_ff2"][l], P["b_ff2"][l], x1, g_f,
                              P["ln_ff_g"][l], P["ln_ff_b"][l], n("ff2"), mod_next=above)
    saved = dict(x=x, u=u, zp=zp, qkv=qkv, zc=zc, zg=zg, p=p, feat_pool=feat_pool, probs=probs, o=o, cv=cv,
                 feat_conv=feat_conv, ys=ys, merged=merged, mix=mix, x1=x1, u2=u2, hpre=hpre, hid=hid, ff=ff,
                 u_next=u_next[0] if u_next else None)
    return x2, saved


def _layer_bwd(l, dx2, mod, W, P, A, hooks=None, tgt=None, nxt=None):
    hooks = hooks or {}
    sh_m, sc_m, g_m, sh_f, sc_f, g_f = [mod[l:l + 1, D_MODEL * j:D_MODEL * (j + 1)] for j in range(6)]
    n = lambda t: f"{t}{l}"
    gw, gs = {}, {}

    if isinstance(dx2, tuple):
        dres, dff, gs["ln_ff_g"], gs["ln_ff_b"], dg_f, gs["b_ff2"] = dx2
    else:
        dres, dff, gs["ln_ff_g"], gs["ln_ff_b"], dg_f, gs["b_ff2"], *loss_part = _resid_ln_bwd(
            dx2, A["x1"], A["ff"], g_f, P["ln_ff_g"][l], n("resid_ln_ff_bwd"), tgt=tgt)
    gw["w_ff2"] = _mm(A["hid"], dff, "tn", tm=512, tn=1024, out_dtype=BF16, name=n("dw_ff2"), split_n=512)
    hook = hooks["ff_hidden_bwd"](gw) if "ff_hidden_bwd" in hooks else None
    dhpre, gs["b_ff1"] = _hosted(_ff_hidden_bwd, hook, dff, W["w_ff2"][l], A["hpre"], n("ff_hidden_bwd"))
    gw["w_ff1"] = _mm(dhpre, A["u2"], "tn", tm=512, tn=1024, out_dtype=BF16, name=n("dw_ff1"), split_n=512)

    hook = hooks["du_ff"](gw) if "du_ff" in hooks else None
    dres, dmix, dsc_f, dsh_f, gs["ln_mix_g"], gs["ln_mix_b"], dg_m, _ = _hosted(
        _mm_ln_mod_bwd, hook, dhpre, W["w_ff1"][l], A["x1"], sc_f, dres, n("du_ff"),
        nxt=(A["x"], A["mix"], g_m, P["ln_mix_g"][l]))
    gw["w_o"] = _mm(A["merged"], dmix, "tn", tm=512, tn=1024, out_dtype=BF16, name=n("dw_o"), split_n=512)
    dy_pool, dy_attn, dy_conv, dzg, gs["b_gate"] = _merge_bwd(dmix, W["w_o"][l], A["zg"], P["b_gate"][l], A["ys"],
                                                              n("merge_bwd"))

    dys = (dy_pool, dy_attn, dy_conv)
    gw["w_br_pool"], gw["w_br_attn"], gw["w_br_conv"] = _branch_dw(
        dys, (A["feat_pool"], A["o"], A["feat_conv"]), n("dw_branch"))
    dfeat_pool, do, dfeat_conv = _branch_in_bwd(
        dys, (W["w_br_pool"][l], W["w_br_attn"][l], W["w_br_conv"][l]), (F32, BF16, F32), n("d_branch_in"))

    dzp, dwp_bd, gs["pool_scale"] = _pool_bwd(dfeat_pool, A["p"], P["wp_bd"][l], P["pool_scale"][l], n("pool_bwd"))
    gs["w_pool"] = jnp.stack([dwp_bd[POOL_GROUP * g:POOL_GROUP * (g + 1), POOL_GROUP * g:POOL_GROUP * (g + 1)]
                              for g in range(len(POOL_WINDOWS))])
    hook = hooks["attn"](gw) if "attn" in hooks else None
    dq, dk, dv, ds_acc = _hosted(_attn_bwd, hook, A["qkv"], do, A["probs"], n("attn_bwd"))
    gs["rel_bias"] = _bias_block_bwd(ds_acc, n("bias_block_bwd"))
    dzc, dcw, gs["conv_b"], gs["conv_ln_g"], gs["conv_ln_b"] = _conv_bwd(
        dfeat_conv, A["cv"], A["zc"], P["conv_w"][l], P["conv_ln_g"][l], P["conv_ln_b"][l], n("conv_bwd"))
    gs["conv_w"] = dcw[:CONV_WIDTH]

    dz = [dzp, dq, dk, dv, dzc, dzg]
    gw["w_in"] = _dw_segments(dz, A["u"], n("dw_in"))
    hook = hooks["du_mix"](gw) if "du_mix" in hooks else None
    res = _hosted(_mm_ln_mod_bwd, hook, dz, W["w_in"][l], A["x"], sc_m, dres, n("du_mix"), nxt=nxt)
    if nxt is None:
        dx, dsc_m, dsh_m = res
    else:
        dx, dsc_m, dsh_m = (res[0], res[1], *res[4:]), res[2], res[3]
    dmod = jnp.concatenate([dsh_m, dsc_m, dg_m, dsh_f, dsc_f, dg_f], axis=1)
    return (dx, gw, gs, dmod) if tgt is None else (dx, gw, gs, dmod, loss_part[0])


def _small_shapes():
    return {"b_ada": (6 * D_MODEL,), "b_gate": (3 * D_MODEL,), "w_pool": (4, POOL_GROUP, POOL_GROUP),
            "pool_scale": (D_POOL,), "rel_bias": (N_HEADS, N_REL), "conv_w": (CONV_WIDTH, D_CONV),
            "conv_b": (D_CONV,), "conv_ln_g": (D_CONV,), "conv_ln_b": (D_CONV,), "ln_mix_g": (D_MODEL,),
            "ln_mix_b": (D_MODEL,), "b_ff1": (D_FF,), "b_ff2": (D_MODEL,), "ln_ff_g": (D_MODEL,), "ln_ff_b": (D_MODEL,)}


def kernel(x, c, w_ada, b_ada, w_in, b_gate, w_pool, pool_scale, rel_bias, conv_w, conv_b, conv_ln_g, conv_ln_b, w_br_pool, w_br_attn, w_br_conv, w_o, ln_mix_g, ln_mix_b, w_ff1, b_ff1, w_ff2, b_ff2, ln_ff_g, ln_ff_b, loss_target, m_w_ada, m_b_ada, m_w_in, m_b_gate, m_w_pool, m_pool_scale, m_rel_bias, m_conv_w, m_conv_b, m_conv_ln_g, m_conv_ln_b, m_w_br_pool, m_w_br_attn, m_w_br_conv, m_w_o, m_ln_mix_g, m_ln_mix_b, m_w_ff1, m_b_ff1, m_w_ff2, m_b_ff2, m_ln_ff_g, m_ln_ff_b, v_w_ada, v_b_ada, v_w_in, v_b_gate, v_w_pool, v_pool_scale, v_rel_bias, v_conv_w, v_conv_b, v_conv_ln_g, v_conv_ln_b, v_w_br_pool, v_w_br_attn, v_w_br_conv, v_w_o, v_ln_mix_g, v_ln_mix_b, v_w_ff1, v_b_ff1, v_w_ff2, v_b_ff2, v_ln_ff_g, v_ln_ff_b):
    env = dict(locals())
    xi, yi, ci = _me()
    chip = 2 * xi + yi
    me = 4 * xi + 2 * yi + ci
    xs = x[0]
    tgt = loss_target[0]
    L = DEPTH

    first = _allgather_small(jnp.concatenate([c.reshape(8, 128), _pack([conv_w]).reshape(-1, 128)]), "gather_c_conv_w")
    c_all = first[:, :8].reshape(N_DEV, D_MODEL)
    ada_cols = w_ada.shape[2]
    b_ada_sh = lax.dynamic_slice_in_dim(b_ada, chip * ada_cols, ada_cols, axis=1).reshape(L, 1, ada_cols)
    mod_part = _mod_fwd(c_all, w_ada, b_ada_sh, "mod_fwd")

    W = {k: [None] * L for k in BIG}

    def weight_gather(*items):
        shards = [(jnp.swapaxes(env[k][l], 0, 1) if k in COL_SHARDED else env[k][l]).astype(BF16) for k, l in items]
        shards = [a.reshape(2, a.shape[0] // 2, a.shape[1]) for a in shards]

        def done(outs):
            for (k, l), g in zip(items, outs):
                W[k][l] = g.reshape(-1, g.shape[-1])

        return _gather_rider(shards), done

    branch = lambda l: [(k, l) for k in ("w_br_pool", "w_br_attn", "w_br_conv", "w_o")]
    rider, done = weight_gather(("w_in", 0))
    first_out = _run_rider(_join_riders(_allgather_rider(mod_part.reshape(-1, 128)), rider), "gather_mod_w_in0")
    done(first_out[1:])
    mod_g = first_out[0].reshape(N_CHIP, 2, L, N_DEV, ada_cols)[:, 0]
    mod_all = jnp.transpose(mod_g, (1, 2, 0, 3)).reshape(L, N_DEV, 6 * D_MODEL)
    mod = lax.dynamic_index_in_dim(mod_all, me, axis=1, keepdims=False)
    fwd_hooks = [{"z_gate": weight_gather(*branch(0)), "attn": weight_gather(("w_ff1", 0), ("w_ff2", 0)),
                  "ff1": weight_gather(*branch(1)), "ff2": weight_gather(("w_in", 1))},
                 {"attn": weight_gather(("w_ff1", 1), ("w_ff2", 1))}]

    P = {k: env[k] for k in ("rel_bias", "conv_w")}
    for k in ("b_gate", "pool_scale", "conv_b", "conv_ln_g", "conv_ln_b", "ln_mix_g", "ln_mix_b", "b_ff1", "b_ff2",
              "ln_ff_g", "ln_ff_b"):
        P[k] = env[k].reshape(L, 1, -1)
    n_cw = conv_w.size
    cw = first[:, 8:].reshape(N_CHIP, 2, -1)[:, 0, :n_cw].reshape(N_CHIP, L, CONV_WIDTH, D_CONV // N_CHIP)
    P["conv_w"] = jnp.transpose(cw, (1, 2, 0, 3)).reshape(L, CONV_WIDTH, D_CONV)
    wp_bd = jnp.zeros((L, D_POOL, D_POOL), F32)
    for g in range(len(POOL_WINDOWS)):
        sl = slice(POOL_GROUP * g, POOL_GROUP * (g + 1))
        wp_bd = wp_bd.at[:, sl, sl].set(w_pool[:, g])
    P["wp_bd"] = wp_bd.astype(BF16)

    acts = []
    h = xs
    for l in range(L):
        h, saved = _layer_fwd(l, h, mod, W, P, fwd_hooks[l], u=acts[-1]["u_next"] if acts else None)
        acts.append(saved)

    place = jnp.stack([ci, chip, chip ^ 1, chip ^ 2, chip ^ 3]).astype(jnp.int32)
    scattered = {}

    swapped = {}

    def swap_hook(names, l):
        def hook(gw):
            def done(outs):
                swapped.update({(k, l): o for k, o in zip(names, outs)})
            return _sibling_rider([gw[k] for k in names], other_half=True), done
        return hook

    def scatter_hook(names, l, host, then=None):
        def hook(gw):
            todo = [k for k in names if (k, l) not in swapped]
            if todo:
                got = _sibling_send([gw[k] for k in todo], f"swap_blocks_{host}{l}", other_half=True)
                swapped.update({(k, l): o for k, o in zip(todo, got)})
            sums = _sum_cores([gw[k] for k in names], [swapped[(k, l)] for k in names], place, f"sum_cores_{host}{l}")
            both = [hh.reshape(N_CHIP, -1, hh.shape[-1]) for hh in sums]
            rider = _scatter_rider(both)
            more = then(gw) if then is not None else None

            def done(outs):
                for k, hh, r in zip(names, both, outs):
                    scattered[(k, l)] = (hh, r)
                if more is not None:
                    more[1](outs[len(names):])

            return (rider if more is None else _join_riders(rider, more[0])), done
        return hook

    gws, gss, dmods = [None] * L, [None] * L, [None] * L
    dh = h
    for l in reversed(range(L)):
        hooks = {"ff_hidden_bwd": swap_hook(("w_ff2",), l),
                 "du_ff": scatter_hook(("w_ff2",), l, "du_ff", then=swap_hook(("w_ff1",), l)),
                 "attn": scatter_hook(("w_ff1", "w_o", "w_br_pool", "w_br_attn", "w_br_conv"), l, "attn_bwd"),
                 "du_mix": scatter_hook(("w_in",), l, "du_mix")}
        below = None
        if l > 0:
            below = (acts[l - 1]["x1"], acts[l - 1]["ff"], mod[l - 1:l, 5 * D_MODEL:], P["ln_ff_g"][l - 1])
        if l == L - 1:
            dh, gws[l], gss[l], dmods[l], loss_part = _layer_bwd(l, dh, mod, W, P, acts[l], hooks, tgt=tgt, nxt=below)
        else:
            dh, gws[l], gss[l], dmods[l] = _layer_bwd(l, dh, mod, W, P, acts[l], hooks, nxt=below)
    grad_x = dh[None]

    reduced = [[None] * L for _ in BIG]
    groups = (("w_in", "w_br_pool", "w_br_attn", "w_br_conv"), ("w_o", "w_ff1", "w_ff2"))
    for l in range(L):
        for gi, names in enumerate(groups):
            pairs = [scattered[(k, l)] for k in names]
            sums = _sum_chips([p[0] for p in pairs], [p[1] for p in pairs], place, f"sum_chips_{gi}_{l}")
            for k, t in zip(names, sums):
                reduced[BIG.index(k)][l] = t
    flat_reduced = [t for per_weight in reduced for t in per_weight]

    shapes = _small_shapes()
    small_names = [k for k in SMALL if k != "b_ada"]
    dmod_own = jnp.concatenate(dmods, axis=0)
    pack = _pack([dmod_own] + [jnp.stack([gss[l][k].reshape(shapes[k]) for l in range(L)]) for k in small_names]
                 + [loss_part])
    last = _run_rider(_join_riders(_sibling_rider(flat_reduced), _allgather_rider(pack.reshape(-1, 128))),
                      "swap_reduced_gather_small")
    flat_other, g_all = last[:-1], last[-1].reshape(N_DEV, -1, PACK_W)

    out = {}
    for j, k in enumerate(BIG):
        own, other = reduced[j], flat_other[L * j:L * (j + 1)]
        if k == "w_in":
            t = lambda a: jnp.swapaxes(a, 1, 2)
            res = _adamw_halves(t(env[k]), t(env["m_" + k]), t(env["v_" + k]), own, other, place, "cols", f"adamw_{k}")
            res = [t(a) for a in res]
        else:
            if k in COL_SHARDED:
                own, other = [a.T for a in own], [a.T for a in other]
            res = _adamw_halves(env[k], env["m_" + k], env["v_" + k], own, other, place,
                                "rows" if k in COL_SHARDED else "cols", f"adamw_{k}")
        out[k] = tuple(res)

    dmod_all = g_all[:, :L * 6].reshape(N_DEV, L, 6 * D_MODEL)
    dmod_sh = jnp.transpose(lax.dynamic_slice_in_dim(dmod_all, chip * ada_cols, ada_cols, axis=2), (1, 0, 2))
    g_ada = _mod_bwd(c_all, dmod_sh, "mod_bwd")
    g_, d_, m_, v_ = _adamw(w_ada.reshape(-1, ada_cols), m_w_ada.reshape(-1, ada_cols), v_w_ada.reshape(-1, ada_cols),
                            [g_ada.reshape(-1, ada_cols)], "adamw_w_ada")
    out["w_ada"] = tuple(a.reshape(w_ada.shape) for a in (g_, d_, m_, v_))

    def small_pack(prefix):
        parts = [env[prefix + "b_ada"]]
        for k in small_names:
            a = env[prefix + k]
            if k == "conv_w":
                a = jnp.zeros((L,) + shapes[k], F32)
            parts.append(a)
        return _pack(parts + [jnp.zeros_like(loss_part)])

    gp, dp, mp, vp = _adamw_small(small_pack(""), small_pack("m_"), small_pack("v_"), g_all, "adamw_small")
    full_shapes = [(L,) + shapes["b_ada"]] + [(L,) + shapes[k] for k in small_names]
    loss = _unpack(gp, full_shapes + [(128,)])[-1][0]
    for tag, packed in (("g", gp), ("d", dp), ("m", mp), ("v", vp)):
        for k, a in zip(["b_ada"] + small_names, _unpack(packed, full_shapes)):
            out.setdefault(k, {})
            out[k][tag] = a
    g_cw_full = out["conv_w"]["g"]
    cw_cols = D_CONV // N_CHIP
    g_cw = lax.dynamic_slice_in_dim(g_cw_full, chip * cw_cols, cw_cols, axis=2)
    pad_rows = lambda a: jnp.pad(a.reshape(L * CONV_WIDTH, cw_cols), ((0, 2), (0, 0)))
    g_, d_, m_, v_ = _adamw(pad_rows(conv_w), pad_rows(m_conv_w), pad_rows(v_conv_w), [pad_rows(g_cw)], "adamw_conv_w")
    out["conv_w"] = tuple(a[:L * CONV_WIDTH].reshape(L, CONV_WIDTH, cw_cols) for a in (g_, d_, m_, v_))

    names = ["w_ada", "b_ada", "w_in", "b_gate", "w_pool", "pool_scale", "rel_bias", "conv_w", "conv_b", "conv_ln_g",
             "conv_ln_b", "w_br_pool", "w_br_attn", "w_br_conv", "w_o", "ln_mix_g", "ln_mix_b", "w_ff1", "b_ff1",
             "w_ff2", "b_ff2", "ln_ff_g", "ln_ff_b"]

    def pick(k, i):
        o = out[k]
        return o[i] if isinstance(o, tuple) else o["gdmv"[i]].reshape(env[k].shape)

    return (loss, grad_x, *[pick(k, 0) for k in names], *[pick(k, 1) for k in names],
            *[pick(k, 2) for k in names], *[pick(k, 3) for k in names])
```

```python
import jax
import jax.numpy as jnp
import numpy as np
from jax import lax
from jax.experimental import pallas as pl
from jax.experimental.pallas import tpu as pltpu

F32 = jnp.float32
BF16 = jnp.bfloat16

D_MODEL = 1024
DEPTH = 2
CHUNK = 64
POOL_WINDOWS = (2, 4, 8, 16)
POOL_GROUP = 64
D_POOL = 256
N_HEADS = 8
HEAD_DIM = 64
D_ATTN = 512
N_PREV_CHUNKS = 8
REL_CLIP = 128
N_REL = 2 * REL_CLIP + 1
D_CONV = 256
CONV_WIDTH = 31
D_FF = 4 * D_MODEL
OFF_POOL, OFF_QKV, OFF_CONV, OFF_GATE = 0, 256, 1792, 2304
ALPHA = (2.0 * DEPTH) ** 0.25
LN_EPS = 1e-5
NEG_INF = -1e30
ADAM_LR, ADAM_B1, ADAM_B2, ADAM_EPS, ADAM_WD, ADAM_STEP = 0.001, 0.9, 0.999, 1e-08, 0.01, 10

N_DEV = 8
N_CHIP = 4
MESH = pl.DeviceIdType.MESH

QB = 2 * CHUNK
KPAD = N_PREV_CHUNKS * CHUNK
KW = QB + KPAD
SKEW_W = 768

VMEM_LIMIT = 56 * 1024 * 1024


def _cparams(**kw):
    return pltpu.CompilerParams(vmem_limit_bytes=VMEM_LIMIT, **kw)


def _full(shape):
    n = len(shape)
    return pl.BlockSpec(shape, lambda *_: (0,) * n)


_DIMS = {"nn": (((1,), (0,)), ((), ())), "nt": (((1,), (1,)), ((), ())), "tn": (((0,), (0,)), ((), ()))}


def _relu2(t):
    r = jnp.maximum(t, 0.0)
    return r * r


def _mm(a, b, mode, *, tm, tn, out_dtype, name, b_col0=0, n_out=None, bias=None, split_n=0, rider=None):
    if mode == "tn":
        k, m = a.shape
        n = b.shape[1] if n_out is None else n_out
        a_spec = pl.BlockSpec((k, tm), lambda i, j: (0, i))
        b_spec = pl.BlockSpec((k, tn), lambda i, j: (0, j + b_col0))
    elif mode == "nn":
        m, k = a.shape
        n = b.shape[1] if n_out is None else n_out
        a_spec = pl.BlockSpec((tm, k), lambda i, j: (i, 0))
        b_spec = pl.BlockSpec((k, tn), lambda i, j: (0, j + b_col0))
    else:
        m, k = a.shape
        n = b.shape[0] if n_out is None else n_out
        a_spec = pl.BlockSpec((tm, k), lambda i, j: (i, 0))
        b_spec = pl.BlockSpec((tn, k), lambda i, j: (j + b_col0, 0))
    assert m % tm == 0 and n % tn == 0, (name, m, n, tm, tn)
    dims = _DIMS[mode]

    def body(*refs):
        if bias is None:
            a_ref, b_ref, o_ref = refs
        else:
            a_ref, b_ref, bias_ref, o_ref = refs
        acc = lax.dot_general(a_ref[...].astype(BF16), b_ref[...].astype(BF16), dims, preferred_element_type=F32)
        if bias is not None:
            acc = acc + bias_ref[...]
        if split_n:
            for c in range(tn // split_n):
                o_ref[c] = acc[:, c * split_n:(c + 1) * split_n].astype(out_dtype)
        else:
            o_ref[...] = acc.astype(out_dtype)

    in_specs = [a_spec, b_spec]
    args = [a, b]
    if bias is not None:
        in_specs.append(pl.BlockSpec((1, tn), lambda i, j: (0, j)))
        args.append(bias)
    if split_n:
        out_spec = pl.BlockSpec((tn // split_n, tm, split_n), lambda i, j: (j, i, 0))
        out_shape = jax.ShapeDtypeStruct((n // split_n, m, split_n), out_dtype)
    else:
        out_spec = pl.BlockSpec((tm, tn), lambda i, j: (i, j))
        out_shape = jax.ShapeDtypeStruct((m, n), out_dtype)
    res = _call(body, name=name, grid=(m // tm, n // tn), in_specs=in_specs, out_specs=[out_spec],
                out_shape=[out_shape], scratch_shapes=[], args=args, rider=rider)
    return res[0] if rider is None else (res[0][0], res[1])


def _ln_hat(x):
    mu = jnp.mean(x, axis=-1, keepdims=True)
    xc = x - mu
    var = jnp.mean(xc * xc, axis=-1, keepdims=True)
    rstd = lax.rsqrt(var + LN_EPS)
    return xc * rstd, rstd


def _ln_hat_bwd(dhat, xhat, rstd):
    m1 = jnp.mean(dhat, axis=-1, keepdims=True)
    m2 = jnp.mean(dhat * xhat, axis=-1, keepdims=True)
    return rstd * (dhat - m1 - xhat * m2)


def _row_tile(s):
    return min(512, s)


def _acc_rows(ref, val, first):
    @pl.when(first)
    def _():
        ref[...] = jnp.zeros_like(ref)
    ref[...] += jnp.sum(val, axis=0, keepdims=True)


def _ln_mod(x, sc, sh, name):
    s, d = x.shape
    tm = _row_tile(s)

    def body(x_ref, sc_ref, sh_ref, u_ref):
        xhat, _ = _ln_hat(x_ref[...])
        u_ref[...] = (xhat * (1.0 + sc_ref[...]) + sh_ref[...]).astype(BF16)

    row = pl.BlockSpec((tm, d), lambda i: (i, 0))
    vec = pl.BlockSpec((1, d), lambda i: (0, 0))
    return pl.pallas_call(body, grid=(s // tm,), in_specs=[row, vec, vec], out_specs=row,
                          out_shape=jax.ShapeDtypeStruct((s, d), BF16), name=name, compiler_params=_cparams())(x, sc, sh)


def _resid_bwd_tile(dxo, x, f, g, gam):
    rhat, rstd = _ln_hat(ALPHA * x + g * f)
    dr = _ln_hat_bwd(dxo * gam, rhat, rstd)
    return ALPHA * dr, g * dr, dxo * rhat, dr * f


def _mm_ln_mod_bwd(a, b, x, sc, dres, name, rider=None, nxt=None):
    segs = list(a) if isinstance(a, (list, tuple)) else [a]
    s = segs[0].shape[0]
    k, d = b.shape
    assert sum(t.shape[1] for t in segs) == k
    tm = min(512 if k <= 4096 and nxt is None else 256, s)
    ns = len(segs)

    def body(*refs):
        seg_refs = refs[:ns]
        if nxt is None:
            b_ref, x_ref, sc_ref, dres_ref, dx_ref, dsc_ref, dsh_ref = refs[ns:]
        else:
            (b_ref, x_ref, sc_ref, dres_ref, xp_ref, fp_ref, gp_ref, gamp_ref,
             dresp_ref, dfp_ref, dsc_ref, dsh_ref, dgam_ref, dbet_ref, dg_ref, dbias_ref) = refs[ns:]
        first = pl.program_id(0) == 0
        duv, r0 = None, 0
        for seg_ref in seg_refs:
            w = seg_ref.shape[1]
            part = jnp.dot(seg_ref[...], b_ref[r0:r0 + w, :], preferred_element_type=F32)
            duv = part if duv is None else duv + part
            r0 += w
        xhat, rstd = _ln_hat(x_ref[...])
        dxv = dres_ref[...] + _ln_hat_bwd(duv * (1.0 + sc_ref[...]), xhat, rstd)
        _acc_rows(dsc_ref, duv * xhat, first)
        _acc_rows(dsh_ref, duv, first)
        if nxt is None:
            dx_ref[...] = dxv
        else:
            dresp, dfp, t_gam, t_g = _resid_bwd_tile(dxv, xp_ref[...], fp_ref[...], gp_ref[...], gamp_ref[...])
            dresp_ref[...] = dresp
            dfp_ref[...] = dfp.astype(BF16)
            _acc_rows(dgam_ref, t_gam, first)
            _acc_rows(dbet_ref, dxv, first)
            _acc_rows(dg_ref, t_g, first)
            _acc_rows(dbias_ref, dfp, first)

    row = pl.BlockSpec((tm, d), lambda i: (i, 0))
    vec = pl.BlockSpec((1, d), lambda i: (0, 0))
    vs = jax.ShapeDtypeStruct((1, d), F32)
    rows = jax.ShapeDtypeStruct((s, d), F32)
    in_specs = [pl.BlockSpec((tm, t.shape[1]), lambda i: (i, 0)) for t in segs] + [_full((k, d)), row, vec, row]
    args = (*segs, b, x, sc, dres)
    if nxt is None:
        out_specs, out_shape = [row, vec, vec], [rows, vs, vs]
    else:
        in_specs += [row, row, vec, vec]
        args += tuple(nxt)
        out_specs = [row, row] + [vec] * 6
        out_shape = [rows, jax.ShapeDtypeStruct((s, d), BF16)] + [vs] * 6
    res = _call(body, name=name, grid=(s // tm,), in_specs=in_specs, out_specs=out_specs, out_shape=out_shape,
                scratch_shapes=[], args=args, rider=rider)
    return tuple(res) if rider is None else (tuple(res[0]), res[1])


def _dw_segments(segs, u, name):
    s, d = u.shape
    tw = 256
    tiles = [t.shape[1] // tw for t in segs]
    starts = [sum(tiles[:j]) for j in range(len(segs))]
    ns = len(segs)

    def body(*refs):
        seg_refs, u_ref, o_ref = refs[:ns], refs[ns], refs[ns + 1]
        i = pl.program_id(0)
        for seg_ref, t0, nt in zip(seg_refs, starts, tiles):
            @pl.when((i >= t0) & (i < t0 + nt))
            def _(seg_ref=seg_ref):
                acc = lax.dot_general(seg_ref[...], u_ref[...], _DIMS["tn"], preferred_element_type=F32)
                o_ref[0] = acc[:, :d // 2].astype(BF16)
                o_ref[1] = acc[:, d // 2:].astype(BF16)

    def seg_spec(t0, nt):
        return pl.BlockSpec((s, tw), lambda i: (0, jnp.clip(i - t0, 0, nt - 1)))

    return pl.pallas_call(
        body, grid=(sum(tiles),), in_specs=[seg_spec(t0, nt) for t0, nt in zip(starts, tiles)] + [_full((s, d))],
        out_specs=pl.BlockSpec((2, tw, d // 2), lambda i: (0, i, 0)),
        out_shape=jax.ShapeDtypeStruct((2, sum(tiles) * tw, d // 2), BF16), name=name, compiler_params=_cparams(),
    )(*segs, u)


def _mm_resid_ln(a, b, bias, x, g, gam, bet, name, rider=None, mod_next=None):
    s, k = a.shape
    d = b.shape[1]
    tm = min(512, s)
    nb, nm = int(bias is not None), 2 * int(mod_next is not None)

    def body(*refs):
        a_ref, b_ref = refs[:2]
        x_ref, g_ref, gam_ref, bet_ref = refs[2 + nb:6 + nb]
        f_ref, o_ref = refs[6 + nb + nm:8 + nb + nm]
        f = jnp.dot(a_ref[...], b_ref[...], preferred_element_type=F32)
        if bias is not None:
            f = f + refs[2][...]
        f_ref[...] = f
        rhat, _ = _ln_hat(ALPHA * x_ref[...] + g_ref[...] * f)
        y = rhat * gam_ref[...] + bet_ref[...]
        o_ref[...] = y
        if mod_next is not None:
            sc_ref, sh_ref = refs[6 + nb:8 + nb]
            yhat, _ = _ln_hat(y)
            refs[8 + nb + nm][...] = (yhat * (1.0 + sc_ref[...]) + sh_ref[...]).astype(BF16)

    row = pl.BlockSpec((tm, d), lambda i: (i, 0))
    vec = pl.BlockSpec((1, d), lambda i: (0, 0))
    in_specs = [pl.BlockSpec((tm, k), lambda i: (i, 0)), _full((k, d))] + [vec] * nb + [row, vec, vec, vec] + [vec] * nm
    args = [a, b] + ([bias] if nb else []) + [x, g, gam, bet] + (list(mod_next) if nm else [])
    sh = jax.ShapeDtypeStruct((s, d), F32)
    out_specs, out_shape = [row, row], [sh, sh]
    if nm:
        out_specs, out_shape = out_specs + [row], out_shape + [jax.ShapeDtypeStruct((s, d), BF16)]
    res = _call(body, name=name, grid=(s // tm,), in_specs=in_specs, out_specs=out_specs, out_shape=out_shape,
                scratch_shapes=[], args=args, rider=rider)
    return tuple(res) if rider is None else (tuple(res[0]), res[1])


def _resid_ln_bwd(dxo, x, f, g, gam, name, tgt=None):
    s, d = x.shape
    tm = _row_tile(s)
    n = s // tm

    def body(*refs):
        if tgt is None:
            dxo_ref, x_ref, f_ref, g_ref, gam_ref, dres_ref, df_ref, dgam_ref, dbet_ref, dg_ref, dbias_ref = refs
            dxov = dxo_ref[...]
        else:
            (dxo_ref, t_ref, x_ref, f_ref, g_ref, gam_ref, dres_ref, df_ref, dgam_ref, dbet_ref, dg_ref, dbias_ref,
             loss_ref, sq_ref) = refs
            err = dxo_ref[...] - t_ref[...]
            dxov = err * (1.0 / d)
            _acc_rows(sq_ref, err * err, pl.program_id(0) == 0)

            @pl.when(pl.program_id(0) == n - 1)
            def _():
                tot = jnp.sum(sq_ref[...], axis=1, keepdims=True) * (0.5 / d)
                loss_ref[...] = jnp.broadcast_to(tot, (1, 128))

        first = pl.program_id(0) == 0
        dres, dfv, t_gam, t_g = _resid_bwd_tile(dxov, x_ref[...], f_ref[...], g_ref[...], gam_ref[...])
        dres_ref[...] = dres
        df_ref[...] = dfv.astype(BF16)
        _acc_rows(dgam_ref, t_gam, first)
        _acc_rows(dbet_ref, dxov, first)
        _acc_rows(dg_ref, t_g, first)
        _acc_rows(dbias_ref, dfv, first)

    row = pl.BlockSpec((tm, d), lambda i: (i, 0))
    vec = pl.BlockSpec((1, d), lambda i: (0, 0))
    vs = jax.ShapeDtypeStruct((1, d), F32)
    out_specs = [row, row, vec, vec, vec, vec]
    out_shape = [jax.ShapeDtypeStruct((s, d), F32), jax.ShapeDtypeStruct((s, d), BF16), vs, vs, vs, vs]
    if tgt is None:
        return pl.pallas_call(body, grid=(n,), in_specs=[row, row, row, vec, vec], out_specs=out_specs,
                              out_shape=out_shape, name=name, compiler_params=_cparams())(dxo, x, f, g, gam)
    return pl.pallas_call(body, grid=(n,), in_specs=[row, row, row, row, vec, vec],
                          out_specs=out_specs + [pl.BlockSpec((1, 128), lambda i: (0, 0))],
                          out_shape=out_shape + [jax.ShapeDtypeStruct((1, 128), F32)],
                          scratch_shapes=[pltpu.VMEM((1, d), F32)], name=name,
                          compiler_params=_cparams())(dxo, tgt, x, f, g, gam)


POOL_HALO = 16
POOL_ROWS = 256


def _pool_counts(r0, rows):
    t1 = (lax.broadcasted_iota(jnp.int32, (rows, 128), 0) + r0 + 1).astype(F32)
    low = lax.broadcasted_iota(jnp.int32, (rows, 128), 1) < POOL_GROUP
    wa = jnp.where(low, float(POOL_WINDOWS[0]), float(POOL_WINDOWS[1]))
    wb = jnp.where(low, float(POOL_WINDOWS[2]), float(POOL_WINDOWS[3]))
    return jnp.minimum(t1, wa), jnp.minimum(t1, wb), low


def _window_sums(win, off, rows, sign):
    def sl(j, half):
        return win[off + sign * j: off + sign * j + rows, 128 * half:128 * half + 128]
    a2 = sl(0, 0) + sl(1, 0)
    a4 = a2 + sl(2, 0) + sl(3, 0)
    a8 = sl(0, 1)
    for j in range(1, 8):
        a8 = a8 + sl(j, 1)
    a16 = a8
    for j in range(8, 16):
        a16 = a16 + sl(j, 1)
    return a2, a4, a8, a16


def _pool_fwd(zp, wp_bd, pscale, name):
    s = zp.shape[0]
    r = min(POOL_ROWS, s)

    def body(z_ref, wp_ref, sc_ref, p_ref, feat_ref, pad):
        pad[0:POOL_HALO, :] = jnp.zeros((POOL_HALO, D_POOL), F32)
        pad[POOL_HALO:, :] = z_ref[...]

        def step(i, carry):
            r0 = pl.multiple_of(i * r, r)
            win = pad[pl.ds(r0, r + POOL_HALO), :]
            a2, a4, a8, a16 = _window_sums(win, POOL_HALO, r, -1)
            ca, cb, low = _pool_counts(r0, r)
            x0 = win[POOL_HALO:, :]
            pa = jnp.where(low, a2, a4) / ca
            pb = jnp.where(low, a8, a16) / cb
            p = (jnp.concatenate([pa, pb], axis=1) - x0).astype(BF16)
            p_ref[pl.ds(r0, r), :] = p
            pw = jnp.dot(p, wp_ref[...], preferred_element_type=F32)
            feat_ref[pl.ds(r0, r), :] = (pw * sc_ref[...]).astype(BF16)
            return carry

        lax.fori_loop(0, s // r, step, 0)

    return pl.pallas_call(
        body, out_shape=[jax.ShapeDtypeStruct((s, D_POOL), BF16), jax.ShapeDtypeStruct((s, D_POOL), BF16)],
        scratch_shapes=[pltpu.VMEM((s + POOL_HALO, D_POOL), F32)], name=name, compiler_params=_cparams(),
    )(zp, wp_bd, pscale)


def _pool_bwd(dfeat, p, wp_bd, pscale, name):
    s = p.shape[0]
    r = min(POOL_ROWS, s)

    def body(df_ref, p_ref, wp_ref, sc_ref, dz_ref, dwp_ref, dsc_ref, gpad, dpbuf):
        dwp_ref[...] = jnp.zeros_like(dwp_ref)
        dsc_ref[...] = jnp.zeros_like(dsc_ref)
        gpad[s:, :] = jnp.zeros((POOL_HALO, D_POOL), F32)

        def step1(i, carry):
            r0 = pl.multiple_of(i * r, r)
            pv = p_ref[pl.ds(r0, r), :]
            dfv = df_ref[pl.ds(r0, r), :]
            pw = jnp.dot(pv, wp_ref[...], preferred_element_type=F32)
            dsc_ref[...] += jnp.sum(dfv * pw, axis=0, keepdims=True)
            dpw = (dfv * sc_ref[...]).astype(BF16)
            dwp_ref[...] += lax.dot_general(pv, dpw, _DIMS["tn"], preferred_element_type=F32)
            dp = lax.dot_general(dpw, wp_ref[...], _DIMS["nt"], preferred_element_type=F32)
            ca, cb, _ = _pool_counts(r0, r)
            gpad[pl.ds(r0, r), :] = dp / jnp.concatenate([ca, cb], axis=1)
            dpbuf[pl.ds(r0, r), :] = dp
            return carry

        lax.fori_loop(0, s // r, step1, 0)

        def step2(i, carry):
            r0 = pl.multiple_of(i * r, r)
            win = gpad[pl.ds(r0, r + POOL_HALO), :]
            a2, a4, a8, a16 = _window_sums(win, 0, r, 1)
            low = lax.broadcasted_iota(jnp.int32, (r, 128), 1) < POOL_GROUP
            acc = jnp.concatenate([jnp.where(low, a2, a4), jnp.where(low, a8, a16)], axis=1)
            dz_ref[pl.ds(r0, r), :] = (acc - dpbuf[pl.ds(r0, r), :]).astype(BF16)
            return carry

        lax.fori_loop(0, s // r, step2, 0)

    return pl.pallas_call(
        body,
        out_shape=[jax.ShapeDtypeStruct((s, D_POOL), BF16), jax.ShapeDtypeStruct((D_POOL, D_POOL), F32),
                   jax.ShapeDtypeStruct((1, D_POOL), F32)],
        scratch_shapes=[pltpu.VMEM((s + POOL_HALO, D_POOL), F32), pltpu.VMEM((s, D_POOL), F32)],
        name=name, compiler_params=_cparams(),
    )(dfeat, p, wp_bd, pscale)


def _skew_index():
    cp = lax.broadcasted_iota(jnp.int32, (SKEW_W, N_REL), 0)
    dist = jnp.where(cp < KW, KPAD - cp, KPAD + SKEW_W - cp)
    idx = jnp.clip(dist, -REL_CLIP, REL_CLIP) + REL_CLIP
    return (idx == lax.broadcasted_iota(jnp.int32, (SKEW_W, N_REL), 1)).astype(F32)


def _row_bits(b):
    return (lax.broadcasted_iota(jnp.int32, (QB, SKEW_W), 0) >> b) & 1 == 1


N_EDGE = KPAD // QB


def _bias_block(rel_bias, name):
    def body(rb_ref, o_ref):
        onehot = _skew_index()
        row0 = lax.dot_general(rb_ref[...], onehot, _DIMS["nt"], precision=lax.Precision.HIGHEST,
                               preferred_element_type=F32)
        r = lax.broadcasted_iota(jnp.int32, (QB, KW), 0)
        kk = lax.broadcasted_iota(jnp.int32, (QB, KW), 1)
        cq, ck = r // CHUNK, kk // CHUNK
        band = (ck >= cq) & (ck <= cq + N_PREV_CHUNKS)
        for h in range(N_HEADS):
            t = jnp.broadcast_to(row0[h:h + 1, :], (QB, SKEW_W))
            for b in range(7):
                t = jnp.where(_row_bits(b), pltpu.roll(t, 1 << b, 1), t)
            for e in range(N_EDGE + 1):
                o_ref[e, h] = jnp.where(band & (kk >= KPAD - e * QB), t[:, :KW], NEG_INF)

    return pl.pallas_call(body, out_shape=jax.ShapeDtypeStruct((N_EDGE + 1, N_HEADS, QB, KW), F32), name=name,
                          compiler_params=_cparams())(rel_bias)


def _bias_spec():
    return pl.BlockSpec((None, N_HEADS, QB, KW), lambda i: (jnp.minimum(i, N_EDGE), 0, 0, 0))


def _bias_block_bwd(ds_acc, name):
    def body(ds_ref, o_ref):
        sums = []
        for h in range(N_HEADS):
            t = jnp.concatenate([ds_ref[h], jnp.zeros((QB, SKEW_W - KW), F32)], axis=1)
            for b in range(7):
                t = jnp.where(_row_bits(b), pltpu.roll(t, SKEW_W - (1 << b), 1), t)
            sums.append(jnp.sum(t, axis=0, keepdims=True))
        allh = jnp.concatenate(sums, axis=0)
        o_ref[...] = jnp.dot(allh, _skew_index(), precision=lax.Precision.HIGHEST, preferred_element_type=F32)

    return pl.pallas_call(body, out_shape=jax.ShapeDtypeStruct((N_HEADS, N_REL), F32), name=name,
                          compiler_params=_cparams())(ds_acc)


def _scaled(q):
    return (q.astype(F32) * (HEAD_DIM ** -0.5)).astype(BF16)


def _probs(q, kw, bias_ref):
    sc = jnp.stack([lax.dot_general(q[:, HEAD_DIM * h:HEAD_DIM * (h + 1)], kw[:, HEAD_DIM * h:HEAD_DIM * (h + 1)],
                                    _DIMS["nt"], preferred_element_type=F32) + bias_ref[h] for h in range(N_HEADS)])
    e = jnp.exp(sc - jnp.max(sc, axis=-1, keepdims=True))
    return e * (1.0 / jnp.sum(e, axis=-1, keepdims=True))


def _load_padded_kv(qkv_hbm, kpad, vpad, sems, s):
    kpad[0:KPAD, :] = jnp.zeros((KPAD, D_ATTN), BF16)
    vpad[0:KPAD, :] = jnp.zeros((KPAD, D_ATTN), BF16)
    ck = pltpu.make_async_copy(qkv_hbm.at[:, D_ATTN:2 * D_ATTN], kpad.at[pl.ds(KPAD, s), :], sems.at[0])
    cv = pltpu.make_async_copy(qkv_hbm.at[:, 2 * D_ATTN:3 * D_ATTN], vpad.at[pl.ds(KPAD, s), :], sems.at[1])
    ck.start()
    cv.start()
    ck.wait()
    cv.wait()


def _attn_fwd(qkv, bias, name, rider=None):
    s = qkv.shape[0]

    def body(q_ref, qkv_hbm, bias_ref, o_ref, p_ref, kpad, vpad, sems):
        i = pl.program_id(0)

        @pl.when(i == 0)
        def _():
            _load_padded_kv(qkv_hbm, kpad, vpad, sems, s)

        base = pl.multiple_of(i * QB, QB)
        kw = kpad[pl.ds(base, KW), :]
        vw = vpad[pl.ds(base, KW), :]
        q = _scaled(q_ref[...])
        p = _probs(q, kw, bias_ref).astype(BF16)
        p_ref[...] = p
        outs = [jnp.dot(p[h], vw[:, HEAD_DIM * h:HEAD_DIM * (h + 1)], preferred_element_type=F32)
                for h in range(N_HEADS)]
        o_ref[...] = jnp.concatenate(outs, axis=1).astype(BF16)

    res = _call(
        body, name=name, grid=(s // QB,),
        in_specs=[pl.BlockSpec((QB, D_ATTN), lambda i: (i, 0)), pl.BlockSpec(memory_space=pl.ANY),
                  _bias_spec()],
        out_specs=[pl.BlockSpec((QB, D_ATTN), lambda i: (i, 0)), _probs_spec()],
        out_shape=[jax.ShapeDtypeStruct((s, D_ATTN), BF16), jax.ShapeDtypeStruct((N_HEADS, s, KW), BF16)],
        scratch_shapes=[pltpu.VMEM((s + KPAD, D_ATTN), BF16), pltpu.VMEM((s + KPAD, D_ATTN), BF16),
                        pltpu.SemaphoreType.DMA((2,))],
        args=(qkv, qkv, bias), rider=rider)
    return tuple(res) if rider is None else (tuple(res[0]), res[1])


def _probs_spec():
    return pl.BlockSpec((N_HEADS, QB, KW), lambda i: (0, i, 0))


def _attn_bwd(qkv, do, probs, name, rider=None):
    s = qkv.shape[0]
    n = s // QB

    def body(q_ref, qkv_hbm, do_ref, p_ref, dq_ref, dk_hbm, dv_hbm, ds_ref, kpad, vpad, dkacc, dvacc, sems):
        i = pl.program_id(0)

        @pl.when(i == 0)
        def _():
            _load_padded_kv(qkv_hbm, kpad, vpad, sems, s)
            dkacc[...] = jnp.zeros_like(dkacc)
            dvacc[...] = jnp.zeros_like(dvacc)
            ds_ref[...] = jnp.zeros_like(ds_ref)

        base = pl.multiple_of(i * QB, QB)
        kw = kpad[pl.ds(base, KW), :]
        vw = vpad[pl.ds(base, KW), :]
        q = _scaled(q_ref[...])
        dov = do_ref[...]
        heads = [slice(HEAD_DIM * h, HEAD_DIM * (h + 1)) for h in range(N_HEADS)]
        pb = p_ref[...]
        p = pb.astype(F32)
        dp = jnp.stack([lax.dot_general(dov[:, hs], vw[:, hs], _DIMS["nt"], preferred_element_type=F32) for hs in heads])
        ds = p * (dp - jnp.sum(dp * p, axis=-1, keepdims=True))
        ds_ref[...] += ds
        dsb = ds.astype(BF16)
        dvs = [lax.dot_general(pb[h], dov[:, hs], _DIMS["tn"], preferred_element_type=F32) for h, hs in enumerate(heads)]
        dqs = [jnp.dot(dsb[h], kw[:, hs], preferred_element_type=F32) for h, hs in enumerate(heads)]
        dks = [lax.dot_general(dsb[h], q[:, hs], _DIMS["tn"], preferred_element_type=F32) for h, hs in enumerate(heads)]
        dq_ref[...] = (jnp.concatenate(dqs, axis=1) * (HEAD_DIM ** -0.5)).astype(BF16)
        dkacc[pl.ds(base, KW), :] += jnp.concatenate(dks, axis=1)
        dvacc[pl.ds(base, KW), :] += jnp.concatenate(dvs, axis=1)

        @pl.when(i == n - 1)
        def _():
            def cast(j, carry):
                rows = pl.ds(pl.multiple_of(KPAD + j * 512, 512), 512)
                kpad[rows, :] = dkacc[rows, :].astype(BF16)
                vpad[rows, :] = dvacc[rows, :].astype(BF16)
                return carry

            lax.fori_loop(0, s // 512, cast, 0)
            ck = pltpu.make_async_copy(kpad.at[pl.ds(KPAD, s), :], dk_hbm, sems.at[0])
            cv = pltpu.make_async_copy(vpad.at[pl.ds(KPAD, s), :], dv_hbm, sems.at[1])
            ck.start()
            cv.start()
            ck.wait()
            cv.wait()

    blk = pl.BlockSpec((QB, D_ATTN), lambda i: (i, 0))
    acc_shape = jax.ShapeDtypeStruct((s, D_ATTN), BF16)
    return _call(
        body, name=name, grid=(n,),
        in_specs=[blk, pl.BlockSpec(memory_space=pl.ANY), blk, _probs_spec()],
        out_specs=[blk, pl.BlockSpec(memory_space=pl.ANY), pl.BlockSpec(memory_space=pl.ANY), _full((N_HEADS, QB, KW))],
        out_shape=[jax.ShapeDtypeStruct((s, D_ATTN), BF16), acc_shape, acc_shape,
                   jax.ShapeDtypeStruct((N_HEADS, QB, KW), F32)],
        scratch_shapes=[pltpu.VMEM((s + KPAD, D_ATTN), BF16), pltpu.VMEM((s + KPAD, D_ATTN), BF16),
                        pltpu.VMEM((s + KPAD, D_ATTN), F32), pltpu.VMEM((s + KPAD, D_ATTN), F32),
                        pltpu.SemaphoreType.DMA((2,))],
        args=(qkv, qkv, do, probs), rider=rider)


CONV_HALO = 32
CONV_ROWS = 64


def _sigmoid(t):
    return 1.0 / (1.0 + jnp.exp(-t))


CONV_WIN = CONV_ROWS + CONV_HALO - 8


def _row_windows(ref, r0, buf):
    win = ref[pl.ds(r0, CONV_ROWS + CONV_HALO), :]
    for j in range(1, 8):
        buf[j - 1] = win[j:j + CONV_WIN, :]

    def get(o):
        j, a = o % 8, o - o % 8
        if j == 0:
            return ref[pl.ds(r0 + a, CONV_ROWS), :]
        return buf[j - 1, a:a + CONV_ROWS, :]

    return get


def _glu_rows(z_ref, r0, rows):
    a = z_ref[pl.ds(r0, rows), 0:D_CONV]
    b = z_ref[pl.ds(r0, rows), D_CONV:2 * D_CONV]
    return a, _sigmoid(b)


def _conv_fwd(zc, conv_w, conv_b, ln_g, ln_b, name):
    s = zc.shape[0]
    rt = min(256, s)

    def body(z_ref, w_ref, cb_ref, g_ref, b_ref, cv_ref, feat_ref, hpad, shifts):
        hpad[0:CONV_HALO, :] = jnp.zeros((CONV_HALO, D_CONV), F32)

        def glu(i, carry):
            r0 = pl.multiple_of(i * rt, rt)
            a, sb = _glu_rows(z_ref, r0, rt)
            hpad[pl.ds(r0 + CONV_HALO, rt), :] = a * sb
            return carry

        lax.fori_loop(0, s // rt, glu, 0)
        w = w_ref[...]

        def conv(i, carry):
            r0 = pl.multiple_of(i * CONV_ROWS, CONV_ROWS)
            win = _row_windows(hpad, r0, shifts)
            acc = jnp.broadcast_to(cb_ref[...], (CONV_ROWS, D_CONV))
            for k in range(CONV_WIDTH):
                acc = acc + win(2 + k) * w[k:k + 1, :]
            cv_ref[pl.ds(r0, CONV_ROWS), :] = acc
            yhat, _ = _ln_hat(acc)
            y = yhat * g_ref[...] + b_ref[...]
            feat_ref[pl.ds(r0, CONV_ROWS), :] = (y * _sigmoid(y)).astype(BF16)
            return carry

        lax.fori_loop(0, s // CONV_ROWS, conv, 0)

    return pl.pallas_call(
        body, out_shape=[jax.ShapeDtypeStruct((s, D_CONV), F32), jax.ShapeDtypeStruct((s, D_CONV), BF16)],
        scratch_shapes=[pltpu.VMEM((s + CONV_HALO, D_CONV), F32), pltpu.VMEM((7, CONV_WIN, D_CONV), F32)],
        name=name, compiler_params=_cparams(),
    )(zc, conv_w, conv_b, ln_g, ln_b)


def _conv_bwd(dfeat, cv, zc, conv_w, ln_g, ln_b, name):
    s = zc.shape[0]
    rt = min(256, s)

    def body(df_ref, cv_ref, z_ref, w_ref, g_ref, b_ref, dz_ref, dw_ref, dcb_ref, dg_ref, db_ref, hpad, dcvpad, dwacc,
             hshifts, dshifts):
        hpad[0:CONV_HALO, :] = jnp.zeros((CONV_HALO, D_CONV), F32)
        dcvpad[s:, :] = jnp.zeros((CONV_HALO, D_CONV), F32)
        dwacc[...] = jnp.zeros_like(dwacc)
        dcb_ref[...] = jnp.zeros_like(dcb_ref)
        dg_ref[...] = jnp.zeros_like(dg_ref)
        db_ref[...] = jnp.zeros_like(db_ref)

        def pass1(i, carry):
            r0 = pl.multiple_of(i * rt, rt)
            a, sb = _glu_rows(z_ref, r0, rt)
            hpad[pl.ds(r0 + CONV_HALO, rt), :] = a * sb
            cvhat, rstd = _ln_hat(cv_ref[pl.ds(r0, rt), :])
            y = cvhat * g_ref[...] + b_ref[...]
            sg = _sigmoid(y)
            dy = df_ref[pl.ds(r0, rt), :] * (sg * (1.0 + y * (1.0 - sg)))
            dg_ref[...] += jnp.sum(dy * cvhat, axis=0, keepdims=True)
            db_ref[...] += jnp.sum(dy, axis=0, keepdims=True)
            dcv = _ln_hat_bwd(dy * g_ref[...], cvhat, rstd)
            dcb_ref[...] += jnp.sum(dcv, axis=0, keepdims=True)
            dcvpad[pl.ds(r0, rt), :] = dcv
            return carry

        lax.fori_loop(0, s // rt, pass1, 0)
        w = w_ref[...]

        def pass2(i, carry):
            r0 = pl.multiple_of(i * CONV_ROWS, CONV_ROWS)
            dwin = _row_windows(dcvpad, r0, dshifts)
            hwin = _row_windows(hpad, r0, hshifts)
            dcv = dwin(0)
            dh = jnp.zeros((CONV_ROWS, D_CONV), F32)
            for k in range(CONV_WIDTH):
                dh = dh + dwin(30 - k) * w[k:k + 1, :]
                prod = dcv * hwin(2 + k)
                dwacc[8 * k:8 * k + 8, :] += jnp.sum(prod.reshape(CONV_ROWS // 8, 8, D_CONV), axis=0)
            a, sb = _glu_rows(z_ref, r0, CONV_ROWS)
            dz_ref[pl.ds(r0, CONV_ROWS), :] = jnp.concatenate([dh * sb, dh * a * sb * (1.0 - sb)], axis=1).astype(BF16)
            return carry

        lax.fori_loop(0, s // CONV_ROWS, pass2, 0)
        dw_ref[...] = jnp.sum(dwacc[...].reshape(32, 8, D_CONV), axis=1)

    vs = jax.ShapeDtypeStruct((1, D_CONV), F32)
    return pl.pallas_call(
        body,
        out_shape=[jax.ShapeDtypeStruct((s, 2 * D_CONV), BF16), jax.ShapeDtypeStruct((32, D_CONV), F32), vs, vs, vs],
        scratch_shapes=[pltpu.VMEM((s + CONV_HALO, D_CONV), F32), pltpu.VMEM((s + CONV_HALO, D_CONV), F32),
                        pltpu.VMEM((256, D_CONV), F32), pltpu.VMEM((7, CONV_WIN, D_CONV), F32),
                        pltpu.VMEM((7, CONV_WIN, D_CONV), F32)],
        name=name, compiler_params=_cparams(),
    )(dfeat, cv, zc, conv_w, ln_g, ln_b)


def _branch_out(feats, wts, name):
    s = feats[0].shape[0]
    tm = min(1024, s)

    def body(*refs):
        for f_ref, w_ref, o_ref in zip(refs[:3], refs[3:6], refs[6:]):
            o_ref[...] = lax.dot_general(f_ref[...], w_ref[...], _DIMS["nt"], preferred_element_type=F32).astype(BF16)

    row = pl.BlockSpec((tm, D_MODEL), lambda i: (i, 0))
    sh = jax.ShapeDtypeStruct((s, D_MODEL), BF16)
    return pl.pallas_call(
        body, grid=(s // tm,),
        in_specs=[pl.BlockSpec((tm, f.shape[1]), lambda i: (i, 0)) for f in feats] + [_full(w.shape) for w in wts],
        out_specs=[row] * 3, out_shape=[sh] * 3, name=name, compiler_params=_cparams(),
    )(*feats, *wts)


def _branch_in_bwd(dys, wts, out_dtypes, name):
    s = dys[0].shape[0]
    tm = min(1024, s)

    def body(*refs):
        for d_ref, w_ref, o_ref in zip(refs[:3], refs[3:6], refs[6:]):
            o_ref[...] = jnp.dot(d_ref[...], w_ref[...], preferred_element_type=F32).astype(o_ref.dtype)

    row = pl.BlockSpec((tm, D_MODEL), lambda i: (i, 0))
    return pl.pallas_call(
        body, grid=(s // tm,), in_specs=[row] * 3 + [_full(w.shape) for w in wts],
        out_specs=[pl.BlockSpec((tm, w.shape[1]), lambda i: (i, 0)) for w in wts],
        out_shape=[jax.ShapeDtypeStruct((s, w.shape[1]), dt) for w, dt in zip(wts, out_dtypes)],
        name=name, compiler_params=_cparams(),
    )(*dys, *wts)


def _branch_dw(dys, feats, name):
    s = dys[0].shape[0]
    tm = 512

    def body(*refs):
        for d_ref, f_ref, o_ref in zip(refs[:3], refs[3:6], refs[6:]):
            acc = lax.dot_general(d_ref[...], f_ref[...], _DIMS["tn"], preferred_element_type=F32)
            half = acc.shape[1] // 2
            o_ref[0] = acc[:, :half].astype(BF16)
            o_ref[1] = acc[:, half:].astype(BF16)

    return pl.pallas_call(
        body, grid=(D_MODEL // tm,),
        in_specs=[pl.BlockSpec((s, tm), lambda i: (0, i))] * 3 + [_full(f.shape) for f in feats],
        out_specs=[pl.BlockSpec((2, tm, f.shape[1] // 2), lambda i: (0, i, 0)) for f in feats],
        out_shape=[jax.ShapeDtypeStruct((2, D_MODEL, f.shape[1] // 2), BF16) for f in feats],
        name=name, compiler_params=_cparams(),
    )(*dys, *feats)


def _merge(zg, b_gate, ys, name):
    s = zg.shape[0]
    tm = _row_tile(s)

    def body(zg_ref, bg_ref, y0_ref, y1_ref, y2_ref, o_ref):
        acc = None
        for j, y_ref in enumerate((y0_ref, y1_ref, y2_ref)):
            cs = slice(D_MODEL * j, D_MODEL * (j + 1))
            t = _sigmoid(zg_ref[:, cs] + bg_ref[:, cs]) * y_ref[...]
            acc = t if acc is None else acc + t
        o_ref[...] = acc.astype(BF16)

    row = pl.BlockSpec((tm, D_MODEL), lambda i: (i, 0))
    return pl.pallas_call(
        body, grid=(s // tm,),
        in_specs=[pl.BlockSpec((tm, 3 * D_MODEL), lambda i: (i, 0)), _full((1, 3 * D_MODEL)), row, row, row],
        out_specs=row, out_shape=jax.ShapeDtypeStruct((s, D_MODEL), BF16), name=name, compiler_params=_cparams(),
    )(zg, b_gate, *ys)


def _merge_bwd(dmix, w_o, zg, b_gate, ys, name):
    s = zg.shape[0]
    tm = min(256, s)

    def body(dmix_ref, wo_ref, zg_ref, bg_ref, y0_ref, y1_ref, y2_ref, d0_ref, d1_ref, d2_ref, dzg_ref, dbg_ref):
        first = pl.program_id(0) == 0

        @pl.when(first)
        def _():
            dbg_ref[...] = jnp.zeros_like(dbg_ref)

        dmv = lax.dot_general(dmix_ref[...], wo_ref[...], _DIMS["nt"], preferred_element_type=F32)
        for j, (y_ref, d_ref) in enumerate(((y0_ref, d0_ref), (y1_ref, d1_ref), (y2_ref, d2_ref))):
            cs = slice(D_MODEL * j, D_MODEL * (j + 1))
            g = _sigmoid(zg_ref[:, cs] + bg_ref[:, cs])
            d_ref[...] = (dmv * g).astype(BF16)
            dzg = dmv * y_ref[...] * g * (1.0 - g)
            dzg_ref[:, cs] = dzg.astype(BF16)
            dbg_ref[:, cs] += jnp.sum(dzg, axis=0, keepdims=True)

    row = pl.BlockSpec((tm, D_MODEL), lambda i: (i, 0))
    wide = pl.BlockSpec((tm, 3 * D_MODEL), lambda i: (i, 0))
    yb = jax.ShapeDtypeStruct((s, D_MODEL), BF16)
    return pl.pallas_call(
        body, grid=(s // tm,),
        in_specs=[row, _full(w_o.shape), wide, _full((1, 3 * D_MODEL)), row, row, row],
        out_specs=[row, row, row, wide, _full((1, 3 * D_MODEL))],
        out_shape=[yb, yb, yb, jax.ShapeDtypeStruct((s, 3 * D_MODEL), BF16), jax.ShapeDtypeStruct((1, 3 * D_MODEL), F32)],
        name=name, compiler_params=_cparams(),
    )(dmix, w_o, zg, b_gate, *ys)


def _ff_hidden(u2, w_ff1t, b_ff1, name, rider=None):
    s = u2.shape[0]
    tm, tn = min(2048, s), 1024

    def body(a_ref, b_ref, bias_ref, pre_ref, h_ref):
        acc = lax.dot_general(a_ref[...], b_ref[...], _DIMS["nt"], preferred_element_type=F32) + bias_ref[...]
        pre_ref[...] = acc.astype(BF16)
        h_ref[...] = _relu2(acc).astype(BF16)

    blk = pl.BlockSpec((tm, tn), lambda i, j: (i, j))
    sh = jax.ShapeDtypeStruct((s, D_FF), BF16)
    res = _call(body, name=name, grid=(s // tm, D_FF // tn),
                in_specs=[pl.BlockSpec((tm, D_MODEL), lambda i, j: (i, 0)), pl.BlockSpec((tn, D_MODEL), lambda i, j: (j, 0)),
                          pl.BlockSpec((1, tn), lambda i, j: (0, j))],
                out_specs=[blk, blk], out_shape=[sh, sh], scratch_shapes=[], args=(u2, w_ff1t, b_ff1), rider=rider)
    return tuple(res) if rider is None else (tuple(res[0]), res[1])


def _ff_hidden_bwd(dff, w_ff2, hpre, name, rider=None):
    s = dff.shape[0]
    tm, tn = min(1024, s), 1024

    def body(a_ref, b_ref, h_ref, o_ref, sum_ref):
        dh = lax.dot_general(a_ref[...], b_ref[...], _DIMS["nt"], preferred_element_type=F32)
        dpre = dh * (2.0 * jnp.maximum(h_ref[...].astype(F32), 0.0))
        o_ref[...] = dpre.astype(BF16)
        _acc_rows(sum_ref, dpre, pl.program_id(1) == 0)

    res = _call(
        body, name=name, grid=(D_FF // tn, s // tm),
        in_specs=[pl.BlockSpec((tm, D_MODEL), lambda j, i: (i, 0)), pl.BlockSpec((tn, D_MODEL), lambda j, i: (j, 0)),
                  pl.BlockSpec((tm, tn), lambda j, i: (i, j))],
        out_specs=[pl.BlockSpec((tm, tn), lambda j, i: (i, j)), pl.BlockSpec((1, tn), lambda j, i: (0, j))],
        out_shape=[jax.ShapeDtypeStruct((s, D_FF), BF16), jax.ShapeDtypeStruct((1, D_FF), F32)],
        scratch_shapes=[], args=(dff, w_ff2, hpre), rider=rider)
    return tuple(res) if rider is None else (tuple(res[0]), res[1])


def _silu(t):
    return t * _sigmoid(t)


def _mod_fwd(c_all, w_ada_sh, b_ada_sh, name):
    cols = w_ada_sh.shape[2]

    def body(c_ref, w_ref, b_ref, o_ref):
        ca = _silu(c_ref[...]).astype(BF16)
        o_ref[0] = jnp.dot(ca, w_ref[0].astype(BF16), preferred_element_type=F32) + b_ref[0]

    return pl.pallas_call(
        body, grid=(DEPTH,),
        in_specs=[_full((N_DEV, D_MODEL)), pl.BlockSpec((1, D_MODEL, cols), lambda l: (l, 0, 0)),
                  pl.BlockSpec((1, 1, cols), lambda l: (l, 0, 0))],
        out_specs=pl.BlockSpec((1, N_DEV, cols), lambda l: (l, 0, 0)),
        out_shape=jax.ShapeDtypeStruct((DEPTH, N_DEV, cols), F32), name=name, compiler_params=_cparams(),
    )(c_all, w_ada_sh, b_ada_sh)


def _mod_bwd(c_all, dmod_sh, name):
    cols = dmod_sh.shape[2]

    def body(c_ref, d_ref, o_ref):
        ca = _silu(c_ref[...])
        o_ref[0] = lax.dot_general(ca, d_ref[0], _DIMS["tn"], precision=lax.Precision.HIGHEST,
                                   preferred_element_type=F32)

    return pl.pallas_call(
        body, grid=(DEPTH,),
        in_specs=[_full((N_DEV, D_MODEL)), pl.BlockSpec((1, N_DEV, cols), lambda l: (l, 0, 0))],
        out_specs=pl.BlockSpec((1, D_MODEL, cols), lambda l: (l, 0, 0)),
        out_shape=jax.ShapeDtypeStruct((DEPTH, D_MODEL, cols), F32), name=name, compiler_params=_cparams(),
    )(c_all, dmod_sh)


def _flat_tiles(rows, cols, itemsize_total):
    budget = 12 * 1024 * 1024
    tr = rows
    while tr % 32 == 0 and tr * cols * itemsize_total > budget:
        tr //= 2
    return tr


def _sum_cores(dws, recvs, place, name):
    k = len(dws)

    def body(place_ref, *refs):
        for a_ref, b_ref, o_ref in zip(refs[:k], refs[k:2 * k], refs[2 * k:]):
            o_ref[...] = (a_ref[...].astype(F32) + b_ref[...].astype(F32)).astype(BF16)

    whole = [pl.BlockSpec(a.shape[1:], lambda i, pr: (0, 0)) for a in dws]
    mine = [pl.BlockSpec((None,) + a.shape[1:], lambda i, pr: (pr[0], 0, 0)) for a in dws]
    grid_spec = pltpu.PrefetchScalarGridSpec(num_scalar_prefetch=1, grid=(1,), in_specs=mine + whole, out_specs=whole)
    return pl.pallas_call(body, grid_spec=grid_spec, out_shape=[jax.ShapeDtypeStruct(a.shape[1:], BF16) for a in dws],
                          name=name, compiler_params=_cparams())(place, *dws, *recvs)


def _sum_chips(hs, rs, place, name):
    k = len(hs)

    def body(place_ref, *refs):
        for h_ref, r_ref, o_ref in zip(refs[:k], refs[k:2 * k], refs[2 * k:]):
            o_ref[...] = ((h_ref[...].astype(F32) + r_ref[0].astype(F32)) + r_ref[1].astype(F32)) + r_ref[2].astype(F32)

    own = [pl.BlockSpec((None,) + h.shape[1:], lambda i, pr: (pr[1], 0, 0)) for h in hs]
    got = [pl.BlockSpec(r.shape, lambda i, pr: (0, 0, 0)) for r in rs]
    out = [pl.BlockSpec(h.shape[1:], lambda i, pr: (0, 0)) for h in hs]
    grid_spec = pltpu.PrefetchScalarGridSpec(num_scalar_prefetch=1, grid=(1,), in_specs=own + got, out_specs=out)
    return pl.pallas_call(body, grid_spec=grid_spec, out_shape=[jax.ShapeDtypeStruct(h.shape[1:], F32) for h in hs],
                          name=name, compiler_params=_cparams())(place, *hs, *rs)


def _adam_math(w, g, m, v):
    m2 = ADAM_B1 * m + (1.0 - ADAM_B1) * g
    v2 = ADAM_B2 * v + (1.0 - ADAM_B2) * (g * g)
    m_hat = m2 / (1.0 - ADAM_B1 ** ADAM_STEP)
    v_hat = v2 / (1.0 - ADAM_B2 ** ADAM_STEP)
    delta = -ADAM_LR * (m_hat / (jnp.sqrt(v_hat) + ADAM_EPS) + ADAM_WD * w)
    return delta, m2, v2


def _adamw(w, m, v, grads, name):
    r, c = w.shape
    tr = _flat_tiles(r, c, 4 * (7 + len(grads)))

    def body(*refs):
        w_ref, m_ref, v_ref = refs[:3]
        g_refs = refs[3:3 + len(grads)]
        g_ref, d_ref, m2_ref, v2_ref = refs[3 + len(grads):]
        g = g_refs[0][...]
        for gr in g_refs[1:]:
            g = g + gr[...]
        delta, m2, v2 = _adam_math(w_ref[...], g, m_ref[...], v_ref[...])
        g_ref[...] = g
        d_ref[...] = delta
        m2_ref[...] = m2
        v2_ref[...] = v2

    blk = pl.BlockSpec((tr, c), lambda i: (i, 0))
    sh = jax.ShapeDtypeStruct((r, c), F32)
    return pl.pallas_call(body, grid=(r // tr,), in_specs=[blk] * (3 + len(grads)), out_specs=[blk] * 4,
                          out_shape=[sh] * 4, name=name, compiler_params=_cparams())(w, m, v, *grads)


def _adamw_halves(w, m, v, own, other, place, split, name):
    nl, r, c = w.shape
    hr, hc = own[0].shape
    tr = _flat_tiles(hr, hc, 4 * (7 + 2 * nl))
    nt = hr // tr
    if split == "rows":
        w_spec = pl.BlockSpec((None, tr, c), lambda l, h, t, pr: (l, h * nt + t, 0))
    else:
        w_spec = pl.BlockSpec((None, tr, hc), lambda l, h, t, pr: (l, t, h))

    def g_spec(layer, mine):
        return pl.BlockSpec((tr, hc), lambda l, h, t, pr: (jnp.where((l == layer) & ((h == pr[0]) == mine), t, nt - 1), 0))

    def body(place_ref, w_ref, m_ref, v_ref, *refs):
        own_refs, other_refs = refs[:nl], refs[nl:2 * nl]
        g_ref, d_ref, m2_ref, v2_ref = refs[2 * nl:]
        layer = pl.program_id(0)
        mine = pl.program_id(1) == place_ref[0]
        g = None
        for li in range(nl):
            cand = jnp.where(mine, own_refs[li][...], other_refs[li][...])
            g = cand if g is None else jnp.where(layer == li, cand, g)
        delta, m2, v2 = _adam_math(w_ref[...], g, m_ref[...], v_ref[...])
        g_ref[...] = g
        d_ref[...] = delta
        m2_ref[...] = m2
        v2_ref[...] = v2

    sh = jax.ShapeDtypeStruct((nl, r, c), F32)
    g_specs = [g_spec(li, True) for li in range(nl)] + [g_spec(li, False) for li in range(nl)]
    return _call(body, name=name, grid=(nl, 2, nt), in_specs=[w_spec] * 3 + g_specs, out_specs=[w_spec] * 4,
                 out_shape=[sh] * 4, scratch_shapes=[], args=(w, m, v, *own, *other), prefetch=(place,))


def _adamw_small(w, m, v, g_all, name):
    r, c = w.shape

    def body(w_ref, m_ref, v_ref, g_ref, go_ref, d_ref, m2_ref, v2_ref):
        g = g_ref[0]
        for b in range(1, N_DEV):
            g = g + g_ref[b]
        delta, m2, v2 = _adam_math(w_ref[...], g, m_ref[...], v_ref[...])
        go_ref[...] = g
        d_ref[...] = delta
        m2_ref[...] = m2
        v2_ref[...] = v2

    sh = jax.ShapeDtypeStruct((r, c), F32)
    return pl.pallas_call(body, out_shape=[sh] * 4, name=name, compiler_params=_cparams())(w, m, v, g_all)


def _me():
    return lax.axis_index("x"), lax.axis_index("y"), lax.axis_index("c")


def _flip(v, bit):
    return 1 - v if bit else v


def _allgather_small(blk, name):
    r, c = blk.shape

    def body(x_ref, o_ref, send_sems, recv_sems):
        x, y, cc = _me()
        me = 4 * x + 2 * y + cc
        copies = []
        for k in range(1, N_DEV):
            peer = (_flip(x, k & 4), _flip(y, k & 2), _flip(cc, k & 1))
            cp = pltpu.make_async_remote_copy(src_ref=x_ref, dst_ref=o_ref.at[me], send_sem=send_sems.at[k - 1],
                                              recv_sem=recv_sems.at[k - 1], device_id=peer, device_id_type=MESH)
            cp.start()
            copies.append(cp)
        o_ref[me] = x_ref[...]
        for cp in copies:
            cp.wait()

    return pl.pallas_call(
        body, out_shape=jax.ShapeDtypeStruct((N_DEV, r, c), F32),
        in_specs=[pl.BlockSpec(memory_space=pltpu.VMEM)], out_specs=pl.BlockSpec(memory_space=pltpu.VMEM),
        scratch_shapes=[pltpu.SemaphoreType.DMA((N_DEV - 1,)), pltpu.SemaphoreType.DMA((N_DEV - 1,))],
        name=name, compiler_params=_cparams(),
    )(blk)


class _Rider:
    def __init__(self, arrays, out_shapes, scratch_shapes, start, finish):
        self.arrays, self.out_shapes, self.scratch_shapes = list(arrays), list(out_shapes), list(scratch_shapes)
        self.start, self.finish = start, finish


def _call(body, *, name, grid, in_specs, out_specs, out_shape, scratch_shapes, args, rider=None, prefetch=()):
    npf = len(prefetch)

    def launch(fn, in_specs, out_specs, out_shape, scratch_shapes, args):
        grid_spec = pltpu.PrefetchScalarGridSpec(num_scalar_prefetch=npf, grid=grid, in_specs=in_specs,
                                                 out_specs=out_specs, scratch_shapes=scratch_shapes)
        return pl.pallas_call(fn, grid_spec=grid_spec, out_shape=out_shape, name=name,
                              compiler_params=_cparams())(*prefetch, *args)

    if rider is None:
        return launch(body, list(in_specs), list(out_specs), list(out_shape), list(scratch_shapes), args)
    ni, no, ns = len(in_specs), len(out_specs), len(scratch_shapes)
    ri, ro = len(rider.arrays), len(rider.out_shapes)
    steps = int(np.prod(grid))

    def wrapped(*refs):
        pf, refs = refs[:npf], refs[npf:]
        h_in, r_in = refs[:ni], refs[ni:ni + ri]
        h_out, r_out = refs[ni + ri:ni + ri + no], refs[ni + ri + no:ni + ri + no + ro]
        h_scr, r_scr = refs[ni + ri + no + ro:ni + ri + no + ro + ns], refs[ni + ri + no + ro + ns:]
        step = pl.program_id(0)
        for d in range(1, len(grid)):
            step = step * grid[d] + pl.program_id(d)

        @pl.when(step == 0)
        def _():
            rider.start(r_in, r_out, r_scr)

        body(*pf, *h_in, *h_out, *h_scr)

        @pl.when(step == steps - 1)
        def _():
            rider.finish(r_in, r_out, r_scr)

    anyspec = pl.BlockSpec(memory_space=pl.ANY)
    res = launch(wrapped, list(in_specs) + [anyspec] * ri, list(out_specs) + [anyspec] * ro,
                 list(out_shape) + rider.out_shapes, list(scratch_shapes) + rider.scratch_shapes,
                 list(args) + rider.arrays)
    return res[:no], res[no:]


def _run_rider(rider, name):
    ri = len(rider.arrays)

    def body(*refs):
        r_in, r_out, r_scr = refs[:ri], refs[ri:ri + len(rider.out_shapes)], refs[ri + len(rider.out_shapes):]
        rider.start(r_in, r_out, r_scr)
        rider.finish(r_in, r_out, r_scr)

    anyspec = pl.BlockSpec(memory_space=pl.ANY)
    return pl.pallas_call(body, in_specs=[anyspec] * ri, out_specs=[anyspec] * len(rider.out_shapes),
                          out_shape=rider.out_shapes, scratch_shapes=rider.scratch_shapes, name=name,
                          compiler_params=_cparams())(*rider.arrays)


def _allgather_rider(blk):
    def copies(ins, outs, scr):
        send_sems, recv_sems, loc_sems, stage = scr
        x, y, cc = _me()
        me = 4 * x + 2 * y + cc
        remote = [pltpu.make_async_remote_copy(
            src_ref=ins[0], dst_ref=outs[0].at[me], send_sem=send_sems.at[k - 1], recv_sem=recv_sems.at[k - 1],
            device_id=(_flip(x, k & 4), _flip(y, k & 2), _flip(cc, k & 1)), device_id_type=MESH) for k in range(1, N_DEV)]
        return remote, pltpu.make_async_copy(ins[0], stage, loc_sems.at[0]), (outs[0].at[me], stage, loc_sems.at[1])

    def start(ins, outs, scr):
        remote, lin, _ = copies(ins, outs, scr)
        lin.start()
        for cp in remote:
            cp.start()

    def finish(ins, outs, scr):
        remote, lin, (dst, stage, sem) = copies(ins, outs, scr)
        lin.wait()
        lout = pltpu.make_async_copy(stage, dst, sem)
        lout.start()
        for cp in remote:
            cp.wait()
        lout.wait()

    return _Rider([blk], [jax.ShapeDtypeStruct((N_DEV,) + blk.shape, blk.dtype)],
                  [pltpu.SemaphoreType.DMA((N_DEV - 1,)), pltpu.SemaphoreType.DMA((N_DEV - 1,)),
                   pltpu.SemaphoreType.DMA((2,)), pltpu.VMEM(blk.shape, blk.dtype)], start, finish)


def _gather_rider(shards):
    n = len(shards)

    def copies(ins, outs, scr, relay=True):
        ici_send, ici_recv, d2d_send, d2d_recv, loc_sems = scr[:5]
        stage = scr[5:]
        x, y, cc = _me()
        chip = 2 * x + y
        sibling = (x, y, 1 - cc)
        local, sends, relays = [], [], []
        for j in range(n):
            def rows(ch, h, j=j):
                return outs[j].at[ch, h]

            lc = pltpu.make_async_copy(ins[j], stage[j], loc_sems.at[j])
            local.append((lc, pltpu.make_async_copy(stage[j], outs[j].at[chip], loc_sems.at[n + j]) if relay else None))
            for k in range(1, N_CHIP):
                px, py = _flip(x, k & 2), _flip(y, k & 1)
                pchip = 2 * px + py
                q = 3 * j + k - 1
                out_cp = pltpu.make_async_remote_copy(src_ref=ins[j].at[cc], dst_ref=rows(chip, cc),
                                                      send_sem=ici_send.at[q], recv_sem=ici_recv.at[q],
                                                      device_id=(px, py, cc), device_id_type=MESH)
                sends.append(out_cp)
                if not relay:
                    continue
                arrival = pltpu.make_async_remote_copy(src_ref=rows(pchip, cc), dst_ref=rows(pchip, cc),
                                                       send_sem=ici_send.at[q], recv_sem=ici_recv.at[q],
                                                       device_id=(px, py, cc), device_id_type=MESH)
                forward = pltpu.make_async_remote_copy(src_ref=rows(pchip, cc), dst_ref=rows(pchip, cc),
                                                       send_sem=d2d_send.at[q], recv_sem=d2d_recv.at[q],
                                                       device_id=sibling, device_id_type=MESH)
                from_sibling = pltpu.make_async_remote_copy(src_ref=rows(pchip, 1 - cc), dst_ref=rows(pchip, 1 - cc),
                                                            send_sem=d2d_send.at[q], recv_sem=d2d_recv.at[q],
                                                            device_id=sibling, device_id_type=MESH)
                relays.append((arrival, forward, from_sibling))
        return local, sends, relays

    def start(ins, outs, scr):
        local, sends, _ = copies(ins, outs, scr, relay=False)
        for lin, _ in local:
            lin.start()
        for cp in sends:
            cp.start()

    def finish(ins, outs, scr):
        local, sends, relays = copies(ins, outs, scr)
        for lin, lout in local:
            lin.wait()
            lout.start()
        for arrival, forward, _ in relays:
            arrival.wait_recv()
            forward.start()
        for cp in sends:
            cp.wait_send()
        for _, forward, from_sibling in relays:
            forward.wait_send()
            from_sibling.wait_recv()
        for _, lout in local:
            lout.wait()

    scratch = [pltpu.SemaphoreType.DMA((3 * n,)), pltpu.SemaphoreType.DMA((3 * n,)), pltpu.SemaphoreType.DMA((3 * n,)),
               pltpu.SemaphoreType.DMA((3 * n,)), pltpu.SemaphoreType.DMA((2 * n,))]
    scratch += [pltpu.VMEM(a.shape, a.dtype) for a in shards]
    return _Rider(shards, [jax.ShapeDtypeStruct((N_CHIP,) + a.shape, a.dtype) for a in shards], scratch, start, finish)


def _sibling_rider(arrs, other_half=False):
    n = len(arrs)

    def copies(ins, outs, scr):
        send_sems, recv_sems = scr
        x, y, cc = _me()
        return [pltpu.make_async_remote_copy(
            src_ref=ins[j].at[1 - cc] if other_half else ins[j], dst_ref=outs[j], send_sem=send_sems.at[j],
            recv_sem=recv_sems.at[j], device_id=(x, y, 1 - cc), device_id_type=MESH) for j in range(n)]

    def start(ins, outs, scr):
        for cp in copies(ins, outs, scr):
            cp.start()

    def finish(ins, outs, scr):
        for cp in copies(ins, outs, scr):
            cp.wait()

    return _Rider(arrs, [jax.ShapeDtypeStruct(a.shape[1:] if other_half else a.shape, a.dtype) for a in arrs],
                  [pltpu.SemaphoreType.DMA((n,)), pltpu.SemaphoreType.DMA((n,))], start, finish)


def _sibling_send(arrs, name, other_half=False):
    return _run_rider(_sibling_rider(arrs, other_half), name)


def _join_riders(first, second):
    ni, no, ns = len(first.arrays), len(first.out_shapes), len(first.scratch_shapes)

    def split(ins, outs, scr):
        return (ins[:ni], outs[:no], scr[:ns]), (ins[ni:], outs[no:], scr[ns:])

    def start(ins, outs, scr):
        a, b = split(ins, outs, scr)
        first.start(*a)
        second.start(*b)

    def finish(ins, outs, scr):
        a, b = split(ins, outs, scr)
        first.finish(*a)
        second.finish(*b)

    return _Rider(first.arrays + second.arrays, first.out_shapes + second.out_shapes,
                  first.scratch_shapes + second.scratch_shapes, start, finish)


def _scatter_rider(arrs):
    n = len(arrs)

    def copies(ins, outs, scr):
        send_sems, recv_sems = scr
        x, y, cc = _me()
        cps = []
        for j in range(n):
            for k in range(1, N_CHIP):
                px, py = _flip(x, k & 2), _flip(y, k & 1)
                cps.append(pltpu.make_async_remote_copy(
                    src_ref=ins[j].at[2 * px + py], dst_ref=outs[j].at[k - 1], send_sem=send_sems.at[3 * j + k - 1],
                    recv_sem=recv_sems.at[3 * j + k - 1], device_id=(px, py, cc), device_id_type=MESH))
        return cps

    def start(ins, outs, scr):
        for cp in copies(ins, outs, scr):
            cp.start()

    def finish(ins, outs, scr):
        for cp in copies(ins, outs, scr):
            cp.wait()

    return _Rider(arrs, [jax.ShapeDtypeStruct((N_CHIP - 1,) + a.shape[1:], a.dtype) for a in arrs],
                  [pltpu.SemaphoreType.DMA((3 * n,)), pltpu.SemaphoreType.DMA((3 * n,))], start, finish)


COL_SHARDED = ("w_in", "w_br_pool", "w_br_attn", "w_br_conv", "w_ff1")
ROW_SHARDED = ("w_o", "w_ff2")
BIG = COL_SHARDED + ROW_SHARDED
SMALL = ("b_ada", "b_gate", "w_pool", "pool_scale", "rel_bias", "conv_w", "conv_b", "conv_ln_g", "conv_ln_b",
         "ln_mix_g", "ln_mix_b", "b_ff1", "b_ff2", "ln_ff_g", "ln_ff_b")
PACK_W = 1024


def _pack(parts):
    rows = []
    for a in parts:
        flat = a.reshape(-1)
        n = -(-flat.shape[0] // PACK_W) * PACK_W
        rows.append(jnp.pad(flat, (0, n - flat.shape[0])).reshape(-1, PACK_W))
    out = jnp.concatenate(rows, axis=0)
    r = -(-out.shape[0] // 8) * 8
    return jnp.pad(out, ((0, r - out.shape[0]), (0, 0)))


def _unpack(packed, shapes):
    out, r0 = [], 0
    for shp in shapes:
        size = int(np.prod(shp))
        nr = -(-size // PACK_W)
        out.append(packed[r0:r0 + nr].reshape(-1)[:size].reshape(shp))
        r0 += nr
    return out


def _hosted(fn, hook, *args, **kw):
    if hook is None:
        return fn(*args, **kw)
    res, rider_out = fn(*args, rider=hook[0], **kw)
    hook[1](rider_out)
    return res


def _layer_fwd(l, x, mod, W, P, hooks=None, u=None):
    hooks = hooks or {}
    s = x.shape[0]
    sh_m, sc_m, g_m, sh_f, sc_f, g_f = [mod[l:l + 1, D_MODEL * j:D_MODEL * (j + 1)] for j in range(6)]
    n = lambda t: f"{t}{l}"
    w_in = W["w_in"][l]
    if u is None:
        u = _ln_mod(x, sc_m, sh_m, n("ln_mod_mix"))
    zp = _mm(u, w_in, "nt", tm=s, tn=256, out_dtype=F32, name=n("z_pool"), b_col0=0, n_out=D_POOL)
    qkv = _mm(u, w_in, "nt", tm=s, tn=256, out_dtype=BF16, name=n("z_qkv"), b_col0=OFF_QKV // 256, n_out=3 * D_ATTN)
    zc = _mm(u, w_in, "nt", tm=s, tn=256, out_dtype=F32, name=n("z_conv"), b_col0=OFF_CONV // 256, n_out=2 * D_CONV)
    zg = _hosted(_mm, hooks.get("z_gate"), u, w_in, "nt", tm=s, tn=768, out_dtype=BF16, name=n("z_gate"),
                 b_col0=OFF_GATE // 768, n_out=3 * D_MODEL)

    p, feat_pool = _pool_fwd(zp, P["wp_bd"][l], P["pool_scale"][l], n("pool_fwd"))
    bias = _bias_block(P["rel_bias"][l], n("bias_block"))
    o, probs = _hosted(_attn_fwd, hooks.get("attn"), qkv, bias, n("attn_fwd"))
    cv, feat_conv = _conv_fwd(zc, P["conv_w"][l], P["conv_b"][l], P["conv_ln_g"][l], P["conv_ln_b"][l], n("conv_fwd"))

    branch_w = (W["w_br_pool"][l], W["w_br_attn"][l], W["w_br_conv"][l])
    ys = tuple(_branch_out((feat_pool, o, feat_conv), branch_w, n("branch_out")))
    merged = _merge(zg, P["b_gate"][l], ys, n("merge"))
    mix, x1, u2 = _mm_resid_ln(merged, W["w_o"][l], None, x, g_m, P["ln_mix_g"][l], P["ln_mix_b"][l], n("mix_out"),
                               mod_next=(sc_f, sh_f))

    hpre, hid = _hosted(_ff_hidden, hooks.get("ff1"), u2, W["w_ff1"][l], P["b_ff1"][l], n("ff1"))
    above = None if l + 1 == mod.shape[0] else (mod[l + 1:l + 2, D_MODEL:2 * D_MODEL], mod[l + 1:l + 2, 0:D_MODEL])
    ff, x2, *u_next = _hosted(_mm_resid_ln, hooks.get("ff2"), hid, W["w_ff2"][l], P["b_ff2"][l], x1, g_f,
                              P["ln_ff_g"][l], P["ln_ff_b"][l], n("ff2"), mod_next=above)
    saved = dict(x=x, u=u, zp=zp, qkv=qkv, zc=zc, zg=zg, p=p, feat_pool=feat_pool, probs=probs, o=o, cv=cv,
                 feat_conv=feat_conv, ys=ys, merged=merged, mix=mix, x1=x1, u2=u2, hpre=hpre, hid=hid, ff=ff,
                 u_next=u_next[0] if u_next else None)
    return x2, saved


def _layer_bwd(l, dx2, mod, W, P, A, hooks=None, tgt=None, nxt=None):
    hooks = hooks or {}
    sh_m, sc_m, g_m, sh_f, sc_f, g_f = [mod[l:l + 1, D_MODEL * j:D_MODEL * (j + 1)] for j in range(6)]
    n = lambda t: f"{t}{l}"
    gw, gs = {}, {}

    if isinstance(dx2, tuple):
        dres, dff, gs["ln_ff_g"], gs["ln_ff_b"], dg_f, gs["b_ff2"] = dx2
    else:
        dres, dff, gs["ln_ff_g"], gs["ln_ff_b"], dg_f, gs["b_ff2"], *loss_part = _resid_ln_bwd(
            dx2, A["x1"], A["ff"], g_f, P["ln_ff_g"][l], n("resid_ln_ff_bwd"), tgt=tgt)
    gw["w_ff2"] = _mm(A["hid"], dff, "tn", tm=512, tn=1024, out_dtype=BF16, name=n("dw_ff2"), split_n=512)
    hook = hooks["ff_hidden_bwd"](gw) if "ff_hidden_bwd" in hooks else None
    dhpre, gs["b_ff1"] = _hosted(_ff_hidden_bwd, hook, dff, W["w_ff2"][l], A["hpre"], n("ff_hidden_bwd"))
    gw["w_ff1"] = _mm(dhpre, A["u2"], "tn", tm=512, tn=1024, out_dtype=BF16, name=n("dw_ff1"), split_n=512)

    hook = hooks["du_ff"](gw) if "du_ff" in hooks else None
    dres, dmix, dsc_f, dsh_f, gs["ln_mix_g"], gs["ln_mix_b"], dg_m, _ = _hosted(
        _mm_ln_mod_bwd, hook, dhpre, W["w_ff1"][l], A["x1"], sc_f, dres, n("du_ff"),
        nxt=(A["x"], A["mix"], g_m, P["ln_mix_g"][l]))
    gw["w_o"] = _mm(A["merged"], dmix, "tn", tm=512, tn=1024, out_dtype=BF16, name=n("dw_o"), split_n=512)
    dy_pool, dy_attn, dy_conv, dzg, gs["b_gate"] = _merge_bwd(dmix, W["w_o"][l], A["zg"], P["b_gate"][l], A["ys"],
                                                              n("merge_bwd"))

    dys = (dy_pool, dy_attn, dy_conv)
    gw["w_br_pool"], gw["w_br_attn"], gw["w_br_conv"] = _branch_dw(
        dys, (A["feat_pool"], A["o"], A["feat_conv"]), n("dw_branch"))
    dfeat_pool, do, dfeat_conv = _branch_in_bwd(
        dys, (W["w_br_pool"][l], W["w_br_attn"][l], W["w_br_conv"][l]), (F32, BF16, F32), n("d_branch_in"))

    dzp, dwp_bd, gs["pool_scale"] = _pool_bwd(dfeat_pool, A["p"], P["wp_bd"][l], P["pool_scale"][l], n("pool_bwd"))
    gs["w_pool"] = jnp.stack([dwp_bd[POOL_GROUP * g:POOL_GROUP * (g + 1), POOL_GROUP * g:POOL_GROUP * (g + 1)]
                              for g in range(len(POOL_WINDOWS))])
    hook = hooks["attn"](gw) if "attn" in hooks else None
    dq, dk, dv, ds_acc = _hosted(_attn_bwd, hook, A["qkv"], do, A["probs"], n("attn_bwd"))
    gs["rel_bias"] = _bias_block_bwd(ds_acc, n("bias_block_bwd"))
    dzc, dcw, gs["conv_b"], gs["conv_ln_g"], gs["conv_ln_b"] = _conv_bwd(
        dfeat_conv, A["cv"], A["zc"], P["conv_w"][l], P["conv_ln_g"][l], P["conv_ln_b"][l], n("conv_bwd"))
    gs["conv_w"] = dcw[:CONV_WIDTH]

    dz = [dzp, dq, dk, dv, dzc, dzg]
    gw["w_in"] = _dw_segments(dz, A["u"], n("dw_in"))
    hook = hooks["du_mix"](gw) if "du_mix" in hooks else None
    res = _hosted(_mm_ln_mod_bwd, hook, dz, W["w_in"][l], A["x"], sc_m, dres, n("du_mix"), nxt=nxt)
    if nxt is None:
        dx, dsc_m, dsh_m = res
    else:
        dx, dsc_m, dsh_m = (res[0], res[1], *res[4:]), res[2], res[3]
    dmod = jnp.concatenate([dsh_m, dsc_m, dg_m, dsh_f, dsc_f, dg_f], axis=1)
    return (dx, gw, gs, dmod) if tgt is None else (dx, gw, gs, dmod, loss_part[0])


def _small_shapes():
    return {"b_ada": (6 * D_MODEL,), "b_gate": (3 * D_MODEL,), "w_pool": (4, POOL_GROUP, POOL_GROUP),
            "pool_scale": (D_POOL,), "rel_bias": (N_HEADS, N_REL), "conv_w": (CONV_WIDTH, D_CONV),
            "conv_b": (D_CONV,), "conv_ln_g": (D_CONV,), "conv_ln_b": (D_CONV,), "ln_mix_g": (D_MODEL,),
            "ln_mix_b": (D_MODEL,), "b_ff1": (D_FF,), "b_ff2": (D_MODEL,), "ln_ff_g": (D_MODEL,), "ln_ff_b": (D_MODEL,)}


def kernel(x, c, w_ada, b_ada, w_in, b_gate, w_pool, pool_scale, rel_bias, conv_w, conv_b, conv_ln_g, conv_ln_b, w_br_pool, w_br_attn, w_br_conv, w_o, ln_mix_g, ln_mix_b, w_ff1, b_ff1, w_ff2, b_ff2, ln_ff_g, ln_ff_b, loss_target, m_w_ada, m_b_ada, m_w_in, m_b_gate, m_w_pool, m_pool_scale, m_rel_bias, m_conv_w, m_conv_b, m_conv_ln_g, m_conv_ln_b, m_w_br_pool, m_w_br_attn, m_w_br_conv, m_w_o, m_ln_mix_g, m_ln_mix_b, m_w_ff1, m_b_ff1, m_w_ff2, m_b_ff2, m_ln_ff_g, m_ln_ff_b, v_w_ada, v_b_ada, v_w_in, v_b_gate, v_w_pool, v_pool_scale, v_rel_bias, v_conv_w, v_conv_b, v_conv_ln_g, v_conv_ln_b, v_w_br_pool, v_w_br_attn, v_w_br_conv, v_w_o, v_ln_mix_g, v_ln_mix_b, v_w_ff1, v_b_ff1, v_w_ff2, v_b_ff2, v_ln_ff_g, v_ln_ff_b):
    env = dict(locals())
    xi, yi, ci = _me()
    chip = 2 * xi + yi
    me = 4 * xi + 2 * yi + ci
    xs = x[0]
    tgt = loss_target[0]
    L = DEPTH

    first = _allgather_small(jnp.concatenate([c.reshape(8, 128), _pack([conv_w]).reshape(-1, 128)]), "gather_c_conv_w")
    c_all = first[:, :8].reshape(N_DEV, D_MODEL)
    ada_cols = w_ada.shape[2]
    b_ada_sh = lax.dynamic_slice_in_dim(b_ada, chip * ada_cols, ada_cols, axis=1).reshape(L, 1, ada_cols)
    mod_part = _mod_fwd(c_all, w_ada, b_ada_sh, "mod_fwd")

    W = {k: [None] * L for k in BIG}

    def weight_gather(*items):
        shards = [(jnp.swapaxes(env[k][l], 0, 1) if k in COL_SHARDED else env[k][l]).astype(BF16) for k, l in items]
        shards = [a.reshape(2, a.shape[0] // 2, a.shape[1]) for a in shards]

        def done(outs):
            for (k, l), g in zip(items, outs):
                W[k][l] = g.reshape(-1, g.shape[-1])

        return _gather_rider(shards), done

    branch = lambda l: [(k, l) for k in ("w_br_pool", "w_br_attn", "w_br_conv", "w_o")]
    rider, done = weight_gather(("w_in", 0))
    first_out = _run_rider(_join_riders(_allgather_rider(mod_part.reshape(-1, 128)), rider), "gather_mod_w_in0")
    done(first_out[1:])
    mod_g = first_out[0].reshape(N_CHIP, 2, L, N_DEV, ada_cols)[:, 0]
    mod_all = jnp.transpose(mod_g, (1, 2, 0, 3)).reshape(L, N_DEV, 6 * D_MODEL)
    mod = lax.dynamic_index_in_dim(mod_all, me, axis=1, keepdims=False)
    fwd_hooks = [{"z_gate": weight_gather(*branch(0)), "attn": weight_gather(("w_ff1", 0), ("w_ff2", 0)),
                  "ff1": weight_gather(*branch(1)), "ff2": weight_gather(("w_in", 1))},
                 {"attn": weight_gather(("w_ff1", 1), ("w_ff2", 1))}]

    P = {k: env[k] for k in ("rel_bias", "conv_w")}
    for k in ("b_gate", "pool_scale", "conv_b", "conv_ln_g", "conv_ln_b", "ln_mix_g", "ln_mix_b", "b_ff1", "b_ff2",
              "ln_ff_g", "ln_ff_b"):
        P[k] = env[k].reshape(L, 1, -1)
    n_cw = conv_w.size
    cw = first[:, 8:].reshape(N_CHIP, 2, -1)[:, 0, :n_cw].reshape(N_CHIP, L, CONV_WIDTH, D_CONV // N_CHIP)
    P["conv_w"] = jnp.transpose(cw, (1, 2, 0, 3)).reshape(L, CONV_WIDTH, D_CONV)
    wp_bd = jnp.zeros((L, D_POOL, D_POOL), F32)
    for g in range(len(POOL_WINDOWS)):
        sl = slice(POOL_GROUP * g, POOL_GROUP * (g + 1))
        wp_bd = wp_bd.at[:, sl, sl].set(w_pool[:, g])
    P["wp_bd"] = wp_bd.astype(BF16)

    acts = []
    h = xs
    for l in range(L):
        h, saved = _layer_fwd(l, h, mod, W, P, fwd_hooks[l], u=acts[-1]["u_next"] if acts else None)
        acts.append(saved)

    place = jnp.stack([ci, chip, chip ^ 1, chip ^ 2, chip ^ 3]).astype(jnp.int32)
    scattered = {}

    swapped = {}

    def swap_hook(names, l):
        def hook(gw):
            def done(outs):
                swapped.update({(k, l): o for k, o in zip(names, outs)})
            return _sibling_rider([gw[k] for k in names], other_half=True), done
        return hook

    def scatter_hook(names, l, host, then=None):
        def hook(gw):
            todo = [k for k in names if (k, l) not in swapped]
            if todo:
                got = _sibling_send([gw[k] for k in todo], f"swap_blocks_{host}{l}", other_half=True)
                swapped.update({(k, l): o for k, o in zip(todo, got)})
            sums = _sum_cores([gw[k] for k in names], [swapped[(k, l)] for k in names], place, f"sum_cores_{host}{l}")
            both = [hh.reshape(N_CHIP, -1, hh.shape[-1]) for hh in sums]
            rider = _scatter_rider(both)
            more = then(gw) if then is not None else None

            def done(outs):
                for k, hh, r in zip(names, both, outs):
                    scattered[(k, l)] = (hh, r)
                if more is not None:
                    more[1](outs[len(names):])

            return (rider if more is None else _join_riders(rider, more[0])), done
        return hook

    gws, gss, dmods = [None] * L, [None] * L, [None] * L
    dh = h
    for l in reversed(range(L)):
        hooks = {"ff_hidden_bwd": swap_hook(("w_ff2",), l),
                 "du_ff": scatter_hook(("w_ff2",), l, "du_ff", then=swap_hook(("w_ff1",), l)),
                 "attn": scatter_hook(("w_ff1", "w_o", "w_br_pool", "w_br_attn", "w_br_conv"), l, "attn_bwd"),
                 "du_mix": scatter_hook(("w_in",), l, "du_mix")}
        below = None
        if l > 0:
            below = (acts[l - 1]["x1"], acts[l - 1]["ff"], mod[l - 1:l, 5 * D_MODEL:], P["ln_ff_g"][l - 1])
        if l == L - 1:
            dh, gws[l], gss[l], dmods[l], loss_part = _layer_bwd(l, dh, mod, W, P, acts[l], hooks, tgt=tgt, nxt=below)
        else:
            dh, gws[l], gss[l], dmods[l] = _layer_bwd(l, dh, mod, W, P, acts[l], hooks, nxt=below)
    grad_x = dh[None]

    reduced = [[None] * L for _ in BIG]
    groups = (("w_in", "w_br_pool", "w_br_attn", "w_br_conv"), ("w_o", "w_ff1", "w_ff2"))
    for l in range(L):
        for gi, names in enumerate(groups):
            pairs = [scattered[(k, l)] for k in names]
            sums = _sum_chips([p[0] for p in pairs], [p[1] for p in pairs], place, f"sum_chips_{gi}_{l}")
            for k, t in zip(names, sums):
                reduced[BIG.index(k)][l] = t
    flat_reduced = [t for per_weight in reduced for t in per_weight]

    shapes = _small_shapes()
    small_names = [k for k in SMALL if k != "b_ada"]
    dmod_own = jnp.concatenate(dmods, axis=0)
    pack = _pack([dmod_own] + [jnp.stack([gss[l][k].reshape(shapes[k]) for l in range(L)]) for k in small_names]
                 + [loss_part])
    last = _run_rider(_join_riders(_sibling_rider(flat_reduced), _allgather_rider(pack.reshape(-1, 128))),
                      "swap_reduced_gather_small")
    flat_other, g_all = last[:-1], last[-1].reshape(N_DEV, -1, PACK_W)

    out = {}
    for j, k in enumerate(BIG):
        own, other = reduced[j], flat_other[L * j:L * (j + 1)]
        if k == "w_in":
            t = lambda a: jnp.swapaxes(a, 1, 2)
            res = _adamw_halves(t(env[k]), t(env["m_" + k]), t(env["v_" + k]), own, other, place, "cols", f"adamw_{k}")
            res = [t(a) for a in res]
        else:
            if k in COL_SHARDED:
                own, other = [a.T for a in own], [a.T for a in other]
            res = _adamw_halves(env[k], env["m_" + k], env["v_" + k], own, other, place,
                                "rows" if k in COL_SHARDED else "cols", f"adamw_{k}")
        out[k] = tuple(res)

    dmod_all = g_all[:, :L * 6].reshape(N_DEV, L, 6 * D_MODEL)
    dmod_sh = jnp.transpose(lax.dynamic_slice_in_dim(dmod_all, chip * ada_cols, ada_cols, axis=2), (1, 0, 2))
    g_ada = _mod_bwd(c_all, dmod_sh, "mod_bwd")
    g_, d_, m_, v_ = _adamw(w_ada.reshape(-1, ada_cols), m_w_ada.reshape(-1, ada_cols), v_w_ada.reshape(-1, ada_cols),
                            [g_ada.reshape(-1, ada_cols)], "adamw_w_ada")
    out["w_ada"] = tuple(a.reshape(w_ada.shape) for a in (g_, d_, m_, v_))

    def small_pack(prefix):
        parts = [env[prefix + "b_ada"]]
        for k in small_names:
            a = env[prefix + k]
            if k == "conv_w":
                a = jnp.zeros((L,) + shapes[k], F32)
            parts.append(a)
        return _pack(parts + [jnp.zeros_like(loss_part)])

    gp, dp, mp, vp = _adamw_small(small_pack(""), small_pack("m_"), small_pack("v_"), g_all, "adamw_small")
    full_shapes = [(L,) + shapes["b_ada"]] + [(L,) + shapes[k] for k in small_names]
    loss = _unpack(gp, full_shapes + [(128,)])[-1][0]
    for tag, packed in (("g", gp), ("d", dp), ("m", mp), ("v", vp)):
        for k, a in zip(["b_ada"] + small_names, _unpack(packed, full_shapes)):
            out.setdefault(k, {})
            out[k][tag] = a
    g_cw_full = out["conv_w"]["g"]
    cw_cols = D_CONV // N_CHIP
    g_cw = lax.dynamic_slice_in_dim(g_cw_full, chip * cw_cols, cw_cols, axis=2)
    pad_rows = lambda a: jnp.pad(a.reshape(L * CONV_WIDTH, cw_cols), ((0, 2), (0, 0)))
    g_, d_, m_, v_ = _adamw(pad_rows(conv_w), pad_rows(m_conv_w), pad_rows(v_conv_w), [pad_rows(g_cw)], "adamw_conv_w")
    out["conv_w"] = tuple(a[:L * CONV_WIDTH].reshape(L, CONV_WIDTH, cw_cols) for a in (g_, d_, m_, v_))

    names = ["w_ada", "b_ada", "w_in", "b_gate", "w_pool", "pool_scale", "rel_bias", "conv_w", "conv_b", "conv_ln_g",
             "conv_ln_b", "w_br_pool", "w_br_attn", "w_br_conv", "w_o", "ln_mix_g", "ln_mix_b", "w_ff1", "b_ff1",
             "w_ff2", "b_ff2", "ln_ff_g", "ln_ff_b"]

    def pick(k, i):
        o = out[k]
        return o[i] if isinstance(o, tuple) else o["gdmv"[i]].reshape(env[k].shape)

    return (loss, grad_x, *[pick(k, 0) for k in names], *[pick(k, 1) for k in names],
            *[pick(k, 2) for k in names], *[pick(k, 3) for k in names])
```

```python
import jax
import jax.numpy as jnp
import numpy as np
from jax import lax
from jax.experimental import pallas as pl
from jax.experimental.pallas import tpu as pltpu

F32 = jnp.float32
BF16 = jnp.bfloat16

D_MODEL = 1024
DEPTH = 2
CHUNK = 64
POOL_WINDOWS = (2, 4, 8, 16)
POOL_GROUP = 64
D_POOL = 256
N_HEADS = 8
HEAD_DIM = 64
D_ATTN = 512
N_PREV_CHUNKS = 8
REL_CLIP = 128
N_REL = 2 * REL_CLIP + 1
D_CONV = 256
CONV_WIDTH = 31
D_FF = 4 * D_MODEL
D_IN = 5376
OFF_POOL, OFF_QKV, OFF_CONV, OFF_GATE = 0, 256, 1792, 2304
ALPHA = (2.0 * DEPTH) ** 0.25
LN_EPS = 1e-5
NEG_INF = -1e30
ADAM_LR, ADAM_B1, ADAM_B2, ADAM_EPS, ADAM_WD, ADAM_STEP = 0.001, 0.9, 0.999, 1e-08, 0.01, 10

N_DEV = 8
N_CHIP = 4
MESH = pl.DeviceIdType.MESH

QB = 2 * CHUNK
KPAD = N_PREV_CHUNKS * CHUNK
KW = QB + KPAD
SKEW_W = 768

VMEM_LIMIT = 56 * 1024 * 1024


def _cparams(**kw):
    return pltpu.CompilerParams(vmem_limit_bytes=VMEM_LIMIT, **kw)


def _full(shape):
    n = len(shape)
    return pl.BlockSpec(shape, lambda *_: (0,) * n)


def _resident(shape):
    n = len(shape)
    return pl.BlockSpec(shape, lambda *_: (0,) * n, pipeline_mode=pl.Buffered(1))


_DIMS = {"nn": (((1,), (0,)), ((), ())), "nt": (((1,), (1,)), ((), ())), "tn": (((0,), (0,)), ((), ()))}


def _relu2(t):
    r = jnp.maximum(t, 0.0)
    return r * r


def _mm(a, b, mode, *, tm, tn, out_dtype, name, b_col0=0, n_out=None, bias=None, split_n=0, rider=None):
    if mode == "tn":
        k, m = a.shape
        n = b.shape[1] if n_out is None else n_out
        a_spec = pl.BlockSpec((k, tm), lambda i, j: (0, i))
        b_spec = pl.BlockSpec((k, tn), lambda i, j: (0, j + b_col0))
    elif mode == "nn":
        m, k = a.shape
        n = b.shape[1] if n_out is None else n_out
        a_spec = pl.BlockSpec((tm, k), lambda i, j: (i, 0))
        b_spec = pl.BlockSpec((k, tn), lambda i, j: (0, j + b_col0))
    else:
        m, k = a.shape
        n = b.shape[0] if n_out is None else n_out
        a_spec = pl.BlockSpec((tm, k), lambda i, j: (i, 0))
        b_spec = pl.BlockSpec((tn, k), lambda i, j: (j + b_col0, 0))
    assert m % tm == 0 and n % tn == 0, (name, m, n, tm, tn)
    dims = _DIMS[mode]

    def body(*refs):
        if bias is None:
            a_ref, b_ref, o_ref = refs
        else:
            a_ref, b_ref, bias_ref, o_ref = refs
        acc = lax.dot_general(a_ref[...].astype(BF16), b_ref[...].astype(BF16), dims, preferred_element_type=F32)
        if bias is not None:
            acc = acc + bias_ref[...]
        if split_n:
            for c in range(tn // split_n):
                o_ref[c] = acc[:, c * split_n:(c + 1) * split_n].astype(out_dtype)
        else:
            o_ref[...] = acc.astype(out_dtype)

    in_specs = [a_spec, b_spec]
    args = [a, b]
    if bias is not None:
        in_specs.append(pl.BlockSpec((1, tn), lambda i, j: (0, j)))
        args.append(bias)
    if split_n:
        out_spec = pl.BlockSpec((tn // split_n, tm, split_n), lambda i, j: (j, i, 0))
        out_shape = jax.ShapeDtypeStruct((n // split_n, m, split_n), out_dtype)
    else:
        out_spec = pl.BlockSpec((tm, tn), lambda i, j: (i, j))
        out_shape = jax.ShapeDtypeStruct((m, n), out_dtype)
    res = _call(body, name=name, grid=(m // tm, n // tn), in_specs=in_specs, out_specs=[out_spec],
                out_shape=[out_shape], scratch_shapes=[], args=args, rider=rider)
    return res[0] if rider is None else (res[0][0], res[1])


def _ln_hat(x):
    mu = jnp.mean(x, axis=-1, keepdims=True)
    xc = x - mu
    var = jnp.mean(xc * xc, axis=-1, keepdims=True)
    rstd = lax.rsqrt(var + LN_EPS)
    return xc * rstd, rstd


def _ln_hat_bwd(dhat, xhat, rstd):
    m1 = jnp.mean(dhat, axis=-1, keepdims=True)
    m2 = jnp.mean(dhat * xhat, axis=-1, keepdims=True)
    return rstd * (dhat - m1 - xhat * m2)


def _row_tile(s):
    return min(512, s)


def _acc_rows(ref, val, first):
    @pl.when(first)
    def _():
        ref[...] = jnp.zeros_like(ref)
    ref[...] += jnp.sum(val, axis=0, keepdims=True)


def _ln_mod(x, sc, sh, name):
    s, d = x.shape
    tm = _row_tile(s)

    def body(x_ref, sc_ref, sh_ref, u_ref):
        xhat, _ = _ln_hat(x_ref[...])
        u_ref[...] = (xhat * (1.0 + sc_ref[...]) + sh_ref[...]).astype(BF16)

    row = pl.BlockSpec((tm, d), lambda i: (i, 0))
    vec = pl.BlockSpec((1, d), lambda i: (0, 0))
    return pl.pallas_call(body, grid=(s // tm,), in_specs=[row, vec, vec], out_specs=row,
                          out_shape=jax.ShapeDtypeStruct((s, d), BF16), name=name, compiler_params=_cparams())(x, sc, sh)


def _resid_bwd_tile(dxo, x, f, g, gam):
    rhat, rstd = _ln_hat(ALPHA * x + g * f)
    dr = _ln_hat_bwd(dxo * gam, rhat, rstd)
    return ALPHA * dr, g * dr, dxo * rhat, dr * f


def _mm_ln_mod_bwd(a, b, x, sc, dres, name, rider=None, nxt=None):
    segs = list(a) if isinstance(a, (list, tuple)) else [a]
    s = segs[0].shape[0]
    k, d = b.shape
    assert sum(t.shape[1] for t in segs) == k
    tm = min(512 if k <= 4096 else 256, s)
    ns = len(segs)

    def body(*refs):
        seg_refs = refs[:ns]
        if nxt is None:
            b_ref, x_ref, sc_ref, dres_ref, dx_ref, dsc_ref, dsh_ref = refs[ns:]
        else:
            (b_ref, x_ref, sc_ref, dres_ref, xp_ref, fp_ref, gp_ref, gamp_ref,
             dresp_ref, dfp_ref, dsc_ref, dsh_ref, dgam_ref, dbet_ref, dg_ref, dbias_ref) = refs[ns:]
        first = pl.program_id(0) == 0
        duv, r0 = None, 0
        for seg_ref in seg_refs:
            w = seg_ref.shape[1]
            part = jnp.dot(seg_ref[...], b_ref[r0:r0 + w, :], preferred_element_type=F32)
            duv = part if duv is None else duv + part
            r0 += w
        xhat, rstd = _ln_hat(x_ref[...])
        dxv = dres_ref[...] + _ln_hat_bwd(duv * (1.0 + sc_ref[...]), xhat, rstd)
        _acc_rows(dsc_ref, duv * xhat, first)
        _acc_rows(dsh_ref, duv, first)
        if nxt is None:
            dx_ref[...] = dxv
        else:
            dresp, dfp, t_gam, t_g = _resid_bwd_tile(dxv, xp_ref[...], fp_ref[...], gp_ref[...], gamp_ref[...])
            dresp_ref[...] = dresp
            dfp_ref[...] = dfp.astype(BF16)
            _acc_rows(dgam_ref, t_gam, first)
            _acc_rows(dbet_ref, dxv, first)
            _acc_rows(dg_ref, t_g, first)
            _acc_rows(dbias_ref, dfp, first)

    row = pl.BlockSpec((tm, d), lambda i: (i, 0))
    vec = pl.BlockSpec((1, d), lambda i: (0, 0))
    vs = jax.ShapeDtypeStruct((1, d), F32)
    rows = jax.ShapeDtypeStruct((s, d), F32)
    in_specs = [pl.BlockSpec((tm, t.shape[1]), lambda i: (i, 0)) for t in segs] + [_resident((k, d)), row, vec, row]
    args = (*segs, b, x, sc, dres)
    if nxt is None:
        out_specs, out_shape = [row, vec, vec], [rows, vs, vs]
    else:
        in_specs += [row, row, vec, vec]
        args += tuple(nxt)
        out_specs = [row, row] + [vec] * 6
        out_shape = [rows, jax.ShapeDtypeStruct((s, d), BF16)] + [vs] * 6
    res = _call(body, name=name, grid=(s // tm,), in_specs=in_specs, out_specs=out_specs, out_shape=out_shape,
                scratch_shapes=[], args=args, rider=rider)
    return tuple(res) if rider is None else (tuple(res[0]), res[1])


def _dw_segments(segs, u, name):
    s, d = u.shape
    tw = 256
    tiles = [t.shape[1] // tw for t in segs]
    starts = [sum(tiles[:j]) for j in range(len(segs))]
    ns = len(segs)

    def body(*refs):
        seg_refs, u_ref, o_ref = refs[:ns], refs[ns], refs[ns + 1]
        i = pl.program_id(0)
        for seg_ref, t0, nt in zip(seg_refs, starts, tiles):
            @pl.when((i >= t0) & (i < t0 + nt))
            def _(seg_ref=seg_ref):
                acc = lax.dot_general(seg_ref[...], u_ref[...], _DIMS["tn"], preferred_element_type=F32)
                o_ref[0] = acc[:, :d // 2].astype(BF16)
                o_ref[1] = acc[:, d // 2:].astype(BF16)

    def seg_spec(t0, nt):
        return pl.BlockSpec((s, tw), lambda i: (0, jnp.clip(i - t0, 0, nt - 1)))

    return pl.pallas_call(
        body, grid=(sum(tiles),), in_specs=[seg_spec(t0, nt) for t0, nt in zip(starts, tiles)] + [_full((s, d))],
        out_specs=pl.BlockSpec((2, tw, d // 2), lambda i: (0, i, 0)),
        out_shape=jax.ShapeDtypeStruct((2, sum(tiles) * tw, d // 2), BF16), name=name, compiler_params=_cparams(),
    )(*segs, u)


def _mm_resid_ln(a, b, bias, x, g, gam, bet, name, rider=None, mod_next=None):
    s, k = a.shape
    d = b.shape[1]
    tm = min(512, s)
    nb, nm = int(bias is not None), 2 * int(mod_next is not None)

    def body(*refs):
        a_ref, b_ref = refs[:2]
        x_ref, g_ref, gam_ref, bet_ref = refs[2 + nb:6 + nb]
        f_ref, o_ref = refs[6 + nb + nm:8 + nb + nm]
        f = jnp.dot(a_ref[...], b_ref[...], preferred_element_type=F32)
        if bias is not None:
            f = f + refs[2][...]
        f_ref[...] = f
        rhat, _ = _ln_hat(ALPHA * x_ref[...] + g_ref[...] * f)
        y = rhat * gam_ref[...] + bet_ref[...]
        o_ref[...] = y
        if mod_next is not None:
            sc_ref, sh_ref = refs[6 + nb:8 + nb]
            yhat, _ = _ln_hat(y)
            refs[8 + nb + nm][...] = (yhat * (1.0 + sc_ref[...]) + sh_ref[...]).astype(BF16)

    row = pl.BlockSpec((tm, d), lambda i: (i, 0))
    vec = pl.BlockSpec((1, d), lambda i: (0, 0))
    in_specs = [pl.BlockSpec((tm, k), lambda i: (i, 0)), _resident((k, d))] + [vec] * nb + [row, vec, vec, vec] + [vec] * nm
    args = [a, b] + ([bias] if nb else []) + [x, g, gam, bet] + (list(mod_next) if nm else [])
    sh = jax.ShapeDtypeStruct((s, d), F32)
    out_specs, out_shape = [row, row], [sh, sh]
    if nm:
        out_specs, out_shape = out_specs + [row], out_shape + [jax.ShapeDtypeStruct((s, d), BF16)]
    res = _call(body, name=name, grid=(s // tm,), in_specs=in_specs, out_specs=out_specs, out_shape=out_shape,
                scratch_shapes=[], args=args, rider=rider)
    return tuple(res) if rider is None else (tuple(res[0]), res[1])


def _resid_ln_bwd(dxo, x, f, g, gam, name, tgt=None):
    s, d = x.shape
    tm = _row_tile(s)
    n = s // tm

    def body(*refs):
        if tgt is None:
            dxo_ref, x_ref, f_ref, g_ref, gam_ref, dres_ref, df_ref, dgam_ref, dbet_ref, dg_ref, dbias_ref = refs
            dxov = dxo_ref[...]
        else:
            (dxo_ref, t_ref, x_ref, f_ref, g_ref, gam_ref, dres_ref, df_ref, dgam_ref, dbet_ref, dg_ref, dbias_ref,
             loss_ref, sq_ref) = refs
            err = dxo_ref[...] - t_ref[...]
            dxov = err * (1.0 / d)
            _acc_rows(sq_ref, err * err, pl.program_id(0) == 0)

            @pl.when(pl.program_id(0) == n - 1)
            def _():
                tot = jnp.sum(sq_ref[...], axis=1, keepdims=True) * (0.5 / d)
                loss_ref[...] = jnp.broadcast_to(tot, (1, 128))

        first = pl.program_id(0) == 0
        dres, dfv, t_gam, t_g = _resid_bwd_tile(dxov, x_ref[...], f_ref[...], g_ref[...], gam_ref[...])
        dres_ref[...] = dres
        df_ref[...] = dfv.astype(BF16)
        _acc_rows(dgam_ref, t_gam, first)
        _acc_rows(dbet_ref, dxov, first)
        _acc_rows(dg_ref, t_g, first)
        _acc_rows(dbias_ref, dfv, first)

    row = pl.BlockSpec((tm, d), lambda i: (i, 0))
    vec = pl.BlockSpec((1, d), lambda i: (0, 0))
    vs = jax.ShapeDtypeStruct((1, d), F32)
    out_specs = [row, row, vec, vec, vec, vec]
    out_shape = [jax.ShapeDtypeStruct((s, d), F32), jax.ShapeDtypeStruct((s, d), BF16), vs, vs, vs, vs]
    if tgt is None:
        return pl.pallas_call(body, grid=(n,), in_specs=[row, row, row, vec, vec], out_specs=out_specs,
                              out_shape=out_shape, name=name, compiler_params=_cparams())(dxo, x, f, g, gam)
    return pl.pallas_call(body, grid=(n,), in_specs=[row, row, row, row, vec, vec],
                          out_specs=out_specs + [pl.BlockSpec((1, 128), lambda i: (0, 0))],
                          out_shape=out_shape + [jax.ShapeDtypeStruct((1, 128), F32)],
                          scratch_shapes=[pltpu.VMEM((1, d), F32)], name=name,
                          compiler_params=_cparams())(dxo, tgt, x, f, g, gam)


POOL_HALO = 16
POOL_ROWS = 256


def _pool_counts(r0, rows):
    t1 = (lax.broadcasted_iota(jnp.int32, (rows, 128), 0) + r0 + 1).astype(F32)
    low = lax.broadcasted_iota(jnp.int32, (rows, 128), 1) < POOL_GROUP
    wa = jnp.where(low, float(POOL_WINDOWS[0]), float(POOL_WINDOWS[1]))
    wb = jnp.where(low, float(POOL_WINDOWS[2]), float(POOL_WINDOWS[3]))
    return jnp.minimum(t1, wa), jnp.minimum(t1, wb), low


def _window_sums(win, off, rows, sign):
    def sl(j, half):
        return win[off + sign * j: off + sign * j + rows, 128 * half:128 * half + 128]
    a2 = sl(0, 0) + sl(1, 0)
    a4 = a2 + sl(2, 0) + sl(3, 0)
    a8 = sl(0, 1)
    for j in range(1, 8):
        a8 = a8 + sl(j, 1)
    a16 = a8
    for j in range(8, 16):
        a16 = a16 + sl(j, 1)
    return a2, a4, a8, a16


def _pool_fwd(zp, wp_bd, pscale, name):
    s = zp.shape[0]
    r = min(POOL_ROWS, s)

    def body(z_ref, wp_ref, sc_ref, p_ref, feat_ref, pad):
        pad[0:POOL_HALO, :] = jnp.zeros((POOL_HALO, D_POOL), F32)
        pad[POOL_HALO:, :] = z_ref[...]

        def step(i, carry):
            r0 = pl.multiple_of(i * r, r)
            win = pad[pl.ds(r0, r + POOL_HALO), :]
            a2, a4, a8, a16 = _window_sums(win, POOL_HALO, r, -1)
            ca, cb, low = _pool_counts(r0, r)
            x0 = win[POOL_HALO:, :]
            pa = jnp.where(low, a2, a4) / ca
            pb = jnp.where(low, a8, a16) / cb
            p = (jnp.concatenate([pa, pb], axis=1) - x0).astype(BF16)
            p_ref[pl.ds(r0, r), :] = p
            pw = jnp.dot(p, wp_ref[...], preferred_element_type=F32)
            feat_ref[pl.ds(r0, r), :] = (pw * sc_ref[...]).astype(BF16)
            return carry

        lax.fori_loop(0, s // r, step, 0)

    return pl.pallas_call(
        body, out_shape=[jax.ShapeDtypeStruct((s, D_POOL), BF16), jax.ShapeDtypeStruct((s, D_POOL), BF16)],
        scratch_shapes=[pltpu.VMEM((s + POOL_HALO, D_POOL), F32)], name=name, compiler_params=_cparams(),
    )(zp, wp_bd, pscale)


def _pool_bwd(dfeat, p, wp_bd, pscale, name):
    s = p.shape[0]
    r = min(POOL_ROWS, s)

    def body(df_ref, p_ref, wp_ref, sc_ref, dz_ref, dwp_ref, dsc_ref, gpad, dpbuf):
        dwp_ref[...] = jnp.zeros_like(dwp_ref)
        dsc_ref[...] = jnp.zeros_like(dsc_ref)
        gpad[s:, :] = jnp.zeros((POOL_HALO, D_POOL), F32)

        def step1(i, carry):
            r0 = pl.multiple_of(i * r, r)
            pv = p_ref[pl.ds(r0, r), :]
            dfv = df_ref[pl.ds(r0, r), :]
            pw = jnp.dot(pv, wp_ref[...], preferred_element_type=F32)
            dsc_ref[...] += jnp.sum(dfv * pw, axis=0, keepdims=True)
            dpw = (dfv * sc_ref[...]).astype(BF16)
            dwp_ref[...] += lax.dot_general(pv, dpw, _DIMS["tn"], preferred_element_type=F32)
            dp = lax.dot_general(dpw, wp_ref[...], _DIMS["nt"], preferred_element_type=F32)
            ca, cb, _ = _pool_counts(r0, r)
            gpad[pl.ds(r0, r), :] = dp / jnp.concatenate([ca, cb], axis=1)
            dpbuf[pl.ds(r0, r), :] = dp
            return carry

        lax.fori_loop(0, s // r, step1, 0)

        def step2(i, carry):
            r0 = pl.multiple_of(i * r, r)
            win = gpad[pl.ds(r0, r + POOL_HALO), :]
            a2, a4, a8, a16 = _window_sums(win, 0, r, 1)
            low = lax.broadcasted_iota(jnp.int32, (r, 128), 1) < POOL_GROUP
            acc = jnp.concatenate([jnp.where(low, a2, a4), jnp.where(low, a8, a16)], axis=1)
            dz_ref[pl.ds(r0, r), :] = (acc - dpbuf[pl.ds(r0, r), :]).astype(BF16)
            return carry

        lax.fori_loop(0, s // r, step2, 0)

    return pl.pallas_call(
        body,
        out_shape=[jax.ShapeDtypeStruct((s, D_POOL), BF16), jax.ShapeDtypeStruct((D_POOL, D_POOL), F32),
                   jax.ShapeDtypeStruct((1, D_POOL), F32)],
        scratch_shapes=[pltpu.VMEM((s + POOL_HALO, D_POOL), F32), pltpu.VMEM((s, D_POOL), F32)],
        name=name, compiler_params=_cparams(),
    )(dfeat, p, wp_bd, pscale)


def _skew_index():
    cp = lax.broadcasted_iota(jnp.int32, (SKEW_W, N_REL), 0)
    dist = jnp.where(cp < KW, KPAD - cp, KPAD + SKEW_W - cp)
    idx = jnp.clip(dist, -REL_CLIP, REL_CLIP) + REL_CLIP
    return (idx == lax.broadcasted_iota(jnp.int32, (SKEW_W, N_REL), 1)).astype(F32)


def _row_bits(b):
    return (lax.broadcasted_iota(jnp.int32, (QB, SKEW_W), 0) >> b) & 1 == 1


N_EDGE = KPAD // QB


def _bias_block(rel_bias, name):
    def body(rb_ref, o_ref):
        onehot = _skew_index()
        row0 = lax.dot_general(rb_ref[...], onehot, _DIMS["nt"], precision=lax.Precision.HIGHEST,
                               preferred_element_type=F32)
        r = lax.broadcasted_iota(jnp.int32, (QB, KW), 0)
        kk = lax.broadcasted_iota(jnp.int32, (QB, KW), 1)
        cq, ck = r // CHUNK, kk // CHUNK
        band = (ck >= cq) & (ck <= cq + N_PREV_CHUNKS)
        for h in range(N_HEADS):
            t = jnp.broadcast_to(row0[h:h + 1, :], (QB, SKEW_W))
            for b in range(7):
                t = jnp.where(_row_bits(b), pltpu.roll(t, 1 << b, 1), t)
            for e in range(N_EDGE + 1):
                o_ref[e, h] = jnp.where(band & (kk >= KPAD - e * QB), t[:, :KW], NEG_INF)

    return pl.pallas_call(body, out_shape=jax.ShapeDtypeStruct((N_EDGE + 1, N_HEADS, QB, KW), F32), name=name,
                          compiler_params=_cparams())(rel_bias)


def _bias_spec():
    return pl.BlockSpec((None, N_HEADS, QB, KW), lambda i: (jnp.minimum(i, N_EDGE), 0, 0, 0))


def _bias_block_bwd(ds_acc, name):
    def body(ds_ref, o_ref):
        sums = []
        for h in range(N_HEADS):
            t = jnp.concatenate([ds_ref[h], jnp.zeros((QB, SKEW_W - KW), F32)], axis=1)
            for b in range(7):
                t = jnp.where(_row_bits(b), pltpu.roll(t, SKEW_W - (1 << b), 1), t)
            sums.append(jnp.sum(t, axis=0, keepdims=True))
        allh = jnp.concatenate(sums, axis=0)
        o_ref[...] = jnp.dot(allh, _skew_index(), precision=lax.Precision.HIGHEST, preferred_element_type=F32)

    return pl.pallas_call(body, out_shape=jax.ShapeDtypeStruct((N_HEADS, N_REL), F32), name=name,
                          compiler_params=_cparams())(ds_acc)


def _scaled(q):
    return (q.astype(F32) * (HEAD_DIM ** -0.5)).astype(BF16)


def _probs(q, kw, bias_ref):
    sc = jnp.stack([lax.dot_general(q[:, HEAD_DIM * h:HEAD_DIM * (h + 1)], kw[:, HEAD_DIM * h:HEAD_DIM * (h + 1)],
                                    _DIMS["nt"], preferred_element_type=F32) + bias_ref[h] for h in range(N_HEADS)])
    e = jnp.exp(sc - jnp.max(sc, axis=-1, keepdims=True))
    return e * (1.0 / jnp.sum(e, axis=-1, keepdims=True))


def _load_padded_kv(qkv_hbm, kpad, vpad, sems, s):
    kpad[0:KPAD, :] = jnp.zeros((KPAD, D_ATTN), BF16)
    vpad[0:KPAD, :] = jnp.zeros((KPAD, D_ATTN), BF16)
    ck = pltpu.make_async_copy(qkv_hbm.at[:, D_ATTN:2 * D_ATTN], kpad.at[pl.ds(KPAD, s), :], sems.at[0])
    cv = pltpu.make_async_copy(qkv_hbm.at[:, 2 * D_ATTN:3 * D_ATTN], vpad.at[pl.ds(KPAD, s), :], sems.at[1])
    ck.start()
    cv.start()
    ck.wait()
    cv.wait()


def _attn_fwd(qkv, bias, name, rider=None):
    s = qkv.shape[0]

    def body(q_ref, qkv_hbm, bias_ref, o_ref, p_ref, kpad, vpad, sems):
        i = pl.program_id(0)

        @pl.when(i == 0)
        def _():
            _load_padded_kv(qkv_hbm, kpad, vpad, sems, s)

        base = pl.multiple_of(i * QB, QB)
        kw = kpad[pl.ds(base, KW), :]
        vw = vpad[pl.ds(base, KW), :]
        q = _scaled(q_ref[...])
        p = _probs(q, kw, bias_ref).astype(BF16)
        p_ref[...] = p
        outs = [jnp.dot(p[h], vw[:, HEAD_DIM * h:HEAD_DIM * (h + 1)], preferred_element_type=F32)
                for h in range(N_HEADS)]
        o_ref[...] = jnp.concatenate(outs, axis=1).astype(BF16)

    res = _call(
        body, name=name, grid=(s // QB,),
        in_specs=[pl.BlockSpec((QB, D_ATTN), lambda i: (i, 0)), pl.BlockSpec(memory_space=pl.ANY),
                  _bias_spec()],
        out_specs=[pl.BlockSpec((QB, D_ATTN), lambda i: (i, 0)), _probs_spec()],
        out_shape=[jax.ShapeDtypeStruct((s, D_ATTN), BF16), jax.ShapeDtypeStruct((N_HEADS, s, KW), BF16)],
        scratch_shapes=[pltpu.VMEM((s + KPAD, D_ATTN), BF16), pltpu.VMEM((s + KPAD, D_ATTN), BF16),
                        pltpu.SemaphoreType.DMA((2,))],
        args=(qkv, qkv, bias), rider=rider)
    return tuple(res) if rider is None else (tuple(res[0]), res[1])


def _probs_spec():
    return pl.BlockSpec((N_HEADS, QB, KW), lambda i: (0, i, 0))


def _attn_bwd(qkv, do, probs, name, rider=None):
    s = qkv.shape[0]
    n = s // QB

    def body(q_ref, qkv_hbm, do_ref, p_ref, dq_ref, dk_hbm, dv_hbm, ds_ref, kpad, vpad, dkacc, dvacc, sems):
        i = pl.program_id(0)

        @pl.when(i == 0)
        def _():
            _load_padded_kv(qkv_hbm, kpad, vpad, sems, s)
            dkacc[...] = jnp.zeros_like(dkacc)
            dvacc[...] = jnp.zeros_like(dvacc)
            ds_ref[...] = jnp.zeros_like(ds_ref)

        base = pl.multiple_of(i * QB, QB)
        kw = kpad[pl.ds(base, KW), :]
        vw = vpad[pl.ds(base, KW), :]
        q = _scaled(q_ref[...])
        dov = do_ref[...]
        heads = [slice(HEAD_DIM * h, HEAD_DIM * (h + 1)) for h in range(N_HEADS)]
        pb = p_ref[...]
        p = pb.astype(F32)
        dp = jnp.stack([lax.dot_general(dov[:, hs], vw[:, hs], _DIMS["nt"], preferred_element_type=F32) for hs in heads])
        ds = p * (dp - jnp.sum(dp * p, axis=-1, keepdims=True))
        ds_ref[...] += ds
        dsb = ds.astype(BF16)
        dvs = [lax.dot_general(pb[h], dov[:, hs], _DIMS["tn"], preferred_element_type=F32) for h, hs in enumerate(heads)]
        dqs = [jnp.dot(dsb[h], kw[:, hs], preferred_element_type=F32) for h, hs in enumerate(heads)]
        dks = [lax.dot_general(dsb[h], q[:, hs], _DIMS["tn"], preferred_element_type=F32) for h, hs in enumerate(heads)]
        dq_ref[...] = (jnp.concatenate(dqs, axis=1) * (HEAD_DIM ** -0.5)).astype(BF16)
        dkacc[pl.ds(base, KW), :] += jnp.concatenate(dks, axis=1)
        dvacc[pl.ds(base, KW), :] += jnp.concatenate(dvs, axis=1)

        @pl.when(i == n - 1)
        def _():
            def cast(j, carry):
                rows = pl.ds(pl.multiple_of(KPAD + j * 512, 512), 512)
                kpad[rows, :] = dkacc[rows, :].astype(BF16)
                vpad[rows, :] = dvacc[rows, :].astype(BF16)
                return carry

            lax.fori_loop(0, s // 512, cast, 0)
            ck = pltpu.make_async_copy(kpad.at[pl.ds(KPAD, s), :], dk_hbm, sems.at[0])
            cv = pltpu.make_async_copy(vpad.at[pl.ds(KPAD, s), :], dv_hbm, sems.at[1])
            ck.start()
            cv.start()
            ck.wait()
            cv.wait()

    blk = pl.BlockSpec((QB, D_ATTN), lambda i: (i, 0))
    acc_shape = jax.ShapeDtypeStruct((s, D_ATTN), BF16)
    return _call(
        body, name=name, grid=(n,),
        in_specs=[blk, pl.BlockSpec(memory_space=pl.ANY), blk, _probs_spec()],
        out_specs=[blk, pl.BlockSpec(memory_space=pl.ANY), pl.BlockSpec(memory_space=pl.ANY), _full((N_HEADS, QB, KW))],
        out_shape=[jax.ShapeDtypeStruct((s, D_ATTN), BF16), acc_shape, acc_shape,
                   jax.ShapeDtypeStruct((N_HEADS, QB, KW), F32)],
        scratch_shapes=[pltpu.VMEM((s + KPAD, D_ATTN), BF16), pltpu.VMEM((s + KPAD, D_ATTN), BF16),
                        pltpu.VMEM((s + KPAD, D_ATTN), F32), pltpu.VMEM((s + KPAD, D_ATTN), F32),
                        pltpu.SemaphoreType.DMA((2,))],
        args=(qkv, qkv, do, probs), rider=rider)


CONV_HALO = 32
CONV_ROWS = 64


def _sigmoid(t):
    return 1.0 / (1.0 + jnp.exp(-t))


CONV_WIN = CONV_ROWS + CONV_HALO - 8


def _row_windows(ref, r0, buf):
    win = ref[pl.ds(r0, CONV_ROWS + CONV_HALO), :]
    for j in range(1, 8):
        buf[j - 1] = win[j:j + CONV_WIN, :]

    def get(o):
        j, a = o % 8, o - o % 8
        if j == 0:
            return ref[pl.ds(r0 + a, CONV_ROWS), :]
        return buf[j - 1, a:a + CONV_ROWS, :]

    return get


def _glu_rows(z_ref, r0, rows):
    a = z_ref[pl.ds(r0, rows), 0:D_CONV]
    b = z_ref[pl.ds(r0, rows), D_CONV:2 * D_CONV]
    return a, _sigmoid(b)


def _conv_fwd(zc, conv_w, conv_b, ln_g, ln_b, name):
    s = zc.shape[0]
    rt = min(256, s)

    def body(z_ref, w_ref, cb_ref, g_ref, b_ref, cv_ref, feat_ref, hpad, shifts):
        hpad[0:CONV_HALO, :] = jnp.zeros((CONV_HALO, D_CONV), F32)

        def glu(i, carry):
            r0 = pl.multiple_of(i * rt, rt)
            a, sb = _glu_rows(z_ref, r0, rt)
            hpad[pl.ds(r0 + CONV_HALO, rt), :] = a * sb
            return carry

        lax.fori_loop(0, s // rt, glu, 0)
        w = w_ref[...]

        def conv(i, carry):
            r0 = pl.multiple_of(i * CONV_ROWS, CONV_ROWS)
            win = _row_windows(hpad, r0, shifts)
            acc = jnp.broadcast_to(cb_ref[...], (CONV_ROWS, D_CONV))
            for k in range(CONV_WIDTH):
                acc = acc + win(2 + k) * w[k:k + 1, :]
            cv_ref[pl.ds(r0, CONV_ROWS), :] = acc
            yhat, _ = _ln_hat(acc)
            y = yhat * g_ref[...] + b_ref[...]
            feat_ref[pl.ds(r0, CONV_ROWS), :] = (y * _sigmoid(y)).astype(BF16)
            return carry

        lax.fori_loop(0, s // CONV_ROWS, conv, 0)

    return pl.pallas_call(
        body, out_shape=[jax.ShapeDtypeStruct((s, D_CONV), F32), jax.ShapeDtypeStruct((s, D_CONV), BF16)],
        scratch_shapes=[pltpu.VMEM((s + CONV_HALO, D_CONV), F32), pltpu.VMEM((7, CONV_WIN, D_CONV), F32)],
        name=name, compiler_params=_cparams(),
    )(zc, conv_w, conv_b, ln_g, ln_b)


def _conv_bwd(dfeat, cv, zc, conv_w, ln_g, ln_b, name):
    s = zc.shape[0]
    rt = min(256, s)

    def body(df_ref, cv_ref, z_ref, w_ref, g_ref, b_ref, dz_ref, dw_ref, dcb_ref, dg_ref, db_ref, hpad, dcvpad, dwacc,
             hshifts, dshifts):
        hpad[0:CONV_HALO, :] = jnp.zeros((CONV_HALO, D_CONV), F32)
        dcvpad[s:, :] = jnp.zeros((CONV_HALO, D_CONV), F32)
        dwacc[...] = jnp.zeros_like(dwacc)
        dcb_ref[...] = jnp.zeros_like(dcb_ref)
        dg_ref[...] = jnp.zeros_like(dg_ref)
        db_ref[...] = jnp.zeros_like(db_ref)

        def pass1(i, carry):
            r0 = pl.multiple_of(i * rt, rt)
            a, sb = _glu_rows(z_ref, r0, rt)
            hpad[pl.ds(r0 + CONV_HALO, rt), :] = a * sb
            cvhat, rstd = _ln_hat(cv_ref[pl.ds(r0, rt), :])
            y = cvhat * g_ref[...] + b_ref[...]
            sg = _sigmoid(y)
            dy = df_ref[pl.ds(r0, rt), :] * (sg * (1.0 + y * (1.0 - sg)))
            dg_ref[...] += jnp.sum(dy * cvhat, axis=0, keepdims=True)
            db_ref[...] += jnp.sum(dy, axis=0, keepdims=True)
            dcv = _ln_hat_bwd(dy * g_ref[...], cvhat, rstd)
            dcb_ref[...] += jnp.sum(dcv, axis=0, keepdims=True)
            dcvpad[pl.ds(r0, rt), :] = dcv
            return carry

        lax.fori_loop(0, s // rt, pass1, 0)
        w = w_ref[...]

        def pass2(i, carry):
            r0 = pl.multiple_of(i * CONV_ROWS, CONV_ROWS)
            dwin = _row_windows(dcvpad, r0, dshifts)
            hwin = _row_windows(hpad, r0, hshifts)
            dcv = dwin(0)
            dh = jnp.zeros((CONV_ROWS, D_CONV), F32)
            for k in range(CONV_WIDTH):
                dh = dh + dwin(30 - k) * w[k:k + 1, :]
                prod = dcv * hwin(2 + k)
                dwacc[8 * k:8 * k + 8, :] += jnp.sum(prod.reshape(CONV_ROWS // 8, 8, D_CONV), axis=0)
            a, sb = _glu_rows(z_ref, r0, CONV_ROWS)
            dz_ref[pl.ds(r0, CONV_ROWS), :] = jnp.concatenate([dh * sb, dh * a * sb * (1.0 - sb)], axis=1).astype(BF16)
            return carry

        lax.fori_loop(0, s // CONV_ROWS, pass2, 0)
        dw_ref[...] = jnp.sum(dwacc[...].reshape(32, 8, D_CONV), axis=1)

    vs = jax.ShapeDtypeStruct((1, D_CONV), F32)
    return pl.pallas_call(
        body,
        out_shape=[jax.ShapeDtypeStruct((s, 2 * D_CONV), BF16), jax.ShapeDtypeStruct((32, D_CONV), F32), vs, vs, vs],
        scratch_shapes=[pltpu.VMEM((s + CONV_HALO, D_CONV), F32), pltpu.VMEM((s + CONV_HALO, D_CONV), F32),
                        pltpu.VMEM((256, D_CONV), F32), pltpu.VMEM((7, CONV_WIN, D_CONV), F32),
                        pltpu.VMEM((7, CONV_WIN, D_CONV), F32)],
        name=name, compiler_params=_cparams(),
    )(dfeat, cv, zc, conv_w, ln_g, ln_b)


def _branch_out(feats, wts, name):
    s = feats[0].shape[0]
    tm = min(1024, s)

    def body(*refs):
        for f_ref, w_ref, o_ref in zip(refs[:3], refs[3:6], refs[6:]):
            o_ref[...] = lax.dot_general(f_ref[...], w_ref[...], _DIMS["nt"], preferred_element_type=F32).astype(BF16)

    row = pl.BlockSpec((tm, D_MODEL), lambda i: (i, 0))
    sh = jax.ShapeDtypeStruct((s, D_MODEL), BF16)
    return pl.pallas_call(
        body, grid=(s // tm,),
        in_specs=[pl.BlockSpec((tm, f.shape[1]), lambda i: (i, 0)) for f in feats] + [_full(w.shape) for w in wts],
        out_specs=[row] * 3, out_shape=[sh] * 3, name=name, compiler_params=_cparams(),
    )(*feats, *wts)


def _branch_in_bwd(dys, wts, out_dtypes, name):
    s = dys[0].shape[0]
    tm = min(1024, s)

    def body(*refs):
        for d_ref, w_ref, o_ref in zip(refs[:3], refs[3:6], refs[6:]):
            o_ref[...] = jnp.dot(d_ref[...], w_ref[...], preferred_element_type=F32).astype(o_ref.dtype)

    row = pl.BlockSpec((tm, D_MODEL), lambda i: (i, 0))
    return pl.pallas_call(
        body, grid=(s // tm,), in_specs=[row] * 3 + [_full(w.shape) for w in wts],
        out_specs=[pl.BlockSpec((tm, w.shape[1]), lambda i: (i, 0)) for w in wts],
        out_shape=[jax.ShapeDtypeStruct((s, w.shape[1]), dt) for w, dt in zip(wts, out_dtypes)],
        name=name, compiler_params=_cparams(),
    )(*dys, *wts)


def _branch_dw(dys, feats, name):
    s = dys[0].shape[0]
    tm = 512

    def body(*refs):
        for d_ref, f_ref, o_ref in zip(refs[:3], refs[3:6], refs[6:]):
            acc = lax.dot_general(d_ref[...], f_ref[...], _DIMS["tn"], preferred_element_type=F32)
            half = acc.shape[1] // 2
            o_ref[0] = acc[:, :half].astype(BF16)
            o_ref[1] = acc[:, half:].astype(BF16)

    return pl.pallas_call(
        body, grid=(D_MODEL // tm,),
        in_specs=[pl.BlockSpec((s, tm), lambda i: (0, i))] * 3 + [_full(f.shape) for f in feats],
        out_specs=[pl.BlockSpec((2, tm, f.shape[1] // 2), lambda i: (0, i, 0)) for f in feats],
        out_shape=[jax.ShapeDtypeStruct((2, D_MODEL, f.shape[1] // 2), BF16) for f in feats],
        name=name, compiler_params=_cparams(),
    )(*dys, *feats)


def _merge(zg, b_gate, ys, name):
    s = zg.shape[0]
    tm = _row_tile(s)

    def body(zg_ref, bg_ref, y0_ref, y1_ref, y2_ref, o_ref):
        acc = None
        for j, y_ref in enumerate((y0_ref, y1_ref, y2_ref)):
            cs = slice(D_MODEL * j, D_MODEL * (j + 1))
            t = _sigmoid(zg_ref[:, cs] + bg_ref[:, cs]) * y_ref[...]
            acc = t if acc is None else acc + t
        o_ref[...] = acc.astype(BF16)

    row = pl.BlockSpec((tm, D_MODEL), lambda i: (i, 0))
    return pl.pallas_call(
        body, grid=(s // tm,),
        in_specs=[pl.BlockSpec((tm, 3 * D_MODEL), lambda i: (i, 0)), _full((1, 3 * D_MODEL)), row, row, row],
        out_specs=row, out_shape=jax.ShapeDtypeStruct((s, D_MODEL), BF16), name=name, compiler_params=_cparams(),
    )(zg, b_gate, *ys)


def _merge_bwd(dmix, w_o, zg, b_gate, ys, name):
    s = zg.shape[0]
    tm = min(256, s)

    def body(dmix_ref, wo_ref, zg_ref, bg_ref, y0_ref, y1_ref, y2_ref, d0_ref, d1_ref, d2_ref, dzg_ref, dbg_ref):
        first = pl.program_id(0) == 0

        @pl.when(first)
        def _():
            dbg_ref[...] = jnp.zeros_like(dbg_ref)

        dmv = lax.dot_general(dmix_ref[...], wo_ref[...], _DIMS["nt"], preferred_element_type=F32)
        for j, (y_ref, d_ref) in enumerate(((y0_ref, d0_ref), (y1_ref, d1_ref), (y2_ref, d2_ref))):
            cs = slice(D_MODEL * j, D_MODEL * (j + 1))
            g = _sigmoid(zg_ref[:, cs] + bg_ref[:, cs])
            d_ref[...] = (dmv * g).astype(BF16)
            dzg = dmv * y_ref[...] * g * (1.0 - g)
            dzg_ref[:, cs] = dzg.astype(BF16)
            dbg_ref[:, cs] += jnp.sum(dzg, axis=0, keepdims=True)

    row = pl.BlockSpec((tm, D_MODEL), lambda i: (i, 0))
    wide = pl.BlockSpec((tm, 3 * D_MODEL), lambda i: (i, 0))
    yb = jax.ShapeDtypeStruct((s, D_MODEL), BF16)
    return pl.pallas_call(
        body, grid=(s // tm,),
        in_specs=[row, _full(w_o.shape), wide, _full((1, 3 * D_MODEL)), row, row, row],
        out_specs=[row, row, row, wide, _full((1, 3 * D_MODEL))],
        out_shape=[yb, yb, yb, jax.ShapeDtypeStruct((s, 3 * D_MODEL), BF16), jax.ShapeDtypeStruct((1, 3 * D_MODEL), F32)],
        name=name, compiler_params=_cparams(),
    )(dmix, w_o, zg, b_gate, *ys)


def _ff_hidden(u2, w_ff1t, b_ff1, name, rider=None):
    s = u2.shape[0]
    tm, tn = min(2048, s), 1024

    def body(a_ref, b_ref, bias_ref, pre_ref, h_ref):
        acc = lax.dot_general(a_ref[...], b_ref[...], _DIMS["nt"], preferred_element_type=F32) + bias_ref[...]
        pre_ref[...] = acc.astype(BF16)
        h_ref[...] = _relu2(acc).astype(BF16)

    blk = pl.BlockSpec((tm, tn), lambda i, j: (i, j))
    sh = jax.ShapeDtypeStruct((s, D_FF), BF16)
    res = _call(body, name=name, grid=(s // tm, D_FF // tn),
                in_specs=[pl.BlockSpec((tm, D_MODEL), lambda i, j: (i, 0)), pl.BlockSpec((tn, D_MODEL), lambda i, j: (j, 0)),
                          pl.BlockSpec((1, tn), lambda i, j: (0, j))],
                out_specs=[blk, blk], out_shape=[sh, sh], scratch_shapes=[], args=(u2, w_ff1t, b_ff1), rider=rider)
    return tuple(res) if rider is None else (tuple(res[0]), res[1])


def _ff_hidden_bwd(dff, w_ff2, hpre, name, rider=None):
    s = dff.shape[0]
    tm, tn = min(1024, s), 1024

    def body(a_ref, b_ref, h_ref, o_ref, sum_ref):
        dh = lax.dot_general(a_ref[...], b_ref[...], _DIMS["nt"], preferred_element_type=F32)
        dpre = dh * (2.0 * jnp.maximum(h_ref[...].astype(F32), 0.0))
        o_ref[...] = dpre.astype(BF16)
        _acc_rows(sum_ref, dpre, pl.program_id(1) == 0)

    res = _call(
        body, name=name, grid=(D_FF // tn, s // tm),
        in_specs=[pl.BlockSpec((tm, D_MODEL), lambda j, i: (i, 0)), pl.BlockSpec((tn, D_MODEL), lambda j, i: (j, 0)),
                  pl.BlockSpec((tm, tn), lambda j, i: (i, j))],
        out_specs=[pl.BlockSpec((tm, tn), lambda j, i: (i, j)), pl.BlockSpec((1, tn), lambda j, i: (0, j))],
        out_shape=[jax.ShapeDtypeStruct((s, D_FF), BF16), jax.ShapeDtypeStruct((1, D_FF), F32)],
        scratch_shapes=[], args=(dff, w_ff2, hpre), rider=rider)
    return tuple(res) if rider is None else (tuple(res[0]), res[1])


def _silu(t):
    return t * _sigmoid(t)


def _mod_fwd(c_all, w_ada_sh, b_ada_sh, name):
    cols = w_ada_sh.shape[2]

    def body(c_ref, w_ref, b_ref, o_ref):
        ca = _silu(c_ref[...]).astype(BF16)
        o_ref[0] = jnp.dot(ca, w_ref[0].astype(BF16), preferred_element_type=F32) + b_ref[0]

    return pl.pallas_call(
        body, grid=(DEPTH,),
        in_specs=[_full((N_DEV, D_MODEL)), pl.BlockSpec((1, D_MODEL, cols), lambda l: (l, 0, 0)),
                  pl.BlockSpec((1, 1, cols), lambda l: (l, 0, 0))],
        out_specs=pl.BlockSpec((1, N_DEV, cols), lambda l: (l, 0, 0)),
        out_shape=jax.ShapeDtypeStruct((DEPTH, N_DEV, cols), F32), name=name, compiler_params=_cparams(),
    )(c_all, w_ada_sh, b_ada_sh)


def _mod_bwd(c_all, dmod_sh, name):
    cols = dmod_sh.shape[2]

    def body(c_ref, d_ref, o_ref):
        ca = _silu(c_ref[...])
        o_ref[0] = lax.dot_general(ca, d_ref[0], _DIMS["tn"], precision=lax.Precision.HIGHEST,
                                   preferred_element_type=F32)

    return pl.pallas_call(
        body, grid=(DEPTH,),
        in_specs=[_full((N_DEV, D_MODEL)), pl.BlockSpec((1, N_DEV, cols), lambda l: (l, 0, 0))],
        out_specs=pl.BlockSpec((1, D_MODEL, cols), lambda l: (l, 0, 0)),
        out_shape=jax.ShapeDtypeStruct((DEPTH, D_MODEL, cols), F32), name=name, compiler_params=_cparams(),
    )(c_all, dmod_sh)


def _flat_tiles(rows, cols, itemsize_total):
    budget = 12 * 1024 * 1024
    tr = rows
    while tr % 32 == 0 and tr * cols * itemsize_total > budget:
        tr //= 2
    return tr


def _sum_cores(dws, recvs, place, name):
    k = len(dws)

    def body(place_ref, *refs):
        for a_ref, b_ref, o_ref in zip(refs[:k], refs[k:2 * k], refs[2 * k:]):
            o_ref[...] = (a_ref[...].astype(F32) + b_ref[...].astype(F32)).astype(BF16)

    whole = [pl.BlockSpec(a.shape[1:], lambda i, pr: (0, 0)) for a in dws]
    mine = [pl.BlockSpec((None,) + a.shape[1:], lambda i, pr: (pr[0], 0, 0)) for a in dws]
    grid_spec = pltpu.PrefetchScalarGridSpec(num_scalar_prefetch=1, grid=(1,), in_specs=mine + whole, out_specs=whole)
    return pl.pallas_call(body, grid_spec=grid_spec, out_shape=[jax.ShapeDtypeStruct(a.shape[1:], BF16) for a in dws],
                          name=name, compiler_params=_cparams())(place, *dws, *recvs)


def _sum_chips(hs, rs, place, name):
    k = len(hs)

    def body(place_ref, *refs):
        for h_ref, r_ref, o_ref in zip(refs[:k], refs[k:2 * k], refs[2 * k:]):
            o_ref[...] = ((h_ref[...].astype(F32) + r_ref[0].astype(F32)) + r_ref[1].astype(F32)) + r_ref[2].astype(F32)

    own = [pl.BlockSpec((None,) + h.shape[1:], lambda i, pr: (pr[1], 0, 0)) for h in hs]
    got = [pl.BlockSpec(r.shape, lambda i, pr: (0, 0, 0)) for r in rs]
    out = [pl.BlockSpec(h.shape[1:], lambda i, pr: (0, 0)) for h in hs]
    grid_spec = pltpu.PrefetchScalarGridSpec(num_scalar_prefetch=1, grid=(1,), in_specs=own + got, out_specs=out)
    return pl.pallas_call(body, grid_spec=grid_spec, out_shape=[jax.ShapeDtypeStruct(h.shape[1:], F32) for h in hs],
                          name=name, compiler_params=_cparams())(place, *hs, *rs)


def _adam_math(w, g, m, v):
    m2 = ADAM_B1 * m + (1.0 - ADAM_B1) * g
    v2 = ADAM_B2 * v + (1.0 - ADAM_B2) * (g * g)
    m_hat = m2 / (1.0 - ADAM_B1 ** ADAM_STEP)
    v_hat = v2 / (1.0 - ADAM_B2 ** ADAM_STEP)
    delta = -ADAM_LR * (m_hat / (jnp.sqrt(v_hat) + ADAM_EPS) + ADAM_WD * w)
    return delta, m2, v2


def _adamw(w, m, v, grads, name):
    r, c = w.shape
    tr = _flat_tiles(r, c, 4 * (7 + len(grads)))

    def body(*refs):
        w_ref, m_ref, v_ref = refs[:3]
        g_refs = refs[3:3 + len(grads)]
        g_ref, d_ref, m2_ref, v2_ref = refs[3 + len(grads):]
        g = g_refs[0][...]
        for gr in g_refs[1:]:
            g = g + gr[...]
        delta, m2, v2 = _adam_math(w_ref[...], g, m_ref[...], v_ref[...])
        g_ref[...] = g
        d_ref[...] = delta
        m2_ref[...] = m2
        v2_ref[...] = v2

    blk = pl.BlockSpec((tr, c), lambda i: (i, 0))
    sh = jax.ShapeDtypeStruct((r, c), F32)
    return pl.pallas_call(body, grid=(r // tr,), in_specs=[blk] * (3 + len(grads)), out_specs=[blk] * 4,
                          out_shape=[sh] * 4, name=name, compiler_params=_cparams())(w, m, v, *grads)


def _adamw_halves(w, m, v, own, other, place, split, name):
    nl, r, c = w.shape
    hr, hc = own[0].shape
    tr = _flat_tiles(hr, hc, 4 * (7 + 2 * nl))
    nt = hr // tr
    if split == "rows":
        w_spec = pl.BlockSpec((None, tr, c), lambda l, h, t, pr: (l, h * nt + t, 0))
    else:
        w_spec = pl.BlockSpec((None, tr, hc), lambda l, h, t, pr: (l, t, h))

    def g_spec(layer, mine):
        return pl.BlockSpec((tr, hc), lambda l, h, t, pr: (jnp.where((l == layer) & ((h == pr[0]) == mine), t, nt - 1), 0))

    def body(place_ref, w_ref, m_ref, v_ref, *refs):
        own_refs, other_refs = refs[:nl], refs[nl:2 * nl]
        g_ref, d_ref, m2_ref, v2_ref = refs[2 * nl:]
        layer = pl.program_id(0)
        mine = pl.program_id(1) == place_ref[0]
        g = None
        for li in range(nl):
            cand = jnp.where(mine, own_refs[li][...], other_refs[li][...])
            g = cand if g is None else jnp.where(layer == li, cand, g)
        delta, m2, v2 = _adam_math(w_ref[...], g, m_ref[...], v_ref[...])
        g_ref[...] = g
        d_ref[...] = delta
        m2_ref[...] = m2
        v2_ref[...] = v2

    sh = jax.ShapeDtypeStruct((nl, r, c), F32)
    g_specs = [g_spec(li, True) for li in range(nl)] + [g_spec(li, False) for li in range(nl)]
    return _call(body, name=name, grid=(nl, 2, nt), in_specs=[w_spec] * 3 + g_specs, out_specs=[w_spec] * 4,
                 out_shape=[sh] * 4, scratch_shapes=[], args=(w, m, v, *own, *other), prefetch=(place,))


def _adamw_small(w, m, v, g_all, name):
    r, c = w.shape

    def body(w_ref, m_ref, v_ref, g_ref, go_ref, d_ref, m2_ref, v2_ref):
        g = g_ref[0]
        for b in range(1, N_DEV):
            g = g + g_ref[b]
        delta, m2, v2 = _adam_math(w_ref[...], g, m_ref[...], v_ref[...])
        go_ref[...] = g
        d_ref[...] = delta
        m2_ref[...] = m2
        v2_ref[...] = v2

    sh = jax.ShapeDtypeStruct((r, c), F32)
    return pl.pallas_call(body, out_shape=[sh] * 4, name=name, compiler_params=_cparams())(w, m, v, g_all)


def _me():
    return lax.axis_index("x"), lax.axis_index("y"), lax.axis_index("c")


def _flip(v, bit):
    return 1 - v if bit else v


def _allgather_small(blk, name):
    r, c = blk.shape

    def body(x_ref, o_ref, send_sems, recv_sems):
        x, y, cc = _me()
        me = 4 * x + 2 * y + cc
        copies = []
        for k in range(1, N_DEV):
            peer = (_flip(x, k & 4), _flip(y, k & 2), _flip(cc, k & 1))
            cp = pltpu.make_async_remote_copy(src_ref=x_ref, dst_ref=o_ref.at[me], send_sem=send_sems.at[k - 1],
                                              recv_sem=recv_sems.at[k - 1], device_id=peer, device_id_type=MESH)
            cp.start()
            copies.append(cp)
        o_ref[me] = x_ref[...]
        for cp in copies:
            cp.wait()

    return pl.pallas_call(
        body, out_shape=jax.ShapeDtypeStruct((N_DEV, r, c), F32),
        in_specs=[pl.BlockSpec(memory_space=pltpu.VMEM)], out_specs=pl.BlockSpec(memory_space=pltpu.VMEM),
        scratch_shapes=[pltpu.SemaphoreType.DMA((N_DEV - 1,)), pltpu.SemaphoreType.DMA((N_DEV - 1,))],
        name=name, compiler_params=_cparams(),
    )(blk)


class _Rider:
    def __init__(self, arrays, out_shapes, scratch_shapes, start, finish):
        self.arrays, self.out_shapes, self.scratch_shapes = list(arrays), list(out_shapes), list(scratch_shapes)
        self.start, self.finish = start, finish


def _call(body, *, name, grid, in_specs, out_specs, out_shape, scratch_shapes, args, rider=None, prefetch=()):
    npf = len(prefetch)

    def launch(fn, in_specs, out_specs, out_shape, scratch_shapes, args):
        grid_spec = pltpu.PrefetchScalarGridSpec(num_scalar_prefetch=npf, grid=grid, in_specs=in_specs,
                                                 out_specs=out_specs, scratch_shapes=scratch_shapes)
        return pl.pallas_call(fn, grid_spec=grid_spec, out_shape=out_shape, name=name,
                              compiler_params=_cparams())(*prefetch, *args)

    if rider is None:
        return launch(body, list(in_specs), list(out_specs), list(out_shape), list(scratch_shapes), args)
    ni, no, ns = len(in_specs), len(out_specs), len(scratch_shapes)
    ri, ro = len(rider.arrays), len(rider.out_shapes)
    steps = int(np.prod(grid))

    def wrapped(*refs):
        pf, refs = refs[:npf], refs[npf:]
        h_in, r_in = refs[:ni], refs[ni:ni + ri]
        h_out, r_out = refs[ni + ri:ni + ri + no], refs[ni + ri + no:ni + ri + no + ro]
        h_scr, r_scr = refs[ni + ri + no + ro:ni + ri + no + ro + ns], refs[ni + ri + no + ro + ns:]
        step = pl.program_id(0)
        for d in range(1, len(grid)):
            step = step * grid[d] + pl.program_id(d)

        @pl.when(step == 0)
        def _():
            rider.start(r_in, r_out, r_scr)

        body(*pf, *h_in, *h_out, *h_scr)

        @pl.when(step == steps - 1)
        def _():
            rider.finish(r_in, r_out, r_scr)

    anyspec = pl.BlockSpec(memory_space=pl.ANY)
    res = launch(wrapped, list(in_specs) + [anyspec] * ri, list(out_specs) + [anyspec] * ro,
                 list(out_shape) + rider.out_shapes, list(scratch_shapes) + rider.scratch_shapes,
                 list(args) + rider.arrays)
    return res[:no], res[no:]


def _run_rider(rider, name):
    ri = len(rider.arrays)

    def body(*refs):
        r_in, r_out, r_scr = refs[:ri], refs[ri:ri + len(rider.out_shapes)], refs[ri + len(rider.out_shapes):]
        rider.start(r_in, r_out, r_scr)
        rider.finish(r_in, r_out, r_scr)

    anyspec = pl.BlockSpec(memory_space=pl.ANY)
    return pl.pallas_call(body, in_specs=[anyspec] * ri, out_specs=[anyspec] * len(rider.out_shapes),
                          out_shape=rider.out_shapes, scratch_shapes=rider.scratch_shapes, name=name,
                          compiler_params=_cparams())(*rider.arrays)


def _allgather_rider(blk):
    def copies(ins, outs, scr):
        send_sems, recv_sems, loc_sems, stage = scr
        x, y, cc = _me()
        me = 4 * x + 2 * y + cc
        remote = [pltpu.make_async_remote_copy(
            src_ref=ins[0], dst_ref=outs[0].at[me], send_sem=send_sems.at[k - 1], recv_sem=recv_sems.at[k - 1],
            device_id=(_flip(x, k & 4), _flip(y, k & 2), _flip(cc, k & 1)), device_id_type=MESH) for k in range(1, N_DEV)]
        return remote, pltpu.make_async_copy(ins[0], stage, loc_sems.at[0]), (outs[0].at[me], stage, loc_sems.at[1])

    def start(ins, outs, scr):
        remote, lin, _ = copies(ins, outs, scr)
        lin.start()
        for cp in remote:
            cp.start()

    def finish(ins, outs, scr):
        remote, lin, (dst, stage, sem) = copies(ins, outs, scr)
        lin.wait()
        lout = pltpu.make_async_copy(stage, dst, sem)
        lout.start()
        for cp in remote:
            cp.wait()
        lout.wait()

    return _Rider([blk], [jax.ShapeDtypeStruct((N_DEV,) + blk.shape, blk.dtype)],
                  [pltpu.SemaphoreType.DMA((N_DEV - 1,)), pltpu.SemaphoreType.DMA((N_DEV - 1,)),
                   pltpu.SemaphoreType.DMA((2,)), pltpu.VMEM(blk.shape, blk.dtype)], start, finish)


def _gather_rider(shards):
    n = len(shards)

    def copies(ins, outs, scr, relay=True):
        ici_send, ici_recv, d2d_send, d2d_recv, loc_sems = scr[:5]
        stage = scr[5:]
        x, y, cc = _me()
        chip = 2 * x + y
        sibling = (x, y, 1 - cc)
        local, sends, relays = [], [], []
        for j in range(n):
            def rows(ch, h, j=j):
                return outs[j].at[ch, h]

            lc = pltpu.make_async_copy(ins[j], stage[j], loc_sems.at[j])
            local.append((lc, pltpu.make_async_copy(stage[j], outs[j].at[chip], loc_sems.at[n + j]) if relay else None))
            for k in range(1, N_CHIP):
                px, py = _flip(x, k & 2), _flip(y, k & 1)
                pchip = 2 * px + py
                q = 3 * j + k - 1
                out_cp = pltpu.make_async_remote_copy(src_ref=ins[j].at[cc], dst_ref=rows(chip, cc),
                                                      send_sem=ici_send.at[q], recv_sem=ici_recv.at[q],
                                                      device_id=(px, py, cc), device_id_type=MESH)
                sends.append(out_cp)
                if not relay:
                    continue
                arrival = pltpu.make_async_remote_copy(src_ref=rows(pchip, cc), dst_ref=rows(pchip, cc),
                                                       send_sem=ici_send.at[q], recv_sem=ici_recv.at[q],
                                                       device_id=(px, py, cc), device_id_type=MESH)
                forward = pltpu.make_async_remote_copy(src_ref=rows(pchip, cc), dst_ref=rows(pchip, cc),
                                                       send_sem=d2d_send.at[q], recv_sem=d2d_recv.at[q],
                                                       device_id=sibling, device_id_type=MESH)
                from_sibling = pltpu.make_async_remote_copy(src_ref=rows(pchip, 1 - cc), dst_ref=rows(pchip, 1 - cc),
                                                            send_sem=d2d_send.at[q], recv_sem=d2d_recv.at[q],
                                                            device_id=sibling, device_id_type=MESH)
                relays.append((arrival, forward, from_sibling))
        return local, sends, relays

    def start(ins, outs, scr):
        local, sends, _ = copies(ins, outs, scr, relay=False)
        for lin, _ in local:
            lin.start()
        for cp in sends:
            cp.start()

    def finish(ins, outs, scr):
        local, sends, relays = copies(ins, outs, scr)
        for lin, lout in local:
            lin.wait()
            lout.start()
        for arrival, forward, _ in relays:
            arrival.wait_recv()
            forward.start()
        for cp in sends:
            cp.wait_send()
        for _, forward, from_sibling in relays:
            forward.wait_send()
            from_sibling.wait_recv()
        for _, lout in local:
            lout.wait()

    scratch = [pltpu.SemaphoreType.DMA((3 * n,)), pltpu.SemaphoreType.DMA((3 * n,)), pltpu.SemaphoreType.DMA((3 * n,)),
               pltpu.SemaphoreType.DMA((3 * n,)), pltpu.SemaphoreType.DMA((2 * n,))]
    scratch += [pltpu.VMEM(a.shape, a.dtype) for a in shards]
    return _Rider(shards, [jax.ShapeDtypeStruct((N_CHIP,) + a.shape, a.dtype) for a in shards], scratch, start, finish)


def _sibling_rider(arrs, other_half=False):
    n = len(arrs)

    def copies(ins, outs, scr):
        send_sems, recv_sems = scr
        x, y, cc = _me()
        return [pltpu.make_async_remote_copy(
            src_ref=ins[j].at[1 - cc] if other_half else ins[j], dst_ref=outs[j], send_sem=send_sems.at[j],
            recv_sem=recv_sems.at[j], device_id=(x, y, 1 - cc), device_id_type=MESH) for j in range(n)]

    def start(ins, outs, scr):
        for cp in copies(ins, outs, scr):
            cp.start()

    def finish(ins, outs, scr):
        for cp in copies(ins, outs, scr):
            cp.wait()

    return _Rider(arrs, [jax.ShapeDtypeStruct(a.shape[1:] if other_half else a.shape, a.dtype) for a in arrs],
                  [pltpu.SemaphoreType.DMA((n,)), pltpu.SemaphoreType.DMA((n,))], start, finish)


def _sibling_send(arrs, name, other_half=False):
    return _run_rider(_sibling_rider(arrs, other_half), name)


def _join_riders(first, second):
    ni, no, ns = len(first.arrays), len(first.out_shapes), len(first.scratch_shapes)

    def split(ins, outs, scr):
        return (ins[:ni], outs[:no], scr[:ns]), (ins[ni:], outs[no:], scr[ns:])

    def start(ins, outs, scr):
        a, b = split(ins, outs, scr)
        first.start(*a)
        second.start(*b)

    def finish(ins, outs, scr):
        a, b = split(ins, outs, scr)
        first.finish(*a)
        second.finish(*b)

    return _Rider(first.arrays + second.arrays, first.out_shapes + second.out_shapes,
                  first.scratch_shapes + second.scratch_shapes, start, finish)


def _scatter_rider(arrs):
    n = len(arrs)

    def copies(ins, outs, scr):
        send_sems, recv_sems = scr
        x, y, cc = _me()
        cps = []
        for j in range(n):
            for k in range(1, N_CHIP):
                px, py = _flip(x, k & 2), _flip(y, k & 1)
                cps.append(pltpu.make_async_remote_copy(
                    src_ref=ins[j].at[2 * px + py], dst_ref=outs[j].at[k - 1], send_sem=send_sems.at[3 * j + k - 1],
                    recv_sem=recv_sems.at[3 * j + k - 1], device_id=(px, py, cc), device_id_type=MESH))
        return cps

    def start(ins, outs, scr):
        for cp in copies(ins, outs, scr):
            cp.start()

    def finish(ins, outs, scr):
        for cp in copies(ins, outs, scr):
            cp.wait()

    return _Rider(arrs, [jax.ShapeDtypeStruct((N_CHIP - 1,) + a.shape[1:], a.dtype) for a in arrs],
                  [pltpu.SemaphoreType.DMA((3 * n,)), pltpu.SemaphoreType.DMA((3 * n,))], start, finish)


COL_SHARDED = ("w_in", "w_br_pool", "w_br_attn", "w_br_conv", "w_ff1")
ROW_SHARDED = ("w_o", "w_ff2")
BIG = COL_SHARDED + ROW_SHARDED
SMALL = ("b_ada", "b_gate", "w_pool", "pool_scale", "rel_bias", "conv_w", "conv_b", "conv_ln_g", "conv_ln_b",
         "ln_mix_g", "ln_mix_b", "b_ff1", "b_ff2", "ln_ff_g", "ln_ff_b")
PACK_W = 1024


def _pack(parts):
    rows = []
    for a in parts:
        flat = a.reshape(-1)
        n = -(-flat.shape[0] // PACK_W) * PACK_W
        rows.append(jnp.pad(flat, (0, n - flat.shape[0])).reshape(-1, PACK_W))
    out = jnp.concatenate(rows, axis=0)
    r = -(-out.shape[0] // 8) * 8
    return jnp.pad(out, ((0, r - out.shape[0]), (0, 0)))


def _unpack(packed, shapes):
    out, r0 = [], 0
    for shp in shapes:
        size = int(np.prod(shp))
        nr = -(-size // PACK_W)
        out.append(packed[r0:r0 + nr].reshape(-1)[:size].reshape(shp))
        r0 += nr
    return out


def _hosted(fn, hook, *args, **kw):
    if hook is None:
        return fn(*args, **kw)
    res, rider_out = fn(*args, rider=hook[0], **kw)
    hook[1](rider_out)
    return res


def _layer_fwd(l, x, mod, W, P, hooks=None, u=None):
    hooks = hooks or {}
    s = x.shape[0]
    sh_m, sc_m, g_m, sh_f, sc_f, g_f = [mod[l:l + 1, D_MODEL * j:D_MODEL * (j + 1)] for j in range(6)]
    n = lambda t: f"{t}{l}"
    w_in = W["w_in"][l]
    if u is None:
        u = _ln_mod(x, sc_m, sh_m, n("ln_mod_mix"))
    zp = _mm(u, w_in, "nt", tm=s, tn=256, out_dtype=F32, name=n("z_pool"), b_col0=0, n_out=D_POOL)
    qkv = _mm(u, w_in, "nt", tm=s, tn=256, out_dtype=BF16, name=n("z_qkv"), b_col0=OFF_QKV // 256, n_out=3 * D_ATTN)
    zc = _mm(u, w_in, "nt", tm=s, tn=256, out_dtype=F32, name=n("z_conv"), b_col0=OFF_CONV // 256, n_out=2 * D_CONV)
    zg = _hosted(_mm, hooks.get("z_gate"), u, w_in, "nt", tm=min(2048, s), tn=768, out_dtype=BF16, name=n("z_gate"),
                 b_col0=OFF_GATE // 768, n_out=3 * D_MODEL)

    p, feat_pool = _pool_fwd(zp, P["wp_bd"][l], P["pool_scale"][l], n("pool_fwd"))
    bias = _bias_block(P["rel_bias"][l], n("bias_block"))
    o, probs = _hosted(_attn_fwd, hooks.get("attn"), qkv, bias, n("attn_fwd"))
    cv, feat_conv = _conv_fwd(zc, P["conv_w"][l], P["conv_b"][l], P["conv_ln_g"][l], P["conv_ln_b"][l], n("conv_fwd"))

    branch_w = (W["w_br_pool"][l], W["w_br_attn"][l], W["w_br_conv"][l])
    ys = tuple(_branch_out((feat_pool, o, feat_conv), branch_w, n("branch_out")))
    merged = _merge(zg, P["b_gate"][l], ys, n("merge"))
    mix, x1, u2 = _mm_resid_ln(merged, W["w_o"][l], None, x, g_m, P["ln_mix_g"][l], P["ln_mix_b"][l], n("mix_out"),
                               mod_next=(sc_f, sh_f))

    hpre, hid = _hosted(_ff_hidden, hooks.get("ff1"), u2, W["w_ff1"][l], P["b_ff1"][l], n("ff1"))
    above = None if l + 1 == mod.shape[0] else (mod[l + 1:l + 2, D_MODEL:2 * D_MODEL], mod[l + 1:l + 2, 0:D_MODEL])
    ff, x2, *u_next = _hosted(_mm_resid_ln, hooks.get("ff2"), hid, W["w_ff2"][l], P["b_ff2"][l], x1, g_f,
                              P["ln_ff_g"][l], P["ln_ff_b"][l], n("ff2"), mod_next=above)
    saved = dict(x=x, u=u, zp=zp, qkv=qkv, zc=zc, zg=zg, p=p, feat_pool=feat_pool, probs=probs, o=o, cv=cv,
                 feat_conv=feat_conv, ys=ys, merged=merged, mix=mix, x1=x1, u2=u2, hpre=hpre, hid=hid, ff=ff,
                 u_next=u_next[0] if u_next else None)
    return x2, saved


def _layer_bwd(l, dx2, mod, W, P, A, hooks=None, tgt=None, nxt=None):
    hooks = hooks or {}
    sh_m, sc_m, g_m, sh_f, sc_f, g_f = [mod[l:l + 1, D_MODEL * j:D_MODEL * (j + 1)] for j in range(6)]
    n = lambda t: f"{t}{l}"
    gw, gs = {}, {}

    if isinstance(dx2, tuple):
        dres, dff, gs["ln_ff_g"], gs["ln_ff_b"], dg_f, gs["b_ff2"] = dx2
    else:
        dres, dff, gs["ln_ff_g"], gs["ln_ff_b"], dg_f, gs["b_ff2"], *loss_part = _resid_ln_bwd(
            dx2, A["x1"], A["ff"], g_f, P["ln_ff_g"][l], n("resid_ln_ff_bwd"), tgt=tgt)
    gw["w_ff2"] = _mm(A["hid"], dff, "tn", tm=512, tn=1024, out_dtype=BF16, name=n("dw_ff2"), split_n=512)
    hook = hooks["ff_hidden_bwd"](gw) if "ff_hidden_bwd" in hooks else None
    dhpre, gs["b_ff1"] = _hosted(_ff_hidden_bwd, hook, dff, W["w_ff2"][l], A["hpre"], n("ff_hidden_bwd"))
    gw["w_ff1"] = _mm(dhpre, A["u2"], "tn", tm=512, tn=1024, out_dtype=BF16, name=n("dw_ff1"), split_n=512)

    hook = hooks["du_ff"](gw) if "du_ff" in hooks else None
    dres, dmix, dsc_f, dsh_f, gs["ln_mix_g"], gs["ln_mix_b"], dg_m, _ = _hosted(
        _mm_ln_mod_bwd, hook, dhpre, W["w_ff1"][l], A["x1"], sc_f, dres, n("du_ff"),
        nxt=(A["x"], A["mix"], g_m, P["ln_mix_g"][l]))
    gw["w_o"] = _mm(A["merged"], dmix, "tn", tm=512, tn=1024, out_dtype=BF16, name=n("dw_o"), split_n=512)
    dy_pool, dy_attn, dy_conv, dzg, gs["b_gate"] = _merge_bwd(dmix, W["w_o"][l], A["zg"], P["b_gate"][l], A["ys"],
                                                              n("merge_bwd"))

    dys = (dy_pool, dy_attn, dy_conv)
    gw["w_br_pool"], gw["w_br_attn"], gw["w_br_conv"] = _branch_dw(
        dys, (A["feat_pool"], A["o"], A["feat_conv"]), n("dw_branch"))
    dfeat_pool, do, dfeat_conv = _branch_in_bwd(
        dys, (W["w_br_pool"][l], W["w_br_attn"][l], W["w_br_conv"][l]), (F32, BF16, F32), n("d_branch_in"))

    dzp, dwp_bd, gs["pool_scale"] = _pool_bwd(dfeat_pool, A["p"], P["wp_bd"][l], P["pool_scale"][l], n("pool_bwd"))
    gs["w_pool"] = jnp.stack([dwp_bd[POOL_GROUP * g:POOL_GROUP * (g + 1), POOL_GROUP * g:POOL_GROUP * (g + 1)]
                              for g in range(len(POOL_WINDOWS))])
    hook = hooks["attn"](gw) if "attn" in hooks else None
    dq, dk, dv, ds_acc = _hosted(_attn_bwd, hook, A["qkv"], do, A["probs"], n("attn_bwd"))
    gs["rel_bias"] = _bias_block_bwd(ds_acc, n("bias_block_bwd"))
    dzc, dcw, gs["conv_b"], gs["conv_ln_g"], gs["conv_ln_b"] = _conv_bwd(
        dfeat_conv, A["cv"], A["zc"], P["conv_w"][l], P["conv_ln_g"][l], P["conv_ln_b"][l], n("conv_bwd"))
    gs["conv_w"] = dcw[:CONV_WIDTH]

    dz = [dzp, dq, dk, dv, dzc, dzg]
    gw["w_in"] = _dw_segments(dz, A["u"], n("dw_in"))
    hook = hooks["du_mix"](gw) if "du_mix" in hooks else None
    res = _hosted(_mm_ln_mod_bwd, hook, dz, W["w_in"][l], A["x"], sc_m, dres, n("du_mix"), nxt=nxt)
    if nxt is None:
        dx, dsc_m, dsh_m = res
    else:
        dx, dsc_m, dsh_m = (res[0], res[1], *res[4:]), res[2], res[3]
    dmod = jnp.concatenate([dsh_m, dsc_m, dg_m, dsh_f, dsc_f, dg_f], axis=1)
    return (dx, gw, gs, dmod) if tgt is None else (dx, gw, gs, dmod, loss_part[0])


def _small_shapes():
    return {"b_ada": (6 * D_MODEL,), "b_gate": (3 * D_MODEL,), "w_pool": (4, POOL_GROUP, POOL_GROUP),
            "pool_scale": (D_POOL,), "rel_bias": (N_HEADS, N_REL), "conv_w": (CONV_WIDTH, D_CONV),
            "conv_b": (D_CONV,), "conv_ln_g": (D_CONV,), "conv_ln_b": (D_CONV,), "ln_mix_g": (D_MODEL,),
            "ln_mix_b": (D_MODEL,), "b_ff1": (D_FF,), "b_ff2": (D_MODEL,), "ln_ff_g": (D_MODEL,), "ln_ff_b": (D_MODEL,)}


def kernel(x, c, w_ada, b_ada, w_in, b_gate, w_pool, pool_scale, rel_bias, conv_w, conv_b, conv_ln_g, conv_ln_b, w_br_pool, w_br_attn, w_br_conv, w_o, ln_mix_g, ln_mix_b, w_ff1, b_ff1, w_ff2, b_ff2, ln_ff_g, ln_ff_b, loss_target, m_w_ada, m_b_ada, m_w_in, m_b_gate, m_w_pool, m_pool_scale, m_rel_bias, m_conv_w, m_conv_b, m_conv_ln_g, m_conv_ln_b, m_w_br_pool, m_w_br_attn, m_w_br_conv, m_w_o, m_ln_mix_g, m_ln_mix_b, m_w_ff1, m_b_ff1, m_w_ff2, m_b_ff2, m_ln_ff_g, m_ln_ff_b, v_w_ada, v_b_ada, v_w_in, v_b_gate, v_w_pool, v_pool_scale, v_rel_bias, v_conv_w, v_conv_b, v_conv_ln_g, v_conv_ln_b, v_w_br_pool, v_w_br_attn, v_w_br_conv, v_w_o, v_ln_mix_g, v_ln_mix_b, v_w_ff1, v_b_ff1, v_w_ff2, v_b_ff2, v_ln_ff_g, v_ln_ff_b):
    env = dict(locals())
    xi, yi, ci = _me()
    chip = 2 * xi + yi
    me = 4 * xi + 2 * yi + ci
    xs = x[0]
    tgt = loss_target[0]
    L = DEPTH

    first = _allgather_small(jnp.concatenate([c.reshape(8, 128), _pack([conv_w]).reshape(-1, 128)]), "gather_c_conv_w")
    c_all = first[:, :8].reshape(N_DEV, D_MODEL)
    ada_cols = w_ada.shape[2]
    b_ada_sh = lax.dynamic_slice_in_dim(b_ada, chip * ada_cols, ada_cols, axis=1).reshape(L, 1, ada_cols)
    mod_part = _mod_fwd(c_all, w_ada, b_ada_sh, "mod_fwd")

    W = {k: [None] * L for k in BIG}

    def weight_gather(*items):
        shards = [(jnp.swapaxes(env[k][l], 0, 1) if k in COL_SHARDED else env[k][l]).astype(BF16) for k, l in items]
        shards = [a.reshape(2, a.shape[0] // 2, a.shape[1]) for a in shards]

        def done(outs):
            for (k, l), g in zip(items, outs):
                W[k][l] = g.reshape(-1, g.shape[-1])

        return _gather_rider(shards), done

    branch = lambda l: [(k, l) for k in ("w_br_pool", "w_br_attn", "w_br_conv", "w_o")]
    rider, done = weight_gather(("w_in", 0))
    first_out = _run_rider(_join_riders(_allgather_rider(mod_part.reshape(-1, 128)), rider), "gather_mod_w_in0")
    done(first_out[1:])
    mod_g = first_out[0].reshape(N_CHIP, 2, L, N_DEV, ada_cols)[:, 0]
    mod_all = jnp.transpose(mod_g, (1, 2, 0, 3)).reshape(L, N_DEV, 6 * D_MODEL)
    mod = lax.dynamic_index_in_dim(mod_all, me, axis=1, keepdims=False)
    fwd_hooks = [{"z_gate": weight_gather(*branch(0)), "attn": weight_gather(("w_ff1", 0), ("w_ff2", 0)),
                  "ff1": weight_gather(*branch(1)), "ff2": weight_gather(("w_in", 1))},
                 {"attn": weight_gather(("w_ff1", 1), ("w_ff2", 1))}]

    P = {k: env[k] for k in ("rel_bias", "conv_w")}
    for k in ("b_gate", "pool_scale", "conv_b", "conv_ln_g", "conv_ln_b", "ln_mix_g", "ln_mix_b", "b_ff1", "b_ff2",
              "ln_ff_g", "ln_ff_b"):
        P[k] = env[k].reshape(L, 1, -1)
    n_cw = conv_w.size
    cw = first[:, 8:].reshape(N_CHIP, 2, -1)[:, 0, :n_cw].reshape(N_CHIP, L, CONV_WIDTH, D_CONV // N_CHIP)
    P["conv_w"] = jnp.transpose(cw, (1, 2, 0, 3)).reshape(L, CONV_WIDTH, D_CONV)
    wp_bd = jnp.zeros((L, D_POOL, D_POOL), F32)
    for g in range(len(POOL_WINDOWS)):
        sl = slice(POOL_GROUP * g, POOL_GROUP * (g + 1))
        wp_bd = wp_bd.at[:, sl, sl].set(w_pool[:, g])
    P["wp_bd"] = wp_bd.astype(BF16)

    acts = []
    h = xs
    for l in range(L):
        h, saved = _layer_fwd(l, h, mod, W, P, fwd_hooks[l], u=acts[-1]["u_next"] if acts else None)
        acts.append(saved)

    place = jnp.stack([ci, chip, chip ^ 1, chip ^ 2, chip ^ 3]).astype(jnp.int32)
    scattered = {}

    swapped = {}

    def swap_hook(names, l):
        def hook(gw):
            def done(outs):
                swapped.update({(k, l): o for k, o in zip(names, outs)})
            return _sibling_rider([gw[k] for k in names], other_half=True), done
        return hook

    def scatter_hook(names, l, host, then=None):
        def hook(gw):
            todo = [k for k in names if (k, l) not in swapped]
            if todo:
                got = _sibling_send([gw[k] for k in todo], f"swap_blocks_{host}{l}", other_half=True)
                swapped.update({(k, l): o for k, o in zip(todo, got)})
            sums = _sum_cores([gw[k] for k in names], [swapped[(k, l)] for k in names], place, f"sum_cores_{host}{l}")
            both = [hh.reshape(N_CHIP, -1, hh.shape[-1]) for hh in sums]
            rider = _scatter_rider(both)
            more = then(gw) if then is not None else None

            def done(outs):
                for k, hh, r in zip(names, both, outs):
                    scattered[(k, l)] = (hh, r)
                if more is not None:
                    more[1](outs[len(names):])

            return (rider if more is None else _join_riders(rider, more[0])), done
        return hook

    gws, gss, dmods = [None] * L, [None] * L, [None] * L
    dh = h
    for l in reversed(range(L)):
        hooks = {"ff_hidden_bwd": swap_hook(("w_ff2",), l),
                 "du_ff": scatter_hook(("w_ff2",), l, "du_ff", then=swap_hook(("w_ff1",), l)),
                 "attn": scatter_hook(("w_ff1", "w_o", "w_br_pool", "w_br_attn", "w_br_conv"), l, "attn_bwd"),
                 "du_mix": scatter_hook(("w_in",), l, "du_mix")}
        below = None
        if l > 0:
            below = (acts[l - 1]["x1"], acts[l - 1]["ff"], mod[l - 1:l, 5 * D_MODEL:], P["ln_ff_g"][l - 1])
        if l == L - 1:
            dh, gws[l], gss[l], dmods[l], loss_part = _layer_bwd(l, dh, mod, W, P, acts[l], hooks, tgt=tgt, nxt=below)
        else:
            dh, gws[l], gss[l], dmods[l] = _layer_bwd(l, dh, mod, W, P, acts[l], hooks, nxt=below)
    grad_x = dh[None]

    reduced = [[None] * L for _ in BIG]
    groups = (("w_in", "w_br_pool", "w_br_attn", "w_br_conv"), ("w_o", "w_ff1", "w_ff2"))
    for l in range(L):
        for gi, names in enumerate(groups):
            pairs = [scattered[(k, l)] for k in names]
            sums = _sum_chips([p[0] for p in pairs], [p[1] for p in pairs], place, f"sum_chips_{gi}_{l}")
            for k, t in zip(names, sums):
                reduced[BIG.index(k)][l] = t
    flat_reduced = [t for per_weight in reduced for t in per_weight]

    shapes = _small_shapes()
    small_names = [k for k in SMALL if k != "b_ada"]
    dmod_own = jnp.concatenate(dmods, axis=0)
    pack = _pack([dmod_own] + [jnp.stack([gss[l][k].reshape(shapes[k]) for l in range(L)]) for k in small_names]
                 + [loss_part])
    last = _run_rider(_join_riders(_sibling_rider(flat_reduced), _allgather_rider(pack.reshape(-1, 128))),
                      "swap_reduced_gather_small")
    flat_other, g_all = last[:-1], last[-1].reshape(N_DEV, -1, PACK_W)

    out = {}
    for j, k in enumerate(BIG):
        own, other = reduced[j], flat_other[L * j:L * (j + 1)]
        if k == "w_in":
            t = lambda a: jnp.swapaxes(a, 1, 2)
            res = _adamw_halves(t(env[k]), t(env["m_" + k]), t(env["v_" + k]), own, other, place, "cols", f"adamw_{k}")
            res = [t(a) for a in res]
        else:
            if k in COL_SHARDED:
                own, other = [a.T for a in own], [a.T for a in other]
            res = _adamw_halves(env[k], env["m_" + k], env["v_" + k], own, other, place,
                                "rows" if k in COL_SHARDED else "cols", f"adamw_{k}")
        out[k] = tuple(res)

    dmod_all = g_all[:, :L * 6].reshape(N_DEV, L, 6 * D_MODEL)
    dmod_sh = jnp.transpose(lax.dynamic_slice_in_dim(dmod_all, chip * ada_cols, ada_cols, axis=2), (1, 0, 2))
    g_ada = _mod_bwd(c_all, dmod_sh, "mod_bwd")
    g_, d_, m_, v_ = _adamw(w_ada.reshape(-1, ada_cols), m_w_ada.reshape(-1, ada_cols), v_w_ada.reshape(-1, ada_cols),
                            [g_ada.reshape(-1, ada_cols)], "adamw_w_ada")
    out["w_ada"] = tuple(a.reshape(w_ada.shape) for a in (g_, d_, m_, v_))

    def small_pack(prefix):
        parts = [env[prefix + "b_ada"]]
        for k in small_names:
            a = env[prefix + k]
            if k == "conv_w":
                a = jnp.zeros((L,) + shapes[k], F32)
            parts.append(a)
        return _pack(parts + [jnp.zeros_like(loss_part)])

    gp, dp, mp, vp = _adamw_small(small_pack(""), small_pack("m_"), small_pack("v_"), g_all, "adamw_small")
    full_shapes = [(L,) + shapes["b_ada"]] + [(L,) + shapes[k] for k in small_names]
    loss = _unpack(gp, full_shapes + [(128,)])[-1][0]
    for tag, packed in (("g", gp), ("d", dp), ("m", mp), ("v", vp)):
        for k, a in zip(["b_ada"] + small_names, _unpack(packed, full_shapes)):
            out.setdefault(k, {})
            out[k][tag] = a
    g_cw_full = out["conv_w"]["g"]
    cw_cols = D_CONV // N_CHIP
    g_cw = lax.dynamic_slice_in_dim(g_cw_full, chip * cw_cols, cw_cols, axis=2)
    pad_rows = lambda a: jnp.pad(a.reshape(L * CONV_WIDTH, cw_cols), ((0, 2), (0, 0)))
    g_, d_, m_, v_ = _adamw(pad_rows(conv_w), pad_rows(m_conv_w), pad_rows(v_conv_w), [pad_rows(g_cw)], "adamw_conv_w")
    out["conv_w"] = tuple(a[:L * CONV_WIDTH].reshape(L, CONV_WIDTH, cw_cols) for a in (g_, d_, m_, v_))

    names = ["w_ada", "b_ada", "w_in", "b_gate", "w_pool", "pool_scale", "rel_bias", "conv_w", "conv_b", "conv_ln_g",
             "conv_ln_b", "w_br_pool", "w_br_attn", "w_br_conv", "w_o", "ln_mix_g", "ln_mix_b", "w_ff1", "b_ff1",
             "w_ff2", "b_ff2", "ln_ff_g", "ln_ff_b"]

    def pick(k, i):
        o = out[k]
        return o[i] if isinstance(o, tuple) else o["gdmv"[i]].reshape(env[k].shape)

    return (loss, grad_x, *[pick(k, 0) for k in names], *[pick(k, 1) for k in names],
            *[pick(k, 2) for k in names], *[pick(k, 3) for k in names])
```

```python
import jax
import jax.numpy as jnp
import numpy as np
from jax import lax
from jax.experimental import pallas as pl
from jax.experimental.pallas import tpu as pltpu

F32 = jnp.float32
BF16 = jnp.bfloat16

D_MODEL = 1024
DEPTH = 2
CHUNK = 64
POOL_WINDOWS = (2, 4, 8, 16)
POOL_GROUP = 64
D_POOL = 256
N_HEADS = 8
HEAD_DIM = 64
D_ATTN = 512
N_PREV_CHUNKS = 8
REL_CLIP = 128
N_REL = 2 * REL_CLIP + 1
D_CONV = 256
CONV_WIDTH = 31
D_FF = 4 * D_MODEL
D_IN = 5376
OFF_POOL, OFF_QKV, OFF_CONV, OFF_GATE = 0, 256, 1792, 2304
ALPHA = (2.0 * DEPTH) ** 0.25
LN_EPS = 1e-5
NEG_INF = -1e30
ADAM_LR, ADAM_B1, ADAM_B2, ADAM_EPS, ADAM_WD, ADAM_STEP = 0.001, 0.9, 0.999, 1e-08, 0.01, 10

N_DEV = 8
N_CHIP = 4
MESH = pl.DeviceIdType.MESH

QB = 2 * CHUNK
KPAD = N_PREV_CHUNKS * CHUNK
KW = QB + KPAD
SKEW_W = 768

VMEM_LIMIT = 56 * 1024 * 1024


def _cparams(**kw):
    return pltpu.CompilerParams(vmem_limit_bytes=VMEM_LIMIT, **kw)


def _full(shape):
    n = len(shape)
    return pl.BlockSpec(shape, lambda *_: (0,) * n)


def _resident(shape):
    n = len(shape)
    return pl.BlockSpec(shape, lambda *_: (0,) * n, pipeline_mode=pl.Buffered(1))


_DIMS = {"nn": (((1,), (0,)), ((), ())), "nt": (((1,), (1,)), ((), ())), "tn": (((0,), (0,)), ((), ()))}


def _relu2(t):
    r = jnp.maximum(t, 0.0)
    return r * r


def _mm(a, b, mode, *, tm, tn, out_dtype, name, b_col0=0, n_out=None, bias=None, split_n=0, rider=None):
    if mode == "tn":
        k, m = a.shape
        n = b.shape[1] if n_out is None else n_out
        a_spec = pl.BlockSpec((k, tm), lambda i, j: (0, i))
        b_spec = pl.BlockSpec((k, tn), lambda i, j: (0, j + b_col0))
    elif mode == "nn":
        m, k = a.shape
        n = b.shape[1] if n_out is None else n_out
        a_spec = pl.BlockSpec((tm, k), lambda i, j: (i, 0))
        b_spec = pl.BlockSpec((k, tn), lambda i, j: (0, j + b_col0))
    else:
        m, k = a.shape
        n = b.shape[0] if n_out is None else n_out
        a_spec = pl.BlockSpec((tm, k), lambda i, j: (i, 0))
        b_spec = pl.BlockSpec((tn, k), lambda i, j: (j + b_col0, 0))
    assert m % tm == 0 and n % tn == 0, (name, m, n, tm, tn)
    dims = _DIMS[mode]

    def body(*refs):
        if bias is None:
            a_ref, b_ref, o_ref = refs
        else:
            a_ref, b_ref, bias_ref, o_ref = refs
        acc = lax.dot_general(a_ref[...].astype(BF16), b_ref[...].astype(BF16), dims, preferred_element_type=F32)
        if bias is not None:
            acc = acc + bias_ref[...]
        if split_n:
            for c in range(tn // split_n):
                o_ref[c] = acc[:, c * split_n:(c + 1) * split_n].astype(out_dtype)
        else:
            o_ref[...] = acc.astype(out_dtype)

    in_specs = [a_spec, b_spec]
    args = [a, b]
    if bias is not None:
        in_specs.append(pl.BlockSpec((1, tn), lambda i, j: (0, j)))
        args.append(bias)
    if split_n:
        out_spec = pl.BlockSpec((tn // split_n, tm, split_n), lambda i, j: (j, i, 0))
        out_shape = jax.ShapeDtypeStruct((n // split_n, m, split_n), out_dtype)
    else:
        out_spec = pl.BlockSpec((tm, tn), lambda i, j: (i, j))
        out_shape = jax.ShapeDtypeStruct((m, n), out_dtype)
    res = _call(body, name=name, grid=(m // tm, n // tn), in_specs=in_specs, out_specs=[out_spec],
                out_shape=[out_shape], scratch_shapes=[], args=args, rider=rider)
    return res[0] if rider is None else (res[0][0], res[1])


def _ln_hat(x):
    mu = jnp.mean(x, axis=-1, keepdims=True)
    xc = x - mu
    var = jnp.mean(xc * xc, axis=-1, keepdims=True)
    rstd = lax.rsqrt(var + LN_EPS)
    return xc * rstd, rstd


def _ln_hat_bwd(dhat, xhat, rstd):
    m1 = jnp.mean(dhat, axis=-1, keepdims=True)
    m2 = jnp.mean(dhat * xhat, axis=-1, keepdims=True)
    return rstd * (dhat - m1 - xhat * m2)


def _row_tile(s):
    return min(512, s)


def _acc_rows(ref, val, first):
    @pl.when(first)
    def _():
        ref[...] = jnp.zeros_like(ref)
    ref[...] += jnp.sum(val, axis=0, keepdims=True)


def _ln_mod(x, sc, sh, name):
    s, d = x.shape
    tm = _row_tile(s)

    def body(x_ref, sc_ref, sh_ref, u_ref):
        xhat, _ = _ln_hat(x_ref[...])
        u_ref[...] = (xhat * (1.0 + sc_ref[...]) + sh_ref[...]).astype(BF16)

    row = pl.BlockSpec((tm, d), lambda i: (i, 0))
    vec = pl.BlockSpec((1, d), lambda i: (0, 0))
    return pl.pallas_call(body, grid=(s // tm,), in_specs=[row, vec, vec], out_specs=row,
                          out_shape=jax.ShapeDtypeStruct((s, d), BF16), name=name, compiler_params=_cparams())(x, sc, sh)


def _resid_bwd_tile(dxo, x, f, g, gam):
    rhat, rstd = _ln_hat(ALPHA * x + g * f)
    dr = _ln_hat_bwd(dxo * gam, rhat, rstd)
    return ALPHA * dr, g * dr, dxo * rhat, dr * f


def _mm_ln_mod_bwd(a, b, x, sc, dres, name, rider=None, nxt=None):
    segs = list(a) if isinstance(a, (list, tuple)) else [a]
    s = segs[0].shape[0]
    k, d = b.shape
    assert sum(t.shape[1] for t in segs) == k
    tm = min(512, s)
    ns = len(segs)

    def body(*refs):
        seg_refs = refs[:ns]
        if nxt is None:
            b_ref, x_ref, sc_ref, dres_ref, dx_ref, dsc_ref, dsh_ref = refs[ns:]
        else:
            (b_ref, x_ref, sc_ref, dres_ref, xp_ref, fp_ref, gp_ref, gamp_ref,
             dresp_ref, dfp_ref, dsc_ref, dsh_ref, dgam_ref, dbet_ref, dg_ref, dbias_ref) = refs[ns:]
        first = pl.program_id(0) == 0
        duv, r0 = None, 0
        for seg_ref in seg_refs:
            w = seg_ref.shape[1]
            part = jnp.dot(seg_ref[...], b_ref[r0:r0 + w, :], preferred_element_type=F32)
            duv = part if duv is None else duv + part
            r0 += w
        xhat, rstd = _ln_hat(x_ref[...])
        dxv = dres_ref[...] + _ln_hat_bwd(duv * (1.0 + sc_ref[...]), xhat, rstd)
        _acc_rows(dsc_ref, duv * xhat, first)
        _acc_rows(dsh_ref, duv, first)
        if nxt is None:
            dx_ref[...] = dxv
        else:
            dresp, dfp, t_gam, t_g = _resid_bwd_tile(dxv, xp_ref[...], fp_ref[...], gp_ref[...], gamp_ref[...])
            dresp_ref[...] = dresp
            dfp_ref[...] = dfp.astype(BF16)
            _acc_rows(dgam_ref, t_gam, first)
            _acc_rows(dbet_ref, dxv, first)
            _acc_rows(dg_ref, t_g, first)
            _acc_rows(dbias_ref, dfp, first)

    row = pl.BlockSpec((tm, d), lambda i: (i, 0))
    vec = pl.BlockSpec((1, d), lambda i: (0, 0))
    vs = jax.ShapeDtypeStruct((1, d), F32)
    rows = jax.ShapeDtypeStruct((s, d), F32)
    in_specs = [pl.BlockSpec((tm, t.shape[1]), lambda i: (i, 0)) for t in segs] + [_resident((k, d)), row, vec, row]
    args = (*segs, b, x, sc, dres)
    if nxt is None:
        out_specs, out_shape = [row, vec, vec], [rows, vs, vs]
    else:
        in_specs += [row, row, vec, vec]
        args += tuple(nxt)
        out_specs = [row, row] + [vec] * 6
        out_shape = [rows, jax.ShapeDtypeStruct((s, d), BF16)] + [vs] * 6
    res = _call(body, name=name, grid=(s // tm,), in_specs=in_specs, out_specs=out_specs, out_shape=out_shape,
                scratch_shapes=[], args=args, rider=rider)
    return tuple(res) if rider is None else (tuple(res[0]), res[1])


def _dw_segments(segs, u, name):
    s, d = u.shape
    tw = 256
    tiles = [t.shape[1] // tw for t in segs]
    starts = [sum(tiles[:j]) for j in range(len(segs))]
    ns = len(segs)

    def body(*refs):
        seg_refs, u_ref, o_ref = refs[:ns], refs[ns], refs[ns + 1]
        i = pl.program_id(0)
        for seg_ref, t0, nt in zip(seg_refs, starts, tiles):
            @pl.when((i >= t0) & (i < t0 + nt))
            def _(seg_ref=seg_ref):
                acc = lax.dot_general(seg_ref[...], u_ref[...], _DIMS["tn"], preferred_element_type=F32)
                o_ref[0] = acc[:, :d // 2].astype(BF16)
                o_ref[1] = acc[:, d // 2:].astype(BF16)

    def seg_spec(t0, nt):
        return pl.BlockSpec((s, tw), lambda i: (0, jnp.clip(i - t0, 0, nt - 1)))

    return pl.pallas_call(
        body, grid=(sum(tiles),), in_specs=[seg_spec(t0, nt) for t0, nt in zip(starts, tiles)] + [_full((s, d))],
        out_specs=pl.BlockSpec((2, tw, d // 2), lambda i: (0, i, 0)),
        out_shape=jax.ShapeDtypeStruct((2, sum(tiles) * tw, d // 2), BF16), name=name, compiler_params=_cparams(),
    )(*segs, u)


def _mm_resid_ln(a, b, bias, x, g, gam, bet, name, rider=None, mod_next=None):
    s, k = a.shape
    d = b.shape[1]
    tm = min(512, s)
    nb, nm = int(bias is not None), 2 * int(mod_next is not None)

    def body(*refs):
        a_ref, b_ref = refs[:2]
        x_ref, g_ref, gam_ref, bet_ref = refs[2 + nb:6 + nb]
        f_ref, o_ref = refs[6 + nb + nm:8 + nb + nm]
        f = jnp.dot(a_ref[...], b_ref[...], preferred_element_type=F32)
        if bias is not None:
            f = f + refs[2][...]
        f_ref[...] = f
        rhat, _ = _ln_hat(ALPHA * x_ref[...] + g_ref[...] * f)
        y = rhat * gam_ref[...] + bet_ref[...]
        o_ref[...] = y
        if mod_next is not None:
            sc_ref, sh_ref = refs[6 + nb:8 + nb]
            yhat, _ = _ln_hat(y)
            refs[8 + nb + nm][...] = (yhat * (1.0 + sc_ref[...]) + sh_ref[...]).astype(BF16)

    row = pl.BlockSpec((tm, d), lambda i: (i, 0))
    vec = pl.BlockSpec((1, d), lambda i: (0, 0))
    in_specs = [pl.BlockSpec((tm, k), lambda i: (i, 0)), _resident((k, d))] + [vec] * nb + [row, vec, vec, vec] + [vec] * nm
    args = [a, b] + ([bias] if nb else []) + [x, g, gam, bet] + (list(mod_next) if nm else [])
    sh = jax.ShapeDtypeStruct((s, d), F32)
    out_specs, out_shape = [row, row], [sh, sh]
    if nm:
        out_specs, out_shape = out_specs + [row], out_shape + [jax.ShapeDtypeStruct((s, d), BF16)]
    res = _call(body, name=name, grid=(s // tm,), in_specs=in_specs, out_specs=out_specs, out_shape=out_shape,
                scratch_shapes=[], args=args, rider=rider)
    return tuple(res) if rider is None else (tuple(res[0]), res[1])


def _resid_ln_bwd(dxo, x, f, g, gam, name, tgt=None):
    s, d = x.shape
    tm = _row_tile(s)
    n = s // tm

    def body(*refs):
        if tgt is None:
            dxo_ref, x_ref, f_ref, g_ref, gam_ref, dres_ref, df_ref, dgam_ref, dbet_ref, dg_ref, dbias_ref = refs
            dxov = dxo_ref[...]
        else:
            (dxo_ref, t_ref, x_ref, f_ref, g_ref, gam_ref, dres_ref, df_ref, dgam_ref, dbet_ref, dg_ref, dbias_ref,
             loss_ref, sq_ref) = refs
            err = dxo_ref[...] - t_ref[...]
            dxov = err * (1.0 / d)
            _acc_rows(sq_ref, err * err, pl.program_id(0) == 0)

            @pl.when(pl.program_id(0) == n - 1)
            def _():
                tot = jnp.sum(sq_ref[...], axis=1, keepdims=True) * (0.5 / d)
                loss_ref[...] = jnp.broadcast_to(tot, (1, 128))

        first = pl.program_id(0) == 0
        dres, dfv, t_gam, t_g = _resid_bwd_tile(dxov, x_ref[...], f_ref[...], g_ref[...], gam_ref[...])
        dres_ref[...] = dres
        df_ref[...] = dfv.astype(BF16)
        _acc_rows(dgam_ref, t_gam, first)
        _acc_rows(dbet_ref, dxov, first)
        _acc_rows(dg_ref, t_g, first)
        _acc_rows(dbias_ref, dfv, first)

    row = pl.BlockSpec((tm, d), lambda i: (i, 0))
    vec = pl.BlockSpec((1, d), lambda i: (0, 0))
    vs = jax.ShapeDtypeStruct((1, d), F32)
    out_specs = [row, row, vec, vec, vec, vec]
    out_shape = [jax.ShapeDtypeStruct((s, d), F32), jax.ShapeDtypeStruct((s, d), BF16), vs, vs, vs, vs]
    if tgt is None:
        return pl.pallas_call(body, grid=(n,), in_specs=[row, row, row, vec, vec], out_specs=out_specs,
                              out_shape=out_shape, name=name, compiler_params=_cparams())(dxo, x, f, g, gam)
    return pl.pallas_call(body, grid=(n,), in_specs=[row, row, row, row, vec, vec],
                          out_specs=out_specs + [pl.BlockSpec((1, 128), lambda i: (0, 0))],
                          out_shape=out_shape + [jax.ShapeDtypeStruct((1, 128), F32)],
                          scratch_shapes=[pltpu.VMEM((1, d), F32)], name=name,
                          compiler_params=_cparams())(dxo, tgt, x, f, g, gam)


POOL_HALO = 16
POOL_ROWS = 256


def _pool_counts(r0, rows):
    t1 = (lax.broadcasted_iota(jnp.int32, (rows, 128), 0) + r0 + 1).astype(F32)
    low = lax.broadcasted_iota(jnp.int32, (rows, 128), 1) < POOL_GROUP
    wa = jnp.where(low, float(POOL_WINDOWS[0]), float(POOL_WINDOWS[1]))
    wb = jnp.where(low, float(POOL_WINDOWS[2]), float(POOL_WINDOWS[3]))
    return jnp.minimum(t1, wa), jnp.minimum(t1, wb), low


def _window_sums(win, off, rows, sign):
    def sl(j, half):
        return win[off + sign * j: off + sign * j + rows, 128 * half:128 * half + 128]
    a2 = sl(0, 0) + sl(1, 0)
    a4 = a2 + sl(2, 0) + sl(3, 0)
    a8 = sl(0, 1)
    for j in range(1, 8):
        a8 = a8 + sl(j, 1)
    a16 = a8
    for j in range(8, 16):
        a16 = a16 + sl(j, 1)
    return a2, a4, a8, a16


def _pool_fwd(zp, wp_bd, pscale, name):
    s = zp.shape[0]
    r = min(POOL_ROWS, s)

    def body(z_ref, wp_ref, sc_ref, p_ref, feat_ref, pad):
        pad[0:POOL_HALO, :] = jnp.zeros((POOL_HALO, D_POOL), F32)
        pad[POOL_HALO:, :] = z_ref[...]

        def step(i, carry):
            r0 = pl.multiple_of(i * r, r)
            win = pad[pl.ds(r0, r + POOL_HALO), :]
            a2, a4, a8, a16 = _window_sums(win, POOL_HALO, r, -1)
            ca, cb, low = _pool_counts(r0, r)
            x0 = win[POOL_HALO:, :]
            pa = jnp.where(low, a2, a4) / ca
            pb = jnp.where(low, a8, a16) / cb
            p = (jnp.concatenate([pa, pb], axis=1) - x0).astype(BF16)
            p_ref[pl.ds(r0, r), :] = p
            pw = jnp.dot(p, wp_ref[...], preferred_element_type=F32)
            feat_ref[pl.ds(r0, r), :] = (pw * sc_ref[...]).astype(BF16)
            return carry

        lax.fori_loop(0, s // r, step, 0)

    return pl.pallas_call(
        body, out_shape=[jax.ShapeDtypeStruct((s, D_POOL), BF16), jax.ShapeDtypeStruct((s, D_POOL), BF16)],
        scratch_shapes=[pltpu.VMEM((s + POOL_HALO, D_POOL), F32)], name=name, compiler_params=_cparams(),
    )(zp, wp_bd, pscale)


def _pool_bwd(dfeat, p, wp_bd, pscale, name):
    s = p.shape[0]
    r = min(POOL_ROWS, s)

    def body(df_ref, p_ref, wp_ref, sc_ref, dz_ref, dwp_ref, dsc_ref, gpad, dpbuf):
        dwp_ref[...] = jnp.zeros_like(dwp_ref)
        dsc_ref[...] = jnp.zeros_like(dsc_ref)
        gpad[s:, :] = jnp.zeros((POOL_HALO, D_POOL), F32)

        def step1(i, carry):
            r0 = pl.multiple_of(i * r, r)
            pv = p_ref[pl.ds(r0, r), :]
            dfv = df_ref[pl.ds(r0, r), :]
            pw = jnp.dot(pv, wp_ref[...], preferred_element_type=F32)
            dsc_ref[...] += jnp.sum(dfv * pw, axis=0, keepdims=True)
            dpw = (dfv * sc_ref[...]).astype(BF16)
            dwp_ref[...] += lax.dot_general(pv, dpw, _DIMS["tn"], preferred_element_type=F32)
            dp = lax.dot_general(dpw, wp_ref[...], _DIMS["nt"], preferred_element_type=F32)
            ca, cb, _ = _pool_counts(r0, r)
            gpad[pl.ds(r0, r), :] = dp / jnp.concatenate([ca, cb], axis=1)
            dpbuf[pl.ds(r0, r), :] = dp
            return carry

        lax.fori_loop(0, s // r, step1, 0)

        def step2(i, carry):
            r0 = pl.multiple_of(i * r, r)
            win = gpad[pl.ds(r0, r + POOL_HALO), :]
            a2, a4, a8, a16 = _window_sums(win, 0, r, 1)
            low = lax.broadcasted_iota(jnp.int32, (r, 128), 1) < POOL_GROUP
            acc = jnp.concatenate([jnp.where(low, a2, a4), jnp.where(low, a8, a16)], axis=1)
            dz_ref[pl.ds(r0, r), :] = (acc - dpbuf[pl.ds(r0, r), :]).astype(BF16)
            return carry

        lax.fori_loop(0, s // r, step2, 0)

    return pl.pallas_call(
        body,
        out_shape=[jax.ShapeDtypeStruct((s, D_POOL), BF16), jax.ShapeDtypeStruct((D_POOL, D_POOL), F32),
                   jax.ShapeDtypeStruct((1, D_POOL), F32)],
        scratch_shapes=[pltpu.VMEM((s + POOL_HALO, D_POOL), F32), pltpu.VMEM((s, D_POOL), F32)],
        name=name, compiler_params=_cparams(),
    )(dfeat, p, wp_bd, pscale)


def _skew_index():
    cp = lax.broadcasted_iota(jnp.int32, (SKEW_W, N_REL), 0)
    dist = jnp.where(cp < KW, KPAD - cp, KPAD + SKEW_W - cp)
    idx = jnp.clip(dist, -REL_CLIP, REL_CLIP) + REL_CLIP
    return (idx == lax.broadcasted_iota(jnp.int32, (SKEW_W, N_REL), 1)).astype(F32)


def _row_bits(b):
    return (lax.broadcasted_iota(jnp.int32, (QB, SKEW_W), 0) >> b) & 1 == 1


N_EDGE = KPAD // QB


def _bias_block(rel_bias, name):
    def body(rb_ref, o_ref):
        onehot = _skew_index()
        row0 = lax.dot_general(rb_ref[...], onehot, _DIMS["nt"], precision=lax.Precision.HIGHEST,
                               preferred_element_type=F32)
        r = lax.broadcasted_iota(jnp.int32, (QB, KW), 0)
        kk = lax.broadcasted_iota(jnp.int32, (QB, KW), 1)
        cq, ck = r // CHUNK, kk // CHUNK
        band = (ck >= cq) & (ck <= cq + N_PREV_CHUNKS)
        for h in range(N_HEADS):
            t = jnp.broadcast_to(row0[h:h + 1, :], (QB, SKEW_W))
            for b in range(7):
                t = jnp.where(_row_bits(b), pltpu.roll(t, 1 << b, 1), t)
            for e in range(N_EDGE + 1):
                o_ref[e, h] = jnp.where(band & (kk >= KPAD - e * QB), t[:, :KW], NEG_INF)

    return pl.pallas_call(body, out_shape=jax.ShapeDtypeStruct((N_EDGE + 1, N_HEADS, QB, KW), F32), name=name,
                          compiler_params=_cparams())(rel_bias)


def _bias_spec():
    return pl.BlockSpec((None, N_HEADS, QB, KW), lambda i: (jnp.minimum(i, N_EDGE), 0, 0, 0))


def _bias_block_bwd(ds_acc, name):
    def body(ds_ref, o_ref):
        sums = []
        for h in range(N_HEADS):
            t = jnp.concatenate([ds_ref[h], jnp.zeros((QB, SKEW_W - KW), F32)], axis=1)
            for b in range(7):
                t = jnp.where(_row_bits(b), pltpu.roll(t, SKEW_W - (1 << b), 1), t)
            sums.append(jnp.sum(t, axis=0, keepdims=True))
        allh = jnp.concatenate(sums, axis=0)
        o_ref[...] = jnp.dot(allh, _skew_index(), precision=lax.Precision.HIGHEST, preferred_element_type=F32)

    return pl.pallas_call(body, out_shape=jax.ShapeDtypeStruct((N_HEADS, N_REL), F32), name=name,
                          compiler_params=_cparams())(ds_acc)


def _scaled(q):
    return (q.astype(F32) * (HEAD_DIM ** -0.5)).astype(BF16)


def _probs(q, kw, bias_ref):
    sc = jnp.stack([lax.dot_general(q[:, HEAD_DIM * h:HEAD_DIM * (h + 1)], kw[:, HEAD_DIM * h:HEAD_DIM * (h + 1)],
                                    _DIMS["nt"], preferred_element_type=F32) + bias_ref[h] for h in range(N_HEADS)])
    e = jnp.exp(sc - jnp.max(sc, axis=-1, keepdims=True))
    return e * (1.0 / jnp.sum(e, axis=-1, keepdims=True))


def _load_padded_kv(qkv_hbm, kpad, vpad, sems, s):
    kpad[0:KPAD, :] = jnp.zeros((KPAD, D_ATTN), BF16)
    vpad[0:KPAD, :] = jnp.zeros((KPAD, D_ATTN), BF16)
    ck = pltpu.make_async_copy(qkv_hbm.at[:, D_ATTN:2 * D_ATTN], kpad.at[pl.ds(KPAD, s), :], sems.at[0])
    cv = pltpu.make_async_copy(qkv_hbm.at[:, 2 * D_ATTN:3 * D_ATTN], vpad.at[pl.ds(KPAD, s), :], sems.at[1])
    ck.start()
    cv.start()
    ck.wait()
    cv.wait()


def _attn_fwd(qkv, bias, name, rider=None):
    s = qkv.shape[0]

    def body(q_ref, qkv_hbm, bias_ref, o_ref, p_ref, kpad, vpad, sems):
        i = pl.program_id(0)

        @pl.when(i == 0)
        def _():
            _load_padded_kv(qkv_hbm, kpad, vpad, sems, s)

        base = pl.multiple_of(i * QB, QB)
        kw = kpad[pl.ds(base, KW), :]
        vw = vpad[pl.ds(base, KW), :]
        q = _scaled(q_ref[...])
        p = _probs(q, kw, bias_ref).astype(BF16)
        p_ref[...] = p
        outs = [jnp.dot(p[h], vw[:, HEAD_DIM * h:HEAD_DIM * (h + 1)], preferred_element_type=F32)
                for h in range(N_HEADS)]
        o_ref[...] = jnp.concatenate(outs, axis=1).astype(BF16)

    res = _call(
        body, name=name, grid=(s // QB,),
        in_specs=[pl.BlockSpec((QB, D_ATTN), lambda i: (i, 0)), pl.BlockSpec(memory_space=pl.ANY),
                  _bias_spec()],
        out_specs=[pl.BlockSpec((QB, D_ATTN), lambda i: (i, 0)), _probs_spec()],
        out_shape=[jax.ShapeDtypeStruct((s, D_ATTN), BF16), jax.ShapeDtypeStruct((N_HEADS, s, KW), BF16)],
        scratch_shapes=[pltpu.VMEM((s + KPAD, D_ATTN), BF16), pltpu.VMEM((s + KPAD, D_ATTN), BF16),
                        pltpu.SemaphoreType.DMA((2,))],
        args=(qkv, qkv, bias), rider=rider)
    return tuple(res) if rider is None else (tuple(res[0]), res[1])


def _probs_spec():
    return pl.BlockSpec((N_HEADS, QB, KW), lambda i: (0, i, 0))


def _attn_bwd(qkv, do, probs, name, rider=None):
    s = qkv.shape[0]
    n = s // QB

    def body(q_ref, qkv_hbm, do_ref, p_ref, dq_ref, dk_hbm, dv_hbm, ds_ref, kpad, vpad, dkacc, dvacc, sems):
        i = pl.program_id(0)

        @pl.when(i == 0)
        def _():
            _load_padded_kv(qkv_hbm, kpad, vpad, sems, s)
            dkacc[...] = jnp.zeros_like(dkacc)
            dvacc[...] = jnp.zeros_like(dvacc)
            ds_ref[...] = jnp.zeros_like(ds_ref)

        base = pl.multiple_of(i * QB, QB)
        kw = kpad[pl.ds(base, KW), :]
        vw = vpad[pl.ds(base, KW), :]
        q = _scaled(q_ref[...])
        dov = do_ref[...]
        heads = [slice(HEAD_DIM * h, HEAD_DIM * (h + 1)) for h in range(N_HEADS)]
        pb = p_ref[...]
        p = pb.astype(F32)
        dp = jnp.stack([lax.dot_general(dov[:, hs], vw[:, hs], _DIMS["nt"], preferred_element_type=F32) for hs in heads])
        ds = p * (dp - jnp.sum(dp * p, axis=-1, keepdims=True))
        ds_ref[...] += ds
        dsb = ds.astype(BF16)
        dvs = [lax.dot_general(pb[h], dov[:, hs], _DIMS["tn"], preferred_element_type=F32) for h, hs in enumerate(heads)]
        dqs = [jnp.dot(dsb[h], kw[:, hs], preferred_element_type=F32) for h, hs in enumerate(heads)]
        dks = [lax.dot_general(dsb[h], q[:, hs], _DIMS["tn"], preferred_element_type=F32) for h, hs in enumerate(heads)]
        dq_ref[...] = (jnp.concatenate(dqs, axis=1) * (HEAD_DIM ** -0.5)).astype(BF16)
        dkacc[pl.ds(base, KW), :] += jnp.concatenate(dks, axis=1)
        dvacc[pl.ds(base, KW), :] += jnp.concatenate(dvs, axis=1)

        @pl.when(i == n - 1)
        def _():
            def cast(j, carry):
                rows = pl.ds(pl.multiple_of(KPAD + j * 512, 512), 512)
                kpad[rows, :] = dkacc[rows, :].astype(BF16)
                vpad[rows, :] = dvacc[rows, :].astype(BF16)
                return carry

            lax.fori_loop(0, s // 512, cast, 0)
            ck = pltpu.make_async_copy(kpad.at[pl.ds(KPAD, s), :], dk_hbm, sems.at[0])
            cv = pltpu.make_async_copy(vpad.at[pl.ds(KPAD, s), :], dv_hbm, sems.at[1])
            ck.start()
            cv.start()
            ck.wait()
            cv.wait()

    blk = pl.BlockSpec((QB, D_ATTN), lambda i: (i, 0))
    acc_shape = jax.ShapeDtypeStruct((s, D_ATTN), BF16)
    return _call(
        body, name=name, grid=(n,),
        in_specs=[blk, pl.BlockSpec(memory_space=pl.ANY), blk, _probs_spec()],
        out_specs=[blk, pl.BlockSpec(memory_space=pl.ANY), pl.BlockSpec(memory_space=pl.ANY), _full((N_HEADS, QB, KW))],
        out_shape=[jax.ShapeDtypeStruct((s, D_ATTN), BF16), acc_shape, acc_shape,
                   jax.ShapeDtypeStruct((N_HEADS, QB, KW), F32)],
        scratch_shapes=[pltpu.VMEM((s + KPAD, D_ATTN), BF16), pltpu.VMEM((s + KPAD, D_ATTN), BF16),
                        pltpu.VMEM((s + KPAD, D_ATTN), F32), pltpu.VMEM((s + KPAD, D_ATTN), F32),
                        pltpu.SemaphoreType.DMA((2,))],
        args=(qkv, qkv, do, probs), rider=rider)


CONV_HALO = 32
CONV_ROWS = 64


def _sigmoid(t):
    return 1.0 / (1.0 + jnp.exp(-t))


CONV_WIN = CONV_ROWS + CONV_HALO - 8


def _row_windows(ref, r0, buf):
    win = ref[pl.ds(r0, CONV_ROWS + CONV_HALO), :]
    for j in range(1, 8):
        buf[j - 1] = win[j:j + CONV_WIN, :]

    def get(o):
        j, a = o % 8, o - o % 8
        if j == 0:
            return ref[pl.ds(r0 + a, CONV_ROWS), :]
        return buf[j - 1, a:a + CONV_ROWS, :]

    return get


def _glu_rows(z_ref, r0, rows):
    a = z_ref[pl.ds(r0, rows), 0:D_CONV]
    b = z_ref[pl.ds(r0, rows), D_CONV:2 * D_CONV]
    return a, _sigmoid(b)


def _conv_fwd(zc, conv_w, conv_b, ln_g, ln_b, name):
    s = zc.shape[0]
    rt = min(256, s)

    def body(z_ref, w_ref, cb_ref, g_ref, b_ref, cv_ref, feat_ref, hpad, shifts):
        hpad[0:CONV_HALO, :] = jnp.zeros((CONV_HALO, D_CONV), F32)

        def glu(i, carry):
            r0 = pl.multiple_of(i * rt, rt)
            a, sb = _glu_rows(z_ref, r0, rt)
            hpad[pl.ds(r0 + CONV_HALO, rt), :] = a * sb
            return carry

        lax.fori_loop(0, s // rt, glu, 0)
        w = w_ref[...]

        def conv(i, carry):
            r0 = pl.multiple_of(i * CONV_ROWS, CONV_ROWS)
            win = _row_windows(hpad, r0, shifts)
            acc = jnp.broadcast_to(cb_ref[...], (CONV_ROWS, D_CONV))
            for k in range(CONV_WIDTH):
                acc = acc + win(2 + k) * w[k:k + 1, :]
            cv_ref[pl.ds(r0, CONV_ROWS), :] = acc
            yhat, _ = _ln_hat(acc)
            y = yhat * g_ref[...] + b_ref[...]
            feat_ref[pl.ds(r0, CONV_ROWS), :] = (y * _sigmoid(y)).astype(BF16)
            return carry

        lax.fori_loop(0, s // CONV_ROWS, conv, 0)

    return pl.pallas_call(
        body, out_shape=[jax.ShapeDtypeStruct((s, D_CONV), F32), jax.ShapeDtypeStruct((s, D_CONV), BF16)],
        scratch_shapes=[pltpu.VMEM((s + CONV_HALO, D_CONV), F32), pltpu.VMEM((7, CONV_WIN, D_CONV), F32)],
        name=name, compiler_params=_cparams(),
    )(zc, conv_w, conv_b, ln_g, ln_b)


def _conv_bwd(dfeat, cv, zc, conv_w, ln_g, ln_b, name):
    s = zc.shape[0]
    rt = min(256, s)

    def body(df_ref, cv_ref, z_ref, w_ref, g_ref, b_ref, dz_ref, dw_ref, dcb_ref, dg_ref, db_ref, hpad, dcvpad, dwacc,
             hshifts, dshifts):
        hpad[0:CONV_HALO, :] = jnp.zeros((CONV_HALO, D_CONV), F32)
        dcvpad[s:, :] = jnp.zeros((CONV_HALO, D_CONV), F32)
        dwacc[...] = jnp.zeros_like(dwacc)
        dcb_ref[...] = jnp.zeros_like(dcb_ref)
        dg_ref[...] = jnp.zeros_like(dg_ref)
        db_ref[...] = jnp.zeros_like(db_ref)

        def pass1(i, carry):
            r0 = pl.multiple_of(i * rt, rt)
            a, sb = _glu_rows(z_ref, r0, rt)
            hpad[pl.ds(r0 + CONV_HALO, rt), :] = a * sb
            cvhat, rstd = _ln_hat(cv_ref[pl.ds(r0, rt), :])
            y = cvhat * g_ref[...] + b_ref[...]
            sg = _sigmoid(y)
            dy = df_ref[pl.ds(r0, rt), :] * (sg * (1.0 + y * (1.0 - sg)))
            dg_ref[...] += jnp.sum(dy * cvhat, axis=0, keepdims=True)
            db_ref[...] += jnp.sum(dy, axis=0, keepdims=True)
            dcv = _ln_hat_bwd(dy * g_ref[...], cvhat, rstd)
            dcb_ref[...] += jnp.sum(dcv, axis=0, keepdims=True)
            dcvpad[pl.ds(r0, rt), :] = dcv
            return carry

        lax.fori_loop(0, s // rt, pass1, 0)
        w = w_ref[...]

        def pass2(i, carry):
            r0 = pl.multiple_of(i * CONV_ROWS, CONV_ROWS)
            dwin = _row_windows(dcvpad, r0, dshifts)
            hwin = _row_windows(hpad, r0, hshifts)
            dcv = dwin(0)
            dh = jnp.zeros((CONV_ROWS, D_CONV), F32)
            for k in range(CONV_WIDTH):
                dh = dh + dwin(30 - k) * w[k:k + 1, :]
                prod = dcv * hwin(2 + k)
                dwacc[8 * k:8 * k + 8, :] += jnp.sum(prod.reshape(CONV_ROWS // 8, 8, D_CONV), axis=0)
            a, sb = _glu_rows(z_ref, r0, CONV_ROWS)
            dz_ref[pl.ds(r0, CONV_ROWS), :] = jnp.concatenate([dh * sb, dh * a * sb * (1.0 - sb)], axis=1).astype(BF16)
            return carry

        lax.fori_loop(0, s // CONV_ROWS, pass2, 0)
        dw_ref[...] = jnp.sum(dwacc[...].reshape(32, 8, D_CONV), axis=1)

    vs = jax.ShapeDtypeStruct((1, D_CONV), F32)
    return pl.pallas_call(
        body,
        out_shape=[jax.ShapeDtypeStruct((s, 2 * D_CONV), BF16), jax.ShapeDtypeStruct((32, D_CONV), F32), vs, vs, vs],
        scratch_shapes=[pltpu.VMEM((s + CONV_HALO, D_CONV), F32), pltpu.VMEM((s + CONV_HALO, D_CONV), F32),
                        pltpu.VMEM((256, D_CONV), F32), pltpu.VMEM((7, CONV_WIN, D_CONV), F32),
                        pltpu.VMEM((7, CONV_WIN, D_CONV), F32)],
        name=name, compiler_params=_cparams(),
    )(dfeat, cv, zc, conv_w, ln_g, ln_b)


def _branch_out(feats, wts, name):
    s = feats[0].shape[0]
    tm = min(1024, s)

    def body(*refs):
        for f_ref, w_ref, o_ref in zip(refs[:3], refs[3:6], refs[6:]):
            o_ref[...] = lax.dot_general(f_ref[...], w_ref[...], _DIMS["nt"], preferred_element_type=F32).astype(BF16)

    row = pl.BlockSpec((tm, D_MODEL), lambda i: (i, 0))
    sh = jax.ShapeDtypeStruct((s, D_MODEL), BF16)
    return pl.pallas_call(
        body, grid=(s // tm,),
        in_specs=[pl.BlockSpec((tm, f.shape[1]), lambda i: (i, 0)) for f in feats] + [_full(w.shape) for w in wts],
        out_specs=[row] * 3, out_shape=[sh] * 3, name=name, compiler_params=_cparams(),
    )(*feats, *wts)


def _branch_in_bwd(dys, wts, out_dtypes, name):
    s = dys[0].shape[0]
    tm = min(1024, s)

    def body(*refs):
        for d_ref, w_ref, o_ref in zip(refs[:3], refs[3:6], refs[6:]):
            o_ref[...] = jnp.dot(d_ref[...], w_ref[...], preferred_element_type=F32).astype(o_ref.dtype)

    row = pl.BlockSpec((tm, D_MODEL), lambda i: (i, 0))
    return pl.pallas_call(
        body, grid=(s // tm,), in_specs=[row] * 3 + [_full(w.shape) for w in wts],
        out_specs=[pl.BlockSpec((tm, w.shape[1]), lambda i: (i, 0)) for w in wts],
        out_shape=[jax.ShapeDtypeStruct((s, w.shape[1]), dt) for w, dt in zip(wts, out_dtypes)],
        name=name, compiler_params=_cparams(),
    )(*dys, *wts)


def _branch_dw(dys, feats, name):
    s = dys[0].shape[0]
    tm = 512

    def body(*refs):
        for d_ref, f_ref, o_ref in zip(refs[:3], refs[3:6], refs[6:]):
            acc = lax.dot_general(d_ref[...], f_ref[...], _DIMS["tn"], preferred_element_type=F32)
            half = acc.shape[1] // 2
            o_ref[0] = acc[:, :half].astype(BF16)
            o_ref[1] = acc[:, half:].astype(BF16)

    return pl.pallas_call(
        body, grid=(D_MODEL // tm,),
        in_specs=[pl.BlockSpec((s, tm), lambda i: (0, i))] * 3 + [_full(f.shape) for f in feats],
        out_specs=[pl.BlockSpec((2, tm, f.shape[1] // 2), lambda i: (0, i, 0)) for f in feats],
        out_shape=[jax.ShapeDtypeStruct((2, D_MODEL, f.shape[1] // 2), BF16) for f in feats],
        name=name, compiler_params=_cparams(),
    )(*dys, *feats)


def _merge(zg, b_gate, ys, name):
    s = zg.shape[0]
    tm = _row_tile(s)

    def body(zg_ref, bg_ref, y0_ref, y1_ref, y2_ref, o_ref):
        acc = None
        for j, y_ref in enumerate((y0_ref, y1_ref, y2_ref)):
            cs = slice(D_MODEL * j, D_MODEL * (j + 1))
            t = _sigmoid(zg_ref[:, cs] + bg_ref[:, cs]) * y_ref[...]
            acc = t if acc is None else acc + t
        o_ref[...] = acc.astype(BF16)

    row = pl.BlockSpec((tm, D_MODEL), lambda i: (i, 0))
    return pl.pallas_call(
        body, grid=(s // tm,),
        in_specs=[pl.BlockSpec((tm, 3 * D_MODEL), lambda i: (i, 0)), _full((1, 3 * D_MODEL)), row, row, row],
        out_specs=row, out_shape=jax.ShapeDtypeStruct((s, D_MODEL), BF16), name=name, compiler_params=_cparams(),
    )(zg, b_gate, *ys)


def _merge_bwd(dmix, w_o, zg, b_gate, ys, name):
    s = zg.shape[0]
    tm = min(256, s)

    def body(dmix_ref, wo_ref, zg_ref, bg_ref, y0_ref, y1_ref, y2_ref, d0_ref, d1_ref, d2_ref, dzg_ref, dbg_ref):
        first = pl.program_id(0) == 0

        @pl.when(first)
        def _():
            dbg_ref[...] = jnp.zeros_like(dbg_ref)

        dmv = lax.dot_general(dmix_ref[...], wo_ref[...], _DIMS["nt"], preferred_element_type=F32)
        for j, (y_ref, d_ref) in enumerate(((y0_ref, d0_ref), (y1_ref, d1_ref), (y2_ref, d2_ref))):
            cs = slice(D_MODEL * j, D_MODEL * (j + 1))
            g = _sigmoid(zg_ref[:, cs] + bg_ref[:, cs])
            d_ref[...] = (dmv * g).astype(BF16)
            dzg = dmv * y_ref[...] * g * (1.0 - g)
            dzg_ref[:, cs] = dzg.astype(BF16)
            dbg_ref[:, cs] += jnp.sum(dzg, axis=0, keepdims=True)

    row = pl.BlockSpec((tm, D_MODEL), lambda i: (i, 0))
    wide = pl.BlockSpec((tm, 3 * D_MODEL), lambda i: (i, 0))
    yb = jax.ShapeDtypeStruct((s, D_MODEL), BF16)
    return pl.pallas_call(
        body, grid=(s // tm,),
        in_specs=[row, _full(w_o.shape), wide, _full((1, 3 * D_MODEL)), row, row, row],
        out_specs=[row, row, row, wide, _full((1, 3 * D_MODEL))],
        out_shape=[yb, yb, yb, jax.ShapeDtypeStruct((s, 3 * D_MODEL), BF16), jax.ShapeDtypeStruct((1, 3 * D_MODEL), F32)],
        name=name, compiler_params=_cparams(),
    )(dmix, w_o, zg, b_gate, *ys)


def _ff_hidden(u2, w_ff1t, b_ff1, name, rider=None):
    s = u2.shape[0]
    tm, tn = min(2048, s), 1024

    def body(a_ref, b_ref, bias_ref, pre_ref, h_ref):
        acc = lax.dot_general(a_ref[...], b_ref[...], _DIMS["nt"], preferred_element_type=F32) + bias_ref[...]
        pre_ref[...] = acc.astype(BF16)
        h_ref[...] = _relu2(acc).astype(BF16)

    blk = pl.BlockSpec((tm, tn), lambda i, j: (i, j))
    sh = jax.ShapeDtypeStruct((s, D_FF), BF16)
    res = _call(body, name=name, grid=(s // tm, D_FF // tn),
                in_specs=[pl.BlockSpec((tm, D_MODEL), lambda i, j: (i, 0)), pl.BlockSpec((tn, D_MODEL), lambda i, j: (j, 0)),
                          pl.BlockSpec((1, tn), lambda i, j: (0, j))],
                out_specs=[blk, blk], out_shape=[sh, sh], scratch_shapes=[], args=(u2, w_ff1t, b_ff1), rider=rider)
    return tuple(res) if rider is None else (tuple(res[0]), res[1])


def _ff_hidden_bwd(dff, w_ff2, hpre, name, rider=None):
    s = dff.shape[0]
    tm, tn = min(1024, s), 1024

    def body(a_ref, b_ref, h_ref, o_ref, sum_ref):
        dh = lax.dot_general(a_ref[...], b_ref[...], _DIMS["nt"], preferred_element_type=F32)
        dpre = dh * (2.0 * jnp.maximum(h_ref[...].astype(F32), 0.0))
        o_ref[...] = dpre.astype(BF16)
        _acc_rows(sum_ref, dpre, pl.program_id(1) == 0)

    res = _call(
        body, name=name, grid=(D_FF // tn, s // tm),
        in_specs=[pl.BlockSpec((tm, D_MODEL), lambda j, i: (i, 0)), pl.BlockSpec((tn, D_MODEL), lambda j, i: (j, 0)),
                  pl.BlockSpec((tm, tn), lambda j, i: (i, j))],
        out_specs=[pl.BlockSpec((tm, tn), lambda j, i: (i, j)), pl.BlockSpec((1, tn), lambda j, i: (0, j))],
        out_shape=[jax.ShapeDtypeStruct((s, D_FF), BF16), jax.ShapeDtypeStruct((1, D_FF), F32)],
        scratch_shapes=[], args=(dff, w_ff2, hpre), rider=rider)
    return tuple(res) if rider is None else (tuple(res[0]), res[1])


def _silu(t):
    return t * _sigmoid(t)


def _mod_fwd(c_all, w_ada_sh, b_ada_sh, name):
    cols = w_ada_sh.shape[2]

    def body(c_ref, w_ref, b_ref, o_ref):
        ca = _silu(c_ref[...]).astype(BF16)
        o_ref[0] = jnp.dot(ca, w_ref[0].astype(BF16), preferred_element_type=F32) + b_ref[0]

    return pl.pallas_call(
        body, grid=(DEPTH,),
        in_specs=[_full((N_DEV, D_MODEL)), pl.BlockSpec((1, D_MODEL, cols), lambda l: (l, 0, 0)),
                  pl.BlockSpec((1, 1, cols), lambda l: (l, 0, 0))],
        out_specs=pl.BlockSpec((1, N_DEV, cols), lambda l: (l, 0, 0)),
        out_shape=jax.ShapeDtypeStruct((DEPTH, N_DEV, cols), F32), name=name, compiler_params=_cparams(),
    )(c_all, w_ada_sh, b_ada_sh)


def _mod_bwd(c_all, dmod_sh, name):
    cols = dmod_sh.shape[2]

    def body(c_ref, d_ref, o_ref):
        ca = _silu(c_ref[...])
        o_ref[0] = lax.dot_general(ca, d_ref[0], _DIMS["tn"], precision=lax.Precision.HIGHEST,
                                   preferred_element_type=F32)

    return pl.pallas_call(
        body, grid=(DEPTH,),
        in_specs=[_full((N_DEV, D_MODEL)), pl.BlockSpec((1, N_DEV, cols), lambda l: (l, 0, 0))],
        out_specs=pl.BlockSpec((1, D_MODEL, cols), lambda l: (l, 0, 0)),
        out_shape=jax.ShapeDtypeStruct((DEPTH, D_MODEL, cols), F32), name=name, compiler_params=_cparams(),
    )(c_all, dmod_sh)


def _flat_tiles(rows, cols, itemsize_total):
    budget = 12 * 1024 * 1024
    tr = rows
    while tr % 32 == 0 and tr * cols * itemsize_total > budget:
        tr //= 2
    return tr


def _sum_cores(dws, recvs, place, name):
    k = len(dws)

    def body(place_ref, *refs):
        for a_ref, b_ref, o_ref in zip(refs[:k], refs[k:2 * k], refs[2 * k:]):
            o_ref[...] = (a_ref[...].astype(F32) + b_ref[...].astype(F32)).astype(BF16)

    whole = [pl.BlockSpec(a.shape[1:], lambda i, pr: (0, 0)) for a in dws]
    mine = [pl.BlockSpec((None,) + a.shape[1:], lambda i, pr: (pr[0], 0, 0)) for a in dws]
    grid_spec = pltpu.PrefetchScalarGridSpec(num_scalar_prefetch=1, grid=(1,), in_specs=mine + whole, out_specs=whole)
    return pl.pallas_call(body, grid_spec=grid_spec, out_shape=[jax.ShapeDtypeStruct(a.shape[1:], BF16) for a in dws],
                          name=name, compiler_params=_cparams())(place, *dws, *recvs)


def _sum_chips(hs, rs, place, name):
    k = len(hs)

    def body(place_ref, *refs):
        for h_ref, r_ref, o_ref in zip(refs[:k], refs[k:2 * k], refs[2 * k:]):
            o_ref[...] = ((h_ref[...].astype(F32) + r_ref[0].astype(F32)) + r_ref[1].astype(F32)) + r_ref[2].astype(F32)

    own = [pl.BlockSpec((None,) + h.shape[1:], lambda i, pr: (pr[1], 0, 0)) for h in hs]
    got = [pl.BlockSpec(r.shape, lambda i, pr: (0, 0, 0)) for r in rs]
    out = [pl.BlockSpec(h.shape[1:], lambda i, pr: (0, 0)) for h in hs]
    grid_spec = pltpu.PrefetchScalarGridSpec(num_scalar_prefetch=1, grid=(1,), in_specs=own + got, out_specs=out)
    return pl.pallas_call(body, grid_spec=grid_spec, out_shape=[jax.ShapeDtypeStruct(h.shape[1:], F32) for h in hs],
                          name=name, compiler_params=_cparams())(place, *hs, *rs)


def _adam_math(w, g, m, v):
    m2 = ADAM_B1 * m + (1.0 - ADAM_B1) * g
    v2 = ADAM_B2 * v + (1.0 - ADAM_B2) * (g * g)
    m_hat = m2 / (1.0 - ADAM_B1 ** ADAM_STEP)
    v_hat = v2 / (1.0 - ADAM_B2 ** ADAM_STEP)
    delta = -ADAM_LR * (m_hat / (jnp.sqrt(v_hat) + ADAM_EPS) + ADAM_WD * w)
    return delta, m2, v2


def _adamw(w, m, v, grads, name):
    r, c = w.shape
    tr = _flat_tiles(r, c, 4 * (7 + len(grads)))

    def body(*refs):
        w_ref, m_ref, v_ref = refs[:3]
        g_refs = refs[3:3 + len(grads)]
        g_ref, d_ref, m2_ref, v2_ref = refs[3 + len(grads):]
        g = g_refs[0][...]
        for gr in g_refs[1:]:
            g = g + gr[...]
        delta, m2, v2 = _adam_math(w_ref[...], g, m_ref[...], v_ref[...])
        g_ref[...] = g
        d_ref[...] = delta
        m2_ref[...] = m2
        v2_ref[...] = v2

    blk = pl.BlockSpec((tr, c), lambda i: (i, 0))
    sh = jax.ShapeDtypeStruct((r, c), F32)
    return pl.pallas_call(body, grid=(r // tr,), in_specs=[blk] * (3 + len(grads)), out_specs=[blk] * 4,
                          out_shape=[sh] * 4, name=name, compiler_params=_cparams())(w, m, v, *grads)


def _adamw_halves(w, m, v, own, other, place, split, name):
    nl, r, c = w.shape
    hr, hc = own[0].shape
    tr = _flat_tiles(hr, hc, 4 * (7 + 2 * nl))
    nt = hr // tr
    if split == "rows":
        w_spec = pl.BlockSpec((None, tr, c), lambda l, h, t, pr: (l, h * nt + t, 0))
    else:
        w_spec = pl.BlockSpec((None, tr, hc), lambda l, h, t, pr: (l, t, h))

    def g_spec(layer, mine):
        return pl.BlockSpec((tr, hc), lambda l, h, t, pr: (jnp.where((l == layer) & ((h == pr[0]) == mine), t, nt - 1), 0))

    def body(place_ref, w_ref, m_ref, v_ref, *refs):
        own_refs, other_refs = refs[:nl], refs[nl:2 * nl]
        g_ref, d_ref, m2_ref, v2_ref = refs[2 * nl:]
        layer = pl.program_id(0)
        mine = pl.program_id(1) == place_ref[0]
        g = None
        for li in range(nl):
            cand = jnp.where(mine, own_refs[li][...], other_refs[li][...])
            g = cand if g is None else jnp.where(layer == li, cand, g)
        delta, m2, v2 = _adam_math(w_ref[...], g, m_ref[...], v_ref[...])
        g_ref[...] = g
        d_ref[...] = delta
        m2_ref[...] = m2
        v2_ref[...] = v2

    sh = jax.ShapeDtypeStruct((nl, r, c), F32)
    g_specs = [g_spec(li, True) for li in range(nl)] + [g_spec(li, False) for li in range(nl)]
    return _call(body, name=name, grid=(nl, 2, nt), in_specs=[w_spec] * 3 + g_specs, out_specs=[w_spec] * 4,
                 out_shape=[sh] * 4, scratch_shapes=[], args=(w, m, v, *own, *other), prefetch=(place,))


def _adamw_small(w, m, v, g_all, name):
    r, c = w.shape

    def body(w_ref, m_ref, v_ref, g_ref, go_ref, d_ref, m2_ref, v2_ref):
        g = g_ref[0]
        for b in range(1, N_DEV):
            g = g + g_ref[b]
        delta, m2, v2 = _adam_math(w_ref[...], g, m_ref[...], v_ref[...])
        go_ref[...] = g
        d_ref[...] = delta
        m2_ref[...] = m2
        v2_ref[...] = v2

    sh = jax.ShapeDtypeStruct((r, c), F32)
    return pl.pallas_call(body, out_shape=[sh] * 4, name=name, compiler_params=_cparams())(w, m, v, g_all)


def _me():
    return lax.axis_index("x"), lax.axis_index("y"), lax.axis_index("c")


def _flip(v, bit):
    return 1 - v if bit else v


def _allgather_small(blk, name):
    r, c = blk.shape

    def body(x_ref, o_ref, send_sems, recv_sems):
        x, y, cc = _me()
        me = 4 * x + 2 * y + cc
        copies = []
        for k in range(1, N_DEV):
            peer = (_flip(x, k & 4), _flip(y, k & 2), _flip(cc, k & 1))
            cp = pltpu.make_async_remote_copy(src_ref=x_ref, dst_ref=o_ref.at[me], send_sem=send_sems.at[k - 1],
                                              recv_sem=recv_sems.at[k - 1], device_id=peer, device_id_type=MESH)
            cp.start()
            copies.append(cp)
        o_ref[me] = x_ref[...]
        for cp in copies:
            cp.wait()

    return pl.pallas_call(
        body, out_shape=jax.ShapeDtypeStruct((N_DEV, r, c), F32),
        in_specs=[pl.BlockSpec(memory_space=pltpu.VMEM)], out_specs=pl.BlockSpec(memory_space=pltpu.VMEM),
        scratch_shapes=[pltpu.SemaphoreType.DMA((N_DEV - 1,)), pltpu.SemaphoreType.DMA((N_DEV - 1,))],
        name=name, compiler_params=_cparams(),
    )(blk)


class _Rider:
    def __init__(self, arrays, out_shapes, scratch_shapes, start, finish):
        self.arrays, self.out_shapes, self.scratch_shapes = list(arrays), list(out_shapes), list(scratch_shapes)
        self.start, self.finish = start, finish


def _call(body, *, name, grid, in_specs, out_specs, out_shape, scratch_shapes, args, rider=None, prefetch=()):
    npf = len(prefetch)

    def launch(fn, in_specs, out_specs, out_shape, scratch_shapes, args):
        grid_spec = pltpu.PrefetchScalarGridSpec(num_scalar_prefetch=npf, grid=grid, in_specs=in_specs,
                                                 out_specs=out_specs, scratch_shapes=scratch_shapes)
        return pl.pallas_call(fn, grid_spec=grid_spec, out_shape=out_shape, name=name,
                              compiler_params=_cparams())(*prefetch, *args)

    if rider is None:
        return launch(body, list(in_specs), list(out_specs), list(out_shape), list(scratch_shapes), args)
    ni, no, ns = len(in_specs), len(out_specs), len(scratch_shapes)
    ri, ro = len(rider.arrays), len(rider.out_shapes)
    steps = int(np.prod(grid))

    def wrapped(*refs):
        pf, refs = refs[:npf], refs[npf:]
        h_in, r_in = refs[:ni], refs[ni:ni + ri]
        h_out, r_out = refs[ni + ri:ni + ri + no], refs[ni + ri + no:ni + ri + no + ro]
        h_scr, r_scr = refs[ni + ri + no + ro:ni + ri + no + ro + ns], refs[ni + ri + no + ro + ns:]
        step = pl.program_id(0)
        for d in range(1, len(grid)):
            step = step * grid[d] + pl.program_id(d)

        @pl.when(step == 0)
        def _():
            rider.start(r_in, r_out, r_scr)

        body(*pf, *h_in, *h_out, *h_scr)

        @pl.when(step == steps - 1)
        def _():
            rider.finish(r_in, r_out, r_scr)

    anyspec = pl.BlockSpec(memory_space=pl.ANY)
    res = launch(wrapped, list(in_specs) + [anyspec] * ri, list(out_specs) + [anyspec] * ro,
                 list(out_shape) + rider.out_shapes, list(scratch_shapes) + rider.scratch_shapes,
                 list(args) + rider.arrays)
    return res[:no], res[no:]


def _run_rider(rider, name):
    ri = len(rider.arrays)

    def body(*refs):
        r_in, r_out, r_scr = refs[:ri], refs[ri:ri + len(rider.out_shapes)], refs[ri + len(rider.out_shapes):]
        rider.start(r_in, r_out, r_scr)
        rider.finish(r_in, r_out, r_scr)

    anyspec = pl.BlockSpec(memory_space=pl.ANY)
    return pl.pallas_call(body, in_specs=[anyspec] * ri, out_specs=[anyspec] * len(rider.out_shapes),
                          out_shape=rider.out_shapes, scratch_shapes=rider.scratch_shapes, name=name,
                          compiler_params=_cparams())(*rider.arrays)


def _allgather_rider(blk):
    def copies(ins, outs, scr):
        send_sems, recv_sems, loc_sems, stage = scr
        x, y, cc = _me()
        me = 4 * x + 2 * y + cc
        remote = [pltpu.make_async_remote_copy(
            src_ref=ins[0], dst_ref=outs[0].at[me], send_sem=send_sems.at[k - 1], recv_sem=recv_sems.at[k - 1],
            device_id=(_flip(x, k & 4), _flip(y, k & 2), _flip(cc, k & 1)), device_id_type=MESH) for k in range(1, N_DEV)]
        return remote, pltpu.make_async_copy(ins[0], stage, loc_sems.at[0]), (outs[0].at[me], stage, loc_sems.at[1])

    def start(ins, outs, scr):
        remote, lin, _ = copies(ins, outs, scr)
        lin.start()
        for cp in remote:
            cp.start()

    def finish(ins, outs, scr):
        remote, lin, (dst, stage, sem) = copies(ins, outs, scr)
        lin.wait()
        lout = pltpu.make_async_copy(stage, dst, sem)
        lout.start()
        for cp in remote:
            cp.wait()
        lout.wait()

    return _Rider([blk], [jax.ShapeDtypeStruct((N_DEV,) + blk.shape, blk.dtype)],
                  [pltpu.SemaphoreType.DMA((N_DEV - 1,)), pltpu.SemaphoreType.DMA((N_DEV - 1,)),
                   pltpu.SemaphoreType.DMA((2,)), pltpu.VMEM(blk.shape, blk.dtype)], start, finish)


def _gather_rider(shards):
    n = len(shards)

    def copies(ins, outs, scr, relay=True):
        ici_send, ici_recv, d2d_send, d2d_recv, loc_sems = scr[:5]
        stage = scr[5:]
        x, y, cc = _me()
        chip = 2 * x + y
        sibling = (x, y, 1 - cc)
        local, sends, relays = [], [], []
        for j in range(n):
            def rows(ch, h, j=j):
                return outs[j].at[ch, h]

            lc = pltpu.make_async_copy(ins[j], stage[j], loc_sems.at[j])
            local.append((lc, pltpu.make_async_copy(stage[j], outs[j].at[chip], loc_sems.at[n + j]) if relay else None))
            for k in range(1, N_CHIP):
                px, py = _flip(x, k & 2), _flip(y, k & 1)
                pchip = 2 * px + py
                q = 3 * j + k - 1
                out_cp = pltpu.make_async_remote_copy(src_ref=ins[j].at[cc], dst_ref=rows(chip, cc),
                                                      send_sem=ici_send.at[q], recv_sem=ici_recv.at[q],
                                                      device_id=(px, py, cc), device_id_type=MESH)
                sends.append(out_cp)
                if not relay:
                    continue
                arrival = pltpu.make_async_remote_copy(src_ref=rows(pchip, cc), dst_ref=rows(pchip, cc),
                                                       send_sem=ici_send.at[q], recv_sem=ici_recv.at[q],
                                                       device_id=(px, py, cc), device_id_type=MESH)
                forward = pltpu.make_async_remote_copy(src_ref=rows(pchip, cc), dst_ref=rows(pchip, cc),
                                                       send_sem=d2d_send.at[q], recv_sem=d2d_recv.at[q],
                                                       device_id=sibling, device_id_type=MESH)
                from_sibling = pltpu.make_async_remote_copy(src_ref=rows(pchip, 1 - cc), dst_ref=rows(pchip, 1 - cc),
                                                            send_sem=d2d_send.at[q], recv_sem=d2d_recv.at[q],
                                                            device_id=sibling, device_id_type=MESH)
                relays.append((arrival, forward, from_sibling))
        return local, sends, relays

    def start(ins, outs, scr):
        local, sends, _ = copies(ins, outs, scr, relay=False)
        for lin, _ in local:
            lin.start()
        for cp in sends:
            cp.start()

    def finish(ins, outs, scr):
        local, sends, relays = copies(ins, outs, scr)
        for lin, lout in local:
            lin.wait()
            lout.start()
        for arrival, forward, _ in relays:
            arrival.wait_recv()
            forward.start()
        for cp in sends:
            cp.wait_send()
        for _, forward, from_sibling in relays:
            forward.wait_send()
            from_sibling.wait_recv()
        for _, lout in local:
            lout.wait()

    scratch = [pltpu.SemaphoreType.DMA((3 * n,)), pltpu.SemaphoreType.DMA((3 * n,)), pltpu.SemaphoreType.DMA((3 * n,)),
               pltpu.SemaphoreType.DMA((3 * n,)), pltpu.SemaphoreType.DMA((2 * n,))]
    scratch += [pltpu.VMEM(a.shape, a.dtype) for a in shards]
    return _Rider(shards, [jax.ShapeDtypeStruct((N_CHIP,) + a.shape, a.dtype) for a in shards], scratch, start, finish)


def _sibling_rider(arrs, other_half=False):
    n = len(arrs)

    def copies(ins, outs, scr):
        send_sems, recv_sems = scr
        x, y, cc = _me()
        return [pltpu.make_async_remote_copy(
            src_ref=ins[j].at[1 - cc] if other_half else ins[j], dst_ref=outs[j], send_sem=send_sems.at[j],
            recv_sem=recv_sems.at[j], device_id=(x, y, 1 - cc), device_id_type=MESH) for j in range(n)]

    def start(ins, outs, scr):
        for cp in copies(ins, outs, scr):
            cp.start()

    def finish(ins, outs, scr):
        for cp in copies(ins, outs, scr):
            cp.wait()

    return _Rider(arrs, [jax.ShapeDtypeStruct(a.shape[1:] if other_half else a.shape, a.dtype) for a in arrs],
                  [pltpu.SemaphoreType.DMA((n,)), pltpu.SemaphoreType.DMA((n,))], start, finish)


def _sibling_send(arrs, name, other_half=False):
    return _run_rider(_sibling_rider(arrs, other_half), name)


def _join_riders(first, second):
    ni, no, ns = len(first.arrays), len(first.out_shapes), len(first.scratch_shapes)

    def split(ins, outs, scr):
        return (ins[:ni], outs[:no], scr[:ns]), (ins[ni:], outs[no:], scr[ns:])

    def start(ins, outs, scr):
        a, b = split(ins, outs, scr)
        first.start(*a)
        second.start(*b)

    def finish(ins, outs, scr):
        a, b = split(ins, outs, scr)
        first.finish(*a)
        second.finish(*b)

    return _Rider(first.arrays + second.arrays, first.out_shapes + second.out_shapes,
                  first.scratch_shapes + second.scratch_shapes, start, finish)


def _scatter_rider(arrs):
    n = len(arrs)

    def copies(ins, outs, scr):
        send_sems, recv_sems = scr
        x, y, cc = _me()
        cps = []
        for j in range(n):
            for k in range(1, N_CHIP):
                px, py = _flip(x, k & 2), _flip(y, k & 1)
                cps.append(pltpu.make_async_remote_copy(
                    src_ref=ins[j].at[2 * px + py], dst_ref=outs[j].at[k - 1], send_sem=send_sems.at[3 * j + k - 1],
                    recv_sem=recv_sems.at[3 * j + k - 1], device_id=(px, py, cc), device_id_type=MESH))
        return cps

    def start(ins, outs, scr):
        for cp in copies(ins, outs, scr):
            cp.start()

    def finish(ins, outs, scr):
        for cp in copies(ins, outs, scr):
            cp.wait()

    return _Rider(arrs, [jax.ShapeDtypeStruct((N_CHIP - 1,) + a.shape[1:], a.dtype) for a in arrs],
                  [pltpu.SemaphoreType.DMA((3 * n,)), pltpu.SemaphoreType.DMA((3 * n,))], start, finish)


COL_SHARDED = ("w_in", "w_br_pool", "w_br_attn", "w_br_conv", "w_ff1")
ROW_SHARDED = ("w_o", "w_ff2")
BIG = COL_SHARDED + ROW_SHARDED
SMALL = ("b_ada", "b_gate", "w_pool", "pool_scale", "rel_bias", "conv_w", "conv_b", "conv_ln_g", "conv_ln_b",
         "ln_mix_g", "ln_mix_b", "b_ff1", "b_ff2", "ln_ff_g", "ln_ff_b")
PACK_W = 1024


def _pack(parts):
    rows = []
    for a in parts:
        flat = a.reshape(-1)
        n = -(-flat.shape[0] // PACK_W) * PACK_W
        rows.append(jnp.pad(flat, (0, n - flat.shape[0])).reshape(-1, PACK_W))
    out = jnp.concatenate(rows, axis=0)
    r = -(-out.shape[0] // 8) * 8
    return jnp.pad(out, ((0, r - out.shape[0]), (0, 0)))


def _unpack(packed, shapes):
    out, r0 = [], 0
    for shp in shapes:
        size = int(np.prod(shp))
        nr = -(-size // PACK_W)
        out.append(packed[r0:r0 + nr].reshape(-1)[:size].reshape(shp))
        r0 += nr
    return out


def _hosted(fn, hook, *args, **kw):
    if hook is None:
        return fn(*args, **kw)
    res, rider_out = fn(*args, rider=hook[0], **kw)
    hook[1](rider_out)
    return res


def _layer_fwd(l, x, mod, W, P, hooks=None, u=None):
    hooks = hooks or {}
    s = x.shape[0]
    sh_m, sc_m, g_m, sh_f, sc_f, g_f = [mod[l:l + 1, D_MODEL * j:D_MODEL * (j + 1)] for j in range(6)]
    n = lambda t: f"{t}{l}"
    w_in = W["w_in"][l]
    if u is None:
        u = _ln_mod(x, sc_m, sh_m, n("ln_mod_mix"))
    zp = _mm(u, w_in, "nt", tm=s, tn=256, out_dtype=F32, name=n("z_pool"), b_col0=0, n_out=D_POOL)
    qkv = _mm(u, w_in, "nt", tm=s, tn=256, out_dtype=BF16, name=n("z_qkv"), b_col0=OFF_QKV // 256, n_out=3 * D_ATTN)
    zc = _mm(u, w_in, "nt", tm=s, tn=256, out_dtype=F32, name=n("z_conv"), b_col0=OFF_CONV // 256, n_out=2 * D_CONV)
    zg = _hosted(_mm, hooks.get("z_gate"), u, w_in, "nt", tm=min(2048, s), tn=768, out_dtype=BF16, name=n("z_gate"),
                 b_col0=OFF_GATE // 768, n_out=3 * D_MODEL)

    p, feat_pool = _pool_fwd(zp, P["wp_bd"][l], P["pool_scale"][l], n("pool_fwd"))
    bias = _bias_block(P["rel_bias"][l], n("bias_block"))
    o, probs = _hosted(_attn_fwd, hooks.get("attn"), qkv, bias, n("attn_fwd"))
    cv, feat_conv = _conv_fwd(zc, P["conv_w"][l], P["conv_b"][l], P["conv_ln_g"][l], P["conv_ln_b"][l], n("conv_fwd"))

    branch_w = (W["w_br_pool"][l], W["w_br_attn"][l], W["w_br_conv"][l])
    ys = tuple(_branch_out((feat_pool, o, feat_conv), branch_w, n("branch_out")))
    merged = _merge(zg, P["b_gate"][l], ys, n("merge"))
    mix, x1, u2 = _mm_resid_ln(merged, W["w_o"][l], None, x, g_m, P["ln_mix_g"][l], P["ln_mix_b"][l], n("mix_out"),
                               mod_next=(sc_f, sh_f))

    hpre, hid = _hosted(_ff_hidden, hooks.get("ff1"), u2, W["w_ff1"][l], P["b_ff1"][l], n("ff1"))
    above = None if l + 1 == mod.shape[0] else (mod[l + 1:l + 2, D_MODEL:2 * D_MODEL], mod[l + 1:l + 2, 0:D_MODEL])
    ff, x2, *u_next = _hosted(_mm_resid_ln, hooks.get("ff2"), hid, W["w_ff2"][l], P["b_ff2"][l], x1, g_f,
                              P["ln_ff_g"][l], P["ln_ff_b"][l], n("ff2"), mod_next=above)
    saved = dict(x=x, u=u, zp=zp, qkv=qkv, zc=zc, zg=zg, p=p, feat_pool=feat_pool, probs=probs, o=o, cv=cv,
                 feat_conv=feat_conv, ys=ys, merged=merged, mix=mix, x1=x1, u2=u2, hpre=hpre, hid=hid, ff=ff,
                 u_next=u_next[0] if u_next else None)
    return x2, saved


def _layer_bwd(l, dx2, mod, W, P, A, hooks=None, tgt=None, nxt=None):
    hooks = hooks or {}
    sh_m, sc_m, g_m, sh_f, sc_f, g_f = [mod[l:l + 1, D_MODEL * j:D_MODEL * (j + 1)] for j in range(6)]
    n = lambda t: f"{t}{l}"
    gw, gs = {}, {}

    if isinstance(dx2, tuple):
        dres, dff, gs["ln_ff_g"], gs["ln_ff_b"], dg_f, gs["b_ff2"] = dx2
    else:
        dres, dff, gs["ln_ff_g"], gs["ln_ff_b"], dg_f, gs["b_ff2"], *loss_part = _resid_ln_bwd(
            dx2, A["x1"], A["ff"], g_f, P["ln_ff_g"][l], n("resid_ln_ff_bwd"), tgt=tgt)
    gw["w_ff2"] = _mm(A["hid"], dff, "tn", tm=512, tn=1024, out_dtype=BF16, name=n("dw_ff2"), split_n=512)
    hook = hooks["ff_hidden_bwd"](gw) if "ff_hidden_bwd" in hooks else None
    dhpre, gs["b_ff1"] = _hosted(_ff_hidden_bwd, hook, dff, W["w_ff2"][l], A["hpre"], n("ff_hidden_bwd"))
    gw["w_ff1"] = _mm(dhpre, A["u2"], "tn", tm=512, tn=1024, out_dtype=BF16, name=n("dw_ff1"), split_n=512)

    hook = hooks["du_ff"](gw) if "du_ff" in hooks else None
    dres, dmix, dsc_f, dsh_f, gs["ln_mix_g"], gs["ln_mix_b"], dg_m, _ = _hosted(
        _mm_ln_mod_bwd, hook, dhpre, W["w_ff1"][l], A["x1"], sc_f, dres, n("du_ff"),
        nxt=(A["x"], A["mix"], g_m, P["ln_mix_g"][l]))
    gw["w_o"] = _mm(A["merged"], dmix, "tn", tm=512, tn=1024, out_dtype=BF16, name=n("dw_o"), split_n=512)
    dy_pool, dy_attn, dy_conv, dzg, gs["b_gate"] = _merge_bwd(dmix, W["w_o"][l], A["zg"], P["b_gate"][l], A["ys"],
                                                              n("merge_bwd"))

    dys = (dy_pool, dy_attn, dy_conv)
    gw["w_br_pool"], gw["w_br_attn"], gw["w_br_conv"] = _branch_dw(
        dys, (A["feat_pool"], A["o"], A["feat_conv"]), n("dw_branch"))
    dfeat_pool, do, dfeat_conv = _branch_in_bwd(
        dys, (W["w_br_pool"][l], W["w_br_attn"][l], W["w_br_conv"][l]), (F32, BF16, F32), n("d_branch_in"))

    dzp, dwp_bd, gs["pool_scale"] = _pool_bwd(dfeat_pool, A["p"], P["wp_bd"][l], P["pool_scale"][l], n("pool_bwd"))
    gs["w_pool"] = jnp.stack([dwp_bd[POOL_GROUP * g:POOL_GROUP * (g + 1), POOL_GROUP * g:POOL_GROUP * (g + 1)]
                              for g in range(len(POOL_WINDOWS))])
    hook = hooks["attn"](gw) if "attn" in hooks else None
    dq, dk, dv, ds_acc = _hosted(_attn_bwd, hook, A["qkv"], do, A["probs"], n("attn_bwd"))
    gs["rel_bias"] = _bias_block_bwd(ds_acc, n("bias_block_bwd"))
    dzc, dcw, gs["conv_b"], gs["conv_ln_g"], gs["conv_ln_b"] = _conv_bwd(
        dfeat_conv, A["cv"], A["zc"], P["conv_w"][l], P["conv_ln_g"][l], P["conv_ln_b"][l], n("conv_bwd"))
    gs["conv_w"] = dcw[:CONV_WIDTH]

    dz = [dzp, dq, dk, dv, dzc, dzg]
    gw["w_in"] = _dw_segments(dz, A["u"], n("dw_in"))
    hook = hooks["du_mix"](gw) if "du_mix" in hooks else None
    res = _hosted(_mm_ln_mod_bwd, hook, dz, W["w_in"][l], A["x"], sc_m, dres, n("du_mix"), nxt=nxt)
    if nxt is None:
        dx, dsc_m, dsh_m = res
    else:
        dx, dsc_m, dsh_m = (res[0], res[1], *res[4:]), res[2], res[3]
    dmod = jnp.concatenate([dsh_m, dsc_m, dg_m, dsh_f, dsc_f, dg_f], axis=1)
    return (dx, gw, gs, dmod) if tgt is None else (dx, gw, gs, dmod, loss_part[0])


def _small_shapes():
    return {"b_ada": (6 * D_MODEL,), "b_gate": (3 * D_MODEL,), "w_pool": (4, POOL_GROUP, POOL_GROUP),
            "pool_scale": (D_POOL,), "rel_bias": (N_HEADS, N_REL), "conv_w": (CONV_WIDTH, D_CONV),
            "conv_b": (D_CONV,), "conv_ln_g": (D_CONV,), "conv_ln_b": (D_CONV,), "ln_mix_g": (D_MODEL,),
            "ln_mix_b": (D_MODEL,), "b_ff1": (D_FF,), "b_ff2": (D_MODEL,), "ln_ff_g": (D_MODEL,), "ln_ff_b": (D_MODEL,)}


def kernel(x, c, w_ada, b_ada, w_in, b_gate, w_pool, pool_scale, rel_bias, conv_w, conv_b, conv_ln_g, conv_ln_b, w_br_pool, w_br_attn, w_br_conv, w_o, ln_mix_g, ln_mix_b, w_ff1, b_ff1, w_ff2, b_ff2, ln_ff_g, ln_ff_b, loss_target, m_w_ada, m_b_ada, m_w_in, m_b_gate, m_w_pool, m_pool_scale, m_rel_bias, m_conv_w, m_conv_b, m_conv_ln_g, m_conv_ln_b, m_w_br_pool, m_w_br_attn, m_w_br_conv, m_w_o, m_ln_mix_g, m_ln_mix_b, m_w_ff1, m_b_ff1, m_w_ff2, m_b_ff2, m_ln_ff_g, m_ln_ff_b, v_w_ada, v_b_ada, v_w_in, v_b_gate, v_w_pool, v_pool_scale, v_rel_bias, v_conv_w, v_conv_b, v_conv_ln_g, v_conv_ln_b, v_w_br_pool, v_w_br_attn, v_w_br_conv, v_w_o, v_ln_mix_g, v_ln_mix_b, v_w_ff1, v_b_ff1, v_w_ff2, v_b_ff2, v_ln_ff_g, v_ln_ff_b):
    env = dict(locals())
    xi, yi, ci = _me()
    chip = 2 * xi + yi
    me = 4 * xi + 2 * yi + ci
    xs = x[0]
    tgt = loss_target[0]
    L = DEPTH

    first = _allgather_small(jnp.concatenate([c.reshape(8, 128), _pack([conv_w]).reshape(-1, 128)]), "gather_c_conv_w")
    c_all = first[:, :8].reshape(N_DEV, D_MODEL)
    ada_cols = w_ada.shape[2]
    b_ada_sh = lax.dynamic_slice_in_dim(b_ada, chip * ada_cols, ada_cols, axis=1).reshape(L, 1, ada_cols)
    mod_part = _mod_fwd(c_all, w_ada, b_ada_sh, "mod_fwd")

    W = {k: [None] * L for k in BIG}

    def weight_gather(*items):
        shards = [(jnp.swapaxes(env[k][l], 0, 1) if k in COL_SHARDED else env[k][l]).astype(BF16) for k, l in items]
        shards = [a.reshape(2, a.shape[0] // 2, a.shape[1]) for a in shards]

        def done(outs):
            for (k, l), g in zip(items, outs):
                W[k][l] = g.reshape(-1, g.shape[-1])

        return _gather_rider(shards), done

    branch = lambda l: [(k, l) for k in ("w_br_pool", "w_br_attn", "w_br_conv", "w_o")]
    rider, done = weight_gather(("w_in", 0))
    first_out = _run_rider(_join_riders(_allgather_rider(mod_part.reshape(-1, 128)), rider), "gather_mod_w_in0")
    done(first_out[1:])
    mod_g = first_out[0].reshape(N_CHIP, 2, L, N_DEV, ada_cols)[:, 0]
    mod_all = jnp.transpose(mod_g, (1, 2, 0, 3)).reshape(L, N_DEV, 6 * D_MODEL)
    mod = lax.dynamic_index_in_dim(mod_all, me, axis=1, keepdims=False)
    fwd_hooks = [{"z_gate": weight_gather(*branch(0)), "attn": weight_gather(("w_ff1", 0), ("w_ff2", 0)),
                  "ff1": weight_gather(*branch(1)), "ff2": weight_gather(("w_in", 1))},
                 {"attn": weight_gather(("w_ff1", 1), ("w_ff2", 1))}]

    P = {k: env[k] for k in ("rel_bias", "conv_w")}
    for k in ("b_gate", "pool_scale", "conv_b", "conv_ln_g", "conv_ln_b", "ln_mix_g", "ln_mix_b", "b_ff1", "b_ff2",
              "ln_ff_g", "ln_ff_b"):
        P[k] = env[k].reshape(L, 1, -1)
    n_cw = conv_w.size
    cw = first[:, 8:].reshape(N_CHIP, 2, -1)[:, 0, :n_cw].reshape(N_CHIP, L, CONV_WIDTH, D_CONV // N_CHIP)
    P["conv_w"] = jnp.transpose(cw, (1, 2, 0, 3)).reshape(L, CONV_WIDTH, D_CONV)
    wp_bd = jnp.zeros((L, D_POOL, D_POOL), F32)
    for g in range(len(POOL_WINDOWS)):
        sl = slice(POOL_GROUP * g, POOL_GROUP * (g + 1))
        wp_bd = wp_bd.at[:, sl, sl].set(w_pool[:, g])
    P["wp_bd"] = wp_bd.astype(BF16)

    acts = []
    h = xs
    for l in range(L):
        h, saved = _layer_fwd(l, h, mod, W, P, fwd_hooks[l], u=acts[-1]["u_next"] if acts else None)
        acts.append(saved)

    place = jnp.stack([ci, chip, chip ^ 1, chip ^ 2, chip ^ 3]).astype(jnp.int32)
    scattered = {}

    swapped = {}

    def swap_hook(names, l):
        def hook(gw):
            def done(outs):
                swapped.update({(k, l): o for k, o in zip(names, outs)})
            return _sibling_rider([gw[k] for k in names], other_half=True), done
        return hook

    def scatter_hook(names, l, host, then=None):
        def hook(gw):
            todo = [k for k in names if (k, l) not in swapped]
            if todo:
                got = _sibling_send([gw[k] for k in todo], f"swap_blocks_{host}{l}", other_half=True)
                swapped.update({(k, l): o for k, o in zip(todo, got)})
            sums = _sum_cores([gw[k] for k in names], [swapped[(k, l)] for k in names], place, f"sum_cores_{host}{l}")
            both = [hh.reshape(N_CHIP, -1, hh.shape[-1]) for hh in sums]
            rider = _scatter_rider(both)
            more = then(gw) if then is not None else None

            def done(outs):
                for k, hh, r in zip(names, both, outs):
                    scattered[(k, l)] = (hh, r)
                if more is not None:
                    more[1](outs[len(names):])

            return (rider if more is None else _join_riders(rider, more[0])), done
        return hook

    gws, gss, dmods = [None] * L, [None] * L, [None] * L
    dh = h
    for l in reversed(range(L)):
        hooks = {"ff_hidden_bwd": swap_hook(("w_ff2",), l),
                 "du_ff": scatter_hook(("w_ff2",), l, "du_ff", then=swap_hook(("w_ff1",), l)),
                 "attn": scatter_hook(("w_ff1", "w_o", "w_br_pool", "w_br_attn", "w_br_conv"), l, "attn_bwd"),
                 "du_mix": scatter_hook(("w_in",), l, "du_mix")}
        below = None
        if l > 0:
            below = (acts[l - 1]["x1"], acts[l - 1]["ff"], mod[l - 1:l, 5 * D_MODEL:], P["ln_ff_g"][l - 1])
        if l == L - 1:
            dh, gws[l], gss[l], dmods[l], loss_part = _layer_bwd(l, dh, mod, W, P, acts[l], hooks, tgt=tgt, nxt=below)
        else:
            dh, gws[l], gss[l], dmods[l] = _layer_bwd(l, dh, mod, W, P, acts[l], hooks, nxt=below)
    grad_x = dh[None]

    reduced = [[None] * L for _ in BIG]
    groups = (("w_in", "w_br_pool", "w_br_attn", "w_br_conv"), ("w_o", "w_ff1", "w_ff2"))
    for l in range(L):
        for gi, names in enumerate(groups):
            pairs = [scattered[(k, l)] for k in names]
            sums = _sum_chips([p[0] for p in pairs], [p[1] for p in pairs], place, f"sum_chips_{gi}_{l}")
            for k, t in zip(names, sums):
                reduced[BIG.index(k)][l] = t
    flat_reduced = [t for per_weight in reduced for t in per_weight]

    shapes = _small_shapes()
    small_names = [k for k in SMALL if k != "b_ada"]
    dmod_own = jnp.concatenate(dmods, axis=0)
    pack = _pack([dmod_own] + [jnp.stack([gss[l][k].reshape(shapes[k]) for l in range(L)]) for k in small_names]
                 + [loss_part])
    last = _run_rider(_join_riders(_sibling_rider(flat_reduced), _allgather_rider(pack.reshape(-1, 128))),
                      "swap_reduced_gather_small")
    flat_other, g_all = last[:-1], last[-1].reshape(N_DEV, -1, PACK_W)

    out = {}
    for j, k in enumerate(BIG):
        own, other = reduced[j], flat_other[L * j:L * (j + 1)]
        if k == "w_in":
            t = lambda a: jnp.swapaxes(a, 1, 2)
            res = _adamw_halves(t(env[k]), t(env["m_" + k]), t(env["v_" + k]), own, other, place, "cols", f"adamw_{k}")
            res = [t(a) for a in res]
        else:
            if k in COL_SHARDED:
                own, other = [a.T for a in own], [a.T for a in other]
            res = _adamw_halves(env[k], env["m_" + k], env["v_" + k], own, other, place,
                                "rows" if k in COL_SHARDED else "cols", f"adamw_{k}")
        out[k] = tuple(res)

    dmod_all = g_all[:, :L * 6].reshape(N_DEV, L, 6 * D_MODEL)
    dmod_sh = jnp.transpose(lax.dynamic_slice_in_dim(dmod_all, chip * ada_cols, ada_cols, axis=2), (1, 0, 2))
    g_ada = _mod_bwd(c_all, dmod_sh, "mod_bwd")
    g_, d_, m_, v_ = _adamw(w_ada.reshape(-1, ada_cols), m_w_ada.reshape(-1, ada_cols), v_w_ada.reshape(-1, ada_cols),
                            [g_ada.reshape(-1, ada_cols)], "adamw_w_ada")
    out["w_ada"] = tuple(a.reshape(w_ada.shape) for a in (g_, d_, m_, v_))

    def small_pack(prefix):
        parts = [env[prefix + "b_ada"]]
        for k in small_names:
            a = env[prefix + k]
            if k == "conv_w":
                a = jnp.zeros((L,) + shapes[k], F32)
            parts.append(a)
        return _pack(parts + [jnp.zeros_like(loss_part)])

    gp, dp, mp, vp = _adamw_small(small_pack(""), small_pack("m_"), small_pack("v_"), g_all, "adamw_small")
    full_shapes = [(L,) + shapes["b_ada"]] + [(L,) + shapes[k] for k in small_names]
    loss = _unpack(gp, full_shapes + [(128,)])[-1][0]
    for tag, packed in (("g", gp), ("d", dp), ("m", mp), ("v", vp)):
        for k, a in zip(["b_ada"] + small_names, _unpack(packed, full_shapes)):
            out.setdefault(k, {})
            out[k][tag] = a
    g_cw_full = out["conv_w"]["g"]
    cw_cols = D_CONV // N_CHIP
    g_cw = lax.dynamic_slice_in_dim(g_cw_full, chip * cw_cols, cw_cols, axis=2)
    pad_rows = lambda a: jnp.pad(a.reshape(L * CONV_WIDTH, cw_cols), ((0, 2), (0, 0)))
    g_, d_, m_, v_ = _adamw(pad_rows(conv_w), pad_rows(m_conv_w), pad_rows(v_conv_w), [pad_rows(g_cw)], "adamw_conv_w")
    out["conv_w"] = tuple(a[:L * CONV_WIDTH].reshape(L, CONV_WIDTH, cw_cols) for a in (g_, d_, m_, v_))

    names = ["w_ada", "b_ada", "w_in", "b_gate", "w_pool", "pool_scale", "rel_bias", "conv_w", "conv_b", "conv_ln_g",
             "conv_ln_b", "w_br_pool", "w_br_attn", "w_br_conv", "w_o", "ln_mix_g", "ln_mix_b", "w_ff1", "b_ff1",
             "w_ff2", "b_ff2", "ln_ff_g", "ln_ff_b"]

    def pick(k, i):
        o = out[k]
        return o[i] if isinstance(o, tuple) else o["gdmv"[i]].reshape(env[k].shape)

    return (loss, grad_x, *[pick(k, 0) for k in names], *[pick(k, 1) for k in names],
            *[pick(k, 2) for k in names], *[pick(k, 3) for k in names])
```

```python
import jax
import jax.numpy as jnp
import numpy as np
from jax import lax
from jax.experimental import pallas as pl
from jax.experimental.pallas import tpu as pltpu

F32 = jnp.float32
BF16 = jnp.bfloat16

D_MODEL = 1024
DEPTH = 2
CHUNK = 64
POOL_WINDOWS = (2, 4, 8, 16)
POOL_GROUP = 64
D_POOL = 256
N_HEADS = 8
HEAD_DIM = 64
D_ATTN = 512
N_PREV_CHUNKS = 8
REL_CLIP = 128
N_REL = 2 * REL_CLIP + 1
D_CONV = 256
CONV_WIDTH = 31
D_FF = 4 * D_MODEL
D_IN = 5376
OFF_POOL, OFF_QKV, OFF_CONV, OFF_GATE = 0, 256, 1792, 2304
ALPHA = (2.0 * DEPTH) ** 0.25
LN_EPS = 1e-5
NEG_INF = -1e30
ADAM_LR, ADAM_B1, ADAM_B2, ADAM_EPS, ADAM_WD, ADAM_STEP = 0.001, 0.9, 0.999, 1e-08, 0.01, 10

N_DEV = 8
N_CHIP = 4
MESH = pl.DeviceIdType.MESH

QB = 2 * CHUNK
KPAD = N_PREV_CHUNKS * CHUNK
KW = QB + KPAD
SKEW_W = 768

VMEM_LIMIT = 56 * 1024 * 1024


def _cparams(**kw):
    return pltpu.CompilerParams(vmem_limit_bytes=VMEM_LIMIT, **kw)


def _full(shape):
    n = len(shape)
    return pl.BlockSpec(shape, lambda *_: (0,) * n)


_DIMS = {"nn": (((1,), (0,)), ((), ())), "nt": (((1,), (1,)), ((), ())), "tn": (((0,), (0,)), ((), ()))}


def _relu2(t):
    r = jnp.maximum(t, 0.0)
    return r * r


def _mm(a, b, mode, *, tm, tn, out_dtype, name, b_col0=0, n_out=None, bias=None, split_n=0, rider=None):
    if mode == "tn":
        k, m = a.shape
        n = b.shape[1] if n_out is None else n_out
        a_spec = pl.BlockSpec((k, tm), lambda i, j: (0, i))
        b_spec = pl.BlockSpec((k, tn), lambda i, j: (0, j + b_col0))
    elif mode == "nn":
        m, k = a.shape
        n = b.shape[1] if n_out is None else n_out
        a_spec = pl.BlockSpec((tm, k), lambda i, j: (i, 0))
        b_spec = pl.BlockSpec((k, tn), lambda i, j: (0, j + b_col0))
    else:
        m, k = a.shape
        n = b.shape[0] if n_out is None else n_out
        a_spec = pl.BlockSpec((tm, k), lambda i, j: (i, 0))
        b_spec = pl.BlockSpec((tn, k), lambda i, j: (j + b_col0, 0))
    assert m % tm == 0 and n % tn == 0, (name, m, n, tm, tn)
    dims = _DIMS[mode]

    def body(*refs):
        if bias is None:
            a_ref, b_ref, o_ref = refs
        else:
            a_ref, b_ref, bias_ref, o_ref = refs
        acc = lax.dot_general(a_ref[...].astype(BF16), b_ref[...].astype(BF16), dims, preferred_element_type=F32)
        if bias is not None:
            acc = acc + bias_ref[...]
        if split_n:
            for c in range(tn // split_n):
                o_ref[c] = acc[:, c * split_n:(c + 1) * split_n].astype(out_dtype)
        else:
            o_ref[...] = acc.astype(out_dtype)

    in_specs = [a_spec, b_spec]
    args = [a, b]
    if bias is not None:
        in_specs.append(pl.BlockSpec((1, tn), lambda i, j: (0, j)))
        args.append(bias)
    if split_n:
        out_spec = pl.BlockSpec((tn // split_n, tm, split_n), lambda i, j: (j, i, 0))
        out_shape = jax.ShapeDtypeStruct((n // split_n, m, split_n), out_dtype)
    else:
        out_spec = pl.BlockSpec((tm, tn), lambda i, j: (i, j))
        out_shape = jax.ShapeDtypeStruct((m, n), out_dtype)
    res = _call(body, name=name, grid=(m // tm, n // tn), in_specs=in_specs, out_specs=[out_spec],
                out_shape=[out_shape], scratch_shapes=[], args=args, rider=rider)
    return res[0] if rider is None else (res[0][0], res[1])


def _ln_hat(x):
    mu = jnp.mean(x, axis=-1, keepdims=True)
    xc = x - mu
    var = jnp.mean(xc * xc, axis=-1, keepdims=True)
    rstd = lax.rsqrt(var + LN_EPS)
    return xc * rstd, rstd


def _ln_hat_bwd(dhat, xhat, rstd):
    m1 = jnp.mean(dhat, axis=-1, keepdims=True)
    m2 = jnp.mean(dhat * xhat, axis=-1, keepdims=True)
    return rstd * (dhat - m1 - xhat * m2)


def _row_tile(s):
    return min(512, s)


def _acc_rows(ref, val, first):
    @pl.when(first)
    def _():
        ref[...] = jnp.zeros_like(ref)
    ref[...] += jnp.sum(val, axis=0, keepdims=True)


def _ln_mod(x, sc, sh, name):
    s, d = x.shape
    tm = _row_tile(s)

    def body(x_ref, sc_ref, sh_ref, u_ref):
        xhat, _ = _ln_hat(x_ref[...])
        u_ref[...] = (xhat * (1.0 + sc_ref[...]) + sh_ref[...]).astype(BF16)

    row = pl.BlockSpec((tm, d), lambda i: (i, 0))
    vec = pl.BlockSpec((1, d), lambda i: (0, 0))
    return pl.pallas_call(body, grid=(s // tm,), in_specs=[row, vec, vec], out_specs=row,
                          out_shape=jax.ShapeDtypeStruct((s, d), BF16), name=name, compiler_params=_cparams())(x, sc, sh)


def _resid_bwd_tile(dxo, x, f, g, gam):
    rhat, rstd = _ln_hat(ALPHA * x + g * f)
    dr = _ln_hat_bwd(dxo * gam, rhat, rstd)
    return ALPHA * dr, g * dr, dxo * rhat, dr * f


def _mm_ln_mod_bwd(a, b, x, sc, dres, name, rider=None, nxt=None):
    segs = list(a) if isinstance(a, (list, tuple)) else [a]
    s = segs[0].shape[0]
    k, d = b.shape
    assert sum(t.shape[1] for t in segs) == k
    tm = min(512 if k <= 4096 and nxt is None else 256, s)
    ns = len(segs)

    def body(*refs):
        seg_refs = refs[:ns]
        if nxt is None:
            b_ref, x_ref, sc_ref, dres_ref, dx_ref, dsc_ref, dsh_ref = refs[ns:]
        else:
            (b_ref, x_ref, sc_ref, dres_ref, xp_ref, fp_ref, gp_ref, gamp_ref,
             dresp_ref, dfp_ref, dsc_ref, dsh_ref, dgam_ref, dbet_ref, dg_ref, dbias_ref) = refs[ns:]
        first = pl.program_id(0) == 0
        duv, r0 = None, 0
        for seg_ref in seg_refs:
            w = seg_ref.shape[1]
            part = jnp.dot(seg_ref[...], b_ref[r0:r0 + w, :], preferred_element_type=F32)
            duv = part if duv is None else duv + part
            r0 += w
        xhat, rstd = _ln_hat(x_ref[...])
        dxv = dres_ref[...] + _ln_hat_bwd(duv * (1.0 + sc_ref[...]), xhat, rstd)
        _acc_rows(dsc_ref, duv * xhat, first)
        _acc_rows(dsh_ref, duv, first)
        if nxt is None:
            dx_ref[...] = dxv
        else:
            dresp, dfp, t_gam, t_g = _resid_bwd_tile(dxv, xp_ref[...], fp_ref[...], gp_ref[...], gamp_ref[...])
            dresp_ref[...] = dresp
            dfp_ref[...] = dfp.astype(BF16)
            _acc_rows(dgam_ref, t_gam, first)
            _acc_rows(dbet_ref, dxv, first)
            _acc_rows(dg_ref, t_g, first)
            _acc_rows(dbias_ref, dfp, first)

    row = pl.BlockSpec((tm, d), lambda i: (i, 0))
    vec = pl.BlockSpec((1, d), lambda i: (0, 0))
    vs = jax.ShapeDtypeStruct((1, d), F32)
    rows = jax.ShapeDtypeStruct((s, d), F32)
    in_specs = [pl.BlockSpec((tm, t.shape[1]), lambda i: (i, 0)) for t in segs] + [_full((k, d)), row, vec, row]
    args = (*segs, b, x, sc, dres)
    if nxt is None:
        out_specs, out_shape = [row, vec, vec], [rows, vs, vs]
    else:
        in_specs += [row, row, vec, vec]
        args += tuple(nxt)
        out_specs = [row, row] + [vec] * 6
        out_shape = [rows, jax.ShapeDtypeStruct((s, d), BF16)] + [vs] * 6
    res = _call(body, name=name, grid=(s // tm,), in_specs=in_specs, out_specs=out_specs, out_shape=out_shape,
                scratch_shapes=[], args=args, rider=rider)
    return tuple(res) if rider is None else (tuple(res[0]), res[1])


def _dw_segments(segs, u, name):
    s, d = u.shape
    tw = 256
    tiles = [t.shape[1] // tw for t in segs]
    starts = [sum(tiles[:j]) for j in range(len(segs))]
    ns = len(segs)

    def body(*refs):
        seg_refs, u_ref, o_ref = refs[:ns], refs[ns], refs[ns + 1]
        i = pl.program_id(0)
        for seg_ref, t0, nt in zip(seg_refs, starts, tiles):
            @pl.when((i >= t0) & (i < t0 + nt))
            def _(seg_ref=seg_ref):
                acc = lax.dot_general(seg_ref[...], u_ref[...], _DIMS["tn"], preferred_element_type=F32)
                o_ref[0] = acc[:, :d // 2].astype(BF16)
                o_ref[1] = acc[:, d // 2:].astype(BF16)

    def seg_spec(t0, nt):
        return pl.BlockSpec((s, tw), lambda i: (0, jnp.clip(i - t0, 0, nt - 1)))

    return pl.pallas_call(
        body, grid=(sum(tiles),), in_specs=[seg_spec(t0, nt) for t0, nt in zip(starts, tiles)] + [_full((s, d))],
        out_specs=pl.BlockSpec((2, tw, d // 2), lambda i: (0, i, 0)),
        out_shape=jax.ShapeDtypeStruct((2, sum(tiles) * tw, d // 2), BF16), name=name, compiler_params=_cparams(),
    )(*segs, u)


def _mm_resid_ln(a, b, bias, x, g, gam, bet, name, rider=None, mod_next=None):
    s, k = a.shape
    d = b.shape[1]
    tm = min(512, s)
    nb, nm = int(bias is not None), 2 * int(mod_next is not None)

    def body(*refs):
        a_ref, b_ref = refs[:2]
        x_ref, g_ref, gam_ref, bet_ref = refs[2 + nb:6 + nb]
        f_ref, o_ref = refs[6 + nb + nm:8 + nb + nm]
        f = jnp.dot(a_ref[...], b_ref[...], preferred_element_type=F32)
        if bias is not None:
            f = f + refs[2][...]
        f_ref[...] = f
        rhat, _ = _ln_hat(ALPHA * x_ref[...] + g_ref[...] * f)
        y = rhat * gam_ref[...] + bet_ref[...]
        o_ref[...] = y
        if mod_next is not None:
            sc_ref, sh_ref = refs[6 + nb:8 + nb]
            yhat, _ = _ln_hat(y)
            refs[8 + nb + nm][...] = (yhat * (1.0 + sc_ref[...]) + sh_ref[...]).astype(BF16)

    row = pl.BlockSpec((tm, d), lambda i: (i, 0))
    vec = pl.BlockSpec((1, d), lambda i: (0, 0))
    in_specs = [pl.BlockSpec((tm, k), lambda i: (i, 0)), _full((k, d))] + [vec] * nb + [row, vec, vec, vec] + [vec] * nm
    args = [a, b] + ([bias] if nb else []) + [x, g, gam, bet] + (list(mod_next) if nm else [])
    sh = jax.ShapeDtypeStruct((s, d), F32)
    out_specs, out_shape = [row, row], [sh, sh]
    if nm:
        out_specs, out_shape = out_specs + [row], out_shape + [jax.ShapeDtypeStruct((s, d), BF16)]
    res = _call(body, name=name, grid=(s // tm,), in_specs=in_specs, out_specs=out_specs, out_shape=out_shape,
                scratch_shapes=[], args=args, rider=rider)
    return tuple(res) if rider is None else (tuple(res[0]), res[1])


def _resid_ln_bwd(dxo, x, f, g, gam, name, tgt=None):
    s, d = x.shape
    tm = _row_tile(s)
    n = s // tm

    def body(*refs):
        if tgt is None:
            dxo_ref, x_ref, f_ref, g_ref, gam_ref, dres_ref, df_ref, dgam_ref, dbet_ref, dg_ref, dbias_ref = refs
            dxov = dxo_ref[...]
        else:
            (dxo_ref, t_ref, x_ref, f_ref, g_ref, gam_ref, dres_ref, df_ref, dgam_ref, dbet_ref, dg_ref, dbias_ref,
             loss_ref, sq_ref) = refs
            err = dxo_ref[...] - t_ref[...]
            dxov = err * (1.0 / d)
            _acc_rows(sq_ref, err * err, pl.program_id(0) == 0)

            @pl.when(pl.program_id(0) == n - 1)
            def _():
                tot = jnp.sum(sq_ref[...], axis=1, keepdims=True) * (0.5 / d)
                loss_ref[...] = jnp.broadcast_to(tot, (1, 128))

        first = pl.program_id(0) == 0
        dres, dfv, t_gam, t_g = _resid_bwd_tile(dxov, x_ref[...], f_ref[...], g_ref[...], gam_ref[...])
        dres_ref[...] = dres
        df_ref[...] = dfv.astype(BF16)
        _acc_rows(dgam_ref, t_gam, first)
        _acc_rows(dbet_ref, dxov, first)
        _acc_rows(dg_ref, t_g, first)
        _acc_rows(dbias_ref, dfv, first)

    row = pl.BlockSpec((tm, d), lambda i: (i, 0))
    vec = pl.BlockSpec((1, d), lambda i: (0, 0))
    vs = jax.ShapeDtypeStruct((1, d), F32)
    out_specs = [row, row, vec, vec, vec, vec]
    out_shape = [jax.ShapeDtypeStruct((s, d), F32), jax.ShapeDtypeStruct((s, d), BF16), vs, vs, vs, vs]
    if tgt is None:
        return pl.pallas_call(body, grid=(n,), in_specs=[row, row, row, vec, vec], out_specs=out_specs,
                              out_shape=out_shape, name=name, compiler_params=_cparams())(dxo, x, f, g, gam)
    return pl.pallas_call(body, grid=(n,), in_specs=[row, row, row, row, vec, vec],
                          out_specs=out_specs + [pl.BlockSpec((1, 128), lambda i: (0, 0))],
                          out_shape=out_shape + [jax.ShapeDtypeStruct((1, 128), F32)],
                          scratch_shapes=[pltpu.VMEM((1, d), F32)], name=name,
                          compiler_params=_cparams())(dxo, tgt, x, f, g, gam)


POOL_HALO = 16
POOL_ROWS = 256


def _pool_counts(r0, rows):
    t1 = (lax.broadcasted_iota(jnp.int32, (rows, 128), 0) + r0 + 1).astype(F32)
    low = lax.broadcasted_iota(jnp.int32, (rows, 128), 1) < POOL_GROUP
    wa = jnp.where(low, float(POOL_WINDOWS[0]), float(POOL_WINDOWS[1]))
    wb = jnp.where(low, float(POOL_WINDOWS[2]), float(POOL_WINDOWS[3]))
    return jnp.minimum(t1, wa), jnp.minimum(t1, wb), low


def _window_sums(win, off, rows, sign):
    def sl(j, half):
        return win[off + sign * j: off + sign * j + rows, 128 * half:128 * half + 128]
    a2 = sl(0, 0) + sl(1, 0)
    a4 = a2 + sl(2, 0) + sl(3, 0)
    a8 = sl(0, 1)
    for j in range(1, 8):
        a8 = a8 + sl(j, 1)
    a16 = a8
    for j in range(8, 16):
        a16 = a16 + sl(j, 1)
    return a2, a4, a8, a16


def _pool_fwd(zp, wp_bd, pscale, name):
    s = zp.shape[0]
    r = min(POOL_ROWS, s)

    def body(z_ref, wp_ref, sc_ref, p_ref, feat_ref, pad):
        pad[0:POOL_HALO, :] = jnp.zeros((POOL_HALO, D_POOL), F32)
        pad[POOL_HALO:, :] = z_ref[...]

        def step(i, carry):
            r0 = pl.multiple_of(i * r, r)
            win = pad[pl.ds(r0, r + POOL_HALO), :]
            a2, a4, a8, a16 = _window_sums(win, POOL_HALO, r, -1)
            ca, cb, low = _pool_counts(r0, r)
            x0 = win[POOL_HALO:, :]
            pa = jnp.where(low, a2, a4) / ca
            pb = jnp.where(low, a8, a16) / cb
            p = (jnp.concatenate([pa, pb], axis=1) - x0).astype(BF16)
            p_ref[pl.ds(r0, r), :] = p
            pw = jnp.dot(p, wp_ref[...], preferred_element_type=F32)
            feat_ref[pl.ds(r0, r), :] = (pw * sc_ref[...]).astype(BF16)
            return carry

        lax.fori_loop(0, s // r, step, 0)

    return pl.pallas_call(
        body, out_shape=[jax.ShapeDtypeStruct((s, D_POOL), BF16), jax.ShapeDtypeStruct((s, D_POOL), BF16)],
        scratch_shapes=[pltpu.VMEM((s + POOL_HALO, D_POOL), F32)], name=name, compiler_params=_cparams(),
    )(zp, wp_bd, pscale)


def _pool_bwd(dfeat, p, wp_bd, pscale, name):
    s = p.shape[0]
    r = min(POOL_ROWS, s)

    def body(df_ref, p_ref, wp_ref, sc_ref, dz_ref, dwp_ref, dsc_ref, gpad, dpbuf):
        dwp_ref[...] = jnp.zeros_like(dwp_ref)
        dsc_ref[...] = jnp.zeros_like(dsc_ref)
        gpad[s:, :] = jnp.zeros((POOL_HALO, D_POOL), F32)

        def step1(i, carry):
            r0 = pl.multiple_of(i * r, r)
            pv = p_ref[pl.ds(r0, r), :]
            dfv = df_ref[pl.ds(r0, r), :]
            pw = jnp.dot(pv, wp_ref[...], preferred_element_type=F32)
            dsc_ref[...] += jnp.sum(dfv * pw, axis=0, keepdims=True)
            dpw = (dfv * sc_ref[...]).astype(BF16)
            dwp_ref[...] += lax.dot_general(pv, dpw, _DIMS["tn"], preferred_element_type=F32)
            dp = lax.dot_general(dpw, wp_ref[...], _DIMS["nt"], preferred_element_type=F32)
            ca, cb, _ = _pool_counts(r0, r)
            gpad[pl.ds(r0, r), :] = dp / jnp.concatenate([ca, cb], axis=1)
            dpbuf[pl.ds(r0, r), :] = dp
            return carry

        lax.fori_loop(0, s // r, step1, 0)

        def step2(i, carry):
            r0 = pl.multiple_of(i * r, r)
            win = gpad[pl.ds(r0, r + POOL_HALO), :]
            a2, a4, a8, a16 = _window_sums(win, 0, r, 1)
            low = lax.broadcasted_iota(jnp.int32, (r, 128), 1) < POOL_GROUP
            acc = jnp.concatenate([jnp.where(low, a2, a4), jnp.where(low, a8, a16)], axis=1)
            dz_ref[pl.ds(r0, r), :] = (acc - dpbuf[pl.ds(r0, r), :]).astype(BF16)
            return carry

        lax.fori_loop(0, s // r, step2, 0)

    return pl.pallas_call(
        body,
        out_shape=[jax.ShapeDtypeStruct((s, D_POOL), BF16), jax.ShapeDtypeStruct((D_POOL, D_POOL), F32),
                   jax.ShapeDtypeStruct((1, D_POOL), F32)],
        scratch_shapes=[pltpu.VMEM((s + POOL_HALO, D_POOL), F32), pltpu.VMEM((s, D_POOL), F32)],
        name=name, compiler_params=_cparams(),
    )(dfeat, p, wp_bd, pscale)


def _skew_index():
    cp = lax.broadcasted_iota(jnp.int32, (SKEW_W, N_REL), 0)
    dist = jnp.where(cp < KW, KPAD - cp, KPAD + SKEW_W - cp)
    idx = jnp.clip(dist, -REL_CLIP, REL_CLIP) + REL_CLIP
    return (idx == lax.broadcasted_iota(jnp.int32, (SKEW_W, N_REL), 1)).astype(F32)


def _row_bits(b):
    return (lax.broadcasted_iota(jnp.int32, (QB, SKEW_W), 0) >> b) & 1 == 1


N_EDGE = KPAD // QB


def _bias_block(rel_bias, name):
    def body(rb_ref, o_ref):
        onehot = _skew_index()
        row0 = lax.dot_general(rb_ref[...], onehot, _DIMS["nt"], precision=lax.Precision.HIGHEST,
                               preferred_element_type=F32)
        r = lax.broadcasted_iota(jnp.int32, (QB, KW), 0)
        kk = lax.broadcasted_iota(jnp.int32, (QB, KW), 1)
        cq, ck = r // CHUNK, kk // CHUNK
        band = (ck >= cq) & (ck <= cq + N_PREV_CHUNKS)
        for h in range(N_HEADS):
            t = jnp.broadcast_to(row0[h:h + 1, :], (QB, SKEW_W))
            for b in range(7):
                t = jnp.where(_row_bits(b), pltpu.roll(t, 1 << b, 1), t)
            for e in range(N_EDGE + 1):
                o_ref[e, h] = jnp.where(band & (kk >= KPAD - e * QB), t[:, :KW], NEG_INF)

    return pl.pallas_call(body, out_shape=jax.ShapeDtypeStruct((N_EDGE + 1, N_HEADS, QB, KW), F32), name=name,
                          compiler_params=_cparams())(rel_bias)


def _bias_spec():
    return pl.BlockSpec((None, N_HEADS, QB, KW), lambda i: (jnp.minimum(i, N_EDGE), 0, 0, 0))


def _bias_block_bwd(ds_acc, name):
    def body(ds_ref, o_ref):
        sums = []
        for h in range(N_HEADS):
            t = jnp.concatenate([ds_ref[h], jnp.zeros((QB, SKEW_W - KW), F32)], axis=1)
            for b in range(7):
                t = jnp.where(_row_bits(b), pltpu.roll(t, SKEW_W - (1 << b), 1), t)
            sums.append(jnp.sum(t, axis=0, keepdims=True))
        allh = jnp.concatenate(sums, axis=0)
        o_ref[...] = jnp.dot(allh, _skew_index(), precision=lax.Precision.HIGHEST, preferred_element_type=F32)

    return pl.pallas_call(body, out_shape=jax.ShapeDtypeStruct((N_HEADS, N_REL), F32), name=name,
                          compiler_params=_cparams())(ds_acc)


def _scaled(q):
    return (q.astype(F32) * (HEAD_DIM ** -0.5)).astype(BF16)


def _probs(q, kw, bias_ref):
    sc = jnp.stack([lax.dot_general(q[:, HEAD_DIM * h:HEAD_DIM * (h + 1)], kw[:, HEAD_DIM * h:HEAD_DIM * (h + 1)],
                                    _DIMS["nt"], preferred_element_type=F32) + bias_ref[h] for h in range(N_HEADS)])
    e = jnp.exp(sc - jnp.max(sc, axis=-1, keepdims=True))
    return e * (1.0 / jnp.sum(e, axis=-1, keepdims=True))


def _load_padded_kv(qkv_hbm, kpad, vpad, sems, s):
    kpad[0:KPAD, :] = jnp.zeros((KPAD, D_ATTN), BF16)
    vpad[0:KPAD, :] = jnp.zeros((KPAD, D_ATTN), BF16)
    ck = pltpu.make_async_copy(qkv_hbm.at[:, D_ATTN:2 * D_ATTN], kpad.at[pl.ds(KPAD, s), :], sems.at[0])
    cv = pltpu.make_async_copy(qkv_hbm.at[:, 2 * D_ATTN:3 * D_ATTN], vpad.at[pl.ds(KPAD, s), :], sems.at[1])
    ck.start()
    cv.start()
    ck.wait()
    cv.wait()


def _attn_fwd(qkv, bias, name, rider=None):
    s = qkv.shape[0]

    def body(q_ref, qkv_hbm, bias_ref, o_ref, p_ref, kpad, vpad, sems):
        i = pl.program_id(0)

        @pl.when(i == 0)
        def _():
            _load_padded_kv(qkv_hbm, kpad, vpad, sems, s)

        base = pl.multiple_of(i * QB, QB)
        kw = kpad[pl.ds(base, KW), :]
        vw = vpad[pl.ds(base, KW), :]
        q = _scaled(q_ref[...])
        p = _probs(q, kw, bias_ref).astype(BF16)
        p_ref[...] = p
        outs = [jnp.dot(p[h], vw[:, HEAD_DIM * h:HEAD_DIM * (h + 1)], preferred_element_type=F32)
                for h in range(N_HEADS)]
        o_ref[...] = jnp.concatenate(outs, axis=1).astype(BF16)

    res = _call(
        body, name=name, grid=(s // QB,),
        in_specs=[pl.BlockSpec((QB, D_ATTN), lambda i: (i, 0)), pl.BlockSpec(memory_space=pl.ANY),
                  _bias_spec()],
        out_specs=[pl.BlockSpec((QB, D_ATTN), lambda i: (i, 0)), _probs_spec()],
        out_shape=[jax.ShapeDtypeStruct((s, D_ATTN), BF16), jax.ShapeDtypeStruct((N_HEADS, s, KW), BF16)],
        scratch_shapes=[pltpu.VMEM((s + KPAD, D_ATTN), BF16), pltpu.VMEM((s + KPAD, D_ATTN), BF16),
                        pltpu.SemaphoreType.DMA((2,))],
        args=(qkv, qkv, bias), rider=rider)
    return tuple(res) if rider is None else (tuple(res[0]), res[1])


def _probs_spec():
    return pl.BlockSpec((N_HEADS, QB, KW), lambda i: (0, i, 0))


def _attn_bwd(qkv, do, probs, name, rider=None):
    s = qkv.shape[0]
    n = s // QB

    def body(q_ref, qkv_hbm, do_ref, p_ref, dq_ref, dk_hbm, dv_hbm, ds_ref, kpad, vpad, dkacc, dvacc, sems):
        i = pl.program_id(0)

        @pl.when(i == 0)
        def _():
            _load_padded_kv(qkv_hbm, kpad, vpad, sems, s)
            dkacc[...] = jnp.zeros_like(dkacc)
            dvacc[...] = jnp.zeros_like(dvacc)
            ds_ref[...] = jnp.zeros_like(ds_ref)

        base = pl.multiple_of(i * QB, QB)
        kw = kpad[pl.ds(base, KW), :]
        vw = vpad[pl.ds(base, KW), :]
        q = _scaled(q_ref[...])
        dov = do_ref[...]
        heads = [slice(HEAD_DIM * h, HEAD_DIM * (h + 1)) for h in range(N_HEADS)]
        pb = p_ref[...]
        p = pb.astype(F32)
        dp = jnp.stack([lax.dot_general(dov[:, hs], vw[:, hs], _DIMS["nt"], preferred_element_type=F32) for hs in heads])
        ds = p * (dp - jnp.sum(dp * p, axis=-1, keepdims=True))
        ds_ref[...] += ds
        dsb = ds.astype(BF16)
        dvs = [lax.dot_general(pb[h], dov[:, hs], _DIMS["tn"], preferred_element_type=F32) for h, hs in enumerate(heads)]
        dqs = [jnp.dot(dsb[h], kw[:, hs], preferred_element_type=F32) for h, hs in enumerate(heads)]
        dks = [lax.dot_general(dsb[h], q[:, hs], _DIMS["tn"], preferred_element_type=F32) for h, hs in enumerate(heads)]
        dq_ref[...] = (jnp.concatenate(dqs, axis=1) * (HEAD_DIM ** -0.5)).astype(BF16)
        dkacc[pl.ds(base, KW), :] += jnp.concatenate(dks, axis=1)
        dvacc[pl.ds(base, KW), :] += jnp.concatenate(dvs, axis=1)

        @pl.when(i == n - 1)
        def _():
            def cast(j, carry):
                rows = pl.ds(pl.multiple_of(KPAD + j * 512, 512), 512)
                kpad[rows, :] = dkacc[rows, :].astype(BF16)
                vpad[rows, :] = dvacc[rows, :].astype(BF16)
                return carry

            lax.fori_loop(0, s // 512, cast, 0)
            ck = pltpu.make_async_copy(kpad.at[pl.ds(KPAD, s), :], dk_hbm, sems.at[0])
            cv = pltpu.make_async_copy(vpad.at[pl.ds(KPAD, s), :], dv_hbm, sems.at[1])
            ck.start()
            cv.start()
            ck.wait()
            cv.wait()

    blk = pl.BlockSpec((QB, D_ATTN), lambda i: (i, 0))
    acc_shape = jax.ShapeDtypeStruct((s, D_ATTN), BF16)
    return _call(
        body, name=name, grid=(n,),
        in_specs=[blk, pl.BlockSpec(memory_space=pl.ANY), blk, _probs_spec()],
        out_specs=[blk, pl.BlockSpec(memory_space=pl.ANY), pl.BlockSpec(memory_space=pl.ANY), _full((N_HEADS, QB, KW))],
        out_shape=[jax.ShapeDtypeStruct((s, D_ATTN), BF16), acc_shape, acc_shape,
                   jax.ShapeDtypeStruct((N_HEADS, QB, KW), F32)],
        scratch_shapes=[pltpu.VMEM((s + KPAD, D_ATTN), BF16), pltpu.VMEM((s + KPAD, D_ATTN), BF16),
                        pltpu.VMEM((s + KPAD, D_ATTN), F32), pltpu.VMEM((s + KPAD, D_ATTN), F32),
                        pltpu.SemaphoreType.DMA((2,))],
        args=(qkv, qkv, do, probs), rider=rider)


CONV_HALO = 32
CONV_ROWS = 64


def _sigmoid(t):
    return 1.0 / (1.0 + jnp.exp(-t))


CONV_WIN = CONV_ROWS + CONV_HALO - 8


def _row_windows(ref, r0, buf):
    win = ref[pl.ds(r0, CONV_ROWS + CONV_HALO), :]
    for j in range(1, 8):
        buf[j - 1] = win[j:j + CONV_WIN, :]

    def get(o):
        j, a = o % 8, o - o % 8
        if j == 0:
            return ref[pl.ds(r0 + a, CONV_ROWS), :]
        return buf[j - 1, a:a + CONV_ROWS, :]

    return get


def _glu_rows(z_ref, r0, rows):
    a = z_ref[pl.ds(r0, rows), 0:D_CONV]
    b = z_ref[pl.ds(r0, rows), D_CONV:2 * D_CONV]
    return a, _sigmoid(b)


def _conv_fwd(zc, conv_w, conv_b, ln_g, ln_b, name):
    s = zc.shape[0]
    rt = min(256, s)

    def body(z_ref, w_ref, cb_ref, g_ref, b_ref, cv_ref, feat_ref, hpad, shifts):
        hpad[0:CONV_HALO, :] = jnp.zeros((CONV_HALO, D_CONV), F32)

        def glu(i, carry):
            r0 = pl.multiple_of(i * rt, rt)
            a, sb = _glu_rows(z_ref, r0, rt)
            hpad[pl.ds(r0 + CONV_HALO, rt), :] = a * sb
            return carry

        lax.fori_loop(0, s // rt, glu, 0)
        w = w_ref[...]

        def conv(i, carry):
            r0 = pl.multiple_of(i * CONV_ROWS, CONV_ROWS)
            win = _row_windows(hpad, r0, shifts)
            acc = jnp.broadcast_to(cb_ref[...], (CONV_ROWS, D_CONV))
            for k in range(CONV_WIDTH):
                acc = acc + win(2 + k) * w[k:k + 1, :]
            cv_ref[pl.ds(r0, CONV_ROWS), :] = acc
            yhat, _ = _ln_hat(acc)
            y = yhat * g_ref[...] + b_ref[...]
            feat_ref[pl.ds(r0, CONV_ROWS), :] = (y * _sigmoid(y)).astype(BF16)
            return carry

        lax.fori_loop(0, s // CONV_ROWS, conv, 0)

    return pl.pallas_call(
        body, out_shape=[jax.ShapeDtypeStruct((s, D_CONV), F32), jax.ShapeDtypeStruct((s, D_CONV), BF16)],
        scratch_shapes=[pltpu.VMEM((s + CONV_HALO, D_CONV), F32), pltpu.VMEM((7, CONV_WIN, D_CONV), F32)],
        name=name, compiler_params=_cparams(),
    )(zc, conv_w, conv_b, ln_g, ln_b)


def _conv_bwd(dfeat, cv, zc, conv_w, ln_g, ln_b, name):
    s = zc.shape[0]
    rt = min(256, s)

    def body(df_ref, cv_ref, z_ref, w_ref, g_ref, b_ref, dz_ref, dw_ref, dcb_ref, dg_ref, db_ref, hpad, dcvpad, dwacc,
             hshifts, dshifts):
        hpad[0:CONV_HALO, :] = jnp.zeros((CONV_HALO, D_CONV), F32)
        dcvpad[s:, :] = jnp.zeros((CONV_HALO, D_CONV), F32)
        dwacc[...] = jnp.zeros_like(dwacc)
        dcb_ref[...] = jnp.zeros_like(dcb_ref)
        dg_ref[...] = jnp.zeros_like(dg_ref)
        db_ref[...] = jnp.zeros_like(db_ref)

        def pass1(i, carry):
            r0 = pl.multiple_of(i * rt, rt)
            a, sb = _glu_rows(z_ref, r0, rt)
            hpad[pl.ds(r0 + CONV_HALO, rt), :] = a * sb
            cvhat, rstd = _ln_hat(cv_ref[pl.ds(r0, rt), :])
            y = cvhat * g_ref[...] + b_ref[...]
            sg = _sigmoid(y)
            dy = df_ref[pl.ds(r0, rt), :] * (sg * (1.0 + y * (1.0 - sg)))
            dg_ref[...] += jnp.sum(dy * cvhat, axis=0, keepdims=True)
            db_ref[...] += jnp.sum(dy, axis=0, keepdims=True)
            dcv = _ln_hat_bwd(dy * g_ref[...], cvhat, rstd)
            dcb_ref[...] += jnp.sum(dcv, axis=0, keepdims=True)
            dcvpad[pl.ds(r0, rt), :] = dcv
            return carry

        lax.fori_loop(0, s // rt, pass1, 0)
        w = w_ref[...]

        def pass2(i, carry):
            r0 = pl.multiple_of(i * CONV_ROWS, CONV_ROWS)
            dwin = _row_windows(dcvpad, r0, dshifts)
            hwin = _row_windows(hpad, r0, hshifts)
            dcv = dwin(0)
            dh = jnp.zeros((CONV_ROWS, D_CONV), F32)
            for k in range(CONV_WIDTH):
                dh = dh + dwin(30 - k) * w[k:k + 1, :]
                prod = dcv * hwin(2 + k)
                dwacc[8 * k:8 * k + 8, :] += jnp.sum(prod.reshape(CONV_ROWS // 8, 8, D_CONV), axis=0)
            a, sb = _glu_rows(z_ref, r0, CONV_ROWS)
            dz_ref[pl.ds(r0, CONV_ROWS), :] = jnp.concatenate([dh * sb, dh * a * sb * (1.0 - sb)], axis=1).astype(BF16)
            return carry

        lax.fori_loop(0, s // CONV_ROWS, pass2, 0)
        dw_ref[...] = jnp.sum(dwacc[...].reshape(32, 8, D_CONV), axis=1)

    vs = jax.ShapeDtypeStruct((1, D_CONV), F32)
    return pl.pallas_call(
        body,
        out_shape=[jax.ShapeDtypeStruct((s, 2 * D_CONV), BF16), jax.ShapeDtypeStruct((32, D_CONV), F32), vs, vs, vs],
        scratch_shapes=[pltpu.VMEM((s + CONV_HALO, D_CONV), F32), pltpu.VMEM((s + CONV_HALO, D_CONV), F32),
                        pltpu.VMEM((256, D_CONV), F32), pltpu.VMEM((7, CONV_WIN, D_CONV), F32),
                        pltpu.VMEM((7, CONV_WIN, D_CONV), F32)],
        name=name, compiler_params=_cparams(),
    )(dfeat, cv, zc, conv_w, ln_g, ln_b)


def _branch_out(feats, wts, name):
    s = feats[0].shape[0]
    tm = min(1024, s)

    def body(*refs):
        for f_ref, w_ref, o_ref in zip(refs[:3], refs[3:6], refs[6:]):
            o_ref[...] = lax.dot_general(f_ref[...], w_ref[...], _DIMS["nt"], preferred_element_type=F32).astype(BF16)

    row = pl.BlockSpec((tm, D_MODEL), lambda i: (i, 0))
    sh = jax.ShapeDtypeStruct((s, D_MODEL), BF16)
    return pl.pallas_call(
        body, grid=(s // tm,),
        in_specs=[pl.BlockSpec((tm, f.shape[1]), lambda i: (i, 0)) for f in feats] + [_full(w.shape) for w in wts],
        out_specs=[row] * 3, out_shape=[sh] * 3, name=name, compiler_params=_cparams(),
    )(*feats, *wts)


def _branch_in_bwd(dys, wts, out_dtypes, name):
    s = dys[0].shape[0]
    tm = min(1024, s)

    def body(*refs):
        for d_ref, w_ref, o_ref in zip(refs[:3], refs[3:6], refs[6:]):
            o_ref[...] = jnp.dot(d_ref[...], w_ref[...], preferred_element_type=F32).astype(o_ref.dtype)

    row = pl.BlockSpec((tm, D_MODEL), lambda i: (i, 0))
    return pl.pallas_call(
        body, grid=(s // tm,), in_specs=[row] * 3 + [_full(w.shape) for w in wts],
        out_specs=[pl.BlockSpec((tm, w.shape[1]), lambda i: (i, 0)) for w in wts],
        out_shape=[jax.ShapeDtypeStruct((s, w.shape[1]), dt) for w, dt in zip(wts, out_dtypes)],
        name=name, compiler_params=_cparams(),
    )(*dys, *wts)


def _branch_dw(dys, feats, name):
    s = dys[0].shape[0]
    tm = 512

    def body(*refs):
        for d_ref, f_ref, o_ref in zip(refs[:3], refs[3:6], refs[6:]):
            acc = lax.dot_general(d_ref[...], f_ref[...], _DIMS["tn"], preferred_element_type=F32)
            half = acc.shape[1] // 2
            o_ref[0] = acc[:, :half].astype(BF16)
            o_ref[1] = acc[:, half:].astype(BF16)

    return pl.pallas_call(
        body, grid=(D_MODEL // tm,),
        in_specs=[pl.BlockSpec((s, tm), lambda i: (0, i))] * 3 + [_full(f.shape) for f in feats],
        out_specs=[pl.BlockSpec((2, tm, f.shape[1] // 2), lambda i: (0, i, 0)) for f in feats],
        out_shape=[jax.ShapeDtypeStruct((2, D_MODEL, f.shape[1] // 2), BF16) for f in feats],
        name=name, compiler_params=_cparams(),
    )(*dys, *feats)


def _merge(zg, b_gate, ys, name):
    s = zg.shape[0]
    tm = _row_tile(s)

    def body(zg_ref, bg_ref, y0_ref, y1_ref, y2_ref, o_ref):
        acc = None
        for j, y_ref in enumerate((y0_ref, y1_ref, y2_ref)):
            cs = slice(D_MODEL * j, D_MODEL * (j + 1))
            t = _sigmoid(zg_ref[:, cs] + bg_ref[:, cs]) * y_ref[...]
            acc = t if acc is None else acc + t
        o_ref[...] = acc.astype(BF16)

    row = pl.BlockSpec((tm, D_MODEL), lambda i: (i, 0))
    return pl.pallas_call(
        body, grid=(s // tm,),
        in_specs=[pl.BlockSpec((tm, 3 * D_MODEL), lambda i: (i, 0)), _full((1, 3 * D_MODEL)), row, row, row],
        out_specs=row, out_shape=jax.ShapeDtypeStruct((s, D_MODEL), BF16), name=name, compiler_params=_cparams(),
    )(zg, b_gate, *ys)


def _merge_bwd(dmix, w_o, zg, b_gate, ys, name):
    s = zg.shape[0]
    tm = min(256, s)

    def body(dmix_ref, wo_ref, zg_ref, bg_ref, y0_ref, y1_ref, y2_ref, d0_ref, d1_ref, d2_ref, dzg_ref, dbg_ref):
        first = pl.program_id(0) == 0

        @pl.when(first)
        def _():
            dbg_ref[...] = jnp.zeros_like(dbg_ref)

        dmv = lax.dot_general(dmix_ref[...], wo_ref[...], _DIMS["nt"], preferred_element_type=F32)
        for j, (y_ref, d_ref) in enumerate(((y0_ref, d0_ref), (y1_ref, d1_ref), (y2_ref, d2_ref))):
            cs = slice(D_MODEL * j, D_MODEL * (j + 1))
            g = _sigmoid(zg_ref[:, cs] + bg_ref[:, cs])
            d_ref[...] = (dmv * g).astype(BF16)
            dzg = dmv * y_ref[...] * g * (1.0 - g)
            dzg_ref[:, cs] = dzg.astype(BF16)
            dbg_ref[:, cs] += jnp.sum(dzg, axis=0, keepdims=True)

    row = pl.BlockSpec((tm, D_MODEL), lambda i: (i, 0))
    wide = pl.BlockSpec((tm, 3 * D_MODEL), lambda i: (i, 0))
    yb = jax.ShapeDtypeStruct((s, D_MODEL), BF16)
    return pl.pallas_call(
        body, grid=(s // tm,),
        in_specs=[row, _full(w_o.shape), wide, _full((1, 3 * D_MODEL)), row, row, row],
        out_specs=[row, row, row, wide, _full((1, 3 * D_MODEL))],
        out_shape=[yb, yb, yb, jax.ShapeDtypeStruct((s, 3 * D_MODEL), BF16), jax.ShapeDtypeStruct((1, 3 * D_MODEL), F32)],
        name=name, compiler_params=_cparams(),
    )(dmix, w_o, zg, b_gate, *ys)


def _ff_hidden(u2, w_ff1t, b_ff1, name, rider=None):
    s = u2.shape[0]
    tm, tn = min(2048, s), 1024

    def body(a_ref, b_ref, bias_ref, pre_ref, h_ref):
        acc = lax.dot_general(a_ref[...], b_ref[...], _DIMS["nt"], preferred_element_type=F32) + bias_ref[...]
        pre_ref[...] = acc.astype(BF16)
        h_ref[...] = _relu2(acc).astype(BF16)

    blk = pl.BlockSpec((tm, tn), lambda i, j: (i, j))
    sh = jax.ShapeDtypeStruct((s, D_FF), BF16)
    res = _call(body, name=name, grid=(s // tm, D_FF // tn),
                in_specs=[pl.BlockSpec((tm, D_MODEL), lambda i, j: (i, 0)), pl.BlockSpec((tn, D_MODEL), lambda i, j: (j, 0)),
                          pl.BlockSpec((1, tn), lambda i, j: (0, j))],
                out_specs=[blk, blk], out_shape=[sh, sh], scratch_shapes=[], args=(u2, w_ff1t, b_ff1), rider=rider)
    return tuple(res) if rider is None else (tuple(res[0]), res[1])


def _ff_hidden_bwd(dff, w_ff2, hpre, name, rider=None):
    s = dff.shape[0]
    tm, tn = min(1024, s), 1024

    def body(a_ref, b_ref, h_ref, o_ref, sum_ref):
        dh = lax.dot_general(a_ref[...], b_ref[...], _DIMS["nt"], preferred_element_type=F32)
        dpre = dh * (2.0 * jnp.maximum(h_ref[...].astype(F32), 0.0))
        o_ref[...] = dpre.astype(BF16)
        _acc_rows(sum_ref, dpre, pl.program_id(1) == 0)

    res = _call(
        body, name=name, grid=(D_FF // tn, s // tm),
        in_specs=[pl.BlockSpec((tm, D_MODEL), lambda j, i: (i, 0)), pl.BlockSpec((tn, D_MODEL), lambda j, i: (j, 0)),
                  pl.BlockSpec((tm, tn), lambda j, i: (i, j))],
        out_specs=[pl.BlockSpec((tm, tn), lambda j, i: (i, j)), pl.BlockSpec((1, tn), lambda j, i: (0, j))],
        out_shape=[jax.ShapeDtypeStruct((s, D_FF), BF16), jax.ShapeDtypeStruct((1, D_FF), F32)],
        scratch_shapes=[], args=(dff, w_ff2, hpre), rider=rider)
    return tuple(res) if rider is None else (tuple(res[0]), res[1])


def _silu(t):
    return t * _sigmoid(t)


def _mod_fwd(c_all, w_ada_sh, b_ada_sh, name):
    cols = w_ada_sh.shape[2]

    def body(c_ref, w_ref, b_ref, o_ref):
        ca = _silu(c_ref[...]).astype(BF16)
        o_ref[0] = jnp.dot(ca, w_ref[0].astype(BF16), preferred_element_type=F32) + b_ref[0]

    return pl.pallas_call(
        body, grid=(DEPTH,),
        in_specs=[_full((N_DEV, D_MODEL)), pl.BlockSpec((1, D_MODEL, cols), lambda l: (l, 0, 0)),
                  pl.BlockSpec((1, 1, cols), lambda l: (l, 0, 0))],
        out_specs=pl.BlockSpec((1, N_DEV, cols), lambda l: (l, 0, 0)),
        out_shape=jax.ShapeDtypeStruct((DEPTH, N_DEV, cols), F32), name=name, compiler_params=_cparams(),
    )(c_all, w_ada_sh, b_ada_sh)


def _mod_bwd(c_all, dmod_sh, name):
    cols = dmod_sh.shape[2]

    def body(c_ref, d_ref, o_ref):
        ca = _silu(c_ref[...])
        o_ref[0] = lax.dot_general(ca, d_ref[0], _DIMS["tn"], precision=lax.Precision.HIGHEST,
                                   preferred_element_type=F32)

    return pl.pallas_call(
        body, grid=(DEPTH,),
        in_specs=[_full((N_DEV, D_MODEL)), pl.BlockSpec((1, N_DEV, cols), lambda l: (l, 0, 0))],
        out_specs=pl.BlockSpec((1, D_MODEL, cols), lambda l: (l, 0, 0)),
        out_shape=jax.ShapeDtypeStruct((DEPTH, D_MODEL, cols), F32), name=name, compiler_params=_cparams(),
    )(c_all, dmod_sh)


def _flat_tiles(rows, cols, itemsize_total):
    budget = 12 * 1024 * 1024
    tr = rows
    while tr % 32 == 0 and tr * cols * itemsize_total > budget:
        tr //= 2
    return tr


def _sum_cores(dws, recvs, place, name):
    k = len(dws)

    def body(place_ref, *refs):
        for a_ref, b_ref, o_ref in zip(refs[:k], refs[k:2 * k], refs[2 * k:]):
            o_ref[...] = (a_ref[...].astype(F32) + b_ref[...].astype(F32)).astype(BF16)

    whole = [pl.BlockSpec(a.shape[1:], lambda i, pr: (0, 0)) for a in dws]
    mine = [pl.BlockSpec((None,) + a.shape[1:], lambda i, pr: (pr[0], 0, 0)) for a in dws]
    grid_spec = pltpu.PrefetchScalarGridSpec(num_scalar_prefetch=1, grid=(1,), in_specs=mine + whole, out_specs=whole)
    return pl.pallas_call(body, grid_spec=grid_spec, out_shape=[jax.ShapeDtypeStruct(a.shape[1:], BF16) for a in dws],
                          name=name, compiler_params=_cparams())(place, *dws, *recvs)


def _sum_chips(hs, rs, place, name):
    k = len(hs)

    def body(place_ref, *refs):
        for h_ref, r_ref, o_ref in zip(refs[:k], refs[k:2 * k], refs[2 * k:]):
            o_ref[...] = ((h_ref[...].astype(F32) + r_ref[0].astype(F32)) + r_ref[1].astype(F32)) + r_ref[2].astype(F32)

    own = [pl.BlockSpec((None,) + h.shape[1:], lambda i, pr: (pr[1], 0, 0)) for h in hs]
    got = [pl.BlockSpec(r.shape, lambda i, pr: (0, 0, 0)) for r in rs]
    out = [pl.BlockSpec(h.shape[1:], lambda i, pr: (0, 0)) for h in hs]
    grid_spec = pltpu.PrefetchScalarGridSpec(num_scalar_prefetch=1, grid=(1,), in_specs=own + got, out_specs=out)
    return pl.pallas_call(body, grid_spec=grid_spec, out_shape=[jax.ShapeDtypeStruct(h.shape[1:], F32) for h in hs],
                          name=name, compiler_params=_cparams())(place, *hs, *rs)


def _adam_math(w, g, m, v):
    m2 = ADAM_B1 * m + (1.0 - ADAM_B1) * g
    v2 = ADAM_B2 * v + (1.0 - ADAM_B2) * (g * g)
    m_hat = m2 / (1.0 - ADAM_B1 ** ADAM_STEP)
    v_hat = v2 / (1.0 - ADAM_B2 ** ADAM_STEP)
    delta = -ADAM_LR * (m_hat / (jnp.sqrt(v_hat) + ADAM_EPS) + ADAM_WD * w)
    return delta, m2, v2


def _adamw(w, m, v, grads, name):
    r, c = w.shape
    tr = _flat_tiles(r, c, 4 * (7 + len(grads)))

    def body(*refs):
        w_ref, m_ref, v_ref = refs[:3]
        g_refs = refs[3:3 + len(grads)]
        g_ref, d_ref, m2_ref, v2_ref = refs[3 + len(grads):]
        g = g_refs[0][...]
        for gr in g_refs[1:]:
            g = g + gr[...]
        delta, m2, v2 = _adam_math(w_ref[...], g, m_ref[...], v_ref[...])
        g_ref[...] = g
        d_ref[...] = delta
        m2_ref[...] = m2
        v2_ref[...] = v2

    blk = pl.BlockSpec((tr, c), lambda i: (i, 0))
    sh = jax.ShapeDtypeStruct((r, c), F32)
    return pl.pallas_call(body, grid=(r // tr,), in_specs=[blk] * (3 + len(grads)), out_specs=[blk] * 4,
                          out_shape=[sh] * 4, name=name, compiler_params=_cparams())(w, m, v, *grads)


def _adamw_halves(w, m, v, own, other, place, split, name):
    nl, r, c = w.shape
    hr, hc = own[0].shape
    tr = _flat_tiles(hr, hc, 4 * (7 + 2 * nl))
    nt = hr // tr
    if split == "rows":
        w_spec = pl.BlockSpec((None, tr, c), lambda l, h, t, pr: (l, h * nt + t, 0))
    else:
        w_spec = pl.BlockSpec((None, tr, hc), lambda l, h, t, pr: (l, t, h))

    def g_spec(layer, mine):
        return pl.BlockSpec((tr, hc), lambda l, h, t, pr: (jnp.where((l == layer) & ((h == pr[0]) == mine), t, nt - 1), 0))

    def body(place_ref, w_ref, m_ref, v_ref, *refs):
        own_refs, other_refs = refs[:nl], refs[nl:2 * nl]
        g_ref, d_ref, m2_ref, v2_ref = refs[2 * nl:]
        layer = pl.program_id(0)
        mine = pl.program_id(1) == place_ref[0]
        g = None
        for li in range(nl):
            cand = jnp.where(mine, own_refs[li][...], other_refs[li][...])
            g = cand if g is None else jnp.where(layer == li, cand, g)
        delta, m2, v2 = _adam_math(w_ref[...], g, m_ref[...], v_ref[...])
        g_ref[...] = g
        d_ref[...] = delta
        m2_ref[...] = m2
        v2_ref[...] = v2

    sh = jax.ShapeDtypeStruct((nl, r, c), F32)
    g_specs = [g_spec(li, True) for li in range(nl)] + [g_spec(li, False) for li in range(nl)]
    return _call(body, name=name, grid=(nl, 2, nt), in_specs=[w_spec] * 3 + g_specs, out_specs=[w_spec] * 4,
                 out_shape=[sh] * 4, scratch_shapes=[], args=(w, m, v, *own, *other), prefetch=(place,))


def _adamw_small(w, m, v, g_all, name):
    r, c = w.shape

    def body(w_ref, m_ref, v_ref, g_ref, go_ref, d_ref, m2_ref, v2_ref):
        g = g_ref[0]
        for b in range(1, N_DEV):
            g = g + g_ref[b]
        delta, m2, v2 = _adam_math(w_ref[...], g, m_ref[...], v_ref[...])
        go_ref[...] = g
        d_ref[...] = delta
        m2_ref[...] = m2
        v2_ref[...] = v2

    sh = jax.ShapeDtypeStruct((r, c), F32)
    return pl.pallas_call(body, out_shape=[sh] * 4, name=name, compiler_params=_cparams())(w, m, v, g_all)


def _me():
    return lax.axis_index("x"), lax.axis_index("y"), lax.axis_index("c")


def _flip(v, bit):
    return 1 - v if bit else v


def _allgather_small(blk, name):
    r, c = blk.shape

    def body(x_ref, o_ref, send_sems, recv_sems):
        x, y, cc = _me()
        me = 4 * x + 2 * y + cc
        copies = []
        for k in range(1, N_DEV):
            peer = (_flip(x, k & 4), _flip(y, k & 2), _flip(cc, k & 1))
            cp = pltpu.make_async_remote_copy(src_ref=x_ref, dst_ref=o_ref.at[me], send_sem=send_sems.at[k - 1],
                                              recv_sem=recv_sems.at[k - 1], device_id=peer, device_id_type=MESH)
            cp.start()
            copies.append(cp)
        o_ref[me] = x_ref[...]
        for cp in copies:
            cp.wait()

    return pl.pallas_call(
        body, out_shape=jax.ShapeDtypeStruct((N_DEV, r, c), F32),
        in_specs=[pl.BlockSpec(memory_space=pltpu.VMEM)], out_specs=pl.BlockSpec(memory_space=pltpu.VMEM),
        scratch_shapes=[pltpu.SemaphoreType.DMA((N_DEV - 1,)), pltpu.SemaphoreType.DMA((N_DEV - 1,))],
        name=name, compiler_params=_cparams(),
    )(blk)


class _Rider:
    def __init__(self, arrays, out_shapes, scratch_shapes, start, finish):
        self.arrays, self.out_shapes, self.scratch_shapes = list(arrays), list(out_shapes), list(scratch_shapes)
        self.start, self.finish = start, finish


def _call(body, *, name, grid, in_specs, out_specs, out_shape, scratch_shapes, args, rider=None, prefetch=()):
    npf = len(prefetch)

    def launch(fn, in_specs, out_specs, out_shape, scratch_shapes, args):
        grid_spec = pltpu.PrefetchScalarGridSpec(num_scalar_prefetch=npf, grid=grid, in_specs=in_specs,
                                                 out_specs=out_specs, scratch_shapes=scratch_shapes)
        return pl.pallas_call(fn, grid_spec=grid_spec, out_shape=out_shape, name=name,
                              compiler_params=_cparams())(*prefetch, *args)

    if rider is None:
        return launch(body, list(in_specs), list(out_specs), list(out_shape), list(scratch_shapes), args)
    ni, no, ns = len(in_specs), len(out_specs), len(scratch_shapes)
    ri, ro = len(rider.arrays), len(rider.out_shapes)
    steps = int(np.prod(grid))

    def wrapped(*refs):
        pf, refs = refs[:npf], refs[npf:]
        h_in, r_in = refs[:ni], refs[ni:ni + ri]
        h_out, r_out = refs[ni + ri:ni + ri + no], refs[ni + ri + no:ni + ri + no + ro]
        h_scr, r_scr = refs[ni + ri + no + ro:ni + ri + no + ro + ns], refs[ni + ri + no + ro + ns:]
        step = pl.program_id(0)
        for d in range(1, len(grid)):
            step = step * grid[d] + pl.program_id(d)

        @pl.when(step == 0)
        def _():
            rider.start(r_in, r_out, r_scr)

        body(*pf, *h_in, *h_out, *h_scr)

        @pl.when(step == steps - 1)
        def _():
            rider.finish(r_in, r_out, r_scr)

    anyspec = pl.BlockSpec(memory_space=pl.ANY)
    res = launch(wrapped, list(in_specs) + [anyspec] * ri, list(out_specs) + [anyspec] * ro,
                 list(out_shape) + rider.out_shapes, list(scratch_shapes) + rider.scratch_shapes,
                 list(args) + rider.arrays)
    return res[:no], res[no:]


def _run_rider(rider, name):
    ri = len(rider.arrays)

    def body(*refs):
        r_in, r_out, r_scr = refs[:ri], refs[ri:ri + len(rider.out_shapes)], refs[ri + len(rider.out_shapes):]
        rider.start(r_in, r_out, r_scr)
        rider.finish(r_in, r_out, r_scr)

    anyspec = pl.BlockSpec(memory_space=pl.ANY)
    return pl.pallas_call(body, in_specs=[anyspec] * ri, out_specs=[anyspec] * len(rider.out_shapes),
                          out_shape=rider.out_shapes, scratch_shapes=rider.scratch_shapes, name=name,
                          compiler_params=_cparams())(*rider.arrays)


def _allgather_rider(blk):
    def copies(ins, outs, scr):
        send_sems, recv_sems, loc_sems, stage = scr
        x, y, cc = _me()
        me = 4 * x + 2 * y + cc
        remote = [pltpu.make_async_remote_copy(
            src_ref=ins[0], dst_ref=outs[0].at[me], send_sem=send_sems.at[k - 1], recv_sem=recv_sems.at[k - 1],
            device_id=(_flip(x, k & 4), _flip(y, k & 2), _flip(cc, k & 1)), device_id_type=MESH) for k in range(1, N_DEV)]
        return remote, pltpu.make_async_copy(ins[0], stage, loc_sems.at[0]), (outs[0].at[me], stage, loc_sems.at[1])

    def start(ins, outs, scr):
        remote, lin, _ = copies(ins, outs, scr)
        lin.start()
        for cp in remote:
            cp.start()

    def finish(ins, outs, scr):
        remote, lin, (dst, stage, sem) = copies(ins, outs, scr)
        lin.wait()
        lout = pltpu.make_async_copy(stage, dst, sem)
        lout.start()
        for cp in remote:
            cp.wait()
        lout.wait()

    return _Rider([blk], [jax.ShapeDtypeStruct((N_DEV,) + blk.shape, blk.dtype)],
                  [pltpu.SemaphoreType.DMA((N_DEV - 1,)), pltpu.SemaphoreType.DMA((N_DEV - 1,)),
                   pltpu.SemaphoreType.DMA((2,)), pltpu.VMEM(blk.shape, blk.dtype)], start, finish)


def _gather_rider(shards):
    n = len(shards)

    def copies(ins, outs, scr, relay=True):
        ici_send, ici_recv, d2d_send, d2d_recv, loc_sems = scr[:5]
        stage = scr[5:]
        x, y, cc = _me()
        chip = 2 * x + y
        sibling = (x, y, 1 - cc)
        local, sends, relays = [], [], []
        for j in range(n):
            def rows(ch, h, j=j):
                return outs[j].at[ch, h]

            lc = pltpu.make_async_copy(ins[j], stage[j], loc_sems.at[j])
            local.append((lc, pltpu.make_async_copy(stage[j], outs[j].at[chip], loc_sems.at[n + j]) if relay else None))
            for k in range(1, N_CHIP):
                px, py = _flip(x, k & 2), _flip(y, k & 1)
                pchip = 2 * px + py
                q = 3 * j + k - 1
                out_cp = pltpu.make_async_remote_copy(src_ref=ins[j].at[cc], dst_ref=rows(chip, cc),
                                                      send_sem=ici_send.at[q], recv_sem=ici_recv.at[q],
                                                      device_id=(px, py, cc), device_id_type=MESH)
                sends.append(out_cp)
                if not relay:
                    continue
                arrival = pltpu.make_async_remote_copy(src_ref=rows(pchip, cc), dst_ref=rows(pchip, cc),
                                                       send_sem=ici_send.at[q], recv_sem=ici_recv.at[q],
                                                       device_id=(px, py, cc), device_id_type=MESH)
                forward = pltpu.make_async_remote_copy(src_ref=rows(pchip, cc), dst_ref=rows(pchip, cc),
                                                       send_sem=d2d_send.at[q], recv_sem=d2d_recv.at[q],
                                                       device_id=sibling, device_id_type=MESH)
                from_sibling = pltpu.make_async_remote_copy(src_ref=rows(pchip, 1 - cc), dst_ref=rows(pchip, 1 - cc),
                                                            send_sem=d2d_send.at[q], recv_sem=d2d_recv.at[q],
                                                            device_id=sibling, device_id_type=MESH)
                relays.append((arrival, forward, from_sibling))
        return local, sends, relays

    def start(ins, outs, scr):
        local, sends, _ = copies(ins, outs, scr, relay=False)
        for lin, _ in local:
            lin.start(priority=1)
        for cp in sends:
            cp.start()

    def finish(ins, outs, scr):
        local, sends, relays = copies(ins, outs, scr)
        for lin, lout in local:
            lin.wait()
            lout.start(priority=1)
        for arrival, forward, _ in relays:
            arrival.wait_recv()
            forward.start()
        for cp in sends:
            cp.wait_send()
        for _, forward, from_sibling in relays:
            forward.wait_send()
            from_sibling.wait_recv()
        for _, lout in local:
            lout.wait()

    scratch = [pltpu.SemaphoreType.DMA((3 * n,)), pltpu.SemaphoreType.DMA((3 * n,)), pltpu.SemaphoreType.DMA((3 * n,)),
               pltpu.SemaphoreType.DMA((3 * n,)), pltpu.SemaphoreType.DMA((2 * n,))]
    scratch += [pltpu.VMEM(a.shape, a.dtype) for a in shards]
    return _Rider(shards, [jax.ShapeDtypeStruct((N_CHIP,) + a.shape, a.dtype) for a in shards], scratch, start, finish)


def _sibling_rider(arrs, other_half=False):
    n = len(arrs)

    def copies(ins, outs, scr):
        send_sems, recv_sems = scr
        x, y, cc = _me()
        return [pltpu.make_async_remote_copy(
            src_ref=ins[j].at[1 - cc] if other_half else ins[j], dst_ref=outs[j], send_sem=send_sems.at[j],
            recv_sem=recv_sems.at[j], device_id=(x, y, 1 - cc), device_id_type=MESH) for j in range(n)]

    def start(ins, outs, scr):
        for cp in copies(ins, outs, scr):
            cp.start()

    def finish(ins, outs, scr):
        for cp in copies(ins, outs, scr):
            cp.wait()

    return _Rider(arrs, [jax.ShapeDtypeStruct(a.shape[1:] if other_half else a.shape, a.dtype) for a in arrs],
                  [pltpu.SemaphoreType.DMA((n,)), pltpu.SemaphoreType.DMA((n,))], start, finish)


def _sibling_send(arrs, name, other_half=False):
    return _run_rider(_sibling_rider(arrs, other_half), name)


def _join_riders(first, second):
    ni, no, ns = len(first.arrays), len(first.out_shapes), len(first.scratch_shapes)

    def split(ins, outs, scr):
        return (ins[:ni], outs[:no], scr[:ns]), (ins[ni:], outs[no:], scr[ns:])

    def start(ins, outs, scr):
        a, b = split(ins, outs, scr)
        first.start(*a)
        second.start(*b)

    def finish(ins, outs, scr):
        a, b = split(ins, outs, scr)
        first.finish(*a)
        second.finish(*b)

    return _Rider(first.arrays + second.arrays, first.out_shapes + second.out_shapes,
                  first.scratch_shapes + second.scratch_shapes, start, finish)


def _scatter_rider(arrs):
    n = len(arrs)

    def copies(ins, outs, scr):
        send_sems, recv_sems = scr
        x, y, cc = _me()
        cps = []
        for j in range(n):
            for k in range(1, N_CHIP):
                px, py = _flip(x, k & 2), _flip(y, k & 1)
                cps.append(pltpu.make_async_remote_copy(
                    src_ref=ins[j].at[2 * px + py], dst_ref=outs[j].at[k - 1], send_sem=send_sems.at[3 * j + k - 1],
                    recv_sem=recv_sems.at[3 * j + k - 1], device_id=(px, py, cc), device_id_type=MESH))
        return cps

    def start(ins, outs, scr):
        for cp in copies(ins, outs, scr):
            cp.start()

    def finish(ins, outs, scr):
        for cp in copies(ins, outs, scr):
            cp.wait()

    return _Rider(arrs, [jax.ShapeDtypeStruct((N_CHIP - 1,) + a.shape[1:], a.dtype) for a in arrs],
                  [pltpu.SemaphoreType.DMA((3 * n,)), pltpu.SemaphoreType.DMA((3 * n,))], start, finish)


COL_SHARDED = ("w_in", "w_br_pool", "w_br_attn", "w_br_conv", "w_ff1")
ROW_SHARDED = ("w_o", "w_ff2")
BIG = COL_SHARDED + ROW_SHARDED
SMALL = ("b_ada", "b_gate", "w_pool", "pool_scale", "rel_bias", "conv_w", "conv_b", "conv_ln_g", "conv_ln_b",
         "ln_mix_g", "ln_mix_b", "b_ff1", "b_ff2", "ln_ff_g", "ln_ff_b")
PACK_W = 1024


def _pack(parts):
    rows = []
    for a in parts:
        flat = a.reshape(-1)
        n = -(-flat.shape[0] // PACK_W) * PACK_W
        rows.append(jnp.pad(flat, (0, n - flat.shape[0])).reshape(-1, PACK_W))
    out = jnp.concatenate(rows, axis=0)
    r = -(-out.shape[0] // 8) * 8
    return jnp.pad(out, ((0, r - out.shape[0]), (0, 0)))


def _unpack(packed, shapes):
    out, r0 = [], 0
    for shp in shapes:
        size = int(np.prod(shp))
        nr = -(-size // PACK_W)
        out.append(packed[r0:r0 + nr].reshape(-1)[:size].reshape(shp))
        r0 += nr
    return out


def _hosted(fn, hook, *args, **kw):
    if hook is None:
        return fn(*args, **kw)
    res, rider_out = fn(*args, rider=hook[0], **kw)
    hook[1](rider_out)
    return res


def _layer_fwd(l, x, mod, W, P, hooks=None, u=None):
    hooks = hooks or {}
    s = x.shape[0]
    sh_m, sc_m, g_m, sh_f, sc_f, g_f = [mod[l:l + 1, D_MODEL * j:D_MODEL * (j + 1)] for j in range(6)]
    n = lambda t: f"{t}{l}"
    w_in = W["w_in"][l]
    if u is None:
        u = _ln_mod(x, sc_m, sh_m, n("ln_mod_mix"))
    zp = _mm(u, w_in, "nt", tm=s, tn=256, out_dtype=F32, name=n("z_pool"), b_col0=0, n_out=D_POOL)
    qkv = _mm(u, w_in, "nt", tm=s, tn=256, out_dtype=BF16, name=n("z_qkv"), b_col0=OFF_QKV // 256, n_out=3 * D_ATTN)
    zc = _mm(u, w_in, "nt", tm=s, tn=256, out_dtype=F32, name=n("z_conv"), b_col0=OFF_CONV // 256, n_out=2 * D_CONV)
    zg = _hosted(_mm, hooks.get("z_gate"), u, w_in, "nt", tm=min(2048, s), tn=768, out_dtype=BF16, name=n("z_gate"),
                 b_col0=OFF_GATE // 768, n_out=3 * D_MODEL)

    p, feat_pool = _pool_fwd(zp, P["wp_bd"][l], P["pool_scale"][l], n("pool_fwd"))
    bias = _bias_block(P["rel_bias"][l], n("bias_block"))
    o, probs = _hosted(_attn_fwd, hooks.get("attn"), qkv, bias, n("attn_fwd"))
    cv, feat_conv = _conv_fwd(zc, P["conv_w"][l], P["conv_b"][l], P["conv_ln_g"][l], P["conv_ln_b"][l], n("conv_fwd"))

    branch_w = (W["w_br_pool"][l], W["w_br_attn"][l], W["w_br_conv"][l])
    ys = tuple(_branch_out((feat_pool, o, feat_conv), branch_w, n("branch_out")))
    merged = _merge(zg, P["b_gate"][l], ys, n("merge"))
    mix, x1, u2 = _mm_resid_ln(merged, W["w_o"][l], None, x, g_m, P["ln_mix_g"][l], P["ln_mix_b"][l], n("mix_out"),
                               mod_next=(sc_f, sh_f))

    hpre, hid = _hosted(_ff_hidden, hooks.get("ff1"), u2, W["w_ff1"][l], P["b_ff1"][l], n("ff1"))
    above = None if l + 1 == mod.shape[0] else (mod[l + 1:l + 2, D_MODEL:2 * D_MODEL], mod[l + 1:l + 2, 0:D_MODEL])
    ff, x2, *u_next = _hosted(_mm_resid_ln, hooks.get("ff2"), hid, W["w_ff2"][l], P["b_ff2"][l], x1, g_f,
                              P["ln_ff_g"][l], P["ln_ff_b"][l], n("ff2"), mod_next=above)
    saved = dict(x=x, u=u, zp=zp, qkv=qkv, zc=zc, zg=zg, p=p, feat_pool=feat_pool, probs=probs, o=o, cv=cv,
                 feat_conv=feat_conv, ys=ys, merged=merged, mix=mix, x1=x1, u2=u2, hpre=hpre, hid=hid, ff=ff,
                 u_next=u_next[0] if u_next else None)
    return x2, saved


def _layer_bwd(l, dx2, mod, W, P, A, hooks=None, tgt=None, nxt=None):
    hooks = hooks or {}
    sh_m, sc_m, g_m, sh_f, sc_f, g_f = [mod[l:l + 1, D_MODEL * j:D_MODEL * (j + 1)] for j in range(6)]
    n = lambda t: f"{t}{l}"
    gw, gs = {}, {}

    if isinstance(dx2, tuple):
        dres, dff, gs["ln_ff_g"], gs["ln_ff_b"], dg_f, gs["b_ff2"] = dx2
    else:
        dres, dff, gs["ln_ff_g"], gs["ln_ff_b"], dg_f, gs["b_ff2"], *loss_part = _resid_ln_bwd(
            dx2, A["x1"], A["ff"], g_f, P["ln_ff_g"][l], n("resid_ln_ff_bwd"), tgt=tgt)
    gw["w_ff2"] = _mm(A["hid"], dff, "tn", tm=512, tn=1024, out_dtype=BF16, name=n("dw_ff2"), split_n=512)
    hook = hooks["ff_hidden_bwd"](gw) if "ff_hidden_bwd" in hooks else None
    dhpre, gs["b_ff1"] = _hosted(_ff_hidden_bwd, hook, dff, W["w_ff2"][l], A["hpre"], n("ff_hidden_bwd"))
    gw["w_ff1"] = _mm(dhpre, A["u2"], "tn", tm=512, tn=1024, out_dtype=BF16, name=n("dw_ff1"), split_n=512)

    hook = hooks["du_ff"](gw) if "du_ff" in hooks else None
    dres, dmix, dsc_f, dsh_f, gs["ln_mix_g"], gs["ln_mix_b"], dg_m, _ = _hosted(
        _mm_ln_mod_bwd, hook, dhpre, W["w_ff1"][l], A["x1"], sc_f, dres, n("du_ff"),
        nxt=(A["x"], A["mix"], g_m, P["ln_mix_g"][l]))
    gw["w_o"] = _mm(A["merged"], dmix, "tn", tm=512, tn=1024, out_dtype=BF16, name=n("dw_o"), split_n=512)
    dy_pool, dy_attn, dy_conv, dzg, gs["b_gate"] = _merge_bwd(dmix, W["w_o"][l], A["zg"], P["b_gate"][l], A["ys"],
                                                              n("merge_bwd"))

    dys = (dy_pool, dy_attn, dy_conv)
    gw["w_br_pool"], gw["w_br_attn"], gw["w_br_conv"] = _branch_dw(
        dys, (A["feat_pool"], A["o"], A["feat_conv"]), n("dw_branch"))
    dfeat_pool, do, dfeat_conv = _branch_in_bwd(
        dys, (W["w_br_pool"][l], W["w_br_attn"][l], W["w_br_conv"][l]), (F32, BF16, F32), n("d_branch_in"))

    dzp, dwp_bd, gs["pool_scale"] = _pool_bwd(dfeat_pool, A["p"], P["wp_bd"][l], P["pool_scale"][l], n("pool_bwd"))
    gs["w_pool"] = jnp.stack([dwp_bd[POOL_GROUP * g:POOL_GROUP * (g + 1), POOL_GROUP * g:POOL_GROUP * (g + 1)]
                              for g in range(len(POOL_WINDOWS))])
    hook = hooks["attn"](gw) if "attn" in hooks else None
    dq, dk, dv, ds_acc = _hosted(_attn_bwd, hook, A["qkv"], do, A["probs"], n("attn_bwd"))
    gs["rel_bias"] = _bias_block_bwd(ds_acc, n("bias_block_bwd"))
    dzc, dcw, gs["conv_b"], gs["conv_ln_g"], gs["conv_ln_b"] = _conv_bwd(
        dfeat_conv, A["cv"], A["zc"], P["conv_w"][l], P["conv_ln_g"][l], P["conv_ln_b"][l], n("conv_bwd"))
    gs["conv_w"] = dcw[:CONV_WIDTH]

    dz = [dzp, dq, dk, dv, dzc, dzg]
    gw["w_in"] = _dw_segments(dz, A["u"], n("dw_in"))
    hook = hooks["du_mix"](gw) if "du_mix" in hooks else None
    res = _hosted(_mm_ln_mod_bwd, hook, dz, W["w_in"][l], A["x"], sc_m, dres, n("du_mix"), nxt=nxt)
    if nxt is None:
        dx, dsc_m, dsh_m = res
    else:
        dx, dsc_m, dsh_m = (res[0], res[1], *res[4:]), res[2], res[3]
    dmod = jnp.concatenate([dsh_m, dsc_m, dg_m, dsh_f, dsc_f, dg_f], axis=1)
    return (dx, gw, gs, dmod) if tgt is None else (dx, gw, gs, dmod, loss_part[0])


def _small_shapes():
    return {"b_ada": (6 * D_MODEL,), "b_gate": (3 * D_MODEL,), "w_pool": (4, POOL_GROUP, POOL_GROUP),
            "pool_scale": (D_POOL,), "rel_bias": (N_HEADS, N_REL), "conv_w": (CONV_WIDTH, D_CONV),
            "conv_b": (D_CONV,), "conv_ln_g": (D_CONV,), "conv_ln_b": (D_CONV,), "ln_mix_g": (D_MODEL,),
            "ln_mix_b": (D_MODEL,), "b_ff1": (D_FF,), "b_ff2": (D_MODEL,), "ln_ff_g": (D_MODEL,), "ln_ff_b": (D_MODEL,)}


def kernel(x, c, w_ada, b_ada, w_in, b_gate, w_pool, pool_scale, rel_bias, conv_w, conv_b, conv_ln_g, conv_ln_b, w_br_pool, w_br_attn, w_br_conv, w_o, ln_mix_g, ln_mix_b, w_ff1, b_ff1, w_ff2, b_ff2, ln_ff_g, ln_ff_b, loss_target, m_w_ada, m_b_ada, m_w_in, m_b_gate, m_w_pool, m_pool_scale, m_rel_bias, m_conv_w, m_conv_b, m_conv_ln_g, m_conv_ln_b, m_w_br_pool, m_w_br_attn, m_w_br_conv, m_w_o, m_ln_mix_g, m_ln_mix_b, m_w_ff1, m_b_ff1, m_w_ff2, m_b_ff2, m_ln_ff_g, m_ln_ff_b, v_w_ada, v_b_ada, v_w_in, v_b_gate, v_w_pool, v_pool_scale, v_rel_bias, v_conv_w, v_conv_b, v_conv_ln_g, v_conv_ln_b, v_w_br_pool, v_w_br_attn, v_w_br_conv, v_w_o, v_ln_mix_g, v_ln_mix_b, v_w_ff1, v_b_ff1, v_w_ff2, v_b_ff2, v_ln_ff_g, v_ln_ff_b):
    env = dict(locals())
    xi, yi, ci = _me()
    chip = 2 * xi + yi
    me = 4 * xi + 2 * yi + ci
    xs = x[0]
    tgt = loss_target[0]
    L = DEPTH

    first = _allgather_small(jnp.concatenate([c.reshape(8, 128), _pack([conv_w]).reshape(-1, 128)]), "gather_c_conv_w")
    c_all = first[:, :8].reshape(N_DEV, D_MODEL)
    ada_cols = w_ada.shape[2]
    b_ada_sh = lax.dynamic_slice_in_dim(b_ada, chip * ada_cols, ada_cols, axis=1).reshape(L, 1, ada_cols)
    mod_part = _mod_fwd(c_all, w_ada, b_ada_sh, "mod_fwd")

    W = {k: [None] * L for k in BIG}

    def weight_gather(*items):
        shards = [(jnp.swapaxes(env[k][l], 0, 1) if k in COL_SHARDED else env[k][l]).astype(BF16) for k, l in items]
        shards = [a.reshape(2, a.shape[0] // 2, a.shape[1]) for a in shards]

        def done(outs):
            for (k, l), g in zip(items, outs):
                W[k][l] = g.reshape(-1, g.shape[-1])

        return _gather_rider(shards), done

    branch = lambda l: [(k, l) for k in ("w_br_pool", "w_br_attn", "w_br_conv", "w_o")]
    rider, done = weight_gather(("w_in", 0))
    first_out = _run_rider(_join_riders(_allgather_rider(mod_part.reshape(-1, 128)), rider), "gather_mod_w_in0")
    done(first_out[1:])
    mod_g = first_out[0].reshape(N_CHIP, 2, L, N_DEV, ada_cols)[:, 0]
    mod_all = jnp.transpose(mod_g, (1, 2, 0, 3)).reshape(L, N_DEV, 6 * D_MODEL)
    mod = lax.dynamic_index_in_dim(mod_all, me, axis=1, keepdims=False)
    fwd_hooks = [{"z_gate": weight_gather(*branch(0)), "attn": weight_gather(("w_ff1", 0), ("w_ff2", 0)),
                  "ff1": weight_gather(*branch(1)), "ff2": weight_gather(("w_in", 1))},
                 {"attn": weight_gather(("w_ff1", 1), ("w_ff2", 1))}]

    P = {k: env[k] for k in ("rel_bias", "conv_w")}
    for k in ("b_gate", "pool_scale", "conv_b", "conv_ln_g", "conv_ln_b", "ln_mix_g", "ln_mix_b", "b_ff1", "b_ff2",
              "ln_ff_g", "ln_ff_b"):
        P[k] = env[k].reshape(L, 1, -1)
    n_cw = conv_w.size
    cw = first[:, 8:].reshape(N_CHIP, 2, -1)[:, 0, :n_cw].reshape(N_CHIP, L, CONV_WIDTH, D_CONV // N_CHIP)
    P["conv_w"] = jnp.transpose(cw, (1, 2, 0, 3)).reshape(L, CONV_WIDTH, D_CONV)
    wp_bd = jnp.zeros((L, D_POOL, D_POOL), F32)
    for g in range(len(POOL_WINDOWS)):
        sl = slice(POOL_GROUP * g, POOL_GROUP * (g + 1))
        wp_bd = wp_bd.at[:, sl, sl].set(w_pool[:, g])
    P["wp_bd"] = wp_bd.astype(BF16)

    acts = []
    h = xs
    for l in range(L):
        h, saved = _layer_fwd(l, h, mod, W, P, fwd_hooks[l], u=acts[-1]["u_next"] if acts else None)
        acts.append(saved)

    place = jnp.stack([ci, chip, chip ^ 1, chip ^ 2, chip ^ 3]).astype(jnp.int32)
    scattered = {}

    swapped = {}

    def swap_hook(names, l):
        def hook(gw):
            def done(outs):
                swapped.update({(k, l): o for k, o in zip(names, outs)})
            return _sibling_rider([gw[k] for k in names], other_half=True), done
        return hook

    def scatter_hook(names, l, host, then=None):
        def hook(gw):
            todo = [k for k in names if (k, l) not in swapped]
            if todo:
                got = _sibling_send([gw[k] for k in todo], f"swap_blocks_{host}{l}", other_half=True)
                swapped.update({(k, l): o for k, o in zip(todo, got)})
            sums = _sum_cores([gw[k] for k in names], [swapped[(k, l)] for k in names], place, f"sum_cores_{host}{l}")
            both = [hh.reshape(N_CHIP, -1, hh.shape[-1]) for hh in sums]
            rider = _scatter_rider(both)
            more = then(gw) if then is not None else None

            def done(outs):
                for k, hh, r in zip(names, both, outs):
                    scattered[(k, l)] = (hh, r)
                if more is not None:
                    more[1](outs[len(names):])

            return (rider if more is None else _join_riders(rider, more[0])), done
        return hook

    gws, gss, dmods = [None] * L, [None] * L, [None] * L
    dh = h
    for l in reversed(range(L)):
        hooks = {"ff_hidden_bwd": swap_hook(("w_ff2",), l),
                 "du_ff": scatter_hook(("w_ff2",), l, "du_ff", then=swap_hook(("w_ff1",), l)),
                 "attn": scatter_hook(("w_ff1", "w_o", "w_br_pool", "w_br_attn", "w_br_conv"), l, "attn_bwd"),
                 "du_mix": scatter_hook(("w_in",), l, "du_mix")}
        below = None
        if l > 0:
            below = (acts[l - 1]["x1"], acts[l - 1]["ff"], mod[l - 1:l, 5 * D_MODEL:], P["ln_ff_g"][l - 1])
        if l == L - 1:
            dh, gws[l], gss[l], dmods[l], loss_part = _layer_bwd(l, dh, mod, W, P, acts[l], hooks, tgt=tgt, nxt=below)
        else:
            dh, gws[l], gss[l], dmods[l] = _layer_bwd(l, dh, mod, W, P, acts[l], hooks, nxt=below)
    grad_x = dh[None]

    reduced = [[None] * L for _ in BIG]
    groups = (("w_in", "w_br_pool", "w_br_attn", "w_br_conv"), ("w_o", "w_ff1", "w_ff2"))
    for l in range(L):
        for gi, names in enumerate(groups):
            pairs = [scattered[(k, l)] for k in names]
            sums = _sum_chips([p[0] for p in pairs], [p[1] for p in pairs], place, f"sum_chips_{gi}_{l}")
            for k, t in zip(names, sums):
                reduced[BIG.index(k)][l] = t
    flat_reduced = [t for per_weight in reduced for t in per_weight]

    shapes = _small_shapes()
    small_names = [k for k in SMALL if k != "b_ada"]
    dmod_own = jnp.concatenate(dmods, axis=0)
    pack = _pack([dmod_own] + [jnp.stack([gss[l][k].reshape(shapes[k]) for l in range(L)]) for k in small_names]
                 + [loss_part])
    last = _run_rider(_join_riders(_sibling_rider(flat_reduced), _allgather_rider(pack.reshape(-1, 128))),
                      "swap_reduced_gather_small")
    flat_other, g_all = last[:-1], last[-1].reshape(N_DEV, -1, PACK_W)

    out = {}
    for j, k in enumerate(BIG):
        own, other = reduced[j], flat_other[L * j:L * (j + 1)]
        if k == "w_in":
            t = lambda a: jnp.swapaxes(a, 1, 2)
            res = _adamw_halves(t(env[k]), t(env["m_" + k]), t(env["v_" + k]), own, other, place, "cols", f"adamw_{k}")
            res = [t(a) for a in res]
        else:
            if k in COL_SHARDED:
                own, other = [a.T for a in own], [a.T for a in other]
            res = _adamw_halves(env[k], env["m_" + k], env["v_" + k], own, other, place,
                                "rows" if k in COL_SHARDED else "cols", f"adamw_{k}")
        out[k] = tuple(res)

    dmod_all = g_all[:, :L * 6].reshape(N_DEV, L, 6 * D_MODEL)
    dmod_sh = jnp.transpose(lax.dynamic_slice_in_dim(dmod_all, chip * ada_cols, ada_cols, axis=2), (1, 0, 2))
    g_ada = _mod_bwd(c_all, dmod_sh, "mod_bwd")
    g_, d_, m_, v_ = _adamw(w_ada.reshape(-1, ada_cols), m_w_ada.reshape(-1, ada_cols), v_w_ada.reshape(-1, ada_cols),
                            [g_ada.reshape(-1, ada_cols)], "adamw_w_ada")
    out["w_ada"] = tuple(a.reshape(w_ada.shape) for a in (g_, d_, m_, v_))

    def small_pack(prefix):
        parts = [env[prefix + "b_ada"]]
        for k in small_names:
            a = env[prefix + k]
            if k == "conv_w":
                a = jnp.zeros((L,) + shapes[k], F32)
            parts.append(a)
        return _pack(parts + [jnp.zeros_like(loss_part)])

    gp, dp, mp, vp = _adamw_small(small_pack(""), small_pack("m_"), small_pack("v_"), g_all, "adamw_small")
    full_shapes = [(L,) + shapes["b_ada"]] + [(L,) + shapes[k] for k in small_names]
    loss = _unpack(gp, full_shapes + [(128,)])[-1][0]
    for tag, packed in (("g", gp), ("d", dp), ("m", mp), ("v", vp)):
        for k, a in zip(["b_ada"] + small_names, _unpack(packed, full_shapes)):
            out.setdefault(k, {})
            out[k][tag] = a
    g_cw_full = out["conv_w"]["g"]
    cw_cols = D_CONV // N_CHIP
    g_cw = lax.dynamic_slice_in_dim(g_cw_full, chip * cw_cols, cw_cols, axis=2)
    pad_rows = lambda a: jnp.pad(a.reshape(L * CONV_WIDTH, cw_cols), ((0, 2), (0, 0)))
    g_, d_, m_, v_ = _adamw(pad_rows(conv_w), pad_rows(m_conv_w), pad_rows(v_conv_w), [pad_rows(g_cw)], "adamw_conv_w")
    out["conv_w"] = tuple(a[:L * CONV_WIDTH].reshape(L, CONV_WIDTH, cw_cols) for a in (g_, d_, m_, v_))

    names = ["w_ada", "b_ada", "w_in", "b_gate", "w_pool", "pool_scale", "rel_bias", "conv_w", "conv_b", "conv_ln_g",
             "conv_ln_b", "w_br_pool", "w_br_attn", "w_br_conv", "w_o", "ln_mix_g", "ln_mix_b", "w_ff1", "b_ff1",
             "w_ff2", "b_ff2", "ln_ff_g", "ln_ff_b"]

    def pick(k, i):
        o = out[k]
        return o[i] if isinstance(o, tuple) else o["gdmv"[i]].reshape(env[k].shape)

    return (loss, grad_x, *[pick(k, 0) for k in names], *[pick(k, 1) for k in names],
            *[pick(k, 2) for k in names], *[pick(k, 3) for k in names])
```
